```python
import math
import jax, jax.numpy as jnp
from jax import lax
import numpy as np

D_MODEL = 2048
BATCH = 8
SEQ = 4096
DEPTH = 2

N_A_LAYERS = DEPTH // 2
N_B_LAYERS = DEPTH - N_A_LAYERS
S5_WIDTH = D_MODEL
S5_GROUP = 16
S5_GROUPS = S5_WIDTH // S5_GROUP
S5_STATE = 64
DT_MIN = 1e-3
DT_MAX = 1e-1
FOX_HEAD_DIM = 128
FOX_HEADS = D_MODEL // FOX_HEAD_DIM
FOX_WIDTH = FOX_HEADS * FOX_HEAD_DIM
Q_BLOCK = 128
RMS_EPS = 1e-6
NEG_INF = -1e30

kernel_name = "yoco_s5_fox_hybrid"

F32 = jnp.float32


def rmsnorm(x, g):
    xf = x.astype(F32)
    y = xf * lax.rsqrt(jnp.mean(xf * xf, axis=-1, keepdims=True) + RMS_EPS)
    return (y * g.astype(F32)).astype(x.dtype)


def s5_ssm(u, a_re, a_im, log_dt, b_re, b_im, c_re, c_im, d_skip):
    bsz, seq, _ = u.shape
    uf = u.astype(F32).reshape(bsz, seq, S5_GROUPS, S5_GROUP)
    dt = jnp.exp(log_dt.astype(F32))[:, None]
    ar = a_re.astype(F32)
    ai = a_im.astype(F32)
    mag = jnp.exp(ar * dt)
    abar_re = mag * jnp.cos(ai * dt)
    abar_im = mag * jnp.sin(ai * dt)
    den = ar * ar + ai * ai
    nr = abar_re - 1.0
    coef_re = (nr * ar + abar_im * ai) / den
    coef_im = (abar_im * ar - nr * ai) / den
    bu_re = jnp.einsum('bsgc,gpc->bsgp', uf, b_re.astype(F32))
    bu_im = jnp.einsum('bsgc,gpc->bsgp', uf, b_im.astype(F32))
    x_re = coef_re * bu_re - coef_im * bu_im
    x_im = coef_re * bu_im + coef_im * bu_re
    shape_a = (1, seq, S5_GROUPS, S5_STATE)
    a_seq_re = jnp.broadcast_to(abar_re, shape_a)
    a_seq_im = jnp.broadcast_to(abar_im, shape_a)

    def combine(left, right):
        a1r, a1i, b1r, b1i = left
        a2r, a2i, b2r, b2i = right
        return (a2r * a1r - a2i * a1i,
                a2r * a1i + a2i * a1r,
                a2r * b1r - a2i * b1i + b2r,
                a2r * b1i + a2i * b1r + b2i)

    _, _, h_re, h_im = lax.associative_scan(combine, (a_seq_re, a_seq_im, x_re, x_im), axis=1)
    y = (jnp.einsum('bsgp,gcp->bsgc', h_re, c_re.astype(F32))
         - jnp.einsum('bsgp,gcp->bsgc', h_im, c_im.astype(F32)))
    y = y + d_skip.astype(F32).reshape(S5_GROUPS, S5_GROUP) * uf
    return y.reshape(bsz, seq, S5_WIDTH)


def s5_layer(h, g_pre, g_post, w_in, a_re, a_im, log_dt, b_re, b_im, c_re, c_im, d_skip, w_glu, b_glu, w_out):
    xn = rmsnorm(h, g_pre)
    uz = xn @ w_in
    u, z = jnp.split(uz, 2, axis=-1)
    y = s5_ssm(u, a_re, a_im, log_dt, b_re, b_im, c_re, c_im, d_skip)
    y = jax.nn.gelu(y)
    y = y * jax.nn.sigmoid(y @ w_glu.astype(F32) + b_glu.astype(F32))
    y = y.astype(h.dtype) * jax.nn.silu(z)
    return h + rmsnorm(y @ w_out, g_post)


def shared_kv(h, g_kv, w_kv, b_f):
    bsz, seq, _ = h.shape
    kvf = rmsnorm(h, g_kv) @ w_kv
    k = kvf[..., :FOX_WIDTH]
    v = kvf[..., FOX_WIDTH:2 * FOX_WIDTH]
    f_logit = kvf[..., 2 * FOX_WIDTH:]
    k = k.reshape(bsz, seq, FOX_HEADS, FOX_HEAD_DIM).transpose(0, 2, 1, 3)
    v = v.reshape(bsz, seq, FOX_HEADS, FOX_HEAD_DIM).transpose(0, 2, 1, 3)
    log_f = jax.nn.log_sigmoid(f_logit.astype(F32) + b_f.astype(F32))
    cum = jnp.cumsum(log_f, axis=1).transpose(0, 2, 1)
    return k, v, cum


def fox_attention(q, k, v, cum):
    bsz, nh, seq, dh = q.shape
    nblk = seq // Q_BLOCK
    qb = q.reshape(bsz, nh, nblk, Q_BLOCK, dh).transpose(2, 0, 1, 3, 4)
    cb = cum.reshape(bsz, nh, nblk, Q_BLOCK).transpose(2, 0, 1, 3)
    kpos = jnp.arange(seq)
    scale = dh ** -0.5

    def block(args):
        q_i, c_i, i = args
        s = jnp.einsum('bhqd,bhkd->bhqk', q_i, k, preferred_element_type=F32) * scale
        s = s + c_i[..., :, None] - cum[:, :, None, :]
        qpos = i * Q_BLOCK + jnp.arange(Q_BLOCK)
        s = jnp.where(kpos[None, :] <= qpos[:, None], s, NEG_INF)
        p = jax.nn.softmax(s, axis=-1)
        return jnp.einsum('bhqk,bhkd->bhqd', p.astype(v.dtype), v)

    o = lax.map(block, (qb, cb, jnp.arange(nblk)))
    return o.transpose(1, 2, 0, 3, 4).reshape(bsz, nh, seq, dh)


def fox_layer(h, g_pre, g_post, w_in, w_out, k, v, cum):
    bsz, seq, _ = h.shape
    qz = rmsnorm(h, g_pre) @ w_in
    q, z = jnp.split(qz, 2, axis=-1)
    q = q.reshape(bsz, seq, FOX_HEADS, FOX_HEAD_DIM).transpose(0, 2, 1, 3)
    o = fox_attention(q, k, v, cum)
    o = o.transpose(0, 2, 1, 3).reshape(bsz, seq, FOX_WIDTH)
    o = o.astype(h.dtype) * jax.nn.silu(z)
    return h + rmsnorm(o @ w_out, g_post)


def _fwd_setup_inputs(seed: int = 0) -> dict:
    key = jax.random.key(seed)
    ks = jax.random.split(key, 24)
    nrm = lambda k, shp, s: jax.random.normal(k, shp, F32) * s
    n = jnp.arange(S5_STATE, dtype=F32)
    a_re = -0.5 + nrm(ks[4], (N_A_LAYERS, S5_GROUPS, S5_STATE), 0.01)
    a_im = math.pi * n + nrm(ks[5], (N_A_LAYERS, S5_GROUPS, S5_STATE), 0.01)
    log_dt = jax.random.uniform(ks[6], (N_A_LAYERS, S5_GROUPS), F32, math.log(DT_MIN), math.log(DT_MAX))
    return {
        "x": nrm(ks[0], (BATCH, SEQ, D_MODEL), 1.0),
        "norm_pre": 1.0 + nrm(ks[1], (DEPTH, D_MODEL), 0.02),
        "norm_post": 1.0 + nrm(ks[2], (DEPTH, D_MODEL), 0.02),
        "s5_w_in": nrm(ks[3], (N_A_LAYERS, D_MODEL, 2 * S5_WIDTH), D_MODEL ** -0.5),
        "s5_a_re": a_re,
        "s5_a_im": a_im,
        "s5_log_dt": log_dt,
        "s5_b_re": nrm(ks[7], (N_A_LAYERS, S5_GROUPS, S5_STATE, S5_GROUP), (2 * S5_GROUP) ** -0.5),
        "s5_b_im": nrm(ks[8], (N_A_LAYERS, S5_GROUPS, S5_STATE, S5_GROUP), (2 * S5_GROUP) ** -0.5),
        "s5_c_re": nrm(ks[9], (N_A_LAYERS, S5_GROUPS, S5_GROUP, S5_STATE), S5_STATE ** -0.5),
        "s5_c_im": nrm(ks[10], (N_A_LAYERS, S5_GROUPS, S5_GROUP, S5_STATE), S5_STATE ** -0.5),
        "s5_d": nrm(ks[11], (N_A_LAYERS, S5_WIDTH), 1.0),
        "s5_w_glu": nrm(ks[12], (N_A_LAYERS, S5_WIDTH, S5_WIDTH), S5_WIDTH ** -0.5),
        "s5_b_glu": nrm(ks[13], (N_A_LAYERS, S5_WIDTH), 0.01),
        "s5_w_out": nrm(ks[14], (N_A_LAYERS, S5_WIDTH, D_MODEL), S5_WIDTH ** -0.5),
        "kv_norm": 1.0 + nrm(ks[15], (D_MODEL,), 0.02),
        "kv_w": nrm(ks[16], (D_MODEL, 2 * FOX_WIDTH + FOX_HEADS), D_MODEL ** -0.5),
        "kv_b_f": jax.random.uniform(ks[17], (FOX_HEADS,), F32, 2.0, 7.0),
        "fox_w_in": nrm(ks[18], (N_B_LAYERS, D_MODEL, 2 * FOX_WIDTH), D_MODEL ** -0.5),
        "fox_w_out": nrm(ks[19], (N_B_LAYERS, FOX_WIDTH, D_MODEL), FOX_WIDTH ** -0.5),
    }


def _fwd_reference(x, norm_pre, norm_post, s5_w_in, s5_a_re, s5_a_im, s5_log_dt, s5_b_re, s5_b_im,
              s5_c_re, s5_c_im, s5_d, s5_w_glu, s5_b_glu, s5_w_out, kv_norm, kv_w, kv_b_f,
              fox_w_in, fox_w_out):
    h = x
    k = v = cum = None
    for layer in range(DEPTH):
        if layer < N_A_LAYERS:
            i = layer
            h = s5_layer(h, norm_pre[layer], norm_post[layer], s5_w_in[i], s5_a_re[i], s5_a_im[i],
                         s5_log_dt[i], s5_b_re[i], s5_b_im[i], s5_c_re[i], s5_c_im[i], s5_d[i],
                         s5_w_glu[i], s5_b_glu[i], s5_w_out[i])
        else:
            if layer == N_A_LAYERS:
                k, v, cum = shared_kv(h, kv_norm, kv_w, kv_b_f)
            j = layer - N_A_LAYERS
            h = fox_layer(h, norm_pre[layer], norm_post[layer], fox_w_in[j], fox_w_out[j], k, v, cum)
    return h


import jax as _jax
import jax.numpy as _jnp

TWIN_FORMAT = 'train_step'
FWD_PARAMS = ['x', 'norm_pre', 'norm_post', 's5_w_in', 's5_a_re', 's5_a_im', 's5_log_dt', 's5_b_re', 's5_b_im', 's5_c_re', 's5_c_im', 's5_d', 's5_w_glu', 's5_b_glu', 's5_w_out', 'kv_norm', 'kv_w', 'kv_b_f', 'fox_w_in', 'fox_w_out']
TWIN_WEIGHTS = ['norm_pre', 'norm_post', 's5_w_in', 's5_a_re', 's5_a_im', 's5_log_dt', 's5_b_re', 's5_b_im', 's5_c_re', 's5_c_im', 's5_d', 's5_w_glu', 's5_b_glu', 's5_w_out', 'kv_norm', 'kv_w', 'kv_b_f', 'fox_w_in', 'fox_w_out']
TWIN_DIFF_INPUT = 'x'
TWIN_INPUTS = ['x', 'norm_pre', 'norm_post', 's5_w_in', 's5_a_re', 's5_a_im', 's5_log_dt', 's5_b_re', 's5_b_im', 's5_c_re', 's5_c_im', 's5_d', 's5_w_glu', 's5_b_glu', 's5_w_out', 'kv_norm', 'kv_w', 'kv_b_f', 'fox_w_in', 'fox_w_out', 'loss_target', 'm_norm_pre', 'm_norm_post', 'm_s5_w_in', 'm_s5_a_re', 'm_s5_a_im', 'm_s5_log_dt', 'm_s5_b_re', 'm_s5_b_im', 'm_s5_c_re', 'm_s5_c_im', 'm_s5_d', 'm_s5_w_glu', 'm_s5_b_glu', 'm_s5_w_out', 'm_kv_norm', 'm_kv_w', 'm_kv_b_f', 'm_fox_w_in', 'm_fox_w_out', 'v_norm_pre', 'v_norm_post', 'v_s5_w_in', 'v_s5_a_re', 'v_s5_a_im', 'v_s5_log_dt', 'v_s5_b_re', 'v_s5_b_im', 'v_s5_c_re', 'v_s5_c_im', 'v_s5_d', 'v_s5_w_glu', 'v_s5_b_glu', 'v_s5_w_out', 'v_kv_norm', 'v_kv_w', 'v_kv_b_f', 'v_fox_w_in', 'v_fox_w_out']
TWIN_OUTPUTS = ['loss', 'grad_x', 'grad_norm_pre', 'grad_norm_post', 'grad_s5_w_in', 'grad_s5_a_re', 'grad_s5_a_im', 'grad_s5_log_dt', 'grad_s5_b_re', 'grad_s5_b_im', 'grad_s5_c_re', 'grad_s5_c_im', 'grad_s5_d', 'grad_s5_w_glu', 'grad_s5_b_glu', 'grad_s5_w_out', 'grad_kv_norm', 'grad_kv_w', 'grad_kv_b_f', 'grad_fox_w_in', 'grad_fox_w_out', 'delta_norm_pre', 'delta_norm_post', 'delta_s5_w_in', 'delta_s5_a_re', 'delta_s5_a_im', 'delta_s5_log_dt', 'delta_s5_b_re', 'delta_s5_b_im', 'delta_s5_c_re', 'delta_s5_c_im', 'delta_s5_d', 'delta_s5_w_glu', 'delta_s5_b_glu', 'delta_s5_w_out', 'delta_kv_norm', 'delta_kv_w', 'delta_kv_b_f', 'delta_fox_w_in', 'delta_fox_w_out', 'new_m_norm_pre', 'new_m_norm_post', 'new_m_s5_w_in', 'new_m_s5_a_re', 'new_m_s5_a_im', 'new_m_s5_log_dt', 'new_m_s5_b_re', 'new_m_s5_b_im', 'new_m_s5_c_re', 'new_m_s5_c_im', 'new_m_s5_d', 'new_m_s5_w_glu', 'new_m_s5_b_glu', 'new_m_s5_w_out', 'new_m_kv_norm', 'new_m_kv_w', 'new_m_kv_b_f', 'new_m_fox_w_in', 'new_m_fox_w_out', 'new_v_norm_pre', 'new_v_norm_post', 'new_v_s5_w_in', 'new_v_s5_a_re', 'new_v_s5_a_im', 'new_v_s5_log_dt', 'new_v_s5_b_re', 'new_v_s5_b_im', 'new_v_s5_c_re', 'new_v_s5_c_im', 'new_v_s5_d', 'new_v_s5_w_glu', 'new_v_s5_b_glu', 'new_v_s5_w_out', 'new_v_kv_norm', 'new_v_kv_w', 'new_v_kv_b_f', 'new_v_fox_w_in', 'new_v_fox_w_out']
TWIN_LEAF_KINDS = {'loss': 'loss', 'grad_x': 'grad_x', 'grad_norm_pre': 'grad_w', 'grad_norm_post': 'grad_w', 'grad_s5_w_in': 'grad_w', 'grad_s5_a_re': 'grad_w', 'grad_s5_a_im': 'grad_w', 'grad_s5_log_dt': 'grad_w', 'grad_s5_b_re': 'grad_w', 'grad_s5_b_im': 'grad_w', 'grad_s5_c_re': 'grad_w', 'grad_s5_c_im': 'grad_w', 'grad_s5_d': 'grad_w', 'grad_s5_w_glu': 'grad_w', 'grad_s5_b_glu': 'grad_w', 'grad_s5_w_out': 'grad_w', 'grad_kv_norm': 'grad_w', 'grad_kv_w': 'grad_w', 'grad_kv_b_f': 'grad_w', 'grad_fox_w_in': 'grad_w', 'grad_fox_w_out': 'grad_w', 'delta_norm_pre': 'delta_w', 'delta_norm_post': 'delta_w', 'delta_s5_w_in': 'delta_w', 'delta_s5_a_re': 'delta_w', 'delta_s5_a_im': 'delta_w', 'delta_s5_log_dt': 'delta_w', 'delta_s5_b_re': 'delta_w', 'delta_s5_b_im': 'delta_w', 'delta_s5_c_re': 'delta_w', 'delta_s5_c_im': 'delta_w', 'delta_s5_d': 'delta_w', 'delta_s5_w_glu': 'delta_w', 'delta_s5_b_glu': 'delta_w', 'delta_s5_w_out': 'delta_w', 'delta_kv_norm': 'delta_w', 'delta_kv_w': 'delta_w', 'delta_kv_b_f': 'delta_w', 'delta_fox_w_in': 'delta_w', 'delta_fox_w_out': 'delta_w', 'new_m_norm_pre': 'new_m', 'new_m_norm_post': 'new_m', 'new_m_s5_w_in': 'new_m', 'new_m_s5_a_re': 'new_m', 'new_m_s5_a_im': 'new_m', 'new_m_s5_log_dt': 'new_m', 'new_m_s5_b_re': 'new_m', 'new_m_s5_b_im': 'new_m', 'new_m_s5_c_re': 'new_m', 'new_m_s5_c_im': 'new_m', 'new_m_s5_d': 'new_m', 'new_m_s5_w_glu': 'new_m', 'new_m_s5_b_glu': 'new_m', 'new_m_s5_w_out': 'new_m', 'new_m_kv_norm': 'new_m', 'new_m_kv_w': 'new_m', 'new_m_kv_b_f': 'new_m', 'new_m_fox_w_in': 'new_m', 'new_m_fox_w_out': 'new_m', 'new_v_norm_pre': 'new_v', 'new_v_norm_post': 'new_v', 'new_v_s5_w_in': 'new_v', 'new_v_s5_a_re': 'new_v', 'new_v_s5_a_im': 'new_v', 'new_v_s5_log_dt': 'new_v', 'new_v_s5_b_re': 'new_v', 'new_v_s5_b_im': 'new_v', 'new_v_s5_c_re': 'new_v', 'new_v_s5_c_im': 'new_v', 'new_v_s5_d': 'new_v', 'new_v_s5_w_glu': 'new_v', 'new_v_s5_b_glu': 'new_v', 'new_v_s5_w_out': 'new_v', 'new_v_kv_norm': 'new_v', 'new_v_kv_w': 'new_v', 'new_v_kv_b_f': 'new_v', 'new_v_fox_w_in': 'new_v', 'new_v_fox_w_out': 'new_v'}


def _forward(args):
    return _fwd_reference(*[args[k] for k in FWD_PARAMS])


def _output_shape():
    def fwd():
        inp = _fwd_setup_inputs(0)
        return _fwd_reference(*[inp[k] for k in FWD_PARAMS])
    out = _jax.eval_shape(fwd)
    return out.shape, out.dtype

N_MICROBATCH = 1
ADAM_LR = 0.001
ADAM_B1 = 0.9
ADAM_B2 = 0.999
ADAM_EPS = 1e-08
ADAM_WD = 0.01
ADAM_STEP = 10
PER_EXAMPLE_BATCH_AXIS = {'x': 0, 'loss_target': 0}
SHARED_INPUTS = []
_WEIGHT_DTYPES = {'norm_pre': _jnp.float32, 'norm_post': _jnp.float32, 's5_w_in': _jnp.float32, 's5_a_re': _jnp.float32, 's5_a_im': _jnp.float32, 's5_log_dt': _jnp.float32, 's5_b_re': _jnp.float32, 's5_b_im': _jnp.float32, 's5_c_re': _jnp.float32, 's5_c_im': _jnp.float32, 's5_d': _jnp.float32, 's5_w_glu': _jnp.float32, 's5_b_glu': _jnp.float32, 's5_w_out': _jnp.float32, 'kv_norm': _jnp.float32, 'kv_w': _jnp.float32, 'kv_b_f': _jnp.float32, 'fox_w_in': _jnp.float32, 'fox_w_out': _jnp.float32}
MOMENT_SCALE = {'norm_pre': 3.412151e-01, 'norm_post': 1.606636e+01, 's5_w_in': 2.742967e-01, 's5_a_re': 2.077014e-02, 's5_a_im': 1.855462e-02, 's5_log_dt': 1.344335e+01, 's5_b_re': 1.269061e-02, 's5_b_im': 1.250384e-02, 's5_c_re': 1.798920e-02, 's5_c_im': 1.799417e-02, 's5_d': 7.119333e-01, 's5_w_glu': 1.082278e-01, 's5_b_glu': 3.182902e-01, 's5_w_out': 7.052450e-01, 'kv_norm': 4.856482e-01, 'kv_w': 3.171474e-01, 'kv_b_f': 1.207832e+00, 'fox_w_in': 2.113790e-01, 'fox_w_out': 4.781310e-01}


def _to_microbatches(a, axis):
    t = _jnp.moveaxis(a, axis, 0)
    t = t.reshape((N_MICROBATCH, t.shape[0] // N_MICROBATCH) + t.shape[1:])
    return _jnp.moveaxis(t, 1, axis + 1)


def setup_inputs(seed: int = 0) -> dict:
    inp = _fwd_setup_inputs(seed)
    key = _jax.random.fold_in(_jax.random.key(seed), 7919)
    shape, _ = _output_shape()
    out = dict(inp)
    out["loss_target"] = _jax.random.normal(_jax.random.fold_in(key, 0), shape, _jnp.float32)
    for i, name in enumerate(TWIN_WEIGHTS):
        w = inp[name].astype(_jnp.float32)
        if MOMENT_SCALE is None:
            s = _jnp.sqrt(_jnp.mean(_jnp.square(w)) + 1e-30)
        else:
            s = MOMENT_SCALE[name]
        km, kv = _jax.random.split(_jax.random.fold_in(key, i + 1))
        out[name] = w
        out["m_" + name] = s * _jax.random.normal(km, w.shape, _jnp.float32)
        out["v_" + name] = (s * s) * _jax.random.uniform(kv, w.shape, _jnp.float32, 0.5, 1.5)
    if N_MICROBATCH > 1:
        for name, axis in PER_EXAMPLE_BATCH_AXIS.items():
            out[name] = _to_microbatches(out[name], axis)
    return {'x': out['x'], 'norm_pre': out['norm_pre'], 'norm_post': out['norm_post'], 's5_w_in': out['s5_w_in'], 's5_a_re': out['s5_a_re'], 's5_a_im': out['s5_a_im'], 's5_log_dt': out['s5_log_dt'], 's5_b_re': out['s5_b_re'], 's5_b_im': out['s5_b_im'], 's5_c_re': out['s5_c_re'], 's5_c_im': out['s5_c_im'], 's5_d': out['s5_d'], 's5_w_glu': out['s5_w_glu'], 's5_b_glu': out['s5_b_glu'], 's5_w_out': out['s5_w_out'], 'kv_norm': out['kv_norm'], 'kv_w': out['kv_w'], 'kv_b_f': out['kv_b_f'], 'fox_w_in': out['fox_w_in'], 'fox_w_out': out['fox_w_out'], 'loss_target': out['loss_target'], 'm_norm_pre': out['m_norm_pre'], 'm_norm_post': out['m_norm_post'], 'm_s5_w_in': out['m_s5_w_in'], 'm_s5_a_re': out['m_s5_a_re'], 'm_s5_a_im': out['m_s5_a_im'], 'm_s5_log_dt': out['m_s5_log_dt'], 'm_s5_b_re': out['m_s5_b_re'], 'm_s5_b_im': out['m_s5_b_im'], 'm_s5_c_re': out['m_s5_c_re'], 'm_s5_c_im': out['m_s5_c_im'], 'm_s5_d': out['m_s5_d'], 'm_s5_w_glu': out['m_s5_w_glu'], 'm_s5_b_glu': out['m_s5_b_glu'], 'm_s5_w_out': out['m_s5_w_out'], 'm_kv_norm': out['m_kv_norm'], 'm_kv_w': out['m_kv_w'], 'm_kv_b_f': out['m_kv_b_f'], 'm_fox_w_in': out['m_fox_w_in'], 'm_fox_w_out': out['m_fox_w_out'], 'v_norm_pre': out['v_norm_pre'], 'v_norm_post': out['v_norm_post'], 'v_s5_w_in': out['v_s5_w_in'], 'v_s5_a_re': out['v_s5_a_re'], 'v_s5_a_im': out['v_s5_a_im'], 'v_s5_log_dt': out['v_s5_log_dt'], 'v_s5_b_re': out['v_s5_b_re'], 'v_s5_b_im': out['v_s5_b_im'], 'v_s5_c_re': out['v_s5_c_re'], 'v_s5_c_im': out['v_s5_c_im'], 'v_s5_d': out['v_s5_d'], 'v_s5_w_glu': out['v_s5_w_glu'], 'v_s5_b_glu': out['v_s5_b_glu'], 'v_s5_w_out': out['v_s5_w_out'], 'v_kv_norm': out['v_kv_norm'], 'v_kv_w': out['v_kv_w'], 'v_kv_b_f': out['v_kv_b_f'], 'v_fox_w_in': out['v_fox_w_in'], 'v_fox_w_out': out['v_fox_w_out']}


def _loss(weights, diff, rest, loss_target):
    with _jax.named_scope("forward"):
        args = {**rest, TWIN_DIFF_INPUT: diff, **{k: w.astype(_WEIGHT_DTYPES[k]) for k, w in weights.items()}}
        y = _forward(args)
    with _jax.named_scope("loss_head"):
        err = _jnp.square(y.astype(_jnp.float32) - loss_target)
        return 0.5 * _jnp.sum(_jnp.mean(err, axis=-1)) if err.ndim else 0.5 * err


def _adamw(w, g, m, v):
    m = ADAM_B1 * m + (1.0 - ADAM_B1) * g
    v = ADAM_B2 * v + (1.0 - ADAM_B2) * _jnp.square(g)
    m_hat = m / (1.0 - ADAM_B1 ** ADAM_STEP)
    v_hat = v / (1.0 - ADAM_B2 ** ADAM_STEP)
    delta = -ADAM_LR * (m_hat / (_jnp.sqrt(v_hat) + ADAM_EPS) + ADAM_WD * w)
    return delta, m, v


def reference(x, norm_pre, norm_post, s5_w_in, s5_a_re, s5_a_im, s5_log_dt, s5_b_re, s5_b_im, s5_c_re, s5_c_im, s5_d, s5_w_glu, s5_b_glu, s5_w_out, kv_norm, kv_w, kv_b_f, fox_w_in, fox_w_out, loss_target, m_norm_pre, m_norm_post, m_s5_w_in, m_s5_a_re, m_s5_a_im, m_s5_log_dt, m_s5_b_re, m_s5_b_im, m_s5_c_re, m_s5_c_im, m_s5_d, m_s5_w_glu, m_s5_b_glu, m_s5_w_out, m_kv_norm, m_kv_w, m_kv_b_f, m_fox_w_in, m_fox_w_out, v_norm_pre, v_norm_post, v_s5_w_in, v_s5_a_re, v_s5_a_im, v_s5_log_dt, v_s5_b_re, v_s5_b_im, v_s5_c_re, v_s5_c_im, v_s5_d, v_s5_w_glu, v_s5_b_glu, v_s5_w_out, v_kv_norm, v_kv_w, v_kv_b_f, v_fox_w_in, v_fox_w_out):
    given = dict(x=x, norm_pre=norm_pre, norm_post=norm_post, s5_w_in=s5_w_in, s5_a_re=s5_a_re, s5_a_im=s5_a_im, s5_log_dt=s5_log_dt, s5_b_re=s5_b_re, s5_b_im=s5_b_im, s5_c_re=s5_c_re, s5_c_im=s5_c_im, s5_d=s5_d, s5_w_glu=s5_w_glu, s5_b_glu=s5_b_glu, s5_w_out=s5_w_out, kv_norm=kv_norm, kv_w=kv_w, kv_b_f=kv_b_f, fox_w_in=fox_w_in, fox_w_out=fox_w_out, loss_target=loss_target, m_norm_pre=m_norm_pre, m_norm_post=m_norm_post, m_s5_w_in=m_s5_w_in, m_s5_a_re=m_s5_a_re, m_s5_a_im=m_s5_a_im, m_s5_log_dt=m_s5_log_dt, m_s5_b_re=m_s5_b_re, m_s5_b_im=m_s5_b_im, m_s5_c_re=m_s5_c_re, m_s5_c_im=m_s5_c_im, m_s5_d=m_s5_d, m_s5_w_glu=m_s5_w_glu, m_s5_b_glu=m_s5_b_glu, m_s5_w_out=m_s5_w_out, m_kv_norm=m_kv_norm, m_kv_w=m_kv_w, m_kv_b_f=m_kv_b_f, m_fox_w_in=m_fox_w_in, m_fox_w_out=m_fox_w_out, v_norm_pre=v_norm_pre, v_norm_post=v_norm_post, v_s5_w_in=v_s5_w_in, v_s5_a_re=v_s5_a_re, v_s5_a_im=v_s5_a_im, v_s5_log_dt=v_s5_log_dt, v_s5_b_re=v_s5_b_re, v_s5_b_im=v_s5_b_im, v_s5_c_re=v_s5_c_re, v_s5_c_im=v_s5_c_im, v_s5_d=v_s5_d, v_s5_w_glu=v_s5_w_glu, v_s5_b_glu=v_s5_b_glu, v_s5_w_out=v_s5_w_out, v_kv_norm=v_kv_norm, v_kv_w=v_kv_w, v_kv_b_f=v_kv_b_f, v_fox_w_in=v_fox_w_in, v_fox_w_out=v_fox_w_out)
    weights = {n: given[n] for n in TWIN_WEIGHTS}
    shared = {n: given[n] for n in SHARED_INPUTS}
    per_example = {n: given[n] for n in ['x']}
    grad_fn = _jax.value_and_grad(_loss, argnums=(0, 1))

    def one_microbatch(ex, loss_target):
        ex = dict(ex)
        diff = ex.pop(TWIN_DIFF_INPUT)
        return grad_fn(weights, diff, {**shared, **ex}, loss_target)

    if N_MICROBATCH == 1:
        loss, (grad_w, grad_x) = one_microbatch(per_example, given["loss_target"])
    else:
        def body(carry, xs):
            loss_sum, grad_sum = carry
            l_k, (gw_k, gx_k) = one_microbatch(xs[0], xs[1])
            with _jax.named_scope("update"):
                return (loss_sum + l_k, _jax.tree.map(_jnp.add, grad_sum, gw_k)), gx_k

        init = (_jnp.zeros((), _jnp.float32), _jax.tree.map(_jnp.zeros_like, weights))
        (loss, grad_w), grad_x = _jax.lax.scan(body, init, (per_example, given["loss_target"]))
    with _jax.named_scope("update"):
        delta_w, new_m, new_v = {}, {}, {}
        for n in TWIN_WEIGHTS:
            delta_w[n], new_m[n], new_v[n] = _adamw(weights[n], grad_w[n], given["m_" + n], given["v_" + n])
    return (loss, grad_x, *[grad_w[n] for n in TWIN_WEIGHTS], *[delta_w[n] for n in TWIN_WEIGHTS],
            *[new_m[n] for n in TWIN_WEIGHTS], *[new_v[n] for n in TWIN_WEIGHTS])
```

```python
import functools
import math

import jax
import jax.numpy as jnp
from jax import lax
from jax.experimental import pallas as pl
from jax.experimental.pallas import tpu as pltpu

F32 = jnp.float32
BF16 = jnp.bfloat16

N_DEV = 8
MESH_AXES = ("x", "y", "c")
S5_GROUP = 16
S5_STATE = 64
LANES = 128
SUBLANES = 8
GROUPS_PER_BLOCK = LANES // S5_GROUP
BLOCK_STATE = GROUPS_PER_BLOCK * S5_STATE
N_SEG = SUBLANES
HEAD_DIM = 128
RMS_EPS = 1e-6
NEG_INF = -1e30
ADAM_LR = 0.001
ADAM_B1 = 0.9
ADAM_B2 = 0.999
ADAM_EPS = 1e-08
ADAM_WD = 0.01
ADAM_STEP = 10
VMEM_LIMIT = 56 * 1024 * 1024


def _tile(n, pref, quantum=LANES):
    if n <= pref:
        return n
    t = (pref // quantum) * quantum
    while t >= quantum:
        if n % t == 0:
            return t
        t -= quantum
    return n


def _cparams(*sem):
    return pltpu.CompilerParams(dimension_semantics=sem if sem else None, vmem_limit_bytes=VMEM_LIMIT)


_DOT_DIMS = {"nn": ((1,), (0,)), "nt": ((1,), (1,)), "tn": ((0,), (0,))}


def _mm(a, b, mode, out_dtype, name, add=None):
    if mode == "nn":
        (M, K), (K2, N) = a.shape, b.shape
    elif mode == "nt":
        (M, K), (N, K2) = a.shape, b.shape
    else:
        (K, M), (K2, N) = a.shape, b.shape
    assert K == K2, (name, a.shape, b.shape)
    tm, tn, tk = _tile(M, 1024), _tile(N, 1024), _tile(K, 1024)
    nk = K // tk
    dims = (_DOT_DIMS[mode], ((), ()))

    def body(*refs):
        if add is None:
            a_ref, b_ref, o_ref, acc = refs
        else:
            a_ref, b_ref, c_ref, o_ref, acc = refs
        k = pl.program_id(2)

        @pl.when(k == 0)
        def _():
            acc[...] = jnp.zeros_like(acc)

        acc[...] += lax.dot_general(a_ref[...], b_ref[...], dims, preferred_element_type=F32)

        @pl.when(k == nk - 1)
        def _():
            r = acc[...]
            if add is not None:
                r = r + c_ref[...]
            o_ref[...] = r.astype(out_dtype)

    if mode == "tn":
        a_spec = pl.BlockSpec((tk, tm), lambda i, j, k: (k, i))
    else:
        a_spec = pl.BlockSpec((tm, tk), lambda i, j, k: (i, k))
    if mode == "nt":
        b_spec = pl.BlockSpec((tn, tk), lambda i, j, k: (j, k))
    else:
        b_spec = pl.BlockSpec((tk, tn), lambda i, j, k: (k, j))
    o_spec = pl.BlockSpec((tm, tn), lambda i, j, k: (i, j))
    in_specs = [a_spec, b_spec] + ([o_spec] if add is not None else [])
    args = (a, b) + ((add,) if add is not None else ())
    return pl.pallas_call(
        body, name=name, grid=(M // tm, N // tn, nk),
        in_specs=in_specs, out_specs=o_spec,
        out_shape=jax.ShapeDtypeStruct((M, N), out_dtype),
        scratch_shapes=[pltpu.VMEM((tm, tn), F32)],
        compiler_params=_cparams("parallel", "parallel", "arbitrary"),
    )(*args)


def _rowcall(body, name, n_rows, ins, outs, tile_rows=256):
    tr = _tile(n_rows, tile_rows, SUBLANES * 2)
    n_in = len(ins)
    kinds = [k for _, _, k in outs]

    def kern(*refs):
        @pl.when(pl.program_id(0) == 0)
        def _():
            for r, kind in zip(refs[n_in:], kinds):
                if kind == "acc":
                    r[...] = jnp.zeros_like(r)

        body(*refs)

    in_specs = []
    for arr, kind in ins:
        if kind == "row":
            in_specs.append(pl.BlockSpec((tr, arr.shape[1]), lambda i: (i, 0)))
        else:
            in_specs.append(pl.BlockSpec(arr.shape, lambda i, nd=arr.ndim: (0,) * nd))
    out_specs, out_shape = [], []
    for width, dtype, kind in outs:
        if kind == "row":
            out_specs.append(pl.BlockSpec((tr, width), lambda i: (i, 0)))
            out_shape.append(jax.ShapeDtypeStruct((n_rows, width), dtype))
        else:
            out_specs.append(pl.BlockSpec((1, width), lambda i: (0, 0)))
            out_shape.append(jax.ShapeDtypeStruct((1, width), F32))
    return pl.pallas_call(
        kern, name=name, grid=(n_rows // tr,), in_specs=in_specs, out_specs=out_specs, out_shape=out_shape,
        compiler_params=_cparams("arbitrary"),
    )(*[a for a, _ in ins])


def _rstd(x):
    return lax.rsqrt(jnp.mean(x * x, axis=-1, keepdims=True) + RMS_EPS)


def _rms_bwd(x, g, dy):
    xh = x * _rstd(x)
    dxh = dy * g
    dx = _rstd(x) * (dxh - xh * jnp.mean(dxh * xh, axis=-1, keepdims=True))
    return dx, jnp.sum(dy * xh, axis=0, keepdims=True)


def _silu(z):
    return z * jax.nn.sigmoid(z)


def _norm_cast(x, g, name):
    def body(x_ref, g_ref, o_ref):
        x = x_ref[...]
        o_ref[...] = (x * _rstd(x) * g_ref[...]).astype(BF16)

    return _rowcall(body, name, x.shape[0], [(x, "row"), (g, "full")], [(x.shape[1], BF16, "row")])[0]


def _resid_norm2(x, r0, g_kv, g_pre, name):
    def body(x_ref, r_ref, gk_ref, gp_ref, h_ref, nk_ref, np_ref):
        h = x_ref[...] + r_ref[...]
        h_ref[...] = h
        hn = h * _rstd(h)
        nk_ref[...] = (hn * gk_ref[...]).astype(BF16)
        np_ref[...] = (hn * gp_ref[...]).astype(BF16)

    d = x.shape[1]
    return _rowcall(body, name, x.shape[0], [(x, "row"), (r0, "row"), (g_kv, "full"), (g_pre, "full")],
                    [(d, F32, "row"), (d, BF16, "row"), (d, BF16, "row")])


def _post_norm(o, g, name):
    def body(o_ref, g_ref, r_ref):
        o = o_ref[...]
        r_ref[...] = o * _rstd(o) * g_ref[...]

    return _rowcall(body, name, o.shape[0], [(o, "row"), (g, "full")], [(o.shape[1], F32, "row")])[0]


def _post_norm_loss(o, g, h1, target, name):
    d = o.shape[1]

    def body(o_ref, g_ref, h_ref, t_ref, dh_ref, acc_ref):
        o = o_ref[...]
        e = h_ref[...] + o * _rstd(o) * g_ref[...] - t_ref[...]
        dh_ref[...] = e * (1.0 / d)
        acc_ref[...] += jnp.sum(e * e, axis=0, keepdims=True)

    return _rowcall(body, name, o.shape[0], [(o, "row"), (g, "full"), (h1, "row"), (target, "row")],
                    [(d, F32, "row"), (d, F32, "acc")])


def _post_norm_bwd(dy, o, g, name):
    def body(dy_ref, o_ref, g_ref, do_ref, dg_ref):
        dx, dg = _rms_bwd(o_ref[...], g_ref[...], dy_ref[...])
        do_ref[...] = dx.astype(BF16)
        dg_ref[...] += dg

    d = o.shape[1]
    return _rowcall(body, name, o.shape[0], [(dy, "row"), (o, "row"), (g, "full")], [(d, BF16, "row"), (d, F32, "acc")])


def _gate_mul(o, z, name):
    def body(o_ref, z_ref, r_ref):
        r_ref[...] = (o_ref[...] * _silu(z_ref[...])).astype(BF16)

    return _rowcall(body, name, o.shape[0], [(o, "row"), (z, "row")], [(o.shape[1], BF16, "row")])[0]


def _gate_bwd(d_oz, o, z, name):
    def body(d_ref, o_ref, z_ref, do_ref, dz_ref):
        _, vjp = jax.vjp(lambda o, z: o * _silu(z), o_ref[...], z_ref[...])
        do, dz = vjp(d_ref[...])
        do_ref[...] = do.astype(BF16)
        dz_ref[...] = dz.astype(BF16)

    w = o.shape[1]
    return _rowcall(body, name, o.shape[0], [(d_oz, "row"), (o, "row"), (z, "row")], [(w, BF16, "row"), (w, BF16, "row")])


def _norm_bwd2(dh2, h1, dxn1, dhn_kv, g_pre, g_kv, name):
    def body(dh2_ref, h_ref, d1_ref, dk_ref, gp_ref, gk_ref, dh1_ref, dgp_ref, dgk_ref):
        h = h_ref[...]
        dx1, dg1 = _rms_bwd(h, gp_ref[...], d1_ref[...])
        dxk, dgk = _rms_bwd(h, gk_ref[...], dk_ref[...])
        dh1_ref[...] = dh2_ref[...] + dx1 + dxk
        dgp_ref[...] += dg1
        dgk_ref[...] += dgk

    d = h1.shape[1]
    return _rowcall(body, name, h1.shape[0],
                    [(dh2, "row"), (h1, "row"), (dxn1, "row"), (dhn_kv, "row"), (g_pre, "full"), (g_kv, "full")],
                    [(d, F32, "row"), (d, F32, "acc"), (d, F32, "acc")])


def _norm_bwd1(dres, x, dxn, g, name):
    def body(dr_ref, x_ref, dn_ref, g_ref, dx_ref, dg_ref):
        dx, dg = _rms_bwd(x_ref[...], g_ref[...], dn_ref[...])
        dx_ref[...] = dr_ref[...] + dx
        dg_ref[...] += dg

    d = x.shape[1]
    return _rowcall(body, name, x.shape[0], [(dres, "row"), (x, "row"), (dxn, "row"), (g, "full")],
                    [(d, F32, "row"), (d, F32, "acc")])


def _gelu_cast(y, name):
    def body(y_ref, o_ref):
        o_ref[...] = jax.nn.gelu(y_ref[...]).astype(BF16)

    return _rowcall(body, name, y.shape[0], [(y, "row")], [(y.shape[1], BF16, "row")])[0]


def _s5_gate(y_ssm, gp, b_glu, z, name):
    def body(y_ref, gp_ref, b_ref, z_ref, o_ref):
        yg = jax.nn.gelu(y_ref[...])
        o_ref[...] = (yg * jax.nn.sigmoid(gp_ref[...] + b_ref[...]) * _silu(z_ref[...])).astype(BF16)

    return _rowcall(body, name, y_ssm.shape[0], [(y_ssm, "row"), (gp, "row"), (b_glu, "full"), (z, "row")],
                    [(y_ssm.shape[1], BF16, "row")])[0]


def _s5_gate_bwd(dy3, y_ssm, gp, b_glu, z, name):
    def body(d_ref, y_ref, gp_ref, b_ref, z_ref, dz_ref, dgp_ref, dyg_ref, db_ref):
        yg = jax.nn.gelu(y_ref[...])
        _, vjp = jax.vjp(lambda yg, gp, z: yg * jax.nn.sigmoid(gp) * _silu(z), yg, gp_ref[...] + b_ref[...], z_ref[...])
        dyg, dgp, dz = vjp(d_ref[...])
        dz_ref[...] = dz.astype(BF16)
        dgp_ref[...] = dgp.astype(BF16)
        dyg_ref[...] = dyg
        db_ref[...] += jnp.sum(dgp, axis=0, keepdims=True)

    w = y_ssm.shape[1]
    return _rowcall(body, name, y_ssm.shape[0],
                    [(dy3, "row"), (y_ssm, "row"), (gp, "row"), (b_glu, "full"), (z, "row")],
                    [(w, BF16, "row"), (w, BF16, "row"), (w, F32, "row"), (w, F32, "acc")])


def _gelu_bwd(dyg, y_ssm, name):
    def body(d_ref, y_ref, o_ref):
        _, vjp = jax.vjp(jax.nn.gelu, y_ref[...])
        o_ref[...] = vjp(d_ref[...])[0]

    return _rowcall(body, name, y_ssm.shape[0], [(dyg, "row"), (y_ssm, "row")], [(y_ssm.shape[1], F32, "row")])[0]


def _concat_cast(a, b, name):
    def body(a_ref, b_ref, o_ref):
        w = a_ref.shape[1]
        o_ref[:, :w] = a_ref[...].astype(BF16)
        o_ref[:, w:] = b_ref[...].astype(BF16)

    return _rowcall(body, name, a.shape[0], [(a, "row"), (b, "row")], [(a.shape[1] + b.shape[1], BF16, "row")])[0]


def _disc(ar, ai, ldt):
    dt = jnp.exp(ldt)
    mag = jnp.exp(ar * dt)
    abr = mag * jnp.cos(ai * dt)
    abi = mag * jnp.sin(ai * dt)
    den = ar * ar + ai * ai
    nr = abr - 1.0
    return abr, abi, (nr * ar + abi * ai) / den, (abi * ar - nr * ai) / den


def _s5_disc_fwd(a_re, a_im, ldt):
    def body(ar, ai, ld, o1, o2, o3, o4):
        o1[...], o2[...], o3[...], o4[...] = _disc(ar[...], ai[...], ld[...])

    sh = jax.ShapeDtypeStruct(a_re.shape, F32)
    return pl.pallas_call(body, name="s5_disc_fwd", out_shape=(sh, sh, sh, sh))(a_re, a_im, ldt)


def _s5_disc_bwd(a_re, a_im, ldt, d_abr, d_abi, d_cr, d_ci):
    def body(ar, ai, ld, g1, g2, g3, g4, o1, o2, o3):
        _, vjp = jax.vjp(_disc, ar[...], ai[...], ld[...])
        o1[...], o2[...], o3[...] = vjp((g1[...], g2[...], g3[...], g4[...]))

    sh = jax.ShapeDtypeStruct(a_re.shape, F32)
    return pl.pallas_call(body, name="s5_disc_bwd", out_shape=(sh, sh, jax.ShapeDtypeStruct(ldt.shape, F32)))(
        a_re, a_im, ldt, d_abr, d_abi, d_cr, d_ci)


def _bbar(cr, ci, br, bi):
    return cr * br - ci * bi, cr * bi + ci * br


def _s5_bbar_fwd(cr_col, ci_col, b_re, b_im):
    def body(cr, ci, br, bi, o1, o2):
        o1[...], o2[...] = _bbar(cr[...], ci[...], br[...], bi[...])

    w = b_re.shape[1]
    return _rowcall(body, "s5_bbar_fwd", b_re.shape[0], [(cr_col, "row"), (ci_col, "row"), (b_re, "row"), (b_im, "row")],
                    [(w, F32, "row"), (w, F32, "row")], tile_rows=1024)


def _s5_bbar_bwd(cr_col, ci_col, b_re, b_im, d_re, d_im):
    def body(cr, ci, br, bi, g1, g2, o1, o2, o3, o4):
        _, vjp = jax.vjp(_bbar, cr[...], ci[...], br[...], bi[...])
        o1[...], o2[...], o3[...], o4[...] = vjp((g1[...], g2[...]))

    w = b_re.shape[1]
    return _rowcall(body, "s5_bbar_bwd", b_re.shape[0],
                    [(cr_col, "row"), (ci_col, "row"), (b_re, "row"), (b_im, "row"), (d_re, "row"), (d_im, "row")],
                    [(1, F32, "row"), (1, F32, "row"), (w, F32, "row"), (w, F32, "row")], tile_rows=1024)


def _block_diag_in(t):
    g, p, c = t.shape
    nb = g // GROUPS_PER_BLOCK
    t4 = t.reshape(nb, GROUPS_PER_BLOCK, p, c).transpose(0, 1, 3, 2)
    eye = jnp.eye(GROUPS_PER_BLOCK, dtype=t.dtype)
    return (t4[:, :, :, None, :] * eye[None, :, None, :, None]).reshape(nb, GROUPS_PER_BLOCK * c, GROUPS_PER_BLOCK * p)


def _block_diag_in_extract(d, p, c):
    nb = d.shape[0]
    d5 = d.reshape(nb, GROUPS_PER_BLOCK, c, GROUPS_PER_BLOCK, p)
    diag = jnp.stack([d5[:, g, :, g, :] for g in range(GROUPS_PER_BLOCK)], axis=1)
    return diag.transpose(0, 1, 3, 2).reshape(nb * GROUPS_PER_BLOCK, p, c)


def _block_diag_out(t):
    g, c, p = t.shape
    nb = g // GROUPS_PER_BLOCK
    t4 = t.reshape(nb, GROUPS_PER_BLOCK, c, p).transpose(0, 1, 3, 2)
    eye = jnp.eye(GROUPS_PER_BLOCK, dtype=t.dtype)
    return (t4[:, :, :, None, :] * eye[None, :, None, :, None]).reshape(nb, GROUPS_PER_BLOCK * p, GROUPS_PER_BLOCK * c)


def _block_diag_out_extract(d, c, p):
    nb = d.shape[0]
    d5 = d.reshape(nb, GROUPS_PER_BLOCK, p, GROUPS_PER_BLOCK, c)
    diag = jnp.stack([d5[:, g, :, g, :] for g in range(GROUPS_PER_BLOCK)], axis=1)
    return diag.transpose(0, 1, 3, 2).reshape(nb * GROUPS_PER_BLOCK, c, p)


def _scan_step(ar, ai, hr, hi, xr, xi):
    return ar * hr - ai * hi + xr, ar * hi + ai * hr + xi


def _s5_scan_fwd(u, bd_re, bd_im, cd_re, cd_im, ab_re, ab_im, init_re, init_im, d_row, full, name):
    s, w = u.shape
    nb = w // LANES
    rows = _tile(s, 512, SUBLANES)
    nc = s // rows
    steps = rows // N_SEG
    ns = nb * BLOCK_STATE

    def body(u_ref, bdr, bdi, cdr, cdi, ar_ref, ai_ref, ir_ref, ii_ref, d_ref, *outs):
        if full:
            y_ref, hr_ref, hi_ref, er_ref, ei_ref, cr, ci = outs
        else:
            er_ref, ei_ref, hr_ref, hi_ref, cr, ci = outs
        c = pl.program_id(1)

        @pl.when(c == 0)
        def _():
            cr[...] = ir_ref[...]
            ci[...] = ii_ref[...]

        ub = u_ref[...].astype(BF16)
        hr_ref[...] = jnp.dot(ub, bdr[...], preferred_element_type=F32)
        hi_ref[...] = jnp.dot(ub, bdi[...], preferred_element_type=F32)
        ar, ai = ar_ref[...], ai_ref[...]

        def step(j, carry):
            off = pl.multiple_of(j * N_SEG, N_SEG)
            nr, ni = _scan_step(ar, ai, carry[0], carry[1], hr_ref[pl.ds(off, N_SEG), :], hi_ref[pl.ds(off, N_SEG), :])
            hr_ref[pl.ds(off, N_SEG), :] = nr
            hi_ref[pl.ds(off, N_SEG), :] = ni
            return nr, ni

        hr, hi = lax.fori_loop(0, steps, step, (cr[...], ci[...]), unroll=8)
        cr[...] = hr
        ci[...] = hi
        if full:
            y_ref[...] = (jnp.dot(hr_ref[...].astype(BF16), cdr[...], preferred_element_type=F32)
                          + jnp.dot(hi_ref[...].astype(BF16), cdi[...], preferred_element_type=F32)
                          + d_ref[...] * u_ref[...])

        @pl.when(c == nc - 1)
        def _():
            er_ref[...] = hr
            ei_ref[...] = hi

    blk3 = lambda a: pl.BlockSpec((None,) + a.shape[1:], lambda k, c: (k, 0, 0))
    seg = pl.BlockSpec((N_SEG, BLOCK_STATE), lambda k, c: (0, k))
    st = pl.BlockSpec((rows, BLOCK_STATE), lambda k, c: (c, k))
    in_specs = [pl.BlockSpec((rows, LANES), lambda k, c: (c, k)), blk3(bd_re), blk3(bd_im), blk3(cd_re), blk3(cd_im),
                seg, seg, seg, seg, pl.BlockSpec((1, LANES), lambda k, c: (0, k))]
    seg_shape = jax.ShapeDtypeStruct((N_SEG, ns), F32)
    st_shape = jax.ShapeDtypeStruct((s, ns), F32)
    carry = [pltpu.VMEM((N_SEG, BLOCK_STATE), F32)] * 2
    if full:
        out_specs = [pl.BlockSpec((rows, LANES), lambda k, c: (c, k)), st, st, seg, seg]
        out_shape = [jax.ShapeDtypeStruct((s, w), F32), st_shape, st_shape, seg_shape, seg_shape]
        scratch = carry
    else:
        out_specs = [seg, seg]
        out_shape = [seg_shape, seg_shape]
        scratch = [pltpu.VMEM((rows, BLOCK_STATE), F32)] * 2 + carry
    return pl.pallas_call(
        body, name=name, grid=(nb, nc), in_specs=in_specs, out_specs=out_specs, out_shape=out_shape,
        scratch_shapes=scratch, compiler_params=_cparams("parallel", "arbitrary"),
    )(u, bd_re, bd_im, cd_re, cd_im, ab_re, ab_im, init_re, init_im, d_row)


def _s5_seg_fix(e_re, e_im, ab_re, ab_im, seg_len, reverse, name):
    assert seg_len & (seg_len - 1) == 0

    def body(er, ei, ar, ai, o_re, o_im):
        pr, pi = ar[0:1, :], ai[0:1, :]
        for _ in range(int(math.log2(seg_len))):
            pr, pi = pr * pr - pi * pi, 2.0 * pr * pi
        tr = jnp.zeros_like(pr)
        ti = jnp.zeros_like(pr)
        order = list(range(N_SEG - 1, -1, -1)) if reverse else list(range(N_SEG))
        for n, sgm in enumerate(order):
            o_re[sgm:sgm + 1, :] = tr
            o_im[sgm:sgm + 1, :] = ti
            if n < N_SEG - 1:
                tr, ti = _scan_step(pr, pi, tr, ti, er[sgm:sgm + 1, :], ei[sgm:sgm + 1, :])

    sh = jax.ShapeDtypeStruct(e_re.shape, F32)
    return pl.pallas_call(body, name=name, out_shape=(sh, sh))(e_re, e_im, ab_re, ab_im)


def _s5_scan_bwd(dy, u, h_re, h_im, bd_re, bd_im, cd_re, cd_im, ab_re, ab_imn, gin_re, gin_im, d_row, full, name):
    s, w = u.shape
    nb = w // LANES
    rows = _tile(s, 512, SUBLANES)
    nc = s // rows
    steps = rows // N_SEG
    ns = nb * BLOCK_STATE

    def body(dy_ref, u_ref, hr_ref, hi_ref, bdr, bdi, cdr, cdi, ar_ref, ai_ref, ir_ref, ii_ref, d_ref, *outs):
        if full:
            du_ref, dbr_ref, dbi_ref, dcr_ref, dci_ref, dar_ref, dai_ref, dd_ref, gr, gi, accr, acci = outs
        else:
            er_ref, ei_ref, gr, gi = outs
        c = pl.program_id(1)

        @pl.when(c == 0)
        def _():
            gr[pl.ds(rows, N_SEG), :] = ir_ref[...]
            gi[pl.ds(rows, N_SEG), :] = ii_ref[...]
            if full:
                for r in (dbr_ref, dbi_ref, dcr_ref, dci_ref, dd_ref, accr, acci):
                    r[...] = jnp.zeros_like(r)

        dyb = dy_ref[...].astype(BF16)
        nt = (_DOT_DIMS["nt"], ((), ()))
        tn = (_DOT_DIMS["tn"], ((), ()))
        gr[pl.ds(0, rows), :] = lax.dot_general(dyb, cdr[...], nt, preferred_element_type=F32)
        gi[pl.ds(0, rows), :] = lax.dot_general(dyb, cdi[...], nt, preferred_element_type=F32)
        ar, ai = ar_ref[...], ai_ref[...]

        def step(jj, carry):
            off = pl.multiple_of((steps - 1 - jj) * N_SEG, N_SEG)
            nr, ni = _scan_step(ar, ai, carry[0], carry[1], gr[pl.ds(off, N_SEG), :], gi[pl.ds(off, N_SEG), :])
            gr[pl.ds(off, N_SEG), :] = nr
            gi[pl.ds(off, N_SEG), :] = ni
            return nr, ni

        g0r, g0i = lax.fori_loop(0, steps, step, (gr[pl.ds(rows, N_SEG), :], gi[pl.ds(rows, N_SEG), :]), unroll=8)
        if full:
            hr, hi = hr_ref[...], hi_ref[...]
            gnr, gni = gr[pl.ds(N_SEG, rows), :], gi[pl.ds(N_SEG, rows), :]
            accr[...] += jnp.sum((gnr * hr + gni * hi).reshape(steps, N_SEG, BLOCK_STATE), axis=0)
            acci[...] += jnp.sum((gni * hr - gnr * hi).reshape(steps, N_SEG, BLOCK_STATE), axis=0)
        gr[pl.ds(rows, N_SEG), :] = g0r
        gi[pl.ds(rows, N_SEG), :] = g0i
        if full:
            ub = u_ref[...].astype(BF16)
            gbr, gbi = gr[pl.ds(0, rows), :].astype(BF16), gi[pl.ds(0, rows), :].astype(BF16)
            dcr_ref[...] += lax.dot_general(hr.astype(BF16), dyb, tn, preferred_element_type=F32)
            dci_ref[...] += lax.dot_general(hi.astype(BF16), dyb, tn, preferred_element_type=F32)
            dbr_ref[...] += lax.dot_general(ub, gbr, tn, preferred_element_type=F32)
            dbi_ref[...] += lax.dot_general(ub, gbi, tn, preferred_element_type=F32)
            du_ref[...] = (lax.dot_general(gbr, bdr[...], nt, preferred_element_type=F32)
                           + lax.dot_general(gbi, bdi[...], nt, preferred_element_type=F32)
                           + d_ref[...] * dy_ref[...])
            dd_ref[...] += jnp.sum(dy_ref[...] * u_ref[...], axis=0, keepdims=True)

        @pl.when(c == nc - 1)
        def _():
            if full:
                dar_ref[...] = jnp.sum(accr[...], axis=0, keepdims=True)
                dai_ref[...] = jnp.sum(acci[...], axis=0, keepdims=True)
            else:
                er_ref[...] = g0r
                ei_ref[...] = g0i

    rev = lambda k, c: (nc - 1 - c, k)
    blk3 = lambda a: pl.BlockSpec((None,) + a.shape[1:], lambda k, c: (k, 0, 0))
    seg = pl.BlockSpec((N_SEG, BLOCK_STATE), lambda k, c: (0, k))
    st = pl.BlockSpec((rows, BLOCK_STATE), rev)
    ch = pl.BlockSpec((rows, LANES), rev)
    vec = pl.BlockSpec((1, LANES), lambda k, c: (0, k))
    in_specs = [ch, ch, st, st, blk3(bd_re), blk3(bd_im), blk3(cd_re), blk3(cd_im), seg, seg, seg, seg, vec]
    gbuf = [pltpu.VMEM((rows + N_SEG, BLOCK_STATE), F32)] * 2
    if full:
        row1 = pl.BlockSpec((1, BLOCK_STATE), lambda k, c: (0, k))
        out_specs = [ch, blk3(bd_re), blk3(bd_im), blk3(cd_re), blk3(cd_im), row1, row1, vec]
        out_shape = [jax.ShapeDtypeStruct((s, w), F32),
                     jax.ShapeDtypeStruct(bd_re.shape, F32), jax.ShapeDtypeStruct(bd_im.shape, F32),
                     jax.ShapeDtypeStruct(cd_re.shape, F32), jax.ShapeDtypeStruct(cd_im.shape, F32),
                     jax.ShapeDtypeStruct((1, ns), F32), jax.ShapeDtypeStruct((1, ns), F32),
                     jax.ShapeDtypeStruct((1, w), F32)]
        scratch = gbuf + [pltpu.VMEM((N_SEG, BLOCK_STATE), F32)] * 2
    else:
        out_specs = [seg, seg]
        out_shape = [jax.ShapeDtypeStruct((N_SEG, ns), F32)] * 2
        scratch = gbuf
    return pl.pallas_call(
        body, name=name, grid=(nb, nc), in_specs=in_specs, out_specs=out_specs, out_shape=out_shape,
        scratch_shapes=scratch, compiler_params=_cparams("parallel", "arbitrary"),
    )(dy, u, h_re, h_im, bd_re, bd_im, cd_re, cd_im, ab_re, ab_imn, gin_re, gin_im, d_row)


def _log_sigmoid(x):
    return jnp.minimum(x, 0.0) - jnp.log(1.0 + jnp.exp(-jnp.abs(x)))


def _tri(n, upper):
    r = lax.broadcasted_iota(jnp.int32, (n, n), 0)
    c = lax.broadcasted_iota(jnp.int32, (n, n), 1)
    return jnp.where((c >= r) if upper else (r >= c), 1.0, 0.0).astype(F32)


def _cum_fwd(f_logit, b_row, name):
    s, w = f_logit.shape
    t = _tile(s, 256, SUBLANES)

    def body(f_ref, b_ref, o_ref, carry):
        @pl.when(pl.program_id(0) == 0)
        def _():
            carry[...] = jnp.zeros_like(carry)

        lf = _log_sigmoid(f_ref[...] + b_ref[...])
        cum = jnp.dot(_tri(t, False), lf, precision=lax.Precision.HIGHEST, preferred_element_type=F32) + carry[...]
        o_ref[...] = cum
        carry[...] = cum[t - 1:t, :]

    return pl.pallas_call(
        body, name=name, grid=(s // t,),
        in_specs=[pl.BlockSpec((t, w), lambda i: (i, 0)), pl.BlockSpec((1, w), lambda i: (0, 0))],
        out_specs=pl.BlockSpec((t, w), lambda i: (i, 0)), out_shape=jax.ShapeDtypeStruct((s, w), F32),
        scratch_shapes=[pltpu.VMEM((1, w), F32)], compiler_params=_cparams("arbitrary"),
    )(f_logit, b_row)


def _cum_bwd(dcq, dck, f_logit, b_row, name):
    s, w = f_logit.shape
    t = _tile(s, 256, SUBLANES)
    nt = s // t

    def body(q_ref, k_ref, f_ref, b_ref, df_ref, db_ref, carry):
        @pl.when(pl.program_id(0) == 0)
        def _():
            carry[...] = jnp.zeros_like(carry)
            db_ref[...] = jnp.zeros_like(db_ref)

        dc = q_ref[...] - k_ref[...]
        rc = jnp.dot(_tri(t, True), dc, precision=lax.Precision.HIGHEST, preferred_element_type=F32) + carry[...]
        carry[...] = rc[0:1, :]
        df = rc * (1.0 - jax.nn.sigmoid(f_ref[...] + b_ref[...]))
        df_ref[...] = df.astype(BF16)
        db_ref[...] += jnp.sum(df, axis=0, keepdims=True)

    rev = pl.BlockSpec((t, w), lambda i: (nt - 1 - i, 0))
    one = pl.BlockSpec((1, w), lambda i: (0, 0))
    return pl.pallas_call(
        body, name=name, grid=(nt,), in_specs=[rev, rev, rev, one], out_specs=[rev, one],
        out_shape=[jax.ShapeDtypeStruct((s, w), BF16), jax.ShapeDtypeStruct((1, w), F32)],
        scratch_shapes=[pltpu.VMEM((1, w), F32)], compiler_params=_cparams("arbitrary"),
    )(dcq, dck, f_logit, b_row)


def _head_col(cum_tile, h):
    lane = lax.broadcasted_iota(jnp.int32, cum_tile.shape, 1)
    return jnp.sum(jnp.where(lane == h, cum_tile, 0.0), axis=1, keepdims=True)


def _attn_tiles(s):
    return _tile(s, 512, LANES)


def _fox_fwd(q, kv, cum, cum_t, name):
    s, w = q.shape
    nh = w // HEAD_DIM
    tq = _attn_tiles(s)
    nq = s // tq
    scale = HEAD_DIM ** -0.5
    nt = (_DOT_DIMS["nt"], ((), ()))

    def body(q_ref, k_ref, v_ref, c_ref, ct_ref, o_ref, lse_ref, m_s, l_s, acc_s):
        h, i = pl.program_id(0), pl.program_id(1)
        qb = q_ref[...]
        cq = _head_col(c_ref[...], h)
        m_s[...] = jnp.full_like(m_s, NEG_INF)
        l_s[...] = jnp.zeros_like(l_s)
        acc_s[...] = jnp.zeros_like(acc_s)

        def tile(j, masked):
            off = pl.multiple_of(j * tq, tq)
            kb = k_ref[pl.ds(off, tq), :]
            sc = lax.dot_general(qb, kb, nt, preferred_element_type=F32) * scale + (cq - ct_ref[:, pl.ds(off, tq)])
            if masked:
                r = lax.broadcasted_iota(jnp.int32, sc.shape, 0)
                cc = lax.broadcasted_iota(jnp.int32, sc.shape, 1)
                sc = jnp.where(cc <= r, sc, NEG_INF)
            m_old = m_s[...]
            m_new = jnp.maximum(m_old, jnp.max(sc, axis=1, keepdims=True))
            p = jnp.exp(sc - m_new)
            alpha = jnp.exp(m_old - m_new)
            l_s[...] = alpha * l_s[...] + jnp.sum(p, axis=1, keepdims=True)
            acc_s[...] = alpha * acc_s[...] + jnp.dot(p.astype(BF16), v_ref[pl.ds(off, tq), :], preferred_element_type=F32)
            m_s[...] = m_new

        def loop(j, carry):
            tile(j, False)
            return carry

        lax.fori_loop(0, i, loop, 0)
        tile(i, True)
        o_ref[...] = acc_s[...] / l_s[...]
        lse_ref[...] = jnp.broadcast_to(m_s[...] + jnp.log(l_s[...]), lse_ref.shape)

    return pl.pallas_call(
        body, name=name, grid=(nh, nq),
        in_specs=[pl.BlockSpec((tq, HEAD_DIM), lambda h, i: (i, h)),
                  pl.BlockSpec((s, HEAD_DIM), lambda h, i: (0, h)),
                  pl.BlockSpec((s, HEAD_DIM), lambda h, i: (0, nh + h)),
                  pl.BlockSpec((tq, LANES), lambda h, i: (i, 0)),
                  pl.BlockSpec((None, 1, s), lambda h, i: (h, 0, 0))],
        out_specs=[pl.BlockSpec((tq, HEAD_DIM), lambda h, i: (i, h)),
                   pl.BlockSpec((None, tq, LANES), lambda h, i: (h, i, 0))],
        out_shape=[jax.ShapeDtypeStruct((s, w), F32), jax.ShapeDtypeStruct((nh, s, LANES), F32)],
        scratch_shapes=[pltpu.VMEM((tq, 1), F32), pltpu.VMEM((tq, 1), F32), pltpu.VMEM((tq, HEAD_DIM), F32)],
        compiler_params=_cparams("parallel", "arbitrary"),
    )(q, kv, kv, cum, cum_t)


def _fox_bwd_dq(q, kv, do, o, lse, cum, cum_t, name):
    s, w = q.shape
    nh = w // HEAD_DIM
    tq = _attn_tiles(s)
    nq = s // tq
    scale = HEAD_DIM ** -0.5
    nt = (_DOT_DIMS["nt"], ((), ()))

    def body(q_ref, k_ref, v_ref, do_ref, o_ref, lse_ref, c_ref, ct_ref, dq_ref, dcq_ref, dl_ref, acc_s, dc_s):
        h, i = pl.program_id(0), pl.program_id(1)
        qb = q_ref[...]
        dob = do_ref[...]
        cq = _head_col(c_ref[...], h)
        lse = lse_ref[:, 0:1]
        delta = jnp.sum(dob.astype(F32) * o_ref[...], axis=1, keepdims=True)
        acc_s[...] = jnp.zeros_like(acc_s)
        dc_s[...] = jnp.zeros_like(dc_s)

        def tile(j, masked):
            off = pl.multiple_of(j * tq, tq)
            kb = k_ref[pl.ds(off, tq), :]
            sc = lax.dot_general(qb, kb, nt, preferred_element_type=F32) * scale + (cq - ct_ref[:, pl.ds(off, tq)])
            if masked:
                r = lax.broadcasted_iota(jnp.int32, sc.shape, 0)
                cc = lax.broadcasted_iota(jnp.int32, sc.shape, 1)
                sc = jnp.where(cc <= r, sc, NEG_INF)
            p = jnp.exp(sc - lse)
            dp = lax.dot_general(dob, v_ref[pl.ds(off, tq), :], nt, preferred_element_type=F32)
            ds = p * (dp - delta)
            acc_s[...] += jnp.dot(ds.astype(BF16), kb, preferred_element_type=F32)
            dc_s[...] += jnp.sum(ds, axis=1, keepdims=True)

        def loop(j, carry):
            tile(j, False)
            return carry

        lax.fori_loop(0, i, loop, 0)
        tile(i, True)
        dq_ref[...] = (acc_s[...] * scale).astype(BF16)
        dcq_ref[...] = jnp.broadcast_to(dc_s[...], dcq_ref.shape)
        dl_ref[...] = jnp.broadcast_to(delta, dl_ref.shape)

    qspec = pl.BlockSpec((tq, HEAD_DIM), lambda h, i: (i, h))
    rep = pl.BlockSpec((None, tq, LANES), lambda h, i: (h, i, 0))
    return pl.pallas_call(
        body, name=name, grid=(nh, nq),
        in_specs=[qspec,
                  pl.BlockSpec((s, HEAD_DIM), lambda h, i: (0, h)),
                  pl.BlockSpec((s, HEAD_DIM), lambda h, i: (0, nh + h)),
                  qspec, qspec, rep,
                  pl.BlockSpec((tq, LANES), lambda h, i: (i, 0)),
                  pl.BlockSpec((None, 1, s), lambda h, i: (h, 0, 0))],
        out_specs=[qspec, rep, rep],
        out_shape=[jax.ShapeDtypeStruct((s, w), BF16), jax.ShapeDtypeStruct((nh, s, LANES), F32),
                   jax.ShapeDtypeStruct((nh, s, LANES), F32)],
        scratch_shapes=[pltpu.VMEM((tq, HEAD_DIM), F32), pltpu.VMEM((tq, 1), F32)],
        compiler_params=_cparams("parallel", "arbitrary"),
    )(q, kv, kv, do, o, lse, cum, cum_t)


def _fox_bwd_dkv(q, kv, do, lse_t, delta_t, cum, cum_t, name):
    s, w = q.shape
    nh = w // HEAD_DIM
    tk = _attn_tiles(s)
    nk = s // tk
    scale = HEAD_DIM ** -0.5
    nt = (_DOT_DIMS["nt"], ((), ()))

    def body(q_ref, k_ref, v_ref, do_ref, lse_ref, dl_ref, c_ref, ct_ref, dk_ref, dv_ref, dck_ref, dk_s, dv_s, dc_s):
        h, j = pl.program_id(0), pl.program_id(1)
        kb = k_ref[...]
        vb = v_ref[...]
        ck = _head_col(c_ref[...], h)
        dk_s[...] = jnp.zeros_like(dk_s)
        dv_s[...] = jnp.zeros_like(dv_s)
        dc_s[...] = jnp.zeros_like(dc_s)

        def tile(i, masked):
            off = pl.multiple_of(i * tk, tk)
            qb = q_ref[pl.ds(off, tk), :]
            dob = do_ref[pl.ds(off, tk), :]
            sc = lax.dot_general(kb, qb, nt, preferred_element_type=F32) * scale + (ct_ref[:, pl.ds(off, tk)] - ck)
            if masked:
                r = lax.broadcasted_iota(jnp.int32, sc.shape, 0)
                cc = lax.broadcasted_iota(jnp.int32, sc.shape, 1)
                sc = jnp.where(r <= cc, sc, NEG_INF)
            p = jnp.exp(sc - lse_ref[:, pl.ds(off, tk)])
            dv_s[...] += jnp.dot(p.astype(BF16), dob, preferred_element_type=F32)
            dp = lax.dot_general(vb, dob, nt, preferred_element_type=F32)
            ds = p * (dp - dl_ref[:, pl.ds(off, tk)])
            dk_s[...] += jnp.dot(ds.astype(BF16), qb, preferred_element_type=F32)
            dc_s[...] += jnp.sum(ds, axis=1, keepdims=True)

        tile(j, True)

        def loop(i, carry):
            tile(i, False)
            return carry

        lax.fori_loop(j + 1, nk, loop, 0)
        dk_ref[...] = (dk_s[...] * scale).astype(BF16)
        dv_ref[...] = dv_s[...].astype(BF16)
        dck_ref[...] = jnp.broadcast_to(dc_s[...], dck_ref.shape)

    col = lambda off: pl.BlockSpec((s, HEAD_DIM), lambda h, j, off=off: (0, off + h))
    row = pl.BlockSpec((None, 1, s), lambda h, j: (h, 0, 0))
    kspec = pl.BlockSpec((tk, HEAD_DIM), lambda h, j: (j, h))
    return pl.pallas_call(
        body, name=name, grid=(nh, nk),
        in_specs=[col(0), kspec, pl.BlockSpec((tk, HEAD_DIM), lambda h, j: (j, nh + h)), col(0), row, row,
                  pl.BlockSpec((tk, LANES), lambda h, j: (j, 0)), row],
        out_specs=[kspec, kspec, pl.BlockSpec((None, tk, LANES), lambda h, j: (h, j, 0))],
        out_shape=[jax.ShapeDtypeStruct((s, w), BF16), jax.ShapeDtypeStruct((s, w), BF16),
                   jax.ShapeDtypeStruct((nh, s, LANES), F32)],
        scratch_shapes=[pltpu.VMEM((tk, HEAD_DIM), F32), pltpu.VMEM((tk, HEAD_DIM), F32), pltpu.VMEM((tk, 1), F32)],
        compiler_params=_cparams("parallel", "arbitrary"),
    )(q, kv, kv, do, lse_t, delta_t, cum, cum_t)


def _exchange(arrs, scatter, name):
    n = len(arrs)

    def body(*refs):
        ins, outs = refs[:n], refs[n:2 * n]
        send_sems, recv_sems, local_sems = refs[2 * n:]
        x, y, c = (lax.axis_index(a) for a in MESH_AXES)
        me = 4 * x + 2 * y + c

        def peer(k):
            kx, ky, kc = (k >> 2) & 1, (k >> 1) & 1, k & 1
            px, py, pc = (1 - x if kx else x), (1 - y if ky else y), (1 - c if kc else c)
            return (px, py, pc), 4 * px + 2 * py + pc

        local, remote = [], []
        for a in range(n):
            src = ins[a].at[me] if scatter else ins[a]
            cp = pltpu.make_async_copy(src, outs[a].at[me], local_sems.at[a])
            cp.start()
            local.append(cp)
            for k in range(1, N_DEV):
                pid, pflat = peer(k)
                cp = pltpu.make_async_remote_copy(
                    src_ref=ins[a].at[pflat] if scatter else ins[a], dst_ref=outs[a].at[me],
                    send_sem=send_sems.at[a, k - 1], recv_sem=recv_sems.at[a, k - 1],
                    device_id=pid, device_id_type=pl.DeviceIdType.MESH)
                cp.start()
                remote.append(cp)
        for cp in remote:
            cp.wait_send()
            cp.wait_recv()
        for cp in local:
            cp.wait()

    out_shape = [jax.ShapeDtypeStruct(((N_DEV,) + a.shape[1:]) if scatter else ((N_DEV,) + a.shape), a.dtype) for a in arrs]
    return pl.pallas_call(
        body, name=name, out_shape=out_shape,
        in_specs=[pl.BlockSpec(memory_space=pl.ANY)] * n, out_specs=[pl.BlockSpec(memory_space=pl.ANY)] * n,
        scratch_shapes=[pltpu.SemaphoreType.DMA((n, N_DEV - 1)), pltpu.SemaphoreType.DMA((n, N_DEV - 1)),
                        pltpu.SemaphoreType.DMA((n,))],
    )(*arrs)


def _adamw_math(w, g, m, v):
    m = ADAM_B1 * m + (1.0 - ADAM_B1) * g
    v = ADAM_B2 * v + (1.0 - ADAM_B2) * (g * g)
    m_hat = m / (1.0 - ADAM_B1 ** ADAM_STEP)
    v_hat = v / (1.0 - ADAM_B2 ** ADAM_STEP)
    return -ADAM_LR * (m_hat / (jnp.sqrt(v_hat) + ADAM_EPS) + ADAM_WD * w), m, v


def _adamw(parts, w, m, v, name):
    r, c = w.shape
    tr = _tile(r, max(SUBLANES, (256 * 1024) // c // SUBLANES * SUBLANES), SUBLANES)

    def body(p_ref, w_ref, m_ref, v_ref, g_ref, d_ref, nm_ref, nv_ref):
        g = p_ref[0].astype(F32)
        for d in range(1, N_DEV):
            g = g + p_ref[d].astype(F32)
        g_ref[...] = g
        d_ref[...], nm_ref[...], nv_ref[...] = _adamw_math(w_ref[...], g, m_ref[...], v_ref[...])

    blk = pl.BlockSpec((tr, c), lambda i: (i, 0))
    sh = jax.ShapeDtypeStruct((r, c), F32)
    return pl.pallas_call(
        body, name=name, grid=(r // tr,),
        in_specs=[pl.BlockSpec((N_DEV, tr, c), lambda i: (0, i, 0)), blk, blk, blk],
        out_specs=[blk] * 4, out_shape=[sh] * 4, compiler_params=_cparams("parallel"),
    )(parts, w, m, v)


def _perm(a):
    s, d = a.shape
    return a.reshape(N_SEG, s // N_SEG, d).transpose(1, 0, 2).reshape(s, d)


def _unperm(a):
    s, d = a.shape
    return a.reshape(s // N_SEG, N_SEG, d).transpose(1, 0, 2).reshape(s, d)


def _lane_pad(a, width=LANES):
    return jnp.pad(a, ((0, 0), (0, width - a.shape[1])))


def _local_step(x, target, norm_pre, norm_post, kv_norm, kv_b_f, a_re, a_im, log_dt, b_re, b_im, c_re, c_im,
                d_skip, b_glu, w_in, w_glu, w_out, w_kv, fw_in, fw_out):
    s, d = x.shape
    w = w_glu.shape[0]
    fw = fw_out.shape[0]
    nh = fw // HEAD_DIM
    g, p = a_re.shape
    seg_len = s // N_SEG
    row = lambda v: v.reshape(1, -1)
    g_pre0, g_pre1, g_post0, g_post1, g_kv = row(norm_pre[0]), row(norm_pre[1]), row(norm_post[0]), row(norm_post[1]), row(kv_norm)
    d_row, bglu_row = row(d_skip), row(b_glu)

    ldt = log_dt.reshape(g, 1)
    abr, abi, cr, ci = _s5_disc_fwd(a_re, a_im, ldt)
    cr_col, ci_col = cr.reshape(g * p, 1), ci.reshape(g * p, 1)
    b_re2, b_im2 = b_re.reshape(g * p, S5_GROUP), b_im.reshape(g * p, S5_GROUP)
    bb_re, bb_im = _s5_bbar_fwd(cr_col, ci_col, b_re2, b_im2)
    bd_re = _block_diag_in(bb_re.reshape(g, p, S5_GROUP)).astype(BF16)
    bd_im = _block_diag_in(bb_im.reshape(g, p, S5_GROUP)).astype(BF16)
    cd_re = _block_diag_out(c_re).astype(BF16)
    cd_im = _block_diag_out(-c_im).astype(BF16)
    ab_re = jnp.broadcast_to(abr.reshape(1, g * p), (N_SEG, g * p))
    ab_im = jnp.broadcast_to(abi.reshape(1, g * p), (N_SEG, g * p))
    zero_seg = jnp.zeros((N_SEG, g * p), F32)

    xp = _perm(x)
    xn0 = _norm_cast(xp, g_pre0, "norm_pre0")
    uz = _mm(xn0, w_in, "nn", F32, "s5_in")
    u, z0 = uz[:, :w], uz[:, w:]
    e_re, e_im = _s5_scan_fwd(u, bd_re, bd_im, cd_re, cd_im, ab_re, ab_im, zero_seg, zero_seg, d_row, False, "s5_scan_ends")
    i_re, i_im = _s5_seg_fix(e_re, e_im, ab_re, ab_im, seg_len, False, "s5_seg_fix")
    y_ssm, h_re, h_im, _, _ = _s5_scan_fwd(u, bd_re, bd_im, cd_re, cd_im, ab_re, ab_im, i_re, i_im, d_row, True, "s5_scan")
    yg = _gelu_cast(y_ssm, "s5_gelu")
    gp = _mm(yg, w_glu, "nn", F32, "s5_glu")
    y3 = _s5_gate(y_ssm, gp, bglu_row, z0, "s5_gate")
    o0 = _mm(y3, w_out, "nn", F32, "s5_out")
    r0 = _unperm(_post_norm(o0, g_post0, "norm_post0"))

    h1, hn_kv, xn1 = _resid_norm2(x, r0, g_kv, g_pre1, "resid_norms")
    w_kvm = w_kv[:, :2 * fw]
    w_f = _lane_pad(w_kv[:, 2 * fw:])
    kv = _mm(hn_kv, w_kvm, "nn", BF16, "kv_proj")
    f_logit = _mm(hn_kv, w_f, "nn", F32, "f_proj")
    bf_row = _lane_pad(row(kv_b_f))
    cum = _cum_fwd(f_logit, bf_row, "cum_fwd")
    cum_t = cum[:, :nh].T.reshape(nh, 1, s)
    q = _mm(xn1, fw_in[:, :fw], "nn", BF16, "fox_q")
    z1 = _mm(xn1, fw_in[:, fw:], "nn", F32, "fox_z")
    o, lse = _fox_fwd(q, kv, cum, cum_t, "fox_fwd")
    oz = _gate_mul(o, z1, "fox_gate")
    o1 = _mm(oz, fw_out, "nn", F32, "fox_out")
    dh2, sq = _post_norm_loss(o1, g_post1, h1, target, "norm_post1_loss")
    loss = 0.5 * jnp.sum(sq) / d

    do1, dg_post1 = _post_norm_bwd(dh2, o1, g_post1, "norm_post1_bwd")
    d_fw_out = _mm(oz, do1, "tn", F32, "fox_out_dw")
    d_oz = _mm(do1, fw_out, "nt", F32, "fox_out_dx")
    do, dz1 = _gate_bwd(d_oz, o, z1, "fox_gate_bwd")
    dq, dcq, delta = _fox_bwd_dq(q, kv, do, o, lse, cum, cum_t, "fox_bwd_dq")
    lse_t = lse[:, :, 0].reshape(nh, 1, s)
    delta_t = delta[:, :, 0].reshape(nh, 1, s)
    dk, dv, dck = _fox_bwd_dkv(q, kv, do, lse_t, delta_t, cum, cum_t, "fox_bwd_dkv")
    dqz = _concat_cast(dq, dz1, "fox_dqz")
    d_fw_in = _mm(xn1, dqz, "tn", F32, "fox_in_dw")
    dxn1 = _mm(dqz, fw_in, "nt", F32, "fox_in_dx")
    dcq_sl = _lane_pad(dcq[:, :, 0].T)
    dck_sl = _lane_pad(dck[:, :, 0].T)
    df, db_f = _cum_bwd(dcq_sl, dck_sl, f_logit, bf_row, "cum_bwd")
    dkv = _concat_cast(dk, dv, "fox_dkv")
    d_w_kvm = _mm(hn_kv, dkv, "tn", F32, "kv_dw")
    d_w_f = _mm(hn_kv, df, "tn", F32, "f_dw")
    dhn_f = _mm(df, w_f, "nt", F32, "f_dx")
    dhn_kv = _mm(dkv, w_kvm, "nt", F32, "kv_dx", add=dhn_f)
    d_w_kv = jnp.concatenate([d_w_kvm, d_w_f[:, :nh]], axis=1)
    dh1, dg_pre1, dg_kv = _norm_bwd2(dh2, h1, dxn1, dhn_kv, g_pre1, g_kv, "resid_norms_bwd")

    dh1p = _perm(dh1)
    do0, dg_post0 = _post_norm_bwd(dh1p, o0, g_post0, "norm_post0_bwd")
    d_w_out = _mm(y3, do0, "tn", F32, "s5_out_dw")
    dy3 = _mm(do0, w_out, "nt", F32, "s5_out_dx")
    dz0, dgp, dyg_direct, db_glu = _s5_gate_bwd(dy3, y_ssm, gp, bglu_row, z0, "s5_gate_bwd")
    d_w_glu = _mm(yg, dgp, "tn", F32, "s5_glu_dw")
    dyg = _mm(dgp, w_glu, "nt", F32, "s5_glu_dx", add=dyg_direct)
    dy_ssm = _gelu_bwd(dyg, y_ssm, "s5_gelu_bwd")
    ab_imn = -ab_im
    ge_re, ge_im = _s5_scan_bwd(dy_ssm, u, h_re, h_im, bd_re, bd_im, cd_re, cd_im, ab_re, ab_imn, zero_seg, zero_seg,
                                d_row, False, "s5_adj_ends")
    gi_re, gi_im = _s5_seg_fix(ge_re, ge_im, ab_re, ab_imn, seg_len, True, "s5_adj_fix")
    du, dbd_re, dbd_im, dcd_re, dcd_im, dab_re, dab_im, dd = _s5_scan_bwd(
        dy_ssm, u, h_re, h_im, bd_re, bd_im, cd_re, cd_im, ab_re, ab_imn, gi_re, gi_im, d_row, True, "s5_adj")
    duz = _concat_cast(du, dz0, "s5_duz")
    d_w_in = _mm(xn0, duz, "tn", F32, "s5_in_dw")
    dxn0 = _mm(duz, w_in, "nt", F32, "s5_in_dx")
    dxp, dg_pre0 = _norm_bwd1(dh1p, xp, dxn0, g_pre0, "norm_pre0_bwd")
    grad_x = _unperm(dxp)

    dbb_re = _block_diag_in_extract(dbd_re, p, S5_GROUP).reshape(g * p, S5_GROUP)
    dbb_im = _block_diag_in_extract(dbd_im, p, S5_GROUP).reshape(g * p, S5_GROUP)
    dcr_col, dci_col, db_re, db_im = _s5_bbar_bwd(cr_col, ci_col, b_re2, b_im2, dbb_re, dbb_im)
    da_re, da_im, dldt = _s5_disc_bwd(a_re, a_im, ldt, dab_re.reshape(g, p), dab_im.reshape(g, p),
                                      dcr_col.reshape(g, p), dci_col.reshape(g, p))
    dc_re = _block_diag_out_extract(dcd_re, S5_GROUP, p)
    dc_im = -_block_diag_out_extract(dcd_im, S5_GROUP, p)

    small = dict(
        norm_pre=jnp.concatenate([dg_pre0, dg_pre1], axis=0), norm_post=jnp.concatenate([dg_post0, dg_post1], axis=0),
        s5_a_re=da_re, s5_a_im=da_im, s5_log_dt=dldt.reshape(g), s5_b_re=db_re.reshape(g, p, S5_GROUP),
        s5_b_im=db_im.reshape(g, p, S5_GROUP), s5_c_re=dc_re, s5_c_im=dc_im, s5_d=dd.reshape(-1),
        s5_b_glu=db_glu.reshape(-1), kv_norm=dg_kv.reshape(-1), kv_b_f=db_f[0, :nh])
    big = dict(s5_w_in=d_w_in, s5_w_glu=d_w_glu, s5_w_out=d_w_out, kv_w=d_w_kv, fox_w_in=d_fw_in, fox_w_out=d_fw_out)
    return loss, grad_x, big, small


_BIG = ("s5_w_in", "s5_w_glu", "s5_w_out", "kv_w", "fox_w_in", "fox_w_out")
_COL_SHARDED = ("s5_w_in", "kv_w", "fox_w_in")
_SMALL = ("norm_pre", "norm_post", "s5_a_re", "s5_a_im", "s5_log_dt", "s5_b_re", "s5_b_im", "s5_c_re", "s5_c_im",
          "s5_d", "s5_b_glu", "kv_norm", "kv_b_f")
_SMALL_SHARDED = ("s5_d", "s5_b_glu")
_PACK_QUANTUM = SUBLANES * LANES
_WEIGHTS = ('norm_pre', 'norm_post', 's5_w_in', 's5_a_re', 's5_a_im', 's5_log_dt', 's5_b_re', 's5_b_im', 's5_c_re', 's5_c_im',
            's5_d', 's5_w_glu', 's5_b_glu', 's5_w_out', 'kv_norm', 'kv_w', 'kv_b_f', 'fox_w_in', 'fox_w_out')


def _full_from_slots(name, slots):
    n, r, c = slots.shape
    if name in _COL_SHARDED:
        return slots.transpose(1, 0, 2).reshape(r, n * c)
    return slots.reshape(n * r, c)


def _slots_from_full(name, full):
    if name in _COL_SHARDED:
        r, nc = full.shape
        return full.reshape(r, N_DEV, nc // N_DEV).transpose(1, 0, 2)
    nr, c = full.shape
    return full.reshape(N_DEV, nr // N_DEV, c)


def _pack(vals):
    parts = []
    for v in vals:
        flat = v.reshape(-1)
        parts.append(jnp.pad(flat, (0, (-flat.shape[0]) % _PACK_QUANTUM)))
    return jnp.concatenate(parts).reshape(-1, LANES)


def _unpack(packed, shapes):
    flat = packed.reshape(-1)
    out, off = [], 0
    for sh in shapes:
        n = math.prod(sh)
        out.append(flat[off:off + n].reshape(sh))
        off += n + (-n) % _PACK_QUANTUM
    return out


def kernel(x, norm_pre, norm_post, s5_w_in, s5_a_re, s5_a_im, s5_log_dt, s5_b_re, s5_b_im, s5_c_re, s5_c_im, s5_d, s5_w_glu, s5_b_glu, s5_w_out, kv_norm, kv_w, kv_b_f, fox_w_in, fox_w_out, loss_target, m_norm_pre, m_norm_post, m_s5_w_in, m_s5_a_re, m_s5_a_im, m_s5_log_dt, m_s5_b_re, m_s5_b_im, m_s5_c_re, m_s5_c_im, m_s5_d, m_s5_w_glu, m_s5_b_glu, m_s5_w_out, m_kv_norm, m_kv_w, m_kv_b_f, m_fox_w_in, m_fox_w_out, v_norm_pre, v_norm_post, v_s5_w_in, v_s5_a_re, v_s5_a_im, v_s5_log_dt, v_s5_b_re, v_s5_b_im, v_s5_c_re, v_s5_c_im, v_s5_d, v_s5_w_glu, v_s5_b_glu, v_s5_w_out, v_kv_norm, v_kv_w, v_kv_b_f, v_fox_w_in, v_fox_w_out):
    env = dict(locals())
    wts = {n: env[n] for n in _WEIGHTS}
    mom = {n: env["m_" + n] for n in _WEIGHTS}
    var = {n: env["v_" + n] for n in _WEIGHTS}
    me = 4 * lax.axis_index("x") + 2 * lax.axis_index("y") + lax.axis_index("c")

    shard2d = {n: wts[n].reshape(wts[n].shape[-2:]) for n in _BIG}
    gathered = _exchange([shard2d[n].astype(BF16) for n in _BIG] + [wts[n].reshape(1, -1) for n in _SMALL_SHARDED],
                         False, "gather_weights")
    full = {n: _full_from_slots(n, gathered[i]) for i, n in enumerate(_BIG)}
    d_skip = gathered[len(_BIG)].reshape(-1)
    b_glu = gathered[len(_BIG) + 1].reshape(-1)

    loss_local, grad_x, big, small = _local_step(
        x[0], loss_target[0], norm_pre, norm_post, kv_norm, kv_b_f, s5_a_re[0], s5_a_im[0], s5_log_dt[0],
        s5_b_re[0], s5_b_im[0], s5_c_re[0], s5_c_im[0], d_skip, b_glu,
        full["s5_w_in"], full["s5_w_glu"], full["s5_w_out"], full["kv_w"], full["fox_w_in"], full["fox_w_out"])
    loss = lax.psum(loss_local, MESH_AXES)

    recv = _exchange([_slots_from_full(n, big[n]).astype(BF16) for n in _BIG], True, "exchange_grads")
    res = {}
    for i, n in enumerate(_BIG):
        outs = _adamw(recv[i], shard2d[n], mom[n].reshape(shard2d[n].shape), var[n].reshape(shard2d[n].shape), "adamw_" + n)
        res[n] = [o.reshape(wts[n].shape) for o in outs]

    full_shape = {n: (small[n].shape if n in _SMALL_SHARDED else wts[n].shape) for n in _SMALL}

    def spread(n, v):
        if n not in _SMALL_SHARDED:
            return v
        flat = v.reshape(-1)
        return lax.dynamic_update_slice(jnp.zeros(full_shape[n], F32), flat, (me * flat.shape[0],))

    g_all = _exchange([_pack([small[n] for n in _SMALL])], False, "gather_small_grads")[0]
    packed = [_pack([spread(n, src[n]) for n in _SMALL]) for src in (wts, mom, var)]
    outs = _adamw(g_all, *packed, "adamw_small")
    unpacked = [_unpack(o, [full_shape[n] for n in _SMALL]) for o in outs]
    for i, n in enumerate(_SMALL):
        vals = [u[i] for u in unpacked]
        if n in _SMALL_SHARDED:
            k = wts[n].size
            vals = [lax.dynamic_slice(v, (me * k,), (k,)) for v in vals]
        res[n] = [v.reshape(wts[n].shape) for v in vals]

    return (loss, grad_x[None], *[res[n][0] for n in _WEIGHTS], *[res[n][1] for n in _WEIGHTS],
            *[res[n][2] for n in _WEIGHTS], *[res[n][3] for n in _WEIGHTS])
```

```python
import functools
import math

import jax
import jax.numpy as jnp
from jax import lax
from jax.experimental import pallas as pl
from jax.experimental.pallas import tpu as pltpu

F32 = jnp.float32
BF16 = jnp.bfloat16

N_DEV = 8
MESH_AXES = ("x", "y", "c")
S5_GROUP = 16
S5_STATE = 64
LANES = 128
SUBLANES = 8
GROUPS_PER_BLOCK = LANES // S5_GROUP
BLOCK_STATE = GROUPS_PER_BLOCK * S5_STATE
N_SEG = SUBLANES
HEAD_DIM = 128
RMS_EPS = 1e-6
NEG_INF = -1e30
LOG2E = math.log2(math.e)
ADAM_LR = 0.001
ADAM_B1 = 0.9
ADAM_B2 = 0.999
ADAM_EPS = 1e-08
ADAM_WD = 0.01
ADAM_STEP = 10
VMEM_LIMIT = 56 * 1024 * 1024


def _tile(n, pref, quantum=LANES):
    if n <= pref:
        return n
    t = (pref // quantum) * quantum
    while t >= quantum:
        if n % t == 0:
            return t
        t -= quantum
    return n


def _cparams(*sem):
    return pltpu.CompilerParams(dimension_semantics=sem if sem else None, vmem_limit_bytes=VMEM_LIMIT)


_DOT_DIMS = {"nn": ((1,), (0,)), "nt": ((1,), (1,)), "tn": ((0,), (0,))}


def _mm(a, b, mode, out_dtype, name, add=None, scale=None, b_cols=None):
    b_shape = b.shape if b_cols is None else (b.shape[0], b_cols[1])
    if mode == "nn":
        (M, K), (K2, N) = a.shape, b_shape
    elif mode == "nt":
        (M, K), (N, K2) = a.shape, b_shape
    else:
        (K, M), (K2, N) = a.shape, b_shape
    assert K == K2, (name, a.shape, b_shape)
    tm, tn, tk = _tile(M, 1024), _tile(N, 1024), _tile(K, 1024)
    nk = K // tk
    dims = (_DOT_DIMS[mode], ((), ()))
    col0 = 0
    if b_cols is not None:
        assert mode != "tn" and b_cols[0] % (tn if mode == "nn" else tk) == 0
        col0 = b_cols[0] // (tn if mode == "nn" else tk)

    def body(*refs):
        if add is None:
            a_ref, b_ref, o_ref, acc = refs
        else:
            a_ref, b_ref, c_ref, o_ref, acc = refs
        k = pl.program_id(2)

        @pl.when(k == 0)
        def _():
            acc[...] = jnp.zeros_like(acc)

        acc[...] += lax.dot_general(a_ref[...], b_ref[...], dims, preferred_element_type=F32)

        @pl.when(k == nk - 1)
        def _():
            r = acc[...]
            if scale is not None:
                r = r * scale
            if add is not None:
                r = r + c_ref[...]
            o_ref[...] = r.astype(out_dtype)

    if mode == "tn":
        a_spec = pl.BlockSpec((tk, tm), lambda i, j, k: (k, i))
    else:
        a_spec = pl.BlockSpec((tm, tk), lambda i, j, k: (i, k))
    if mode == "nt":
        b_spec = pl.BlockSpec((tn, tk), lambda i, j, k: (j, k + col0))
    else:
        b_spec = pl.BlockSpec((tk, tn), lambda i, j, k: (k, j + col0))
    o_spec = pl.BlockSpec((tm, tn), lambda i, j, k: (i, j))
    in_specs = [a_spec, b_spec] + ([o_spec] if add is not None else [])
    args = (a, b) + ((add,) if add is not None else ())
    return pl.pallas_call(
        body, name=name, grid=(M // tm, N // tn, nk),
        in_specs=in_specs, out_specs=o_spec,
        out_shape=jax.ShapeDtypeStruct((M, N), out_dtype),
        scratch_shapes=[pltpu.VMEM((tm, tn), F32)],
        compiler_params=_cparams("parallel", "parallel", "arbitrary"),
    )(*args)


class _NatIn:
    def __init__(self, ref):
        self.ref = ref

    def __getitem__(self, idx):
        v = jnp.swapaxes(self.ref[...], 0, 1)
        return v.reshape(v.shape[0] * N_SEG, v.shape[2])


class _NatOut:
    def __init__(self, ref):
        self.ref = ref

    def __setitem__(self, idx, val):
        self.ref[...] = jnp.swapaxes(val.reshape(val.shape[0] // N_SEG, N_SEG, val.shape[1]), 0, 1)


def _rowcall(body, name, n_rows, ins, outs, tile_rows=256):
    tr = _tile(n_rows, tile_rows, SUBLANES * 2)
    n_in = len(ins)
    in_kinds = [k for _, k in ins]
    kinds = [k for _, _, k in outs]

    def kern(*refs):
        @pl.when(pl.program_id(0) == 0)
        def _():
            for r, kind in zip(refs[n_in:], kinds):
                if kind == "acc":
                    r[...] = jnp.zeros_like(r)

        wrapped = [_NatIn(r) if k == "nat" else r for r, k in zip(refs[:n_in], in_kinds)]
        wrapped += [_NatOut(r) if k == "nat" else r for r, k in zip(refs[n_in:], kinds)]
        body(*wrapped)

    in_specs, args = [], []
    for arr, kind in ins:
        if kind == "row":
            in_specs.append(pl.BlockSpec((tr, arr.shape[1]), lambda i: (i, 0)))
        elif kind == "nat":
            in_specs.append(pl.BlockSpec((N_SEG, tr // N_SEG, arr.shape[1]), lambda i: (0, i, 0)))
            arr = arr.reshape(N_SEG, n_rows // N_SEG, arr.shape[1])
        else:
            in_specs.append(pl.BlockSpec(arr.shape, lambda i, nd=arr.ndim: (0,) * nd))
        args.append(arr)
    out_specs, out_shape = [], []
    for width, dtype, kind in outs:
        if kind == "row":
            out_specs.append(pl.BlockSpec((tr, width), lambda i: (i, 0)))
            out_shape.append(jax.ShapeDtypeStruct((n_rows, width), dtype))
        elif kind == "nat":
            out_specs.append(pl.BlockSpec((N_SEG, tr // N_SEG, width), lambda i: (0, i, 0)))
            out_shape.append(jax.ShapeDtypeStruct((N_SEG, n_rows // N_SEG, width), dtype))
        else:
            out_specs.append(pl.BlockSpec((1, width), lambda i: (0, 0)))
            out_shape.append(jax.ShapeDtypeStruct((1, width), F32))
    res = pl.pallas_call(
        kern, name=name, grid=(n_rows // tr,), in_specs=in_specs, out_specs=out_specs, out_shape=out_shape,
        compiler_params=_cparams("arbitrary"),
    )(*args)
    return [r.reshape(n_rows, r.shape[2]) if k == "nat" else r for r, k in zip(res, kinds)]


def _rstd(x):
    return lax.rsqrt(jnp.mean(x * x, axis=-1, keepdims=True) + RMS_EPS)


def _rms_bwd(x, g, dy):
    xh = x * _rstd(x)
    dxh = dy * g
    dx = _rstd(x) * (dxh - xh * jnp.mean(dxh * xh, axis=-1, keepdims=True))
    return dx, jnp.sum(dy * xh, axis=0, keepdims=True)


def _silu(z):
    return z * jax.nn.sigmoid(z)


def _norm_cast(x, g, name, x_kind="row"):
    def body(x_ref, g_ref, o_ref):
        x = x_ref[...]
        o_ref[...] = (x * _rstd(x) * g_ref[...]).astype(BF16)

    return _rowcall(body, name, x.shape[0], [(x, x_kind), (g, "full")], [(x.shape[1], BF16, "row")])[0]


def _resid_norm2(x, r0, g_kv, g_pre, name):
    def body(x_ref, r_ref, gk_ref, gp_ref, h_ref, nk_ref, np_ref):
        h = x_ref[...] + r_ref[...]
        h_ref[...] = h
        hn = h * _rstd(h)
        nk_ref[...] = (hn * gk_ref[...]).astype(BF16)
        np_ref[...] = (hn * gp_ref[...]).astype(BF16)

    d = x.shape[1]
    return _rowcall(body, name, x.shape[0], [(x, "row"), (r0, "row"), (g_kv, "full"), (g_pre, "full")],
                    [(d, F32, "row"), (d, BF16, "row"), (d, BF16, "row")])


def _post_norm(o, g, name, out_kind="row"):
    def body(o_ref, g_ref, r_ref):
        o = o_ref[...]
        r_ref[...] = o * _rstd(o) * g_ref[...]

    return _rowcall(body, name, o.shape[0], [(o, "row"), (g, "full")], [(o.shape[1], F32, out_kind)])[0]


def _post_norm_loss(o, g, h1, target, name):
    d = o.shape[1]

    def body(o_ref, g_ref, h_ref, t_ref, dh_ref, acc_ref):
        o = o_ref[...]
        e = h_ref[...] + o * _rstd(o) * g_ref[...] - t_ref[...]
        dh_ref[...] = e * (1.0 / d)
        acc_ref[...] += jnp.sum(e * e, axis=0, keepdims=True)

    return _rowcall(body, name, o.shape[0], [(o, "row"), (g, "full"), (h1, "row"), (target, "row")],
                    [(d, F32, "row"), (d, F32, "acc")])


def _post_norm_bwd(dy, o, g, name, dy_kind="row"):
    def body(dy_ref, o_ref, g_ref, do_ref, dg_ref):
        dx, dg = _rms_bwd(o_ref[...], g_ref[...], dy_ref[...])
        do_ref[...] = dx.astype(BF16)
        dg_ref[...] += dg

    d = o.shape[1]
    return _rowcall(body, name, o.shape[0], [(dy, dy_kind), (o, "row"), (g, "full")], [(d, BF16, "row"), (d, F32, "acc")])


def _gate_mul(o, z, name):
    def body(o_ref, z_ref, r_ref):
        r_ref[...] = (o_ref[...] * _silu(z_ref[...])).astype(BF16)

    return _rowcall(body, name, o.shape[0], [(o, "row"), (z, "row")], [(o.shape[1], BF16, "row")])[0]


def _gate_bwd(d_oz, o, z, name):
    def body(d_ref, o_ref, z_ref, do_ref, dz_ref):
        _, vjp = jax.vjp(lambda o, z: o * _silu(z), o_ref[...], z_ref[...])
        do, dz = vjp(d_ref[...])
        do_ref[...] = do.astype(BF16)
        dz_ref[...] = dz.astype(BF16)

    w = o.shape[1]
    return _rowcall(body, name, o.shape[0], [(d_oz, "row"), (o, "row"), (z, "row")], [(w, BF16, "row"), (w, BF16, "row")])


def _norm_bwd2(dh2, h1, dxn1, dhn_kv, g_pre, g_kv, name):
    def body(dh2_ref, h_ref, d1_ref, dk_ref, gp_ref, gk_ref, dh1_ref, dgp_ref, dgk_ref):
        h = h_ref[...]
        dx1, dg1 = _rms_bwd(h, gp_ref[...], d1_ref[...])
        dxk, dgk = _rms_bwd(h, gk_ref[...], dk_ref[...])
        dh1_ref[...] = dh2_ref[...] + dx1 + dxk
        dgp_ref[...] += dg1
        dgk_ref[...] += dgk

    d = h1.shape[1]
    return _rowcall(body, name, h1.shape[0],
                    [(dh2, "row"), (h1, "row"), (dxn1, "row"), (dhn_kv, "row"), (g_pre, "full"), (g_kv, "full")],
                    [(d, F32, "row"), (d, F32, "acc"), (d, F32, "acc")])


def _norm_bwd1(dres, x, dxn, g, name):
    def body(dr_ref, x_ref, dn_ref, g_ref, dx_ref, dg_ref):
        dx, dg = _rms_bwd(x_ref[...], g_ref[...], dn_ref[...])
        dx_ref[...] = dr_ref[...] + dx
        dg_ref[...] += dg

    d = x.shape[1]
    return _rowcall(body, name, x.shape[0], [(dres, "nat"), (x, "nat"), (dxn, "row"), (g, "full")],
                    [(d, F32, "nat"), (d, F32, "acc")])


def _gelu_cast(y, name):
    def body(y_ref, o_ref):
        o_ref[...] = jax.nn.gelu(y_ref[...]).astype(BF16)

    return _rowcall(body, name, y.shape[0], [(y, "row")], [(y.shape[1], BF16, "row")])[0]


def _s5_gate(y_ssm, gp, b_glu, z, name):
    def body(y_ref, gp_ref, b_ref, z_ref, o_ref):
        yg = jax.nn.gelu(y_ref[...])
        o_ref[...] = (yg * jax.nn.sigmoid(gp_ref[...] + b_ref[...]) * _silu(z_ref[...])).astype(BF16)

    return _rowcall(body, name, y_ssm.shape[0], [(y_ssm, "row"), (gp, "row"), (b_glu, "full"), (z, "row")],
                    [(y_ssm.shape[1], BF16, "row")])[0]


def _s5_gate_bwd(dy3, y_ssm, gp, b_glu, z, name):
    def body(d_ref, y_ref, gp_ref, b_ref, z_ref, dz_ref, dgp_ref, dyg_ref, db_ref):
        yg = jax.nn.gelu(y_ref[...])
        _, vjp = jax.vjp(lambda yg, gp, z: yg * jax.nn.sigmoid(gp) * _silu(z), yg, gp_ref[...] + b_ref[...], z_ref[...])
        dyg, dgp, dz = vjp(d_ref[...])
        dz_ref[...] = dz.astype(BF16)
        dgp_ref[...] = dgp.astype(BF16)
        dyg_ref[...] = dyg
        db_ref[...] += jnp.sum(dgp, axis=0, keepdims=True)

    w = y_ssm.shape[1]
    return _rowcall(body, name, y_ssm.shape[0],
                    [(dy3, "row"), (y_ssm, "row"), (gp, "row"), (b_glu, "full"), (z, "row")],
                    [(w, BF16, "row"), (w, BF16, "row"), (w, F32, "row"), (w, F32, "acc")])


def _gelu_bwd(dyg, y_ssm, name):
    def body(d_ref, y_ref, o_ref):
        _, vjp = jax.vjp(jax.nn.gelu, y_ref[...])
        o_ref[...] = vjp(d_ref[...])[0]

    return _rowcall(body, name, y_ssm.shape[0], [(dyg, "row"), (y_ssm, "row")], [(y_ssm.shape[1], F32, "row")])[0]


def _concat_cast(a, b, name):
    def body(a_ref, b_ref, o_ref):
        w = a_ref.shape[1]
        o_ref[:, :w] = a_ref[...].astype(BF16)
        o_ref[:, w:] = b_ref[...].astype(BF16)

    return _rowcall(body, name, a.shape[0], [(a, "row"), (b, "row")], [(a.shape[1] + b.shape[1], BF16, "row")])[0]


def _disc(ar, ai, ldt):
    dt = jnp.exp(ldt)
    mag = jnp.exp(ar * dt)
    abr = mag * jnp.cos(ai * dt)
    abi = mag * jnp.sin(ai * dt)
    den = ar * ar + ai * ai
    nr = abr - 1.0
    return abr, abi, (nr * ar + abi * ai) / den, (abi * ar - nr * ai) / den


def _s5_disc_fwd(a_re, a_im, ldt):
    def body(ar, ai, ld, o1, o2, o3, o4):
        o1[...], o2[...], o3[...], o4[...] = _disc(ar[...], ai[...], ld[...])

    sh = jax.ShapeDtypeStruct(a_re.shape, F32)
    return pl.pallas_call(body, name="s5_disc_fwd", out_shape=(sh, sh, sh, sh))(a_re, a_im, ldt)


def _s5_disc_bwd(a_re, a_im, ldt, d_abr, d_abi, d_cr, d_ci):
    def body(ar, ai, ld, g1, g2, g3, g4, o1, o2, o3):
        _, vjp = jax.vjp(_disc, ar[...], ai[...], ld[...])
        o1[...], o2[...], o3[...] = vjp((g1[...], g2[...], g3[...], g4[...]))

    sh = jax.ShapeDtypeStruct(a_re.shape, F32)
    return pl.pallas_call(body, name="s5_disc_bwd", out_shape=(sh, sh, jax.ShapeDtypeStruct(ldt.shape, F32)))(
        a_re, a_im, ldt, d_abr, d_abi, d_cr, d_ci)


def _bbar(cr, ci, br, bi):
    return cr * br - ci * bi, cr * bi + ci * br


def _s5_bbar_fwd(cr_col, ci_col, b_re, b_im):
    def body(cr, ci, br, bi, o1, o2):
        o1[...], o2[...] = _bbar(cr[...], ci[...], br[...], bi[...])

    w = b_re.shape[1]
    return _rowcall(body, "s5_bbar_fwd", b_re.shape[0], [(cr_col, "row"), (ci_col, "row"), (b_re, "row"), (b_im, "row")],
                    [(w, F32, "row"), (w, F32, "row")], tile_rows=1024)


def _s5_bbar_bwd(cr_col, ci_col, b_re, b_im, d_re, d_im):
    def body(cr, ci, br, bi, g1, g2, o1, o2, o3, o4):
        _, vjp = jax.vjp(_bbar, cr[...], ci[...], br[...], bi[...])
        o1[...], o2[...], o3[...], o4[...] = vjp((g1[...], g2[...]))

    w = b_re.shape[1]
    return _rowcall(body, "s5_bbar_bwd", b_re.shape[0],
                    [(cr_col, "row"), (ci_col, "row"), (b_re, "row"), (b_im, "row"), (d_re, "row"), (d_im, "row")],
                    [(1, F32, "row"), (1, F32, "row"), (w, F32, "row"), (w, F32, "row")], tile_rows=1024)


def _block_diag_in(t):
    g, p, c = t.shape
    nb = g // GROUPS_PER_BLOCK
    t4 = t.reshape(nb, GROUPS_PER_BLOCK, p, c).transpose(0, 1, 3, 2)
    eye = jnp.eye(GROUPS_PER_BLOCK, dtype=t.dtype)
    return (t4[:, :, :, None, :] * eye[None, :, None, :, None]).reshape(nb, GROUPS_PER_BLOCK * c, GROUPS_PER_BLOCK * p)


def _block_diag_in_extract(d, p, c):
    nb = d.shape[0]
    d5 = d.reshape(nb, GROUPS_PER_BLOCK, c, GROUPS_PER_BLOCK, p)
    diag = jnp.stack([d5[:, g, :, g, :] for g in range(GROUPS_PER_BLOCK)], axis=1)
    return diag.transpose(0, 1, 3, 2).reshape(nb * GROUPS_PER_BLOCK, p, c)


def _block_diag_out(t):
    g, c, p = t.shape
    nb = g // GROUPS_PER_BLOCK
    t4 = t.reshape(nb, GROUPS_PER_BLOCK, c, p).transpose(0, 1, 3, 2)
    eye = jnp.eye(GROUPS_PER_BLOCK, dtype=t.dtype)
    return (t4[:, :, :, None, :] * eye[None, :, None, :, None]).reshape(nb, GROUPS_PER_BLOCK * p, GROUPS_PER_BLOCK * c)


def _block_diag_out_extract(d, c, p):
    nb = d.shape[0]
    d5 = d.reshape(nb, GROUPS_PER_BLOCK, p, GROUPS_PER_BLOCK, c)
    diag = jnp.stack([d5[:, g, :, g, :] for g in range(GROUPS_PER_BLOCK)], axis=1)
    return diag.transpose(0, 1, 3, 2).reshape(nb * GROUPS_PER_BLOCK, c, p)


def _scan_step(ar, ai, hr, hi, xr, xi):
    return ar * hr - ai * hi + xr, ar * hi + ai * hr + xi


def _s5_scan_fwd(u, bd_re, bd_im, cd_re, cd_im, ab_re, ab_im, init_re, init_im, d_row, full, name):
    s, w = u.shape
    nb = w // LANES
    rows = _tile(s, 512, SUBLANES)
    nc = s // rows
    steps = rows // N_SEG
    ns = nb * BLOCK_STATE

    def body(u_ref, bdr, bdi, cdr, cdi, ar_ref, ai_ref, ir_ref, ii_ref, d_ref, *outs):
        if full:
            y_ref, hr_ref, hi_ref, er_ref, ei_ref, cr, ci = outs
        else:
            er_ref, ei_ref, hr_ref, hi_ref, cr, ci = outs
        c = pl.program_id(1)

        @pl.when(c == 0)
        def _():
            cr[...] = ir_ref[...]
            ci[...] = ii_ref[...]

        ub = u_ref[...].astype(BF16)
        hr_ref[...] = jnp.dot(ub, bdr[...], preferred_element_type=F32)
        hi_ref[...] = jnp.dot(ub, bdi[...], preferred_element_type=F32)
        ar, ai = ar_ref[...], ai_ref[...]

        def step(j, carry):
            off = pl.multiple_of(j * N_SEG, N_SEG)
            nr, ni = _scan_step(ar, ai, carry[0], carry[1], hr_ref[pl.ds(off, N_SEG), :], hi_ref[pl.ds(off, N_SEG), :])
            hr_ref[pl.ds(off, N_SEG), :] = nr
            hi_ref[pl.ds(off, N_SEG), :] = ni
            return nr, ni

        hr, hi = lax.fori_loop(0, steps, step, (cr[...], ci[...]), unroll=8)
        cr[...] = hr
        ci[...] = hi
        if full:
            y_ref[...] = (jnp.dot(hr_ref[...].astype(BF16), cdr[...], preferred_element_type=F32)
                          + jnp.dot(hi_ref[...].astype(BF16), cdi[...], preferred_element_type=F32)
                          + d_ref[...] * u_ref[...])

        @pl.when(c == nc - 1)
        def _():
            er_ref[...] = hr
            ei_ref[...] = hi

    blk3 = lambda a: pl.BlockSpec((None,) + a.shape[1:], lambda k, c: (k, 0, 0))
    seg = pl.BlockSpec((N_SEG, BLOCK_STATE), lambda k, c: (0, k))
    st = pl.BlockSpec((rows, BLOCK_STATE), lambda k, c: (c, k))
    in_specs = [pl.BlockSpec((rows, LANES), lambda k, c: (c, k)), blk3(bd_re), blk3(bd_im), blk3(cd_re), blk3(cd_im),
                seg, seg, seg, seg, pl.BlockSpec((1, LANES), lambda k, c: (0, k))]
    seg_shape = jax.ShapeDtypeStruct((N_SEG, ns), F32)
    st_shape = jax.ShapeDtypeStruct((s, ns), F32)
    carry = [pltpu.VMEM((N_SEG, BLOCK_STATE), F32)] * 2
    if full:
        out_specs = [pl.BlockSpec((rows, LANES), lambda k, c: (c, k)), st, st, seg, seg]
        out_shape = [jax.ShapeDtypeStruct((s, w), F32), st_shape, st_shape, seg_shape, seg_shape]
        scratch = carry
    else:
        out_specs = [seg, seg]
        out_shape = [seg_shape, seg_shape]
        scratch = [pltpu.VMEM((rows, BLOCK_STATE), F32)] * 2 + carry
    return pl.pallas_call(
        body, name=name, grid=(nb, nc), in_specs=in_specs, out_specs=out_specs, out_shape=out_shape,
        scratch_shapes=scratch, compiler_params=_cparams("parallel", "arbitrary"),
    )(u, bd_re, bd_im, cd_re, cd_im, ab_re, ab_im, init_re, init_im, d_row)


def _s5_seg_fix(e_re, e_im, ab_re, ab_im, seg_len, reverse, name):
    assert seg_len & (seg_len - 1) == 0

    def body(er, ei, ar, ai, o_re, o_im):
        pr, pi = ar[0:1, :], ai[0:1, :]
        for _ in range(int(math.log2(seg_len))):
            pr, pi = pr * pr - pi * pi, 2.0 * pr * pi
        tr = jnp.zeros_like(pr)
        ti = jnp.zeros_like(pr)
        order = list(range(N_SEG - 1, -1, -1)) if reverse else list(range(N_SEG))
        for n, sgm in enumerate(order):
            o_re[sgm:sgm + 1, :] = tr
            o_im[sgm:sgm + 1, :] = ti
            if n < N_SEG - 1:
                tr, ti = _scan_step(pr, pi, tr, ti, er[sgm:sgm + 1, :], ei[sgm:sgm + 1, :])

    sh = jax.ShapeDtypeStruct(e_re.shape, F32)
    return pl.pallas_call(body, name=name, out_shape=(sh, sh))(e_re, e_im, ab_re, ab_im)


def _s5_scan_bwd(dy, u, h_re, h_im, bd_re, bd_im, cd_re, cd_im, ab_re, ab_imn, gin_re, gin_im, d_row, full, name):
    s, w = u.shape
    nb = w // LANES
    rows = _tile(s, 512, SUBLANES)
    nc = s // rows
    steps = rows // N_SEG
    ns = nb * BLOCK_STATE

    def body(dy_ref, u_ref, hr_ref, hi_ref, bdr, bdi, cdr, cdi, ar_ref, ai_ref, ir_ref, ii_ref, d_ref, *outs):
        if full:
            du_ref, dbr_ref, dbi_ref, dcr_ref, dci_ref, dar_ref, dai_ref, dd_ref, gr, gi, accr, acci = outs
        else:
            er_ref, ei_ref, gr, gi = outs
        c = pl.program_id(1)

        @pl.when(c == 0)
        def _():
            gr[pl.ds(rows, N_SEG), :] = ir_ref[...]
            gi[pl.ds(rows, N_SEG), :] = ii_ref[...]
            if full:
                for r in (dbr_ref, dbi_ref, dcr_ref, dci_ref, dd_ref, accr, acci):
                    r[...] = jnp.zeros_like(r)

        dyb = dy_ref[...].astype(BF16)
        nt = (_DOT_DIMS["nt"], ((), ()))
        tn = (_DOT_DIMS["tn"], ((), ()))
        gr[pl.ds(0, rows), :] = lax.dot_general(dyb, cdr[...], nt, preferred_element_type=F32)
        gi[pl.ds(0, rows), :] = lax.dot_general(dyb, cdi[...], nt, preferred_element_type=F32)
        ar, ai = ar_ref[...], ai_ref[...]

        def step(jj, carry):
            off = pl.multiple_of((steps - 1 - jj) * N_SEG, N_SEG)
            nr, ni = _scan_step(ar, ai, carry[0], carry[1], gr[pl.ds(off, N_SEG), :], gi[pl.ds(off, N_SEG), :])
            gr[pl.ds(off, N_SEG), :] = nr
            gi[pl.ds(off, N_SEG), :] = ni
            return nr, ni

        g0r, g0i = lax.fori_loop(0, steps, step, (gr[pl.ds(rows, N_SEG), :], gi[pl.ds(rows, N_SEG), :]), unroll=8)
        if full:
            hr, hi = hr_ref[...], hi_ref[...]
            gnr, gni = gr[pl.ds(N_SEG, rows), :], gi[pl.ds(N_SEG, rows), :]
            accr[...] += jnp.sum((gnr * hr + gni * hi).reshape(steps, N_SEG, BLOCK_STATE), axis=0)
            acci[...] += jnp.sum((gni * hr - gnr * hi).reshape(steps, N_SEG, BLOCK_STATE), axis=0)
        gr[pl.ds(rows, N_SEG), :] = g0r
        gi[pl.ds(rows, N_SEG), :] = g0i
        if full:
            ub = u_ref[...].astype(BF16)
            gbr, gbi = gr[pl.ds(0, rows), :].astype(BF16), gi[pl.ds(0, rows), :].astype(BF16)
            dcr_ref[...] += lax.dot_general(hr.astype(BF16), dyb, tn, preferred_element_type=F32)
            dci_ref[...] += lax.dot_general(hi.astype(BF16), dyb, tn, preferred_element_type=F32)
            dbr_ref[...] += lax.dot_general(ub, gbr, tn, preferred_element_type=F32)
            dbi_ref[...] += lax.dot_general(ub, gbi, tn, preferred_element_type=F32)
            du_ref[...] = (lax.dot_general(gbr, bdr[...], nt, preferred_element_type=F32)
                           + lax.dot_general(gbi, bdi[...], nt, preferred_element_type=F32)
                           + d_ref[...] * dy_ref[...])
            dd_ref[...] += jnp.sum(dy_ref[...] * u_ref[...], axis=0, keepdims=True)

        @pl.when(c == nc - 1)
        def _():
            if full:
                dar_ref[...] = jnp.sum(accr[...], axis=0, keepdims=True)
                dai_ref[...] = jnp.sum(acci[...], axis=0, keepdims=True)
            else:
                er_ref[...] = g0r
                ei_ref[...] = g0i

    rev = lambda k, c: (nc - 1 - c, k)
    blk3 = lambda a: pl.BlockSpec((None,) + a.shape[1:], lambda k, c: (k, 0, 0))
    seg = pl.BlockSpec((N_SEG, BLOCK_STATE), lambda k, c: (0, k))
    st = pl.BlockSpec((rows, BLOCK_STATE), rev)
    ch = pl.BlockSpec((rows, LANES), rev)
    vec = pl.BlockSpec((1, LANES), lambda k, c: (0, k))
    in_specs = [ch, ch, st, st, blk3(bd_re), blk3(bd_im), blk3(cd_re), blk3(cd_im), seg, seg, seg, seg, vec]
    gbuf = [pltpu.VMEM((rows + N_SEG, BLOCK_STATE), F32)] * 2
    if full:
        row1 = pl.BlockSpec((1, BLOCK_STATE), lambda k, c: (0, k))
        out_specs = [ch, blk3(bd_re), blk3(bd_im), blk3(cd_re), blk3(cd_im), row1, row1, vec]
        out_shape = [jax.ShapeDtypeStruct((s, w), F32),
                     jax.ShapeDtypeStruct(bd_re.shape, F32), jax.ShapeDtypeStruct(bd_im.shape, F32),
                     jax.ShapeDtypeStruct(cd_re.shape, F32), jax.ShapeDtypeStruct(cd_im.shape, F32),
                     jax.ShapeDtypeStruct((1, ns), F32), jax.ShapeDtypeStruct((1, ns), F32),
                     jax.ShapeDtypeStruct((1, w), F32)]
        scratch = gbuf + [pltpu.VMEM((N_SEG, BLOCK_STATE), F32)] * 2
    else:
        out_specs = [seg, seg]
        out_shape = [jax.ShapeDtypeStruct((N_SEG, ns), F32)] * 2
        scratch = gbuf
    return pl.pallas_call(
        body, name=name, grid=(nb, nc), in_specs=in_specs, out_specs=out_specs, out_shape=out_shape,
        scratch_shapes=scratch, compiler_params=_cparams("parallel", "arbitrary"),
    )(dy, u, h_re, h_im, bd_re, bd_im, cd_re, cd_im, ab_re, ab_imn, gin_re, gin_im, d_row)


def _log_sigmoid(x):
    return jnp.minimum(x, 0.0) - jnp.log(1.0 + jnp.exp(-jnp.abs(x)))


def _tri(n, upper):
    r = lax.broadcasted_iota(jnp.int32, (n, n), 0)
    c = lax.broadcasted_iota(jnp.int32, (n, n), 1)
    return jnp.where((c >= r) if upper else (r >= c), 1.0, 0.0).astype(F32)


def _cum_fwd(f_logit, b_row, name):
    s, w = f_logit.shape
    t = _tile(s, 256, SUBLANES)

    def body(f_ref, b_ref, o_ref, carry):
        @pl.when(pl.program_id(0) == 0)
        def _():
            carry[...] = jnp.zeros_like(carry)

        lf = _log_sigmoid(f_ref[...] + b_ref[...])
        cum = jnp.dot(_tri(t, False), lf, precision=lax.Precision.HIGHEST, preferred_element_type=F32) + carry[...]
        o_ref[...] = cum * LOG2E
        carry[...] = cum[t - 1:t, :]

    return pl.pallas_call(
        body, name=name, grid=(s // t,),
        in_specs=[pl.BlockSpec((t, w), lambda i: (i, 0)), pl.BlockSpec((1, w), lambda i: (0, 0))],
        out_specs=pl.BlockSpec((t, w), lambda i: (i, 0)), out_shape=jax.ShapeDtypeStruct((s, w), F32),
        scratch_shapes=[pltpu.VMEM((1, w), F32)], compiler_params=_cparams("arbitrary"),
    )(f_logit, b_row)


def _cum_bwd(dcq, dck, f_logit, b_row, name):
    s, w = f_logit.shape
    t = _tile(s, 256, SUBLANES)
    nt = s // t

    def body(q_ref, k_ref, f_ref, b_ref, df_ref, db_ref, carry):
        @pl.when(pl.program_id(0) == 0)
        def _():
            carry[...] = jnp.zeros_like(carry)
            db_ref[...] = jnp.zeros_like(db_ref)

        dc = q_ref[...] - k_ref[...]
        rc = jnp.dot(_tri(t, True), dc, precision=lax.Precision.HIGHEST, preferred_element_type=F32) + carry[...]
        carry[...] = rc[0:1, :]
        df = rc * (1.0 - jax.nn.sigmoid(f_ref[...] + b_ref[...]))
        df_ref[...] = df.astype(BF16)
        db_ref[...] += jnp.sum(df, axis=0, keepdims=True)

    rev = pl.BlockSpec((t, w), lambda i: (nt - 1 - i, 0))
    one = pl.BlockSpec((1, w), lambda i: (0, 0))
    return pl.pallas_call(
        body, name=name, grid=(nt,), in_specs=[rev, rev, rev, one], out_specs=[rev, one],
        out_shape=[jax.ShapeDtypeStruct((s, w), BF16), jax.ShapeDtypeStruct((1, w), F32)],
        scratch_shapes=[pltpu.VMEM((1, w), F32)], compiler_params=_cparams("arbitrary"),
    )(dcq, dck, f_logit, b_row)


def _head_col(cum_tile, h):
    lane = lax.broadcasted_iota(jnp.int32, cum_tile.shape, 1)
    return jnp.sum(jnp.where(lane == h, cum_tile, 0.0), axis=1, keepdims=True)


def _attn_tiles(s):
    return _tile(s, 512, LANES)


def _exp2_rows(sc, sub):
    return jnp.concatenate([jnp.exp2(sc[:, b * LANES:(b + 1) * LANES] - sub) for b in range(sc.shape[1] // LANES)], axis=1)


def _causal(sc, keys_on_rows):
    r = lax.broadcasted_iota(jnp.int32, sc.shape, 0)
    c = lax.broadcasted_iota(jnp.int32, sc.shape, 1)
    return jnp.where((r <= c) if keys_on_rows else (c <= r), sc, NEG_INF)


def _fox_fwd(q2, kv, cum2_t, name):
    s, w = q2.shape
    nh = w // HEAD_DIM
    tq = _attn_tiles(s)
    nq = s // tq
    nt = (_DOT_DIMS["nt"], ((), ()))

    def body(q_ref, k_ref, v_ref, ct_ref, o_ref, lse_ref, m_s, acc_s, vaug, s_buf):
        i = pl.program_id(1)

        @pl.when(i == 0)
        def _():
            vaug[:, :HEAD_DIM] = v_ref[...]
            vaug[:, HEAD_DIM:] = jnp.ones((s, LANES), BF16)

        qb = q_ref[...]
        m_s[...] = jnp.full_like(m_s, NEG_INF)
        acc_s[...] = jnp.zeros_like(acc_s)

        def scores(j):
            off = pl.multiple_of(j * tq, tq)
            return lax.dot_general(qb, k_ref[pl.ds(off, tq), :], nt, preferred_element_type=F32) - ct_ref[:, pl.ds(off, tq)]

        def softmax_pv(j, sc):
            m_old = m_s[...]
            m_new = jnp.maximum(m_old, jnp.max(sc, axis=1, keepdims=True))
            p = _exp2_rows(sc, m_new)
            alpha = jnp.exp2(m_old - m_new)
            pv = jnp.dot(p.astype(BF16), vaug[pl.ds(pl.multiple_of(j * tq, tq), tq), :], preferred_element_type=F32)
            acc_s[...] = jnp.concatenate([alpha, alpha], axis=1) * acc_s[...] + pv
            m_s[...] = m_new

        s_buf[...] = scores(0)

        def loop(j, carry):
            nxt = scores(j + 1)
            softmax_pv(j, s_buf[...])
            s_buf[...] = nxt
            return carry

        lax.fori_loop(0, i, loop, 0)
        softmax_pv(i, _causal(s_buf[...], False))
        l = acc_s[:, HEAD_DIM:]
        o_ref[...] = acc_s[:, :HEAD_DIM] / l
        lse_ref[...] = m_s[...] + jnp.log(l) * LOG2E

    return pl.pallas_call(
        body, name=name, grid=(nh, nq),
        in_specs=[pl.BlockSpec((tq, HEAD_DIM), lambda h, i: (i, h)),
                  pl.BlockSpec((s, HEAD_DIM), lambda h, i: (0, h)),
                  pl.BlockSpec((s, HEAD_DIM), lambda h, i: (0, nh + h)),
                  pl.BlockSpec((None, 1, s), lambda h, i: (h, 0, 0))],
        out_specs=[pl.BlockSpec((tq, HEAD_DIM), lambda h, i: (i, h)),
                   pl.BlockSpec((None, tq, LANES), lambda h, i: (h, i, 0))],
        out_shape=[jax.ShapeDtypeStruct((s, w), F32), jax.ShapeDtypeStruct((nh, s, LANES), F32)],
        scratch_shapes=[pltpu.VMEM((tq, LANES), F32), pltpu.VMEM((tq, HEAD_DIM + LANES), F32),
                        pltpu.VMEM((s, HEAD_DIM + LANES), BF16), pltpu.VMEM((tq, tq), F32)],
        compiler_params=_cparams("arbitrary", "arbitrary"),
    )(q2, kv, kv, cum2_t)


def _fox_bwd_dq(q2, kv, do, o, lse2, cum2_t, name):
    s, w = q2.shape
    nh = w // HEAD_DIM
    tq = _attn_tiles(s)
    nq = s // tq
    scale = HEAD_DIM ** -0.5
    nt = (_DOT_DIMS["nt"], ((), ()))

    def body(q_ref, k_ref, v_ref, do_ref, o_ref, lse_ref, ct_ref, dq_ref, dl_ref, dcq_ref, acc_s, dc_s):
        i = pl.program_id(1)
        qb = q_ref[...]
        dob = do_ref[...]
        lse = lse_ref[...]
        delta = jnp.broadcast_to(jnp.sum(dob.astype(F32) * o_ref[...], axis=1, keepdims=True), (tq, LANES))
        acc_s[...] = jnp.zeros_like(acc_s)
        dc_s[...] = jnp.zeros_like(dc_s)

        def tile(j, masked):
            off = pl.multiple_of(j * tq, tq)
            kb = k_ref[pl.ds(off, tq), :]
            sc = lax.dot_general(qb, kb, nt, preferred_element_type=F32) - ct_ref[:, pl.ds(off, tq)]
            if masked:
                sc = _causal(sc, False)
            p = _exp2_rows(sc, lse)
            dp = lax.dot_general(dob, v_ref[pl.ds(off, tq), :], nt, preferred_element_type=F32)
            ds = p * (dp - jnp.concatenate([delta] * (tq // LANES), axis=1))
            acc_s[...] += jnp.dot(ds.astype(BF16), kb, preferred_element_type=F32)
            part = ds[:, :LANES]
            for b in range(1, tq // LANES):
                part = part + ds[:, b * LANES:(b + 1) * LANES]
            dc_s[...] += part

        def loop(j, carry):
            tile(j, False)
            return carry

        lax.fori_loop(0, i, loop, 0)
        tile(i, True)
        dq_ref[...] = (acc_s[...] * scale).astype(BF16)
        dl_ref[...] = delta
        dcq_ref[...] = jnp.broadcast_to(jnp.sum(dc_s[...], axis=1, keepdims=True), dcq_ref.shape)

    qspec = pl.BlockSpec((tq, HEAD_DIM), lambda h, i: (i, h))
    rep = pl.BlockSpec((None, tq, LANES), lambda h, i: (h, i, 0))
    return pl.pallas_call(
        body, name=name, grid=(nh, nq),
        in_specs=[qspec,
                  pl.BlockSpec((s, HEAD_DIM), lambda h, i: (0, h)),
                  pl.BlockSpec((s, HEAD_DIM), lambda h, i: (0, nh + h)),
                  qspec, qspec, rep,
                  pl.BlockSpec((None, 1, s), lambda h, i: (h, 0, 0))],
        out_specs=[qspec, rep, rep],
        out_shape=[jax.ShapeDtypeStruct((s, w), BF16), jax.ShapeDtypeStruct((nh, s, LANES), F32),
                   jax.ShapeDtypeStruct((nh, s, LANES), F32)],
        scratch_shapes=[pltpu.VMEM((tq, HEAD_DIM), F32), pltpu.VMEM((tq, LANES), F32)],
        compiler_params=_cparams("parallel", "arbitrary"),
    )(q2, kv, kv, do, o, lse2, cum2_t)


def _fox_bwd_dkv(q2, kv, do, lse2_t, delta_t, cum2, name):
    s, w = q2.shape
    nh = w // HEAD_DIM
    tk = _attn_tiles(s)
    nk = s // tk
    nt = (_DOT_DIMS["nt"], ((), ()))

    def body(q_ref, k_ref, v_ref, do_ref, lse_ref, dl_ref, c_ref, dk_ref, dv_ref, dck_ref, dk_s, dv_s, dc_s, s_buf, dp_buf):
        h, j = pl.program_id(0), pl.program_id(1)
        kb = k_ref[...]
        vb = v_ref[...]
        ck = jnp.broadcast_to(_head_col(c_ref[...], h), (tk, LANES))
        dk_s[...] = jnp.zeros_like(dk_s)
        dv_s[...] = jnp.zeros_like(dv_s)
        dc_s[...] = jnp.zeros_like(dc_s)

        def scores(i):
            off = pl.multiple_of(i * tk, tk)
            sc = lax.dot_general(kb, q_ref[pl.ds(off, tk), :], nt, preferred_element_type=F32) - lse_ref[:, pl.ds(off, tk)]
            dp = lax.dot_general(vb, do_ref[pl.ds(off, tk), :], nt, preferred_element_type=F32) - dl_ref[:, pl.ds(off, tk)]
            return sc, dp

        def accumulate(i, sc, dp):
            off = pl.multiple_of(i * tk, tk)
            p = _exp2_rows(sc, ck)
            dv_s[...] += jnp.dot(p.astype(BF16), do_ref[pl.ds(off, tk), :], preferred_element_type=F32)
            ds = p * dp
            dk_s[...] += jnp.dot(ds.astype(BF16), q_ref[pl.ds(off, tk), :], preferred_element_type=F32)
            part = ds[:, :LANES]
            for b in range(1, tk // LANES):
                part = part + ds[:, b * LANES:(b + 1) * LANES]
            dc_s[...] += part

        sc0, dp0 = scores(j)
        s_buf[...] = _causal(sc0, True)
        dp_buf[...] = dp0

        def loop(i, carry):
            nxt = scores(i + 1)
            accumulate(i, s_buf[...], dp_buf[...])
            s_buf[...], dp_buf[...] = nxt
            return carry

        lax.fori_loop(j, nk - 1, loop, 0)
        accumulate(nk - 1, s_buf[...], dp_buf[...])
        dk_ref[...] = (dk_s[...] * (1.0 / LOG2E)).astype(BF16)
        dv_ref[...] = dv_s[...].astype(BF16)
        dck_ref[...] = jnp.broadcast_to(jnp.sum(dc_s[...], axis=1, keepdims=True), dck_ref.shape)

    col = pl.BlockSpec((s, HEAD_DIM), lambda h, j: (0, h))
    row = pl.BlockSpec((None, 1, s), lambda h, j: (h, 0, 0))
    kspec = pl.BlockSpec((tk, HEAD_DIM), lambda h, j: (j, h))
    return pl.pallas_call(
        body, name=name, grid=(nh, nk),
        in_specs=[col, kspec, pl.BlockSpec((tk, HEAD_DIM), lambda h, j: (j, nh + h)), col, row, row,
                  pl.BlockSpec((tk, LANES), lambda h, j: (j, 0))],
        out_specs=[kspec, kspec, pl.BlockSpec((None, tk, LANES), lambda h, j: (h, j, 0))],
        out_shape=[jax.ShapeDtypeStruct((s, w), BF16), jax.ShapeDtypeStruct((s, w), BF16),
                   jax.ShapeDtypeStruct((nh, s, LANES), F32)],
        scratch_shapes=[pltpu.VMEM((tk, HEAD_DIM), F32), pltpu.VMEM((tk, HEAD_DIM), F32),
                        pltpu.VMEM((tk, LANES), F32), pltpu.VMEM((tk, tk), F32), pltpu.VMEM((tk, tk), F32)],
        compiler_params=_cparams("parallel", "arbitrary"),
    )(q2, kv, kv, do, lse2_t, delta_t, cum2)


def _exchange(arrs, scatter, name):
    n = len(arrs)

    def body(*refs):
        ins, outs = refs[:n], refs[n:2 * n]
        send_sems, recv_sems, local_sems = refs[2 * n:]
        x, y, c = (lax.axis_index(a) for a in MESH_AXES)
        me = 4 * x + 2 * y + c

        def peer(k):
            kx, ky, kc = (k >> 2) & 1, (k >> 1) & 1, k & 1
            px, py, pc = (1 - x if kx else x), (1 - y if ky else y), (1 - c if kc else c)
            return (px, py, pc), 4 * px + 2 * py + pc

        local, remote = [], []
        for a in range(n):
            src = ins[a].at[me] if scatter else ins[a]
            cp = pltpu.make_async_copy(src, outs[a].at[me], local_sems.at[a])
            cp.start()
            local.append(cp)
            for k in range(1, N_DEV):
                pid, pflat = peer(k)
                cp = pltpu.make_async_remote_copy(
                    src_ref=ins[a].at[pflat] if scatter else ins[a], dst_ref=outs[a].at[me],
                    send_sem=send_sems.at[a, k - 1], recv_sem=recv_sems.at[a, k - 1],
                    device_id=pid, device_id_type=pl.DeviceIdType.MESH)
                cp.start()
                remote.append(cp)
        for cp in remote:
            cp.wait_send()
            cp.wait_recv()
        for cp in local:
            cp.wait()

    out_shape = [jax.ShapeDtypeStruct(((N_DEV,) + a.shape[1:]) if scatter else ((N_DEV,) + a.shape), a.dtype) for a in arrs]
    return pl.pallas_call(
        body, name=name, out_shape=out_shape,
        in_specs=[pl.BlockSpec(memory_space=pl.ANY)] * n, out_specs=[pl.BlockSpec(memory_space=pl.ANY)] * n,
        scratch_shapes=[pltpu.SemaphoreType.DMA((n, N_DEV - 1)), pltpu.SemaphoreType.DMA((n, N_DEV - 1)),
                        pltpu.SemaphoreType.DMA((n,))],
    )(*arrs)


def _adamw_math(w, g, m, v):
    m = ADAM_B1 * m + (1.0 - ADAM_B1) * g
    v = ADAM_B2 * v + (1.0 - ADAM_B2) * (g * g)
    m_hat = m / (1.0 - ADAM_B1 ** ADAM_STEP)
    v_hat = v / (1.0 - ADAM_B2 ** ADAM_STEP)
    return -ADAM_LR * (m_hat / (jnp.sqrt(v_hat) + ADAM_EPS) + ADAM_WD * w), m, v


def _adamw(parts, w, m, v, name):
    r, c = w.shape
    tr = _tile(r, max(SUBLANES, (256 * 1024) // c // SUBLANES * SUBLANES), SUBLANES)

    def body(p_ref, w_ref, m_ref, v_ref, g_ref, d_ref, nm_ref, nv_ref):
        g = p_ref[0].astype(F32)
        for d in range(1, N_DEV):
            g = g + p_ref[d].astype(F32)
        g_ref[...] = g
        d_ref[...], nm_ref[...], nv_ref[...] = _adamw_math(w_ref[...], g, m_ref[...], v_ref[...])

    blk = pl.BlockSpec((tr, c), lambda i: (i, 0))
    sh = jax.ShapeDtypeStruct((r, c), F32)
    return pl.pallas_call(
        body, name=name, grid=(r // tr,),
        in_specs=[pl.BlockSpec((N_DEV, tr, c), lambda i: (0, i, 0)), blk, blk, blk],
        out_specs=[blk] * 4, out_shape=[sh] * 4, compiler_params=_cparams("parallel"),
    )(parts, w, m, v)


def _perm(a):
    s, d = a.shape
    return a.reshape(N_SEG, s // N_SEG, d).transpose(1, 0, 2).reshape(s, d)


def _unperm(a):
    s, d = a.shape
    return a.reshape(s // N_SEG, N_SEG, d).transpose(1, 0, 2).reshape(s, d)


def _lane_pad(a, width=LANES):
    return jnp.pad(a, ((0, 0), (0, width - a.shape[1])))


def _local_step(x, target, norm_pre, norm_post, kv_norm, kv_b_f, a_re, a_im, log_dt, b_re, b_im, c_re, c_im,
                d_skip, b_glu, w_in, w_glu, w_out, w_kv, fw_in, fw_out):
    s, d = x.shape
    w = w_glu.shape[0]
    fw = fw_out.shape[0]
    nh = fw // HEAD_DIM
    g, p = a_re.shape
    seg_len = s // N_SEG
    row = lambda v: v.reshape(1, -1)
    g_pre0, g_pre1, g_post0, g_post1, g_kv = row(norm_pre[0]), row(norm_pre[1]), row(norm_post[0]), row(norm_post[1]), row(kv_norm)
    d_row, bglu_row = row(d_skip), row(b_glu)

    ldt = log_dt.reshape(g, 1)
    abr, abi, cr, ci = _s5_disc_fwd(a_re, a_im, ldt)
    cr_col, ci_col = cr.reshape(g * p, 1), ci.reshape(g * p, 1)
    b_re2, b_im2 = b_re.reshape(g * p, S5_GROUP), b_im.reshape(g * p, S5_GROUP)
    bb_re, bb_im = _s5_bbar_fwd(cr_col, ci_col, b_re2, b_im2)
    bd_re = _block_diag_in(bb_re.reshape(g, p, S5_GROUP)).astype(BF16)
    bd_im = _block_diag_in(bb_im.reshape(g, p, S5_GROUP)).astype(BF16)
    cd_re = _block_diag_out(c_re).astype(BF16)
    cd_im = _block_diag_out(-c_im).astype(BF16)
    ab_re = jnp.broadcast_to(abr.reshape(1, g * p), (N_SEG, g * p))
    ab_im = jnp.broadcast_to(abi.reshape(1, g * p), (N_SEG, g * p))
    zero_seg = jnp.zeros((N_SEG, g * p), F32)

    xn0 = _norm_cast(x, g_pre0, "norm_pre0", x_kind="nat")
    u = _mm(xn0, w_in, "nn", F32, "s5_in_u", b_cols=(0, w))
    z0 = _mm(xn0, w_in, "nn", F32, "s5_in_z", b_cols=(w, w))
    e_re, e_im = _s5_scan_fwd(u, bd_re, bd_im, cd_re, cd_im, ab_re, ab_im, zero_seg, zero_seg, d_row, False, "s5_scan_ends")
    i_re, i_im = _s5_seg_fix(e_re, e_im, ab_re, ab_im, seg_len, False, "s5_seg_fix")
    y_ssm, h_re, h_im, _, _ = _s5_scan_fwd(u, bd_re, bd_im, cd_re, cd_im, ab_re, ab_im, i_re, i_im, d_row, True, "s5_scan")
    yg = _gelu_cast(y_ssm, "s5_gelu")
    gp = _mm(yg, w_glu, "nn", F32, "s5_glu")
    y3 = _s5_gate(y_ssm, gp, bglu_row, z0, "s5_gate")
    o0 = _mm(y3, w_out, "nn", F32, "s5_out")
    r0 = _post_norm(o0, g_post0, "norm_post0", out_kind="nat")

    h1, hn_kv, xn1 = _resid_norm2(x, r0, g_kv, g_pre1, "resid_norms")
    w_f = _lane_pad(w_kv[:, 2 * fw:])
    kv = _mm(hn_kv, w_kv, "nn", BF16, "kv_proj", b_cols=(0, 2 * fw))
    f_logit = _mm(hn_kv, w_f, "nn", F32, "f_proj")
    bf_row = _lane_pad(row(kv_b_f))
    cum2 = _cum_fwd(f_logit, bf_row, "cum_fwd")
    cum2_t = cum2[:, :nh].T.reshape(nh, 1, s)
    q2 = _mm(xn1, fw_in, "nn", BF16, "fox_q", scale=HEAD_DIM ** -0.5 * LOG2E, b_cols=(0, fw))
    z1 = _mm(xn1, fw_in, "nn", F32, "fox_z", b_cols=(fw, fw))
    o, lse2 = _fox_fwd(q2, kv, cum2_t, "fox_fwd")
    oz = _gate_mul(o, z1, "fox_gate")
    o1 = _mm(oz, fw_out, "nn", F32, "fox_out")
    dh2, sq = _post_norm_loss(o1, g_post1, h1, target, "norm_post1_loss")
    loss = 0.5 * jnp.sum(sq) / d

    do1, dg_post1 = _post_norm_bwd(dh2, o1, g_post1, "norm_post1_bwd")
    d_fw_out = _mm(oz, do1, "tn", F32, "fox_out_dw")
    d_oz = _mm(do1, fw_out, "nt", F32, "fox_out_dx")
    do, dz1 = _gate_bwd(d_oz, o, z1, "fox_gate_bwd")
    dq, delta, dcq = _fox_bwd_dq(q2, kv, do, o, lse2, cum2_t, "fox_bwd_dq")
    lse2_t = lse2[:, :, 0].reshape(nh, 1, s)
    delta_t = delta[:, :, 0].reshape(nh, 1, s)
    dk, dv, dck = _fox_bwd_dkv(q2, kv, do, lse2_t, delta_t, cum2, "fox_bwd_dkv")
    dqz = _concat_cast(dq, dz1, "fox_dqz")
    d_fw_in = _mm(xn1, dqz, "tn", F32, "fox_in_dw")
    dxn1 = _mm(dqz, fw_in, "nt", F32, "fox_in_dx")
    dcq_sl = _lane_pad(dcq[:, :, 0].T)
    dck_sl = _lane_pad(dck[:, :, 0].T)
    df, db_f = _cum_bwd(dcq_sl, dck_sl, f_logit, bf_row, "cum_bwd")
    dkv = _concat_cast(dk, dv, "fox_dkv")
    d_w_kvm = _mm(hn_kv, dkv, "tn", F32, "kv_dw")
    d_w_f = _mm(hn_kv, df, "tn", F32, "f_dw")
    dhn_f = _mm(df, w_f, "nt", F32, "f_dx")
    dhn_kv = _mm(dkv, w_kv, "nt", F32, "kv_dx", add=dhn_f, b_cols=(0, 2 * fw))
    d_w_kv = jnp.concatenate([d_w_kvm, d_w_f[:, :nh]], axis=1)
    dh1, dg_pre1, dg_kv = _norm_bwd2(dh2, h1, dxn1, dhn_kv, g_pre1, g_kv, "resid_norms_bwd")

    do0, dg_post0 = _post_norm_bwd(dh1, o0, g_post0, "norm_post0_bwd", dy_kind="nat")
    d_w_out = _mm(y3, do0, "tn", F32, "s5_out_dw")
    dy3 = _mm(do0, w_out, "nt", F32, "s5_out_dx")
    dz0, dgp, dyg_direct, db_glu = _s5_gate_bwd(dy3, y_ssm, gp, bglu_row, z0, "s5_gate_bwd")
    d_w_glu = _mm(yg, dgp, "tn", F32, "s5_glu_dw")
    dyg = _mm(dgp, w_glu, "nt", F32, "s5_glu_dx", add=dyg_direct)
    dy_ssm = _gelu_bwd(dyg, y_ssm, "s5_gelu_bwd")
    ab_imn = -ab_im
    ge_re, ge_im = _s5_scan_bwd(dy_ssm, u, h_re, h_im, bd_re, bd_im, cd_re, cd_im, ab_re, ab_imn, zero_seg, zero_seg,
                                d_row, False, "s5_adj_ends")
    gi_re, gi_im = _s5_seg_fix(ge_re, ge_im, ab_re, ab_imn, seg_len, True, "s5_adj_fix")
    du, dbd_re, dbd_im, dcd_re, dcd_im, dab_re, dab_im, dd = _s5_scan_bwd(
        dy_ssm, u, h_re, h_im, bd_re, bd_im, cd_re, cd_im, ab_re, ab_imn, gi_re, gi_im, d_row, True, "s5_adj")
    duz = _concat_cast(du, dz0, "s5_duz")
    d_w_in = _mm(xn0, duz, "tn", F32, "s5_in_dw")
    dxn0 = _mm(duz, w_in, "nt", F32, "s5_in_dx")
    grad_x, dg_pre0 = _norm_bwd1(dh1, x, dxn0, g_pre0, "norm_pre0_bwd")

    dbb_re = _block_diag_in_extract(dbd_re, p, S5_GROUP).reshape(g * p, S5_GROUP)
    dbb_im = _block_diag_in_extract(dbd_im, p, S5_GROUP).reshape(g * p, S5_GROUP)
    dcr_col, dci_col, db_re, db_im = _s5_bbar_bwd(cr_col, ci_col, b_re2, b_im2, dbb_re, dbb_im)
    da_re, da_im, dldt = _s5_disc_bwd(a_re, a_im, ldt, dab_re.reshape(g, p), dab_im.reshape(g, p),
                                      dcr_col.reshape(g, p), dci_col.reshape(g, p))
    dc_re = _block_diag_out_extract(dcd_re, S5_GROUP, p)
    dc_im = -_block_diag_out_extract(dcd_im, S5_GROUP, p)

    small = dict(
        norm_pre=jnp.concatenate([dg_pre0, dg_pre1], axis=0), norm_post=jnp.concatenate([dg_post0, dg_post1], axis=0),
        s5_a_re=da_re, s5_a_im=da_im, s5_log_dt=dldt.reshape(g), s5_b_re=db_re.reshape(g, p, S5_GROUP),
        s5_b_im=db_im.reshape(g, p, S5_GROUP), s5_c_re=dc_re, s5_c_im=dc_im, s5_d=dd.reshape(-1),
        s5_b_glu=db_glu.reshape(-1), kv_norm=dg_kv.reshape(-1), kv_b_f=db_f[0, :nh])
    big = dict(s5_w_in=d_w_in, s5_w_glu=d_w_glu, s5_w_out=d_w_out, kv_w=d_w_kv, fox_w_in=d_fw_in, fox_w_out=d_fw_out)
    return loss, grad_x, big, small


_BIG = ("s5_w_in", "s5_w_glu", "s5_w_out", "kv_w", "fox_w_in", "fox_w_out")
_COL_SHARDED = ("s5_w_in", "kv_w", "fox_w_in")
_SMALL = ("norm_pre", "norm_post", "s5_a_re", "s5_a_im", "s5_log_dt", "s5_b_re", "s5_b_im", "s5_c_re", "s5_c_im",
          "s5_d", "s5_b_glu", "kv_norm", "kv_b_f")
_SMALL_SHARDED = ("s5_d", "s5_b_glu")
_PACK_QUANTUM = SUBLANES * LANES
_WEIGHTS = ('norm_pre', 'norm_post', 's5_w_in', 's5_a_re', 's5_a_im', 's5_log_dt', 's5_b_re', 's5_b_im', 's5_c_re', 's5_c_im',
            's5_d', 's5_w_glu', 's5_b_glu', 's5_w_out', 'kv_norm', 'kv_w', 'kv_b_f', 'fox_w_in', 'fox_w_out')


def _full_from_slots(name, slots):
    n, r, c = slots.shape
    if name in _COL_SHARDED:
        return slots.transpose(1, 0, 2).reshape(r, n * c)
    return slots.reshape(n * r, c)


def _slots_from_full(name, full):
    if name in _COL_SHARDED:
        r, nc = full.shape
        return full.reshape(r, N_DEV, nc // N_DEV).transpose(1, 0, 2)
    nr, c = full.shape
    return full.reshape(N_DEV, nr // N_DEV, c)


def _pack(vals):
    parts = []
    for v in vals:
        flat = v.reshape(-1)
        parts.append(jnp.pad(flat, (0, (-flat.shape[0]) % _PACK_QUANTUM)))
    return jnp.concatenate(parts).reshape(-1, LANES)


def _unpack(packed, shapes):
    flat = packed.reshape(-1)
    out, off = [], 0
    for sh in shapes:
        n = math.prod(sh)
        out.append(flat[off:off + n].reshape(sh))
        off += n + (-n) % _PACK_QUANTUM
    return out


def kernel(x, norm_pre, norm_post, s5_w_in, s5_a_re, s5_a_im, s5_log_dt, s5_b_re, s5_b_im, s5_c_re, s5_c_im, s5_d, s5_w_glu, s5_b_glu, s5_w_out, kv_norm, kv_w, kv_b_f, fox_w_in, fox_w_out, loss_target, m_norm_pre, m_norm_post, m_s5_w_in, m_s5_a_re, m_s5_a_im, m_s5_log_dt, m_s5_b_re, m_s5_b_im, m_s5_c_re, m_s5_c_im, m_s5_d, m_s5_w_glu, m_s5_b_glu, m_s5_w_out, m_kv_norm, m_kv_w, m_kv_b_f, m_fox_w_in, m_fox_w_out, v_norm_pre, v_norm_post, v_s5_w_in, v_s5_a_re, v_s5_a_im, v_s5_log_dt, v_s5_b_re, v_s5_b_im, v_s5_c_re, v_s5_c_im, v_s5_d, v_s5_w_glu, v_s5_b_glu, v_s5_w_out, v_kv_norm, v_kv_w, v_kv_b_f, v_fox_w_in, v_fox_w_out):
    env = dict(locals())
    wts = {n: env[n] for n in _WEIGHTS}
    mom = {n: env["m_" + n] for n in _WEIGHTS}
    var = {n: env["v_" + n] for n in _WEIGHTS}
    me = 4 * lax.axis_index("x") + 2 * lax.axis_index("y") + lax.axis_index("c")

    shard2d = {n: wts[n].reshape(wts[n].shape[-2:]) for n in _BIG}
    gathered = _exchange([shard2d[n].astype(BF16) for n in _BIG] + [wts[n].reshape(1, -1) for n in _SMALL_SHARDED],
                         False, "gather_weights")
    full = {n: _full_from_slots(n, gathered[i]) for i, n in enumerate(_BIG)}
    d_skip = gathered[len(_BIG)].reshape(-1)
    b_glu = gathered[len(_BIG) + 1].reshape(-1)

    loss_local, grad_x, big, small = _local_step(
        x[0], loss_target[0], norm_pre, norm_post, kv_norm, kv_b_f, s5_a_re[0], s5_a_im[0], s5_log_dt[0],
        s5_b_re[0], s5_b_im[0], s5_c_re[0], s5_c_im[0], d_skip, b_glu,
        full["s5_w_in"], full["s5_w_glu"], full["s5_w_out"], full["kv_w"], full["fox_w_in"], full["fox_w_out"])
    loss = lax.psum(loss_local, MESH_AXES)

    recv = _exchange([_slots_from_full(n, big[n]).astype(BF16) for n in _BIG], True, "exchange_grads")
    res = {}
    for i, n in enumerate(_BIG):
        outs = _adamw(recv[i], shard2d[n], mom[n].reshape(shard2d[n].shape), var[n].reshape(shard2d[n].shape), "adamw_" + n)
        res[n] = [o.reshape(wts[n].shape) for o in outs]

    full_shape = {n: (small[n].shape if n in _SMALL_SHARDED else wts[n].shape) for n in _SMALL}

    def spread(n, v):
        if n not in _SMALL_SHARDED:
            return v
        flat = v.reshape(-1)
        return lax.dynamic_update_slice(jnp.zeros(full_shape[n], F32), flat, (me * flat.shape[0],))

    g_all = _exchange([_pack([small[n] for n in _SMALL])], False, "gather_small_grads")[0]
    packed = [_pack([spread(n, src[n]) for n in _SMALL]) for src in (wts, mom, var)]
    outs = _adamw(g_all, *packed, "adamw_small")
    unpacked = [_unpack(o, [full_shape[n] for n in _SMALL]) for o in outs]
    for i, n in enumerate(_SMALL):
        vals = [u[i] for u in unpacked]
        if n in _SMALL_SHARDED:
            k = wts[n].size
            vals = [lax.dynamic_slice(v, (me * k,), (k,)) for v in vals]
        res[n] = [v.reshape(wts[n].shape) for v in vals]

    return (loss, grad_x[None], *[res[n][0] for n in _WEIGHTS], *[res[n][1] for n in _WEIGHTS],
            *[res[n][2] for n in _WEIGHTS], *[res[n][3] for n in _WEIGHTS])
```

```python
import functools
import math

import jax
import jax.numpy as jnp
from jax import lax
from jax.experimental import pallas as pl
from jax.experimental.pallas import tpu as pltpu

F32 = jnp.float32
BF16 = jnp.bfloat16

N_DEV = 8
MESH_AXES = ("x", "y", "c")
S5_GROUP = 16
S5_STATE = 64
LANES = 128
SUBLANES = 8
GROUPS_PER_BLOCK = LANES // S5_GROUP
BLOCK_STATE = GROUPS_PER_BLOCK * S5_STATE
N_SEG = SUBLANES
HEAD_DIM = 128
RMS_EPS = 1e-6
NEG_INF = -1e30
LOG2E = math.log2(math.e)
ADAM_LR = 0.001
ADAM_B1 = 0.9
ADAM_B2 = 0.999
ADAM_EPS = 1e-08
ADAM_WD = 0.01
ADAM_STEP = 10
VMEM_LIMIT = 56 * 1024 * 1024


def _tile(n, pref, quantum=LANES):
    if n <= pref:
        return n
    t = (pref // quantum) * quantum
    while t >= quantum:
        if n % t == 0:
            return t
        t -= quantum
    return n


def _cparams(*sem):
    return pltpu.CompilerParams(dimension_semantics=sem if sem else None, vmem_limit_bytes=VMEM_LIMIT)


_DOT_DIMS = {"nn": ((1,), (0,)), "nt": ((1,), (1,)), "tn": ((0,), (0,))}


def _mm(a, b, mode, out_dtype, name, add=None, scale=None, b_cols=None):
    b_shape = b.shape if b_cols is None else (b.shape[0], b_cols[1])
    if mode == "nn":
        (M, K), (K2, N) = a.shape, b_shape
    elif mode == "nt":
        (M, K), (N, K2) = a.shape, b_shape
    else:
        (K, M), (K2, N) = a.shape, b_shape
    assert K == K2, (name, a.shape, b_shape)
    tm, tn, tk = _tile(M, 1024), _tile(N, 1024), _tile(K, 1024)
    nk = K // tk
    dims = (_DOT_DIMS[mode], ((), ()))
    col0 = 0
    if b_cols is not None:
        assert mode != "tn" and b_cols[0] % (tn if mode == "nn" else tk) == 0
        col0 = b_cols[0] // (tn if mode == "nn" else tk)

    def body(*refs):
        if add is None:
            a_ref, b_ref, o_ref, acc = refs
        else:
            a_ref, b_ref, c_ref, o_ref, acc = refs
        k = pl.program_id(2)

        @pl.when(k == 0)
        def _():
            acc[...] = jnp.zeros_like(acc)

        acc[...] += lax.dot_general(a_ref[...], b_ref[...], dims, preferred_element_type=F32)

        @pl.when(k == nk - 1)
        def _():
            r = acc[...]
            if scale is not None:
                r = r * scale
            if add is not None:
                r = r + c_ref[...]
            o_ref[...] = r.astype(out_dtype)

    if mode == "tn":
        a_spec = pl.BlockSpec((tk, tm), lambda i, j, k: (k, i))
    else:
        a_spec = pl.BlockSpec((tm, tk), lambda i, j, k: (i, k))
    if mode == "nt":
        b_spec = pl.BlockSpec((tn, tk), lambda i, j, k: (j, k + col0))
    else:
        b_spec = pl.BlockSpec((tk, tn), lambda i, j, k: (k, j + col0))
    o_spec = pl.BlockSpec((tm, tn), lambda i, j, k: (i, j))
    in_specs = [a_spec, b_spec] + ([o_spec] if add is not None else [])
    args = (a, b) + ((add,) if add is not None else ())
    return pl.pallas_call(
        body, name=name, grid=(M // tm, N // tn, nk),
        in_specs=in_specs, out_specs=o_spec,
        out_shape=jax.ShapeDtypeStruct((M, N), out_dtype),
        scratch_shapes=[pltpu.VMEM((tm, tn), F32)],
        compiler_params=_cparams("parallel", "parallel", "arbitrary"),
    )(*args)


class _NatIn:
    def __init__(self, ref):
        self.ref = ref

    def __getitem__(self, idx):
        v = jnp.swapaxes(self.ref[...], 0, 1)
        return v.reshape(v.shape[0] * N_SEG, v.shape[2])


class _NatOut:
    def __init__(self, ref):
        self.ref = ref

    def __setitem__(self, idx, val):
        self.ref[...] = jnp.swapaxes(val.reshape(val.shape[0] // N_SEG, N_SEG, val.shape[1]), 0, 1)


def _rowcall(body, name, n_rows, ins, outs, tile_rows=256):
    tr = _tile(n_rows, tile_rows, SUBLANES * 2)
    n_in = len(ins)
    in_kinds = [k for _, k in ins]
    kinds = [k for _, _, k in outs]

    def kern(*refs):
        @pl.when(pl.program_id(0) == 0)
        def _():
            for r, kind in zip(refs[n_in:], kinds):
                if kind == "acc":
                    r[...] = jnp.zeros_like(r)

        wrapped = [_NatIn(r) if k == "nat" else r for r, k in zip(refs[:n_in], in_kinds)]
        wrapped += [_NatOut(r) if k == "nat" else r for r, k in zip(refs[n_in:], kinds)]
        body(*wrapped)

    in_specs, args = [], []
    for arr, kind in ins:
        if kind == "row":
            in_specs.append(pl.BlockSpec((tr, arr.shape[1]), lambda i: (i, 0)))
        elif kind == "nat":
            in_specs.append(pl.BlockSpec((N_SEG, tr // N_SEG, arr.shape[1]), lambda i: (0, i, 0)))
            arr = arr.reshape(N_SEG, n_rows // N_SEG, arr.shape[1])
        else:
            in_specs.append(pl.BlockSpec(arr.shape, lambda i, nd=arr.ndim: (0,) * nd))
        args.append(arr)
    out_specs, out_shape = [], []
    for width, dtype, kind in outs:
        if kind == "row":
            out_specs.append(pl.BlockSpec((tr, width), lambda i: (i, 0)))
            out_shape.append(jax.ShapeDtypeStruct((n_rows, width), dtype))
        elif kind == "nat":
            out_specs.append(pl.BlockSpec((N_SEG, tr // N_SEG, width), lambda i: (0, i, 0)))
            out_shape.append(jax.ShapeDtypeStruct((N_SEG, n_rows // N_SEG, width), dtype))
        else:
            out_specs.append(pl.BlockSpec((1, width), lambda i: (0, 0)))
            out_shape.append(jax.ShapeDtypeStruct((1, width), F32))
    res = pl.pallas_call(
        kern, name=name, grid=(n_rows // tr,), in_specs=in_specs, out_specs=out_specs, out_shape=out_shape,
        compiler_params=_cparams("arbitrary"),
    )(*args)
    return [r.reshape(n_rows, r.shape[2]) if k == "nat" else r for r, k in zip(res, kinds)]


def _rstd(x):
    return lax.rsqrt(jnp.mean(x * x, axis=-1, keepdims=True) + RMS_EPS)


def _rms_bwd(x, g, dy):
    xh = x * _rstd(x)
    dxh = dy * g
    dx = _rstd(x) * (dxh - xh * jnp.mean(dxh * xh, axis=-1, keepdims=True))
    return dx, jnp.sum(dy * xh, axis=0, keepdims=True)


def _silu(z):
    return z * jax.nn.sigmoid(z)


def _norm_cast(x, g, name, x_kind="row"):
    def body(x_ref, g_ref, o_ref):
        x = x_ref[...]
        o_ref[...] = (x * _rstd(x) * g_ref[...]).astype(BF16)

    return _rowcall(body, name, x.shape[0], [(x, x_kind), (g, "full")], [(x.shape[1], BF16, "row")])[0]


def _resid_norm2(x, r0, g_kv, g_pre, name):
    def body(x_ref, r_ref, gk_ref, gp_ref, h_ref, nk_ref, np_ref):
        h = x_ref[...] + r_ref[...]
        h_ref[...] = h
        hn = h * _rstd(h)
        nk_ref[...] = (hn * gk_ref[...]).astype(BF16)
        np_ref[...] = (hn * gp_ref[...]).astype(BF16)

    d = x.shape[1]
    return _rowcall(body, name, x.shape[0], [(x, "row"), (r0, "row"), (g_kv, "full"), (g_pre, "full")],
                    [(d, F32, "row"), (d, BF16, "row"), (d, BF16, "row")])


def _post_norm(o, g, name, out_kind="row"):
    def body(o_ref, g_ref, r_ref):
        o = o_ref[...]
        r_ref[...] = o * _rstd(o) * g_ref[...]

    return _rowcall(body, name, o.shape[0], [(o, "row"), (g, "full")], [(o.shape[1], F32, out_kind)])[0]


def _post_norm_loss(o, g, h1, target, name):
    d = o.shape[1]

    def body(o_ref, g_ref, h_ref, t_ref, dh_ref, acc_ref):
        o = o_ref[...]
        e = h_ref[...] + o * _rstd(o) * g_ref[...] - t_ref[...]
        dh_ref[...] = e * (1.0 / d)
        acc_ref[...] += jnp.sum(e * e, axis=0, keepdims=True)

    return _rowcall(body, name, o.shape[0], [(o, "row"), (g, "full"), (h1, "row"), (target, "row")],
                    [(d, F32, "row"), (d, F32, "acc")])


def _post_norm_bwd(dy, o, g, name, dy_kind="row"):
    def body(dy_ref, o_ref, g_ref, do_ref, dg_ref):
        dx, dg = _rms_bwd(o_ref[...], g_ref[...], dy_ref[...])
        do_ref[...] = dx.astype(BF16)
        dg_ref[...] += dg

    d = o.shape[1]
    return _rowcall(body, name, o.shape[0], [(dy, dy_kind), (o, "row"), (g, "full")], [(d, BF16, "row"), (d, F32, "acc")])


def _gate_mul(o, z, name):
    def body(o_ref, z_ref, r_ref):
        r_ref[...] = (o_ref[...] * _silu(z_ref[...])).astype(BF16)

    return _rowcall(body, name, o.shape[0], [(o, "row"), (z, "row")], [(o.shape[1], BF16, "row")])[0]


def _gate_bwd(d_oz, o, z, name):
    def body(d_ref, o_ref, z_ref, do_ref, dz_ref):
        _, vjp = jax.vjp(lambda o, z: o * _silu(z), o_ref[...], z_ref[...])
        do, dz = vjp(d_ref[...])
        do_ref[...] = do.astype(BF16)
        dz_ref[...] = dz.astype(BF16)

    w = o.shape[1]
    return _rowcall(body, name, o.shape[0], [(d_oz, "row"), (o, "row"), (z, "row")], [(w, BF16, "row"), (w, BF16, "row")])


def _norm_bwd2(dh2, h1, dxn1, dhn_kv, g_pre, g_kv, name):
    def body(dh2_ref, h_ref, d1_ref, dk_ref, gp_ref, gk_ref, dh1_ref, dgp_ref, dgk_ref):
        h = h_ref[...]
        dx1, dg1 = _rms_bwd(h, gp_ref[...], d1_ref[...])
        dxk, dgk = _rms_bwd(h, gk_ref[...], dk_ref[...])
        dh1_ref[...] = dh2_ref[...] + dx1 + dxk
        dgp_ref[...] += dg1
        dgk_ref[...] += dgk

    d = h1.shape[1]
    return _rowcall(body, name, h1.shape[0],
                    [(dh2, "row"), (h1, "row"), (dxn1, "row"), (dhn_kv, "row"), (g_pre, "full"), (g_kv, "full")],
                    [(d, F32, "row"), (d, F32, "acc"), (d, F32, "acc")])


def _norm_bwd1(dres, x, dxn, g, name):
    def body(dr_ref, x_ref, dn_ref, g_ref, dx_ref, dg_ref):
        dx, dg = _rms_bwd(x_ref[...], g_ref[...], dn_ref[...])
        dx_ref[...] = dr_ref[...] + dx
        dg_ref[...] += dg

    d = x.shape[1]
    return _rowcall(body, name, x.shape[0], [(dres, "nat"), (x, "nat"), (dxn, "row"), (g, "full")],
                    [(d, F32, "nat"), (d, F32, "acc")])


def _gelu_cast(y, name):
    def body(y_ref, o_ref):
        o_ref[...] = jax.nn.gelu(y_ref[...]).astype(BF16)

    return _rowcall(body, name, y.shape[0], [(y, "row")], [(y.shape[1], BF16, "row")])[0]


def _s5_gate(y_ssm, gp, b_glu, z, name):
    def body(y_ref, gp_ref, b_ref, z_ref, o_ref):
        yg = jax.nn.gelu(y_ref[...])
        o_ref[...] = (yg * jax.nn.sigmoid(gp_ref[...] + b_ref[...]) * _silu(z_ref[...])).astype(BF16)

    return _rowcall(body, name, y_ssm.shape[0], [(y_ssm, "row"), (gp, "row"), (b_glu, "full"), (z, "row")],
                    [(y_ssm.shape[1], BF16, "row")])[0]


def _s5_gate_bwd(dy3, y_ssm, gp, b_glu, z, name):
    def body(d_ref, y_ref, gp_ref, b_ref, z_ref, dz_ref, dgp_ref, dyg_ref, db_ref):
        yg = jax.nn.gelu(y_ref[...])
        _, vjp = jax.vjp(lambda yg, gp, z: yg * jax.nn.sigmoid(gp) * _silu(z), yg, gp_ref[...] + b_ref[...], z_ref[...])
        dyg, dgp, dz = vjp(d_ref[...])
        dz_ref[...] = dz.astype(BF16)
        dgp_ref[...] = dgp.astype(BF16)
        dyg_ref[...] = dyg
        db_ref[...] += jnp.sum(dgp, axis=0, keepdims=True)

    w = y_ssm.shape[1]
    return _rowcall(body, name, y_ssm.shape[0],
                    [(dy3, "row"), (y_ssm, "row"), (gp, "row"), (b_glu, "full"), (z, "row")],
                    [(w, BF16, "row"), (w, BF16, "row"), (w, F32, "row"), (w, F32, "acc")])


def _gelu_bwd(dyg, y_ssm, name):
    def body(d_ref, y_ref, o_ref):
        _, vjp = jax.vjp(jax.nn.gelu, y_ref[...])
        o_ref[...] = vjp(d_ref[...])[0]

    return _rowcall(body, name, y_ssm.shape[0], [(dyg, "row"), (y_ssm, "row")], [(y_ssm.shape[1], F32, "row")])[0]


def _concat_cast(a, b, name):
    def body(a_ref, b_ref, o_ref):
        w = a_ref.shape[1]
        o_ref[:, :w] = a_ref[...].astype(BF16)
        o_ref[:, w:] = b_ref[...].astype(BF16)

    return _rowcall(body, name, a.shape[0], [(a, "row"), (b, "row")], [(a.shape[1] + b.shape[1], BF16, "row")])[0]


def _disc(ar, ai, ldt):
    dt = jnp.exp(ldt)
    mag = jnp.exp(ar * dt)
    abr = mag * jnp.cos(ai * dt)
    abi = mag * jnp.sin(ai * dt)
    den = ar * ar + ai * ai
    nr = abr - 1.0
    return abr, abi, (nr * ar + abi * ai) / den, (abi * ar - nr * ai) / den


def _s5_disc_fwd(a_re, a_im, ldt):
    def body(ar, ai, ld, o1, o2, o3, o4):
        o1[...], o2[...], o3[...], o4[...] = _disc(ar[...], ai[...], ld[...])

    sh = jax.ShapeDtypeStruct(a_re.shape, F32)
    return pl.pallas_call(body, name="s5_disc_fwd", out_shape=(sh, sh, sh, sh))(a_re, a_im, ldt)


def _s5_disc_bwd(a_re, a_im, ldt, d_abr, d_abi, d_cr, d_ci):
    def body(ar, ai, ld, g1, g2, g3, g4, o1, o2, o3):
        _, vjp = jax.vjp(_disc, ar[...], ai[...], ld[...])
        o1[...], o2[...], o3[...] = vjp((g1[...], g2[...], g3[...], g4[...]))

    sh = jax.ShapeDtypeStruct(a_re.shape, F32)
    return pl.pallas_call(body, name="s5_disc_bwd", out_shape=(sh, sh, jax.ShapeDtypeStruct(ldt.shape, F32)))(
        a_re, a_im, ldt, d_abr, d_abi, d_cr, d_ci)


def _bbar(cr, ci, br, bi):
    return cr * br - ci * bi, cr * bi + ci * br


def _s5_bbar_fwd(cr_col, ci_col, b_re, b_im):
    def body(cr, ci, br, bi, o1, o2):
        o1[...], o2[...] = _bbar(cr[...], ci[...], br[...], bi[...])

    w = b_re.shape[1]
    return _rowcall(body, "s5_bbar_fwd", b_re.shape[0], [(cr_col, "row"), (ci_col, "row"), (b_re, "row"), (b_im, "row")],
                    [(w, F32, "row"), (w, F32, "row")], tile_rows=1024)


def _s5_bbar_bwd(cr_col, ci_col, b_re, b_im, d_re, d_im):
    def body(cr, ci, br, bi, g1, g2, o1, o2, o3, o4):
        _, vjp = jax.vjp(_bbar, cr[...], ci[...], br[...], bi[...])
        o1[...], o2[...], o3[...], o4[...] = vjp((g1[...], g2[...]))

    w = b_re.shape[1]
    return _rowcall(body, "s5_bbar_bwd", b_re.shape[0],
                    [(cr_col, "row"), (ci_col, "row"), (b_re, "row"), (b_im, "row"), (d_re, "row"), (d_im, "row")],
                    [(1, F32, "row"), (1, F32, "row"), (w, F32, "row"), (w, F32, "row")], tile_rows=1024)


def _block_diag_in(t):
    g, p, c = t.shape
    nb = g // GROUPS_PER_BLOCK
    t4 = t.reshape(nb, GROUPS_PER_BLOCK, p, c).transpose(0, 1, 3, 2)
    eye = jnp.eye(GROUPS_PER_BLOCK, dtype=t.dtype)
    return (t4[:, :, :, None, :] * eye[None, :, None, :, None]).reshape(nb, GROUPS_PER_BLOCK * c, GROUPS_PER_BLOCK * p)


def _block_diag_in_extract(d, p, c):
    nb = d.shape[0]
    d5 = d.reshape(nb, GROUPS_PER_BLOCK, c, GROUPS_PER_BLOCK, p)
    diag = jnp.stack([d5[:, g, :, g, :] for g in range(GROUPS_PER_BLOCK)], axis=1)
    return diag.transpose(0, 1, 3, 2).reshape(nb * GROUPS_PER_BLOCK, p, c)


def _block_diag_out(t):
    g, c, p = t.shape
    nb = g // GROUPS_PER_BLOCK
    t4 = t.reshape(nb, GROUPS_PER_BLOCK, c, p).transpose(0, 1, 3, 2)
    eye = jnp.eye(GROUPS_PER_BLOCK, dtype=t.dtype)
    return (t4[:, :, :, None, :] * eye[None, :, None, :, None]).reshape(nb, GROUPS_PER_BLOCK * p, GROUPS_PER_BLOCK * c)


def _block_diag_out_extract(d, c, p):
    nb = d.shape[0]
    d5 = d.reshape(nb, GROUPS_PER_BLOCK, p, GROUPS_PER_BLOCK, c)
    diag = jnp.stack([d5[:, g, :, g, :] for g in range(GROUPS_PER_BLOCK)], axis=1)
    return diag.transpose(0, 1, 3, 2).reshape(nb * GROUPS_PER_BLOCK, c, p)


def _scan_step(ar, ai, hr, hi, xr, xi):
    return ar * hr - ai * hi + xr, ar * hi + ai * hr + xi


def _s5_scan_fwd(u, bd_re, bd_im, cd_re, cd_im, ab_re, ab_im, init_re, init_im, d_row, full, name):
    s, w = u.shape
    nb = w // LANES
    rows = _tile(s, 512, SUBLANES)
    nc = s // rows
    steps = rows // N_SEG
    ns = nb * BLOCK_STATE

    def body(u_ref, bdr, bdi, cdr, cdi, ar_ref, ai_ref, ir_ref, ii_ref, d_ref, *outs):
        if full:
            y_ref, hr_ref, hi_ref, er_ref, ei_ref, cr, ci = outs
        else:
            er_ref, ei_ref, hr_ref, hi_ref, cr, ci = outs
        c = pl.program_id(1)

        @pl.when(c == 0)
        def _():
            cr[...] = ir_ref[...]
            ci[...] = ii_ref[...]

        ub = u_ref[...].astype(BF16)
        hr_ref[...] = jnp.dot(ub, bdr[...], preferred_element_type=F32)
        hi_ref[...] = jnp.dot(ub, bdi[...], preferred_element_type=F32)
        ar, ai = ar_ref[...], ai_ref[...]

        def step(j, carry):
            off = pl.multiple_of(j * N_SEG, N_SEG)
            nr, ni = _scan_step(ar, ai, carry[0], carry[1], hr_ref[pl.ds(off, N_SEG), :], hi_ref[pl.ds(off, N_SEG), :])
            hr_ref[pl.ds(off, N_SEG), :] = nr
            hi_ref[pl.ds(off, N_SEG), :] = ni
            return nr, ni

        hr, hi = lax.fori_loop(0, steps, step, (cr[...], ci[...]), unroll=8)
        cr[...] = hr
        ci[...] = hi
        if full:
            y_ref[...] = (jnp.dot(hr_ref[...].astype(BF16), cdr[...], preferred_element_type=F32)
                          + jnp.dot(hi_ref[...].astype(BF16), cdi[...], preferred_element_type=F32)
                          + d_ref[...] * u_ref[...])

        @pl.when(c == nc - 1)
        def _():
            er_ref[...] = hr
            ei_ref[...] = hi

    blk3 = lambda a: pl.BlockSpec((None,) + a.shape[1:], lambda k, c: (k, 0, 0))
    seg = pl.BlockSpec((N_SEG, BLOCK_STATE), lambda k, c: (0, k))
    st = pl.BlockSpec((rows, BLOCK_STATE), lambda k, c: (c, k))
    in_specs = [pl.BlockSpec((rows, LANES), lambda k, c: (c, k)), blk3(bd_re), blk3(bd_im), blk3(cd_re), blk3(cd_im),
                seg, seg, seg, seg, pl.BlockSpec((1, LANES), lambda k, c: (0, k))]
    seg_shape = jax.ShapeDtypeStruct((N_SEG, ns), F32)
    st_shape = jax.ShapeDtypeStruct((s, ns), F32)
    carry = [pltpu.VMEM((N_SEG, BLOCK_STATE), F32)] * 2
    if full:
        out_specs = [pl.BlockSpec((rows, LANES), lambda k, c: (c, k)), st, st, seg, seg]
        out_shape = [jax.ShapeDtypeStruct((s, w), F32), st_shape, st_shape, seg_shape, seg_shape]
        scratch = carry
    else:
        out_specs = [seg, seg]
        out_shape = [seg_shape, seg_shape]
        scratch = [pltpu.VMEM((rows, BLOCK_STATE), F32)] * 2 + carry
    return pl.pallas_call(
        body, name=name, grid=(nb, nc), in_specs=in_specs, out_specs=out_specs, out_shape=out_shape,
        scratch_shapes=scratch, compiler_params=_cparams("parallel", "arbitrary"),
    )(u, bd_re, bd_im, cd_re, cd_im, ab_re, ab_im, init_re, init_im, d_row)


def _s5_seg_fix(e_re, e_im, ab_re, ab_im, seg_len, reverse, name):
    assert seg_len & (seg_len - 1) == 0

    def body(er, ei, ar, ai, o_re, o_im):
        pr, pi = ar[0:1, :], ai[0:1, :]
        for _ in range(int(math.log2(seg_len))):
            pr, pi = pr * pr - pi * pi, 2.0 * pr * pi
        tr = jnp.zeros_like(pr)
        ti = jnp.zeros_like(pr)
        order = list(range(N_SEG - 1, -1, -1)) if reverse else list(range(N_SEG))
        for n, sgm in enumerate(order):
            o_re[sgm:sgm + 1, :] = tr
            o_im[sgm:sgm + 1, :] = ti
            if n < N_SEG - 1:
                tr, ti = _scan_step(pr, pi, tr, ti, er[sgm:sgm + 1, :], ei[sgm:sgm + 1, :])

    sh = jax.ShapeDtypeStruct(e_re.shape, F32)
    return pl.pallas_call(body, name=name, out_shape=(sh, sh))(e_re, e_im, ab_re, ab_im)


def _s5_scan_bwd(dy, u, h_re, h_im, bd_re, bd_im, cd_re, cd_im, ab_re, ab_imn, gin_re, gin_im, d_row, full, name):
    s, w = u.shape
    nb = w // LANES
    rows = _tile(s, 512, SUBLANES)
    nc = s // rows
    steps = rows // N_SEG
    ns = nb * BLOCK_STATE

    def body(dy_ref, u_ref, hr_ref, hi_ref, bdr, bdi, cdr, cdi, ar_ref, ai_ref, ir_ref, ii_ref, d_ref, *outs):
        if full:
            du_ref, dbr_ref, dbi_ref, dcr_ref, dci_ref, dar_ref, dai_ref, dd_ref, gr, gi, accr, acci = outs
        else:
            er_ref, ei_ref, gr, gi = outs
        c = pl.program_id(1)

        @pl.when(c == 0)
        def _():
            gr[pl.ds(rows, N_SEG), :] = ir_ref[...]
            gi[pl.ds(rows, N_SEG), :] = ii_ref[...]
            if full:
                for r in (dbr_ref, dbi_ref, dcr_ref, dci_ref, dd_ref, accr, acci):
                    r[...] = jnp.zeros_like(r)

        dyb = dy_ref[...].astype(BF16)
        nt = (_DOT_DIMS["nt"], ((), ()))
        tn = (_DOT_DIMS["tn"], ((), ()))
        gr[pl.ds(0, rows), :] = lax.dot_general(dyb, cdr[...], nt, preferred_element_type=F32)
        gi[pl.ds(0, rows), :] = lax.dot_general(dyb, cdi[...], nt, preferred_element_type=F32)
        ar, ai = ar_ref[...], ai_ref[...]

        def step(jj, carry):
            off = pl.multiple_of((steps - 1 - jj) * N_SEG, N_SEG)
            nr, ni = _scan_step(ar, ai, carry[0], carry[1], gr[pl.ds(off, N_SEG), :], gi[pl.ds(off, N_SEG), :])
            gr[pl.ds(off, N_SEG), :] = nr
            gi[pl.ds(off, N_SEG), :] = ni
            return nr, ni

        g0r, g0i = lax.fori_loop(0, steps, step, (gr[pl.ds(rows, N_SEG), :], gi[pl.ds(rows, N_SEG), :]), unroll=8)
        if full:
            hr, hi = hr_ref[...], hi_ref[...]
            gnr, gni = gr[pl.ds(N_SEG, rows), :], gi[pl.ds(N_SEG, rows), :]
            accr[...] += jnp.sum((gnr * hr + gni * hi).reshape(steps, N_SEG, BLOCK_STATE), axis=0)
            acci[...] += jnp.sum((gni * hr - gnr * hi).reshape(steps, N_SEG, BLOCK_STATE), axis=0)
        gr[pl.ds(rows, N_SEG), :] = g0r
        gi[pl.ds(rows, N_SEG), :] = g0i
        if full:
            ub = u_ref[...].astype(BF16)
            gbr, gbi = gr[pl.ds(0, rows), :].astype(BF16), gi[pl.ds(0, rows), :].astype(BF16)
            dcr_ref[...] += lax.dot_general(hr.astype(BF16), dyb, tn, preferred_element_type=F32)
            dci_ref[...] += lax.dot_general(hi.astype(BF16), dyb, tn, preferred_element_type=F32)
            dbr_ref[...] += lax.dot_general(ub, gbr, tn, preferred_element_type=F32)
            dbi_ref[...] += lax.dot_general(ub, gbi, tn, preferred_element_type=F32)
            du_ref[...] = (lax.dot_general(gbr, bdr[...], nt, preferred_element_type=F32)
                           + lax.dot_general(gbi, bdi[...], nt, preferred_element_type=F32)
                           + d_ref[...] * dy_ref[...])
            dd_ref[...] += jnp.sum(dy_ref[...] * u_ref[...], axis=0, keepdims=True)

        @pl.when(c == nc - 1)
        def _():
            if full:
                dar_ref[...] = jnp.sum(accr[...], axis=0, keepdims=True)
                dai_ref[...] = jnp.sum(acci[...], axis=0, keepdims=True)
            else:
                er_ref[...] = g0r
                ei_ref[...] = g0i

    rev = lambda k, c: (nc - 1 - c, k)
    blk3 = lambda a: pl.BlockSpec((None,) + a.shape[1:], lambda k, c: (k, 0, 0))
    seg = pl.BlockSpec((N_SEG, BLOCK_STATE), lambda k, c: (0, k))
    st = pl.BlockSpec((rows, BLOCK_STATE), rev)
    ch = pl.BlockSpec((rows, LANES), rev)
    vec = pl.BlockSpec((1, LANES), lambda k, c: (0, k))
    in_specs = [ch, ch, st, st, blk3(bd_re), blk3(bd_im), blk3(cd_re), blk3(cd_im), seg, seg, seg, seg, vec]
    gbuf = [pltpu.VMEM((rows + N_SEG, BLOCK_STATE), F32)] * 2
    if full:
        row1 = pl.BlockSpec((1, BLOCK_STATE), lambda k, c: (0, k))
        out_specs = [ch, blk3(bd_re), blk3(bd_im), blk3(cd_re), blk3(cd_im), row1, row1, vec]
        out_shape = [jax.ShapeDtypeStruct((s, w), F32),
                     jax.ShapeDtypeStruct(bd_re.shape, F32), jax.ShapeDtypeStruct(bd_im.shape, F32),
                     jax.ShapeDtypeStruct(cd_re.shape, F32), jax.ShapeDtypeStruct(cd_im.shape, F32),
                     jax.ShapeDtypeStruct((1, ns), F32), jax.ShapeDtypeStruct((1, ns), F32),
                     jax.ShapeDtypeStruct((1, w), F32)]
        scratch = gbuf + [pltpu.VMEM((N_SEG, BLOCK_STATE), F32)] * 2
    else:
        out_specs = [seg, seg]
        out_shape = [jax.ShapeDtypeStruct((N_SEG, ns), F32)] * 2
        scratch = gbuf
    return pl.pallas_call(
        body, name=name, grid=(nb, nc), in_specs=in_specs, out_specs=out_specs, out_shape=out_shape,
        scratch_shapes=scratch, compiler_params=_cparams("parallel", "arbitrary"),
    )(dy, u, h_re, h_im, bd_re, bd_im, cd_re, cd_im, ab_re, ab_imn, gin_re, gin_im, d_row)


def _log_sigmoid(x):
    return jnp.minimum(x, 0.0) - jnp.log(1.0 + jnp.exp(-jnp.abs(x)))


def _tri(n, upper):
    r = lax.broadcasted_iota(jnp.int32, (n, n), 0)
    c = lax.broadcasted_iota(jnp.int32, (n, n), 1)
    return jnp.where((c >= r) if upper else (r >= c), 1.0, 0.0).astype(F32)


def _cum_fwd(f_logit, b_row, name):
    s, w = f_logit.shape
    t = _tile(s, 256, SUBLANES)

    def body(f_ref, b_ref, o_ref, carry):
        @pl.when(pl.program_id(0) == 0)
        def _():
            carry[...] = jnp.zeros_like(carry)

        lf = _log_sigmoid(f_ref[...] + b_ref[...])
        cum = jnp.dot(_tri(t, False), lf, precision=lax.Precision.HIGHEST, preferred_element_type=F32) + carry[...]
        o_ref[...] = cum * LOG2E
        carry[...] = cum[t - 1:t, :]

    return pl.pallas_call(
        body, name=name, grid=(s // t,),
        in_specs=[pl.BlockSpec((t, w), lambda i: (i, 0)), pl.BlockSpec((1, w), lambda i: (0, 0))],
        out_specs=pl.BlockSpec((t, w), lambda i: (i, 0)), out_shape=jax.ShapeDtypeStruct((s, w), F32),
        scratch_shapes=[pltpu.VMEM((1, w), F32)], compiler_params=_cparams("arbitrary"),
    )(f_logit, b_row)


def _cum_bwd(dcq, dck, f_logit, b_row, name):
    s, w = f_logit.shape
    t = _tile(s, 256, SUBLANES)
    nt = s // t

    def body(q_ref, k_ref, f_ref, b_ref, df_ref, db_ref, carry):
        @pl.when(pl.program_id(0) == 0)
        def _():
            carry[...] = jnp.zeros_like(carry)
            db_ref[...] = jnp.zeros_like(db_ref)

        dc = q_ref[...] - k_ref[...]
        rc = jnp.dot(_tri(t, True), dc, precision=lax.Precision.HIGHEST, preferred_element_type=F32) + carry[...]
        carry[...] = rc[0:1, :]
        df = rc * (1.0 - jax.nn.sigmoid(f_ref[...] + b_ref[...]))
        df_ref[...] = df.astype(BF16)
        db_ref[...] += jnp.sum(df, axis=0, keepdims=True)

    rev = pl.BlockSpec((t, w), lambda i: (nt - 1 - i, 0))
    one = pl.BlockSpec((1, w), lambda i: (0, 0))
    return pl.pallas_call(
        body, name=name, grid=(nt,), in_specs=[rev, rev, rev, one], out_specs=[rev, one],
        out_shape=[jax.ShapeDtypeStruct((s, w), BF16), jax.ShapeDtypeStruct((1, w), F32)],
        scratch_shapes=[pltpu.VMEM((1, w), F32)], compiler_params=_cparams("arbitrary"),
    )(dcq, dck, f_logit, b_row)


def _head_col(cum_tile, h):
    lane = lax.broadcasted_iota(jnp.int32, cum_tile.shape, 1)
    return jnp.sum(jnp.where(lane == h, cum_tile, 0.0), axis=1, keepdims=True)


def _attn_tiles(s):
    return _tile(s, 512, LANES)


def _exp2_rows(sc, sub):
    return jnp.concatenate([jnp.exp2(sc[:, b * LANES:(b + 1) * LANES] - sub) for b in range(sc.shape[1] // LANES)], axis=1)


def _causal(sc, keys_on_rows):
    r = lax.broadcasted_iota(jnp.int32, sc.shape, 0)
    c = lax.broadcasted_iota(jnp.int32, sc.shape, 1)
    return jnp.where((r <= c) if keys_on_rows else (c <= r), sc, NEG_INF)


def _fox_fwd(q2, kv, cum2_t, name):
    s, w = q2.shape
    nh = w // HEAD_DIM
    tq = _attn_tiles(s)
    nq = s // tq
    nt = (_DOT_DIMS["nt"], ((), ()))

    def body(q_ref, k_ref, v_ref, ct_ref, o_ref, lse_ref, m_s, acc_s, vaug, s_buf):
        i = pl.program_id(1)

        @pl.when(i == 0)
        def _():
            vaug[:, :HEAD_DIM] = v_ref[...]
            vaug[:, HEAD_DIM:] = jnp.ones((s, LANES), BF16)

        qb = q_ref[...]
        m_s[...] = jnp.full_like(m_s, NEG_INF)
        acc_s[...] = jnp.zeros_like(acc_s)

        def scores(j):
            off = pl.multiple_of(j * tq, tq)
            return lax.dot_general(qb, k_ref[pl.ds(off, tq), :], nt, preferred_element_type=F32) - ct_ref[:, pl.ds(off, tq)]

        def softmax_pv(j, sc):
            m_old = m_s[...]
            m_new = jnp.maximum(m_old, jnp.max(sc, axis=1, keepdims=True))
            p = _exp2_rows(sc, m_new)
            alpha = jnp.exp2(m_old - m_new)
            pv = jnp.dot(p.astype(BF16), vaug[pl.ds(pl.multiple_of(j * tq, tq), tq), :], preferred_element_type=F32)
            acc_s[...] = jnp.concatenate([alpha, alpha], axis=1) * acc_s[...] + pv
            m_s[...] = m_new

        s_buf[...] = scores(0)

        def loop(j, carry):
            nxt = scores(j + 1)
            softmax_pv(j, s_buf[...])
            s_buf[...] = nxt
            return carry

        lax.fori_loop(0, i, loop, 0)
        softmax_pv(i, _causal(s_buf[...], False))
        l = acc_s[:, HEAD_DIM:]
        o_ref[...] = acc_s[:, :HEAD_DIM] / l
        lse_ref[...] = m_s[...] + jnp.log(l) * LOG2E

    return pl.pallas_call(
        body, name=name, grid=(nh, nq),
        in_specs=[pl.BlockSpec((tq, HEAD_DIM), lambda h, i: (i, h)),
                  pl.BlockSpec((s, HEAD_DIM), lambda h, i: (0, h)),
                  pl.BlockSpec((s, HEAD_DIM), lambda h, i: (0, nh + h)),
                  pl.BlockSpec((None, 1, s), lambda h, i: (h, 0, 0))],
        out_specs=[pl.BlockSpec((tq, HEAD_DIM), lambda h, i: (i, h)),
                   pl.BlockSpec((None, tq, LANES), lambda h, i: (h, i, 0))],
        out_shape=[jax.ShapeDtypeStruct((s, w), F32), jax.ShapeDtypeStruct((nh, s, LANES), F32)],
        scratch_shapes=[pltpu.VMEM((tq, LANES), F32), pltpu.VMEM((tq, HEAD_DIM + LANES), F32),
                        pltpu.VMEM((s, HEAD_DIM + LANES), BF16), pltpu.VMEM((tq, tq), F32)],
        compiler_params=_cparams("arbitrary", "arbitrary"),
    )(q2, kv, kv, cum2_t)


def _fox_bwd_dq(q2, kv, do, o, lse2, cum2_t, name):
    s, w = q2.shape
    nh = w // HEAD_DIM
    tq = _attn_tiles(s)
    nq = s // tq
    scale = HEAD_DIM ** -0.5
    nt = (_DOT_DIMS["nt"], ((), ()))

    def body(q_ref, k_ref, v_ref, do_ref, o_ref, lse_ref, ct_ref, dq_ref, dl_ref, dcq_ref, acc_s, dc_s):
        i = pl.program_id(1)
        qb = q_ref[...]
        dob = do_ref[...]
        lse = lse_ref[...]
        delta = jnp.broadcast_to(jnp.sum(dob.astype(F32) * o_ref[...], axis=1, keepdims=True), (tq, LANES))
        acc_s[...] = jnp.zeros_like(acc_s)
        dc_s[...] = jnp.zeros_like(dc_s)

        def tile(j, masked):
            off = pl.multiple_of(j * tq, tq)
            kb = k_ref[pl.ds(off, tq), :]
            sc = lax.dot_general(qb, kb, nt, preferred_element_type=F32) - ct_ref[:, pl.ds(off, tq)]
            if masked:
                sc = _causal(sc, False)
            p = _exp2_rows(sc, lse)
            dp = lax.dot_general(dob, v_ref[pl.ds(off, tq), :], nt, preferred_element_type=F32)
            ds = p * (dp - jnp.concatenate([delta] * (tq // LANES), axis=1))
            acc_s[...] += jnp.dot(ds.astype(BF16), kb, preferred_element_type=F32)
            part = ds[:, :LANES]
            for b in range(1, tq // LANES):
                part = part + ds[:, b * LANES:(b + 1) * LANES]
            dc_s[...] += part

        def loop(j, carry):
            tile(j, False)
            return carry

        lax.fori_loop(0, i, loop, 0)
        tile(i, True)
        dq_ref[...] = (acc_s[...] * scale).astype(BF16)
        dl_ref[...] = delta
        dcq_ref[...] = jnp.broadcast_to(jnp.sum(dc_s[...], axis=1, keepdims=True), dcq_ref.shape)

    qspec = pl.BlockSpec((tq, HEAD_DIM), lambda h, i: (i, h))
    rep = pl.BlockSpec((None, tq, LANES), lambda h, i: (h, i, 0))
    return pl.pallas_call(
        body, name=name, grid=(nh, nq),
        in_specs=[qspec,
                  pl.BlockSpec((s, HEAD_DIM), lambda h, i: (0, h)),
                  pl.BlockSpec((s, HEAD_DIM), lambda h, i: (0, nh + h)),
                  qspec, qspec, rep,
                  pl.BlockSpec((None, 1, s), lambda h, i: (h, 0, 0))],
        out_specs=[qspec, rep, rep],
        out_shape=[jax.ShapeDtypeStruct((s, w), BF16), jax.ShapeDtypeStruct((nh, s, LANES), F32),
                   jax.ShapeDtypeStruct((nh, s, LANES), F32)],
        scratch_shapes=[pltpu.VMEM((tq, HEAD_DIM), F32), pltpu.VMEM((tq, LANES), F32)],
        compiler_params=_cparams("parallel", "arbitrary"),
    )(q2, kv, kv, do, o, lse2, cum2_t)


def _fox_bwd_dkv(q2, kv, do, lse2_t, delta_t, cum2, name):
    s, w = q2.shape
    nh = w // HEAD_DIM
    tk = _attn_tiles(s)
    nk = s // tk
    nt = (_DOT_DIMS["nt"], ((), ()))

    def body(q_ref, k_ref, v_ref, do_ref, lse_ref, dl_ref, c_ref, dk_ref, dv_ref, dck_ref, dk_s, dv_s, dc_s, s_buf, dp_buf):
        h, j = pl.program_id(0), pl.program_id(1)
        kb = k_ref[...]
        vb = v_ref[...]
        ck = jnp.broadcast_to(_head_col(c_ref[...], h), (tk, LANES))
        dk_s[...] = jnp.zeros_like(dk_s)
        dv_s[...] = jnp.zeros_like(dv_s)
        dc_s[...] = jnp.zeros_like(dc_s)

        def scores(i):
            off = pl.multiple_of(i * tk, tk)
            sc = lax.dot_general(kb, q_ref[pl.ds(off, tk), :], nt, preferred_element_type=F32) - lse_ref[:, pl.ds(off, tk)]
            dp = lax.dot_general(vb, do_ref[pl.ds(off, tk), :], nt, preferred_element_type=F32) - dl_ref[:, pl.ds(off, tk)]
            return sc, dp

        def accumulate(i, sc, dp):
            off = pl.multiple_of(i * tk, tk)
            p = _exp2_rows(sc, ck)
            dv_s[...] += jnp.dot(p.astype(BF16), do_ref[pl.ds(off, tk), :], preferred_element_type=F32)
            ds = p * dp
            dk_s[...] += jnp.dot(ds.astype(BF16), q_ref[pl.ds(off, tk), :], preferred_element_type=F32)
            part = ds[:, :LANES]
            for b in range(1, tk // LANES):
                part = part + ds[:, b * LANES:(b + 1) * LANES]
            dc_s[...] += part

        sc0, dp0 = scores(j)
        s_buf[...] = _causal(sc0, True)
        dp_buf[...] = dp0

        def loop(i, carry):
            nxt = scores(i + 1)
            accumulate(i, s_buf[...], dp_buf[...])
            s_buf[...], dp_buf[...] = nxt
            return carry

        lax.fori_loop(j, nk - 1, loop, 0)
        accumulate(nk - 1, s_buf[...], dp_buf[...])
        dk_ref[...] = (dk_s[...] * (1.0 / LOG2E)).astype(BF16)
        dv_ref[...] = dv_s[...].astype(BF16)
        dck_ref[...] = jnp.broadcast_to(jnp.sum(dc_s[...], axis=1, keepdims=True), dck_ref.shape)

    col = pl.BlockSpec((s, HEAD_DIM), lambda h, j: (0, h))
    row = pl.BlockSpec((None, 1, s), lambda h, j: (h, 0, 0))
    kspec = pl.BlockSpec((tk, HEAD_DIM), lambda h, j: (j, h))
    return pl.pallas_call(
        body, name=name, grid=(nh, nk),
        in_specs=[col, kspec, pl.BlockSpec((tk, HEAD_DIM), lambda h, j: (j, nh + h)), col, row, row,
                  pl.BlockSpec((tk, LANES), lambda h, j: (j, 0))],
        out_specs=[kspec, kspec, pl.BlockSpec((None, tk, LANES), lambda h, j: (h, j, 0))],
        out_shape=[jax.ShapeDtypeStruct((s, w), BF16), jax.ShapeDtypeStruct((s, w), BF16),
                   jax.ShapeDtypeStruct((nh, s, LANES), F32)],
        scratch_shapes=[pltpu.VMEM((tk, HEAD_DIM), F32), pltpu.VMEM((tk, HEAD_DIM), F32),
                        pltpu.VMEM((tk, LANES), F32), pltpu.VMEM((tk, tk), F32), pltpu.VMEM((tk, tk), F32)],
        compiler_params=_cparams("parallel", "arbitrary"),
    )(q2, kv, kv, do, lse2_t, delta_t, cum2)


def _exchange_copies(ins, outs, send_sems, recv_sems, local_sems, scatter):
    x, y, c = (lax.axis_index(a) for a in MESH_AXES)
    me = 4 * x + 2 * y + c
    local, remote = [], []
    for a in range(len(ins)):
        local.append(pltpu.make_async_copy(ins[a].at[me] if scatter else ins[a], outs[a].at[me], local_sems.at[a]))
        for k in range(1, N_DEV):
            px, py, pc = (1 - x if k & 4 else x), (1 - y if k & 2 else y), (1 - c if k & 1 else c)
            remote.append(pltpu.make_async_remote_copy(
                src_ref=ins[a].at[4 * px + 2 * py + pc] if scatter else ins[a], dst_ref=outs[a].at[me],
                send_sem=send_sems.at[a * (N_DEV - 1) + k - 1], recv_sem=recv_sems.at[a * (N_DEV - 1) + k - 1],
                device_id=(px, py, pc), device_id_type=pl.DeviceIdType.MESH))
    return local, remote


def _exchange_out_shapes(arrs, scatter):
    return [((N_DEV,) + a.shape[1:]) if scatter else ((N_DEV,) + a.shape) for a in arrs]


def _exchange(arrs, scatter, name):
    n = len(arrs)

    def body(*refs):
        local, remote = _exchange_copies(refs[:n], refs[n:2 * n], *refs[2 * n:], scatter)
        for cp in local + remote:
            cp.start()
        for cp in remote:
            cp.wait_send()
            cp.wait_recv()
        for cp in local:
            cp.wait()

    out_shape = [jax.ShapeDtypeStruct(s, a.dtype) for s, a in zip(_exchange_out_shapes(arrs, scatter), arrs)]
    return pl.pallas_call(
        body, name=name, out_shape=out_shape,
        in_specs=[pl.BlockSpec(memory_space=pl.ANY)] * n, out_specs=[pl.BlockSpec(memory_space=pl.ANY)] * n,
        scratch_shapes=[pltpu.SemaphoreType.DMA((n * (N_DEV - 1),)), pltpu.SemaphoreType.DMA((n * (N_DEV - 1),)),
                        pltpu.SemaphoreType.DMA((n,))],
    )(*arrs)


_HBM = pl.BlockSpec(memory_space=pltpu.HBM)
_SEM = pl.BlockSpec(memory_space=pltpu.SEMAPHORE)


def _exchange_start(arrs, scatter, name):
    n = len(arrs)
    lands = [lax.empty(s, a.dtype) for s, a in zip(_exchange_out_shapes(arrs, scatter), arrs)]

    def body(*refs):
        ins, outs = refs[:n], refs[n:2 * n]
        send_sems, recv_sems, local_sems = refs[2 * n:2 * n + 3]
        token = refs[-1]
        local, remote = _exchange_copies(ins, outs, send_sems, recv_sems, local_sems, scatter)
        for cp in local + remote:
            cp.start()
        token[...] = jnp.zeros_like(token)

    hbm = lambda a: pltpu.HBM(a.shape, a.dtype)
    res = pl.pallas_call(
        body, name=name,
        out_shape=(pltpu.SemaphoreType.DMA((n * (N_DEV - 1),)), pltpu.SemaphoreType.DMA((n * (N_DEV - 1),)),
                   pltpu.SemaphoreType.DMA((n,)), *[hbm(a) for a in arrs], *[hbm(a) for a in lands],
                   jax.ShapeDtypeStruct((SUBLANES, LANES), F32)),
        in_specs=[_HBM] * (2 * n),
        out_specs=(_SEM, _SEM, _SEM, *[_HBM] * (2 * n), pl.BlockSpec(memory_space=pltpu.VMEM)),
        input_output_aliases={i: 3 + i for i in range(2 * n)},
        compiler_params=pltpu.CompilerParams(has_side_effects=pltpu.SideEffectType.DATAFLOW_SIDE_EFFECTING),
    )(*[pltpu.with_memory_space_constraint(a, pltpu.HBM) for a in list(arrs) + lands])
    return (n, scatter, res[:3], res[3:3 + n], res[3 + n:3 + 2 * n]), res[-1]


def _exchange_wait(state, after, name):
    n, scatter, sems, srcs, lands = state

    def body(*refs):
        ins, outs = refs[:n], refs[n:2 * n]
        send_sems, recv_sems, local_sems = refs[2 * n:2 * n + 3]
        local, remote = _exchange_copies(ins, outs, send_sems, recv_sems, local_sems, scatter)
        for cp in remote:
            cp.wait_send()
            cp.wait_recv()
        for cp in local:
            cp.wait()

    hbm = lambda a: pltpu.HBM(a.shape, a.dtype)
    res = pl.pallas_call(
        body, name=name,
        out_shape=(*[hbm(a) for a in srcs], *[hbm(a) for a in lands]),
        in_specs=[_HBM] * (2 * n) + [_SEM] * 3 + [pl.BlockSpec(memory_space=pl.ANY)],
        out_specs=tuple([_HBM] * (2 * n)),
        input_output_aliases={i: i for i in range(2 * n)},
        compiler_params=pltpu.CompilerParams(has_side_effects=pltpu.SideEffectType.DATAFLOW_SIDE_EFFECTING),
    )(*srcs, *lands, *sems, after)
    return list(res[n:])


def _adamw_math(w, g, m, v):
    m = ADAM_B1 * m + (1.0 - ADAM_B1) * g
    v = ADAM_B2 * v + (1.0 - ADAM_B2) * (g * g)
    m_hat = m / (1.0 - ADAM_B1 ** ADAM_STEP)
    v_hat = v / (1.0 - ADAM_B2 ** ADAM_STEP)
    return -ADAM_LR * (m_hat / (jnp.sqrt(v_hat) + ADAM_EPS) + ADAM_WD * w), m, v


def _adamw(parts, w, m, v, name):
    r, c = w.shape
    tr = _tile(r, max(SUBLANES, (256 * 1024) // c // SUBLANES * SUBLANES), SUBLANES)

    def body(p_ref, w_ref, m_ref, v_ref, g_ref, d_ref, nm_ref, nv_ref):
        g = p_ref[0].astype(F32)
        for d in range(1, N_DEV):
            g = g + p_ref[d].astype(F32)
        g_ref[...] = g
        d_ref[...], nm_ref[...], nv_ref[...] = _adamw_math(w_ref[...], g, m_ref[...], v_ref[...])

    blk = pl.BlockSpec((tr, c), lambda i: (i, 0))
    sh = jax.ShapeDtypeStruct((r, c), F32)
    return pl.pallas_call(
        body, name=name, grid=(r // tr,),
        in_specs=[pl.BlockSpec((N_DEV, tr, c), lambda i: (0, i, 0)), blk, blk, blk],
        out_specs=[blk] * 4, out_shape=[sh] * 4, compiler_params=_cparams("parallel"),
    )(parts, w, m, v)


def _perm(a):
    s, d = a.shape
    return a.reshape(N_SEG, s // N_SEG, d).transpose(1, 0, 2).reshape(s, d)


def _unperm(a):
    s, d = a.shape
    return a.reshape(s // N_SEG, N_SEG, d).transpose(1, 0, 2).reshape(s, d)


def _lane_pad(a, width=LANES):
    return jnp.pad(a, ((0, 0), (0, width - a.shape[1])))


def _local_step(x, target, norm_pre, norm_post, kv_norm, kv_b_f, a_re, a_im, log_dt, b_re, b_im, c_re, c_im, comm):
    s, d = x.shape
    g, p = a_re.shape
    w = g * S5_GROUP
    fw = d
    nh = fw // HEAD_DIM
    seg_len = s // N_SEG
    row = lambda v: v.reshape(1, -1)
    g_pre0, g_pre1, g_post0, g_post1, g_kv = row(norm_pre[0]), row(norm_pre[1]), row(norm_post[0]), row(norm_post[1]), row(kv_norm)
    d_row, bglu_row = row(comm.vector("s5_d")), row(comm.vector("s5_b_glu"))

    ldt = log_dt.reshape(g, 1)
    abr, abi, cr, ci = _s5_disc_fwd(a_re, a_im, ldt)
    cr_col, ci_col = cr.reshape(g * p, 1), ci.reshape(g * p, 1)
    b_re2, b_im2 = b_re.reshape(g * p, S5_GROUP), b_im.reshape(g * p, S5_GROUP)
    bb_re, bb_im = _s5_bbar_fwd(cr_col, ci_col, b_re2, b_im2)
    bd_re = _block_diag_in(bb_re.reshape(g, p, S5_GROUP)).astype(BF16)
    bd_im = _block_diag_in(bb_im.reshape(g, p, S5_GROUP)).astype(BF16)
    cd_re = _block_diag_out(c_re).astype(BF16)
    cd_im = _block_diag_out(-c_im).astype(BF16)
    ab_re = jnp.broadcast_to(abr.reshape(1, g * p), (N_SEG, g * p))
    ab_im = jnp.broadcast_to(abi.reshape(1, g * p), (N_SEG, g * p))
    zero_seg = jnp.zeros((N_SEG, g * p), F32)

    w_in = comm.weight("s5_w_in", None)
    xn0 = _norm_cast(x, g_pre0 + comm.token, "norm_pre0", x_kind="nat")
    u = _mm(xn0, w_in, "nn", F32, "s5_in_u", b_cols=(0, w))
    z0 = _mm(xn0, w_in, "nn", F32, "s5_in_z", b_cols=(w, w))
    e_re, e_im = _s5_scan_fwd(u, bd_re, bd_im, cd_re, cd_im, ab_re, ab_im, zero_seg, zero_seg, d_row, False, "s5_scan_ends")
    i_re, i_im = _s5_seg_fix(e_re, e_im, ab_re, ab_im, seg_len, False, "s5_seg_fix")
    y_ssm, h_re, h_im, _, _ = _s5_scan_fwd(u, bd_re, bd_im, cd_re, cd_im, ab_re, ab_im, i_re, i_im, d_row, True, "s5_scan")
    yg = _gelu_cast(y_ssm, "s5_gelu")
    w_glu, w_out = comm.weight("s5_w_glu", yg), comm.weight("s5_w_out", yg)
    gp = _mm(yg, w_glu, "nn", F32, "s5_glu")
    y3 = _s5_gate(y_ssm, gp, bglu_row, z0, "s5_gate")
    o0 = _mm(y3, w_out, "nn", F32, "s5_out")
    r0 = _post_norm(o0, g_post0, "norm_post0", out_kind="nat")

    h1, hn_kv, xn1 = _resid_norm2(x, r0, g_kv, g_pre1, "resid_norms")
    w_kv, fw_in, fw_out = comm.weight("kv_w", hn_kv), comm.weight("fox_w_in", hn_kv), comm.weight("fox_w_out", hn_kv)
    w_f =_lane_pad(w_kv[:, 2 * fw:])
    kv = _mm(hn_kv, w_kv, "nn", BF16, "kv_proj", b_cols=(0, 2 * fw))
    f_logit = _mm(hn_kv, w_f, "nn", F32, "f_proj")
    bf_row = _lane_pad(row(kv_b_f))
    cum2 = _cum_fwd(f_logit, bf_row, "cum_fwd")
    cum2_t = cum2[:, :nh].T.reshape(nh, 1, s)
    q2 = _mm(xn1, fw_in, "nn", BF16, "fox_q", scale=HEAD_DIM ** -0.5 * LOG2E, b_cols=(0, fw))
    z1 = _mm(xn1, fw_in, "nn", F32, "fox_z", b_cols=(fw, fw))
    o, lse2 = _fox_fwd(q2, kv, cum2_t, "fox_fwd")
    oz = _gate_mul(o, z1, "fox_gate")
    o1 = _mm(oz, fw_out, "nn", F32, "fox_out")
    dh2, sq = _post_norm_loss(o1, g_post1, h1, target, "norm_post1_loss")
    loss = 0.5 * jnp.sum(sq) / d

    do1, dg_post1 = _post_norm_bwd(dh2, o1, g_post1, "norm_post1_bwd")
    d_fw_out = _mm(oz, do1, "tn", F32, "fox_out_dw")
    d_oz = _mm(do1, fw_out, "nt", F32, "fox_out_dx")
    do, dz1 = _gate_bwd(d_oz, o, z1, "fox_gate_bwd")
    dq, delta, dcq = _fox_bwd_dq(q2, kv, do, o, lse2, cum2_t, "fox_bwd_dq")
    lse2_t = lse2[:, :, 0].reshape(nh, 1, s)
    delta_t = delta[:, :, 0].reshape(nh, 1, s)
    dk, dv, dck = _fox_bwd_dkv(q2, kv, do, lse2_t, delta_t, cum2, "fox_bwd_dkv")
    dqz = _concat_cast(dq, dz1, "fox_dqz")
    d_fw_in = _mm(xn1, dqz, "tn", F32, "fox_in_dw")
    dxn1 = _mm(dqz, fw_in, "nt", F32, "fox_in_dx")
    dcq_sl = _lane_pad(dcq[:, :, 0].T)
    dck_sl = _lane_pad(dck[:, :, 0].T)
    df, db_f = _cum_bwd(dcq_sl, dck_sl, f_logit, bf_row, "cum_bwd")
    dkv = _concat_cast(dk, dv, "fox_dkv")
    d_w_kvm = _mm(hn_kv, dkv, "tn", F32, "kv_dw")
    d_w_f = _mm(hn_kv, df, "tn", F32, "f_dw")
    dhn_f = _mm(df, w_f, "nt", F32, "f_dx")
    dhn_kv = _mm(dkv, w_kv, "nt", F32, "kv_dx", add=dhn_f, b_cols=(0, 2 * fw))
    d_w_kv = jnp.concatenate([d_w_kvm, d_w_f[:, :nh]], axis=1)
    tok = comm.send_grads(dict(fox_w_out=d_fw_out, fox_w_in=d_fw_in, kv_w=d_w_kv), "exchange_fox")
    dh1, dg_pre1, dg_kv = _norm_bwd2(dh2, h1, dxn1, dhn_kv, g_pre1, g_kv, "resid_norms_bwd")

    do0, dg_post0 = _post_norm_bwd(dh1, o0, g_post0 + tok, "norm_post0_bwd", dy_kind="nat")
    d_w_out = _mm(y3, do0, "tn", F32, "s5_out_dw")
    dy3 = _mm(do0, w_out, "nt", F32, "s5_out_dx")
    dz0, dgp, dyg_direct, db_glu = _s5_gate_bwd(dy3, y_ssm, gp, bglu_row, z0, "s5_gate_bwd")
    d_w_glu = _mm(yg, dgp, "tn", F32, "s5_glu_dw")
    dyg = _mm(dgp, w_glu, "nt", F32, "s5_glu_dx", add=dyg_direct)
    dy_ssm = _gelu_bwd(dyg, y_ssm, "s5_gelu_bwd")
    d_row = d_row + comm.send_grads(dict(s5_w_out=d_w_out, s5_w_glu=d_w_glu), "exchange_s5")
    ab_imn = -ab_im
    ge_re, ge_im = _s5_scan_bwd(dy_ssm, u, h_re, h_im, bd_re, bd_im, cd_re, cd_im, ab_re, ab_imn, zero_seg, zero_seg,
                                d_row, False, "s5_adj_ends")
    gi_re, gi_im = _s5_seg_fix(ge_re, ge_im, ab_re, ab_imn, seg_len, True, "s5_adj_fix")
    du, dbd_re, dbd_im, dcd_re, dcd_im, dab_re, dab_im, dd = _s5_scan_bwd(
        dy_ssm, u, h_re, h_im, bd_re, bd_im, cd_re, cd_im, ab_re, ab_imn, gi_re, gi_im, d_row, True, "s5_adj")
    duz = _concat_cast(du, dz0, "s5_duz")
    d_w_in = _mm(xn0, duz, "tn", F32, "s5_in_dw")
    comm.send_grads(dict(s5_w_in=d_w_in), "exchange_s5_in")
    dxn0 =_mm(duz, w_in, "nt", F32, "s5_in_dx")
    grad_x, dg_pre0 = _norm_bwd1(dh1, x, dxn0, g_pre0, "norm_pre0_bwd")

    dbb_re = _block_diag_in_extract(dbd_re, p, S5_GROUP).reshape(g * p, S5_GROUP)
    dbb_im = _block_diag_in_extract(dbd_im, p, S5_GROUP).reshape(g * p, S5_GROUP)
    dcr_col, dci_col, db_re, db_im = _s5_bbar_bwd(cr_col, ci_col, b_re2, b_im2, dbb_re, dbb_im)
    da_re, da_im, dldt = _s5_disc_bwd(a_re, a_im, ldt, dab_re.reshape(g, p), dab_im.reshape(g, p),
                                      dcr_col.reshape(g, p), dci_col.reshape(g, p))
    dc_re = _block_diag_out_extract(dcd_re, S5_GROUP, p)
    dc_im = -_block_diag_out_extract(dcd_im, S5_GROUP, p)

    small = dict(
        norm_pre=jnp.concatenate([dg_pre0, dg_pre1], axis=0), norm_post=jnp.concatenate([dg_post0, dg_post1], axis=0),
        s5_a_re=da_re, s5_a_im=da_im, s5_log_dt=dldt.reshape(g), s5_b_re=db_re.reshape(g, p, S5_GROUP),
        s5_b_im=db_im.reshape(g, p, S5_GROUP), s5_c_re=dc_re, s5_c_im=dc_im, s5_d=dd.reshape(-1),
        s5_b_glu=db_glu.reshape(-1), kv_norm=dg_kv.reshape(-1), kv_b_f=db_f[0, :nh])
    return loss, grad_x, small


_BIG = ("s5_w_in", "s5_w_glu", "s5_w_out", "kv_w", "fox_w_in", "fox_w_out")
_COL_SHARDED = ("s5_w_in", "kv_w", "fox_w_in")
_SMALL = ("norm_pre", "norm_post", "s5_a_re", "s5_a_im", "s5_log_dt", "s5_b_re", "s5_b_im", "s5_c_re", "s5_c_im",
          "s5_d", "s5_b_glu", "kv_norm", "kv_b_f")
_SMALL_SHARDED = ("s5_d", "s5_b_glu")
_PACK_QUANTUM = SUBLANES * LANES
_WEIGHTS = ('norm_pre', 'norm_post', 's5_w_in', 's5_a_re', 's5_a_im', 's5_log_dt', 's5_b_re', 's5_b_im', 's5_c_re', 's5_c_im',
            's5_d', 's5_w_glu', 's5_b_glu', 's5_w_out', 'kv_norm', 'kv_w', 'kv_b_f', 'fox_w_in', 'fox_w_out')


def _full_from_slots(name, slots):
    n, r, c = slots.shape
    if name in _COL_SHARDED:
        return slots.transpose(1, 0, 2).reshape(r, n * c)
    return slots.reshape(n * r, c)


def _slots_from_full(name, full):
    if name in _COL_SHARDED:
        r, nc = full.shape
        return full.reshape(r, N_DEV, nc // N_DEV).transpose(1, 0, 2)
    nr, c = full.shape
    return full.reshape(N_DEV, nr // N_DEV, c)


def _pack(vals):
    parts = []
    for v in vals:
        flat = v.reshape(-1)
        parts.append(jnp.pad(flat, (0, (-flat.shape[0]) % _PACK_QUANTUM)))
    return jnp.concatenate(parts).reshape(-1, LANES)


def _unpack(packed, shapes):
    flat = packed.reshape(-1)
    out, off = [], 0
    for sh in shapes:
        n = math.prod(sh)
        out.append(flat[off:off + n].reshape(sh))
        off += n + (-n) % _PACK_QUANTUM
    return out


class _Comm:
    _GROUPS = (("s5_w_glu", "s5_w_out"), ("kv_w", "fox_w_in", "fox_w_out"))

    def __init__(self, shards, vectors):
        first = _exchange([shards["s5_w_in"]] + [vectors[n] for n in _SMALL_SHARDED], False, "gather_first")
        self._full = {"s5_w_in": _full_from_slots("s5_w_in", first[0])}
        self._vectors = {n: first[1 + i].reshape(-1) for i, n in enumerate(_SMALL_SHARDED)}
        self._gathers = {}
        self.token = jnp.zeros((), F32)
        for group in self._GROUPS:
            state, tok = _exchange_start([shards[n] for n in group], False, "gather_start_" + group[0])
            self._gathers[group] = state
            self.token = self.token + tok[0, 0]
        self._sent = []

    def vector(self, name):
        return self._vectors[name]

    def weight(self, name, after):
        if name not in self._full:
            group = next(g for g in self._GROUPS if name in g)
            slots = _exchange_wait(self._gathers.pop(group), after, "gather_wait_" + group[0])
            for n, sl in zip(group, slots):
                self._full[n] = _full_from_slots(n, sl)
        return self._full[name]

    def send_grads(self, grads, name):
        names = list(grads)
        state, tok = _exchange_start([_slots_from_full(n, grads[n]).astype(BF16) for n in names], True, name + "_start")
        self._sent.append((names, state, name + "_wait"))
        return tok[0, 0]

    def received_grads(self, after):
        for names, state, name in self._sent:
            for n, recv in zip(names, _exchange_wait(state, after, name)):
                yield n, recv


def kernel(x, norm_pre, norm_post, s5_w_in, s5_a_re, s5_a_im, s5_log_dt, s5_b_re, s5_b_im, s5_c_re, s5_c_im, s5_d, s5_w_glu, s5_b_glu, s5_w_out, kv_norm, kv_w, kv_b_f, fox_w_in, fox_w_out, loss_target, m_norm_pre, m_norm_post, m_s5_w_in, m_s5_a_re, m_s5_a_im, m_s5_log_dt, m_s5_b_re, m_s5_b_im, m_s5_c_re, m_s5_c_im, m_s5_d, m_s5_w_glu, m_s5_b_glu, m_s5_w_out, m_kv_norm, m_kv_w, m_kv_b_f, m_fox_w_in, m_fox_w_out, v_norm_pre, v_norm_post, v_s5_w_in, v_s5_a_re, v_s5_a_im, v_s5_log_dt, v_s5_b_re, v_s5_b_im, v_s5_c_re, v_s5_c_im, v_s5_d, v_s5_w_glu, v_s5_b_glu, v_s5_w_out, v_kv_norm, v_kv_w, v_kv_b_f, v_fox_w_in, v_fox_w_out):
    env = dict(locals())
    wts = {n: env[n] for n in _WEIGHTS}
    mom = {n: env["m_" + n] for n in _WEIGHTS}
    var = {n: env["v_" + n] for n in _WEIGHTS}
    me = 4 * lax.axis_index("x") + 2 * lax.axis_index("y") + lax.axis_index("c")
    shard2d = {n: wts[n].reshape(wts[n].shape[-2:]) for n in _BIG}
    comm = _Comm({n: shard2d[n].astype(BF16) for n in _BIG}, {n: wts[n].reshape(1, -1) for n in _SMALL_SHARDED})

    loss_local, grad_x, small = _local_step(
        x[0], loss_target[0], norm_pre, norm_post, kv_norm, kv_b_f, s5_a_re[0], s5_a_im[0], s5_log_dt[0],
        s5_b_re[0], s5_b_im[0], s5_c_re[0], s5_c_im[0], comm)
    loss = lax.psum(loss_local, MESH_AXES)

    small_state, small_tok = _exchange_start([_pack([small[n] for n in _SMALL])], False, "gather_small_start")

    res = {}
    for n, recv in comm.received_grads(small_tok):
        outs = _adamw(recv, shard2d[n], mom[n].reshape(shard2d[n].shape), var[n].reshape(shard2d[n].shape), "adamw_" + n)
        res[n] = [o.reshape(wts[n].shape) for o in outs]

    full_shape = {n: (small[n].shape if n in _SMALL_SHARDED else wts[n].shape) for n in _SMALL}

    def spread(n, v):
        if n not in _SMALL_SHARDED:
            return v
        flat = v.reshape(-1)
        return lax.dynamic_update_slice(jnp.zeros(full_shape[n], F32), flat, (me * flat.shape[0],))

    g_all = _exchange_wait(small_state, res[_BIG[0]][0], "gather_small_wait")[0]
    packed =[_pack([spread(n, src[n]) for n in _SMALL]) for src in (wts, mom, var)]
    outs = _adamw(g_all, *packed, "adamw_small")
    unpacked = [_unpack(o, [full_shape[n] for n in _SMALL]) for o in outs]
    for i, n in enumerate(_SMALL):
        vals = [u[i] for u in unpacked]
        if n in _SMALL_SHARDED:
            k = wts[n].size
            vals = [lax.dynamic_slice(v, (me * k,), (k,)) for v in vals]
        res[n] = [v.reshape(wts[n].shape) for v in vals]

    return (loss, grad_x[None], *[res[n][0] for n in _WEIGHTS], *[res[n][1] for n in _WEIGHTS],
            *[res[n][2] for n in _WEIGHTS], *[res[n][3] for n in _WEIGHTS])
```

```python
import functools
import math

import jax
import jax.numpy as jnp
from jax import lax
from jax.experimental import pallas as pl
from jax.experimental.pallas import tpu as pltpu

F32 = jnp.float32
BF16 = jnp.bfloat16

N_DEV = 8
MESH_AXES = ("x", "y", "c")
S5_GROUP = 16
S5_STATE = 64
LANES = 128
SUBLANES = 8
GROUPS_PER_BLOCK = LANES // S5_GROUP
BLOCK_STATE = GROUPS_PER_BLOCK * S5_STATE
N_SEG = SUBLANES
HEAD_DIM = 128
RMS_EPS = 1e-6
NEG_INF = -1e30
LOG2E = math.log2(math.e)
ADAM_LR = 0.001
ADAM_B1 = 0.9
ADAM_B2 = 0.999
ADAM_EPS = 1e-08
ADAM_WD = 0.01
ADAM_STEP = 10
VMEM_LIMIT = 56 * 1024 * 1024


def _tile(n, pref, quantum=LANES):
    if n <= pref:
        return n
    t = (pref // quantum) * quantum
    while t >= quantum:
        if n % t == 0:
            return t
        t -= quantum
    return n


def _cparams(*sem):
    return pltpu.CompilerParams(dimension_semantics=sem if sem else None, vmem_limit_bytes=VMEM_LIMIT)


_DOT_DIMS = {"nn": ((1,), (0,)), "nt": ((1,), (1,)), "tn": ((0,), (0,))}


def _mm(a, b, mode, out_dtype, name, add=None, scale=None, b_cols=None):
    b_shape = b.shape if b_cols is None else (b.shape[0], b_cols[1])
    if mode == "nn":
        (M, K), (K2, N) = a.shape, b_shape
    elif mode == "nt":
        (M, K), (N, K2) = a.shape, b_shape
    else:
        (K, M), (K2, N) = a.shape, b_shape
    assert K == K2, (name, a.shape, b_shape)
    tm, tn, tk = _tile(M, 1024), _tile(N, 1024), _tile(K, 1024)
    nk = K // tk
    dims = (_DOT_DIMS[mode], ((), ()))
    col0 = 0
    if b_cols is not None:
        assert mode != "tn" and b_cols[0] % (tn if mode == "nn" else tk) == 0
        col0 = b_cols[0] // (tn if mode == "nn" else tk)

    def body(*refs):
        if add is None:
            a_ref, b_ref, o_ref, acc = refs
        else:
            a_ref, b_ref, c_ref, o_ref, acc = refs
        k = pl.program_id(2)

        @pl.when(k == 0)
        def _():
            acc[...] = jnp.zeros_like(acc)

        acc[...] += lax.dot_general(a_ref[...], b_ref[...], dims, preferred_element_type=F32)

        @pl.when(k == nk - 1)
        def _():
            r = acc[...]
            if scale is not None:
                r = r * scale
            if add is not None:
                r = r + c_ref[...]
            o_ref[...] = r.astype(out_dtype)

    if mode == "tn":
        a_spec = pl.BlockSpec((tk, tm), lambda i, j, k: (k, i))
    else:
        a_spec = pl.BlockSpec((tm, tk), lambda i, j, k: (i, k))
    if mode == "nt":
        b_spec = pl.BlockSpec((tn, tk), lambda i, j, k: (j, k + col0))
    else:
        b_spec = pl.BlockSpec((tk, tn), lambda i, j, k: (k, j + col0))
    o_spec = pl.BlockSpec((tm, tn), lambda i, j, k: (i, j))
    in_specs = [a_spec, b_spec] + ([o_spec] if add is not None else [])
    args = (a, b) + ((add,) if add is not None else ())
    return pl.pallas_call(
        body, name=name, grid=(M // tm, N // tn, nk),
        in_specs=in_specs, out_specs=o_spec,
        out_shape=jax.ShapeDtypeStruct((M, N), out_dtype),
        scratch_shapes=[pltpu.VMEM((tm, tn), F32)],
        compiler_params=_cparams("parallel", "parallel", "arbitrary"),
    )(*args)


class _NatIn:
    def __init__(self, ref):
        self.ref = ref

    def __getitem__(self, idx):
        v = jnp.swapaxes(self.ref[...], 0, 1)
        return v.reshape(v.shape[0] * N_SEG, v.shape[2])


class _NatOut:
    def __init__(self, ref):
        self.ref = ref

    def __setitem__(self, idx, val):
        self.ref[...] = jnp.swapaxes(val.reshape(val.shape[0] // N_SEG, N_SEG, val.shape[1]), 0, 1)


def _rowcall(body, name, n_rows, ins, outs, tile_rows=256):
    tr = _tile(n_rows, tile_rows, SUBLANES * 2)
    n_in = len(ins)
    in_kinds = [k for _, k in ins]
    kinds = [k for _, _, k in outs]

    def kern(*refs):
        @pl.when(pl.program_id(0) == 0)
        def _():
            for r, kind in zip(refs[n_in:], kinds):
                if kind == "acc":
                    r[...] = jnp.zeros_like(r)

        wrapped = [_NatIn(r) if k == "nat" else r for r, k in zip(refs[:n_in], in_kinds)]
        wrapped += [_NatOut(r) if k == "nat" else r for r, k in zip(refs[n_in:], kinds)]
        body(*wrapped)

    in_specs, args = [], []
    for arr, kind in ins:
        if kind == "row":
            in_specs.append(pl.BlockSpec((tr, arr.shape[1]), lambda i: (i, 0)))
        elif kind == "nat":
            in_specs.append(pl.BlockSpec((N_SEG, tr // N_SEG, arr.shape[1]), lambda i: (0, i, 0)))
            arr = arr.reshape(N_SEG, n_rows // N_SEG, arr.shape[1])
        else:
            in_specs.append(pl.BlockSpec(arr.shape, lambda i, nd=arr.ndim: (0,) * nd))
        args.append(arr)
    out_specs, out_shape = [], []
    for width, dtype, kind in outs:
        if kind == "row":
            out_specs.append(pl.BlockSpec((tr, width), lambda i: (i, 0)))
            out_shape.append(jax.ShapeDtypeStruct((n_rows, width), dtype))
        elif kind == "nat":
            out_specs.append(pl.BlockSpec((N_SEG, tr // N_SEG, width), lambda i: (0, i, 0)))
            out_shape.append(jax.ShapeDtypeStruct((N_SEG, n_rows // N_SEG, width), dtype))
        else:
            out_specs.append(pl.BlockSpec((1, width), lambda i: (0, 0)))
            out_shape.append(jax.ShapeDtypeStruct((1, width), F32))
    res = pl.pallas_call(
        kern, name=name, grid=(n_rows // tr,), in_specs=in_specs, out_specs=out_specs, out_shape=out_shape,
        compiler_params=_cparams("arbitrary"),
    )(*args)
    return [r.reshape(n_rows, r.shape[2]) if k == "nat" else r for r, k in zip(res, kinds)]


def _rstd(x):
    return lax.rsqrt(jnp.mean(x * x, axis=-1, keepdims=True) + RMS_EPS)


def _rms_bwd(x, g, dy):
    xh = x * _rstd(x)
    dxh = dy * g
    dx = _rstd(x) * (dxh - xh * jnp.mean(dxh * xh, axis=-1, keepdims=True))
    return dx, jnp.sum(dy * xh, axis=0, keepdims=True)


def _silu(z):
    return z * jax.nn.sigmoid(z)


def _norm_cast(x, g, name, x_kind="row"):
    def body(x_ref, g_ref, o_ref):
        x = x_ref[...]
        o_ref[...] = (x * _rstd(x) * g_ref[...]).astype(BF16)

    return _rowcall(body, name, x.shape[0], [(x, x_kind), (g, "full")], [(x.shape[1], BF16, "row")])[0]


def _resid_norm2(x, r0, g_kv, g_pre, name):
    def body(x_ref, r_ref, gk_ref, gp_ref, h_ref, nk_ref, np_ref):
        h = x_ref[...] + r_ref[...]
        h_ref[...] = h
        hn = h * _rstd(h)
        nk_ref[...] = (hn * gk_ref[...]).astype(BF16)
        np_ref[...] = (hn * gp_ref[...]).astype(BF16)

    d = x.shape[1]
    return _rowcall(body, name, x.shape[0], [(x, "row"), (r0, "row"), (g_kv, "full"), (g_pre, "full")],
                    [(d, F32, "row"), (d, BF16, "row"), (d, BF16, "row")])


def _post_norm(o, g, name, out_kind="row"):
    def body(o_ref, g_ref, r_ref):
        o = o_ref[...]
        r_ref[...] = o * _rstd(o) * g_ref[...]

    return _rowcall(body, name, o.shape[0], [(o, "row"), (g, "full")], [(o.shape[1], F32, out_kind)])[0]


def _post_norm_loss(o, g, h1, target, name):
    d = o.shape[1]

    def body(o_ref, g_ref, h_ref, t_ref, dh_ref, acc_ref):
        o = o_ref[...]
        e = h_ref[...] + o * _rstd(o) * g_ref[...] - t_ref[...]
        dh_ref[...] = e * (1.0 / d)
        acc_ref[...] += jnp.sum(e * e, axis=0, keepdims=True)

    return _rowcall(body, name, o.shape[0], [(o, "row"), (g, "full"), (h1, "row"), (target, "row")],
                    [(d, F32, "row"), (d, F32, "acc")])


def _post_norm_bwd(dy, o, g, name, dy_kind="row"):
    def body(dy_ref, o_ref, g_ref, do_ref, dg_ref):
        dx, dg = _rms_bwd(o_ref[...], g_ref[...], dy_ref[...])
        do_ref[...] = dx.astype(BF16)
        dg_ref[...] += dg

    d = o.shape[1]
    return _rowcall(body, name, o.shape[0], [(dy, dy_kind), (o, "row"), (g, "full")], [(d, BF16, "row"), (d, F32, "acc")])


def _gate_mul(o, z, name):
    def body(o_ref, z_ref, r_ref):
        r_ref[...] = (o_ref[...] * _silu(z_ref[...])).astype(BF16)

    return _rowcall(body, name, o.shape[0], [(o, "row"), (z, "row")], [(o.shape[1], BF16, "row")])[0]


def _gate_bwd(d_oz, o, z, name):
    def body(d_ref, o_ref, z_ref, do_ref, dz_ref):
        _, vjp = jax.vjp(lambda o, z: o * _silu(z), o_ref[...], z_ref[...])
        do, dz = vjp(d_ref[...])
        do_ref[...] = do.astype(BF16)
        dz_ref[...] = dz.astype(BF16)

    w = o.shape[1]
    return _rowcall(body, name, o.shape[0], [(d_oz, "row"), (o, "row"), (z, "row")], [(w, BF16, "row"), (w, BF16, "row")])


def _norm_bwd2(dh2, h1, dxn1, dhn_kv, g_pre, g_kv, name):
    def body(dh2_ref, h_ref, d1_ref, dk_ref, gp_ref, gk_ref, dh1_ref, dgp_ref, dgk_ref):
        h = h_ref[...]
        dx1, dg1 = _rms_bwd(h, gp_ref[...], d1_ref[...])
        dxk, dgk = _rms_bwd(h, gk_ref[...], dk_ref[...])
        dh1_ref[...] = dh2_ref[...] + dx1 + dxk
        dgp_ref[...] += dg1
        dgk_ref[...] += dgk

    d = h1.shape[1]
    return _rowcall(body, name, h1.shape[0],
                    [(dh2, "row"), (h1, "row"), (dxn1, "row"), (dhn_kv, "row"), (g_pre, "full"), (g_kv, "full")],
                    [(d, F32, "row"), (d, F32, "acc"), (d, F32, "acc")])


def _norm_bwd1(dres, x, dxn, g, name):
    def body(dr_ref, x_ref, dn_ref, g_ref, dx_ref, dg_ref):
        dx, dg = _rms_bwd(x_ref[...], g_ref[...], dn_ref[...])
        dx_ref[...] = dr_ref[...] + dx
        dg_ref[...] += dg

    d = x.shape[1]
    return _rowcall(body, name, x.shape[0], [(dres, "nat"), (x, "nat"), (dxn, "row"), (g, "full")],
                    [(d, F32, "nat"), (d, F32, "acc")])


def _gelu_cast(y, name):
    def body(y_ref, o_ref):
        o_ref[...] = jax.nn.gelu(y_ref[...]).astype(BF16)

    return _rowcall(body, name, y.shape[0], [(y, "row")], [(y.shape[1], BF16, "row")])[0]


def _s5_gate(y_ssm, gp, b_glu, z, name):
    def body(y_ref, gp_ref, b_ref, z_ref, o_ref):
        yg = jax.nn.gelu(y_ref[...])
        o_ref[...] = (yg * jax.nn.sigmoid(gp_ref[...] + b_ref[...]) * _silu(z_ref[...])).astype(BF16)

    return _rowcall(body, name, y_ssm.shape[0], [(y_ssm, "row"), (gp, "row"), (b_glu, "full"), (z, "row")],
                    [(y_ssm.shape[1], BF16, "row")])[0]


def _s5_gate_bwd(dy3, y_ssm, gp, b_glu, z, name):
    def body(d_ref, y_ref, gp_ref, b_ref, z_ref, dz_ref, dgp_ref, dyg_ref, db_ref):
        yg = jax.nn.gelu(y_ref[...])
        _, vjp = jax.vjp(lambda yg, gp, z: yg * jax.nn.sigmoid(gp) * _silu(z), yg, gp_ref[...] + b_ref[...], z_ref[...])
        dyg, dgp, dz = vjp(d_ref[...])
        dz_ref[...] = dz.astype(BF16)
        dgp_ref[...] = dgp.astype(BF16)
        dyg_ref[...] = dyg
        db_ref[...] += jnp.sum(dgp, axis=0, keepdims=True)

    w = y_ssm.shape[1]
    return _rowcall(body, name, y_ssm.shape[0],
                    [(dy3, "row"), (y_ssm, "row"), (gp, "row"), (b_glu, "full"), (z, "row")],
                    [(w, BF16, "row"), (w, BF16, "row"), (w, F32, "row"), (w, F32, "acc")])


def _gelu_bwd(dyg, y_ssm, name):
    def body(d_ref, y_ref, o_ref):
        _, vjp = jax.vjp(jax.nn.gelu, y_ref[...])
        o_ref[...] = vjp(d_ref[...])[0]

    return _rowcall(body, name, y_ssm.shape[0], [(dyg, "row"), (y_ssm, "row")], [(y_ssm.shape[1], F32, "row")])[0]


def _concat_cast(a, b, name):
    def body(a_ref, b_ref, o_ref):
        w = a_ref.shape[1]
        o_ref[:, :w] = a_ref[...].astype(BF16)
        o_ref[:, w:] = b_ref[...].astype(BF16)

    return _rowcall(body, name, a.shape[0], [(a, "row"), (b, "row")], [(a.shape[1] + b.shape[1], BF16, "row")])[0]


def _disc(ar, ai, ldt):
    dt = jnp.exp(ldt)
    mag = jnp.exp(ar * dt)
    abr = mag * jnp.cos(ai * dt)
    abi = mag * jnp.sin(ai * dt)
    den = ar * ar + ai * ai
    nr = abr - 1.0
    return abr, abi, (nr * ar + abi * ai) / den, (abi * ar - nr * ai) / den


def _s5_disc_fwd(a_re, a_im, ldt):
    def body(ar, ai, ld, o1, o2, o3, o4):
        o1[...], o2[...], o3[...], o4[...] = _disc(ar[...], ai[...], ld[...])

    sh = jax.ShapeDtypeStruct(a_re.shape, F32)
    return pl.pallas_call(body, name="s5_disc_fwd", out_shape=(sh, sh, sh, sh))(a_re, a_im, ldt)


def _s5_disc_bwd(a_re, a_im, ldt, d_abr, d_abi, d_cr, d_ci):
    def body(ar, ai, ld, g1, g2, g3, g4, o1, o2, o3):
        _, vjp = jax.vjp(_disc, ar[...], ai[...], ld[...])
        o1[...], o2[...], o3[...] = vjp((g1[...], g2[...], g3[...], g4[...]))

    sh = jax.ShapeDtypeStruct(a_re.shape, F32)
    return pl.pallas_call(body, name="s5_disc_bwd", out_shape=(sh, sh, jax.ShapeDtypeStruct(ldt.shape, F32)))(
        a_re, a_im, ldt, d_abr, d_abi, d_cr, d_ci)


def _bbar(cr, ci, br, bi):
    return cr * br - ci * bi, cr * bi + ci * br


def _s5_bbar_fwd(cr_col, ci_col, b_re, b_im):
    def body(cr, ci, br, bi, o1, o2):
        o1[...], o2[...] = _bbar(cr[...], ci[...], br[...], bi[...])

    w = b_re.shape[1]
    return _rowcall(body, "s5_bbar_fwd", b_re.shape[0], [(cr_col, "row"), (ci_col, "row"), (b_re, "row"), (b_im, "row")],
                    [(w, F32, "row"), (w, F32, "row")], tile_rows=1024)


def _s5_bbar_bwd(cr_col, ci_col, b_re, b_im, d_re, d_im):
    def body(cr, ci, br, bi, g1, g2, o1, o2, o3, o4):
        _, vjp = jax.vjp(_bbar, cr[...], ci[...], br[...], bi[...])
        o1[...], o2[...], o3[...], o4[...] = vjp((g1[...], g2[...]))

    w = b_re.shape[1]
    return _rowcall(body, "s5_bbar_bwd", b_re.shape[0],
                    [(cr_col, "row"), (ci_col, "row"), (b_re, "row"), (b_im, "row"), (d_re, "row"), (d_im, "row")],
                    [(1, F32, "row"), (1, F32, "row"), (w, F32, "row"), (w, F32, "row")], tile_rows=1024)


def _block_diag_in(t):
    g, p, c = t.shape
    nb = g // GROUPS_PER_BLOCK
    t4 = t.reshape(nb, GROUPS_PER_BLOCK, p, c).transpose(0, 1, 3, 2)
    eye = jnp.eye(GROUPS_PER_BLOCK, dtype=t.dtype)
    return (t4[:, :, :, None, :] * eye[None, :, None, :, None]).reshape(nb, GROUPS_PER_BLOCK * c, GROUPS_PER_BLOCK * p)


def _block_diag_in_extract(d, p, c):
    nb = d.shape[0]
    d5 = d.reshape(nb, GROUPS_PER_BLOCK, c, GROUPS_PER_BLOCK, p)
    diag = jnp.stack([d5[:, g, :, g, :] for g in range(GROUPS_PER_BLOCK)], axis=1)
    return diag.transpose(0, 1, 3, 2).reshape(nb * GROUPS_PER_BLOCK, p, c)


def _block_diag_out(t):
    g, c, p = t.shape
    nb = g // GROUPS_PER_BLOCK
    t4 = t.reshape(nb, GROUPS_PER_BLOCK, c, p).transpose(0, 1, 3, 2)
    eye = jnp.eye(GROUPS_PER_BLOCK, dtype=t.dtype)
    return (t4[:, :, :, None, :] * eye[None, :, None, :, None]).reshape(nb, GROUPS_PER_BLOCK * p, GROUPS_PER_BLOCK * c)


def _block_diag_out_extract(d, c, p):
    nb = d.shape[0]
    d5 = d.reshape(nb, GROUPS_PER_BLOCK, p, GROUPS_PER_BLOCK, c)
    diag = jnp.stack([d5[:, g, :, g, :] for g in range(GROUPS_PER_BLOCK)], axis=1)
    return diag.transpose(0, 1, 3, 2).reshape(nb * GROUPS_PER_BLOCK, c, p)


def _scan_step(ar, ai, hr, hi, xr, xi):
    return ar * hr - ai * hi + xr, ar * hi + ai * hr + xi


def _s5_scan_fwd(u, bd_re, bd_im, cd_re, cd_im, ab_re, ab_im, init_re, init_im, d_row, full, name):
    s, w = u.shape
    nb = w // LANES
    rows = _tile(s, 512, SUBLANES)
    nc = s // rows
    steps = rows // N_SEG
    ns = nb * BLOCK_STATE

    def body(u_ref, bdr, bdi, cdr, cdi, ar_ref, ai_ref, ir_ref, ii_ref, d_ref, *outs):
        if full:
            y_ref, hr_ref, hi_ref, er_ref, ei_ref, cr, ci = outs
        else:
            er_ref, ei_ref, hr_ref, hi_ref, cr, ci = outs
        c = pl.program_id(1)

        @pl.when(c == 0)
        def _():
            cr[...] = ir_ref[...]
            ci[...] = ii_ref[...]

        ub = u_ref[...].astype(BF16)
        hr_ref[...] = jnp.dot(ub, bdr[...], preferred_element_type=F32)
        hi_ref[...] = jnp.dot(ub, bdi[...], preferred_element_type=F32)
        ar, ai = ar_ref[...], ai_ref[...]

        def step(j, carry):
            off = pl.multiple_of(j * N_SEG, N_SEG)
            nr, ni = _scan_step(ar, ai, carry[0], carry[1], hr_ref[pl.ds(off, N_SEG), :], hi_ref[pl.ds(off, N_SEG), :])
            hr_ref[pl.ds(off, N_SEG), :] = nr
            hi_ref[pl.ds(off, N_SEG), :] = ni
            return nr, ni

        hr, hi = lax.fori_loop(0, steps, step, (cr[...], ci[...]), unroll=8)
        cr[...] = hr
        ci[...] = hi
        if full:
            y_ref[...] = (jnp.dot(hr_ref[...].astype(BF16), cdr[...], preferred_element_type=F32)
                          + jnp.dot(hi_ref[...].astype(BF16), cdi[...], preferred_element_type=F32)
                          + d_ref[...] * u_ref[...])

        @pl.when(c == nc - 1)
        def _():
            er_ref[...] = hr
            ei_ref[...] = hi

    blk3 = lambda a: pl.BlockSpec((None,) + a.shape[1:], lambda k, c: (k, 0, 0))
    seg = pl.BlockSpec((N_SEG, BLOCK_STATE), lambda k, c: (0, k))
    st = pl.BlockSpec((rows, BLOCK_STATE), lambda k, c: (c, k))
    in_specs = [pl.BlockSpec((rows, LANES), lambda k, c: (c, k)), blk3(bd_re), blk3(bd_im), blk3(cd_re), blk3(cd_im),
                seg, seg, seg, seg, pl.BlockSpec((1, LANES), lambda k, c: (0, k))]
    seg_shape = jax.ShapeDtypeStruct((N_SEG, ns), F32)
    st_shape = jax.ShapeDtypeStruct((s, ns), F32)
    carry = [pltpu.VMEM((N_SEG, BLOCK_STATE), F32)] * 2
    if full:
        out_specs = [pl.BlockSpec((rows, LANES), lambda k, c: (c, k)), st, st, seg, seg]
        out_shape = [jax.ShapeDtypeStruct((s, w), F32), st_shape, st_shape, seg_shape, seg_shape]
        scratch = carry
    else:
        out_specs = [seg, seg]
        out_shape = [seg_shape, seg_shape]
        scratch = [pltpu.VMEM((rows, BLOCK_STATE), F32)] * 2 + carry
    return pl.pallas_call(
        body, name=name, grid=(nb, nc), in_specs=in_specs, out_specs=out_specs, out_shape=out_shape,
        scratch_shapes=scratch, compiler_params=_cparams("parallel", "arbitrary"),
    )(u, bd_re, bd_im, cd_re, cd_im, ab_re, ab_im, init_re, init_im, d_row)


def _s5_seg_fix(e_re, e_im, ab_re, ab_im, seg_len, reverse, name):
    assert seg_len & (seg_len - 1) == 0

    def body(er, ei, ar, ai, o_re, o_im):
        pr, pi = ar[0:1, :], ai[0:1, :]
        for _ in range(int(math.log2(seg_len))):
            pr, pi = pr * pr - pi * pi, 2.0 * pr * pi
        tr = jnp.zeros_like(pr)
        ti = jnp.zeros_like(pr)
        order = list(range(N_SEG - 1, -1, -1)) if reverse else list(range(N_SEG))
        for n, sgm in enumerate(order):
            o_re[sgm:sgm + 1, :] = tr
            o_im[sgm:sgm + 1, :] = ti
            if n < N_SEG - 1:
                tr, ti = _scan_step(pr, pi, tr, ti, er[sgm:sgm + 1, :], ei[sgm:sgm + 1, :])

    sh = jax.ShapeDtypeStruct(e_re.shape, F32)
    return pl.pallas_call(body, name=name, out_shape=(sh, sh))(e_re, e_im, ab_re, ab_im)


def _s5_scan_bwd(dy, u, h_re, h_im, bd_re, bd_im, cd_re, cd_im, ab_re, ab_imn, gin_re, gin_im, d_row, full, name):
    s, w = u.shape
    nb = w // LANES
    rows = _tile(s, 512, SUBLANES)
    nc = s // rows
    steps = rows // N_SEG
    ns = nb * BLOCK_STATE

    def body(dy_ref, u_ref, hr_ref, hi_ref, bdr, bdi, cdr, cdi, ar_ref, ai_ref, ir_ref, ii_ref, d_ref, *outs):
        if full:
            du_ref, dbr_ref, dbi_ref, dcr_ref, dci_ref, dar_ref, dai_ref, dd_ref, gr, gi, accr, acci = outs
        else:
            er_ref, ei_ref, gr, gi = outs
        c = pl.program_id(1)

        @pl.when(c == 0)
        def _():
            gr[pl.ds(rows, N_SEG), :] = ir_ref[...]
            gi[pl.ds(rows, N_SEG), :] = ii_ref[...]
            if full:
                for r in (dbr_ref, dbi_ref, dcr_ref, dci_ref, dd_ref, accr, acci):
                    r[...] = jnp.zeros_like(r)

        dyb = dy_ref[...].astype(BF16)
        nt = (_DOT_DIMS["nt"], ((), ()))
        tn = (_DOT_DIMS["tn"], ((), ()))
        gr[pl.ds(0, rows), :] = lax.dot_general(dyb, cdr[...], nt, preferred_element_type=F32)
        gi[pl.ds(0, rows), :] = lax.dot_general(dyb, cdi[...], nt, preferred_element_type=F32)
        ar, ai = ar_ref[...], ai_ref[...]

        def step(jj, carry):
            off = pl.multiple_of((steps - 1 - jj) * N_SEG, N_SEG)
            nr, ni = _scan_step(ar, ai, carry[0], carry[1], gr[pl.ds(off, N_SEG), :], gi[pl.ds(off, N_SEG), :])
            gr[pl.ds(off, N_SEG), :] = nr
            gi[pl.ds(off, N_SEG), :] = ni
            return nr, ni

        g0r, g0i = lax.fori_loop(0, steps, step, (gr[pl.ds(rows, N_SEG), :], gi[pl.ds(rows, N_SEG), :]), unroll=8)
        if full:
            hr, hi = hr_ref[...], hi_ref[...]
            gnr, gni = gr[pl.ds(N_SEG, rows), :], gi[pl.ds(N_SEG, rows), :]
            accr[...] += jnp.sum((gnr * hr + gni * hi).reshape(steps, N_SEG, BLOCK_STATE), axis=0)
            acci[...] += jnp.sum((gni * hr - gnr * hi).reshape(steps, N_SEG, BLOCK_STATE), axis=0)
        gr[pl.ds(rows, N_SEG), :] = g0r
        gi[pl.ds(rows, N_SEG), :] = g0i
        if full:
            ub = u_ref[...].astype(BF16)
            gbr, gbi = gr[pl.ds(0, rows), :].astype(BF16), gi[pl.ds(0, rows), :].astype(BF16)
            dcr_ref[...] += lax.dot_general(hr.astype(BF16), dyb, tn, preferred_element_type=F32)
            dci_ref[...] += lax.dot_general(hi.astype(BF16), dyb, tn, preferred_element_type=F32)
            dbr_ref[...] += lax.dot_general(ub, gbr, tn, preferred_element_type=F32)
            dbi_ref[...] += lax.dot_general(ub, gbi, tn, preferred_element_type=F32)
            du_ref[...] = (lax.dot_general(gbr, bdr[...], nt, preferred_element_type=F32)
                           + lax.dot_general(gbi, bdi[...], nt, preferred_element_type=F32)
                           + d_ref[...] * dy_ref[...])
            dd_ref[...] += jnp.sum(dy_ref[...] * u_ref[...], axis=0, keepdims=True)

        @pl.when(c == nc - 1)
        def _():
            if full:
                dar_ref[...] = jnp.sum(accr[...], axis=0, keepdims=True)
                dai_ref[...] = jnp.sum(acci[...], axis=0, keepdims=True)
            else:
                er_ref[...] = g0r
                ei_ref[...] = g0i

    rev = lambda k, c: (nc - 1 - c, k)
    blk3 = lambda a: pl.BlockSpec((None,) + a.shape[1:], lambda k, c: (k, 0, 0))
    seg = pl.BlockSpec((N_SEG, BLOCK_STATE), lambda k, c: (0, k))
    st = pl.BlockSpec((rows, BLOCK_STATE), rev)
    ch = pl.BlockSpec((rows, LANES), rev)
    vec = pl.BlockSpec((1, LANES), lambda k, c: (0, k))
    if not full:
        st = pl.BlockSpec((rows, BLOCK_STATE), lambda k, c: (0, k))
    in_specs = [ch, ch if full else pl.BlockSpec((rows, LANES), lambda k, c: (0, k)), st, st,
                blk3(bd_re), blk3(bd_im), blk3(cd_re), blk3(cd_im), seg, seg, seg, seg, vec]
    gbuf =[pltpu.VMEM((rows + N_SEG, BLOCK_STATE), F32)] * 2
    if full:
        row1 = pl.BlockSpec((1, BLOCK_STATE), lambda k, c: (0, k))
        out_specs = [ch, blk3(bd_re), blk3(bd_im), blk3(cd_re), blk3(cd_im), row1, row1, vec]
        out_shape = [jax.ShapeDtypeStruct((s, w), F32),
                     jax.ShapeDtypeStruct(bd_re.shape, F32), jax.ShapeDtypeStruct(bd_im.shape, F32),
                     jax.ShapeDtypeStruct(cd_re.shape, F32), jax.ShapeDtypeStruct(cd_im.shape, F32),
                     jax.ShapeDtypeStruct((1, ns), F32), jax.ShapeDtypeStruct((1, ns), F32),
                     jax.ShapeDtypeStruct((1, w), F32)]
        scratch = gbuf + [pltpu.VMEM((N_SEG, BLOCK_STATE), F32)] * 2
    else:
        out_specs = [seg, seg]
        out_shape = [jax.ShapeDtypeStruct((N_SEG, ns), F32)] * 2
        scratch = gbuf
    return pl.pallas_call(
        body, name=name, grid=(nb, nc), in_specs=in_specs, out_specs=out_specs, out_shape=out_shape,
        scratch_shapes=scratch, compiler_params=_cparams("parallel", "arbitrary"),
    )(dy, u, h_re, h_im, bd_re, bd_im, cd_re, cd_im, ab_re, ab_imn, gin_re, gin_im, d_row)


def _log_sigmoid(x):
    return jnp.minimum(x, 0.0) - jnp.log(1.0 + jnp.exp(-jnp.abs(x)))


def _tri(n, upper):
    r = lax.broadcasted_iota(jnp.int32, (n, n), 0)
    c = lax.broadcasted_iota(jnp.int32, (n, n), 1)
    return jnp.where((c >= r) if upper else (r >= c), 1.0, 0.0).astype(F32)


def _cum_fwd(f_logit, b_row, name):
    s, w = f_logit.shape
    t = _tile(s, 256, SUBLANES)

    def body(f_ref, b_ref, o_ref, carry):
        @pl.when(pl.program_id(0) == 0)
        def _():
            carry[...] = jnp.zeros_like(carry)

        lf = _log_sigmoid(f_ref[...] + b_ref[...])
        cum = jnp.dot(_tri(t, False), lf, precision=lax.Precision.HIGHEST, preferred_element_type=F32) + carry[...]
        o_ref[...] = cum * LOG2E
        carry[...] = cum[t - 1:t, :]

    return pl.pallas_call(
        body, name=name, grid=(s // t,),
        in_specs=[pl.BlockSpec((t, w), lambda i: (i, 0)), pl.BlockSpec((1, w), lambda i: (0, 0))],
        out_specs=pl.BlockSpec((t, w), lambda i: (i, 0)), out_shape=jax.ShapeDtypeStruct((s, w), F32),
        scratch_shapes=[pltpu.VMEM((1, w), F32)], compiler_params=_cparams("arbitrary"),
    )(f_logit, b_row)


def _cum_bwd(dcq, dck, f_logit, b_row, name):
    s, w = f_logit.shape
    t = _tile(s, 256, SUBLANES)
    nt = s // t

    def body(q_ref, k_ref, f_ref, b_ref, df_ref, db_ref, carry):
        @pl.when(pl.program_id(0) == 0)
        def _():
            carry[...] = jnp.zeros_like(carry)
            db_ref[...] = jnp.zeros_like(db_ref)

        dc = q_ref[...] - k_ref[...]
        rc = jnp.dot(_tri(t, True), dc, precision=lax.Precision.HIGHEST, preferred_element_type=F32) + carry[...]
        carry[...] = rc[0:1, :]
        df = rc * (1.0 - jax.nn.sigmoid(f_ref[...] + b_ref[...]))
        df_ref[...] = df.astype(BF16)
        db_ref[...] += jnp.sum(df, axis=0, keepdims=True)

    rev = pl.BlockSpec((t, w), lambda i: (nt - 1 - i, 0))
    one = pl.BlockSpec((1, w), lambda i: (0, 0))
    return pl.pallas_call(
        body, name=name, grid=(nt,), in_specs=[rev, rev, rev, one], out_specs=[rev, one],
        out_shape=[jax.ShapeDtypeStruct((s, w), BF16), jax.ShapeDtypeStruct((1, w), F32)],
        scratch_shapes=[pltpu.VMEM((1, w), F32)], compiler_params=_cparams("arbitrary"),
    )(dcq, dck, f_logit, b_row)


def _head_col(cum_tile, h):
    lane = lax.broadcasted_iota(jnp.int32, cum_tile.shape, 1)
    return jnp.sum(jnp.where(lane == h, cum_tile, 0.0), axis=1, keepdims=True)


def _attn_tiles(s):
    return _tile(s, 512, LANES)


def _exp2_rows(sc, sub):
    return jnp.concatenate([jnp.exp2(sc[:, b * LANES:(b + 1) * LANES] - sub) for b in range(sc.shape[1] // LANES)], axis=1)


def _row_of(rep):
    return jnp.transpose(rep)[0:1, :]


def _causal(sc, keys_on_rows):
    r = lax.broadcasted_iota(jnp.int32, sc.shape, 0)
    c = lax.broadcasted_iota(jnp.int32, sc.shape, 1)
    return jnp.where((r <= c) if keys_on_rows else (c <= r), sc, NEG_INF)


def _fox_fwd(q2, kv, cum2_t, name):
    s, w = q2.shape
    nh = w // HEAD_DIM
    tq = _attn_tiles(s)
    nq = s // tq
    nt = (_DOT_DIMS["nt"], ((), ()))

    def body(q_ref, k_ref, v_ref, ct_ref, o_ref, lse_ref, lse_row_ref, m_s, acc_s, vaug, s_buf):
        i = pl.program_id(1)

        @pl.when(i == 0)
        def _():
            vaug[:, :HEAD_DIM] = v_ref[...]
            vaug[:, HEAD_DIM:] = jnp.ones((s, LANES), BF16)

        qb = q_ref[...]
        m_s[...] = jnp.full_like(m_s, NEG_INF)
        acc_s[...] = jnp.zeros_like(acc_s)

        def scores(j):
            off = pl.multiple_of(j * tq, tq)
            return lax.dot_general(qb, k_ref[pl.ds(off, tq), :], nt, preferred_element_type=F32) - ct_ref[:, pl.ds(off, tq)]

        def softmax_pv(j, sc):
            m_old = m_s[...]
            m_new = jnp.maximum(m_old, jnp.max(sc, axis=1, keepdims=True))
            p = _exp2_rows(sc, m_new)
            alpha = jnp.exp2(m_old - m_new)
            pv = jnp.dot(p.astype(BF16), vaug[pl.ds(pl.multiple_of(j * tq, tq), tq), :], preferred_element_type=F32)
            acc_s[...] = jnp.concatenate([alpha, alpha], axis=1) * acc_s[...] + pv
            m_s[...] = m_new

        s_buf[...] = scores(0)

        def loop(j, carry):
            nxt = scores(j + 1)
            softmax_pv(j, s_buf[...])
            s_buf[...] = nxt
            return carry

        lax.fori_loop(0, i, loop, 0)
        softmax_pv(i, _causal(s_buf[...], False))
        l = acc_s[:, HEAD_DIM:]
        o_ref[...] = acc_s[:, :HEAD_DIM] / l
        lse = m_s[...] + jnp.log(l) * LOG2E
        lse_ref[...] = lse
        lse_row_ref[...] = _row_of(lse)

    return pl.pallas_call(
        body, name=name, grid=(nh, nq),
        in_specs=[pl.BlockSpec((tq, HEAD_DIM), lambda h, i: (i, h)),
                  pl.BlockSpec((s, HEAD_DIM), lambda h, i: (0, h)),
                  pl.BlockSpec((s, HEAD_DIM), lambda h, i: (0, nh + h)),
                  pl.BlockSpec((None, 1, s), lambda h, i: (h, 0, 0))],
        out_specs=[pl.BlockSpec((tq, HEAD_DIM), lambda h, i: (i, h)),
                   pl.BlockSpec((None, tq, LANES), lambda h, i: (h, i, 0)),
                   pl.BlockSpec((None, 1, tq), lambda h, i: (h, 0, i))],
        out_shape=[jax.ShapeDtypeStruct((s, w), F32), jax.ShapeDtypeStruct((nh, s, LANES), F32),
                   jax.ShapeDtypeStruct((nh, 1, s), F32)],
        scratch_shapes=[pltpu.VMEM((tq, LANES), F32), pltpu.VMEM((tq, HEAD_DIM + LANES), F32),
                        pltpu.VMEM((s, HEAD_DIM + LANES), BF16), pltpu.VMEM((tq, tq), F32)],
        compiler_params=_cparams("arbitrary", "arbitrary"),
    )(q2, kv, kv, cum2_t)


def _fox_bwd_dq(q2, kv, do, o, lse2, cum2_t, name):
    s, w = q2.shape
    nh = w // HEAD_DIM
    tq = _attn_tiles(s)
    nq = s // tq
    scale = HEAD_DIM ** -0.5
    nt = (_DOT_DIMS["nt"], ((), ()))

    def body(q_ref, k_ref, v_ref, do_ref, o_ref, lse_ref, ct_ref, dq_ref, dl_ref, dcq_ref, acc_s, dc_s):
        i = pl.program_id(1)
        qb = q_ref[...]
        dob = do_ref[...]
        lse = lse_ref[...]
        delta = jnp.broadcast_to(jnp.sum(dob.astype(F32) * o_ref[...], axis=1, keepdims=True), (tq, LANES))
        acc_s[...] = jnp.zeros_like(acc_s)
        dc_s[...] = jnp.zeros_like(dc_s)

        def tile(j, masked):
            off = pl.multiple_of(j * tq, tq)
            kb = k_ref[pl.ds(off, tq), :]
            sc = lax.dot_general(qb, kb, nt, preferred_element_type=F32) - ct_ref[:, pl.ds(off, tq)]
            if masked:
                sc = _causal(sc, False)
            p = _exp2_rows(sc, lse)
            dp = lax.dot_general(dob, v_ref[pl.ds(off, tq), :], nt, preferred_element_type=F32)
            ds = p * (dp - jnp.concatenate([delta] * (tq // LANES), axis=1))
            acc_s[...] += jnp.dot(ds.astype(BF16), kb, preferred_element_type=F32)
            part = ds[:, :LANES]
            for b in range(1, tq // LANES):
                part = part + ds[:, b * LANES:(b + 1) * LANES]
            dc_s[...] += part

        def loop(j, carry):
            tile(j, False)
            return carry

        lax.fori_loop(0, i, loop, 0)
        tile(i, True)
        dq_ref[...] = (acc_s[...] * scale).astype(BF16)
        dl_ref[...] = _row_of(delta)
        dcq_ref[...] = jnp.sum(jnp.transpose(dc_s[...]), axis=0, keepdims=True)

    qspec = pl.BlockSpec((tq, HEAD_DIM), lambda h, i: (i, h))
    rep = pl.BlockSpec((None, tq, LANES), lambda h, i: (h, i, 0))
    rowspec = pl.BlockSpec((None, 1, tq), lambda h, i: (h, 0, i))
    return pl.pallas_call(
        body, name=name, grid=(nh, nq),
        in_specs=[qspec,
                  pl.BlockSpec((s, HEAD_DIM), lambda h, i: (0, h)),
                  pl.BlockSpec((s, HEAD_DIM), lambda h, i: (0, nh + h)),
                  qspec, qspec, rep,
                  pl.BlockSpec((None, 1, s), lambda h, i: (h, 0, 0))],
        out_specs=[qspec, rowspec, rowspec],
        out_shape=[jax.ShapeDtypeStruct((s, w), BF16), jax.ShapeDtypeStruct((nh, 1, s), F32),
                   jax.ShapeDtypeStruct((nh, 1, s), F32)],
        scratch_shapes=[pltpu.VMEM((tq, HEAD_DIM), F32), pltpu.VMEM((tq, LANES), F32)],
        compiler_params=_cparams("parallel", "arbitrary"),
    )(q2, kv, kv, do, o, lse2, cum2_t)


def _fox_bwd_dkv(q2, kv, do, lse2_t, delta_t, cum2, name):
    s, w = q2.shape
    nh = w // HEAD_DIM
    tk = _attn_tiles(s)
    nk = s // tk
    nt = (_DOT_DIMS["nt"], ((), ()))

    def body(q_ref, k_ref, v_ref, do_ref, lse_ref, dl_ref, c_ref, dk_ref, dv_ref, dck_ref, dk_s, dv_s, dc_s, s_buf, dp_buf):
        h, j = pl.program_id(0), pl.program_id(1)
        kb = k_ref[...]
        vb = v_ref[...]
        ck = jnp.broadcast_to(_head_col(c_ref[...], h), (tk, LANES))
        dk_s[...] = jnp.zeros_like(dk_s)
        dv_s[...] = jnp.zeros_like(dv_s)
        dc_s[...] = jnp.zeros_like(dc_s)

        def scores(i):
            off = pl.multiple_of(i * tk, tk)
            sc = lax.dot_general(kb, q_ref[pl.ds(off, tk), :], nt, preferred_element_type=F32) - lse_ref[:, pl.ds(off, tk)]
            dp = lax.dot_general(vb, do_ref[pl.ds(off, tk), :], nt, preferred_element_type=F32) - dl_ref[:, pl.ds(off, tk)]
            return sc, dp

        def accumulate(i, sc, dp):
            off = pl.multiple_of(i * tk, tk)
            p = _exp2_rows(sc, ck)
            dv_s[...] += jnp.dot(p.astype(BF16), do_ref[pl.ds(off, tk), :], preferred_element_type=F32)
            ds = p * dp
            dk_s[...] += jnp.dot(ds.astype(BF16), q_ref[pl.ds(off, tk), :], preferred_element_type=F32)
            part = ds[:, :LANES]
            for b in range(1, tk // LANES):
                part = part + ds[:, b * LANES:(b + 1) * LANES]
            dc_s[...] += part

        sc0, dp0 = scores(j)
        s_buf[...] = _causal(sc0, True)
        dp_buf[...] = dp0

        def loop(i, carry):
            nxt = scores(i + 1)
            accumulate(i, s_buf[...], dp_buf[...])
            s_buf[...], dp_buf[...] = nxt
            return carry

        lax.fori_loop(j, nk - 1, loop, 0)
        accumulate(nk - 1, s_buf[...], dp_buf[...])
        dk_ref[...] = (dk_s[...] * (1.0 / LOG2E)).astype(BF16)
        dv_ref[...] = dv_s[...].astype(BF16)
        dck_ref[...] = jnp.sum(jnp.transpose(dc_s[...]), axis=0, keepdims=True)

    col = pl.BlockSpec((s, HEAD_DIM), lambda h, j: (0, h))
    row = pl.BlockSpec((None, 1, s), lambda h, j: (h, 0, 0))
    kspec = pl.BlockSpec((tk, HEAD_DIM), lambda h, j: (j, h))
    return pl.pallas_call(
        body, name=name, grid=(nh, nk),
        in_specs=[col, kspec, pl.BlockSpec((tk, HEAD_DIM), lambda h, j: (j, nh + h)), col, row, row,
                  pl.BlockSpec((tk, LANES), lambda h, j: (j, 0))],
        out_specs=[kspec, kspec, pl.BlockSpec((None, 1, tk), lambda h, j: (h, 0, j))],
        out_shape=[jax.ShapeDtypeStruct((s, w), BF16), jax.ShapeDtypeStruct((s, w), BF16),
                   jax.ShapeDtypeStruct((nh, 1, s), F32)],
        scratch_shapes=[pltpu.VMEM((tk, HEAD_DIM), F32), pltpu.VMEM((tk, HEAD_DIM), F32),
                        pltpu.VMEM((tk, LANES), F32), pltpu.VMEM((tk, tk), F32), pltpu.VMEM((tk, tk), F32)],
        compiler_params=_cparams("parallel", "arbitrary"),
    )(q2, kv, kv, do, lse2_t, delta_t, cum2)


def _exchange_copies(ins, outs, send_sems, recv_sems, local_sems, scatter):
    x, y, c = (lax.axis_index(a) for a in MESH_AXES)
    me = 4 * x + 2 * y + c
    local, remote = [], []
    for a in range(len(ins)):
        local.append(pltpu.make_async_copy(ins[a].at[me] if scatter else ins[a], outs[a].at[me], local_sems.at[a]))
        for k in range(1, N_DEV):
            px, py, pc = (1 - x if k & 4 else x), (1 - y if k & 2 else y), (1 - c if k & 1 else c)
            remote.append(pltpu.make_async_remote_copy(
                src_ref=ins[a].at[4 * px + 2 * py + pc] if scatter else ins[a], dst_ref=outs[a].at[me],
                send_sem=send_sems.at[a * (N_DEV - 1) + k - 1], recv_sem=recv_sems.at[a * (N_DEV - 1) + k - 1],
                device_id=(px, py, pc), device_id_type=pl.DeviceIdType.MESH))
    return local, remote


def _exchange_out_shapes(arrs, scatter):
    return [((N_DEV,) + a.shape[1:]) if scatter else ((N_DEV,) + a.shape) for a in arrs]


def _exchange(arrs, scatter, name):
    n = len(arrs)

    def body(*refs):
        local, remote = _exchange_copies(refs[:n], refs[n:2 * n], *refs[2 * n:], scatter)
        for cp in local + remote:
            cp.start()
        for cp in remote:
            cp.wait_send()
            cp.wait_recv()
        for cp in local:
            cp.wait()

    out_shape = [jax.ShapeDtypeStruct(s, a.dtype) for s, a in zip(_exchange_out_shapes(arrs, scatter), arrs)]
    return pl.pallas_call(
        body, name=name, out_shape=out_shape,
        in_specs=[pl.BlockSpec(memory_space=pl.ANY)] * n, out_specs=[pl.BlockSpec(memory_space=pl.ANY)] * n,
        scratch_shapes=[pltpu.SemaphoreType.DMA((n * (N_DEV - 1),)), pltpu.SemaphoreType.DMA((n * (N_DEV - 1),)),
                        pltpu.SemaphoreType.DMA((n,))],
    )(*arrs)


_HBM = pl.BlockSpec(memory_space=pltpu.HBM)
_SEM = pl.BlockSpec(memory_space=pltpu.SEMAPHORE)


def _exchange_start(arrs, scatter, name):
    n = len(arrs)
    lands = [lax.empty(s, a.dtype) for s, a in zip(_exchange_out_shapes(arrs, scatter), arrs)]

    def body(*refs):
        ins, outs = refs[:n], refs[n:2 * n]
        send_sems, recv_sems, local_sems = refs[2 * n:2 * n + 3]
        token = refs[-1]
        local, remote = _exchange_copies(ins, outs, send_sems, recv_sems, local_sems, scatter)
        for cp in local + remote:
            cp.start()
        token[...] = jnp.zeros_like(token)

    hbm = lambda a: pltpu.HBM(a.shape, a.dtype)
    res = pl.pallas_call(
        body, name=name,
        out_shape=(pltpu.SemaphoreType.DMA((n * (N_DEV - 1),)), pltpu.SemaphoreType.DMA((n * (N_DEV - 1),)),
                   pltpu.SemaphoreType.DMA((n,)), *[hbm(a) for a in arrs], *[hbm(a) for a in lands],
                   jax.ShapeDtypeStruct((SUBLANES, LANES), F32)),
        in_specs=[_HBM] * (2 * n),
        out_specs=(_SEM, _SEM, _SEM, *[_HBM] * (2 * n), pl.BlockSpec(memory_space=pltpu.VMEM)),
        input_output_aliases={i: 3 + i for i in range(2 * n)},
        compiler_params=pltpu.CompilerParams(has_side_effects=pltpu.SideEffectType.DATAFLOW_SIDE_EFFECTING),
    )(*[pltpu.with_memory_space_constraint(a, pltpu.HBM) for a in list(arrs) + lands])
    return (n, scatter, res[:3], res[3:3 + n], res[3 + n:3 + 2 * n]), res[-1]


def _exchange_wait(state, after, name):
    n, scatter, sems, srcs, lands = state

    def body(*refs):
        ins, outs = refs[:n], refs[n:2 * n]
        send_sems, recv_sems, local_sems = refs[2 * n:2 * n + 3]
        local, remote = _exchange_copies(ins, outs, send_sems, recv_sems, local_sems, scatter)
        for cp in remote:
            cp.wait_send()
            cp.wait_recv()
        for cp in local:
            cp.wait()

    hbm = lambda a: pltpu.HBM(a.shape, a.dtype)
    res = pl.pallas_call(
        body, name=name,
        out_shape=(*[hbm(a) for a in srcs], *[hbm(a) for a in lands]),
        in_specs=[_HBM] * (2 * n) + [_SEM] * 3 + [pl.BlockSpec(memory_space=pl.ANY)],
        out_specs=tuple([_HBM] * (2 * n)),
        input_output_aliases={i: i for i in range(2 * n)},
        compiler_params=pltpu.CompilerParams(has_side_effects=pltpu.SideEffectType.DATAFLOW_SIDE_EFFECTING),
    )(*srcs, *lands, *sems, after)
    return list(res[n:])


def _adamw_math(w, g, m, v):
    m = ADAM_B1 * m + (1.0 - ADAM_B1) * g
    v = ADAM_B2 * v + (1.0 - ADAM_B2) * (g * g)
    m_hat = m / (1.0 - ADAM_B1 ** ADAM_STEP)
    v_hat = v / (1.0 - ADAM_B2 ** ADAM_STEP)
    return -ADAM_LR * (m_hat / (jnp.sqrt(v_hat) + ADAM_EPS) + ADAM_WD * w), m, v


def _slot_sum(p_ref):
    g = p_ref[0].astype(F32)
    for d in range(1, p_ref.shape[0]):
        g = g + p_ref[d].astype(F32)
    return g


def _adamw_tile(r, c):
    return _tile(r, max(SUBLANES, (256 * 1024) // c // SUBLANES * SUBLANES), SUBLANES)


def _adamw(parts, w, m, v, name):
    r, c = w.shape
    tr = _adamw_tile(r, c)

    def body(p_ref, w_ref, m_ref, v_ref, g_ref, d_ref, nm_ref, nv_ref):
        g = _slot_sum(p_ref)
        g_ref[...] = g
        d_ref[...], nm_ref[...], nv_ref[...] = _adamw_math(w_ref[...], g, m_ref[...], v_ref[...])

    blk = pl.BlockSpec((tr, c), lambda i: (i, 0))
    sh = jax.ShapeDtypeStruct((r, c), F32)
    return pl.pallas_call(
        body, name=name, grid=(r // tr,),
        in_specs=[pl.BlockSpec((parts.shape[0], tr, c), lambda i: (0, i, 0)), blk, blk, blk],
        out_specs=[blk] * 4, out_shape=[sh] * 4, compiler_params=_cparams("parallel"),
    )(parts, w, m, v)


def _sum_parts(parts, name):
    _, r, c = parts.shape
    tr = _adamw_tile(r, c)

    def body(p_ref, o_ref):
        o_ref[...] = _slot_sum(p_ref)

    return pl.pallas_call(
        body, name=name, grid=(r // tr,),
        in_specs=[pl.BlockSpec((parts.shape[0], tr, c), lambda i: (0, i, 0))],
        out_specs=pl.BlockSpec((tr, c), lambda i: (i, 0)), out_shape=jax.ShapeDtypeStruct((r, c), F32),
        compiler_params=_cparams("parallel"),
    )(parts)


def _perm(a):
    s, d = a.shape
    return a.reshape(N_SEG, s // N_SEG, d).transpose(1, 0, 2).reshape(s, d)


def _unperm(a):
    s, d = a.shape
    return a.reshape(s // N_SEG, N_SEG, d).transpose(1, 0, 2).reshape(s, d)


def _lane_pad(a, width=LANES):
    return jnp.pad(a, ((0, 0), (0, width - a.shape[1])))


def _local_step(x, target, norm_pre, norm_post, kv_norm, kv_b_f, a_re, a_im, log_dt, b_re, b_im, c_re, c_im, comm):
    s, d = x.shape
    g, p = a_re.shape
    w = g * S5_GROUP
    fw = d
    nh = fw // HEAD_DIM
    seg_len = s // N_SEG
    row = lambda v: v.reshape(1, -1)
    g_pre0, g_pre1, g_post0, g_post1, g_kv = row(norm_pre[0]), row(norm_pre[1]), row(norm_post[0]), row(norm_post[1]), row(kv_norm)

    ldt = log_dt.reshape(g, 1)
    abr, abi, cr, ci = _s5_disc_fwd(a_re, a_im, ldt)
    cr_col, ci_col = cr.reshape(g * p, 1), ci.reshape(g * p, 1)
    b_re2, b_im2 = b_re.reshape(g * p, S5_GROUP), b_im.reshape(g * p, S5_GROUP)
    bb_re, bb_im = _s5_bbar_fwd(cr_col, ci_col, b_re2, b_im2)
    bd_re = _block_diag_in(bb_re.reshape(g, p, S5_GROUP)).astype(BF16)
    bd_im = _block_diag_in(bb_im.reshape(g, p, S5_GROUP)).astype(BF16)
    cd_re = _block_diag_out(c_re).astype(BF16)
    cd_im = _block_diag_out(-c_im).astype(BF16)
    ab_re = jnp.broadcast_to(abr.reshape(1, g * p), (N_SEG, g * p))
    ab_im = jnp.broadcast_to(abi.reshape(1, g * p), (N_SEG, g * p))
    zero_seg = jnp.zeros((N_SEG, g * p), F32)

    xn0 = _norm_cast(x, g_pre0 + comm.token, "norm_pre0", x_kind="nat")
    w_in = comm.weight("s5_w_in", xn0)
    d_row, bglu_row = row(comm.vector("s5_d")), row(comm.vector("s5_b_glu"))
    u = _mm(xn0, w_in, "nn", F32, "s5_in_u", b_cols=(0, w))
    z0 = _mm(xn0, w_in, "nn", F32, "s5_in_z", b_cols=(w, w))
    e_re, e_im = _s5_scan_fwd(u, bd_re, bd_im, cd_re, cd_im, ab_re, ab_im, zero_seg, zero_seg, d_row, False, "s5_scan_ends")
    i_re, i_im = _s5_seg_fix(e_re, e_im, ab_re, ab_im, seg_len, False, "s5_seg_fix")
    y_ssm, h_re, h_im, _, _ = _s5_scan_fwd(u, bd_re, bd_im, cd_re, cd_im, ab_re, ab_im, i_re, i_im, d_row, True, "s5_scan")
    yg = _gelu_cast(y_ssm, "s5_gelu")
    w_glu, w_out = comm.weight("s5_w_glu", yg), comm.weight("s5_w_out", yg)
    gp = _mm(yg, w_glu, "nn", F32, "s5_glu")
    y3 = _s5_gate(y_ssm, gp, bglu_row, z0, "s5_gate")
    o0 = _mm(y3, w_out, "nn", F32, "s5_out")
    r0 = _post_norm(o0, g_post0, "norm_post0", out_kind="nat")

    h1, hn_kv, xn1 = _resid_norm2(x, r0, g_kv, g_pre1, "resid_norms")
    w_kv, fw_in, fw_out = comm.weight("kv_w", hn_kv), comm.weight("fox_w_in", hn_kv), comm.weight("fox_w_out", hn_kv)
    w_f =_lane_pad(w_kv[:, 2 * fw:])
    kv = _mm(hn_kv, w_kv, "nn", BF16, "kv_proj", b_cols=(0, 2 * fw))
    f_logit = _mm(hn_kv, w_f, "nn", F32, "f_proj")
    bf_row = _lane_pad(row(kv_b_f))
    cum2 = _cum_fwd(f_logit, bf_row, "cum_fwd")
    cum2_t = cum2[:, :nh].T.reshape(nh, 1, s)
    q2 = _mm(xn1, fw_in, "nn", BF16, "fox_q", scale=HEAD_DIM ** -0.5 * LOG2E, b_cols=(0, fw))
    z1 = _mm(xn1, fw_in, "nn", F32, "fox_z", b_cols=(fw, fw))
    o, lse2, lse2_t = _fox_fwd(q2, kv, cum2_t, "fox_fwd")
    oz = _gate_mul(o, z1, "fox_gate")
    o1 = _mm(oz, fw_out, "nn", F32, "fox_out")
    dh2, sq = _post_norm_loss(o1, g_post1, h1, target, "norm_post1_loss")
    loss = 0.5 * jnp.sum(sq) / d

    do1, dg_post1 = _post_norm_bwd(dh2, o1, g_post1, "norm_post1_bwd")
    d_fw_out = _mm(oz, do1, "tn", F32, "fox_out_dw")
    d_oz = _mm(do1, fw_out, "nt", F32, "fox_out_dx")
    do, dz1 = _gate_bwd(d_oz, o, z1, "fox_gate_bwd")
    dq, delta_t, dcq = _fox_bwd_dq(q2, kv, do, o, lse2, cum2_t, "fox_bwd_dq")
    dk, dv, dck = _fox_bwd_dkv(q2, kv, do, lse2_t, delta_t, cum2, "fox_bwd_dkv")
    dqz = _concat_cast(dq, dz1, "fox_dqz")
    d_fw_in = _mm(xn1, dqz, "tn", F32, "fox_in_dw")
    dxn1 = _mm(dqz, fw_in, "nt", F32, "fox_in_dx")
    dcq_sl = _lane_pad(dcq.reshape(nh, s).T)
    dck_sl = _lane_pad(dck.reshape(nh, s).T)
    df, db_f = _cum_bwd(dcq_sl, dck_sl, f_logit, bf_row, "cum_bwd")
    dkv = _concat_cast(dk, dv, "fox_dkv")
    d_w_kvm = _mm(hn_kv, dkv, "tn", F32, "kv_dw")
    d_w_f = _mm(hn_kv, df, "tn", F32, "f_dw")
    dhn_f = _mm(df, w_f, "nt", F32, "f_dx")
    dhn_kv = _mm(dkv, w_kv, "nt", F32, "kv_dx", add=dhn_f, b_cols=(0, 2 * fw))
    d_w_kv = jnp.concatenate([d_w_kvm, d_w_f[:, :nh]], axis=1)
    tok = comm.send_grads(dict(fox_w_out=d_fw_out, fox_w_in=d_fw_in, kv_w=d_w_kv), "exchange_fox")
    dh1, dg_pre1, dg_kv = _norm_bwd2(dh2, h1, dxn1, dhn_kv, g_pre1, g_kv, "resid_norms_bwd")

    do0, dg_post0 = _post_norm_bwd(dh1, o0, g_post0 + tok, "norm_post0_bwd", dy_kind="nat")
    d_w_out = _mm(y3, do0, "tn", F32, "s5_out_dw")
    dy3 = _mm(do0, w_out, "nt", F32, "s5_out_dx")
    dz0, dgp, dyg_direct, db_glu = _s5_gate_bwd(dy3, y_ssm, gp, bglu_row, z0, "s5_gate_bwd")
    d_w_glu = _mm(yg, dgp, "tn", F32, "s5_glu_dw")
    dyg = _mm(dgp, w_glu, "nt", F32, "s5_glu_dx", add=dyg_direct)
    dy_ssm = _gelu_bwd(dyg, y_ssm, "s5_gelu_bwd")
    d_row = d_row + comm.send_grads(dict(s5_w_out=d_w_out, s5_w_glu=d_w_glu), "exchange_s5")
    ab_imn = -ab_im
    ge_re, ge_im = _s5_scan_bwd(dy_ssm, u, h_re, h_im, bd_re, bd_im, cd_re, cd_im, ab_re, ab_imn, zero_seg, zero_seg,
                                d_row, False, "s5_adj_ends")
    gi_re, gi_im = _s5_seg_fix(ge_re, ge_im, ab_re, ab_imn, seg_len, True, "s5_adj_fix")
    du, dbd_re, dbd_im, dcd_re, dcd_im, dab_re, dab_im, dd = _s5_scan_bwd(
        dy_ssm, u, h_re, h_im, bd_re, bd_im, cd_re, cd_im, ab_re, ab_imn, gi_re, gi_im, d_row, True, "s5_adj")
    duz = _concat_cast(du, dz0, "s5_duz")
    d_w_in = _mm(xn0, duz, "tn", F32, "s5_in_dw")
    comm.send_grads(dict(s5_w_in=d_w_in), "exchange_s5_in")
    dxn0 =_mm(duz, w_in, "nt", F32, "s5_in_dx")
    grad_x, dg_pre0 = _norm_bwd1(dh1, x, dxn0, g_pre0, "norm_pre0_bwd")

    dbb_re = _block_diag_in_extract(dbd_re, p, S5_GROUP).reshape(g * p, S5_GROUP)
    dbb_im = _block_diag_in_extract(dbd_im, p, S5_GROUP).reshape(g * p, S5_GROUP)
    dcr_col, dci_col, db_re, db_im = _s5_bbar_bwd(cr_col, ci_col, b_re2, b_im2, dbb_re, dbb_im)
    da_re, da_im, dldt = _s5_disc_bwd(a_re, a_im, ldt, dab_re.reshape(g, p), dab_im.reshape(g, p),
                                      dcr_col.reshape(g, p), dci_col.reshape(g, p))
    dc_re = _block_diag_out_extract(dcd_re, S5_GROUP, p)
    dc_im = -_block_diag_out_extract(dcd_im, S5_GROUP, p)

    small = dict(
        norm_pre=jnp.concatenate([dg_pre0, dg_pre1], axis=0), norm_post=jnp.concatenate([dg_post0, dg_post1], axis=0),
        s5_a_re=da_re, s5_a_im=da_im, s5_log_dt=dldt.reshape(g), s5_b_re=db_re.reshape(g, p, S5_GROUP),
        s5_b_im=db_im.reshape(g, p, S5_GROUP), s5_c_re=dc_re, s5_c_im=dc_im, s5_d=dd.reshape(-1),
        s5_b_glu=db_glu.reshape(-1), kv_norm=dg_kv.reshape(-1), kv_b_f=db_f[0, :nh])
    return loss, grad_x, small


_BIG = ("s5_w_in", "s5_w_glu", "s5_w_out", "kv_w", "fox_w_in", "fox_w_out")
_COL_SHARDED = ("s5_w_in", "kv_w", "fox_w_in")
_SMALL = ("norm_pre", "norm_post", "s5_a_re", "s5_a_im", "s5_log_dt", "s5_b_re", "s5_b_im", "s5_c_re", "s5_c_im",
          "s5_d", "s5_b_glu", "kv_norm", "kv_b_f")
_SMALL_SHARDED = ("s5_d", "s5_b_glu")
_PACK_QUANTUM = SUBLANES * LANES
_WEIGHTS = ('norm_pre', 'norm_post', 's5_w_in', 's5_a_re', 's5_a_im', 's5_log_dt', 's5_b_re', 's5_b_im', 's5_c_re', 's5_c_im',
            's5_d', 's5_w_glu', 's5_b_glu', 's5_w_out', 'kv_norm', 'kv_w', 'kv_b_f', 'fox_w_in', 'fox_w_out')


def _full_from_slots(name, slots):
    n, r, c = slots.shape
    if name in _COL_SHARDED:
        return slots.transpose(1, 0, 2).reshape(r, n * c)
    return slots.reshape(n * r, c)


def _slots_from_full(name, full):
    if name in _COL_SHARDED:
        r, nc = full.shape
        return full.reshape(r, N_DEV, nc // N_DEV).transpose(1, 0, 2)
    nr, c = full.shape
    return full.reshape(N_DEV, nr // N_DEV, c)


def _pack(vals):
    parts = []
    for v in vals:
        flat = v.reshape(-1)
        parts.append(jnp.pad(flat, (0, (-flat.shape[0]) % _PACK_QUANTUM)))
    total = sum(p.shape[0] for p in parts)
    parts.append(jnp.zeros(((-total) % (N_DEV * _PACK_QUANTUM),), F32))
    return jnp.concatenate(parts).reshape(-1, LANES)


def _unpack(packed, shapes):
    flat = packed.reshape(-1)
    out, off = [], 0
    for sh in shapes:
        n = math.prod(sh)
        out.append(flat[off:off + n].reshape(sh))
        off += n + (-n) % _PACK_QUANTUM
    return out


class _Comm:
    _GROUPS = (("s5_w_in",) + _SMALL_SHARDED, ("s5_w_glu", "s5_w_out"), ("kv_w", "fox_w_in", "fox_w_out"))

    def __init__(self, shards, vectors):
        shards = {**shards, **vectors}
        self._full, self._gathers = {}, {}
        self.token = jnp.zeros((), F32)
        for group in self._GROUPS:
            state, tok = _exchange_start([shards[n] for n in group], False, "gather_start_" + group[0])
            self._gathers[group] = state
            self.token = self.token + tok[0, 0]
        self._sent = []

    def vector(self, name):
        return self._full[name]

    def weight(self, name, after):
        if name not in self._full:
            group = next(g for g in self._GROUPS if name in g)
            slots = _exchange_wait(self._gathers.pop(group), after, "gather_wait_" + group[0])
            for n, sl in zip(group, slots):
                self._full[n] = sl.reshape(-1) if n in _SMALL_SHARDED else _full_from_slots(n, sl)
        return self._full[name]

    def send_grads(self, grads, name):
        names = list(grads)
        state, tok = _exchange_start([_slots_from_full(n, grads[n]).astype(BF16) for n in names], True, name + "_start")
        self._sent.append((names, state, name + "_wait"))
        return tok[0, 0]

    def received_grads(self, after):
        for names, state, name in self._sent:
            for n, recv in zip(names, _exchange_wait(state, after, name)):
                yield n, recv


def kernel(x, norm_pre, norm_post, s5_w_in, s5_a_re, s5_a_im, s5_log_dt, s5_b_re, s5_b_im, s5_c_re, s5_c_im, s5_d, s5_w_glu, s5_b_glu, s5_w_out, kv_norm, kv_w, kv_b_f, fox_w_in, fox_w_out, loss_target, m_norm_pre, m_norm_post, m_s5_w_in, m_s5_a_re, m_s5_a_im, m_s5_log_dt, m_s5_b_re, m_s5_b_im, m_s5_c_re, m_s5_c_im, m_s5_d, m_s5_w_glu, m_s5_b_glu, m_s5_w_out, m_kv_norm, m_kv_w, m_kv_b_f, m_fox_w_in, m_fox_w_out, v_norm_pre, v_norm_post, v_s5_w_in, v_s5_a_re, v_s5_a_im, v_s5_log_dt, v_s5_b_re, v_s5_b_im, v_s5_c_re, v_s5_c_im, v_s5_d, v_s5_w_glu, v_s5_b_glu, v_s5_w_out, v_kv_norm, v_kv_w, v_kv_b_f, v_fox_w_in, v_fox_w_out):
    env = dict(locals())
    wts = {n: env[n] for n in _WEIGHTS}
    mom = {n: env["m_" + n] for n in _WEIGHTS}
    var = {n: env["v_" + n] for n in _WEIGHTS}
    me = 4 * lax.axis_index("x") + 2 * lax.axis_index("y") + lax.axis_index("c")
    shard2d = {n: wts[n].reshape(wts[n].shape[-2:]) for n in _BIG}
    comm = _Comm({n: shard2d[n].astype(BF16) for n in _BIG}, {n: wts[n].reshape(1, -1) for n in _SMALL_SHARDED})

    loss_local, grad_x, small = _local_step(
        x[0], loss_target[0], norm_pre, norm_post, kv_norm, kv_b_f, s5_a_re[0], s5_a_im[0], s5_log_dt[0],
        s5_b_re[0], s5_b_im[0], s5_c_re[0], s5_c_im[0], comm)
    loss = lax.psum(loss_local, MESH_AXES)

    small_pack = _pack([small[n] for n in _SMALL])
    slice_rows = small_pack.shape[0] // N_DEV
    small_state, small_tok = _exchange_start([small_pack.reshape(N_DEV, slice_rows, LANES)], True, "reduce_small_start")

    res = {}
    for n, recv in comm.received_grads(small_tok):
        outs = _adamw(recv, shard2d[n], mom[n].reshape(shard2d[n].shape), var[n].reshape(shard2d[n].shape), "adamw_" + n)
        res[n] = [o.reshape(wts[n].shape) for o in outs]

    full_shape = {n: (small[n].shape if n in _SMALL_SHARDED else wts[n].shape) for n in _SMALL}

    def spread(n, v):
        if n not in _SMALL_SHARDED:
            return v
        flat = v.reshape(-1)
        return lax.dynamic_update_slice(jnp.zeros(full_shape[n], F32), flat, (me * flat.shape[0],))

    my_sum = _sum_parts(_exchange_wait(small_state, res[_BIG[0]][0], "reduce_small_wait")[0], "sum_small")
    g_all = _exchange([my_sum], False, "gather_small")[0].reshape(1, small_pack.shape[0], LANES)
    packed =[_pack([spread(n, src[n]) for n in _SMALL]) for src in (wts, mom, var)]
    outs = _adamw(g_all, *packed, "adamw_small")
    unpacked = [_unpack(o, [full_shape[n] for n in _SMALL]) for o in outs]
    for i, n in enumerate(_SMALL):
        vals = [u[i] for u in unpacked]
        if n in _SMALL_SHARDED:
            k = wts[n].size
            vals = [lax.dynamic_slice(v, (me * k,), (k,)) for v in vals]
        res[n] = [v.reshape(wts[n].shape) for v in vals]

    return (loss, grad_x[None], *[res[n][0] for n in _WEIGHTS], *[res[n][1] for n in _WEIGHTS],
            *[res[n][2] for n in _WEIGHTS], *[res[n][3] for n in _WEIGHTS])
```

```python
import functools
import math

import jax
import jax.numpy as jnp
from jax import lax
from jax.experimental import pallas as pl
from jax.experimental.pallas import tpu as pltpu

F32 = jnp.float32
BF16 = jnp.bfloat16

N_DEV = 8
MESH_AXES = ("x", "y", "c")
S5_GROUP = 16
S5_STATE = 64
LANES = 128
SUBLANES = 8
GROUPS_PER_BLOCK = LANES // S5_GROUP
BLOCK_STATE = GROUPS_PER_BLOCK * S5_STATE
N_SEG = SUBLANES
HEAD_DIM = 128
RMS_EPS = 1e-6
NEG_INF = -1e30
LOG2E = math.log2(math.e)
ADAM_LR = 0.001
ADAM_B1 = 0.9
ADAM_B2 = 0.999
ADAM_EPS = 1e-08
ADAM_WD = 0.01
ADAM_STEP = 10
VMEM_LIMIT = 56 * 1024 * 1024


def _tile(n, pref, quantum=LANES):
    if n <= pref:
        return n
    t = (pref // quantum) * quantum
    while t >= quantum:
        if n % t == 0:
            return t
        t -= quantum
    return n


def _cparams(*sem):
    return pltpu.CompilerParams(dimension_semantics=sem if sem else None, vmem_limit_bytes=VMEM_LIMIT)


_DOT_DIMS = {"nn": ((1,), (0,)), "nt": ((1,), (1,)), "tn": ((0,), (0,))}


def _mm(a, b, mode, out_dtype, name, add=None, scale=None, b_cols=None):
    b_shape = b.shape if b_cols is None else (b.shape[0], b_cols[1])
    if mode == "nn":
        (M, K), (K2, N) = a.shape, b_shape
    elif mode == "nt":
        (M, K), (N, K2) = a.shape, b_shape
    else:
        (K, M), (K2, N) = a.shape, b_shape
    assert K == K2, (name, a.shape, b_shape)
    tm, tn, tk = _tile(M, 1024 if K <= 2048 else 512), _tile(N, 1024), _tile(K, 4096)
    nk = K // tk
    dims = (_DOT_DIMS[mode], ((), ()))
    col0 = 0
    if b_cols is not None:
        assert mode != "tn" and b_cols[0] % (tn if mode == "nn" else tk) == 0
        col0 = b_cols[0] // (tn if mode == "nn" else tk)

    def body(*refs):
        a_ref, b_ref = refs[:2]
        c_ref = refs[2] if add is not None else None
        o_ref = refs[3 if add is not None else 2]
        part = lax.dot_general(a_ref[...], b_ref[...], dims, preferred_element_type=F32)

        def finish(r):
            if scale is not None:
                r = r * scale
            if add is not None:
                r = r + c_ref[...]
            o_ref[...] = r.astype(out_dtype)

        if nk == 1:
            finish(part)
            return
        acc = refs[-1]
        k = pl.program_id(2)

        @pl.when(k == 0)
        def _():
            acc[...] = part

        @pl.when(jnp.logical_and(k > 0, k < nk - 1))
        def _():
            acc[...] += part

        @pl.when(k == nk - 1)
        def _():
            finish(acc[...] + part)

    if mode == "tn":
        a_spec = pl.BlockSpec((tk, tm), lambda i, j, k: (k, i))
    else:
        a_spec = pl.BlockSpec((tm, tk), lambda i, j, k: (i, k))
    if mode == "nt":
        b_spec = pl.BlockSpec((tn, tk), lambda i, j, k: (j, k + col0))
    else:
        b_spec = pl.BlockSpec((tk, tn), lambda i, j, k: (k, j + col0))
    o_spec = pl.BlockSpec((tm, tn), lambda i, j, k: (i, j))
    in_specs = [a_spec, b_spec] + ([o_spec] if add is not None else [])
    args = (a, b) + ((add,) if add is not None else ())
    return pl.pallas_call(
        body, name=name, grid=(M // tm, N // tn, nk),
        in_specs=in_specs, out_specs=o_spec,
        out_shape=jax.ShapeDtypeStruct((M, N), out_dtype),
        scratch_shapes=[pltpu.VMEM((tm, tn), F32)] if nk > 1 else [],
        compiler_params=_cparams("parallel", "parallel", "arbitrary"),
    )(*args)


class _NatIn:
    def __init__(self, ref):
        self.ref = ref

    def __getitem__(self, idx):
        v = jnp.swapaxes(self.ref[...], 0, 1)
        return v.reshape(v.shape[0] * N_SEG, v.shape[2])


class _NatOut:
    def __init__(self, ref):
        self.ref = ref

    def __setitem__(self, idx, val):
        self.ref[...] = jnp.swapaxes(val.reshape(val.shape[0] // N_SEG, N_SEG, val.shape[1]), 0, 1)


def _rowcall(body, name, n_rows, ins, outs, tile_rows=256):
    tr = _tile(n_rows, tile_rows, SUBLANES * 2)
    n_in = len(ins)
    in_kinds = [k for _, k in ins]
    kinds = [k for _, _, k in outs]

    def kern(*refs):
        @pl.when(pl.program_id(0) == 0)
        def _():
            for r, kind in zip(refs[n_in:], kinds):
                if kind == "acc":
                    r[...] = jnp.zeros_like(r)

        wrapped = [_NatIn(r) if k == "nat" else r for r, k in zip(refs[:n_in], in_kinds)]
        wrapped += [_NatOut(r) if k == "nat" else r for r, k in zip(refs[n_in:], kinds)]
        body(*wrapped)

    in_specs, args = [], []
    for arr, kind in ins:
        if kind == "row":
            in_specs.append(pl.BlockSpec((tr, arr.shape[1]), lambda i: (i, 0)))
        elif kind == "nat":
            in_specs.append(pl.BlockSpec((N_SEG, tr // N_SEG, arr.shape[1]), lambda i: (0, i, 0)))
            arr = arr.reshape(N_SEG, n_rows // N_SEG, arr.shape[1])
        else:
            in_specs.append(pl.BlockSpec(arr.shape, lambda i, nd=arr.ndim: (0,) * nd))
        args.append(arr)
    out_specs, out_shape = [], []
    for width, dtype, kind in outs:
        if kind == "row":
            out_specs.append(pl.BlockSpec((tr, width), lambda i: (i, 0)))
            out_shape.append(jax.ShapeDtypeStruct((n_rows, width), dtype))
        elif kind == "right":
            out_specs.append(pl.BlockSpec((tr, width), lambda i: (i, 1)))
            out_shape.append(jax.ShapeDtypeStruct((n_rows, 2 * width), dtype))
        elif kind == "nat":
            out_specs.append(pl.BlockSpec((N_SEG, tr // N_SEG, width), lambda i: (0, i, 0)))
            out_shape.append(jax.ShapeDtypeStruct((N_SEG, n_rows // N_SEG, width), dtype))
        else:
            out_specs.append(pl.BlockSpec((1, width), lambda i: (0, 0)))
            out_shape.append(jax.ShapeDtypeStruct((1, width), F32))
    res = pl.pallas_call(
        kern, name=name, grid=(n_rows // tr,), in_specs=in_specs, out_specs=out_specs, out_shape=out_shape,
        compiler_params=_cparams("arbitrary"),
    )(*args)
    return [r.reshape(n_rows, r.shape[2]) if k == "nat" else r for r, k in zip(res, kinds)]


def _rstd(x):
    return lax.rsqrt(jnp.mean(x * x, axis=-1, keepdims=True) + RMS_EPS)


def _rms_bwd(x, g, dy):
    xh = x * _rstd(x)
    dxh = dy * g
    dx = _rstd(x) * (dxh - xh * jnp.mean(dxh * xh, axis=-1, keepdims=True))
    return dx, jnp.sum(dy * xh, axis=0, keepdims=True)


def _silu(z):
    return z * jax.nn.sigmoid(z)


def _norm_cast(x, g, name, x_kind="row"):
    def body(x_ref, g_ref, o_ref):
        x = x_ref[...]
        o_ref[...] = (x * _rstd(x) * g_ref[...]).astype(BF16)

    return _rowcall(body, name, x.shape[0], [(x, x_kind), (g, "full")], [(x.shape[1], BF16, "row")])[0]


def _resid_norm2(x, r0, g_kv, g_pre, name):
    def body(x_ref, r_ref, gk_ref, gp_ref, h_ref, nk_ref, np_ref):
        h = x_ref[...] + r_ref[...]
        h_ref[...] = h
        hn = h * _rstd(h)
        nk_ref[...] = (hn * gk_ref[...]).astype(BF16)
        np_ref[...] = (hn * gp_ref[...]).astype(BF16)

    d = x.shape[1]
    return _rowcall(body, name, x.shape[0], [(x, "row"), (r0, "row"), (g_kv, "full"), (g_pre, "full")],
                    [(d, F32, "row"), (d, BF16, "row"), (d, BF16, "row")])


def _post_norm(o, g, name, out_kind="row"):
    def body(o_ref, g_ref, r_ref):
        o = o_ref[...]
        r_ref[...] = o * _rstd(o) * g_ref[...]

    return _rowcall(body, name, o.shape[0], [(o, "row"), (g, "full")], [(o.shape[1], F32, out_kind)])[0]


def _post_norm_loss(o, g, h1, target, name):
    d = o.shape[1]

    def body(o_ref, g_ref, h_ref, t_ref, dh_ref, acc_ref):
        o = o_ref[...]
        e = h_ref[...] + o * _rstd(o) * g_ref[...] - t_ref[...]
        dh_ref[...] = e * (1.0 / d)
        acc_ref[...] += jnp.sum(e * e, axis=0, keepdims=True)

    return _rowcall(body, name, o.shape[0], [(o, "row"), (g, "full"), (h1, "row"), (target, "row")],
                    [(d, F32, "row"), (d, F32, "acc")])


def _post_norm_bwd(dy, o, g, name, dy_kind="row"):
    def body(dy_ref, o_ref, g_ref, do_ref, dg_ref):
        dx, dg = _rms_bwd(o_ref[...], g_ref[...], dy_ref[...])
        do_ref[...] = dx.astype(BF16)
        dg_ref[...] += dg

    d = o.shape[1]
    return _rowcall(body, name, o.shape[0], [(dy, dy_kind), (o, "row"), (g, "full")], [(d, BF16, "row"), (d, F32, "acc")])


def _gate_mul(o, z, name):
    def body(o_ref, z_ref, r_ref):
        r_ref[...] = (o_ref[...] * _silu(z_ref[...])).astype(BF16)

    return _rowcall(body, name, o.shape[0], [(o, "row"), (z, "row")], [(o.shape[1], BF16, "row")])[0]


def _gate_bwd(d_oz, o, z, name):
    def body(d_ref, o_ref, z_ref, do_ref, dz_ref):
        _, vjp = jax.vjp(lambda o, z: o * _silu(z), o_ref[...], z_ref[...])
        do, dz = vjp(d_ref[...])
        do_ref[...] = do.astype(BF16)
        dz_ref[...] = dz.astype(BF16)

    w = o.shape[1]
    return _rowcall(body, name, o.shape[0], [(d_oz, "row"), (o, "row"), (z, "row")], [(w, BF16, "row"), (w, BF16, "right")])


def _norm_bwd2(dh2, h1, dxn1, dhn_kv, g_pre, g_kv, name):
    def body(dh2_ref, h_ref, d1_ref, dk_ref, gp_ref, gk_ref, dh1_ref, dgp_ref, dgk_ref):
        h = h_ref[...]
        dx1, dg1 = _rms_bwd(h, gp_ref[...], d1_ref[...])
        dxk, dgk = _rms_bwd(h, gk_ref[...], dk_ref[...])
        dh1_ref[...] = dh2_ref[...] + dx1 + dxk
        dgp_ref[...] += dg1
        dgk_ref[...] += dgk

    d = h1.shape[1]
    return _rowcall(body, name, h1.shape[0],
                    [(dh2, "row"), (h1, "row"), (dxn1, "row"), (dhn_kv, "row"), (g_pre, "full"), (g_kv, "full")],
                    [(d, F32, "row"), (d, F32, "acc"), (d, F32, "acc")])


def _norm_bwd1(dres, x, dxn, g, name):
    def body(dr_ref, x_ref, dn_ref, g_ref, dx_ref, dg_ref):
        dx, dg = _rms_bwd(x_ref[...], g_ref[...], dn_ref[...])
        dx_ref[...] = dr_ref[...] + dx
        dg_ref[...] += dg

    d = x.shape[1]
    return _rowcall(body, name, x.shape[0], [(dres, "nat"), (x, "nat"), (dxn, "row"), (g, "full")],
                    [(d, F32, "nat"), (d, F32, "acc")])


def _gelu_cast(y, name):
    def body(y_ref, o_ref):
        o_ref[...] = jax.nn.gelu(y_ref[...]).astype(BF16)

    return _rowcall(body, name, y.shape[0], [(y, "row")], [(y.shape[1], BF16, "row")])[0]


def _s5_gate(y_ssm, gp, b_glu, z, name):
    def body(y_ref, gp_ref, b_ref, z_ref, o_ref):
        yg = jax.nn.gelu(y_ref[...])
        o_ref[...] = (yg * jax.nn.sigmoid(gp_ref[...] + b_ref[...]) * _silu(z_ref[...])).astype(BF16)

    return _rowcall(body, name, y_ssm.shape[0], [(y_ssm, "row"), (gp, "row"), (b_glu, "full"), (z, "row")],
                    [(y_ssm.shape[1], BF16, "row")])[0]


def _s5_gate_bwd(dy3, y_ssm, gp, b_glu, z, name):
    def body(d_ref, y_ref, gp_ref, b_ref, z_ref, dz_ref, dgp_ref, dyg_ref, db_ref):
        yg = jax.nn.gelu(y_ref[...])
        _, vjp = jax.vjp(lambda yg, gp, z: yg * jax.nn.sigmoid(gp) * _silu(z), yg, gp_ref[...] + b_ref[...], z_ref[...])
        dyg, dgp, dz = vjp(d_ref[...])
        dz_ref[...] = dz.astype(BF16)
        dgp_ref[...] = dgp.astype(BF16)
        dyg_ref[...] = dyg
        db_ref[...] += jnp.sum(dgp, axis=0, keepdims=True)

    w = y_ssm.shape[1]
    return _rowcall(body, name, y_ssm.shape[0],
                    [(dy3, "row"), (y_ssm, "row"), (gp, "row"), (b_glu, "full"), (z, "row")],
                    [(w, BF16, "right"), (w, BF16, "row"), (w, F32, "row"), (w, F32, "acc")])


def _gelu_bwd(dyg, y_ssm, name):
    def body(d_ref, y_ref, o_ref):
        _, vjp = jax.vjp(jax.nn.gelu, y_ref[...])
        o_ref[...] = vjp(d_ref[...])[0]

    return _rowcall(body, name, y_ssm.shape[0], [(dyg, "row"), (y_ssm, "row")], [(y_ssm.shape[1], F32, "row")])[0]


def _concat_cast(a, b, name):
    def body(a_ref, b_ref, o_ref):
        w = a_ref.shape[1]
        o_ref[:, :w] = a_ref[...].astype(BF16)
        o_ref[:, w:] = b_ref[...].astype(BF16)

    return _rowcall(body, name, a.shape[0], [(a, "row"), (b, "row")], [(a.shape[1] + b.shape[1], BF16, "row")])[0]


def _disc(ar, ai, ldt):
    dt = jnp.exp(ldt)
    mag = jnp.exp(ar * dt)
    abr = mag * jnp.cos(ai * dt)
    abi = mag * jnp.sin(ai * dt)
    den = ar * ar + ai * ai
    nr = abr - 1.0
    return abr, abi, (nr * ar + abi * ai) / den, (abi * ar - nr * ai) / den


def _s5_disc_fwd(a_re, a_im, ldt):
    def body(ar, ai, ld, o1, o2, o3, o4):
        o1[...], o2[...], o3[...], o4[...] = _disc(ar[...], ai[...], ld[...])

    sh = jax.ShapeDtypeStruct(a_re.shape, F32)
    return pl.pallas_call(body, name="s5_disc_fwd", out_shape=(sh, sh, sh, sh))(a_re, a_im, ldt)


def _s5_disc_bwd(a_re, a_im, ldt, d_abr, d_abi, d_cr, d_ci):
    def body(ar, ai, ld, g1, g2, g3, g4, o1, o2, o3):
        _, vjp = jax.vjp(_disc, ar[...], ai[...], ld[...])
        o1[...], o2[...], o3[...] = vjp((g1[...], g2[...], g3[...], g4[...]))

    sh = jax.ShapeDtypeStruct(a_re.shape, F32)
    return pl.pallas_call(body, name="s5_disc_bwd", out_shape=(sh, sh, jax.ShapeDtypeStruct(ldt.shape, F32)))(
        a_re, a_im, ldt, d_abr, d_abi, d_cr, d_ci)


def _bbar(cr, ci, br, bi):
    return cr * br - ci * bi, cr * bi + ci * br


def _s5_bbar_fwd(cr_col, ci_col, b_re, b_im):
    def body(cr, ci, br, bi, o1, o2):
        o1[...], o2[...] = _bbar(cr[...], ci[...], br[...], bi[...])

    w = b_re.shape[1]
    return _rowcall(body, "s5_bbar_fwd", b_re.shape[0], [(cr_col, "row"), (ci_col, "row"), (b_re, "row"), (b_im, "row")],
                    [(w, F32, "row"), (w, F32, "row")], tile_rows=1024)


def _s5_bbar_bwd(cr_col, ci_col, b_re, b_im, d_re, d_im):
    def body(cr, ci, br, bi, g1, g2, o1, o2, o3, o4):
        _, vjp = jax.vjp(_bbar, cr[...], ci[...], br[...], bi[...])
        o1[...], o2[...], o3[...], o4[...] = vjp((g1[...], g2[...]))

    w = b_re.shape[1]
    return _rowcall(body, "s5_bbar_bwd", b_re.shape[0],
                    [(cr_col, "row"), (ci_col, "row"), (b_re, "row"), (b_im, "row"), (d_re, "row"), (d_im, "row")],
                    [(1, F32, "row"), (1, F32, "row"), (w, F32, "row"), (w, F32, "row")], tile_rows=1024)


def _block_diag_in(t):
    g, p, c = t.shape
    nb = g // GROUPS_PER_BLOCK
    t4 = t.reshape(nb, GROUPS_PER_BLOCK, p, c).transpose(0, 1, 3, 2)
    eye = jnp.eye(GROUPS_PER_BLOCK, dtype=t.dtype)
    return (t4[:, :, :, None, :] * eye[None, :, None, :, None]).reshape(nb, GROUPS_PER_BLOCK * c, GROUPS_PER_BLOCK * p)


def _block_diag_in_extract(d, p, c):
    nb = d.shape[0]
    d5 = d.reshape(nb, GROUPS_PER_BLOCK, c, GROUPS_PER_BLOCK, p)
    diag = jnp.stack([d5[:, g, :, g, :] for g in range(GROUPS_PER_BLOCK)], axis=1)
    return diag.transpose(0, 1, 3, 2).reshape(nb * GROUPS_PER_BLOCK, p, c)


def _block_diag_out(t):
    g, c, p = t.shape
    nb = g // GROUPS_PER_BLOCK
    t4 = t.reshape(nb, GROUPS_PER_BLOCK, c, p).transpose(0, 1, 3, 2)
    eye = jnp.eye(GROUPS_PER_BLOCK, dtype=t.dtype)
    return (t4[:, :, :, None, :] * eye[None, :, None, :, None]).reshape(nb, GROUPS_PER_BLOCK * p, GROUPS_PER_BLOCK * c)


def _block_diag_out_extract(d, c, p):
    nb = d.shape[0]
    d5 = d.reshape(nb, GROUPS_PER_BLOCK, p, GROUPS_PER_BLOCK, c)
    diag = jnp.stack([d5[:, g, :, g, :] for g in range(GROUPS_PER_BLOCK)], axis=1)
    return diag.transpose(0, 1, 3, 2).reshape(nb * GROUPS_PER_BLOCK, c, p)


def _scan_step(ar, ai, hr, hi, xr, xi):
    return ar * hr - ai * hi + xr, ar * hi + ai * hr + xi


def _s5_scan_fwd(u, bd_re, bd_im, cd_re, cd_im, ab_re, ab_im, init_re, init_im, d_row, full, name):
    s, w = u.shape
    nb = w // LANES
    rows = _tile(s, 512, SUBLANES)
    nc = s // rows
    steps = rows // N_SEG
    ns = nb * BLOCK_STATE

    def body(u_ref, bdr, bdi, cdr, cdi, ar_ref, ai_ref, ir_ref, ii_ref, d_ref, *outs):
        if full:
            y_ref, yg_ref, hr_ref, hi_ref, er_ref, ei_ref, cr, ci = outs
        else:
            er_ref, ei_ref, hr_ref, hi_ref, cr, ci = outs
        c = pl.program_id(1)

        @pl.when(c == 0)
        def _():
            cr[...] = ir_ref[...]
            ci[...] = ii_ref[...]

        ub = u_ref[...].astype(BF16)
        hr_ref[...] = jnp.dot(ub, bdr[...], preferred_element_type=F32)
        hi_ref[...] = jnp.dot(ub, bdi[...], preferred_element_type=F32)
        ar, ai = ar_ref[...], ai_ref[...]

        def step(j, carry):
            off = pl.multiple_of(j * N_SEG, N_SEG)
            nr, ni = _scan_step(ar, ai, carry[0], carry[1], hr_ref[pl.ds(off, N_SEG), :], hi_ref[pl.ds(off, N_SEG), :])
            hr_ref[pl.ds(off, N_SEG), :] = nr
            hi_ref[pl.ds(off, N_SEG), :] = ni
            return nr, ni

        hr, hi = lax.fori_loop(0, steps, step, (cr[...], ci[...]), unroll=8)
        cr[...] = hr
        ci[...] = hi
        if full:
            y = (jnp.dot(hr_ref[...].astype(BF16), cdr[...], preferred_element_type=F32)
                 + jnp.dot(hi_ref[...].astype(BF16), cdi[...], preferred_element_type=F32)
                 + d_ref[...] * u_ref[...])
            y_ref[...] = y
            yg_ref[...] = jax.nn.gelu(y).astype(BF16)

        @pl.when(c == nc - 1)
        def _():
            er_ref[...] = hr
            ei_ref[...] = hi

    blk3 = lambda a: pl.BlockSpec((None,) + a.shape[1:], lambda k, c: (k, 0, 0))
    seg = pl.BlockSpec((N_SEG, BLOCK_STATE), lambda k, c: (0, k))
    st = pl.BlockSpec((rows, BLOCK_STATE), lambda k, c: (c, k))
    in_specs = [pl.BlockSpec((rows, LANES), lambda k, c: (c, k)), blk3(bd_re), blk3(bd_im), blk3(cd_re), blk3(cd_im),
                seg, seg, seg, seg, pl.BlockSpec((1, LANES), lambda k, c: (0, k))]
    seg_shape = jax.ShapeDtypeStruct((N_SEG, ns), F32)
    st_shape = jax.ShapeDtypeStruct((s, ns), F32)
    carry = [pltpu.VMEM((N_SEG, BLOCK_STATE), F32)] * 2
    if full:
        ych = pl.BlockSpec((rows, LANES), lambda k, c: (c, k))
        out_specs = [ych, ych, st, st, seg, seg]
        out_shape = [jax.ShapeDtypeStruct((s, w), F32), jax.ShapeDtypeStruct((s, w), BF16), st_shape, st_shape, seg_shape, seg_shape]
        scratch = carry
    else:
        out_specs = [seg, seg]
        out_shape = [seg_shape, seg_shape]
        scratch = [pltpu.VMEM((rows, BLOCK_STATE), F32)] * 2 + carry
    return pl.pallas_call(
        body, name=name, grid=(nb, nc), in_specs=in_specs, out_specs=out_specs, out_shape=out_shape,
        scratch_shapes=scratch, compiler_params=_cparams("parallel", "arbitrary"),
    )(u, bd_re, bd_im, cd_re, cd_im, ab_re, ab_im, init_re, init_im, d_row)


def _s5_seg_fix(e_re, e_im, ab_re, ab_im, seg_len, reverse, name):
    assert seg_len & (seg_len - 1) == 0

    def body(er, ei, ar, ai, o_re, o_im):
        pr, pi = ar[0:1, :], ai[0:1, :]
        for _ in range(int(math.log2(seg_len))):
            pr, pi = pr * pr - pi * pi, 2.0 * pr * pi
        tr = jnp.zeros_like(pr)
        ti = jnp.zeros_like(pr)
        order = list(range(N_SEG - 1, -1, -1)) if reverse else list(range(N_SEG))
        for n, sgm in enumerate(order):
            o_re[sgm:sgm + 1, :] = tr
            o_im[sgm:sgm + 1, :] = ti
            if n < N_SEG - 1:
                tr, ti = _scan_step(pr, pi, tr, ti, er[sgm:sgm + 1, :], ei[sgm:sgm + 1, :])

    sh = jax.ShapeDtypeStruct(e_re.shape, F32)
    return pl.pallas_call(body, name=name, out_shape=(sh, sh))(e_re, e_im, ab_re, ab_im)


def _s5_scan_bwd(dy, u, h_re, h_im, bd_re, bd_im, cd_re, cd_im, ab_re, ab_imn, gin_re, gin_im, d_row, full, name, duz=None):
    s, w = u.shape
    nb = w // LANES
    rows = _tile(s, 512, SUBLANES)
    nc = s // rows
    steps = rows // N_SEG
    ns = nb * BLOCK_STATE

    def body(dy_ref, u_ref, hr_ref, hi_ref, bdr, bdi, cdr, cdi, ar_ref, ai_ref, ir_ref, ii_ref, d_ref, *outs):
        if full:
            _, du_ref, dbr_ref, dbi_ref, dcr_ref, dci_ref, dar_ref, dai_ref, dd_ref, gr, gi, accr, acci = outs
        else:
            er_ref, ei_ref, gr, gi = outs
        c = pl.program_id(1)

        @pl.when(c == 0)
        def _():
            gr[pl.ds(rows, N_SEG), :] = ir_ref[...]
            gi[pl.ds(rows, N_SEG), :] = ii_ref[...]
            if full:
                for r in (dbr_ref, dbi_ref, dcr_ref, dci_ref, dd_ref, accr, acci):
                    r[...] = jnp.zeros_like(r)

        dyb = dy_ref[...].astype(BF16)
        nt = (_DOT_DIMS["nt"], ((), ()))
        tn = (_DOT_DIMS["tn"], ((), ()))
        gr[pl.ds(0, rows), :] = lax.dot_general(dyb, cdr[...], nt, preferred_element_type=F32)
        gi[pl.ds(0, rows), :] = lax.dot_general(dyb, cdi[...], nt, preferred_element_type=F32)
        ar, ai = ar_ref[...], ai_ref[...]

        def step(jj, carry):
            off = pl.multiple_of((steps - 1 - jj) * N_SEG, N_SEG)
            nr, ni = _scan_step(ar, ai, carry[0], carry[1], gr[pl.ds(off, N_SEG), :], gi[pl.ds(off, N_SEG), :])
            gr[pl.ds(off, N_SEG), :] = nr
            gi[pl.ds(off, N_SEG), :] = ni
            return nr, ni

        g0r, g0i = lax.fori_loop(0, steps, step, (gr[pl.ds(rows, N_SEG), :], gi[pl.ds(rows, N_SEG), :]), unroll=8)
        if full:
            hr, hi = hr_ref[...], hi_ref[...]
            gnr, gni = gr[pl.ds(N_SEG, rows), :], gi[pl.ds(N_SEG, rows), :]
            accr[...] += jnp.sum((gnr * hr + gni * hi).reshape(steps, N_SEG, BLOCK_STATE), axis=0)
            acci[...] += jnp.sum((gni * hr - gnr * hi).reshape(steps, N_SEG, BLOCK_STATE), axis=0)
        gr[pl.ds(rows, N_SEG), :] = g0r
        gi[pl.ds(rows, N_SEG), :] = g0i
        if full:
            ub = u_ref[...].astype(BF16)
            gbr, gbi = gr[pl.ds(0, rows), :].astype(BF16), gi[pl.ds(0, rows), :].astype(BF16)
            dcr_ref[...] += lax.dot_general(hr.astype(BF16), dyb, tn, preferred_element_type=F32)
            dci_ref[...] += lax.dot_general(hi.astype(BF16), dyb, tn, preferred_element_type=F32)
            dbr_ref[...] += lax.dot_general(ub, gbr, tn, preferred_element_type=F32)
            dbi_ref[...] += lax.dot_general(ub, gbi, tn, preferred_element_type=F32)
            du_ref[...] = (lax.dot_general(gbr, bdr[...], nt, preferred_element_type=F32)
                           + lax.dot_general(gbi, bdi[...], nt, preferred_element_type=F32)
                           + d_ref[...] * dy_ref[...]).astype(BF16)
            dd_ref[...] += jnp.sum(dy_ref[...] * u_ref[...], axis=0, keepdims=True)

        @pl.when(c == nc - 1)
        def _():
            if full:
                dar_ref[...] = jnp.sum(accr[...], axis=0, keepdims=True)
                dai_ref[...] = jnp.sum(acci[...], axis=0, keepdims=True)
            else:
                er_ref[...] = g0r
                ei_ref[...] = g0i

    rev = lambda k, c: (nc - 1 - c, k)
    blk3 = lambda a: pl.BlockSpec((None,) + a.shape[1:], lambda k, c: (k, 0, 0))
    seg = pl.BlockSpec((N_SEG, BLOCK_STATE), lambda k, c: (0, k))
    st = pl.BlockSpec((rows, BLOCK_STATE), rev)
    ch = pl.BlockSpec((rows, LANES), rev)
    vec = pl.BlockSpec((1, LANES), lambda k, c: (0, k))
    if not full:
        st = pl.BlockSpec((rows, BLOCK_STATE), lambda k, c: (0, k))
    in_specs = [ch, ch if full else pl.BlockSpec((rows, LANES), lambda k, c: (0, k)), st, st,
                blk3(bd_re), blk3(bd_im), blk3(cd_re), blk3(cd_im), seg, seg, seg, seg, vec]
    args = [dy, u, h_re, h_im, bd_re, bd_im, cd_re, cd_im, ab_re, ab_imn, gin_re, gin_im, d_row]
    gbuf = [pltpu.VMEM((rows + N_SEG, BLOCK_STATE), F32)] * 2
    if full:
        row1 = pl.BlockSpec((1, BLOCK_STATE), lambda k, c: (0, k))
        out_specs = [ch, blk3(bd_re), blk3(bd_im), blk3(cd_re), blk3(cd_im), row1, row1, vec]
        out_shape = [jax.ShapeDtypeStruct(duz.shape, BF16),
                     jax.ShapeDtypeStruct(bd_re.shape, F32), jax.ShapeDtypeStruct(bd_im.shape, F32),
                     jax.ShapeDtypeStruct(cd_re.shape, F32), jax.ShapeDtypeStruct(cd_im.shape, F32),
                     jax.ShapeDtypeStruct((1, ns), F32), jax.ShapeDtypeStruct((1, ns), F32),
                     jax.ShapeDtypeStruct((1, w), F32)]
        scratch = gbuf + [pltpu.VMEM((N_SEG, BLOCK_STATE), F32)] * 2
        in_specs.append(pl.BlockSpec(memory_space=pl.ANY))
        args.append(duz)
        aliases = {len(args) - 1: 0}
    else:
        out_specs = [seg, seg]
        out_shape = [jax.ShapeDtypeStruct((N_SEG, ns), F32)] * 2
        scratch = gbuf
        aliases = {}
    return pl.pallas_call(
        body, name=name, grid=(nb, nc), in_specs=in_specs, out_specs=out_specs, out_shape=out_shape,
        input_output_aliases=aliases, scratch_shapes=scratch, compiler_params=_cparams("parallel", "arbitrary"),
    )(*args)


def _log_sigmoid(x):
    return jnp.minimum(x, 0.0) - jnp.log(1.0 + jnp.exp(-jnp.abs(x)))


def _tri(n, upper):
    r = lax.broadcasted_iota(jnp.int32, (n, n), 0)
    c = lax.broadcasted_iota(jnp.int32, (n, n), 1)
    return jnp.where((c >= r) if upper else (r >= c), 1.0, 0.0).astype(F32)


def _cum_fwd(f_logit, b_row, name):
    s, w = f_logit.shape
    t = _tile(s, 256, SUBLANES)

    def body(f_ref, b_ref, o_ref, carry):
        @pl.when(pl.program_id(0) == 0)
        def _():
            carry[...] = jnp.zeros_like(carry)

        lf = _log_sigmoid(f_ref[...] + b_ref[...])
        cum = jnp.dot(_tri(t, False), lf, precision=lax.Precision.HIGHEST, preferred_element_type=F32) + carry[...]
        o_ref[...] = cum * LOG2E
        carry[...] = cum[t - 1:t, :]

    return pl.pallas_call(
        body, name=name, grid=(s // t,),
        in_specs=[pl.BlockSpec((t, w), lambda i: (i, 0)), pl.BlockSpec((1, w), lambda i: (0, 0))],
        out_specs=pl.BlockSpec((t, w), lambda i: (i, 0)), out_shape=jax.ShapeDtypeStruct((s, w), F32),
        scratch_shapes=[pltpu.VMEM((1, w), F32)], compiler_params=_cparams("arbitrary"),
    )(f_logit, b_row)


def _cum_bwd(dcq, dck, f_logit, b_row, name):
    s, w = f_logit.shape
    t = _tile(s, 256, SUBLANES)
    nt = s // t

    def body(q_ref, k_ref, f_ref, b_ref, df_ref, db_ref, carry):
        @pl.when(pl.program_id(0) == 0)
        def _():
            carry[...] = jnp.zeros_like(carry)
            db_ref[...] = jnp.zeros_like(db_ref)

        dc = q_ref[...] - k_ref[...]
        rc = jnp.dot(_tri(t, True), dc, precision=lax.Precision.HIGHEST, preferred_element_type=F32) + carry[...]
        carry[...] = rc[0:1, :]
        df = rc * (1.0 - jax.nn.sigmoid(f_ref[...] + b_ref[...]))
        df_ref[...] = df.astype(BF16)
        db_ref[...] += jnp.sum(df, axis=0, keepdims=True)

    rev = pl.BlockSpec((t, w), lambda i: (nt - 1 - i, 0))
    one = pl.BlockSpec((1, w), lambda i: (0, 0))
    return pl.pallas_call(
        body, name=name, grid=(nt,), in_specs=[rev, rev, rev, one], out_specs=[rev, one],
        out_shape=[jax.ShapeDtypeStruct((s, w), BF16), jax.ShapeDtypeStruct((1, w), F32)],
        scratch_shapes=[pltpu.VMEM((1, w), F32)], compiler_params=_cparams("arbitrary"),
    )(dcq, dck, f_logit, b_row)


def _head_col(cum_tile, h):
    lane = lax.broadcasted_iota(jnp.int32, cum_tile.shape, 1)
    return jnp.sum(jnp.where(lane == h, cum_tile, 0.0), axis=1, keepdims=True)


def _attn_tiles(s):
    return _tile(s, 512, LANES)


def _exp2_rows(sc, sub):
    return jnp.concatenate([jnp.exp2(sc[:, b * LANES:(b + 1) * LANES] - sub) for b in range(sc.shape[1] // LANES)], axis=1)


def _row_of(rep):
    return jnp.transpose(rep)[0:1, :]


def _causal(sc, keys_on_rows):
    r = lax.broadcasted_iota(jnp.int32, sc.shape, 0)
    c = lax.broadcasted_iota(jnp.int32, sc.shape, 1)
    return jnp.where((r <= c) if keys_on_rows else (c <= r), sc, NEG_INF)


def _fox_fwd(q2, kv, cum2_t, z, name):
    s, w = q2.shape
    nh = w // HEAD_DIM
    tq = _attn_tiles(s)
    nq = s // tq
    nt = (_DOT_DIMS["nt"], ((), ()))

    def body(q_ref, k_ref, v_ref, ct_ref, z_ref, o_ref, oz_ref, lse_ref, lse_row_ref, m_s, acc_s, vaug, s_buf):
        i = pl.program_id(1)

        @pl.when(i == 0)
        def _():
            vaug[:, :HEAD_DIM] = v_ref[...]
            vaug[:, HEAD_DIM:] = jnp.ones((s, LANES), BF16)

        qb = q_ref[...]
        m_s[...] = jnp.full_like(m_s, NEG_INF)
        acc_s[...] = jnp.zeros_like(acc_s)

        def scores(j):
            off = pl.multiple_of(j * tq, tq)
            return lax.dot_general(qb, k_ref[pl.ds(off, tq), :], nt, preferred_element_type=F32) - ct_ref[:, pl.ds(off, tq)]

        def softmax_pv(j, sc):
            m_old = m_s[...]
            m_new = jnp.maximum(m_old, jnp.max(sc, axis=1, keepdims=True))
            p = _exp2_rows(sc, m_new)
            alpha = jnp.exp2(m_old - m_new)
            pv = jnp.dot(p.astype(BF16), vaug[pl.ds(pl.multiple_of(j * tq, tq), tq), :], preferred_element_type=F32)
            acc_s[...] = jnp.concatenate([alpha, alpha], axis=1) * acc_s[...] + pv
            m_s[...] = m_new

        s_buf[...] = scores(0)

        def loop(j, carry):
            nxt = scores(j + 1)
            softmax_pv(j, s_buf[...])
            s_buf[...] = nxt
            return carry

        lax.fori_loop(0, i, loop, 0)
        softmax_pv(i, _causal(s_buf[...], False))
        l = acc_s[:, HEAD_DIM:]
        o = acc_s[:, :HEAD_DIM] / l
        o_ref[...] = o
        oz_ref[...] = (o * _silu(z_ref[...])).astype(BF16)
        lse = m_s[...] + jnp.log(l) * LOG2E
        lse_ref[...] = lse
        lse_row_ref[...] = _row_of(lse)

    return pl.pallas_call(
        body, name=name, grid=(nh, nq),
        in_specs=[pl.BlockSpec((tq, HEAD_DIM), lambda h, i: (i, h)),
                  pl.BlockSpec((s, HEAD_DIM), lambda h, i: (0, h)),
                  pl.BlockSpec((s, HEAD_DIM), lambda h, i: (0, nh + h)),
                  pl.BlockSpec((None, 1, s), lambda h, i: (h, 0, 0)),
                  pl.BlockSpec((tq, HEAD_DIM), lambda h, i: (i, h))],
        out_specs=[pl.BlockSpec((tq, HEAD_DIM), lambda h, i: (i, h)),
                   pl.BlockSpec((tq, HEAD_DIM), lambda h, i: (i, h)),
                   pl.BlockSpec((None, tq, LANES), lambda h, i: (h, i, 0)),
                   pl.BlockSpec((None, 1, tq), lambda h, i: (h, 0, i))],
        out_shape=[jax.ShapeDtypeStruct((s, w), F32), jax.ShapeDtypeStruct((s, w), BF16),
                   jax.ShapeDtypeStruct((nh, s, LANES), F32), jax.ShapeDtypeStruct((nh, 1, s), F32)],
        scratch_shapes=[pltpu.VMEM((tq, LANES), F32), pltpu.VMEM((tq, HEAD_DIM + LANES), F32),
                        pltpu.VMEM((s, HEAD_DIM + LANES), BF16), pltpu.VMEM((tq, tq), F32)],
        compiler_params=_cparams("arbitrary", "arbitrary"),
    )(q2, kv, kv, cum2_t, z)


def _fox_bwd_dq(q2, kv, do, o, lse2, cum2_t, dqz, name):
    s, w = q2.shape
    nh = w // HEAD_DIM
    tq = _attn_tiles(s)
    nq = s // tq
    scale = HEAD_DIM ** -0.5
    nt = (_DOT_DIMS["nt"], ((), ()))

    def body(q_ref, k_ref, v_ref, do_ref, o_ref, lse_ref, ct_ref, _, dq_ref, dl_ref, dcq_ref, acc_s, dc_s):
        i = pl.program_id(1)
        qb = q_ref[...]
        dob = do_ref[...]
        lse = lse_ref[...]
        delta = jnp.broadcast_to(jnp.sum(dob.astype(F32) * o_ref[...], axis=1, keepdims=True), (tq, LANES))
        acc_s[...] = jnp.zeros_like(acc_s)
        dc_s[...] = jnp.zeros_like(dc_s)

        def tile(j, masked):
            off = pl.multiple_of(j * tq, tq)
            kb = k_ref[pl.ds(off, tq), :]
            sc = lax.dot_general(qb, kb, nt, preferred_element_type=F32) - ct_ref[:, pl.ds(off, tq)]
            if masked:
                sc = _causal(sc, False)
            p = _exp2_rows(sc, lse)
            dp = lax.dot_general(dob, v_ref[pl.ds(off, tq), :], nt, preferred_element_type=F32)
            ds = p * (dp - jnp.concatenate([delta] * (tq // LANES), axis=1))
            acc_s[...] += jnp.dot(ds.astype(BF16), kb, preferred_element_type=F32)
            part = ds[:, :LANES]
            for b in range(1, tq // LANES):
                part = part + ds[:, b * LANES:(b + 1) * LANES]
            dc_s[...] += part

        def loop(j, carry):
            tile(j, False)
            return carry

        lax.fori_loop(0, i, loop, 0)
        tile(i, True)
        dq_ref[...] = (acc_s[...] * scale).astype(BF16)
        dl_ref[...] = _row_of(delta)
        dcq_ref[...] = jnp.sum(jnp.transpose(dc_s[...]), axis=0, keepdims=True)

    qspec = pl.BlockSpec((tq, HEAD_DIM), lambda h, i: (i, h))
    rep = pl.BlockSpec((None, tq, LANES), lambda h, i: (h, i, 0))
    rowspec = pl.BlockSpec((None, 1, tq), lambda h, i: (h, 0, i))
    return pl.pallas_call(
        body, name=name, grid=(nh, nq),
        in_specs=[qspec,
                  pl.BlockSpec((s, HEAD_DIM), lambda h, i: (0, h)),
                  pl.BlockSpec((s, HEAD_DIM), lambda h, i: (0, nh + h)),
                  qspec, qspec, rep,
                  pl.BlockSpec((None, 1, s), lambda h, i: (h, 0, 0)),
                  pl.BlockSpec(memory_space=pl.ANY)],
        out_specs=[qspec, rowspec, rowspec],
        out_shape=[jax.ShapeDtypeStruct(dqz.shape, BF16), jax.ShapeDtypeStruct((nh, 1, s), F32),
                   jax.ShapeDtypeStruct((nh, 1, s), F32)],
        input_output_aliases={7: 0},
        scratch_shapes=[pltpu.VMEM((tq, HEAD_DIM), F32), pltpu.VMEM((tq, LANES), F32)],
        compiler_params=_cparams("parallel", "arbitrary"),
    )(q2, kv, kv, do, o, lse2, cum2_t, dqz)


def _fox_bwd_dkv(q2, kv, do, lse2_t, delta_t, cum2, name):
    s, w = q2.shape
    nh = w // HEAD_DIM
    tk = _attn_tiles(s)
    nk = s // tk
    nt = (_DOT_DIMS["nt"], ((), ()))

    def body(q_ref, k_ref, v_ref, do_ref, lse_ref, dl_ref, c_ref, dk_ref, dv_ref, dck_ref, dk_s, dv_s, dc_s, s_buf, dp_buf):
        h, j = pl.program_id(0), pl.program_id(1)
        kb = k_ref[...]
        vb = v_ref[...]
        ck = jnp.broadcast_to(_head_col(c_ref[...], h), (tk, LANES))
        dk_s[...] = jnp.zeros_like(dk_s)
        dv_s[...] = jnp.zeros_like(dv_s)
        dc_s[...] = jnp.zeros_like(dc_s)

        def scores(i):
            off = pl.multiple_of(i * tk, tk)
            sc = lax.dot_general(kb, q_ref[pl.ds(off, tk), :], nt, preferred_element_type=F32) - lse_ref[:, pl.ds(off, tk)]
            dp = lax.dot_general(vb, do_ref[pl.ds(off, tk), :], nt, preferred_element_type=F32) - dl_ref[:, pl.ds(off, tk)]
            return sc, dp

        def accumulate(i, sc, dp):
            off = pl.multiple_of(i * tk, tk)
            p = _exp2_rows(sc, ck)
            dv_s[...] += jnp.dot(p.astype(BF16), do_ref[pl.ds(off, tk), :], preferred_element_type=F32)
            ds = p * dp
            dk_s[...] += jnp.dot(ds.astype(BF16), q_ref[pl.ds(off, tk), :], preferred_element_type=F32)
            part = ds[:, :LANES]
            for b in range(1, tk // LANES):
                part = part + ds[:, b * LANES:(b + 1) * LANES]
            dc_s[...] += part

        sc0, dp0 = scores(j)
        s_buf[...] = _causal(sc0, True)
        dp_buf[...] = dp0

        def loop(i, carry):
            nxt = scores(i + 1)
            accumulate(i, s_buf[...], dp_buf[...])
            s_buf[...], dp_buf[...] = nxt
            return carry

        lax.fori_loop(j, nk - 1, loop, 0)
        accumulate(nk - 1, s_buf[...], dp_buf[...])
        dk_ref[...] = (dk_s[...] * (1.0 / LOG2E)).astype(BF16)
        dv_ref[...] = dv_s[...].astype(BF16)
        dck_ref[...] = jnp.sum(jnp.transpose(dc_s[...]), axis=0, keepdims=True)

    col = pl.BlockSpec((s, HEAD_DIM), lambda h, j: (0, h))
    row = pl.BlockSpec((None, 1, s), lambda h, j: (h, 0, 0))
    kspec = pl.BlockSpec((tk, HEAD_DIM), lambda h, j: (j, h))
    return pl.pallas_call(
        body, name=name, grid=(nh, nk),
        in_specs=[col, kspec, pl.BlockSpec((tk, HEAD_DIM), lambda h, j: (j, nh + h)), col, row, row,
                  pl.BlockSpec((tk, LANES), lambda h, j: (j, 0))],
        out_specs=[kspec, kspec, pl.BlockSpec((None, 1, tk), lambda h, j: (h, 0, j))],
        out_shape=[jax.ShapeDtypeStruct((s, w), BF16), jax.ShapeDtypeStruct((s, w), BF16),
                   jax.ShapeDtypeStruct((nh, 1, s), F32)],
        scratch_shapes=[pltpu.VMEM((tk, HEAD_DIM), F32), pltpu.VMEM((tk, HEAD_DIM), F32),
                        pltpu.VMEM((tk, LANES), F32), pltpu.VMEM((tk, tk), F32), pltpu.VMEM((tk, tk), F32)],
        compiler_params=_cparams("parallel", "arbitrary"),
    )(q2, kv, kv, do, lse2_t, delta_t, cum2)


def _exchange_copies(ins, outs, send_sems, recv_sems, local_sems, scatter):
    x, y, c = (lax.axis_index(a) for a in MESH_AXES)
    me = 4 * x + 2 * y + c
    local, remote = [], []
    for a in range(len(ins)):
        local.append(pltpu.make_async_copy(ins[a].at[me] if scatter else ins[a], outs[a].at[me], local_sems.at[a]))
        for k in range(1, N_DEV):
            px, py, pc = (1 - x if k & 4 else x), (1 - y if k & 2 else y), (1 - c if k & 1 else c)
            remote.append(pltpu.make_async_remote_copy(
                src_ref=ins[a].at[4 * px + 2 * py + pc] if scatter else ins[a], dst_ref=outs[a].at[me],
                send_sem=send_sems.at[a * (N_DEV - 1) + k - 1], recv_sem=recv_sems.at[a * (N_DEV - 1) + k - 1],
                device_id=(px, py, pc), device_id_type=pl.DeviceIdType.MESH))
    return local, remote


def _exchange_out_shapes(arrs, scatter):
    return [((N_DEV,) + a.shape[1:]) if scatter else ((N_DEV,) + a.shape) for a in arrs]


def _exchange(arrs, scatter, name):
    n = len(arrs)

    def body(*refs):
        local, remote = _exchange_copies(refs[:n], refs[n:2 * n], *refs[2 * n:], scatter)
        for cp in local + remote:
            cp.start()
        for cp in remote:
            cp.wait_send()
            cp.wait_recv()
        for cp in local:
            cp.wait()

    out_shape = [jax.ShapeDtypeStruct(s, a.dtype) for s, a in zip(_exchange_out_shapes(arrs, scatter), arrs)]
    return pl.pallas_call(
        body, name=name, out_shape=out_shape,
        in_specs=[pl.BlockSpec(memory_space=pl.ANY)] * n, out_specs=[pl.BlockSpec(memory_space=pl.ANY)] * n,
        scratch_shapes=[pltpu.SemaphoreType.DMA((n * (N_DEV - 1),)), pltpu.SemaphoreType.DMA((n * (N_DEV - 1),)),
                        pltpu.SemaphoreType.DMA((n,))],
    )(*arrs)


_HBM = pl.BlockSpec(memory_space=pltpu.HBM)
_SEM = pl.BlockSpec(memory_space=pltpu.SEMAPHORE)


def _exchange_start(arrs, scatter, name):
    n = len(arrs)
    lands = [lax.empty(s, a.dtype) for s, a in zip(_exchange_out_shapes(arrs, scatter), arrs)]

    def body(*refs):
        ins, outs = refs[:n], refs[n:2 * n]
        send_sems, recv_sems, local_sems = refs[2 * n:2 * n + 3]
        token = refs[-1]
        local, remote = _exchange_copies(ins, outs, send_sems, recv_sems, local_sems, scatter)
        for cp in local + remote:
            cp.start()
        token[...] = jnp.zeros_like(token)

    hbm = lambda a: pltpu.HBM(a.shape, a.dtype)
    res = pl.pallas_call(
        body, name=name,
        out_shape=(pltpu.SemaphoreType.DMA((n * (N_DEV - 1),)), pltpu.SemaphoreType.DMA((n * (N_DEV - 1),)),
                   pltpu.SemaphoreType.DMA((n,)), *[hbm(a) for a in arrs], *[hbm(a) for a in lands],
                   jax.ShapeDtypeStruct((SUBLANES, LANES), F32)),
        in_specs=[_HBM] * (2 * n),
        out_specs=(_SEM, _SEM, _SEM, *[_HBM] * (2 * n), pl.BlockSpec(memory_space=pltpu.VMEM)),
        input_output_aliases={i: 3 + i for i in range(2 * n)},
        compiler_params=pltpu.CompilerParams(has_side_effects=pltpu.SideEffectType.DATAFLOW_SIDE_EFFECTING),
    )(*[pltpu.with_memory_space_constraint(a, pltpu.HBM) for a in list(arrs) + lands])
    return (n, scatter, res[:3], res[3:3 + n], res[3 + n:3 + 2 * n]), res[-1]


def _exchange_wait(state, after, name):
    n, scatter, sems, srcs, lands = state

    def body(*refs):
        ins, outs = refs[:n], refs[n:2 * n]
        send_sems, recv_sems, local_sems = refs[2 * n:2 * n + 3]
        local, remote = _exchange_copies(ins, outs, send_sems, recv_sems, local_sems, scatter)
        for cp in remote:
            cp.wait_send()
            cp.wait_recv()
        for cp in local:
            cp.wait()

    hbm = lambda a: pltpu.HBM(a.shape, a.dtype)
    res = pl.pallas_call(
        body, name=name,
        out_shape=(*[hbm(a) for a in srcs], *[hbm(a) for a in lands]),
        in_specs=[_HBM] * (2 * n) + [_SEM] * 3 + [pl.BlockSpec(memory_space=pl.ANY)],
        out_specs=tuple([_HBM] * (2 * n)),
        input_output_aliases={i: i for i in range(2 * n)},
        compiler_params=pltpu.CompilerParams(has_side_effects=pltpu.SideEffectType.DATAFLOW_SIDE_EFFECTING),
    )(*srcs, *lands, *sems, after)
    return list(res[n:])


def _adamw_math(w, g, m, v):
    m = ADAM_B1 * m + (1.0 - ADAM_B1) * g
    v = ADAM_B2 * v + (1.0 - ADAM_B2) * (g * g)
    m_hat = m / (1.0 - ADAM_B1 ** ADAM_STEP)
    v_hat = v / (1.0 - ADAM_B2 ** ADAM_STEP)
    return -ADAM_LR * (m_hat / (jnp.sqrt(v_hat) + ADAM_EPS) + ADAM_WD * w), m, v


def _slot_sum(p_ref):
    g = p_ref[0].astype(F32)
    for d in range(1, p_ref.shape[0]):
        g = g + p_ref[d].astype(F32)
    return g


def _adamw_tile(r, c):
    return _tile(r, max(SUBLANES, (256 * 1024) // c // SUBLANES * SUBLANES), SUBLANES)


def _adamw(parts, w, m, v, name):
    r, c = w.shape
    tr = _adamw_tile(r, c)

    def body(p_ref, w_ref, m_ref, v_ref, g_ref, d_ref, nm_ref, nv_ref):
        g = _slot_sum(p_ref)
        g_ref[...] = g
        d_ref[...], nm_ref[...], nv_ref[...] = _adamw_math(w_ref[...], g, m_ref[...], v_ref[...])

    blk = pl.BlockSpec((tr, c), lambda i: (i, 0))
    sh = jax.ShapeDtypeStruct((r, c), F32)
    return pl.pallas_call(
        body, name=name, grid=(r // tr,),
        in_specs=[pl.BlockSpec((parts.shape[0], tr, c), lambda i: (0, i, 0)), blk, blk, blk],
        out_specs=[blk] * 4, out_shape=[sh] * 4, compiler_params=_cparams("parallel"),
    )(parts, w, m, v)


def _sum_parts(parts, name):
    _, r, c = parts.shape
    tr = _adamw_tile(r, c)

    def body(p_ref, o_ref):
        o_ref[...] = _slot_sum(p_ref)

    return pl.pallas_call(
        body, name=name, grid=(r // tr,),
        in_specs=[pl.BlockSpec((parts.shape[0], tr, c), lambda i: (0, i, 0))],
        out_specs=pl.BlockSpec((tr, c), lambda i: (i, 0)), out_shape=jax.ShapeDtypeStruct((r, c), F32),
        compiler_params=_cparams("parallel"),
    )(parts)


def _perm(a):
    s, d = a.shape
    return a.reshape(N_SEG, s // N_SEG, d).transpose(1, 0, 2).reshape(s, d)


def _unperm(a):
    s, d = a.shape
    return a.reshape(s // N_SEG, N_SEG, d).transpose(1, 0, 2).reshape(s, d)


def _lane_pad(a, width=LANES):
    return jnp.pad(a, ((0, 0), (0, width - a.shape[1])))


def _local_step(x, target, norm_pre, norm_post, kv_norm, kv_b_f, a_re, a_im, log_dt, b_re, b_im, c_re, c_im, comm):
    s, d = x.shape
    g, p = a_re.shape
    w = g * S5_GROUP
    fw = d
    nh = fw // HEAD_DIM
    seg_len = s // N_SEG
    row = lambda v: v.reshape(1, -1)
    g_pre0, g_pre1, g_post0, g_post1, g_kv = row(norm_pre[0]), row(norm_pre[1]), row(norm_post[0]), row(norm_post[1]), row(kv_norm)

    ldt = log_dt.reshape(g, 1)
    abr, abi, cr, ci = _s5_disc_fwd(a_re, a_im, ldt)
    cr_col, ci_col = cr.reshape(g * p, 1), ci.reshape(g * p, 1)
    b_re2, b_im2 = b_re.reshape(g * p, S5_GROUP), b_im.reshape(g * p, S5_GROUP)
    bb_re, bb_im = _s5_bbar_fwd(cr_col, ci_col, b_re2, b_im2)
    bd_re = _block_diag_in(bb_re.reshape(g, p, S5_GROUP)).astype(BF16)
    bd_im = _block_diag_in(bb_im.reshape(g, p, S5_GROUP)).astype(BF16)
    cd_re = _block_diag_out(c_re).astype(BF16)
    cd_im = _block_diag_out(-c_im).astype(BF16)
    ab_re = jnp.broadcast_to(abr.reshape(1, g * p), (N_SEG, g * p))
    ab_im = jnp.broadcast_to(abi.reshape(1, g * p), (N_SEG, g * p))
    zero_seg = jnp.zeros((N_SEG, g * p), F32)

    xn0 = _norm_cast(x, g_pre0 + comm.token, "norm_pre0", x_kind="nat")
    w_in = comm.weight("s5_w_in", xn0)
    d_row, bglu_row = row(comm.vector("s5_d")), row(comm.vector("s5_b_glu"))
    u = _mm(xn0, w_in, "nn", F32, "s5_in_u", b_cols=(0, w))
    z0 = _mm(xn0, w_in, "nn", F32, "s5_in_z", b_cols=(w, w))
    e_re, e_im = _s5_scan_fwd(u, bd_re, bd_im, cd_re, cd_im, ab_re, ab_im, zero_seg, zero_seg, d_row, False, "s5_scan_ends")
    i_re, i_im = _s5_seg_fix(e_re, e_im, ab_re, ab_im, seg_len, False, "s5_seg_fix")
    y_ssm, yg, h_re, h_im, _, _ = _s5_scan_fwd(u, bd_re, bd_im, cd_re, cd_im, ab_re, ab_im, i_re, i_im, d_row, True, "s5_scan")
    w_glu, w_out = comm.weight("s5_w_glu", yg), comm.weight("s5_w_out", yg)
    gp = _mm(yg, w_glu, "nn", F32, "s5_glu")
    y3 = _s5_gate(y_ssm, gp, bglu_row, z0, "s5_gate")
    o0 = _mm(y3, w_out, "nn", F32, "s5_out")
    r0 = _post_norm(o0, g_post0, "norm_post0", out_kind="nat")

    h1, hn_kv, xn1 = _resid_norm2(x, r0, g_kv, g_pre1, "resid_norms")
    w_kv, fw_in, fw_out = comm.weight("kv_w", hn_kv), comm.weight("fox_w_in", hn_kv), comm.weight("fox_w_out", hn_kv)
    w_f =_lane_pad(w_kv[:, 2 * fw:])
    kv = _mm(hn_kv, w_kv, "nn", BF16, "kv_proj", b_cols=(0, 2 * fw))
    f_logit = _mm(hn_kv, w_f, "nn", F32, "f_proj")
    bf_row = _lane_pad(row(kv_b_f))
    cum2 = _cum_fwd(f_logit, bf_row, "cum_fwd")
    cum2_t = cum2[:, :nh].T.reshape(nh, 1, s)
    q2 = _mm(xn1, fw_in, "nn", BF16, "fox_q", scale=HEAD_DIM ** -0.5 * LOG2E, b_cols=(0, fw))
    z1 = _mm(xn1, fw_in, "nn", F32, "fox_z", b_cols=(fw, fw))
    o, oz, lse2, lse2_t = _fox_fwd(q2, kv, cum2_t, z1, "fox_fwd")
    o1 = _mm(oz, fw_out, "nn", F32, "fox_out")
    dh2, sq = _post_norm_loss(o1, g_post1, h1, target, "norm_post1_loss")
    loss = 0.5 * jnp.sum(sq) / d

    do1, dg_post1 = _post_norm_bwd(dh2, o1, g_post1, "norm_post1_bwd")
    d_fw_out = _mm(oz, do1, "tn", F32, "fox_out_dw")
    d_oz = _mm(do1, fw_out, "nt", F32, "fox_out_dx")
    do, dqz = _gate_bwd(d_oz, o, z1, "fox_gate_bwd")
    dqz, delta_t, dcq = _fox_bwd_dq(q2, kv, do, o, lse2, cum2_t, dqz, "fox_bwd_dq")
    dk, dv, dck = _fox_bwd_dkv(q2, kv, do, lse2_t, delta_t, cum2, "fox_bwd_dkv")
    d_fw_in = _mm(xn1, dqz, "tn", F32, "fox_in_dw")
    dxn1 = _mm(dqz, fw_in, "nt", F32, "fox_in_dx")
    dcq_sl = _lane_pad(dcq.reshape(nh, s).T)
    dck_sl = _lane_pad(dck.reshape(nh, s).T)
    df, db_f = _cum_bwd(dcq_sl, dck_sl, f_logit, bf_row, "cum_bwd")
    dkv = _concat_cast(dk, dv, "fox_dkv")
    d_w_kvm = _mm(hn_kv, dkv, "tn", F32, "kv_dw")
    d_w_f = _mm(hn_kv, df, "tn", F32, "f_dw")
    dhn_f = _mm(df, w_f, "nt", F32, "f_dx")
    dhn_kv = _mm(dkv, w_kv, "nt", F32, "kv_dx", add=dhn_f, b_cols=(0, 2 * fw))
    d_w_kv = jnp.concatenate([d_w_kvm, d_w_f[:, :nh]], axis=1)
    tok = comm.send_grads(dict(fox_w_out=d_fw_out, fox_w_in=d_fw_in, kv_w=d_w_kv), "exchange_fox")
    dh1, dg_pre1, dg_kv = _norm_bwd2(dh2, h1, dxn1, dhn_kv, g_pre1, g_kv, "resid_norms_bwd")

    do0, dg_post0 = _post_norm_bwd(dh1, o0, g_post0 + tok, "norm_post0_bwd", dy_kind="nat")
    d_w_out = _mm(y3, do0, "tn", F32, "s5_out_dw")
    dy3 = _mm(do0, w_out, "nt", F32, "s5_out_dx")
    duz, dgp, dyg_direct, db_glu = _s5_gate_bwd(dy3, y_ssm, gp, bglu_row, z0, "s5_gate_bwd")
    d_w_glu = _mm(yg, dgp, "tn", F32, "s5_glu_dw")
    dyg = _mm(dgp, w_glu, "nt", F32, "s5_glu_dx", add=dyg_direct)
    dy_ssm = _gelu_bwd(dyg, y_ssm, "s5_gelu_bwd")
    d_row = d_row + comm.send_grads(dict(s5_w_out=d_w_out, s5_w_glu=d_w_glu), "exchange_s5")
    ab_imn = -ab_im
    ge_re, ge_im = _s5_scan_bwd(dy_ssm, u, h_re, h_im, bd_re, bd_im, cd_re, cd_im, ab_re, ab_imn, zero_seg, zero_seg,
                                d_row, False, "s5_adj_ends")
    gi_re, gi_im = _s5_seg_fix(ge_re, ge_im, ab_re, ab_imn, seg_len, True, "s5_adj_fix")
    duz, dbd_re, dbd_im, dcd_re, dcd_im, dab_re, dab_im, dd = _s5_scan_bwd(
        dy_ssm, u, h_re, h_im, bd_re, bd_im, cd_re, cd_im, ab_re, ab_imn, gi_re, gi_im, d_row, True, "s5_adj", duz=duz)
    d_w_in = _mm(xn0, duz, "tn", F32, "s5_in_dw")
    comm.send_grads(dict(s5_w_in=d_w_in), "exchange_s5_in")
    dxn0 =_mm(duz, w_in, "nt", F32, "s5_in_dx")
    grad_x, dg_pre0 = _norm_bwd1(dh1, x, dxn0, g_pre0, "norm_pre0_bwd")

    dbb_re = _block_diag_in_extract(dbd_re, p, S5_GROUP).reshape(g * p, S5_GROUP)
    dbb_im = _block_diag_in_extract(dbd_im, p, S5_GROUP).reshape(g * p, S5_GROUP)
    dcr_col, dci_col, db_re, db_im = _s5_bbar_bwd(cr_col, ci_col, b_re2, b_im2, dbb_re, dbb_im)
    da_re, da_im, dldt = _s5_disc_bwd(a_re, a_im, ldt, dab_re.reshape(g, p), dab_im.reshape(g, p),
                                      dcr_col.reshape(g, p), dci_col.reshape(g, p))
    dc_re = _block_diag_out_extract(dcd_re, S5_GROUP, p)
    dc_im = -_block_diag_out_extract(dcd_im, S5_GROUP, p)

    small = dict(
        norm_pre=jnp.concatenate([dg_pre0, dg_pre1], axis=0), norm_post=jnp.concatenate([dg_post0, dg_post1], axis=0),
        s5_a_re=da_re, s5_a_im=da_im, s5_log_dt=dldt.reshape(g), s5_b_re=db_re.reshape(g, p, S5_GROUP),
        s5_b_im=db_im.reshape(g, p, S5_GROUP), s5_c_re=dc_re, s5_c_im=dc_im, s5_d=dd.reshape(-1),
        s5_b_glu=db_glu.reshape(-1), kv_norm=dg_kv.reshape(-1), kv_b_f=db_f[0, :nh])
    return loss, grad_x, small


_BIG = ("s5_w_in", "s5_w_glu", "s5_w_out", "kv_w", "fox_w_in", "fox_w_out")
_COL_SHARDED = ("s5_w_in", "kv_w", "fox_w_in")
_SMALL = ("norm_pre", "norm_post", "s5_a_re", "s5_a_im", "s5_log_dt", "s5_b_re", "s5_b_im", "s5_c_re", "s5_c_im",
          "s5_d", "s5_b_glu", "kv_norm", "kv_b_f")
_SMALL_SHARDED = ("s5_d", "s5_b_glu")
_PACK_QUANTUM = SUBLANES * LANES
_WEIGHTS = ('norm_pre', 'norm_post', 's5_w_in', 's5_a_re', 's5_a_im', 's5_log_dt', 's5_b_re', 's5_b_im', 's5_c_re', 's5_c_im',
            's5_d', 's5_w_glu', 's5_b_glu', 's5_w_out', 'kv_norm', 'kv_w', 'kv_b_f', 'fox_w_in', 'fox_w_out')


def _full_from_slots(name, slots):
    n, r, c = slots.shape
    if name in _COL_SHARDED:
        return slots.transpose(1, 0, 2).reshape(r, n * c)
    return slots.reshape(n * r, c)


def _slots_from_full(name, full):
    if name in _COL_SHARDED:
        r, nc = full.shape
        return full.reshape(r, N_DEV, nc // N_DEV).transpose(1, 0, 2)
    nr, c = full.shape
    return full.reshape(N_DEV, nr // N_DEV, c)


def _pack(vals):
    parts = []
    for v in vals:
        flat = v.reshape(-1)
        parts.append(jnp.pad(flat, (0, (-flat.shape[0]) % _PACK_QUANTUM)))
    total = sum(p.shape[0] for p in parts)
    parts.append(jnp.zeros(((-total) % (N_DEV * _PACK_QUANTUM),), F32))
    return jnp.concatenate(parts).reshape(-1, LANES)


def _unpack(packed, shapes):
    flat = packed.reshape(-1)
    out, off = [], 0
    for sh in shapes:
        n = math.prod(sh)
        out.append(flat[off:off + n].reshape(sh))
        off += n + (-n) % _PACK_QUANTUM
    return out


class _Comm:
    _GROUPS = (("s5_w_in",) + _SMALL_SHARDED, ("s5_w_glu", "s5_w_out"), ("kv_w", "fox_w_in", "fox_w_out"))

    def __init__(self, shards, vectors):
        shards = {**shards, **vectors}
        self._full, self._gathers = {}, {}
        self.token = jnp.zeros((), F32)
        for group in self._GROUPS:
            state, tok = _exchange_start([shards[n] for n in group], False, "gather_start_" + group[0])
            self._gathers[group] = state
            self.token = self.token + tok[0, 0]
        self._sent = []

    def vector(self, name):
        return self._full[name]

    def weight(self, name, after):
        if name not in self._full:
            group = next(g for g in self._GROUPS if name in g)
            slots = _exchange_wait(self._gathers.pop(group), after, "gather_wait_" + group[0])
            for n, sl in zip(group, slots):
                self._full[n] = sl.reshape(-1) if n in _SMALL_SHARDED else _full_from_slots(n, sl)
        return self._full[name]

    def send_grads(self, grads, name):
        names = list(grads)
        state, tok = _exchange_start([_slots_from_full(n, grads[n]).astype(BF16) for n in names], True, name + "_start")
        self._sent.append((names, state, name + "_wait"))
        return tok[0, 0]

    def received_grads(self, after):
        for names, state, name in self._sent:
            for n, recv in zip(names, _exchange_wait(state, after, name)):
                yield n, recv


def kernel(x, norm_pre, norm_post, s5_w_in, s5_a_re, s5_a_im, s5_log_dt, s5_b_re, s5_b_im, s5_c_re, s5_c_im, s5_d, s5_w_glu, s5_b_glu, s5_w_out, kv_norm, kv_w, kv_b_f, fox_w_in, fox_w_out, loss_target, m_norm_pre, m_norm_post, m_s5_w_in, m_s5_a_re, m_s5_a_im, m_s5_log_dt, m_s5_b_re, m_s5_b_im, m_s5_c_re, m_s5_c_im, m_s5_d, m_s5_w_glu, m_s5_b_glu, m_s5_w_out, m_kv_norm, m_kv_w, m_kv_b_f, m_fox_w_in, m_fox_w_out, v_norm_pre, v_norm_post, v_s5_w_in, v_s5_a_re, v_s5_a_im, v_s5_log_dt, v_s5_b_re, v_s5_b_im, v_s5_c_re, v_s5_c_im, v_s5_d, v_s5_w_glu, v_s5_b_glu, v_s5_w_out, v_kv_norm, v_kv_w, v_kv_b_f, v_fox_w_in, v_fox_w_out):
    env = dict(locals())
    wts = {n: env[n] for n in _WEIGHTS}
    mom = {n: env["m_" + n] for n in _WEIGHTS}
    var = {n: env["v_" + n] for n in _WEIGHTS}
    me = 4 * lax.axis_index("x") + 2 * lax.axis_index("y") + lax.axis_index("c")
    shard2d = {n: wts[n].reshape(wts[n].shape[-2:]) for n in _BIG}
    comm = _Comm({n: shard2d[n].astype(BF16) for n in _BIG}, {n: wts[n].reshape(1, -1) for n in _SMALL_SHARDED})

    loss_local, grad_x, small = _local_step(
        x[0], loss_target[0], norm_pre, norm_post, kv_norm, kv_b_f, s5_a_re[0], s5_a_im[0], s5_log_dt[0],
        s5_b_re[0], s5_b_im[0], s5_c_re[0], s5_c_im[0], comm)
    loss = lax.psum(loss_local, MESH_AXES)

    small_pack = _pack([small[n] for n in _SMALL])
    slice_rows = small_pack.shape[0] // N_DEV
    small_state, small_tok = _exchange_start([small_pack.reshape(N_DEV, slice_rows, LANES)], True, "reduce_small_start")

    res = {}
    for n, recv in comm.received_grads(small_tok):
        outs = _adamw(recv, shard2d[n], mom[n].reshape(shard2d[n].shape), var[n].reshape(shard2d[n].shape), "adamw_" + n)
        res[n] = [o.reshape(wts[n].shape) for o in outs]

    full_shape = {n: (small[n].shape if n in _SMALL_SHARDED else wts[n].shape) for n in _SMALL}

    def spread(n, v):
        if n not in _SMALL_SHARDED:
            return v
        flat = v.reshape(-1)
        return lax.dynamic_update_slice(jnp.zeros(full_shape[n], F32), flat, (me * flat.shape[0],))

    my_sum = _sum_parts(_exchange_wait(small_state, res[_BIG[0]][0], "reduce_small_wait")[0], "sum_small")
    g_all = _exchange([my_sum], False, "gather_small")[0].reshape(1, small_pack.shape[0], LANES)
    packed =[_pack([spread(n, src[n]) for n in _SMALL]) for src in (wts, mom, var)]
    outs = _adamw(g_all, *packed, "adamw_small")
    unpacked = [_unpack(o, [full_shape[n] for n in _SMALL]) for o in outs]
    for i, n in enumerate(_SMALL):
        vals = [u[i] for u in unpacked]
        if n in _SMALL_SHARDED:
            k = wts[n].size
            vals = [lax.dynamic_slice(v, (me * k,), (k,)) for v in vals]
        res[n] = [v.reshape(wts[n].shape) for v in vals]

    return (loss, grad_x[None], *[res[n][0] for n in _WEIGHTS], *[res[n][1] for n in _WEIGHTS],
            *[res[n][2] for n in _WEIGHTS], *[res[n][3] for n in _WEIGHTS])
```

```python
import functools
import math

import jax
import jax.numpy as jnp
from jax import lax
from jax.experimental import pallas as pl
from jax.experimental.pallas import tpu as pltpu

F32 = jnp.float32
BF16 = jnp.bfloat16

N_DEV = 8
MESH_AXES = ("x", "y", "c")
S5_GROUP = 16
S5_STATE = 64
LANES = 128
SUBLANES = 8
GROUPS_PER_BLOCK = LANES // S5_GROUP
BLOCK_STATE = GROUPS_PER_BLOCK * S5_STATE
N_SEG = SUBLANES
HEAD_DIM = 128
RMS_EPS = 1e-6
NEG_INF = -1e30
LOG2E = math.log2(math.e)
ADAM_LR = 0.001
ADAM_B1 = 0.9
ADAM_B2 = 0.999
ADAM_EPS = 1e-08
ADAM_WD = 0.01
ADAM_STEP = 10
VMEM_LIMIT = 56 * 1024 * 1024


def _tile(n, pref, quantum=LANES):
    if n <= pref:
        return n
    t = (pref // quantum) * quantum
    while t >= quantum:
        if n % t == 0:
            return t
        t -= quantum
    return n


def _cparams(*sem):
    return pltpu.CompilerParams(dimension_semantics=sem if sem else None, vmem_limit_bytes=VMEM_LIMIT)


_DOT_DIMS = {"nn": ((1,), (0,)), "nt": ((1,), (1,)), "tn": ((0,), (0,))}


def _mm(a, b, mode, out_dtype, name, add=None, scale=None, b_cols=None, after=None, col_slots=False):
    b_shape = b.shape if b_cols is None else (b.shape[0], b_cols[1])
    if mode == "nn":
        (M, K), (K2, N) = a.shape, b_shape
    elif mode == "nt":
        (M, K), (N, K2) = a.shape, b_shape
    else:
        (K, M), (K2, N) = a.shape, b_shape
    assert K == K2, (name, a.shape, b_shape)
    tm, tn, tk = _tile(M, 1024 if K <= 2048 else 512), (N // N_DEV if col_slots else _tile(N, 1024)), _tile(K, 4096)
    nk = K // tk
    dims = (_DOT_DIMS[mode], ((), ()))
    col0 = 0
    if b_cols is not None:
        assert mode != "tn" and b_cols[0] % (tn if mode == "nn" else tk) == 0
        col0 = b_cols[0] // (tn if mode == "nn" else tk)

    def body(*refs):
        a_ref, b_ref = refs[:2]
        c_ref = refs[2] if add is not None else None
        o_ref = refs[2 + (add is not None) + (after is not None)]
        part = lax.dot_general(a_ref[...], b_ref[...], dims, preferred_element_type=F32)

        def finish(r):
            if scale is not None:
                r = r * scale
            if add is not None:
                r = r + c_ref[...]
            o_ref[...] = r.astype(out_dtype)

        if nk == 1:
            finish(part)
            return
        acc = refs[-1]
        k = pl.program_id(2)

        @pl.when(k == 0)
        def _():
            acc[...] = part

        @pl.when(jnp.logical_and(k > 0, k < nk - 1))
        def _():
            acc[...] += part

        @pl.when(k == nk - 1)
        def _():
            finish(acc[...] + part)

    if mode == "tn":
        a_spec = pl.BlockSpec((tk, tm), lambda i, j, k: (k, i))
    else:
        a_spec = pl.BlockSpec((tm, tk), lambda i, j, k: (i, k))
    if mode == "nt":
        b_spec = pl.BlockSpec((tn, tk), lambda i, j, k: (j, k + col0))
    else:
        b_spec = pl.BlockSpec((tk, tn), lambda i, j, k: (k, j + col0))
    o_spec = pl.BlockSpec((tm, tn), lambda i, j, k: (i, j))
    in_specs = [a_spec, b_spec] + ([o_spec] if add is not None else [])
    args = (a, b) + ((add,) if add is not None else ())
    if after is not None:
        in_specs.append(pl.BlockSpec(after.shape, lambda i, j, k: (0, 0)))
        args += (after,)
    out_shape = jax.ShapeDtypeStruct((M, N), out_dtype)
    if col_slots:
        assert add is None
        o_spec = pl.BlockSpec((None, tm, tn), lambda i, j, k: (j, i, 0))
        out_shape = jax.ShapeDtypeStruct((N_DEV, M, tn), out_dtype)
    return pl.pallas_call(
        body, name=name, grid=(M // tm, N // tn, nk),
        in_specs=in_specs, out_specs=o_spec,
        out_shape=out_shape,
        scratch_shapes=[pltpu.VMEM((tm, tn), F32)] if nk > 1 else [],
        compiler_params=_cparams("parallel", "parallel", "arbitrary"),
    )(*args)


class _NatIn:
    def __init__(self, ref):
        self.ref = ref

    def __getitem__(self, idx):
        v = jnp.swapaxes(self.ref[...], 0, 1)
        return v.reshape(v.shape[0] * N_SEG, v.shape[2])


class _NatOut:
    def __init__(self, ref):
        self.ref = ref

    def __setitem__(self, idx, val):
        self.ref[...] = jnp.swapaxes(val.reshape(val.shape[0] // N_SEG, N_SEG, val.shape[1]), 0, 1)


def _rowcall(body, name, n_rows, ins, outs, tile_rows=256):
    tr = _tile(n_rows, tile_rows, SUBLANES * 2)
    n_in = len(ins)
    in_kinds = [k for _, k in ins]
    kinds = [k for _, _, k in outs]

    def kern(*refs):
        @pl.when(pl.program_id(0) == 0)
        def _():
            for r, kind in zip(refs[n_in:], kinds):
                if kind == "acc":
                    r[...] = jnp.zeros_like(r)

        wrapped = [_NatIn(r) if k == "nat" else r for r, k in zip(refs[:n_in], in_kinds)]
        wrapped += [_NatOut(r) if k == "nat" else r for r, k in zip(refs[n_in:], kinds)]
        body(*wrapped)

    in_specs, args = [], []
    for arr, kind in ins:
        if kind == "row":
            in_specs.append(pl.BlockSpec((tr, arr.shape[1]), lambda i: (i, 0)))
        elif kind == "nat":
            in_specs.append(pl.BlockSpec((N_SEG, tr // N_SEG, arr.shape[1]), lambda i: (0, i, 0)))
            arr = arr.reshape(N_SEG, n_rows // N_SEG, arr.shape[1])
        else:
            in_specs.append(pl.BlockSpec(arr.shape, lambda i, nd=arr.ndim: (0,) * nd))
        args.append(arr)
    out_specs, out_shape = [], []
    for width, dtype, kind in outs:
        if kind == "row":
            out_specs.append(pl.BlockSpec((tr, width), lambda i: (i, 0)))
            out_shape.append(jax.ShapeDtypeStruct((n_rows, width), dtype))
        elif kind == "right":
            out_specs.append(pl.BlockSpec((tr, width), lambda i: (i, 1)))
            out_shape.append(jax.ShapeDtypeStruct((n_rows, 2 * width), dtype))
        elif kind == "nat":
            out_specs.append(pl.BlockSpec((N_SEG, tr // N_SEG, width), lambda i: (0, i, 0)))
            out_shape.append(jax.ShapeDtypeStruct((N_SEG, n_rows // N_SEG, width), dtype))
        else:
            out_specs.append(pl.BlockSpec((1, width), lambda i: (0, 0)))
            out_shape.append(jax.ShapeDtypeStruct((1, width), F32))
    res = pl.pallas_call(
        kern, name=name, grid=(n_rows // tr,), in_specs=in_specs, out_specs=out_specs, out_shape=out_shape,
        compiler_params=_cparams("arbitrary"),
    )(*args)
    return [r.reshape(n_rows, r.shape[2]) if k == "nat" else r for r, k in zip(res, kinds)]


def _rstd(x):
    return lax.rsqrt(jnp.mean(x * x, axis=-1, keepdims=True) + RMS_EPS)


def _rms_bwd(x, g, dy):
    xh = x * _rstd(x)
    dxh = dy * g
    dx = _rstd(x) * (dxh - xh * jnp.mean(dxh * xh, axis=-1, keepdims=True))
    return dx, jnp.sum(dy * xh, axis=0, keepdims=True)


def _silu(z):
    return z * jax.nn.sigmoid(z)


def _norm_cast(x, g, name, x_kind="row"):
    def body(x_ref, g_ref, o_ref):
        x = x_ref[...]
        o_ref[...] = (x * _rstd(x) * g_ref[...]).astype(BF16)

    return _rowcall(body, name, x.shape[0], [(x, x_kind), (g, "full")], [(x.shape[1], BF16, "row")])[0]


def _resid_norm2(x, r0, g_kv, g_pre, name):
    def body(x_ref, r_ref, gk_ref, gp_ref, h_ref, nk_ref, np_ref):
        h = x_ref[...] + r_ref[...]
        h_ref[...] = h
        hn = h * _rstd(h)
        nk_ref[...] = (hn * gk_ref[...]).astype(BF16)
        np_ref[...] = (hn * gp_ref[...]).astype(BF16)

    d = x.shape[1]
    return _rowcall(body, name, x.shape[0], [(x, "row"), (r0, "row"), (g_kv, "full"), (g_pre, "full")],
                    [(d, F32, "row"), (d, BF16, "row"), (d, BF16, "row")])


def _post_norm(o, g, name, out_kind="row"):
    def body(o_ref, g_ref, r_ref):
        o = o_ref[...]
        r_ref[...] = o * _rstd(o) * g_ref[...]

    return _rowcall(body, name, o.shape[0], [(o, "row"), (g, "full")], [(o.shape[1], F32, out_kind)])[0]


def _post_norm_loss(o, g, h1, target, name):
    d = o.shape[1]

    def body(o_ref, g_ref, h_ref, t_ref, dh_ref, acc_ref):
        o = o_ref[...]
        e = h_ref[...] + o * _rstd(o) * g_ref[...] - t_ref[...]
        dh_ref[...] = e * (1.0 / d)
        acc_ref[...] += jnp.sum(e * e, axis=0, keepdims=True)

    return _rowcall(body, name, o.shape[0], [(o, "row"), (g, "full"), (h1, "row"), (target, "row")],
                    [(d, F32, "row"), (d, F32, "acc")])


def _post_norm_bwd(dy, o, g, name, dy_kind="row"):
    def body(dy_ref, o_ref, g_ref, do_ref, dg_ref):
        dx, dg = _rms_bwd(o_ref[...], g_ref[...], dy_ref[...])
        do_ref[...] = dx.astype(BF16)
        dg_ref[...] += dg

    d = o.shape[1]
    return _rowcall(body, name, o.shape[0], [(dy, dy_kind), (o, "row"), (g, "full")], [(d, BF16, "row"), (d, F32, "acc")])


def _gate_mul(o, z, name):
    def body(o_ref, z_ref, r_ref):
        r_ref[...] = (o_ref[...] * _silu(z_ref[...])).astype(BF16)

    return _rowcall(body, name, o.shape[0], [(o, "row"), (z, "row")], [(o.shape[1], BF16, "row")])[0]


def _gate_bwd(d_oz, o, z, name):
    def body(d_ref, o_ref, z_ref, do_ref, dz_ref):
        _, vjp = jax.vjp(lambda o, z: o * _silu(z), o_ref[...], z_ref[...])
        do, dz = vjp(d_ref[...])
        do_ref[...] = do.astype(BF16)
        dz_ref[...] = dz.astype(BF16)

    w = o.shape[1]
    return _rowcall(body, name, o.shape[0], [(d_oz, "row"), (o, "row"), (z, "row")], [(w, BF16, "row"), (w, BF16, "right")])


def _norm_bwd2(dh2, h1, dxn1, dhn_kv, g_pre, g_kv, name):
    def body(dh2_ref, h_ref, d1_ref, dk_ref, gp_ref, gk_ref, dh1_ref, dgp_ref, dgk_ref):
        h = h_ref[...]
        dx1, dg1 = _rms_bwd(h, gp_ref[...], d1_ref[...])
        dxk, dgk = _rms_bwd(h, gk_ref[...], dk_ref[...])
        dh1_ref[...] = dh2_ref[...] + dx1 + dxk
        dgp_ref[...] += dg1
        dgk_ref[...] += dgk

    d = h1.shape[1]
    return _rowcall(body, name, h1.shape[0],
                    [(dh2, "row"), (h1, "row"), (dxn1, "row"), (dhn_kv, "row"), (g_pre, "full"), (g_kv, "full")],
                    [(d, F32, "row"), (d, F32, "acc"), (d, F32, "acc")])


def _norm_bwd1(dres, x, dxn, g, name):
    def body(dr_ref, x_ref, dn_ref, g_ref, dx_ref, dg_ref):
        dx, dg = _rms_bwd(x_ref[...], g_ref[...], dn_ref[...])
        dx_ref[...] = dr_ref[...] + dx
        dg_ref[...] += dg

    d = x.shape[1]
    return _rowcall(body, name, x.shape[0], [(dres, "nat"), (x, "nat"), (dxn, "row"), (g, "full")],
                    [(d, F32, "nat"), (d, F32, "acc")])


def _gelu_cast(y, name):
    def body(y_ref, o_ref):
        o_ref[...] = jax.nn.gelu(y_ref[...]).astype(BF16)

    return _rowcall(body, name, y.shape[0], [(y, "row")], [(y.shape[1], BF16, "row")])[0]


def _s5_gate(y_ssm, gp, b_glu, z, name):
    def body(y_ref, gp_ref, b_ref, z_ref, o_ref):
        yg = jax.nn.gelu(y_ref[...])
        o_ref[...] = (yg * jax.nn.sigmoid(gp_ref[...] + b_ref[...]) * _silu(z_ref[...])).astype(BF16)

    return _rowcall(body, name, y_ssm.shape[0], [(y_ssm, "row"), (gp, "row"), (b_glu, "full"), (z, "row")],
                    [(y_ssm.shape[1], BF16, "row")])[0]


def _s5_gate_bwd(dy3, y_ssm, gp, b_glu, z, name):
    def body(d_ref, y_ref, gp_ref, b_ref, z_ref, dz_ref, dgp_ref, dyg_ref, db_ref):
        yg = jax.nn.gelu(y_ref[...])
        _, vjp = jax.vjp(lambda yg, gp, z: yg * jax.nn.sigmoid(gp) * _silu(z), yg, gp_ref[...] + b_ref[...], z_ref[...])
        dyg, dgp, dz = vjp(d_ref[...])
        dz_ref[...] = dz.astype(BF16)
        dgp_ref[...] = dgp.astype(BF16)
        dyg_ref[...] = dyg
        db_ref[...] += jnp.sum(dgp, axis=0, keepdims=True)

    w = y_ssm.shape[1]
    return _rowcall(body, name, y_ssm.shape[0],
                    [(dy3, "row"), (y_ssm, "row"), (gp, "row"), (b_glu, "full"), (z, "row")],
                    [(w, BF16, "right"), (w, BF16, "row"), (w, F32, "row"), (w, F32, "acc")])


def _gelu_bwd(dyg, y_ssm, name):
    def body(d_ref, y_ref, o_ref):
        _, vjp = jax.vjp(jax.nn.gelu, y_ref[...])
        o_ref[...] = vjp(d_ref[...])[0]

    return _rowcall(body, name, y_ssm.shape[0], [(dyg, "row"), (y_ssm, "row")], [(y_ssm.shape[1], F32, "row")])[0]


def _concat_cast(a, b, name):
    def body(a_ref, b_ref, o_ref):
        w = a_ref.shape[1]
        o_ref[:, :w] = a_ref[...].astype(BF16)
        o_ref[:, w:] = b_ref[...].astype(BF16)

    return _rowcall(body, name, a.shape[0], [(a, "row"), (b, "row")], [(a.shape[1] + b.shape[1], BF16, "row")])[0]


def _disc(ar, ai, ldt):
    dt = jnp.exp(ldt)
    mag = jnp.exp(ar * dt)
    abr = mag * jnp.cos(ai * dt)
    abi = mag * jnp.sin(ai * dt)
    den = ar * ar + ai * ai
    nr = abr - 1.0
    return abr, abi, (nr * ar + abi * ai) / den, (abi * ar - nr * ai) / den


def _s5_disc_fwd(a_re, a_im, ldt):
    def body(ar, ai, ld, o1, o2, o3, o4):
        o1[...], o2[...], o3[...], o4[...] = _disc(ar[...], ai[...], ld[...])

    sh = jax.ShapeDtypeStruct(a_re.shape, F32)
    return pl.pallas_call(body, name="s5_disc_fwd", out_shape=(sh, sh, sh, sh))(a_re, a_im, ldt)


def _s5_disc_bwd(a_re, a_im, ldt, d_abr, d_abi, d_cr, d_ci):
    def body(ar, ai, ld, g1, g2, g3, g4, o1, o2, o3):
        _, vjp = jax.vjp(_disc, ar[...], ai[...], ld[...])
        o1[...], o2[...], o3[...] = vjp((g1[...], g2[...], g3[...], g4[...]))

    sh = jax.ShapeDtypeStruct(a_re.shape, F32)
    return pl.pallas_call(body, name="s5_disc_bwd", out_shape=(sh, sh, jax.ShapeDtypeStruct(ldt.shape, F32)))(
        a_re, a_im, ldt, d_abr, d_abi, d_cr, d_ci)


def _bbar(cr, ci, br, bi):
    return cr * br - ci * bi, cr * bi + ci * br


def _s5_bbar_fwd(cr_col, ci_col, b_re, b_im):
    def body(cr, ci, br, bi, o1, o2):
        o1[...], o2[...] = _bbar(cr[...], ci[...], br[...], bi[...])

    w = b_re.shape[1]
    return _rowcall(body, "s5_bbar_fwd", b_re.shape[0], [(cr_col, "row"), (ci_col, "row"), (b_re, "row"), (b_im, "row")],
                    [(w, F32, "row"), (w, F32, "row")], tile_rows=1024)


def _s5_bbar_bwd(cr_col, ci_col, b_re, b_im, d_re, d_im):
    def body(cr, ci, br, bi, g1, g2, o1, o2, o3, o4):
        _, vjp = jax.vjp(_bbar, cr[...], ci[...], br[...], bi[...])
        o1[...], o2[...], o3[...], o4[...] = vjp((g1[...], g2[...]))

    w = b_re.shape[1]
    return _rowcall(body, "s5_bbar_bwd", b_re.shape[0],
                    [(cr_col, "row"), (ci_col, "row"), (b_re, "row"), (b_im, "row"), (d_re, "row"), (d_im, "row")],
                    [(1, F32, "row"), (1, F32, "row"), (w, F32, "row"), (w, F32, "row")], tile_rows=1024)


def _block_diag_in(t):
    g, p, c = t.shape
    nb = g // GROUPS_PER_BLOCK
    t4 = t.reshape(nb, GROUPS_PER_BLOCK, p, c).transpose(0, 1, 3, 2)
    eye = jnp.eye(GROUPS_PER_BLOCK, dtype=t.dtype)
    return (t4[:, :, :, None, :] * eye[None, :, None, :, None]).reshape(nb, GROUPS_PER_BLOCK * c, GROUPS_PER_BLOCK * p)


def _block_diag_in_extract(d, p, c):
    nb = d.shape[0]
    d5 = d.reshape(nb, GROUPS_PER_BLOCK, c, GROUPS_PER_BLOCK, p)
    diag = jnp.stack([d5[:, g, :, g, :] for g in range(GROUPS_PER_BLOCK)], axis=1)
    return diag.transpose(0, 1, 3, 2).reshape(nb * GROUPS_PER_BLOCK, p, c)


def _block_diag_out(t):
    g, c, p = t.shape
    nb = g // GROUPS_PER_BLOCK
    t4 = t.reshape(nb, GROUPS_PER_BLOCK, c, p).transpose(0, 1, 3, 2)
    eye = jnp.eye(GROUPS_PER_BLOCK, dtype=t.dtype)
    return (t4[:, :, :, None, :] * eye[None, :, None, :, None]).reshape(nb, GROUPS_PER_BLOCK * p, GROUPS_PER_BLOCK * c)


def _block_diag_out_extract(d, c, p):
    nb = d.shape[0]
    d5 = d.reshape(nb, GROUPS_PER_BLOCK, p, GROUPS_PER_BLOCK, c)
    diag = jnp.stack([d5[:, g, :, g, :] for g in range(GROUPS_PER_BLOCK)], axis=1)
    return diag.transpose(0, 1, 3, 2).reshape(nb * GROUPS_PER_BLOCK, c, p)


def _scan_step(ar, ai, hr, hi, xr, xi):
    return ar * hr - ai * hi + xr, ar * hi + ai * hr + xi


def _s5_scan_fwd(u, bd_re, bd_im, cd_re, cd_im, ab_re, ab_im, init_re, init_im, d_row, full, name):
    s, w = u.shape
    nb = w // LANES
    rows = _tile(s, 512, SUBLANES)
    nc = s // rows
    steps = rows // N_SEG
    ns = nb * BLOCK_STATE

    def body(u_ref, bdr, bdi, cdr, cdi, ar_ref, ai_ref, ir_ref, ii_ref, d_ref, *outs):
        if full:
            y_ref, yg_ref, hr_ref, hi_ref, er_ref, ei_ref, cr, ci = outs
        else:
            er_ref, ei_ref, hr_ref, hi_ref, cr, ci = outs
        c = pl.program_id(1)

        @pl.when(c == 0)
        def _():
            cr[...] = ir_ref[...]
            ci[...] = ii_ref[...]

        ub = u_ref[...].astype(BF16)
        hr_ref[...] = jnp.dot(ub, bdr[...], preferred_element_type=F32)
        hi_ref[...] = jnp.dot(ub, bdi[...], preferred_element_type=F32)
        ar, ai = ar_ref[...], ai_ref[...]

        def step(j, carry):
            off = pl.multiple_of(j * N_SEG, N_SEG)
            nr, ni = _scan_step(ar, ai, carry[0], carry[1], hr_ref[pl.ds(off, N_SEG), :], hi_ref[pl.ds(off, N_SEG), :])
            hr_ref[pl.ds(off, N_SEG), :] = nr
            hi_ref[pl.ds(off, N_SEG), :] = ni
            return nr, ni

        hr, hi = lax.fori_loop(0, steps, step, (cr[...], ci[...]), unroll=8)
        cr[...] = hr
        ci[...] = hi
        if full:
            y = (jnp.dot(hr_ref[...].astype(BF16), cdr[...], preferred_element_type=F32)
                 + jnp.dot(hi_ref[...].astype(BF16), cdi[...], preferred_element_type=F32)
                 + d_ref[...] * u_ref[...])
            y_ref[...] = y
            yg_ref[...] = jax.nn.gelu(y).astype(BF16)

        @pl.when(c == nc - 1)
        def _():
            er_ref[...] = hr
            ei_ref[...] = hi

    blk3 = lambda a: pl.BlockSpec((None,) + a.shape[1:], lambda k, c: (k, 0, 0))
    seg = pl.BlockSpec((N_SEG, BLOCK_STATE), lambda k, c: (0, k))
    st = pl.BlockSpec((rows, BLOCK_STATE), lambda k, c: (c, k))
    in_specs = [pl.BlockSpec((rows, LANES), lambda k, c: (c, k)), blk3(bd_re), blk3(bd_im), blk3(cd_re), blk3(cd_im),
                seg, seg, seg, seg, pl.BlockSpec((1, LANES), lambda k, c: (0, k))]
    seg_shape = jax.ShapeDtypeStruct((N_SEG, ns), F32)
    st_shape = jax.ShapeDtypeStruct((s, ns), F32)
    carry = [pltpu.VMEM((N_SEG, BLOCK_STATE), F32)] * 2
    if full:
        ych = pl.BlockSpec((rows, LANES), lambda k, c: (c, k))
        out_specs = [ych, ych, st, st, seg, seg]
        out_shape = [jax.ShapeDtypeStruct((s, w), F32), jax.ShapeDtypeStruct((s, w), BF16), st_shape, st_shape, seg_shape, seg_shape]
        scratch = carry
    else:
        out_specs = [seg, seg]
        out_shape = [seg_shape, seg_shape]
        scratch = [pltpu.VMEM((rows, BLOCK_STATE), F32)] * 2 + carry
    return pl.pallas_call(
        body, name=name, grid=(nb, nc), in_specs=in_specs, out_specs=out_specs, out_shape=out_shape,
        scratch_shapes=scratch, compiler_params=_cparams("parallel", "arbitrary"),
    )(u, bd_re, bd_im, cd_re, cd_im, ab_re, ab_im, init_re, init_im, d_row)


def _s5_seg_fix(e_re, e_im, ab_re, ab_im, seg_len, reverse, name):
    assert seg_len & (seg_len - 1) == 0

    def body(er, ei, ar, ai, o_re, o_im):
        pr, pi = ar[0:1, :], ai[0:1, :]
        for _ in range(int(math.log2(seg_len))):
            pr, pi = pr * pr - pi * pi, 2.0 * pr * pi
        tr = jnp.zeros_like(pr)
        ti = jnp.zeros_like(pr)
        order = list(range(N_SEG - 1, -1, -1)) if reverse else list(range(N_SEG))
        for n, sgm in enumerate(order):
            o_re[sgm:sgm + 1, :] = tr
            o_im[sgm:sgm + 1, :] = ti
            if n < N_SEG - 1:
                tr, ti = _scan_step(pr, pi, tr, ti, er[sgm:sgm + 1, :], ei[sgm:sgm + 1, :])

    sh = jax.ShapeDtypeStruct(e_re.shape, F32)
    return pl.pallas_call(body, name=name, out_shape=(sh, sh))(e_re, e_im, ab_re, ab_im)


def _s5_scan_bwd(dy, u, h_re, h_im, bd_re, bd_im, cd_re, cd_im, ab_re, ab_imn, gin_re, gin_im, d_row, full, name, duz=None):
    s, w = u.shape
    nb = w // LANES
    rows = _tile(s, 512, SUBLANES)
    nc = s // rows
    steps = rows // N_SEG
    ns = nb * BLOCK_STATE

    def body(dy_ref, u_ref, hr_ref, hi_ref, bdr, bdi, cdr, cdi, ar_ref, ai_ref, ir_ref, ii_ref, d_ref, *outs):
        if full:
            _, du_ref, dbr_ref, dbi_ref, dcr_ref, dci_ref, dar_ref, dai_ref, dd_ref, gr, gi, accr, acci = outs
        else:
            er_ref, ei_ref, gr, gi = outs
        c = pl.program_id(1)

        @pl.when(c == 0)
        def _():
            gr[pl.ds(rows, N_SEG), :] = ir_ref[...]
            gi[pl.ds(rows, N_SEG), :] = ii_ref[...]
            if full:
                for r in (dbr_ref, dbi_ref, dcr_ref, dci_ref, dd_ref, accr, acci):
                    r[...] = jnp.zeros_like(r)

        dyb = dy_ref[...].astype(BF16)
        nt = (_DOT_DIMS["nt"], ((), ()))
        tn = (_DOT_DIMS["tn"], ((), ()))
        gr[pl.ds(0, rows), :] = lax.dot_general(dyb, cdr[...], nt, preferred_element_type=F32)
        gi[pl.ds(0, rows), :] = lax.dot_general(dyb, cdi[...], nt, preferred_element_type=F32)
        ar, ai = ar_ref[...], ai_ref[...]

        def step(jj, carry):
            off = pl.multiple_of((steps - 1 - jj) * N_SEG, N_SEG)
            nr, ni = _scan_step(ar, ai, carry[0], carry[1], gr[pl.ds(off, N_SEG), :], gi[pl.ds(off, N_SEG), :])
            gr[pl.ds(off, N_SEG), :] = nr
            gi[pl.ds(off, N_SEG), :] = ni
            return nr, ni

        g0r, g0i = lax.fori_loop(0, steps, step, (gr[pl.ds(rows, N_SEG), :], gi[pl.ds(rows, N_SEG), :]), unroll=8)
        if full:
            hr, hi = hr_ref[...], hi_ref[...]
            gnr, gni = gr[pl.ds(N_SEG, rows), :], gi[pl.ds(N_SEG, rows), :]
            accr[...] += jnp.sum((gnr * hr + gni * hi).reshape(steps, N_SEG, BLOCK_STATE), axis=0)
            acci[...] += jnp.sum((gni * hr - gnr * hi).reshape(steps, N_SEG, BLOCK_STATE), axis=0)
        gr[pl.ds(rows, N_SEG), :] = g0r
        gi[pl.ds(rows, N_SEG), :] = g0i
        if full:
            ub = u_ref[...].astype(BF16)
            gbr, gbi = gr[pl.ds(0, rows), :].astype(BF16), gi[pl.ds(0, rows), :].astype(BF16)
            dcr_ref[...] += lax.dot_general(hr.astype(BF16), dyb, tn, preferred_element_type=F32)
            dci_ref[...] += lax.dot_general(hi.astype(BF16), dyb, tn, preferred_element_type=F32)
            dbr_ref[...] += lax.dot_general(ub, gbr, tn, preferred_element_type=F32)
            dbi_ref[...] += lax.dot_general(ub, gbi, tn, preferred_element_type=F32)
            du_ref[...] = (lax.dot_general(gbr, bdr[...], nt, preferred_element_type=F32)
                           + lax.dot_general(gbi, bdi[...], nt, preferred_element_type=F32)
                           + d_ref[...] * dy_ref[...]).astype(BF16)
            dd_ref[...] += jnp.sum(dy_ref[...] * u_ref[...], axis=0, keepdims=True)

        @pl.when(c == nc - 1)
        def _():
            if full:
                dar_ref[...] = jnp.sum(accr[...], axis=0, keepdims=True)
                dai_ref[...] = jnp.sum(acci[...], axis=0, keepdims=True)
            else:
                er_ref[...] = g0r
                ei_ref[...] = g0i

    rev = lambda k, c: (nc - 1 - c, k)
    blk3 = lambda a: pl.BlockSpec((None,) + a.shape[1:], lambda k, c: (k, 0, 0))
    seg = pl.BlockSpec((N_SEG, BLOCK_STATE), lambda k, c: (0, k))
    st = pl.BlockSpec((rows, BLOCK_STATE), rev)
    ch = pl.BlockSpec((rows, LANES), rev)
    vec = pl.BlockSpec((1, LANES), lambda k, c: (0, k))
    if not full:
        st = pl.BlockSpec((rows, BLOCK_STATE), lambda k, c: (0, k))
    in_specs = [ch, ch if full else pl.BlockSpec((rows, LANES), lambda k, c: (0, k)), st, st,
                blk3(bd_re), blk3(bd_im), blk3(cd_re), blk3(cd_im), seg, seg, seg, seg, vec]
    args = [dy, u, h_re, h_im, bd_re, bd_im, cd_re, cd_im, ab_re, ab_imn, gin_re, gin_im, d_row]
    gbuf = [pltpu.VMEM((rows + N_SEG, BLOCK_STATE), F32)] * 2
    if full:
        row1 = pl.BlockSpec((1, BLOCK_STATE), lambda k, c: (0, k))
        out_specs = [ch, blk3(bd_re), blk3(bd_im), blk3(cd_re), blk3(cd_im), row1, row1, vec]
        out_shape = [jax.ShapeDtypeStruct(duz.shape, BF16),
                     jax.ShapeDtypeStruct(bd_re.shape, F32), jax.ShapeDtypeStruct(bd_im.shape, F32),
                     jax.ShapeDtypeStruct(cd_re.shape, F32), jax.ShapeDtypeStruct(cd_im.shape, F32),
                     jax.ShapeDtypeStruct((1, ns), F32), jax.ShapeDtypeStruct((1, ns), F32),
                     jax.ShapeDtypeStruct((1, w), F32)]
        scratch = gbuf + [pltpu.VMEM((N_SEG, BLOCK_STATE), F32)] * 2
        in_specs.append(pl.BlockSpec(memory_space=pl.ANY))
        args.append(duz)
        aliases = {len(args) - 1: 0}
    else:
        out_specs = [seg, seg]
        out_shape = [jax.ShapeDtypeStruct((N_SEG, ns), F32)] * 2
        scratch = gbuf
        aliases = {}
    return pl.pallas_call(
        body, name=name, grid=(nb, nc), in_specs=in_specs, out_specs=out_specs, out_shape=out_shape,
        input_output_aliases=aliases, scratch_shapes=scratch, compiler_params=_cparams("parallel", "arbitrary"),
    )(*args)


def _log_sigmoid(x):
    return jnp.minimum(x, 0.0) - jnp.log(1.0 + jnp.exp(-jnp.abs(x)))


def _tri(n, upper):
    r = lax.broadcasted_iota(jnp.int32, (n, n), 0)
    c = lax.broadcasted_iota(jnp.int32, (n, n), 1)
    return jnp.where((c >= r) if upper else (r >= c), 1.0, 0.0).astype(F32)


def _cum_fwd(f_logit, b_row, name):
    s, w = f_logit.shape
    t = _tile(s, 256, SUBLANES)

    def body(f_ref, b_ref, o_ref, carry):
        @pl.when(pl.program_id(0) == 0)
        def _():
            carry[...] = jnp.zeros_like(carry)

        lf = _log_sigmoid(f_ref[...] + b_ref[...])
        cum = jnp.dot(_tri(t, False), lf, precision=lax.Precision.HIGHEST, preferred_element_type=F32) + carry[...]
        o_ref[...] = cum * LOG2E
        carry[...] = cum[t - 1:t, :]

    return pl.pallas_call(
        body, name=name, grid=(s // t,),
        in_specs=[pl.BlockSpec((t, w), lambda i: (i, 0)), pl.BlockSpec((1, w), lambda i: (0, 0))],
        out_specs=pl.BlockSpec((t, w), lambda i: (i, 0)), out_shape=jax.ShapeDtypeStruct((s, w), F32),
        scratch_shapes=[pltpu.VMEM((1, w), F32)], compiler_params=_cparams("arbitrary"),
    )(f_logit, b_row)


def _cum_bwd(dcq, dck, f_logit, b_row, name):
    s, w = f_logit.shape
    t = _tile(s, 256, SUBLANES)
    nt = s // t

    def body(q_ref, k_ref, f_ref, b_ref, df_ref, db_ref, carry):
        @pl.when(pl.program_id(0) == 0)
        def _():
            carry[...] = jnp.zeros_like(carry)
            db_ref[...] = jnp.zeros_like(db_ref)

        dc = q_ref[...] - k_ref[...]
        rc = jnp.dot(_tri(t, True), dc, precision=lax.Precision.HIGHEST, preferred_element_type=F32) + carry[...]
        carry[...] = rc[0:1, :]
        df = rc * (1.0 - jax.nn.sigmoid(f_ref[...] + b_ref[...]))
        df_ref[...] = df.astype(BF16)
        db_ref[...] += jnp.sum(df, axis=0, keepdims=True)

    rev = pl.BlockSpec((t, w), lambda i: (nt - 1 - i, 0))
    one = pl.BlockSpec((1, w), lambda i: (0, 0))
    return pl.pallas_call(
        body, name=name, grid=(nt,), in_specs=[rev, rev, rev, one], out_specs=[rev, one],
        out_shape=[jax.ShapeDtypeStruct((s, w), BF16), jax.ShapeDtypeStruct((1, w), F32)],
        scratch_shapes=[pltpu.VMEM((1, w), F32)], compiler_params=_cparams("arbitrary"),
    )(dcq, dck, f_logit, b_row)


def _head_col(cum_tile, h):
    lane = lax.broadcasted_iota(jnp.int32, cum_tile.shape, 1)
    return jnp.sum(jnp.where(lane == h, cum_tile, 0.0), axis=1, keepdims=True)


def _attn_tiles(s):
    return _tile(s, 512, LANES)


def _exp2_rows(sc, sub):
    return jnp.concatenate([jnp.exp2(sc[:, b * LANES:(b + 1) * LANES] - sub) for b in range(sc.shape[1] // LANES)], axis=1)


def _row_of(rep):
    return jnp.transpose(rep)[0:1, :]


def _causal(sc, keys_on_rows):
    r = lax.broadcasted_iota(jnp.int32, sc.shape, 0)
    c = lax.broadcasted_iota(jnp.int32, sc.shape, 1)
    return jnp.where((r <= c) if keys_on_rows else (c <= r), sc, NEG_INF)


def _fox_fwd(q2, kv, cum2_t, z, name):
    s, w = q2.shape
    nh = w // HEAD_DIM
    tq = _attn_tiles(s)
    nq = s // tq
    nt = (_DOT_DIMS["nt"], ((), ()))

    def body(q_ref, k_ref, v_ref, ct_ref, z_ref, o_ref, oz_ref, lse_ref, lse_row_ref, m_s, acc_s, vaug, s_buf):
        i = pl.program_id(1)

        @pl.when(i == 0)
        def _():
            vaug[:, :HEAD_DIM] = v_ref[...]
            vaug[:, HEAD_DIM:] = jnp.ones((s, LANES), BF16)

        qb = q_ref[...]
        m_s[...] = jnp.full_like(m_s, NEG_INF)
        acc_s[...] = jnp.zeros_like(acc_s)

        def scores(j):
            off = pl.multiple_of(j * tq, tq)
            return lax.dot_general(qb, k_ref[pl.ds(off, tq), :], nt, preferred_element_type=F32) - ct_ref[:, pl.ds(off, tq)]

        def softmax_pv(j, sc):
            m_old = m_s[...]
            m_new = jnp.maximum(m_old, jnp.max(sc, axis=1, keepdims=True))
            p = _exp2_rows(sc, m_new)
            alpha = jnp.exp2(m_old - m_new)
            pv = jnp.dot(p.astype(BF16), vaug[pl.ds(pl.multiple_of(j * tq, tq), tq), :], preferred_element_type=F32)
            acc_s[...] = jnp.concatenate([alpha, alpha], axis=1) * acc_s[...] + pv
            m_s[...] = m_new

        s_buf[...] = scores(0)

        def loop(j, carry):
            nxt = scores(j + 1)
            softmax_pv(j, s_buf[...])
            s_buf[...] = nxt
            return carry

        lax.fori_loop(0, i, loop, 0)
        softmax_pv(i, _causal(s_buf[...], False))
        l = acc_s[:, HEAD_DIM:]
        o = acc_s[:, :HEAD_DIM] / l
        o_ref[...] = o
        oz_ref[...] = (o * _silu(z_ref[...])).astype(BF16)
        lse = m_s[...] + jnp.log(l) * LOG2E
        lse_ref[...] = lse
        lse_row_ref[...] = _row_of(lse)

    return pl.pallas_call(
        body, name=name, grid=(nh, nq),
        in_specs=[pl.BlockSpec((tq, HEAD_DIM), lambda h, i: (i, h)),
                  pl.BlockSpec((s, HEAD_DIM), lambda h, i: (0, h)),
                  pl.BlockSpec((s, HEAD_DIM), lambda h, i: (0, nh + h)),
                  pl.BlockSpec((None, 1, s), lambda h, i: (h, 0, 0)),
                  pl.BlockSpec((tq, HEAD_DIM), lambda h, i: (i, h))],
        out_specs=[pl.BlockSpec((tq, HEAD_DIM), lambda h, i: (i, h)),
                   pl.BlockSpec((tq, HEAD_DIM), lambda h, i: (i, h)),
                   pl.BlockSpec((None, tq, LANES), lambda h, i: (h, i, 0)),
                   pl.BlockSpec((None, 1, tq), lambda h, i: (h, 0, i))],
        out_shape=[jax.ShapeDtypeStruct((s, w), F32), jax.ShapeDtypeStruct((s, w), BF16),
                   jax.ShapeDtypeStruct((nh, s, LANES), F32), jax.ShapeDtypeStruct((nh, 1, s), F32)],
        scratch_shapes=[pltpu.VMEM((tq, LANES), F32), pltpu.VMEM((tq, HEAD_DIM + LANES), F32),
                        pltpu.VMEM((s, HEAD_DIM + LANES), BF16), pltpu.VMEM((tq, tq), F32)],
        compiler_params=_cparams("arbitrary", "arbitrary"),
    )(q2, kv, kv, cum2_t, z)


def _fox_bwd_dq(q2, kv, do, o, lse2, cum2_t, dqz, name):
    s, w = q2.shape
    nh = w // HEAD_DIM
    tq = _attn_tiles(s)
    nq = s // tq
    scale = HEAD_DIM ** -0.5
    nt = (_DOT_DIMS["nt"], ((), ()))

    def body(q_ref, k_ref, v_ref, do_ref, o_ref, lse_ref, ct_ref, _, dq_ref, dl_ref, dcq_ref, acc_s, dc_s):
        i = pl.program_id(1)
        qb = q_ref[...]
        dob = do_ref[...]
        lse = lse_ref[...]
        delta = jnp.broadcast_to(jnp.sum(dob.astype(F32) * o_ref[...], axis=1, keepdims=True), (tq, LANES))
        acc_s[...] = jnp.zeros_like(acc_s)
        dc_s[...] = jnp.zeros_like(dc_s)

        def tile(j, masked):
            off = pl.multiple_of(j * tq, tq)
            kb = k_ref[pl.ds(off, tq), :]
            sc = lax.dot_general(qb, kb, nt, preferred_element_type=F32) - ct_ref[:, pl.ds(off, tq)]
            if masked:
                sc = _causal(sc, False)
            p = _exp2_rows(sc, lse)
            dp = lax.dot_general(dob, v_ref[pl.ds(off, tq), :], nt, preferred_element_type=F32)
            ds = p * (dp - jnp.concatenate([delta] * (tq // LANES), axis=1))
            acc_s[...] += jnp.dot(ds.astype(BF16), kb, preferred_element_type=F32)
            part = ds[:, :LANES]
            for b in range(1, tq // LANES):
                part = part + ds[:, b * LANES:(b + 1) * LANES]
            dc_s[...] += part

        def loop(j, carry):
            tile(j, False)
            return carry

        lax.fori_loop(0, i, loop, 0)
        tile(i, True)
        dq_ref[...] = (acc_s[...] * scale).astype(BF16)
        dl_ref[...] = _row_of(delta)
        dcq_ref[...] = jnp.sum(jnp.transpose(dc_s[...]), axis=0, keepdims=True)

    qspec = pl.BlockSpec((tq, HEAD_DIM), lambda h, i: (i, h))
    rep = pl.BlockSpec((None, tq, LANES), lambda h, i: (h, i, 0))
    rowspec = pl.BlockSpec((None, 1, tq), lambda h, i: (h, 0, i))
    return pl.pallas_call(
        body, name=name, grid=(nh, nq),
        in_specs=[qspec,
                  pl.BlockSpec((s, HEAD_DIM), lambda h, i: (0, h)),
                  pl.BlockSpec((s, HEAD_DIM), lambda h, i: (0, nh + h)),
                  qspec, qspec, rep,
                  pl.BlockSpec((None, 1, s), lambda h, i: (h, 0, 0)),
                  pl.BlockSpec(memory_space=pl.ANY)],
        out_specs=[qspec, rowspec, rowspec],
        out_shape=[jax.ShapeDtypeStruct(dqz.shape, BF16), jax.ShapeDtypeStruct((nh, 1, s), F32),
                   jax.ShapeDtypeStruct((nh, 1, s), F32)],
        input_output_aliases={7: 0},
        scratch_shapes=[pltpu.VMEM((tq, HEAD_DIM), F32), pltpu.VMEM((tq, LANES), F32)],
        compiler_params=_cparams("parallel", "arbitrary"),
    )(q2, kv, kv, do, o, lse2, cum2_t, dqz)


def _fox_bwd_dkv(q2, kv, do, lse2_t, delta_t, cum2, name):
    s, w = q2.shape
    nh = w // HEAD_DIM
    tk = _attn_tiles(s)
    nk = s // tk
    nt = (_DOT_DIMS["nt"], ((), ()))

    def body(q_ref, k_ref, v_ref, do_ref, lse_ref, dl_ref, c_ref, dk_ref, dv_ref, dck_ref, dk_s, dv_s, dc_s, s_buf, dp_buf):
        h, j = pl.program_id(0), pl.program_id(1)
        kb = k_ref[...]
        vb = v_ref[...]
        ck = jnp.broadcast_to(_head_col(c_ref[...], h), (tk, LANES))
        dk_s[...] = jnp.zeros_like(dk_s)
        dv_s[...] = jnp.zeros_like(dv_s)
        dc_s[...] = jnp.zeros_like(dc_s)

        def scores(i):
            off = pl.multiple_of(i * tk, tk)
            sc = lax.dot_general(kb, q_ref[pl.ds(off, tk), :], nt, preferred_element_type=F32) - lse_ref[:, pl.ds(off, tk)]
            dp = lax.dot_general(vb, do_ref[pl.ds(off, tk), :], nt, preferred_element_type=F32) - dl_ref[:, pl.ds(off, tk)]
            return sc, dp

        def accumulate(i, sc, dp):
            off = pl.multiple_of(i * tk, tk)
            p = _exp2_rows(sc, ck)
            dv_s[...] += jnp.dot(p.astype(BF16), do_ref[pl.ds(off, tk), :], preferred_element_type=F32)
            ds = p * dp
            dk_s[...] += jnp.dot(ds.astype(BF16), q_ref[pl.ds(off, tk), :], preferred_element_type=F32)
            part = ds[:, :LANES]
            for b in range(1, tk // LANES):
                part = part + ds[:, b * LANES:(b + 1) * LANES]
            dc_s[...] += part

        sc0, dp0 = scores(j)
        s_buf[...] = _causal(sc0, True)
        dp_buf[...] = dp0

        def loop(i, carry):
            nxt = scores(i + 1)
            accumulate(i, s_buf[...], dp_buf[...])
            s_buf[...], dp_buf[...] = nxt
            return carry

        lax.fori_loop(j, nk - 1, loop, 0)
        accumulate(nk - 1, s_buf[...], dp_buf[...])
        dk_ref[...] = (dk_s[...] * (1.0 / LOG2E)).astype(BF16)
        dv_ref[...] = dv_s[...].astype(BF16)
        dck_ref[...] = jnp.sum(jnp.transpose(dc_s[...]), axis=0, keepdims=True)

    col = pl.BlockSpec((s, HEAD_DIM), lambda h, j: (0, h))
    row = pl.BlockSpec((None, 1, s), lambda h, j: (h, 0, 0))
    kspec = pl.BlockSpec((tk, HEAD_DIM), lambda h, j: (j, h))
    return pl.pallas_call(
        body, name=name, grid=(nh, nk),
        in_specs=[col, kspec, pl.BlockSpec((tk, HEAD_DIM), lambda h, j: (j, nh + h)), col, row, row,
                  pl.BlockSpec((tk, LANES), lambda h, j: (j, 0))],
        out_specs=[kspec, kspec, pl.BlockSpec((None, 1, tk), lambda h, j: (h, 0, j))],
        out_shape=[jax.ShapeDtypeStruct((s, w), BF16), jax.ShapeDtypeStruct((s, w), BF16),
                   jax.ShapeDtypeStruct((nh, 1, s), F32)],
        scratch_shapes=[pltpu.VMEM((tk, HEAD_DIM), F32), pltpu.VMEM((tk, HEAD_DIM), F32),
                        pltpu.VMEM((tk, LANES), F32), pltpu.VMEM((tk, tk), F32), pltpu.VMEM((tk, tk), F32)],
        compiler_params=_cparams("parallel", "arbitrary"),
    )(q2, kv, kv, do, lse2_t, delta_t, cum2)


def _exchange_copies(ins, outs, send_sems, recv_sems, local_sems, scatter):
    x, y, c = (lax.axis_index(a) for a in MESH_AXES)
    me = 4 * x + 2 * y + c
    local, remote = [], []
    for a in range(len(ins)):
        local.append(pltpu.make_async_copy(ins[a].at[me] if scatter else ins[a], outs[a].at[me], local_sems.at[a]))
        for k in range(1, N_DEV):
            px, py, pc = (1 - x if k & 4 else x), (1 - y if k & 2 else y), (1 - c if k & 1 else c)
            remote.append(pltpu.make_async_remote_copy(
                src_ref=ins[a].at[4 * px + 2 * py + pc] if scatter else ins[a], dst_ref=outs[a].at[me],
                send_sem=send_sems.at[a * (N_DEV - 1) + k - 1], recv_sem=recv_sems.at[a * (N_DEV - 1) + k - 1],
                device_id=(px, py, pc), device_id_type=pl.DeviceIdType.MESH))
    return local, remote


def _exchange_out_shapes(arrs, scatter):
    return [((N_DEV,) + a.shape[1:]) if scatter else ((N_DEV,) + a.shape) for a in arrs]


def _exchange(arrs, scatter, name):
    n = len(arrs)

    def body(*refs):
        local, remote = _exchange_copies(refs[:n], refs[n:2 * n], *refs[2 * n:], scatter)
        for cp in local + remote:
            cp.start()
        for cp in remote:
            cp.wait_send()
            cp.wait_recv()
        for cp in local:
            cp.wait()

    out_shape = [jax.ShapeDtypeStruct(s, a.dtype) for s, a in zip(_exchange_out_shapes(arrs, scatter), arrs)]
    return pl.pallas_call(
        body, name=name, out_shape=out_shape,
        in_specs=[pl.BlockSpec(memory_space=pl.ANY)] * n, out_specs=[pl.BlockSpec(memory_space=pl.ANY)] * n,
        scratch_shapes=[pltpu.SemaphoreType.DMA((n * (N_DEV - 1),)), pltpu.SemaphoreType.DMA((n * (N_DEV - 1),)),
                        pltpu.SemaphoreType.DMA((n,))],
    )(*arrs)


_HBM = pl.BlockSpec(memory_space=pltpu.HBM)
_SEM = pl.BlockSpec(memory_space=pltpu.SEMAPHORE)


def _exchange_start(arrs, scatter, name):
    n = len(arrs)
    lands = [lax.empty(s, a.dtype) for s, a in zip(_exchange_out_shapes(arrs, scatter), arrs)]

    def body(*refs):
        ins, outs = refs[:n], refs[n:2 * n]
        send_sems, recv_sems, local_sems = refs[2 * n:2 * n + 3]
        token = refs[-1]
        local, remote = _exchange_copies(ins, outs, send_sems, recv_sems, local_sems, scatter)
        for cp in local + remote:
            cp.start()
        token[...] = jnp.zeros_like(token)

    hbm = lambda a: pltpu.HBM(a.shape, a.dtype)
    res = pl.pallas_call(
        body, name=name,
        out_shape=(pltpu.SemaphoreType.DMA((n * (N_DEV - 1),)), pltpu.SemaphoreType.DMA((n * (N_DEV - 1),)),
                   pltpu.SemaphoreType.DMA((n,)), *[hbm(a) for a in arrs], *[hbm(a) for a in lands],
                   jax.ShapeDtypeStruct((SUBLANES, LANES), F32)),
        in_specs=[_HBM] * (2 * n),
        out_specs=(_SEM, _SEM, _SEM, *[_HBM] * (2 * n), pl.BlockSpec(memory_space=pltpu.VMEM)),
        input_output_aliases={i: 3 + i for i in range(2 * n)},
        compiler_params=pltpu.CompilerParams(has_side_effects=pltpu.SideEffectType.DATAFLOW_SIDE_EFFECTING),
    )(*[pltpu.with_memory_space_constraint(a, pltpu.HBM) for a in list(arrs) + lands])
    return (n, scatter, res[:3], res[3:3 + n], res[3 + n:3 + 2 * n]), res[-1]


def _exchange_wait(state, after, name):
    n, scatter, sems, srcs, lands = state
    after = list(after) if isinstance(after, (list, tuple)) else [after]

    def body(*refs):
        ins, outs = refs[:n], refs[n:2 * n]
        send_sems, recv_sems, local_sems = refs[2 * n:2 * n + 3]
        local, remote = _exchange_copies(ins, outs, send_sems, recv_sems, local_sems, scatter)
        for cp in remote:
            cp.wait_send()
            cp.wait_recv()
        for cp in local:
            cp.wait()

    hbm = lambda a: pltpu.HBM(a.shape, a.dtype)
    res = pl.pallas_call(
        body, name=name,
        out_shape=(*[hbm(a) for a in srcs], *[hbm(a) for a in lands]),
        in_specs=[_HBM] * (2 * n) + [_SEM] * 3 + [pl.BlockSpec(memory_space=pl.ANY)] * len(after),
        out_specs=tuple([_HBM] * (2 * n)),
        input_output_aliases={i: i for i in range(2 * n)},
        compiler_params=pltpu.CompilerParams(has_side_effects=pltpu.SideEffectType.DATAFLOW_SIDE_EFFECTING),
    )(*srcs, *lands, *sems, *after)
    return list(res[n:])


def _adamw_math(w, g, m, v):
    m = ADAM_B1 * m + (1.0 - ADAM_B1) * g
    v = ADAM_B2 * v + (1.0 - ADAM_B2) * (g * g)
    m_hat = m / (1.0 - ADAM_B1 ** ADAM_STEP)
    v_hat = v / (1.0 - ADAM_B2 ** ADAM_STEP)
    return -ADAM_LR * (m_hat / (jnp.sqrt(v_hat) + ADAM_EPS) + ADAM_WD * w), m, v


def _slot_sum(p_ref):
    g = p_ref[0].astype(F32)
    for d in range(1, p_ref.shape[0]):
        g = g + p_ref[d].astype(F32)
    return g


def _adamw_tile(r, c):
    return _tile(r, max(SUBLANES, (256 * 1024) // c // SUBLANES * SUBLANES), SUBLANES)


def _adamw(parts, w, m, v, name):
    r, c = w.shape
    tr = _adamw_tile(r, c)

    def body(p_ref, w_ref, m_ref, v_ref, g_ref, d_ref, nm_ref, nv_ref):
        g = _slot_sum(p_ref)
        g_ref[...] = g
        d_ref[...], nm_ref[...], nv_ref[...] = _adamw_math(w_ref[...], g, m_ref[...], v_ref[...])

    blk = pl.BlockSpec((tr, c), lambda i: (i, 0))
    sh = jax.ShapeDtypeStruct((r, c), F32)
    return pl.pallas_call(
        body, name=name, grid=(r // tr,),
        in_specs=[pl.BlockSpec((parts.shape[0], tr, c), lambda i: (0, i, 0)), blk, blk, blk],
        out_specs=[blk] * 4, out_shape=[sh] * 4, compiler_params=_cparams("parallel"),
    )(parts, w, m, v)


def _sum_parts(parts, name):
    _, r, c = parts.shape
    tr = _adamw_tile(r, c)

    def body(p_ref, o_ref):
        o_ref[...] = _slot_sum(p_ref)

    return pl.pallas_call(
        body, name=name, grid=(r // tr,),
        in_specs=[pl.BlockSpec((parts.shape[0], tr, c), lambda i: (0, i, 0))],
        out_specs=pl.BlockSpec((tr, c), lambda i: (i, 0)), out_shape=jax.ShapeDtypeStruct((r, c), F32),
        compiler_params=_cparams("parallel"),
    )(parts)


def _perm(a):
    s, d = a.shape
    return a.reshape(N_SEG, s // N_SEG, d).transpose(1, 0, 2).reshape(s, d)


def _unperm(a):
    s, d = a.shape
    return a.reshape(s // N_SEG, N_SEG, d).transpose(1, 0, 2).reshape(s, d)


def _lane_pad(a, width=LANES):
    return jnp.pad(a, ((0, 0), (0, width - a.shape[1])))


def _local_step(x, target, norm_pre, norm_post, kv_norm, kv_b_f, a_re, a_im, log_dt, b_re, b_im, c_re, c_im, comm):
    s, d = x.shape
    g, p = a_re.shape
    w = g * S5_GROUP
    fw = d
    nh = fw // HEAD_DIM
    seg_len = s // N_SEG
    row = lambda v: v.reshape(1, -1)
    g_pre0, g_pre1, g_post0, g_post1, g_kv = row(norm_pre[0]), row(norm_pre[1]), row(norm_post[0]), row(norm_post[1]), row(kv_norm)

    ldt = log_dt.reshape(g, 1)
    abr, abi, cr, ci = _s5_disc_fwd(a_re, a_im, ldt)
    cr_col, ci_col = cr.reshape(g * p, 1), ci.reshape(g * p, 1)
    b_re2, b_im2 = b_re.reshape(g * p, S5_GROUP), b_im.reshape(g * p, S5_GROUP)
    bb_re, bb_im = _s5_bbar_fwd(cr_col, ci_col, b_re2, b_im2)
    bd_re = _block_diag_in(bb_re.reshape(g, p, S5_GROUP)).astype(BF16)
    bd_im = _block_diag_in(bb_im.reshape(g, p, S5_GROUP)).astype(BF16)
    cd_re = _block_diag_out(c_re).astype(BF16)
    cd_im = _block_diag_out(-c_im).astype(BF16)
    ab_re = jnp.broadcast_to(abr.reshape(1, g * p), (N_SEG, g * p))
    ab_im = jnp.broadcast_to(abi.reshape(1, g * p), (N_SEG, g * p))
    zero_seg = jnp.zeros((N_SEG, g * p), F32)

    xn0 = _norm_cast(x, g_pre0 + comm.token, "norm_pre0", x_kind="nat")
    w_in = comm.weight("s5_w_in", [xn0, bd_re, bd_im, cd_re, cd_im, ab_re, ab_im])
    d_row, bglu_row = row(comm.vector("s5_d")), row(comm.vector("s5_b_glu"))
    u = _mm(xn0, w_in, "nn", F32, "s5_in_u", b_cols=(0, w))
    z0 = _mm(xn0, w_in, "nn", F32, "s5_in_z", b_cols=(w, w))
    e_re, e_im = _s5_scan_fwd(u, bd_re, bd_im, cd_re, cd_im, ab_re, ab_im, zero_seg, zero_seg, d_row, False, "s5_scan_ends")
    i_re, i_im = _s5_seg_fix(e_re, e_im, ab_re, ab_im, seg_len, False, "s5_seg_fix")
    y_ssm, yg, h_re, h_im, _, _ = _s5_scan_fwd(u, bd_re, bd_im, cd_re, cd_im, ab_re, ab_im, i_re, i_im, d_row, True, "s5_scan")
    w_glu, w_out = comm.weight("s5_w_glu", yg), comm.weight("s5_w_out", yg)
    gp = _mm(yg, w_glu, "nn", F32, "s5_glu")
    y3 = _s5_gate(y_ssm, gp, bglu_row, z0, "s5_gate")
    w_kv, fw_in, fw_out = comm.weight("kv_w", y3), comm.weight("fox_w_in", y3), comm.weight("fox_w_out", y3)
    w_f = _lane_pad(w_kv[:, 2 * fw:])
    o0 = _mm(y3, w_out, "nn", F32, "s5_out")
    r0 = _post_norm(o0, g_post0, "norm_post0", out_kind="nat")

    h1, hn_kv, xn1 = _resid_norm2(x, r0, g_kv, g_pre1, "resid_norms")
    kv = _mm(hn_kv, w_kv, "nn", BF16, "kv_proj", b_cols=(0, 2 * fw))
    f_logit = _mm(hn_kv, w_f, "nn", F32, "f_proj")
    bf_row = _lane_pad(row(kv_b_f))
    cum2 = _cum_fwd(f_logit, bf_row, "cum_fwd")
    cum2_t = cum2[:, :nh].T.reshape(nh, 1, s)
    q2 = _mm(xn1, fw_in, "nn", BF16, "fox_q", scale=HEAD_DIM ** -0.5 * LOG2E, b_cols=(0, fw))
    z1 = _mm(xn1, fw_in, "nn", F32, "fox_z", b_cols=(fw, fw))
    o, oz, lse2, lse2_t = _fox_fwd(q2, kv, cum2_t, z1, "fox_fwd")
    o1 = _mm(oz, fw_out, "nn", F32, "fox_out")
    dh2, sq = _post_norm_loss(o1, g_post1, h1, target, "norm_post1_loss")
    loss = 0.5 * jnp.sum(sq) / d

    do1, dg_post1 = _post_norm_bwd(dh2, o1, g_post1, "norm_post1_bwd")
    d_fw_out = _mm(oz, do1, "tn", BF16, "fox_out_dw")
    d_oz = _mm(do1, fw_out, "nt", F32, "fox_out_dx")
    do, dqz = _gate_bwd(d_oz, o, z1, "fox_gate_bwd")
    dqz, delta_t, dcq = _fox_bwd_dq(q2, kv, do, o, lse2, cum2_t, dqz, "fox_bwd_dq")
    dk, dv, dck = _fox_bwd_dkv(q2, kv, do, lse2_t, delta_t, cum2, "fox_bwd_dkv")
    d_fw_in = _mm(xn1, dqz, "tn", BF16, "fox_in_dw", col_slots=True)
    dxn1 = _mm(dqz, fw_in, "nt", F32, "fox_in_dx")
    dcq_sl = _lane_pad(dcq.reshape(nh, s).T)
    dck_sl = _lane_pad(dck.reshape(nh, s).T)
    df, db_f = _cum_bwd(dcq_sl, dck_sl, f_logit, bf_row, "cum_bwd")
    dkv = _concat_cast(dk, dv, "fox_dkv")
    d_w_kvm = _mm(hn_kv, dkv, "tn", F32, "kv_dw")
    d_w_f = _mm(hn_kv, df, "tn", F32, "f_dw")
    dhn_f = _mm(df, w_f, "nt", F32, "f_dx")
    dhn_kv = _mm(dkv, w_kv, "nt", F32, "kv_dx", add=dhn_f, b_cols=(0, 2 * fw))
    d_w_kv = jnp.concatenate([d_w_kvm, d_w_f[:, :nh]], axis=1)
    tok = comm.send_grads(dict(fox_w_out=d_fw_out, fox_w_in=d_fw_in, kv_w=d_w_kv), "exchange_fox")
    dh1, dg_pre1, dg_kv = _norm_bwd2(dh2, h1, dxn1, dhn_kv, g_pre1, g_kv, "resid_norms_bwd")

    do0, dg_post0 = _post_norm_bwd(dh1, o0, g_post0 + tok[0, 0], "norm_post0_bwd", dy_kind="nat")
    d_w_out = _mm(y3, do0, "tn", BF16, "s5_out_dw")
    dy3 = _mm(do0, w_out, "nt", F32, "s5_out_dx")
    duz, dgp, dyg_direct, db_glu = _s5_gate_bwd(dy3, y_ssm, gp, bglu_row, z0, "s5_gate_bwd")
    d_w_glu = _mm(yg, dgp, "tn", BF16, "s5_glu_dw")
    dyg = _mm(dgp, w_glu, "nt", F32, "s5_glu_dx", add=dyg_direct)
    dy_ssm = _gelu_bwd(dyg, y_ssm, "s5_gelu_bwd")
    d_row = d_row + comm.send_grads(dict(s5_w_out=d_w_out, s5_w_glu=d_w_glu), "exchange_s5")[0, 0]
    ab_imn = -ab_im
    ge_re, ge_im = _s5_scan_bwd(dy_ssm, u, h_re, h_im, bd_re, bd_im, cd_re, cd_im, ab_re, ab_imn, zero_seg, zero_seg,
                                d_row, False, "s5_adj_ends")
    gi_re, gi_im = _s5_seg_fix(ge_re, ge_im, ab_re, ab_imn, seg_len, True, "s5_adj_fix")
    duz, dbd_re, dbd_im, dcd_re, dcd_im, dab_re, dab_im, dd = _s5_scan_bwd(
        dy_ssm, u, h_re, h_im, bd_re, bd_im, cd_re, cd_im, ab_re, ab_imn, gi_re, gi_im, d_row, True, "s5_adj", duz=duz)
    d_w_in = _mm(xn0, duz, "tn", BF16, "s5_in_dw", col_slots=True)
    tok = comm.send_grads(dict(s5_w_in=d_w_in), "exchange_s5_in")
    dxn0 = _mm(duz, w_in, "nt", F32, "s5_in_dx", after=tok)
    grad_x, dg_pre0 = _norm_bwd1(dh1, x, dxn0, g_pre0, "norm_pre0_bwd")

    dbb_re = _block_diag_in_extract(dbd_re, p, S5_GROUP).reshape(g * p, S5_GROUP)
    dbb_im = _block_diag_in_extract(dbd_im, p, S5_GROUP).reshape(g * p, S5_GROUP)
    dcr_col, dci_col, db_re, db_im = _s5_bbar_bwd(cr_col, ci_col, b_re2, b_im2, dbb_re, dbb_im)
    da_re, da_im, dldt = _s5_disc_bwd(a_re, a_im, ldt, dab_re.reshape(g, p), dab_im.reshape(g, p),
                                      dcr_col.reshape(g, p), dci_col.reshape(g, p))
    dc_re = _block_diag_out_extract(dcd_re, S5_GROUP, p)
    dc_im = -_block_diag_out_extract(dcd_im, S5_GROUP, p)

    small = dict(
        norm_pre=jnp.concatenate([dg_pre0, dg_pre1], axis=0), norm_post=jnp.concatenate([dg_post0, dg_post1], axis=0),
        s5_a_re=da_re, s5_a_im=da_im, s5_log_dt=dldt.reshape(g), s5_b_re=db_re.reshape(g, p, S5_GROUP),
        s5_b_im=db_im.reshape(g, p, S5_GROUP), s5_c_re=dc_re, s5_c_im=dc_im, s5_d=dd.reshape(-1),
        s5_b_glu=db_glu.reshape(-1), kv_norm=dg_kv.reshape(-1), kv_b_f=db_f[0, :nh])
    return loss, grad_x, small


_BIG = ("s5_w_in", "s5_w_glu", "s5_w_out", "kv_w", "fox_w_in", "fox_w_out")
_COL_SHARDED = ("s5_w_in", "kv_w", "fox_w_in")
_SMALL = ("norm_pre", "norm_post", "s5_a_re", "s5_a_im", "s5_log_dt", "s5_b_re", "s5_b_im", "s5_c_re", "s5_c_im",
          "s5_d", "s5_b_glu", "kv_norm", "kv_b_f")
_SMALL_SHARDED = ("s5_d", "s5_b_glu")
_PACK_QUANTUM = SUBLANES * LANES
_WEIGHTS = ('norm_pre', 'norm_post', 's5_w_in', 's5_a_re', 's5_a_im', 's5_log_dt', 's5_b_re', 's5_b_im', 's5_c_re', 's5_c_im',
            's5_d', 's5_w_glu', 's5_b_glu', 's5_w_out', 'kv_norm', 'kv_w', 'kv_b_f', 'fox_w_in', 'fox_w_out')


def _full_from_slots(name, slots):
    n, r, c = slots.shape
    if name in _COL_SHARDED:
        return slots.transpose(1, 0, 2).reshape(r, n * c)
    return slots.reshape(n * r, c)


def _slots_from_full(name, full):
    if name in _COL_SHARDED:
        r, nc = full.shape
        return full.reshape(r, N_DEV, nc // N_DEV).transpose(1, 0, 2)
    nr, c = full.shape
    return full.reshape(N_DEV, nr // N_DEV, c)


def _pack(vals):
    parts = []
    for v in vals:
        flat = v.reshape(-1)
        parts.append(jnp.pad(flat, (0, (-flat.shape[0]) % _PACK_QUANTUM)))
    total = sum(p.shape[0] for p in parts)
    parts.append(jnp.zeros(((-total) % (N_DEV * _PACK_QUANTUM),), F32))
    return jnp.concatenate(parts).reshape(-1, LANES)


def _unpack(packed, shapes):
    flat = packed.reshape(-1)
    out, off = [], 0
    for sh in shapes:
        n = math.prod(sh)
        out.append(flat[off:off + n].reshape(sh))
        off += n + (-n) % _PACK_QUANTUM
    return out


class _Comm:
    _GROUPS = (("s5_w_in",) + _SMALL_SHARDED, ("s5_w_glu", "s5_w_out"), ("kv_w", "fox_w_in", "fox_w_out"))

    def __init__(self, shards, vectors):
        shards = {**shards, **vectors}
        self._full, self._gathers = {}, {}
        self.token = jnp.zeros((), F32)
        for group in self._GROUPS:
            state, tok = _exchange_start([shards[n] for n in group], False, "gather_start_" + group[0])
            self._gathers[group] = state
            self.token = self.token + tok[0, 0]
        self._sent = []

    def vector(self, name):
        return self._full[name]

    def weight(self, name, after):
        if name not in self._full:
            group = next(g for g in self._GROUPS if name in g)
            slots = _exchange_wait(self._gathers.pop(group), after, "gather_wait_" + group[0])
            for n, sl in zip(group, slots):
                self._full[n] = sl.reshape(-1) if n in _SMALL_SHARDED else _full_from_slots(n, sl)
        return self._full[name]

    def send_grads(self, grads, name):
        names = list(grads)
        slots = [grads[n] if grads[n].ndim == 3 else _slots_from_full(n, grads[n]).astype(BF16) for n in names]
        state, tok = _exchange_start(slots, True, name + "_start")
        self._sent.append((names, state, name + "_wait"))
        return tok

    def received_grads(self, after):
        for names, state, name in self._sent:
            for n, recv in zip(names, _exchange_wait(state, after, name)):
                yield n, recv


def kernel(x, norm_pre, norm_post, s5_w_in, s5_a_re, s5_a_im, s5_log_dt, s5_b_re, s5_b_im, s5_c_re, s5_c_im, s5_d, s5_w_glu, s5_b_glu, s5_w_out, kv_norm, kv_w, kv_b_f, fox_w_in, fox_w_out, loss_target, m_norm_pre, m_norm_post, m_s5_w_in, m_s5_a_re, m_s5_a_im, m_s5_log_dt, m_s5_b_re, m_s5_b_im, m_s5_c_re, m_s5_c_im, m_s5_d, m_s5_w_glu, m_s5_b_glu, m_s5_w_out, m_kv_norm, m_kv_w, m_kv_b_f, m_fox_w_in, m_fox_w_out, v_norm_pre, v_norm_post, v_s5_w_in, v_s5_a_re, v_s5_a_im, v_s5_log_dt, v_s5_b_re, v_s5_b_im, v_s5_c_re, v_s5_c_im, v_s5_d, v_s5_w_glu, v_s5_b_glu, v_s5_w_out, v_kv_norm, v_kv_w, v_kv_b_f, v_fox_w_in, v_fox_w_out):
    env = dict(locals())
    wts = {n: env[n] for n in _WEIGHTS}
    mom = {n: env["m_" + n] for n in _WEIGHTS}
    var = {n: env["v_" + n] for n in _WEIGHTS}
    me = 4 * lax.axis_index("x") + 2 * lax.axis_index("y") + lax.axis_index("c")
    shard2d = {n: wts[n].reshape(wts[n].shape[-2:]) for n in _BIG}
    comm = _Comm({n: shard2d[n].astype(BF16) for n in _BIG}, {n: wts[n].reshape(1, -1) for n in _SMALL_SHARDED})

    loss_local, grad_x, small = _local_step(
        x[0], loss_target[0], norm_pre, norm_post, kv_norm, kv_b_f, s5_a_re[0], s5_a_im[0], s5_log_dt[0],
        s5_b_re[0], s5_b_im[0], s5_c_re[0], s5_c_im[0], comm)

    small_pack = _pack([small[n] for n in _SMALL] + [loss_local.reshape(1)])
    slice_rows = small_pack.shape[0] // N_DEV
    small_state, small_tok = _exchange_start([small_pack.reshape(N_DEV, slice_rows, LANES)], True, "reduce_small_start")

    res = {}
    for n, recv in comm.received_grads([small_tok, grad_x]):
        outs = _adamw(recv, shard2d[n], mom[n].reshape(shard2d[n].shape), var[n].reshape(shard2d[n].shape), "adamw_" + n)
        res[n] = [o.reshape(wts[n].shape) for o in outs]

    full_shape = {n: (small[n].shape if n in _SMALL_SHARDED else wts[n].shape) for n in _SMALL}

    def spread(n, v):
        if n not in _SMALL_SHARDED:
            return v
        flat = v.reshape(-1)
        return lax.dynamic_update_slice(jnp.zeros(full_shape[n], F32), flat, (me * flat.shape[0],))

    my_sum = _sum_parts(_exchange_wait(small_state, res[_BIG[0]][0], "reduce_small_wait")[0], "sum_small")
    g_all = _exchange([my_sum], False, "gather_small")[0].reshape(1, small_pack.shape[0], LANES)
    packed = [_pack([spread(n, src[n]) for n in _SMALL] + [jnp.zeros((1,), F32)]) for src in (wts, mom, var)]
    outs = _adamw(g_all, *packed, "adamw_small")
    unpacked = [_unpack(o, [full_shape[n] for n in _SMALL] + [(1,)]) for o in outs]
    loss = unpacked[0][-1][0]
    for i, n in enumerate(_SMALL):
        vals = [u[i] for u in unpacked]
        if n in _SMALL_SHARDED:
            k = wts[n].size
            vals = [lax.dynamic_slice(v, (me * k,), (k,)) for v in vals]
        res[n] = [v.reshape(wts[n].shape) for v in vals]

    return (loss, grad_x[None], *[res[n][0] for n in _WEIGHTS], *[res[n][1] for n in _WEIGHTS],
            *[res[n][2] for n in _WEIGHTS], *[res[n][3] for n in _WEIGHTS])
```

```python
import functools
import math

import jax
import jax.numpy as jnp
from jax import lax
from jax.experimental import pallas as pl
from jax.experimental.pallas import tpu as pltpu

F32 = jnp.float32
BF16 = jnp.bfloat16

N_DEV = 8
MESH_AXES = ("x", "y", "c")
S5_GROUP = 16
S5_STATE = 64
LANES = 128
SUBLANES = 8
GROUPS_PER_BLOCK = LANES // S5_GROUP
BLOCK_STATE = GROUPS_PER_BLOCK * S5_STATE
N_SEG = SUBLANES
HEAD_DIM = 128
RMS_EPS = 1e-6
NEG_INF = -1e30
LOG2E = math.log2(math.e)
ADAM_LR = 0.001
ADAM_B1 = 0.9
ADAM_B2 = 0.999
ADAM_EPS = 1e-08
ADAM_WD = 0.01
ADAM_STEP = 10
VMEM_LIMIT = 56 * 1024 * 1024


def _tile(n, pref, quantum=LANES):
    if n <= pref:
        return n
    t = (pref // quantum) * quantum
    while t >= quantum:
        if n % t == 0:
            return t
        t -= quantum
    return n


def _cparams(*sem):
    return pltpu.CompilerParams(dimension_semantics=sem if sem else None, vmem_limit_bytes=VMEM_LIMIT)


_DOT_DIMS = {"nn": ((1,), (0,)), "nt": ((1,), (1,)), "tn": ((0,), (0,))}


def _mm(a, b, mode, out_dtype, name, add=None, scale=None, b_cols=None, after=None, col_slots=False):
    b_shape = b.shape if b_cols is None else (b.shape[0], b_cols[1])
    if mode == "nn":
        (M, K), (K2, N) = a.shape, b_shape
    elif mode == "nt":
        (M, K), (N, K2) = a.shape, b_shape
    else:
        (K, M), (K2, N) = a.shape, b_shape
    assert K == K2, (name, a.shape, b_shape)
    tm, tn, tk = _tile(M, 1024 if K <= 2048 else 512), (N // N_DEV if col_slots else _tile(N, 1024)), _tile(K, 4096)
    nk = K // tk
    dims = (_DOT_DIMS[mode], ((), ()))
    col0 = 0
    if b_cols is not None:
        assert mode != "tn" and b_cols[0] % (tn if mode == "nn" else tk) == 0
        col0 = b_cols[0] // (tn if mode == "nn" else tk)

    def body(*refs):
        a_ref, b_ref = refs[:2]
        c_ref = refs[2] if add is not None else None
        o_ref = refs[2 + (add is not None) + (after is not None)]
        part = lax.dot_general(a_ref[...], b_ref[...], dims, preferred_element_type=F32)

        def finish(r):
            if scale is not None:
                r = r * scale
            if add is not None:
                r = r + c_ref[...]
            o_ref[...] = r.astype(out_dtype)

        if nk == 1:
            finish(part)
            return
        acc = refs[-1]
        k = pl.program_id(2)

        @pl.when(k == 0)
        def _():
            acc[...] = part

        @pl.when(jnp.logical_and(k > 0, k < nk - 1))
        def _():
            acc[...] += part

        @pl.when(k == nk - 1)
        def _():
            finish(acc[...] + part)

    if mode == "tn":
        a_spec = pl.BlockSpec((tk, tm), lambda i, j, k: (k, i))
    else:
        a_spec = pl.BlockSpec((tm, tk), lambda i, j, k: (i, k))
    if mode == "nt":
        b_spec = pl.BlockSpec((tn, tk), lambda i, j, k: (j, k + col0))
    else:
        b_spec = pl.BlockSpec((tk, tn), lambda i, j, k: (k, j + col0))
    o_spec = pl.BlockSpec((tm, tn), lambda i, j, k: (i, j))
    in_specs = [a_spec, b_spec] + ([o_spec] if add is not None else [])
    args = (a, b) + ((add,) if add is not None else ())
    if after is not None:
        in_specs.append(pl.BlockSpec(after.shape, lambda i, j, k: (0, 0)))
        args += (after,)
    out_shape = jax.ShapeDtypeStruct((M, N), out_dtype)
    if col_slots:
        assert add is None
        o_spec = pl.BlockSpec((None, tm, tn), lambda i, j, k: (j, i, 0))
        out_shape = jax.ShapeDtypeStruct((N_DEV, M, tn), out_dtype)
    return pl.pallas_call(
        body, name=name, grid=(M // tm, N // tn, nk),
        in_specs=in_specs, out_specs=o_spec,
        out_shape=out_shape,
        scratch_shapes=[pltpu.VMEM((tm, tn), F32)] if nk > 1 else [],
        compiler_params=_cparams("parallel", "parallel", "arbitrary"),
    )(*args)


class _NatIn:
    def __init__(self, ref):
        self.ref = ref

    def __getitem__(self, idx):
        v = jnp.swapaxes(self.ref[...], 0, 1)
        return v.reshape(v.shape[0] * N_SEG, v.shape[2])


class _NatOut:
    def __init__(self, ref):
        self.ref = ref

    def __setitem__(self, idx, val):
        self.ref[...] = jnp.swapaxes(val.reshape(val.shape[0] // N_SEG, N_SEG, val.shape[1]), 0, 1)


def _rowcall(body, name, n_rows, ins, outs, tile_rows=256):
    tr = _tile(n_rows, tile_rows, SUBLANES * 2)
    n_in = len(ins)
    in_kinds = [k for _, k in ins]
    kinds = [k for _, _, k in outs]

    def kern(*refs):
        @pl.when(pl.program_id(0) == 0)
        def _():
            for r, kind in zip(refs[n_in:], kinds):
                if kind == "acc":
                    r[...] = jnp.zeros_like(r)

        wrapped = [_NatIn(r) if k == "nat" else r for r, k in zip(refs[:n_in], in_kinds)]
        wrapped += [_NatOut(r) if k == "nat" else r for r, k in zip(refs[n_in:], kinds)]
        body(*wrapped)

    in_specs, args = [], []
    for arr, kind in ins:
        if kind == "row":
            in_specs.append(pl.BlockSpec((tr, arr.shape[1]), lambda i: (i, 0)))
        elif kind == "nat":
            in_specs.append(pl.BlockSpec((N_SEG, tr // N_SEG, arr.shape[1]), lambda i: (0, i, 0)))
            arr = arr.reshape(N_SEG, n_rows // N_SEG, arr.shape[1])
        else:
            in_specs.append(pl.BlockSpec(arr.shape, lambda i, nd=arr.ndim: (0,) * nd))
        args.append(arr)
    out_specs, out_shape = [], []
    for width, dtype, kind in outs:
        if kind == "row":
            out_specs.append(pl.BlockSpec((tr, width), lambda i: (i, 0)))
            out_shape.append(jax.ShapeDtypeStruct((n_rows, width), dtype))
        elif kind == "right":
            out_specs.append(pl.BlockSpec((tr, width), lambda i: (i, 1)))
            out_shape.append(jax.ShapeDtypeStruct((n_rows, 2 * width), dtype))
        elif kind == "nat":
            out_specs.append(pl.BlockSpec((N_SEG, tr // N_SEG, width), lambda i: (0, i, 0)))
            out_shape.append(jax.ShapeDtypeStruct((N_SEG, n_rows // N_SEG, width), dtype))
        else:
            out_specs.append(pl.BlockSpec((1, width), lambda i: (0, 0)))
            out_shape.append(jax.ShapeDtypeStruct((1, width), F32))
    res = pl.pallas_call(
        kern, name=name, grid=(n_rows // tr,), in_specs=in_specs, out_specs=out_specs, out_shape=out_shape,
        compiler_params=_cparams("arbitrary"),
    )(*args)
    return [r.reshape(n_rows, r.shape[2]) if k == "nat" else r for r, k in zip(res, kinds)]


def _rstd(x):
    return lax.rsqrt(jnp.mean(x * x, axis=-1, keepdims=True) + RMS_EPS)


def _rms_bwd(x, g, dy):
    xh = x * _rstd(x)
    dxh = dy * g
    dx = _rstd(x) * (dxh - xh * jnp.mean(dxh * xh, axis=-1, keepdims=True))
    return dx, jnp.sum(dy * xh, axis=0, keepdims=True)


def _silu(z):
    return z * jax.nn.sigmoid(z)


def _norm_cast(x, g, name, x_kind="row"):
    def body(x_ref, g_ref, o_ref):
        x = x_ref[...]
        o_ref[...] = (x * _rstd(x) * g_ref[...]).astype(BF16)

    return _rowcall(body, name, x.shape[0], [(x, x_kind), (g, "full")], [(x.shape[1], BF16, "row")])[0]


def _resid_norm2(x, r0, g_kv, g_pre, name):
    def body(x_ref, r_ref, gk_ref, gp_ref, h_ref, nk_ref, np_ref):
        h = x_ref[...] + r_ref[...]
        h_ref[...] = h
        hn = h * _rstd(h)
        nk_ref[...] = (hn * gk_ref[...]).astype(BF16)
        np_ref[...] = (hn * gp_ref[...]).astype(BF16)

    d = x.shape[1]
    return _rowcall(body, name, x.shape[0], [(x, "row"), (r0, "row"), (g_kv, "full"), (g_pre, "full")],
                    [(d, F32, "row"), (d, BF16, "row"), (d, BF16, "row")])


def _post_norm(o, g, name, out_kind="row"):
    def body(o_ref, g_ref, r_ref):
        o = o_ref[...]
        r_ref[...] = o * _rstd(o) * g_ref[...]

    return _rowcall(body, name, o.shape[0], [(o, "row"), (g, "full")], [(o.shape[1], F32, out_kind)])[0]


def _post_norm_loss(o, g, h1, target, name):
    d = o.shape[1]

    def body(o_ref, g_ref, h_ref, t_ref, dh_ref, acc_ref):
        o = o_ref[...]
        e = h_ref[...] + o * _rstd(o) * g_ref[...] - t_ref[...]
        dh_ref[...] = e * (1.0 / d)
        acc_ref[...] += jnp.sum(e * e, axis=0, keepdims=True)

    return _rowcall(body, name, o.shape[0], [(o, "row"), (g, "full"), (h1, "row"), (target, "row")],
                    [(d, F32, "row"), (d, F32, "acc")])


def _post_norm_bwd(dy, o, g, name, dy_kind="row"):
    def body(dy_ref, o_ref, g_ref, do_ref, dg_ref):
        dx, dg = _rms_bwd(o_ref[...], g_ref[...], dy_ref[...])
        do_ref[...] = dx.astype(BF16)
        dg_ref[...] += dg

    d = o.shape[1]
    return _rowcall(body, name, o.shape[0], [(dy, dy_kind), (o, "row"), (g, "full")], [(d, BF16, "row"), (d, F32, "acc")])


def _gate_mul(o, z, name):
    def body(o_ref, z_ref, r_ref):
        r_ref[...] = (o_ref[...] * _silu(z_ref[...])).astype(BF16)

    return _rowcall(body, name, o.shape[0], [(o, "row"), (z, "row")], [(o.shape[1], BF16, "row")])[0]


def _gate_bwd(d_oz, o, z, name):
    def body(d_ref, o_ref, z_ref, do_ref, dz_ref):
        _, vjp = jax.vjp(lambda o, z: o * _silu(z), o_ref[...], z_ref[...])
        do, dz = vjp(d_ref[...])
        do_ref[...] = do.astype(BF16)
        dz_ref[...] = dz.astype(BF16)

    w = o.shape[1]
    return _rowcall(body, name, o.shape[0], [(d_oz, "row"), (o, "row"), (z, "row")], [(w, BF16, "row"), (w, BF16, "right")])


def _norm_bwd2(dh2, h1, dxn1, dhn_kv, g_pre, g_kv, name):
    def body(dh2_ref, h_ref, d1_ref, dk_ref, gp_ref, gk_ref, dh1_ref, dgp_ref, dgk_ref):
        h = h_ref[...]
        dx1, dg1 = _rms_bwd(h, gp_ref[...], d1_ref[...])
        dxk, dgk = _rms_bwd(h, gk_ref[...], dk_ref[...])
        dh1_ref[...] = dh2_ref[...] + dx1 + dxk
        dgp_ref[...] += dg1
        dgk_ref[...] += dgk

    d = h1.shape[1]
    return _rowcall(body, name, h1.shape[0],
                    [(dh2, "row"), (h1, "row"), (dxn1, "row"), (dhn_kv, "row"), (g_pre, "full"), (g_kv, "full")],
                    [(d, F32, "row"), (d, F32, "acc"), (d, F32, "acc")])


def _norm_bwd1(dres, x, dxn, g, name):
    def body(dr_ref, x_ref, dn_ref, g_ref, dx_ref, dg_ref):
        dx, dg = _rms_bwd(x_ref[...], g_ref[...], dn_ref[...])
        dx_ref[...] = dr_ref[...] + dx
        dg_ref[...] += dg

    d = x.shape[1]
    return _rowcall(body, name, x.shape[0], [(dres, "nat"), (x, "nat"), (dxn, "row"), (g, "full")],
                    [(d, F32, "nat"), (d, F32, "acc")])


def _gelu_cast(y, name):
    def body(y_ref, o_ref):
        o_ref[...] = jax.nn.gelu(y_ref[...]).astype(BF16)

    return _rowcall(body, name, y.shape[0], [(y, "row")], [(y.shape[1], BF16, "row")])[0]


def _s5_gate(y_ssm, gp, b_glu, z, name):
    def body(y_ref, gp_ref, b_ref, z_ref, o_ref):
        yg = jax.nn.gelu(y_ref[...])
        o_ref[...] = (yg * jax.nn.sigmoid(gp_ref[...] + b_ref[...]) * _silu(z_ref[...])).astype(BF16)

    return _rowcall(body, name, y_ssm.shape[0], [(y_ssm, "row"), (gp, "row"), (b_glu, "full"), (z, "row")],
                    [(y_ssm.shape[1], BF16, "row")])[0]


def _s5_gate_bwd(dy3, y_ssm, gp, b_glu, z, name):
    def body(d_ref, y_ref, gp_ref, b_ref, z_ref, dz_ref, dgp_ref, dyg_ref, db_ref):
        yg = jax.nn.gelu(y_ref[...])
        _, vjp = jax.vjp(lambda yg, gp, z: yg * jax.nn.sigmoid(gp) * _silu(z), yg, gp_ref[...] + b_ref[...], z_ref[...])
        dyg, dgp, dz = vjp(d_ref[...])
        dz_ref[...] = dz.astype(BF16)
        dgp_ref[...] = dgp.astype(BF16)
        dyg_ref[...] = dyg
        db_ref[...] += jnp.sum(dgp, axis=0, keepdims=True)

    w = y_ssm.shape[1]
    return _rowcall(body, name, y_ssm.shape[0],
                    [(dy3, "row"), (y_ssm, "row"), (gp, "row"), (b_glu, "full"), (z, "row")],
                    [(w, BF16, "right"), (w, BF16, "row"), (w, F32, "row"), (w, F32, "acc")])


def _gelu_bwd(dyg, y_ssm, name):
    def body(d_ref, y_ref, o_ref):
        _, vjp = jax.vjp(jax.nn.gelu, y_ref[...])
        o_ref[...] = vjp(d_ref[...])[0]

    return _rowcall(body, name, y_ssm.shape[0], [(dyg, "row"), (y_ssm, "row")], [(y_ssm.shape[1], F32, "row")])[0]


def _concat_cast(a, b, name):
    def body(a_ref, b_ref, o_ref):
        w = a_ref.shape[1]
        o_ref[:, :w] = a_ref[...].astype(BF16)
        o_ref[:, w:] = b_ref[...].astype(BF16)

    return _rowcall(body, name, a.shape[0], [(a, "row"), (b, "row")], [(a.shape[1] + b.shape[1], BF16, "row")])[0]


def _disc(ar, ai, ldt):
    dt = jnp.exp(ldt)
    mag = jnp.exp(ar * dt)
    abr = mag * jnp.cos(ai * dt)
    abi = mag * jnp.sin(ai * dt)
    den = ar * ar + ai * ai
    nr = abr - 1.0
    return abr, abi, (nr * ar + abi * ai) / den, (abi * ar - nr * ai) / den


def _s5_disc_fwd(a_re, a_im, ldt):
    def body(ar, ai, ld, o1, o2, o3, o4):
        o1[...], o2[...], o3[...], o4[...] = _disc(ar[...], ai[...], ld[...])

    sh = jax.ShapeDtypeStruct(a_re.shape, F32)
    return pl.pallas_call(body, name="s5_disc_fwd", out_shape=(sh, sh, sh, sh))(a_re, a_im, ldt)


def _s5_disc_bwd(a_re, a_im, ldt, d_abr, d_abi, d_cr, d_ci):
    def body(ar, ai, ld, g1, g2, g3, g4, o1, o2, o3):
        _, vjp = jax.vjp(_disc, ar[...], ai[...], ld[...])
        o1[...], o2[...], o3[...] = vjp((g1[...], g2[...], g3[...], g4[...]))

    sh = jax.ShapeDtypeStruct(a_re.shape, F32)
    return pl.pallas_call(body, name="s5_disc_bwd", out_shape=(sh, sh, jax.ShapeDtypeStruct(ldt.shape, F32)))(
        a_re, a_im, ldt, d_abr, d_abi, d_cr, d_ci)


def _bbar(cr, ci, br, bi):
    return cr * br - ci * bi, cr * bi + ci * br


def _s5_bbar_fwd(cr_col, ci_col, b_re, b_im):
    def body(cr, ci, br, bi, o1, o2):
        o1[...], o2[...] = _bbar(cr[...], ci[...], br[...], bi[...])

    w = b_re.shape[1]
    return _rowcall(body, "s5_bbar_fwd", b_re.shape[0], [(cr_col, "row"), (ci_col, "row"), (b_re, "row"), (b_im, "row")],
                    [(w, F32, "row"), (w, F32, "row")], tile_rows=1024)


def _s5_bbar_bwd(cr_col, ci_col, b_re, b_im, d_re, d_im):
    def body(cr, ci, br, bi, g1, g2, o1, o2, o3, o4):
        _, vjp = jax.vjp(_bbar, cr[...], ci[...], br[...], bi[...])
        o1[...], o2[...], o3[...], o4[...] = vjp((g1[...], g2[...]))

    w = b_re.shape[1]
    return _rowcall(body, "s5_bbar_bwd", b_re.shape[0],
                    [(cr_col, "row"), (ci_col, "row"), (b_re, "row"), (b_im, "row"), (d_re, "row"), (d_im, "row")],
                    [(1, F32, "row"), (1, F32, "row"), (w, F32, "row"), (w, F32, "row")], tile_rows=1024)


def _block_diag_in(t):
    g, p, c = t.shape
    nb = g // GROUPS_PER_BLOCK
    t4 = t.reshape(nb, GROUPS_PER_BLOCK, p, c).transpose(0, 1, 3, 2)
    eye = jnp.eye(GROUPS_PER_BLOCK, dtype=t.dtype)
    return (t4[:, :, :, None, :] * eye[None, :, None, :, None]).reshape(nb, GROUPS_PER_BLOCK * c, GROUPS_PER_BLOCK * p)


def _block_diag_in_extract(d, p, c):
    nb = d.shape[0]
    d5 = d.reshape(nb, GROUPS_PER_BLOCK, c, GROUPS_PER_BLOCK, p)
    diag = jnp.stack([d5[:, g, :, g, :] for g in range(GROUPS_PER_BLOCK)], axis=1)
    return diag.transpose(0, 1, 3, 2).reshape(nb * GROUPS_PER_BLOCK, p, c)


def _block_diag_out(t):
    g, c, p = t.shape
    nb = g // GROUPS_PER_BLOCK
    t4 = t.reshape(nb, GROUPS_PER_BLOCK, c, p).transpose(0, 1, 3, 2)
    eye = jnp.eye(GROUPS_PER_BLOCK, dtype=t.dtype)
    return (t4[:, :, :, None, :] * eye[None, :, None, :, None]).reshape(nb, GROUPS_PER_BLOCK * p, GROUPS_PER_BLOCK * c)


def _block_diag_out_extract(d, c, p):
    nb = d.shape[0]
    d5 = d.reshape(nb, GROUPS_PER_BLOCK, p, GROUPS_PER_BLOCK, c)
    diag = jnp.stack([d5[:, g, :, g, :] for g in range(GROUPS_PER_BLOCK)], axis=1)
    return diag.transpose(0, 1, 3, 2).reshape(nb * GROUPS_PER_BLOCK, c, p)


def _scan_step(ar, ai, hr, hi, xr, xi):
    return ar * hr - ai * hi + xr, ar * hi + ai * hr + xi


def _s5_scan_fwd(u, bd_re, bd_im, cd_re, cd_im, ab_re, ab_im, init_re, init_im, d_row, full, name):
    s, w = u.shape
    nb = w // LANES
    rows = _tile(s, 512, SUBLANES)
    nc = s // rows
    steps = rows // N_SEG
    ns = nb * BLOCK_STATE

    def body(u_ref, bdr, bdi, cdr, cdi, ar_ref, ai_ref, ir_ref, ii_ref, d_ref, *outs):
        if full:
            y_ref, yg_ref, hr_ref, hi_ref, er_ref, ei_ref, cr, ci = outs
        else:
            er_ref, ei_ref, hr_ref, hi_ref, cr, ci = outs
        c = pl.program_id(1)

        @pl.when(c == 0)
        def _():
            cr[...] = ir_ref[...]
            ci[...] = ii_ref[...]

        ub = u_ref[...].astype(BF16)
        hr_ref[...] = jnp.dot(ub, bdr[...], preferred_element_type=F32)
        hi_ref[...] = jnp.dot(ub, bdi[...], preferred_element_type=F32)
        ar, ai = ar_ref[...], ai_ref[...]

        def step(j, carry):
            off = pl.multiple_of(j * N_SEG, N_SEG)
            nr, ni = _scan_step(ar, ai, carry[0], carry[1], hr_ref[pl.ds(off, N_SEG), :], hi_ref[pl.ds(off, N_SEG), :])
            hr_ref[pl.ds(off, N_SEG), :] = nr
            hi_ref[pl.ds(off, N_SEG), :] = ni
            return nr, ni

        hr, hi = lax.fori_loop(0, steps, step, (cr[...], ci[...]), unroll=8)
        cr[...] = hr
        ci[...] = hi
        if full:
            y = (jnp.dot(hr_ref[...].astype(BF16), cdr[...], preferred_element_type=F32)
                 + jnp.dot(hi_ref[...].astype(BF16), cdi[...], preferred_element_type=F32)
                 + d_ref[...] * u_ref[...])
            y_ref[...] = y
            yg_ref[...] = jax.nn.gelu(y).astype(BF16)

        @pl.when(c == nc - 1)
        def _():
            er_ref[...] = hr
            ei_ref[...] = hi

    blk3 = lambda a: pl.BlockSpec((None,) + a.shape[1:], lambda k, c: (k, 0, 0))
    seg = pl.BlockSpec((N_SEG, BLOCK_STATE), lambda k, c: (0, k))
    st = pl.BlockSpec((rows, BLOCK_STATE), lambda k, c: (c, k))
    in_specs = [pl.BlockSpec((rows, LANES), lambda k, c: (c, k)), blk3(bd_re), blk3(bd_im), blk3(cd_re), blk3(cd_im),
                seg, seg, seg, seg, pl.BlockSpec((1, LANES), lambda k, c: (0, k))]
    seg_shape = jax.ShapeDtypeStruct((N_SEG, ns), F32)
    st_shape = jax.ShapeDtypeStruct((s, ns), F32)
    carry = [pltpu.VMEM((N_SEG, BLOCK_STATE), F32)] * 2
    if full:
        ych = pl.BlockSpec((rows, LANES), lambda k, c: (c, k))
        out_specs = [ych, ych, st, st, seg, seg]
        out_shape = [jax.ShapeDtypeStruct((s, w), F32), jax.ShapeDtypeStruct((s, w), BF16), st_shape, st_shape, seg_shape, seg_shape]
        scratch = carry
    else:
        out_specs = [seg, seg]
        out_shape = [seg_shape, seg_shape]
        scratch = [pltpu.VMEM((rows, BLOCK_STATE), F32)] * 2 + carry
    return pl.pallas_call(
        body, name=name, grid=(nb, nc), in_specs=in_specs, out_specs=out_specs, out_shape=out_shape,
        scratch_shapes=scratch, compiler_params=_cparams("parallel", "arbitrary"),
    )(u, bd_re, bd_im, cd_re, cd_im, ab_re, ab_im, init_re, init_im, d_row)


def _s5_seg_fix(e_re, e_im, ab_re, ab_im, seg_len, reverse, name):
    assert seg_len & (seg_len - 1) == 0

    def body(er, ei, ar, ai, o_re, o_im):
        pr, pi = ar[0:1, :], ai[0:1, :]
        for _ in range(int(math.log2(seg_len))):
            pr, pi = pr * pr - pi * pi, 2.0 * pr * pi
        tr = jnp.zeros_like(pr)
        ti = jnp.zeros_like(pr)
        order = list(range(N_SEG - 1, -1, -1)) if reverse else list(range(N_SEG))
        for n, sgm in enumerate(order):
            o_re[sgm:sgm + 1, :] = tr
            o_im[sgm:sgm + 1, :] = ti
            if n < N_SEG - 1:
                tr, ti = _scan_step(pr, pi, tr, ti, er[sgm:sgm + 1, :], ei[sgm:sgm + 1, :])

    sh = jax.ShapeDtypeStruct(e_re.shape, F32)
    return pl.pallas_call(body, name=name, out_shape=(sh, sh))(e_re, e_im, ab_re, ab_im)


def _s5_scan_bwd(dy, u, h_re, h_im, bd_re, bd_im, cd_re, cd_im, ab_re, ab_imn, gin_re, gin_im, d_row, full, name, duz=None):
    s, w = u.shape
    nb = w // LANES
    rows = _tile(s, 512, SUBLANES)
    nc = s // rows
    steps = rows // N_SEG
    ns = nb * BLOCK_STATE

    def body(dy_ref, u_ref, hr_ref, hi_ref, bdr, bdi, cdr, cdi, ar_ref, ai_ref, ir_ref, ii_ref, d_ref, *outs):
        if full:
            _, du_ref, dbr_ref, dbi_ref, dcr_ref, dci_ref, dar_ref, dai_ref, dd_ref, gr, gi, accr, acci = outs
        else:
            er_ref, ei_ref, gr, gi = outs
        c = pl.program_id(1)

        @pl.when(c == 0)
        def _():
            gr[pl.ds(rows, N_SEG), :] = ir_ref[...]
            gi[pl.ds(rows, N_SEG), :] = ii_ref[...]
            if full:
                for r in (dbr_ref, dbi_ref, dcr_ref, dci_ref, dd_ref, accr, acci):
                    r[...] = jnp.zeros_like(r)

        dyb = dy_ref[...].astype(BF16)
        nt = (_DOT_DIMS["nt"], ((), ()))
        tn = (_DOT_DIMS["tn"], ((), ()))
        gr[pl.ds(0, rows), :] = lax.dot_general(dyb, cdr[...], nt, preferred_element_type=F32)
        gi[pl.ds(0, rows), :] = lax.dot_general(dyb, cdi[...], nt, preferred_element_type=F32)
        ar, ai = ar_ref[...], ai_ref[...]

        def step(jj, carry):
            off = pl.multiple_of((steps - 1 - jj) * N_SEG, N_SEG)
            nr, ni = _scan_step(ar, ai, carry[0], carry[1], gr[pl.ds(off, N_SEG), :], gi[pl.ds(off, N_SEG), :])
            gr[pl.ds(off, N_SEG), :] = nr
            gi[pl.ds(off, N_SEG), :] = ni
            return nr, ni

        g0r, g0i = lax.fori_loop(0, steps, step, (gr[pl.ds(rows, N_SEG), :], gi[pl.ds(rows, N_SEG), :]), unroll=8)
        if full:
            hr, hi = hr_ref[...], hi_ref[...]
            gnr, gni = gr[pl.ds(N_SEG, rows), :], gi[pl.ds(N_SEG, rows), :]
            accr[...] += jnp.sum((gnr * hr + gni * hi).reshape(steps, N_SEG, BLOCK_STATE), axis=0)
            acci[...] += jnp.sum((gni * hr - gnr * hi).reshape(steps, N_SEG, BLOCK_STATE), axis=0)
        gr[pl.ds(rows, N_SEG), :] = g0r
        gi[pl.ds(rows, N_SEG), :] = g0i
        if full:
            ub = u_ref[...].astype(BF16)
            gbr, gbi = gr[pl.ds(0, rows), :].astype(BF16), gi[pl.ds(0, rows), :].astype(BF16)
            dcr_ref[...] += lax.dot_general(hr.astype(BF16), dyb, tn, preferred_element_type=F32)
            dci_ref[...] += lax.dot_general(hi.astype(BF16), dyb, tn, preferred_element_type=F32)
            dbr_ref[...] += lax.dot_general(ub, gbr, tn, preferred_element_type=F32)
            dbi_ref[...] += lax.dot_general(ub, gbi, tn, preferred_element_type=F32)
            du_ref[...] = (lax.dot_general(gbr, bdr[...], nt, preferred_element_type=F32)
                           + lax.dot_general(gbi, bdi[...], nt, preferred_element_type=F32)
                           + d_ref[...] * dy_ref[...]).astype(BF16)
            dd_ref[...] += jnp.sum(dy_ref[...] * u_ref[...], axis=0, keepdims=True)

        @pl.when(c == nc - 1)
        def _():
            if full:
                dar_ref[...] = jnp.sum(accr[...], axis=0, keepdims=True)
                dai_ref[...] = jnp.sum(acci[...], axis=0, keepdims=True)
            else:
                er_ref[...] = g0r
                ei_ref[...] = g0i

    rev = lambda k, c: (nc - 1 - c, k)
    blk3 = lambda a: pl.BlockSpec((None,) + a.shape[1:], lambda k, c: (k, 0, 0))
    seg = pl.BlockSpec((N_SEG, BLOCK_STATE), lambda k, c: (0, k))
    st = pl.BlockSpec((rows, BLOCK_STATE), rev)
    ch = pl.BlockSpec((rows, LANES), rev)
    vec = pl.BlockSpec((1, LANES), lambda k, c: (0, k))
    if not full:
        st = pl.BlockSpec((rows, BLOCK_STATE), lambda k, c: (0, k))
    in_specs = [ch, ch if full else pl.BlockSpec((rows, LANES), lambda k, c: (0, k)), st, st,
                blk3(bd_re), blk3(bd_im), blk3(cd_re), blk3(cd_im), seg, seg, seg, seg, vec]
    args = [dy, u, h_re, h_im, bd_re, bd_im, cd_re, cd_im, ab_re, ab_imn, gin_re, gin_im, d_row]
    gbuf = [pltpu.VMEM((rows + N_SEG, BLOCK_STATE), F32)] * 2
    if full:
        row1 = pl.BlockSpec((1, BLOCK_STATE), lambda k, c: (0, k))
        out_specs = [ch, blk3(bd_re), blk3(bd_im), blk3(cd_re), blk3(cd_im), row1, row1, vec]
        out_shape = [jax.ShapeDtypeStruct(duz.shape, BF16),
                     jax.ShapeDtypeStruct(bd_re.shape, F32), jax.ShapeDtypeStruct(bd_im.shape, F32),
                     jax.ShapeDtypeStruct(cd_re.shape, F32), jax.ShapeDtypeStruct(cd_im.shape, F32),
                     jax.ShapeDtypeStruct((1, ns), F32), jax.ShapeDtypeStruct((1, ns), F32),
                     jax.ShapeDtypeStruct((1, w), F32)]
        scratch = gbuf + [pltpu.VMEM((N_SEG, BLOCK_STATE), F32)] * 2
        in_specs.append(pl.BlockSpec(memory_space=pl.ANY))
        args.append(duz)
        aliases = {len(args) - 1: 0}
    else:
        out_specs = [seg, seg]
        out_shape = [jax.ShapeDtypeStruct((N_SEG, ns), F32)] * 2
        scratch = gbuf
        aliases = {}
    return pl.pallas_call(
        body, name=name, grid=(nb, nc), in_specs=in_specs, out_specs=out_specs, out_shape=out_shape,
        input_output_aliases=aliases, scratch_shapes=scratch, compiler_params=_cparams("parallel", "arbitrary"),
    )(*args)


def _log_sigmoid(x):
    return jnp.minimum(x, 0.0) - jnp.log(1.0 + jnp.exp(-jnp.abs(x)))


def _tri(n, upper):
    r = lax.broadcasted_iota(jnp.int32, (n, n), 0)
    c = lax.broadcasted_iota(jnp.int32, (n, n), 1)
    return jnp.where((c >= r) if upper else (r >= c), 1.0, 0.0).astype(F32)


def _cum_fwd(f_logit, b_row, name):
    s, w = f_logit.shape
    t = _tile(s, 256, SUBLANES)

    def body(f_ref, b_ref, o_ref, carry):
        @pl.when(pl.program_id(0) == 0)
        def _():
            carry[...] = jnp.zeros_like(carry)

        lf = _log_sigmoid(f_ref[...] + b_ref[...])
        cum = jnp.dot(_tri(t, False), lf, precision=lax.Precision.HIGHEST, preferred_element_type=F32) + carry[...]
        o_ref[...] = cum * LOG2E
        carry[...] = cum[t - 1:t, :]

    return pl.pallas_call(
        body, name=name, grid=(s // t,),
        in_specs=[pl.BlockSpec((t, w), lambda i: (i, 0)), pl.BlockSpec((1, w), lambda i: (0, 0))],
        out_specs=pl.BlockSpec((t, w), lambda i: (i, 0)), out_shape=jax.ShapeDtypeStruct((s, w), F32),
        scratch_shapes=[pltpu.VMEM((1, w), F32)], compiler_params=_cparams("arbitrary"),
    )(f_logit, b_row)


def _cum_bwd(dcq, dck, f_logit, b_row, name):
    s, w = f_logit.shape
    t = _tile(s, 256, SUBLANES)
    nt = s // t

    def body(q_ref, k_ref, f_ref, b_ref, df_ref, db_ref, carry):
        @pl.when(pl.program_id(0) == 0)
        def _():
            carry[...] = jnp.zeros_like(carry)
            db_ref[...] = jnp.zeros_like(db_ref)

        dc = q_ref[...] - k_ref[...]
        rc = jnp.dot(_tri(t, True), dc, precision=lax.Precision.HIGHEST, preferred_element_type=F32) + carry[...]
        carry[...] = rc[0:1, :]
        df = rc * (1.0 - jax.nn.sigmoid(f_ref[...] + b_ref[...]))
        df_ref[...] = df.astype(BF16)
        db_ref[...] += jnp.sum(df, axis=0, keepdims=True)

    rev = pl.BlockSpec((t, w), lambda i: (nt - 1 - i, 0))
    one = pl.BlockSpec((1, w), lambda i: (0, 0))
    return pl.pallas_call(
        body, name=name, grid=(nt,), in_specs=[rev, rev, rev, one], out_specs=[rev, one],
        out_shape=[jax.ShapeDtypeStruct((s, w), BF16), jax.ShapeDtypeStruct((1, w), F32)],
        scratch_shapes=[pltpu.VMEM((1, w), F32)], compiler_params=_cparams("arbitrary"),
    )(dcq, dck, f_logit, b_row)


def _head_col(cum_tile, h):
    lane = lax.broadcasted_iota(jnp.int32, cum_tile.shape, 1)
    return jnp.sum(jnp.where(lane == h, cum_tile, 0.0), axis=1, keepdims=True)


def _attn_tiles(s):
    return _tile(s, 512, LANES)


def _exp2_rows(sc, sub):
    return jnp.concatenate([jnp.exp2(sc[:, b * LANES:(b + 1) * LANES] - sub) for b in range(sc.shape[1] // LANES)], axis=1)


def _row_of(rep):
    return jnp.transpose(rep)[0:1, :]


def _causal(sc, keys_on_rows):
    r = lax.broadcasted_iota(jnp.int32, sc.shape, 0)
    c = lax.broadcasted_iota(jnp.int32, sc.shape, 1)
    return jnp.where((r <= c) if keys_on_rows else (c <= r), sc, NEG_INF)


def _fox_fwd(q2, kv, cum2_t, z, name):
    s, w = q2.shape
    nh = w // HEAD_DIM
    tq = _attn_tiles(s)
    nq = s // tq
    nt = (_DOT_DIMS["nt"], ((), ()))

    def body(q_ref, k_ref, v_ref, ct_ref, z_ref, o_ref, oz_ref, lse_row_ref, m_s, acc_s, vaug, s_buf):
        i = pl.program_id(1)

        @pl.when(i == 0)
        def _():
            vaug[:, :HEAD_DIM] = v_ref[...]
            vaug[:, HEAD_DIM:] = jnp.ones((s, LANES), BF16)

        qb = q_ref[...]
        m_s[...] = jnp.full_like(m_s, NEG_INF)
        acc_s[...] = jnp.zeros_like(acc_s)

        def scores(j):
            off = pl.multiple_of(j * tq, tq)
            return lax.dot_general(qb, k_ref[pl.ds(off, tq), :], nt, preferred_element_type=F32) - ct_ref[:, pl.ds(off, tq)]

        def softmax_pv(j, sc):
            m_old = m_s[...]
            m_new = jnp.maximum(m_old, jnp.max(sc, axis=1, keepdims=True))
            p = _exp2_rows(sc, m_new)
            alpha = jnp.exp2(m_old - m_new)
            pv = jnp.dot(p.astype(BF16), vaug[pl.ds(pl.multiple_of(j * tq, tq), tq), :], preferred_element_type=F32)
            acc_s[...] = jnp.concatenate([alpha, alpha], axis=1) * acc_s[...] + pv
            m_s[...] = m_new

        s_buf[...] = scores(0)

        def loop(j, carry):
            nxt = scores(j + 1)
            softmax_pv(j, s_buf[...])
            s_buf[...] = nxt
            return carry

        lax.fori_loop(0, i, loop, 0)
        softmax_pv(i, _causal(s_buf[...], False))
        l = acc_s[:, HEAD_DIM:]
        o = acc_s[:, :HEAD_DIM] / l
        o_ref[...] = o
        oz_ref[...] = (o * _silu(z_ref[...])).astype(BF16)
        lse_row_ref[...] = _row_of(m_s[...] + jnp.log(l) * LOG2E)

    return pl.pallas_call(
        body, name=name, grid=(nh, nq),
        in_specs=[pl.BlockSpec((tq, HEAD_DIM), lambda h, i: (i, h)),
                  pl.BlockSpec((s, HEAD_DIM), lambda h, i: (0, h)),
                  pl.BlockSpec((s, HEAD_DIM), lambda h, i: (0, nh + h)),
                  pl.BlockSpec((None, 1, s), lambda h, i: (h, 0, 0)),
                  pl.BlockSpec((tq, HEAD_DIM), lambda h, i: (i, h))],
        out_specs=[pl.BlockSpec((tq, HEAD_DIM), lambda h, i: (i, h)),
                   pl.BlockSpec((tq, HEAD_DIM), lambda h, i: (i, h)),
                   pl.BlockSpec((None, 1, tq), lambda h, i: (h, 0, i))],
        out_shape=[jax.ShapeDtypeStruct((s, w), F32), jax.ShapeDtypeStruct((s, w), BF16),
                   jax.ShapeDtypeStruct((nh, 1, s), F32)],
        scratch_shapes=[pltpu.VMEM((tq, LANES), F32), pltpu.VMEM((tq, HEAD_DIM + LANES), F32),
                        pltpu.VMEM((s, HEAD_DIM + LANES), BF16), pltpu.VMEM((tq, tq), F32)],
        compiler_params=_cparams("arbitrary", "arbitrary"),
    )(q2, kv, kv, cum2_t, z)


def _fox_bwd(q2, kv, do, o, lse2_t, cum2, dqz, name):
    s, w = q2.shape
    nh = w // HEAD_DIM
    tk = _attn_tiles(s)
    nk = s // tk
    scale = HEAD_DIM ** -0.5
    nt = (_DOT_DIMS["nt"], ((), ()))
    tn = (_DOT_DIMS["tn"], ((), ()))

    def body(q_ref, k_ref, v_ref, do_ref, o_ref, lse_ref, c_ref, _, dk_ref, dv_ref, dq_ref, dcq_ref, dck_ref,
             dk_s, dv_s, dc_s, dq_s, dcq_s, dl_s, s_buf, dp_buf):
        h, j = pl.program_id(0), pl.program_id(1)

        @pl.when(j == 0)
        def _():
            dq_s[...] = jnp.zeros_like(dq_s)
            dcq_s[...] = jnp.zeros_like(dcq_s)
            for i in range(nk):
                rows = pl.ds(i * tk, tk)
                d = jnp.sum(do_ref[rows, :].astype(F32) * o_ref[rows, :], axis=1, keepdims=True)
                dl_s[:, i * tk:(i + 1) * tk] = _row_of(jnp.broadcast_to(d, (tk, LANES)))

        kb = k_ref[...]
        vb = v_ref[...]
        ck = jnp.broadcast_to(_head_col(c_ref[...], h), (tk, LANES))
        dk_s[...] = jnp.zeros_like(dk_s)
        dv_s[...] = jnp.zeros_like(dv_s)
        dc_s[...] = jnp.zeros_like(dc_s)

        def scores(i):
            off = pl.multiple_of(i * tk, tk)
            sc = lax.dot_general(kb, q_ref[pl.ds(off, tk), :], nt, preferred_element_type=F32) - lse_ref[:, pl.ds(off, tk)]
            dp = lax.dot_general(vb, do_ref[pl.ds(off, tk), :], nt, preferred_element_type=F32) - dl_s[:, pl.ds(off, tk)]
            return sc, dp

        def accumulate(i, sc, dp):
            off = pl.multiple_of(i * tk, tk)
            p = _exp2_rows(sc, ck)
            dv_s[...] += jnp.dot(p.astype(BF16), do_ref[pl.ds(off, tk), :], preferred_element_type=F32)
            ds = p * dp
            dsb = ds.astype(BF16)
            dk_s[...] += jnp.dot(dsb, q_ref[pl.ds(off, tk), :], preferred_element_type=F32)
            dq_s[pl.ds(off, tk), :] += lax.dot_general(dsb, kb, tn, preferred_element_type=F32)
            dcq_s[:, pl.ds(off, tk)] += jnp.sum(ds, axis=0, keepdims=True)
            part = ds[:, :LANES]
            for b in range(1, tk // LANES):
                part = part + ds[:, b * LANES:(b + 1) * LANES]
            dc_s[...] += part

        sc0, dp0 = scores(j)
        s_buf[...] = _causal(sc0, True)
        dp_buf[...] = dp0

        def loop(i, carry):
            nxt = scores(i + 1)
            accumulate(i, s_buf[...], dp_buf[...])
            s_buf[...], dp_buf[...] = nxt
            return carry

        lax.fori_loop(j, nk - 1, loop, 0)
        accumulate(nk - 1, s_buf[...], dp_buf[...])
        dk_ref[...] = (dk_s[...] * (1.0 / LOG2E)).astype(BF16)
        dv_ref[...] = dv_s[...].astype(BF16)
        dck_ref[...] = jnp.sum(jnp.transpose(dc_s[...]), axis=0, keepdims=True)

        @pl.when(j == nk - 1)
        def _():
            dq_ref[...] = (dq_s[...] * scale).astype(BF16)
            dcq_ref[...] = dcq_s[...]

    col = pl.BlockSpec((s, HEAD_DIM), lambda h, j: (0, h))
    row = pl.BlockSpec((None, 1, s), lambda h, j: (h, 0, 0))
    kspec = pl.BlockSpec((tk, HEAD_DIM), lambda h, j: (j, h))
    return pl.pallas_call(
        body, name=name, grid=(nh, nk),
        in_specs=[col, kspec, pl.BlockSpec((tk, HEAD_DIM), lambda h, j: (j, nh + h)), col, col, row,
                  pl.BlockSpec((tk, LANES), lambda h, j: (j, 0)), pl.BlockSpec(memory_space=pl.ANY)],
        out_specs=[kspec, kspec, col, row, pl.BlockSpec((None, 1, tk), lambda h, j: (h, 0, j))],
        out_shape=[jax.ShapeDtypeStruct((s, w), BF16), jax.ShapeDtypeStruct((s, w), BF16),
                   jax.ShapeDtypeStruct(dqz.shape, BF16), jax.ShapeDtypeStruct((nh, 1, s), F32),
                   jax.ShapeDtypeStruct((nh, 1, s), F32)],
        input_output_aliases={7: 2},
        scratch_shapes=[pltpu.VMEM((tk, HEAD_DIM), F32), pltpu.VMEM((tk, HEAD_DIM), F32), pltpu.VMEM((tk, LANES), F32),
                        pltpu.VMEM((s, HEAD_DIM), F32), pltpu.VMEM((1, s), F32), pltpu.VMEM((1, s), F32),
                        pltpu.VMEM((tk, tk), F32), pltpu.VMEM((tk, tk), F32)],
        compiler_params=_cparams("arbitrary", "arbitrary"),
    )(q2, kv, kv, do, o, lse2_t, cum2, dqz)


def _fox_bwd_dq(q2, kv, do, o, lse2, cum2_t, dqz, name):
    s, w = q2.shape
    nh = w // HEAD_DIM
    tq = _attn_tiles(s)
    nq = s // tq
    scale = HEAD_DIM ** -0.5
    nt = (_DOT_DIMS["nt"], ((), ()))

    def body(q_ref, k_ref, v_ref, do_ref, o_ref, lse_ref, ct_ref, _, dq_ref, dl_ref, dcq_ref, acc_s, dc_s):
        i = pl.program_id(1)
        qb = q_ref[...]
        dob = do_ref[...]
        lse = lse_ref[...]
        delta = jnp.broadcast_to(jnp.sum(dob.astype(F32) * o_ref[...], axis=1, keepdims=True), (tq, LANES))
        acc_s[...] = jnp.zeros_like(acc_s)
        dc_s[...] = jnp.zeros_like(dc_s)

        def tile(j, masked):
            off = pl.multiple_of(j * tq, tq)
            kb = k_ref[pl.ds(off, tq), :]
            sc = lax.dot_general(qb, kb, nt, preferred_element_type=F32) - ct_ref[:, pl.ds(off, tq)]
            if masked:
                sc = _causal(sc, False)
            p = _exp2_rows(sc, lse)
            dp = lax.dot_general(dob, v_ref[pl.ds(off, tq), :], nt, preferred_element_type=F32)
            ds = p * (dp - jnp.concatenate([delta] * (tq // LANES), axis=1))
            acc_s[...] += jnp.dot(ds.astype(BF16), kb, preferred_element_type=F32)
            part = ds[:, :LANES]
            for b in range(1, tq // LANES):
                part = part + ds[:, b * LANES:(b + 1) * LANES]
            dc_s[...] += part

        def loop(j, carry):
            tile(j, False)
            return carry

        lax.fori_loop(0, i, loop, 0)
        tile(i, True)
        dq_ref[...] = (acc_s[...] * scale).astype(BF16)
        dl_ref[...] = _row_of(delta)
        dcq_ref[...] = jnp.sum(jnp.transpose(dc_s[...]), axis=0, keepdims=True)

    qspec = pl.BlockSpec((tq, HEAD_DIM), lambda h, i: (i, h))
    rep = pl.BlockSpec((None, tq, LANES), lambda h, i: (h, i, 0))
    rowspec = pl.BlockSpec((None, 1, tq), lambda h, i: (h, 0, i))
    return pl.pallas_call(
        body, name=name, grid=(nh, nq),
        in_specs=[qspec,
                  pl.BlockSpec((s, HEAD_DIM), lambda h, i: (0, h)),
                  pl.BlockSpec((s, HEAD_DIM), lambda h, i: (0, nh + h)),
                  qspec, qspec, rep,
                  pl.BlockSpec((None, 1, s), lambda h, i: (h, 0, 0)),
                  pl.BlockSpec(memory_space=pl.ANY)],
        out_specs=[qspec, rowspec, rowspec],
        out_shape=[jax.ShapeDtypeStruct(dqz.shape, BF16), jax.ShapeDtypeStruct((nh, 1, s), F32),
                   jax.ShapeDtypeStruct((nh, 1, s), F32)],
        input_output_aliases={7: 0},
        scratch_shapes=[pltpu.VMEM((tq, HEAD_DIM), F32), pltpu.VMEM((tq, LANES), F32)],
        compiler_params=_cparams("parallel", "arbitrary"),
    )(q2, kv, kv, do, o, lse2, cum2_t, dqz)


def _fox_bwd_dkv(q2, kv, do, lse2_t, delta_t, cum2, name):
    s, w = q2.shape
    nh = w // HEAD_DIM
    tk = _attn_tiles(s)
    nk = s // tk
    nt = (_DOT_DIMS["nt"], ((), ()))

    def body(q_ref, k_ref, v_ref, do_ref, lse_ref, dl_ref, c_ref, dk_ref, dv_ref, dck_ref, dk_s, dv_s, dc_s, s_buf, dp_buf):
        h, j = pl.program_id(0), pl.program_id(1)
        kb = k_ref[...]
        vb = v_ref[...]
        ck = jnp.broadcast_to(_head_col(c_ref[...], h), (tk, LANES))
        dk_s[...] = jnp.zeros_like(dk_s)
        dv_s[...] = jnp.zeros_like(dv_s)
        dc_s[...] = jnp.zeros_like(dc_s)

        def scores(i):
            off = pl.multiple_of(i * tk, tk)
            sc = lax.dot_general(kb, q_ref[pl.ds(off, tk), :], nt, preferred_element_type=F32) - lse_ref[:, pl.ds(off, tk)]
            dp = lax.dot_general(vb, do_ref[pl.ds(off, tk), :], nt, preferred_element_type=F32) - dl_ref[:, pl.ds(off, tk)]
            return sc, dp

        def accumulate(i, sc, dp):
            off = pl.multiple_of(i * tk, tk)
            p = _exp2_rows(sc, ck)
            dv_s[...] += jnp.dot(p.astype(BF16), do_ref[pl.ds(off, tk), :], preferred_element_type=F32)
            ds = p * dp
            dk_s[...] += jnp.dot(ds.astype(BF16), q_ref[pl.ds(off, tk), :], preferred_element_type=F32)
            part = ds[:, :LANES]
            for b in range(1, tk // LANES):
                part = part + ds[:, b * LANES:(b + 1) * LANES]
            dc_s[...] += part

        sc0, dp0 = scores(j)
        s_buf[...] = _causal(sc0, True)
        dp_buf[...] = dp0

        def loop(i, carry):
            nxt = scores(i + 1)
            accumulate(i, s_buf[...], dp_buf[...])
            s_buf[...], dp_buf[...] = nxt
            return carry

        lax.fori_loop(j, nk - 1, loop, 0)
        accumulate(nk - 1, s_buf[...], dp_buf[...])
        dk_ref[...] = (dk_s[...] * (1.0 / LOG2E)).astype(BF16)
        dv_ref[...] = dv_s[...].astype(BF16)
        dck_ref[...] = jnp.sum(jnp.transpose(dc_s[...]), axis=0, keepdims=True)

    col = pl.BlockSpec((s, HEAD_DIM), lambda h, j: (0, h))
    row = pl.BlockSpec((None, 1, s), lambda h, j: (h, 0, 0))
    kspec = pl.BlockSpec((tk, HEAD_DIM), lambda h, j: (j, h))
    return pl.pallas_call(
        body, name=name, grid=(nh, nk),
        in_specs=[col, kspec, pl.BlockSpec((tk, HEAD_DIM), lambda h, j: (j, nh + h)), col, row, row,
                  pl.BlockSpec((tk, LANES), lambda h, j: (j, 0))],
        out_specs=[kspec, kspec, pl.BlockSpec((None, 1, tk), lambda h, j: (h, 0, j))],
        out_shape=[jax.ShapeDtypeStruct((s, w), BF16), jax.ShapeDtypeStruct((s, w), BF16),
                   jax.ShapeDtypeStruct((nh, 1, s), F32)],
        scratch_shapes=[pltpu.VMEM((tk, HEAD_DIM), F32), pltpu.VMEM((tk, HEAD_DIM), F32),
                        pltpu.VMEM((tk, LANES), F32), pltpu.VMEM((tk, tk), F32), pltpu.VMEM((tk, tk), F32)],
        compiler_params=_cparams("parallel", "arbitrary"),
    )(q2, kv, kv, do, lse2_t, delta_t, cum2)


def _exchange_copies(ins, outs, send_sems, recv_sems, local_sems, scatter):
    x, y, c = (lax.axis_index(a) for a in MESH_AXES)
    me = 4 * x + 2 * y + c
    local, remote = [], []
    for a in range(len(ins)):
        local.append(pltpu.make_async_copy(ins[a].at[me] if scatter else ins[a], outs[a].at[me], local_sems.at[a]))
        for k in range(1, N_DEV):
            px, py, pc = (1 - x if k & 4 else x), (1 - y if k & 2 else y), (1 - c if k & 1 else c)
            remote.append(pltpu.make_async_remote_copy(
                src_ref=ins[a].at[4 * px + 2 * py + pc] if scatter else ins[a], dst_ref=outs[a].at[me],
                send_sem=send_sems.at[a * (N_DEV - 1) + k - 1], recv_sem=recv_sems.at[a * (N_DEV - 1) + k - 1],
                device_id=(px, py, pc), device_id_type=pl.DeviceIdType.MESH))
    return local, remote


def _exchange_out_shapes(arrs, scatter):
    return [((N_DEV,) + a.shape[1:]) if scatter else ((N_DEV,) + a.shape) for a in arrs]


def _exchange(arrs, scatter, name):
    n = len(arrs)

    def body(*refs):
        local, remote = _exchange_copies(refs[:n], refs[n:2 * n], *refs[2 * n:], scatter)
        for cp in local + remote:
            cp.start()
        for cp in remote:
            cp.wait_send()
            cp.wait_recv()
        for cp in local:
            cp.wait()

    out_shape = [jax.ShapeDtypeStruct(s, a.dtype) for s, a in zip(_exchange_out_shapes(arrs, scatter), arrs)]
    return pl.pallas_call(
        body, name=name, out_shape=out_shape,
        in_specs=[pl.BlockSpec(memory_space=pl.ANY)] * n, out_specs=[pl.BlockSpec(memory_space=pl.ANY)] * n,
        scratch_shapes=[pltpu.SemaphoreType.DMA((n * (N_DEV - 1),)), pltpu.SemaphoreType.DMA((n * (N_DEV - 1),)),
                        pltpu.SemaphoreType.DMA((n,))],
    )(*arrs)


_HBM = pl.BlockSpec(memory_space=pltpu.HBM)
_SEM = pl.BlockSpec(memory_space=pltpu.SEMAPHORE)


def _exchange_start(arrs, scatter, name, after=()):
    n = len(arrs)
    after = list(after)
    lands = [lax.empty(s, a.dtype) for s, a in zip(_exchange_out_shapes(arrs, scatter), arrs)]

    def body(*refs):
        ins, outs = refs[:n], refs[n:2 * n]
        send_sems, recv_sems, local_sems = refs[2 * n + len(after):2 * n + len(after) + 3]
        token = refs[-1]
        local, remote = _exchange_copies(ins, outs, send_sems, recv_sems, local_sems, scatter)
        for cp in local + remote:
            cp.start()
        token[...] = jnp.zeros_like(token)

    hbm = lambda a: pltpu.HBM(a.shape, a.dtype)
    res = pl.pallas_call(
        body, name=name,
        out_shape=(pltpu.SemaphoreType.DMA((n * (N_DEV - 1),)), pltpu.SemaphoreType.DMA((n * (N_DEV - 1),)),
                   pltpu.SemaphoreType.DMA((n,)), *[hbm(a) for a in arrs], *[hbm(a) for a in lands],
                   jax.ShapeDtypeStruct((SUBLANES, LANES), F32)),
        in_specs=[_HBM] * (2 * n) + [pl.BlockSpec(memory_space=pl.ANY)] * len(after),
        out_specs=(_SEM, _SEM, _SEM, *[_HBM] * (2 * n), pl.BlockSpec(memory_space=pltpu.VMEM)),
        input_output_aliases={i: 3 + i for i in range(2 * n)},
        compiler_params=pltpu.CompilerParams(has_side_effects=pltpu.SideEffectType.DATAFLOW_SIDE_EFFECTING),
    )(*[pltpu.with_memory_space_constraint(a, pltpu.HBM) for a in list(arrs) + lands], *after)
    return (n, scatter, res[:3], res[3:3 + n], res[3 + n:3 + 2 * n]), res[-1]


def _exchange_wait(state, after, name):
    n, scatter, sems, srcs, lands = state
    after = list(after) if isinstance(after, (list, tuple)) else [after]

    def body(*refs):
        ins, outs = refs[:n], refs[n:2 * n]
        send_sems, recv_sems, local_sems = refs[2 * n:2 * n + 3]
        local, remote = _exchange_copies(ins, outs, send_sems, recv_sems, local_sems, scatter)
        for cp in remote:
            cp.wait_send()
            cp.wait_recv()
        for cp in local:
            cp.wait()

    hbm = lambda a: pltpu.HBM(a.shape, a.dtype)
    res = pl.pallas_call(
        body, name=name,
        out_shape=(*[hbm(a) for a in srcs], *[hbm(a) for a in lands]),
        in_specs=[_HBM] * (2 * n) + [_SEM] * 3 + [pl.BlockSpec(memory_space=pl.ANY)] * len(after),
        out_specs=tuple([_HBM] * (2 * n)),
        input_output_aliases={i: i for i in range(2 * n)},
        compiler_params=pltpu.CompilerParams(has_side_effects=pltpu.SideEffectType.DATAFLOW_SIDE_EFFECTING),
    )(*srcs, *lands, *sems, *after)
    return list(res[n:])


def _adamw_math(w, g, m, v):
    m = ADAM_B1 * m + (1.0 - ADAM_B1) * g
    v = ADAM_B2 * v + (1.0 - ADAM_B2) * (g * g)
    m_hat = m / (1.0 - ADAM_B1 ** ADAM_STEP)
    v_hat = v / (1.0 - ADAM_B2 ** ADAM_STEP)
    return -ADAM_LR * (m_hat / (jnp.sqrt(v_hat) + ADAM_EPS) + ADAM_WD * w), m, v


def _slot_sum(p_ref):
    g = p_ref[0].astype(F32)
    for d in range(1, p_ref.shape[0]):
        g = g + p_ref[d].astype(F32)
    return g


def _adamw_tile(r, c):
    return _tile(r, max(SUBLANES, (256 * 1024) // c // SUBLANES * SUBLANES), SUBLANES)


def _adamw(parts, w, m, v, name):
    r, c = w.shape[-2:]
    tr = _adamw_tile(r, c)

    def body(p_ref, w_ref, m_ref, v_ref, g_ref, d_ref, nm_ref, nv_ref):
        g = _slot_sum(p_ref)
        g_ref[...] = g
        d_ref[...], nm_ref[...], nv_ref[...] = _adamw_math(w_ref[...], g, m_ref[...], v_ref[...])

    if w.ndim == 3:
        blk = pl.BlockSpec((None, tr, c), lambda i: (0, i, 0))
    else:
        blk = pl.BlockSpec((tr, c), lambda i: (i, 0))
    sh = jax.ShapeDtypeStruct(w.shape, F32)
    return pl.pallas_call(
        body, name=name, grid=(r // tr,),
        in_specs=[pl.BlockSpec((parts.shape[0], tr, c), lambda i: (0, i, 0)), blk, blk, blk],
        out_specs=[blk] * 4, out_shape=[sh] * 4, compiler_params=_cparams("parallel"),
    )(parts, w, m, v)


def _sum_parts(parts, name):
    _, r, c = parts.shape
    tr = _adamw_tile(r, c)

    def body(p_ref, o_ref):
        o_ref[...] = _slot_sum(p_ref)

    return pl.pallas_call(
        body, name=name, grid=(r // tr,),
        in_specs=[pl.BlockSpec((parts.shape[0], tr, c), lambda i: (0, i, 0))],
        out_specs=pl.BlockSpec((tr, c), lambda i: (i, 0)), out_shape=jax.ShapeDtypeStruct((r, c), F32),
        compiler_params=_cparams("parallel"),
    )(parts)


def _perm(a):
    s, d = a.shape
    return a.reshape(N_SEG, s // N_SEG, d).transpose(1, 0, 2).reshape(s, d)


def _unperm(a):
    s, d = a.shape
    return a.reshape(s // N_SEG, N_SEG, d).transpose(1, 0, 2).reshape(s, d)


def _lane_pad(a, width=LANES):
    return jnp.pad(a, ((0, 0), (0, width - a.shape[1])))


def _local_step(x, target, norm_pre, norm_post, kv_norm, kv_b_f, a_re, a_im, log_dt, b_re, b_im, c_re, c_im, comm):
    s, d = x.shape
    g, p = a_re.shape
    w = g * S5_GROUP
    fw = d
    nh = fw // HEAD_DIM
    seg_len = s // N_SEG
    row = lambda v: v.reshape(1, -1)
    g_pre0, g_pre1, g_post0, g_post1, g_kv = row(norm_pre[0]), row(norm_pre[1]), row(norm_post[0]), row(norm_post[1]), row(kv_norm)

    ldt = log_dt.reshape(g, 1)
    abr, abi, cr, ci = _s5_disc_fwd(a_re, a_im, ldt)
    cr_col, ci_col = cr.reshape(g * p, 1), ci.reshape(g * p, 1)
    b_re2, b_im2 = b_re.reshape(g * p, S5_GROUP), b_im.reshape(g * p, S5_GROUP)
    bb_re, bb_im = _s5_bbar_fwd(cr_col, ci_col, b_re2, b_im2)
    bd_re = _block_diag_in(bb_re.reshape(g, p, S5_GROUP)).astype(BF16)
    bd_im = _block_diag_in(bb_im.reshape(g, p, S5_GROUP)).astype(BF16)
    cd_re = _block_diag_out(c_re).astype(BF16)
    cd_im = _block_diag_out(-c_im).astype(BF16)
    ab_re = jnp.broadcast_to(abr.reshape(1, g * p), (N_SEG, g * p))
    ab_im = jnp.broadcast_to(abi.reshape(1, g * p), (N_SEG, g * p))
    zero_seg = jnp.zeros((N_SEG, g * p), F32)

    xn0 = _norm_cast(x, g_pre0 + comm.token, "norm_pre0", x_kind="nat")
    w_in = comm.weight("s5_w_in", [xn0, bd_re, bd_im, cd_re, cd_im, ab_re, ab_im])
    d_row, bglu_row = row(comm.vector("s5_d")), row(comm.vector("s5_b_glu"))
    u = _mm(xn0, w_in, "nn", F32, "s5_in_u", b_cols=(0, w), after=comm.start_rest(w_in))
    z0 = _mm(xn0, w_in, "nn", F32, "s5_in_z", b_cols=(w, w))
    e_re, e_im = _s5_scan_fwd(u, bd_re, bd_im, cd_re, cd_im, ab_re, ab_im, zero_seg, zero_seg, d_row, False, "s5_scan_ends")
    i_re, i_im = _s5_seg_fix(e_re, e_im, ab_re, ab_im, seg_len, False, "s5_seg_fix")
    y_ssm, yg, h_re, h_im, _, _ = _s5_scan_fwd(u, bd_re, bd_im, cd_re, cd_im, ab_re, ab_im, i_re, i_im, d_row, True, "s5_scan")
    w_glu, w_out = comm.weight("s5_w_glu", yg), comm.weight("s5_w_out", yg)
    gp = _mm(yg, w_glu, "nn", F32, "s5_glu")
    y3 = _s5_gate(y_ssm, gp, bglu_row, z0, "s5_gate")
    w_kv, fw_in, fw_out = comm.weight("kv_w", y3), comm.weight("fox_w_in", y3), comm.weight("fox_w_out", y3)
    w_f = _lane_pad(w_kv[:, 2 * fw:])
    o0 = _mm(y3, w_out, "nn", F32, "s5_out")
    r0 = _post_norm(o0, g_post0, "norm_post0", out_kind="nat")

    h1, hn_kv, xn1 = _resid_norm2(x, r0, g_kv, g_pre1, "resid_norms")
    kv = _mm(hn_kv, w_kv, "nn", BF16, "kv_proj", b_cols=(0, 2 * fw))
    f_logit = _mm(hn_kv, w_f, "nn", F32, "f_proj")
    bf_row = _lane_pad(row(kv_b_f))
    cum2 = _cum_fwd(f_logit, bf_row, "cum_fwd")
    cum2_t = cum2[:, :nh].T.reshape(nh, 1, s)
    q2 = _mm(xn1, fw_in, "nn", BF16, "fox_q", scale=HEAD_DIM ** -0.5 * LOG2E, b_cols=(0, fw))
    z1 = _mm(xn1, fw_in, "nn", F32, "fox_z", b_cols=(fw, fw))
    o, oz, lse2_t = _fox_fwd(q2, kv, cum2_t, z1, "fox_fwd")
    o1 = _mm(oz, fw_out, "nn", F32, "fox_out")
    dh2, sq = _post_norm_loss(o1, g_post1, h1, target, "norm_post1_loss")
    loss = 0.5 * jnp.sum(sq) / d

    do1, dg_post1 = _post_norm_bwd(dh2, o1, g_post1, "norm_post1_bwd")
    d_fw_out = _mm(oz, do1, "tn", BF16, "fox_out_dw")
    d_oz = _mm(do1, fw_out, "nt", F32, "fox_out_dx")
    do, dqz = _gate_bwd(d_oz, o, z1, "fox_gate_bwd")
    dk, dv, dqz, dcq, dck = _fox_bwd(q2, kv, do, o, lse2_t, cum2, dqz, "fox_bwd")
    d_fw_in = _mm(xn1, dqz, "tn", BF16, "fox_in_dw", col_slots=True)
    dxn1 = _mm(dqz, fw_in, "nt", F32, "fox_in_dx")
    dcq_sl = _lane_pad(dcq.reshape(nh, s).T)
    dck_sl = _lane_pad(dck.reshape(nh, s).T)
    df, db_f = _cum_bwd(dcq_sl, dck_sl, f_logit, bf_row, "cum_bwd")
    dkv = _concat_cast(dk, dv, "fox_dkv")
    d_w_kvm = _mm(hn_kv, dkv, "tn", F32, "kv_dw")
    d_w_f = _mm(hn_kv, df, "tn", F32, "f_dw")
    dhn_f = _mm(df, w_f, "nt", F32, "f_dx")
    dhn_kv = _mm(dkv, w_kv, "nt", F32, "kv_dx", add=dhn_f, b_cols=(0, 2 * fw))
    d_w_kv = jnp.concatenate([d_w_kvm, d_w_f[:, :nh]], axis=1)
    tok = comm.send_grads(dict(fox_w_out=d_fw_out, fox_w_in=d_fw_in, kv_w=d_w_kv), "exchange_fox")
    dh1, dg_pre1, dg_kv = _norm_bwd2(dh2, h1, dxn1, dhn_kv, g_pre1, g_kv, "resid_norms_bwd")

    do0, dg_post0 = _post_norm_bwd(dh1, o0, g_post0 + tok[0, 0], "norm_post0_bwd", dy_kind="nat")
    d_w_out = _mm(y3, do0, "tn", BF16, "s5_out_dw")
    dy3 = _mm(do0, w_out, "nt", F32, "s5_out_dx")
    duz, dgp, dyg_direct, db_glu = _s5_gate_bwd(dy3, y_ssm, gp, bglu_row, z0, "s5_gate_bwd")
    d_w_glu = _mm(yg, dgp, "tn", BF16, "s5_glu_dw")
    dyg = _mm(dgp, w_glu, "nt", F32, "s5_glu_dx", add=dyg_direct)
    dy_ssm = _gelu_bwd(dyg, y_ssm, "s5_gelu_bwd")
    d_row = d_row + comm.send_grads(dict(s5_w_out=d_w_out, s5_w_glu=d_w_glu), "exchange_s5")[0, 0]
    ab_imn = -ab_im
    ge_re, ge_im = _s5_scan_bwd(dy_ssm, u, h_re, h_im, bd_re, bd_im, cd_re, cd_im, ab_re, ab_imn, zero_seg, zero_seg,
                                d_row, False, "s5_adj_ends")
    gi_re, gi_im = _s5_seg_fix(ge_re, ge_im, ab_re, ab_imn, seg_len, True, "s5_adj_fix")
    duz, dbd_re, dbd_im, dcd_re, dcd_im, dab_re, dab_im, dd = _s5_scan_bwd(
        dy_ssm, u, h_re, h_im, bd_re, bd_im, cd_re, cd_im, ab_re, ab_imn, gi_re, gi_im, d_row, True, "s5_adj", duz=duz)
    d_w_in = _mm(xn0, duz, "tn", BF16, "s5_in_dw", col_slots=True)
    tok = comm.send_grads(dict(s5_w_in=d_w_in), "exchange_s5_in")
    dxn0 = _mm(duz, w_in, "nt", F32, "s5_in_dx", after=tok)
    grad_x, dg_pre0 = _norm_bwd1(dh1, x, dxn0, g_pre0, "norm_pre0_bwd")

    dbb_re = _block_diag_in_extract(dbd_re, p, S5_GROUP).reshape(g * p, S5_GROUP)
    dbb_im = _block_diag_in_extract(dbd_im, p, S5_GROUP).reshape(g * p, S5_GROUP)
    dcr_col, dci_col, db_re, db_im = _s5_bbar_bwd(cr_col, ci_col, b_re2, b_im2, dbb_re, dbb_im)
    da_re, da_im, dldt = _s5_disc_bwd(a_re, a_im, ldt, dab_re.reshape(g, p), dab_im.reshape(g, p),
                                      dcr_col.reshape(g, p), dci_col.reshape(g, p))
    dc_re = _block_diag_out_extract(dcd_re, S5_GROUP, p)
    dc_im = -_block_diag_out_extract(dcd_im, S5_GROUP, p)

    small = dict(
        norm_pre=jnp.concatenate([dg_pre0, dg_pre1], axis=0), norm_post=jnp.concatenate([dg_post0, dg_post1], axis=0),
        s5_a_re=da_re, s5_a_im=da_im, s5_log_dt=dldt.reshape(g), s5_b_re=db_re.reshape(g, p, S5_GROUP),
        s5_b_im=db_im.reshape(g, p, S5_GROUP), s5_c_re=dc_re, s5_c_im=dc_im, s5_d=dd.reshape(-1),
        s5_b_glu=db_glu.reshape(-1), kv_norm=dg_kv.reshape(-1), kv_b_f=db_f[0, :nh])
    return loss, grad_x, small


_BIG = ("s5_w_in", "s5_w_glu", "s5_w_out", "kv_w", "fox_w_in", "fox_w_out")
_COL_SHARDED = ("s5_w_in", "kv_w", "fox_w_in")
_SMALL = ("norm_pre", "norm_post", "s5_a_re", "s5_a_im", "s5_log_dt", "s5_b_re", "s5_b_im", "s5_c_re", "s5_c_im",
          "s5_d", "s5_b_glu", "kv_norm", "kv_b_f")
_SMALL_SHARDED = ("s5_d", "s5_b_glu")
_PACK_QUANTUM = SUBLANES * LANES
_WEIGHTS = ('norm_pre', 'norm_post', 's5_w_in', 's5_a_re', 's5_a_im', 's5_log_dt', 's5_b_re', 's5_b_im', 's5_c_re', 's5_c_im',
            's5_d', 's5_w_glu', 's5_b_glu', 's5_w_out', 'kv_norm', 'kv_w', 'kv_b_f', 'fox_w_in', 'fox_w_out')


def _full_from_slots(name, slots):
    n, r, c = slots.shape
    if name in _COL_SHARDED:
        return slots.transpose(1, 0, 2).reshape(r, n * c)
    return slots.reshape(n * r, c)


def _slots_from_full(name, full):
    if name in _COL_SHARDED:
        r, nc = full.shape
        return full.reshape(r, N_DEV, nc // N_DEV).transpose(1, 0, 2)
    nr, c = full.shape
    return full.reshape(N_DEV, nr // N_DEV, c)


def _pack(vals):
    parts = []
    for v in vals:
        flat = v.reshape(-1)
        parts.append(jnp.pad(flat, (0, (-flat.shape[0]) % _PACK_QUANTUM)))
    total = sum(p.shape[0] for p in parts)
    parts.append(jnp.zeros(((-total) % (N_DEV * _PACK_QUANTUM),), F32))
    return jnp.concatenate(parts).reshape(-1, LANES)


def _unpack(packed, shapes):
    flat = packed.reshape(-1)
    out, off = [], 0
    for sh in shapes:
        n = math.prod(sh)
        out.append(flat[off:off + n].reshape(sh))
        off += n + (-n) % _PACK_QUANTUM
    return out


class _Comm:
    _GROUPS = (("s5_w_in",) + _SMALL_SHARDED, ("s5_w_glu", "s5_w_out"), ("kv_w", "fox_w_in", "fox_w_out"))

    def __init__(self, shards, vectors):
        self._shards = {**shards, **vectors}
        self._full, self._gathers = {}, {}
        self.token = self._start(self._GROUPS[0], ())[0, 0]
        self._sent = []

    def _start(self, group, after):
        state, tok = _exchange_start([self._shards[n] for n in group], False, "gather_start_" + group[0], after)
        self._gathers[group] = state
        return tok

    def start_rest(self, after):
        toks = [self._start(group, [after]) for group in self._GROUPS[1:]]
        return toks[0] + toks[1]

    def vector(self, name):
        return self._full[name]

    def weight(self, name, after):
        if name not in self._full:
            group = next(g for g in self._GROUPS if name in g)
            slots = _exchange_wait(self._gathers.pop(group), after, "gather_wait_" + group[0])
            for n, sl in zip(group, slots):
                self._full[n] = sl.reshape(-1) if n in _SMALL_SHARDED else _full_from_slots(n, sl)
        return self._full[name]

    def send_grads(self, grads, name):
        names = list(grads)
        slots = [grads[n] if grads[n].ndim == 3 else _slots_from_full(n, grads[n]).astype(BF16) for n in names]
        state, tok = _exchange_start(slots, True, name + "_start")
        self._sent.append((names, state, name + "_wait"))
        return tok

    def received_grads(self, after):
        for names, state, name in self._sent:
            for n, recv in zip(names, _exchange_wait(state, after, name)):
                yield n, recv


def kernel(x, norm_pre, norm_post, s5_w_in, s5_a_re, s5_a_im, s5_log_dt, s5_b_re, s5_b_im, s5_c_re, s5_c_im, s5_d, s5_w_glu, s5_b_glu, s5_w_out, kv_norm, kv_w, kv_b_f, fox_w_in, fox_w_out, loss_target, m_norm_pre, m_norm_post, m_s5_w_in, m_s5_a_re, m_s5_a_im, m_s5_log_dt, m_s5_b_re, m_s5_b_im, m_s5_c_re, m_s5_c_im, m_s5_d, m_s5_w_glu, m_s5_b_glu, m_s5_w_out, m_kv_norm, m_kv_w, m_kv_b_f, m_fox_w_in, m_fox_w_out, v_norm_pre, v_norm_post, v_s5_w_in, v_s5_a_re, v_s5_a_im, v_s5_log_dt, v_s5_b_re, v_s5_b_im, v_s5_c_re, v_s5_c_im, v_s5_d, v_s5_w_glu, v_s5_b_glu, v_s5_w_out, v_kv_norm, v_kv_w, v_kv_b_f, v_fox_w_in, v_fox_w_out):
    env = dict(locals())
    wts = {n: env[n] for n in _WEIGHTS}
    mom = {n: env["m_" + n] for n in _WEIGHTS}
    var = {n: env["v_" + n] for n in _WEIGHTS}
    me = 4 * lax.axis_index("x") + 2 * lax.axis_index("y") + lax.axis_index("c")
    shard2d = {n: wts[n].reshape(wts[n].shape[-2:]) for n in _BIG}
    comm = _Comm({n: shard2d[n].astype(BF16) for n in _BIG}, {n: wts[n].reshape(1, -1) for n in _SMALL_SHARDED})

    loss_local, grad_x, small = _local_step(
        x[0], loss_target[0], norm_pre, norm_post, kv_norm, kv_b_f, s5_a_re[0], s5_a_im[0], s5_log_dt[0],
        s5_b_re[0], s5_b_im[0], s5_c_re[0], s5_c_im[0], comm)

    small_pack = _pack([small[n] for n in _SMALL] + [loss_local.reshape(1)])
    slice_rows = small_pack.shape[0] // N_DEV
    small_state, small_tok = _exchange_start([small_pack.reshape(N_DEV, slice_rows, LANES)], True, "reduce_small_start")

    res = {}
    for n, recv in comm.received_grads([small_tok, grad_x]):
        res[n] = _adamw(recv, wts[n], mom[n], var[n], "adamw_" + n)

    full_shape = {n: (small[n].shape if n in _SMALL_SHARDED else wts[n].shape) for n in _SMALL}

    def spread(n, v):
        if n not in _SMALL_SHARDED:
            return v
        flat = v.reshape(-1)
        return lax.dynamic_update_slice(jnp.zeros(full_shape[n], F32), flat, (me * flat.shape[0],))

    my_sum = _sum_parts(_exchange_wait(small_state, res[_BIG[0]][0], "reduce_small_wait")[0], "sum_small")
    g_all = _exchange([my_sum], False, "gather_small")[0].reshape(1, small_pack.shape[0], LANES)
    packed = [_pack([spread(n, src[n]) for n in _SMALL] + [jnp.zeros((1,), F32)]) for src in (wts, mom, var)]
    outs = _adamw(g_all, *packed, "adamw_small")
    unpacked = [_unpack(o, [full_shape[n] for n in _SMALL] + [(1,)]) for o in outs]
    loss = unpacked[0][-1][0]
    for i, n in enumerate(_SMALL):
        vals = [u[i] for u in unpacked]
        if n in _SMALL_SHARDED:
            k = wts[n].size
            vals = [lax.dynamic_slice(v, (me * k,), (k,)) for v in vals]
        res[n] = [v.reshape(wts[n].shape) for v in vals]

    return (loss, grad_x[None], *[res[n][0] for n in _WEIGHTS], *[res[n][1] for n in _WEIGHTS],
            *[res[n][2] for n in _WEIGHTS], *[res[n][3] for n in _WEIGHTS])
```

```python
import functools
import math

import jax
import jax.numpy as jnp
from jax import lax
from jax.experimental import pallas as pl
from jax.experimental.pallas import tpu as pltpu

F32 = jnp.float32
BF16 = jnp.bfloat16

N_DEV = 8
MESH_AXES = ("x", "y", "c")
S5_GROUP = 16
S5_STATE = 64
LANES = 128
SUBLANES = 8
GROUPS_PER_BLOCK = LANES // S5_GROUP
BLOCK_STATE = GROUPS_PER_BLOCK * S5_STATE
N_SEG = SUBLANES
HEAD_DIM = 128
RMS_EPS = 1e-6
NEG_INF = -1e30
LOG2E = math.log2(math.e)
ADAM_LR = 0.001
ADAM_B1 = 0.9
ADAM_B2 = 0.999
ADAM_EPS = 1e-08
ADAM_WD = 0.01
ADAM_STEP = 10
VMEM_LIMIT = 56 * 1024 * 1024


def _tile(n, pref, quantum=LANES):
    if n <= pref:
        return n
    t = (pref // quantum) * quantum
    while t >= quantum:
        if n % t == 0:
            return t
        t -= quantum
    return n


def _cparams(*sem):
    return pltpu.CompilerParams(dimension_semantics=sem if sem else None, vmem_limit_bytes=VMEM_LIMIT)


_DOT_DIMS = {"nn": ((1,), (0,)), "nt": ((1,), (1,)), "tn": ((0,), (0,))}


def _mm(a, b, mode, out_dtype, name, add=None, scale=None, b_cols=None, after=None, col_slots=False):
    b_shape = b.shape if b_cols is None else (b.shape[0], b_cols[1])
    if mode == "nn":
        (M, K), (K2, N) = a.shape, b_shape
    elif mode == "nt":
        (M, K), (N, K2) = a.shape, b_shape
    else:
        (K, M), (K2, N) = a.shape, b_shape
    assert K == K2, (name, a.shape, b_shape)
    tm, tn, tk = _tile(M, 1024 if K <= 2048 else 512), (N // N_DEV if col_slots else _tile(N, 1024)), _tile(K, 4096)
    nk = K // tk
    dims = (_DOT_DIMS[mode], ((), ()))
    col0 = 0
    if b_cols is not None:
        assert mode != "tn" and b_cols[0] % (tn if mode == "nn" else tk) == 0
        col0 = b_cols[0] // (tn if mode == "nn" else tk)

    def body(*refs):
        a_ref, b_ref = refs[:2]
        c_ref = refs[2] if add is not None else None
        o_ref = refs[2 + (add is not None) + (after is not None)]
        part = lax.dot_general(a_ref[...], b_ref[...], dims, preferred_element_type=F32)

        def finish(r):
            if scale is not None:
                r = r * scale
            if add is not None:
                r = r + c_ref[...]
            o_ref[...] = r.astype(out_dtype)

        if nk == 1:
            finish(part)
            return
        acc = refs[-1]
        k = pl.program_id(2)

        @pl.when(k == 0)
        def _():
            acc[...] = part

        @pl.when(jnp.logical_and(k > 0, k < nk - 1))
        def _():
            acc[...] += part

        @pl.when(k == nk - 1)
        def _():
            finish(acc[...] + part)

    if mode == "tn":
        a_spec = pl.BlockSpec((tk, tm), lambda i, j, k: (k, i))
    else:
        a_spec = pl.BlockSpec((tm, tk), lambda i, j, k: (i, k))
    if mode == "nt":
        b_spec = pl.BlockSpec((tn, tk), lambda i, j, k: (j, k + col0))
    else:
        b_spec = pl.BlockSpec((tk, tn), lambda i, j, k: (k, j + col0))
    o_spec = pl.BlockSpec((tm, tn), lambda i, j, k: (i, j))
    in_specs = [a_spec, b_spec] + ([o_spec] if add is not None else [])
    args = (a, b) + ((add,) if add is not None else ())
    if after is not None:
        in_specs.append(pl.BlockSpec(after.shape, lambda i, j, k: (0, 0)))
        args += (after,)
    out_shape = jax.ShapeDtypeStruct((M, N), out_dtype)
    if col_slots:
        assert add is None
        o_spec = pl.BlockSpec((None, tm, tn), lambda i, j, k: (j, i, 0))
        out_shape = jax.ShapeDtypeStruct((N_DEV, M, tn), out_dtype)
    return pl.pallas_call(
        body, name=name, grid=(M // tm, N // tn, nk),
        in_specs=in_specs, out_specs=o_spec,
        out_shape=out_shape,
        scratch_shapes=[pltpu.VMEM((tm, tn), F32)] if nk > 1 else [],
        compiler_params=_cparams("parallel", "parallel", "arbitrary"),
    )(*args)


class _NatIn:
    def __init__(self, ref):
        self.ref = ref

    def __getitem__(self, idx):
        v = jnp.swapaxes(self.ref[...], 0, 1)
        return v.reshape(v.shape[0] * N_SEG, v.shape[2])


class _NatOut:
    def __init__(self, ref):
        self.ref = ref

    def __setitem__(self, idx, val):
        self.ref[...] = jnp.swapaxes(val.reshape(val.shape[0] // N_SEG, N_SEG, val.shape[1]), 0, 1)


def _rowcall(body, name, n_rows, ins, outs, tile_rows=256):
    tr = _tile(n_rows, tile_rows, SUBLANES * 2)
    n_in = len(ins)
    in_kinds = [k for _, k in ins]
    kinds = [k for _, _, k in outs]

    def kern(*refs):
        @pl.when(pl.program_id(0) == 0)
        def _():
            for r, kind in zip(refs[n_in:], kinds):
                if kind == "acc":
                    r[...] = jnp.zeros_like(r)

        wrapped = [_NatIn(r) if k == "nat" else r for r, k in zip(refs[:n_in], in_kinds)]
        wrapped += [_NatOut(r) if k == "nat" else r for r, k in zip(refs[n_in:], kinds)]
        body(*wrapped)

    in_specs, args = [], []
    for arr, kind in ins:
        if kind == "row":
            in_specs.append(pl.BlockSpec((tr, arr.shape[1]), lambda i: (i, 0)))
        elif kind == "nat":
            in_specs.append(pl.BlockSpec((N_SEG, tr // N_SEG, arr.shape[1]), lambda i: (0, i, 0)))
            arr = arr.reshape(N_SEG, n_rows // N_SEG, arr.shape[1])
        else:
            in_specs.append(pl.BlockSpec(arr.shape, lambda i, nd=arr.ndim: (0,) * nd))
        args.append(arr)
    out_specs, out_shape = [], []
    for width, dtype, kind in outs:
        if kind == "row":
            out_specs.append(pl.BlockSpec((tr, width), lambda i: (i, 0)))
            out_shape.append(jax.ShapeDtypeStruct((n_rows, width), dtype))
        elif kind == "right":
            out_specs.append(pl.BlockSpec((tr, width), lambda i: (i, 1)))
            out_shape.append(jax.ShapeDtypeStruct((n_rows, 2 * width), dtype))
        elif kind == "nat":
            out_specs.append(pl.BlockSpec((N_SEG, tr // N_SEG, width), lambda i: (0, i, 0)))
            out_shape.append(jax.ShapeDtypeStruct((N_SEG, n_rows // N_SEG, width), dtype))
        else:
            out_specs.append(pl.BlockSpec((1, width), lambda i: (0, 0)))
            out_shape.append(jax.ShapeDtypeStruct((1, width), F32))
    res = pl.pallas_call(
        kern, name=name, grid=(n_rows // tr,), in_specs=in_specs, out_specs=out_specs, out_shape=out_shape,
        compiler_params=_cparams("arbitrary"),
    )(*args)
    return [r.reshape(n_rows, r.shape[2]) if k == "nat" else r for r, k in zip(res, kinds)]


def _rstd(x):
    return lax.rsqrt(jnp.mean(x * x, axis=-1, keepdims=True) + RMS_EPS)


def _rms_bwd(x, g, dy):
    xh = x * _rstd(x)
    dxh = dy * g
    dx = _rstd(x) * (dxh - xh * jnp.mean(dxh * xh, axis=-1, keepdims=True))
    return dx, jnp.sum(dy * xh, axis=0, keepdims=True)


def _silu(z):
    return z * jax.nn.sigmoid(z)


def _norm_cast(x, g, name, x_kind="row"):
    def body(x_ref, g_ref, o_ref):
        x = x_ref[...]
        o_ref[...] = (x * _rstd(x) * g_ref[...]).astype(BF16)

    return _rowcall(body, name, x.shape[0], [(x, x_kind), (g, "full")], [(x.shape[1], BF16, "row")])[0]


def _resid_norm2(x, r0, g_kv, g_pre, name):
    def body(x_ref, r_ref, gk_ref, gp_ref, h_ref, nk_ref, np_ref):
        h = x_ref[...] + r_ref[...]
        h_ref[...] = h
        hn = h * _rstd(h)
        nk_ref[...] = (hn * gk_ref[...]).astype(BF16)
        np_ref[...] = (hn * gp_ref[...]).astype(BF16)

    d = x.shape[1]
    return _rowcall(body, name, x.shape[0], [(x, "row"), (r0, "row"), (g_kv, "full"), (g_pre, "full")],
                    [(d, F32, "row"), (d, BF16, "row"), (d, BF16, "row")])


def _post_norm(o, g, name, out_kind="row"):
    def body(o_ref, g_ref, r_ref):
        o = o_ref[...]
        r_ref[...] = o * _rstd(o) * g_ref[...]

    return _rowcall(body, name, o.shape[0], [(o, "row"), (g, "full")], [(o.shape[1], F32, out_kind)])[0]


def _post_norm_loss(o, g, h1, target, name):
    d = o.shape[1]

    def body(o_ref, g_ref, h_ref, t_ref, dh_ref, do_ref, acc_ref, dg_ref):
        o = o_ref[...]
        e = h_ref[...] + o * _rstd(o) * g_ref[...] - t_ref[...]
        dh = e * (1.0 / d)
        dh_ref[...] = dh
        acc_ref[...] += jnp.sum(e * e, axis=0, keepdims=True)
        dx, dg = _rms_bwd(o, g_ref[...], dh)
        do_ref[...] = dx.astype(BF16)
        dg_ref[...] += dg

    return _rowcall(body, name, o.shape[0], [(o, "row"), (g, "full"), (h1, "row"), (target, "row")],
                    [(d, F32, "row"), (d, BF16, "row"), (d, F32, "acc"), (d, F32, "acc")])


def _post_norm_bwd(dy, o, g, name, dy_kind="row"):
    def body(dy_ref, o_ref, g_ref, do_ref, dg_ref):
        dx, dg = _rms_bwd(o_ref[...], g_ref[...], dy_ref[...])
        do_ref[...] = dx.astype(BF16)
        dg_ref[...] += dg

    d = o.shape[1]
    return _rowcall(body, name, o.shape[0], [(dy, dy_kind), (o, "row"), (g, "full")], [(d, BF16, "row"), (d, F32, "acc")])


def _gate_mul(o, z, name):
    def body(o_ref, z_ref, r_ref):
        r_ref[...] = (o_ref[...] * _silu(z_ref[...])).astype(BF16)

    return _rowcall(body, name, o.shape[0], [(o, "row"), (z, "row")], [(o.shape[1], BF16, "row")])[0]


def _gate_bwd(d_oz, o, z, name):
    def body(d_ref, o_ref, z_ref, do_ref, dz_ref):
        _, vjp = jax.vjp(lambda o, z: o * _silu(z), o_ref[...], z_ref[...].astype(F32))
        do, dz = vjp(d_ref[...])
        do_ref[...] = do.astype(BF16)
        dz_ref[...] = dz.astype(BF16)

    w = o.shape[1]
    return _rowcall(body, name, o.shape[0], [(d_oz, "row"), (o, "row"), (z, "row")], [(w, BF16, "row"), (w, BF16, "right")])


def _norm_bwd2(dh2, h1, dxn1, dhn_kv, g_pre, g_kv, name):
    def body(dh2_ref, h_ref, d1_ref, dk_ref, gp_ref, gk_ref, dh1_ref, dgp_ref, dgk_ref):
        h = h_ref[...]
        dx1, dg1 = _rms_bwd(h, gp_ref[...], d1_ref[...])
        dxk, dgk = _rms_bwd(h, gk_ref[...], dk_ref[...])
        dh1_ref[...] = dh2_ref[...] + dx1 + dxk
        dgp_ref[...] += dg1
        dgk_ref[...] += dgk

    d = h1.shape[1]
    return _rowcall(body, name, h1.shape[0],
                    [(dh2, "row"), (h1, "row"), (dxn1, "row"), (dhn_kv, "row"), (g_pre, "full"), (g_kv, "full")],
                    [(d, F32, "row"), (d, F32, "acc"), (d, F32, "acc")])


def _norm_bwd1(dres, x, dxn, g, name):
    def body(dr_ref, x_ref, dn_ref, g_ref, dx_ref, dg_ref):
        dx, dg = _rms_bwd(x_ref[...], g_ref[...], dn_ref[...])
        dx_ref[...] = dr_ref[...] + dx
        dg_ref[...] += dg

    d = x.shape[1]
    return _rowcall(body, name, x.shape[0], [(dres, "nat"), (x, "nat"), (dxn, "row"), (g, "full")],
                    [(d, F32, "nat"), (d, F32, "acc")])


def _gelu_cast(y, name):
    def body(y_ref, o_ref):
        o_ref[...] = jax.nn.gelu(y_ref[...]).astype(BF16)

    return _rowcall(body, name, y.shape[0], [(y, "row")], [(y.shape[1], BF16, "row")])[0]


def _s5_gate(y_ssm, gp, b_glu, z, name):
    def body(y_ref, gp_ref, b_ref, z_ref, o_ref):
        yg = jax.nn.gelu(y_ref[...])
        o_ref[...] = (yg * jax.nn.sigmoid(gp_ref[...] + b_ref[...]) * _silu(z_ref[...].astype(F32))).astype(BF16)

    return _rowcall(body, name, y_ssm.shape[0], [(y_ssm, "row"), (gp, "row"), (b_glu, "full"), (z, "row")],
                    [(y_ssm.shape[1], BF16, "row")])[0]


def _s5_gate_bwd(dy3, y_ssm, gp, b_glu, z, name):
    def body(d_ref, y_ref, gp_ref, b_ref, z_ref, dz_ref, dgp_ref, dyg_ref, db_ref):
        yg = jax.nn.gelu(y_ref[...])
        _, vjp = jax.vjp(lambda yg, gp, z: yg * jax.nn.sigmoid(gp) * _silu(z), yg, gp_ref[...] + b_ref[...],
                         z_ref[...].astype(F32))
        dyg, dgp, dz = vjp(d_ref[...])
        dz_ref[...] = dz.astype(BF16)
        dgp_ref[...] = dgp.astype(BF16)
        dyg_ref[...] = dyg
        db_ref[...] += jnp.sum(dgp, axis=0, keepdims=True)

    w = y_ssm.shape[1]
    return _rowcall(body, name, y_ssm.shape[0],
                    [(dy3, "row"), (y_ssm, "row"), (gp, "row"), (b_glu, "full"), (z, "row")],
                    [(w, BF16, "right"), (w, BF16, "row"), (w, F32, "row"), (w, F32, "acc")])


def _gelu_bwd(dyg, y_ssm, name):
    def body(d_ref, y_ref, o_ref):
        _, vjp = jax.vjp(jax.nn.gelu, y_ref[...])
        o_ref[...] = vjp(d_ref[...])[0]

    return _rowcall(body, name, y_ssm.shape[0], [(dyg, "row"), (y_ssm, "row")], [(y_ssm.shape[1], F32, "row")])[0]


def _concat_cast(a, b, name):
    def body(a_ref, b_ref, o_ref):
        w = a_ref.shape[1]
        o_ref[:, :w] = a_ref[...].astype(BF16)
        o_ref[:, w:] = b_ref[...].astype(BF16)

    return _rowcall(body, name, a.shape[0], [(a, "row"), (b, "row")], [(a.shape[1] + b.shape[1], BF16, "row")])[0]


def _disc(ar, ai, ldt):
    dt = jnp.exp(ldt)
    mag = jnp.exp(ar * dt)
    abr = mag * jnp.cos(ai * dt)
    abi = mag * jnp.sin(ai * dt)
    den = ar * ar + ai * ai
    nr = abr - 1.0
    return abr, abi, (nr * ar + abi * ai) / den, (abi * ar - nr * ai) / den


def _s5_disc_fwd(a_re, a_im, ldt):
    def body(ar, ai, ld, o1, o2, o3, o4):
        o1[...], o2[...], o3[...], o4[...] = _disc(ar[...], ai[...], ld[...])

    sh = jax.ShapeDtypeStruct(a_re.shape, F32)
    return pl.pallas_call(body, name="s5_disc_fwd", out_shape=(sh, sh, sh, sh))(a_re, a_im, ldt)


def _s5_disc_bwd(a_re, a_im, ldt, d_abr, d_abi, d_cr, d_ci):
    def body(ar, ai, ld, g1, g2, g3, g4, o1, o2, o3):
        _, vjp = jax.vjp(_disc, ar[...], ai[...], ld[...])
        o1[...], o2[...], o3[...] = vjp((g1[...], g2[...], g3[...], g4[...]))

    sh = jax.ShapeDtypeStruct(a_re.shape, F32)
    return pl.pallas_call(body, name="s5_disc_bwd", out_shape=(sh, sh, jax.ShapeDtypeStruct(ldt.shape, F32)))(
        a_re, a_im, ldt, d_abr, d_abi, d_cr, d_ci)


def _bbar(cr, ci, br, bi):
    return cr * br - ci * bi, cr * bi + ci * br


def _s5_bbar_fwd(cr_col, ci_col, b_re, b_im):
    def body(cr, ci, br, bi, o1, o2):
        o1[...], o2[...] = _bbar(cr[...], ci[...], br[...], bi[...])

    w = b_re.shape[1]
    return _rowcall(body, "s5_bbar_fwd", b_re.shape[0], [(cr_col, "row"), (ci_col, "row"), (b_re, "row"), (b_im, "row")],
                    [(w, F32, "row"), (w, F32, "row")], tile_rows=1024)


def _s5_bbar_bwd(cr_col, ci_col, b_re, b_im, d_re, d_im):
    def body(cr, ci, br, bi, g1, g2, o1, o2, o3, o4):
        _, vjp = jax.vjp(_bbar, cr[...], ci[...], br[...], bi[...])
        o1[...], o2[...], o3[...], o4[...] = vjp((g1[...], g2[...]))

    w = b_re.shape[1]
    return _rowcall(body, "s5_bbar_bwd", b_re.shape[0],
                    [(cr_col, "row"), (ci_col, "row"), (b_re, "row"), (b_im, "row"), (d_re, "row"), (d_im, "row")],
                    [(1, F32, "row"), (1, F32, "row"), (w, F32, "row"), (w, F32, "row")], tile_rows=1024)


def _block_diag_in(t):
    g, p, c = t.shape
    nb = g // GROUPS_PER_BLOCK
    t4 = t.reshape(nb, GROUPS_PER_BLOCK, p, c).transpose(0, 1, 3, 2)
    eye = jnp.eye(GROUPS_PER_BLOCK, dtype=t.dtype)
    return (t4[:, :, :, None, :] * eye[None, :, None, :, None]).reshape(nb, GROUPS_PER_BLOCK * c, GROUPS_PER_BLOCK * p)


def _block_diag_in_extract(d, p, c):
    nb = d.shape[0]
    d5 = d.reshape(nb, GROUPS_PER_BLOCK, c, GROUPS_PER_BLOCK, p)
    diag = jnp.stack([d5[:, g, :, g, :] for g in range(GROUPS_PER_BLOCK)], axis=1)
    return diag.transpose(0, 1, 3, 2).reshape(nb * GROUPS_PER_BLOCK, p, c)


def _block_diag_out(t):
    g, c, p = t.shape
    nb = g // GROUPS_PER_BLOCK
    t4 = t.reshape(nb, GROUPS_PER_BLOCK, c, p).transpose(0, 1, 3, 2)
    eye = jnp.eye(GROUPS_PER_BLOCK, dtype=t.dtype)
    return (t4[:, :, :, None, :] * eye[None, :, None, :, None]).reshape(nb, GROUPS_PER_BLOCK * p, GROUPS_PER_BLOCK * c)


def _block_diag_out_extract(d, c, p):
    nb = d.shape[0]
    d5 = d.reshape(nb, GROUPS_PER_BLOCK, p, GROUPS_PER_BLOCK, c)
    diag = jnp.stack([d5[:, g, :, g, :] for g in range(GROUPS_PER_BLOCK)], axis=1)
    return diag.transpose(0, 1, 3, 2).reshape(nb * GROUPS_PER_BLOCK, c, p)


def _scan_step(ar, ai, hr, hi, xr, xi):
    return ar * hr - ai * hi + xr, ar * hi + ai * hr + xi


def _s5_scan_fwd(u, bd_re, bd_im, cd_re, cd_im, ab_re, ab_im, init_re, init_im, d_row, full, name):
    s, w = u.shape
    nb = w // LANES
    rows = _tile(s, 512, SUBLANES)
    nc = s // rows
    steps = rows // N_SEG
    ns = nb * BLOCK_STATE

    def body(u_ref, bdr, bdi, cdr, cdi, ar_ref, ai_ref, ir_ref, ii_ref, d_ref, *outs):
        if full:
            y_ref, yg_ref, hr_ref, hi_ref, er_ref, ei_ref, cr, ci = outs
        else:
            er_ref, ei_ref, hr_ref, hi_ref, cr, ci = outs
        c = pl.program_id(1)

        @pl.when(c == 0)
        def _():
            cr[...] = ir_ref[...]
            ci[...] = ii_ref[...]

        ub = u_ref[...].astype(BF16)
        hr_ref[...] = jnp.dot(ub, bdr[...], preferred_element_type=F32)
        hi_ref[...] = jnp.dot(ub, bdi[...], preferred_element_type=F32)
        ar, ai = ar_ref[...], ai_ref[...]

        def step(j, carry):
            off = pl.multiple_of(j * N_SEG, N_SEG)
            nr, ni = _scan_step(ar, ai, carry[0], carry[1], hr_ref[pl.ds(off, N_SEG), :], hi_ref[pl.ds(off, N_SEG), :])
            hr_ref[pl.ds(off, N_SEG), :] = nr
            hi_ref[pl.ds(off, N_SEG), :] = ni
            return nr, ni

        hr, hi = lax.fori_loop(0, steps, step, (cr[...], ci[...]), unroll=8)
        cr[...] = hr
        ci[...] = hi
        if full:
            y = (jnp.dot(hr_ref[...].astype(BF16), cdr[...], preferred_element_type=F32)
                 + jnp.dot(hi_ref[...].astype(BF16), cdi[...], preferred_element_type=F32)
                 + d_ref[...] * u_ref[...])
            y_ref[...] = y
            yg_ref[...] = jax.nn.gelu(y).astype(BF16)

        @pl.when(c == nc - 1)
        def _():
            er_ref[...] = hr
            ei_ref[...] = hi

    blk3 = lambda a: pl.BlockSpec((None,) + a.shape[1:], lambda k, c: (k, 0, 0))
    seg = pl.BlockSpec((N_SEG, BLOCK_STATE), lambda k, c: (0, k))
    st = pl.BlockSpec((rows, BLOCK_STATE), lambda k, c: (c, k))
    in_specs = [pl.BlockSpec((rows, LANES), lambda k, c: (c, k)), blk3(bd_re), blk3(bd_im), blk3(cd_re), blk3(cd_im),
                seg, seg, seg, seg, pl.BlockSpec((1, LANES), lambda k, c: (0, k))]
    seg_shape = jax.ShapeDtypeStruct((N_SEG, ns), F32)
    st_shape = jax.ShapeDtypeStruct((s, ns), F32)
    carry = [pltpu.VMEM((N_SEG, BLOCK_STATE), F32)] * 2
    if full:
        ych = pl.BlockSpec((rows, LANES), lambda k, c: (c, k))
        out_specs = [ych, ych, st, st, seg, seg]
        out_shape = [jax.ShapeDtypeStruct((s, w), F32), jax.ShapeDtypeStruct((s, w), BF16), st_shape, st_shape, seg_shape, seg_shape]
        scratch = carry
    else:
        out_specs = [seg, seg]
        out_shape = [seg_shape, seg_shape]
        scratch = [pltpu.VMEM((rows, BLOCK_STATE), F32)] * 2 + carry
    return pl.pallas_call(
        body, name=name, grid=(nb, nc), in_specs=in_specs, out_specs=out_specs, out_shape=out_shape,
        scratch_shapes=scratch, compiler_params=_cparams("parallel", "arbitrary"),
    )(u, bd_re, bd_im, cd_re, cd_im, ab_re, ab_im, init_re, init_im, d_row)


def _s5_seg_fix(e_re, e_im, ab_re, ab_im, seg_len, reverse, name):
    assert seg_len & (seg_len - 1) == 0

    def body(er, ei, ar, ai, o_re, o_im):
        pr, pi = ar[0:1, :], ai[0:1, :]
        for _ in range(int(math.log2(seg_len))):
            pr, pi = pr * pr - pi * pi, 2.0 * pr * pi
        tr = jnp.zeros_like(pr)
        ti = jnp.zeros_like(pr)
        order = list(range(N_SEG - 1, -1, -1)) if reverse else list(range(N_SEG))
        for n, sgm in enumerate(order):
            o_re[sgm:sgm + 1, :] = tr
            o_im[sgm:sgm + 1, :] = ti
            if n < N_SEG - 1:
                tr, ti = _scan_step(pr, pi, tr, ti, er[sgm:sgm + 1, :], ei[sgm:sgm + 1, :])

    sh = jax.ShapeDtypeStruct(e_re.shape, F32)
    return pl.pallas_call(body, name=name, out_shape=(sh, sh))(e_re, e_im, ab_re, ab_im)


def _s5_scan_bwd(dy, u, h_re, h_im, bd_re, bd_im, cd_re, cd_im, ab_re, ab_imn, gin_re, gin_im, d_row, full, name, duz=None):
    s, w = u.shape
    nb = w // LANES
    rows = _tile(s, 512, SUBLANES)
    nc = s // rows
    steps = rows // N_SEG
    ns = nb * BLOCK_STATE

    def body(dy_ref, u_ref, hr_ref, hi_ref, bdr, bdi, cdr, cdi, ar_ref, ai_ref, ir_ref, ii_ref, d_ref, *outs):
        if full:
            _, du_ref, dbr_ref, dbi_ref, dcr_ref, dci_ref, dar_ref, dai_ref, dd_ref, gr, gi, accr, acci = outs
        else:
            er_ref, ei_ref, gr, gi = outs
        c = pl.program_id(1)

        @pl.when(c == 0)
        def _():
            gr[pl.ds(rows, N_SEG), :] = ir_ref[...]
            gi[pl.ds(rows, N_SEG), :] = ii_ref[...]
            if full:
                for r in (dbr_ref, dbi_ref, dcr_ref, dci_ref, dd_ref, accr, acci):
                    r[...] = jnp.zeros_like(r)

        dyb = dy_ref[...].astype(BF16)
        nt = (_DOT_DIMS["nt"], ((), ()))
        tn = (_DOT_DIMS["tn"], ((), ()))
        gr[pl.ds(0, rows), :] = lax.dot_general(dyb, cdr[...], nt, preferred_element_type=F32)
        gi[pl.ds(0, rows), :] = lax.dot_general(dyb, cdi[...], nt, preferred_element_type=F32)
        ar, ai = ar_ref[...], ai_ref[...]

        def step(jj, carry):
            off = pl.multiple_of((steps - 1 - jj) * N_SEG, N_SEG)
            nr, ni = _scan_step(ar, ai, carry[0], carry[1], gr[pl.ds(off, N_SEG), :], gi[pl.ds(off, N_SEG), :])
            gr[pl.ds(off, N_SEG), :] = nr
            gi[pl.ds(off, N_SEG), :] = ni
            return nr, ni

        g0r, g0i = lax.fori_loop(0, steps, step, (gr[pl.ds(rows, N_SEG), :], gi[pl.ds(rows, N_SEG), :]), unroll=8)
        if full:
            hr, hi = hr_ref[...], hi_ref[...]
            gnr, gni = gr[pl.ds(N_SEG, rows), :], gi[pl.ds(N_SEG, rows), :]
            accr[...] += jnp.sum((gnr * hr + gni * hi).reshape(steps, N_SEG, BLOCK_STATE), axis=0)
            acci[...] += jnp.sum((gni * hr - gnr * hi).reshape(steps, N_SEG, BLOCK_STATE), axis=0)
        gr[pl.ds(rows, N_SEG), :] = g0r
        gi[pl.ds(rows, N_SEG), :] = g0i
        if full:
            ub = u_ref[...].astype(BF16)
            gbr, gbi = gr[pl.ds(0, rows), :].astype(BF16), gi[pl.ds(0, rows), :].astype(BF16)
            dcr_ref[...] += lax.dot_general(hr.astype(BF16), dyb, tn, preferred_element_type=F32)
            dci_ref[...] += lax.dot_general(hi.astype(BF16), dyb, tn, preferred_element_type=F32)
            dbr_ref[...] += lax.dot_general(ub, gbr, tn, preferred_element_type=F32)
            dbi_ref[...] += lax.dot_general(ub, gbi, tn, preferred_element_type=F32)
            du_ref[...] = (lax.dot_general(gbr, bdr[...], nt, preferred_element_type=F32)
                           + lax.dot_general(gbi, bdi[...], nt, preferred_element_type=F32)
                           + d_ref[...] * dy_ref[...]).astype(BF16)
            dd_ref[...] += jnp.sum(dy_ref[...] * u_ref[...], axis=0, keepdims=True)

        @pl.when(c == nc - 1)
        def _():
            if full:
                dar_ref[...] = jnp.sum(accr[...], axis=0, keepdims=True)
                dai_ref[...] = jnp.sum(acci[...], axis=0, keepdims=True)
            else:
                er_ref[...] = g0r
                ei_ref[...] = g0i

    rev = lambda k, c: (nc - 1 - c, k)
    blk3 = lambda a: pl.BlockSpec((None,) + a.shape[1:], lambda k, c: (k, 0, 0))
    seg = pl.BlockSpec((N_SEG, BLOCK_STATE), lambda k, c: (0, k))
    st = pl.BlockSpec((rows, BLOCK_STATE), rev)
    ch = pl.BlockSpec((rows, LANES), rev)
    vec = pl.BlockSpec((1, LANES), lambda k, c: (0, k))
    if not full:
        st = pl.BlockSpec((rows, BLOCK_STATE), lambda k, c: (0, k))
    in_specs = [ch, ch if full else pl.BlockSpec((rows, LANES), lambda k, c: (0, k)), st, st,
                blk3(bd_re), blk3(bd_im), blk3(cd_re), blk3(cd_im), seg, seg, seg, seg, vec]
    args = [dy, u, h_re, h_im, bd_re, bd_im, cd_re, cd_im, ab_re, ab_imn, gin_re, gin_im, d_row]
    gbuf = [pltpu.VMEM((rows + N_SEG, BLOCK_STATE), F32)] * 2
    if full:
        row1 = pl.BlockSpec((1, BLOCK_STATE), lambda k, c: (0, k))
        out_specs = [ch, blk3(bd_re), blk3(bd_im), blk3(cd_re), blk3(cd_im), row1, row1, vec]
        out_shape = [jax.ShapeDtypeStruct(duz.shape, BF16),
                     jax.ShapeDtypeStruct(bd_re.shape, F32), jax.ShapeDtypeStruct(bd_im.shape, F32),
                     jax.ShapeDtypeStruct(cd_re.shape, F32), jax.ShapeDtypeStruct(cd_im.shape, F32),
                     jax.ShapeDtypeStruct((1, ns), F32), jax.ShapeDtypeStruct((1, ns), F32),
                     jax.ShapeDtypeStruct((1, w), F32)]
        scratch = gbuf + [pltpu.VMEM((N_SEG, BLOCK_STATE), F32)] * 2
        in_specs.append(pl.BlockSpec(memory_space=pl.ANY))
        args.append(duz)
        aliases = {len(args) - 1: 0}
    else:
        out_specs = [seg, seg]
        out_shape = [jax.ShapeDtypeStruct((N_SEG, ns), F32)] * 2
        scratch = gbuf
        aliases = {}
    return pl.pallas_call(
        body, name=name, grid=(nb, nc), in_specs=in_specs, out_specs=out_specs, out_shape=out_shape,
        input_output_aliases=aliases, scratch_shapes=scratch, compiler_params=_cparams("parallel", "arbitrary"),
    )(*args)


def _log_sigmoid(x):
    return jnp.minimum(x, 0.0) - jnp.log(1.0 + jnp.exp(-jnp.abs(x)))


def _tri(n, upper):
    r = lax.broadcasted_iota(jnp.int32, (n, n), 0)
    c = lax.broadcasted_iota(jnp.int32, (n, n), 1)
    return jnp.where((c >= r) if upper else (r >= c), 1.0, 0.0).astype(F32)


def _cum_fwd(f_logit, b_row, name):
    s, w = f_logit.shape
    t = _tile(s, 256, SUBLANES)

    def body(f_ref, b_ref, o_ref, carry):
        @pl.when(pl.program_id(0) == 0)
        def _():
            carry[...] = jnp.zeros_like(carry)

        lf = _log_sigmoid(f_ref[...] + b_ref[...])
        cum = jnp.dot(_tri(t, False), lf, precision=lax.Precision.HIGHEST, preferred_element_type=F32) + carry[...]
        o_ref[...] = cum * LOG2E
        carry[...] = cum[t - 1:t, :]

    return pl.pallas_call(
        body, name=name, grid=(s // t,),
        in_specs=[pl.BlockSpec((t, w), lambda i: (i, 0)), pl.BlockSpec((1, w), lambda i: (0, 0))],
        out_specs=pl.BlockSpec((t, w), lambda i: (i, 0)), out_shape=jax.ShapeDtypeStruct((s, w), F32),
        scratch_shapes=[pltpu.VMEM((1, w), F32)], compiler_params=_cparams("arbitrary"),
    )(f_logit, b_row)


def _cum_bwd(dcq, dck, f_logit, b_row, name):
    s, w = f_logit.shape
    t = _tile(s, 256, SUBLANES)
    nt = s // t

    def body(q_ref, k_ref, f_ref, b_ref, df_ref, db_ref, carry):
        @pl.when(pl.program_id(0) == 0)
        def _():
            carry[...] = jnp.zeros_like(carry)
            db_ref[...] = jnp.zeros_like(db_ref)

        dc = q_ref[...] - k_ref[...]
        rc = jnp.dot(_tri(t, True), dc, precision=lax.Precision.HIGHEST, preferred_element_type=F32) + carry[...]
        carry[...] = rc[0:1, :]
        df = rc * (1.0 - jax.nn.sigmoid(f_ref[...] + b_ref[...]))
        df_ref[...] = df.astype(BF16)
        db_ref[...] += jnp.sum(df, axis=0, keepdims=True)

    rev = pl.BlockSpec((t, w), lambda i: (nt - 1 - i, 0))
    one = pl.BlockSpec((1, w), lambda i: (0, 0))
    return pl.pallas_call(
        body, name=name, grid=(nt,), in_specs=[rev, rev, rev, one], out_specs=[rev, one],
        out_shape=[jax.ShapeDtypeStruct((s, w), BF16), jax.ShapeDtypeStruct((1, w), F32)],
        scratch_shapes=[pltpu.VMEM((1, w), F32)], compiler_params=_cparams("arbitrary"),
    )(dcq, dck, f_logit, b_row)


def _head_col(cum_tile, h):
    lane = lax.broadcasted_iota(jnp.int32, cum_tile.shape, 1)
    return jnp.sum(jnp.where(lane == h, cum_tile, 0.0), axis=1, keepdims=True)


def _attn_tiles(s):
    return _tile(s, 512, LANES)


def _exp2_rows(sc, sub):
    return jnp.concatenate([jnp.exp2(sc[:, b * LANES:(b + 1) * LANES] - sub) for b in range(sc.shape[1] // LANES)], axis=1)


def _row_of(rep):
    return jnp.transpose(rep)[0:1, :]


def _causal(sc, keys_on_rows):
    r = lax.broadcasted_iota(jnp.int32, sc.shape, 0)
    c = lax.broadcasted_iota(jnp.int32, sc.shape, 1)
    return jnp.where((r <= c) if keys_on_rows else (c <= r), sc, NEG_INF)


def _fox_fwd(q2, kv, cum2_t, z, name):
    s, w = q2.shape
    nh = w // HEAD_DIM
    tq = _attn_tiles(s)
    nq = s // tq
    nt = (_DOT_DIMS["nt"], ((), ()))

    def body(q_ref, k_ref, v_ref, ct_ref, z_ref, o_ref, oz_ref, lse_row_ref, m_s, acc_s, vaug, s_buf):
        i = pl.program_id(1)

        @pl.when(i == 0)
        def _():
            vaug[:, :HEAD_DIM] = v_ref[...]
            vaug[:, HEAD_DIM:] = jnp.ones((s, LANES), BF16)

        qb = q_ref[...]
        m_s[...] = jnp.full_like(m_s, NEG_INF)
        acc_s[...] = jnp.zeros_like(acc_s)

        def scores(j):
            off = pl.multiple_of(j * tq, tq)
            return lax.dot_general(qb, k_ref[pl.ds(off, tq), :], nt, preferred_element_type=F32) - ct_ref[:, pl.ds(off, tq)]

        def softmax_pv(j, sc):
            m_old = m_s[...]
            m_new = jnp.maximum(m_old, jnp.max(sc, axis=1, keepdims=True))
            p = _exp2_rows(sc, m_new)
            alpha = jnp.exp2(m_old - m_new)
            pv = jnp.dot(p.astype(BF16), vaug[pl.ds(pl.multiple_of(j * tq, tq), tq), :], preferred_element_type=F32)
            acc_s[...] = jnp.concatenate([alpha, alpha], axis=1) * acc_s[...] + pv
            m_s[...] = m_new

        s_buf[...] = scores(0)

        def loop(j, carry):
            nxt = scores(j + 1)
            softmax_pv(j, s_buf[...])
            s_buf[...] = nxt
            return carry

        lax.fori_loop(0, i, loop, 0)
        softmax_pv(i, _causal(s_buf[...], False))
        l = acc_s[:, HEAD_DIM:]
        o = acc_s[:, :HEAD_DIM] / l
        o_ref[...] = o
        oz_ref[...] = (o * _silu(z_ref[...].astype(F32))).astype(BF16)
        lse_row_ref[...] = _row_of(m_s[...] + jnp.log(l) * LOG2E)

    return pl.pallas_call(
        body, name=name, grid=(nh, nq),
        in_specs=[pl.BlockSpec((tq, HEAD_DIM), lambda h, i: (i, h)),
                  pl.BlockSpec((s, HEAD_DIM), lambda h, i: (0, h)),
                  pl.BlockSpec((s, HEAD_DIM), lambda h, i: (0, nh + h)),
                  pl.BlockSpec((None, 1, s), lambda h, i: (h, 0, 0)),
                  pl.BlockSpec((tq, HEAD_DIM), lambda h, i: (i, h))],
        out_specs=[pl.BlockSpec((tq, HEAD_DIM), lambda h, i: (i, h)),
                   pl.BlockSpec((tq, HEAD_DIM), lambda h, i: (i, h)),
                   pl.BlockSpec((None, 1, tq), lambda h, i: (h, 0, i))],
        out_shape=[jax.ShapeDtypeStruct((s, w), F32), jax.ShapeDtypeStruct((s, w), BF16),
                   jax.ShapeDtypeStruct((nh, 1, s), F32)],
        scratch_shapes=[pltpu.VMEM((tq, LANES), F32), pltpu.VMEM((tq, HEAD_DIM + LANES), F32),
                        pltpu.VMEM((s, HEAD_DIM + LANES), BF16), pltpu.VMEM((tq, tq), F32)],
        compiler_params=_cparams("arbitrary", "arbitrary"),
    )(q2, kv, kv, cum2_t, z)


def _fox_bwd(q2, kv, do, o, lse2_t, cum2, dqz, name):
    s, w = q2.shape
    nh = w // HEAD_DIM
    tk = _attn_tiles(s)
    nk = s // tk
    scale = HEAD_DIM ** -0.5
    nt = (_DOT_DIMS["nt"], ((), ()))
    tn = (_DOT_DIMS["tn"], ((), ()))

    def body(q_ref, k_ref, v_ref, do_ref, o_ref, lse_ref, c_ref, _, dk_ref, dv_ref, dq_ref, dcq_ref, dck_ref,
             dk_s, dv_s, dc_s, dq_s, dcq_s, dl_s, s_buf, dp_buf):
        h, j = pl.program_id(0), pl.program_id(1)

        @pl.when(j == 0)
        def _():
            dq_s[...] = jnp.zeros_like(dq_s)
            dcq_s[...] = jnp.zeros_like(dcq_s)
            for i in range(nk):
                rows = pl.ds(i * tk, tk)
                d = jnp.sum(do_ref[rows, :].astype(F32) * o_ref[rows, :], axis=1, keepdims=True)
                dl_s[:, i * tk:(i + 1) * tk] = _row_of(jnp.broadcast_to(d, (tk, LANES)))

        kb = k_ref[...]
        vb = v_ref[...]
        ck = jnp.broadcast_to(_head_col(c_ref[...], h), (tk, LANES))
        dk_s[...] = jnp.zeros_like(dk_s)
        dv_s[...] = jnp.zeros_like(dv_s)
        dc_s[...] = jnp.zeros_like(dc_s)

        def scores(i):
            off = pl.multiple_of(i * tk, tk)
            sc = lax.dot_general(kb, q_ref[pl.ds(off, tk), :], nt, preferred_element_type=F32) - lse_ref[:, pl.ds(off, tk)]
            dp = lax.dot_general(vb, do_ref[pl.ds(off, tk), :], nt, preferred_element_type=F32) - dl_s[:, pl.ds(off, tk)]
            return sc, dp

        def accumulate(i, sc, dp):
            off = pl.multiple_of(i * tk, tk)
            p = _exp2_rows(sc, ck)
            dv_s[...] += jnp.dot(p.astype(BF16), do_ref[pl.ds(off, tk), :], preferred_element_type=F32)
            ds = p * dp
            dsb = ds.astype(BF16)
            dk_s[...] += jnp.dot(dsb, q_ref[pl.ds(off, tk), :], preferred_element_type=F32)
            dq_s[pl.ds(off, tk), :] += lax.dot_general(dsb, kb, tn, preferred_element_type=F32)
            dcq_s[:, pl.ds(off, tk)] += jnp.sum(ds, axis=0, keepdims=True)
            part = ds[:, :LANES]
            for b in range(1, tk // LANES):
                part = part + ds[:, b * LANES:(b + 1) * LANES]
            dc_s[...] += part

        sc0, dp0 = scores(j)
        s_buf[...] = _causal(sc0, True)
        dp_buf[...] = dp0

        def loop(i, carry):
            nxt = scores(i + 1)
            accumulate(i, s_buf[...], dp_buf[...])
            s_buf[...], dp_buf[...] = nxt
            return carry

        lax.fori_loop(j, nk - 1, loop, 0)
        accumulate(nk - 1, s_buf[...], dp_buf[...])
        dk_ref[...] = (dk_s[...] * (1.0 / LOG2E)).astype(BF16)
        dv_ref[...] = dv_s[...].astype(BF16)
        dck_ref[...] = jnp.sum(jnp.transpose(dc_s[...]), axis=0, keepdims=True)

        @pl.when(j == nk - 1)
        def _():
            dq_ref[...] = (dq_s[...] * scale).astype(BF16)
            dcq_ref[...] = dcq_s[...]

    col = pl.BlockSpec((s, HEAD_DIM), lambda h, j: (0, h))
    row = pl.BlockSpec((None, 1, s), lambda h, j: (h, 0, 0))
    kspec = pl.BlockSpec((tk, HEAD_DIM), lambda h, j: (j, h))
    return pl.pallas_call(
        body, name=name, grid=(nh, nk),
        in_specs=[col, kspec, pl.BlockSpec((tk, HEAD_DIM), lambda h, j: (j, nh + h)), col, col, row,
                  pl.BlockSpec((tk, LANES), lambda h, j: (j, 0)), pl.BlockSpec(memory_space=pl.ANY)],
        out_specs=[kspec, kspec, col, row, pl.BlockSpec((None, 1, tk), lambda h, j: (h, 0, j))],
        out_shape=[jax.ShapeDtypeStruct((s, w), BF16), jax.ShapeDtypeStruct((s, w), BF16),
                   jax.ShapeDtypeStruct(dqz.shape, BF16), jax.ShapeDtypeStruct((nh, 1, s), F32),
                   jax.ShapeDtypeStruct((nh, 1, s), F32)],
        input_output_aliases={7: 2},
        scratch_shapes=[pltpu.VMEM((tk, HEAD_DIM), F32), pltpu.VMEM((tk, HEAD_DIM), F32), pltpu.VMEM((tk, LANES), F32),
                        pltpu.VMEM((s, HEAD_DIM), F32), pltpu.VMEM((1, s), F32), pltpu.VMEM((1, s), F32),
                        pltpu.VMEM((tk, tk), F32), pltpu.VMEM((tk, tk), F32)],
        compiler_params=_cparams("arbitrary", "arbitrary"),
    )(q2, kv, kv, do, o, lse2_t, cum2, dqz)


def _fox_bwd_dq(q2, kv, do, o, lse2, cum2_t, dqz, name):
    s, w = q2.shape
    nh = w // HEAD_DIM
    tq = _attn_tiles(s)
    nq = s // tq
    scale = HEAD_DIM ** -0.5
    nt = (_DOT_DIMS["nt"], ((), ()))

    def body(q_ref, k_ref, v_ref, do_ref, o_ref, lse_ref, ct_ref, _, dq_ref, dl_ref, dcq_ref, acc_s, dc_s):
        i = pl.program_id(1)
        qb = q_ref[...]
        dob = do_ref[...]
        lse = lse_ref[...]
        delta = jnp.broadcast_to(jnp.sum(dob.astype(F32) * o_ref[...], axis=1, keepdims=True), (tq, LANES))
        acc_s[...] = jnp.zeros_like(acc_s)
        dc_s[...] = jnp.zeros_like(dc_s)

        def tile(j, masked):
            off = pl.multiple_of(j * tq, tq)
            kb = k_ref[pl.ds(off, tq), :]
            sc = lax.dot_general(qb, kb, nt, preferred_element_type=F32) - ct_ref[:, pl.ds(off, tq)]
            if masked:
                sc = _causal(sc, False)
            p = _exp2_rows(sc, lse)
            dp = lax.dot_general(dob, v_ref[pl.ds(off, tq), :], nt, preferred_element_type=F32)
            ds = p * (dp - jnp.concatenate([delta] * (tq // LANES), axis=1))
            acc_s[...] += jnp.dot(ds.astype(BF16), kb, preferred_element_type=F32)
            part = ds[:, :LANES]
            for b in range(1, tq // LANES):
                part = part + ds[:, b * LANES:(b + 1) * LANES]
            dc_s[...] += part

        def loop(j, carry):
            tile(j, False)
            return carry

        lax.fori_loop(0, i, loop, 0)
        tile(i, True)
        dq_ref[...] = (acc_s[...] * scale).astype(BF16)
        dl_ref[...] = _row_of(delta)
        dcq_ref[...] = jnp.sum(jnp.transpose(dc_s[...]), axis=0, keepdims=True)

    qspec = pl.BlockSpec((tq, HEAD_DIM), lambda h, i: (i, h))
    rep = pl.BlockSpec((None, tq, LANES), lambda h, i: (h, i, 0))
    rowspec = pl.BlockSpec((None, 1, tq), lambda h, i: (h, 0, i))
    return pl.pallas_call(
        body, name=name, grid=(nh, nq),
        in_specs=[qspec,
                  pl.BlockSpec((s, HEAD_DIM), lambda h, i: (0, h)),
                  pl.BlockSpec((s, HEAD_DIM), lambda h, i: (0, nh + h)),
                  qspec, qspec, rep,
                  pl.BlockSpec((None, 1, s), lambda h, i: (h, 0, 0)),
                  pl.BlockSpec(memory_space=pl.ANY)],
        out_specs=[qspec, rowspec, rowspec],
        out_shape=[jax.ShapeDtypeStruct(dqz.shape, BF16), jax.ShapeDtypeStruct((nh, 1, s), F32),
                   jax.ShapeDtypeStruct((nh, 1, s), F32)],
        input_output_aliases={7: 0},
        scratch_shapes=[pltpu.VMEM((tq, HEAD_DIM), F32), pltpu.VMEM((tq, LANES), F32)],
        compiler_params=_cparams("parallel", "arbitrary"),
    )(q2, kv, kv, do, o, lse2, cum2_t, dqz)


def _fox_bwd_dkv(q2, kv, do, lse2_t, delta_t, cum2, name):
    s, w = q2.shape
    nh = w // HEAD_DIM
    tk = _attn_tiles(s)
    nk = s // tk
    nt = (_DOT_DIMS["nt"], ((), ()))

    def body(q_ref, k_ref, v_ref, do_ref, lse_ref, dl_ref, c_ref, dk_ref, dv_ref, dck_ref, dk_s, dv_s, dc_s, s_buf, dp_buf):
        h, j = pl.program_id(0), pl.program_id(1)
        kb = k_ref[...]
        vb = v_ref[...]
        ck = jnp.broadcast_to(_head_col(c_ref[...], h), (tk, LANES))
        dk_s[...] = jnp.zeros_like(dk_s)
        dv_s[...] = jnp.zeros_like(dv_s)
        dc_s[...] = jnp.zeros_like(dc_s)

        def scores(i):
            off = pl.multiple_of(i * tk, tk)
            sc = lax.dot_general(kb, q_ref[pl.ds(off, tk), :], nt, preferred_element_type=F32) - lse_ref[:, pl.ds(off, tk)]
            dp = lax.dot_general(vb, do_ref[pl.ds(off, tk), :], nt, preferred_element_type=F32) - dl_ref[:, pl.ds(off, tk)]
            return sc, dp

        def accumulate(i, sc, dp):
            off = pl.multiple_of(i * tk, tk)
            p = _exp2_rows(sc, ck)
            dv_s[...] += jnp.dot(p.astype(BF16), do_ref[pl.ds(off, tk), :], preferred_element_type=F32)
            ds = p * dp
            dk_s[...] += jnp.dot(ds.astype(BF16), q_ref[pl.ds(off, tk), :], preferred_element_type=F32)
            part = ds[:, :LANES]
            for b in range(1, tk // LANES):
                part = part + ds[:, b * LANES:(b + 1) * LANES]
            dc_s[...] += part

        sc0, dp0 = scores(j)
        s_buf[...] = _causal(sc0, True)
        dp_buf[...] = dp0

        def loop(i, carry):
            nxt = scores(i + 1)
            accumulate(i, s_buf[...], dp_buf[...])
            s_buf[...], dp_buf[...] = nxt
            return carry

        lax.fori_loop(j, nk - 1, loop, 0)
        accumulate(nk - 1, s_buf[...], dp_buf[...])
        dk_ref[...] = (dk_s[...] * (1.0 / LOG2E)).astype(BF16)
        dv_ref[...] = dv_s[...].astype(BF16)
        dck_ref[...] = jnp.sum(jnp.transpose(dc_s[...]), axis=0, keepdims=True)

    col = pl.BlockSpec((s, HEAD_DIM), lambda h, j: (0, h))
    row = pl.BlockSpec((None, 1, s), lambda h, j: (h, 0, 0))
    kspec = pl.BlockSpec((tk, HEAD_DIM), lambda h, j: (j, h))
    return pl.pallas_call(
        body, name=name, grid=(nh, nk),
        in_specs=[col, kspec, pl.BlockSpec((tk, HEAD_DIM), lambda h, j: (j, nh + h)), col, row, row,
                  pl.BlockSpec((tk, LANES), lambda h, j: (j, 0))],
        out_specs=[kspec, kspec, pl.BlockSpec((None, 1, tk), lambda h, j: (h, 0, j))],
        out_shape=[jax.ShapeDtypeStruct((s, w), BF16), jax.ShapeDtypeStruct((s, w), BF16),
                   jax.ShapeDtypeStruct((nh, 1, s), F32)],
        scratch_shapes=[pltpu.VMEM((tk, HEAD_DIM), F32), pltpu.VMEM((tk, HEAD_DIM), F32),
                        pltpu.VMEM((tk, LANES), F32), pltpu.VMEM((tk, tk), F32), pltpu.VMEM((tk, tk), F32)],
        compiler_params=_cparams("parallel", "arbitrary"),
    )(q2, kv, kv, do, lse2_t, delta_t, cum2)


def _exchange_copies(ins, outs, send_sems, recv_sems, local_sems, scatter):
    x, y, c = (lax.axis_index(a) for a in MESH_AXES)
    me = 4 * x + 2 * y + c
    local, remote = [], []
    for a in range(len(ins)):
        local.append(pltpu.make_async_copy(ins[a].at[me] if scatter else ins[a], outs[a].at[me], local_sems.at[a]))
        for k in range(1, N_DEV):
            px, py, pc = (1 - x if k & 4 else x), (1 - y if k & 2 else y), (1 - c if k & 1 else c)
            remote.append(pltpu.make_async_remote_copy(
                src_ref=ins[a].at[4 * px + 2 * py + pc] if scatter else ins[a], dst_ref=outs[a].at[me],
                send_sem=send_sems.at[a * (N_DEV - 1) + k - 1], recv_sem=recv_sems.at[a * (N_DEV - 1) + k - 1],
                device_id=(px, py, pc), device_id_type=pl.DeviceIdType.MESH))
    return local, remote


def _exchange_out_shapes(arrs, scatter):
    return [((N_DEV,) + a.shape[1:]) if scatter else ((N_DEV,) + a.shape) for a in arrs]


def _exchange(arrs, scatter, name):
    n = len(arrs)

    def body(*refs):
        local, remote = _exchange_copies(refs[:n], refs[n:2 * n], *refs[2 * n:], scatter)
        for cp in local + remote:
            cp.start()
        for cp in remote:
            cp.wait_send()
            cp.wait_recv()
        for cp in local:
            cp.wait()

    out_shape = [jax.ShapeDtypeStruct(s, a.dtype) for s, a in zip(_exchange_out_shapes(arrs, scatter), arrs)]
    return pl.pallas_call(
        body, name=name, out_shape=out_shape,
        in_specs=[pl.BlockSpec(memory_space=pl.ANY)] * n, out_specs=[pl.BlockSpec(memory_space=pl.ANY)] * n,
        scratch_shapes=[pltpu.SemaphoreType.DMA((n * (N_DEV - 1),)), pltpu.SemaphoreType.DMA((n * (N_DEV - 1),)),
                        pltpu.SemaphoreType.DMA((n,))],
    )(*arrs)


_HBM = pl.BlockSpec(memory_space=pltpu.HBM)
_SEM = pl.BlockSpec(memory_space=pltpu.SEMAPHORE)


def _exchange_start(arrs, scatter, name, after=()):
    n = len(arrs)
    after = list(after)
    lands = [lax.empty(s, a.dtype) for s, a in zip(_exchange_out_shapes(arrs, scatter), arrs)]

    def body(*refs):
        ins, outs = refs[:n], refs[n:2 * n]
        send_sems, recv_sems, local_sems = refs[2 * n + len(after):2 * n + len(after) + 3]
        token = refs[-1]
        local, remote = _exchange_copies(ins, outs, send_sems, recv_sems, local_sems, scatter)
        for cp in local + remote:
            cp.start()
        token[...] = jnp.zeros_like(token)

    hbm = lambda a: pltpu.HBM(a.shape, a.dtype)
    res = pl.pallas_call(
        body, name=name,
        out_shape=(pltpu.SemaphoreType.DMA((n * (N_DEV - 1),)), pltpu.SemaphoreType.DMA((n * (N_DEV - 1),)),
                   pltpu.SemaphoreType.DMA((n,)), *[hbm(a) for a in arrs], *[hbm(a) for a in lands],
                   jax.ShapeDtypeStruct((SUBLANES, LANES), F32)),
        in_specs=[_HBM] * (2 * n) + [pl.BlockSpec(memory_space=pl.ANY)] * len(after),
        out_specs=(_SEM, _SEM, _SEM, *[_HBM] * (2 * n), pl.BlockSpec(memory_space=pltpu.VMEM)),
        input_output_aliases={i: 3 + i for i in range(2 * n)},
        compiler_params=pltpu.CompilerParams(has_side_effects=pltpu.SideEffectType.DATAFLOW_SIDE_EFFECTING),
    )(*[pltpu.with_memory_space_constraint(a, pltpu.HBM) for a in list(arrs) + lands], *after)
    return (n, scatter, res[:3], res[3:3 + n], res[3 + n:3 + 2 * n]), res[-1]


def _exchange_wait(state, after, name):
    n, scatter, sems, srcs, lands = state
    after = list(after) if isinstance(after, (list, tuple)) else [after]

    def body(*refs):
        ins, outs = refs[:n], refs[n:2 * n]
        send_sems, recv_sems, local_sems = refs[2 * n:2 * n + 3]
        local, remote = _exchange_copies(ins, outs, send_sems, recv_sems, local_sems, scatter)
        for cp in remote:
            cp.wait_send()
            cp.wait_recv()
        for cp in local:
            cp.wait()

    hbm = lambda a: pltpu.HBM(a.shape, a.dtype)
    res = pl.pallas_call(
        body, name=name,
        out_shape=(*[hbm(a) for a in srcs], *[hbm(a) for a in lands]),
        in_specs=[_HBM] * (2 * n) + [_SEM] * 3 + [pl.BlockSpec(memory_space=pl.ANY)] * len(after),
        out_specs=tuple([_HBM] * (2 * n)),
        input_output_aliases={i: i for i in range(2 * n)},
        compiler_params=pltpu.CompilerParams(has_side_effects=pltpu.SideEffectType.DATAFLOW_SIDE_EFFECTING),
    )(*srcs, *lands, *sems, *after)
    return list(res[n:])


def _adamw_math(w, g, m, v):
    m = ADAM_B1 * m + (1.0 - ADAM_B1) * g
    v = ADAM_B2 * v + (1.0 - ADAM_B2) * (g * g)
    m_hat = m / (1.0 - ADAM_B1 ** ADAM_STEP)
    v_hat = v / (1.0 - ADAM_B2 ** ADAM_STEP)
    return -ADAM_LR * (m_hat / (jnp.sqrt(v_hat) + ADAM_EPS) + ADAM_WD * w), m, v


def _slot_sum(p_ref):
    g = p_ref[0].astype(F32)
    for d in range(1, p_ref.shape[0]):
        g = g + p_ref[d].astype(F32)
    return g


def _adamw_tile(r, c):
    return _tile(r, max(SUBLANES, (256 * 1024) // c // SUBLANES * SUBLANES), SUBLANES)


def _adamw(parts, w, m, v, name):
    r, c = w.shape[-2:]
    tr = _adamw_tile(r, c)

    def body(p_ref, w_ref, m_ref, v_ref, g_ref, d_ref, nm_ref, nv_ref):
        g = _slot_sum(p_ref)
        g_ref[...] = g
        d_ref[...], nm_ref[...], nv_ref[...] = _adamw_math(w_ref[...], g, m_ref[...], v_ref[...])

    if w.ndim == 3:
        blk = pl.BlockSpec((None, tr, c), lambda i: (0, i, 0))
    else:
        blk = pl.BlockSpec((tr, c), lambda i: (i, 0))
    sh = jax.ShapeDtypeStruct(w.shape, F32)
    return pl.pallas_call(
        body, name=name, grid=(r // tr,),
        in_specs=[pl.BlockSpec((parts.shape[0], tr, c), lambda i: (0, i, 0)), blk, blk, blk],
        out_specs=[blk] * 4, out_shape=[sh] * 4, compiler_params=_cparams("parallel"),
    )(parts, w, m, v)


def _sum_parts(parts, name):
    _, r, c = parts.shape
    tr = _adamw_tile(r, c)

    def body(p_ref, o_ref):
        o_ref[...] = _slot_sum(p_ref)

    return pl.pallas_call(
        body, name=name, grid=(r // tr,),
        in_specs=[pl.BlockSpec((parts.shape[0], tr, c), lambda i: (0, i, 0))],
        out_specs=pl.BlockSpec((tr, c), lambda i: (i, 0)), out_shape=jax.ShapeDtypeStruct((r, c), F32),
        compiler_params=_cparams("parallel"),
    )(parts)


def _perm(a):
    s, d = a.shape
    return a.reshape(N_SEG, s // N_SEG, d).transpose(1, 0, 2).reshape(s, d)


def _unperm(a):
    s, d = a.shape
    return a.reshape(s // N_SEG, N_SEG, d).transpose(1, 0, 2).reshape(s, d)


def _lane_pad(a, width=LANES):
    return jnp.pad(a, ((0, 0), (0, width - a.shape[1])))


def _local_step(x, target, norm_pre, norm_post, kv_norm, kv_b_f, a_re, a_im, log_dt, b_re, b_im, c_re, c_im, comm):
    s, d = x.shape
    g, p = a_re.shape
    w = g * S5_GROUP
    fw = d
    nh = fw // HEAD_DIM
    seg_len = s // N_SEG
    row = lambda v: v.reshape(1, -1)
    g_pre0, g_pre1, g_post0, g_post1, g_kv = row(norm_pre[0]), row(norm_pre[1]), row(norm_post[0]), row(norm_post[1]), row(kv_norm)

    ldt = log_dt.reshape(g, 1)
    abr, abi, cr, ci = _s5_disc_fwd(a_re, a_im, ldt)
    cr_col, ci_col = cr.reshape(g * p, 1), ci.reshape(g * p, 1)
    b_re2, b_im2 = b_re.reshape(g * p, S5_GROUP), b_im.reshape(g * p, S5_GROUP)
    bb_re, bb_im = _s5_bbar_fwd(cr_col, ci_col, b_re2, b_im2)
    bd_re = _block_diag_in(bb_re.reshape(g, p, S5_GROUP)).astype(BF16)
    bd_im = _block_diag_in(bb_im.reshape(g, p, S5_GROUP)).astype(BF16)
    cd_re = _block_diag_out(c_re).astype(BF16)
    cd_im = _block_diag_out(-c_im).astype(BF16)
    ab_re = jnp.broadcast_to(abr.reshape(1, g * p), (N_SEG, g * p))
    ab_im = jnp.broadcast_to(abi.reshape(1, g * p), (N_SEG, g * p))
    zero_seg = jnp.zeros((N_SEG, g * p), F32)

    xn0 = _norm_cast(x, g_pre0 + comm.token, "norm_pre0", x_kind="nat")
    w_in = comm.weight("s5_w_in", [xn0, bd_re, bd_im, cd_re, cd_im, ab_re, ab_im])
    d_row, bglu_row = row(comm.vector("s5_d")), row(comm.vector("s5_b_glu"))
    u = _mm(xn0, w_in, "nn", F32, "s5_in_u", b_cols=(0, w))
    z0 = _mm(xn0, w_in, "nn", BF16, "s5_in_z", b_cols=(w, w))
    e_re, e_im = _s5_scan_fwd(u, bd_re, bd_im, cd_re, cd_im, ab_re, ab_im, zero_seg, zero_seg, d_row, False, "s5_scan_ends")
    i_re, i_im = _s5_seg_fix(e_re, e_im, ab_re, ab_im, seg_len, False, "s5_seg_fix")
    y_ssm, yg, h_re, h_im, _, _ = _s5_scan_fwd(u, bd_re, bd_im, cd_re, cd_im, ab_re, ab_im, i_re, i_im, d_row, True, "s5_scan")
    w_glu, w_out = comm.weight("s5_w_glu", yg), comm.weight("s5_w_out", yg)
    gp = _mm(yg, w_glu, "nn", BF16, "s5_glu")
    y3 = _s5_gate(y_ssm, gp, bglu_row, z0, "s5_gate")
    w_kv, fw_in, fw_out = comm.weight("kv_w", y3), comm.weight("fox_w_in", y3), comm.weight("fox_w_out", y3)
    w_f = _lane_pad(w_kv[:, 2 * fw:])
    o0 = _mm(y3, w_out, "nn", F32, "s5_out")
    r0 = _post_norm(o0, g_post0, "norm_post0", out_kind="nat")

    h1, hn_kv, xn1 = _resid_norm2(x, r0, g_kv, g_pre1, "resid_norms")
    kv = _mm(hn_kv, w_kv, "nn", BF16, "kv_proj", b_cols=(0, 2 * fw))
    f_logit = _mm(hn_kv, w_f, "nn", F32, "f_proj")
    bf_row = _lane_pad(row(kv_b_f))
    cum2 = _cum_fwd(f_logit, bf_row, "cum_fwd")
    cum2_t = cum2[:, :nh].T.reshape(nh, 1, s)
    q2 = _mm(xn1, fw_in, "nn", BF16, "fox_q", scale=HEAD_DIM ** -0.5 * LOG2E, b_cols=(0, fw))
    z1 = _mm(xn1, fw_in, "nn", BF16, "fox_z", b_cols=(fw, fw))
    o, oz, lse2_t = _fox_fwd(q2, kv, cum2_t, z1, "fox_fwd")
    o1 = _mm(oz, fw_out, "nn", F32, "fox_out")
    dh2, do1, sq, dg_post1 = _post_norm_loss(o1, g_post1, h1, target, "norm_post1_loss")
    loss = 0.5 * jnp.sum(sq) / d

    d_fw_out = _mm(oz, do1, "tn", BF16, "fox_out_dw")
    d_oz = _mm(do1, fw_out, "nt", F32, "fox_out_dx")
    do, dqz = _gate_bwd(d_oz, o, z1, "fox_gate_bwd")
    dk, dv, dqz, dcq, dck = _fox_bwd(q2, kv, do, o, lse2_t, cum2, dqz, "fox_bwd")
    d_fw_in = _mm(xn1, dqz, "tn", BF16, "fox_in_dw", col_slots=True)
    dxn1 = _mm(dqz, fw_in, "nt", F32, "fox_in_dx")
    dcq_sl = _lane_pad(dcq.reshape(nh, s).T)
    dck_sl = _lane_pad(dck.reshape(nh, s).T)
    df, db_f = _cum_bwd(dcq_sl, dck_sl, f_logit, bf_row, "cum_bwd")
    dkv = _concat_cast(dk, dv, "fox_dkv")
    d_w_kvm = _mm(hn_kv, dkv, "tn", F32, "kv_dw")
    d_w_f = _mm(hn_kv, df, "tn", F32, "f_dw")
    dhn_f = _mm(df, w_f, "nt", F32, "f_dx")
    dhn_kv = _mm(dkv, w_kv, "nt", F32, "kv_dx", add=dhn_f, b_cols=(0, 2 * fw))
    d_w_kv = jnp.concatenate([d_w_kvm, d_w_f[:, :nh]], axis=1)
    tok = comm.send_grads(dict(fox_w_out=d_fw_out, fox_w_in=d_fw_in, kv_w=d_w_kv), "exchange_fox")
    dh1, dg_pre1, dg_kv = _norm_bwd2(dh2, h1, dxn1, dhn_kv, g_pre1, g_kv, "resid_norms_bwd")

    do0, dg_post0 = _post_norm_bwd(dh1, o0, g_post0 + tok[0, 0], "norm_post0_bwd", dy_kind="nat")
    d_w_out = _mm(y3, do0, "tn", BF16, "s5_out_dw")
    dy3 = _mm(do0, w_out, "nt", F32, "s5_out_dx")
    duz, dgp, dyg_direct, db_glu = _s5_gate_bwd(dy3, y_ssm, gp, bglu_row, z0, "s5_gate_bwd")
    d_w_glu = _mm(yg, dgp, "tn", BF16, "s5_glu_dw")
    dyg = _mm(dgp, w_glu, "nt", F32, "s5_glu_dx", add=dyg_direct)
    dy_ssm = _gelu_bwd(dyg, y_ssm, "s5_gelu_bwd")
    d_row = d_row + comm.send_grads(dict(s5_w_out=d_w_out, s5_w_glu=d_w_glu), "exchange_s5")[0, 0]
    ab_imn = -ab_im
    ge_re, ge_im = _s5_scan_bwd(dy_ssm, u, h_re, h_im, bd_re, bd_im, cd_re, cd_im, ab_re, ab_imn, zero_seg, zero_seg,
                                d_row, False, "s5_adj_ends")
    gi_re, gi_im = _s5_seg_fix(ge_re, ge_im, ab_re, ab_imn, seg_len, True, "s5_adj_fix")
    duz, dbd_re, dbd_im, dcd_re, dcd_im, dab_re, dab_im, dd = _s5_scan_bwd(
        dy_ssm, u, h_re, h_im, bd_re, bd_im, cd_re, cd_im, ab_re, ab_imn, gi_re, gi_im, d_row, True, "s5_adj", duz=duz)
    d_w_in = _mm(xn0, duz, "tn", BF16, "s5_in_dw", col_slots=True)
    tok = comm.send_grads(dict(s5_w_in=d_w_in), "exchange_s5_in")
    dxn0 = _mm(duz, w_in, "nt", F32, "s5_in_dx", after=tok)
    grad_x, dg_pre0 = _norm_bwd1(dh1, x, dxn0, g_pre0, "norm_pre0_bwd")

    dbb_re = _block_diag_in_extract(dbd_re, p, S5_GROUP).reshape(g * p, S5_GROUP)
    dbb_im = _block_diag_in_extract(dbd_im, p, S5_GROUP).reshape(g * p, S5_GROUP)
    dcr_col, dci_col, db_re, db_im = _s5_bbar_bwd(cr_col, ci_col, b_re2, b_im2, dbb_re, dbb_im)
    da_re, da_im, dldt = _s5_disc_bwd(a_re, a_im, ldt, dab_re.reshape(g, p), dab_im.reshape(g, p),
                                      dcr_col.reshape(g, p), dci_col.reshape(g, p))
    dc_re = _block_diag_out_extract(dcd_re, S5_GROUP, p)
    dc_im = -_block_diag_out_extract(dcd_im, S5_GROUP, p)

    small = dict(
        norm_pre=jnp.concatenate([dg_pre0, dg_pre1], axis=0), norm_post=jnp.concatenate([dg_post0, dg_post1], axis=0),
        s5_a_re=da_re, s5_a_im=da_im, s5_log_dt=dldt.reshape(g), s5_b_re=db_re.reshape(g, p, S5_GROUP),
        s5_b_im=db_im.reshape(g, p, S5_GROUP), s5_c_re=dc_re, s5_c_im=dc_im, s5_d=dd.reshape(-1),
        s5_b_glu=db_glu.reshape(-1), kv_norm=dg_kv.reshape(-1), kv_b_f=db_f[0, :nh])
    return loss, grad_x, small


_BIG = ("s5_w_in", "s5_w_glu", "s5_w_out", "kv_w", "fox_w_in", "fox_w_out")
_COL_SHARDED = ("s5_w_in", "kv_w", "fox_w_in")
_SMALL = ("norm_pre", "norm_post", "s5_a_re", "s5_a_im", "s5_log_dt", "s5_b_re", "s5_b_im", "s5_c_re", "s5_c_im",
          "s5_d", "s5_b_glu", "kv_norm", "kv_b_f")
_SMALL_SHARDED = ("s5_d", "s5_b_glu")
_PACK_QUANTUM = SUBLANES * LANES
_WEIGHTS = ('norm_pre', 'norm_post', 's5_w_in', 's5_a_re', 's5_a_im', 's5_log_dt', 's5_b_re', 's5_b_im', 's5_c_re', 's5_c_im',
            's5_d', 's5_w_glu', 's5_b_glu', 's5_w_out', 'kv_norm', 'kv_w', 'kv_b_f', 'fox_w_in', 'fox_w_out')


def _full_from_slots(name, slots):
    n, r, c = slots.shape
    if name in _COL_SHARDED:
        return slots.transpose(1, 0, 2).reshape(r, n * c)
    return slots.reshape(n * r, c)


def _slots_from_full(name, full):
    if name in _COL_SHARDED:
        r, nc = full.shape
        return full.reshape(r, N_DEV, nc // N_DEV).transpose(1, 0, 2)
    nr, c = full.shape
    return full.reshape(N_DEV, nr // N_DEV, c)


def _pack(vals):
    parts = []
    for v in vals:
        flat = v.reshape(-1)
        parts.append(jnp.pad(flat, (0, (-flat.shape[0]) % _PACK_QUANTUM)))
    total = sum(p.shape[0] for p in parts)
    parts.append(jnp.zeros(((-total) % (N_DEV * _PACK_QUANTUM),), F32))
    return jnp.concatenate(parts).reshape(-1, LANES)


def _unpack(packed, shapes):
    flat = packed.reshape(-1)
    out, off = [], 0
    for sh in shapes:
        n = math.prod(sh)
        out.append(flat[off:off + n].reshape(sh))
        off += n + (-n) % _PACK_QUANTUM
    return out


class _Comm:
    _GROUPS = (("s5_w_in",) + _SMALL_SHARDED, ("s5_w_glu", "s5_w_out"), ("kv_w", "fox_w_in", "fox_w_out"))

    def __init__(self, shards, vectors, early=()):
        shards = {**shards, **vectors}
        self._full, self._gathers = {}, {}
        self._early = list(early)
        self.token = jnp.zeros((), F32)
        for group in self._GROUPS:
            state, tok = _exchange_start([shards[n] for n in group], False, "gather_start_" + group[0])
            self._gathers[group] = state
            self.token = self.token + tok[0, 0]
        self._sent = []

    def vector(self, name):
        return self._full[name]

    def weight(self, name, after):
        if name not in self._full:
            group = next(g for g in self._GROUPS if name in g)
            if group == self._GROUPS[0]:
                after = (list(after) if isinstance(after, (list, tuple)) else [after]) + self._early
            slots = _exchange_wait(self._gathers.pop(group), after, "gather_wait_" + group[0])
            for n, sl in zip(group, slots):
                self._full[n] = sl.reshape(-1) if n in _SMALL_SHARDED else _full_from_slots(n, sl)
        return self._full[name]

    def send_grads(self, grads, name):
        names = list(grads)
        slots = [grads[n] if grads[n].ndim == 3 else _slots_from_full(n, grads[n]).astype(BF16) for n in names]
        state, tok = _exchange_start(slots, True, name + "_start")
        self._sent.append((names, state, name + "_wait"))
        return tok

    def received_grads(self, after):
        for names, state, name in self._sent:
            for n, recv in zip(names, _exchange_wait(state, after, name)):
                yield n, recv


def kernel(x, norm_pre, norm_post, s5_w_in, s5_a_re, s5_a_im, s5_log_dt, s5_b_re, s5_b_im, s5_c_re, s5_c_im, s5_d, s5_w_glu, s5_b_glu, s5_w_out, kv_norm, kv_w, kv_b_f, fox_w_in, fox_w_out, loss_target, m_norm_pre, m_norm_post, m_s5_w_in, m_s5_a_re, m_s5_a_im, m_s5_log_dt, m_s5_b_re, m_s5_b_im, m_s5_c_re, m_s5_c_im, m_s5_d, m_s5_w_glu, m_s5_b_glu, m_s5_w_out, m_kv_norm, m_kv_w, m_kv_b_f, m_fox_w_in, m_fox_w_out, v_norm_pre, v_norm_post, v_s5_w_in, v_s5_a_re, v_s5_a_im, v_s5_log_dt, v_s5_b_re, v_s5_b_im, v_s5_c_re, v_s5_c_im, v_s5_d, v_s5_w_glu, v_s5_b_glu, v_s5_w_out, v_kv_norm, v_kv_w, v_kv_b_f, v_fox_w_in, v_fox_w_out):
    env = dict(locals())
    wts = {n: env[n] for n in _WEIGHTS}
    mom = {n: env["m_" + n] for n in _WEIGHTS}
    var = {n: env["v_" + n] for n in _WEIGHTS}
    me = 4 * lax.axis_index("x") + 2 * lax.axis_index("y") + lax.axis_index("c")
    shard2d = {n: wts[n].reshape(wts[n].shape[-2:]) for n in _BIG}
    full_shape = {n: ((wts[n].size * N_DEV,) if n in _SMALL_SHARDED else wts[n].shape) for n in _SMALL}

    def spread(n, v):
        if n not in _SMALL_SHARDED:
            return v
        flat = v.reshape(-1)
        return lax.dynamic_update_slice(jnp.zeros(full_shape[n], F32), flat, (me * flat.shape[0],))

    packed = [_pack([spread(n, src[n]) for n in _SMALL] + [jnp.zeros((1,), F32)]) for src in (wts, mom, var)]
    comm = _Comm({n: shard2d[n].astype(BF16) for n in _BIG}, {n: wts[n].reshape(1, -1) for n in _SMALL_SHARDED}, packed)

    loss_local, grad_x, small = _local_step(
        x[0], loss_target[0], norm_pre, norm_post, kv_norm, kv_b_f, s5_a_re[0], s5_a_im[0], s5_log_dt[0],
        s5_b_re[0], s5_b_im[0], s5_c_re[0], s5_c_im[0], comm)

    small_pack = _pack([small[n] for n in _SMALL] + [loss_local.reshape(1)])
    slice_rows = small_pack.shape[0] // N_DEV
    small_state, small_tok = _exchange_start([small_pack.reshape(N_DEV, slice_rows, LANES)], True, "reduce_small_start")

    res = {}
    for n, recv in comm.received_grads([small_tok, grad_x]):
        res[n] = _adamw(recv, wts[n], mom[n], var[n], "adamw_" + n)

    my_sum = _sum_parts(_exchange_wait(small_state, res[_BIG[0]][0], "reduce_small_wait")[0], "sum_small")
    g_all = _exchange([my_sum], False, "gather_small")[0].reshape(1, small_pack.shape[0], LANES)
    outs = _adamw(g_all, *packed, "adamw_small")
    unpacked = [_unpack(o, [full_shape[n] for n in _SMALL] + [(1,)]) for o in outs]
    loss = unpacked[0][-1][0]
    for i, n in enumerate(_SMALL):
        vals = [u[i] for u in unpacked]
        if n in _SMALL_SHARDED:
            k = wts[n].size
            vals = [lax.dynamic_slice(v, (me * k,), (k,)) for v in vals]
        res[n] = [v.reshape(wts[n].shape) for v in vals]

    return (loss, grad_x[None], *[res[n][0] for n in _WEIGHTS], *[res[n][1] for n in _WEIGHTS],
            *[res[n][2] for n in _WEIGHTS], *[res[n][3] for n in _WEIGHTS])
```

```python
import functools
import math

import jax
import jax.numpy as jnp
from jax import lax
from jax.experimental import pallas as pl
from jax.experimental.pallas import tpu as pltpu

F32 = jnp.float32
BF16 = jnp.bfloat16

N_DEV = 8
MESH_AXES = ("x", "y", "c")
S5_GROUP = 16
S5_STATE = 64
LANES = 128
SUBLANES = 8
GROUPS_PER_BLOCK = LANES // S5_GROUP
BLOCK_STATE = GROUPS_PER_BLOCK * S5_STATE
N_SEG = SUBLANES
HEAD_DIM = 128
RMS_EPS = 1e-6
NEG_INF = -1e30
LOG2E = math.log2(math.e)
ADAM_LR = 0.001
ADAM_B1 = 0.9
ADAM_B2 = 0.999
ADAM_EPS = 1e-08
ADAM_WD = 0.01
ADAM_STEP = 10
VMEM_LIMIT = 56 * 1024 * 1024


def _tile(n, pref, quantum=LANES):
    if n <= pref:
        return n
    t = (pref // quantum) * quantum
    while t >= quantum:
        if n % t == 0:
            return t
        t -= quantum
    return n


def _cparams(*sem):
    return pltpu.CompilerParams(dimension_semantics=sem if sem else None, vmem_limit_bytes=VMEM_LIMIT)


_DOT_DIMS = {"nn": ((1,), (0,)), "nt": ((1,), (1,)), "tn": ((0,), (0,))}


def _mm(a, b, mode, out_dtype, name, add=None, scale=None, b_cols=None, after=None, col_slots=False):
    b_shape = b.shape if b_cols is None else (b.shape[0], b_cols[1])
    if mode == "nn":
        (M, K), (K2, N) = a.shape, b_shape
    elif mode == "nt":
        (M, K), (N, K2) = a.shape, b_shape
    else:
        (K, M), (K2, N) = a.shape, b_shape
    assert K == K2, (name, a.shape, b_shape)
    tm, tn, tk = _tile(M, 1024 if K <= 2048 else 512), (N // N_DEV if col_slots else _tile(N, 1024)), _tile(K, 4096)
    nk = K // tk
    dims = (_DOT_DIMS[mode], ((), ()))
    col0 = 0
    if b_cols is not None:
        assert mode != "tn" and b_cols[0] % (tn if mode == "nn" else tk) == 0
        col0 = b_cols[0] // (tn if mode == "nn" else tk)

    def body(*refs):
        a_ref, b_ref = refs[:2]
        c_ref = refs[2] if add is not None else None
        o_ref = refs[2 + (add is not None) + (after is not None)]
        part = lax.dot_general(a_ref[...], b_ref[...], dims, preferred_element_type=F32)

        def finish(r):
            if scale is not None:
                r = r * scale
            if add is not None:
                r = r + c_ref[...]
            o_ref[...] = r.astype(out_dtype)

        if nk == 1:
            finish(part)
            return
        acc = refs[-1]
        k = pl.program_id(2)

        @pl.when(k == 0)
        def _():
            acc[...] = part

        @pl.when(jnp.logical_and(k > 0, k < nk - 1))
        def _():
            acc[...] += part

        @pl.when(k == nk - 1)
        def _():
            finish(acc[...] + part)

    if mode == "tn":
        a_spec = pl.BlockSpec((tk, tm), lambda i, j, k: (k, i))
    else:
        a_spec = pl.BlockSpec((tm, tk), lambda i, j, k: (i, k))
    if mode == "nt":
        b_spec = pl.BlockSpec((tn, tk), lambda i, j, k: (j, k + col0))
    else:
        b_spec = pl.BlockSpec((tk, tn), lambda i, j, k: (k, j + col0))
    o_spec = pl.BlockSpec((tm, tn), lambda i, j, k: (i, j))
    in_specs = [a_spec, b_spec] + ([o_spec] if add is not None else [])
    args = (a, b) + ((add,) if add is not None else ())
    if after is not None:
        in_specs.append(pl.BlockSpec(after.shape, lambda i, j, k: (0, 0)))
        args += (after,)
    out_shape = jax.ShapeDtypeStruct((M, N), out_dtype)
    if col_slots:
        assert add is None
        o_spec = pl.BlockSpec((None, tm, tn), lambda i, j, k: (j, i, 0))
        out_shape = jax.ShapeDtypeStruct((N_DEV, M, tn), out_dtype)
    return pl.pallas_call(
        body, name=name, grid=(M // tm, N // tn, nk),
        in_specs=in_specs, out_specs=o_spec,
        out_shape=out_shape,
        scratch_shapes=[pltpu.VMEM((tm, tn), F32)] if nk > 1 else [],
        compiler_params=_cparams("parallel", "parallel", "arbitrary"),
    )(*args)


class _NatIn:
    def __init__(self, ref):
        self.ref = ref

    def __getitem__(self, idx):
        v = jnp.swapaxes(self.ref[...], 0, 1)
        return v.reshape(v.shape[0] * N_SEG, v.shape[2])


class _NatOut:
    def __init__(self, ref):
        self.ref = ref

    def __setitem__(self, idx, val):
        self.ref[...] = jnp.swapaxes(val.reshape(val.shape[0] // N_SEG, N_SEG, val.shape[1]), 0, 1)


def _rowcall(body, name, n_rows, ins, outs, tile_rows=256):
    tr = _tile(n_rows, tile_rows, SUBLANES * 2)
    n_in = len(ins)
    in_kinds = [k for _, k in ins]
    kinds = [k for _, _, k in outs]

    def kern(*refs):
        @pl.when(pl.program_id(0) == 0)
        def _():
            for r, kind in zip(refs[n_in:], kinds):
                if kind == "acc":
                    r[...] = jnp.zeros_like(r)

        wrapped = [_NatIn(r) if k == "nat" else r for r, k in zip(refs[:n_in], in_kinds)]
        wrapped += [_NatOut(r) if k == "nat" else r for r, k in zip(refs[n_in:], kinds)]
        body(*wrapped)

    in_specs, args = [], []
    for arr, kind in ins:
        if kind == "row":
            in_specs.append(pl.BlockSpec((tr, arr.shape[1]), lambda i: (i, 0)))
        elif kind == "nat":
            in_specs.append(pl.BlockSpec((N_SEG, tr // N_SEG, arr.shape[1]), lambda i: (0, i, 0)))
            arr = arr.reshape(N_SEG, n_rows // N_SEG, arr.shape[1])
        else:
            in_specs.append(pl.BlockSpec(arr.shape, lambda i, nd=arr.ndim: (0,) * nd))
        args.append(arr)
    out_specs, out_shape = [], []
    for width, dtype, kind in outs:
        if kind == "row":
            out_specs.append(pl.BlockSpec((tr, width), lambda i: (i, 0)))
            out_shape.append(jax.ShapeDtypeStruct((n_rows, width), dtype))
        elif kind == "right":
            out_specs.append(pl.BlockSpec((tr, width), lambda i: (i, 1)))
            out_shape.append(jax.ShapeDtypeStruct((n_rows, 2 * width), dtype))
        elif kind == "nat":
            out_specs.append(pl.BlockSpec((N_SEG, tr // N_SEG, width), lambda i: (0, i, 0)))
            out_shape.append(jax.ShapeDtypeStruct((N_SEG, n_rows // N_SEG, width), dtype))
        else:
            out_specs.append(pl.BlockSpec((1, width), lambda i: (0, 0)))
            out_shape.append(jax.ShapeDtypeStruct((1, width), F32))
    res = pl.pallas_call(
        kern, name=name, grid=(n_rows // tr,), in_specs=in_specs, out_specs=out_specs, out_shape=out_shape,
        compiler_params=_cparams("arbitrary"),
    )(*args)
    return [r.reshape(n_rows, r.shape[2]) if k == "nat" else r for r, k in zip(res, kinds)]


def _rstd(x):
    return lax.rsqrt(jnp.mean(x * x, axis=-1, keepdims=True) + RMS_EPS)


def _rms_bwd(x, g, dy):
    xh = x * _rstd(x)
    dxh = dy * g
    dx = _rstd(x) * (dxh - xh * jnp.mean(dxh * xh, axis=-1, keepdims=True))
    return dx, jnp.sum(dy * xh, axis=0, keepdims=True)


def _silu(z):
    return z * jax.nn.sigmoid(z)


def _norm_cast(x, g, name, x_kind="row"):
    def body(x_ref, g_ref, o_ref):
        x = x_ref[...]
        o_ref[...] = (x * _rstd(x) * g_ref[...]).astype(BF16)

    return _rowcall(body, name, x.shape[0], [(x, x_kind), (g, "full")], [(x.shape[1], BF16, "row")])[0]


def _resid_norm2(x, r0, g_kv, g_pre, name):
    def body(x_ref, r_ref, gk_ref, gp_ref, h_ref, nk_ref, np_ref):
        h = x_ref[...] + r_ref[...]
        h_ref[...] = h
        hn = h * _rstd(h)
        nk_ref[...] = (hn * gk_ref[...]).astype(BF16)
        np_ref[...] = (hn * gp_ref[...]).astype(BF16)

    d = x.shape[1]
    return _rowcall(body, name, x.shape[0], [(x, "row"), (r0, "row"), (g_kv, "full"), (g_pre, "full")],
                    [(d, F32, "row"), (d, BF16, "row"), (d, BF16, "row")])


def _post_norm(o, g, name, out_kind="row"):
    def body(o_ref, g_ref, r_ref):
        o = o_ref[...]
        r_ref[...] = o * _rstd(o) * g_ref[...]

    return _rowcall(body, name, o.shape[0], [(o, "row"), (g, "full")], [(o.shape[1], F32, out_kind)])[0]


def _post_norm_loss(o, g, h1, target, name):
    d = o.shape[1]

    def body(o_ref, g_ref, h_ref, t_ref, dh_ref, do_ref, acc_ref, dg_ref):
        o = o_ref[...]
        e = h_ref[...] + o * _rstd(o) * g_ref[...] - t_ref[...]
        dh = e * (1.0 / d)
        dh_ref[...] = dh
        acc_ref[...] += jnp.sum(e * e, axis=0, keepdims=True)
        dx, dg = _rms_bwd(o, g_ref[...], dh)
        do_ref[...] = dx.astype(BF16)
        dg_ref[...] += dg

    return _rowcall(body, name, o.shape[0], [(o, "row"), (g, "full"), (h1, "row"), (target, "row")],
                    [(d, F32, "row"), (d, BF16, "row"), (d, F32, "acc"), (d, F32, "acc")])


def _post_norm_bwd(dy, o, g, name, dy_kind="row"):
    def body(dy_ref, o_ref, g_ref, do_ref, dg_ref):
        dx, dg = _rms_bwd(o_ref[...], g_ref[...], dy_ref[...])
        do_ref[...] = dx.astype(BF16)
        dg_ref[...] += dg

    d = o.shape[1]
    return _rowcall(body, name, o.shape[0], [(dy, dy_kind), (o, "row"), (g, "full")], [(d, BF16, "row"), (d, F32, "acc")])


def _gate_mul(o, z, name):
    def body(o_ref, z_ref, r_ref):
        r_ref[...] = (o_ref[...] * _silu(z_ref[...])).astype(BF16)

    return _rowcall(body, name, o.shape[0], [(o, "row"), (z, "row")], [(o.shape[1], BF16, "row")])[0]


def _gate_bwd(d_oz, o, z, name):
    def body(d_ref, o_ref, z_ref, do_ref, dz_ref):
        _, vjp = jax.vjp(lambda o, z: o * _silu(z), o_ref[...], z_ref[...].astype(F32))
        do, dz = vjp(d_ref[...])
        do_ref[...] = do.astype(BF16)
        dz_ref[...] = dz.astype(BF16)

    w = o.shape[1]
    return _rowcall(body, name, o.shape[0], [(d_oz, "row"), (o, "row"), (z, "row")], [(w, BF16, "row"), (w, BF16, "right")])


def _norm_bwd2(dh2, h1, dxn1, dhn_kv, g_pre, g_kv, name):
    def body(dh2_ref, h_ref, d1_ref, dk_ref, gp_ref, gk_ref, dh1_ref, dgp_ref, dgk_ref):
        h = h_ref[...]
        dx1, dg1 = _rms_bwd(h, gp_ref[...], d1_ref[...])
        dxk, dgk = _rms_bwd(h, gk_ref[...], dk_ref[...])
        dh1_ref[...] = dh2_ref[...] + dx1 + dxk
        dgp_ref[...] += dg1
        dgk_ref[...] += dgk

    d = h1.shape[1]
    return _rowcall(body, name, h1.shape[0],
                    [(dh2, "row"), (h1, "row"), (dxn1, "row"), (dhn_kv, "row"), (g_pre, "full"), (g_kv, "full")],
                    [(d, F32, "row"), (d, F32, "acc"), (d, F32, "acc")])


def _norm_bwd1(dres, x, dxn, g, name):
    def body(dr_ref, x_ref, dn_ref, g_ref, dx_ref, dg_ref):
        dx, dg = _rms_bwd(x_ref[...], g_ref[...], dn_ref[...])
        dx_ref[...] = dr_ref[...] + dx
        dg_ref[...] += dg

    d = x.shape[1]
    return _rowcall(body, name, x.shape[0], [(dres, "nat"), (x, "nat"), (dxn, "row"), (g, "full")],
                    [(d, F32, "nat"), (d, F32, "acc")])


def _gelu_cast(y, name):
    def body(y_ref, o_ref):
        o_ref[...] = jax.nn.gelu(y_ref[...]).astype(BF16)

    return _rowcall(body, name, y.shape[0], [(y, "row")], [(y.shape[1], BF16, "row")])[0]


def _s5_gate(y_ssm, gp, b_glu, z, name):
    def body(y_ref, gp_ref, b_ref, z_ref, o_ref):
        yg = jax.nn.gelu(y_ref[...])
        o_ref[...] = (yg * jax.nn.sigmoid(gp_ref[...] + b_ref[...]) * _silu(z_ref[...].astype(F32))).astype(BF16)

    return _rowcall(body, name, y_ssm.shape[0], [(y_ssm, "row"), (gp, "row"), (b_glu, "full"), (z, "row")],
                    [(y_ssm.shape[1], BF16, "row")])[0]


def _s5_gate_bwd(dy3, y_ssm, gp, b_glu, z, name):
    def body(d_ref, y_ref, gp_ref, b_ref, z_ref, dz_ref, dgp_ref, dyg_ref, db_ref):
        yg = jax.nn.gelu(y_ref[...])
        _, vjp = jax.vjp(lambda yg, gp, z: yg * jax.nn.sigmoid(gp) * _silu(z), yg, gp_ref[...] + b_ref[...],
                         z_ref[...].astype(F32))
        dyg, dgp, dz = vjp(d_ref[...])
        dz_ref[...] = dz.astype(BF16)
        dgp_ref[...] = dgp.astype(BF16)
        dyg_ref[...] = dyg
        db_ref[...] += jnp.sum(dgp, axis=0, keepdims=True)

    w = y_ssm.shape[1]
    return _rowcall(body, name, y_ssm.shape[0],
                    [(dy3, "row"), (y_ssm, "row"), (gp, "row"), (b_glu, "full"), (z, "row")],
                    [(w, BF16, "right"), (w, BF16, "row"), (w, F32, "row"), (w, F32, "acc")])


def _gelu_bwd(dyg, y_ssm, name):
    def body(d_ref, y_ref, o_ref):
        _, vjp = jax.vjp(jax.nn.gelu, y_ref[...])
        o_ref[...] = vjp(d_ref[...])[0]

    return _rowcall(body, name, y_ssm.shape[0], [(dyg, "row"), (y_ssm, "row")], [(y_ssm.shape[1], F32, "row")])[0]


def _concat_cast(a, b, name):
    def body(a_ref, b_ref, o_ref):
        w = a_ref.shape[1]
        o_ref[:, :w] = a_ref[...].astype(BF16)
        o_ref[:, w:] = b_ref[...].astype(BF16)

    return _rowcall(body, name, a.shape[0], [(a, "row"), (b, "row")], [(a.shape[1] + b.shape[1], BF16, "row")])[0]


def _disc(ar, ai, ldt):
    dt = jnp.exp(ldt)
    mag = jnp.exp(ar * dt)
    abr = mag * jnp.cos(ai * dt)
    abi = mag * jnp.sin(ai * dt)
    den = ar * ar + ai * ai
    nr = abr - 1.0
    return abr, abi, (nr * ar + abi * ai) / den, (abi * ar - nr * ai) / den


def _s5_disc_fwd(a_re, a_im, ldt):
    def body(ar, ai, ld, o1, o2, o3, o4):
        o1[...], o2[...], o3[...], o4[...] = _disc(ar[...], ai[...], ld[...])

    sh = jax.ShapeDtypeStruct(a_re.shape, F32)
    return pl.pallas_call(body, name="s5_disc_fwd", out_shape=(sh, sh, sh, sh))(a_re, a_im, ldt)


def _s5_disc_bwd(a_re, a_im, ldt, d_abr, d_abi, d_cr, d_ci):
    def body(ar, ai, ld, g1, g2, g3, g4, o1, o2, o3):
        _, vjp = jax.vjp(_disc, ar[...], ai[...], ld[...])
        o1[...], o2[...], o3[...] = vjp((g1[...], g2[...], g3[...], g4[...]))

    sh = jax.ShapeDtypeStruct(a_re.shape, F32)
    return pl.pallas_call(body, name="s5_disc_bwd", out_shape=(sh, sh, jax.ShapeDtypeStruct(ldt.shape, F32)))(
        a_re, a_im, ldt, d_abr, d_abi, d_cr, d_ci)


def _bbar(cr, ci, br, bi):
    return cr * br - ci * bi, cr * bi + ci * br


def _s5_bbar_fwd(cr_col, ci_col, b_re, b_im):
    def body(cr, ci, br, bi, o1, o2):
        o1[...], o2[...] = _bbar(cr[...], ci[...], br[...], bi[...])

    w = b_re.shape[1]
    return _rowcall(body, "s5_bbar_fwd", b_re.shape[0], [(cr_col, "row"), (ci_col, "row"), (b_re, "row"), (b_im, "row")],
                    [(w, F32, "row"), (w, F32, "row")], tile_rows=1024)


def _s5_bbar_bwd(cr_col, ci_col, b_re, b_im, d_re, d_im):
    def body(cr, ci, br, bi, g1, g2, o1, o2, o3, o4):
        _, vjp = jax.vjp(_bbar, cr[...], ci[...], br[...], bi[...])
        o1[...], o2[...], o3[...], o4[...] = vjp((g1[...], g2[...]))

    w = b_re.shape[1]
    return _rowcall(body, "s5_bbar_bwd", b_re.shape[0],
                    [(cr_col, "row"), (ci_col, "row"), (b_re, "row"), (b_im, "row"), (d_re, "row"), (d_im, "row")],
                    [(1, F32, "row"), (1, F32, "row"), (w, F32, "row"), (w, F32, "row")], tile_rows=1024)


def _block_diag_in(t):
    g, p, c = t.shape
    nb = g // GROUPS_PER_BLOCK
    t4 = t.reshape(nb, GROUPS_PER_BLOCK, p, c).transpose(0, 1, 3, 2)
    eye = jnp.eye(GROUPS_PER_BLOCK, dtype=t.dtype)
    return (t4[:, :, :, None, :] * eye[None, :, None, :, None]).reshape(nb, GROUPS_PER_BLOCK * c, GROUPS_PER_BLOCK * p)


def _block_diag_in_extract(d, p, c):
    nb = d.shape[0]
    d5 = d.reshape(nb, GROUPS_PER_BLOCK, c, GROUPS_PER_BLOCK, p)
    diag = jnp.stack([d5[:, g, :, g, :] for g in range(GROUPS_PER_BLOCK)], axis=1)
    return diag.transpose(0, 1, 3, 2).reshape(nb * GROUPS_PER_BLOCK, p, c)


def _block_diag_out(t):
    g, c, p = t.shape
    nb = g // GROUPS_PER_BLOCK
    t4 = t.reshape(nb, GROUPS_PER_BLOCK, c, p).transpose(0, 1, 3, 2)
    eye = jnp.eye(GROUPS_PER_BLOCK, dtype=t.dtype)
    return (t4[:, :, :, None, :] * eye[None, :, None, :, None]).reshape(nb, GROUPS_PER_BLOCK * p, GROUPS_PER_BLOCK * c)


def _block_diag_out_extract(d, c, p):
    nb = d.shape[0]
    d5 = d.reshape(nb, GROUPS_PER_BLOCK, p, GROUPS_PER_BLOCK, c)
    diag = jnp.stack([d5[:, g, :, g, :] for g in range(GROUPS_PER_BLOCK)], axis=1)
    return diag.transpose(0, 1, 3, 2).reshape(nb * GROUPS_PER_BLOCK, c, p)


def _scan_step(ar, ai, hr, hi, xr, xi):
    return ar * hr - ai * hi + xr, ar * hi + ai * hr + xi


def _s5_scan_fwd(u, bd_re, bd_im, cd_re, cd_im, ab_re, ab_im, init_re, init_im, d_row, full, name):
    s, w = u.shape
    nb = w // LANES
    rows = _tile(s, 512, SUBLANES)
    nc = s // rows
    steps = rows // N_SEG
    ns = nb * BLOCK_STATE

    def body(u_ref, bdr, bdi, cdr, cdi, ar_ref, ai_ref, ir_ref, ii_ref, d_ref, *outs):
        if full:
            y_ref, yg_ref, hr_ref, hi_ref, er_ref, ei_ref, cr, ci = outs
        else:
            er_ref, ei_ref, hr_ref, hi_ref, cr, ci = outs
        c = pl.program_id(1)

        @pl.when(c == 0)
        def _():
            cr[...] = ir_ref[...]
            ci[...] = ii_ref[...]

        ub = u_ref[...].astype(BF16)
        hr_ref[...] = jnp.dot(ub, bdr[...], preferred_element_type=F32)
        hi_ref[...] = jnp.dot(ub, bdi[...], preferred_element_type=F32)
        ar, ai = ar_ref[...], ai_ref[...]

        hr, hi = cr[...], ci[...]
        for j in range(steps):
            rows_j = pl.ds(j * N_SEG, N_SEG)
            hr, hi = _scan_step(ar, ai, hr, hi, hr_ref[rows_j, :], hi_ref[rows_j, :])
            hr_ref[rows_j, :] = hr
            hi_ref[rows_j, :] = hi
        cr[...] = hr
        ci[...] = hi
        if full:
            y = (jnp.dot(hr_ref[...].astype(BF16), cdr[...], preferred_element_type=F32)
                 + jnp.dot(hi_ref[...].astype(BF16), cdi[...], preferred_element_type=F32)
                 + d_ref[...] * u_ref[...])
            y_ref[...] = y
            yg_ref[...] = jax.nn.gelu(y).astype(BF16)

        @pl.when(c == nc - 1)
        def _():
            er_ref[...] = hr
            ei_ref[...] = hi

    blk3 = lambda a: pl.BlockSpec((None,) + a.shape[1:], lambda k, c: (k, 0, 0))
    seg = pl.BlockSpec((N_SEG, BLOCK_STATE), lambda k, c: (0, k))
    st = pl.BlockSpec((rows, BLOCK_STATE), lambda k, c: (c, k))
    in_specs = [pl.BlockSpec((rows, LANES), lambda k, c: (c, k)), blk3(bd_re), blk3(bd_im), blk3(cd_re), blk3(cd_im),
                seg, seg, seg, seg, pl.BlockSpec((1, LANES), lambda k, c: (0, k))]
    seg_shape = jax.ShapeDtypeStruct((N_SEG, ns), F32)
    st_shape = jax.ShapeDtypeStruct((s, ns), F32)
    carry = [pltpu.VMEM((N_SEG, BLOCK_STATE), F32)] * 2
    if full:
        ych = pl.BlockSpec((rows, LANES), lambda k, c: (c, k))
        out_specs = [ych, ych, st, st, seg, seg]
        out_shape = [jax.ShapeDtypeStruct((s, w), F32), jax.ShapeDtypeStruct((s, w), BF16), st_shape, st_shape, seg_shape, seg_shape]
        scratch = carry
    else:
        out_specs = [seg, seg]
        out_shape = [seg_shape, seg_shape]
        scratch = [pltpu.VMEM((rows, BLOCK_STATE), F32)] * 2 + carry
    return pl.pallas_call(
        body, name=name, grid=(nb, nc), in_specs=in_specs, out_specs=out_specs, out_shape=out_shape,
        scratch_shapes=scratch, compiler_params=_cparams("parallel", "arbitrary"),
    )(u, bd_re, bd_im, cd_re, cd_im, ab_re, ab_im, init_re, init_im, d_row)


def _s5_seg_fix(e_re, e_im, ab_re, ab_im, seg_len, reverse, name):
    assert seg_len & (seg_len - 1) == 0

    def body(er, ei, ar, ai, o_re, o_im):
        pr, pi = ar[0:1, :], ai[0:1, :]
        for _ in range(int(math.log2(seg_len))):
            pr, pi = pr * pr - pi * pi, 2.0 * pr * pi
        tr = jnp.zeros_like(pr)
        ti = jnp.zeros_like(pr)
        order = list(range(N_SEG - 1, -1, -1)) if reverse else list(range(N_SEG))
        for n, sgm in enumerate(order):
            o_re[sgm:sgm + 1, :] = tr
            o_im[sgm:sgm + 1, :] = ti
            if n < N_SEG - 1:
                tr, ti = _scan_step(pr, pi, tr, ti, er[sgm:sgm + 1, :], ei[sgm:sgm + 1, :])

    sh = jax.ShapeDtypeStruct(e_re.shape, F32)
    return pl.pallas_call(body, name=name, out_shape=(sh, sh))(e_re, e_im, ab_re, ab_im)


def _s5_scan_bwd(dy, u, h_re, h_im, bd_re, bd_im, cd_re, cd_im, ab_re, ab_imn, gin_re, gin_im, d_row, full, name, duz=None):
    s, w = u.shape
    nb = w // LANES
    rows = _tile(s, 512, SUBLANES)
    nc = s // rows
    steps = rows // N_SEG
    ns = nb * BLOCK_STATE

    def body(dy_ref, u_ref, hr_ref, hi_ref, bdr, bdi, cdr, cdi, ar_ref, ai_ref, ir_ref, ii_ref, d_ref, *outs):
        if full:
            _, du_ref, dbr_ref, dbi_ref, dcr_ref, dci_ref, dar_ref, dai_ref, dd_ref, gr, gi, accr, acci = outs
        else:
            er_ref, ei_ref, gr, gi = outs
        c = pl.program_id(1)

        @pl.when(c == 0)
        def _():
            gr[pl.ds(rows, N_SEG), :] = ir_ref[...]
            gi[pl.ds(rows, N_SEG), :] = ii_ref[...]
            if full:
                for r in (dbr_ref, dbi_ref, dcr_ref, dci_ref, dd_ref, accr, acci):
                    r[...] = jnp.zeros_like(r)

        dyb = dy_ref[...].astype(BF16)
        nt = (_DOT_DIMS["nt"], ((), ()))
        tn = (_DOT_DIMS["tn"], ((), ()))
        gr[pl.ds(0, rows), :] = lax.dot_general(dyb, cdr[...], nt, preferred_element_type=F32)
        gi[pl.ds(0, rows), :] = lax.dot_general(dyb, cdi[...], nt, preferred_element_type=F32)
        ar, ai = ar_ref[...], ai_ref[...]

        g0r, g0i = gr[pl.ds(rows, N_SEG), :], gi[pl.ds(rows, N_SEG), :]
        for j in range(steps - 1, -1, -1):
            rows_j = pl.ds(j * N_SEG, N_SEG)
            g0r, g0i = _scan_step(ar, ai, g0r, g0i, gr[rows_j, :], gi[rows_j, :])
            gr[rows_j, :] = g0r
            gi[rows_j, :] = g0i
        if full:
            hr, hi = hr_ref[...], hi_ref[...]
            gnr, gni = gr[pl.ds(N_SEG, rows), :], gi[pl.ds(N_SEG, rows), :]
            accr[...] += jnp.sum((gnr * hr + gni * hi).reshape(steps, N_SEG, BLOCK_STATE), axis=0)
            acci[...] += jnp.sum((gni * hr - gnr * hi).reshape(steps, N_SEG, BLOCK_STATE), axis=0)
        gr[pl.ds(rows, N_SEG), :] = g0r
        gi[pl.ds(rows, N_SEG), :] = g0i
        if full:
            ub = u_ref[...].astype(BF16)
            gbr, gbi = gr[pl.ds(0, rows), :].astype(BF16), gi[pl.ds(0, rows), :].astype(BF16)
            dcr_ref[...] += lax.dot_general(hr.astype(BF16), dyb, tn, preferred_element_type=F32)
            dci_ref[...] += lax.dot_general(hi.astype(BF16), dyb, tn, preferred_element_type=F32)
            dbr_ref[...] += lax.dot_general(ub, gbr, tn, preferred_element_type=F32)
            dbi_ref[...] += lax.dot_general(ub, gbi, tn, preferred_element_type=F32)
            du_ref[...] = (lax.dot_general(gbr, bdr[...], nt, preferred_element_type=F32)
                           + lax.dot_general(gbi, bdi[...], nt, preferred_element_type=F32)
                           + d_ref[...] * dy_ref[...]).astype(BF16)
            dd_ref[...] += jnp.sum(dy_ref[...] * u_ref[...], axis=0, keepdims=True)

        @pl.when(c == nc - 1)
        def _():
            if full:
                dar_ref[...] = jnp.sum(accr[...], axis=0, keepdims=True)
                dai_ref[...] = jnp.sum(acci[...], axis=0, keepdims=True)
            else:
                er_ref[...] = g0r
                ei_ref[...] = g0i

    rev = lambda k, c: (nc - 1 - c, k)
    blk3 = lambda a: pl.BlockSpec((None,) + a.shape[1:], lambda k, c: (k, 0, 0))
    seg = pl.BlockSpec((N_SEG, BLOCK_STATE), lambda k, c: (0, k))
    st = pl.BlockSpec((rows, BLOCK_STATE), rev)
    ch = pl.BlockSpec((rows, LANES), rev)
    vec = pl.BlockSpec((1, LANES), lambda k, c: (0, k))
    if not full:
        st = pl.BlockSpec((rows, BLOCK_STATE), lambda k, c: (0, k))
    in_specs = [ch, ch if full else pl.BlockSpec((rows, LANES), lambda k, c: (0, k)), st, st,
                blk3(bd_re), blk3(bd_im), blk3(cd_re), blk3(cd_im), seg, seg, seg, seg, vec]
    args = [dy, u, h_re, h_im, bd_re, bd_im, cd_re, cd_im, ab_re, ab_imn, gin_re, gin_im, d_row]
    gbuf = [pltpu.VMEM((rows + N_SEG, BLOCK_STATE), F32)] * 2
    if full:
        row1 = pl.BlockSpec((1, BLOCK_STATE), lambda k, c: (0, k))
        out_specs = [ch, blk3(bd_re), blk3(bd_im), blk3(cd_re), blk3(cd_im), row1, row1, vec]
        out_shape = [jax.ShapeDtypeStruct(duz.shape, BF16),
                     jax.ShapeDtypeStruct(bd_re.shape, F32), jax.ShapeDtypeStruct(bd_im.shape, F32),
                     jax.ShapeDtypeStruct(cd_re.shape, F32), jax.ShapeDtypeStruct(cd_im.shape, F32),
                     jax.ShapeDtypeStruct((1, ns), F32), jax.ShapeDtypeStruct((1, ns), F32),
                     jax.ShapeDtypeStruct((1, w), F32)]
        scratch = gbuf + [pltpu.VMEM((N_SEG, BLOCK_STATE), F32)] * 2
        in_specs.append(pl.BlockSpec(memory_space=pl.ANY))
        args.append(duz)
        aliases = {len(args) - 1: 0}
    else:
        out_specs = [seg, seg]
        out_shape = [jax.ShapeDtypeStruct((N_SEG, ns), F32)] * 2
        scratch = gbuf
        aliases = {}
    return pl.pallas_call(
        body, name=name, grid=(nb, nc), in_specs=in_specs, out_specs=out_specs, out_shape=out_shape,
        input_output_aliases=aliases, scratch_shapes=scratch, compiler_params=_cparams("parallel", "arbitrary"),
    )(*args)


def _log_sigmoid(x):
    return jnp.minimum(x, 0.0) - jnp.log(1.0 + jnp.exp(-jnp.abs(x)))


def _tri(n, upper):
    r = lax.broadcasted_iota(jnp.int32, (n, n), 0)
    c = lax.broadcasted_iota(jnp.int32, (n, n), 1)
    return jnp.where((c >= r) if upper else (r >= c), 1.0, 0.0).astype(F32)


def _cum_fwd(f_logit, b_row, name):
    s, w = f_logit.shape
    t = _tile(s, 256, SUBLANES)

    def body(f_ref, b_ref, o_ref, carry):
        @pl.when(pl.program_id(0) == 0)
        def _():
            carry[...] = jnp.zeros_like(carry)

        lf = _log_sigmoid(f_ref[...] + b_ref[...])
        cum = jnp.dot(_tri(t, False), lf, precision=lax.Precision.HIGHEST, preferred_element_type=F32) + carry[...]
        o_ref[...] = cum * LOG2E
        carry[...] = cum[t - 1:t, :]

    return pl.pallas_call(
        body, name=name, grid=(s // t,),
        in_specs=[pl.BlockSpec((t, w), lambda i: (i, 0)), pl.BlockSpec((1, w), lambda i: (0, 0))],
        out_specs=pl.BlockSpec((t, w), lambda i: (i, 0)), out_shape=jax.ShapeDtypeStruct((s, w), F32),
        scratch_shapes=[pltpu.VMEM((1, w), F32)], compiler_params=_cparams("arbitrary"),
    )(f_logit, b_row)


def _cum_bwd(dcq, dck, f_logit, b_row, name):
    s, w = f_logit.shape
    t = _tile(s, 256, SUBLANES)
    nt = s // t

    def body(q_ref, k_ref, f_ref, b_ref, df_ref, db_ref, carry):
        @pl.when(pl.program_id(0) == 0)
        def _():
            carry[...] = jnp.zeros_like(carry)
            db_ref[...] = jnp.zeros_like(db_ref)

        dc = q_ref[...] - k_ref[...]
        rc = jnp.dot(_tri(t, True), dc, precision=lax.Precision.HIGHEST, preferred_element_type=F32) + carry[...]
        carry[...] = rc[0:1, :]
        df = rc * (1.0 - jax.nn.sigmoid(f_ref[...] + b_ref[...]))
        df_ref[...] = df.astype(BF16)
        db_ref[...] += jnp.sum(df, axis=0, keepdims=True)

    rev = pl.BlockSpec((t, w), lambda i: (nt - 1 - i, 0))
    one = pl.BlockSpec((1, w), lambda i: (0, 0))
    return pl.pallas_call(
        body, name=name, grid=(nt,), in_specs=[rev, rev, rev, one], out_specs=[rev, one],
        out_shape=[jax.ShapeDtypeStruct((s, w), BF16), jax.ShapeDtypeStruct((1, w), F32)],
        scratch_shapes=[pltpu.VMEM((1, w), F32)], compiler_params=_cparams("arbitrary"),
    )(dcq, dck, f_logit, b_row)


def _head_col(cum_tile, h):
    lane = lax.broadcasted_iota(jnp.int32, cum_tile.shape, 1)
    return jnp.sum(jnp.where(lane == h, cum_tile, 0.0), axis=1, keepdims=True)


def _attn_tiles(s):
    return _tile(s, 512, LANES)


def _exp2_rows(sc, sub):
    return jnp.concatenate([jnp.exp2(sc[:, b * LANES:(b + 1) * LANES] - sub) for b in range(sc.shape[1] // LANES)], axis=1)


def _row_of(rep):
    return jnp.transpose(rep)[0:1, :]


def _causal(sc, keys_on_rows):
    r = lax.broadcasted_iota(jnp.int32, sc.shape, 0)
    c = lax.broadcasted_iota(jnp.int32, sc.shape, 1)
    return jnp.where((r <= c) if keys_on_rows else (c <= r), sc, NEG_INF)


def _fox_fwd(q2, kv, cum2_t, z, name):
    s, w = q2.shape
    nh = w // HEAD_DIM
    tq = _attn_tiles(s)
    nq = s // tq
    nt = (_DOT_DIMS["nt"], ((), ()))

    def body(q_ref, k_ref, v_ref, ct_ref, z_ref, o_ref, oz_ref, lse_row_ref, m_s, acc_s, vaug, s_buf):
        i = pl.program_id(1)

        @pl.when(i == 0)
        def _():
            vaug[:, :HEAD_DIM] = v_ref[...]
            vaug[:, HEAD_DIM:] = jnp.ones((s, LANES), BF16)

        qb = q_ref[...]
        m_s[...] = jnp.full_like(m_s, NEG_INF)
        acc_s[...] = jnp.zeros_like(acc_s)

        def scores(j):
            off = pl.multiple_of(j * tq, tq)
            return lax.dot_general(qb, k_ref[pl.ds(off, tq), :], nt, preferred_element_type=F32) - ct_ref[:, pl.ds(off, tq)]

        def softmax_pv(j, sc):
            m_old = m_s[...]
            m_new = jnp.maximum(m_old, jnp.max(sc, axis=1, keepdims=True))
            p = _exp2_rows(sc, m_new)
            alpha = jnp.exp2(m_old - m_new)
            pv = jnp.dot(p.astype(BF16), vaug[pl.ds(pl.multiple_of(j * tq, tq), tq), :], preferred_element_type=F32)
            acc_s[...] = jnp.concatenate([alpha, alpha], axis=1) * acc_s[...] + pv
            m_s[...] = m_new

        s_buf[...] = scores(0)

        def loop(j, carry):
            nxt = scores(j + 1)
            softmax_pv(j, s_buf[...])
            s_buf[...] = nxt
            return carry

        lax.fori_loop(0, i, loop, 0)
        softmax_pv(i, _causal(s_buf[...], False))
        l = acc_s[:, HEAD_DIM:]
        o = acc_s[:, :HEAD_DIM] / l
        o_ref[...] = o
        oz_ref[...] = (o * _silu(z_ref[...].astype(F32))).astype(BF16)
        lse_row_ref[...] = _row_of(m_s[...] + jnp.log(l) * LOG2E)

    return pl.pallas_call(
        body, name=name, grid=(nh, nq),
        in_specs=[pl.BlockSpec((tq, HEAD_DIM), lambda h, i: (i, h)),
                  pl.BlockSpec((s, HEAD_DIM), lambda h, i: (0, h)),
                  pl.BlockSpec((s, HEAD_DIM), lambda h, i: (0, nh + h)),
                  pl.BlockSpec((None, 1, s), lambda h, i: (h, 0, 0)),
                  pl.BlockSpec((tq, HEAD_DIM), lambda h, i: (i, h))],
        out_specs=[pl.BlockSpec((tq, HEAD_DIM), lambda h, i: (i, h)),
                   pl.BlockSpec((tq, HEAD_DIM), lambda h, i: (i, h)),
                   pl.BlockSpec((None, 1, tq), lambda h, i: (h, 0, i))],
        out_shape=[jax.ShapeDtypeStruct((s, w), F32), jax.ShapeDtypeStruct((s, w), BF16),
                   jax.ShapeDtypeStruct((nh, 1, s), F32)],
        scratch_shapes=[pltpu.VMEM((tq, LANES), F32), pltpu.VMEM((tq, HEAD_DIM + LANES), F32),
                        pltpu.VMEM((s, HEAD_DIM + LANES), BF16), pltpu.VMEM((tq, tq), F32)],
        compiler_params=_cparams("arbitrary", "arbitrary"),
    )(q2, kv, kv, cum2_t, z)


def _fox_bwd(q2, kv, do, o, lse2_t, cum2, dqz, name):
    s, w = q2.shape
    nh = w // HEAD_DIM
    tk = _attn_tiles(s)
    nk = s // tk
    scale = HEAD_DIM ** -0.5
    nt = (_DOT_DIMS["nt"], ((), ()))
    tn = (_DOT_DIMS["tn"], ((), ()))

    def body(q_ref, k_ref, v_ref, do_ref, o_ref, lse_ref, c_ref, _, dk_ref, dv_ref, dq_ref, dcq_ref, dck_ref,
             dk_s, dv_s, dc_s, dq_s, dcq_s, dl_s, s_buf, dp_buf):
        h, j = pl.program_id(0), pl.program_id(1)

        @pl.when(j == 0)
        def _():
            dq_s[...] = jnp.zeros_like(dq_s)
            dcq_s[...] = jnp.zeros_like(dcq_s)
            for i in range(nk):
                rows = pl.ds(i * tk, tk)
                d = jnp.sum(do_ref[rows, :].astype(F32) * o_ref[rows, :], axis=1, keepdims=True)
                dl_s[:, i * tk:(i + 1) * tk] = _row_of(jnp.broadcast_to(d, (tk, LANES)))

        kb = k_ref[...]
        vb = v_ref[...]
        ck = jnp.broadcast_to(_head_col(c_ref[...], h), (tk, LANES))
        dk_s[...] = jnp.zeros_like(dk_s)
        dv_s[...] = jnp.zeros_like(dv_s)
        dc_s[...] = jnp.zeros_like(dc_s)

        def scores(i):
            off = pl.multiple_of(i * tk, tk)
            sc = lax.dot_general(kb, q_ref[pl.ds(off, tk), :], nt, preferred_element_type=F32) - lse_ref[:, pl.ds(off, tk)]
            dp = lax.dot_general(vb, do_ref[pl.ds(off, tk), :], nt, preferred_element_type=F32) - dl_s[:, pl.ds(off, tk)]
            return sc, dp

        def accumulate(i, sc, dp):
            off = pl.multiple_of(i * tk, tk)
            p = _exp2_rows(sc, ck)
            dv_s[...] += jnp.dot(p.astype(BF16), do_ref[pl.ds(off, tk), :], preferred_element_type=F32)
            ds = p * dp
            dsb = ds.astype(BF16)
            dk_s[...] += jnp.dot(dsb, q_ref[pl.ds(off, tk), :], preferred_element_type=F32)
            dq_s[pl.ds(off, tk), :] += lax.dot_general(dsb, kb, tn, preferred_element_type=F32)
            dcq_s[:, pl.ds(off, tk)] += jnp.sum(ds, axis=0, keepdims=True)
            part = ds[:, :LANES]
            for b in range(1, tk // LANES):
                part = part + ds[:, b * LANES:(b + 1) * LANES]
            dc_s[...] += part

        sc0, dp0 = scores(j)
        s_buf[...] = _causal(sc0, True)
        dp_buf[...] = dp0

        def loop(i, carry):
            nxt = scores(i + 1)
            accumulate(i, s_buf[...], dp_buf[...])
            s_buf[...], dp_buf[...] = nxt
            return carry

        lax.fori_loop(j, nk - 1, loop, 0)
        accumulate(nk - 1, s_buf[...], dp_buf[...])
        dk_ref[...] = (dk_s[...] * (1.0 / LOG2E)).astype(BF16)
        dv_ref[...] = dv_s[...].astype(BF16)
        dck_ref[...] = jnp.sum(jnp.transpose(dc_s[...]), axis=0, keepdims=True)

        @pl.when(j == nk - 1)
        def _():
            dq_ref[...] = (dq_s[...] * scale).astype(BF16)
            dcq_ref[...] = dcq_s[...]

    col = pl.BlockSpec((s, HEAD_DIM), lambda h, j: (0, h))
    row = pl.BlockSpec((None, 1, s), lambda h, j: (h, 0, 0))
    kspec = pl.BlockSpec((tk, HEAD_DIM), lambda h, j: (j, h))
    return pl.pallas_call(
        body, name=name, grid=(nh, nk),
        in_specs=[col, kspec, pl.BlockSpec((tk, HEAD_DIM), lambda h, j: (j, nh + h)), col, col, row,
                  pl.BlockSpec((tk, LANES), lambda h, j: (j, 0)), pl.BlockSpec(memory_space=pl.ANY)],
        out_specs=[kspec, kspec, col, row, pl.BlockSpec((None, 1, tk), lambda h, j: (h, 0, j))],
        out_shape=[jax.ShapeDtypeStruct((s, w), BF16), jax.ShapeDtypeStruct((s, w), BF16),
                   jax.ShapeDtypeStruct(dqz.shape, BF16), jax.ShapeDtypeStruct((nh, 1, s), F32),
                   jax.ShapeDtypeStruct((nh, 1, s), F32)],
        input_output_aliases={7: 2},
        scratch_shapes=[pltpu.VMEM((tk, HEAD_DIM), F32), pltpu.VMEM((tk, HEAD_DIM), F32), pltpu.VMEM((tk, LANES), F32),
                        pltpu.VMEM((s, HEAD_DIM), F32), pltpu.VMEM((1, s), F32), pltpu.VMEM((1, s), F32),
                        pltpu.VMEM((tk, tk), F32), pltpu.VMEM((tk, tk), F32)],
        compiler_params=_cparams("arbitrary", "arbitrary"),
    )(q2, kv, kv, do, o, lse2_t, cum2, dqz)


def _fox_bwd_dq(q2, kv, do, o, lse2, cum2_t, dqz, name):
    s, w = q2.shape
    nh = w // HEAD_DIM
    tq = _attn_tiles(s)
    nq = s // tq
    scale = HEAD_DIM ** -0.5
    nt = (_DOT_DIMS["nt"], ((), ()))

    def body(q_ref, k_ref, v_ref, do_ref, o_ref, lse_ref, ct_ref, _, dq_ref, dl_ref, dcq_ref, acc_s, dc_s):
        i = pl.program_id(1)
        qb = q_ref[...]
        dob = do_ref[...]
        lse = lse_ref[...]
        delta = jnp.broadcast_to(jnp.sum(dob.astype(F32) * o_ref[...], axis=1, keepdims=True), (tq, LANES))
        acc_s[...] = jnp.zeros_like(acc_s)
        dc_s[...] = jnp.zeros_like(dc_s)

        def tile(j, masked):
            off = pl.multiple_of(j * tq, tq)
            kb = k_ref[pl.ds(off, tq), :]
            sc = lax.dot_general(qb, kb, nt, preferred_element_type=F32) - ct_ref[:, pl.ds(off, tq)]
            if masked:
                sc = _causal(sc, False)
            p = _exp2_rows(sc, lse)
            dp = lax.dot_general(dob, v_ref[pl.ds(off, tq), :], nt, preferred_element_type=F32)
            ds = p * (dp - jnp.concatenate([delta] * (tq // LANES), axis=1))
            acc_s[...] += jnp.dot(ds.astype(BF16), kb, preferred_element_type=F32)
            part = ds[:, :LANES]
            for b in range(1, tq // LANES):
                part = part + ds[:, b * LANES:(b + 1) * LANES]
            dc_s[...] += part

        def loop(j, carry):
            tile(j, False)
            return carry

        lax.fori_loop(0, i, loop, 0)
        tile(i, True)
        dq_ref[...] = (acc_s[...] * scale).astype(BF16)
        dl_ref[...] = _row_of(delta)
        dcq_ref[...] = jnp.sum(jnp.transpose(dc_s[...]), axis=0, keepdims=True)

    qspec = pl.BlockSpec((tq, HEAD_DIM), lambda h, i: (i, h))
    rep = pl.BlockSpec((None, tq, LANES), lambda h, i: (h, i, 0))
    rowspec = pl.BlockSpec((None, 1, tq), lambda h, i: (h, 0, i))
    return pl.pallas_call(
        body, name=name, grid=(nh, nq),
        in_specs=[qspec,
                  pl.BlockSpec((s, HEAD_DIM), lambda h, i: (0, h)),
                  pl.BlockSpec((s, HEAD_DIM), lambda h, i: (0, nh + h)),
                  qspec, qspec, rep,
                  pl.BlockSpec((None, 1, s), lambda h, i: (h, 0, 0)),
                  pl.BlockSpec(memory_space=pl.ANY)],
        out_specs=[qspec, rowspec, rowspec],
        out_shape=[jax.ShapeDtypeStruct(dqz.shape, BF16), jax.ShapeDtypeStruct((nh, 1, s), F32),
                   jax.ShapeDtypeStruct((nh, 1, s), F32)],
        input_output_aliases={7: 0},
        scratch_shapes=[pltpu.VMEM((tq, HEAD_DIM), F32), pltpu.VMEM((tq, LANES), F32)],
        compiler_params=_cparams("parallel", "arbitrary"),
    )(q2, kv, kv, do, o, lse2, cum2_t, dqz)


def _fox_bwd_dkv(q2, kv, do, lse2_t, delta_t, cum2, name):
    s, w = q2.shape
    nh = w // HEAD_DIM
    tk = _attn_tiles(s)
    nk = s // tk
    nt = (_DOT_DIMS["nt"], ((), ()))

    def body(q_ref, k_ref, v_ref, do_ref, lse_ref, dl_ref, c_ref, dk_ref, dv_ref, dck_ref, dk_s, dv_s, dc_s, s_buf, dp_buf):
        h, j = pl.program_id(0), pl.program_id(1)
        kb = k_ref[...]
        vb = v_ref[...]
        ck = jnp.broadcast_to(_head_col(c_ref[...], h), (tk, LANES))
        dk_s[...] = jnp.zeros_like(dk_s)
        dv_s[...] = jnp.zeros_like(dv_s)
        dc_s[...] = jnp.zeros_like(dc_s)

        def scores(i):
            off = pl.multiple_of(i * tk, tk)
            sc = lax.dot_general(kb, q_ref[pl.ds(off, tk), :], nt, preferred_element_type=F32) - lse_ref[:, pl.ds(off, tk)]
            dp = lax.dot_general(vb, do_ref[pl.ds(off, tk), :], nt, preferred_element_type=F32) - dl_ref[:, pl.ds(off, tk)]
            return sc, dp

        def accumulate(i, sc, dp):
            off = pl.multiple_of(i * tk, tk)
            p = _exp2_rows(sc, ck)
            dv_s[...] += jnp.dot(p.astype(BF16), do_ref[pl.ds(off, tk), :], preferred_element_type=F32)
            ds = p * dp
            dk_s[...] += jnp.dot(ds.astype(BF16), q_ref[pl.ds(off, tk), :], preferred_element_type=F32)
            part = ds[:, :LANES]
            for b in range(1, tk // LANES):
                part = part + ds[:, b * LANES:(b + 1) * LANES]
            dc_s[...] += part

        sc0, dp0 = scores(j)
        s_buf[...] = _causal(sc0, True)
        dp_buf[...] = dp0

        def loop(i, carry):
            nxt = scores(i + 1)
            accumulate(i, s_buf[...], dp_buf[...])
            s_buf[...], dp_buf[...] = nxt
            return carry

        lax.fori_loop(j, nk - 1, loop, 0)
        accumulate(nk - 1, s_buf[...], dp_buf[...])
        dk_ref[...] = (dk_s[...] * (1.0 / LOG2E)).astype(BF16)
        dv_ref[...] = dv_s[...].astype(BF16)
        dck_ref[...] = jnp.sum(jnp.transpose(dc_s[...]), axis=0, keepdims=True)

    col = pl.BlockSpec((s, HEAD_DIM), lambda h, j: (0, h))
    row = pl.BlockSpec((None, 1, s), lambda h, j: (h, 0, 0))
    kspec = pl.BlockSpec((tk, HEAD_DIM), lambda h, j: (j, h))
    return pl.pallas_call(
        body, name=name, grid=(nh, nk),
        in_specs=[col, kspec, pl.BlockSpec((tk, HEAD_DIM), lambda h, j: (j, nh + h)), col, row, row,
                  pl.BlockSpec((tk, LANES), lambda h, j: (j, 0))],
        out_specs=[kspec, kspec, pl.BlockSpec((None, 1, tk), lambda h, j: (h, 0, j))],
        out_shape=[jax.ShapeDtypeStruct((s, w), BF16), jax.ShapeDtypeStruct((s, w), BF16),
                   jax.ShapeDtypeStruct((nh, 1, s), F32)],
        scratch_shapes=[pltpu.VMEM((tk, HEAD_DIM), F32), pltpu.VMEM((tk, HEAD_DIM), F32),
                        pltpu.VMEM((tk, LANES), F32), pltpu.VMEM((tk, tk), F32), pltpu.VMEM((tk, tk), F32)],
        compiler_params=_cparams("parallel", "arbitrary"),
    )(q2, kv, kv, do, lse2_t, delta_t, cum2)


def _exchange_copies(ins, outs, send_sems, recv_sems, local_sems, scatter):
    x, y, c = (lax.axis_index(a) for a in MESH_AXES)
    me = 4 * x + 2 * y + c
    local, remote = [], []
    for a in range(len(ins)):
        local.append(pltpu.make_async_copy(ins[a].at[me] if scatter else ins[a], outs[a].at[me], local_sems.at[a]))
        for k in range(1, N_DEV):
            px, py, pc = (1 - x if k & 4 else x), (1 - y if k & 2 else y), (1 - c if k & 1 else c)
            remote.append(pltpu.make_async_remote_copy(
                src_ref=ins[a].at[4 * px + 2 * py + pc] if scatter else ins[a], dst_ref=outs[a].at[me],
                send_sem=send_sems.at[a * (N_DEV - 1) + k - 1], recv_sem=recv_sems.at[a * (N_DEV - 1) + k - 1],
                device_id=(px, py, pc), device_id_type=pl.DeviceIdType.MESH))
    return local, remote


def _exchange_out_shapes(arrs, scatter):
    return [((N_DEV,) + a.shape[1:]) if scatter else ((N_DEV,) + a.shape) for a in arrs]


def _exchange(arrs, scatter, name):
    n = len(arrs)

    def body(*refs):
        local, remote = _exchange_copies(refs[:n], refs[n:2 * n], *refs[2 * n:], scatter)
        for cp in local + remote:
            cp.start()
        for cp in remote:
            cp.wait_send()
            cp.wait_recv()
        for cp in local:
            cp.wait()

    out_shape = [jax.ShapeDtypeStruct(s, a.dtype) for s, a in zip(_exchange_out_shapes(arrs, scatter), arrs)]
    return pl.pallas_call(
        body, name=name, out_shape=out_shape,
        in_specs=[pl.BlockSpec(memory_space=pl.ANY)] * n, out_specs=[pl.BlockSpec(memory_space=pl.ANY)] * n,
        scratch_shapes=[pltpu.SemaphoreType.DMA((n * (N_DEV - 1),)), pltpu.SemaphoreType.DMA((n * (N_DEV - 1),)),
                        pltpu.SemaphoreType.DMA((n,))],
    )(*arrs)


_HBM = pl.BlockSpec(memory_space=pltpu.HBM)
_SEM = pl.BlockSpec(memory_space=pltpu.SEMAPHORE)


def _exchange_start(arrs, scatter, name, after=()):
    n = len(arrs)
    after = list(after)
    lands = [lax.empty(s, a.dtype) for s, a in zip(_exchange_out_shapes(arrs, scatter), arrs)]

    def body(*refs):
        ins, outs = refs[:n], refs[n:2 * n]
        send_sems, recv_sems, local_sems = refs[2 * n + len(after):2 * n + len(after) + 3]
        token = refs[-1]
        local, remote = _exchange_copies(ins, outs, send_sems, recv_sems, local_sems, scatter)
        for cp in local + remote:
            cp.start()
        token[...] = jnp.zeros_like(token)

    hbm = lambda a: pltpu.HBM(a.shape, a.dtype)
    res = pl.pallas_call(
        body, name=name,
        out_shape=(pltpu.SemaphoreType.DMA((n * (N_DEV - 1),)), pltpu.SemaphoreType.DMA((n * (N_DEV - 1),)),
                   pltpu.SemaphoreType.DMA((n,)), *[hbm(a) for a in arrs], *[hbm(a) for a in lands],
                   jax.ShapeDtypeStruct((SUBLANES, LANES), F32)),
        in_specs=[_HBM] * (2 * n) + [pl.BlockSpec(memory_space=pl.ANY)] * len(after),
        out_specs=(_SEM, _SEM, _SEM, *[_HBM] * (2 * n), pl.BlockSpec(memory_space=pltpu.VMEM)),
        input_output_aliases={i: 3 + i for i in range(2 * n)},
        compiler_params=pltpu.CompilerParams(has_side_effects=pltpu.SideEffectType.DATAFLOW_SIDE_EFFECTING),
    )(*[pltpu.with_memory_space_constraint(a, pltpu.HBM) for a in list(arrs) + lands], *after)
    return (n, scatter, res[:3], res[3:3 + n], res[3 + n:3 + 2 * n]), res[-1]


def _exchange_wait(state, after, name):
    n, scatter, sems, srcs, lands = state
    after = list(after) if isinstance(after, (list, tuple)) else [after]

    def body(*refs):
        ins, outs = refs[:n], refs[n:2 * n]
        send_sems, recv_sems, local_sems = refs[2 * n:2 * n + 3]
        local, remote = _exchange_copies(ins, outs, send_sems, recv_sems, local_sems, scatter)
        for cp in remote:
            cp.wait_send()
            cp.wait_recv()
        for cp in local:
            cp.wait()

    hbm = lambda a: pltpu.HBM(a.shape, a.dtype)
    res = pl.pallas_call(
        body, name=name,
        out_shape=(*[hbm(a) for a in srcs], *[hbm(a) for a in lands]),
        in_specs=[_HBM] * (2 * n) + [_SEM] * 3 + [pl.BlockSpec(memory_space=pl.ANY)] * len(after),
        out_specs=tuple([_HBM] * (2 * n)),
        input_output_aliases={i: i for i in range(2 * n)},
        compiler_params=pltpu.CompilerParams(has_side_effects=pltpu.SideEffectType.DATAFLOW_SIDE_EFFECTING),
    )(*srcs, *lands, *sems, *after)
    return list(res[n:])


def _adamw_math(w, g, m, v):
    m = ADAM_B1 * m + (1.0 - ADAM_B1) * g
    v = ADAM_B2 * v + (1.0 - ADAM_B2) * (g * g)
    m_hat = m / (1.0 - ADAM_B1 ** ADAM_STEP)
    v_hat = v / (1.0 - ADAM_B2 ** ADAM_STEP)
    return -ADAM_LR * (m_hat / (jnp.sqrt(v_hat) + ADAM_EPS) + ADAM_WD * w), m, v


def _slot_sum(p_ref):
    g = p_ref[0].astype(F32)
    for d in range(1, p_ref.shape[0]):
        g = g + p_ref[d].astype(F32)
    return g


def _adamw_tile(r, c):
    return _tile(r, max(SUBLANES, (256 * 1024) // c // SUBLANES * SUBLANES), SUBLANES)


def _adamw(parts, w, m, v, name):
    r, c = w.shape[-2:]
    tr = _adamw_tile(r, c)

    def body(p_ref, w_ref, m_ref, v_ref, g_ref, d_ref, nm_ref, nv_ref):
        g = _slot_sum(p_ref)
        g_ref[...] = g
        d_ref[...], nm_ref[...], nv_ref[...] = _adamw_math(w_ref[...], g, m_ref[...], v_ref[...])

    if w.ndim == 3:
        blk = pl.BlockSpec((None, tr, c), lambda i: (0, i, 0))
    else:
        blk = pl.BlockSpec((tr, c), lambda i: (i, 0))
    sh = jax.ShapeDtypeStruct(w.shape, F32)
    return pl.pallas_call(
        body, name=name, grid=(r // tr,),
        in_specs=[pl.BlockSpec((parts.shape[0], tr, c), lambda i: (0, i, 0)), blk, blk, blk],
        out_specs=[blk] * 4, out_shape=[sh] * 4, compiler_params=_cparams("parallel"),
    )(parts, w, m, v)


def _sum_parts(parts, name):
    _, r, c = parts.shape
    tr = _adamw_tile(r, c)

    def body(p_ref, o_ref):
        o_ref[...] = _slot_sum(p_ref)

    return pl.pallas_call(
        body, name=name, grid=(r // tr,),
        in_specs=[pl.BlockSpec((parts.shape[0], tr, c), lambda i: (0, i, 0))],
        out_specs=pl.BlockSpec((tr, c), lambda i: (i, 0)), out_shape=jax.ShapeDtypeStruct((r, c), F32),
        compiler_params=_cparams("parallel"),
    )(parts)


def _perm(a):
    s, d = a.shape
    return a.reshape(N_SEG, s // N_SEG, d).transpose(1, 0, 2).reshape(s, d)


def _unperm(a):
    s, d = a.shape
    return a.reshape(s // N_SEG, N_SEG, d).transpose(1, 0, 2).reshape(s, d)


def _lane_pad(a, width=LANES):
    return jnp.pad(a, ((0, 0), (0, width - a.shape[1])))


def _local_step(x, target, norm_pre, norm_post, kv_norm, kv_b_f, a_re, a_im, log_dt, b_re, b_im, c_re, c_im, comm):
    s, d = x.shape
    g, p = a_re.shape
    w = g * S5_GROUP
    fw = d
    nh = fw // HEAD_DIM
    seg_len = s // N_SEG
    row = lambda v: v.reshape(1, -1)
    g_pre0, g_pre1, g_post0, g_post1, g_kv = row(norm_pre[0]), row(norm_pre[1]), row(norm_post[0]), row(norm_post[1]), row(kv_norm)

    ldt = log_dt.reshape(g, 1)
    abr, abi, cr, ci = _s5_disc_fwd(a_re, a_im, ldt)
    cr_col, ci_col = cr.reshape(g * p, 1), ci.reshape(g * p, 1)
    b_re2, b_im2 = b_re.reshape(g * p, S5_GROUP), b_im.reshape(g * p, S5_GROUP)
    bb_re, bb_im = _s5_bbar_fwd(cr_col, ci_col, b_re2, b_im2)
    bd_re = _block_diag_in(bb_re.reshape(g, p, S5_GROUP)).astype(BF16)
    bd_im = _block_diag_in(bb_im.reshape(g, p, S5_GROUP)).astype(BF16)
    cd_re = _block_diag_out(c_re).astype(BF16)
    cd_im = _block_diag_out(-c_im).astype(BF16)
    ab_re = jnp.broadcast_to(abr.reshape(1, g * p), (N_SEG, g * p))
    ab_im = jnp.broadcast_to(abi.reshape(1, g * p), (N_SEG, g * p))
    zero_seg = jnp.zeros((N_SEG, g * p), F32)

    xn0 = _norm_cast(x, g_pre0 + comm.token, "norm_pre0", x_kind="nat")
    w_in = comm.weight("s5_w_in", [xn0, bd_re, bd_im, cd_re, cd_im, ab_re, ab_im])
    d_row, bglu_row = row(comm.vector("s5_d")), row(comm.vector("s5_b_glu"))
    u = _mm(xn0, w_in, "nn", F32, "s5_in_u", b_cols=(0, w))
    z0 = _mm(xn0, w_in, "nn", BF16, "s5_in_z", b_cols=(w, w))
    e_re, e_im = _s5_scan_fwd(u, bd_re, bd_im, cd_re, cd_im, ab_re, ab_im, zero_seg, zero_seg, d_row, False, "s5_scan_ends")
    i_re, i_im = _s5_seg_fix(e_re, e_im, ab_re, ab_im, seg_len, False, "s5_seg_fix")
    y_ssm, yg, h_re, h_im, _, _ = _s5_scan_fwd(u, bd_re, bd_im, cd_re, cd_im, ab_re, ab_im, i_re, i_im, d_row, True, "s5_scan")
    w_glu, w_out = comm.weight("s5_w_glu", yg), comm.weight("s5_w_out", yg)
    gp = _mm(yg, w_glu, "nn", BF16, "s5_glu")
    y3 = _s5_gate(y_ssm, gp, bglu_row, z0, "s5_gate")
    w_kv, fw_in, fw_out = comm.weight("kv_w", y3), comm.weight("fox_w_in", y3), comm.weight("fox_w_out", y3)
    w_f = _lane_pad(w_kv[:, 2 * fw:])
    o0 = _mm(y3, w_out, "nn", F32, "s5_out")
    r0 = _post_norm(o0, g_post0, "norm_post0", out_kind="nat")

    h1, hn_kv, xn1 = _resid_norm2(x, r0, g_kv, g_pre1, "resid_norms")
    kv = _mm(hn_kv, w_kv, "nn", BF16, "kv_proj", b_cols=(0, 2 * fw))
    f_logit = _mm(hn_kv, w_f, "nn", F32, "f_proj")
    bf_row = _lane_pad(row(kv_b_f))
    cum2 = _cum_fwd(f_logit, bf_row, "cum_fwd")
    cum2_t = cum2[:, :nh].T.reshape(nh, 1, s)
    q2 = _mm(xn1, fw_in, "nn", BF16, "fox_q", scale=HEAD_DIM ** -0.5 * LOG2E, b_cols=(0, fw))
    z1 = _mm(xn1, fw_in, "nn", BF16, "fox_z", b_cols=(fw, fw))
    o, oz, lse2_t = _fox_fwd(q2, kv, cum2_t, z1, "fox_fwd")
    o1 = _mm(oz, fw_out, "nn", F32, "fox_out")
    dh2, do1, sq, dg_post1 = _post_norm_loss(o1, g_post1, h1, target, "norm_post1_loss")
    loss = 0.5 * jnp.sum(sq) / d

    d_fw_out = _mm(oz, do1, "tn", BF16, "fox_out_dw")
    d_oz = _mm(do1, fw_out, "nt", F32, "fox_out_dx")
    do, dqz = _gate_bwd(d_oz, o, z1, "fox_gate_bwd")
    dk, dv, dqz, dcq, dck = _fox_bwd(q2, kv, do, o, lse2_t, cum2, dqz, "fox_bwd")
    d_fw_in = _mm(xn1, dqz, "tn", BF16, "fox_in_dw", col_slots=True)
    dxn1 = _mm(dqz, fw_in, "nt", F32, "fox_in_dx")
    dcq_sl = _lane_pad(dcq.reshape(nh, s).T)
    dck_sl = _lane_pad(dck.reshape(nh, s).T)
    df, db_f = _cum_bwd(dcq_sl, dck_sl, f_logit, bf_row, "cum_bwd")
    dkv = _concat_cast(dk, dv, "fox_dkv")
    d_w_kvm = _mm(hn_kv, dkv, "tn", F32, "kv_dw")
    d_w_f = _mm(hn_kv, df, "tn", F32, "f_dw")
    dhn_f = _mm(df, w_f, "nt", F32, "f_dx")
    dhn_kv = _mm(dkv, w_kv, "nt", F32, "kv_dx", add=dhn_f, b_cols=(0, 2 * fw))
    d_w_kv = jnp.concatenate([d_w_kvm, d_w_f[:, :nh]], axis=1)
    tok = comm.send_grads(dict(fox_w_out=d_fw_out, fox_w_in=d_fw_in, kv_w=d_w_kv), "exchange_fox")
    dh1, dg_pre1, dg_kv = _norm_bwd2(dh2, h1, dxn1, dhn_kv, g_pre1, g_kv, "resid_norms_bwd")

    do0, dg_post0 = _post_norm_bwd(dh1, o0, g_post0 + tok[0, 0], "norm_post0_bwd", dy_kind="nat")
    d_w_out = _mm(y3, do0, "tn", BF16, "s5_out_dw")
    dy3 = _mm(do0, w_out, "nt", F32, "s5_out_dx")
    duz, dgp, dyg_direct, db_glu = _s5_gate_bwd(dy3, y_ssm, gp, bglu_row, z0, "s5_gate_bwd")
    d_w_glu = _mm(yg, dgp, "tn", BF16, "s5_glu_dw")
    dyg = _mm(dgp, w_glu, "nt", F32, "s5_glu_dx", add=dyg_direct)
    dy_ssm = _gelu_bwd(dyg, y_ssm, "s5_gelu_bwd")
    d_row = d_row + comm.send_grads(dict(s5_w_out=d_w_out, s5_w_glu=d_w_glu), "exchange_s5")[0, 0]
    ab_imn = -ab_im
    ge_re, ge_im = _s5_scan_bwd(dy_ssm, u, h_re, h_im, bd_re, bd_im, cd_re, cd_im, ab_re, ab_imn, zero_seg, zero_seg,
                                d_row, False, "s5_adj_ends")
    gi_re, gi_im = _s5_seg_fix(ge_re, ge_im, ab_re, ab_imn, seg_len, True, "s5_adj_fix")
    duz, dbd_re, dbd_im, dcd_re, dcd_im, dab_re, dab_im, dd = _s5_scan_bwd(
        dy_ssm, u, h_re, h_im, bd_re, bd_im, cd_re, cd_im, ab_re, ab_imn, gi_re, gi_im, d_row, True, "s5_adj", duz=duz)
    d_w_in = _mm(xn0, duz, "tn", BF16, "s5_in_dw", col_slots=True)
    tok = comm.send_grads(dict(s5_w_in=d_w_in), "exchange_s5_in")
    dxn0 = _mm(duz, w_in, "nt", F32, "s5_in_dx", after=tok)
    grad_x, dg_pre0 = _norm_bwd1(dh1, x, dxn0, g_pre0, "norm_pre0_bwd")

    dbb_re = _block_diag_in_extract(dbd_re, p, S5_GROUP).reshape(g * p, S5_GROUP)
    dbb_im = _block_diag_in_extract(dbd_im, p, S5_GROUP).reshape(g * p, S5_GROUP)
    dcr_col, dci_col, db_re, db_im = _s5_bbar_bwd(cr_col, ci_col, b_re2, b_im2, dbb_re, dbb_im)
    da_re, da_im, dldt = _s5_disc_bwd(a_re, a_im, ldt, dab_re.reshape(g, p), dab_im.reshape(g, p),
                                      dcr_col.reshape(g, p), dci_col.reshape(g, p))
    dc_re = _block_diag_out_extract(dcd_re, S5_GROUP, p)
    dc_im = -_block_diag_out_extract(dcd_im, S5_GROUP, p)

    small = dict(
        norm_pre=jnp.concatenate([dg_pre0, dg_pre1], axis=0), norm_post=jnp.concatenate([dg_post0, dg_post1], axis=0),
        s5_a_re=da_re, s5_a_im=da_im, s5_log_dt=dldt.reshape(g), s5_b_re=db_re.reshape(g, p, S5_GROUP),
        s5_b_im=db_im.reshape(g, p, S5_GROUP), s5_c_re=dc_re, s5_c_im=dc_im, s5_d=dd.reshape(-1),
        s5_b_glu=db_glu.reshape(-1), kv_norm=dg_kv.reshape(-1), kv_b_f=db_f[0, :nh])
    return loss, grad_x, small


_BIG = ("s5_w_in", "s5_w_glu", "s5_w_out", "kv_w", "fox_w_in", "fox_w_out")
_COL_SHARDED = ("s5_w_in", "kv_w", "fox_w_in")
_SMALL = ("norm_pre", "norm_post", "s5_a_re", "s5_a_im", "s5_log_dt", "s5_b_re", "s5_b_im", "s5_c_re", "s5_c_im",
          "s5_d", "s5_b_glu", "kv_norm", "kv_b_f")
_SMALL_SHARDED = ("s5_d", "s5_b_glu")
_PACK_QUANTUM = SUBLANES * LANES
_WEIGHTS = ('norm_pre', 'norm_post', 's5_w_in', 's5_a_re', 's5_a_im', 's5_log_dt', 's5_b_re', 's5_b_im', 's5_c_re', 's5_c_im',
            's5_d', 's5_w_glu', 's5_b_glu', 's5_w_out', 'kv_norm', 'kv_w', 'kv_b_f', 'fox_w_in', 'fox_w_out')


def _full_from_slots(name, slots):
    n, r, c = slots.shape
    if name in _COL_SHARDED:
        return slots.transpose(1, 0, 2).reshape(r, n * c)
    return slots.reshape(n * r, c)


def _slots_from_full(name, full):
    if name in _COL_SHARDED:
        r, nc = full.shape
        return full.reshape(r, N_DEV, nc // N_DEV).transpose(1, 0, 2)
    nr, c = full.shape
    return full.reshape(N_DEV, nr // N_DEV, c)


def _pack(vals):
    parts = []
    for v in vals:
        flat = v.reshape(-1)
        parts.append(jnp.pad(flat, (0, (-flat.shape[0]) % _PACK_QUANTUM)))
    total = sum(p.shape[0] for p in parts)
    parts.append(jnp.zeros(((-total) % (N_DEV * _PACK_QUANTUM),), F32))
    return jnp.concatenate(parts).reshape(-1, LANES)


def _unpack(packed, shapes):
    flat = packed.reshape(-1)
    out, off = [], 0
    for sh in shapes:
        n = math.prod(sh)
        out.append(flat[off:off + n].reshape(sh))
        off += n + (-n) % _PACK_QUANTUM
    return out


class _Comm:
    _GROUPS = (("s5_w_in",) + _SMALL_SHARDED, ("s5_w_glu", "s5_w_out"), ("kv_w", "fox_w_in", "fox_w_out"))

    def __init__(self, shards, vectors, early=()):
        shards = {**shards, **vectors}
        self._full, self._gathers = {}, {}
        self._early = list(early)
        self.token = jnp.zeros((), F32)
        for group in self._GROUPS:
            state, tok = _exchange_start([shards[n] for n in group], False, "gather_start_" + group[0])
            self._gathers[group] = state
            self.token = self.token + tok[0, 0]
        self._sent = []

    def vector(self, name):
        return self._full[name]

    def weight(self, name, after):
        if name not in self._full:
            group = next(g for g in self._GROUPS if name in g)
            if group == self._GROUPS[0]:
                after = (list(after) if isinstance(after, (list, tuple)) else [after]) + self._early
            slots = _exchange_wait(self._gathers.pop(group), after, "gather_wait_" + group[0])
            for n, sl in zip(group, slots):
                self._full[n] = sl.reshape(-1) if n in _SMALL_SHARDED else _full_from_slots(n, sl)
        return self._full[name]

    def send_grads(self, grads, name):
        names = list(grads)
        slots = [grads[n] if grads[n].ndim == 3 else _slots_from_full(n, grads[n]).astype(BF16) for n in names]
        state, tok = _exchange_start(slots, True, name + "_start")
        self._sent.append((names, state, name + "_wait"))
        return tok

    def received_grads(self, after):
        for names, state, name in self._sent:
            for n, recv in zip(names, _exchange_wait(state, after, name)):
                yield n, recv


def kernel(x, norm_pre, norm_post, s5_w_in, s5_a_re, s5_a_im, s5_log_dt, s5_b_re, s5_b_im, s5_c_re, s5_c_im, s5_d, s5_w_glu, s5_b_glu, s5_w_out, kv_norm, kv_w, kv_b_f, fox_w_in, fox_w_out, loss_target, m_norm_pre, m_norm_post, m_s5_w_in, m_s5_a_re, m_s5_a_im, m_s5_log_dt, m_s5_b_re, m_s5_b_im, m_s5_c_re, m_s5_c_im, m_s5_d, m_s5_w_glu, m_s5_b_glu, m_s5_w_out, m_kv_norm, m_kv_w, m_kv_b_f, m_fox_w_in, m_fox_w_out, v_norm_pre, v_norm_post, v_s5_w_in, v_s5_a_re, v_s5_a_im, v_s5_log_dt, v_s5_b_re, v_s5_b_im, v_s5_c_re, v_s5_c_im, v_s5_d, v_s5_w_glu, v_s5_b_glu, v_s5_w_out, v_kv_norm, v_kv_w, v_kv_b_f, v_fox_w_in, v_fox_w_out):
    env = dict(locals())
    wts = {n: env[n] for n in _WEIGHTS}
    mom = {n: env["m_" + n] for n in _WEIGHTS}
    var = {n: env["v_" + n] for n in _WEIGHTS}
    me = 4 * lax.axis_index("x") + 2 * lax.axis_index("y") + lax.axis_index("c")
    shard2d = {n: wts[n].reshape(wts[n].shape[-2:]) for n in _BIG}
    full_shape = {n: ((wts[n].size * N_DEV,) if n in _SMALL_SHARDED else wts[n].shape) for n in _SMALL}

    def spread(n, v):
        if n not in _SMALL_SHARDED:
            return v
        flat = v.reshape(-1)
        return lax.dynamic_update_slice(jnp.zeros(full_shape[n], F32), flat, (me * flat.shape[0],))

    packed = [_pack([spread(n, src[n]) for n in _SMALL] + [jnp.zeros((1,), F32)]) for src in (wts, mom, var)]
    comm = _Comm({n: shard2d[n].astype(BF16) for n in _BIG}, {n: wts[n].reshape(1, -1) for n in _SMALL_SHARDED}, packed)

    loss_local, grad_x, small = _local_step(
        x[0], loss_target[0], norm_pre, norm_post, kv_norm, kv_b_f, s5_a_re[0], s5_a_im[0], s5_log_dt[0],
        s5_b_re[0], s5_b_im[0], s5_c_re[0], s5_c_im[0], comm)

    small_pack = _pack([small[n] for n in _SMALL] + [loss_local.reshape(1)])
    slice_rows = small_pack.shape[0] // N_DEV
    small_state, small_tok = _exchange_start([small_pack.reshape(N_DEV, slice_rows, LANES)], True, "reduce_small_start")

    res = {}
    for n, recv in comm.received_grads([small_tok, grad_x]):
        res[n] = _adamw(recv, wts[n], mom[n], var[n], "adamw_" + n)

    my_sum = _sum_parts(_exchange_wait(small_state, res[_BIG[0]][0], "reduce_small_wait")[0], "sum_small")
    g_all = _exchange([my_sum], False, "gather_small")[0].reshape(1, small_pack.shape[0], LANES)
    outs = _adamw(g_all, *packed, "adamw_small")
    unpacked = [_unpack(o, [full_shape[n] for n in _SMALL] + [(1,)]) for o in outs]
    loss = unpacked[0][-1][0]
    for i, n in enumerate(_SMALL):
        vals = [u[i] for u in unpacked]
        if n in _SMALL_SHARDED:
            k = wts[n].size
            vals = [lax.dynamic_slice(v, (me * k,), (k,)) for v in vals]
        res[n] = [v.reshape(wts[n].shape) for v in vals]

    return (loss, grad_x[None], *[res[n][0] for n in _WEIGHTS], *[res[n][1] for n in _WEIGHTS],
            *[res[n][2] for n in _WEIGHTS], *[res[n][3] for n in _WEIGHTS])
```

```python
import functools
import math

import jax
import jax.numpy as jnp
from jax import lax
from jax.experimental import pallas as pl
from jax.experimental.pallas import tpu as pltpu

F32 = jnp.float32
BF16 = jnp.bfloat16

N_DEV = 8
MESH_AXES = ("x", "y", "c")
S5_GROUP = 16
S5_STATE = 64
LANES = 128
SUBLANES = 8
GROUPS_PER_BLOCK = LANES // S5_GROUP
BLOCK_STATE = GROUPS_PER_BLOCK * S5_STATE
N_SEG = SUBLANES
HEAD_DIM = 128
RMS_EPS = 1e-6
NEG_INF = -1e30
LOG2E = math.log2(math.e)
ADAM_LR = 0.001
ADAM_B1 = 0.9
ADAM_B2 = 0.999
ADAM_EPS = 1e-08
ADAM_WD = 0.01
ADAM_STEP = 10
VMEM_LIMIT = 56 * 1024 * 1024


def _tile(n, pref, quantum=LANES):
    if n <= pref:
        return n
    t = (pref // quantum) * quantum
    while t >= quantum:
        if n % t == 0:
            return t
        t -= quantum
    return n


def _cparams(*sem):
    return pltpu.CompilerParams(dimension_semantics=sem if sem else None, vmem_limit_bytes=VMEM_LIMIT)


_DOT_DIMS = {"nn": ((1,), (0,)), "nt": ((1,), (1,)), "tn": ((0,), (0,))}


def _mm(a, b, mode, out_dtype, name, add=None, scale=None, b_cols=None, after=None, col_slots=False, b_slots=False):
    slot_w = b.shape[2] if b_slots else None
    b2d = (b.shape[1], b.shape[0] * b.shape[2]) if b_slots else b.shape
    b_shape = b2d if b_cols is None else (b2d[0], b_cols[1])
    if mode == "nn":
        (M, K), (K2, N) = a.shape, b_shape
    elif mode == "nt":
        (M, K), (N, K2) = a.shape, b_shape
    else:
        (K, M), (K2, N) = a.shape, b_shape
    assert K == K2, (name, a.shape, b_shape)
    tm, tn, tk = _tile(M, 1024 if K <= 2048 else 512), (N // N_DEV if col_slots else _tile(N, 1024)), _tile(K, 4096)
    if b_slots and mode == "nn":
        tn = slot_w
    nk = K // tk
    dims = (_DOT_DIMS[mode], ((), ()))
    col0 = 0
    if b_cols is not None:
        assert mode != "tn" and b_cols[0] % (tn if mode == "nn" else tk) == 0
        col0 = b_cols[0] // (tn if mode == "nn" else tk)
    assert not b_slots or (mode == "nn" or (mode == "nt" and nk == 1 and b_cols is None))

    def body(*refs):
        a_ref, b_ref = refs[:2]
        c_ref = refs[2] if add is not None else None
        o_ref = refs[2 + (add is not None) + (after is not None)]
        if b_slots and mode == "nt":
            part = lax.dot_general(a_ref[:, :slot_w], b_ref[0], dims, preferred_element_type=F32)
            for sl in range(1, b_ref.shape[0]):
                part += lax.dot_general(a_ref[:, sl * slot_w:(sl + 1) * slot_w], b_ref[sl], dims, preferred_element_type=F32)
        else:
            part = lax.dot_general(a_ref[...], b_ref[...], dims, preferred_element_type=F32)

        def finish(r):
            if scale is not None:
                r = r * scale
            if add is not None:
                r = r + c_ref[...]
            o_ref[...] = r.astype(out_dtype)

        if nk == 1:
            finish(part)
            return
        acc = refs[-1]
        k = pl.program_id(2)

        @pl.when(k == 0)
        def _():
            acc[...] = part

        @pl.when(jnp.logical_and(k > 0, k < nk - 1))
        def _():
            acc[...] += part

        @pl.when(k == nk - 1)
        def _():
            finish(acc[...] + part)

    if mode == "tn":
        a_spec = pl.BlockSpec((tk, tm), lambda i, j, k: (k, i))
    else:
        a_spec = pl.BlockSpec((tm, tk), lambda i, j, k: (i, k))
    if b_slots and mode == "nn":
        b_spec = pl.BlockSpec((None, tk, tn), lambda i, j, k: (j + col0, k, 0))
    elif b_slots:
        b_spec = pl.BlockSpec((b.shape[0], tn, slot_w), lambda i, j, k: (0, j, 0))
    elif mode == "nt":
        b_spec = pl.BlockSpec((tn, tk), lambda i, j, k: (j, k + col0))
    else:
        b_spec = pl.BlockSpec((tk, tn), lambda i, j, k: (k, j + col0))
    o_spec = pl.BlockSpec((tm, tn), lambda i, j, k: (i, j))
    in_specs = [a_spec, b_spec] + ([o_spec] if add is not None else [])
    args = (a, b) + ((add,) if add is not None else ())
    if after is not None:
        in_specs.append(pl.BlockSpec(after.shape, lambda i, j, k: (0, 0)))
        args += (after,)
    out_shape = jax.ShapeDtypeStruct((M, N), out_dtype)
    if col_slots:
        assert add is None
        o_spec = pl.BlockSpec((None, tm, tn), lambda i, j, k: (j, i, 0))
        out_shape = jax.ShapeDtypeStruct((N_DEV, M, tn), out_dtype)
    return pl.pallas_call(
        body, name=name, grid=(M // tm, N // tn, nk),
        in_specs=in_specs, out_specs=o_spec,
        out_shape=out_shape,
        scratch_shapes=[pltpu.VMEM((tm, tn), F32)] if nk > 1 else [],
        compiler_params=_cparams("parallel", "parallel", "arbitrary"),
    )(*args)


class _NatIn:
    def __init__(self, ref):
        self.ref = ref

    def __getitem__(self, idx):
        v = jnp.swapaxes(self.ref[...], 0, 1)
        return v.reshape(v.shape[0] * N_SEG, v.shape[2])


class _NatOut:
    def __init__(self, ref):
        self.ref = ref

    def __setitem__(self, idx, val):
        self.ref[...] = jnp.swapaxes(val.reshape(val.shape[0] // N_SEG, N_SEG, val.shape[1]), 0, 1)


def _rowcall(body, name, n_rows, ins, outs, tile_rows=256):
    tr = _tile(n_rows, tile_rows, SUBLANES * 2)
    n_in = len(ins)
    in_kinds = [k for _, k in ins]
    kinds = [k for _, _, k in outs]

    def kern(*refs):
        @pl.when(pl.program_id(0) == 0)
        def _():
            for r, kind in zip(refs[n_in:], kinds):
                if kind == "acc":
                    r[...] = jnp.zeros_like(r)

        wrapped = [_NatIn(r) if k == "nat" else r for r, k in zip(refs[:n_in], in_kinds)]
        wrapped += [_NatOut(r) if k == "nat" else r for r, k in zip(refs[n_in:], kinds)]
        body(*wrapped)

    in_specs, args = [], []
    for arr, kind in ins:
        if kind == "row":
            in_specs.append(pl.BlockSpec((tr, arr.shape[1]), lambda i: (i, 0)))
        elif kind == "nat":
            in_specs.append(pl.BlockSpec((N_SEG, tr // N_SEG, arr.shape[1]), lambda i: (0, i, 0)))
            arr = arr.reshape(N_SEG, n_rows // N_SEG, arr.shape[1])
        else:
            in_specs.append(pl.BlockSpec(arr.shape, lambda i, nd=arr.ndim: (0,) * nd))
        args.append(arr)
    out_specs, out_shape = [], []
    for width, dtype, kind in outs:
        if kind == "row":
            out_specs.append(pl.BlockSpec((tr, width), lambda i: (i, 0)))
            out_shape.append(jax.ShapeDtypeStruct((n_rows, width), dtype))
        elif kind == "right":
            out_specs.append(pl.BlockSpec((tr, width), lambda i: (i, 1)))
            out_shape.append(jax.ShapeDtypeStruct((n_rows, 2 * width), dtype))
        elif kind == "nat":
            out_specs.append(pl.BlockSpec((N_SEG, tr // N_SEG, width), lambda i: (0, i, 0)))
            out_shape.append(jax.ShapeDtypeStruct((N_SEG, n_rows // N_SEG, width), dtype))
        else:
            out_specs.append(pl.BlockSpec((1, width), lambda i: (0, 0)))
            out_shape.append(jax.ShapeDtypeStruct((1, width), F32))
    res = pl.pallas_call(
        kern, name=name, grid=(n_rows // tr,), in_specs=in_specs, out_specs=out_specs, out_shape=out_shape,
        compiler_params=_cparams("arbitrary"),
    )(*args)
    return [r.reshape(n_rows, r.shape[2]) if k == "nat" else r for r, k in zip(res, kinds)]


def _rstd(x):
    return lax.rsqrt(jnp.mean(x * x, axis=-1, keepdims=True) + RMS_EPS)


def _rms_bwd(x, g, dy):
    xh = x * _rstd(x)
    dxh = dy * g
    dx = _rstd(x) * (dxh - xh * jnp.mean(dxh * xh, axis=-1, keepdims=True))
    return dx, jnp.sum(dy * xh, axis=0, keepdims=True)


def _silu(z):
    return z * jax.nn.sigmoid(z)


def _norm_cast(x, g, name, x_kind="row"):
    def body(x_ref, g_ref, o_ref):
        x = x_ref[...]
        o_ref[...] = (x * _rstd(x) * g_ref[...]).astype(BF16)

    return _rowcall(body, name, x.shape[0], [(x, x_kind), (g, "full")], [(x.shape[1], BF16, "row")])[0]


def _resid_norm2(x, r0, g_kv, g_pre, name):
    def body(x_ref, r_ref, gk_ref, gp_ref, h_ref, nk_ref, np_ref):
        h = x_ref[...] + r_ref[...]
        h_ref[...] = h
        hn = h * _rstd(h)
        nk_ref[...] = (hn * gk_ref[...]).astype(BF16)
        np_ref[...] = (hn * gp_ref[...]).astype(BF16)

    d = x.shape[1]
    return _rowcall(body, name, x.shape[0], [(x, "row"), (r0, "row"), (g_kv, "full"), (g_pre, "full")],
                    [(d, F32, "row"), (d, BF16, "row"), (d, BF16, "row")])


def _post_norm(o, g, name, out_kind="row"):
    def body(o_ref, g_ref, r_ref):
        o = o_ref[...]
        r_ref[...] = o * _rstd(o) * g_ref[...]

    return _rowcall(body, name, o.shape[0], [(o, "row"), (g, "full")], [(o.shape[1], F32, out_kind)])[0]


def _post_norm_loss(o, g, h1, target, name):
    d = o.shape[1]

    def body(o_ref, g_ref, h_ref, t_ref, dh_ref, do_ref, acc_ref, dg_ref):
        o = o_ref[...]
        e = h_ref[...] + o * _rstd(o) * g_ref[...] - t_ref[...]
        dh = e * (1.0 / d)
        dh_ref[...] = dh
        acc_ref[...] += jnp.sum(e * e, axis=0, keepdims=True)
        dx, dg = _rms_bwd(o, g_ref[...], dh)
        do_ref[...] = dx.astype(BF16)
        dg_ref[...] += dg

    return _rowcall(body, name, o.shape[0], [(o, "row"), (g, "full"), (h1, "row"), (target, "row")],
                    [(d, F32, "row"), (d, BF16, "row"), (d, F32, "acc"), (d, F32, "acc")])


def _post_norm_bwd(dy, o, g, name, dy_kind="row"):
    def body(dy_ref, o_ref, g_ref, do_ref, dg_ref):
        dx, dg = _rms_bwd(o_ref[...], g_ref[...], dy_ref[...])
        do_ref[...] = dx.astype(BF16)
        dg_ref[...] += dg

    d = o.shape[1]
    return _rowcall(body, name, o.shape[0], [(dy, dy_kind), (o, "row"), (g, "full")], [(d, BF16, "row"), (d, F32, "acc")])


def _gate_mul(o, z, name):
    def body(o_ref, z_ref, r_ref):
        r_ref[...] = (o_ref[...] * _silu(z_ref[...])).astype(BF16)

    return _rowcall(body, name, o.shape[0], [(o, "row"), (z, "row")], [(o.shape[1], BF16, "row")])[0]


def _gate_bwd(d_oz, o, z, name):
    def body(d_ref, o_ref, z_ref, do_ref, dz_ref):
        _, vjp = jax.vjp(lambda o, z: o * _silu(z), o_ref[...], z_ref[...].astype(F32))
        do, dz = vjp(d_ref[...])
        do_ref[...] = do.astype(BF16)
        dz_ref[...] = dz.astype(BF16)

    w = o.shape[1]
    return _rowcall(body, name, o.shape[0], [(d_oz, "row"), (o, "row"), (z, "row")], [(w, BF16, "row"), (w, BF16, "right")])


def _norm_bwd2(dh2, h1, dxn1, dhn_kv, g_pre, g_kv, name):
    def body(dh2_ref, h_ref, d1_ref, dk_ref, gp_ref, gk_ref, dh1_ref, dgp_ref, dgk_ref):
        h = h_ref[...]
        dx1, dg1 = _rms_bwd(h, gp_ref[...], d1_ref[...])
        dxk, dgk = _rms_bwd(h, gk_ref[...], dk_ref[...])
        dh1_ref[...] = dh2_ref[...] + dx1 + dxk
        dgp_ref[...] += dg1
        dgk_ref[...] += dgk

    d = h1.shape[1]
    return _rowcall(body, name, h1.shape[0],
                    [(dh2, "row"), (h1, "row"), (dxn1, "row"), (dhn_kv, "row"), (g_pre, "full"), (g_kv, "full")],
                    [(d, F32, "row"), (d, F32, "acc"), (d, F32, "acc")])


def _norm_bwd1(dres, x, dxn, g, name):
    def body(dr_ref, x_ref, dn_ref, g_ref, dx_ref, dg_ref):
        dx, dg = _rms_bwd(x_ref[...], g_ref[...], dn_ref[...])
        dx_ref[...] = dr_ref[...] + dx
        dg_ref[...] += dg

    d = x.shape[1]
    return _rowcall(body, name, x.shape[0], [(dres, "nat"), (x, "nat"), (dxn, "row"), (g, "full")],
                    [(d, F32, "nat"), (d, F32, "acc")])


def _gelu_cast(y, name):
    def body(y_ref, o_ref):
        o_ref[...] = jax.nn.gelu(y_ref[...]).astype(BF16)

    return _rowcall(body, name, y.shape[0], [(y, "row")], [(y.shape[1], BF16, "row")])[0]


def _s5_gate(y_ssm, gp, b_glu, z, name):
    def body(y_ref, gp_ref, b_ref, z_ref, o_ref):
        yg = jax.nn.gelu(y_ref[...])
        o_ref[...] = (yg * jax.nn.sigmoid(gp_ref[...] + b_ref[...]) * _silu(z_ref[...].astype(F32))).astype(BF16)

    return _rowcall(body, name, y_ssm.shape[0], [(y_ssm, "row"), (gp, "row"), (b_glu, "full"), (z, "row")],
                    [(y_ssm.shape[1], BF16, "row")])[0]


def _s5_gate_bwd(dy3, y_ssm, gp, b_glu, z, name):
    def body(d_ref, y_ref, gp_ref, b_ref, z_ref, dz_ref, dgp_ref, dyg_ref, db_ref):
        yg = jax.nn.gelu(y_ref[...])
        _, vjp = jax.vjp(lambda yg, gp, z: yg * jax.nn.sigmoid(gp) * _silu(z), yg, gp_ref[...] + b_ref[...],
                         z_ref[...].astype(F32))
        dyg, dgp, dz = vjp(d_ref[...])
        dz_ref[...] = dz.astype(BF16)
        dgp_ref[...] = dgp.astype(BF16)
        dyg_ref[...] = dyg
        db_ref[...] += jnp.sum(dgp, axis=0, keepdims=True)

    w = y_ssm.shape[1]
    return _rowcall(body, name, y_ssm.shape[0],
                    [(dy3, "row"), (y_ssm, "row"), (gp, "row"), (b_glu, "full"), (z, "row")],
                    [(w, BF16, "right"), (w, BF16, "row"), (w, F32, "row"), (w, F32, "acc")])


def _gelu_bwd(dyg, y_ssm, name):
    def body(d_ref, y_ref, o_ref):
        _, vjp = jax.vjp(jax.nn.gelu, y_ref[...])
        o_ref[...] = vjp(d_ref[...])[0]

    return _rowcall(body, name, y_ssm.shape[0], [(dyg, "row"), (y_ssm, "row")], [(y_ssm.shape[1], F32, "row")])[0]


def _concat_cast(a, b, name):
    def body(a_ref, b_ref, o_ref):
        w = a_ref.shape[1]
        o_ref[:, :w] = a_ref[...].astype(BF16)
        o_ref[:, w:] = b_ref[...].astype(BF16)

    return _rowcall(body, name, a.shape[0], [(a, "row"), (b, "row")], [(a.shape[1] + b.shape[1], BF16, "row")])[0]


def _disc(ar, ai, ldt):
    dt = jnp.exp(ldt)
    mag = jnp.exp(ar * dt)
    abr = mag * jnp.cos(ai * dt)
    abi = mag * jnp.sin(ai * dt)
    den = ar * ar + ai * ai
    nr = abr - 1.0
    return abr, abi, (nr * ar + abi * ai) / den, (abi * ar - nr * ai) / den


def _s5_disc_fwd(a_re, a_im, ldt):
    def body(ar, ai, ld, o1, o2, o3, o4):
        o1[...], o2[...], o3[...], o4[...] = _disc(ar[...], ai[...], ld[...])

    sh = jax.ShapeDtypeStruct(a_re.shape, F32)
    return pl.pallas_call(body, name="s5_disc_fwd", out_shape=(sh, sh, sh, sh))(a_re, a_im, ldt)


def _s5_disc_bwd(a_re, a_im, ldt, d_abr, d_abi, d_cr, d_ci):
    def body(ar, ai, ld, g1, g2, g3, g4, o1, o2, o3):
        _, vjp = jax.vjp(_disc, ar[...], ai[...], ld[...])
        o1[...], o2[...], o3[...] = vjp((g1[...], g2[...], g3[...], g4[...]))

    sh = jax.ShapeDtypeStruct(a_re.shape, F32)
    return pl.pallas_call(body, name="s5_disc_bwd", out_shape=(sh, sh, jax.ShapeDtypeStruct(ldt.shape, F32)))(
        a_re, a_im, ldt, d_abr, d_abi, d_cr, d_ci)


def _bbar(cr, ci, br, bi):
    return cr * br - ci * bi, cr * bi + ci * br


def _s5_bbar_fwd(cr_col, ci_col, b_re, b_im):
    def body(cr, ci, br, bi, o1, o2):
        o1[...], o2[...] = _bbar(cr[...], ci[...], br[...], bi[...])

    w = b_re.shape[1]
    return _rowcall(body, "s5_bbar_fwd", b_re.shape[0], [(cr_col, "row"), (ci_col, "row"), (b_re, "row"), (b_im, "row")],
                    [(w, F32, "row"), (w, F32, "row")], tile_rows=1024)


def _s5_bbar_bwd(cr_col, ci_col, b_re, b_im, d_re, d_im):
    def body(cr, ci, br, bi, g1, g2, o1, o2, o3, o4):
        _, vjp = jax.vjp(_bbar, cr[...], ci[...], br[...], bi[...])
        o1[...], o2[...], o3[...], o4[...] = vjp((g1[...], g2[...]))

    w = b_re.shape[1]
    return _rowcall(body, "s5_bbar_bwd", b_re.shape[0],
                    [(cr_col, "row"), (ci_col, "row"), (b_re, "row"), (b_im, "row"), (d_re, "row"), (d_im, "row")],
                    [(1, F32, "row"), (1, F32, "row"), (w, F32, "row"), (w, F32, "row")], tile_rows=1024)


def _block_diag_in(t):
    g, p, c = t.shape
    nb = g // GROUPS_PER_BLOCK
    t4 = t.reshape(nb, GROUPS_PER_BLOCK, p, c).transpose(0, 1, 3, 2)
    eye = jnp.eye(GROUPS_PER_BLOCK, dtype=t.dtype)
    return (t4[:, :, :, None, :] * eye[None, :, None, :, None]).reshape(nb, GROUPS_PER_BLOCK * c, GROUPS_PER_BLOCK * p)


def _block_diag_in_extract(d, p, c):
    nb = d.shape[0]
    d5 = d.reshape(nb, GROUPS_PER_BLOCK, c, GROUPS_PER_BLOCK, p)
    diag = jnp.stack([d5[:, g, :, g, :] for g in range(GROUPS_PER_BLOCK)], axis=1)
    return diag.transpose(0, 1, 3, 2).reshape(nb * GROUPS_PER_BLOCK, p, c)


def _block_diag_out(t):
    g, c, p = t.shape
    nb = g // GROUPS_PER_BLOCK
    t4 = t.reshape(nb, GROUPS_PER_BLOCK, c, p).transpose(0, 1, 3, 2)
    eye = jnp.eye(GROUPS_PER_BLOCK, dtype=t.dtype)
    return (t4[:, :, :, None, :] * eye[None, :, None, :, None]).reshape(nb, GROUPS_PER_BLOCK * p, GROUPS_PER_BLOCK * c)


def _block_diag_out_extract(d, c, p):
    nb = d.shape[0]
    d5 = d.reshape(nb, GROUPS_PER_BLOCK, p, GROUPS_PER_BLOCK, c)
    diag = jnp.stack([d5[:, g, :, g, :] for g in range(GROUPS_PER_BLOCK)], axis=1)
    return diag.transpose(0, 1, 3, 2).reshape(nb * GROUPS_PER_BLOCK, c, p)


def _scan_step(ar, ai, hr, hi, xr, xi):
    return ar * hr - ai * hi + xr, ar * hi + ai * hr + xi


def _s5_scan_fwd(u, bd_re, bd_im, cd_re, cd_im, ab_re, ab_im, init_re, init_im, d_row, full, name):
    s, w = u.shape
    nb = w // LANES
    rows = _tile(s, 512, SUBLANES)
    nc = s // rows
    steps = rows // N_SEG
    ns = nb * BLOCK_STATE

    def body(u_ref, bdr, bdi, cdr, cdi, ar_ref, ai_ref, ir_ref, ii_ref, d_ref, *outs):
        if full:
            y_ref, yg_ref, hr_ref, hi_ref, er_ref, ei_ref, cr, ci = outs
        else:
            er_ref, ei_ref, hr_ref, hi_ref, cr, ci = outs
        c = pl.program_id(1)

        @pl.when(c == 0)
        def _():
            cr[...] = ir_ref[...]
            ci[...] = ii_ref[...]

        ub = u_ref[...].astype(BF16)
        hr_ref[...] = jnp.dot(ub, bdr[...], preferred_element_type=F32)
        hi_ref[...] = jnp.dot(ub, bdi[...], preferred_element_type=F32)
        ar, ai = ar_ref[...], ai_ref[...]

        hr, hi = cr[...], ci[...]
        for j in range(steps):
            rows_j = pl.ds(j * N_SEG, N_SEG)
            hr, hi = _scan_step(ar, ai, hr, hi, hr_ref[rows_j, :], hi_ref[rows_j, :])
            hr_ref[rows_j, :] = hr
            hi_ref[rows_j, :] = hi
        cr[...] = hr
        ci[...] = hi
        if full:
            y = (jnp.dot(hr_ref[...].astype(BF16), cdr[...], preferred_element_type=F32)
                 + jnp.dot(hi_ref[...].astype(BF16), cdi[...], preferred_element_type=F32)
                 + d_ref[...] * u_ref[...])
            y_ref[...] = y
            yg_ref[...] = jax.nn.gelu(y).astype(BF16)

        @pl.when(c == nc - 1)
        def _():
            er_ref[...] = hr
            ei_ref[...] = hi

    blk3 = lambda a: pl.BlockSpec((None,) + a.shape[1:], lambda k, c: (k, 0, 0))
    seg = pl.BlockSpec((N_SEG, BLOCK_STATE), lambda k, c: (0, k))
    st = pl.BlockSpec((rows, BLOCK_STATE), lambda k, c: (c, k))
    in_specs = [pl.BlockSpec((rows, LANES), lambda k, c: (c, k)), blk3(bd_re), blk3(bd_im), blk3(cd_re), blk3(cd_im),
                seg, seg, seg, seg, pl.BlockSpec((1, LANES), lambda k, c: (0, k))]
    seg_shape = jax.ShapeDtypeStruct((N_SEG, ns), F32)
    st_shape = jax.ShapeDtypeStruct((s, ns), F32)
    carry = [pltpu.VMEM((N_SEG, BLOCK_STATE), F32)] * 2
    if full:
        ych = pl.BlockSpec((rows, LANES), lambda k, c: (c, k))
        out_specs = [ych, ych, st, st, seg, seg]
        out_shape = [jax.ShapeDtypeStruct((s, w), F32), jax.ShapeDtypeStruct((s, w), BF16), st_shape, st_shape, seg_shape, seg_shape]
        scratch = carry
    else:
        out_specs = [seg, seg]
        out_shape = [seg_shape, seg_shape]
        scratch = [pltpu.VMEM((rows, BLOCK_STATE), F32)] * 2 + carry
    return pl.pallas_call(
        body, name=name, grid=(nb, nc), in_specs=in_specs, out_specs=out_specs, out_shape=out_shape,
        scratch_shapes=scratch, compiler_params=_cparams("parallel", "arbitrary"),
    )(u, bd_re, bd_im, cd_re, cd_im, ab_re, ab_im, init_re, init_im, d_row)


def _s5_seg_fix(e_re, e_im, ab_re, ab_im, seg_len, reverse, name):
    assert seg_len & (seg_len - 1) == 0

    def body(er, ei, ar, ai, o_re, o_im):
        pr, pi = ar[0:1, :], ai[0:1, :]
        for _ in range(int(math.log2(seg_len))):
            pr, pi = pr * pr - pi * pi, 2.0 * pr * pi
        tr = jnp.zeros_like(pr)
        ti = jnp.zeros_like(pr)
        order = list(range(N_SEG - 1, -1, -1)) if reverse else list(range(N_SEG))
        for n, sgm in enumerate(order):
            o_re[sgm:sgm + 1, :] = tr
            o_im[sgm:sgm + 1, :] = ti
            if n < N_SEG - 1:
                tr, ti = _scan_step(pr, pi, tr, ti, er[sgm:sgm + 1, :], ei[sgm:sgm + 1, :])

    sh = jax.ShapeDtypeStruct(e_re.shape, F32)
    return pl.pallas_call(body, name=name, out_shape=(sh, sh))(e_re, e_im, ab_re, ab_im)


def _s5_scan_bwd(dy, u, h_re, h_im, bd_re, bd_im, cd_re, cd_im, ab_re, ab_imn, gin_re, gin_im, d_row, full, name, duz=None):
    s, w = u.shape
    nb = w // LANES
    rows = _tile(s, 512, SUBLANES)
    nc = s // rows
    steps = rows // N_SEG
    ns = nb * BLOCK_STATE

    def body(dy_ref, u_ref, hr_ref, hi_ref, bdr, bdi, cdr, cdi, ar_ref, ai_ref, ir_ref, ii_ref, d_ref, *outs):
        if full:
            _, du_ref, dbr_ref, dbi_ref, dcr_ref, dci_ref, dar_ref, dai_ref, dd_ref, gr, gi, accr, acci = outs
        else:
            er_ref, ei_ref, gr, gi = outs
        c = pl.program_id(1)

        @pl.when(c == 0)
        def _():
            gr[pl.ds(rows, N_SEG), :] = ir_ref[...]
            gi[pl.ds(rows, N_SEG), :] = ii_ref[...]
            if full:
                for r in (dbr_ref, dbi_ref, dcr_ref, dci_ref, dd_ref, accr, acci):
                    r[...] = jnp.zeros_like(r)

        dyb = dy_ref[...].astype(BF16)
        nt = (_DOT_DIMS["nt"], ((), ()))
        tn = (_DOT_DIMS["tn"], ((), ()))
        gr[pl.ds(0, rows), :] = lax.dot_general(dyb, cdr[...], nt, preferred_element_type=F32)
        gi[pl.ds(0, rows), :] = lax.dot_general(dyb, cdi[...], nt, preferred_element_type=F32)
        ar, ai = ar_ref[...], ai_ref[...]

        g0r, g0i = gr[pl.ds(rows, N_SEG), :], gi[pl.ds(rows, N_SEG), :]
        for j in range(steps - 1, -1, -1):
            rows_j = pl.ds(j * N_SEG, N_SEG)
            g0r, g0i = _scan_step(ar, ai, g0r, g0i, gr[rows_j, :], gi[rows_j, :])
            gr[rows_j, :] = g0r
            gi[rows_j, :] = g0i
        if full:
            hr, hi = hr_ref[...], hi_ref[...]
            gnr, gni = gr[pl.ds(N_SEG, rows), :], gi[pl.ds(N_SEG, rows), :]
            accr[...] += jnp.sum((gnr * hr + gni * hi).reshape(steps, N_SEG, BLOCK_STATE), axis=0)
            acci[...] += jnp.sum((gni * hr - gnr * hi).reshape(steps, N_SEG, BLOCK_STATE), axis=0)
        gr[pl.ds(rows, N_SEG), :] = g0r
        gi[pl.ds(rows, N_SEG), :] = g0i
        if full:
            ub = u_ref[...].astype(BF16)
            gbr, gbi = gr[pl.ds(0, rows), :].astype(BF16), gi[pl.ds(0, rows), :].astype(BF16)
            dcr_ref[...] += lax.dot_general(hr.astype(BF16), dyb, tn, preferred_element_type=F32)
            dci_ref[...] += lax.dot_general(hi.astype(BF16), dyb, tn, preferred_element_type=F32)
            dbr_ref[...] += lax.dot_general(ub, gbr, tn, preferred_element_type=F32)
            dbi_ref[...] += lax.dot_general(ub, gbi, tn, preferred_element_type=F32)
            du_ref[...] = (lax.dot_general(gbr, bdr[...], nt, preferred_element_type=F32)
                           + lax.dot_general(gbi, bdi[...], nt, preferred_element_type=F32)
                           + d_ref[...] * dy_ref[...]).astype(BF16)
            dd_ref[...] += jnp.sum(dy_ref[...] * u_ref[...], axis=0, keepdims=True)

        @pl.when(c == nc - 1)
        def _():
            if full:
                dar_ref[...] = jnp.sum(accr[...], axis=0, keepdims=True)
                dai_ref[...] = jnp.sum(acci[...], axis=0, keepdims=True)
            else:
                er_ref[...] = g0r
                ei_ref[...] = g0i

    rev = lambda k, c: (nc - 1 - c, k)
    blk3 = lambda a: pl.BlockSpec((None,) + a.shape[1:], lambda k, c: (k, 0, 0))
    seg = pl.BlockSpec((N_SEG, BLOCK_STATE), lambda k, c: (0, k))
    st = pl.BlockSpec((rows, BLOCK_STATE), rev)
    ch = pl.BlockSpec((rows, LANES), rev)
    vec = pl.BlockSpec((1, LANES), lambda k, c: (0, k))
    if not full:
        st = pl.BlockSpec((rows, BLOCK_STATE), lambda k, c: (0, k))
    in_specs = [ch, ch if full else pl.BlockSpec((rows, LANES), lambda k, c: (0, k)), st, st,
                blk3(bd_re), blk3(bd_im), blk3(cd_re), blk3(cd_im), seg, seg, seg, seg, vec]
    args = [dy, u, h_re, h_im, bd_re, bd_im, cd_re, cd_im, ab_re, ab_imn, gin_re, gin_im, d_row]
    gbuf = [pltpu.VMEM((rows + N_SEG, BLOCK_STATE), F32)] * 2
    if full:
        row1 = pl.BlockSpec((1, BLOCK_STATE), lambda k, c: (0, k))
        out_specs = [ch, blk3(bd_re), blk3(bd_im), blk3(cd_re), blk3(cd_im), row1, row1, vec]
        out_shape = [jax.ShapeDtypeStruct(duz.shape, BF16),
                     jax.ShapeDtypeStruct(bd_re.shape, F32), jax.ShapeDtypeStruct(bd_im.shape, F32),
                     jax.ShapeDtypeStruct(cd_re.shape, F32), jax.ShapeDtypeStruct(cd_im.shape, F32),
                     jax.ShapeDtypeStruct((1, ns), F32), jax.ShapeDtypeStruct((1, ns), F32),
                     jax.ShapeDtypeStruct((1, w), F32)]
        scratch = gbuf + [pltpu.VMEM((N_SEG, BLOCK_STATE), F32)] * 2
        in_specs.append(pl.BlockSpec(memory_space=pl.ANY))
        args.append(duz)
        aliases = {len(args) - 1: 0}
    else:
        out_specs = [seg, seg]
        out_shape = [jax.ShapeDtypeStruct((N_SEG, ns), F32)] * 2
        scratch = gbuf
        aliases = {}
    return pl.pallas_call(
        body, name=name, grid=(nb, nc), in_specs=in_specs, out_specs=out_specs, out_shape=out_shape,
        input_output_aliases=aliases, scratch_shapes=scratch, compiler_params=_cparams("parallel", "arbitrary"),
    )(*args)


def _log_sigmoid(x):
    return jnp.minimum(x, 0.0) - jnp.log(1.0 + jnp.exp(-jnp.abs(x)))


def _tri(n, upper):
    r = lax.broadcasted_iota(jnp.int32, (n, n), 0)
    c = lax.broadcasted_iota(jnp.int32, (n, n), 1)
    return jnp.where((c >= r) if upper else (r >= c), 1.0, 0.0).astype(F32)


def _cum_fwd(f_logit, b_row, name):
    s, w = f_logit.shape
    t = _tile(s, 256, SUBLANES)

    def body(f_ref, b_ref, o_ref, carry):
        @pl.when(pl.program_id(0) == 0)
        def _():
            carry[...] = jnp.zeros_like(carry)

        lf = _log_sigmoid(f_ref[...] + b_ref[...])
        cum = jnp.dot(_tri(t, False), lf, precision=lax.Precision.HIGHEST, preferred_element_type=F32) + carry[...]
        o_ref[...] = cum * LOG2E
        carry[...] = cum[t - 1:t, :]

    return pl.pallas_call(
        body, name=name, grid=(s // t,),
        in_specs=[pl.BlockSpec((t, w), lambda i: (i, 0)), pl.BlockSpec((1, w), lambda i: (0, 0))],
        out_specs=pl.BlockSpec((t, w), lambda i: (i, 0)), out_shape=jax.ShapeDtypeStruct((s, w), F32),
        scratch_shapes=[pltpu.VMEM((1, w), F32)], compiler_params=_cparams("arbitrary"),
    )(f_logit, b_row)


def _cum_bwd(dcq, dck, f_logit, b_row, name):
    s, w = f_logit.shape
    t = _tile(s, 256, SUBLANES)
    nt = s // t

    def body(q_ref, k_ref, f_ref, b_ref, df_ref, db_ref, carry):
        @pl.when(pl.program_id(0) == 0)
        def _():
            carry[...] = jnp.zeros_like(carry)
            db_ref[...] = jnp.zeros_like(db_ref)

        dc = q_ref[...] - k_ref[...]
        rc = jnp.dot(_tri(t, True), dc, precision=lax.Precision.HIGHEST, preferred_element_type=F32) + carry[...]
        carry[...] = rc[0:1, :]
        df = rc * (1.0 - jax.nn.sigmoid(f_ref[...] + b_ref[...]))
        df_ref[...] = df.astype(BF16)
        db_ref[...] += jnp.sum(df, axis=0, keepdims=True)

    rev = pl.BlockSpec((t, w), lambda i: (nt - 1 - i, 0))
    one = pl.BlockSpec((1, w), lambda i: (0, 0))
    return pl.pallas_call(
        body, name=name, grid=(nt,), in_specs=[rev, rev, rev, one], out_specs=[rev, one],
        out_shape=[jax.ShapeDtypeStruct((s, w), BF16), jax.ShapeDtypeStruct((1, w), F32)],
        scratch_shapes=[pltpu.VMEM((1, w), F32)], compiler_params=_cparams("arbitrary"),
    )(dcq, dck, f_logit, b_row)


def _head_col(cum_tile, h):
    lane = lax.broadcasted_iota(jnp.int32, cum_tile.shape, 1)
    return jnp.sum(jnp.where(lane == h, cum_tile, 0.0), axis=1, keepdims=True)


def _attn_tiles(s):
    return _tile(s, 512, LANES)


def _exp2_rows(sc, sub):
    return jnp.concatenate([jnp.exp2(sc[:, b * LANES:(b + 1) * LANES] - sub) for b in range(sc.shape[1] // LANES)], axis=1)


def _row_of(rep):
    return jnp.transpose(rep)[0:1, :]


def _causal(sc, keys_on_rows):
    r = lax.broadcasted_iota(jnp.int32, sc.shape, 0)
    c = lax.broadcasted_iota(jnp.int32, sc.shape, 1)
    return jnp.where((r <= c) if keys_on_rows else (c <= r), sc, NEG_INF)


def _fox_fwd(q2, kv, cum2_t, z, name):
    s, w = q2.shape
    nh = w // HEAD_DIM
    tq = _attn_tiles(s)
    nq = s // tq
    nt = (_DOT_DIMS["nt"], ((), ()))

    def body(q_ref, k_ref, v_ref, ct_ref, z_ref, o_ref, oz_ref, lse_row_ref, m_s, acc_s, vaug, s_buf):
        i = pl.program_id(1)

        @pl.when(i == 0)
        def _():
            vaug[:, :HEAD_DIM] = v_ref[...]
            vaug[:, HEAD_DIM:] = jnp.ones((s, LANES), BF16)

        qb = q_ref[...]
        m_s[...] = jnp.full_like(m_s, NEG_INF)
        acc_s[...] = jnp.zeros_like(acc_s)

        def scores(j):
            off = pl.multiple_of(j * tq, tq)
            return lax.dot_general(qb, k_ref[pl.ds(off, tq), :], nt, preferred_element_type=F32) - ct_ref[:, pl.ds(off, tq)]

        def softmax_pv(j, sc):
            m_old = m_s[...]
            m_new = jnp.maximum(m_old, jnp.max(sc, axis=1, keepdims=True))
            p = _exp2_rows(sc, m_new)
            alpha = jnp.exp2(m_old - m_new)
            pv = jnp.dot(p.astype(BF16), vaug[pl.ds(pl.multiple_of(j * tq, tq), tq), :], preferred_element_type=F32)
            acc_s[...] = jnp.concatenate([alpha, alpha], axis=1) * acc_s[...] + pv
            m_s[...] = m_new

        s_buf[...] = scores(0)

        def loop(j, carry):
            nxt = scores(j + 1)
            softmax_pv(j, s_buf[...])
            s_buf[...] = nxt
            return carry

        lax.fori_loop(0, i, loop, 0)
        softmax_pv(i, _causal(s_buf[...], False))
        l = acc_s[:, HEAD_DIM:]
        o = acc_s[:, :HEAD_DIM] / l
        o_ref[...] = o
        oz_ref[...] = (o * _silu(z_ref[...].astype(F32))).astype(BF16)
        lse_row_ref[...] = _row_of(m_s[...] + jnp.log(l) * LOG2E)

    return pl.pallas_call(
        body, name=name, grid=(nh, nq),
        in_specs=[pl.BlockSpec((tq, HEAD_DIM), lambda h, i: (i, h)),
                  pl.BlockSpec((s, HEAD_DIM), lambda h, i: (0, h)),
                  pl.BlockSpec((s, HEAD_DIM), lambda h, i: (0, nh + h)),
                  pl.BlockSpec((None, 1, s), lambda h, i: (h, 0, 0)),
                  pl.BlockSpec((tq, HEAD_DIM), lambda h, i: (i, h))],
        out_specs=[pl.BlockSpec((tq, HEAD_DIM), lambda h, i: (i, h)),
                   pl.BlockSpec((tq, HEAD_DIM), lambda h, i: (i, h)),
                   pl.BlockSpec((None, 1, tq), lambda h, i: (h, 0, i))],
        out_shape=[jax.ShapeDtypeStruct((s, w), F32), jax.ShapeDtypeStruct((s, w), BF16),
                   jax.ShapeDtypeStruct((nh, 1, s), F32)],
        scratch_shapes=[pltpu.VMEM((tq, LANES), F32), pltpu.VMEM((tq, HEAD_DIM + LANES), F32),
                        pltpu.VMEM((s, HEAD_DIM + LANES), BF16), pltpu.VMEM((tq, tq), F32)],
        compiler_params=_cparams("arbitrary", "arbitrary"),
    )(q2, kv, kv, cum2_t, z)


def _fox_bwd(q2, kv, do, o, lse2_t, cum2, dqz, name):
    s, w = q2.shape
    nh = w // HEAD_DIM
    tk = _attn_tiles(s)
    nk = s // tk
    scale = HEAD_DIM ** -0.5
    nt = (_DOT_DIMS["nt"], ((), ()))
    tn = (_DOT_DIMS["tn"], ((), ()))

    def body(q_ref, k_ref, v_ref, do_ref, o_ref, lse_ref, c_ref, _, dk_ref, dv_ref, dq_ref, dcq_ref, dck_ref,
             dk_s, dv_s, dc_s, dq_s, dcq_s, dl_s, s_buf, dp_buf):
        h, j = pl.program_id(0), pl.program_id(1)

        @pl.when(j == 0)
        def _():
            dq_s[...] = jnp.zeros_like(dq_s)
            dcq_s[...] = jnp.zeros_like(dcq_s)
            for i in range(nk):
                rows = pl.ds(i * tk, tk)
                d = jnp.sum(do_ref[rows, :].astype(F32) * o_ref[rows, :], axis=1, keepdims=True)
                dl_s[:, i * tk:(i + 1) * tk] = _row_of(jnp.broadcast_to(d, (tk, LANES)))

        kb = k_ref[...]
        vb = v_ref[...]
        ck = jnp.broadcast_to(_head_col(c_ref[...], h), (tk, LANES))
        dk_s[...] = jnp.zeros_like(dk_s)
        dv_s[...] = jnp.zeros_like(dv_s)
        dc_s[...] = jnp.zeros_like(dc_s)

        def scores(i):
            off = pl.multiple_of(i * tk, tk)
            sc = lax.dot_general(kb, q_ref[pl.ds(off, tk), :], nt, preferred_element_type=F32) - lse_ref[:, pl.ds(off, tk)]
            dp = lax.dot_general(vb, do_ref[pl.ds(off, tk), :], nt, preferred_element_type=F32) - dl_s[:, pl.ds(off, tk)]
            return sc, dp

        def accumulate(i, sc, dp):
            off = pl.multiple_of(i * tk, tk)
            p = _exp2_rows(sc, ck)
            dv_s[...] += jnp.dot(p.astype(BF16), do_ref[pl.ds(off, tk), :], preferred_element_type=F32)
            ds = p * dp
            dsb = ds.astype(BF16)
            dk_s[...] += jnp.dot(dsb, q_ref[pl.ds(off, tk), :], preferred_element_type=F32)
            dq_s[pl.ds(off, tk), :] += lax.dot_general(dsb, kb, tn, preferred_element_type=F32)
            dcq_s[:, pl.ds(off, tk)] += jnp.sum(ds, axis=0, keepdims=True)
            part = ds[:, :LANES]
            for b in range(1, tk // LANES):
                part = part + ds[:, b * LANES:(b + 1) * LANES]
            dc_s[...] += part

        sc0, dp0 = scores(j)
        s_buf[...] = _causal(sc0, True)
        dp_buf[...] = dp0

        def loop(i, carry):
            nxt = scores(i + 1)
            accumulate(i, s_buf[...], dp_buf[...])
            s_buf[...], dp_buf[...] = nxt
            return carry

        lax.fori_loop(j, nk - 1, loop, 0)
        accumulate(nk - 1, s_buf[...], dp_buf[...])
        dk_ref[...] = (dk_s[...] * (1.0 / LOG2E)).astype(BF16)
        dv_ref[...] = dv_s[...].astype(BF16)
        dck_ref[...] = jnp.sum(jnp.transpose(dc_s[...]), axis=0, keepdims=True)

        @pl.when(j == nk - 1)
        def _():
            dq_ref[...] = (dq_s[...] * scale).astype(BF16)
            dcq_ref[...] = dcq_s[...]

    col = pl.BlockSpec((s, HEAD_DIM), lambda h, j: (0, h))
    row = pl.BlockSpec((None, 1, s), lambda h, j: (h, 0, 0))
    kspec = pl.BlockSpec((tk, HEAD_DIM), lambda h, j: (j, h))
    return pl.pallas_call(
        body, name=name, grid=(nh, nk),
        in_specs=[col, kspec, pl.BlockSpec((tk, HEAD_DIM), lambda h, j: (j, nh + h)), col, col, row,
                  pl.BlockSpec((tk, LANES), lambda h, j: (j, 0)), pl.BlockSpec(memory_space=pl.ANY)],
        out_specs=[kspec, kspec, col, row, pl.BlockSpec((None, 1, tk), lambda h, j: (h, 0, j))],
        out_shape=[jax.ShapeDtypeStruct((s, w), BF16), jax.ShapeDtypeStruct((s, w), BF16),
                   jax.ShapeDtypeStruct(dqz.shape, BF16), jax.ShapeDtypeStruct((nh, 1, s), F32),
                   jax.ShapeDtypeStruct((nh, 1, s), F32)],
        input_output_aliases={7: 2},
        scratch_shapes=[pltpu.VMEM((tk, HEAD_DIM), F32), pltpu.VMEM((tk, HEAD_DIM), F32), pltpu.VMEM((tk, LANES), F32),
                        pltpu.VMEM((s, HEAD_DIM), F32), pltpu.VMEM((1, s), F32), pltpu.VMEM((1, s), F32),
                        pltpu.VMEM((tk, tk), F32), pltpu.VMEM((tk, tk), F32)],
        compiler_params=_cparams("arbitrary", "arbitrary"),
    )(q2, kv, kv, do, o, lse2_t, cum2, dqz)


def _fox_bwd_dq(q2, kv, do, o, lse2, cum2_t, dqz, name):
    s, w = q2.shape
    nh = w // HEAD_DIM
    tq = _attn_tiles(s)
    nq = s // tq
    scale = HEAD_DIM ** -0.5
    nt = (_DOT_DIMS["nt"], ((), ()))

    def body(q_ref, k_ref, v_ref, do_ref, o_ref, lse_ref, ct_ref, _, dq_ref, dl_ref, dcq_ref, acc_s, dc_s):
        i = pl.program_id(1)
        qb = q_ref[...]
        dob = do_ref[...]
        lse = lse_ref[...]
        delta = jnp.broadcast_to(jnp.sum(dob.astype(F32) * o_ref[...], axis=1, keepdims=True), (tq, LANES))
        acc_s[...] = jnp.zeros_like(acc_s)
        dc_s[...] = jnp.zeros_like(dc_s)

        def tile(j, masked):
            off = pl.multiple_of(j * tq, tq)
            kb = k_ref[pl.ds(off, tq), :]
            sc = lax.dot_general(qb, kb, nt, preferred_element_type=F32) - ct_ref[:, pl.ds(off, tq)]
            if masked:
                sc = _causal(sc, False)
            p = _exp2_rows(sc, lse)
            dp = lax.dot_general(dob, v_ref[pl.ds(off, tq), :], nt, preferred_element_type=F32)
            ds = p * (dp - jnp.concatenate([delta] * (tq // LANES), axis=1))
            acc_s[...] += jnp.dot(ds.astype(BF16), kb, preferred_element_type=F32)
            part = ds[:, :LANES]
            for b in range(1, tq // LANES):
                part = part + ds[:, b * LANES:(b + 1) * LANES]
            dc_s[...] += part

        def loop(j, carry):
            tile(j, False)
            return carry

        lax.fori_loop(0, i, loop, 0)
        tile(i, True)
        dq_ref[...] = (acc_s[...] * scale).astype(BF16)
        dl_ref[...] = _row_of(delta)
        dcq_ref[...] = jnp.sum(jnp.transpose(dc_s[...]), axis=0, keepdims=True)

    qspec = pl.BlockSpec((tq, HEAD_DIM), lambda h, i: (i, h))
    rep = pl.BlockSpec((None, tq, LANES), lambda h, i: (h, i, 0))
    rowspec = pl.BlockSpec((None, 1, tq), lambda h, i: (h, 0, i))
    return pl.pallas_call(
        body, name=name, grid=(nh, nq),
        in_specs=[qspec,
                  pl.BlockSpec((s, HEAD_DIM), lambda h, i: (0, h)),
                  pl.BlockSpec((s, HEAD_DIM), lambda h, i: (0, nh + h)),
                  qspec, qspec, rep,
                  pl.BlockSpec((None, 1, s), lambda h, i: (h, 0, 0)),
                  pl.BlockSpec(memory_space=pl.ANY)],
        out_specs=[qspec, rowspec, rowspec],
        out_shape=[jax.ShapeDtypeStruct(dqz.shape, BF16), jax.ShapeDtypeStruct((nh, 1, s), F32),
                   jax.ShapeDtypeStruct((nh, 1, s), F32)],
        input_output_aliases={7: 0},
        scratch_shapes=[pltpu.VMEM((tq, HEAD_DIM), F32), pltpu.VMEM((tq, LANES), F32)],
        compiler_params=_cparams("parallel", "arbitrary"),
    )(q2, kv, kv, do, o, lse2, cum2_t, dqz)


def _fox_bwd_dkv(q2, kv, do, lse2_t, delta_t, cum2, name):
    s, w = q2.shape
    nh = w // HEAD_DIM
    tk = _attn_tiles(s)
    nk = s // tk
    nt = (_DOT_DIMS["nt"], ((), ()))

    def body(q_ref, k_ref, v_ref, do_ref, lse_ref, dl_ref, c_ref, dk_ref, dv_ref, dck_ref, dk_s, dv_s, dc_s, s_buf, dp_buf):
        h, j = pl.program_id(0), pl.program_id(1)
        kb = k_ref[...]
        vb = v_ref[...]
        ck = jnp.broadcast_to(_head_col(c_ref[...], h), (tk, LANES))
        dk_s[...] = jnp.zeros_like(dk_s)
        dv_s[...] = jnp.zeros_like(dv_s)
        dc_s[...] = jnp.zeros_like(dc_s)

        def scores(i):
            off = pl.multiple_of(i * tk, tk)
            sc = lax.dot_general(kb, q_ref[pl.ds(off, tk), :], nt, preferred_element_type=F32) - lse_ref[:, pl.ds(off, tk)]
            dp = lax.dot_general(vb, do_ref[pl.ds(off, tk), :], nt, preferred_element_type=F32) - dl_ref[:, pl.ds(off, tk)]
            return sc, dp

        def accumulate(i, sc, dp):
            off = pl.multiple_of(i * tk, tk)
            p = _exp2_rows(sc, ck)
            dv_s[...] += jnp.dot(p.astype(BF16), do_ref[pl.ds(off, tk), :], preferred_element_type=F32)
            ds = p * dp
            dk_s[...] += jnp.dot(ds.astype(BF16), q_ref[pl.ds(off, tk), :], preferred_element_type=F32)
            part = ds[:, :LANES]
            for b in range(1, tk // LANES):
                part = part + ds[:, b * LANES:(b + 1) * LANES]
            dc_s[...] += part

        sc0, dp0 = scores(j)
        s_buf[...] = _causal(sc0, True)
        dp_buf[...] = dp0

        def loop(i, carry):
            nxt = scores(i + 1)
            accumulate(i, s_buf[...], dp_buf[...])
            s_buf[...], dp_buf[...] = nxt
            return carry

        lax.fori_loop(j, nk - 1, loop, 0)
        accumulate(nk - 1, s_buf[...], dp_buf[...])
        dk_ref[...] = (dk_s[...] * (1.0 / LOG2E)).astype(BF16)
        dv_ref[...] = dv_s[...].astype(BF16)
        dck_ref[...] = jnp.sum(jnp.transpose(dc_s[...]), axis=0, keepdims=True)

    col = pl.BlockSpec((s, HEAD_DIM), lambda h, j: (0, h))
    row = pl.BlockSpec((None, 1, s), lambda h, j: (h, 0, 0))
    kspec = pl.BlockSpec((tk, HEAD_DIM), lambda h, j: (j, h))
    return pl.pallas_call(
        body, name=name, grid=(nh, nk),
        in_specs=[col, kspec, pl.BlockSpec((tk, HEAD_DIM), lambda h, j: (j, nh + h)), col, row, row,
                  pl.BlockSpec((tk, LANES), lambda h, j: (j, 0))],
        out_specs=[kspec, kspec, pl.BlockSpec((None, 1, tk), lambda h, j: (h, 0, j))],
        out_shape=[jax.ShapeDtypeStruct((s, w), BF16), jax.ShapeDtypeStruct((s, w), BF16),
                   jax.ShapeDtypeStruct((nh, 1, s), F32)],
        scratch_shapes=[pltpu.VMEM((tk, HEAD_DIM), F32), pltpu.VMEM((tk, HEAD_DIM), F32),
                        pltpu.VMEM((tk, LANES), F32), pltpu.VMEM((tk, tk), F32), pltpu.VMEM((tk, tk), F32)],
        compiler_params=_cparams("parallel", "arbitrary"),
    )(q2, kv, kv, do, lse2_t, delta_t, cum2)


def _exchange_copies(ins, outs, send_sems, recv_sems, local_sems, scatter):
    x, y, c = (lax.axis_index(a) for a in MESH_AXES)
    me = 4 * x + 2 * y + c
    local, remote = [], []
    for a in range(len(ins)):
        local.append(pltpu.make_async_copy(ins[a].at[me] if scatter else ins[a], outs[a].at[me], local_sems.at[a]))
        for k in range(1, N_DEV):
            px, py, pc = (1 - x if k & 4 else x), (1 - y if k & 2 else y), (1 - c if k & 1 else c)
            remote.append(pltpu.make_async_remote_copy(
                src_ref=ins[a].at[4 * px + 2 * py + pc] if scatter else ins[a], dst_ref=outs[a].at[me],
                send_sem=send_sems.at[a * (N_DEV - 1) + k - 1], recv_sem=recv_sems.at[a * (N_DEV - 1) + k - 1],
                device_id=(px, py, pc), device_id_type=pl.DeviceIdType.MESH))
    return local, remote


def _exchange_out_shapes(arrs, scatter):
    return [((N_DEV,) + a.shape[1:]) if scatter else ((N_DEV,) + a.shape) for a in arrs]


def _exchange(arrs, scatter, name):
    n = len(arrs)

    def body(*refs):
        local, remote = _exchange_copies(refs[:n], refs[n:2 * n], *refs[2 * n:], scatter)
        for cp in local + remote:
            cp.start()
        for cp in remote:
            cp.wait_send()
            cp.wait_recv()
        for cp in local:
            cp.wait()

    out_shape = [jax.ShapeDtypeStruct(s, a.dtype) for s, a in zip(_exchange_out_shapes(arrs, scatter), arrs)]
    return pl.pallas_call(
        body, name=name, out_shape=out_shape,
        in_specs=[pl.BlockSpec(memory_space=pl.ANY)] * n, out_specs=[pl.BlockSpec(memory_space=pl.ANY)] * n,
        scratch_shapes=[pltpu.SemaphoreType.DMA((n * (N_DEV - 1),)), pltpu.SemaphoreType.DMA((n * (N_DEV - 1),)),
                        pltpu.SemaphoreType.DMA((n,))],
    )(*arrs)


_HBM = pl.BlockSpec(memory_space=pltpu.HBM)
_SEM = pl.BlockSpec(memory_space=pltpu.SEMAPHORE)


def _exchange_start(arrs, scatter, name, after=()):
    n = len(arrs)
    after = list(after)
    lands = [lax.empty(s, a.dtype) for s, a in zip(_exchange_out_shapes(arrs, scatter), arrs)]

    def body(*refs):
        ins, outs = refs[:n], refs[n:2 * n]
        send_sems, recv_sems, local_sems = refs[2 * n + len(after):2 * n + len(after) + 3]
        token = refs[-1]
        local, remote = _exchange_copies(ins, outs, send_sems, recv_sems, local_sems, scatter)
        for cp in local + remote:
            cp.start()
        token[...] = jnp.zeros_like(token)

    hbm = lambda a: pltpu.HBM(a.shape, a.dtype)
    res = pl.pallas_call(
        body, name=name,
        out_shape=(pltpu.SemaphoreType.DMA((n * (N_DEV - 1),)), pltpu.SemaphoreType.DMA((n * (N_DEV - 1),)),
                   pltpu.SemaphoreType.DMA((n,)), *[hbm(a) for a in arrs], *[hbm(a) for a in lands],
                   jax.ShapeDtypeStruct((SUBLANES, LANES), F32)),
        in_specs=[_HBM] * (2 * n) + [pl.BlockSpec(memory_space=pl.ANY)] * len(after),
        out_specs=(_SEM, _SEM, _SEM, *[_HBM] * (2 * n), pl.BlockSpec(memory_space=pltpu.VMEM)),
        input_output_aliases={i: 3 + i for i in range(2 * n)},
        compiler_params=pltpu.CompilerParams(has_side_effects=pltpu.SideEffectType.DATAFLOW_SIDE_EFFECTING),
    )(*[pltpu.with_memory_space_constraint(a, pltpu.HBM) for a in list(arrs) + lands], *after)
    return (n, scatter, res[:3], res[3:3 + n], res[3 + n:3 + 2 * n]), res[-1]


def _exchange_wait(state, after, name):
    n, scatter, sems, srcs, lands = state
    after = list(after) if isinstance(after, (list, tuple)) else [after]

    def body(*refs):
        ins, outs = refs[:n], refs[n:2 * n]
        send_sems, recv_sems, local_sems = refs[2 * n:2 * n + 3]
        local, remote = _exchange_copies(ins, outs, send_sems, recv_sems, local_sems, scatter)
        for cp in remote:
            cp.wait_send()
            cp.wait_recv()
        for cp in local:
            cp.wait()

    hbm = lambda a: pltpu.HBM(a.shape, a.dtype)
    res = pl.pallas_call(
        body, name=name,
        out_shape=(*[hbm(a) for a in srcs], *[hbm(a) for a in lands]),
        in_specs=[_HBM] * (2 * n) + [_SEM] * 3 + [pl.BlockSpec(memory_space=pl.ANY)] * len(after),
        out_specs=tuple([_HBM] * (2 * n)),
        input_output_aliases={i: i for i in range(2 * n)},
        compiler_params=pltpu.CompilerParams(has_side_effects=pltpu.SideEffectType.DATAFLOW_SIDE_EFFECTING),
    )(*srcs, *lands, *sems, *after)
    return list(res[n:])


def _adamw_math(w, g, m, v):
    m = ADAM_B1 * m + (1.0 - ADAM_B1) * g
    v = ADAM_B2 * v + (1.0 - ADAM_B2) * (g * g)
    m_hat = m / (1.0 - ADAM_B1 ** ADAM_STEP)
    v_hat = v / (1.0 - ADAM_B2 ** ADAM_STEP)
    return -ADAM_LR * (m_hat / (jnp.sqrt(v_hat) + ADAM_EPS) + ADAM_WD * w), m, v


def _slot_sum(p_ref):
    g = p_ref[0].astype(F32)
    for d in range(1, p_ref.shape[0]):
        g = g + p_ref[d].astype(F32)
    return g


def _adamw_tile(r, c):
    return _tile(r, max(SUBLANES, (256 * 1024) // c // SUBLANES * SUBLANES), SUBLANES)


def _adamw(parts, w, m, v, name):
    r, c = w.shape[-2:]
    tr = _adamw_tile(r, c)

    def body(p_ref, w_ref, m_ref, v_ref, g_ref, d_ref, nm_ref, nv_ref):
        g = _slot_sum(p_ref)
        g_ref[...] = g
        d_ref[...], nm_ref[...], nv_ref[...] = _adamw_math(w_ref[...], g, m_ref[...], v_ref[...])

    if w.ndim == 3:
        blk = pl.BlockSpec((None, tr, c), lambda i: (0, i, 0))
    else:
        blk = pl.BlockSpec((tr, c), lambda i: (i, 0))
    sh = jax.ShapeDtypeStruct(w.shape, F32)
    return pl.pallas_call(
        body, name=name, grid=(r // tr,),
        in_specs=[pl.BlockSpec((parts.shape[0], tr, c), lambda i: (0, i, 0)), blk, blk, blk],
        out_specs=[blk] * 4, out_shape=[sh] * 4, compiler_params=_cparams("parallel"),
    )(parts, w, m, v)


def _sum_parts(parts, name):
    _, r, c = parts.shape
    tr = _adamw_tile(r, c)

    def body(p_ref, o_ref):
        o_ref[...] = _slot_sum(p_ref)

    return pl.pallas_call(
        body, name=name, grid=(r // tr,),
        in_specs=[pl.BlockSpec((parts.shape[0], tr, c), lambda i: (0, i, 0))],
        out_specs=pl.BlockSpec((tr, c), lambda i: (i, 0)), out_shape=jax.ShapeDtypeStruct((r, c), F32),
        compiler_params=_cparams("parallel"),
    )(parts)


def _perm(a):
    s, d = a.shape
    return a.reshape(N_SEG, s // N_SEG, d).transpose(1, 0, 2).reshape(s, d)


def _unperm(a):
    s, d = a.shape
    return a.reshape(s // N_SEG, N_SEG, d).transpose(1, 0, 2).reshape(s, d)


def _lane_pad(a, width=LANES):
    return jnp.pad(a, ((0, 0), (0, width - a.shape[1])))


def _local_step(x, target, norm_pre, norm_post, kv_norm, kv_b_f, a_re, a_im, log_dt, b_re, b_im, c_re, c_im, comm):
    s, d = x.shape
    g, p = a_re.shape
    w = g * S5_GROUP
    fw = d
    nh = fw // HEAD_DIM
    seg_len = s // N_SEG
    row = lambda v: v.reshape(1, -1)
    g_pre0, g_pre1, g_post0, g_post1, g_kv = row(norm_pre[0]), row(norm_pre[1]), row(norm_post[0]), row(norm_post[1]), row(kv_norm)

    ldt = log_dt.reshape(g, 1)
    abr, abi, cr, ci = _s5_disc_fwd(a_re, a_im, ldt)
    cr_col, ci_col = cr.reshape(g * p, 1), ci.reshape(g * p, 1)
    b_re2, b_im2 = b_re.reshape(g * p, S5_GROUP), b_im.reshape(g * p, S5_GROUP)
    bb_re, bb_im = _s5_bbar_fwd(cr_col, ci_col, b_re2, b_im2)
    bd_re = _block_diag_in(bb_re.reshape(g, p, S5_GROUP)).astype(BF16)
    bd_im = _block_diag_in(bb_im.reshape(g, p, S5_GROUP)).astype(BF16)
    cd_re = _block_diag_out(c_re).astype(BF16)
    cd_im = _block_diag_out(-c_im).astype(BF16)
    ab_re = jnp.broadcast_to(abr.reshape(1, g * p), (N_SEG, g * p))
    ab_im = jnp.broadcast_to(abi.reshape(1, g * p), (N_SEG, g * p))
    zero_seg = jnp.zeros((N_SEG, g * p), F32)

    xn0 = _norm_cast(x, g_pre0 + comm.token, "norm_pre0", x_kind="nat")
    w_in = comm.weight("s5_w_in", [xn0, bd_re, bd_im, cd_re, cd_im, ab_re, ab_im])
    d_row, bglu_row = row(comm.vector("s5_d")), row(comm.vector("s5_b_glu"))
    u = _mm(xn0, w_in, "nn", F32, "s5_in_u", b_cols=(0, w), b_slots=True)
    z0 = _mm(xn0, w_in, "nn", BF16, "s5_in_z", b_cols=(w, w), b_slots=True)
    e_re, e_im = _s5_scan_fwd(u, bd_re, bd_im, cd_re, cd_im, ab_re, ab_im, zero_seg, zero_seg, d_row, False, "s5_scan_ends")
    i_re, i_im = _s5_seg_fix(e_re, e_im, ab_re, ab_im, seg_len, False, "s5_seg_fix")
    y_ssm, yg, h_re, h_im, _, _ = _s5_scan_fwd(u, bd_re, bd_im, cd_re, cd_im, ab_re, ab_im, i_re, i_im, d_row, True, "s5_scan")
    w_glu, w_out = comm.weight("s5_w_glu", yg), comm.weight("s5_w_out", yg)
    gp = _mm(yg, w_glu, "nn", BF16, "s5_glu")
    y3 = _s5_gate(y_ssm, gp, bglu_row, z0, "s5_gate")
    w_kv, fw_in = comm.weight("kv_w", y3), comm.weight("fox_w_in", y3)
    w_f = _lane_pad(w_kv[:, 2 * fw:])
    o0 = _mm(y3, w_out, "nn", F32, "s5_out")
    r0 = _post_norm(o0, g_post0 + comm.late_token, "norm_post0", out_kind="nat")

    h1, hn_kv, xn1 = _resid_norm2(x, r0, g_kv, g_pre1, "resid_norms")
    kv = _mm(hn_kv, w_kv, "nn", BF16, "kv_proj", b_cols=(0, 2 * fw))
    f_logit = _mm(hn_kv, w_f, "nn", F32, "f_proj")
    bf_row = _lane_pad(row(kv_b_f))
    cum2 = _cum_fwd(f_logit, bf_row, "cum_fwd")
    cum2_t = cum2[:, :nh].T.reshape(nh, 1, s)
    q2 = _mm(xn1, fw_in, "nn", BF16, "fox_q", scale=HEAD_DIM ** -0.5 * LOG2E, b_cols=(0, fw), b_slots=True)
    z1 = _mm(xn1, fw_in, "nn", BF16, "fox_z", b_cols=(fw, fw), b_slots=True)
    o, oz, lse2_t = _fox_fwd(q2, kv, cum2_t, z1, "fox_fwd")
    fw_out = comm.weight("fox_w_out", oz)
    o1 = _mm(oz, fw_out, "nn", F32, "fox_out")
    dh2, do1, sq, dg_post1 = _post_norm_loss(o1, g_post1, h1, target, "norm_post1_loss")
    loss = 0.5 * jnp.sum(sq) / d

    d_fw_out = _mm(oz, do1, "tn", BF16, "fox_out_dw")
    d_oz = _mm(do1, fw_out, "nt", F32, "fox_out_dx")
    do, dqz = _gate_bwd(d_oz, o, z1, "fox_gate_bwd")
    dk, dv, dqz, dcq, dck = _fox_bwd(q2, kv, do, o, lse2_t, cum2, dqz, "fox_bwd")
    d_fw_in = _mm(xn1, dqz, "tn", BF16, "fox_in_dw", col_slots=True)
    dxn1 = _mm(dqz, fw_in, "nt", F32, "fox_in_dx", b_slots=True)
    dcq_sl = _lane_pad(dcq.reshape(nh, s).T)
    dck_sl = _lane_pad(dck.reshape(nh, s).T)
    df, db_f = _cum_bwd(dcq_sl, dck_sl, f_logit, bf_row, "cum_bwd")
    dkv = _concat_cast(dk, dv, "fox_dkv")
    d_w_kvm = _mm(hn_kv, dkv, "tn", F32, "kv_dw")
    d_w_f = _mm(hn_kv, df, "tn", F32, "f_dw")
    dhn_f = _mm(df, w_f, "nt", F32, "f_dx")
    dhn_kv = _mm(dkv, w_kv, "nt", F32, "kv_dx", add=dhn_f, b_cols=(0, 2 * fw))
    d_w_kv = jnp.concatenate([d_w_kvm, d_w_f[:, :nh]], axis=1)
    tok = comm.send_grads(dict(fox_w_out=d_fw_out, fox_w_in=d_fw_in, kv_w=d_w_kv), "exchange_fox")
    dh1, dg_pre1, dg_kv = _norm_bwd2(dh2, h1, dxn1, dhn_kv, g_pre1, g_kv, "resid_norms_bwd")

    do0, dg_post0 = _post_norm_bwd(dh1, o0, g_post0 + tok[0, 0], "norm_post0_bwd", dy_kind="nat")
    d_w_out = _mm(y3, do0, "tn", BF16, "s5_out_dw")
    dy3 = _mm(do0, w_out, "nt", F32, "s5_out_dx")
    duz, dgp, dyg_direct, db_glu = _s5_gate_bwd(dy3, y_ssm, gp, bglu_row, z0, "s5_gate_bwd")
    d_w_glu = _mm(yg, dgp, "tn", BF16, "s5_glu_dw")
    dyg = _mm(dgp, w_glu, "nt", F32, "s5_glu_dx", add=dyg_direct)
    dy_ssm = _gelu_bwd(dyg, y_ssm, "s5_gelu_bwd")
    d_row = d_row + comm.send_grads(dict(s5_w_out=d_w_out, s5_w_glu=d_w_glu), "exchange_s5")[0, 0]
    ab_imn = -ab_im
    ge_re, ge_im = _s5_scan_bwd(dy_ssm, u, h_re, h_im, bd_re, bd_im, cd_re, cd_im, ab_re, ab_imn, zero_seg, zero_seg,
                                d_row, False, "s5_adj_ends")
    gi_re, gi_im = _s5_seg_fix(ge_re, ge_im, ab_re, ab_imn, seg_len, True, "s5_adj_fix")
    duz, dbd_re, dbd_im, dcd_re, dcd_im, dab_re, dab_im, dd = _s5_scan_bwd(
        dy_ssm, u, h_re, h_im, bd_re, bd_im, cd_re, cd_im, ab_re, ab_imn, gi_re, gi_im, d_row, True, "s5_adj", duz=duz)
    d_w_in = _mm(xn0, duz, "tn", BF16, "s5_in_dw", col_slots=True)
    tok = comm.send_grads(dict(s5_w_in=d_w_in), "exchange_s5_in")
    dxn0 = _mm(duz, w_in, "nt", F32, "s5_in_dx", after=tok, b_slots=True)
    grad_x, dg_pre0 = _norm_bwd1(dh1, x, dxn0, g_pre0, "norm_pre0_bwd")

    dbb_re = _block_diag_in_extract(dbd_re, p, S5_GROUP).reshape(g * p, S5_GROUP)
    dbb_im = _block_diag_in_extract(dbd_im, p, S5_GROUP).reshape(g * p, S5_GROUP)
    dcr_col, dci_col, db_re, db_im = _s5_bbar_bwd(cr_col, ci_col, b_re2, b_im2, dbb_re, dbb_im)
    da_re, da_im, dldt = _s5_disc_bwd(a_re, a_im, ldt, dab_re.reshape(g, p), dab_im.reshape(g, p),
                                      dcr_col.reshape(g, p), dci_col.reshape(g, p))
    dc_re = _block_diag_out_extract(dcd_re, S5_GROUP, p)
    dc_im = -_block_diag_out_extract(dcd_im, S5_GROUP, p)

    small = dict(
        norm_pre=jnp.concatenate([dg_pre0, dg_pre1], axis=0), norm_post=jnp.concatenate([dg_post0, dg_post1], axis=0),
        s5_a_re=da_re, s5_a_im=da_im, s5_log_dt=dldt.reshape(g), s5_b_re=db_re.reshape(g, p, S5_GROUP),
        s5_b_im=db_im.reshape(g, p, S5_GROUP), s5_c_re=dc_re, s5_c_im=dc_im, s5_d=dd.reshape(-1),
        s5_b_glu=db_glu.reshape(-1), kv_norm=dg_kv.reshape(-1), kv_b_f=db_f[0, :nh])
    return loss, grad_x, small


_BIG = ("s5_w_in", "s5_w_glu", "s5_w_out", "kv_w", "fox_w_in", "fox_w_out")
_COL_SHARDED = ("s5_w_in", "kv_w", "fox_w_in")
_SMALL = ("norm_pre", "norm_post", "s5_a_re", "s5_a_im", "s5_log_dt", "s5_b_re", "s5_b_im", "s5_c_re", "s5_c_im",
          "s5_d", "s5_b_glu", "kv_norm", "kv_b_f")
_SMALL_SHARDED = ("s5_d", "s5_b_glu")
_PACK_QUANTUM = SUBLANES * LANES
_WEIGHTS = ('norm_pre', 'norm_post', 's5_w_in', 's5_a_re', 's5_a_im', 's5_log_dt', 's5_b_re', 's5_b_im', 's5_c_re', 's5_c_im',
            's5_d', 's5_w_glu', 's5_b_glu', 's5_w_out', 'kv_norm', 'kv_w', 'kv_b_f', 'fox_w_in', 'fox_w_out')


def _full_from_slots(name, slots):
    n, r, c = slots.shape
    if name in _COL_SHARDED:
        return slots.transpose(1, 0, 2).reshape(r, n * c)
    return slots.reshape(n * r, c)


def _slots_from_full(name, full):
    if name in _COL_SHARDED:
        r, nc = full.shape
        return full.reshape(r, N_DEV, nc // N_DEV).transpose(1, 0, 2)
    nr, c = full.shape
    return full.reshape(N_DEV, nr // N_DEV, c)


def _pack(vals):
    parts = []
    for v in vals:
        flat = v.reshape(-1)
        parts.append(jnp.pad(flat, (0, (-flat.shape[0]) % _PACK_QUANTUM)))
    total = sum(p.shape[0] for p in parts)
    parts.append(jnp.zeros(((-total) % (N_DEV * _PACK_QUANTUM),), F32))
    return jnp.concatenate(parts).reshape(-1, LANES)


def _unpack(packed, shapes):
    flat = packed.reshape(-1)
    out, off = [], 0
    for sh in shapes:
        n = math.prod(sh)
        out.append(flat[off:off + n].reshape(sh))
        off += n + (-n) % _PACK_QUANTUM
    return out


class _Comm:
    _GROUPS = (("s5_w_in",) + _SMALL_SHARDED, ("s5_w_glu", "s5_w_out"), ("kv_w", "fox_w_in"), ("fox_w_out",))
    _SLOT_FORM = ("s5_w_in", "fox_w_in")

    def __init__(self, shards, vectors, early=()):
        self._shards = {**shards, **vectors}
        self._full, self._gathers = {}, {}
        self._early = list(early)
        self.token = jnp.zeros((), F32)
        for group in self._GROUPS[:-1]:
            self.token = self.token + self._start(group, ())[0, 0]
        self.late_token = None
        self._sent = []

    def _start(self, group, after):
        state, tok = _exchange_start([self._shards[n] for n in group], False, "gather_start_" + group[0], after)
        self._gathers[group] = state
        return tok

    def vector(self, name):
        return self._full[name]

    def weight(self, name, after):
        if name not in self._full:
            group = next(g for g in self._GROUPS if name in g)
            if group == self._GROUPS[0]:
                after = (list(after) if isinstance(after, (list, tuple)) else [after]) + self._early
            slots = _exchange_wait(self._gathers.pop(group), after, "gather_wait_" + group[0])
            for n, sl in zip(group, slots):
                if n in _SMALL_SHARDED:
                    self._full[n] = sl.reshape(-1)
                else:
                    self._full[n] = sl if n in self._SLOT_FORM else _full_from_slots(n, sl)
            if group == self._GROUPS[-2]:
                self.late_token = self._start(self._GROUPS[-1], [slots[0]])[0, 0]
        return self._full[name]

    def send_grads(self, grads, name):
        names = list(grads)
        slots = [grads[n] if grads[n].ndim == 3 else _slots_from_full(n, grads[n]).astype(BF16) for n in names]
        state, tok = _exchange_start(slots, True, name + "_start")
        self._sent.append((names, state, name + "_wait"))
        return tok

    def received_grads(self, after):
        for names, state, name in self._sent:
            for n, recv in zip(names, _exchange_wait(state, after, name)):
                yield n, recv


def kernel(x, norm_pre, norm_post, s5_w_in, s5_a_re, s5_a_im, s5_log_dt, s5_b_re, s5_b_im, s5_c_re, s5_c_im, s5_d, s5_w_glu, s5_b_glu, s5_w_out, kv_norm, kv_w, kv_b_f, fox_w_in, fox_w_out, loss_target, m_norm_pre, m_norm_post, m_s5_w_in, m_s5_a_re, m_s5_a_im, m_s5_log_dt, m_s5_b_re, m_s5_b_im, m_s5_c_re, m_s5_c_im, m_s5_d, m_s5_w_glu, m_s5_b_glu, m_s5_w_out, m_kv_norm, m_kv_w, m_kv_b_f, m_fox_w_in, m_fox_w_out, v_norm_pre, v_norm_post, v_s5_w_in, v_s5_a_re, v_s5_a_im, v_s5_log_dt, v_s5_b_re, v_s5_b_im, v_s5_c_re, v_s5_c_im, v_s5_d, v_s5_w_glu, v_s5_b_glu, v_s5_w_out, v_kv_norm, v_kv_w, v_kv_b_f, v_fox_w_in, v_fox_w_out):
    env = dict(locals())
    wts = {n: env[n] for n in _WEIGHTS}
    mom = {n: env["m_" + n] for n in _WEIGHTS}
    var = {n: env["v_" + n] for n in _WEIGHTS}
    me = 4 * lax.axis_index("x") + 2 * lax.axis_index("y") + lax.axis_index("c")
    shard2d = {n: wts[n].reshape(wts[n].shape[-2:]) for n in _BIG}
    full_shape = {n: ((wts[n].size * N_DEV,) if n in _SMALL_SHARDED else wts[n].shape) for n in _SMALL}

    def spread(n, v):
        if n not in _SMALL_SHARDED:
            return v
        flat = v.reshape(-1)
        return lax.dynamic_update_slice(jnp.zeros(full_shape[n], F32), flat, (me * flat.shape[0],))

    packed = [_pack([spread(n, src[n]) for n in _SMALL] + [jnp.zeros((1,), F32)]) for src in (wts, mom, var)]
    comm = _Comm({n: shard2d[n].astype(BF16) for n in _BIG}, {n: wts[n].reshape(1, -1) for n in _SMALL_SHARDED}, packed)

    loss_local, grad_x, small = _local_step(
        x[0], loss_target[0], norm_pre, norm_post, kv_norm, kv_b_f, s5_a_re[0], s5_a_im[0], s5_log_dt[0],
        s5_b_re[0], s5_b_im[0], s5_c_re[0], s5_c_im[0], comm)

    small_pack = _pack([small[n] for n in _SMALL] + [loss_local.reshape(1)])
    slice_rows = small_pack.shape[0] // N_DEV
    small_state, small_tok = _exchange_start([small_pack.reshape(N_DEV, slice_rows, LANES)], True, "reduce_small_start")

    res = {}
    for n, recv in comm.received_grads([small_tok, grad_x]):
        res[n] = _adamw(recv, wts[n], mom[n], var[n], "adamw_" + n)

    my_sum = _sum_parts(_exchange_wait(small_state, res[_BIG[0]][0], "reduce_small_wait")[0], "sum_small")
    g_all = _exchange([my_sum], False, "gather_small")[0].reshape(1, small_pack.shape[0], LANES)
    outs = _adamw(g_all, *packed, "adamw_small")
    unpacked = [_unpack(o, [full_shape[n] for n in _SMALL] + [(1,)]) for o in outs]
    loss = unpacked[0][-1][0]
    for i, n in enumerate(_SMALL):
        vals = [u[i] for u in unpacked]
        if n in _SMALL_SHARDED:
            k = wts[n].size
            vals = [lax.dynamic_slice(v, (me * k,), (k,)) for v in vals]
        res[n] = [v.reshape(wts[n].shape) for v in vals]

    return (loss, grad_x[None], *[res[n][0] for n in _WEIGHTS], *[res[n][1] for n in _WEIGHTS],
            *[res[n][2] for n in _WEIGHTS], *[res[n][3] for n in _WEIGHTS])
```

```python
import functools
import math

import jax
import jax.numpy as jnp
from jax import lax
from jax.experimental import pallas as pl
from jax.experimental.pallas import tpu as pltpu

F32 = jnp.float32
BF16 = jnp.bfloat16

N_DEV = 8
MESH_AXES = ("x", "y", "c")
S5_GROUP = 16
S5_STATE = 64
LANES = 128
SUBLANES = 8
GROUPS_PER_BLOCK = LANES // S5_GROUP
BLOCK_STATE = GROUPS_PER_BLOCK * S5_STATE
N_SEG = SUBLANES
HEAD_DIM = 128
RMS_EPS = 1e-6
NEG_INF = -1e30
LOG2E = math.log2(math.e)
ADAM_LR = 0.001
ADAM_B1 = 0.9
ADAM_B2 = 0.999
ADAM_EPS = 1e-08
ADAM_WD = 0.01
ADAM_STEP = 10
VMEM_LIMIT = 56 * 1024 * 1024


def _tile(n, pref, quantum=LANES):
    if n <= pref:
        return n
    t = (pref // quantum) * quantum
    while t >= quantum:
        if n % t == 0:
            return t
        t -= quantum
    return n


def _cparams(*sem):
    return pltpu.CompilerParams(dimension_semantics=sem if sem else None, vmem_limit_bytes=VMEM_LIMIT)


_DOT_DIMS = {"nn": ((1,), (0,)), "nt": ((1,), (1,)), "tn": ((0,), (0,))}


def _mm(a, b, mode, out_dtype, name, add=None, scale=None, b_cols=None, after=None, col_slots=False, b_slots=False):
    slot_w = b.shape[2] if b_slots else None
    b2d = (b.shape[1], b.shape[0] * b.shape[2]) if b_slots else b.shape
    b_shape = b2d if b_cols is None else (b2d[0], b_cols[1])
    if mode == "nn":
        (M, K), (K2, N) = a.shape, b_shape
    elif mode == "nt":
        (M, K), (N, K2) = a.shape, b_shape
    else:
        (K, M), (K2, N) = a.shape, b_shape
    assert K == K2, (name, a.shape, b_shape)
    tm, tn, tk = _tile(M, 1024 if K <= 2048 else 512), (N // N_DEV if col_slots else _tile(N, 1024)), _tile(K, 4096)
    if b_slots and mode == "nn":
        tn = slot_w
    nk = K // tk
    dims = (_DOT_DIMS[mode], ((), ()))
    col0 = 0
    if b_cols is not None:
        assert mode != "tn" and b_cols[0] % (tn if mode == "nn" else tk) == 0
        col0 = b_cols[0] // (tn if mode == "nn" else tk)
    assert not b_slots or (mode == "nn" or (mode == "nt" and nk == 1 and b_cols is None))

    def body(*refs):
        a_ref, b_ref = refs[:2]
        c_ref = refs[2] if add is not None else None
        o_ref = refs[2 + (add is not None) + (after is not None)]
        if b_slots and mode == "nt":
            part = lax.dot_general(a_ref[:, :slot_w], b_ref[0], dims, preferred_element_type=F32)
            for sl in range(1, b_ref.shape[0]):
                part += lax.dot_general(a_ref[:, sl * slot_w:(sl + 1) * slot_w], b_ref[sl], dims, preferred_element_type=F32)
        else:
            part = lax.dot_general(a_ref[...], b_ref[...], dims, preferred_element_type=F32)

        def finish(r):
            if scale is not None:
                r = r * scale
            if add is not None:
                r = r + c_ref[...]
            o_ref[...] = r.astype(out_dtype)

        if nk == 1:
            finish(part)
            return
        acc = refs[-1]
        k = pl.program_id(2)

        @pl.when(k == 0)
        def _():
            acc[...] = part

        @pl.when(jnp.logical_and(k > 0, k < nk - 1))
        def _():
            acc[...] += part

        @pl.when(k == nk - 1)
        def _():
            finish(acc[...] + part)

    if mode == "tn":
        a_spec = pl.BlockSpec((tk, tm), lambda i, j, k: (k, i))
    else:
        a_spec = pl.BlockSpec((tm, tk), lambda i, j, k: (i, k))
    if b_slots and mode == "nn":
        b_spec = pl.BlockSpec((None, tk, tn), lambda i, j, k: (j + col0, k, 0))
    elif b_slots:
        b_spec = pl.BlockSpec((b.shape[0], tn, slot_w), lambda i, j, k: (0, j, 0))
    elif mode == "nt":
        b_spec = pl.BlockSpec((tn, tk), lambda i, j, k: (j, k + col0))
    else:
        b_spec = pl.BlockSpec((tk, tn), lambda i, j, k: (k, j + col0))
    o_spec = pl.BlockSpec((tm, tn), lambda i, j, k: (i, j))
    in_specs = [a_spec, b_spec] + ([o_spec] if add is not None else [])
    args = (a, b) + ((add,) if add is not None else ())
    if after is not None:
        in_specs.append(pl.BlockSpec(after.shape, lambda i, j, k: (0, 0)))
        args += (after,)
    out_shape = jax.ShapeDtypeStruct((M, N), out_dtype)
    if col_slots:
        assert add is None
        o_spec = pl.BlockSpec((None, tm, tn), lambda i, j, k: (j, i, 0))
        out_shape = jax.ShapeDtypeStruct((N_DEV, M, tn), out_dtype)
    return pl.pallas_call(
        body, name=name, grid=(M // tm, N // tn, nk),
        in_specs=in_specs, out_specs=o_spec,
        out_shape=out_shape,
        scratch_shapes=[pltpu.VMEM((tm, tn), F32)] if nk > 1 else [],
        compiler_params=_cparams("parallel", "parallel", "arbitrary"),
    )(*args)


class _NatIn:
    def __init__(self, ref):
        self.ref = ref

    def __getitem__(self, idx):
        v = jnp.swapaxes(self.ref[...], 0, 1)
        return v.reshape(v.shape[0] * N_SEG, v.shape[2])


class _NatOut:
    def __init__(self, ref):
        self.ref = ref

    def __setitem__(self, idx, val):
        self.ref[...] = jnp.swapaxes(val.reshape(val.shape[0] // N_SEG, N_SEG, val.shape[1]), 0, 1)


def _rowcall(body, name, n_rows, ins, outs, tile_rows=256):
    tr = _tile(n_rows, tile_rows, SUBLANES * 2)
    n_in = len(ins)
    in_kinds = [k for _, k in ins]
    kinds = [k for _, _, k in outs]

    def kern(*refs):
        @pl.when(pl.program_id(0) == 0)
        def _():
            for r, kind in zip(refs[n_in:], kinds):
                if kind == "acc":
                    r[...] = jnp.zeros_like(r)

        wrapped = [_NatIn(r) if k == "nat" else r for r, k in zip(refs[:n_in], in_kinds)]
        wrapped += [_NatOut(r) if k == "nat" else r for r, k in zip(refs[n_in:], kinds)]
        body(*wrapped)

    in_specs, args = [], []
    for arr, kind in ins:
        if kind == "row":
            in_specs.append(pl.BlockSpec((tr, arr.shape[1]), lambda i: (i, 0)))
        elif kind == "nat":
            in_specs.append(pl.BlockSpec((N_SEG, tr // N_SEG, arr.shape[1]), lambda i: (0, i, 0)))
            arr = arr.reshape(N_SEG, n_rows // N_SEG, arr.shape[1])
        else:
            in_specs.append(pl.BlockSpec(arr.shape, lambda i, nd=arr.ndim: (0,) * nd))
        args.append(arr)
    out_specs, out_shape = [], []
    for width, dtype, kind in outs:
        if kind == "row":
            out_specs.append(pl.BlockSpec((tr, width), lambda i: (i, 0)))
            out_shape.append(jax.ShapeDtypeStruct((n_rows, width), dtype))
        elif kind == "right":
            out_specs.append(pl.BlockSpec((tr, width), lambda i: (i, 1)))
            out_shape.append(jax.ShapeDtypeStruct((n_rows, 2 * width), dtype))
        elif kind == "nat":
            out_specs.append(pl.BlockSpec((N_SEG, tr // N_SEG, width), lambda i: (0, i, 0)))
            out_shape.append(jax.ShapeDtypeStruct((N_SEG, n_rows // N_SEG, width), dtype))
        else:
            out_specs.append(pl.BlockSpec((1, width), lambda i: (0, 0)))
            out_shape.append(jax.ShapeDtypeStruct((1, width), F32))
    res = pl.pallas_call(
        kern, name=name, grid=(n_rows // tr,), in_specs=in_specs, out_specs=out_specs, out_shape=out_shape,
        compiler_params=_cparams("arbitrary"),
    )(*args)
    return [r.reshape(n_rows, r.shape[2]) if k == "nat" else r for r, k in zip(res, kinds)]


def _rstd(x):
    return lax.rsqrt(jnp.mean(x * x, axis=-1, keepdims=True) + RMS_EPS)


def _rms_bwd(x, g, dy):
    xh = x * _rstd(x)
    dxh = dy * g
    dx = _rstd(x) * (dxh - xh * jnp.mean(dxh * xh, axis=-1, keepdims=True))
    return dx, jnp.sum(dy * xh, axis=0, keepdims=True)


def _silu(z):
    return z * jax.nn.sigmoid(z)


def _norm_cast(x, g, name, x_kind="row"):
    def body(x_ref, g_ref, o_ref):
        x = x_ref[...]
        o_ref[...] = (x * _rstd(x) * g_ref[...]).astype(BF16)

    return _rowcall(body, name, x.shape[0], [(x, x_kind), (g, "full")], [(x.shape[1], BF16, "row")])[0]


def _resid_norm2(x, r0, g_kv, g_pre, name):
    def body(x_ref, r_ref, gk_ref, gp_ref, h_ref, nk_ref, np_ref):
        h = x_ref[...] + r_ref[...]
        h_ref[...] = h
        hn = h * _rstd(h)
        nk_ref[...] = (hn * gk_ref[...]).astype(BF16)
        np_ref[...] = (hn * gp_ref[...]).astype(BF16)

    d = x.shape[1]
    return _rowcall(body, name, x.shape[0], [(x, "row"), (r0, "row"), (g_kv, "full"), (g_pre, "full")],
                    [(d, F32, "row"), (d, BF16, "row"), (d, BF16, "row")])


def _post_norm(o, g, name, out_kind="row"):
    def body(o_ref, g_ref, r_ref):
        o = o_ref[...]
        r_ref[...] = o * _rstd(o) * g_ref[...]

    return _rowcall(body, name, o.shape[0], [(o, "row"), (g, "full")], [(o.shape[1], F32, out_kind)])[0]


def _post_norm_loss(o, g, h1, target, name):
    d = o.shape[1]

    def body(o_ref, g_ref, h_ref, t_ref, dh_ref, do_ref, acc_ref, dg_ref):
        o = o_ref[...]
        e = h_ref[...] + o * _rstd(o) * g_ref[...] - t_ref[...]
        dh = e * (1.0 / d)
        dh_ref[...] = dh
        acc_ref[...] += jnp.sum(e * e, axis=0, keepdims=True)
        dx, dg = _rms_bwd(o, g_ref[...], dh)
        do_ref[...] = dx.astype(BF16)
        dg_ref[...] += dg

    return _rowcall(body, name, o.shape[0], [(o, "row"), (g, "full"), (h1, "row"), (target, "row")],
                    [(d, F32, "row"), (d, BF16, "row"), (d, F32, "acc"), (d, F32, "acc")])


def _post_norm_bwd(dy, o, g, name, dy_kind="row"):
    def body(dy_ref, o_ref, g_ref, do_ref, dg_ref):
        dx, dg = _rms_bwd(o_ref[...], g_ref[...], dy_ref[...])
        do_ref[...] = dx.astype(BF16)
        dg_ref[...] += dg

    d = o.shape[1]
    return _rowcall(body, name, o.shape[0], [(dy, dy_kind), (o, "row"), (g, "full")], [(d, BF16, "row"), (d, F32, "acc")])


def _gate_mul(o, z, name):
    def body(o_ref, z_ref, r_ref):
        r_ref[...] = (o_ref[...] * _silu(z_ref[...])).astype(BF16)

    return _rowcall(body, name, o.shape[0], [(o, "row"), (z, "row")], [(o.shape[1], BF16, "row")])[0]


def _gate_bwd(d_oz, o, z, name):
    def body(d_ref, o_ref, z_ref, do_ref, dz_ref):
        _, vjp = jax.vjp(lambda o, z: o * _silu(z), o_ref[...], z_ref[...].astype(F32))
        do, dz = vjp(d_ref[...])
        do_ref[...] = do.astype(BF16)
        dz_ref[...] = dz.astype(BF16)

    w = o.shape[1]
    return _rowcall(body, name, o.shape[0], [(d_oz, "row"), (o, "row"), (z, "row")], [(w, BF16, "row"), (w, BF16, "right")])


def _norm_bwd2(dh2, h1, dxn1, dhn_kv, g_pre, g_kv, name):
    def body(dh2_ref, h_ref, d1_ref, dk_ref, gp_ref, gk_ref, dh1_ref, dgp_ref, dgk_ref):
        h = h_ref[...]
        dx1, dg1 = _rms_bwd(h, gp_ref[...], d1_ref[...])
        dxk, dgk = _rms_bwd(h, gk_ref[...], dk_ref[...])
        dh1_ref[...] = dh2_ref[...] + dx1 + dxk
        dgp_ref[...] += dg1
        dgk_ref[...] += dgk

    d = h1.shape[1]
    return _rowcall(body, name, h1.shape[0],
                    [(dh2, "row"), (h1, "row"), (dxn1, "row"), (dhn_kv, "row"), (g_pre, "full"), (g_kv, "full")],
                    [(d, F32, "row"), (d, F32, "acc"), (d, F32, "acc")])


def _norm_bwd1(dres, x, dxn, g, name):
    def body(dr_ref, x_ref, dn_ref, g_ref, dx_ref, dg_ref):
        dx, dg = _rms_bwd(x_ref[...], g_ref[...], dn_ref[...])
        dx_ref[...] = dr_ref[...] + dx
        dg_ref[...] += dg

    d = x.shape[1]
    return _rowcall(body, name, x.shape[0], [(dres, "nat"), (x, "nat"), (dxn, "row"), (g, "full")],
                    [(d, F32, "nat"), (d, F32, "acc")])


def _gelu_cast(y, name):
    def body(y_ref, o_ref):
        o_ref[...] = jax.nn.gelu(y_ref[...]).astype(BF16)

    return _rowcall(body, name, y.shape[0], [(y, "row")], [(y.shape[1], BF16, "row")])[0]


def _s5_gate(y_ssm, gp, b_glu, z, name):
    def body(y_ref, gp_ref, b_ref, z_ref, o_ref):
        yg = jax.nn.gelu(y_ref[...])
        o_ref[...] = (yg * jax.nn.sigmoid(gp_ref[...] + b_ref[...]) * _silu(z_ref[...].astype(F32))).astype(BF16)

    return _rowcall(body, name, y_ssm.shape[0], [(y_ssm, "row"), (gp, "row"), (b_glu, "full"), (z, "row")],
                    [(y_ssm.shape[1], BF16, "row")])[0]


def _s5_gate_bwd(dy3, y_ssm, gp, b_glu, z, name):
    def body(d_ref, y_ref, gp_ref, b_ref, z_ref, dz_ref, dgp_ref, dyg_ref, db_ref):
        yg = jax.nn.gelu(y_ref[...])
        _, vjp = jax.vjp(lambda yg, gp, z: yg * jax.nn.sigmoid(gp) * _silu(z), yg, gp_ref[...] + b_ref[...],
                         z_ref[...].astype(F32))
        dyg, dgp, dz = vjp(d_ref[...])
        dz_ref[...] = dz.astype(BF16)
        dgp_ref[...] = dgp.astype(BF16)
        dyg_ref[...] = dyg
        db_ref[...] += jnp.sum(dgp, axis=0, keepdims=True)

    w = y_ssm.shape[1]
    return _rowcall(body, name, y_ssm.shape[0],
                    [(dy3, "row"), (y_ssm, "row"), (gp, "row"), (b_glu, "full"), (z, "row")],
                    [(w, BF16, "right"), (w, BF16, "row"), (w, F32, "row"), (w, F32, "acc")])


def _gelu_bwd(dyg, y_ssm, name):
    def body(d_ref, y_ref, o_ref):
        _, vjp = jax.vjp(jax.nn.gelu, y_ref[...])
        o_ref[...] = vjp(d_ref[...])[0]

    return _rowcall(body, name, y_ssm.shape[0], [(dyg, "row"), (y_ssm, "row")], [(y_ssm.shape[1], F32, "row")])[0]


def _concat_cast(a, b, name):
    def body(a_ref, b_ref, o_ref):
        w = a_ref.shape[1]
        o_ref[:, :w] = a_ref[...].astype(BF16)
        o_ref[:, w:] = b_ref[...].astype(BF16)

    return _rowcall(body, name, a.shape[0], [(a, "row"), (b, "row")], [(a.shape[1] + b.shape[1], BF16, "row")])[0]


def _disc(ar, ai, ldt):
    dt = jnp.exp(ldt)
    mag = jnp.exp(ar * dt)
    abr = mag * jnp.cos(ai * dt)
    abi = mag * jnp.sin(ai * dt)
    den = ar * ar + ai * ai
    nr = abr - 1.0
    return abr, abi, (nr * ar + abi * ai) / den, (abi * ar - nr * ai) / den


def _s5_disc_fwd(a_re, a_im, ldt):
    def body(ar, ai, ld, o1, o2, o3, o4):
        o1[...], o2[...], o3[...], o4[...] = _disc(ar[...], ai[...], ld[...])

    sh = jax.ShapeDtypeStruct(a_re.shape, F32)
    return pl.pallas_call(body, name="s5_disc_fwd", out_shape=(sh, sh, sh, sh))(a_re, a_im, ldt)


def _s5_disc_bwd(a_re, a_im, ldt, d_abr, d_abi, d_cr, d_ci):
    def body(ar, ai, ld, g1, g2, g3, g4, o1, o2, o3):
        _, vjp = jax.vjp(_disc, ar[...], ai[...], ld[...])
        o1[...], o2[...], o3[...] = vjp((g1[...], g2[...], g3[...], g4[...]))

    sh = jax.ShapeDtypeStruct(a_re.shape, F32)
    return pl.pallas_call(body, name="s5_disc_bwd", out_shape=(sh, sh, jax.ShapeDtypeStruct(ldt.shape, F32)))(
        a_re, a_im, ldt, d_abr, d_abi, d_cr, d_ci)


def _bbar(cr, ci, br, bi):
    return cr * br - ci * bi, cr * bi + ci * br


def _s5_bbar_fwd(cr_col, ci_col, b_re, b_im):
    def body(cr, ci, br, bi, o1, o2):
        o1[...], o2[...] = _bbar(cr[...], ci[...], br[...], bi[...])

    w = b_re.shape[1]
    return _rowcall(body, "s5_bbar_fwd", b_re.shape[0], [(cr_col, "row"), (ci_col, "row"), (b_re, "row"), (b_im, "row")],
                    [(w, F32, "row"), (w, F32, "row")], tile_rows=1024)


def _s5_bbar_bwd(cr_col, ci_col, b_re, b_im, d_re, d_im):
    def body(cr, ci, br, bi, g1, g2, o1, o2, o3, o4):
        _, vjp = jax.vjp(_bbar, cr[...], ci[...], br[...], bi[...])
        o1[...], o2[...], o3[...], o4[...] = vjp((g1[...], g2[...]))

    w = b_re.shape[1]
    return _rowcall(body, "s5_bbar_bwd", b_re.shape[0],
                    [(cr_col, "row"), (ci_col, "row"), (b_re, "row"), (b_im, "row"), (d_re, "row"), (d_im, "row")],
                    [(1, F32, "row"), (1, F32, "row"), (w, F32, "row"), (w, F32, "row")], tile_rows=1024)


def _block_diag_in(t):
    g, p, c = t.shape
    nb = g // GROUPS_PER_BLOCK
    t4 = t.reshape(nb, GROUPS_PER_BLOCK, p, c).transpose(0, 1, 3, 2)
    eye = jnp.eye(GROUPS_PER_BLOCK, dtype=t.dtype)
    return (t4[:, :, :, None, :] * eye[None, :, None, :, None]).reshape(nb, GROUPS_PER_BLOCK * c, GROUPS_PER_BLOCK * p)


def _block_diag_in_extract(d, p, c):
    nb = d.shape[0]
    d5 = d.reshape(nb, GROUPS_PER_BLOCK, c, GROUPS_PER_BLOCK, p)
    diag = jnp.stack([d5[:, g, :, g, :] for g in range(GROUPS_PER_BLOCK)], axis=1)
    return diag.transpose(0, 1, 3, 2).reshape(nb * GROUPS_PER_BLOCK, p, c)


def _block_diag_out(t):
    g, c, p = t.shape
    nb = g // GROUPS_PER_BLOCK
    t4 = t.reshape(nb, GROUPS_PER_BLOCK, c, p).transpose(0, 1, 3, 2)
    eye = jnp.eye(GROUPS_PER_BLOCK, dtype=t.dtype)
    return (t4[:, :, :, None, :] * eye[None, :, None, :, None]).reshape(nb, GROUPS_PER_BLOCK * p, GROUPS_PER_BLOCK * c)


def _block_diag_out_extract(d, c, p):
    nb = d.shape[0]
    d5 = d.reshape(nb, GROUPS_PER_BLOCK, p, GROUPS_PER_BLOCK, c)
    diag = jnp.stack([d5[:, g, :, g, :] for g in range(GROUPS_PER_BLOCK)], axis=1)
    return diag.transpose(0, 1, 3, 2).reshape(nb * GROUPS_PER_BLOCK, c, p)


def _scan_step(ar, ai, hr, hi, xr, xi):
    return ar * hr - ai * hi + xr, ar * hi + ai * hr + xi


def _s5_scan_fwd(u, bd_re, bd_im, cd_re, cd_im, ab_re, ab_im, init_re, init_im, d_row, full, name):
    s, w = u.shape
    nb = w // LANES
    rows = _tile(s, 512, SUBLANES)
    nc = s // rows
    steps = rows // N_SEG
    ns = nb * BLOCK_STATE

    def body(u_ref, bdr, bdi, cdr, cdi, ar_ref, ai_ref, ir_ref, ii_ref, d_ref, *outs):
        if full:
            y_ref, yg_ref, hr_ref, hi_ref, er_ref, ei_ref, cr, ci = outs
        else:
            er_ref, ei_ref, hr_ref, hi_ref, cr, ci = outs
        c = pl.program_id(1)

        @pl.when(c == 0)
        def _():
            cr[...] = ir_ref[...]
            ci[...] = ii_ref[...]

        ub = u_ref[...].astype(BF16)
        hr_ref[...] = jnp.dot(ub, bdr[...], preferred_element_type=F32)
        hi_ref[...] = jnp.dot(ub, bdi[...], preferred_element_type=F32)
        ar, ai = ar_ref[...], ai_ref[...]

        hr, hi = cr[...], ci[...]
        for j in range(steps):
            rows_j = pl.ds(j * N_SEG, N_SEG)
            hr, hi = _scan_step(ar, ai, hr, hi, hr_ref[rows_j, :], hi_ref[rows_j, :])
            hr_ref[rows_j, :] = hr
            hi_ref[rows_j, :] = hi
        cr[...] = hr
        ci[...] = hi
        if full:
            y = (jnp.dot(hr_ref[...].astype(BF16), cdr[...], preferred_element_type=F32)
                 + jnp.dot(hi_ref[...].astype(BF16), cdi[...], preferred_element_type=F32)
                 + d_ref[...] * u_ref[...])
            y_ref[...] = y
            yg_ref[...] = jax.nn.gelu(y).astype(BF16)

        @pl.when(c == nc - 1)
        def _():
            er_ref[...] = hr
            ei_ref[...] = hi

    blk3 = lambda a: pl.BlockSpec((None,) + a.shape[1:], lambda k, c: (k, 0, 0))
    seg = pl.BlockSpec((N_SEG, BLOCK_STATE), lambda k, c: (0, k))
    st = pl.BlockSpec((rows, BLOCK_STATE), lambda k, c: (c, k))
    in_specs = [pl.BlockSpec((rows, LANES), lambda k, c: (c, k)), blk3(bd_re), blk3(bd_im), blk3(cd_re), blk3(cd_im),
                seg, seg, seg, seg, pl.BlockSpec((1, LANES), lambda k, c: (0, k))]
    seg_shape = jax.ShapeDtypeStruct((N_SEG, ns), F32)
    st_shape = jax.ShapeDtypeStruct((s, ns), F32)
    carry = [pltpu.VMEM((N_SEG, BLOCK_STATE), F32)] * 2
    if full:
        ych = pl.BlockSpec((rows, LANES), lambda k, c: (c, k))
        out_specs = [ych, ych, st, st, seg, seg]
        out_shape = [jax.ShapeDtypeStruct((s, w), F32), jax.ShapeDtypeStruct((s, w), BF16), st_shape, st_shape, seg_shape, seg_shape]
        scratch = carry
    else:
        out_specs = [seg, seg]
        out_shape = [seg_shape, seg_shape]
        scratch = [pltpu.VMEM((rows, BLOCK_STATE), F32)] * 2 + carry
    return pl.pallas_call(
        body, name=name, grid=(nb, nc), in_specs=in_specs, out_specs=out_specs, out_shape=out_shape,
        scratch_shapes=scratch, compiler_params=_cparams("parallel", "arbitrary"),
    )(u, bd_re, bd_im, cd_re, cd_im, ab_re, ab_im, init_re, init_im, d_row)


def _s5_seg_fix(e_re, e_im, ab_re, ab_im, seg_len, reverse, name):
    assert seg_len & (seg_len - 1) == 0

    def body(er, ei, ar, ai, o_re, o_im):
        pr, pi = ar[0:1, :], ai[0:1, :]
        for _ in range(int(math.log2(seg_len))):
            pr, pi = pr * pr - pi * pi, 2.0 * pr * pi
        tr = jnp.zeros_like(pr)
        ti = jnp.zeros_like(pr)
        order = list(range(N_SEG - 1, -1, -1)) if reverse else list(range(N_SEG))
        for n, sgm in enumerate(order):
            o_re[sgm:sgm + 1, :] = tr
            o_im[sgm:sgm + 1, :] = ti
            if n < N_SEG - 1:
                tr, ti = _scan_step(pr, pi, tr, ti, er[sgm:sgm + 1, :], ei[sgm:sgm + 1, :])

    sh = jax.ShapeDtypeStruct(e_re.shape, F32)
    return pl.pallas_call(body, name=name, out_shape=(sh, sh))(e_re, e_im, ab_re, ab_im)


def _s5_scan_bwd(dy, u, h_re, h_im, bd_re, bd_im, cd_re, cd_im, ab_re, ab_imn, gin_re, gin_im, d_row, full, name, duz=None):
    s, w = u.shape
    nb = w // LANES
    rows = _tile(s, 512, SUBLANES)
    nc = s // rows
    steps = rows // N_SEG
    ns = nb * BLOCK_STATE

    def body(dy_ref, u_ref, hr_ref, hi_ref, bdr, bdi, cdr, cdi, ar_ref, ai_ref, ir_ref, ii_ref, d_ref, *outs):
        if full:
            _, du_ref, dbr_ref, dbi_ref, dcr_ref, dci_ref, dar_ref, dai_ref, dd_ref, gr, gi, accr, acci = outs
        else:
            er_ref, ei_ref, gr, gi = outs
        c = pl.program_id(1)

        @pl.when(c == 0)
        def _():
            gr[pl.ds(rows, N_SEG), :] = ir_ref[...]
            gi[pl.ds(rows, N_SEG), :] = ii_ref[...]
            if full:
                for r in (dbr_ref, dbi_ref, dcr_ref, dci_ref, dd_ref, accr, acci):
                    r[...] = jnp.zeros_like(r)

        dyb = dy_ref[...].astype(BF16)
        nt = (_DOT_DIMS["nt"], ((), ()))
        tn = (_DOT_DIMS["tn"], ((), ()))
        gr[pl.ds(0, rows), :] = lax.dot_general(dyb, cdr[...], nt, preferred_element_type=F32)
        gi[pl.ds(0, rows), :] = lax.dot_general(dyb, cdi[...], nt, preferred_element_type=F32)
        ar, ai = ar_ref[...], ai_ref[...]

        g0r, g0i = gr[pl.ds(rows, N_SEG), :], gi[pl.ds(rows, N_SEG), :]
        for j in range(steps - 1, -1, -1):
            rows_j = pl.ds(j * N_SEG, N_SEG)
            g0r, g0i = _scan_step(ar, ai, g0r, g0i, gr[rows_j, :], gi[rows_j, :])
            gr[rows_j, :] = g0r
            gi[rows_j, :] = g0i
        if full:
            hr, hi = hr_ref[...], hi_ref[...]
            gnr, gni = gr[pl.ds(N_SEG, rows), :], gi[pl.ds(N_SEG, rows), :]
            accr[...] += jnp.sum((gnr * hr + gni * hi).reshape(steps, N_SEG, BLOCK_STATE), axis=0)
            acci[...] += jnp.sum((gni * hr - gnr * hi).reshape(steps, N_SEG, BLOCK_STATE), axis=0)
        gr[pl.ds(rows, N_SEG), :] = g0r
        gi[pl.ds(rows, N_SEG), :] = g0i
        if full:
            ub = u_ref[...].astype(BF16)
            gbr, gbi = gr[pl.ds(0, rows), :].astype(BF16), gi[pl.ds(0, rows), :].astype(BF16)
            dcr_ref[...] += lax.dot_general(hr.astype(BF16), dyb, tn, preferred_element_type=F32)
            dci_ref[...] += lax.dot_general(hi.astype(BF16), dyb, tn, preferred_element_type=F32)
            dbr_ref[...] += lax.dot_general(ub, gbr, tn, preferred_element_type=F32)
            dbi_ref[...] += lax.dot_general(ub, gbi, tn, preferred_element_type=F32)
            du_ref[...] = (lax.dot_general(gbr, bdr[...], nt, preferred_element_type=F32)
                           + lax.dot_general(gbi, bdi[...], nt, preferred_element_type=F32)
                           + d_ref[...] * dy_ref[...]).astype(BF16)
            dd_ref[...] += jnp.sum(dy_ref[...] * u_ref[...], axis=0, keepdims=True)

        @pl.when(c == nc - 1)
        def _():
            if full:
                dar_ref[...] = jnp.sum(accr[...], axis=0, keepdims=True)
                dai_ref[...] = jnp.sum(acci[...], axis=0, keepdims=True)
            else:
                er_ref[...] = g0r
                ei_ref[...] = g0i

    rev = lambda k, c: (nc - 1 - c, k)
    blk3 = lambda a: pl.BlockSpec((None,) + a.shape[1:], lambda k, c: (k, 0, 0))
    seg = pl.BlockSpec((N_SEG, BLOCK_STATE), lambda k, c: (0, k))
    st = pl.BlockSpec((rows, BLOCK_STATE), rev)
    ch = pl.BlockSpec((rows, LANES), rev)
    vec = pl.BlockSpec((1, LANES), lambda k, c: (0, k))
    if not full:
        st = pl.BlockSpec((rows, BLOCK_STATE), lambda k, c: (0, k))
    in_specs = [ch, ch if full else pl.BlockSpec((rows, LANES), lambda k, c: (0, k)), st, st,
                blk3(bd_re), blk3(bd_im), blk3(cd_re), blk3(cd_im), seg, seg, seg, seg, vec]
    args = [dy, u, h_re, h_im, bd_re, bd_im, cd_re, cd_im, ab_re, ab_imn, gin_re, gin_im, d_row]
    gbuf = [pltpu.VMEM((rows + N_SEG, BLOCK_STATE), F32)] * 2
    if full:
        row1 = pl.BlockSpec((1, BLOCK_STATE), lambda k, c: (0, k))
        out_specs = [ch, blk3(bd_re), blk3(bd_im), blk3(cd_re), blk3(cd_im), row1, row1, vec]
        out_shape = [jax.ShapeDtypeStruct(duz.shape, BF16),
                     jax.ShapeDtypeStruct(bd_re.shape, F32), jax.ShapeDtypeStruct(bd_im.shape, F32),
                     jax.ShapeDtypeStruct(cd_re.shape, F32), jax.ShapeDtypeStruct(cd_im.shape, F32),
                     jax.ShapeDtypeStruct((1, ns), F32), jax.ShapeDtypeStruct((1, ns), F32),
                     jax.ShapeDtypeStruct((1, w), F32)]
        scratch = gbuf + [pltpu.VMEM((N_SEG, BLOCK_STATE), F32)] * 2
        in_specs.append(pl.BlockSpec(memory_space=pl.ANY))
        args.append(duz)
        aliases = {len(args) - 1: 0}
    else:
        out_specs = [seg, seg]
        out_shape = [jax.ShapeDtypeStruct((N_SEG, ns), F32)] * 2
        scratch = gbuf
        aliases = {}
    return pl.pallas_call(
        body, name=name, grid=(nb, nc), in_specs=in_specs, out_specs=out_specs, out_shape=out_shape,
        input_output_aliases=aliases, scratch_shapes=scratch, compiler_params=_cparams("parallel", "arbitrary"),
    )(*args)


def _log_sigmoid(x):
    return jnp.minimum(x, 0.0) - jnp.log(1.0 + jnp.exp(-jnp.abs(x)))


def _tri(n, upper):
    r = lax.broadcasted_iota(jnp.int32, (n, n), 0)
    c = lax.broadcasted_iota(jnp.int32, (n, n), 1)
    return jnp.where((c >= r) if upper else (r >= c), 1.0, 0.0).astype(F32)


def _cum_fwd(f_logit, b_row, name):
    s, w = f_logit.shape
    t = _tile(s, 256, SUBLANES)

    def body(f_ref, b_ref, o_ref, carry):
        @pl.when(pl.program_id(0) == 0)
        def _():
            carry[...] = jnp.zeros_like(carry)

        lf = _log_sigmoid(f_ref[...] + b_ref[...])
        cum = jnp.dot(_tri(t, False), lf, precision=lax.Precision.HIGHEST, preferred_element_type=F32) + carry[...]
        o_ref[...] = cum * LOG2E
        carry[...] = cum[t - 1:t, :]

    return pl.pallas_call(
        body, name=name, grid=(s // t,),
        in_specs=[pl.BlockSpec((t, w), lambda i: (i, 0)), pl.BlockSpec((1, w), lambda i: (0, 0))],
        out_specs=pl.BlockSpec((t, w), lambda i: (i, 0)), out_shape=jax.ShapeDtypeStruct((s, w), F32),
        scratch_shapes=[pltpu.VMEM((1, w), F32)], compiler_params=_cparams("arbitrary"),
    )(f_logit, b_row)


def _cum_bwd(dcq, dck, f_logit, b_row, name):
    s, w = f_logit.shape
    t = _tile(s, 256, SUBLANES)
    nt = s // t

    def body(q_ref, k_ref, f_ref, b_ref, df_ref, db_ref, carry):
        @pl.when(pl.program_id(0) == 0)
        def _():
            carry[...] = jnp.zeros_like(carry)
            db_ref[...] = jnp.zeros_like(db_ref)

        dc = q_ref[...] - k_ref[...]
        rc = jnp.dot(_tri(t, True), dc, precision=lax.Precision.HIGHEST, preferred_element_type=F32) + carry[...]
        carry[...] = rc[0:1, :]
        df = rc * (1.0 - jax.nn.sigmoid(f_ref[...] + b_ref[...]))
        df_ref[...] = df.astype(BF16)
        db_ref[...] += jnp.sum(df, axis=0, keepdims=True)

    rev = pl.BlockSpec((t, w), lambda i: (nt - 1 - i, 0))
    one = pl.BlockSpec((1, w), lambda i: (0, 0))
    return pl.pallas_call(
        body, name=name, grid=(nt,), in_specs=[rev, rev, rev, one], out_specs=[rev, one],
        out_shape=[jax.ShapeDtypeStruct((s, w), BF16), jax.ShapeDtypeStruct((1, w), F32)],
        scratch_shapes=[pltpu.VMEM((1, w), F32)], compiler_params=_cparams("arbitrary"),
    )(dcq, dck, f_logit, b_row)


def _head_col(cum_tile, h):
    lane = lax.broadcasted_iota(jnp.int32, cum_tile.shape, 1)
    return jnp.sum(jnp.where(lane == h, cum_tile, 0.0), axis=1, keepdims=True)


def _attn_tiles(s):
    return _tile(s, 512, LANES)


def _exp2_rows(sc, sub):
    return jnp.concatenate([jnp.exp2(sc[:, b * LANES:(b + 1) * LANES] - sub) for b in range(sc.shape[1] // LANES)], axis=1)


def _row_of(rep):
    return jnp.transpose(rep)[0:1, :]


def _causal(sc, keys_on_rows):
    r = lax.broadcasted_iota(jnp.int32, sc.shape, 0)
    c = lax.broadcasted_iota(jnp.int32, sc.shape, 1)
    return jnp.where((r <= c) if keys_on_rows else (c <= r), sc, NEG_INF)


def _fox_fwd(q2, kv, cum2_t, z, name):
    s, w = q2.shape
    nh = w // HEAD_DIM
    tq = _attn_tiles(s)
    nq = s // tq
    nt = (_DOT_DIMS["nt"], ((), ()))

    def body(q_ref, k_ref, v_ref, ct_ref, z_ref, o_ref, oz_ref, lse_row_ref, m_s, acc_s, vaug, s_buf):
        i = pl.program_id(1)

        @pl.when(i == 0)
        def _():
            vaug[:, :HEAD_DIM] = v_ref[...]
            vaug[:, HEAD_DIM:] = jnp.ones((s, LANES), BF16)

        qb = q_ref[...]
        m_s[...] = jnp.full_like(m_s, NEG_INF)
        acc_s[...] = jnp.zeros_like(acc_s)

        def scores(j):
            off = pl.multiple_of(j * tq, tq)
            return lax.dot_general(qb, k_ref[pl.ds(off, tq), :], nt, preferred_element_type=F32) - ct_ref[:, pl.ds(off, tq)]

        def softmax_pv(j, sc):
            m_old = m_s[...]
            m_new = jnp.maximum(m_old, jnp.max(sc, axis=1, keepdims=True))
            p = _exp2_rows(sc, m_new)
            alpha = jnp.exp2(m_old - m_new)
            pv = jnp.dot(p.astype(BF16), vaug[pl.ds(pl.multiple_of(j * tq, tq), tq), :], preferred_element_type=F32)
            acc_s[...] = jnp.concatenate([alpha, alpha], axis=1) * acc_s[...] + pv
            m_s[...] = m_new

        s_buf[...] = scores(0)

        def loop(j, carry):
            nxt = scores(j + 1)
            softmax_pv(j, s_buf[...])
            s_buf[...] = nxt
            return carry

        lax.fori_loop(0, i, loop, 0)
        softmax_pv(i, _causal(s_buf[...], False))
        l = acc_s[:, HEAD_DIM:]
        o = acc_s[:, :HEAD_DIM] / l
        o_ref[...] = o
        oz_ref[...] = (o * _silu(z_ref[...].astype(F32))).astype(BF16)
        lse_row_ref[...] = _row_of(m_s[...] + jnp.log(l) * LOG2E)

    return pl.pallas_call(
        body, name=name, grid=(nh, nq),
        in_specs=[pl.BlockSpec((tq, HEAD_DIM), lambda h, i: (i, h)),
                  pl.BlockSpec((s, HEAD_DIM), lambda h, i: (0, h)),
                  pl.BlockSpec((s, HEAD_DIM), lambda h, i: (0, nh + h)),
                  pl.BlockSpec((None, 1, s), lambda h, i: (h, 0, 0)),
                  pl.BlockSpec((tq, HEAD_DIM), lambda h, i: (i, h))],
        out_specs=[pl.BlockSpec((tq, HEAD_DIM), lambda h, i: (i, h)),
                   pl.BlockSpec((tq, HEAD_DIM), lambda h, i: (i, h)),
                   pl.BlockSpec((None, 1, tq), lambda h, i: (h, 0, i))],
        out_shape=[jax.ShapeDtypeStruct((s, w), F32), jax.ShapeDtypeStruct((s, w), BF16),
                   jax.ShapeDtypeStruct((nh, 1, s), F32)],
        scratch_shapes=[pltpu.VMEM((tq, LANES), F32), pltpu.VMEM((tq, HEAD_DIM + LANES), F32),
                        pltpu.VMEM((s, HEAD_DIM + LANES), BF16), pltpu.VMEM((tq, tq), F32)],
        compiler_params=_cparams("arbitrary", "arbitrary"),
    )(q2, kv, kv, cum2_t, z)


def _fox_bwd(q2, kv, do, o, lse2_t, cum2, dqz, name):
    s, w = q2.shape
    nh = w // HEAD_DIM
    tk = _attn_tiles(s)
    nk = s // tk
    scale = HEAD_DIM ** -0.5
    nt = (_DOT_DIMS["nt"], ((), ()))
    tn = (_DOT_DIMS["tn"], ((), ()))

    def body(q_ref, k_ref, v_ref, do_ref, o_ref, lse_ref, c_ref, _, dk_ref, dv_ref, dq_ref, dcq_ref, dck_ref,
             dk_s, dv_s, dc_s, dq_s, dcq_s, dl_s, s_buf, dp_buf):
        h, j = pl.program_id(0), pl.program_id(1)

        @pl.when(j == 0)
        def _():
            dq_s[...] = jnp.zeros_like(dq_s)
            dcq_s[...] = jnp.zeros_like(dcq_s)
            for i in range(nk):
                rows = pl.ds(i * tk, tk)
                d = jnp.sum(do_ref[rows, :].astype(F32) * o_ref[rows, :], axis=1, keepdims=True)
                dl_s[:, i * tk:(i + 1) * tk] = _row_of(jnp.broadcast_to(d, (tk, LANES)))

        kb = k_ref[...]
        vb = v_ref[...]
        ck = jnp.broadcast_to(_head_col(c_ref[...], h), (tk, LANES))
        dk_s[...] = jnp.zeros_like(dk_s)
        dv_s[...] = jnp.zeros_like(dv_s)
        dc_s[...] = jnp.zeros_like(dc_s)

        def scores(i):
            off = pl.multiple_of(i * tk, tk)
            sc = lax.dot_general(kb, q_ref[pl.ds(off, tk), :], nt, preferred_element_type=F32) - lse_ref[:, pl.ds(off, tk)]
            dp = lax.dot_general(vb, do_ref[pl.ds(off, tk), :], nt, preferred_element_type=F32) - dl_s[:, pl.ds(off, tk)]
            return sc, dp

        def accumulate(i, sc, dp):
            off = pl.multiple_of(i * tk, tk)
            p = _exp2_rows(sc, ck)
            dv_s[...] += jnp.dot(p.astype(BF16), do_ref[pl.ds(off, tk), :], preferred_element_type=F32)
            ds = p * dp
            dsb = ds.astype(BF16)
            dk_s[...] += jnp.dot(dsb, q_ref[pl.ds(off, tk), :], preferred_element_type=F32)
            dq_s[pl.ds(off, tk), :] += lax.dot_general(dsb, kb, tn, preferred_element_type=F32)
            dcq_s[:, pl.ds(off, tk)] += jnp.sum(ds, axis=0, keepdims=True)
            part = ds[:, :LANES]
            for b in range(1, tk // LANES):
                part = part + ds[:, b * LANES:(b + 1) * LANES]
            dc_s[...] += part

        sc0, dp0 = scores(j)
        s_buf[...] = _causal(sc0, True)
        dp_buf[...] = dp0

        def loop(i, carry):
            nxt = scores(i + 1)
            accumulate(i, s_buf[...], dp_buf[...])
            s_buf[...], dp_buf[...] = nxt
            return carry

        lax.fori_loop(j, nk - 1, loop, 0)
        accumulate(nk - 1, s_buf[...], dp_buf[...])
        dk_ref[...] = (dk_s[...] * (1.0 / LOG2E)).astype(BF16)
        dv_ref[...] = dv_s[...].astype(BF16)
        dck_ref[...] = jnp.sum(jnp.transpose(dc_s[...]), axis=0, keepdims=True)

        @pl.when(j == nk - 1)
        def _():
            dq_ref[...] = (dq_s[...] * scale).astype(BF16)
            dcq_ref[...] = dcq_s[...]

    col = pl.BlockSpec((s, HEAD_DIM), lambda h, j: (0, h))
    row = pl.BlockSpec((None, 1, s), lambda h, j: (h, 0, 0))
    kspec = pl.BlockSpec((tk, HEAD_DIM), lambda h, j: (j, h))
    return pl.pallas_call(
        body, name=name, grid=(nh, nk),
        in_specs=[col, kspec, pl.BlockSpec((tk, HEAD_DIM), lambda h, j: (j, nh + h)), col, col, row,
                  pl.BlockSpec((tk, LANES), lambda h, j: (j, 0)), pl.BlockSpec(memory_space=pl.ANY)],
        out_specs=[kspec, kspec, col, row, pl.BlockSpec((None, 1, tk), lambda h, j: (h, 0, j))],
        out_shape=[jax.ShapeDtypeStruct((s, w), BF16), jax.ShapeDtypeStruct((s, w), BF16),
                   jax.ShapeDtypeStruct(dqz.shape, BF16), jax.ShapeDtypeStruct((nh, 1, s), F32),
                   jax.ShapeDtypeStruct((nh, 1, s), F32)],
        input_output_aliases={7: 2},
        scratch_shapes=[pltpu.VMEM((tk, HEAD_DIM), F32), pltpu.VMEM((tk, HEAD_DIM), F32), pltpu.VMEM((tk, LANES), F32),
                        pltpu.VMEM((s, HEAD_DIM), F32), pltpu.VMEM((1, s), F32), pltpu.VMEM((1, s), F32),
                        pltpu.VMEM((tk, tk), F32), pltpu.VMEM((tk, tk), F32)],
        compiler_params=_cparams("arbitrary", "arbitrary"),
    )(q2, kv, kv, do, o, lse2_t, cum2, dqz)


def _fox_bwd_dq(q2, kv, do, o, lse2, cum2_t, dqz, name):
    s, w = q2.shape
    nh = w // HEAD_DIM
    tq = _attn_tiles(s)
    nq = s // tq
    scale = HEAD_DIM ** -0.5
    nt = (_DOT_DIMS["nt"], ((), ()))

    def body(q_ref, k_ref, v_ref, do_ref, o_ref, lse_ref, ct_ref, _, dq_ref, dl_ref, dcq_ref, acc_s, dc_s):
        i = pl.program_id(1)
        qb = q_ref[...]
        dob = do_ref[...]
        lse = lse_ref[...]
        delta = jnp.broadcast_to(jnp.sum(dob.astype(F32) * o_ref[...], axis=1, keepdims=True), (tq, LANES))
        acc_s[...] = jnp.zeros_like(acc_s)
        dc_s[...] = jnp.zeros_like(dc_s)

        def tile(j, masked):
            off = pl.multiple_of(j * tq, tq)
            kb = k_ref[pl.ds(off, tq), :]
            sc = lax.dot_general(qb, kb, nt, preferred_element_type=F32) - ct_ref[:, pl.ds(off, tq)]
            if masked:
                sc = _causal(sc, False)
            p = _exp2_rows(sc, lse)
            dp = lax.dot_general(dob, v_ref[pl.ds(off, tq), :], nt, preferred_element_type=F32)
            ds = p * (dp - jnp.concatenate([delta] * (tq // LANES), axis=1))
            acc_s[...] += jnp.dot(ds.astype(BF16), kb, preferred_element_type=F32)
            part = ds[:, :LANES]
            for b in range(1, tq // LANES):
                part = part + ds[:, b * LANES:(b + 1) * LANES]
            dc_s[...] += part

        def loop(j, carry):
            tile(j, False)
            return carry

        lax.fori_loop(0, i, loop, 0)
        tile(i, True)
        dq_ref[...] = (acc_s[...] * scale).astype(BF16)
        dl_ref[...] = _row_of(delta)
        dcq_ref[...] = jnp.sum(jnp.transpose(dc_s[...]), axis=0, keepdims=True)

    qspec = pl.BlockSpec((tq, HEAD_DIM), lambda h, i: (i, h))
    rep = pl.BlockSpec((None, tq, LANES), lambda h, i: (h, i, 0))
    rowspec = pl.BlockSpec((None, 1, tq), lambda h, i: (h, 0, i))
    return pl.pallas_call(
        body, name=name, grid=(nh, nq),
        in_specs=[qspec,
                  pl.BlockSpec((s, HEAD_DIM), lambda h, i: (0, h)),
                  pl.BlockSpec((s, HEAD_DIM), lambda h, i: (0, nh + h)),
                  qspec, qspec, rep,
                  pl.BlockSpec((None, 1, s), lambda h, i: (h, 0, 0)),
                  pl.BlockSpec(memory_space=pl.ANY)],
        out_specs=[qspec, rowspec, rowspec],
        out_shape=[jax.ShapeDtypeStruct(dqz.shape, BF16), jax.ShapeDtypeStruct((nh, 1, s), F32),
                   jax.ShapeDtypeStruct((nh, 1, s), F32)],
        input_output_aliases={7: 0},
        scratch_shapes=[pltpu.VMEM((tq, HEAD_DIM), F32), pltpu.VMEM((tq, LANES), F32)],
        compiler_params=_cparams("parallel", "arbitrary"),
    )(q2, kv, kv, do, o, lse2, cum2_t, dqz)


def _fox_bwd_dkv(q2, kv, do, lse2_t, delta_t, cum2, name):
    s, w = q2.shape
    nh = w // HEAD_DIM
    tk = _attn_tiles(s)
    nk = s // tk
    nt = (_DOT_DIMS["nt"], ((), ()))

    def body(q_ref, k_ref, v_ref, do_ref, lse_ref, dl_ref, c_ref, dk_ref, dv_ref, dck_ref, dk_s, dv_s, dc_s, s_buf, dp_buf):
        h, j = pl.program_id(0), pl.program_id(1)
        kb = k_ref[...]
        vb = v_ref[...]
        ck = jnp.broadcast_to(_head_col(c_ref[...], h), (tk, LANES))
        dk_s[...] = jnp.zeros_like(dk_s)
        dv_s[...] = jnp.zeros_like(dv_s)
        dc_s[...] = jnp.zeros_like(dc_s)

        def scores(i):
            off = pl.multiple_of(i * tk, tk)
            sc = lax.dot_general(kb, q_ref[pl.ds(off, tk), :], nt, preferred_element_type=F32) - lse_ref[:, pl.ds(off, tk)]
            dp = lax.dot_general(vb, do_ref[pl.ds(off, tk), :], nt, preferred_element_type=F32) - dl_ref[:, pl.ds(off, tk)]
            return sc, dp

        def accumulate(i, sc, dp):
            off = pl.multiple_of(i * tk, tk)
            p = _exp2_rows(sc, ck)
            dv_s[...] += jnp.dot(p.astype(BF16), do_ref[pl.ds(off, tk), :], preferred_element_type=F32)
            ds = p * dp
            dk_s[...] += jnp.dot(ds.astype(BF16), q_ref[pl.ds(off, tk), :], preferred_element_type=F32)
            part = ds[:, :LANES]
            for b in range(1, tk // LANES):
                part = part + ds[:, b * LANES:(b + 1) * LANES]
            dc_s[...] += part

        sc0, dp0 = scores(j)
        s_buf[...] = _causal(sc0, True)
        dp_buf[...] = dp0

        def loop(i, carry):
            nxt = scores(i + 1)
            accumulate(i, s_buf[...], dp_buf[...])
            s_buf[...], dp_buf[...] = nxt
            return carry

        lax.fori_loop(j, nk - 1, loop, 0)
        accumulate(nk - 1, s_buf[...], dp_buf[...])
        dk_ref[...] = (dk_s[...] * (1.0 / LOG2E)).astype(BF16)
        dv_ref[...] = dv_s[...].astype(BF16)
        dck_ref[...] = jnp.sum(jnp.transpose(dc_s[...]), axis=0, keepdims=True)

    col = pl.BlockSpec((s, HEAD_DIM), lambda h, j: (0, h))
    row = pl.BlockSpec((None, 1, s), lambda h, j: (h, 0, 0))
    kspec = pl.BlockSpec((tk, HEAD_DIM), lambda h, j: (j, h))
    return pl.pallas_call(
        body, name=name, grid=(nh, nk),
        in_specs=[col, kspec, pl.BlockSpec((tk, HEAD_DIM), lambda h, j: (j, nh + h)), col, row, row,
                  pl.BlockSpec((tk, LANES), lambda h, j: (j, 0))],
        out_specs=[kspec, kspec, pl.BlockSpec((None, 1, tk), lambda h, j: (h, 0, j))],
        out_shape=[jax.ShapeDtypeStruct((s, w), BF16), jax.ShapeDtypeStruct((s, w), BF16),
                   jax.ShapeDtypeStruct((nh, 1, s), F32)],
        scratch_shapes=[pltpu.VMEM((tk, HEAD_DIM), F32), pltpu.VMEM((tk, HEAD_DIM), F32),
                        pltpu.VMEM((tk, LANES), F32), pltpu.VMEM((tk, tk), F32), pltpu.VMEM((tk, tk), F32)],
        compiler_params=_cparams("parallel", "arbitrary"),
    )(q2, kv, kv, do, lse2_t, delta_t, cum2)


_ALL_PEERS = tuple(range(1, N_DEV))
_CHIP_PEERS = (1, 2, 4, 6)


def _exchange_copies(ins, outs, send_sems, recv_sems, local_sems, scatter, peers=_ALL_PEERS):
    x, y, c = (lax.axis_index(a) for a in MESH_AXES)
    me = 4 * x + 2 * y + c
    local, remote = [], []
    for a in range(len(ins)):
        local.append(pltpu.make_async_copy(ins[a].at[me] if scatter else ins[a], outs[a].at[me], local_sems.at[a]))
        for k in peers:
            px, py, pc = (1 - x if k & 4 else x), (1 - y if k & 2 else y), (1 - c if k & 1 else c)
            remote.append(pltpu.make_async_remote_copy(
                src_ref=ins[a].at[4 * px + 2 * py + pc] if scatter else ins[a], dst_ref=outs[a].at[me],
                send_sem=send_sems.at[a * (N_DEV - 1) + k - 1], recv_sem=recv_sems.at[a * (N_DEV - 1) + k - 1],
                device_id=(px, py, pc), device_id_type=pl.DeviceIdType.MESH))
    return local, remote


def _exchange_out_shapes(arrs, scatter):
    return [((N_DEV,) + a.shape[1:]) if scatter else ((N_DEV,) + a.shape) for a in arrs]


def _exchange(arrs, scatter, name, peers=_ALL_PEERS):
    n = len(arrs)

    def body(*refs):
        local, remote = _exchange_copies(refs[:n], refs[n:2 * n], *refs[2 * n:], scatter, peers)
        for cp in local + remote:
            cp.start()
        for cp in remote:
            cp.wait_send()
            cp.wait_recv()
        for cp in local:
            cp.wait()

    out_shape = [jax.ShapeDtypeStruct(s, a.dtype) for s, a in zip(_exchange_out_shapes(arrs, scatter), arrs)]
    return pl.pallas_call(
        body, name=name, out_shape=out_shape,
        in_specs=[pl.BlockSpec(memory_space=pl.ANY)] * n, out_specs=[pl.BlockSpec(memory_space=pl.ANY)] * n,
        scratch_shapes=[pltpu.SemaphoreType.DMA((n * (N_DEV - 1),)), pltpu.SemaphoreType.DMA((n * (N_DEV - 1),)),
                        pltpu.SemaphoreType.DMA((n,))],
    )(*arrs)


_HBM = pl.BlockSpec(memory_space=pltpu.HBM)
_SEM = pl.BlockSpec(memory_space=pltpu.SEMAPHORE)


def _exchange_start(arrs, scatter, name, after=(), peers=_ALL_PEERS):
    n = len(arrs)
    after = list(after)
    lands = [lax.empty(s, a.dtype) for s, a in zip(_exchange_out_shapes(arrs, scatter), arrs)]

    def body(*refs):
        ins, outs = refs[:n], refs[n:2 * n]
        send_sems, recv_sems, local_sems = refs[2 * n + len(after):2 * n + len(after) + 3]
        token = refs[-1]
        local, remote = _exchange_copies(ins, outs, send_sems, recv_sems, local_sems, scatter, peers)
        for cp in local + remote:
            cp.start()
        token[...] = jnp.zeros_like(token)

    hbm = lambda a: pltpu.HBM(a.shape, a.dtype)
    res = pl.pallas_call(
        body, name=name,
        out_shape=(pltpu.SemaphoreType.DMA((n * (N_DEV - 1),)), pltpu.SemaphoreType.DMA((n * (N_DEV - 1),)),
                   pltpu.SemaphoreType.DMA((n,)), *[hbm(a) for a in arrs], *[hbm(a) for a in lands],
                   jax.ShapeDtypeStruct((SUBLANES, LANES), F32)),
        in_specs=[_HBM] * (2 * n) + [pl.BlockSpec(memory_space=pl.ANY)] * len(after),
        out_specs=(_SEM, _SEM, _SEM, *[_HBM] * (2 * n), pl.BlockSpec(memory_space=pltpu.VMEM)),
        input_output_aliases={i: 3 + i for i in range(2 * n)},
        compiler_params=pltpu.CompilerParams(has_side_effects=pltpu.SideEffectType.DATAFLOW_SIDE_EFFECTING),
    )(*[pltpu.with_memory_space_constraint(a, pltpu.HBM) for a in list(arrs) + lands], *after)
    return (n, scatter, res[:3], res[3:3 + n], res[3 + n:3 + 2 * n], peers), res[-1]


def _exchange_wait(state, after, name):
    n, scatter, sems, srcs, lands, peers = state
    after = list(after) if isinstance(after, (list, tuple)) else [after]

    def body(*refs):
        ins, outs = refs[:n], refs[n:2 * n]
        send_sems, recv_sems, local_sems = refs[2 * n:2 * n + 3]
        local, remote = _exchange_copies(ins, outs, send_sems, recv_sems, local_sems, scatter, peers)
        for cp in remote:
            cp.wait_send()
            cp.wait_recv()
        for cp in local:
            cp.wait()

    hbm = lambda a: pltpu.HBM(a.shape, a.dtype)
    res = pl.pallas_call(
        body, name=name,
        out_shape=(*[hbm(a) for a in srcs], *[hbm(a) for a in lands]),
        in_specs=[_HBM] * (2 * n) + [_SEM] * 3 + [pl.BlockSpec(memory_space=pl.ANY)] * len(after),
        out_specs=tuple([_HBM] * (2 * n)),
        input_output_aliases={i: i for i in range(2 * n)},
        compiler_params=pltpu.CompilerParams(has_side_effects=pltpu.SideEffectType.DATAFLOW_SIDE_EFFECTING),
    )(*srcs, *lands, *sems, *after)
    return list(res[n:])


def _forward_to_sibling(slots, name):
    n = len(slots)
    hops = (2, 4, 6)

    def body(*refs):
        ins, outs, (send_sems, recv_sems) = refs[:n], refs[n:2 * n], refs[2 * n:]
        x, y, c = (lax.axis_index(a) for a in MESH_AXES)
        copies = []
        for a in range(n):
            for i, k in enumerate(hops):
                slot = 4 * (1 - x if k & 4 else x) + 2 * (1 - y if k & 2 else y) + c
                copies.append(pltpu.make_async_remote_copy(
                    src_ref=ins[a].at[slot], dst_ref=outs[a].at[slot],
                    send_sem=send_sems.at[a * len(hops) + i], recv_sem=recv_sems.at[a * len(hops) + i],
                    device_id=(x, y, 1 - c), device_id_type=pl.DeviceIdType.MESH))
        for cp in copies:
            cp.start()
        for cp in copies:
            cp.wait_send()
            cp.wait_recv()

    return pl.pallas_call(
        body, name=name, out_shape=[jax.ShapeDtypeStruct(s.shape, s.dtype) for s in slots],
        in_specs=[pl.BlockSpec(memory_space=pl.ANY)] * n, out_specs=[pl.BlockSpec(memory_space=pl.ANY)] * n,
        input_output_aliases={i: i for i in range(n)},
        scratch_shapes=[pltpu.SemaphoreType.DMA((n * len(hops),)), pltpu.SemaphoreType.DMA((n * len(hops),))],
    )(*slots)


def _adamw_math(w, g, m, v):
    m = ADAM_B1 * m + (1.0 - ADAM_B1) * g
    v = ADAM_B2 * v + (1.0 - ADAM_B2) * (g * g)
    m_hat = m / (1.0 - ADAM_B1 ** ADAM_STEP)
    v_hat = v / (1.0 - ADAM_B2 ** ADAM_STEP)
    return -ADAM_LR * (m_hat / (jnp.sqrt(v_hat) + ADAM_EPS) + ADAM_WD * w), m, v


def _slot_sum(p_ref):
    g = p_ref[0].astype(F32)
    for d in range(1, p_ref.shape[0]):
        g = g + p_ref[d].astype(F32)
    return g


def _adamw_tile(r, c):
    return _tile(r, max(SUBLANES, (256 * 1024) // c // SUBLANES * SUBLANES), SUBLANES)


def _adamw(parts, w, m, v, name):
    r, c = w.shape[-2:]
    tr = _adamw_tile(r, c)

    def body(p_ref, w_ref, m_ref, v_ref, g_ref, d_ref, nm_ref, nv_ref):
        g = _slot_sum(p_ref)
        g_ref[...] = g
        d_ref[...], nm_ref[...], nv_ref[...] = _adamw_math(w_ref[...], g, m_ref[...], v_ref[...])

    if w.ndim == 3:
        blk = pl.BlockSpec((None, tr, c), lambda i: (0, i, 0))
    else:
        blk = pl.BlockSpec((tr, c), lambda i: (i, 0))
    sh = jax.ShapeDtypeStruct(w.shape, F32)
    return pl.pallas_call(
        body, name=name, grid=(r // tr,),
        in_specs=[pl.BlockSpec((parts.shape[0], tr, c), lambda i: (0, i, 0)), blk, blk, blk],
        out_specs=[blk] * 4, out_shape=[sh] * 4, compiler_params=_cparams("parallel"),
    )(parts, w, m, v)


def _sum_parts(parts, name):
    _, r, c = parts.shape
    tr = _adamw_tile(r, c)

    def body(p_ref, o_ref):
        o_ref[...] = _slot_sum(p_ref)

    return pl.pallas_call(
        body, name=name, grid=(r // tr,),
        in_specs=[pl.BlockSpec((parts.shape[0], tr, c), lambda i: (0, i, 0))],
        out_specs=pl.BlockSpec((tr, c), lambda i: (i, 0)), out_shape=jax.ShapeDtypeStruct((r, c), F32),
        compiler_params=_cparams("parallel"),
    )(parts)


def _perm(a):
    s, d = a.shape
    return a.reshape(N_SEG, s // N_SEG, d).transpose(1, 0, 2).reshape(s, d)


def _unperm(a):
    s, d = a.shape
    return a.reshape(s // N_SEG, N_SEG, d).transpose(1, 0, 2).reshape(s, d)


def _lane_pad(a, width=LANES):
    return jnp.pad(a, ((0, 0), (0, width - a.shape[1])))


def _local_step(x, target, norm_pre, norm_post, kv_norm, kv_b_f, a_re, a_im, log_dt, b_re, b_im, c_re, c_im, comm):
    s, d = x.shape
    g, p = a_re.shape
    w = g * S5_GROUP
    fw = d
    nh = fw // HEAD_DIM
    seg_len = s // N_SEG
    row = lambda v: v.reshape(1, -1)
    g_pre0, g_pre1, g_post0, g_post1, g_kv = row(norm_pre[0]), row(norm_pre[1]), row(norm_post[0]), row(norm_post[1]), row(kv_norm)

    ldt = log_dt.reshape(g, 1)
    abr, abi, cr, ci = _s5_disc_fwd(a_re, a_im, ldt)
    cr_col, ci_col = cr.reshape(g * p, 1), ci.reshape(g * p, 1)
    b_re2, b_im2 = b_re.reshape(g * p, S5_GROUP), b_im.reshape(g * p, S5_GROUP)
    bb_re, bb_im = _s5_bbar_fwd(cr_col, ci_col, b_re2, b_im2)
    bd_re = _block_diag_in(bb_re.reshape(g, p, S5_GROUP)).astype(BF16)
    bd_im = _block_diag_in(bb_im.reshape(g, p, S5_GROUP)).astype(BF16)
    cd_re = _block_diag_out(c_re).astype(BF16)
    cd_im = _block_diag_out(-c_im).astype(BF16)
    ab_re = jnp.broadcast_to(abr.reshape(1, g * p), (N_SEG, g * p))
    ab_im = jnp.broadcast_to(abi.reshape(1, g * p), (N_SEG, g * p))
    zero_seg = jnp.zeros((N_SEG, g * p), F32)

    xn0 = _norm_cast(x, g_pre0 + comm.token, "norm_pre0", x_kind="nat")
    w_in = comm.weight("s5_w_in", [xn0, bd_re, bd_im, cd_re, cd_im, ab_re, ab_im])
    d_row, bglu_row = row(comm.vector("s5_d")), row(comm.vector("s5_b_glu"))
    u = _mm(xn0, w_in, "nn", F32, "s5_in_u", b_cols=(0, w), b_slots=True)
    z0 = _mm(xn0, w_in, "nn", BF16, "s5_in_z", b_cols=(w, w), b_slots=True)
    e_re, e_im = _s5_scan_fwd(u, bd_re, bd_im, cd_re, cd_im, ab_re, ab_im, zero_seg, zero_seg, d_row, False, "s5_scan_ends")
    i_re, i_im = _s5_seg_fix(e_re, e_im, ab_re, ab_im, seg_len, False, "s5_seg_fix")
    y_ssm, yg, h_re, h_im, _, _ = _s5_scan_fwd(u, bd_re, bd_im, cd_re, cd_im, ab_re, ab_im, i_re, i_im, d_row, True, "s5_scan")
    w_glu, w_out = comm.weight("s5_w_glu", yg), comm.weight("s5_w_out", yg)
    gp = _mm(yg, w_glu, "nn", BF16, "s5_glu")
    y3 = _s5_gate(y_ssm, gp, bglu_row, z0, "s5_gate")
    w_kv, fw_in = comm.weight("kv_w", y3), comm.weight("fox_w_in", y3)
    w_f = _lane_pad(w_kv[:, 2 * fw:])
    o0 = _mm(y3, w_out, "nn", F32, "s5_out")
    r0 = _post_norm(o0, g_post0 + comm.late_token, "norm_post0", out_kind="nat")

    h1, hn_kv, xn1 = _resid_norm2(x, r0, g_kv, g_pre1, "resid_norms")
    kv = _mm(hn_kv, w_kv, "nn", BF16, "kv_proj", b_cols=(0, 2 * fw))
    f_logit = _mm(hn_kv, w_f, "nn", F32, "f_proj")
    bf_row = _lane_pad(row(kv_b_f))
    cum2 = _cum_fwd(f_logit, bf_row, "cum_fwd")
    cum2_t = cum2[:, :nh].T.reshape(nh, 1, s)
    q2 = _mm(xn1, fw_in, "nn", BF16, "fox_q", scale=HEAD_DIM ** -0.5 * LOG2E, b_cols=(0, fw), b_slots=True)
    z1 = _mm(xn1, fw_in, "nn", BF16, "fox_z", b_cols=(fw, fw), b_slots=True)
    o, oz, lse2_t = _fox_fwd(q2, kv, cum2_t, z1, "fox_fwd")
    fw_out = comm.weight("fox_w_out", oz)
    o1 = _mm(oz, fw_out, "nn", F32, "fox_out")
    dh2, do1, sq, dg_post1 = _post_norm_loss(o1, g_post1, h1, target, "norm_post1_loss")
    loss = 0.5 * jnp.sum(sq) / d

    d_fw_out = _mm(oz, do1, "tn", BF16, "fox_out_dw")
    d_oz = _mm(do1, fw_out, "nt", F32, "fox_out_dx")
    do, dqz = _gate_bwd(d_oz, o, z1, "fox_gate_bwd")
    dk, dv, dqz, dcq, dck = _fox_bwd(q2, kv, do, o, lse2_t, cum2, dqz, "fox_bwd")
    d_fw_in = _mm(xn1, dqz, "tn", BF16, "fox_in_dw", col_slots=True)
    dxn1 = _mm(dqz, fw_in, "nt", F32, "fox_in_dx", b_slots=True)
    dcq_sl = _lane_pad(dcq.reshape(nh, s).T)
    dck_sl = _lane_pad(dck.reshape(nh, s).T)
    df, db_f = _cum_bwd(dcq_sl, dck_sl, f_logit, bf_row, "cum_bwd")
    dkv = _concat_cast(dk, dv, "fox_dkv")
    d_w_kvm = _mm(hn_kv, dkv, "tn", F32, "kv_dw")
    d_w_f = _mm(hn_kv, df, "tn", F32, "f_dw")
    dhn_f = _mm(df, w_f, "nt", F32, "f_dx")
    dhn_kv = _mm(dkv, w_kv, "nt", F32, "kv_dx", add=dhn_f, b_cols=(0, 2 * fw))
    d_w_kv = jnp.concatenate([d_w_kvm, d_w_f[:, :nh]], axis=1)
    tok = comm.send_grads(dict(fox_w_out=d_fw_out, fox_w_in=d_fw_in, kv_w=d_w_kv), "exchange_fox")
    dh1, dg_pre1, dg_kv = _norm_bwd2(dh2, h1, dxn1, dhn_kv, g_pre1, g_kv, "resid_norms_bwd")

    do0, dg_post0 = _post_norm_bwd(dh1, o0, g_post0 + tok[0, 0], "norm_post0_bwd", dy_kind="nat")
    d_w_out = _mm(y3, do0, "tn", BF16, "s5_out_dw")
    dy3 = _mm(do0, w_out, "nt", F32, "s5_out_dx")
    duz, dgp, dyg_direct, db_glu = _s5_gate_bwd(dy3, y_ssm, gp, bglu_row, z0, "s5_gate_bwd")
    d_w_glu = _mm(yg, dgp, "tn", BF16, "s5_glu_dw")
    dyg = _mm(dgp, w_glu, "nt", F32, "s5_glu_dx", add=dyg_direct)
    dy_ssm = _gelu_bwd(dyg, y_ssm, "s5_gelu_bwd")
    d_row = d_row + comm.send_grads(dict(s5_w_out=d_w_out, s5_w_glu=d_w_glu), "exchange_s5")[0, 0]
    ab_imn = -ab_im
    ge_re, ge_im = _s5_scan_bwd(dy_ssm, u, h_re, h_im, bd_re, bd_im, cd_re, cd_im, ab_re, ab_imn, zero_seg, zero_seg,
                                d_row, False, "s5_adj_ends")
    gi_re, gi_im = _s5_seg_fix(ge_re, ge_im, ab_re, ab_imn, seg_len, True, "s5_adj_fix")
    duz, dbd_re, dbd_im, dcd_re, dcd_im, dab_re, dab_im, dd = _s5_scan_bwd(
        dy_ssm, u, h_re, h_im, bd_re, bd_im, cd_re, cd_im, ab_re, ab_imn, gi_re, gi_im, d_row, True, "s5_adj", duz=duz)
    d_w_in = _mm(xn0, duz, "tn", BF16, "s5_in_dw", col_slots=True)
    tok = comm.send_grads(dict(s5_w_in=d_w_in), "exchange_s5_in")
    dxn0 = _mm(duz, w_in, "nt", F32, "s5_in_dx", after=tok, b_slots=True)
    grad_x, dg_pre0 = _norm_bwd1(dh1, x, dxn0, g_pre0, "norm_pre0_bwd")

    dbb_re = _block_diag_in_extract(dbd_re, p, S5_GROUP).reshape(g * p, S5_GROUP)
    dbb_im = _block_diag_in_extract(dbd_im, p, S5_GROUP).reshape(g * p, S5_GROUP)
    dcr_col, dci_col, db_re, db_im = _s5_bbar_bwd(cr_col, ci_col, b_re2, b_im2, dbb_re, dbb_im)
    da_re, da_im, dldt = _s5_disc_bwd(a_re, a_im, ldt, dab_re.reshape(g, p), dab_im.reshape(g, p),
                                      dcr_col.reshape(g, p), dci_col.reshape(g, p))
    dc_re = _block_diag_out_extract(dcd_re, S5_GROUP, p)
    dc_im = -_block_diag_out_extract(dcd_im, S5_GROUP, p)

    small = dict(
        norm_pre=jnp.concatenate([dg_pre0, dg_pre1], axis=0), norm_post=jnp.concatenate([dg_post0, dg_post1], axis=0),
        s5_a_re=da_re, s5_a_im=da_im, s5_log_dt=dldt.reshape(g), s5_b_re=db_re.reshape(g, p, S5_GROUP),
        s5_b_im=db_im.reshape(g, p, S5_GROUP), s5_c_re=dc_re, s5_c_im=dc_im, s5_d=dd.reshape(-1),
        s5_b_glu=db_glu.reshape(-1), kv_norm=dg_kv.reshape(-1), kv_b_f=db_f[0, :nh])
    return loss, grad_x, small


_BIG = ("s5_w_in", "s5_w_glu", "s5_w_out", "kv_w", "fox_w_in", "fox_w_out")
_COL_SHARDED = ("s5_w_in", "kv_w", "fox_w_in")
_SMALL = ("norm_pre", "norm_post", "s5_a_re", "s5_a_im", "s5_log_dt", "s5_b_re", "s5_b_im", "s5_c_re", "s5_c_im",
          "s5_d", "s5_b_glu", "kv_norm", "kv_b_f")
_SMALL_SHARDED = ("s5_d", "s5_b_glu")
_PACK_QUANTUM = SUBLANES * LANES
_WEIGHTS = ('norm_pre', 'norm_post', 's5_w_in', 's5_a_re', 's5_a_im', 's5_log_dt', 's5_b_re', 's5_b_im', 's5_c_re', 's5_c_im',
            's5_d', 's5_w_glu', 's5_b_glu', 's5_w_out', 'kv_norm', 'kv_w', 'kv_b_f', 'fox_w_in', 'fox_w_out')


def _full_from_slots(name, slots):
    n, r, c = slots.shape
    if name in _COL_SHARDED:
        return slots.transpose(1, 0, 2).reshape(r, n * c)
    return slots.reshape(n * r, c)


def _slots_from_full(name, full):
    if name in _COL_SHARDED:
        r, nc = full.shape
        return full.reshape(r, N_DEV, nc // N_DEV).transpose(1, 0, 2)
    nr, c = full.shape
    return full.reshape(N_DEV, nr // N_DEV, c)


def _pack(vals):
    parts = []
    for v in vals:
        flat = v.reshape(-1)
        parts.append(jnp.pad(flat, (0, (-flat.shape[0]) % _PACK_QUANTUM)))
    total = sum(p.shape[0] for p in parts)
    parts.append(jnp.zeros(((-total) % (N_DEV * _PACK_QUANTUM),), F32))
    return jnp.concatenate(parts).reshape(-1, LANES)


def _unpack(packed, shapes):
    flat = packed.reshape(-1)
    out, off = [], 0
    for sh in shapes:
        n = math.prod(sh)
        out.append(flat[off:off + n].reshape(sh))
        off += n + (-n) % _PACK_QUANTUM
    return out


class _Comm:
    _GROUPS = (("s5_w_in",) + _SMALL_SHARDED, ("s5_w_glu", "s5_w_out"), ("kv_w", "fox_w_in"), ("fox_w_out",))
    _SLOT_FORM = ("s5_w_in", "fox_w_in")

    def __init__(self, shards, vectors, early=()):
        self._shards = {**shards, **vectors}
        self._full, self._gathers = {}, {}
        self._early = list(early)
        self.token = jnp.zeros((), F32)
        for group in self._GROUPS[:-1]:
            self.token = self.token + self._start(group, ())[0, 0]
        self.late_token = None
        self._sent = []

    def _start(self, group, after):
        state, tok = _exchange_start([self._shards[n] for n in group], False, "gather_start_" + group[0], after,
                                     peers=_CHIP_PEERS)
        self._gathers[group] = state
        return tok

    def vector(self, name):
        return self._full[name]

    def weight(self, name, after):
        if name not in self._full:
            group = next(g for g in self._GROUPS if name in g)
            if group == self._GROUPS[0]:
                after = (list(after) if isinstance(after, (list, tuple)) else [after]) + self._early
            slots = _exchange_wait(self._gathers.pop(group), after, "gather_wait_" + group[0])
            slots = _forward_to_sibling(slots, "gather_forward_" + group[0])
            for n, sl in zip(group, slots):
                if n in _SMALL_SHARDED:
                    self._full[n] = sl.reshape(-1)
                else:
                    self._full[n] = sl if n in self._SLOT_FORM else _full_from_slots(n, sl)
            if group == self._GROUPS[-2]:
                self.late_token = self._start(self._GROUPS[-1], [slots[0]])[0, 0]
        return self._full[name]

    def send_grads(self, grads, name):
        names = list(grads)
        slots = [grads[n] if grads[n].ndim == 3 else _slots_from_full(n, grads[n]).astype(BF16) for n in names]
        state, tok = _exchange_start(slots, True, name + "_start")
        self._sent.append((names, state, name + "_wait"))
        return tok

    def received_grads(self, after):
        for names, state, name in self._sent:
            for n, recv in zip(names, _exchange_wait(state, after, name)):
                yield n, recv


def kernel(x, norm_pre, norm_post, s5_w_in, s5_a_re, s5_a_im, s5_log_dt, s5_b_re, s5_b_im, s5_c_re, s5_c_im, s5_d, s5_w_glu, s5_b_glu, s5_w_out, kv_norm, kv_w, kv_b_f, fox_w_in, fox_w_out, loss_target, m_norm_pre, m_norm_post, m_s5_w_in, m_s5_a_re, m_s5_a_im, m_s5_log_dt, m_s5_b_re, m_s5_b_im, m_s5_c_re, m_s5_c_im, m_s5_d, m_s5_w_glu, m_s5_b_glu, m_s5_w_out, m_kv_norm, m_kv_w, m_kv_b_f, m_fox_w_in, m_fox_w_out, v_norm_pre, v_norm_post, v_s5_w_in, v_s5_a_re, v_s5_a_im, v_s5_log_dt, v_s5_b_re, v_s5_b_im, v_s5_c_re, v_s5_c_im, v_s5_d, v_s5_w_glu, v_s5_b_glu, v_s5_w_out, v_kv_norm, v_kv_w, v_kv_b_f, v_fox_w_in, v_fox_w_out):
    env = dict(locals())
    wts = {n: env[n] for n in _WEIGHTS}
    mom = {n: env["m_" + n] for n in _WEIGHTS}
    var = {n: env["v_" + n] for n in _WEIGHTS}
    me = 4 * lax.axis_index("x") + 2 * lax.axis_index("y") + lax.axis_index("c")
    shard2d = {n: wts[n].reshape(wts[n].shape[-2:]) for n in _BIG}
    full_shape = {n: ((wts[n].size * N_DEV,) if n in _SMALL_SHARDED else wts[n].shape) for n in _SMALL}

    def spread(n, v):
        if n not in _SMALL_SHARDED:
            return v
        flat = v.reshape(-1)
        return lax.dynamic_update_slice(jnp.zeros(full_shape[n], F32), flat, (me * flat.shape[0],))

    packed = [_pack([spread(n, src[n]) for n in _SMALL] + [jnp.zeros((1,), F32)]) for src in (wts, mom, var)]
    comm = _Comm({n: shard2d[n].astype(BF16) for n in _BIG}, {n: wts[n].reshape(1, -1) for n in _SMALL_SHARDED}, packed)

    loss_local, grad_x, small = _local_step(
        x[0], loss_target[0], norm_pre, norm_post, kv_norm, kv_b_f, s5_a_re[0], s5_a_im[0], s5_log_dt[0],
        s5_b_re[0], s5_b_im[0], s5_c_re[0], s5_c_im[0], comm)

    small_pack = _pack([small[n] for n in _SMALL] + [loss_local.reshape(1)])
    slice_rows = small_pack.shape[0] // N_DEV
    small_state, small_tok = _exchange_start([small_pack.reshape(N_DEV, slice_rows, LANES)], True, "reduce_small_start")

    res = {}
    for n, recv in comm.received_grads([small_tok, grad_x]):
        res[n] = _adamw(recv, wts[n], mom[n], var[n], "adamw_" + n)

    my_sum = _sum_parts(_exchange_wait(small_state, res[_BIG[0]][0], "reduce_small_wait")[0], "sum_small")
    g_all = _exchange([my_sum], False, "gather_small")[0].reshape(1, small_pack.shape[0], LANES)
    outs = _adamw(g_all, *packed, "adamw_small")
    unpacked = [_unpack(o, [full_shape[n] for n in _SMALL] + [(1,)]) for o in outs]
    loss = unpacked[0][-1][0]
    for i, n in enumerate(_SMALL):
        vals = [u[i] for u in unpacked]
        if n in _SMALL_SHARDED:
            k = wts[n].size
            vals = [lax.dynamic_slice(v, (me * k,), (k,)) for v in vals]
        res[n] = [v.reshape(wts[n].shape) for v in vals]

    return (loss, grad_x[None], *[res[n][0] for n in _WEIGHTS], *[res[n][1] for n in _WEIGHTS],
            *[res[n][2] for n in _WEIGHTS], *[res[n][3] for n in _WEIGHTS])
```

```python
import functools
import math

import jax
import jax.numpy as jnp
from jax import lax
from jax.experimental import pallas as pl
from jax.experimental.pallas import tpu as pltpu

F32 = jnp.float32
BF16 = jnp.bfloat16

N_DEV = 8
MESH_AXES = ("x", "y", "c")
S5_GROUP = 16
S5_STATE = 64
LANES = 128
SUBLANES = 8
GROUPS_PER_BLOCK = LANES // S5_GROUP
BLOCK_STATE = GROUPS_PER_BLOCK * S5_STATE
N_SEG = SUBLANES
HEAD_DIM = 128
RMS_EPS = 1e-6
NEG_INF = -1e30
LOG2E = math.log2(math.e)
ADAM_LR = 0.001
ADAM_B1 = 0.9
ADAM_B2 = 0.999
ADAM_EPS = 1e-08
ADAM_WD = 0.01
ADAM_STEP = 10
VMEM_LIMIT = 56 * 1024 * 1024


def _tile(n, pref, quantum=LANES):
    if n <= pref:
        return n
    t = (pref // quantum) * quantum
    while t >= quantum:
        if n % t == 0:
            return t
        t -= quantum
    return n


def _cparams(*sem):
    return pltpu.CompilerParams(dimension_semantics=sem if sem else None, vmem_limit_bytes=VMEM_LIMIT)


_DOT_DIMS = {"nn": ((1,), (0,)), "nt": ((1,), (1,)), "tn": ((0,), (0,))}


def _mm(a, b, mode, out_dtype, name, add=None, scale=None, b_cols=None, after=None, col_slots=False, b_slots=False,
        b_rows=None):
    slot_w = b.shape[2] if b_slots else None
    b2d = (b.shape[1], b.shape[0] * b.shape[2]) if b_slots else b.shape
    b_shape = b2d if b_cols is None else (b2d[0], b_cols[1])
    if b_rows is not None:
        b_shape = (b_rows, b_shape[1])
    if mode == "nn":
        (M, K), (K2, N) = a.shape, b_shape
    elif mode == "nt":
        (M, K), (N, K2) = a.shape, b_shape
    else:
        (K, M), (K2, N) = a.shape, b_shape
    assert K == K2, (name, a.shape, b_shape)
    tm, tn, tk = _tile(M, 1024 if K <= 2048 else 512), (N // N_DEV if col_slots else _tile(N, 1024)), _tile(K, 4096)
    if b_slots and mode == "nn":
        tn = slot_w
    nk = K // tk
    dims = (_DOT_DIMS[mode], ((), ()))
    col0 = 0
    if b_cols is not None:
        assert mode != "tn" and b_cols[0] % (tn if mode == "nn" else tk) == 0
        col0 = b_cols[0] // (tn if mode == "nn" else tk)
    assert not b_slots or (mode == "nn" or (mode == "nt" and nk == 1 and b_cols is None))

    def body(*refs):
        a_ref, b_ref = refs[:2]
        c_ref = refs[2] if add is not None else None
        o_ref = refs[2 + (add is not None) + (after is not None)]
        if b_slots and mode == "nt":
            part = lax.dot_general(a_ref[:, :slot_w], b_ref[0], dims, preferred_element_type=F32)
            for sl in range(1, b_ref.shape[0]):
                part += lax.dot_general(a_ref[:, sl * slot_w:(sl + 1) * slot_w], b_ref[sl], dims, preferred_element_type=F32)
        else:
            part = lax.dot_general(a_ref[...], b_ref[...], dims, preferred_element_type=F32)

        def finish(r):
            if scale is not None:
                r = r * scale
            if add is not None:
                r = r + c_ref[...]
            o_ref[...] = r.astype(out_dtype)

        if nk == 1:
            finish(part)
            return
        acc = refs[-1]
        k = pl.program_id(2)

        @pl.when(k == 0)
        def _():
            acc[...] = part

        @pl.when(jnp.logical_and(k > 0, k < nk - 1))
        def _():
            acc[...] += part

        @pl.when(k == nk - 1)
        def _():
            finish(acc[...] + part)

    if mode == "tn":
        a_spec = pl.BlockSpec((tk, tm), lambda i, j, k: (k, i))
    else:
        a_spec = pl.BlockSpec((tm, tk), lambda i, j, k: (i, k))
    if b_slots and mode == "nn":
        b_spec = pl.BlockSpec((None, tk, tn), lambda i, j, k: (j + col0, k, 0))
    elif b_slots:
        b_spec = pl.BlockSpec((b.shape[0], tn, slot_w), lambda i, j, k: (0, j, 0))
    elif mode == "nt":
        b_spec = pl.BlockSpec((tn, tk), lambda i, j, k: (j, k + col0))
    else:
        b_spec = pl.BlockSpec((tk, tn), lambda i, j, k: (k, j + col0))
    o_spec = pl.BlockSpec((tm, tn), lambda i, j, k: (i, j))
    in_specs = [a_spec, b_spec] + ([o_spec] if add is not None else [])
    args = (a, b) + ((add,) if add is not None else ())
    if after is not None:
        in_specs.append(pl.BlockSpec(after.shape, lambda i, j, k: (0, 0)))
        args += (after,)
    out_shape = jax.ShapeDtypeStruct((M, N), out_dtype)
    if col_slots:
        assert add is None
        o_spec = pl.BlockSpec((None, tm, tn), lambda i, j, k: (j, i, 0))
        out_shape = jax.ShapeDtypeStruct((N_DEV, M, tn), out_dtype)
    return pl.pallas_call(
        body, name=name, grid=(M // tm, N // tn, nk),
        in_specs=in_specs, out_specs=o_spec,
        out_shape=out_shape,
        scratch_shapes=[pltpu.VMEM((tm, tn), F32)] if nk > 1 else [],
        compiler_params=_cparams("parallel", "parallel", "arbitrary"),
    )(*args)


class _NatIn:
    def __init__(self, ref):
        self.ref = ref

    def __getitem__(self, idx):
        v = jnp.swapaxes(self.ref[...], 0, 1)
        return v.reshape(v.shape[0] * N_SEG, v.shape[2])


class _NatOut:
    def __init__(self, ref):
        self.ref = ref

    def __setitem__(self, idx, val):
        self.ref[...] = jnp.swapaxes(val.reshape(val.shape[0] // N_SEG, N_SEG, val.shape[1]), 0, 1)


def _rowcall(body, name, n_rows, ins, outs, tile_rows=256):
    tr = _tile(n_rows, tile_rows, SUBLANES * 2)
    n_in = len(ins)
    in_kinds = [k for _, k in ins]
    kinds = [k for _, _, k in outs]

    def kern(*refs):
        @pl.when(pl.program_id(0) == 0)
        def _():
            for r, kind in zip(refs[n_in:], kinds):
                if kind == "acc":
                    r[...] = jnp.zeros_like(r)

        wrapped = [_NatIn(r) if k == "nat" else r for r, k in zip(refs[:n_in], in_kinds)]
        wrapped += [_NatOut(r) if k == "nat" else r for r, k in zip(refs[n_in:], kinds)]
        body(*wrapped)

    in_specs, args = [], []
    for arr, kind in ins:
        if kind == "row":
            in_specs.append(pl.BlockSpec((tr, arr.shape[1]), lambda i: (i, 0)))
        elif kind == "nat":
            in_specs.append(pl.BlockSpec((N_SEG, tr // N_SEG, arr.shape[1]), lambda i: (0, i, 0)))
            arr = arr.reshape(N_SEG, n_rows // N_SEG, arr.shape[1])
        else:
            in_specs.append(pl.BlockSpec(arr.shape, lambda i, nd=arr.ndim: (0,) * nd))
        args.append(arr)
    out_specs, out_shape = [], []
    for width, dtype, kind in outs:
        if kind == "row":
            out_specs.append(pl.BlockSpec((tr, width), lambda i: (i, 0)))
            out_shape.append(jax.ShapeDtypeStruct((n_rows, width), dtype))
        elif kind == "right":
            out_specs.append(pl.BlockSpec((tr, width), lambda i: (i, 1)))
            out_shape.append(jax.ShapeDtypeStruct((n_rows, 2 * width), dtype))
        elif kind == "nat":
            out_specs.append(pl.BlockSpec((N_SEG, tr // N_SEG, width), lambda i: (0, i, 0)))
            out_shape.append(jax.ShapeDtypeStruct((N_SEG, n_rows // N_SEG, width), dtype))
        else:
            out_specs.append(pl.BlockSpec((1, width), lambda i: (0, 0)))
            out_shape.append(jax.ShapeDtypeStruct((1, width), F32))
    res = pl.pallas_call(
        kern, name=name, grid=(n_rows // tr,), in_specs=in_specs, out_specs=out_specs, out_shape=out_shape,
        compiler_params=_cparams("arbitrary"),
    )(*args)
    return [r.reshape(n_rows, r.shape[2]) if k == "nat" else r for r, k in zip(res, kinds)]


def _rstd(x):
    return lax.rsqrt(jnp.mean(x * x, axis=-1, keepdims=True) + RMS_EPS)


def _rms_bwd(x, g, dy):
    xh = x * _rstd(x)
    dxh = dy * g
    dx = _rstd(x) * (dxh - xh * jnp.mean(dxh * xh, axis=-1, keepdims=True))
    return dx, jnp.sum(dy * xh, axis=0, keepdims=True)


def _silu(z):
    return z * jax.nn.sigmoid(z)


def _norm_cast(x, g, name, x_kind="row"):
    def body(x_ref, g_ref, o_ref):
        x = x_ref[...]
        o_ref[...] = (x * _rstd(x) * g_ref[...]).astype(BF16)

    return _rowcall(body, name, x.shape[0], [(x, x_kind), (g, "full")], [(x.shape[1], BF16, "row")])[0]


def _resid_norm2(x, r0, g_kv, g_pre, name):
    def body(x_ref, r_ref, gk_ref, gp_ref, h_ref, nk_ref, np_ref):
        h = x_ref[...] + r_ref[...]
        h_ref[...] = h
        hn = h * _rstd(h)
        nk_ref[...] = (hn * gk_ref[...]).astype(BF16)
        np_ref[...] = (hn * gp_ref[...]).astype(BF16)

    d = x.shape[1]
    return _rowcall(body, name, x.shape[0], [(x, "row"), (r0, "row"), (g_kv, "full"), (g_pre, "full")],
                    [(d, F32, "row"), (d, BF16, "row"), (d, BF16, "row")])


def _post_norm(o, g, name, out_kind="row"):
    def body(o_ref, g_ref, r_ref):
        o = o_ref[...]
        r_ref[...] = o * _rstd(o) * g_ref[...]

    return _rowcall(body, name, o.shape[0], [(o, "row"), (g, "full")], [(o.shape[1], F32, out_kind)])[0]


def _post_norm_loss(o, g, h1, target, name):
    d = o.shape[1]

    def body(o_ref, g_ref, h_ref, t_ref, dh_ref, do_ref, acc_ref, dg_ref):
        o = o_ref[...]
        e = h_ref[...] + o * _rstd(o) * g_ref[...] - t_ref[...]
        dh = e * (1.0 / d)
        dh_ref[...] = dh
        acc_ref[...] += jnp.sum(e * e, axis=0, keepdims=True)
        dx, dg = _rms_bwd(o, g_ref[...], dh)
        do_ref[...] = dx.astype(BF16)
        dg_ref[...] += dg

    return _rowcall(body, name, o.shape[0], [(o, "row"), (g, "full"), (h1, "row"), (target, "row")],
                    [(d, F32, "row"), (d, BF16, "row"), (d, F32, "acc"), (d, F32, "acc")])


def _post_norm_bwd(dy, o, g, name, dy_kind="row"):
    def body(dy_ref, o_ref, g_ref, do_ref, dg_ref):
        dx, dg = _rms_bwd(o_ref[...], g_ref[...], dy_ref[...])
        do_ref[...] = dx.astype(BF16)
        dg_ref[...] += dg

    d = o.shape[1]
    return _rowcall(body, name, o.shape[0], [(dy, dy_kind), (o, "row"), (g, "full")], [(d, BF16, "row"), (d, F32, "acc")])


def _gate_mul(o, z, name):
    def body(o_ref, z_ref, r_ref):
        r_ref[...] = (o_ref[...] * _silu(z_ref[...])).astype(BF16)

    return _rowcall(body, name, o.shape[0], [(o, "row"), (z, "row")], [(o.shape[1], BF16, "row")])[0]


def _gate_bwd(d_oz, o, z, name):
    def body(d_ref, o_ref, z_ref, do_ref, dz_ref):
        _, vjp = jax.vjp(lambda o, z: o * _silu(z), o_ref[...], z_ref[...].astype(F32))
        do, dz = vjp(d_ref[...])
        do_ref[...] = do.astype(BF16)
        dz_ref[...] = dz.astype(BF16)

    w = o.shape[1]
    return _rowcall(body, name, o.shape[0], [(d_oz, "row"), (o, "row"), (z, "row")], [(w, BF16, "row"), (w, BF16, "right")])


def _norm_bwd2(dh2, h1, dxn1, dhn_kv, g_pre, g_kv, name):
    def body(dh2_ref, h_ref, d1_ref, dk_ref, gp_ref, gk_ref, dh1_ref, dgp_ref, dgk_ref):
        h = h_ref[...]
        dx1, dg1 = _rms_bwd(h, gp_ref[...], d1_ref[...])
        dxk, dgk = _rms_bwd(h, gk_ref[...], dk_ref[...])
        dh1_ref[...] = dh2_ref[...] + dx1 + dxk
        dgp_ref[...] += dg1
        dgk_ref[...] += dgk

    d = h1.shape[1]
    return _rowcall(body, name, h1.shape[0],
                    [(dh2, "row"), (h1, "row"), (dxn1, "row"), (dhn_kv, "row"), (g_pre, "full"), (g_kv, "full")],
                    [(d, F32, "row"), (d, F32, "acc"), (d, F32, "acc")])


def _norm_bwd1(dres, x, dxn, g, name):
    def body(dr_ref, x_ref, dn_ref, g_ref, dx_ref, dg_ref):
        dx, dg = _rms_bwd(x_ref[...], g_ref[...], dn_ref[...])
        dx_ref[...] = dr_ref[...] + dx
        dg_ref[...] += dg

    d = x.shape[1]
    return _rowcall(body, name, x.shape[0], [(dres, "nat"), (x, "nat"), (dxn, "row"), (g, "full")],
                    [(d, F32, "nat"), (d, F32, "acc")])


def _gelu_cast(y, name):
    def body(y_ref, o_ref):
        o_ref[...] = jax.nn.gelu(y_ref[...]).astype(BF16)

    return _rowcall(body, name, y.shape[0], [(y, "row")], [(y.shape[1], BF16, "row")])[0]


def _s5_gate(y_ssm, gp, b_glu, z, name):
    def body(y_ref, gp_ref, b_ref, z_ref, o_ref):
        yg = jax.nn.gelu(y_ref[...])
        o_ref[...] = (yg * jax.nn.sigmoid(gp_ref[...] + b_ref[...]) * _silu(z_ref[...].astype(F32))).astype(BF16)

    return _rowcall(body, name, y_ssm.shape[0], [(y_ssm, "row"), (gp, "row"), (b_glu, "full"), (z, "row")],
                    [(y_ssm.shape[1], BF16, "row")])[0]


def _s5_gate_bwd(dy3, y_ssm, gp, b_glu, z, name):
    def body(d_ref, y_ref, gp_ref, b_ref, z_ref, dz_ref, dgp_ref, dyg_ref, db_ref):
        yg = jax.nn.gelu(y_ref[...])
        _, vjp = jax.vjp(lambda yg, gp, z: yg * jax.nn.sigmoid(gp) * _silu(z), yg, gp_ref[...] + b_ref[...],
                         z_ref[...].astype(F32))
        dyg, dgp, dz = vjp(d_ref[...])
        dz_ref[...] = dz.astype(BF16)
        dgp_ref[...] = dgp.astype(BF16)
        dyg_ref[...] = dyg
        db_ref[...] += jnp.sum(dgp, axis=0, keepdims=True)

    w = y_ssm.shape[1]
    return _rowcall(body, name, y_ssm.shape[0],
                    [(dy3, "row"), (y_ssm, "row"), (gp, "row"), (b_glu, "full"), (z, "row")],
                    [(w, BF16, "right"), (w, BF16, "row"), (w, F32, "row"), (w, F32, "acc")])


def _gelu_bwd(dyg, y_ssm, name):
    def body(d_ref, y_ref, o_ref):
        _, vjp = jax.vjp(jax.nn.gelu, y_ref[...])
        o_ref[...] = vjp(d_ref[...])[0]

    return _rowcall(body, name, y_ssm.shape[0], [(dyg, "row"), (y_ssm, "row")], [(y_ssm.shape[1], F32, "row")])[0]


def _concat_cast(a, b, name):
    def body(a_ref, b_ref, o_ref):
        w = a_ref.shape[1]
        o_ref[:, :w] = a_ref[...].astype(BF16)
        o_ref[:, w:] = b_ref[...].astype(BF16)

    return _rowcall(body, name, a.shape[0], [(a, "row"), (b, "row")], [(a.shape[1] + b.shape[1], BF16, "row")])[0]


def _disc(ar, ai, ldt):
    dt = jnp.exp(ldt)
    mag = jnp.exp(ar * dt)
    abr = mag * jnp.cos(ai * dt)
    abi = mag * jnp.sin(ai * dt)
    den = ar * ar + ai * ai
    nr = abr - 1.0
    return abr, abi, (nr * ar + abi * ai) / den, (abi * ar - nr * ai) / den


def _s5_disc_fwd(a_re, a_im, ldt):
    def body(ar, ai, ld, o1, o2, o3, o4):
        o1[...], o2[...], o3[...], o4[...] = _disc(ar[...], ai[...], ld[...])

    sh = jax.ShapeDtypeStruct(a_re.shape, F32)
    return pl.pallas_call(body, name="s5_disc_fwd", out_shape=(sh, sh, sh, sh))(a_re, a_im, ldt)


def _s5_disc_bwd(a_re, a_im, ldt, d_abr, d_abi, d_cr, d_ci):
    def body(ar, ai, ld, g1, g2, g3, g4, o1, o2, o3):
        _, vjp = jax.vjp(_disc, ar[...], ai[...], ld[...])
        o1[...], o2[...], o3[...] = vjp((g1[...], g2[...], g3[...], g4[...]))

    sh = jax.ShapeDtypeStruct(a_re.shape, F32)
    return pl.pallas_call(body, name="s5_disc_bwd", out_shape=(sh, sh, jax.ShapeDtypeStruct(ldt.shape, F32)))(
        a_re, a_im, ldt, d_abr, d_abi, d_cr, d_ci)


def _bbar(cr, ci, br, bi):
    return cr * br - ci * bi, cr * bi + ci * br


def _s5_bbar_fwd(cr_col, ci_col, b_re, b_im):
    def body(cr, ci, br, bi, o1, o2):
        o1[...], o2[...] = _bbar(cr[...], ci[...], br[...], bi[...])

    w = b_re.shape[1]
    return _rowcall(body, "s5_bbar_fwd", b_re.shape[0], [(cr_col, "row"), (ci_col, "row"), (b_re, "row"), (b_im, "row")],
                    [(w, F32, "row"), (w, F32, "row")], tile_rows=1024)


def _s5_bbar_bwd(cr_col, ci_col, b_re, b_im, d_re, d_im):
    def body(cr, ci, br, bi, g1, g2, o1, o2, o3, o4):
        _, vjp = jax.vjp(_bbar, cr[...], ci[...], br[...], bi[...])
        o1[...], o2[...], o3[...], o4[...] = vjp((g1[...], g2[...]))

    w = b_re.shape[1]
    return _rowcall(body, "s5_bbar_bwd", b_re.shape[0],
                    [(cr_col, "row"), (ci_col, "row"), (b_re, "row"), (b_im, "row"), (d_re, "row"), (d_im, "row")],
                    [(1, F32, "row"), (1, F32, "row"), (w, F32, "row"), (w, F32, "row")], tile_rows=1024)


def _block_diag_in(t):
    g, p, c = t.shape
    nb = g // GROUPS_PER_BLOCK
    t4 = t.reshape(nb, GROUPS_PER_BLOCK, p, c).transpose(0, 1, 3, 2)
    eye = jnp.eye(GROUPS_PER_BLOCK, dtype=t.dtype)
    return (t4[:, :, :, None, :] * eye[None, :, None, :, None]).reshape(nb, GROUPS_PER_BLOCK * c, GROUPS_PER_BLOCK * p)


def _block_diag_in_extract(d, p, c):
    nb = d.shape[0]
    d5 = d.reshape(nb, GROUPS_PER_BLOCK, c, GROUPS_PER_BLOCK, p)
    diag = jnp.stack([d5[:, g, :, g, :] for g in range(GROUPS_PER_BLOCK)], axis=1)
    return diag.transpose(0, 1, 3, 2).reshape(nb * GROUPS_PER_BLOCK, p, c)


def _block_diag_out(t):
    g, c, p = t.shape
    nb = g // GROUPS_PER_BLOCK
    t4 = t.reshape(nb, GROUPS_PER_BLOCK, c, p).transpose(0, 1, 3, 2)
    eye = jnp.eye(GROUPS_PER_BLOCK, dtype=t.dtype)
    return (t4[:, :, :, None, :] * eye[None, :, None, :, None]).reshape(nb, GROUPS_PER_BLOCK * p, GROUPS_PER_BLOCK * c)


def _block_diag_out_extract(d, c, p):
    nb = d.shape[0]
    d5 = d.reshape(nb, GROUPS_PER_BLOCK, p, GROUPS_PER_BLOCK, c)
    diag = jnp.stack([d5[:, g, :, g, :] for g in range(GROUPS_PER_BLOCK)], axis=1)
    return diag.transpose(0, 1, 3, 2).reshape(nb * GROUPS_PER_BLOCK, c, p)


def _scan_step(ar, ai, hr, hi, xr, xi):
    return ar * hr - ai * hi + xr, ar * hi + ai * hr + xi


def _s5_scan_fwd(u, bd_re, bd_im, cd_re, cd_im, ab_re, ab_im, init_re, init_im, d_row, full, name):
    s, w = u.shape
    nb = w // LANES
    rows = _tile(s, 512, SUBLANES)
    nc = s // rows
    steps = rows // N_SEG
    ns = nb * BLOCK_STATE

    def body(u_ref, bdr, bdi, cdr, cdi, ar_ref, ai_ref, ir_ref, ii_ref, d_ref, *outs):
        if full:
            y_ref, yg_ref, hr_ref, hi_ref, er_ref, ei_ref, cr, ci = outs
        else:
            er_ref, ei_ref, hr_ref, hi_ref, cr, ci = outs
        c = pl.program_id(1)

        @pl.when(c == 0)
        def _():
            cr[...] = ir_ref[...]
            ci[...] = ii_ref[...]

        ub = u_ref[...].astype(BF16)
        hr_ref[...] = jnp.dot(ub, bdr[...], preferred_element_type=F32)
        hi_ref[...] = jnp.dot(ub, bdi[...], preferred_element_type=F32)
        ar, ai = ar_ref[...], ai_ref[...]

        hr, hi = cr[...], ci[...]
        for j in range(steps):
            rows_j = pl.ds(j * N_SEG, N_SEG)
            hr, hi = _scan_step(ar, ai, hr, hi, hr_ref[rows_j, :], hi_ref[rows_j, :])
            hr_ref[rows_j, :] = hr
            hi_ref[rows_j, :] = hi
        cr[...] = hr
        ci[...] = hi
        if full:
            y = (jnp.dot(hr_ref[...].astype(BF16), cdr[...], preferred_element_type=F32)
                 + jnp.dot(hi_ref[...].astype(BF16), cdi[...], preferred_element_type=F32)
                 + d_ref[...] * u_ref[...])
            y_ref[...] = y
            yg_ref[...] = jax.nn.gelu(y).astype(BF16)

        @pl.when(c == nc - 1)
        def _():
            er_ref[...] = hr
            ei_ref[...] = hi

    blk3 = lambda a: pl.BlockSpec((None,) + a.shape[1:], lambda k, c: (k, 0, 0))
    seg = pl.BlockSpec((N_SEG, BLOCK_STATE), lambda k, c: (0, k))
    st = pl.BlockSpec((rows, BLOCK_STATE), lambda k, c: (c, k))
    in_specs = [pl.BlockSpec((rows, LANES), lambda k, c: (c, k)), blk3(bd_re), blk3(bd_im), blk3(cd_re), blk3(cd_im),
                seg, seg, seg, seg, pl.BlockSpec((1, LANES), lambda k, c: (0, k))]
    seg_shape = jax.ShapeDtypeStruct((N_SEG, ns), F32)
    st_shape = jax.ShapeDtypeStruct((s, ns), F32)
    carry = [pltpu.VMEM((N_SEG, BLOCK_STATE), F32)] * 2
    if full:
        ych = pl.BlockSpec((rows, LANES), lambda k, c: (c, k))
        out_specs = [ych, ych, st, st, seg, seg]
        out_shape = [jax.ShapeDtypeStruct((s, w), F32), jax.ShapeDtypeStruct((s, w), BF16), st_shape, st_shape, seg_shape, seg_shape]
        scratch = carry
    else:
        out_specs = [seg, seg]
        out_shape = [seg_shape, seg_shape]
        scratch = [pltpu.VMEM((rows, BLOCK_STATE), F32)] * 2 + carry
    return pl.pallas_call(
        body, name=name, grid=(nb, nc), in_specs=in_specs, out_specs=out_specs, out_shape=out_shape,
        scratch_shapes=scratch, compiler_params=_cparams("parallel", "arbitrary"),
    )(u, bd_re, bd_im, cd_re, cd_im, ab_re, ab_im, init_re, init_im, d_row)


def _s5_seg_fix(e_re, e_im, ab_re, ab_im, seg_len, reverse, name):
    assert seg_len & (seg_len - 1) == 0

    def body(er, ei, ar, ai, o_re, o_im):
        pr, pi = ar[0:1, :], ai[0:1, :]
        for _ in range(int(math.log2(seg_len))):
            pr, pi = pr * pr - pi * pi, 2.0 * pr * pi
        tr = jnp.zeros_like(pr)
        ti = jnp.zeros_like(pr)
        order = list(range(N_SEG - 1, -1, -1)) if reverse else list(range(N_SEG))
        for n, sgm in enumerate(order):
            o_re[sgm:sgm + 1, :] = tr
            o_im[sgm:sgm + 1, :] = ti
            if n < N_SEG - 1:
                tr, ti = _scan_step(pr, pi, tr, ti, er[sgm:sgm + 1, :], ei[sgm:sgm + 1, :])

    sh = jax.ShapeDtypeStruct(e_re.shape, F32)
    return pl.pallas_call(body, name=name, out_shape=(sh, sh))(e_re, e_im, ab_re, ab_im)


def _s5_scan_bwd(dy, u, h_re, h_im, bd_re, bd_im, cd_re, cd_im, ab_re, ab_imn, gin_re, gin_im, d_row, full, name, duz=None):
    s, w = u.shape
    nb = w // LANES
    rows = _tile(s, 512, SUBLANES)
    nc = s // rows
    steps = rows // N_SEG
    ns = nb * BLOCK_STATE

    def body(dy_ref, u_ref, hr_ref, hi_ref, bdr, bdi, cdr, cdi, ar_ref, ai_ref, ir_ref, ii_ref, d_ref, *outs):
        if full:
            _, du_ref, dbr_ref, dbi_ref, dcr_ref, dci_ref, dar_ref, dai_ref, dd_ref, gr, gi, accr, acci = outs
        else:
            er_ref, ei_ref, gr, gi = outs
        c = pl.program_id(1)

        @pl.when(c == 0)
        def _():
            gr[pl.ds(rows, N_SEG), :] = ir_ref[...]
            gi[pl.ds(rows, N_SEG), :] = ii_ref[...]
            if full:
                for r in (dbr_ref, dbi_ref, dcr_ref, dci_ref, dd_ref, accr, acci):
                    r[...] = jnp.zeros_like(r)

        dyb = dy_ref[...].astype(BF16)
        nt = (_DOT_DIMS["nt"], ((), ()))
        tn = (_DOT_DIMS["tn"], ((), ()))
        gr[pl.ds(0, rows), :] = lax.dot_general(dyb, cdr[...], nt, preferred_element_type=F32)
        gi[pl.ds(0, rows), :] = lax.dot_general(dyb, cdi[...], nt, preferred_element_type=F32)
        ar, ai = ar_ref[...], ai_ref[...]

        g0r, g0i = gr[pl.ds(rows, N_SEG), :], gi[pl.ds(rows, N_SEG), :]
        for j in range(steps - 1, -1, -1):
            rows_j = pl.ds(j * N_SEG, N_SEG)
            g0r, g0i = _scan_step(ar, ai, g0r, g0i, gr[rows_j, :], gi[rows_j, :])
            gr[rows_j, :] = g0r
            gi[rows_j, :] = g0i
        if full:
            hr, hi = hr_ref[...], hi_ref[...]
            gnr, gni = gr[pl.ds(N_SEG, rows), :], gi[pl.ds(N_SEG, rows), :]
            accr[...] += jnp.sum((gnr * hr + gni * hi).reshape(steps, N_SEG, BLOCK_STATE), axis=0)
            acci[...] += jnp.sum((gni * hr - gnr * hi).reshape(steps, N_SEG, BLOCK_STATE), axis=0)
        gr[pl.ds(rows, N_SEG), :] = g0r
        gi[pl.ds(rows, N_SEG), :] = g0i
        if full:
            ub = u_ref[...].astype(BF16)
            gbr, gbi = gr[pl.ds(0, rows), :].astype(BF16), gi[pl.ds(0, rows), :].astype(BF16)
            dcr_ref[...] += lax.dot_general(hr.astype(BF16), dyb, tn, preferred_element_type=F32)
            dci_ref[...] += lax.dot_general(hi.astype(BF16), dyb, tn, preferred_element_type=F32)
            dbr_ref[...] += lax.dot_general(ub, gbr, tn, preferred_element_type=F32)
            dbi_ref[...] += lax.dot_general(ub, gbi, tn, preferred_element_type=F32)
            du_ref[...] = (lax.dot_general(gbr, bdr[...], nt, preferred_element_type=F32)
                           + lax.dot_general(gbi, bdi[...], nt, preferred_element_type=F32)
                           + d_ref[...] * dy_ref[...]).astype(BF16)
            dd_ref[...] += jnp.sum(dy_ref[...] * u_ref[...], axis=0, keepdims=True)

        @pl.when(c == nc - 1)
        def _():
            if full:
                dar_ref[...] = jnp.sum(accr[...], axis=0, keepdims=True)
                dai_ref[...] = jnp.sum(acci[...], axis=0, keepdims=True)
            else:
                er_ref[...] = g0r
                ei_ref[...] = g0i

    rev = lambda k, c: (nc - 1 - c, k)
    blk3 = lambda a: pl.BlockSpec((None,) + a.shape[1:], lambda k, c: (k, 0, 0))
    seg = pl.BlockSpec((N_SEG, BLOCK_STATE), lambda k, c: (0, k))
    st = pl.BlockSpec((rows, BLOCK_STATE), rev)
    ch = pl.BlockSpec((rows, LANES), rev)
    vec = pl.BlockSpec((1, LANES), lambda k, c: (0, k))
    if not full:
        st = pl.BlockSpec((rows, BLOCK_STATE), lambda k, c: (0, k))
    in_specs = [ch, ch if full else pl.BlockSpec((rows, LANES), lambda k, c: (0, k)), st, st,
                blk3(bd_re), blk3(bd_im), blk3(cd_re), blk3(cd_im), seg, seg, seg, seg, vec]
    args = [dy, u, h_re, h_im, bd_re, bd_im, cd_re, cd_im, ab_re, ab_imn, gin_re, gin_im, d_row]
    gbuf = [pltpu.VMEM((rows + N_SEG, BLOCK_STATE), F32)] * 2
    if full:
        row1 = pl.BlockSpec((1, BLOCK_STATE), lambda k, c: (0, k))
        out_specs = [ch, blk3(bd_re), blk3(bd_im), blk3(cd_re), blk3(cd_im), row1, row1, vec]
        out_shape = [jax.ShapeDtypeStruct(duz.shape, BF16),
                     jax.ShapeDtypeStruct(bd_re.shape, F32), jax.ShapeDtypeStruct(bd_im.shape, F32),
                     jax.ShapeDtypeStruct(cd_re.shape, F32), jax.ShapeDtypeStruct(cd_im.shape, F32),
                     jax.ShapeDtypeStruct((1, ns), F32), jax.ShapeDtypeStruct((1, ns), F32),
                     jax.ShapeDtypeStruct((1, w), F32)]
        scratch = gbuf + [pltpu.VMEM((N_SEG, BLOCK_STATE), F32)] * 2
        in_specs.append(pl.BlockSpec(memory_space=pl.ANY))
        args.append(duz)
        aliases = {len(args) - 1: 0}
    else:
        out_specs = [seg, seg]
        out_shape = [jax.ShapeDtypeStruct((N_SEG, ns), F32)] * 2
        scratch = gbuf
        aliases = {}
    return pl.pallas_call(
        body, name=name, grid=(nb, nc), in_specs=in_specs, out_specs=out_specs, out_shape=out_shape,
        input_output_aliases=aliases, scratch_shapes=scratch, compiler_params=_cparams("parallel", "arbitrary"),
    )(*args)


def _log_sigmoid(x):
    return jnp.minimum(x, 0.0) - jnp.log(1.0 + jnp.exp(-jnp.abs(x)))


def _tri(n, upper):
    r = lax.broadcasted_iota(jnp.int32, (n, n), 0)
    c = lax.broadcasted_iota(jnp.int32, (n, n), 1)
    return jnp.where((c >= r) if upper else (r >= c), 1.0, 0.0).astype(F32)


def _cum_fwd(f_logit, b_row, name):
    s, w = f_logit.shape
    t = _tile(s, 256, SUBLANES)

    def body(f_ref, b_ref, o_ref, carry):
        @pl.when(pl.program_id(0) == 0)
        def _():
            carry[...] = jnp.zeros_like(carry)

        lf = _log_sigmoid(f_ref[...] + b_ref[...])
        cum = jnp.dot(_tri(t, False), lf, precision=lax.Precision.HIGHEST, preferred_element_type=F32) + carry[...]
        o_ref[...] = cum * LOG2E
        carry[...] = cum[t - 1:t, :]

    return pl.pallas_call(
        body, name=name, grid=(s // t,),
        in_specs=[pl.BlockSpec((t, w), lambda i: (i, 0)), pl.BlockSpec((1, w), lambda i: (0, 0))],
        out_specs=pl.BlockSpec((t, w), lambda i: (i, 0)), out_shape=jax.ShapeDtypeStruct((s, w), F32),
        scratch_shapes=[pltpu.VMEM((1, w), F32)], compiler_params=_cparams("arbitrary"),
    )(f_logit, b_row)


def _cum_bwd(dcq, dck, f_logit, b_row, name):
    s, w = f_logit.shape
    t = _tile(s, 256, SUBLANES)
    nt = s // t

    def body(q_ref, k_ref, f_ref, b_ref, df_ref, db_ref, carry):
        @pl.when(pl.program_id(0) == 0)
        def _():
            carry[...] = jnp.zeros_like(carry)
            db_ref[...] = jnp.zeros_like(db_ref)

        dc = q_ref[...] - k_ref[...]
        rc = jnp.dot(_tri(t, True), dc, precision=lax.Precision.HIGHEST, preferred_element_type=F32) + carry[...]
        carry[...] = rc[0:1, :]
        df = rc * (1.0 - jax.nn.sigmoid(f_ref[...] + b_ref[...]))
        df_ref[...] = df.astype(BF16)
        db_ref[...] += jnp.sum(df, axis=0, keepdims=True)

    rev = pl.BlockSpec((t, w), lambda i: (nt - 1 - i, 0))
    one = pl.BlockSpec((1, w), lambda i: (0, 0))
    return pl.pallas_call(
        body, name=name, grid=(nt,), in_specs=[rev, rev, rev, one], out_specs=[rev, one],
        out_shape=[jax.ShapeDtypeStruct((s, w), BF16), jax.ShapeDtypeStruct((1, w), F32)],
        scratch_shapes=[pltpu.VMEM((1, w), F32)], compiler_params=_cparams("arbitrary"),
    )(dcq, dck, f_logit, b_row)


def _head_col(cum_tile, h):
    lane = lax.broadcasted_iota(jnp.int32, cum_tile.shape, 1)
    return jnp.sum(jnp.where(lane == h, cum_tile, 0.0), axis=1, keepdims=True)


def _attn_tiles(s):
    return _tile(s, 512, LANES)


def _exp2_rows(sc, sub):
    return jnp.concatenate([jnp.exp2(sc[:, b * LANES:(b + 1) * LANES] - sub) for b in range(sc.shape[1] // LANES)], axis=1)


def _row_of(rep):
    return jnp.transpose(rep)[0:1, :]


def _causal(sc, keys_on_rows):
    r = lax.broadcasted_iota(jnp.int32, sc.shape, 0)
    c = lax.broadcasted_iota(jnp.int32, sc.shape, 1)
    return jnp.where((r <= c) if keys_on_rows else (c <= r), sc, NEG_INF)


def _fox_fwd(q2, kv, cum2_t, z, name):
    s, w = q2.shape
    nh = w // HEAD_DIM
    tq = _attn_tiles(s)
    nq = s // tq
    nt = (_DOT_DIMS["nt"], ((), ()))

    def body(q_ref, k_ref, v_ref, ct_ref, z_ref, o_ref, oz_ref, lse_row_ref, m_s, acc_s, vaug, s_buf):
        i = pl.program_id(1)

        @pl.when(i == 0)
        def _():
            vaug[:, :HEAD_DIM] = v_ref[...]
            vaug[:, HEAD_DIM:] = jnp.ones((s, LANES), BF16)

        qb = q_ref[...]
        m_s[...] = jnp.full_like(m_s, NEG_INF)
        acc_s[...] = jnp.zeros_like(acc_s)

        def scores(j):
            off = pl.multiple_of(j * tq, tq)
            return lax.dot_general(qb, k_ref[pl.ds(off, tq), :], nt, preferred_element_type=F32) - ct_ref[:, pl.ds(off, tq)]

        def softmax_pv(j, sc):
            m_old = m_s[...]
            m_new = jnp.maximum(m_old, jnp.max(sc, axis=1, keepdims=True))
            p = _exp2_rows(sc, m_new)
            alpha = jnp.exp2(m_old - m_new)
            pv = jnp.dot(p.astype(BF16), vaug[pl.ds(pl.multiple_of(j * tq, tq), tq), :], preferred_element_type=F32)
            acc_s[...] = jnp.concatenate([alpha, alpha], axis=1) * acc_s[...] + pv
            m_s[...] = m_new

        s_buf[...] = scores(0)

        def loop(j, carry):
            nxt = scores(j + 1)
            softmax_pv(j, s_buf[...])
            s_buf[...] = nxt
            return carry

        lax.fori_loop(0, i, loop, 0)
        softmax_pv(i, _causal(s_buf[...], False))
        l = acc_s[:, HEAD_DIM:]
        o = acc_s[:, :HEAD_DIM] / l
        o_ref[...] = o
        oz_ref[...] = (o * _silu(z_ref[...].astype(F32))).astype(BF16)
        lse_row_ref[...] = _row_of(m_s[...] + jnp.log(l) * LOG2E)

    return pl.pallas_call(
        body, name=name, grid=(nh, nq),
        in_specs=[pl.BlockSpec((tq, HEAD_DIM), lambda h, i: (i, h)),
                  pl.BlockSpec((s, HEAD_DIM), lambda h, i: (0, h)),
                  pl.BlockSpec((s, HEAD_DIM), lambda h, i: (0, nh + h)),
                  pl.BlockSpec((None, 1, s), lambda h, i: (h, 0, 0)),
                  pl.BlockSpec((tq, HEAD_DIM), lambda h, i: (i, h))],
        out_specs=[pl.BlockSpec((tq, HEAD_DIM), lambda h, i: (i, h)),
                   pl.BlockSpec((tq, HEAD_DIM), lambda h, i: (i, h)),
                   pl.BlockSpec((None, 1, tq), lambda h, i: (h, 0, i))],
        out_shape=[jax.ShapeDtypeStruct((s, w), F32), jax.ShapeDtypeStruct((s, w), BF16),
                   jax.ShapeDtypeStruct((nh, 1, s), F32)],
        scratch_shapes=[pltpu.VMEM((tq, LANES), F32), pltpu.VMEM((tq, HEAD_DIM + LANES), F32),
                        pltpu.VMEM((s, HEAD_DIM + LANES), BF16), pltpu.VMEM((tq, tq), F32)],
        compiler_params=_cparams("arbitrary", "arbitrary"),
    )(q2, kv, kv, cum2_t, z)


def _fox_bwd(q2, kv, do, o, lse2_t, cum2, dqz, name):
    s, w = q2.shape
    nh = w // HEAD_DIM
    tk = _attn_tiles(s)
    nk = s // tk
    scale = HEAD_DIM ** -0.5
    nt = (_DOT_DIMS["nt"], ((), ()))
    tn = (_DOT_DIMS["tn"], ((), ()))

    def body(q_ref, k_ref, v_ref, do_ref, o_ref, lse_ref, c_ref, _, dk_ref, dv_ref, dq_ref, dcq_ref, dck_ref,
             dk_s, dv_s, dc_s, dq_s, dcq_s, dl_s, s_buf, dp_buf):
        h, j = pl.program_id(0), pl.program_id(1)

        @pl.when(j == 0)
        def _():
            dq_s[...] = jnp.zeros_like(dq_s)
            dcq_s[...] = jnp.zeros_like(dcq_s)
            for i in range(nk):
                rows = pl.ds(i * tk, tk)
                d = jnp.sum(do_ref[rows, :].astype(F32) * o_ref[rows, :], axis=1, keepdims=True)
                dl_s[:, i * tk:(i + 1) * tk] = _row_of(jnp.broadcast_to(d, (tk, LANES)))

        kb = k_ref[...]
        vb = v_ref[...]
        ck = jnp.broadcast_to(_head_col(c_ref[...], h), (tk, LANES))
        dk_s[...] = jnp.zeros_like(dk_s)
        dv_s[...] = jnp.zeros_like(dv_s)
        dc_s[...] = jnp.zeros_like(dc_s)

        def scores(i):
            off = pl.multiple_of(i * tk, tk)
            sc = lax.dot_general(kb, q_ref[pl.ds(off, tk), :], nt, preferred_element_type=F32) - lse_ref[:, pl.ds(off, tk)]
            dp = lax.dot_general(vb, do_ref[pl.ds(off, tk), :], nt, preferred_element_type=F32) - dl_s[:, pl.ds(off, tk)]
            return sc, dp

        def accumulate(i, sc, dp):
            off = pl.multiple_of(i * tk, tk)
            p = _exp2_rows(sc, ck)
            dv_s[...] += jnp.dot(p.astype(BF16), do_ref[pl.ds(off, tk), :], preferred_element_type=F32)
            ds = p * dp
            dsb = ds.astype(BF16)
            dk_s[...] += jnp.dot(dsb, q_ref[pl.ds(off, tk), :], preferred_element_type=F32)
            dq_s[pl.ds(off, tk), :] += lax.dot_general(dsb, kb, tn, preferred_element_type=F32)
            dcq_s[:, pl.ds(off, tk)] += jnp.sum(ds, axis=0, keepdims=True)
            part = ds[:, :LANES]
            for b in range(1, tk // LANES):
                part = part + ds[:, b * LANES:(b + 1) * LANES]
            dc_s[...] += part

        sc0, dp0 = scores(j)
        s_buf[...] = _causal(sc0, True)
        dp_buf[...] = dp0

        def loop(i, carry):
            nxt = scores(i + 1)
            accumulate(i, s_buf[...], dp_buf[...])
            s_buf[...], dp_buf[...] = nxt
            return carry

        lax.fori_loop(j, nk - 1, loop, 0)
        accumulate(nk - 1, s_buf[...], dp_buf[...])
        dk_ref[...] = (dk_s[...] * (1.0 / LOG2E)).astype(BF16)
        dv_ref[...] = dv_s[...].astype(BF16)
        dck_ref[...] = jnp.sum(jnp.transpose(dc_s[...]), axis=0, keepdims=True)

        @pl.when(j == nk - 1)
        def _():
            dq_ref[...] = (dq_s[...] * scale).astype(BF16)
            dcq_ref[...] = dcq_s[...]

    col = pl.BlockSpec((s, HEAD_DIM), lambda h, j: (0, h))
    row = pl.BlockSpec((None, 1, s), lambda h, j: (h, 0, 0))
    kspec = pl.BlockSpec((tk, HEAD_DIM), lambda h, j: (j, h))
    return pl.pallas_call(
        body, name=name, grid=(nh, nk),
        in_specs=[col, kspec, pl.BlockSpec((tk, HEAD_DIM), lambda h, j: (j, nh + h)), col, col, row,
                  pl.BlockSpec((tk, LANES), lambda h, j: (j, 0)), pl.BlockSpec(memory_space=pl.ANY)],
        out_specs=[kspec, kspec, col, row, pl.BlockSpec((None, 1, tk), lambda h, j: (h, 0, j))],
        out_shape=[jax.ShapeDtypeStruct((s, w), BF16), jax.ShapeDtypeStruct((s, w), BF16),
                   jax.ShapeDtypeStruct(dqz.shape, BF16), jax.ShapeDtypeStruct((nh, 1, s), F32),
                   jax.ShapeDtypeStruct((nh, 1, s), F32)],
        input_output_aliases={7: 2},
        scratch_shapes=[pltpu.VMEM((tk, HEAD_DIM), F32), pltpu.VMEM((tk, HEAD_DIM), F32), pltpu.VMEM((tk, LANES), F32),
                        pltpu.VMEM((s, HEAD_DIM), F32), pltpu.VMEM((1, s), F32), pltpu.VMEM((1, s), F32),
                        pltpu.VMEM((tk, tk), F32), pltpu.VMEM((tk, tk), F32)],
        compiler_params=_cparams("arbitrary", "arbitrary"),
    )(q2, kv, kv, do, o, lse2_t, cum2, dqz)


def _fox_bwd_dq(q2, kv, do, o, lse2, cum2_t, dqz, name):
    s, w = q2.shape
    nh = w // HEAD_DIM
    tq = _attn_tiles(s)
    nq = s // tq
    scale = HEAD_DIM ** -0.5
    nt = (_DOT_DIMS["nt"], ((), ()))

    def body(q_ref, k_ref, v_ref, do_ref, o_ref, lse_ref, ct_ref, _, dq_ref, dl_ref, dcq_ref, acc_s, dc_s):
        i = pl.program_id(1)
        qb = q_ref[...]
        dob = do_ref[...]
        lse = lse_ref[...]
        delta = jnp.broadcast_to(jnp.sum(dob.astype(F32) * o_ref[...], axis=1, keepdims=True), (tq, LANES))
        acc_s[...] = jnp.zeros_like(acc_s)
        dc_s[...] = jnp.zeros_like(dc_s)

        def tile(j, masked):
            off = pl.multiple_of(j * tq, tq)
            kb = k_ref[pl.ds(off, tq), :]
            sc = lax.dot_general(qb, kb, nt, preferred_element_type=F32) - ct_ref[:, pl.ds(off, tq)]
            if masked:
                sc = _causal(sc, False)
            p = _exp2_rows(sc, lse)
            dp = lax.dot_general(dob, v_ref[pl.ds(off, tq), :], nt, preferred_element_type=F32)
            ds = p * (dp - jnp.concatenate([delta] * (tq // LANES), axis=1))
            acc_s[...] += jnp.dot(ds.astype(BF16), kb, preferred_element_type=F32)
            part = ds[:, :LANES]
            for b in range(1, tq // LANES):
                part = part + ds[:, b * LANES:(b + 1) * LANES]
            dc_s[...] += part

        def loop(j, carry):
            tile(j, False)
            return carry

        lax.fori_loop(0, i, loop, 0)
        tile(i, True)
        dq_ref[...] = (acc_s[...] * scale).astype(BF16)
        dl_ref[...] = _row_of(delta)
        dcq_ref[...] = jnp.sum(jnp.transpose(dc_s[...]), axis=0, keepdims=True)

    qspec = pl.BlockSpec((tq, HEAD_DIM), lambda h, i: (i, h))
    rep = pl.BlockSpec((None, tq, LANES), lambda h, i: (h, i, 0))
    rowspec = pl.BlockSpec((None, 1, tq), lambda h, i: (h, 0, i))
    return pl.pallas_call(
        body, name=name, grid=(nh, nq),
        in_specs=[qspec,
                  pl.BlockSpec((s, HEAD_DIM), lambda h, i: (0, h)),
                  pl.BlockSpec((s, HEAD_DIM), lambda h, i: (0, nh + h)),
                  qspec, qspec, rep,
                  pl.BlockSpec((None, 1, s), lambda h, i: (h, 0, 0)),
                  pl.BlockSpec(memory_space=pl.ANY)],
        out_specs=[qspec, rowspec, rowspec],
        out_shape=[jax.ShapeDtypeStruct(dqz.shape, BF16), jax.ShapeDtypeStruct((nh, 1, s), F32),
                   jax.ShapeDtypeStruct((nh, 1, s), F32)],
        input_output_aliases={7: 0},
        scratch_shapes=[pltpu.VMEM((tq, HEAD_DIM), F32), pltpu.VMEM((tq, LANES), F32)],
        compiler_params=_cparams("parallel", "arbitrary"),
    )(q2, kv, kv, do, o, lse2, cum2_t, dqz)


def _fox_bwd_dkv(q2, kv, do, lse2_t, delta_t, cum2, name):
    s, w = q2.shape
    nh = w // HEAD_DIM
    tk = _attn_tiles(s)
    nk = s // tk
    nt = (_DOT_DIMS["nt"], ((), ()))

    def body(q_ref, k_ref, v_ref, do_ref, lse_ref, dl_ref, c_ref, dk_ref, dv_ref, dck_ref, dk_s, dv_s, dc_s, s_buf, dp_buf):
        h, j = pl.program_id(0), pl.program_id(1)
        kb = k_ref[...]
        vb = v_ref[...]
        ck = jnp.broadcast_to(_head_col(c_ref[...], h), (tk, LANES))
        dk_s[...] = jnp.zeros_like(dk_s)
        dv_s[...] = jnp.zeros_like(dv_s)
        dc_s[...] = jnp.zeros_like(dc_s)

        def scores(i):
            off = pl.multiple_of(i * tk, tk)
            sc = lax.dot_general(kb, q_ref[pl.ds(off, tk), :], nt, preferred_element_type=F32) - lse_ref[:, pl.ds(off, tk)]
            dp = lax.dot_general(vb, do_ref[pl.ds(off, tk), :], nt, preferred_element_type=F32) - dl_ref[:, pl.ds(off, tk)]
            return sc, dp

        def accumulate(i, sc, dp):
            off = pl.multiple_of(i * tk, tk)
            p = _exp2_rows(sc, ck)
            dv_s[...] += jnp.dot(p.astype(BF16), do_ref[pl.ds(off, tk), :], preferred_element_type=F32)
            ds = p * dp
            dk_s[...] += jnp.dot(ds.astype(BF16), q_ref[pl.ds(off, tk), :], preferred_element_type=F32)
            part = ds[:, :LANES]
            for b in range(1, tk // LANES):
                part = part + ds[:, b * LANES:(b + 1) * LANES]
            dc_s[...] += part

        sc0, dp0 = scores(j)
        s_buf[...] = _causal(sc0, True)
        dp_buf[...] = dp0

        def loop(i, carry):
            nxt = scores(i + 1)
            accumulate(i, s_buf[...], dp_buf[...])
            s_buf[...], dp_buf[...] = nxt
            return carry

        lax.fori_loop(j, nk - 1, loop, 0)
        accumulate(nk - 1, s_buf[...], dp_buf[...])
        dk_ref[...] = (dk_s[...] * (1.0 / LOG2E)).astype(BF16)
        dv_ref[...] = dv_s[...].astype(BF16)
        dck_ref[...] = jnp.sum(jnp.transpose(dc_s[...]), axis=0, keepdims=True)

    col = pl.BlockSpec((s, HEAD_DIM), lambda h, j: (0, h))
    row = pl.BlockSpec((None, 1, s), lambda h, j: (h, 0, 0))
    kspec = pl.BlockSpec((tk, HEAD_DIM), lambda h, j: (j, h))
    return pl.pallas_call(
        body, name=name, grid=(nh, nk),
        in_specs=[col, kspec, pl.BlockSpec((tk, HEAD_DIM), lambda h, j: (j, nh + h)), col, row, row,
                  pl.BlockSpec((tk, LANES), lambda h, j: (j, 0))],
        out_specs=[kspec, kspec, pl.BlockSpec((None, 1, tk), lambda h, j: (h, 0, j))],
        out_shape=[jax.ShapeDtypeStruct((s, w), BF16), jax.ShapeDtypeStruct((s, w), BF16),
                   jax.ShapeDtypeStruct((nh, 1, s), F32)],
        scratch_shapes=[pltpu.VMEM((tk, HEAD_DIM), F32), pltpu.VMEM((tk, HEAD_DIM), F32),
                        pltpu.VMEM((tk, LANES), F32), pltpu.VMEM((tk, tk), F32), pltpu.VMEM((tk, tk), F32)],
        compiler_params=_cparams("parallel", "arbitrary"),
    )(q2, kv, kv, do, lse2_t, delta_t, cum2)


_ALL_PEERS = tuple(range(1, N_DEV))
_CHIP_PEERS = (1, 2, 4, 6)


def _exchange_copies(ins, outs, send_sems, recv_sems, local_sems, scatter, peers=_ALL_PEERS):
    x, y, c = (lax.axis_index(a) for a in MESH_AXES)
    me = 4 * x + 2 * y + c
    local, remote = [], []
    for a in range(len(ins)):
        local.append(pltpu.make_async_copy(ins[a].at[me] if scatter else ins[a], outs[a].at[me], local_sems.at[a]))
        for k in peers:
            px, py, pc = (1 - x if k & 4 else x), (1 - y if k & 2 else y), (1 - c if k & 1 else c)
            remote.append(pltpu.make_async_remote_copy(
                src_ref=ins[a].at[4 * px + 2 * py + pc] if scatter else ins[a], dst_ref=outs[a].at[me],
                send_sem=send_sems.at[a * (N_DEV - 1) + k - 1], recv_sem=recv_sems.at[a * (N_DEV - 1) + k - 1],
                device_id=(px, py, pc), device_id_type=pl.DeviceIdType.MESH))
    return local, remote


def _exchange_out_shapes(arrs, scatter):
    return [((N_DEV,) + a.shape[1:]) if scatter else ((N_DEV,) + a.shape) for a in arrs]


def _exchange(arrs, scatter, name, peers=_ALL_PEERS):
    n = len(arrs)

    def body(*refs):
        local, remote = _exchange_copies(refs[:n], refs[n:2 * n], *refs[2 * n:], scatter, peers)
        for cp in local + remote:
            cp.start()
        for cp in remote:
            cp.wait_send()
            cp.wait_recv()
        for cp in local:
            cp.wait()

    out_shape = [jax.ShapeDtypeStruct(s, a.dtype) for s, a in zip(_exchange_out_shapes(arrs, scatter), arrs)]
    return pl.pallas_call(
        body, name=name, out_shape=out_shape,
        in_specs=[pl.BlockSpec(memory_space=pl.ANY)] * n, out_specs=[pl.BlockSpec(memory_space=pl.ANY)] * n,
        scratch_shapes=[pltpu.SemaphoreType.DMA((n * (N_DEV - 1),)), pltpu.SemaphoreType.DMA((n * (N_DEV - 1),)),
                        pltpu.SemaphoreType.DMA((n,))],
    )(*arrs)


_HBM = pl.BlockSpec(memory_space=pltpu.HBM)
_SEM = pl.BlockSpec(memory_space=pltpu.SEMAPHORE)


def _exchange_start(arrs, scatter, name, after=(), peers=_ALL_PEERS):
    n = len(arrs)
    after = list(after)
    lands = [lax.empty(s, a.dtype) for s, a in zip(_exchange_out_shapes(arrs, scatter), arrs)]

    def body(*refs):
        ins, outs = refs[:n], refs[n:2 * n]
        send_sems, recv_sems, local_sems = refs[2 * n + len(after):2 * n + len(after) + 3]
        token = refs[-1]
        local, remote = _exchange_copies(ins, outs, send_sems, recv_sems, local_sems, scatter, peers)
        for cp in local + remote:
            cp.start()
        token[...] = jnp.zeros_like(token)

    hbm = lambda a: pltpu.HBM(a.shape, a.dtype)
    res = pl.pallas_call(
        body, name=name,
        out_shape=(pltpu.SemaphoreType.DMA((n * (N_DEV - 1),)), pltpu.SemaphoreType.DMA((n * (N_DEV - 1),)),
                   pltpu.SemaphoreType.DMA((n,)), *[hbm(a) for a in arrs], *[hbm(a) for a in lands],
                   jax.ShapeDtypeStruct((SUBLANES, LANES), F32)),
        in_specs=[_HBM] * (2 * n) + [pl.BlockSpec(memory_space=pl.ANY)] * len(after),
        out_specs=(_SEM, _SEM, _SEM, *[_HBM] * (2 * n), pl.BlockSpec(memory_space=pltpu.VMEM)),
        input_output_aliases={i: 3 + i for i in range(2 * n)},
        compiler_params=pltpu.CompilerParams(has_side_effects=pltpu.SideEffectType.DATAFLOW_SIDE_EFFECTING),
    )(*[pltpu.with_memory_space_constraint(a, pltpu.HBM) for a in list(arrs) + lands], *after)
    return (n, scatter, res[:3], res[3:3 + n], res[3 + n:3 + 2 * n], peers), res[-1]


def _exchange_wait(state, after, name):
    n, scatter, sems, srcs, lands, peers = state
    after = list(after) if isinstance(after, (list, tuple)) else [after]

    def body(*refs):
        ins, outs = refs[:n], refs[n:2 * n]
        send_sems, recv_sems, local_sems = refs[2 * n:2 * n + 3]
        local, remote = _exchange_copies(ins, outs, send_sems, recv_sems, local_sems, scatter, peers)
        for cp in remote:
            cp.wait_send()
            cp.wait_recv()
        for cp in local:
            cp.wait()

    hbm = lambda a: pltpu.HBM(a.shape, a.dtype)
    res = pl.pallas_call(
        body, name=name,
        out_shape=(*[hbm(a) for a in srcs], *[hbm(a) for a in lands]),
        in_specs=[_HBM] * (2 * n) + [_SEM] * 3 + [pl.BlockSpec(memory_space=pl.ANY)] * len(after),
        out_specs=tuple([_HBM] * (2 * n)),
        input_output_aliases={i: i for i in range(2 * n)},
        compiler_params=pltpu.CompilerParams(has_side_effects=pltpu.SideEffectType.DATAFLOW_SIDE_EFFECTING),
    )(*srcs, *lands, *sems, *after)
    return list(res[n:])


def _forward_to_sibling(slots, name):
    n = len(slots)
    hops = (2, 4, 6)

    def body(*refs):
        ins, outs, (send_sems, recv_sems) = refs[:n], refs[n:2 * n], refs[2 * n:]
        x, y, c = (lax.axis_index(a) for a in MESH_AXES)
        copies = []
        for a in range(n):
            for i, k in enumerate(hops):
                slot = 4 * (1 - x if k & 4 else x) + 2 * (1 - y if k & 2 else y) + c
                copies.append(pltpu.make_async_remote_copy(
                    src_ref=ins[a].at[slot], dst_ref=outs[a].at[slot],
                    send_sem=send_sems.at[a * len(hops) + i], recv_sem=recv_sems.at[a * len(hops) + i],
                    device_id=(x, y, 1 - c), device_id_type=pl.DeviceIdType.MESH))
        for cp in copies:
            cp.start()
        for cp in copies:
            cp.wait_send()
            cp.wait_recv()

    return pl.pallas_call(
        body, name=name, out_shape=[jax.ShapeDtypeStruct(s.shape, s.dtype) for s in slots],
        in_specs=[pl.BlockSpec(memory_space=pl.ANY)] * n, out_specs=[pl.BlockSpec(memory_space=pl.ANY)] * n,
        input_output_aliases={i: i for i in range(n)},
        scratch_shapes=[pltpu.SemaphoreType.DMA((n * len(hops),)), pltpu.SemaphoreType.DMA((n * len(hops),))],
    )(*slots)


def _adamw_math(w, g, m, v):
    m = ADAM_B1 * m + (1.0 - ADAM_B1) * g
    v = ADAM_B2 * v + (1.0 - ADAM_B2) * (g * g)
    m_hat = m / (1.0 - ADAM_B1 ** ADAM_STEP)
    v_hat = v / (1.0 - ADAM_B2 ** ADAM_STEP)
    return -ADAM_LR * (m_hat / (jnp.sqrt(v_hat) + ADAM_EPS) + ADAM_WD * w), m, v


def _slot_sum(p_ref):
    g = p_ref[0].astype(F32)
    for d in range(1, p_ref.shape[0]):
        g = g + p_ref[d].astype(F32)
    return g


def _adamw_tile(r, c):
    return _tile(r, max(SUBLANES, (256 * 1024) // c // SUBLANES * SUBLANES), SUBLANES)


def _adamw(parts, w, m, v, name):
    r, c = w.shape[-2:]
    by_cols = r % SUBLANES != 0
    tr, tc = (r, _tile(c, 256)) if by_cols else (_adamw_tile(r, c), c)

    def body(p_ref, w_ref, m_ref, v_ref, g_ref, d_ref, nm_ref, nv_ref):
        g = _slot_sum(p_ref)
        g_ref[...] = g
        d_ref[...], nm_ref[...], nv_ref[...] = _adamw_math(w_ref[...], g, m_ref[...], v_ref[...])

    pos = (lambda i: (0, i)) if by_cols else (lambda i: (i, 0))
    if w.ndim == 3:
        blk = pl.BlockSpec((None, tr, tc), lambda i: (0,) + pos(i))
    else:
        blk = pl.BlockSpec((tr, tc), pos)
    sh = jax.ShapeDtypeStruct(w.shape, F32)
    return pl.pallas_call(
        body, name=name, grid=(c // tc if by_cols else r // tr,),
        in_specs=[pl.BlockSpec((parts.shape[0], tr, tc), lambda i: (0,) + pos(i)), blk, blk, blk],
        out_specs=[blk] * 4, out_shape=[sh] * 4, compiler_params=_cparams("parallel"),
    )(parts, w, m, v)


def _sum_parts(parts, name):
    _, r, c = parts.shape
    tr = _adamw_tile(r, c)

    def body(p_ref, o_ref):
        o_ref[...] = _slot_sum(p_ref)

    return pl.pallas_call(
        body, name=name, grid=(r // tr,),
        in_specs=[pl.BlockSpec((parts.shape[0], tr, c), lambda i: (0, i, 0))],
        out_specs=pl.BlockSpec((tr, c), lambda i: (i, 0)), out_shape=jax.ShapeDtypeStruct((r, c), F32),
        compiler_params=_cparams("parallel"),
    )(parts)


def _perm(a):
    s, d = a.shape
    return a.reshape(N_SEG, s // N_SEG, d).transpose(1, 0, 2).reshape(s, d)


def _unperm(a):
    s, d = a.shape
    return a.reshape(s // N_SEG, N_SEG, d).transpose(1, 0, 2).reshape(s, d)


def _lane_pad(a, width=LANES):
    return jnp.pad(a, ((0, 0), (0, width - a.shape[1])))


def _local_step(x, target, norm_pre, norm_post, kv_norm, kv_b_f, a_re, a_im, log_dt, b_re, b_im, c_re, c_im, comm):
    s, d = x.shape
    g, p = a_re.shape
    w = g * S5_GROUP
    fw = d
    nh = fw // HEAD_DIM
    seg_len = s // N_SEG
    row = lambda v: v.reshape(1, -1)
    g_pre0, g_pre1, g_post0, g_post1, g_kv = row(norm_pre[0]), row(norm_pre[1]), row(norm_post[0]), row(norm_post[1]), row(kv_norm)

    ldt = log_dt.reshape(g, 1)
    abr, abi, cr, ci = _s5_disc_fwd(a_re, a_im, ldt)
    cr_col, ci_col = cr.reshape(g * p, 1), ci.reshape(g * p, 1)
    b_re2, b_im2 = b_re.reshape(g * p, S5_GROUP), b_im.reshape(g * p, S5_GROUP)
    bb_re, bb_im = _s5_bbar_fwd(cr_col, ci_col, b_re2, b_im2)
    bd_re = _block_diag_in(bb_re.reshape(g, p, S5_GROUP)).astype(BF16)
    bd_im = _block_diag_in(bb_im.reshape(g, p, S5_GROUP)).astype(BF16)
    cd_re = _block_diag_out(c_re).astype(BF16)
    cd_im = _block_diag_out(-c_im).astype(BF16)
    ab_re = jnp.broadcast_to(abr.reshape(1, g * p), (N_SEG, g * p))
    ab_im = jnp.broadcast_to(abi.reshape(1, g * p), (N_SEG, g * p))
    zero_seg = jnp.zeros((N_SEG, g * p), F32)

    xn0 = _norm_cast(x, g_pre0 + comm.token, "norm_pre0", x_kind="nat")
    w_in = comm.weight("s5_w_in", [xn0, bd_re, bd_im, cd_re, cd_im, ab_re, ab_im])
    d_row, bglu_row = row(comm.vector("s5_d")), row(comm.vector("s5_b_glu"))
    u = _mm(xn0, w_in, "nn", F32, "s5_in_u", b_cols=(0, w), b_slots=True)
    z0 = _mm(xn0, w_in, "nn", BF16, "s5_in_z", b_cols=(w, w), b_slots=True)
    e_re, e_im = _s5_scan_fwd(u, bd_re, bd_im, cd_re, cd_im, ab_re, ab_im, zero_seg, zero_seg, d_row, False, "s5_scan_ends")
    i_re, i_im = _s5_seg_fix(e_re, e_im, ab_re, ab_im, seg_len, False, "s5_seg_fix")
    y_ssm, yg, h_re, h_im, _, _ = _s5_scan_fwd(u, bd_re, bd_im, cd_re, cd_im, ab_re, ab_im, i_re, i_im, d_row, True, "s5_scan")
    w_glu, w_out = comm.weight("s5_w_glu", yg), comm.weight("s5_w_out", yg)
    gp = _mm(yg, w_glu, "nn", BF16, "s5_glu")
    y3 = _s5_gate(y_ssm, gp, bglu_row, z0, "s5_gate")
    w_kvt, fw_in = comm.weight("kv_w", y3), comm.weight("fox_w_in", y3)
    w_ft = jnp.pad(w_kvt[2 * fw:], ((0, LANES - nh), (0, 0)))
    o0 = _mm(y3, w_out, "nn", F32, "s5_out")
    r0 = _post_norm(o0, g_post0 + comm.late_token, "norm_post0", out_kind="nat")

    h1, hn_kv, xn1 = _resid_norm2(x, r0, g_kv, g_pre1, "resid_norms")
    kv = _mm(hn_kv, w_kvt, "nt", BF16, "kv_proj", b_rows=2 * fw)
    f_logit = _mm(hn_kv, w_ft, "nt", F32, "f_proj")
    bf_row = _lane_pad(row(kv_b_f))
    cum2 = _cum_fwd(f_logit, bf_row, "cum_fwd")
    cum2_t = cum2[:, :nh].T.reshape(nh, 1, s)
    q2 = _mm(xn1, fw_in, "nn", BF16, "fox_q", scale=HEAD_DIM ** -0.5 * LOG2E, b_cols=(0, fw), b_slots=True)
    z1 = _mm(xn1, fw_in, "nn", BF16, "fox_z", b_cols=(fw, fw), b_slots=True)
    o, oz, lse2_t = _fox_fwd(q2, kv, cum2_t, z1, "fox_fwd")
    fw_out = comm.weight("fox_w_out", oz)
    o1 = _mm(oz, fw_out, "nn", F32, "fox_out")
    dh2, do1, sq, dg_post1 = _post_norm_loss(o1, g_post1, h1, target, "norm_post1_loss")
    loss = 0.5 * jnp.sum(sq) / d

    d_fw_out = _mm(oz, do1, "tn", BF16, "fox_out_dw")
    d_oz = _mm(do1, fw_out, "nt", F32, "fox_out_dx")
    do, dqz = _gate_bwd(d_oz, o, z1, "fox_gate_bwd")
    dk, dv, dqz, dcq, dck = _fox_bwd(q2, kv, do, o, lse2_t, cum2, dqz, "fox_bwd")
    d_fw_in = _mm(xn1, dqz, "tn", BF16, "fox_in_dw", col_slots=True)
    dxn1 = _mm(dqz, fw_in, "nt", F32, "fox_in_dx", b_slots=True)
    dcq_sl = _lane_pad(dcq.reshape(nh, s).T)
    dck_sl = _lane_pad(dck.reshape(nh, s).T)
    df, db_f = _cum_bwd(dcq_sl, dck_sl, f_logit, bf_row, "cum_bwd")
    dkv = _concat_cast(dk, dv, "fox_dkv")
    d_w_kvmt = _mm(dkv, hn_kv, "tn", BF16, "kv_dw")
    d_w_ft = _mm(df, hn_kv, "tn", BF16, "f_dw")
    dhn_f = _mm(df, w_ft, "nn", F32, "f_dx")
    dhn_kv = _mm(dkv, w_kvt, "nn", F32, "kv_dx", add=dhn_f, b_rows=2 * fw)
    d_w_kvt = jnp.concatenate([d_w_kvmt, d_w_ft[:nh]], axis=0)
    tok = comm.send_grads(dict(fox_w_out=d_fw_out, fox_w_in=d_fw_in, kv_w=d_w_kvt), "exchange_fox")
    dh1, dg_pre1, dg_kv = _norm_bwd2(dh2, h1, dxn1, dhn_kv, g_pre1, g_kv, "resid_norms_bwd")

    do0, dg_post0 = _post_norm_bwd(dh1, o0, g_post0 + tok[0, 0], "norm_post0_bwd", dy_kind="nat")
    d_w_out = _mm(y3, do0, "tn", BF16, "s5_out_dw")
    dy3 = _mm(do0, w_out, "nt", F32, "s5_out_dx")
    duz, dgp, dyg_direct, db_glu = _s5_gate_bwd(dy3, y_ssm, gp, bglu_row, z0, "s5_gate_bwd")
    d_w_glu = _mm(yg, dgp, "tn", BF16, "s5_glu_dw")
    dyg = _mm(dgp, w_glu, "nt", F32, "s5_glu_dx", add=dyg_direct)
    dy_ssm = _gelu_bwd(dyg, y_ssm, "s5_gelu_bwd")
    d_row = d_row + comm.send_grads(dict(s5_w_out=d_w_out, s5_w_glu=d_w_glu), "exchange_s5")[0, 0]
    ab_imn = -ab_im
    ge_re, ge_im = _s5_scan_bwd(dy_ssm, u, h_re, h_im, bd_re, bd_im, cd_re, cd_im, ab_re, ab_imn, zero_seg, zero_seg,
                                d_row, False, "s5_adj_ends")
    gi_re, gi_im = _s5_seg_fix(ge_re, ge_im, ab_re, ab_imn, seg_len, True, "s5_adj_fix")
    duz, dbd_re, dbd_im, dcd_re, dcd_im, dab_re, dab_im, dd = _s5_scan_bwd(
        dy_ssm, u, h_re, h_im, bd_re, bd_im, cd_re, cd_im, ab_re, ab_imn, gi_re, gi_im, d_row, True, "s5_adj", duz=duz)
    d_w_in = _mm(xn0, duz, "tn", BF16, "s5_in_dw", col_slots=True)
    tok = comm.send_grads(dict(s5_w_in=d_w_in), "exchange_s5_in")
    dxn0 = _mm(duz, w_in, "nt", F32, "s5_in_dx", after=tok, b_slots=True)
    grad_x, dg_pre0 = _norm_bwd1(dh1, x, dxn0, g_pre0, "norm_pre0_bwd")

    dbb_re = _block_diag_in_extract(dbd_re, p, S5_GROUP).reshape(g * p, S5_GROUP)
    dbb_im = _block_diag_in_extract(dbd_im, p, S5_GROUP).reshape(g * p, S5_GROUP)
    dcr_col, dci_col, db_re, db_im = _s5_bbar_bwd(cr_col, ci_col, b_re2, b_im2, dbb_re, dbb_im)
    da_re, da_im, dldt = _s5_disc_bwd(a_re, a_im, ldt, dab_re.reshape(g, p), dab_im.reshape(g, p),
                                      dcr_col.reshape(g, p), dci_col.reshape(g, p))
    dc_re = _block_diag_out_extract(dcd_re, S5_GROUP, p)
    dc_im = -_block_diag_out_extract(dcd_im, S5_GROUP, p)

    small = dict(
        norm_pre=jnp.concatenate([dg_pre0, dg_pre1], axis=0), norm_post=jnp.concatenate([dg_post0, dg_post1], axis=0),
        s5_a_re=da_re, s5_a_im=da_im, s5_log_dt=dldt.reshape(g), s5_b_re=db_re.reshape(g, p, S5_GROUP),
        s5_b_im=db_im.reshape(g, p, S5_GROUP), s5_c_re=dc_re, s5_c_im=dc_im, s5_d=dd.reshape(-1),
        s5_b_glu=db_glu.reshape(-1), kv_norm=dg_kv.reshape(-1), kv_b_f=db_f[0, :nh])
    return loss, grad_x, small


_BIG = ("s5_w_in", "s5_w_glu", "s5_w_out", "kv_w", "fox_w_in", "fox_w_out")
_COL_SHARDED = ("s5_w_in", "fox_w_in")
_SMALL = ("norm_pre", "norm_post", "s5_a_re", "s5_a_im", "s5_log_dt", "s5_b_re", "s5_b_im", "s5_c_re", "s5_c_im",
          "s5_d", "s5_b_glu", "kv_norm", "kv_b_f")
_SMALL_SHARDED = ("s5_d", "s5_b_glu")
_PACK_QUANTUM = SUBLANES * LANES
_WEIGHTS = ('norm_pre', 'norm_post', 's5_w_in', 's5_a_re', 's5_a_im', 's5_log_dt', 's5_b_re', 's5_b_im', 's5_c_re', 's5_c_im',
            's5_d', 's5_w_glu', 's5_b_glu', 's5_w_out', 'kv_norm', 'kv_w', 'kv_b_f', 'fox_w_in', 'fox_w_out')


def _full_from_slots(name, slots):
    n, r, c = slots.shape
    if name in _COL_SHARDED:
        return slots.transpose(1, 0, 2).reshape(r, n * c)
    return slots.reshape(n * r, c)


def _slots_from_full(name, full):
    if name in _COL_SHARDED:
        r, nc = full.shape
        return full.reshape(r, N_DEV, nc // N_DEV).transpose(1, 0, 2)
    nr, c = full.shape
    return full.reshape(N_DEV, nr // N_DEV, c)


def _pack(vals):
    parts = []
    for v in vals:
        flat = v.reshape(-1)
        parts.append(jnp.pad(flat, (0, (-flat.shape[0]) % _PACK_QUANTUM)))
    total = sum(p.shape[0] for p in parts)
    parts.append(jnp.zeros(((-total) % (N_DEV * _PACK_QUANTUM),), F32))
    return jnp.concatenate(parts).reshape(-1, LANES)


def _unpack(packed, shapes):
    flat = packed.reshape(-1)
    out, off = [], 0
    for sh in shapes:
        n = math.prod(sh)
        out.append(flat[off:off + n].reshape(sh))
        off += n + (-n) % _PACK_QUANTUM
    return out


class _Comm:
    _GROUPS = (("s5_w_in",) + _SMALL_SHARDED, ("s5_w_glu", "s5_w_out"), ("kv_w", "fox_w_in"), ("fox_w_out",))
    _SLOT_FORM = ("s5_w_in", "fox_w_in")

    def __init__(self, shards, vectors, early=()):
        self._shards = {**shards, **vectors}
        self._full, self._gathers = {}, {}
        self._early = list(early)
        self.token = jnp.zeros((), F32)
        for group in self._GROUPS[:-1]:
            self.token = self.token + self._start(group, ())[0, 0]
        self.late_token = None
        self._sent = []

    def _start(self, group, after):
        state, tok = _exchange_start([self._shards[n] for n in group], False, "gather_start_" + group[0], after,
                                     peers=_CHIP_PEERS)
        self._gathers[group] = state
        return tok

    def vector(self, name):
        return self._full[name]

    def weight(self, name, after):
        if name not in self._full:
            group = next(g for g in self._GROUPS if name in g)
            if group == self._GROUPS[0]:
                after = (list(after) if isinstance(after, (list, tuple)) else [after]) + self._early
            slots = _exchange_wait(self._gathers.pop(group), after, "gather_wait_" + group[0])
            slots = _forward_to_sibling(slots, "gather_forward_" + group[0])
            for n, sl in zip(group, slots):
                if n in _SMALL_SHARDED:
                    self._full[n] = sl.reshape(-1)
                else:
                    self._full[n] = sl if n in self._SLOT_FORM else _full_from_slots(n, sl)
            if group == self._GROUPS[-2]:
                self.late_token = self._start(self._GROUPS[-1], [slots[0]])[0, 0]
        return self._full[name]

    def send_grads(self, grads, name):
        names = list(grads)
        slots = [grads[n] if grads[n].ndim == 3 else _slots_from_full(n, grads[n]).astype(BF16) for n in names]
        state, tok = _exchange_start(slots, True, name + "_start")
        self._sent.append((names, state, name + "_wait"))
        return tok

    def received_grads(self, after):
        for names, state, name in self._sent:
            for n, recv in zip(names, _exchange_wait(state, after, name)):
                yield n, recv


def kernel(x, norm_pre, norm_post, s5_w_in, s5_a_re, s5_a_im, s5_log_dt, s5_b_re, s5_b_im, s5_c_re, s5_c_im, s5_d, s5_w_glu, s5_b_glu, s5_w_out, kv_norm, kv_w, kv_b_f, fox_w_in, fox_w_out, loss_target, m_norm_pre, m_norm_post, m_s5_w_in, m_s5_a_re, m_s5_a_im, m_s5_log_dt, m_s5_b_re, m_s5_b_im, m_s5_c_re, m_s5_c_im, m_s5_d, m_s5_w_glu, m_s5_b_glu, m_s5_w_out, m_kv_norm, m_kv_w, m_kv_b_f, m_fox_w_in, m_fox_w_out, v_norm_pre, v_norm_post, v_s5_w_in, v_s5_a_re, v_s5_a_im, v_s5_log_dt, v_s5_b_re, v_s5_b_im, v_s5_c_re, v_s5_c_im, v_s5_d, v_s5_w_glu, v_s5_b_glu, v_s5_w_out, v_kv_norm, v_kv_w, v_kv_b_f, v_fox_w_in, v_fox_w_out):
    env = dict(locals())
    wts = {n: env[n] for n in _WEIGHTS}
    mom = {n: env["m_" + n] for n in _WEIGHTS}
    var = {n: env["v_" + n] for n in _WEIGHTS}
    me = 4 * lax.axis_index("x") + 2 * lax.axis_index("y") + lax.axis_index("c")
    shard2d = {n: (wts[n].T if n == "kv_w" else wts[n].reshape(wts[n].shape[-2:])) for n in _BIG}
    full_shape = {n: ((wts[n].size * N_DEV,) if n in _SMALL_SHARDED else wts[n].shape) for n in _SMALL}

    def spread(n, v):
        if n not in _SMALL_SHARDED:
            return v
        flat = v.reshape(-1)
        return lax.dynamic_update_slice(jnp.zeros(full_shape[n], F32), flat, (me * flat.shape[0],))

    packed = [_pack([spread(n, src[n]) for n in _SMALL] + [jnp.zeros((1,), F32)]) for src in (wts, mom, var)]
    comm = _Comm({n: shard2d[n].astype(BF16) for n in _BIG}, {n: wts[n].reshape(1, -1) for n in _SMALL_SHARDED}, packed)

    loss_local, grad_x, small = _local_step(
        x[0], loss_target[0], norm_pre, norm_post, kv_norm, kv_b_f, s5_a_re[0], s5_a_im[0], s5_log_dt[0],
        s5_b_re[0], s5_b_im[0], s5_c_re[0], s5_c_im[0], comm)

    small_pack = _pack([small[n] for n in _SMALL] + [loss_local.reshape(1)])
    slice_rows = small_pack.shape[0] // N_DEV
    small_state, small_tok = _exchange_start([small_pack.reshape(N_DEV, slice_rows, LANES)], True, "reduce_small_start")

    res = {}
    for n, recv in comm.received_grads([small_tok, grad_x]):
        if n == "kv_w":
            res[n] = [o.T for o in _adamw(recv, wts[n].T, mom[n].T, var[n].T, "adamw_" + n)]
        else:
            res[n] = _adamw(recv, wts[n], mom[n], var[n], "adamw_" + n)

    my_sum = _sum_parts(_exchange_wait(small_state, res[_BIG[0]][0], "reduce_small_wait")[0], "sum_small")
    g_all = _exchange([my_sum], False, "gather_small")[0].reshape(1, small_pack.shape[0], LANES)
    outs = _adamw(g_all, *packed, "adamw_small")
    unpacked = [_unpack(o, [full_shape[n] for n in _SMALL] + [(1,)]) for o in outs]
    loss = unpacked[0][-1][0]
    for i, n in enumerate(_SMALL):
        vals = [u[i] for u in unpacked]
        if n in _SMALL_SHARDED:
            k = wts[n].size
            vals = [lax.dynamic_slice(v, (me * k,), (k,)) for v in vals]
        res[n] = [v.reshape(wts[n].shape) for v in vals]

    return (loss, grad_x[None], *[res[n][0] for n in _WEIGHTS], *[res[n][1] for n in _WEIGHTS],
            *[res[n][2] for n in _WEIGHTS], *[res[n][3] for n in _WEIGHTS])
```

```python
import math

import jax
import jax.numpy as jnp
from jax import lax
from jax.experimental import pallas as pl
from jax.experimental.pallas import tpu as pltpu

F32 = jnp.float32
BF16 = jnp.bfloat16

N_DEV = 8
MESH_AXES = ("x", "y", "c")
S5_GROUP = 16
S5_STATE = 64
LANES = 128
SUBLANES = 8
GROUPS_PER_BLOCK = LANES // S5_GROUP
BLOCK_STATE = GROUPS_PER_BLOCK * S5_STATE
N_SEG = SUBLANES
HEAD_DIM = 128
RMS_EPS = 1e-6
NEG_INF = -1e30
LOG2E = math.log2(math.e)
ADAM_LR = 0.001
ADAM_B1 = 0.9
ADAM_B2 = 0.999
ADAM_EPS = 1e-08
ADAM_WD = 0.01
ADAM_STEP = 10
VMEM_LIMIT = 56 * 1024 * 1024


def _tile(n, pref, quantum=LANES):
    if n <= pref:
        return n
    t = (pref // quantum) * quantum
    while t >= quantum:
        if n % t == 0:
            return t
        t -= quantum
    return n


def _cparams(*sem):
    return pltpu.CompilerParams(dimension_semantics=sem if sem else None, vmem_limit_bytes=VMEM_LIMIT)


_DOT_DIMS = {"nn": ((1,), (0,)), "nt": ((1,), (1,)), "tn": ((0,), (0,))}


def _mm(a, b, mode, out_dtype, name, add=None, scale=None, b_cols=None, after=None, col_slots=False, b_slots=False,
        b_rows=None, epilogue=None):
    slot_w = b.shape[2] if b_slots else None
    b2d = (b.shape[1], b.shape[0] * b.shape[2]) if b_slots else b.shape
    b_shape = b2d if b_cols is None else (b2d[0], b_cols[1])
    if b_rows is not None:
        b_shape = (b_rows, b_shape[1])
    if mode == "nn":
        (M, K), (K2, N) = a.shape, b_shape
    elif mode == "nt":
        (M, K), (N, K2) = a.shape, b_shape
    else:
        (K, M), (K2, N) = a.shape, b_shape
    assert K == K2, (name, a.shape, b_shape)
    tm, tn, tk = _tile(M, 1024 if K <= 2048 else 512), (N // N_DEV if col_slots else _tile(N, 1024)), _tile(K, 4096)
    if b_slots and mode == "nn":
        tn = slot_w
    nk = K // tk
    dims = (_DOT_DIMS[mode], ((), ()))
    col0 = 0
    if b_cols is not None:
        assert mode != "tn" and b_cols[0] % (tn if mode == "nn" else tk) == 0
        col0 = b_cols[0] // (tn if mode == "nn" else tk)
    assert not b_slots or (mode == "nn" or (mode == "nt" and nk == 1 and b_cols is None))

    def body(*refs):
        a_ref, b_ref = refs[:2]
        c_ref = refs[2] if add is not None else None
        e_ref = refs[2 + (add is not None)] if epilogue is not None else None
        o_ref = refs[2 + (add is not None) + (epilogue is not None) + (after is not None)]
        if b_slots and mode == "nt":
            part = lax.dot_general(a_ref[:, :slot_w], b_ref[0], dims, preferred_element_type=F32)
            for sl in range(1, b_ref.shape[0]):
                part += lax.dot_general(a_ref[:, sl * slot_w:(sl + 1) * slot_w], b_ref[sl], dims, preferred_element_type=F32)
        else:
            part = lax.dot_general(a_ref[...], b_ref[...], dims, preferred_element_type=F32)

        def finish(r):
            if scale is not None:
                r = r * scale
            if add is not None:
                r = r + c_ref[...]
            if epilogue is not None:
                r = epilogue[0](r, e_ref[...])
            o_ref[...] = r.astype(out_dtype)

        if nk == 1:
            finish(part)
            return
        acc = refs[-1]
        k = pl.program_id(2)

        @pl.when(k == 0)
        def _():
            acc[...] = part

        @pl.when(jnp.logical_and(k > 0, k < nk - 1))
        def _():
            acc[...] += part

        @pl.when(k == nk - 1)
        def _():
            finish(acc[...] + part)

    if mode == "tn":
        a_spec = pl.BlockSpec((tk, tm), lambda i, j, k: (k, i))
    else:
        a_spec = pl.BlockSpec((tm, tk), lambda i, j, k: (i, k))
    if b_slots and mode == "nn":
        b_spec = pl.BlockSpec((None, tk, tn), lambda i, j, k: (j + col0, k, 0))
    elif b_slots:
        b_spec = pl.BlockSpec((b.shape[0], tn, slot_w), lambda i, j, k: (0, j, 0))
    elif mode == "nt":
        b_spec = pl.BlockSpec((tn, tk), lambda i, j, k: (j, k + col0))
    else:
        b_spec = pl.BlockSpec((tk, tn), lambda i, j, k: (k, j + col0))
    o_spec = pl.BlockSpec((tm, tn), lambda i, j, k: (i, j))
    in_specs = [a_spec, b_spec] + ([o_spec] if add is not None else [])
    args = (a, b) + ((add,) if add is not None else ())
    if epilogue is not None:
        in_specs.append(o_spec)
        args += (epilogue[1],)
    if after is not None:
        in_specs.append(pl.BlockSpec(after.shape, lambda i, j, k: (0, 0)))
        args += (after,)
    out_shape = jax.ShapeDtypeStruct((M, N), out_dtype)
    if col_slots:
        assert add is None
        o_spec = pl.BlockSpec((None, tm, tn), lambda i, j, k: (j, i, 0))
        out_shape = jax.ShapeDtypeStruct((N_DEV, M, tn), out_dtype)
    return pl.pallas_call(
        body, name=name, grid=(M // tm, N // tn, nk),
        in_specs=in_specs, out_specs=o_spec,
        out_shape=out_shape,
        scratch_shapes=[pltpu.VMEM((tm, tn), F32)] if nk > 1 else [],
        compiler_params=_cparams("parallel", "parallel", "arbitrary"),
    )(*args)


class _NatIn:
    def __init__(self, ref):
        self.ref = ref

    def __getitem__(self, idx):
        v = jnp.swapaxes(self.ref[...], 0, 1)
        return v.reshape(v.shape[0] * N_SEG, v.shape[2])


class _NatOut:
    def __init__(self, ref):
        self.ref = ref

    def __setitem__(self, idx, val):
        self.ref[...] = jnp.swapaxes(val.reshape(val.shape[0] // N_SEG, N_SEG, val.shape[1]), 0, 1)


def _rowcall(body, name, n_rows, ins, outs, tile_rows=256):
    tr = _tile(n_rows, tile_rows, SUBLANES * 2)
    n_in = len(ins)
    in_kinds = [k for _, k in ins]
    kinds = [k for _, _, k in outs]

    def kern(*refs):
        @pl.when(pl.program_id(0) == 0)
        def _():
            for r, kind in zip(refs[n_in:], kinds):
                if kind == "acc":
                    r[...] = jnp.zeros_like(r)

        wrapped = [_NatIn(r) if k == "nat" else r for r, k in zip(refs[:n_in], in_kinds)]
        wrapped += [_NatOut(r) if k == "nat" else r for r, k in zip(refs[n_in:], kinds)]
        body(*wrapped)

    in_specs, args = [], []
    for arr, kind in ins:
        if kind == "row":
            in_specs.append(pl.BlockSpec((tr, arr.shape[1]), lambda i: (i, 0)))
        elif kind == "nat":
            in_specs.append(pl.BlockSpec((N_SEG, tr // N_SEG, arr.shape[1]), lambda i: (0, i, 0)))
            arr = arr.reshape(N_SEG, n_rows // N_SEG, arr.shape[1])
        else:
            in_specs.append(pl.BlockSpec(arr.shape, lambda i, nd=arr.ndim: (0,) * nd))
        args.append(arr)
    out_specs, out_shape = [], []
    for width, dtype, kind in outs:
        if kind == "row":
            out_specs.append(pl.BlockSpec((tr, width), lambda i: (i, 0)))
            out_shape.append(jax.ShapeDtypeStruct((n_rows, width), dtype))
        elif kind == "right":
            out_specs.append(pl.BlockSpec((tr, width), lambda i: (i, 1)))
            out_shape.append(jax.ShapeDtypeStruct((n_rows, 2 * width), dtype))
        elif kind == "nat":
            out_specs.append(pl.BlockSpec((N_SEG, tr // N_SEG, width), lambda i: (0, i, 0)))
            out_shape.append(jax.ShapeDtypeStruct((N_SEG, n_rows // N_SEG, width), dtype))
        else:
            out_specs.append(pl.BlockSpec((1, width), lambda i: (0, 0)))
            out_shape.append(jax.ShapeDtypeStruct((1, width), F32))
    res = pl.pallas_call(
        kern, name=name, grid=(n_rows // tr,), in_specs=in_specs, out_specs=out_specs, out_shape=out_shape,
        compiler_params=_cparams("arbitrary"),
    )(*args)
    return [r.reshape(n_rows, r.shape[2]) if k == "nat" else r for r, k in zip(res, kinds)]


def _rstd(x):
    return lax.rsqrt(jnp.mean(x * x, axis=-1, keepdims=True) + RMS_EPS)


def _rms_bwd(x, g, dy):
    xh = x * _rstd(x)
    dxh = dy * g
    dx = _rstd(x) * (dxh - xh * jnp.mean(dxh * xh, axis=-1, keepdims=True))
    return dx, jnp.sum(dy * xh, axis=0, keepdims=True)


def _silu(z):
    return z * jax.nn.sigmoid(z)


def _norm_cast(x, g, name, x_kind="row"):
    def body(x_ref, g_ref, o_ref):
        x = x_ref[...]
        o_ref[...] = (x * _rstd(x) * g_ref[...]).astype(BF16)

    return _rowcall(body, name, x.shape[0], [(x, x_kind), (g, "full")], [(x.shape[1], BF16, "row")])[0]


def _resid_norm2(x, r0, g_kv, g_pre, name):
    def body(x_ref, r_ref, gk_ref, gp_ref, h_ref, nk_ref, np_ref):
        h = x_ref[...] + r_ref[...]
        h_ref[...] = h
        hn = h * _rstd(h)
        nk_ref[...] = (hn * gk_ref[...]).astype(BF16)
        np_ref[...] = (hn * gp_ref[...]).astype(BF16)

    d = x.shape[1]
    return _rowcall(body, name, x.shape[0], [(x, "row"), (r0, "row"), (g_kv, "full"), (g_pre, "full")],
                    [(d, F32, "row"), (d, BF16, "row"), (d, BF16, "row")])


def _post_norm(o, g, name, out_kind="row"):
    def body(o_ref, g_ref, r_ref):
        o = o_ref[...]
        r_ref[...] = o * _rstd(o) * g_ref[...]

    return _rowcall(body, name, o.shape[0], [(o, "row"), (g, "full")], [(o.shape[1], F32, out_kind)])[0]


def _post_norm_loss(o, g, h1, target, name):
    d = o.shape[1]

    def body(o_ref, g_ref, h_ref, t_ref, dh_ref, do_ref, acc_ref, dg_ref):
        o = o_ref[...]
        e = h_ref[...] + o * _rstd(o) * g_ref[...] - t_ref[...]
        dh = e * (1.0 / d)
        dh_ref[...] = dh
        acc_ref[...] += jnp.sum(e * e, axis=0, keepdims=True)
        dx, dg = _rms_bwd(o, g_ref[...], dh)
        do_ref[...] = dx.astype(BF16)
        dg_ref[...] += dg

    return _rowcall(body, name, o.shape[0], [(o, "row"), (g, "full"), (h1, "row"), (target, "row")],
                    [(d, F32, "row"), (d, BF16, "row"), (d, F32, "acc"), (d, F32, "acc")])


def _post_norm_bwd(dy, o, g, name, dy_kind="row"):
    def body(dy_ref, o_ref, g_ref, do_ref, dg_ref):
        dx, dg = _rms_bwd(o_ref[...], g_ref[...], dy_ref[...])
        do_ref[...] = dx.astype(BF16)
        dg_ref[...] += dg

    d = o.shape[1]
    return _rowcall(body, name, o.shape[0], [(dy, dy_kind), (o, "row"), (g, "full")], [(d, BF16, "row"), (d, F32, "acc")])


def _gate_bwd(d_oz, o, z, name):
    def body(d_ref, o_ref, z_ref, do_ref, dz_ref):
        _, vjp = jax.vjp(lambda o, z: o * _silu(z), o_ref[...], z_ref[...].astype(F32))
        do, dz = vjp(d_ref[...])
        do_ref[...] = do.astype(BF16)
        dz_ref[...] = dz.astype(BF16)

    w = o.shape[1]
    return _rowcall(body, name, o.shape[0], [(d_oz, "row"), (o, "row"), (z, "row")], [(w, BF16, "row"), (w, BF16, "right")])


def _norm_bwd2(dh2, h1, dxn1, dhn_kv, g_pre, g_kv, name):
    def body(dh2_ref, h_ref, d1_ref, dk_ref, gp_ref, gk_ref, dh1_ref, dgp_ref, dgk_ref):
        h = h_ref[...]
        dx1, dg1 = _rms_bwd(h, gp_ref[...], d1_ref[...])
        dxk, dgk = _rms_bwd(h, gk_ref[...], dk_ref[...])
        dh1_ref[...] = dh2_ref[...] + dx1 + dxk
        dgp_ref[...] += dg1
        dgk_ref[...] += dgk

    d = h1.shape[1]
    return _rowcall(body, name, h1.shape[0],
                    [(dh2, "row"), (h1, "row"), (dxn1, "row"), (dhn_kv, "row"), (g_pre, "full"), (g_kv, "full")],
                    [(d, F32, "row"), (d, F32, "acc"), (d, F32, "acc")])


def _norm_bwd1(dres, x, dxn, g, name):
    def body(dr_ref, x_ref, dn_ref, g_ref, dx_ref, dg_ref):
        dx, dg = _rms_bwd(x_ref[...], g_ref[...], dn_ref[...])
        dx_ref[...] = dr_ref[...] + dx
        dg_ref[...] += dg

    d = x.shape[1]
    return _rowcall(body, name, x.shape[0], [(dres, "nat"), (x, "nat"), (dxn, "row"), (g, "full")],
                    [(d, F32, "nat"), (d, F32, "acc")])


def _s5_gate(y_ssm, gp, b_glu, z, name):
    def body(y_ref, gp_ref, b_ref, z_ref, o_ref):
        yg = jax.nn.gelu(y_ref[...])
        o_ref[...] = (yg * jax.nn.sigmoid(gp_ref[...] + b_ref[...]) * _silu(z_ref[...].astype(F32))).astype(BF16)

    return _rowcall(body, name, y_ssm.shape[0], [(y_ssm, "row"), (gp, "row"), (b_glu, "full"), (z, "row")],
                    [(y_ssm.shape[1], BF16, "row")])[0]


def _s5_gate_bwd(dy3, y_ssm, gp, b_glu, z, name):
    def body(d_ref, y_ref, gp_ref, b_ref, z_ref, dz_ref, dgp_ref, dyg_ref, db_ref):
        yg = jax.nn.gelu(y_ref[...])
        _, vjp = jax.vjp(lambda yg, gp, z: yg * jax.nn.sigmoid(gp) * _silu(z), yg, gp_ref[...] + b_ref[...],
                         z_ref[...].astype(F32))
        dyg, dgp, dz = vjp(d_ref[...])
        dz_ref[...] = dz.astype(BF16)
        dgp_ref[...] = dgp.astype(BF16)
        dyg_ref[...] = dyg
        db_ref[...] += jnp.sum(dgp, axis=0, keepdims=True)

    w = y_ssm.shape[1]
    return _rowcall(body, name, y_ssm.shape[0],
                    [(dy3, "row"), (y_ssm, "row"), (gp, "row"), (b_glu, "full"), (z, "row")],
                    [(w, BF16, "right"), (w, BF16, "row"), (w, F32, "row"), (w, F32, "acc")])


def _cast_bf16(x, name):
    r, c = x.shape
    by_cols = r % (2 * SUBLANES) != 0
    tr, tc = (r, _tile(c, 256)) if by_cols else (_tile(r, 512, 2 * SUBLANES), c)
    pos = (lambda i: (0, i)) if by_cols else (lambda i: (i, 0))

    def body(x_ref, o_ref):
        o_ref[...] = x_ref[...].astype(BF16)

    return pl.pallas_call(
        body, name=name, grid=(c // tc if by_cols else r // tr,),
        in_specs=[pl.BlockSpec((tr, tc), pos)], out_specs=pl.BlockSpec((tr, tc), pos),
        out_shape=jax.ShapeDtypeStruct((r, c), BF16), compiler_params=_cparams("parallel"),
    )(x)


def _concat_cast(a, b, name):
    def body(a_ref, b_ref, o_ref):
        w = a_ref.shape[1]
        o_ref[:, :w] = a_ref[...].astype(BF16)
        o_ref[:, w:] = b_ref[...].astype(BF16)

    return _rowcall(body, name, a.shape[0], [(a, "row"), (b, "row")], [(a.shape[1] + b.shape[1], BF16, "row")])[0]


def _disc(ar, ai, ldt):
    dt = jnp.exp(ldt)
    mag = jnp.exp(ar * dt)
    abr = mag * jnp.cos(ai * dt)
    abi = mag * jnp.sin(ai * dt)
    den = ar * ar + ai * ai
    nr = abr - 1.0
    return abr, abi, (nr * ar + abi * ai) / den, (abi * ar - nr * ai) / den


def _s5_disc_fwd(a_re, a_im, ldt):
    def body(ar, ai, ld, o1, o2, o3, o4):
        o1[...], o2[...], o3[...], o4[...] = _disc(ar[...], ai[...], ld[...])

    sh = jax.ShapeDtypeStruct(a_re.shape, F32)
    return pl.pallas_call(body, name="s5_disc_fwd", out_shape=(sh, sh, sh, sh))(a_re, a_im, ldt)


def _s5_disc_bwd(a_re, a_im, ldt, d_abr, d_abi, d_cr, d_ci):
    def body(ar, ai, ld, g1, g2, g3, g4, o1, o2, o3):
        _, vjp = jax.vjp(_disc, ar[...], ai[...], ld[...])
        o1[...], o2[...], o3[...] = vjp((g1[...], g2[...], g3[...], g4[...]))

    sh = jax.ShapeDtypeStruct(a_re.shape, F32)
    return pl.pallas_call(body, name="s5_disc_bwd", out_shape=(sh, sh, jax.ShapeDtypeStruct(ldt.shape, F32)))(
        a_re, a_im, ldt, d_abr, d_abi, d_cr, d_ci)


def _bbar(cr, ci, br, bi):
    return cr * br - ci * bi, cr * bi + ci * br


def _s5_bbar_fwd(cr_col, ci_col, b_re, b_im):
    def body(cr, ci, br, bi, o1, o2):
        o1[...], o2[...] = _bbar(cr[...], ci[...], br[...], bi[...])

    w = b_re.shape[1]
    return _rowcall(body, "s5_bbar_fwd", b_re.shape[0], [(cr_col, "row"), (ci_col, "row"), (b_re, "row"), (b_im, "row")],
                    [(w, F32, "row"), (w, F32, "row")], tile_rows=1024)


def _s5_bbar_bwd(cr_col, ci_col, b_re, b_im, d_re, d_im):
    def body(cr, ci, br, bi, g1, g2, o1, o2, o3, o4):
        _, vjp = jax.vjp(_bbar, cr[...], ci[...], br[...], bi[...])
        o1[...], o2[...], o3[...], o4[...] = vjp((g1[...], g2[...]))

    w = b_re.shape[1]
    return _rowcall(body, "s5_bbar_bwd", b_re.shape[0],
                    [(cr_col, "row"), (ci_col, "row"), (b_re, "row"), (b_im, "row"), (d_re, "row"), (d_im, "row")],
                    [(1, F32, "row"), (1, F32, "row"), (w, F32, "row"), (w, F32, "row")], tile_rows=1024)


def _block_diag_in(t):
    g, p, c = t.shape
    nb = g // GROUPS_PER_BLOCK
    t4 = t.reshape(nb, GROUPS_PER_BLOCK, p, c).transpose(0, 1, 3, 2)
    eye = jnp.eye(GROUPS_PER_BLOCK, dtype=t.dtype)
    return (t4[:, :, :, None, :] * eye[None, :, None, :, None]).reshape(nb, GROUPS_PER_BLOCK * c, GROUPS_PER_BLOCK * p)


def _block_diag_in_extract(d, p, c):
    nb = d.shape[0]
    d5 = d.reshape(nb, GROUPS_PER_BLOCK, c, GROUPS_PER_BLOCK, p)
    diag = jnp.stack([d5[:, g, :, g, :] for g in range(GROUPS_PER_BLOCK)], axis=1)
    return diag.transpose(0, 1, 3, 2).reshape(nb * GROUPS_PER_BLOCK, p, c)


def _block_diag_out(t):
    g, c, p = t.shape
    nb = g // GROUPS_PER_BLOCK
    t4 = t.reshape(nb, GROUPS_PER_BLOCK, c, p).transpose(0, 1, 3, 2)
    eye = jnp.eye(GROUPS_PER_BLOCK, dtype=t.dtype)
    return (t4[:, :, :, None, :] * eye[None, :, None, :, None]).reshape(nb, GROUPS_PER_BLOCK * p, GROUPS_PER_BLOCK * c)


def _block_diag_out_extract(d, c, p):
    nb = d.shape[0]
    d5 = d.reshape(nb, GROUPS_PER_BLOCK, p, GROUPS_PER_BLOCK, c)
    diag = jnp.stack([d5[:, g, :, g, :] for g in range(GROUPS_PER_BLOCK)], axis=1)
    return diag.transpose(0, 1, 3, 2).reshape(nb * GROUPS_PER_BLOCK, c, p)


def _scan_step(ar, ai, hr, hi, xr, xi):
    return ar * hr - ai * hi + xr, ar * hi + ai * hr + xi


def _s5_scan_fwd(u, bd_re, bd_im, cd_re, cd_im, ab_re, ab_im, init_re, init_im, d_row, full, name):
    s, w = u.shape
    nb = w // LANES
    rows = _tile(s, 512, SUBLANES)
    nc = s // rows
    steps = rows // N_SEG
    ns = nb * BLOCK_STATE

    def body(u_ref, bdr, bdi, cdr, cdi, ar_ref, ai_ref, ir_ref, ii_ref, d_ref, *outs):
        if full:
            y_ref, yg_ref, hr_ref, hi_ref, er_ref, ei_ref, cr, ci = outs
        else:
            er_ref, ei_ref, hr_ref, hi_ref, cr, ci = outs
        c = pl.program_id(1)

        @pl.when(c == 0)
        def _():
            cr[...] = ir_ref[...]
            ci[...] = ii_ref[...]

        ub = u_ref[...].astype(BF16)
        hr_ref[...] = jnp.dot(ub, bdr[...], preferred_element_type=F32)
        hi_ref[...] = jnp.dot(ub, bdi[...], preferred_element_type=F32)
        ar, ai = ar_ref[...], ai_ref[...]

        hr, hi = cr[...], ci[...]
        for j in range(steps):
            rows_j = pl.ds(j * N_SEG, N_SEG)
            hr, hi = _scan_step(ar, ai, hr, hi, hr_ref[rows_j, :], hi_ref[rows_j, :])
            hr_ref[rows_j, :] = hr
            hi_ref[rows_j, :] = hi
        cr[...] = hr
        ci[...] = hi
        if full:
            y = (jnp.dot(hr_ref[...].astype(BF16), cdr[...], preferred_element_type=F32)
                 + jnp.dot(hi_ref[...].astype(BF16), cdi[...], preferred_element_type=F32)
                 + d_ref[...] * u_ref[...])
            y_ref[...] = y
            yg_ref[...] = jax.nn.gelu(y).astype(BF16)

        @pl.when(c == nc - 1)
        def _():
            er_ref[...] = hr
            ei_ref[...] = hi

    blk3 = lambda a: pl.BlockSpec((None,) + a.shape[1:], lambda k, c: (k, 0, 0))
    seg = pl.BlockSpec((N_SEG, BLOCK_STATE), lambda k, c: (0, k))
    st = pl.BlockSpec((rows, BLOCK_STATE), lambda k, c: (c, k))
    in_specs = [pl.BlockSpec((rows, LANES), lambda k, c: (c, k)), blk3(bd_re), blk3(bd_im), blk3(cd_re), blk3(cd_im),
                seg, seg, seg, seg, pl.BlockSpec((1, LANES), lambda k, c: (0, k))]
    seg_shape = jax.ShapeDtypeStruct((N_SEG, ns), F32)
    st_shape = jax.ShapeDtypeStruct((s, ns), F32)
    carry = [pltpu.VMEM((N_SEG, BLOCK_STATE), F32)] * 2
    if full:
        ych = pl.BlockSpec((rows, LANES), lambda k, c: (c, k))
        out_specs = [ych, ych, st, st, seg, seg]
        out_shape = [jax.ShapeDtypeStruct((s, w), F32), jax.ShapeDtypeStruct((s, w), BF16), st_shape, st_shape, seg_shape, seg_shape]
        scratch = carry
    else:
        out_specs = [seg, seg]
        out_shape = [seg_shape, seg_shape]
        scratch = [pltpu.VMEM((rows, BLOCK_STATE), F32)] * 2 + carry
    return pl.pallas_call(
        body, name=name, grid=(nb, nc), in_specs=in_specs, out_specs=out_specs, out_shape=out_shape,
        scratch_shapes=scratch, compiler_params=_cparams("parallel", "arbitrary"),
    )(u, bd_re, bd_im, cd_re, cd_im, ab_re, ab_im, init_re, init_im, d_row)


def _s5_seg_fix(e_re, e_im, ab_re, ab_im, seg_len, reverse, name):
    assert seg_len & (seg_len - 1) == 0

    def body(er, ei, ar, ai, o_re, o_im):
        pr, pi = ar[0:1, :], ai[0:1, :]
        for _ in range(int(math.log2(seg_len))):
            pr, pi = pr * pr - pi * pi, 2.0 * pr * pi
        tr = jnp.zeros_like(pr)
        ti = jnp.zeros_like(pr)
        order = list(range(N_SEG - 1, -1, -1)) if reverse else list(range(N_SEG))
        for n, sgm in enumerate(order):
            o_re[sgm:sgm + 1, :] = tr
            o_im[sgm:sgm + 1, :] = ti
            if n < N_SEG - 1:
                tr, ti = _scan_step(pr, pi, tr, ti, er[sgm:sgm + 1, :], ei[sgm:sgm + 1, :])

    sh = jax.ShapeDtypeStruct(e_re.shape, F32)
    return pl.pallas_call(body, name=name, out_shape=(sh, sh))(e_re, e_im, ab_re, ab_im)


def _s5_scan_bwd(dy, u, h_re, h_im, bd_re, bd_im, cd_re, cd_im, ab_re, ab_imn, gin_re, gin_im, d_row, full, name, duz=None):
    s, w = u.shape
    nb = w // LANES
    rows = _tile(s, 512, SUBLANES)
    nc = s // rows
    steps = rows // N_SEG
    ns = nb * BLOCK_STATE

    def body(dy_ref, u_ref, hr_ref, hi_ref, bdr, bdi, cdr, cdi, ar_ref, ai_ref, ir_ref, ii_ref, d_ref, *outs):
        if full:
            _, du_ref, dbr_ref, dbi_ref, dcr_ref, dci_ref, dar_ref, dai_ref, dd_ref, gr, gi, accr, acci = outs
        else:
            er_ref, ei_ref, gr, gi = outs
        c = pl.program_id(1)

        @pl.when(c == 0)
        def _():
            gr[pl.ds(rows, N_SEG), :] = ir_ref[...]
            gi[pl.ds(rows, N_SEG), :] = ii_ref[...]
            if full:
                for r in (dbr_ref, dbi_ref, dcr_ref, dci_ref, dd_ref, accr, acci):
                    r[...] = jnp.zeros_like(r)

        dyb = dy_ref[...].astype(BF16)
        nt = (_DOT_DIMS["nt"], ((), ()))
        tn = (_DOT_DIMS["tn"], ((), ()))
        gr[pl.ds(0, rows), :] = lax.dot_general(dyb, cdr[...], nt, preferred_element_type=F32)
        gi[pl.ds(0, rows), :] = lax.dot_general(dyb, cdi[...], nt, preferred_element_type=F32)
        ar, ai = ar_ref[...], ai_ref[...]

        g0r, g0i = gr[pl.ds(rows, N_SEG), :], gi[pl.ds(rows, N_SEG), :]
        for j in range(steps - 1, -1, -1):
            rows_j = pl.ds(j * N_SEG, N_SEG)
            g0r, g0i = _scan_step(ar, ai, g0r, g0i, gr[rows_j, :], gi[rows_j, :])
            gr[rows_j, :] = g0r
            gi[rows_j, :] = g0i
        if full:
            hr, hi = hr_ref[...], hi_ref[...]
            gnr, gni = gr[pl.ds(N_SEG, rows), :], gi[pl.ds(N_SEG, rows), :]
            accr[...] += jnp.sum((gnr * hr + gni * hi).reshape(steps, N_SEG, BLOCK_STATE), axis=0)
            acci[...] += jnp.sum((gni * hr - gnr * hi).reshape(steps, N_SEG, BLOCK_STATE), axis=0)
        gr[pl.ds(rows, N_SEG), :] = g0r
        gi[pl.ds(rows, N_SEG), :] = g0i
        if full:
            ub = u_ref[...].astype(BF16)
            gbr, gbi = gr[pl.ds(0, rows), :].astype(BF16), gi[pl.ds(0, rows), :].astype(BF16)
            dcr_ref[...] += lax.dot_general(hr.astype(BF16), dyb, tn, preferred_element_type=F32)
            dci_ref[...] += lax.dot_general(hi.astype(BF16), dyb, tn, preferred_element_type=F32)
            dbr_ref[...] += lax.dot_general(ub, gbr, tn, preferred_element_type=F32)
            dbi_ref[...] += lax.dot_general(ub, gbi, tn, preferred_element_type=F32)
            du_ref[...] = (lax.dot_general(gbr, bdr[...], nt, preferred_element_type=F32)
                           + lax.dot_general(gbi, bdi[...], nt, preferred_element_type=F32)
                           + d_ref[...] * dy_ref[...]).astype(BF16)
            dd_ref[...] += jnp.sum(dy_ref[...] * u_ref[...], axis=0, keepdims=True)

        @pl.when(c == nc - 1)
        def _():
            if full:
                dar_ref[...] = jnp.sum(accr[...], axis=0, keepdims=True)
                dai_ref[...] = jnp.sum(acci[...], axis=0, keepdims=True)
            else:
                er_ref[...] = g0r
                ei_ref[...] = g0i

    rev = lambda k, c: (nc - 1 - c, k)
    blk3 = lambda a: pl.BlockSpec((None,) + a.shape[1:], lambda k, c: (k, 0, 0))
    seg = pl.BlockSpec((N_SEG, BLOCK_STATE), lambda k, c: (0, k))
    st = pl.BlockSpec((rows, BLOCK_STATE), rev)
    ch = pl.BlockSpec((rows, LANES), rev)
    vec = pl.BlockSpec((1, LANES), lambda k, c: (0, k))
    if not full:
        st = pl.BlockSpec((rows, BLOCK_STATE), lambda k, c: (0, k))
    in_specs = [ch, ch if full else pl.BlockSpec((rows, LANES), lambda k, c: (0, k)), st, st,
                blk3(bd_re), blk3(bd_im), blk3(cd_re), blk3(cd_im), seg, seg, seg, seg, vec]
    args = [dy, u, h_re, h_im, bd_re, bd_im, cd_re, cd_im, ab_re, ab_imn, gin_re, gin_im, d_row]
    gbuf = [pltpu.VMEM((rows + N_SEG, BLOCK_STATE), F32)] * 2
    if full:
        row1 = pl.BlockSpec((1, BLOCK_STATE), lambda k, c: (0, k))
        out_specs = [ch, blk3(bd_re), blk3(bd_im), blk3(cd_re), blk3(cd_im), row1, row1, vec]
        out_shape = [jax.ShapeDtypeStruct(duz.shape, BF16),
                     jax.ShapeDtypeStruct(bd_re.shape, F32), jax.ShapeDtypeStruct(bd_im.shape, F32),
                     jax.ShapeDtypeStruct(cd_re.shape, F32), jax.ShapeDtypeStruct(cd_im.shape, F32),
                     jax.ShapeDtypeStruct((1, ns), F32), jax.ShapeDtypeStruct((1, ns), F32),
                     jax.ShapeDtypeStruct((1, w), F32)]
        scratch = gbuf + [pltpu.VMEM((N_SEG, BLOCK_STATE), F32)] * 2
        in_specs.append(pl.BlockSpec(memory_space=pl.ANY))
        args.append(duz)
        aliases = {len(args) - 1: 0}
    else:
        out_specs = [seg, seg]
        out_shape = [jax.ShapeDtypeStruct((N_SEG, ns), F32)] * 2
        scratch = gbuf
        aliases = {}
    return pl.pallas_call(
        body, name=name, grid=(nb, nc), in_specs=in_specs, out_specs=out_specs, out_shape=out_shape,
        input_output_aliases=aliases, scratch_shapes=scratch, compiler_params=_cparams("parallel", "arbitrary"),
    )(*args)


def _log_sigmoid(x):
    return jnp.minimum(x, 0.0) - jnp.log(1.0 + jnp.exp(-jnp.abs(x)))


def _tri(n, upper):
    r = lax.broadcasted_iota(jnp.int32, (n, n), 0)
    c = lax.broadcasted_iota(jnp.int32, (n, n), 1)
    return jnp.where((c >= r) if upper else (r >= c), 1.0, 0.0).astype(F32)


def _cum_fwd(f_logit, b_row, name):
    s, w = f_logit.shape
    t = _tile(s, 256, SUBLANES)

    def body(f_ref, b_ref, o_ref, carry):
        @pl.when(pl.program_id(0) == 0)
        def _():
            carry[...] = jnp.zeros_like(carry)

        lf = _log_sigmoid(f_ref[...] + b_ref[...])
        cum = jnp.dot(_tri(t, False), lf, precision=lax.Precision.HIGHEST, preferred_element_type=F32) + carry[...]
        o_ref[...] = cum * LOG2E
        carry[...] = cum[t - 1:t, :]

    return pl.pallas_call(
        body, name=name, grid=(s // t,),
        in_specs=[pl.BlockSpec((t, w), lambda i: (i, 0)), pl.BlockSpec((1, w), lambda i: (0, 0))],
        out_specs=pl.BlockSpec((t, w), lambda i: (i, 0)), out_shape=jax.ShapeDtypeStruct((s, w), F32),
        scratch_shapes=[pltpu.VMEM((1, w), F32)], compiler_params=_cparams("arbitrary"),
    )(f_logit, b_row)


def _cum_bwd(dcq, dck, f_logit, b_row, name):
    s, w = f_logit.shape
    t = _tile(s, 256, SUBLANES)
    nt = s // t

    def body(q_ref, k_ref, f_ref, b_ref, df_ref, db_ref, carry):
        @pl.when(pl.program_id(0) == 0)
        def _():
            carry[...] = jnp.zeros_like(carry)
            db_ref[...] = jnp.zeros_like(db_ref)

        dc = q_ref[...] - k_ref[...]
        rc = jnp.dot(_tri(t, True), dc, precision=lax.Precision.HIGHEST, preferred_element_type=F32) + carry[...]
        carry[...] = rc[0:1, :]
        df = rc * (1.0 - jax.nn.sigmoid(f_ref[...] + b_ref[...]))
        df_ref[...] = df.astype(BF16)
        db_ref[...] += jnp.sum(df, axis=0, keepdims=True)

    rev = pl.BlockSpec((t, w), lambda i: (nt - 1 - i, 0))
    one = pl.BlockSpec((1, w), lambda i: (0, 0))
    return pl.pallas_call(
        body, name=name, grid=(nt,), in_specs=[rev, rev, rev, one], out_specs=[rev, one],
        out_shape=[jax.ShapeDtypeStruct((s, w), BF16), jax.ShapeDtypeStruct((1, w), F32)],
        scratch_shapes=[pltpu.VMEM((1, w), F32)], compiler_params=_cparams("arbitrary"),
    )(dcq, dck, f_logit, b_row)


def _head_col(cum_tile, h):
    lane = lax.broadcasted_iota(jnp.int32, cum_tile.shape, 1)
    return jnp.sum(jnp.where(lane == h, cum_tile, 0.0), axis=1, keepdims=True)


def _attn_tiles(s):
    return _tile(s, 512, LANES)


def _exp2_rows(sc, sub):
    return jnp.concatenate([jnp.exp2(sc[:, b * LANES:(b + 1) * LANES] - sub) for b in range(sc.shape[1] // LANES)], axis=1)


def _row_of(rep):
    return jnp.transpose(rep)[0:1, :]


def _causal(sc, keys_on_rows):
    r = lax.broadcasted_iota(jnp.int32, sc.shape, 0)
    c = lax.broadcasted_iota(jnp.int32, sc.shape, 1)
    return jnp.where((r <= c) if keys_on_rows else (c <= r), sc, NEG_INF)


def _fox_fwd(q2, kv, cum2_t, z, name):
    s, w = q2.shape
    nh = w // HEAD_DIM
    tq = _attn_tiles(s)
    nq = s // tq
    nt = (_DOT_DIMS["nt"], ((), ()))

    def body(q_ref, k_ref, v_ref, ct_ref, z_ref, o_ref, oz_ref, lse_row_ref, m_s, acc_s, vaug, s_buf):
        i = pl.program_id(1)

        @pl.when(i == 0)
        def _():
            vaug[:, :HEAD_DIM] = v_ref[...]
            vaug[:, HEAD_DIM:] = jnp.ones((s, LANES), BF16)

        qb = q_ref[...]
        m_s[...] = jnp.full_like(m_s, NEG_INF)
        acc_s[...] = jnp.zeros_like(acc_s)

        def scores(j):
            off = pl.multiple_of(j * tq, tq)
            return lax.dot_general(qb, k_ref[pl.ds(off, tq), :], nt, preferred_element_type=F32) - ct_ref[:, pl.ds(off, tq)]

        def softmax_pv(j, sc):
            m_old = m_s[...]
            m_new = jnp.maximum(m_old, jnp.max(sc, axis=1, keepdims=True))
            p = _exp2_rows(sc, m_new)
            alpha = jnp.exp2(m_old - m_new)
            pv = jnp.dot(p.astype(BF16), vaug[pl.ds(pl.multiple_of(j * tq, tq), tq), :], preferred_element_type=F32)
            acc_s[...] = jnp.concatenate([alpha, alpha], axis=1) * acc_s[...] + pv
            m_s[...] = m_new

        s_buf[...] = scores(0)

        def loop(j, carry):
            nxt = scores(j + 1)
            softmax_pv(j, s_buf[...])
            s_buf[...] = nxt
            return carry

        lax.fori_loop(0, i, loop, 0)
        softmax_pv(i, _causal(s_buf[...], False))
        l = acc_s[:, HEAD_DIM:]
        o = acc_s[:, :HEAD_DIM] / l
        o_ref[...] = o
        oz_ref[...] = (o * _silu(z_ref[...].astype(F32))).astype(BF16)
        lse_row_ref[...] = _row_of(m_s[...] + jnp.log(l) * LOG2E)

    return pl.pallas_call(
        body, name=name, grid=(nh, nq),
        in_specs=[pl.BlockSpec((tq, HEAD_DIM), lambda h, i: (i, h)),
                  pl.BlockSpec((s, HEAD_DIM), lambda h, i: (0, h)),
                  pl.BlockSpec((s, HEAD_DIM), lambda h, i: (0, nh + h)),
                  pl.BlockSpec((None, 1, s), lambda h, i: (h, 0, 0)),
                  pl.BlockSpec((tq, HEAD_DIM), lambda h, i: (i, h))],
        out_specs=[pl.BlockSpec((tq, HEAD_DIM), lambda h, i: (i, h)),
                   pl.BlockSpec((tq, HEAD_DIM), lambda h, i: (i, h)),
                   pl.BlockSpec((None, 1, tq), lambda h, i: (h, 0, i))],
        out_shape=[jax.ShapeDtypeStruct((s, w), F32), jax.ShapeDtypeStruct((s, w), BF16),
                   jax.ShapeDtypeStruct((nh, 1, s), F32)],
        scratch_shapes=[pltpu.VMEM((tq, LANES), F32), pltpu.VMEM((tq, HEAD_DIM + LANES), F32),
                        pltpu.VMEM((s, HEAD_DIM + LANES), BF16), pltpu.VMEM((tq, tq), F32)],
        compiler_params=_cparams("arbitrary", "arbitrary"),
    )(q2, kv, kv, cum2_t, z)


def _fox_bwd(q2, kv, do, o, lse2_t, cum2, dqz, name):
    s, w = q2.shape
    nh = w // HEAD_DIM
    tk = _attn_tiles(s)
    nk = s // tk
    scale = HEAD_DIM ** -0.5
    nt = (_DOT_DIMS["nt"], ((), ()))
    tn = (_DOT_DIMS["tn"], ((), ()))

    def body(q_ref, k_ref, v_ref, do_ref, o_ref, lse_ref, c_ref, _, dk_ref, dv_ref, dq_ref, dcq_ref, dck_ref,
             dk_s, dv_s, dc_s, dq_s, dcq_s, dl_s, s_buf, dp_buf):
        h, j = pl.program_id(0), pl.program_id(1)

        @pl.when(j == 0)
        def _():
            dq_s[...] = jnp.zeros_like(dq_s)
            dcq_s[...] = jnp.zeros_like(dcq_s)
            for i in range(nk):
                rows = pl.ds(i * tk, tk)
                d = jnp.sum(do_ref[rows, :].astype(F32) * o_ref[rows, :], axis=1, keepdims=True)
                dl_s[:, i * tk:(i + 1) * tk] = _row_of(jnp.broadcast_to(d, (tk, LANES)))

        kb = k_ref[...]
        vb = v_ref[...]
        ck = jnp.broadcast_to(_head_col(c_ref[...], h), (tk, LANES))
        dk_s[...] = jnp.zeros_like(dk_s)
        dv_s[...] = jnp.zeros_like(dv_s)
        dc_s[...] = jnp.zeros_like(dc_s)

        def scores(i):
            off = pl.multiple_of(i * tk, tk)
            sc = lax.dot_general(kb, q_ref[pl.ds(off, tk), :], nt, preferred_element_type=F32) - lse_ref[:, pl.ds(off, tk)]
            dp = lax.dot_general(vb, do_ref[pl.ds(off, tk), :], nt, preferred_element_type=F32) - dl_s[:, pl.ds(off, tk)]
            return sc, dp

        def accumulate(i, sc, dp):
            off = pl.multiple_of(i * tk, tk)
            p = _exp2_rows(sc, ck)
            dv_s[...] += jnp.dot(p.astype(BF16), do_ref[pl.ds(off, tk), :], preferred_element_type=F32)
            ds = p * dp
            dsb = ds.astype(BF16)
            dk_s[...] += jnp.dot(dsb, q_ref[pl.ds(off, tk), :], preferred_element_type=F32)
            dq_s[pl.ds(off, tk), :] += lax.dot_general(dsb, kb, tn, preferred_element_type=F32)
            dcq_s[:, pl.ds(off, tk)] += jnp.sum(ds, axis=0, keepdims=True)
            part = ds[:, :LANES]
            for b in range(1, tk // LANES):
                part = part + ds[:, b * LANES:(b + 1) * LANES]
            dc_s[...] += part

        sc0, dp0 = scores(j)
        s_buf[...] = _causal(sc0, True)
        dp_buf[...] = dp0

        def loop(i, carry):
            nxt = scores(i + 1)
            accumulate(i, s_buf[...], dp_buf[...])
            s_buf[...], dp_buf[...] = nxt
            return carry

        lax.fori_loop(j, nk - 1, loop, 0)
        accumulate(nk - 1, s_buf[...], dp_buf[...])
        dk_ref[...] = (dk_s[...] * (1.0 / LOG2E)).astype(BF16)
        dv_ref[...] = dv_s[...].astype(BF16)
        dck_ref[...] = jnp.sum(jnp.transpose(dc_s[...]), axis=0, keepdims=True)

        @pl.when(j == nk - 1)
        def _():
            dq_ref[...] = (dq_s[...] * scale).astype(BF16)
            dcq_ref[...] = dcq_s[...]

    col = pl.BlockSpec((s, HEAD_DIM), lambda h, j: (0, h))
    row = pl.BlockSpec((None, 1, s), lambda h, j: (h, 0, 0))
    kspec = pl.BlockSpec((tk, HEAD_DIM), lambda h, j: (j, h))
    return pl.pallas_call(
        body, name=name, grid=(nh, nk),
        in_specs=[col, kspec, pl.BlockSpec((tk, HEAD_DIM), lambda h, j: (j, nh + h)), col, col, row,
                  pl.BlockSpec((tk, LANES), lambda h, j: (j, 0)), pl.BlockSpec(memory_space=pl.ANY)],
        out_specs=[kspec, kspec, col, row, pl.BlockSpec((None, 1, tk), lambda h, j: (h, 0, j))],
        out_shape=[jax.ShapeDtypeStruct((s, w), BF16), jax.ShapeDtypeStruct((s, w), BF16),
                   jax.ShapeDtypeStruct(dqz.shape, BF16), jax.ShapeDtypeStruct((nh, 1, s), F32),
                   jax.ShapeDtypeStruct((nh, 1, s), F32)],
        input_output_aliases={7: 2},
        scratch_shapes=[pltpu.VMEM((tk, HEAD_DIM), F32), pltpu.VMEM((tk, HEAD_DIM), F32), pltpu.VMEM((tk, LANES), F32),
                        pltpu.VMEM((s, HEAD_DIM), F32), pltpu.VMEM((1, s), F32), pltpu.VMEM((1, s), F32),
                        pltpu.VMEM((tk, tk), F32), pltpu.VMEM((tk, tk), F32)],
        compiler_params=_cparams("arbitrary", "arbitrary"),
    )(q2, kv, kv, do, o, lse2_t, cum2, dqz)


_ALL_PEERS = tuple(range(1, N_DEV))
_CHIP_PEERS = (1, 2, 4, 6)


def _exchange_copies(ins, outs, send_sems, recv_sems, local_sems, scatter, peers=_ALL_PEERS):
    x, y, c = (lax.axis_index(a) for a in MESH_AXES)
    me = 4 * x + 2 * y + c
    local, remote = [], []
    for a in range(len(ins)):
        local.append(pltpu.make_async_copy(ins[a].at[me] if scatter else ins[a], outs[a].at[me], local_sems.at[a]))
        for k in peers:
            px, py, pc = (1 - x if k & 4 else x), (1 - y if k & 2 else y), (1 - c if k & 1 else c)
            remote.append(pltpu.make_async_remote_copy(
                src_ref=ins[a].at[4 * px + 2 * py + pc] if scatter else ins[a], dst_ref=outs[a].at[me],
                send_sem=send_sems.at[a * (N_DEV - 1) + k - 1], recv_sem=recv_sems.at[a * (N_DEV - 1) + k - 1],
                device_id=(px, py, pc), device_id_type=pl.DeviceIdType.MESH))
    return local, remote


def _exchange_out_shapes(arrs, scatter):
    return [((N_DEV,) + a.shape[1:]) if scatter else ((N_DEV,) + a.shape) for a in arrs]


def _exchange(arrs, scatter, name, peers=_ALL_PEERS):
    n = len(arrs)

    def body(*refs):
        local, remote = _exchange_copies(refs[:n], refs[n:2 * n], *refs[2 * n:], scatter, peers)
        for cp in local + remote:
            cp.start()
        for cp in remote:
            cp.wait_send()
            cp.wait_recv()
        for cp in local:
            cp.wait()

    out_shape = [jax.ShapeDtypeStruct(s, a.dtype) for s, a in zip(_exchange_out_shapes(arrs, scatter), arrs)]
    return pl.pallas_call(
        body, name=name, out_shape=out_shape,
        in_specs=[pl.BlockSpec(memory_space=pl.ANY)] * n, out_specs=[pl.BlockSpec(memory_space=pl.ANY)] * n,
        scratch_shapes=[pltpu.SemaphoreType.DMA((n * (N_DEV - 1),)), pltpu.SemaphoreType.DMA((n * (N_DEV - 1),)),
                        pltpu.SemaphoreType.DMA((n,))],
    )(*arrs)


_HBM = pl.BlockSpec(memory_space=pltpu.HBM)
_SEM = pl.BlockSpec(memory_space=pltpu.SEMAPHORE)


def _exchange_start(arrs, scatter, name, after=(), peers=_ALL_PEERS):
    n = len(arrs)
    after = list(after)
    lands = [lax.empty(s, a.dtype) for s, a in zip(_exchange_out_shapes(arrs, scatter), arrs)]

    def body(*refs):
        ins, outs = refs[:n], refs[n:2 * n]
        send_sems, recv_sems, local_sems = refs[2 * n + len(after):2 * n + len(after) + 3]
        token = refs[-1]
        local, remote = _exchange_copies(ins, outs, send_sems, recv_sems, local_sems, scatter, peers)
        for cp in local + remote:
            cp.start()
        token[...] = jnp.zeros_like(token)

    hbm = lambda a: pltpu.HBM(a.shape, a.dtype)
    res = pl.pallas_call(
        body, name=name,
        out_shape=(pltpu.SemaphoreType.DMA((n * (N_DEV - 1),)), pltpu.SemaphoreType.DMA((n * (N_DEV - 1),)),
                   pltpu.SemaphoreType.DMA((n,)), *[hbm(a) for a in arrs], *[hbm(a) for a in lands],
                   jax.ShapeDtypeStruct((SUBLANES, LANES), F32)),
        in_specs=[_HBM] * (2 * n) + [pl.BlockSpec(memory_space=pl.ANY)] * len(after),
        out_specs=(_SEM, _SEM, _SEM, *[_HBM] * (2 * n), pl.BlockSpec(memory_space=pltpu.VMEM)),
        input_output_aliases={i: 3 + i for i in range(2 * n)},
        compiler_params=pltpu.CompilerParams(has_side_effects=pltpu.SideEffectType.DATAFLOW_SIDE_EFFECTING),
    )(*[pltpu.with_memory_space_constraint(a, pltpu.HBM) for a in list(arrs) + lands], *after)
    return (n, scatter, res[:3], res[3:3 + n], res[3 + n:3 + 2 * n], peers), res[-1]


def _exchange_wait(state, after, name):
    n, scatter, sems, srcs, lands, peers = state
    after = list(after) if isinstance(after, (list, tuple)) else [after]

    def body(*refs):
        ins, outs = refs[:n], refs[n:2 * n]
        send_sems, recv_sems, local_sems = refs[2 * n:2 * n + 3]
        local, remote = _exchange_copies(ins, outs, send_sems, recv_sems, local_sems, scatter, peers)
        for cp in remote:
            cp.wait_send()
            cp.wait_recv()
        for cp in local:
            cp.wait()

    hbm = lambda a: pltpu.HBM(a.shape, a.dtype)
    res = pl.pallas_call(
        body, name=name,
        out_shape=(*[hbm(a) for a in srcs], *[hbm(a) for a in lands]),
        in_specs=[_HBM] * (2 * n) + [_SEM] * 3 + [pl.BlockSpec(memory_space=pl.ANY)] * len(after),
        out_specs=tuple([_HBM] * (2 * n)),
        input_output_aliases={i: i for i in range(2 * n)},
        compiler_params=pltpu.CompilerParams(has_side_effects=pltpu.SideEffectType.DATAFLOW_SIDE_EFFECTING),
    )(*srcs, *lands, *sems, *after)
    return list(res[n:])


def _forward_to_sibling(slots, name):
    n = len(slots)
    hops = (2, 4, 6)

    def body(*refs):
        ins, outs, (send_sems, recv_sems) = refs[:n], refs[n:2 * n], refs[2 * n:]
        x, y, c = (lax.axis_index(a) for a in MESH_AXES)
        copies = []
        for a in range(n):
            for i, k in enumerate(hops):
                slot = 4 * (1 - x if k & 4 else x) + 2 * (1 - y if k & 2 else y) + c
                copies.append(pltpu.make_async_remote_copy(
                    src_ref=ins[a].at[slot], dst_ref=outs[a].at[slot],
                    send_sem=send_sems.at[a * len(hops) + i], recv_sem=recv_sems.at[a * len(hops) + i],
                    device_id=(x, y, 1 - c), device_id_type=pl.DeviceIdType.MESH))
        for cp in copies:
            cp.start()
        for cp in copies:
            cp.wait_send()
            cp.wait_recv()

    return pl.pallas_call(
        body, name=name, out_shape=[jax.ShapeDtypeStruct(s.shape, s.dtype) for s in slots],
        in_specs=[pl.BlockSpec(memory_space=pl.ANY)] * n, out_specs=[pl.BlockSpec(memory_space=pl.ANY)] * n,
        input_output_aliases={i: i for i in range(n)},
        scratch_shapes=[pltpu.SemaphoreType.DMA((n * len(hops),)), pltpu.SemaphoreType.DMA((n * len(hops),))],
    )(*slots)


def _adamw_math(w, g, m, v):
    m = ADAM_B1 * m + (1.0 - ADAM_B1) * g
    v = ADAM_B2 * v + (1.0 - ADAM_B2) * (g * g)
    m_hat = m / (1.0 - ADAM_B1 ** ADAM_STEP)
    v_hat = v / (1.0 - ADAM_B2 ** ADAM_STEP)
    return -ADAM_LR * (m_hat / (jnp.sqrt(v_hat) + ADAM_EPS) + ADAM_WD * w), m, v


def _slot_sum(p_ref):
    g = p_ref[0].astype(F32)
    for d in range(1, p_ref.shape[0]):
        g = g + p_ref[d].astype(F32)
    return g


def _adamw_tile(r, c):
    return _tile(r, max(SUBLANES, (256 * 1024) // c // SUBLANES * SUBLANES), SUBLANES)


def _adamw(parts, w, m, v, name):
    r, c = w.shape[-2:]
    by_cols = r % SUBLANES != 0
    tr, tc = (r, _tile(c, 256)) if by_cols else (_adamw_tile(r, c), c)

    def body(p_ref, w_ref, m_ref, v_ref, g_ref, d_ref, nm_ref, nv_ref):
        g = _slot_sum(p_ref)
        g_ref[...] = g
        d_ref[...], nm_ref[...], nv_ref[...] = _adamw_math(w_ref[...], g, m_ref[...], v_ref[...])

    pos = (lambda i: (0, i)) if by_cols else (lambda i: (i, 0))
    if w.ndim == 3:
        blk = pl.BlockSpec((None, tr, tc), lambda i: (0,) + pos(i))
    else:
        blk = pl.BlockSpec((tr, tc), pos)
    sh = jax.ShapeDtypeStruct(w.shape, F32)
    return pl.pallas_call(
        body, name=name, grid=(c // tc if by_cols else r // tr,),
        in_specs=[pl.BlockSpec((parts.shape[0], tr, tc), lambda i: (0,) + pos(i)), blk, blk, blk],
        out_specs=[blk] * 4, out_shape=[sh] * 4, compiler_params=_cparams("parallel"),
    )(parts, w, m, v)


def _sum_parts(parts, name):
    _, r, c = parts.shape
    tr = _adamw_tile(r, c)

    def body(p_ref, o_ref):
        o_ref[...] = _slot_sum(p_ref)

    return pl.pallas_call(
        body, name=name, grid=(r // tr,),
        in_specs=[pl.BlockSpec((parts.shape[0], tr, c), lambda i: (0, i, 0))],
        out_specs=pl.BlockSpec((tr, c), lambda i: (i, 0)), out_shape=jax.ShapeDtypeStruct((r, c), F32),
        compiler_params=_cparams("parallel"),
    )(parts)


def _lane_pad(a, width=LANES):
    return jnp.pad(a, ((0, 0), (0, width - a.shape[1])))


def _local_step(x, target, norm_pre, norm_post, kv_norm, kv_b_f, a_re, a_im, log_dt, b_re, b_im, c_re, c_im, comm):
    s, d = x.shape
    g, p = a_re.shape
    w = g * S5_GROUP
    fw = d
    nh = fw // HEAD_DIM
    seg_len = s // N_SEG
    row = lambda v: v.reshape(1, -1)
    g_pre0, g_pre1, g_post0, g_post1, g_kv = row(norm_pre[0]), row(norm_pre[1]), row(norm_post[0]), row(norm_post[1]), row(kv_norm)

    ldt = log_dt.reshape(g, 1)
    abr, abi, cr, ci = _s5_disc_fwd(a_re, a_im, ldt)
    cr_col, ci_col = cr.reshape(g * p, 1), ci.reshape(g * p, 1)
    b_re2, b_im2 = b_re.reshape(g * p, S5_GROUP), b_im.reshape(g * p, S5_GROUP)
    bb_re, bb_im = _s5_bbar_fwd(cr_col, ci_col, b_re2, b_im2)
    bd_re = _block_diag_in(bb_re.reshape(g, p, S5_GROUP)).astype(BF16)
    bd_im = _block_diag_in(bb_im.reshape(g, p, S5_GROUP)).astype(BF16)
    cd_re = _block_diag_out(c_re).astype(BF16)
    cd_im = _block_diag_out(-c_im).astype(BF16)
    ab_re = jnp.broadcast_to(abr.reshape(1, g * p), (N_SEG, g * p))
    ab_im = jnp.broadcast_to(abi.reshape(1, g * p), (N_SEG, g * p))
    zero_seg = jnp.zeros((N_SEG, g * p), F32)

    xn0 = _norm_cast(x, g_pre0 + comm.token, "norm_pre0", x_kind="nat")
    w_in = comm.weight("s5_w_in", [xn0, bd_re, bd_im, cd_re, cd_im, ab_re, ab_im])
    d_row, bglu_row = row(comm.vector("s5_d")), row(comm.vector("s5_b_glu"))
    u = _mm(xn0, w_in, "nn", F32, "s5_in_u", b_cols=(0, w), b_slots=True)
    z0 = _mm(xn0, w_in, "nn", BF16, "s5_in_z", b_cols=(w, w), b_slots=True)
    e_re, e_im = _s5_scan_fwd(u, bd_re, bd_im, cd_re, cd_im, ab_re, ab_im, zero_seg, zero_seg, d_row, False, "s5_scan_ends")
    i_re, i_im = _s5_seg_fix(e_re, e_im, ab_re, ab_im, seg_len, False, "s5_seg_fix")
    y_ssm, yg, h_re, h_im, _, _ = _s5_scan_fwd(u, bd_re, bd_im, cd_re, cd_im, ab_re, ab_im, i_re, i_im, d_row, True, "s5_scan")
    w_glu, w_out = comm.weight("s5_w_glu", yg), comm.weight("s5_w_out", yg)
    gp = _mm(yg, w_glu, "nn", BF16, "s5_glu")
    y3 = _s5_gate(y_ssm, gp, bglu_row, z0, "s5_gate")
    w_kvt, fw_in = comm.weight("kv_w", y3), comm.weight("fox_w_in", y3)
    w_ft = jnp.pad(w_kvt[2 * fw:], ((0, LANES - nh), (0, 0)))
    o0 = _mm(y3, w_out, "nn", F32, "s5_out")
    r0 = _post_norm(o0, g_post0 + comm.late_token, "norm_post0", out_kind="nat")

    h1, hn_kv, xn1 = _resid_norm2(x, r0, g_kv, g_pre1, "resid_norms")
    kv = _mm(hn_kv, w_kvt, "nt", BF16, "kv_proj", b_rows=2 * fw)
    f_logit = _mm(hn_kv, w_ft, "nt", F32, "f_proj")
    bf_row = _lane_pad(row(kv_b_f))
    cum2 = _cum_fwd(f_logit, bf_row, "cum_fwd")
    cum2_t = cum2[:, :nh].T.reshape(nh, 1, s)
    q2 = _mm(xn1, fw_in, "nn", BF16, "fox_q", scale=HEAD_DIM ** -0.5 * LOG2E, b_cols=(0, fw), b_slots=True)
    z1 = _mm(xn1, fw_in, "nn", BF16, "fox_z", b_cols=(fw, fw), b_slots=True)
    o, oz, lse2_t = _fox_fwd(q2, kv, cum2_t, z1, "fox_fwd")
    fw_out = comm.weight("fox_w_out", oz)
    o1 = _mm(oz, fw_out, "nn", F32, "fox_out")
    dh2, do1, sq, dg_post1 = _post_norm_loss(o1, g_post1, h1, target, "norm_post1_loss")
    loss = 0.5 * jnp.sum(sq) / d

    d_fw_out = _mm(oz, do1, "tn", BF16, "fox_out_dw")
    d_oz = _mm(do1, fw_out, "nt", F32, "fox_out_dx")
    do, dqz = _gate_bwd(d_oz, o, z1, "fox_gate_bwd")
    dk, dv, dqz, dcq, dck = _fox_bwd(q2, kv, do, o, lse2_t, cum2, dqz, "fox_bwd")
    d_fw_in = _mm(xn1, dqz, "tn", BF16, "fox_in_dw", col_slots=True)
    dxn1 = _mm(dqz, fw_in, "nt", F32, "fox_in_dx", b_slots=True)
    dcq_sl = _lane_pad(dcq.reshape(nh, s).T)
    dck_sl = _lane_pad(dck.reshape(nh, s).T)
    df, db_f = _cum_bwd(dcq_sl, dck_sl, f_logit, bf_row, "cum_bwd")
    dkv = _concat_cast(dk, dv, "fox_dkv")
    d_w_kvmt = _mm(dkv, hn_kv, "tn", BF16, "kv_dw")
    d_w_ft = _mm(df, hn_kv, "tn", BF16, "f_dw")
    dhn_f = _mm(df, w_ft, "nn", F32, "f_dx")
    dhn_kv = _mm(dkv, w_kvt, "nn", F32, "kv_dx", add=dhn_f, b_rows=2 * fw)
    d_w_kvt = jnp.concatenate([d_w_kvmt, d_w_ft[:nh]], axis=0)
    tok = comm.send_grads(dict(fox_w_out=d_fw_out, fox_w_in=d_fw_in, kv_w=d_w_kvt), "exchange_fox")
    dh1, dg_pre1, dg_kv = _norm_bwd2(dh2, h1, dxn1, dhn_kv, g_pre1, g_kv, "resid_norms_bwd")

    do0, dg_post0 = _post_norm_bwd(dh1, o0, g_post0 + tok[0, 0], "norm_post0_bwd", dy_kind="nat")
    d_w_out = _mm(y3, do0, "tn", BF16, "s5_out_dw")
    dy3 = _mm(do0, w_out, "nt", F32, "s5_out_dx")
    duz, dgp, dyg_direct, db_glu = _s5_gate_bwd(dy3, y_ssm, gp, bglu_row, z0, "s5_gate_bwd")
    d_w_glu = _mm(yg, dgp, "tn", BF16, "s5_glu_dw")
    gelu_bwd = lambda dyg, y: jax.vjp(jax.nn.gelu, y)[1](dyg)[0]
    dy_ssm = _mm(dgp, w_glu, "nt", F32, "s5_glu_dx", add=dyg_direct, epilogue=(gelu_bwd, y_ssm))
    d_row = d_row + comm.send_grads(dict(s5_w_out=d_w_out, s5_w_glu=d_w_glu), "exchange_s5")[0, 0]
    ab_imn = -ab_im
    ge_re, ge_im = _s5_scan_bwd(dy_ssm, u, h_re, h_im, bd_re, bd_im, cd_re, cd_im, ab_re, ab_imn, zero_seg, zero_seg,
                                d_row, False, "s5_adj_ends")
    gi_re, gi_im = _s5_seg_fix(ge_re, ge_im, ab_re, ab_imn, seg_len, True, "s5_adj_fix")
    duz, dbd_re, dbd_im, dcd_re, dcd_im, dab_re, dab_im, dd = _s5_scan_bwd(
        dy_ssm, u, h_re, h_im, bd_re, bd_im, cd_re, cd_im, ab_re, ab_imn, gi_re, gi_im, d_row, True, "s5_adj", duz=duz)
    d_w_in = _mm(xn0, duz, "tn", BF16, "s5_in_dw", col_slots=True)
    tok = comm.send_grads(dict(s5_w_in=d_w_in), "exchange_s5_in")
    dxn0 = _mm(duz, w_in, "nt", F32, "s5_in_dx", after=tok, b_slots=True)
    grad_x, dg_pre0 = _norm_bwd1(dh1, x, dxn0, g_pre0, "norm_pre0_bwd")

    dbb_re = _block_diag_in_extract(dbd_re, p, S5_GROUP).reshape(g * p, S5_GROUP)
    dbb_im = _block_diag_in_extract(dbd_im, p, S5_GROUP).reshape(g * p, S5_GROUP)
    dcr_col, dci_col, db_re, db_im = _s5_bbar_bwd(cr_col, ci_col, b_re2, b_im2, dbb_re, dbb_im)
    da_re, da_im, dldt = _s5_disc_bwd(a_re, a_im, ldt, dab_re.reshape(g, p), dab_im.reshape(g, p),
                                      dcr_col.reshape(g, p), dci_col.reshape(g, p))
    dc_re = _block_diag_out_extract(dcd_re, S5_GROUP, p)
    dc_im = -_block_diag_out_extract(dcd_im, S5_GROUP, p)

    small = dict(
        norm_pre=jnp.concatenate([dg_pre0, dg_pre1], axis=0), norm_post=jnp.concatenate([dg_post0, dg_post1], axis=0),
        s5_a_re=da_re, s5_a_im=da_im, s5_log_dt=dldt.reshape(g), s5_b_re=db_re.reshape(g, p, S5_GROUP),
        s5_b_im=db_im.reshape(g, p, S5_GROUP), s5_c_re=dc_re, s5_c_im=dc_im, s5_d=dd.reshape(-1),
        s5_b_glu=db_glu.reshape(-1), kv_norm=dg_kv.reshape(-1), kv_b_f=db_f[0, :nh])
    return loss, grad_x, small


_BIG = ("s5_w_in", "s5_w_glu", "s5_w_out", "kv_w", "fox_w_in", "fox_w_out")
_COL_SHARDED = ("s5_w_in", "fox_w_in")
_SMALL = ("norm_pre", "norm_post", "s5_a_re", "s5_a_im", "s5_log_dt", "s5_b_re", "s5_b_im", "s5_c_re", "s5_c_im",
          "s5_d", "s5_b_glu", "kv_norm", "kv_b_f")
_SMALL_SHARDED = ("s5_d", "s5_b_glu")
_PACK_QUANTUM = SUBLANES * LANES
_WEIGHTS = ('norm_pre', 'norm_post', 's5_w_in', 's5_a_re', 's5_a_im', 's5_log_dt', 's5_b_re', 's5_b_im', 's5_c_re', 's5_c_im',
            's5_d', 's5_w_glu', 's5_b_glu', 's5_w_out', 'kv_norm', 'kv_w', 'kv_b_f', 'fox_w_in', 'fox_w_out')


def _full_from_slots(name, slots):
    n, r, c = slots.shape
    if name in _COL_SHARDED:
        return slots.transpose(1, 0, 2).reshape(r, n * c)
    return slots.reshape(n * r, c)


def _slots_from_full(name, full):
    if name in _COL_SHARDED:
        r, nc = full.shape
        return full.reshape(r, N_DEV, nc // N_DEV).transpose(1, 0, 2)
    nr, c = full.shape
    return full.reshape(N_DEV, nr // N_DEV, c)


def _pack(vals):
    parts = []
    for v in vals:
        flat = v.reshape(-1)
        parts.append(jnp.pad(flat, (0, (-flat.shape[0]) % _PACK_QUANTUM)))
    total = sum(p.shape[0] for p in parts)
    parts.append(jnp.zeros(((-total) % (N_DEV * _PACK_QUANTUM),), F32))
    return jnp.concatenate(parts).reshape(-1, LANES)


def _unpack(packed, shapes):
    flat = packed.reshape(-1)
    out, off = [], 0
    for sh in shapes:
        n = math.prod(sh)
        out.append(flat[off:off + n].reshape(sh))
        off += n + (-n) % _PACK_QUANTUM
    return out


class _Comm:
    _GROUPS = (("s5_w_in",) + _SMALL_SHARDED, ("s5_w_glu", "s5_w_out"), ("kv_w", "fox_w_in"), ("fox_w_out",))
    _SLOT_FORM = ("s5_w_in", "fox_w_in")

    def __init__(self, shards, vectors, early=()):
        self._shards = {**shards, **vectors}
        self._full, self._gathers = {}, {}
        self._early = list(early)
        self.token = jnp.zeros((), F32)
        for group in self._GROUPS[:-1]:
            self.token = self.token + self._start(group, ())[0, 0]
        self.late_token = None
        self._sent = []

    def _start(self, group, after):
        state, tok = _exchange_start([self._shards[n] for n in group], False, "gather_start_" + group[0], after,
                                     peers=_CHIP_PEERS)
        self._gathers[group] = state
        return tok

    def vector(self, name):
        return self._full[name]

    def weight(self, name, after):
        if name not in self._full:
            group = next(g for g in self._GROUPS if name in g)
            if group == self._GROUPS[0]:
                after = (list(after) if isinstance(after, (list, tuple)) else [after]) + self._early
            slots = _exchange_wait(self._gathers.pop(group), after, "gather_wait_" + group[0])
            slots = _forward_to_sibling(slots, "gather_forward_" + group[0])
            for n, sl in zip(group, slots):
                if n in _SMALL_SHARDED:
                    self._full[n] = sl.reshape(-1)
                else:
                    self._full[n] = sl if n in self._SLOT_FORM else _full_from_slots(n, sl)
            if group == self._GROUPS[-2]:
                self.late_token = self._start(self._GROUPS[-1], [slots[0]])[0, 0]
        return self._full[name]

    def send_grads(self, grads, name):
        names = list(grads)
        slots = [grads[n] if grads[n].ndim == 3 else _slots_from_full(n, grads[n]).astype(BF16) for n in names]
        state, tok = _exchange_start(slots, True, name + "_start")
        self._sent.append((names, state, name + "_wait"))
        return tok

    def received_grads(self, after):
        for names, state, name in self._sent:
            for n, recv in zip(names, _exchange_wait(state, after, name)):
                yield n, recv


def kernel(x, norm_pre, norm_post, s5_w_in, s5_a_re, s5_a_im, s5_log_dt, s5_b_re, s5_b_im, s5_c_re, s5_c_im, s5_d, s5_w_glu, s5_b_glu, s5_w_out, kv_norm, kv_w, kv_b_f, fox_w_in, fox_w_out, loss_target, m_norm_pre, m_norm_post, m_s5_w_in, m_s5_a_re, m_s5_a_im, m_s5_log_dt, m_s5_b_re, m_s5_b_im, m_s5_c_re, m_s5_c_im, m_s5_d, m_s5_w_glu, m_s5_b_glu, m_s5_w_out, m_kv_norm, m_kv_w, m_kv_b_f, m_fox_w_in, m_fox_w_out, v_norm_pre, v_norm_post, v_s5_w_in, v_s5_a_re, v_s5_a_im, v_s5_log_dt, v_s5_b_re, v_s5_b_im, v_s5_c_re, v_s5_c_im, v_s5_d, v_s5_w_glu, v_s5_b_glu, v_s5_w_out, v_kv_norm, v_kv_w, v_kv_b_f, v_fox_w_in, v_fox_w_out):
    env = dict(locals())
    wts = {n: env[n] for n in _WEIGHTS}
    mom = {n: env["m_" + n] for n in _WEIGHTS}
    var = {n: env["v_" + n] for n in _WEIGHTS}
    me = 4 * lax.axis_index("x") + 2 * lax.axis_index("y") + lax.axis_index("c")
    shard2d = {n: (wts[n].T if n == "kv_w" else wts[n].reshape(wts[n].shape[-2:])) for n in _BIG}
    full_shape = {n: ((wts[n].size * N_DEV,) if n in _SMALL_SHARDED else wts[n].shape) for n in _SMALL}

    def spread(n, v):
        if n not in _SMALL_SHARDED:
            return v
        flat = v.reshape(-1)
        return lax.dynamic_update_slice(jnp.zeros(full_shape[n], F32), flat, (me * flat.shape[0],))

    packed = [_pack([spread(n, src[n]) for n in _SMALL] + [jnp.zeros((1,), F32)]) for src in (wts, mom, var)]
    comm = _Comm({n: _cast_bf16(shard2d[n], "cast_" + n) for n in _BIG}, {n: wts[n].reshape(1, -1) for n in _SMALL_SHARDED}, packed)

    loss_local, grad_x, small = _local_step(
        x[0], loss_target[0], norm_pre, norm_post, kv_norm, kv_b_f, s5_a_re[0], s5_a_im[0], s5_log_dt[0],
        s5_b_re[0], s5_b_im[0], s5_c_re[0], s5_c_im[0], comm)

    small_pack = _pack([small[n] for n in _SMALL] + [loss_local.reshape(1)])
    slice_rows = small_pack.shape[0] // N_DEV
    small_state, small_tok = _exchange_start([small_pack.reshape(N_DEV, slice_rows, LANES)], True, "reduce_small_start")

    res = {}
    for n, recv in comm.received_grads([small_tok, grad_x]):
        if n == "kv_w":
            res[n] = [o.T for o in _adamw(recv, wts[n].T, mom[n].T, var[n].T, "adamw_" + n)]
        else:
            res[n] = _adamw(recv, wts[n], mom[n], var[n], "adamw_" + n)

    my_sum = _sum_parts(_exchange_wait(small_state, res[_BIG[0]][0], "reduce_small_wait")[0], "sum_small")
    g_all = _exchange([my_sum], False, "gather_small")[0].reshape(1, small_pack.shape[0], LANES)
    outs = _adamw(g_all, *packed, "adamw_small")
    unpacked = [_unpack(o, [full_shape[n] for n in _SMALL] + [(1,)]) for o in outs]
    loss = unpacked[0][-1][0]
    for i, n in enumerate(_SMALL):
        vals = [u[i] for u in unpacked]
        if n in _SMALL_SHARDED:
            k = wts[n].size
            vals = [lax.dynamic_slice(v, (me * k,), (k,)) for v in vals]
        res[n] = [v.reshape(wts[n].shape) for v in vals]

    return (loss, grad_x[None], *[res[n][0] for n in _WEIGHTS], *[res[n][1] for n in _WEIGHTS],
            *[res[n][2] for n in _WEIGHTS], *[res[n][3] for n in _WEIGHTS])
```

```python
import math

import jax
import jax.numpy as jnp
from jax import lax
from jax.experimental import pallas as pl
from jax.experimental.pallas import tpu as pltpu

F32 = jnp.float32
BF16 = jnp.bfloat16

N_DEV = 8
MESH_AXES = ("x", "y", "c")
S5_GROUP = 16
S5_STATE = 64
LANES = 128
SUBLANES = 8
GROUPS_PER_BLOCK = LANES // S5_GROUP
BLOCK_STATE = GROUPS_PER_BLOCK * S5_STATE
N_SEG = SUBLANES
HEAD_DIM = 128
RMS_EPS = 1e-6
NEG_INF = -1e30
LOG2E = math.log2(math.e)
ADAM_LR = 0.001
ADAM_B1 = 0.9
ADAM_B2 = 0.999
ADAM_EPS = 1e-08
ADAM_WD = 0.01
ADAM_STEP = 10
VMEM_LIMIT = 56 * 1024 * 1024


def _tile(n, pref, quantum=LANES):
    if n <= pref:
        return n
    t = (pref // quantum) * quantum
    while t >= quantum:
        if n % t == 0:
            return t
        t -= quantum
    return n


def _cparams(*sem):
    return pltpu.CompilerParams(dimension_semantics=sem if sem else None, vmem_limit_bytes=VMEM_LIMIT)


_DOT_DIMS = {"nn": ((1,), (0,)), "nt": ((1,), (1,)), "tn": ((0,), (0,))}


def _mm(a, b, mode, out_dtype, name, add=None, scale=None, b_cols=None, after=None, col_slots=False, b_slots=False,
        b_rows=None, epilogue=None):
    slot_w = b.shape[2] if b_slots else None
    b2d = (b.shape[1], b.shape[0] * b.shape[2]) if b_slots else b.shape
    b_shape = b2d if b_cols is None else (b2d[0], b_cols[1])
    if b_rows is not None:
        b_shape = (b_rows, b_shape[1])
    if mode == "nn":
        (M, K), (K2, N) = a.shape, b_shape
    elif mode == "nt":
        (M, K), (N, K2) = a.shape, b_shape
    else:
        (K, M), (K2, N) = a.shape, b_shape
    assert K == K2, (name, a.shape, b_shape)
    tm, tn, tk = _tile(M, 1024 if K <= 2048 else 512), (N // N_DEV if col_slots else _tile(N, 1024)), _tile(K, 4096)
    if b_slots and mode == "nn":
        tn = slot_w
    nk = K // tk
    dims = (_DOT_DIMS[mode], ((), ()))
    col0 = 0
    if b_cols is not None:
        assert mode != "tn" and b_cols[0] % (tn if mode == "nn" else tk) == 0
        col0 = b_cols[0] // (tn if mode == "nn" else tk)
    assert not b_slots or (mode == "nn" or (mode == "nt" and nk == 1 and b_cols is None))

    def body(*refs):
        a_ref, b_ref = refs[:2]
        c_ref = refs[2] if add is not None else None
        e_ref = refs[2 + (add is not None)] if epilogue is not None else None
        o_ref = refs[2 + (add is not None) + (epilogue is not None) + (after is not None)]
        if b_slots and mode == "nt":
            part = lax.dot_general(a_ref[:, :slot_w], b_ref[0], dims, preferred_element_type=F32)
            for sl in range(1, b_ref.shape[0]):
                part += lax.dot_general(a_ref[:, sl * slot_w:(sl + 1) * slot_w], b_ref[sl], dims, preferred_element_type=F32)
        else:
            part = lax.dot_general(a_ref[...], b_ref[...], dims, preferred_element_type=F32)

        def finish(r):
            if scale is not None:
                r = r * scale
            if add is not None:
                r = r + c_ref[...]
            if epilogue is not None:
                r = epilogue[0](r, e_ref[...])
            o_ref[...] = r.astype(out_dtype)

        if nk == 1:
            finish(part)
            return
        acc = refs[-1]
        k = pl.program_id(2)

        @pl.when(k == 0)
        def _():
            acc[...] = part

        @pl.when(jnp.logical_and(k > 0, k < nk - 1))
        def _():
            acc[...] += part

        @pl.when(k == nk - 1)
        def _():
            finish(acc[...] + part)

    if mode == "tn":
        a_spec = pl.BlockSpec((tk, tm), lambda i, j, k: (k, i))
    else:
        a_spec = pl.BlockSpec((tm, tk), lambda i, j, k: (i, k))
    if b_slots and mode == "nn":
        b_spec = pl.BlockSpec((None, tk, tn), lambda i, j, k: (j + col0, k, 0))
    elif b_slots:
        b_spec = pl.BlockSpec((b.shape[0], tn, slot_w), lambda i, j, k: (0, j, 0))
    elif mode == "nt":
        b_spec = pl.BlockSpec((tn, tk), lambda i, j, k: (j, k + col0))
    else:
        b_spec = pl.BlockSpec((tk, tn), lambda i, j, k: (k, j + col0))
    o_spec = pl.BlockSpec((tm, tn), lambda i, j, k: (i, j))
    in_specs = [a_spec, b_spec] + ([o_spec] if add is not None else [])
    args = (a, b) + ((add,) if add is not None else ())
    if epilogue is not None:
        in_specs.append(o_spec)
        args += (epilogue[1],)
    if after is not None:
        in_specs.append(pl.BlockSpec(after.shape, lambda i, j, k: (0, 0)))
        args += (after,)
    out_shape = jax.ShapeDtypeStruct((M, N), out_dtype)
    if col_slots:
        assert add is None
        o_spec = pl.BlockSpec((None, tm, tn), lambda i, j, k: (j, i, 0))
        out_shape = jax.ShapeDtypeStruct((N_DEV, M, tn), out_dtype)
    return pl.pallas_call(
        body, name=name, grid=(M // tm, N // tn, nk),
        in_specs=in_specs, out_specs=o_spec,
        out_shape=out_shape,
        scratch_shapes=[pltpu.VMEM((tm, tn), F32)] if nk > 1 else [],
        compiler_params=_cparams("parallel", "parallel", "arbitrary"),
    )(*args)


class _NatIn:
    def __init__(self, ref):
        self.ref = ref

    def __getitem__(self, idx):
        v = jnp.swapaxes(self.ref[...], 0, 1)
        return v.reshape(v.shape[0] * N_SEG, v.shape[2])


class _NatOut:
    def __init__(self, ref):
        self.ref = ref

    def __setitem__(self, idx, val):
        self.ref[...] = jnp.swapaxes(val.reshape(val.shape[0] // N_SEG, N_SEG, val.shape[1]), 0, 1)


def _rowcall(body, name, n_rows, ins, outs, tile_rows=256):
    tr = _tile(n_rows, tile_rows, SUBLANES * 2)
    n_in = len(ins)
    in_kinds = [k for _, k in ins]
    kinds = [k for _, _, k in outs]

    def kern(*refs):
        @pl.when(pl.program_id(0) == 0)
        def _():
            for r, kind in zip(refs[n_in:], kinds):
                if kind == "acc":
                    r[...] = jnp.zeros_like(r)

        wrapped = [_NatIn(r) if k == "nat" else r for r, k in zip(refs[:n_in], in_kinds)]
        wrapped += [_NatOut(r) if k == "nat" else r for r, k in zip(refs[n_in:], kinds)]
        body(*wrapped)

    in_specs, args = [], []
    for arr, kind in ins:
        if kind == "row":
            in_specs.append(pl.BlockSpec((tr, arr.shape[1]), lambda i: (i, 0)))
        elif kind == "nat":
            in_specs.append(pl.BlockSpec((N_SEG, tr // N_SEG, arr.shape[1]), lambda i: (0, i, 0)))
            arr = arr.reshape(N_SEG, n_rows // N_SEG, arr.shape[1])
        else:
            in_specs.append(pl.BlockSpec(arr.shape, lambda i, nd=arr.ndim: (0,) * nd))
        args.append(arr)
    out_specs, out_shape = [], []
    for width, dtype, kind in outs:
        if kind == "row":
            out_specs.append(pl.BlockSpec((tr, width), lambda i: (i, 0)))
            out_shape.append(jax.ShapeDtypeStruct((n_rows, width), dtype))
        elif kind == "right":
            out_specs.append(pl.BlockSpec((tr, width), lambda i: (i, 1)))
            out_shape.append(jax.ShapeDtypeStruct((n_rows, 2 * width), dtype))
        elif kind == "nat":
            out_specs.append(pl.BlockSpec((N_SEG, tr // N_SEG, width), lambda i: (0, i, 0)))
            out_shape.append(jax.ShapeDtypeStruct((N_SEG, n_rows // N_SEG, width), dtype))
        else:
            out_specs.append(pl.BlockSpec((1, width), lambda i: (0, 0)))
            out_shape.append(jax.ShapeDtypeStruct((1, width), F32))
    res = pl.pallas_call(
        kern, name=name, grid=(n_rows // tr,), in_specs=in_specs, out_specs=out_specs, out_shape=out_shape,
        compiler_params=_cparams("arbitrary"),
    )(*args)
    return [r.reshape(n_rows, r.shape[2]) if k == "nat" else r for r, k in zip(res, kinds)]


def _rstd(x):
    return lax.rsqrt(jnp.mean(x * x, axis=-1, keepdims=True) + RMS_EPS)


def _rms_bwd(x, g, dy):
    xh = x * _rstd(x)
    dxh = dy * g
    dx = _rstd(x) * (dxh - xh * jnp.mean(dxh * xh, axis=-1, keepdims=True))
    return dx, jnp.sum(dy * xh, axis=0, keepdims=True)


def _silu(z):
    return z * jax.nn.sigmoid(z)


def _norm_cast(x, g, name, x_kind="row"):
    def body(x_ref, g_ref, o_ref):
        x = x_ref[...]
        o_ref[...] = (x * _rstd(x) * g_ref[...]).astype(BF16)

    return _rowcall(body, name, x.shape[0], [(x, x_kind), (g, "full")], [(x.shape[1], BF16, "row")])[0]


def _resid_norm2(x, r0, g_kv, g_pre, name):
    def body(x_ref, r_ref, gk_ref, gp_ref, h_ref, nk_ref, np_ref):
        h = x_ref[...] + r_ref[...]
        h_ref[...] = h
        hn = h * _rstd(h)
        nk_ref[...] = (hn * gk_ref[...]).astype(BF16)
        np_ref[...] = (hn * gp_ref[...]).astype(BF16)

    d = x.shape[1]
    return _rowcall(body, name, x.shape[0], [(x, "row"), (r0, "row"), (g_kv, "full"), (g_pre, "full")],
                    [(d, F32, "row"), (d, BF16, "row"), (d, BF16, "row")])


def _post_norm(o, g, name, out_kind="row"):
    def body(o_ref, g_ref, r_ref):
        o = o_ref[...]
        r_ref[...] = o * _rstd(o) * g_ref[...]

    return _rowcall(body, name, o.shape[0], [(o, "row"), (g, "full")], [(o.shape[1], F32, out_kind)])[0]


def _post_norm_loss(o, g, h1, target, name):
    d = o.shape[1]

    def body(o_ref, g_ref, h_ref, t_ref, dh_ref, do_ref, acc_ref, dg_ref):
        o = o_ref[...]
        e = h_ref[...] + o * _rstd(o) * g_ref[...] - t_ref[...]
        dh = e * (1.0 / d)
        dh_ref[...] = dh
        acc_ref[...] += jnp.sum(e * e, axis=0, keepdims=True)
        dx, dg = _rms_bwd(o, g_ref[...], dh)
        do_ref[...] = dx.astype(BF16)
        dg_ref[...] += dg

    return _rowcall(body, name, o.shape[0], [(o, "row"), (g, "full"), (h1, "row"), (target, "row")],
                    [(d, F32, "row"), (d, BF16, "row"), (d, F32, "acc"), (d, F32, "acc")])


def _post_norm_bwd(dy, o, g, name, dy_kind="row"):
    def body(dy_ref, o_ref, g_ref, do_ref, dg_ref):
        dx, dg = _rms_bwd(o_ref[...], g_ref[...], dy_ref[...])
        do_ref[...] = dx.astype(BF16)
        dg_ref[...] += dg

    d = o.shape[1]
    return _rowcall(body, name, o.shape[0], [(dy, dy_kind), (o, "row"), (g, "full")], [(d, BF16, "row"), (d, F32, "acc")])


def _gate_bwd(d_oz, o, z, name):
    def body(d_ref, o_ref, z_ref, do_ref, dz_ref):
        _, vjp = jax.vjp(lambda o, z: o * _silu(z), o_ref[...], z_ref[...].astype(F32))
        do, dz = vjp(d_ref[...])
        do_ref[...] = do.astype(BF16)
        dz_ref[...] = dz.astype(BF16)

    w = o.shape[1]
    return _rowcall(body, name, o.shape[0], [(d_oz, "row"), (o, "row"), (z, "row")], [(w, BF16, "row"), (w, BF16, "right")])


def _norm_bwd2(dh2, h1, dxn1, dhn_kv, g_pre, g_kv, name):
    def body(dh2_ref, h_ref, d1_ref, dk_ref, gp_ref, gk_ref, dh1_ref, dgp_ref, dgk_ref):
        h = h_ref[...]
        dx1, dg1 = _rms_bwd(h, gp_ref[...], d1_ref[...])
        dxk, dgk = _rms_bwd(h, gk_ref[...], dk_ref[...])
        dh1_ref[...] = dh2_ref[...] + dx1 + dxk
        dgp_ref[...] += dg1
        dgk_ref[...] += dgk

    d = h1.shape[1]
    return _rowcall(body, name, h1.shape[0],
                    [(dh2, "row"), (h1, "row"), (dxn1, "row"), (dhn_kv, "row"), (g_pre, "full"), (g_kv, "full")],
                    [(d, F32, "row"), (d, F32, "acc"), (d, F32, "acc")])


def _norm_bwd1(dres, x, dxn, g, name):
    def body(dr_ref, x_ref, dn_ref, g_ref, dx_ref, dg_ref):
        dx, dg = _rms_bwd(x_ref[...], g_ref[...], dn_ref[...])
        dx_ref[...] = dr_ref[...] + dx
        dg_ref[...] += dg

    d = x.shape[1]
    return _rowcall(body, name, x.shape[0], [(dres, "nat"), (x, "nat"), (dxn, "row"), (g, "full")],
                    [(d, F32, "nat"), (d, F32, "acc")])


def _s5_gate(y_ssm, gp, b_glu, z, name):
    def body(y_ref, gp_ref, b_ref, z_ref, o_ref):
        yg = jax.nn.gelu(y_ref[...])
        o_ref[...] = (yg * jax.nn.sigmoid(gp_ref[...] + b_ref[...]) * _silu(z_ref[...].astype(F32))).astype(BF16)

    return _rowcall(body, name, y_ssm.shape[0], [(y_ssm, "row"), (gp, "row"), (b_glu, "full"), (z, "row")],
                    [(y_ssm.shape[1], BF16, "row")])[0]


def _s5_gate_bwd(dy3, y_ssm, gp, b_glu, z, name):
    def body(d_ref, y_ref, gp_ref, b_ref, z_ref, dz_ref, dgp_ref, dyg_ref, db_ref):
        yg = jax.nn.gelu(y_ref[...])
        _, vjp = jax.vjp(lambda yg, gp, z: yg * jax.nn.sigmoid(gp) * _silu(z), yg, gp_ref[...] + b_ref[...],
                         z_ref[...].astype(F32))
        dyg, dgp, dz = vjp(d_ref[...])
        dz_ref[...] = dz.astype(BF16)
        dgp_ref[...] = dgp.astype(BF16)
        dyg_ref[...] = dyg
        db_ref[...] += jnp.sum(dgp, axis=0, keepdims=True)

    w = y_ssm.shape[1]
    return _rowcall(body, name, y_ssm.shape[0],
                    [(dy3, "row"), (y_ssm, "row"), (gp, "row"), (b_glu, "full"), (z, "row")],
                    [(w, BF16, "right"), (w, BF16, "row"), (w, F32, "row"), (w, F32, "acc")])


def _cast_bf16(x, name):
    r, c = x.shape
    by_cols = r % (2 * SUBLANES) != 0
    tr, tc = (r, _tile(c, 256)) if by_cols else (_tile(r, 512, 2 * SUBLANES), c)
    pos = (lambda i: (0, i)) if by_cols else (lambda i: (i, 0))

    def body(x_ref, o_ref):
        o_ref[...] = x_ref[...].astype(BF16)

    return pl.pallas_call(
        body, name=name, grid=(c // tc if by_cols else r // tr,),
        in_specs=[pl.BlockSpec((tr, tc), pos)], out_specs=pl.BlockSpec((tr, tc), pos),
        out_shape=jax.ShapeDtypeStruct((r, c), BF16), compiler_params=_cparams("parallel"),
    )(x)


def _concat_cast(a, b, name):
    def body(a_ref, b_ref, o_ref):
        w = a_ref.shape[1]
        o_ref[:, :w] = a_ref[...].astype(BF16)
        o_ref[:, w:] = b_ref[...].astype(BF16)

    return _rowcall(body, name, a.shape[0], [(a, "row"), (b, "row")], [(a.shape[1] + b.shape[1], BF16, "row")])[0]


def _disc(ar, ai, ldt):
    dt = jnp.exp(ldt)
    mag = jnp.exp(ar * dt)
    abr = mag * jnp.cos(ai * dt)
    abi = mag * jnp.sin(ai * dt)
    den = ar * ar + ai * ai
    nr = abr - 1.0
    return abr, abi, (nr * ar + abi * ai) / den, (abi * ar - nr * ai) / den


def _s5_disc_fwd(a_re, a_im, ldt):
    def body(ar, ai, ld, o1, o2, o3, o4):
        o1[...], o2[...], o3[...], o4[...] = _disc(ar[...], ai[...], ld[...])

    sh = jax.ShapeDtypeStruct(a_re.shape, F32)
    return pl.pallas_call(body, name="s5_disc_fwd", out_shape=(sh, sh, sh, sh))(a_re, a_im, ldt)


def _s5_disc_bwd(a_re, a_im, ldt, d_abr, d_abi, d_cr, d_ci):
    def body(ar, ai, ld, g1, g2, g3, g4, o1, o2, o3):
        _, vjp = jax.vjp(_disc, ar[...], ai[...], ld[...])
        o1[...], o2[...], o3[...] = vjp((g1[...], g2[...], g3[...], g4[...]))

    sh = jax.ShapeDtypeStruct(a_re.shape, F32)
    return pl.pallas_call(body, name="s5_disc_bwd", out_shape=(sh, sh, jax.ShapeDtypeStruct(ldt.shape, F32)))(
        a_re, a_im, ldt, d_abr, d_abi, d_cr, d_ci)


def _bbar(cr, ci, br, bi):
    return cr * br - ci * bi, cr * bi + ci * br


def _s5_bbar_fwd(cr_col, ci_col, b_re, b_im):
    def body(cr, ci, br, bi, o1, o2):
        o1[...], o2[...] = _bbar(cr[...], ci[...], br[...], bi[...])

    w = b_re.shape[1]
    return _rowcall(body, "s5_bbar_fwd", b_re.shape[0], [(cr_col, "row"), (ci_col, "row"), (b_re, "row"), (b_im, "row")],
                    [(w, F32, "row"), (w, F32, "row")], tile_rows=1024)


def _s5_bbar_bwd(cr_col, ci_col, b_re, b_im, d_re, d_im):
    def body(cr, ci, br, bi, g1, g2, o1, o2, o3, o4):
        _, vjp = jax.vjp(_bbar, cr[...], ci[...], br[...], bi[...])
        o1[...], o2[...], o3[...], o4[...] = vjp((g1[...], g2[...]))

    w = b_re.shape[1]
    return _rowcall(body, "s5_bbar_bwd", b_re.shape[0],
                    [(cr_col, "row"), (ci_col, "row"), (b_re, "row"), (b_im, "row"), (d_re, "row"), (d_im, "row")],
                    [(1, F32, "row"), (1, F32, "row"), (w, F32, "row"), (w, F32, "row")], tile_rows=1024)


def _block_diag_in(t):
    g, p, c = t.shape
    nb = g // GROUPS_PER_BLOCK
    t4 = t.reshape(nb, GROUPS_PER_BLOCK, p, c).transpose(0, 1, 3, 2)
    eye = jnp.eye(GROUPS_PER_BLOCK, dtype=t.dtype)
    return (t4[:, :, :, None, :] * eye[None, :, None, :, None]).reshape(nb, GROUPS_PER_BLOCK * c, GROUPS_PER_BLOCK * p)


def _block_diag_in_extract(d, p, c):
    nb = d.shape[0]
    d5 = d.reshape(nb, GROUPS_PER_BLOCK, c, GROUPS_PER_BLOCK, p)
    diag = jnp.stack([d5[:, g, :, g, :] for g in range(GROUPS_PER_BLOCK)], axis=1)
    return diag.transpose(0, 1, 3, 2).reshape(nb * GROUPS_PER_BLOCK, p, c)


def _block_diag_out(t):
    g, c, p = t.shape
    nb = g // GROUPS_PER_BLOCK
    t4 = t.reshape(nb, GROUPS_PER_BLOCK, c, p).transpose(0, 1, 3, 2)
    eye = jnp.eye(GROUPS_PER_BLOCK, dtype=t.dtype)
    return (t4[:, :, :, None, :] * eye[None, :, None, :, None]).reshape(nb, GROUPS_PER_BLOCK * p, GROUPS_PER_BLOCK * c)


def _block_diag_out_extract(d, c, p):
    nb = d.shape[0]
    d5 = d.reshape(nb, GROUPS_PER_BLOCK, p, GROUPS_PER_BLOCK, c)
    diag = jnp.stack([d5[:, g, :, g, :] for g in range(GROUPS_PER_BLOCK)], axis=1)
    return diag.transpose(0, 1, 3, 2).reshape(nb * GROUPS_PER_BLOCK, c, p)


def _scan_step(ar, ai, hr, hi, xr, xi):
    return ar * hr - ai * hi + xr, ar * hi + ai * hr + xi


def _s5_scan_fwd(u, bd_re, bd_im, cd_re, cd_im, ab_re, ab_im, init_re, init_im, d_row, full, name):
    s, w = u.shape
    nb = w // LANES
    rows = _tile(s, 512, SUBLANES)
    nc = s // rows
    steps = rows // N_SEG
    ns = nb * BLOCK_STATE

    def body(u_ref, bdr, bdi, cdr, cdi, ar_ref, ai_ref, ir_ref, ii_ref, d_ref, *outs):
        if full:
            y_ref, yg_ref, hr_ref, hi_ref, er_ref, ei_ref, cr, ci = outs
        else:
            er_ref, ei_ref, hr_ref, hi_ref, cr, ci = outs
        c = pl.program_id(1)

        @pl.when(c == 0)
        def _():
            cr[...] = ir_ref[...]
            ci[...] = ii_ref[...]

        ub = u_ref[...].astype(BF16)
        hr_ref[...] = jnp.dot(ub, bdr[...], preferred_element_type=F32)
        hi_ref[...] = jnp.dot(ub, bdi[...], preferred_element_type=F32)
        ar, ai = ar_ref[...], ai_ref[...]

        hr, hi = cr[...], ci[...]
        for j in range(steps):
            rows_j = pl.ds(j * N_SEG, N_SEG)
            hr, hi = _scan_step(ar, ai, hr, hi, hr_ref[rows_j, :], hi_ref[rows_j, :])
            hr_ref[rows_j, :] = hr
            hi_ref[rows_j, :] = hi
        cr[...] = hr
        ci[...] = hi
        if full:
            y = (jnp.dot(hr_ref[...].astype(BF16), cdr[...], preferred_element_type=F32)
                 + jnp.dot(hi_ref[...].astype(BF16), cdi[...], preferred_element_type=F32)
                 + d_ref[...] * u_ref[...])
            y_ref[...] = y
            yg_ref[...] = jax.nn.gelu(y).astype(BF16)

        @pl.when(c == nc - 1)
        def _():
            er_ref[...] = hr
            ei_ref[...] = hi

    blk3 = lambda a: pl.BlockSpec((None,) + a.shape[1:], lambda k, c: (k, 0, 0))
    seg = pl.BlockSpec((N_SEG, BLOCK_STATE), lambda k, c: (0, k))
    st = pl.BlockSpec((rows, BLOCK_STATE), lambda k, c: (c, k))
    in_specs = [pl.BlockSpec((rows, LANES), lambda k, c: (c, k)), blk3(bd_re), blk3(bd_im), blk3(cd_re), blk3(cd_im),
                seg, seg, seg, seg, pl.BlockSpec((1, LANES), lambda k, c: (0, k))]
    seg_shape = jax.ShapeDtypeStruct((N_SEG, ns), F32)
    st_shape = jax.ShapeDtypeStruct((s, ns), F32)
    carry = [pltpu.VMEM((N_SEG, BLOCK_STATE), F32)] * 2
    if full:
        ych = pl.BlockSpec((rows, LANES), lambda k, c: (c, k))
        out_specs = [ych, ych, st, st, seg, seg]
        out_shape = [jax.ShapeDtypeStruct((s, w), F32), jax.ShapeDtypeStruct((s, w), BF16), st_shape, st_shape, seg_shape, seg_shape]
        scratch = carry
    else:
        out_specs = [seg, seg]
        out_shape = [seg_shape, seg_shape]
        scratch = [pltpu.VMEM((rows, BLOCK_STATE), F32)] * 2 + carry
    return pl.pallas_call(
        body, name=name, grid=(nb, nc), in_specs=in_specs, out_specs=out_specs, out_shape=out_shape,
        scratch_shapes=scratch, compiler_params=_cparams("parallel", "arbitrary"),
    )(u, bd_re, bd_im, cd_re, cd_im, ab_re, ab_im, init_re, init_im, d_row)


def _s5_seg_fix(e_re, e_im, ab_re, ab_im, seg_len, reverse, name):
    assert seg_len & (seg_len - 1) == 0

    def body(er, ei, ar, ai, o_re, o_im):
        pr, pi = ar[0:1, :], ai[0:1, :]
        for _ in range(int(math.log2(seg_len))):
            pr, pi = pr * pr - pi * pi, 2.0 * pr * pi
        tr = jnp.zeros_like(pr)
        ti = jnp.zeros_like(pr)
        order = list(range(N_SEG - 1, -1, -1)) if reverse else list(range(N_SEG))
        for n, sgm in enumerate(order):
            o_re[sgm:sgm + 1, :] = tr
            o_im[sgm:sgm + 1, :] = ti
            if n < N_SEG - 1:
                tr, ti = _scan_step(pr, pi, tr, ti, er[sgm:sgm + 1, :], ei[sgm:sgm + 1, :])

    sh = jax.ShapeDtypeStruct(e_re.shape, F32)
    return pl.pallas_call(body, name=name, out_shape=(sh, sh))(e_re, e_im, ab_re, ab_im)


def _s5_scan_bwd(dy, u, h_re, h_im, bd_re, bd_im, cd_re, cd_im, ab_re, ab_imn, gin_re, gin_im, d_row, full, name, duz=None):
    s, w = u.shape
    nb = w // LANES
    rows = _tile(s, 512, SUBLANES)
    nc = s // rows
    steps = rows // N_SEG
    ns = nb * BLOCK_STATE

    def body(dy_ref, u_ref, hr_ref, hi_ref, bdr, bdi, cdr, cdi, ar_ref, ai_ref, ir_ref, ii_ref, d_ref, *outs):
        if full:
            _, du_ref, dbr_ref, dbi_ref, dcr_ref, dci_ref, dar_ref, dai_ref, dd_ref, gr, gi, accr, acci = outs
        else:
            er_ref, ei_ref, gr, gi = outs
        c = pl.program_id(1)

        @pl.when(c == 0)
        def _():
            gr[pl.ds(rows, N_SEG), :] = ir_ref[...]
            gi[pl.ds(rows, N_SEG), :] = ii_ref[...]
            if full:
                for r in (dbr_ref, dbi_ref, dcr_ref, dci_ref, dd_ref, accr, acci):
                    r[...] = jnp.zeros_like(r)

        dyb = dy_ref[...].astype(BF16)
        nt = (_DOT_DIMS["nt"], ((), ()))
        tn = (_DOT_DIMS["tn"], ((), ()))
        gr[pl.ds(0, rows), :] = lax.dot_general(dyb, cdr[...], nt, preferred_element_type=F32)
        gi[pl.ds(0, rows), :] = lax.dot_general(dyb, cdi[...], nt, preferred_element_type=F32)
        ar, ai = ar_ref[...], ai_ref[...]

        g0r, g0i = gr[pl.ds(rows, N_SEG), :], gi[pl.ds(rows, N_SEG), :]
        for j in range(steps - 1, -1, -1):
            rows_j = pl.ds(j * N_SEG, N_SEG)
            g0r, g0i = _scan_step(ar, ai, g0r, g0i, gr[rows_j, :], gi[rows_j, :])
            gr[rows_j, :] = g0r
            gi[rows_j, :] = g0i
        if full:
            hr, hi = hr_ref[...], hi_ref[...]
            gnr, gni = gr[pl.ds(N_SEG, rows), :], gi[pl.ds(N_SEG, rows), :]
            accr[...] += jnp.sum((gnr * hr + gni * hi).reshape(steps, N_SEG, BLOCK_STATE), axis=0)
            acci[...] += jnp.sum((gni * hr - gnr * hi).reshape(steps, N_SEG, BLOCK_STATE), axis=0)
        gr[pl.ds(rows, N_SEG), :] = g0r
        gi[pl.ds(rows, N_SEG), :] = g0i
        if full:
            ub = u_ref[...].astype(BF16)
            gbr, gbi = gr[pl.ds(0, rows), :].astype(BF16), gi[pl.ds(0, rows), :].astype(BF16)
            dcr_ref[...] += lax.dot_general(hr.astype(BF16), dyb, tn, preferred_element_type=F32)
            dci_ref[...] += lax.dot_general(hi.astype(BF16), dyb, tn, preferred_element_type=F32)
            dbr_ref[...] += lax.dot_general(ub, gbr, tn, preferred_element_type=F32)
            dbi_ref[...] += lax.dot_general(ub, gbi, tn, preferred_element_type=F32)
            du_ref[...] = (lax.dot_general(gbr, bdr[...], nt, preferred_element_type=F32)
                           + lax.dot_general(gbi, bdi[...], nt, preferred_element_type=F32)
                           + d_ref[...] * dy_ref[...]).astype(BF16)
            dd_ref[...] += jnp.sum(dy_ref[...] * u_ref[...], axis=0, keepdims=True)

        @pl.when(c == nc - 1)
        def _():
            if full:
                dar_ref[...] = jnp.sum(accr[...], axis=0, keepdims=True)
                dai_ref[...] = jnp.sum(acci[...], axis=0, keepdims=True)
            else:
                er_ref[...] = g0r
                ei_ref[...] = g0i

    rev = lambda k, c: (nc - 1 - c, k)
    blk3 = lambda a: pl.BlockSpec((None,) + a.shape[1:], lambda k, c: (k, 0, 0))
    seg = pl.BlockSpec((N_SEG, BLOCK_STATE), lambda k, c: (0, k))
    st = pl.BlockSpec((rows, BLOCK_STATE), rev)
    ch = pl.BlockSpec((rows, LANES), rev)
    vec = pl.BlockSpec((1, LANES), lambda k, c: (0, k))
    if not full:
        st = pl.BlockSpec((rows, BLOCK_STATE), lambda k, c: (0, k))
    in_specs = [ch, ch if full else pl.BlockSpec((rows, LANES), lambda k, c: (0, k)), st, st,
                blk3(bd_re), blk3(bd_im), blk3(cd_re), blk3(cd_im), seg, seg, seg, seg, vec]
    args = [dy, u, h_re, h_im, bd_re, bd_im, cd_re, cd_im, ab_re, ab_imn, gin_re, gin_im, d_row]
    gbuf = [pltpu.VMEM((rows + N_SEG, BLOCK_STATE), F32)] * 2
    if full:
        row1 = pl.BlockSpec((1, BLOCK_STATE), lambda k, c: (0, k))
        out_specs = [ch, blk3(bd_re), blk3(bd_im), blk3(cd_re), blk3(cd_im), row1, row1, vec]
        out_shape = [jax.ShapeDtypeStruct(duz.shape, BF16),
                     jax.ShapeDtypeStruct(bd_re.shape, F32), jax.ShapeDtypeStruct(bd_im.shape, F32),
                     jax.ShapeDtypeStruct(cd_re.shape, F32), jax.ShapeDtypeStruct(cd_im.shape, F32),
                     jax.ShapeDtypeStruct((1, ns), F32), jax.ShapeDtypeStruct((1, ns), F32),
                     jax.ShapeDtypeStruct((1, w), F32)]
        scratch = gbuf + [pltpu.VMEM((N_SEG, BLOCK_STATE), F32)] * 2
        in_specs.append(pl.BlockSpec(memory_space=pl.ANY))
        args.append(duz)
        aliases = {len(args) - 1: 0}
    else:
        out_specs = [seg, seg]
        out_shape = [jax.ShapeDtypeStruct((N_SEG, ns), F32)] * 2
        scratch = gbuf
        aliases = {}
    return pl.pallas_call(
        body, name=name, grid=(nb, nc), in_specs=in_specs, out_specs=out_specs, out_shape=out_shape,
        input_output_aliases=aliases, scratch_shapes=scratch, compiler_params=_cparams("parallel", "arbitrary"),
    )(*args)


def _log_sigmoid(x):
    return jnp.minimum(x, 0.0) - jnp.log(1.0 + jnp.exp(-jnp.abs(x)))


def _tri(n, upper):
    r = lax.broadcasted_iota(jnp.int32, (n, n), 0)
    c = lax.broadcasted_iota(jnp.int32, (n, n), 1)
    return jnp.where((c >= r) if upper else (r >= c), 1.0, 0.0).astype(F32)


def _cum_fwd(f_logit, b_row, name):
    s, w = f_logit.shape
    t = _tile(s, 256, SUBLANES)

    def body(f_ref, b_ref, o_ref, carry):
        @pl.when(pl.program_id(0) == 0)
        def _():
            carry[...] = jnp.zeros_like(carry)

        lf = _log_sigmoid(f_ref[...] + b_ref[...])
        cum = jnp.dot(_tri(t, False), lf, precision=lax.Precision.HIGHEST, preferred_element_type=F32) + carry[...]
        o_ref[...] = cum * LOG2E
        carry[...] = cum[t - 1:t, :]

    return pl.pallas_call(
        body, name=name, grid=(s // t,),
        in_specs=[pl.BlockSpec((t, w), lambda i: (i, 0)), pl.BlockSpec((1, w), lambda i: (0, 0))],
        out_specs=pl.BlockSpec((t, w), lambda i: (i, 0)), out_shape=jax.ShapeDtypeStruct((s, w), F32),
        scratch_shapes=[pltpu.VMEM((1, w), F32)], compiler_params=_cparams("arbitrary"),
    )(f_logit, b_row)


def _cum_bwd(dcq, dck, f_logit, b_row, name):
    s, w = f_logit.shape
    t = _tile(s, 256, SUBLANES)
    nt = s // t

    def body(q_ref, k_ref, f_ref, b_ref, df_ref, db_ref, carry):
        @pl.when(pl.program_id(0) == 0)
        def _():
            carry[...] = jnp.zeros_like(carry)
            db_ref[...] = jnp.zeros_like(db_ref)

        dc = q_ref[...] - k_ref[...]
        rc = jnp.dot(_tri(t, True), dc, precision=lax.Precision.HIGHEST, preferred_element_type=F32) + carry[...]
        carry[...] = rc[0:1, :]
        df = rc * (1.0 - jax.nn.sigmoid(f_ref[...] + b_ref[...]))
        df_ref[...] = df.astype(BF16)
        db_ref[...] += jnp.sum(df, axis=0, keepdims=True)

    rev = pl.BlockSpec((t, w), lambda i: (nt - 1 - i, 0))
    one = pl.BlockSpec((1, w), lambda i: (0, 0))
    return pl.pallas_call(
        body, name=name, grid=(nt,), in_specs=[rev, rev, rev, one], out_specs=[rev, one],
        out_shape=[jax.ShapeDtypeStruct((s, w), BF16), jax.ShapeDtypeStruct((1, w), F32)],
        scratch_shapes=[pltpu.VMEM((1, w), F32)], compiler_params=_cparams("arbitrary"),
    )(dcq, dck, f_logit, b_row)


def _head_col(cum_tile, h):
    lane = lax.broadcasted_iota(jnp.int32, cum_tile.shape, 1)
    return jnp.sum(jnp.where(lane == h, cum_tile, 0.0), axis=1, keepdims=True)


def _attn_tiles(s):
    return _tile(s, 512, LANES)


def _exp2_rows(sc, sub):
    return jnp.concatenate([jnp.exp2(sc[:, b * LANES:(b + 1) * LANES] - sub) for b in range(sc.shape[1] // LANES)], axis=1)


def _row_of(rep):
    return jnp.transpose(rep)[0:1, :]


def _causal(sc, keys_on_rows):
    r = lax.broadcasted_iota(jnp.int32, sc.shape, 0)
    c = lax.broadcasted_iota(jnp.int32, sc.shape, 1)
    return jnp.where((r <= c) if keys_on_rows else (c <= r), sc, NEG_INF)


def _fox_fwd(q2, kv, cum2_t, z, name):
    s, w = q2.shape
    nh = w // HEAD_DIM
    tq = _attn_tiles(s)
    nq = s // tq
    nt = (_DOT_DIMS["nt"], ((), ()))

    def body(q_ref, k_ref, v_ref, ct_ref, z_ref, o_ref, oz_ref, lse_row_ref, m_s, acc_s, vaug, s_buf):
        i = pl.program_id(1)

        @pl.when(i == 0)
        def _():
            vaug[:, :HEAD_DIM] = v_ref[...]
            vaug[:, HEAD_DIM:] = jnp.ones((s, LANES), BF16)

        qb = q_ref[...]
        m_s[...] = jnp.full_like(m_s, NEG_INF)
        acc_s[...] = jnp.zeros_like(acc_s)

        def scores(j):
            off = pl.multiple_of(j * tq, tq)
            return lax.dot_general(qb, k_ref[pl.ds(off, tq), :], nt, preferred_element_type=F32) - ct_ref[:, pl.ds(off, tq)]

        def softmax_pv(j, sc):
            m_old = m_s[...]
            m_new = jnp.maximum(m_old, jnp.max(sc, axis=1, keepdims=True))
            p = _exp2_rows(sc, m_new)
            alpha = jnp.exp2(m_old - m_new)
            pv = jnp.dot(p.astype(BF16), vaug[pl.ds(pl.multiple_of(j * tq, tq), tq), :], preferred_element_type=F32)
            acc_s[...] = jnp.concatenate([alpha, alpha], axis=1) * acc_s[...] + pv
            m_s[...] = m_new

        s_buf[...] = scores(0)

        def loop(j, carry):
            nxt = scores(j + 1)
            softmax_pv(j, s_buf[...])
            s_buf[...] = nxt
            return carry

        lax.fori_loop(0, i, loop, 0)
        softmax_pv(i, _causal(s_buf[...], False))
        l = acc_s[:, HEAD_DIM:]
        o = acc_s[:, :HEAD_DIM] / l
        o_ref[...] = o
        oz_ref[...] = (o * _silu(z_ref[...].astype(F32))).astype(BF16)
        lse_row_ref[...] = _row_of(m_s[...] + jnp.log(l) * LOG2E)

    return pl.pallas_call(
        body, name=name, grid=(nh, nq),
        in_specs=[pl.BlockSpec((tq, HEAD_DIM), lambda h, i: (i, h)),
                  pl.BlockSpec((s, HEAD_DIM), lambda h, i: (0, h)),
                  pl.BlockSpec((s, HEAD_DIM), lambda h, i: (0, nh + h)),
                  pl.BlockSpec((None, 1, s), lambda h, i: (h, 0, 0)),
                  pl.BlockSpec((tq, HEAD_DIM), lambda h, i: (i, h))],
        out_specs=[pl.BlockSpec((tq, HEAD_DIM), lambda h, i: (i, h)),
                   pl.BlockSpec((tq, HEAD_DIM), lambda h, i: (i, h)),
                   pl.BlockSpec((None, 1, tq), lambda h, i: (h, 0, i))],
        out_shape=[jax.ShapeDtypeStruct((s, w), F32), jax.ShapeDtypeStruct((s, w), BF16),
                   jax.ShapeDtypeStruct((nh, 1, s), F32)],
        scratch_shapes=[pltpu.VMEM((tq, LANES), F32), pltpu.VMEM((tq, HEAD_DIM + LANES), F32),
                        pltpu.VMEM((s, HEAD_DIM + LANES), BF16), pltpu.VMEM((tq, tq), F32)],
        compiler_params=_cparams("arbitrary", "arbitrary"),
    )(q2, kv, kv, cum2_t, z)


def _fox_bwd(q2, kv, do, o, lse2_t, cum2, dqz, name):
    s, w = q2.shape
    nh = w // HEAD_DIM
    tk = _attn_tiles(s)
    nk = s // tk
    scale = HEAD_DIM ** -0.5
    nt = (_DOT_DIMS["nt"], ((), ()))
    tn = (_DOT_DIMS["tn"], ((), ()))

    def body(q_ref, k_ref, v_ref, do_ref, o_ref, lse_ref, c_ref, _, dk_ref, dv_ref, dq_ref, dcq_ref, dck_ref,
             dk_s, dv_s, dc_s, dq_s, dcq_s, dl_s, s_buf, dp_buf):
        h, j = pl.program_id(0), pl.program_id(1)

        @pl.when(j == 0)
        def _():
            dq_s[...] = jnp.zeros_like(dq_s)
            dcq_s[...] = jnp.zeros_like(dcq_s)
            for i in range(nk):
                rows = pl.ds(i * tk, tk)
                d = jnp.sum(do_ref[rows, :].astype(F32) * o_ref[rows, :], axis=1, keepdims=True)
                dl_s[:, i * tk:(i + 1) * tk] = _row_of(jnp.broadcast_to(d, (tk, LANES)))

        kb = k_ref[...]
        vb = v_ref[...]
        ck = jnp.broadcast_to(_head_col(c_ref[...], h), (tk, LANES))
        dk_s[...] = jnp.zeros_like(dk_s)
        dv_s[...] = jnp.zeros_like(dv_s)
        dc_s[...] = jnp.zeros_like(dc_s)

        def scores(i):
            off = pl.multiple_of(i * tk, tk)
            sc = lax.dot_general(kb, q_ref[pl.ds(off, tk), :], nt, preferred_element_type=F32) - lse_ref[:, pl.ds(off, tk)]
            dp = lax.dot_general(vb, do_ref[pl.ds(off, tk), :], nt, preferred_element_type=F32) - dl_s[:, pl.ds(off, tk)]
            return sc, dp

        def accumulate(i, sc, dp):
            off = pl.multiple_of(i * tk, tk)
            p = _exp2_rows(sc, ck)
            dv_s[...] += jnp.dot(p.astype(BF16), do_ref[pl.ds(off, tk), :], preferred_element_type=F32)
            ds = p * dp
            dsb = ds.astype(BF16)
            dk_s[...] += jnp.dot(dsb, q_ref[pl.ds(off, tk), :], preferred_element_type=F32)
            dq_s[pl.ds(off, tk), :] += lax.dot_general(dsb, kb, tn, preferred_element_type=F32)
            dcq_s[:, pl.ds(off, tk)] += jnp.sum(ds, axis=0, keepdims=True)
            part = ds[:, :LANES]
            for b in range(1, tk // LANES):
                part = part + ds[:, b * LANES:(b + 1) * LANES]
            dc_s[...] += part

        sc0, dp0 = scores(j)
        s_buf[...] = _causal(sc0, True)
        dp_buf[...] = dp0

        def loop(i, carry):
            nxt = scores(i + 1)
            accumulate(i, s_buf[...], dp_buf[...])
            s_buf[...], dp_buf[...] = nxt
            return carry

        lax.fori_loop(j, nk - 1, loop, 0)
        accumulate(nk - 1, s_buf[...], dp_buf[...])
        dk_ref[...] = (dk_s[...] * (1.0 / LOG2E)).astype(BF16)
        dv_ref[...] = dv_s[...].astype(BF16)
        dck_ref[...] = jnp.sum(jnp.transpose(dc_s[...]), axis=0, keepdims=True)

        @pl.when(j == nk - 1)
        def _():
            dq_ref[...] = (dq_s[...] * scale).astype(BF16)
            dcq_ref[...] = dcq_s[...]

    col = pl.BlockSpec((s, HEAD_DIM), lambda h, j: (0, h))
    row = pl.BlockSpec((None, 1, s), lambda h, j: (h, 0, 0))
    kspec = pl.BlockSpec((tk, HEAD_DIM), lambda h, j: (j, h))
    return pl.pallas_call(
        body, name=name, grid=(nh, nk),
        in_specs=[col, kspec, pl.BlockSpec((tk, HEAD_DIM), lambda h, j: (j, nh + h)), col, col, row,
                  pl.BlockSpec((tk, LANES), lambda h, j: (j, 0)), pl.BlockSpec(memory_space=pl.ANY)],
        out_specs=[kspec, kspec, col, row, pl.BlockSpec((None, 1, tk), lambda h, j: (h, 0, j))],
        out_shape=[jax.ShapeDtypeStruct((s, w), BF16), jax.ShapeDtypeStruct((s, w), BF16),
                   jax.ShapeDtypeStruct(dqz.shape, BF16), jax.ShapeDtypeStruct((nh, 1, s), F32),
                   jax.ShapeDtypeStruct((nh, 1, s), F32)],
        input_output_aliases={7: 2},
        scratch_shapes=[pltpu.VMEM((tk, HEAD_DIM), F32), pltpu.VMEM((tk, HEAD_DIM), F32), pltpu.VMEM((tk, LANES), F32),
                        pltpu.VMEM((s, HEAD_DIM), F32), pltpu.VMEM((1, s), F32), pltpu.VMEM((1, s), F32),
                        pltpu.VMEM((tk, tk), F32), pltpu.VMEM((tk, tk), F32)],
        compiler_params=_cparams("arbitrary", "arbitrary"),
    )(q2, kv, kv, do, o, lse2_t, cum2, dqz)


_ALL_PEERS = tuple(range(1, N_DEV))
_CHIP_PEERS = (1, 2, 4, 6)


def _exchange_copies(ins, outs, send_sems, recv_sems, local_sems, scatter, peers=_ALL_PEERS):
    x, y, c = (lax.axis_index(a) for a in MESH_AXES)
    me = 4 * x + 2 * y + c
    local, remote = [], []
    for a in range(len(ins)):
        local.append(pltpu.make_async_copy(ins[a].at[me] if scatter else ins[a], outs[a].at[me], local_sems.at[a]))
        for k in peers:
            px, py, pc = (1 - x if k & 4 else x), (1 - y if k & 2 else y), (1 - c if k & 1 else c)
            remote.append(pltpu.make_async_remote_copy(
                src_ref=ins[a].at[4 * px + 2 * py + pc] if scatter else ins[a], dst_ref=outs[a].at[me],
                send_sem=send_sems.at[a * (N_DEV - 1) + k - 1], recv_sem=recv_sems.at[a * (N_DEV - 1) + k - 1],
                device_id=(px, py, pc), device_id_type=pl.DeviceIdType.MESH))
    return local, remote


def _exchange_out_shapes(arrs, scatter):
    return [((N_DEV,) + a.shape[1:]) if scatter else ((N_DEV,) + a.shape) for a in arrs]


def _exchange(arrs, scatter, name, peers=_ALL_PEERS):
    n = len(arrs)

    def body(*refs):
        local, remote = _exchange_copies(refs[:n], refs[n:2 * n], *refs[2 * n:], scatter, peers)
        for cp in local + remote:
            cp.start()
        for cp in remote:
            cp.wait_send()
            cp.wait_recv()
        for cp in local:
            cp.wait()

    out_shape = [jax.ShapeDtypeStruct(s, a.dtype) for s, a in zip(_exchange_out_shapes(arrs, scatter), arrs)]
    return pl.pallas_call(
        body, name=name, out_shape=out_shape,
        in_specs=[pl.BlockSpec(memory_space=pl.ANY)] * n, out_specs=[pl.BlockSpec(memory_space=pl.ANY)] * n,
        scratch_shapes=[pltpu.SemaphoreType.DMA((n * (N_DEV - 1),)), pltpu.SemaphoreType.DMA((n * (N_DEV - 1),)),
                        pltpu.SemaphoreType.DMA((n,))],
    )(*arrs)


_HBM = pl.BlockSpec(memory_space=pltpu.HBM)
_SEM = pl.BlockSpec(memory_space=pltpu.SEMAPHORE)


def _exchange_start(arrs, scatter, name, after=(), peers=_ALL_PEERS):
    n = len(arrs)
    after = list(after)
    lands = [lax.empty(s, a.dtype) for s, a in zip(_exchange_out_shapes(arrs, scatter), arrs)]

    def body(*refs):
        ins, outs = refs[:n], refs[n:2 * n]
        send_sems, recv_sems, local_sems = refs[2 * n + len(after):2 * n + len(after) + 3]
        token = refs[-1]
        local, remote = _exchange_copies(ins, outs, send_sems, recv_sems, local_sems, scatter, peers)
        for cp in local + remote:
            cp.start()
        token[...] = jnp.zeros_like(token)

    hbm = lambda a: pltpu.HBM(a.shape, a.dtype)
    res = pl.pallas_call(
        body, name=name,
        out_shape=(pltpu.SemaphoreType.DMA((n * (N_DEV - 1),)), pltpu.SemaphoreType.DMA((n * (N_DEV - 1),)),
                   pltpu.SemaphoreType.DMA((n,)), *[hbm(a) for a in arrs], *[hbm(a) for a in lands],
                   jax.ShapeDtypeStruct((SUBLANES, LANES), F32)),
        in_specs=[_HBM] * (2 * n) + [pl.BlockSpec(memory_space=pl.ANY)] * len(after),
        out_specs=(_SEM, _SEM, _SEM, *[_HBM] * (2 * n), pl.BlockSpec(memory_space=pltpu.VMEM)),
        input_output_aliases={i: 3 + i for i in range(2 * n)},
        compiler_params=pltpu.CompilerParams(has_side_effects=pltpu.SideEffectType.DATAFLOW_SIDE_EFFECTING),
    )(*[pltpu.with_memory_space_constraint(a, pltpu.HBM) for a in list(arrs) + lands], *after)
    return (n, scatter, res[:3], res[3:3 + n], res[3 + n:3 + 2 * n], peers), res[-1]


def _exchange_wait(state, after, name):
    n, scatter, sems, srcs, lands, peers = state
    after = list(after) if isinstance(after, (list, tuple)) else [after]

    def body(*refs):
        ins, outs = refs[:n], refs[n:2 * n]
        send_sems, recv_sems, local_sems = refs[2 * n:2 * n + 3]
        local, remote = _exchange_copies(ins, outs, send_sems, recv_sems, local_sems, scatter, peers)
        for cp in remote:
            cp.wait_send()
            cp.wait_recv()
        for cp in local:
            cp.wait()

    hbm = lambda a: pltpu.HBM(a.shape, a.dtype)
    res = pl.pallas_call(
        body, name=name,
        out_shape=(*[hbm(a) for a in srcs], *[hbm(a) for a in lands]),
        in_specs=[_HBM] * (2 * n) + [_SEM] * 3 + [pl.BlockSpec(memory_space=pl.ANY)] * len(after),
        out_specs=tuple([_HBM] * (2 * n)),
        input_output_aliases={i: i for i in range(2 * n)},
        compiler_params=pltpu.CompilerParams(has_side_effects=pltpu.SideEffectType.DATAFLOW_SIDE_EFFECTING),
    )(*srcs, *lands, *sems, *after)
    return list(res[n:])


def _forward_to_sibling(slots, name):
    n = len(slots)
    hops = (2, 4, 6)

    def body(*refs):
        ins, outs, (send_sems, recv_sems) = refs[:n], refs[n:2 * n], refs[2 * n:]
        x, y, c = (lax.axis_index(a) for a in MESH_AXES)
        copies = []
        for a in range(n):
            for i, k in enumerate(hops):
                slot = 4 * (1 - x if k & 4 else x) + 2 * (1 - y if k & 2 else y) + c
                copies.append(pltpu.make_async_remote_copy(
                    src_ref=ins[a].at[slot], dst_ref=outs[a].at[slot],
                    send_sem=send_sems.at[a * len(hops) + i], recv_sem=recv_sems.at[a * len(hops) + i],
                    device_id=(x, y, 1 - c), device_id_type=pl.DeviceIdType.MESH))
        for cp in copies:
            cp.start()
        for cp in copies:
            cp.wait_send()
            cp.wait_recv()

    return pl.pallas_call(
        body, name=name, out_shape=[jax.ShapeDtypeStruct(s.shape, s.dtype) for s in slots],
        in_specs=[pl.BlockSpec(memory_space=pl.ANY)] * n, out_specs=[pl.BlockSpec(memory_space=pl.ANY)] * n,
        input_output_aliases={i: i for i in range(n)},
        scratch_shapes=[pltpu.SemaphoreType.DMA((n * len(hops),)), pltpu.SemaphoreType.DMA((n * len(hops),))],
    )(*slots)


def _adamw_math(w, g, m, v):
    m = ADAM_B1 * m + (1.0 - ADAM_B1) * g
    v = ADAM_B2 * v + (1.0 - ADAM_B2) * (g * g)
    m_hat = m / (1.0 - ADAM_B1 ** ADAM_STEP)
    v_hat = v / (1.0 - ADAM_B2 ** ADAM_STEP)
    return -ADAM_LR * (m_hat / (jnp.sqrt(v_hat) + ADAM_EPS) + ADAM_WD * w), m, v


def _slot_sum(p_ref):
    g = p_ref[0].astype(F32)
    for d in range(1, p_ref.shape[0]):
        g = g + p_ref[d].astype(F32)
    return g


def _adamw_tile(r, c):
    return _tile(r, max(SUBLANES, (256 * 1024) // c // SUBLANES * SUBLANES), SUBLANES)


def _adamw(parts, w, m, v, name):
    r, c = w.shape[-2:]
    by_cols = r % SUBLANES != 0
    tr, tc = (r, _tile(c, 256)) if by_cols else (_adamw_tile(r, c), c)

    def body(p_ref, w_ref, m_ref, v_ref, g_ref, d_ref, nm_ref, nv_ref):
        g = _slot_sum(p_ref)
        g_ref[...] = g
        d_ref[...], nm_ref[...], nv_ref[...] = _adamw_math(w_ref[...], g, m_ref[...], v_ref[...])

    pos = (lambda i: (0, i)) if by_cols else (lambda i: (i, 0))
    if w.ndim == 3:
        blk = pl.BlockSpec((None, tr, tc), lambda i: (0,) + pos(i))
    else:
        blk = pl.BlockSpec((tr, tc), pos)
    sh = jax.ShapeDtypeStruct(w.shape, F32)
    return pl.pallas_call(
        body, name=name, grid=(c // tc if by_cols else r // tr,),
        in_specs=[pl.BlockSpec((parts.shape[0], tr, tc), lambda i: (0,) + pos(i)), blk, blk, blk],
        out_specs=[blk] * 4, out_shape=[sh] * 4, compiler_params=_cparams("parallel"),
    )(parts, w, m, v)


def _sum_parts(parts, name):
    _, r, c = parts.shape
    tr = _adamw_tile(r, c)

    def body(p_ref, o_ref):
        o_ref[...] = _slot_sum(p_ref)

    return pl.pallas_call(
        body, name=name, grid=(r // tr,),
        in_specs=[pl.BlockSpec((parts.shape[0], tr, c), lambda i: (0, i, 0))],
        out_specs=pl.BlockSpec((tr, c), lambda i: (i, 0)), out_shape=jax.ShapeDtypeStruct((r, c), F32),
        compiler_params=_cparams("parallel"),
    )(parts)


def _lane_pad(a, width=LANES):
    return jnp.pad(a, ((0, 0), (0, width - a.shape[1])))


def _local_step(x, target, norm_pre, norm_post, kv_norm, kv_b_f, a_re, a_im, log_dt, b_re, b_im, c_re, c_im, comm):
    s, d = x.shape
    g, p = a_re.shape
    w = g * S5_GROUP
    fw = d
    nh = fw // HEAD_DIM
    seg_len = s // N_SEG
    row = lambda v: v.reshape(1, -1)
    g_pre0, g_pre1, g_post0, g_post1, g_kv = row(norm_pre[0]), row(norm_pre[1]), row(norm_post[0]), row(norm_post[1]), row(kv_norm)

    ldt = log_dt.reshape(g, 1)
    abr, abi, cr, ci = _s5_disc_fwd(a_re, a_im, ldt)
    cr_col, ci_col = cr.reshape(g * p, 1), ci.reshape(g * p, 1)
    b_re2, b_im2 = b_re.reshape(g * p, S5_GROUP), b_im.reshape(g * p, S5_GROUP)
    bb_re, bb_im = _s5_bbar_fwd(cr_col, ci_col, b_re2, b_im2)
    bd_re = _block_diag_in(bb_re.reshape(g, p, S5_GROUP)).astype(BF16)
    bd_im = _block_diag_in(bb_im.reshape(g, p, S5_GROUP)).astype(BF16)
    cd_re = _block_diag_out(c_re).astype(BF16)
    cd_im = _block_diag_out(-c_im).astype(BF16)
    ab_re = jnp.broadcast_to(abr.reshape(1, g * p), (N_SEG, g * p))
    ab_im = jnp.broadcast_to(abi.reshape(1, g * p), (N_SEG, g * p))
    zero_seg = jnp.zeros((N_SEG, g * p), F32)

    xn0 = _norm_cast(x, g_pre0 + comm.token, "norm_pre0", x_kind="nat")
    w_in = comm.weight("s5_w_in", [xn0, bd_re, bd_im, cd_re, cd_im, ab_re, ab_im])
    d_row, bglu_row = row(comm.vector("s5_d")), row(comm.vector("s5_b_glu"))
    u = _mm(xn0, w_in, "nn", F32, "s5_in_u", b_cols=(0, w), b_slots=True)
    z0 = _mm(xn0, w_in, "nn", BF16, "s5_in_z", b_cols=(w, w), b_slots=True)
    e_re, e_im = _s5_scan_fwd(u, bd_re, bd_im, cd_re, cd_im, ab_re, ab_im, zero_seg, zero_seg, d_row, False, "s5_scan_ends")
    i_re, i_im = _s5_seg_fix(e_re, e_im, ab_re, ab_im, seg_len, False, "s5_seg_fix")
    y_ssm, yg, h_re, h_im, _, _ = _s5_scan_fwd(u, bd_re, bd_im, cd_re, cd_im, ab_re, ab_im, i_re, i_im, d_row, True, "s5_scan")
    w_glu, w_out = comm.weight("s5_w_glu", yg), comm.weight("s5_w_out", yg)
    gp = _mm(yg, w_glu, "nn", BF16, "s5_glu")
    y3 = _s5_gate(y_ssm, gp, bglu_row, z0, "s5_gate")
    w_kvt, fw_in = comm.weight("kv_w", y3), comm.weight("fox_w_in", y3)
    w_ft = jnp.pad(w_kvt[2 * fw:], ((0, LANES - nh), (0, 0)))
    o0 = _mm(y3, w_out, "nn", F32, "s5_out")
    r0 = _post_norm(o0, g_post0 + comm.late_token, "norm_post0", out_kind="nat")

    h1, hn_kv, xn1 = _resid_norm2(x, r0, g_kv, g_pre1, "resid_norms")
    kv = _mm(hn_kv, w_kvt, "nt", BF16, "kv_proj", b_rows=2 * fw)
    f_logit = _mm(hn_kv, w_ft, "nt", F32, "f_proj")
    bf_row = _lane_pad(row(kv_b_f))
    cum2 = _cum_fwd(f_logit, bf_row, "cum_fwd")
    cum2_t = cum2[:, :nh].T.reshape(nh, 1, s)
    q2 = _mm(xn1, fw_in, "nn", BF16, "fox_q", scale=HEAD_DIM ** -0.5 * LOG2E, b_cols=(0, fw), b_slots=True)
    z1 = _mm(xn1, fw_in, "nn", BF16, "fox_z", b_cols=(fw, fw), b_slots=True)
    o, oz, lse2_t = _fox_fwd(q2, kv, cum2_t, z1, "fox_fwd")
    fw_out = comm.weight("fox_w_out", oz)
    o1 = _mm(oz, fw_out, "nn", F32, "fox_out")
    dh2, do1, sq, dg_post1 = _post_norm_loss(o1, g_post1, h1, target, "norm_post1_loss")
    loss = 0.5 * jnp.sum(sq) / d

    d_fw_out = _mm(oz, do1, "tn", BF16, "fox_out_dw")
    d_oz = _mm(do1, fw_out, "nt", F32, "fox_out_dx")
    do, dqz = _gate_bwd(d_oz, o, z1, "fox_gate_bwd")
    dk, dv, dqz, dcq, dck = _fox_bwd(q2, kv, do, o, lse2_t, cum2, dqz, "fox_bwd")
    d_fw_in = _mm(xn1, dqz, "tn", BF16, "fox_in_dw", col_slots=True)
    dxn1 = _mm(dqz, fw_in, "nt", F32, "fox_in_dx", b_slots=True)
    dcq_sl = _lane_pad(dcq.reshape(nh, s).T)
    dck_sl = _lane_pad(dck.reshape(nh, s).T)
    df, db_f = _cum_bwd(dcq_sl, dck_sl, f_logit, bf_row, "cum_bwd")
    dkv = _concat_cast(dk, dv, "fox_dkv")
    d_w_kvmt = _mm(dkv, hn_kv, "tn", BF16, "kv_dw")
    d_w_ft = _mm(df, hn_kv, "tn", BF16, "f_dw")
    dhn_f = _mm(df, w_ft, "nn", F32, "f_dx")
    dhn_kv = _mm(dkv, w_kvt, "nn", F32, "kv_dx", add=dhn_f, b_rows=2 * fw)
    d_w_kvt = jnp.concatenate([d_w_kvmt, d_w_ft[:nh]], axis=0)
    tok = comm.send_grads(dict(fox_w_out=d_fw_out, fox_w_in=d_fw_in, kv_w=d_w_kvt), "exchange_fox")
    dh1, dg_pre1, dg_kv = _norm_bwd2(dh2, h1, dxn1, dhn_kv, g_pre1, g_kv, "resid_norms_bwd")

    do0, dg_post0 = _post_norm_bwd(dh1, o0, g_post0 + tok[0, 0], "norm_post0_bwd", dy_kind="nat")
    d_w_out = _mm(y3, do0, "tn", BF16, "s5_out_dw")
    dy3 = _mm(do0, w_out, "nt", F32, "s5_out_dx")
    duz, dgp, dyg_direct, db_glu = _s5_gate_bwd(dy3, y_ssm, gp, bglu_row, z0, "s5_gate_bwd")
    d_w_glu = _mm(yg, dgp, "tn", BF16, "s5_glu_dw")
    gelu_bwd = lambda dyg, y: jax.vjp(jax.nn.gelu, y)[1](dyg)[0]
    dy_ssm = _mm(dgp, w_glu, "nt", F32, "s5_glu_dx", add=dyg_direct, epilogue=(gelu_bwd, y_ssm))
    d_row = d_row + comm.send_grads(dict(s5_w_out=d_w_out, s5_w_glu=d_w_glu), "exchange_s5")[0, 0]
    ab_imn = -ab_im
    ge_re, ge_im = _s5_scan_bwd(dy_ssm, u, h_re, h_im, bd_re, bd_im, cd_re, cd_im, ab_re, ab_imn, zero_seg, zero_seg,
                                d_row, False, "s5_adj_ends")
    gi_re, gi_im = _s5_seg_fix(ge_re, ge_im, ab_re, ab_imn, seg_len, True, "s5_adj_fix")
    duz, dbd_re, dbd_im, dcd_re, dcd_im, dab_re, dab_im, dd = _s5_scan_bwd(
        dy_ssm, u, h_re, h_im, bd_re, bd_im, cd_re, cd_im, ab_re, ab_imn, gi_re, gi_im, d_row, True, "s5_adj", duz=duz)
    d_w_in = _mm(xn0, duz, "tn", BF16, "s5_in_dw", col_slots=True)
    tok = comm.send_grads(dict(s5_w_in=d_w_in), "exchange_s5_in")
    dxn0 = _mm(duz, w_in, "nt", F32, "s5_in_dx", after=tok, b_slots=True)
    grad_x, dg_pre0 = _norm_bwd1(dh1, x, dxn0, g_pre0, "norm_pre0_bwd")

    dbb_re = _block_diag_in_extract(dbd_re, p, S5_GROUP).reshape(g * p, S5_GROUP)
    dbb_im = _block_diag_in_extract(dbd_im, p, S5_GROUP).reshape(g * p, S5_GROUP)
    dcr_col, dci_col, db_re, db_im = _s5_bbar_bwd(cr_col, ci_col, b_re2, b_im2, dbb_re, dbb_im)
    da_re, da_im, dldt = _s5_disc_bwd(a_re, a_im, ldt, dab_re.reshape(g, p), dab_im.reshape(g, p),
                                      dcr_col.reshape(g, p), dci_col.reshape(g, p))
    dc_re = _block_diag_out_extract(dcd_re, S5_GROUP, p)
    dc_im = -_block_diag_out_extract(dcd_im, S5_GROUP, p)

    small = dict(
        norm_pre=jnp.concatenate([dg_pre0, dg_pre1], axis=0), norm_post=jnp.concatenate([dg_post0, dg_post1], axis=0),
        s5_a_re=da_re, s5_a_im=da_im, s5_log_dt=dldt.reshape(g), s5_b_re=db_re.reshape(g, p, S5_GROUP),
        s5_b_im=db_im.reshape(g, p, S5_GROUP), s5_c_re=dc_re, s5_c_im=dc_im, s5_d=dd.reshape(-1),
        s5_b_glu=db_glu.reshape(-1), kv_norm=dg_kv.reshape(-1), kv_b_f=db_f[0, :nh])
    return loss, grad_x, small


_BIG = ("s5_w_in", "s5_w_glu", "s5_w_out", "kv_w", "fox_w_in", "fox_w_out")
_COL_SHARDED = ("s5_w_in", "fox_w_in")
_SMALL = ("norm_pre", "norm_post", "s5_a_re", "s5_a_im", "s5_log_dt", "s5_b_re", "s5_b_im", "s5_c_re", "s5_c_im",
          "s5_d", "s5_b_glu", "kv_norm", "kv_b_f")
_SMALL_SHARDED = ("s5_d", "s5_b_glu")
_PACK_QUANTUM = SUBLANES * LANES
_WEIGHTS = ('norm_pre', 'norm_post', 's5_w_in', 's5_a_re', 's5_a_im', 's5_log_dt', 's5_b_re', 's5_b_im', 's5_c_re', 's5_c_im',
            's5_d', 's5_w_glu', 's5_b_glu', 's5_w_out', 'kv_norm', 'kv_w', 'kv_b_f', 'fox_w_in', 'fox_w_out')


def _full_from_slots(name, slots):
    n, r, c = slots.shape
    if name in _COL_SHARDED:
        return slots.transpose(1, 0, 2).reshape(r, n * c)
    return slots.reshape(n * r, c)


def _slots_from_full(name, full):
    if name in _COL_SHARDED:
        r, nc = full.shape
        return full.reshape(r, N_DEV, nc // N_DEV).transpose(1, 0, 2)
    nr, c = full.shape
    return full.reshape(N_DEV, nr // N_DEV, c)


def _groups_last(shape):
    return len(shape) >= 3 and shape[-1] < LANES and shape[-3] % LANES == 0


def _pack(vals):
    parts = []
    for v in vals:
        flat = jnp.moveaxis(v, -3, -1).reshape(-1) if _groups_last(v.shape) else v.reshape(-1)
        parts.append(jnp.pad(flat, (0, (-flat.shape[0]) % _PACK_QUANTUM)))
    total = sum(p.shape[0] for p in parts)
    parts.append(jnp.zeros(((-total) % (N_DEV * _PACK_QUANTUM),), F32))
    return jnp.concatenate(parts).reshape(-1, LANES)


def _unpack(packed, shapes):
    flat = packed.reshape(-1)
    out, off = [], 0
    for sh in shapes:
        n = math.prod(sh)
        piece = flat[off:off + n]
        if _groups_last(sh):
            piece = jnp.moveaxis(piece.reshape(sh[:-3] + sh[-2:] + sh[-3:-2]), -1, -3)
        out.append(piece.reshape(sh))
        off += n + (-n) % _PACK_QUANTUM
    return out


class _Comm:
    _GROUPS = (("s5_w_in",) + _SMALL_SHARDED, ("s5_w_glu", "s5_w_out"), ("kv_w", "fox_w_in"), ("fox_w_out",))
    _SLOT_FORM = ("s5_w_in", "fox_w_in")

    def __init__(self, shards, vectors, early=()):
        self._shards = {**shards, **vectors}
        self._full, self._gathers = {}, {}
        self._early = list(early)
        self.token = jnp.zeros((), F32)
        for group in self._GROUPS[:-1]:
            self.token = self.token + self._start(group, ())[0, 0]
        self.late_token = None
        self._sent = []

    def _start(self, group, after):
        state, tok = _exchange_start([self._shards[n] for n in group], False, "gather_start_" + group[0], after,
                                     peers=_CHIP_PEERS)
        self._gathers[group] = state
        return tok

    def vector(self, name):
        return self._full[name]

    def weight(self, name, after):
        if name not in self._full:
            group = next(g for g in self._GROUPS if name in g)
            if group == self._GROUPS[0]:
                after = (list(after) if isinstance(after, (list, tuple)) else [after]) + self._early
            slots = _exchange_wait(self._gathers.pop(group), after, "gather_wait_" + group[0])
            slots = _forward_to_sibling(slots, "gather_forward_" + group[0])
            for n, sl in zip(group, slots):
                if n in _SMALL_SHARDED:
                    self._full[n] = sl.reshape(-1)
                else:
                    self._full[n] = sl if n in self._SLOT_FORM else _full_from_slots(n, sl)
            if group == self._GROUPS[-2]:
                self.late_token = self._start(self._GROUPS[-1], [slots[0]])[0, 0]
        return self._full[name]

    def send_grads(self, grads, name):
        names = list(grads)
        slots = [grads[n] if grads[n].ndim == 3 else _slots_from_full(n, grads[n]).astype(BF16) for n in names]
        state, tok = _exchange_start(slots, True, name + "_start")
        self._sent.append((names, state, name + "_wait"))
        return tok

    def received_grads(self, after):
        for names, state, name in self._sent:
            for n, recv in zip(names, _exchange_wait(state, after, name)):
                yield n, recv


def kernel(x, norm_pre, norm_post, s5_w_in, s5_a_re, s5_a_im, s5_log_dt, s5_b_re, s5_b_im, s5_c_re, s5_c_im, s5_d, s5_w_glu, s5_b_glu, s5_w_out, kv_norm, kv_w, kv_b_f, fox_w_in, fox_w_out, loss_target, m_norm_pre, m_norm_post, m_s5_w_in, m_s5_a_re, m_s5_a_im, m_s5_log_dt, m_s5_b_re, m_s5_b_im, m_s5_c_re, m_s5_c_im, m_s5_d, m_s5_w_glu, m_s5_b_glu, m_s5_w_out, m_kv_norm, m_kv_w, m_kv_b_f, m_fox_w_in, m_fox_w_out, v_norm_pre, v_norm_post, v_s5_w_in, v_s5_a_re, v_s5_a_im, v_s5_log_dt, v_s5_b_re, v_s5_b_im, v_s5_c_re, v_s5_c_im, v_s5_d, v_s5_w_glu, v_s5_b_glu, v_s5_w_out, v_kv_norm, v_kv_w, v_kv_b_f, v_fox_w_in, v_fox_w_out):
    env = dict(locals())
    wts = {n: env[n] for n in _WEIGHTS}
    mom = {n: env["m_" + n] for n in _WEIGHTS}
    var = {n: env["v_" + n] for n in _WEIGHTS}
    me = 4 * lax.axis_index("x") + 2 * lax.axis_index("y") + lax.axis_index("c")
    shard2d = {n: (wts[n].T if n == "kv_w" else wts[n].reshape(wts[n].shape[-2:])) for n in _BIG}
    full_shape = {n: ((wts[n].size * N_DEV,) if n in _SMALL_SHARDED else wts[n].shape) for n in _SMALL}

    def spread(n, v):
        if n not in _SMALL_SHARDED:
            return v
        flat = v.reshape(-1)
        return lax.dynamic_update_slice(jnp.zeros(full_shape[n], F32), flat, (me * flat.shape[0],))

    packed = [_pack([spread(n, src[n]) for n in _SMALL] + [jnp.zeros((1,), F32)]) for src in (wts, mom, var)]
    comm = _Comm({n: _cast_bf16(shard2d[n], "cast_" + n) for n in _BIG}, {n: wts[n].reshape(1, -1) for n in _SMALL_SHARDED}, packed)

    loss_local, grad_x, small = _local_step(
        x[0], loss_target[0], norm_pre, norm_post, kv_norm, kv_b_f, s5_a_re[0], s5_a_im[0], s5_log_dt[0],
        s5_b_re[0], s5_b_im[0], s5_c_re[0], s5_c_im[0], comm)

    small_pack = _pack([small[n] for n in _SMALL] + [loss_local.reshape(1)])
    slice_rows = small_pack.shape[0] // N_DEV
    small_state, small_tok = _exchange_start([small_pack.reshape(N_DEV, slice_rows, LANES)], True, "reduce_small_start")

    res = {}
    for n, recv in comm.received_grads([small_tok, grad_x]):
        if n == "kv_w":
            res[n] = [o.T for o in _adamw(recv, wts[n].T, mom[n].T, var[n].T, "adamw_" + n)]
        else:
            res[n] = _adamw(recv, wts[n], mom[n], var[n], "adamw_" + n)

    my_sum = _sum_parts(_exchange_wait(small_state, res[_BIG[0]][0], "reduce_small_wait")[0], "sum_small")
    g_all = _exchange([my_sum], False, "gather_small")[0].reshape(1, small_pack.shape[0], LANES)
    outs = _adamw(g_all, *packed, "adamw_small")
    unpacked = [_unpack(o, [full_shape[n] for n in _SMALL] + [(1,)]) for o in outs]
    loss = unpacked[0][-1][0]
    for i, n in enumerate(_SMALL):
        vals = [u[i] for u in unpacked]
        if n in _SMALL_SHARDED:
            k = wts[n].size
            vals = [lax.dynamic_slice(v, (me * k,), (k,)) for v in vals]
        res[n] = [v.reshape(wts[n].shape) for v in vals]

    return (loss, grad_x[None], *[res[n][0] for n in _WEIGHTS], *[res[n][1] for n in _WEIGHTS],
            *[res[n][2] for n in _WEIGHTS], *[res[n][3] for n in _WEIGHTS])
```

```python
import math

import jax
import jax.numpy as jnp
from jax import lax
from jax.experimental import pallas as pl
from jax.experimental.pallas import tpu as pltpu

F32 = jnp.float32
BF16 = jnp.bfloat16

N_DEV = 8
MESH_AXES = ("x", "y", "c")
S5_GROUP = 16
S5_STATE = 64
LANES = 128
SUBLANES = 8
GROUPS_PER_BLOCK = LANES // S5_GROUP
BLOCK_STATE = GROUPS_PER_BLOCK * S5_STATE
N_SEG = SUBLANES
HEAD_DIM = 128
RMS_EPS = 1e-6
NEG_INF = -1e30
LOG2E = math.log2(math.e)
ADAM_LR = 0.001
ADAM_B1 = 0.9
ADAM_B2 = 0.999
ADAM_EPS = 1e-08
ADAM_WD = 0.01
ADAM_STEP = 10
VMEM_LIMIT = 56 * 1024 * 1024


def _tile(n, pref, quantum=LANES):
    if n <= pref:
        return n
    t = (pref // quantum) * quantum
    while t >= quantum:
        if n % t == 0:
            return t
        t -= quantum
    return n


def _cparams(*sem):
    return pltpu.CompilerParams(dimension_semantics=sem if sem else None, vmem_limit_bytes=VMEM_LIMIT)


_DOT_DIMS = {"nn": ((1,), (0,)), "nt": ((1,), (1,)), "tn": ((0,), (0,))}


def _mm(a, b, mode, out_dtype, name, add=None, scale=None, b_cols=None, after=None, col_slots=False, b_slots=False,
        b_rows=None, epilogue=None):
    slot_w = b.shape[2] if b_slots else None
    b2d = (b.shape[1], b.shape[0] * b.shape[2]) if b_slots else b.shape
    b_shape = b2d if b_cols is None else (b2d[0], b_cols[1])
    if b_rows is not None:
        b_shape = (b_rows, b_shape[1])
    if mode == "nn":
        (M, K), (K2, N) = a.shape, b_shape
    elif mode == "nt":
        (M, K), (N, K2) = a.shape, b_shape
    else:
        (K, M), (K2, N) = a.shape, b_shape
    assert K == K2, (name, a.shape, b_shape)
    tm, tn, tk = _tile(M, 1024 if K <= 2048 else 512), (N // N_DEV if col_slots else _tile(N, 1024)), _tile(K, 4096)
    if b_slots and mode == "nn":
        tn = slot_w
    nk = K // tk
    dims = (_DOT_DIMS[mode], ((), ()))
    col0 = 0
    if b_cols is not None:
        assert mode != "tn" and b_cols[0] % (tn if mode == "nn" else tk) == 0
        col0 = b_cols[0] // (tn if mode == "nn" else tk)
    assert not b_slots or (mode == "nn" or (mode == "nt" and nk == 1 and b_cols is None))

    def body(*refs):
        a_ref, b_ref = refs[:2]
        c_ref = refs[2] if add is not None else None
        e_ref = refs[2 + (add is not None)] if epilogue is not None else None
        o_ref = refs[2 + (add is not None) + (epilogue is not None) + (after is not None)]
        if b_slots and mode == "nt":
            part = lax.dot_general(a_ref[:, :slot_w], b_ref[0], dims, preferred_element_type=F32)
            for sl in range(1, b_ref.shape[0]):
                part += lax.dot_general(a_ref[:, sl * slot_w:(sl + 1) * slot_w], b_ref[sl], dims, preferred_element_type=F32)
        else:
            part = lax.dot_general(a_ref[...], b_ref[...], dims, preferred_element_type=F32)

        def finish(r):
            if scale is not None:
                r = r * scale
            if add is not None:
                r = r + c_ref[...]
            if epilogue is not None:
                r = epilogue[0](r, e_ref[...])
            o_ref[...] = r.astype(out_dtype)

        if nk == 1:
            finish(part)
            return
        acc = refs[-1]
        k = pl.program_id(2)

        @pl.when(k == 0)
        def _():
            acc[...] = part

        @pl.when(jnp.logical_and(k > 0, k < nk - 1))
        def _():
            acc[...] += part

        @pl.when(k == nk - 1)
        def _():
            finish(acc[...] + part)

    if mode == "tn":
        a_spec = pl.BlockSpec((tk, tm), lambda i, j, k: (k, i))
    else:
        a_spec = pl.BlockSpec((tm, tk), lambda i, j, k: (i, k))
    if b_slots and mode == "nn":
        b_spec = pl.BlockSpec((None, tk, tn), lambda i, j, k: (j + col0, k, 0))
    elif b_slots:
        b_spec = pl.BlockSpec((b.shape[0], tn, slot_w), lambda i, j, k: (0, j, 0))
    elif mode == "nt":
        b_spec = pl.BlockSpec((tn, tk), lambda i, j, k: (j, k + col0))
    else:
        b_spec = pl.BlockSpec((tk, tn), lambda i, j, k: (k, j + col0))
    o_spec = pl.BlockSpec((tm, tn), lambda i, j, k: (i, j))
    in_specs = [a_spec, b_spec] + ([o_spec] if add is not None else [])
    args = (a, b) + ((add,) if add is not None else ())
    if epilogue is not None:
        in_specs.append(o_spec)
        args += (epilogue[1],)
    if after is not None:
        in_specs.append(pl.BlockSpec(after.shape, lambda i, j, k: (0, 0)))
        args += (after,)
    out_shape = jax.ShapeDtypeStruct((M, N), out_dtype)
    if col_slots:
        assert add is None
        o_spec = pl.BlockSpec((None, tm, tn), lambda i, j, k: (j, i, 0))
        out_shape = jax.ShapeDtypeStruct((N_DEV, M, tn), out_dtype)
    return pl.pallas_call(
        body, name=name, grid=(M // tm, N // tn, nk),
        in_specs=in_specs, out_specs=o_spec,
        out_shape=out_shape,
        scratch_shapes=[pltpu.VMEM((tm, tn), F32)] if nk > 1 else [],
        compiler_params=_cparams("parallel", "parallel", "arbitrary"),
    )(*args)


class _NatIn:
    def __init__(self, ref):
        self.ref = ref

    def __getitem__(self, idx):
        v = jnp.swapaxes(self.ref[...], 0, 1)
        return v.reshape(v.shape[0] * N_SEG, v.shape[2])


class _NatOut:
    def __init__(self, ref):
        self.ref = ref

    def __setitem__(self, idx, val):
        self.ref[...] = jnp.swapaxes(val.reshape(val.shape[0] // N_SEG, N_SEG, val.shape[1]), 0, 1)


def _rowcall(body, name, n_rows, ins, outs, tile_rows=256):
    tr = _tile(n_rows, tile_rows, SUBLANES * 2)
    n_in = len(ins)
    in_kinds = [k for _, k in ins]
    kinds = [k for _, _, k in outs]

    def kern(*refs):
        @pl.when(pl.program_id(0) == 0)
        def _():
            for r, kind in zip(refs[n_in:], kinds):
                if kind == "acc":
                    r[...] = jnp.zeros_like(r)

        wrapped = [_NatIn(r) if k == "nat" else r for r, k in zip(refs[:n_in], in_kinds)]
        wrapped += [_NatOut(r) if k == "nat" else r for r, k in zip(refs[n_in:], kinds)]
        body(*wrapped)

    in_specs, args = [], []
    for arr, kind in ins:
        if kind == "row":
            in_specs.append(pl.BlockSpec((tr, arr.shape[1]), lambda i: (i, 0)))
        elif kind == "nat":
            in_specs.append(pl.BlockSpec((N_SEG, tr // N_SEG, arr.shape[1]), lambda i: (0, i, 0)))
            arr = arr.reshape(N_SEG, n_rows // N_SEG, arr.shape[1])
        else:
            in_specs.append(pl.BlockSpec(arr.shape, lambda i, nd=arr.ndim: (0,) * nd))
        args.append(arr)
    out_specs, out_shape = [], []
    for width, dtype, kind in outs:
        if kind == "row":
            out_specs.append(pl.BlockSpec((tr, width), lambda i: (i, 0)))
            out_shape.append(jax.ShapeDtypeStruct((n_rows, width), dtype))
        elif kind == "right":
            out_specs.append(pl.BlockSpec((tr, width), lambda i: (i, 1)))
            out_shape.append(jax.ShapeDtypeStruct((n_rows, 2 * width), dtype))
        elif kind == "nat":
            out_specs.append(pl.BlockSpec((N_SEG, tr // N_SEG, width), lambda i: (0, i, 0)))
            out_shape.append(jax.ShapeDtypeStruct((N_SEG, n_rows // N_SEG, width), dtype))
        else:
            out_specs.append(pl.BlockSpec((1, width), lambda i: (0, 0)))
            out_shape.append(jax.ShapeDtypeStruct((1, width), F32))
    res = pl.pallas_call(
        kern, name=name, grid=(n_rows // tr,), in_specs=in_specs, out_specs=out_specs, out_shape=out_shape,
        compiler_params=_cparams("arbitrary"),
    )(*args)
    return [r.reshape(n_rows, r.shape[2]) if k == "nat" else r for r, k in zip(res, kinds)]


def _rstd(x):
    return lax.rsqrt(jnp.mean(x * x, axis=-1, keepdims=True) + RMS_EPS)


def _rms_bwd(x, g, dy):
    xh = x * _rstd(x)
    dxh = dy * g
    dx = _rstd(x) * (dxh - xh * jnp.mean(dxh * xh, axis=-1, keepdims=True))
    return dx, jnp.sum(dy * xh, axis=0, keepdims=True)


def _silu(z):
    return z * jax.nn.sigmoid(z)


def _norm_cast(x, g, name, x_kind="row"):
    def body(x_ref, g_ref, o_ref):
        x = x_ref[...]
        o_ref[...] = (x * _rstd(x) * g_ref[...]).astype(BF16)

    return _rowcall(body, name, x.shape[0], [(x, x_kind), (g, "full")], [(x.shape[1], BF16, "row")])[0]


def _resid_norm2(x, o, g_post, g_kv, g_pre, name):
    def body(x_ref, o_ref, go_ref, gk_ref, gp_ref, h_ref, nk_ref, np_ref):
        o = o_ref[...]
        h = x_ref[...] + o * _rstd(o) * go_ref[...]
        h_ref[...] = h
        hn = h * _rstd(h)
        nk_ref[...] = (hn * gk_ref[...]).astype(BF16)
        np_ref[...] = (hn * gp_ref[...]).astype(BF16)

    d = x.shape[1]
    return _rowcall(body, name, x.shape[0], [(x, "nat"), (o, "row"), (g_post, "full"), (g_kv, "full"), (g_pre, "full")],
                    [(d, F32, "nat"), (d, BF16, "nat"), (d, BF16, "nat")])


def _post_norm_loss(o, g, h1, target, name):
    d = o.shape[1]

    def body(o_ref, g_ref, h_ref, t_ref, dh_ref, do_ref, acc_ref, dg_ref):
        o = o_ref[...]
        e = h_ref[...] + o * _rstd(o) * g_ref[...] - t_ref[...]
        dh = e * (1.0 / d)
        dh_ref[...] = dh
        acc_ref[...] += jnp.sum(e * e, axis=0, keepdims=True)
        dx, dg = _rms_bwd(o, g_ref[...], dh)
        do_ref[...] = dx.astype(BF16)
        dg_ref[...] += dg

    return _rowcall(body, name, o.shape[0], [(o, "row"), (g, "full"), (h1, "row"), (target, "row")],
                    [(d, F32, "row"), (d, BF16, "row"), (d, F32, "acc"), (d, F32, "acc")])


def _post_norm_bwd(dy, o, g, name, dy_kind="row"):
    def body(dy_ref, o_ref, g_ref, do_ref, dg_ref):
        dx, dg = _rms_bwd(o_ref[...], g_ref[...], dy_ref[...])
        do_ref[...] = dx.astype(BF16)
        dg_ref[...] += dg

    d = o.shape[1]
    return _rowcall(body, name, o.shape[0], [(dy, dy_kind), (o, "row"), (g, "full")], [(d, BF16, "row"), (d, F32, "acc")])


def _gate_bwd(d_oz, o, z, name):
    def body(d_ref, o_ref, z_ref, do_ref, dz_ref):
        _, vjp = jax.vjp(lambda o, z: o * _silu(z), o_ref[...], z_ref[...].astype(F32))
        do, dz = vjp(d_ref[...].astype(F32))
        do_ref[...] = do.astype(BF16)
        dz_ref[...] = dz.astype(BF16)

    w = o.shape[1]
    return _rowcall(body, name, o.shape[0], [(d_oz, "row"), (o, "row"), (z, "row")], [(w, BF16, "row"), (w, BF16, "right")])


def _norm_bwd2(dh2, h1, dxn1, dhn_kv, g_pre, g_kv, name):
    def body(dh2_ref, h_ref, d1_ref, dk_ref, gp_ref, gk_ref, dh1_ref, dgp_ref, dgk_ref):
        h = h_ref[...]
        dx1, dg1 = _rms_bwd(h, gp_ref[...], d1_ref[...].astype(F32))
        dxk, dgk = _rms_bwd(h, gk_ref[...], dk_ref[...].astype(F32))
        dh1_ref[...] = dh2_ref[...] + dx1 + dxk
        dgp_ref[...] += dg1
        dgk_ref[...] += dgk

    d = h1.shape[1]
    return _rowcall(body, name, h1.shape[0],
                    [(dh2, "row"), (h1, "row"), (dxn1, "row"), (dhn_kv, "row"), (g_pre, "full"), (g_kv, "full")],
                    [(d, F32, "row"), (d, F32, "acc"), (d, F32, "acc")])


def _norm_bwd1(dres, x, dxn, g, name):
    def body(dr_ref, x_ref, dn_ref, g_ref, dx_ref, dg_ref):
        dx, dg = _rms_bwd(x_ref[...], g_ref[...], dn_ref[...].astype(F32))
        dx_ref[...] = dr_ref[...] + dx
        dg_ref[...] += dg

    d = x.shape[1]
    return _rowcall(body, name, x.shape[0], [(dres, "nat"), (x, "nat"), (dxn, "row"), (g, "full")],
                    [(d, F32, "nat"), (d, F32, "acc")])


def _s5_gate(y_ssm, gp, b_glu, z, name):
    def body(y_ref, gp_ref, b_ref, z_ref, o_ref):
        yg = jax.nn.gelu(y_ref[...])
        o_ref[...] = (yg * jax.nn.sigmoid(gp_ref[...] + b_ref[...]) * _silu(z_ref[...].astype(F32))).astype(BF16)

    return _rowcall(body, name, y_ssm.shape[0], [(y_ssm, "row"), (gp, "row"), (b_glu, "full"), (z, "row")],
                    [(y_ssm.shape[1], BF16, "row")])[0]


def _s5_gate_bwd(dy3, y_ssm, gp, b_glu, z, name):
    def body(d_ref, y_ref, gp_ref, b_ref, z_ref, dz_ref, dgp_ref, dyg_ref, db_ref):
        yg = jax.nn.gelu(y_ref[...])
        _, vjp = jax.vjp(lambda yg, gp, z: yg * jax.nn.sigmoid(gp) * _silu(z), yg, gp_ref[...] + b_ref[...],
                         z_ref[...].astype(F32))
        dyg, dgp, dz = vjp(d_ref[...].astype(F32))
        dz_ref[...] = dz.astype(BF16)
        dgp_ref[...] = dgp.astype(BF16)
        dyg_ref[...] = dyg
        db_ref[...] += jnp.sum(dgp, axis=0, keepdims=True)

    w = y_ssm.shape[1]
    return _rowcall(body, name, y_ssm.shape[0],
                    [(dy3, "row"), (y_ssm, "row"), (gp, "row"), (b_glu, "full"), (z, "row")],
                    [(w, BF16, "right"), (w, BF16, "row"), (w, F32, "row"), (w, F32, "acc")])


def _cast_bf16(x, name):
    r, c = x.shape
    by_cols = r % (2 * SUBLANES) != 0
    tr, tc = (r, _tile(c, 256)) if by_cols else (_tile(r, 512, 2 * SUBLANES), c)
    pos = (lambda i: (0, i)) if by_cols else (lambda i: (i, 0))

    def body(x_ref, o_ref):
        o_ref[...] = x_ref[...].astype(BF16)

    return pl.pallas_call(
        body, name=name, grid=(c // tc if by_cols else r // tr,),
        in_specs=[pl.BlockSpec((tr, tc), pos)], out_specs=pl.BlockSpec((tr, tc), pos),
        out_shape=jax.ShapeDtypeStruct((r, c), BF16), compiler_params=_cparams("parallel"),
    )(x)


def _concat_cast(a, b, name):
    def body(a_ref, b_ref, o_ref):
        w = a_ref.shape[1]
        o_ref[:, :w] = a_ref[...].astype(BF16)
        o_ref[:, w:] = b_ref[...].astype(BF16)

    return _rowcall(body, name, a.shape[0], [(a, "row"), (b, "row")], [(a.shape[1] + b.shape[1], BF16, "row")])[0]


def _disc(ar, ai, ldt):
    dt = jnp.exp(ldt)
    mag = jnp.exp(ar * dt)
    abr = mag * jnp.cos(ai * dt)
    abi = mag * jnp.sin(ai * dt)
    den = ar * ar + ai * ai
    nr = abr - 1.0
    return abr, abi, (nr * ar + abi * ai) / den, (abi * ar - nr * ai) / den


def _s5_disc_fwd(a_re, a_im, ldt):
    def body(ar, ai, ld, o1, o2, o3, o4):
        o1[...], o2[...], o3[...], o4[...] = _disc(ar[...], ai[...], ld[...])

    sh = jax.ShapeDtypeStruct(a_re.shape, F32)
    return pl.pallas_call(body, name="s5_disc_fwd", out_shape=(sh, sh, sh, sh))(a_re, a_im, ldt)


def _s5_disc_bwd(a_re, a_im, ldt, d_abr, d_abi, d_cr, d_ci):
    def body(ar, ai, ld, g1, g2, g3, g4, o1, o2, o3):
        _, vjp = jax.vjp(_disc, ar[...], ai[...], ld[...])
        o1[...], o2[...], o3[...] = vjp((g1[...], g2[...], g3[...], g4[...]))

    sh = jax.ShapeDtypeStruct(a_re.shape, F32)
    return pl.pallas_call(body, name="s5_disc_bwd", out_shape=(sh, sh, jax.ShapeDtypeStruct(ldt.shape, F32)))(
        a_re, a_im, ldt, d_abr, d_abi, d_cr, d_ci)


def _bbar(cr, ci, br, bi):
    return cr * br - ci * bi, cr * bi + ci * br


def _s5_bbar_fwd(cr_col, ci_col, b_re, b_im):
    def body(cr, ci, br, bi, o1, o2):
        o1[...], o2[...] = _bbar(cr[...], ci[...], br[...], bi[...])

    w = b_re.shape[1]
    return _rowcall(body, "s5_bbar_fwd", b_re.shape[0], [(cr_col, "row"), (ci_col, "row"), (b_re, "row"), (b_im, "row")],
                    [(w, F32, "row"), (w, F32, "row")], tile_rows=1024)


def _s5_bbar_bwd(cr_col, ci_col, b_re, b_im, d_re, d_im):
    def body(cr, ci, br, bi, g1, g2, o1, o2, o3, o4):
        _, vjp = jax.vjp(_bbar, cr[...], ci[...], br[...], bi[...])
        o1[...], o2[...], o3[...], o4[...] = vjp((g1[...], g2[...]))

    w = b_re.shape[1]
    return _rowcall(body, "s5_bbar_bwd", b_re.shape[0],
                    [(cr_col, "row"), (ci_col, "row"), (b_re, "row"), (b_im, "row"), (d_re, "row"), (d_im, "row")],
                    [(1, F32, "row"), (1, F32, "row"), (w, F32, "row"), (w, F32, "row")], tile_rows=1024)


def _block_diag_in(t):
    g, p, c = t.shape
    nb = g // GROUPS_PER_BLOCK
    t4 = t.reshape(nb, GROUPS_PER_BLOCK, p, c).transpose(0, 1, 3, 2)
    eye = jnp.eye(GROUPS_PER_BLOCK, dtype=t.dtype)
    return (t4[:, :, :, None, :] * eye[None, :, None, :, None]).reshape(nb, GROUPS_PER_BLOCK * c, GROUPS_PER_BLOCK * p)


def _block_diag_in_extract(d, p, c):
    nb = d.shape[0]
    d5 = d.reshape(nb, GROUPS_PER_BLOCK, c, GROUPS_PER_BLOCK, p)
    diag = jnp.stack([d5[:, g, :, g, :] for g in range(GROUPS_PER_BLOCK)], axis=1)
    return diag.transpose(0, 1, 3, 2).reshape(nb * GROUPS_PER_BLOCK, p, c)


def _block_diag_out(t):
    g, c, p = t.shape
    nb = g // GROUPS_PER_BLOCK
    t4 = t.reshape(nb, GROUPS_PER_BLOCK, c, p).transpose(0, 1, 3, 2)
    eye = jnp.eye(GROUPS_PER_BLOCK, dtype=t.dtype)
    return (t4[:, :, :, None, :] * eye[None, :, None, :, None]).reshape(nb, GROUPS_PER_BLOCK * p, GROUPS_PER_BLOCK * c)


def _block_diag_out_extract(d, c, p):
    nb = d.shape[0]
    d5 = d.reshape(nb, GROUPS_PER_BLOCK, p, GROUPS_PER_BLOCK, c)
    diag = jnp.stack([d5[:, g, :, g, :] for g in range(GROUPS_PER_BLOCK)], axis=1)
    return diag.transpose(0, 1, 3, 2).reshape(nb * GROUPS_PER_BLOCK, c, p)


def _scan_step(ar, ai, hr, hi, xr, xi):
    return ar * hr - ai * hi + xr, ar * hi + ai * hr + xi


def _s5_scan_fwd(u, bd_re, bd_im, cd_re, cd_im, ab_re, ab_im, init_re, init_im, d_row, full, name):
    s, w = u.shape
    nb = w // LANES
    rows = _tile(s, 512, SUBLANES)
    nc = s // rows
    steps = rows // N_SEG
    ns = nb * BLOCK_STATE

    def body(u_ref, bdr, bdi, cdr, cdi, ar_ref, ai_ref, ir_ref, ii_ref, d_ref, *outs):
        if full:
            y_ref, yg_ref, hr_ref, hi_ref, er_ref, ei_ref, cr, ci = outs
        else:
            er_ref, ei_ref, hr_ref, hi_ref, cr, ci = outs
        c = pl.program_id(1)

        @pl.when(c == 0)
        def _():
            cr[...] = ir_ref[...]
            ci[...] = ii_ref[...]

        ub = u_ref[...].astype(BF16)
        hr_ref[...] = jnp.dot(ub, bdr[...], preferred_element_type=F32)
        hi_ref[...] = jnp.dot(ub, bdi[...], preferred_element_type=F32)
        ar, ai = ar_ref[...], ai_ref[...]

        hr, hi = cr[...], ci[...]
        for j in range(steps):
            rows_j = pl.ds(j * N_SEG, N_SEG)
            hr, hi = _scan_step(ar, ai, hr, hi, hr_ref[rows_j, :], hi_ref[rows_j, :])
            hr_ref[rows_j, :] = hr
            hi_ref[rows_j, :] = hi
        cr[...] = hr
        ci[...] = hi
        if full:
            y = (jnp.dot(hr_ref[...].astype(BF16), cdr[...], preferred_element_type=F32)
                 + jnp.dot(hi_ref[...].astype(BF16), cdi[...], preferred_element_type=F32)
                 + d_ref[...] * u_ref[...])
            y_ref[...] = y
            yg_ref[...] = jax.nn.gelu(y).astype(BF16)

        @pl.when(c == nc - 1)
        def _():
            er_ref[...] = hr
            ei_ref[...] = hi

    blk3 = lambda a: pl.BlockSpec((None,) + a.shape[1:], lambda k, c: (k, 0, 0))
    seg = pl.BlockSpec((N_SEG, BLOCK_STATE), lambda k, c: (0, k))
    st = pl.BlockSpec((rows, BLOCK_STATE), lambda k, c: (c, k))
    in_specs = [pl.BlockSpec((rows, LANES), lambda k, c: (c, k)), blk3(bd_re), blk3(bd_im), blk3(cd_re), blk3(cd_im),
                seg, seg, seg, seg, pl.BlockSpec((1, LANES), lambda k, c: (0, k))]
    seg_shape = jax.ShapeDtypeStruct((N_SEG, ns), F32)
    st_shape = jax.ShapeDtypeStruct((s, ns), F32)
    carry = [pltpu.VMEM((N_SEG, BLOCK_STATE), F32)] * 2
    if full:
        ych = pl.BlockSpec((rows, LANES), lambda k, c: (c, k))
        out_specs = [ych, ych, st, st, seg, seg]
        out_shape = [jax.ShapeDtypeStruct((s, w), F32), jax.ShapeDtypeStruct((s, w), BF16), st_shape, st_shape, seg_shape, seg_shape]
        scratch = carry
    else:
        out_specs = [seg, seg]
        out_shape = [seg_shape, seg_shape]
        scratch = [pltpu.VMEM((rows, BLOCK_STATE), F32)] * 2 + carry
    return pl.pallas_call(
        body, name=name, grid=(nb, nc), in_specs=in_specs, out_specs=out_specs, out_shape=out_shape,
        scratch_shapes=scratch, compiler_params=_cparams("parallel", "arbitrary"),
    )(u, bd_re, bd_im, cd_re, cd_im, ab_re, ab_im, init_re, init_im, d_row)


def _s5_seg_fix(e_re, e_im, ab_re, ab_im, seg_len, reverse, name):
    assert seg_len & (seg_len - 1) == 0

    def body(er, ei, ar, ai, o_re, o_im):
        pr, pi = ar[0:1, :], ai[0:1, :]
        for _ in range(int(math.log2(seg_len))):
            pr, pi = pr * pr - pi * pi, 2.0 * pr * pi
        tr = jnp.zeros_like(pr)
        ti = jnp.zeros_like(pr)
        order = list(range(N_SEG - 1, -1, -1)) if reverse else list(range(N_SEG))
        for n, sgm in enumerate(order):
            o_re[sgm:sgm + 1, :] = tr
            o_im[sgm:sgm + 1, :] = ti
            if n < N_SEG - 1:
                tr, ti = _scan_step(pr, pi, tr, ti, er[sgm:sgm + 1, :], ei[sgm:sgm + 1, :])

    sh = jax.ShapeDtypeStruct(e_re.shape, F32)
    return pl.pallas_call(body, name=name, out_shape=(sh, sh))(e_re, e_im, ab_re, ab_im)


def _s5_scan_bwd(dy, u, h_re, h_im, bd_re, bd_im, cd_re, cd_im, ab_re, ab_imn, gin_re, gin_im, d_row, full, name, duz=None):
    s, w = u.shape
    nb = w // LANES
    rows = _tile(s, 512, SUBLANES)
    nc = s // rows
    steps = rows // N_SEG
    ns = nb * BLOCK_STATE

    def body(dy_ref, u_ref, hr_ref, hi_ref, bdr, bdi, cdr, cdi, ar_ref, ai_ref, ir_ref, ii_ref, d_ref, *outs):
        if full:
            _, du_ref, dbr_ref, dbi_ref, dcr_ref, dci_ref, dar_ref, dai_ref, dd_ref, gr, gi, accr, acci = outs
        else:
            er_ref, ei_ref, gr, gi = outs
        c = pl.program_id(1)

        @pl.when(c == 0)
        def _():
            gr[pl.ds(rows, N_SEG), :] = ir_ref[...]
            gi[pl.ds(rows, N_SEG), :] = ii_ref[...]
            if full:
                for r in (dbr_ref, dbi_ref, dcr_ref, dci_ref, dd_ref, accr, acci):
                    r[...] = jnp.zeros_like(r)

        dyb = dy_ref[...].astype(BF16)
        nt = (_DOT_DIMS["nt"], ((), ()))
        tn = (_DOT_DIMS["tn"], ((), ()))
        gr[pl.ds(0, rows), :] = lax.dot_general(dyb, cdr[...], nt, preferred_element_type=F32)
        gi[pl.ds(0, rows), :] = lax.dot_general(dyb, cdi[...], nt, preferred_element_type=F32)
        ar, ai = ar_ref[...], ai_ref[...]

        g0r, g0i = gr[pl.ds(rows, N_SEG), :], gi[pl.ds(rows, N_SEG), :]
        for j in range(steps - 1, -1, -1):
            rows_j = pl.ds(j * N_SEG, N_SEG)
            g0r, g0i = _scan_step(ar, ai, g0r, g0i, gr[rows_j, :], gi[rows_j, :])
            gr[rows_j, :] = g0r
            gi[rows_j, :] = g0i
        if full:
            hr, hi = hr_ref[...], hi_ref[...]
            gnr, gni = gr[pl.ds(N_SEG, rows), :], gi[pl.ds(N_SEG, rows), :]
            accr[...] += jnp.sum((gnr * hr + gni * hi).reshape(steps, N_SEG, BLOCK_STATE), axis=0)
            acci[...] += jnp.sum((gni * hr - gnr * hi).reshape(steps, N_SEG, BLOCK_STATE), axis=0)
        gr[pl.ds(rows, N_SEG), :] = g0r
        gi[pl.ds(rows, N_SEG), :] = g0i
        if full:
            ub = u_ref[...].astype(BF16)
            gbr, gbi = gr[pl.ds(0, rows), :].astype(BF16), gi[pl.ds(0, rows), :].astype(BF16)
            dcr_ref[...] += lax.dot_general(hr.astype(BF16), dyb, tn, preferred_element_type=F32)
            dci_ref[...] += lax.dot_general(hi.astype(BF16), dyb, tn, preferred_element_type=F32)
            dbr_ref[...] += lax.dot_general(ub, gbr, tn, preferred_element_type=F32)
            dbi_ref[...] += lax.dot_general(ub, gbi, tn, preferred_element_type=F32)
            du_ref[...] = (lax.dot_general(gbr, bdr[...], nt, preferred_element_type=F32)
                           + lax.dot_general(gbi, bdi[...], nt, preferred_element_type=F32)
                           + d_ref[...] * dy_ref[...]).astype(BF16)
            dd_ref[...] += jnp.sum(dy_ref[...] * u_ref[...], axis=0, keepdims=True)

        @pl.when(c == nc - 1)
        def _():
            if full:
                dar_ref[...] = jnp.sum(accr[...], axis=0, keepdims=True)
                dai_ref[...] = jnp.sum(acci[...], axis=0, keepdims=True)
            else:
                er_ref[...] = g0r
                ei_ref[...] = g0i

    rev = lambda k, c: (nc - 1 - c, k)
    blk3 = lambda a: pl.BlockSpec((None,) + a.shape[1:], lambda k, c: (k, 0, 0))
    seg = pl.BlockSpec((N_SEG, BLOCK_STATE), lambda k, c: (0, k))
    st = pl.BlockSpec((rows, BLOCK_STATE), rev)
    ch = pl.BlockSpec((rows, LANES), rev)
    vec = pl.BlockSpec((1, LANES), lambda k, c: (0, k))
    if not full:
        st = pl.BlockSpec((rows, BLOCK_STATE), lambda k, c: (0, k))
    in_specs = [ch, ch if full else pl.BlockSpec((rows, LANES), lambda k, c: (0, k)), st, st,
                blk3(bd_re), blk3(bd_im), blk3(cd_re), blk3(cd_im), seg, seg, seg, seg, vec]
    args = [dy, u, h_re, h_im, bd_re, bd_im, cd_re, cd_im, ab_re, ab_imn, gin_re, gin_im, d_row]
    gbuf = [pltpu.VMEM((rows + N_SEG, BLOCK_STATE), F32)] * 2
    if full:
        row1 = pl.BlockSpec((1, BLOCK_STATE), lambda k, c: (0, k))
        out_specs = [ch, blk3(bd_re), blk3(bd_im), blk3(cd_re), blk3(cd_im), row1, row1, vec]
        out_shape = [jax.ShapeDtypeStruct(duz.shape, BF16),
                     jax.ShapeDtypeStruct(bd_re.shape, F32), jax.ShapeDtypeStruct(bd_im.shape, F32),
                     jax.ShapeDtypeStruct(cd_re.shape, F32), jax.ShapeDtypeStruct(cd_im.shape, F32),
                     jax.ShapeDtypeStruct((1, ns), F32), jax.ShapeDtypeStruct((1, ns), F32),
                     jax.ShapeDtypeStruct((1, w), F32)]
        scratch = gbuf + [pltpu.VMEM((N_SEG, BLOCK_STATE), F32)] * 2
        in_specs.append(pl.BlockSpec(memory_space=pl.ANY))
        args.append(duz)
        aliases = {len(args) - 1: 0}
    else:
        out_specs = [seg, seg]
        out_shape = [jax.ShapeDtypeStruct((N_SEG, ns), F32)] * 2
        scratch = gbuf
        aliases = {}
    return pl.pallas_call(
        body, name=name, grid=(nb, nc), in_specs=in_specs, out_specs=out_specs, out_shape=out_shape,
        input_output_aliases=aliases, scratch_shapes=scratch, compiler_params=_cparams("parallel", "arbitrary"),
    )(*args)


def _log_sigmoid(x):
    return jnp.minimum(x, 0.0) - jnp.log(1.0 + jnp.exp(-jnp.abs(x)))


def _tri(n, upper):
    r = lax.broadcasted_iota(jnp.int32, (n, n), 0)
    c = lax.broadcasted_iota(jnp.int32, (n, n), 1)
    return jnp.where((c >= r) if upper else (r >= c), 1.0, 0.0).astype(F32)


def _cum_fwd(f_logit, b_row, name):
    s, w = f_logit.shape
    t = _tile(s, 256, SUBLANES)

    def body(f_ref, b_ref, o_ref, carry):
        @pl.when(pl.program_id(0) == 0)
        def _():
            carry[...] = jnp.zeros_like(carry)

        lf = _log_sigmoid(f_ref[...] + b_ref[...])
        cum = jnp.dot(_tri(t, False), lf, precision=lax.Precision.HIGHEST, preferred_element_type=F32) + carry[...]
        o_ref[...] = cum * LOG2E
        carry[...] = cum[t - 1:t, :]

    return pl.pallas_call(
        body, name=name, grid=(s // t,),
        in_specs=[pl.BlockSpec((t, w), lambda i: (i, 0)), pl.BlockSpec((1, w), lambda i: (0, 0))],
        out_specs=pl.BlockSpec((t, w), lambda i: (i, 0)), out_shape=jax.ShapeDtypeStruct((s, w), F32),
        scratch_shapes=[pltpu.VMEM((1, w), F32)], compiler_params=_cparams("arbitrary"),
    )(f_logit, b_row)


def _cum_bwd(dcq, dck, f_logit, b_row, name):
    s, w = f_logit.shape
    t = _tile(s, 256, SUBLANES)
    nt = s // t

    def body(q_ref, k_ref, f_ref, b_ref, df_ref, db_ref, carry):
        @pl.when(pl.program_id(0) == 0)
        def _():
            carry[...] = jnp.zeros_like(carry)
            db_ref[...] = jnp.zeros_like(db_ref)

        dc = q_ref[...] - k_ref[...]
        rc = jnp.dot(_tri(t, True), dc, precision=lax.Precision.HIGHEST, preferred_element_type=F32) + carry[...]
        carry[...] = rc[0:1, :]
        df = rc * (1.0 - jax.nn.sigmoid(f_ref[...] + b_ref[...]))
        df_ref[...] = df.astype(BF16)
        db_ref[...] += jnp.sum(df, axis=0, keepdims=True)

    rev = pl.BlockSpec((t, w), lambda i: (nt - 1 - i, 0))
    one = pl.BlockSpec((1, w), lambda i: (0, 0))
    return pl.pallas_call(
        body, name=name, grid=(nt,), in_specs=[rev, rev, rev, one], out_specs=[rev, one],
        out_shape=[jax.ShapeDtypeStruct((s, w), BF16), jax.ShapeDtypeStruct((1, w), F32)],
        scratch_shapes=[pltpu.VMEM((1, w), F32)], compiler_params=_cparams("arbitrary"),
    )(dcq, dck, f_logit, b_row)


def _head_col(cum_tile, h):
    lane = lax.broadcasted_iota(jnp.int32, cum_tile.shape, 1)
    return jnp.sum(jnp.where(lane == h, cum_tile, 0.0), axis=1, keepdims=True)


def _attn_tiles(s):
    return _tile(s, 512, LANES)


def _exp2_rows(sc, sub):
    return jnp.concatenate([jnp.exp2(sc[:, b * LANES:(b + 1) * LANES] - sub) for b in range(sc.shape[1] // LANES)], axis=1)


def _row_of(rep):
    return jnp.transpose(rep)[0:1, :]


def _causal(sc, keys_on_rows):
    r = lax.broadcasted_iota(jnp.int32, sc.shape, 0)
    c = lax.broadcasted_iota(jnp.int32, sc.shape, 1)
    return jnp.where((r <= c) if keys_on_rows else (c <= r), sc, NEG_INF)


def _fox_fwd(q2, kv, cum2_t, z, name):
    s, w = q2.shape
    nh = w // HEAD_DIM
    tq = _attn_tiles(s)
    nq = s // tq
    nt = (_DOT_DIMS["nt"], ((), ()))

    def body(q_ref, k_ref, v_ref, ct_ref, z_ref, o_ref, oz_ref, lse_row_ref, m_s, acc_s, vaug, s_buf):
        i = pl.program_id(1)

        @pl.when(i == 0)
        def _():
            vaug[:, :HEAD_DIM] = v_ref[...]
            vaug[:, HEAD_DIM:] = jnp.ones((s, LANES), BF16)

        qb = q_ref[...]
        m_s[...] = jnp.full_like(m_s, NEG_INF)
        acc_s[...] = jnp.zeros_like(acc_s)

        def scores(j):
            off = pl.multiple_of(j * tq, tq)
            return lax.dot_general(qb, k_ref[pl.ds(off, tq), :], nt, preferred_element_type=F32) - ct_ref[:, pl.ds(off, tq)]

        def softmax_pv(j, sc):
            m_old = m_s[...]
            m_new = jnp.maximum(m_old, jnp.max(sc, axis=1, keepdims=True))
            p = _exp2_rows(sc, m_new)
            alpha = jnp.exp2(m_old - m_new)
            pv = jnp.dot(p.astype(BF16), vaug[pl.ds(pl.multiple_of(j * tq, tq), tq), :], preferred_element_type=F32)
            acc_s[...] = jnp.concatenate([alpha, alpha], axis=1) * acc_s[...] + pv
            m_s[...] = m_new

        s_buf[...] = scores(0)

        def loop(j, carry):
            nxt = scores(j + 1)
            softmax_pv(j, s_buf[...])
            s_buf[...] = nxt
            return carry

        lax.fori_loop(0, i, loop, 0)
        softmax_pv(i, _causal(s_buf[...], False))
        l = acc_s[:, HEAD_DIM:]
        o = acc_s[:, :HEAD_DIM] / l
        o_ref[...] = o
        oz_ref[...] = (o * _silu(z_ref[...].astype(F32))).astype(BF16)
        lse_row_ref[...] = _row_of(m_s[...] + jnp.log(l) * LOG2E)

    return pl.pallas_call(
        body, name=name, grid=(nh, nq),
        in_specs=[pl.BlockSpec((tq, HEAD_DIM), lambda h, i: (i, h)),
                  pl.BlockSpec((s, HEAD_DIM), lambda h, i: (0, h)),
                  pl.BlockSpec((s, HEAD_DIM), lambda h, i: (0, nh + h)),
                  pl.BlockSpec((None, 1, s), lambda h, i: (h, 0, 0)),
                  pl.BlockSpec((tq, HEAD_DIM), lambda h, i: (i, h))],
        out_specs=[pl.BlockSpec((tq, HEAD_DIM), lambda h, i: (i, h)),
                   pl.BlockSpec((tq, HEAD_DIM), lambda h, i: (i, h)),
                   pl.BlockSpec((None, 1, tq), lambda h, i: (h, 0, i))],
        out_shape=[jax.ShapeDtypeStruct((s, w), F32), jax.ShapeDtypeStruct((s, w), BF16),
                   jax.ShapeDtypeStruct((nh, 1, s), F32)],
        scratch_shapes=[pltpu.VMEM((tq, LANES), F32), pltpu.VMEM((tq, HEAD_DIM + LANES), F32),
                        pltpu.VMEM((s, HEAD_DIM + LANES), BF16), pltpu.VMEM((tq, tq), F32)],
        compiler_params=_cparams("arbitrary", "arbitrary"),
    )(q2, kv, kv, cum2_t, z)


def _fox_bwd(q2, kv, do, o, lse2_t, cum2, dqz, name):
    s, w = q2.shape
    nh = w // HEAD_DIM
    tk = _attn_tiles(s)
    nk = s // tk
    scale = HEAD_DIM ** -0.5
    nt = (_DOT_DIMS["nt"], ((), ()))
    tn = (_DOT_DIMS["tn"], ((), ()))

    def body(q_ref, k_ref, v_ref, do_ref, o_ref, lse_ref, c_ref, _, dk_ref, dv_ref, dq_ref, dcq_ref, dck_ref,
             dk_s, dv_s, dc_s, dq_s, dcq_s, dl_s, s_buf, dp_buf):
        h, j = pl.program_id(0), pl.program_id(1)

        @pl.when(j == 0)
        def _():
            dq_s[...] = jnp.zeros_like(dq_s)
            dcq_s[...] = jnp.zeros_like(dcq_s)
            for i in range(nk):
                rows = pl.ds(i * tk, tk)
                d = jnp.sum(do_ref[rows, :].astype(F32) * o_ref[rows, :], axis=1, keepdims=True)
                dl_s[:, i * tk:(i + 1) * tk] = _row_of(jnp.broadcast_to(d, (tk, LANES)))

        kb = k_ref[...]
        vb = v_ref[...]
        ck = jnp.broadcast_to(_head_col(c_ref[...], h), (tk, LANES))
        dk_s[...] = jnp.zeros_like(dk_s)
        dv_s[...] = jnp.zeros_like(dv_s)
        dc_s[...] = jnp.zeros_like(dc_s)

        def scores(i):
            off = pl.multiple_of(i * tk, tk)
            sc = lax.dot_general(kb, q_ref[pl.ds(off, tk), :], nt, preferred_element_type=F32) - lse_ref[:, pl.ds(off, tk)]
            dp = lax.dot_general(vb, do_ref[pl.ds(off, tk), :], nt, preferred_element_type=F32) - dl_s[:, pl.ds(off, tk)]
            return sc, dp

        def accumulate(i, sc, dp):
            off = pl.multiple_of(i * tk, tk)
            p = _exp2_rows(sc, ck)
            dv_s[...] += jnp.dot(p.astype(BF16), do_ref[pl.ds(off, tk), :], preferred_element_type=F32)
            ds = p * dp
            dsb = ds.astype(BF16)
            dk_s[...] += jnp.dot(dsb, q_ref[pl.ds(off, tk), :], preferred_element_type=F32)
            dq_s[pl.ds(off, tk), :] += lax.dot_general(dsb, kb, tn, preferred_element_type=F32)
            dcq_s[:, pl.ds(off, tk)] += jnp.sum(ds, axis=0, keepdims=True)
            part = ds[:, :LANES]
            for b in range(1, tk // LANES):
                part = part + ds[:, b * LANES:(b + 1) * LANES]
            dc_s[...] += part

        sc0, dp0 = scores(j)
        s_buf[...] = _causal(sc0, True)
        dp_buf[...] = dp0

        def loop(i, carry):
            nxt = scores(i + 1)
            accumulate(i, s_buf[...], dp_buf[...])
            s_buf[...], dp_buf[...] = nxt
            return carry

        lax.fori_loop(j, nk - 1, loop, 0)
        accumulate(nk - 1, s_buf[...], dp_buf[...])
        dk_ref[...] = (dk_s[...] * (1.0 / LOG2E)).astype(BF16)
        dv_ref[...] = dv_s[...].astype(BF16)
        dck_ref[...] = jnp.sum(jnp.transpose(dc_s[...]), axis=0, keepdims=True)

        @pl.when(j == nk - 1)
        def _():
            dq_ref[...] = (dq_s[...] * scale).astype(BF16)
            dcq_ref[...] = dcq_s[...]

    col = pl.BlockSpec((s, HEAD_DIM), lambda h, j: (0, h))
    row = pl.BlockSpec((None, 1, s), lambda h, j: (h, 0, 0))
    kspec = pl.BlockSpec((tk, HEAD_DIM), lambda h, j: (j, h))
    return pl.pallas_call(
        body, name=name, grid=(nh, nk),
        in_specs=[col, kspec, pl.BlockSpec((tk, HEAD_DIM), lambda h, j: (j, nh + h)), col, col, row,
                  pl.BlockSpec((tk, LANES), lambda h, j: (j, 0)), pl.BlockSpec(memory_space=pl.ANY)],
        out_specs=[kspec, kspec, col, row, pl.BlockSpec((None, 1, tk), lambda h, j: (h, 0, j))],
        out_shape=[jax.ShapeDtypeStruct((s, w), BF16), jax.ShapeDtypeStruct((s, w), BF16),
                   jax.ShapeDtypeStruct(dqz.shape, BF16), jax.ShapeDtypeStruct((nh, 1, s), F32),
                   jax.ShapeDtypeStruct((nh, 1, s), F32)],
        input_output_aliases={7: 2},
        scratch_shapes=[pltpu.VMEM((tk, HEAD_DIM), F32), pltpu.VMEM((tk, HEAD_DIM), F32), pltpu.VMEM((tk, LANES), F32),
                        pltpu.VMEM((s, HEAD_DIM), F32), pltpu.VMEM((1, s), F32), pltpu.VMEM((1, s), F32),
                        pltpu.VMEM((tk, tk), F32), pltpu.VMEM((tk, tk), F32)],
        compiler_params=_cparams("arbitrary", "arbitrary"),
    )(q2, kv, kv, do, o, lse2_t, cum2, dqz)


_ALL_PEERS = tuple(range(1, N_DEV))
_CHIP_PEERS = (1, 2, 4, 6)


def _exchange_copies(ins, outs, send_sems, recv_sems, local_sems, scatter, peers=_ALL_PEERS):
    x, y, c = (lax.axis_index(a) for a in MESH_AXES)
    me = 4 * x + 2 * y + c
    local, remote = [], []
    for a in range(len(ins)):
        local.append(pltpu.make_async_copy(ins[a].at[me] if scatter else ins[a], outs[a].at[me], local_sems.at[a]))
        for k in peers:
            px, py, pc = (1 - x if k & 4 else x), (1 - y if k & 2 else y), (1 - c if k & 1 else c)
            remote.append(pltpu.make_async_remote_copy(
                src_ref=ins[a].at[4 * px + 2 * py + pc] if scatter else ins[a], dst_ref=outs[a].at[me],
                send_sem=send_sems.at[a * (N_DEV - 1) + k - 1], recv_sem=recv_sems.at[a * (N_DEV - 1) + k - 1],
                device_id=(px, py, pc), device_id_type=pl.DeviceIdType.MESH))
    return local, remote


def _exchange_out_shapes(arrs, scatter):
    return [((N_DEV,) + a.shape[1:]) if scatter else ((N_DEV,) + a.shape) for a in arrs]


def _exchange(arrs, scatter, name, peers=_ALL_PEERS):
    n = len(arrs)

    def body(*refs):
        local, remote = _exchange_copies(refs[:n], refs[n:2 * n], *refs[2 * n:], scatter, peers)
        for cp in local + remote:
            cp.start()
        for cp in remote:
            cp.wait_send()
            cp.wait_recv()
        for cp in local:
            cp.wait()

    out_shape = [jax.ShapeDtypeStruct(s, a.dtype) for s, a in zip(_exchange_out_shapes(arrs, scatter), arrs)]
    return pl.pallas_call(
        body, name=name, out_shape=out_shape,
        in_specs=[pl.BlockSpec(memory_space=pl.ANY)] * n, out_specs=[pl.BlockSpec(memory_space=pl.ANY)] * n,
        scratch_shapes=[pltpu.SemaphoreType.DMA((n * (N_DEV - 1),)), pltpu.SemaphoreType.DMA((n * (N_DEV - 1),)),
                        pltpu.SemaphoreType.DMA((n,))],
    )(*arrs)


_HBM = pl.BlockSpec(memory_space=pltpu.HBM)
_SEM = pl.BlockSpec(memory_space=pltpu.SEMAPHORE)


def _exchange_start(arrs, scatter, name, after=(), peers=_ALL_PEERS):
    n = len(arrs)
    after = list(after)
    lands = [lax.empty(s, a.dtype) for s, a in zip(_exchange_out_shapes(arrs, scatter), arrs)]

    def body(*refs):
        ins, outs = refs[:n], refs[n:2 * n]
        send_sems, recv_sems, local_sems = refs[2 * n + len(after):2 * n + len(after) + 3]
        token = refs[-1]
        local, remote = _exchange_copies(ins, outs, send_sems, recv_sems, local_sems, scatter, peers)
        for cp in local + remote:
            cp.start()
        token[...] = jnp.zeros_like(token)

    hbm = lambda a: pltpu.HBM(a.shape, a.dtype)
    res = pl.pallas_call(
        body, name=name,
        out_shape=(pltpu.SemaphoreType.DMA((n * (N_DEV - 1),)), pltpu.SemaphoreType.DMA((n * (N_DEV - 1),)),
                   pltpu.SemaphoreType.DMA((n,)), *[hbm(a) for a in arrs], *[hbm(a) for a in lands],
                   jax.ShapeDtypeStruct((SUBLANES, LANES), F32)),
        in_specs=[_HBM] * (2 * n) + [pl.BlockSpec(memory_space=pl.ANY)] * len(after),
        out_specs=(_SEM, _SEM, _SEM, *[_HBM] * (2 * n), pl.BlockSpec(memory_space=pltpu.VMEM)),
        input_output_aliases={i: 3 + i for i in range(2 * n)},
        compiler_params=pltpu.CompilerParams(has_side_effects=pltpu.SideEffectType.DATAFLOW_SIDE_EFFECTING),
    )(*[pltpu.with_memory_space_constraint(a, pltpu.HBM) for a in list(arrs) + lands], *after)
    return (n, scatter, res[:3], res[3:3 + n], res[3 + n:3 + 2 * n], peers), res[-1]


def _exchange_wait(state, after, name):
    n, scatter, sems, srcs, lands, peers = state
    after = list(after) if isinstance(after, (list, tuple)) else [after]

    def body(*refs):
        ins, outs = refs[:n], refs[n:2 * n]
        send_sems, recv_sems, local_sems = refs[2 * n:2 * n + 3]
        local, remote = _exchange_copies(ins, outs, send_sems, recv_sems, local_sems, scatter, peers)
        for cp in remote:
            cp.wait_send()
            cp.wait_recv()
        for cp in local:
            cp.wait()

    hbm = lambda a: pltpu.HBM(a.shape, a.dtype)
    res = pl.pallas_call(
        body, name=name,
        out_shape=(*[hbm(a) for a in srcs], *[hbm(a) for a in lands]),
        in_specs=[_HBM] * (2 * n) + [_SEM] * 3 + [pl.BlockSpec(memory_space=pl.ANY)] * len(after),
        out_specs=tuple([_HBM] * (2 * n)),
        input_output_aliases={i: i for i in range(2 * n)},
        compiler_params=pltpu.CompilerParams(has_side_effects=pltpu.SideEffectType.DATAFLOW_SIDE_EFFECTING),
    )(*srcs, *lands, *sems, *after)
    return list(res[n:])


def _forward_to_sibling(slots, name):
    n = len(slots)
    hops = (2, 4, 6)

    def body(*refs):
        ins, outs, (send_sems, recv_sems) = refs[:n], refs[n:2 * n], refs[2 * n:]
        x, y, c = (lax.axis_index(a) for a in MESH_AXES)
        copies = []
        for a in range(n):
            for i, k in enumerate(hops):
                slot = 4 * (1 - x if k & 4 else x) + 2 * (1 - y if k & 2 else y) + c
                copies.append(pltpu.make_async_remote_copy(
                    src_ref=ins[a].at[slot], dst_ref=outs[a].at[slot],
                    send_sem=send_sems.at[a * len(hops) + i], recv_sem=recv_sems.at[a * len(hops) + i],
                    device_id=(x, y, 1 - c), device_id_type=pl.DeviceIdType.MESH))
        for cp in copies:
            cp.start()
        for cp in copies:
            cp.wait_send()
            cp.wait_recv()

    return pl.pallas_call(
        body, name=name, out_shape=[jax.ShapeDtypeStruct(s.shape, s.dtype) for s in slots],
        in_specs=[pl.BlockSpec(memory_space=pl.ANY)] * n, out_specs=[pl.BlockSpec(memory_space=pl.ANY)] * n,
        input_output_aliases={i: i for i in range(n)},
        scratch_shapes=[pltpu.SemaphoreType.DMA((n * len(hops),)), pltpu.SemaphoreType.DMA((n * len(hops),))],
    )(*slots)


def _adamw_math(w, g, m, v):
    m = ADAM_B1 * m + (1.0 - ADAM_B1) * g
    v = ADAM_B2 * v + (1.0 - ADAM_B2) * (g * g)
    m_hat = m / (1.0 - ADAM_B1 ** ADAM_STEP)
    v_hat = v / (1.0 - ADAM_B2 ** ADAM_STEP)
    return -ADAM_LR * (m_hat / (jnp.sqrt(v_hat) + ADAM_EPS) + ADAM_WD * w), m, v


def _slot_sum(p_ref):
    g = p_ref[0].astype(F32)
    for d in range(1, p_ref.shape[0]):
        g = g + p_ref[d].astype(F32)
    return g


def _adamw_tile(r, c):
    return _tile(r, max(SUBLANES, (256 * 1024) // c // SUBLANES * SUBLANES), SUBLANES)


def _adamw(parts, w, m, v, name):
    r, c = w.shape[-2:]
    by_cols = r % SUBLANES != 0
    tr, tc = (r, _tile(c, 256)) if by_cols else (_adamw_tile(r, c), c)

    def body(p_ref, w_ref, m_ref, v_ref, g_ref, d_ref, nm_ref, nv_ref):
        g = _slot_sum(p_ref)
        g_ref[...] = g
        d_ref[...], nm_ref[...], nv_ref[...] = _adamw_math(w_ref[...], g, m_ref[...], v_ref[...])

    pos = (lambda i: (0, i)) if by_cols else (lambda i: (i, 0))
    if w.ndim == 3:
        blk = pl.BlockSpec((None, tr, tc), lambda i: (0,) + pos(i))
    else:
        blk = pl.BlockSpec((tr, tc), pos)
    sh = jax.ShapeDtypeStruct(w.shape, F32)
    return pl.pallas_call(
        body, name=name, grid=(c // tc if by_cols else r // tr,),
        in_specs=[pl.BlockSpec((parts.shape[0], tr, tc), lambda i: (0,) + pos(i)), blk, blk, blk],
        out_specs=[blk] * 4, out_shape=[sh] * 4, compiler_params=_cparams("parallel"),
    )(parts, w, m, v)


def _sum_parts(parts, name):
    _, r, c = parts.shape
    tr = _adamw_tile(r, c)

    def body(p_ref, o_ref):
        o_ref[...] = _slot_sum(p_ref)

    return pl.pallas_call(
        body, name=name, grid=(r // tr,),
        in_specs=[pl.BlockSpec((parts.shape[0], tr, c), lambda i: (0, i, 0))],
        out_specs=pl.BlockSpec((tr, c), lambda i: (i, 0)), out_shape=jax.ShapeDtypeStruct((r, c), F32),
        compiler_params=_cparams("parallel"),
    )(parts)


def _lane_pad(a, width=LANES):
    return jnp.pad(a, ((0, 0), (0, width - a.shape[1])))


def _local_step(x, target, norm_pre, norm_post, kv_norm, kv_b_f, a_re, a_im, log_dt, b_re, b_im, c_re, c_im, comm):
    s, d = x.shape
    g, p = a_re.shape
    w = g * S5_GROUP
    fw = d
    nh = fw // HEAD_DIM
    seg_len = s // N_SEG
    row = lambda v: v.reshape(1, -1)
    g_pre0, g_pre1, g_post0, g_post1, g_kv = row(norm_pre[0]), row(norm_pre[1]), row(norm_post[0]), row(norm_post[1]), row(kv_norm)

    ldt = log_dt.reshape(g, 1)
    abr, abi, cr, ci = _s5_disc_fwd(a_re, a_im, ldt)
    cr_col, ci_col = cr.reshape(g * p, 1), ci.reshape(g * p, 1)
    b_re2, b_im2 = b_re.reshape(g * p, S5_GROUP), b_im.reshape(g * p, S5_GROUP)
    bb_re, bb_im = _s5_bbar_fwd(cr_col, ci_col, b_re2, b_im2)
    bd_re = _block_diag_in(bb_re.reshape(g, p, S5_GROUP)).astype(BF16)
    bd_im = _block_diag_in(bb_im.reshape(g, p, S5_GROUP)).astype(BF16)
    cd_re = _block_diag_out(c_re).astype(BF16)
    cd_im = _block_diag_out(-c_im).astype(BF16)
    ab_re = jnp.broadcast_to(abr.reshape(1, g * p), (N_SEG, g * p))
    ab_im = jnp.broadcast_to(abi.reshape(1, g * p), (N_SEG, g * p))
    zero_seg = jnp.zeros((N_SEG, g * p), F32)

    xn0 = _norm_cast(x, g_pre0 + comm.token, "norm_pre0", x_kind="nat")
    w_in = comm.weight("s5_w_in", [xn0, bd_re, bd_im, cd_re, cd_im, ab_re, ab_im])
    d_row, bglu_row = row(comm.vector("s5_d")), row(comm.vector("s5_b_glu"))
    u = _mm(xn0, w_in, "nn", F32, "s5_in_u", b_cols=(0, w), b_slots=True)
    z0 = _mm(xn0, w_in, "nn", BF16, "s5_in_z", b_cols=(w, w), b_slots=True)
    e_re, e_im = _s5_scan_fwd(u, bd_re, bd_im, cd_re, cd_im, ab_re, ab_im, zero_seg, zero_seg, d_row, False, "s5_scan_ends")
    i_re, i_im = _s5_seg_fix(e_re, e_im, ab_re, ab_im, seg_len, False, "s5_seg_fix")
    y_ssm, yg, h_re, h_im, _, _ = _s5_scan_fwd(u, bd_re, bd_im, cd_re, cd_im, ab_re, ab_im, i_re, i_im, d_row, True, "s5_scan")
    w_glu, w_out = comm.weight("s5_w_glu", yg), comm.weight("s5_w_out", yg)
    gp = _mm(yg, w_glu, "nn", BF16, "s5_glu")
    y3 = _s5_gate(y_ssm, gp, bglu_row, z0, "s5_gate")
    w_kvt, fw_in = comm.weight("kv_w", y3), comm.weight("fox_w_in", y3)
    w_ft = jnp.pad(w_kvt[2 * fw:], ((0, LANES - nh), (0, 0)))
    o0 = _mm(y3, w_out, "nn", F32, "s5_out")

    h1, hn_kv, xn1 = _resid_norm2(x, o0, g_post0 + comm.late_token, g_kv, g_pre1, "resid_norms")
    kv = _mm(hn_kv, w_kvt, "nt", BF16, "kv_proj", b_rows=2 * fw)
    f_logit = _mm(hn_kv, w_ft, "nt", F32, "f_proj")
    bf_row = _lane_pad(row(kv_b_f))
    cum2 = _cum_fwd(f_logit, bf_row, "cum_fwd")
    cum2_t = cum2[:, :nh].T.reshape(nh, 1, s)
    q2 = _mm(xn1, fw_in, "nn", BF16, "fox_q", scale=HEAD_DIM ** -0.5 * LOG2E, b_cols=(0, fw), b_slots=True)
    z1 = _mm(xn1, fw_in, "nn", BF16, "fox_z", b_cols=(fw, fw), b_slots=True)
    o, oz, lse2_t = _fox_fwd(q2, kv, cum2_t, z1, "fox_fwd")
    fw_out = comm.weight("fox_w_out", oz)
    o1 = _mm(oz, fw_out, "nn", F32, "fox_out")
    dh2, do1, sq, dg_post1 = _post_norm_loss(o1, g_post1, h1, target, "norm_post1_loss")
    loss = 0.5 * jnp.sum(sq) / d

    d_fw_out = _mm(oz, do1, "tn", BF16, "fox_out_dw")
    d_oz = _mm(do1, fw_out, "nt", BF16, "fox_out_dx")
    do, dqz = _gate_bwd(d_oz, o, z1, "fox_gate_bwd")
    dk, dv, dqz, dcq, dck = _fox_bwd(q2, kv, do, o, lse2_t, cum2, dqz, "fox_bwd")
    d_fw_in = _mm(xn1, dqz, "tn", BF16, "fox_in_dw", col_slots=True)
    dxn1 = _mm(dqz, fw_in, "nt", BF16, "fox_in_dx", b_slots=True)
    dcq_sl = _lane_pad(dcq.reshape(nh, s).T)
    dck_sl = _lane_pad(dck.reshape(nh, s).T)
    df, db_f = _cum_bwd(dcq_sl, dck_sl, f_logit, bf_row, "cum_bwd")
    dkv = _concat_cast(dk, dv, "fox_dkv")
    d_w_kvmt = _mm(dkv, hn_kv, "tn", BF16, "kv_dw")
    d_w_ft = _mm(df, hn_kv, "tn", BF16, "f_dw")
    dhn_f = _mm(df, w_ft, "nn", F32, "f_dx")
    dhn_kv = _mm(dkv, w_kvt, "nn", BF16, "kv_dx", add=dhn_f, b_rows=2 * fw)
    d_w_kvt = jnp.concatenate([d_w_kvmt, d_w_ft[:nh]], axis=0)
    tok = comm.send_grads(dict(fox_w_out=d_fw_out, fox_w_in=d_fw_in, kv_w=d_w_kvt), "exchange_fox")
    dh1, dg_pre1, dg_kv = _norm_bwd2(dh2, h1, dxn1, dhn_kv, g_pre1, g_kv, "resid_norms_bwd")

    do0, dg_post0 = _post_norm_bwd(dh1, o0, g_post0 + tok[0, 0], "norm_post0_bwd", dy_kind="nat")
    d_w_out = _mm(y3, do0, "tn", BF16, "s5_out_dw")
    dy3 = _mm(do0, w_out, "nt", BF16, "s5_out_dx")
    duz, dgp, dyg_direct, db_glu = _s5_gate_bwd(dy3, y_ssm, gp, bglu_row, z0, "s5_gate_bwd")
    d_w_glu = _mm(yg, dgp, "tn", BF16, "s5_glu_dw")
    gelu_bwd = lambda dyg, y: jax.vjp(jax.nn.gelu, y)[1](dyg)[0]
    dy_ssm = _mm(dgp, w_glu, "nt", F32, "s5_glu_dx", add=dyg_direct, epilogue=(gelu_bwd, y_ssm))
    d_row = d_row + comm.send_grads(dict(s5_w_out=d_w_out, s5_w_glu=d_w_glu), "exchange_s5")[0, 0]
    ab_imn = -ab_im
    ge_re, ge_im = _s5_scan_bwd(dy_ssm, u, h_re, h_im, bd_re, bd_im, cd_re, cd_im, ab_re, ab_imn, zero_seg, zero_seg,
                                d_row, False, "s5_adj_ends")
    gi_re, gi_im = _s5_seg_fix(ge_re, ge_im, ab_re, ab_imn, seg_len, True, "s5_adj_fix")
    duz, dbd_re, dbd_im, dcd_re, dcd_im, dab_re, dab_im, dd = _s5_scan_bwd(
        dy_ssm, u, h_re, h_im, bd_re, bd_im, cd_re, cd_im, ab_re, ab_imn, gi_re, gi_im, d_row, True, "s5_adj", duz=duz)
    d_w_in = _mm(xn0, duz, "tn", BF16, "s5_in_dw", col_slots=True)
    tok = comm.send_grads(dict(s5_w_in=d_w_in), "exchange_s5_in")
    dxn0 = _mm(duz, w_in, "nt", BF16, "s5_in_dx", after=tok, b_slots=True)
    grad_x, dg_pre0 = _norm_bwd1(dh1, x, dxn0, g_pre0, "norm_pre0_bwd")

    dbb_re = _block_diag_in_extract(dbd_re, p, S5_GROUP).reshape(g * p, S5_GROUP)
    dbb_im = _block_diag_in_extract(dbd_im, p, S5_GROUP).reshape(g * p, S5_GROUP)
    dcr_col, dci_col, db_re, db_im = _s5_bbar_bwd(cr_col, ci_col, b_re2, b_im2, dbb_re, dbb_im)
    da_re, da_im, dldt = _s5_disc_bwd(a_re, a_im, ldt, dab_re.reshape(g, p), dab_im.reshape(g, p),
                                      dcr_col.reshape(g, p), dci_col.reshape(g, p))
    dc_re = _block_diag_out_extract(dcd_re, S5_GROUP, p)
    dc_im = -_block_diag_out_extract(dcd_im, S5_GROUP, p)

    small = dict(
        norm_pre=jnp.concatenate([dg_pre0, dg_pre1], axis=0), norm_post=jnp.concatenate([dg_post0, dg_post1], axis=0),
        s5_a_re=da_re, s5_a_im=da_im, s5_log_dt=dldt.reshape(g), s5_b_re=db_re.reshape(g, p, S5_GROUP),
        s5_b_im=db_im.reshape(g, p, S5_GROUP), s5_c_re=dc_re, s5_c_im=dc_im, s5_d=dd.reshape(-1),
        s5_b_glu=db_glu.reshape(-1), kv_norm=dg_kv.reshape(-1), kv_b_f=db_f[0, :nh])
    return loss, grad_x, small


_BIG = ("s5_w_in", "s5_w_glu", "s5_w_out", "kv_w", "fox_w_in", "fox_w_out")
_COL_SHARDED = ("s5_w_in", "fox_w_in")
_SMALL = ("norm_pre", "norm_post", "s5_a_re", "s5_a_im", "s5_log_dt", "s5_b_re", "s5_b_im", "s5_c_re", "s5_c_im",
          "s5_d", "s5_b_glu", "kv_norm", "kv_b_f")
_SMALL_SHARDED = ("s5_d", "s5_b_glu")
_PACK_QUANTUM = SUBLANES * LANES
_WEIGHTS = ('norm_pre', 'norm_post', 's5_w_in', 's5_a_re', 's5_a_im', 's5_log_dt', 's5_b_re', 's5_b_im', 's5_c_re', 's5_c_im',
            's5_d', 's5_w_glu', 's5_b_glu', 's5_w_out', 'kv_norm', 'kv_w', 'kv_b_f', 'fox_w_in', 'fox_w_out')


def _full_from_slots(name, slots):
    n, r, c = slots.shape
    if name in _COL_SHARDED:
        return slots.transpose(1, 0, 2).reshape(r, n * c)
    return slots.reshape(n * r, c)


def _slots_from_full(name, full):
    if name in _COL_SHARDED:
        r, nc = full.shape
        return full.reshape(r, N_DEV, nc // N_DEV).transpose(1, 0, 2)
    nr, c = full.shape
    return full.reshape(N_DEV, nr // N_DEV, c)


def _groups_last(shape):
    return len(shape) >= 3 and shape[-1] < LANES and shape[-3] % LANES == 0


def _pack(vals):
    parts = []
    for v in vals:
        flat = jnp.moveaxis(v, -3, -1).reshape(-1) if _groups_last(v.shape) else v.reshape(-1)
        parts.append(jnp.pad(flat, (0, (-flat.shape[0]) % _PACK_QUANTUM)))
    total = sum(p.shape[0] for p in parts)
    parts.append(jnp.zeros(((-total) % (N_DEV * _PACK_QUANTUM),), F32))
    return jnp.concatenate(parts).reshape(-1, LANES)


def _unpack(packed, shapes):
    flat = packed.reshape(-1)
    out, off = [], 0
    for sh in shapes:
        n = math.prod(sh)
        piece = flat[off:off + n]
        if _groups_last(sh):
            piece = jnp.moveaxis(piece.reshape(sh[:-3] + sh[-2:] + sh[-3:-2]), -1, -3)
        out.append(piece.reshape(sh))
        off += n + (-n) % _PACK_QUANTUM
    return out


class _Comm:
    _GROUPS = (("s5_w_in",) + _SMALL_SHARDED, ("s5_w_glu", "s5_w_out"), ("kv_w", "fox_w_in"), ("fox_w_out",))
    _SLOT_FORM = ("s5_w_in", "fox_w_in")

    def __init__(self, shards, vectors, early=()):
        self._shards = {**shards, **vectors}
        self._full, self._gathers = {}, {}
        self._early = list(early)
        self.token = jnp.zeros((), F32)
        for group in self._GROUPS[:-1]:
            self.token = self.token + self._start(group, ())[0, 0]
        self.late_token = None
        self._sent = []

    def _start(self, group, after):
        state, tok = _exchange_start([self._shards[n] for n in group], False, "gather_start_" + group[0], after,
                                     peers=_CHIP_PEERS)
        self._gathers[group] = state
        return tok

    def vector(self, name):
        return self._full[name]

    def weight(self, name, after):
        if name not in self._full:
            group = next(g for g in self._GROUPS if name in g)
            if group == self._GROUPS[0]:
                after = (list(after) if isinstance(after, (list, tuple)) else [after]) + self._early
            slots = _exchange_wait(self._gathers.pop(group), after, "gather_wait_" + group[0])
            slots = _forward_to_sibling(slots, "gather_forward_" + group[0])
            for n, sl in zip(group, slots):
                if n in _SMALL_SHARDED:
                    self._full[n] = sl.reshape(-1)
                else:
                    self._full[n] = sl if n in self._SLOT_FORM else _full_from_slots(n, sl)
            if group == self._GROUPS[-2]:
                self.late_token = self._start(self._GROUPS[-1], [slots[0]])[0, 0]
        return self._full[name]

    def send_grads(self, grads, name):
        names = list(grads)
        slots = [grads[n] if grads[n].ndim == 3 else _slots_from_full(n, grads[n]).astype(BF16) for n in names]
        state, tok = _exchange_start(slots, True, name + "_start")
        self._sent.append((names, state, name + "_wait"))
        return tok

    def received_grads(self, after):
        for names, state, name in self._sent:
            for n, recv in zip(names, _exchange_wait(state, after, name)):
                yield n, recv


def kernel(x, norm_pre, norm_post, s5_w_in, s5_a_re, s5_a_im, s5_log_dt, s5_b_re, s5_b_im, s5_c_re, s5_c_im, s5_d, s5_w_glu, s5_b_glu, s5_w_out, kv_norm, kv_w, kv_b_f, fox_w_in, fox_w_out, loss_target, m_norm_pre, m_norm_post, m_s5_w_in, m_s5_a_re, m_s5_a_im, m_s5_log_dt, m_s5_b_re, m_s5_b_im, m_s5_c_re, m_s5_c_im, m_s5_d, m_s5_w_glu, m_s5_b_glu, m_s5_w_out, m_kv_norm, m_kv_w, m_kv_b_f, m_fox_w_in, m_fox_w_out, v_norm_pre, v_norm_post, v_s5_w_in, v_s5_a_re, v_s5_a_im, v_s5_log_dt, v_s5_b_re, v_s5_b_im, v_s5_c_re, v_s5_c_im, v_s5_d, v_s5_w_glu, v_s5_b_glu, v_s5_w_out, v_kv_norm, v_kv_w, v_kv_b_f, v_fox_w_in, v_fox_w_out):
    env = dict(locals())
    wts = {n: env[n] for n in _WEIGHTS}
    mom = {n: env["m_" + n] for n in _WEIGHTS}
    var = {n: env["v_" + n] for n in _WEIGHTS}
    me = 4 * lax.axis_index("x") + 2 * lax.axis_index("y") + lax.axis_index("c")
    shard2d = {n: (wts[n].T if n == "kv_w" else wts[n].reshape(wts[n].shape[-2:])) for n in _BIG}
    full_shape = {n: ((wts[n].size * N_DEV,) if n in _SMALL_SHARDED else wts[n].shape) for n in _SMALL}

    def spread(n, v):
        if n not in _SMALL_SHARDED:
            return v
        flat = v.reshape(-1)
        return lax.dynamic_update_slice(jnp.zeros(full_shape[n], F32), flat, (me * flat.shape[0],))

    packed = [_pack([spread(n, src[n]) for n in _SMALL] + [jnp.zeros((1,), F32)]) for src in (wts, mom, var)]
    comm = _Comm({n: _cast_bf16(shard2d[n], "cast_" + n) for n in _BIG}, {n: wts[n].reshape(1, -1) for n in _SMALL_SHARDED}, packed)

    loss_local, grad_x, small = _local_step(
        x[0], loss_target[0], norm_pre, norm_post, kv_norm, kv_b_f, s5_a_re[0], s5_a_im[0], s5_log_dt[0],
        s5_b_re[0], s5_b_im[0], s5_c_re[0], s5_c_im[0], comm)

    small_pack = _pack([small[n] for n in _SMALL] + [loss_local.reshape(1)])
    slice_rows = small_pack.shape[0] // N_DEV
    small_state, small_tok = _exchange_start([small_pack.reshape(N_DEV, slice_rows, LANES)], True, "reduce_small_start")

    res = {}
    for n, recv in comm.received_grads([small_tok, grad_x]):
        if n == "kv_w":
            res[n] = [o.T for o in _adamw(recv, wts[n].T, mom[n].T, var[n].T, "adamw_" + n)]
        else:
            res[n] = _adamw(recv, wts[n], mom[n], var[n], "adamw_" + n)

    my_sum = _sum_parts(_exchange_wait(small_state, res[_BIG[0]][0], "reduce_small_wait")[0], "sum_small")
    g_all = _exchange([my_sum], False, "gather_small")[0].reshape(1, small_pack.shape[0], LANES)
    outs = _adamw(g_all, *packed, "adamw_small")
    unpacked = [_unpack(o, [full_shape[n] for n in _SMALL] + [(1,)]) for o in outs]
    loss = unpacked[0][-1][0]
    for i, n in enumerate(_SMALL):
        vals = [u[i] for u in unpacked]
        if n in _SMALL_SHARDED:
            k = wts[n].size
            vals = [lax.dynamic_slice(v, (me * k,), (k,)) for v in vals]
        res[n] = [v.reshape(wts[n].shape) for v in vals]

    return (loss, grad_x[None], *[res[n][0] for n in _WEIGHTS], *[res[n][1] for n in _WEIGHTS],
            *[res[n][2] for n in _WEIGHTS], *[res[n][3] for n in _WEIGHTS])
```

```python
import math

import jax
import jax.numpy as jnp
from jax import lax
from jax.experimental import pallas as pl
from jax.experimental.pallas import tpu as pltpu

F32 = jnp.float32
BF16 = jnp.bfloat16

N_DEV = 8
MESH_AXES = ("x", "y", "c")
S5_GROUP = 16
S5_STATE = 64
LANES = 128
SUBLANES = 8
GROUPS_PER_BLOCK = LANES // S5_GROUP
BLOCK_STATE = GROUPS_PER_BLOCK * S5_STATE
N_SEG = SUBLANES
HEAD_DIM = 128
RMS_EPS = 1e-6
NEG_INF = -1e30
LOG2E = math.log2(math.e)
ADAM_LR = 0.001
ADAM_B1 = 0.9
ADAM_B2 = 0.999
ADAM_EPS = 1e-08
ADAM_WD = 0.01
ADAM_STEP = 10
VMEM_LIMIT = 56 * 1024 * 1024


def _tile(n, pref, quantum=LANES):
    if n <= pref:
        return n
    t = (pref // quantum) * quantum
    while t >= quantum:
        if n % t == 0:
            return t
        t -= quantum
    return n


def _cparams(*sem):
    return pltpu.CompilerParams(dimension_semantics=sem if sem else None, vmem_limit_bytes=VMEM_LIMIT)


_DOT_DIMS = {"nn": ((1,), (0,)), "nt": ((1,), (1,)), "tn": ((0,), (0,))}


def _mm(a, b, mode, out_dtype, name, add=None, scale=None, b_cols=None, after=None, col_slots=False, b_slots=False,
        b_rows=None, epilogue=None):
    slot_w = b.shape[2] if b_slots else None
    b2d = (b.shape[1], b.shape[0] * b.shape[2]) if b_slots else b.shape
    b_shape = b2d if b_cols is None else (b2d[0], b_cols[1])
    if b_rows is not None:
        b_shape = (b_rows, b_shape[1])
    if mode == "nn":
        (M, K), (K2, N) = a.shape, b_shape
    elif mode == "nt":
        (M, K), (N, K2) = a.shape, b_shape
    else:
        (K, M), (K2, N) = a.shape, b_shape
    assert K == K2, (name, a.shape, b_shape)
    tm, tn, tk = _tile(M, 1024 if K <= 2048 else 512), (N // N_DEV if col_slots else _tile(N, 1024)), _tile(K, 4096)
    if b_slots and mode == "nn":
        tn = slot_w
    nk = K // tk
    dims = (_DOT_DIMS[mode], ((), ()))
    col0 = 0
    if b_cols is not None:
        assert mode != "tn" and b_cols[0] % (tn if mode == "nn" else tk) == 0
        col0 = b_cols[0] // (tn if mode == "nn" else tk)
    assert not b_slots or (mode == "nn" or (mode == "nt" and nk == 1 and b_cols is None))

    def body(*refs):
        a_ref, b_ref = refs[:2]
        c_ref = refs[2] if add is not None else None
        e_ref = refs[2 + (add is not None)] if epilogue is not None else None
        o_ref = refs[2 + (add is not None) + (epilogue is not None) + (after is not None)]
        if b_slots and mode == "nt":
            part = lax.dot_general(a_ref[:, :slot_w], b_ref[0], dims, preferred_element_type=F32)
            for sl in range(1, b_ref.shape[0]):
                part += lax.dot_general(a_ref[:, sl * slot_w:(sl + 1) * slot_w], b_ref[sl], dims, preferred_element_type=F32)
        else:
            part = lax.dot_general(a_ref[...], b_ref[...], dims, preferred_element_type=F32)

        def finish(r):
            if scale is not None:
                r = r * scale
            if add is not None:
                r = r + c_ref[...]
            if epilogue is not None:
                r = epilogue[0](r, e_ref[...])
            o_ref[...] = r.astype(out_dtype)

        if nk == 1:
            finish(part)
            return
        acc = refs[-1]
        k = pl.program_id(2)

        @pl.when(k == 0)
        def _():
            acc[...] = part

        @pl.when(jnp.logical_and(k > 0, k < nk - 1))
        def _():
            acc[...] += part

        @pl.when(k == nk - 1)
        def _():
            finish(acc[...] + part)

    if mode == "tn":
        a_spec = pl.BlockSpec((tk, tm), lambda i, j, k: (k, i))
    else:
        a_spec = pl.BlockSpec((tm, tk), lambda i, j, k: (i, k))
    if b_slots and mode == "nn":
        b_spec = pl.BlockSpec((None, tk, tn), lambda i, j, k: (j + col0, k, 0))
    elif b_slots:
        b_spec = pl.BlockSpec((b.shape[0], tn, slot_w), lambda i, j, k: (0, j, 0))
    elif mode == "nt":
        b_spec = pl.BlockSpec((tn, tk), lambda i, j, k: (j, k + col0))
    else:
        b_spec = pl.BlockSpec((tk, tn), lambda i, j, k: (k, j + col0))
    o_spec = pl.BlockSpec((tm, tn), lambda i, j, k: (i, j))
    in_specs = [a_spec, b_spec] + ([o_spec] if add is not None else [])
    args = (a, b) + ((add,) if add is not None else ())
    if epilogue is not None:
        in_specs.append(o_spec)
        args += (epilogue[1],)
    if after is not None:
        in_specs.append(pl.BlockSpec(after.shape, lambda i, j, k: (0, 0)))
        args += (after,)
    out_shape = jax.ShapeDtypeStruct((M, N), out_dtype)
    if col_slots:
        assert add is None
        o_spec = pl.BlockSpec((None, tm, tn), lambda i, j, k: (j, i, 0))
        out_shape = jax.ShapeDtypeStruct((N_DEV, M, tn), out_dtype)
    return pl.pallas_call(
        body, name=name, grid=(M // tm, N // tn, nk),
        in_specs=in_specs, out_specs=o_spec,
        out_shape=out_shape,
        scratch_shapes=[pltpu.VMEM((tm, tn), F32)] if nk > 1 else [],
        compiler_params=_cparams("parallel", "parallel", "arbitrary"),
    )(*args)


class _NatIn:
    def __init__(self, ref):
        self.ref = ref

    def __getitem__(self, idx):
        v = jnp.swapaxes(self.ref[...], 0, 1)
        return v.reshape(v.shape[0] * N_SEG, v.shape[2])


class _NatOut:
    def __init__(self, ref):
        self.ref = ref

    def __setitem__(self, idx, val):
        self.ref[...] = jnp.swapaxes(val.reshape(val.shape[0] // N_SEG, N_SEG, val.shape[1]), 0, 1)


def _rowcall(body, name, n_rows, ins, outs, tile_rows=256):
    tr = _tile(n_rows, tile_rows, SUBLANES * 2)
    n_in = len(ins)
    in_kinds = [k for _, k in ins]
    kinds = [k for _, _, k in outs]

    def kern(*refs):
        @pl.when(pl.program_id(0) == 0)
        def _():
            for r, kind in zip(refs[n_in:], kinds):
                if kind == "acc":
                    r[...] = jnp.zeros_like(r)

        wrapped = [_NatIn(r) if k == "nat" else r for r, k in zip(refs[:n_in], in_kinds)]
        wrapped += [_NatOut(r) if k == "nat" else r for r, k in zip(refs[n_in:], kinds)]
        body(*wrapped)

    in_specs, args = [], []
    for arr, kind in ins:
        if kind == "row":
            in_specs.append(pl.BlockSpec((tr, arr.shape[1]), lambda i: (i, 0)))
        elif kind == "nat":
            in_specs.append(pl.BlockSpec((N_SEG, tr // N_SEG, arr.shape[1]), lambda i: (0, i, 0)))
            arr = arr.reshape(N_SEG, n_rows // N_SEG, arr.shape[1])
        else:
            in_specs.append(pl.BlockSpec(arr.shape, lambda i, nd=arr.ndim: (0,) * nd))
        args.append(arr)
    out_specs, out_shape = [], []
    for width, dtype, kind in outs:
        if kind == "row":
            out_specs.append(pl.BlockSpec((tr, width), lambda i: (i, 0)))
            out_shape.append(jax.ShapeDtypeStruct((n_rows, width), dtype))
        elif kind == "right":
            out_specs.append(pl.BlockSpec((tr, width), lambda i: (i, 1)))
            out_shape.append(jax.ShapeDtypeStruct((n_rows, 2 * width), dtype))
        elif kind == "nat":
            out_specs.append(pl.BlockSpec((N_SEG, tr // N_SEG, width), lambda i: (0, i, 0)))
            out_shape.append(jax.ShapeDtypeStruct((N_SEG, n_rows // N_SEG, width), dtype))
        else:
            out_specs.append(pl.BlockSpec((1, width), lambda i: (0, 0)))
            out_shape.append(jax.ShapeDtypeStruct((1, width), F32))
    res = pl.pallas_call(
        kern, name=name, grid=(n_rows // tr,), in_specs=in_specs, out_specs=out_specs, out_shape=out_shape,
        compiler_params=_cparams("arbitrary"),
    )(*args)
    return [r.reshape(n_rows, r.shape[2]) if k == "nat" else r for r, k in zip(res, kinds)]


def _rstd(x):
    return lax.rsqrt(jnp.mean(x * x, axis=-1, keepdims=True) + RMS_EPS)


def _rms_bwd(x, g, dy):
    xh = x * _rstd(x)
    dxh = dy * g
    dx = _rstd(x) * (dxh - xh * jnp.mean(dxh * xh, axis=-1, keepdims=True))
    return dx, jnp.sum(dy * xh, axis=0, keepdims=True)


def _silu(z):
    return z * jax.nn.sigmoid(z)


def _norm_cast(x, g, name, x_kind="row"):
    def body(x_ref, g_ref, o_ref):
        x = x_ref[...]
        o_ref[...] = (x * _rstd(x) * g_ref[...]).astype(BF16)

    return _rowcall(body, name, x.shape[0], [(x, x_kind), (g, "full")], [(x.shape[1], BF16, "row")])[0]


def _resid_norm2(x, o, g_post, g_kv, g_pre, name):
    def body(x_ref, o_ref, go_ref, gk_ref, gp_ref, h_ref, nk_ref, np_ref):
        o = o_ref[...]
        h = x_ref[...] + o * _rstd(o) * go_ref[...]
        h_ref[...] = h
        hn = h * _rstd(h)
        nk_ref[...] = (hn * gk_ref[...]).astype(BF16)
        np_ref[...] = (hn * gp_ref[...]).astype(BF16)

    d = x.shape[1]
    return _rowcall(body, name, x.shape[0], [(x, "nat"), (o, "row"), (g_post, "full"), (g_kv, "full"), (g_pre, "full")],
                    [(d, F32, "nat"), (d, BF16, "nat"), (d, BF16, "nat")])


def _post_norm_loss(o, g, h1, target, name):
    d = o.shape[1]

    def body(o_ref, g_ref, h_ref, t_ref, dh_ref, do_ref, acc_ref, dg_ref):
        o = o_ref[...]
        e = h_ref[...] + o * _rstd(o) * g_ref[...] - t_ref[...]
        dh = e * (1.0 / d)
        dh_ref[...] = dh
        acc_ref[...] += jnp.sum(e * e, axis=0, keepdims=True)
        dx, dg = _rms_bwd(o, g_ref[...], dh)
        do_ref[...] = dx.astype(BF16)
        dg_ref[...] += dg

    return _rowcall(body, name, o.shape[0], [(o, "row"), (g, "full"), (h1, "row"), (target, "row")],
                    [(d, F32, "row"), (d, BF16, "row"), (d, F32, "acc"), (d, F32, "acc")])


def _gate_bwd(d_oz, o, z, name):
    def body(d_ref, o_ref, z_ref, do_ref, dz_ref):
        _, vjp = jax.vjp(lambda o, z: o * _silu(z), o_ref[...], z_ref[...].astype(F32))
        do, dz = vjp(d_ref[...].astype(F32))
        do_ref[...] = do.astype(BF16)
        dz_ref[...] = dz.astype(BF16)

    w = o.shape[1]
    return _rowcall(body, name, o.shape[0], [(d_oz, "row"), (o, "row"), (z, "row")], [(w, BF16, "row"), (w, BF16, "right")])


def _norm_bwd2(dh2, h1, dxn1, dhn_kv, g_pre, g_kv, o0, g_post0, name):
    def body(dh2_ref, h_ref, d1_ref, dk_ref, gp_ref, gk_ref, o_ref, go_ref, dh1_ref, do_ref, dgp_ref, dgk_ref, dgo_ref):
        h = h_ref[...]
        dx1, dg1 = _rms_bwd(h, gp_ref[...], d1_ref[...].astype(F32))
        dxk, dgk = _rms_bwd(h, gk_ref[...], dk_ref[...].astype(F32))
        dh1 = dh2_ref[...] + dx1 + dxk
        dh1_ref[...] = dh1
        dgp_ref[...] += dg1
        dgk_ref[...] += dgk
        dxo, dgo = _rms_bwd(o_ref[...], go_ref[...], dh1)
        do_ref[...] = dxo.astype(BF16)
        dgo_ref[...] += dgo

    d = h1.shape[1]
    return _rowcall(body, name, h1.shape[0],
                    [(dh2, "nat"), (h1, "nat"), (dxn1, "nat"), (dhn_kv, "nat"), (g_pre, "full"), (g_kv, "full"),
                     (o0, "row"), (g_post0, "full")],
                    [(d, F32, "nat"), (d, BF16, "row"), (d, F32, "acc"), (d, F32, "acc"), (d, F32, "acc")])


def _norm_bwd1(dres, x, dxn, g, name):
    def body(dr_ref, x_ref, dn_ref, g_ref, dx_ref, dg_ref):
        dx, dg = _rms_bwd(x_ref[...], g_ref[...], dn_ref[...].astype(F32))
        dx_ref[...] = dr_ref[...] + dx
        dg_ref[...] += dg

    d = x.shape[1]
    return _rowcall(body, name, x.shape[0], [(dres, "nat"), (x, "nat"), (dxn, "row"), (g, "full")],
                    [(d, F32, "nat"), (d, F32, "acc")])


def _s5_gate(y_ssm, gp, b_glu, z, name):
    def body(y_ref, gp_ref, b_ref, z_ref, o_ref):
        yg = jax.nn.gelu(y_ref[...])
        o_ref[...] = (yg * jax.nn.sigmoid(gp_ref[...] + b_ref[...]) * _silu(z_ref[...].astype(F32))).astype(BF16)

    return _rowcall(body, name, y_ssm.shape[0], [(y_ssm, "row"), (gp, "row"), (b_glu, "full"), (z, "row")],
                    [(y_ssm.shape[1], BF16, "row")])[0]


def _s5_gate_bwd(dy3, y_ssm, gp, b_glu, z, name):
    def body(d_ref, y_ref, gp_ref, b_ref, z_ref, dz_ref, dgp_ref, dyg_ref, db_ref):
        yg = jax.nn.gelu(y_ref[...])
        _, vjp = jax.vjp(lambda yg, gp, z: yg * jax.nn.sigmoid(gp) * _silu(z), yg, gp_ref[...] + b_ref[...],
                         z_ref[...].astype(F32))
        dyg, dgp, dz = vjp(d_ref[...].astype(F32))
        dz_ref[...] = dz.astype(BF16)
        dgp_ref[...] = dgp.astype(BF16)
        dyg_ref[...] = dyg
        db_ref[...] += jnp.sum(dgp, axis=0, keepdims=True)

    w = y_ssm.shape[1]
    return _rowcall(body, name, y_ssm.shape[0],
                    [(dy3, "row"), (y_ssm, "row"), (gp, "row"), (b_glu, "full"), (z, "row")],
                    [(w, BF16, "right"), (w, BF16, "row"), (w, F32, "row"), (w, F32, "acc")])


def _cast_bf16(x, name):
    r, c = x.shape
    by_cols = r % (2 * SUBLANES) != 0
    tr, tc = (r, _tile(c, 256)) if by_cols else (_tile(r, 512, 2 * SUBLANES), c)
    pos = (lambda i: (0, i)) if by_cols else (lambda i: (i, 0))

    def body(x_ref, o_ref):
        o_ref[...] = x_ref[...].astype(BF16)

    return pl.pallas_call(
        body, name=name, grid=(c // tc if by_cols else r // tr,),
        in_specs=[pl.BlockSpec((tr, tc), pos)], out_specs=pl.BlockSpec((tr, tc), pos),
        out_shape=jax.ShapeDtypeStruct((r, c), BF16), compiler_params=_cparams("parallel"),
    )(x)


def _concat_cast(a, b, name):
    def body(a_ref, b_ref, o_ref):
        w = a_ref.shape[1]
        o_ref[:, :w] = a_ref[...].astype(BF16)
        o_ref[:, w:] = b_ref[...].astype(BF16)

    return _rowcall(body, name, a.shape[0], [(a, "row"), (b, "row")], [(a.shape[1] + b.shape[1], BF16, "row")])[0]


def _disc(ar, ai, ldt):
    dt = jnp.exp(ldt)
    mag = jnp.exp(ar * dt)
    abr = mag * jnp.cos(ai * dt)
    abi = mag * jnp.sin(ai * dt)
    den = ar * ar + ai * ai
    nr = abr - 1.0
    return abr, abi, (nr * ar + abi * ai) / den, (abi * ar - nr * ai) / den


def _s5_disc_fwd(a_re, a_im, ldt):
    def body(ar, ai, ld, o1, o2, o3, o4):
        o1[...], o2[...], o3[...], o4[...] = _disc(ar[...], ai[...], ld[...])

    sh = jax.ShapeDtypeStruct(a_re.shape, F32)
    return pl.pallas_call(body, name="s5_disc_fwd", out_shape=(sh, sh, sh, sh))(a_re, a_im, ldt)


def _s5_disc_bwd(a_re, a_im, ldt, d_abr, d_abi, d_cr, d_ci):
    def body(ar, ai, ld, g1, g2, g3, g4, o1, o2, o3):
        _, vjp = jax.vjp(_disc, ar[...], ai[...], ld[...])
        o1[...], o2[...], o3[...] = vjp((g1[...], g2[...], g3[...], g4[...]))

    sh = jax.ShapeDtypeStruct(a_re.shape, F32)
    return pl.pallas_call(body, name="s5_disc_bwd", out_shape=(sh, sh, jax.ShapeDtypeStruct(ldt.shape, F32)))(
        a_re, a_im, ldt, d_abr, d_abi, d_cr, d_ci)


def _bbar(cr, ci, br, bi):
    return cr * br - ci * bi, cr * bi + ci * br


def _s5_bbar_fwd(cr_col, ci_col, b_re, b_im):
    def body(cr, ci, br, bi, o1, o2):
        o1[...], o2[...] = _bbar(cr[...], ci[...], br[...], bi[...])

    w = b_re.shape[1]
    return _rowcall(body, "s5_bbar_fwd", b_re.shape[0], [(cr_col, "row"), (ci_col, "row"), (b_re, "row"), (b_im, "row")],
                    [(w, F32, "row"), (w, F32, "row")], tile_rows=1024)


def _s5_bbar_bwd(cr_col, ci_col, b_re, b_im, d_re, d_im):
    def body(cr, ci, br, bi, g1, g2, o1, o2, o3, o4):
        _, vjp = jax.vjp(_bbar, cr[...], ci[...], br[...], bi[...])
        o1[...], o2[...], o3[...], o4[...] = vjp((g1[...], g2[...]))

    w = b_re.shape[1]
    return _rowcall(body, "s5_bbar_bwd", b_re.shape[0],
                    [(cr_col, "row"), (ci_col, "row"), (b_re, "row"), (b_im, "row"), (d_re, "row"), (d_im, "row")],
                    [(1, F32, "row"), (1, F32, "row"), (w, F32, "row"), (w, F32, "row")], tile_rows=1024)


def _block_diag_in(t):
    g, p, c = t.shape
    nb = g // GROUPS_PER_BLOCK
    t4 = t.reshape(nb, GROUPS_PER_BLOCK, p, c).transpose(0, 1, 3, 2)
    eye = jnp.eye(GROUPS_PER_BLOCK, dtype=t.dtype)
    return (t4[:, :, :, None, :] * eye[None, :, None, :, None]).reshape(nb, GROUPS_PER_BLOCK * c, GROUPS_PER_BLOCK * p)


def _block_diag_in_extract(d, p, c):
    nb = d.shape[0]
    d5 = d.reshape(nb, GROUPS_PER_BLOCK, c, GROUPS_PER_BLOCK, p)
    diag = jnp.stack([d5[:, g, :, g, :] for g in range(GROUPS_PER_BLOCK)], axis=1)
    return diag.transpose(0, 1, 3, 2).reshape(nb * GROUPS_PER_BLOCK, p, c)


def _block_diag_out(t):
    g, c, p = t.shape
    nb = g // GROUPS_PER_BLOCK
    t4 = t.reshape(nb, GROUPS_PER_BLOCK, c, p).transpose(0, 1, 3, 2)
    eye = jnp.eye(GROUPS_PER_BLOCK, dtype=t.dtype)
    return (t4[:, :, :, None, :] * eye[None, :, None, :, None]).reshape(nb, GROUPS_PER_BLOCK * p, GROUPS_PER_BLOCK * c)


def _block_diag_out_extract(d, c, p):
    nb = d.shape[0]
    d5 = d.reshape(nb, GROUPS_PER_BLOCK, p, GROUPS_PER_BLOCK, c)
    diag = jnp.stack([d5[:, g, :, g, :] for g in range(GROUPS_PER_BLOCK)], axis=1)
    return diag.transpose(0, 1, 3, 2).reshape(nb * GROUPS_PER_BLOCK, c, p)


def _scan_step(ar, ai, hr, hi, xr, xi):
    return ar * hr - ai * hi + xr, ar * hi + ai * hr + xi


def _s5_scan_fwd(u, bd_re, bd_im, cd_re, cd_im, ab_re, ab_im, init_re, init_im, d_row, full, name):
    s, w = u.shape
    nb = w // LANES
    rows = _tile(s, 512, SUBLANES)
    nc = s // rows
    steps = rows // N_SEG
    ns = nb * BLOCK_STATE

    def body(u_ref, bdr, bdi, cdr, cdi, ar_ref, ai_ref, ir_ref, ii_ref, d_ref, *outs):
        if full:
            y_ref, yg_ref, hr_ref, hi_ref, er_ref, ei_ref, cr, ci = outs
        else:
            er_ref, ei_ref, hr_ref, hi_ref, cr, ci = outs
        c = pl.program_id(1)

        @pl.when(c == 0)
        def _():
            cr[...] = ir_ref[...]
            ci[...] = ii_ref[...]

        ub = u_ref[...].astype(BF16)
        hr_ref[...] = jnp.dot(ub, bdr[...], preferred_element_type=F32)
        hi_ref[...] = jnp.dot(ub, bdi[...], preferred_element_type=F32)
        ar, ai = ar_ref[...], ai_ref[...]

        hr, hi = cr[...], ci[...]
        for j in range(steps):
            rows_j = pl.ds(j * N_SEG, N_SEG)
            hr, hi = _scan_step(ar, ai, hr, hi, hr_ref[rows_j, :], hi_ref[rows_j, :])
            hr_ref[rows_j, :] = hr
            hi_ref[rows_j, :] = hi
        cr[...] = hr
        ci[...] = hi
        if full:
            y = (jnp.dot(hr_ref[...].astype(BF16), cdr[...], preferred_element_type=F32)
                 + jnp.dot(hi_ref[...].astype(BF16), cdi[...], preferred_element_type=F32)
                 + d_ref[...] * u_ref[...])
            y_ref[...] = y
            yg_ref[...] = jax.nn.gelu(y).astype(BF16)

        @pl.when(c == nc - 1)
        def _():
            er_ref[...] = hr
            ei_ref[...] = hi

    blk3 = lambda a: pl.BlockSpec((None,) + a.shape[1:], lambda k, c: (k, 0, 0))
    seg = pl.BlockSpec((N_SEG, BLOCK_STATE), lambda k, c: (0, k))
    st = pl.BlockSpec((rows, BLOCK_STATE), lambda k, c: (c, k))
    in_specs = [pl.BlockSpec((rows, LANES), lambda k, c: (c, k)), blk3(bd_re), blk3(bd_im), blk3(cd_re), blk3(cd_im),
                seg, seg, seg, seg, pl.BlockSpec((1, LANES), lambda k, c: (0, k))]
    seg_shape = jax.ShapeDtypeStruct((N_SEG, ns), F32)
    st_shape = jax.ShapeDtypeStruct((s, ns), F32)
    carry = [pltpu.VMEM((N_SEG, BLOCK_STATE), F32)] * 2
    if full:
        ych = pl.BlockSpec((rows, LANES), lambda k, c: (c, k))
        out_specs = [ych, ych, st, st, seg, seg]
        out_shape = [jax.ShapeDtypeStruct((s, w), F32), jax.ShapeDtypeStruct((s, w), BF16), st_shape, st_shape, seg_shape, seg_shape]
        scratch = carry
    else:
        out_specs = [seg, seg]
        out_shape = [seg_shape, seg_shape]
        scratch = [pltpu.VMEM((rows, BLOCK_STATE), F32)] * 2 + carry
    return pl.pallas_call(
        body, name=name, grid=(nb, nc), in_specs=in_specs, out_specs=out_specs, out_shape=out_shape,
        scratch_shapes=scratch, compiler_params=_cparams("parallel", "arbitrary"),
    )(u, bd_re, bd_im, cd_re, cd_im, ab_re, ab_im, init_re, init_im, d_row)


def _s5_seg_fix(e_re, e_im, ab_re, ab_im, seg_len, reverse, name):
    assert seg_len & (seg_len - 1) == 0

    def body(er, ei, ar, ai, o_re, o_im):
        pr, pi = ar[0:1, :], ai[0:1, :]
        for _ in range(int(math.log2(seg_len))):
            pr, pi = pr * pr - pi * pi, 2.0 * pr * pi
        tr = jnp.zeros_like(pr)
        ti = jnp.zeros_like(pr)
        order = list(range(N_SEG - 1, -1, -1)) if reverse else list(range(N_SEG))
        for n, sgm in enumerate(order):
            o_re[sgm:sgm + 1, :] = tr
            o_im[sgm:sgm + 1, :] = ti
            if n < N_SEG - 1:
                tr, ti = _scan_step(pr, pi, tr, ti, er[sgm:sgm + 1, :], ei[sgm:sgm + 1, :])

    sh = jax.ShapeDtypeStruct(e_re.shape, F32)
    return pl.pallas_call(body, name=name, out_shape=(sh, sh))(e_re, e_im, ab_re, ab_im)


def _s5_scan_bwd(dy, u, h_re, h_im, bd_re, bd_im, cd_re, cd_im, ab_re, ab_imn, gin_re, gin_im, d_row, full, name, duz=None):
    s, w = u.shape
    nb = w // LANES
    rows = _tile(s, 512, SUBLANES)
    nc = s // rows
    steps = rows // N_SEG
    ns = nb * BLOCK_STATE

    def body(dy_ref, u_ref, hr_ref, hi_ref, bdr, bdi, cdr, cdi, ar_ref, ai_ref, ir_ref, ii_ref, d_ref, *outs):
        if full:
            _, du_ref, dbr_ref, dbi_ref, dcr_ref, dci_ref, dar_ref, dai_ref, dd_ref, gr, gi, accr, acci = outs
        else:
            er_ref, ei_ref, gr, gi = outs
        c = pl.program_id(1)

        @pl.when(c == 0)
        def _():
            gr[pl.ds(rows, N_SEG), :] = ir_ref[...]
            gi[pl.ds(rows, N_SEG), :] = ii_ref[...]
            if full:
                for r in (dbr_ref, dbi_ref, dcr_ref, dci_ref, dd_ref, accr, acci):
                    r[...] = jnp.zeros_like(r)

        dyb = dy_ref[...].astype(BF16)
        nt = (_DOT_DIMS["nt"], ((), ()))
        tn = (_DOT_DIMS["tn"], ((), ()))
        gr[pl.ds(0, rows), :] = lax.dot_general(dyb, cdr[...], nt, preferred_element_type=F32)
        gi[pl.ds(0, rows), :] = lax.dot_general(dyb, cdi[...], nt, preferred_element_type=F32)
        ar, ai = ar_ref[...], ai_ref[...]

        g0r, g0i = gr[pl.ds(rows, N_SEG), :], gi[pl.ds(rows, N_SEG), :]
        for j in range(steps - 1, -1, -1):
            rows_j = pl.ds(j * N_SEG, N_SEG)
            g0r, g0i = _scan_step(ar, ai, g0r, g0i, gr[rows_j, :], gi[rows_j, :])
            gr[rows_j, :] = g0r
            gi[rows_j, :] = g0i
        if full:
            hr, hi = hr_ref[...], hi_ref[...]
            gnr, gni = gr[pl.ds(N_SEG, rows), :], gi[pl.ds(N_SEG, rows), :]
            accr[...] += jnp.sum((gnr * hr + gni * hi).reshape(steps, N_SEG, BLOCK_STATE), axis=0)
            acci[...] += jnp.sum((gni * hr - gnr * hi).reshape(steps, N_SEG, BLOCK_STATE), axis=0)
        gr[pl.ds(rows, N_SEG), :] = g0r
        gi[pl.ds(rows, N_SEG), :] = g0i
        if full:
            ub = u_ref[...].astype(BF16)
            gbr, gbi = gr[pl.ds(0, rows), :].astype(BF16), gi[pl.ds(0, rows), :].astype(BF16)
            dcr_ref[...] += lax.dot_general(hr.astype(BF16), dyb, tn, preferred_element_type=F32)
            dci_ref[...] += lax.dot_general(hi.astype(BF16), dyb, tn, preferred_element_type=F32)
            dbr_ref[...] += lax.dot_general(ub, gbr, tn, preferred_element_type=F32)
            dbi_ref[...] += lax.dot_general(ub, gbi, tn, preferred_element_type=F32)
            du_ref[...] = (lax.dot_general(gbr, bdr[...], nt, preferred_element_type=F32)
                           + lax.dot_general(gbi, bdi[...], nt, preferred_element_type=F32)
                           + d_ref[...] * dy_ref[...]).astype(BF16)
            dd_ref[...] += jnp.sum(dy_ref[...] * u_ref[...], axis=0, keepdims=True)

        @pl.when(c == nc - 1)
        def _():
            if full:
                dar_ref[...] = jnp.sum(accr[...], axis=0, keepdims=True)
                dai_ref[...] = jnp.sum(acci[...], axis=0, keepdims=True)
            else:
                er_ref[...] = g0r
                ei_ref[...] = g0i

    rev = lambda k, c: (nc - 1 - c, k)
    blk3 = lambda a: pl.BlockSpec((None,) + a.shape[1:], lambda k, c: (k, 0, 0))
    seg = pl.BlockSpec((N_SEG, BLOCK_STATE), lambda k, c: (0, k))
    st = pl.BlockSpec((rows, BLOCK_STATE), rev)
    ch = pl.BlockSpec((rows, LANES), rev)
    vec = pl.BlockSpec((1, LANES), lambda k, c: (0, k))
    if not full:
        st = pl.BlockSpec((rows, BLOCK_STATE), lambda k, c: (0, k))
    in_specs = [ch, ch if full else pl.BlockSpec((rows, LANES), lambda k, c: (0, k)), st, st,
                blk3(bd_re), blk3(bd_im), blk3(cd_re), blk3(cd_im), seg, seg, seg, seg, vec]
    args = [dy, u, h_re, h_im, bd_re, bd_im, cd_re, cd_im, ab_re, ab_imn, gin_re, gin_im, d_row]
    gbuf = [pltpu.VMEM((rows + N_SEG, BLOCK_STATE), F32)] * 2
    if full:
        row1 = pl.BlockSpec((1, BLOCK_STATE), lambda k, c: (0, k))
        out_specs = [ch, blk3(bd_re), blk3(bd_im), blk3(cd_re), blk3(cd_im), row1, row1, vec]
        out_shape = [jax.ShapeDtypeStruct(duz.shape, BF16),
                     jax.ShapeDtypeStruct(bd_re.shape, F32), jax.ShapeDtypeStruct(bd_im.shape, F32),
                     jax.ShapeDtypeStruct(cd_re.shape, F32), jax.ShapeDtypeStruct(cd_im.shape, F32),
                     jax.ShapeDtypeStruct((1, ns), F32), jax.ShapeDtypeStruct((1, ns), F32),
                     jax.ShapeDtypeStruct((1, w), F32)]
        scratch = gbuf + [pltpu.VMEM((N_SEG, BLOCK_STATE), F32)] * 2
        in_specs.append(pl.BlockSpec(memory_space=pl.ANY))
        args.append(duz)
        aliases = {len(args) - 1: 0}
    else:
        out_specs = [seg, seg]
        out_shape = [jax.ShapeDtypeStruct((N_SEG, ns), F32)] * 2
        scratch = gbuf
        aliases = {}
    return pl.pallas_call(
        body, name=name, grid=(nb, nc), in_specs=in_specs, out_specs=out_specs, out_shape=out_shape,
        input_output_aliases=aliases, scratch_shapes=scratch, compiler_params=_cparams("parallel", "arbitrary"),
    )(*args)


def _log_sigmoid(x):
    return jnp.minimum(x, 0.0) - jnp.log(1.0 + jnp.exp(-jnp.abs(x)))


def _tri(n, upper):
    r = lax.broadcasted_iota(jnp.int32, (n, n), 0)
    c = lax.broadcasted_iota(jnp.int32, (n, n), 1)
    return jnp.where((c >= r) if upper else (r >= c), 1.0, 0.0).astype(F32)


def _cum_fwd(f_logit, b_row, name):
    s, w = f_logit.shape
    t = _tile(s, 256, SUBLANES)

    def body(f_ref, b_ref, o_ref, carry):
        @pl.when(pl.program_id(0) == 0)
        def _():
            carry[...] = jnp.zeros_like(carry)

        lf = _log_sigmoid(f_ref[...] + b_ref[...])
        cum = jnp.dot(_tri(t, False), lf, precision=lax.Precision.HIGHEST, preferred_element_type=F32) + carry[...]
        o_ref[...] = cum * LOG2E
        carry[...] = cum[t - 1:t, :]

    return pl.pallas_call(
        body, name=name, grid=(s // t,),
        in_specs=[pl.BlockSpec((t, w), lambda i: (i, 0)), pl.BlockSpec((1, w), lambda i: (0, 0))],
        out_specs=pl.BlockSpec((t, w), lambda i: (i, 0)), out_shape=jax.ShapeDtypeStruct((s, w), F32),
        scratch_shapes=[pltpu.VMEM((1, w), F32)], compiler_params=_cparams("arbitrary"),
    )(f_logit, b_row)


def _cum_bwd(dcq, dck, f_logit, b_row, name):
    s, w = f_logit.shape
    t = _tile(s, 256, SUBLANES)
    nt = s // t

    def body(q_ref, k_ref, f_ref, b_ref, df_ref, db_ref, carry):
        @pl.when(pl.program_id(0) == 0)
        def _():
            carry[...] = jnp.zeros_like(carry)
            db_ref[...] = jnp.zeros_like(db_ref)

        dc = q_ref[...] - k_ref[...]
        rc = jnp.dot(_tri(t, True), dc, precision=lax.Precision.HIGHEST, preferred_element_type=F32) + carry[...]
        carry[...] = rc[0:1, :]
        df = rc * (1.0 - jax.nn.sigmoid(f_ref[...] + b_ref[...]))
        df_ref[...] = df.astype(BF16)
        db_ref[...] += jnp.sum(df, axis=0, keepdims=True)

    rev = pl.BlockSpec((t, w), lambda i: (nt - 1 - i, 0))
    one = pl.BlockSpec((1, w), lambda i: (0, 0))
    return pl.pallas_call(
        body, name=name, grid=(nt,), in_specs=[rev, rev, rev, one], out_specs=[rev, one],
        out_shape=[jax.ShapeDtypeStruct((s, w), BF16), jax.ShapeDtypeStruct((1, w), F32)],
        scratch_shapes=[pltpu.VMEM((1, w), F32)], compiler_params=_cparams("arbitrary"),
    )(dcq, dck, f_logit, b_row)


def _head_col(cum_tile, h):
    lane = lax.broadcasted_iota(jnp.int32, cum_tile.shape, 1)
    return jnp.sum(jnp.where(lane == h, cum_tile, 0.0), axis=1, keepdims=True)


def _attn_tiles(s):
    return _tile(s, 512, LANES)


def _exp2_rows(sc, sub):
    return jnp.concatenate([jnp.exp2(sc[:, b * LANES:(b + 1) * LANES] - sub) for b in range(sc.shape[1] // LANES)], axis=1)


def _row_of(rep):
    return jnp.transpose(rep)[0:1, :]


def _causal(sc, keys_on_rows):
    r = lax.broadcasted_iota(jnp.int32, sc.shape, 0)
    c = lax.broadcasted_iota(jnp.int32, sc.shape, 1)
    return jnp.where((r <= c) if keys_on_rows else (c <= r), sc, NEG_INF)


def _fox_fwd(q2, kv, cum2_t, z, name):
    s, w = q2.shape
    nh = w // HEAD_DIM
    tq = _attn_tiles(s)
    nq = s // tq
    nt = (_DOT_DIMS["nt"], ((), ()))

    def body(q_ref, k_ref, v_ref, ct_ref, z_ref, o_ref, oz_ref, lse_row_ref, m_s, acc_s, vaug, s_buf):
        i = pl.program_id(1)

        @pl.when(i == 0)
        def _():
            vaug[:, :HEAD_DIM] = v_ref[...]
            vaug[:, HEAD_DIM:] = jnp.ones((s, LANES), BF16)

        qb = q_ref[...]
        m_s[...] = jnp.full_like(m_s, NEG_INF)
        acc_s[...] = jnp.zeros_like(acc_s)

        def scores(j):
            off = pl.multiple_of(j * tq, tq)
            return lax.dot_general(qb, k_ref[pl.ds(off, tq), :], nt, preferred_element_type=F32) - ct_ref[:, pl.ds(off, tq)]

        def softmax_pv(j, sc):
            m_old = m_s[...]
            m_new = jnp.maximum(m_old, jnp.max(sc, axis=1, keepdims=True))
            p = _exp2_rows(sc, m_new)
            alpha = jnp.exp2(m_old - m_new)
            pv = jnp.dot(p.astype(BF16), vaug[pl.ds(pl.multiple_of(j * tq, tq), tq), :], preferred_element_type=F32)
            acc_s[...] = jnp.concatenate([alpha, alpha], axis=1) * acc_s[...] + pv
            m_s[...] = m_new

        s_buf[...] = scores(0)

        def loop(j, carry):
            nxt = scores(j + 1)
            softmax_pv(j, s_buf[...])
            s_buf[...] = nxt
            return carry

        lax.fori_loop(0, i, loop, 0)
        softmax_pv(i, _causal(s_buf[...], False))
        l = acc_s[:, HEAD_DIM:]
        o = acc_s[:, :HEAD_DIM] / l
        o_ref[...] = o
        oz_ref[...] = (o * _silu(z_ref[...].astype(F32))).astype(BF16)
        lse_row_ref[...] = _row_of(m_s[...] + jnp.log(l) * LOG2E)

    return pl.pallas_call(
        body, name=name, grid=(nh, nq),
        in_specs=[pl.BlockSpec((tq, HEAD_DIM), lambda h, i: (i, h)),
                  pl.BlockSpec((s, HEAD_DIM), lambda h, i: (0, h)),
                  pl.BlockSpec((s, HEAD_DIM), lambda h, i: (0, nh + h)),
                  pl.BlockSpec((None, 1, s), lambda h, i: (h, 0, 0)),
                  pl.BlockSpec((tq, HEAD_DIM), lambda h, i: (i, h))],
        out_specs=[pl.BlockSpec((tq, HEAD_DIM), lambda h, i: (i, h)),
                   pl.BlockSpec((tq, HEAD_DIM), lambda h, i: (i, h)),
                   pl.BlockSpec((None, 1, tq), lambda h, i: (h, 0, i))],
        out_shape=[jax.ShapeDtypeStruct((s, w), F32), jax.ShapeDtypeStruct((s, w), BF16),
                   jax.ShapeDtypeStruct((nh, 1, s), F32)],
        scratch_shapes=[pltpu.VMEM((tq, LANES), F32), pltpu.VMEM((tq, HEAD_DIM + LANES), F32),
                        pltpu.VMEM((s, HEAD_DIM + LANES), BF16), pltpu.VMEM((tq, tq), F32)],
        compiler_params=_cparams("arbitrary", "arbitrary"),
    )(q2, kv, kv, cum2_t, z)


def _fox_bwd(q2, kv, do, o, lse2_t, cum2, dqz, name):
    s, w = q2.shape
    nh = w // HEAD_DIM
    tk = _attn_tiles(s)
    nk = s // tk
    scale = HEAD_DIM ** -0.5
    nt = (_DOT_DIMS["nt"], ((), ()))
    tn = (_DOT_DIMS["tn"], ((), ()))

    def body(q_ref, k_ref, v_ref, do_ref, o_ref, lse_ref, c_ref, _, dk_ref, dv_ref, dq_ref, dcq_ref, dck_ref,
             dk_s, dv_s, dc_s, dq_s, dcq_s, dl_s, s_buf, dp_buf):
        h, j = pl.program_id(0), pl.program_id(1)

        @pl.when(j == 0)
        def _():
            dq_s[...] = jnp.zeros_like(dq_s)
            dcq_s[...] = jnp.zeros_like(dcq_s)
            for i in range(nk):
                rows = pl.ds(i * tk, tk)
                d = jnp.sum(do_ref[rows, :].astype(F32) * o_ref[rows, :], axis=1, keepdims=True)
                dl_s[:, i * tk:(i + 1) * tk] = _row_of(jnp.broadcast_to(d, (tk, LANES)))

        kb = k_ref[...]
        vb = v_ref[...]
        ck = jnp.broadcast_to(_head_col(c_ref[...], h), (tk, LANES))
        dk_s[...] = jnp.zeros_like(dk_s)
        dv_s[...] = jnp.zeros_like(dv_s)
        dc_s[...] = jnp.zeros_like(dc_s)

        def scores(i):
            off = pl.multiple_of(i * tk, tk)
            sc = lax.dot_general(kb, q_ref[pl.ds(off, tk), :], nt, preferred_element_type=F32) - lse_ref[:, pl.ds(off, tk)]
            dp = lax.dot_general(vb, do_ref[pl.ds(off, tk), :], nt, preferred_element_type=F32) - dl_s[:, pl.ds(off, tk)]
            return sc, dp

        def accumulate(i, sc, dp):
            off = pl.multiple_of(i * tk, tk)
            p = _exp2_rows(sc, ck)
            dv_s[...] += jnp.dot(p.astype(BF16), do_ref[pl.ds(off, tk), :], preferred_element_type=F32)
            ds = p * dp
            dsb = ds.astype(BF16)
            dk_s[...] += jnp.dot(dsb, q_ref[pl.ds(off, tk), :], preferred_element_type=F32)
            dq_s[pl.ds(off, tk), :] += lax.dot_general(dsb, kb, tn, preferred_element_type=F32)
            dcq_s[:, pl.ds(off, tk)] += jnp.sum(ds, axis=0, keepdims=True)
            part = ds[:, :LANES]
            for b in range(1, tk // LANES):
                part = part + ds[:, b * LANES:(b + 1) * LANES]
            dc_s[...] += part

        sc0, dp0 = scores(j)
        s_buf[...] = _causal(sc0, True)
        dp_buf[...] = dp0

        def loop(i, carry):
            nxt = scores(i + 1)
            accumulate(i, s_buf[...], dp_buf[...])
            s_buf[...], dp_buf[...] = nxt
            return carry

        lax.fori_loop(j, nk - 1, loop, 0)
        accumulate(nk - 1, s_buf[...], dp_buf[...])
        dk_ref[...] = (dk_s[...] * (1.0 / LOG2E)).astype(BF16)
        dv_ref[...] = dv_s[...].astype(BF16)
        dck_ref[...] = jnp.sum(jnp.transpose(dc_s[...]), axis=0, keepdims=True)

        @pl.when(j == nk - 1)
        def _():
            dq_ref[...] = (dq_s[...] * scale).astype(BF16)
            dcq_ref[...] = dcq_s[...]

    col = pl.BlockSpec((s, HEAD_DIM), lambda h, j: (0, h))
    row = pl.BlockSpec((None, 1, s), lambda h, j: (h, 0, 0))
    kspec = pl.BlockSpec((tk, HEAD_DIM), lambda h, j: (j, h))
    return pl.pallas_call(
        body, name=name, grid=(nh, nk),
        in_specs=[col, kspec, pl.BlockSpec((tk, HEAD_DIM), lambda h, j: (j, nh + h)), col, col, row,
                  pl.BlockSpec((tk, LANES), lambda h, j: (j, 0)), pl.BlockSpec(memory_space=pl.ANY)],
        out_specs=[kspec, kspec, col, row, pl.BlockSpec((None, 1, tk), lambda h, j: (h, 0, j))],
        out_shape=[jax.ShapeDtypeStruct((s, w), BF16), jax.ShapeDtypeStruct((s, w), BF16),
                   jax.ShapeDtypeStruct(dqz.shape, BF16), jax.ShapeDtypeStruct((nh, 1, s), F32),
                   jax.ShapeDtypeStruct((nh, 1, s), F32)],
        input_output_aliases={7: 2},
        scratch_shapes=[pltpu.VMEM((tk, HEAD_DIM), F32), pltpu.VMEM((tk, HEAD_DIM), F32), pltpu.VMEM((tk, LANES), F32),
                        pltpu.VMEM((s, HEAD_DIM), F32), pltpu.VMEM((1, s), F32), pltpu.VMEM((1, s), F32),
                        pltpu.VMEM((tk, tk), F32), pltpu.VMEM((tk, tk), F32)],
        compiler_params=_cparams("arbitrary", "arbitrary"),
    )(q2, kv, kv, do, o, lse2_t, cum2, dqz)


_ALL_PEERS = tuple(range(1, N_DEV))
_CHIP_PEERS = (1, 2, 4, 6)


def _exchange_copies(ins, outs, send_sems, recv_sems, local_sems, scatter, peers=_ALL_PEERS):
    x, y, c = (lax.axis_index(a) for a in MESH_AXES)
    me = 4 * x + 2 * y + c
    local, remote = [], []
    for a in range(len(ins)):
        local.append(pltpu.make_async_copy(ins[a].at[me] if scatter else ins[a], outs[a].at[me], local_sems.at[a]))
        for k in peers:
            px, py, pc = (1 - x if k & 4 else x), (1 - y if k & 2 else y), (1 - c if k & 1 else c)
            remote.append(pltpu.make_async_remote_copy(
                src_ref=ins[a].at[4 * px + 2 * py + pc] if scatter else ins[a], dst_ref=outs[a].at[me],
                send_sem=send_sems.at[a * (N_DEV - 1) + k - 1], recv_sem=recv_sems.at[a * (N_DEV - 1) + k - 1],
                device_id=(px, py, pc), device_id_type=pl.DeviceIdType.MESH))
    return local, remote


def _exchange_out_shapes(arrs, scatter):
    return [((N_DEV,) + a.shape[1:]) if scatter else ((N_DEV,) + a.shape) for a in arrs]


_HBM =pl.BlockSpec(memory_space=pltpu.HBM)
_SEM = pl.BlockSpec(memory_space=pltpu.SEMAPHORE)


def _exchange_start(arrs, scatter, name, after=(), peers=_ALL_PEERS):
    n = len(arrs)
    after = list(after)
    lands = [lax.empty(s, a.dtype) for s, a in zip(_exchange_out_shapes(arrs, scatter), arrs)]

    def body(*refs):
        ins, outs = refs[:n], refs[n:2 * n]
        send_sems, recv_sems, local_sems = refs[2 * n + len(after):2 * n + len(after) + 3]
        token = refs[-1]
        local, remote = _exchange_copies(ins, outs, send_sems, recv_sems, local_sems, scatter, peers)
        for cp in local + remote:
            cp.start()
        token[...] = jnp.zeros_like(token)

    hbm = lambda a: pltpu.HBM(a.shape, a.dtype)
    res = pl.pallas_call(
        body, name=name,
        out_shape=(pltpu.SemaphoreType.DMA((n * (N_DEV - 1),)), pltpu.SemaphoreType.DMA((n * (N_DEV - 1),)),
                   pltpu.SemaphoreType.DMA((n,)), *[hbm(a) for a in arrs], *[hbm(a) for a in lands],
                   jax.ShapeDtypeStruct((SUBLANES, LANES), F32)),
        in_specs=[_HBM] * (2 * n) + [pl.BlockSpec(memory_space=pl.ANY)] * len(after),
        out_specs=(_SEM, _SEM, _SEM, *[_HBM] * (2 * n), pl.BlockSpec(memory_space=pltpu.VMEM)),
        input_output_aliases={i: 3 + i for i in range(2 * n)},
        compiler_params=pltpu.CompilerParams(has_side_effects=pltpu.SideEffectType.DATAFLOW_SIDE_EFFECTING),
    )(*[pltpu.with_memory_space_constraint(a, pltpu.HBM) for a in list(arrs) + lands], *after)
    return (n, scatter, res[:3], res[3:3 + n], res[3 + n:3 + 2 * n], peers), res[-1]


def _exchange_wait(state, after, name):
    n, scatter, sems, srcs, lands, peers = state
    after = list(after) if isinstance(after, (list, tuple)) else [after]

    def body(*refs):
        ins, outs = refs[:n], refs[n:2 * n]
        send_sems, recv_sems, local_sems = refs[2 * n:2 * n + 3]
        local, remote = _exchange_copies(ins, outs, send_sems, recv_sems, local_sems, scatter, peers)
        for cp in remote:
            cp.wait_send()
            cp.wait_recv()
        for cp in local:
            cp.wait()

    hbm = lambda a: pltpu.HBM(a.shape, a.dtype)
    res = pl.pallas_call(
        body, name=name,
        out_shape=(*[hbm(a) for a in srcs], *[hbm(a) for a in lands]),
        in_specs=[_HBM] * (2 * n) + [_SEM] * 3 + [pl.BlockSpec(memory_space=pl.ANY)] * len(after),
        out_specs=tuple([_HBM] * (2 * n)),
        input_output_aliases={i: i for i in range(2 * n)},
        compiler_params=pltpu.CompilerParams(has_side_effects=pltpu.SideEffectType.DATAFLOW_SIDE_EFFECTING),
    )(*srcs, *lands, *sems, *after)
    return list(res[n:])


def _forward_to_sibling(slots, name):
    n = len(slots)
    hops = (2, 4, 6)

    def body(*refs):
        ins, outs, (send_sems, recv_sems) = refs[:n], refs[n:2 * n], refs[2 * n:]
        x, y, c = (lax.axis_index(a) for a in MESH_AXES)
        copies = []
        for a in range(n):
            for i, k in enumerate(hops):
                slot = 4 * (1 - x if k & 4 else x) + 2 * (1 - y if k & 2 else y) + c
                copies.append(pltpu.make_async_remote_copy(
                    src_ref=ins[a].at[slot], dst_ref=outs[a].at[slot],
                    send_sem=send_sems.at[a * len(hops) + i], recv_sem=recv_sems.at[a * len(hops) + i],
                    device_id=(x, y, 1 - c), device_id_type=pl.DeviceIdType.MESH))
        for cp in copies:
            cp.start()
        for cp in copies:
            cp.wait_send()
            cp.wait_recv()

    return pl.pallas_call(
        body, name=name, out_shape=[jax.ShapeDtypeStruct(s.shape, s.dtype) for s in slots],
        in_specs=[pl.BlockSpec(memory_space=pl.ANY)] * n, out_specs=[pl.BlockSpec(memory_space=pl.ANY)] * n,
        input_output_aliases={i: i for i in range(n)},
        scratch_shapes=[pltpu.SemaphoreType.DMA((n * len(hops),)), pltpu.SemaphoreType.DMA((n * len(hops),))],
    )(*slots)


def _adamw_math(w, g, m, v):
    m = ADAM_B1 * m + (1.0 - ADAM_B1) * g
    v = ADAM_B2 * v + (1.0 - ADAM_B2) * (g * g)
    m_hat = m / (1.0 - ADAM_B1 ** ADAM_STEP)
    v_hat = v / (1.0 - ADAM_B2 ** ADAM_STEP)
    return -ADAM_LR * (m_hat / (jnp.sqrt(v_hat) + ADAM_EPS) + ADAM_WD * w), m, v


def _slot_sum(p_ref):
    g = p_ref[0].astype(F32)
    for d in range(1, p_ref.shape[0]):
        g = g + p_ref[d].astype(F32)
    return g


def _adamw_tile(r, c):
    return _tile(r, max(SUBLANES, (256 * 1024) // c // SUBLANES * SUBLANES), SUBLANES)


def _adamw(parts, w, m, v, name):
    r, c = w.shape[-2:]
    by_cols = r % SUBLANES != 0
    tr, tc = (r, _tile(c, 256)) if by_cols else (_adamw_tile(r, c), c)

    def body(p_ref, w_ref, m_ref, v_ref, g_ref, d_ref, nm_ref, nv_ref):
        g = _slot_sum(p_ref)
        g_ref[...] = g
        d_ref[...], nm_ref[...], nv_ref[...] = _adamw_math(w_ref[...], g, m_ref[...], v_ref[...])

    pos = (lambda i: (0, i)) if by_cols else (lambda i: (i, 0))
    if w.ndim == 3:
        blk = pl.BlockSpec((None, tr, tc), lambda i: (0,) + pos(i))
    else:
        blk = pl.BlockSpec((tr, tc), pos)
    sh = jax.ShapeDtypeStruct(w.shape, F32)
    return pl.pallas_call(
        body, name=name, grid=(c // tc if by_cols else r // tr,),
        in_specs=[pl.BlockSpec((parts.shape[0], tr, tc), lambda i: (0,) + pos(i)), blk, blk, blk],
        out_specs=[blk] * 4, out_shape=[sh] * 4, compiler_params=_cparams("parallel"),
    )(parts, w, m, v)


def _sum_parts(parts, name):
    _, r, c = parts.shape
    tr = _adamw_tile(r, c)

    def body(p_ref, o_ref):
        o_ref[...] = _slot_sum(p_ref)

    return pl.pallas_call(
        body, name=name, grid=(r // tr,),
        in_specs=[pl.BlockSpec((parts.shape[0], tr, c), lambda i: (0, i, 0))],
        out_specs=pl.BlockSpec((tr, c), lambda i: (i, 0)), out_shape=jax.ShapeDtypeStruct((r, c), F32),
        compiler_params=_cparams("parallel"),
    )(parts)


def _lane_pad(a, width=LANES):
    return jnp.pad(a, ((0, 0), (0, width - a.shape[1])))


def _local_step(x, target, norm_pre, norm_post, kv_norm, kv_b_f, a_re, a_im, log_dt, b_re, b_im, c_re, c_im, comm):
    s, d = x.shape
    g, p = a_re.shape
    w = g * S5_GROUP
    fw = d
    nh = fw // HEAD_DIM
    seg_len = s // N_SEG
    row = lambda v: v.reshape(1, -1)
    g_pre0, g_pre1, g_post0, g_post1, g_kv = row(norm_pre[0]), row(norm_pre[1]), row(norm_post[0]), row(norm_post[1]), row(kv_norm)

    ldt = log_dt.reshape(g, 1)
    abr, abi, cr, ci = _s5_disc_fwd(a_re, a_im, ldt)
    cr_col, ci_col = cr.reshape(g * p, 1), ci.reshape(g * p, 1)
    b_re2, b_im2 = b_re.reshape(g * p, S5_GROUP), b_im.reshape(g * p, S5_GROUP)
    bb_re, bb_im = _s5_bbar_fwd(cr_col, ci_col, b_re2, b_im2)
    bd_re = _block_diag_in(bb_re.reshape(g, p, S5_GROUP)).astype(BF16)
    bd_im = _block_diag_in(bb_im.reshape(g, p, S5_GROUP)).astype(BF16)
    cd_re = _block_diag_out(c_re).astype(BF16)
    cd_im = _block_diag_out(-c_im).astype(BF16)
    ab_re = jnp.broadcast_to(abr.reshape(1, g * p), (N_SEG, g * p))
    ab_im = jnp.broadcast_to(abi.reshape(1, g * p), (N_SEG, g * p))
    zero_seg = jnp.zeros((N_SEG, g * p), F32)

    xn0 = _norm_cast(x, g_pre0 + comm.token, "norm_pre0", x_kind="nat")
    w_in = comm.weight("s5_w_in", [xn0, bd_re, bd_im, cd_re, cd_im, ab_re, ab_im])
    d_row, bglu_row = row(comm.vector("s5_d")), row(comm.vector("s5_b_glu"))
    u = _mm(xn0, w_in, "nn", BF16, "s5_in_u", b_cols=(0, w), b_slots=True)
    z0 = _mm(xn0, w_in, "nn", BF16, "s5_in_z", b_cols=(w, w), b_slots=True)
    e_re, e_im = _s5_scan_fwd(u, bd_re, bd_im, cd_re, cd_im, ab_re, ab_im, zero_seg, zero_seg, d_row, False, "s5_scan_ends")
    i_re, i_im = _s5_seg_fix(e_re, e_im, ab_re, ab_im, seg_len, False, "s5_seg_fix")
    y_ssm, yg, h_re, h_im, _, _ = _s5_scan_fwd(u, bd_re, bd_im, cd_re, cd_im, ab_re, ab_im, i_re, i_im, d_row, True, "s5_scan")
    w_glu, w_out = comm.weight("s5_w_glu", yg), comm.weight("s5_w_out", yg)
    gp = _mm(yg, w_glu, "nn", BF16, "s5_glu")
    y3 = _s5_gate(y_ssm, gp, bglu_row, z0, "s5_gate")
    w_kvt, fw_in = comm.weight("kv_w", y3), comm.weight("fox_w_in", y3)
    w_ft = jnp.pad(w_kvt[2 * fw:], ((0, LANES - nh), (0, 0)))
    o0 = _mm(y3, w_out, "nn", F32, "s5_out")

    h1, hn_kv, xn1 = _resid_norm2(x, o0, g_post0 + comm.late_token, g_kv, g_pre1, "resid_norms")
    kv = _mm(hn_kv, w_kvt, "nt", BF16, "kv_proj", b_rows=2 * fw)
    f_logit = _mm(hn_kv, w_ft, "nt", F32, "f_proj")
    bf_row = _lane_pad(row(kv_b_f))
    cum2 = _cum_fwd(f_logit, bf_row, "cum_fwd")
    cum2_t = cum2[:, :nh].T.reshape(nh, 1, s)
    q2 = _mm(xn1, fw_in, "nn", BF16, "fox_q", scale=HEAD_DIM ** -0.5 * LOG2E, b_cols=(0, fw), b_slots=True)
    z1 = _mm(xn1, fw_in, "nn", BF16, "fox_z", b_cols=(fw, fw), b_slots=True)
    o, oz, lse2_t = _fox_fwd(q2, kv, cum2_t, z1, "fox_fwd")
    fw_out = comm.weight("fox_w_out", oz)
    o1 = _mm(oz, fw_out, "nn", F32, "fox_out")
    dh2, do1, sq, dg_post1 = _post_norm_loss(o1, g_post1, h1, target, "norm_post1_loss")
    loss = 0.5 * jnp.sum(sq) / d

    d_fw_out = _mm(oz, do1, "tn", BF16, "fox_out_dw")
    d_oz = _mm(do1, fw_out, "nt", BF16, "fox_out_dx")
    do, dqz = _gate_bwd(d_oz, o, z1, "fox_gate_bwd")
    dk, dv, dqz, dcq, dck = _fox_bwd(q2, kv, do, o, lse2_t, cum2, dqz, "fox_bwd")
    d_fw_in = _mm(xn1, dqz, "tn", BF16, "fox_in_dw", col_slots=True)
    dxn1 = _mm(dqz, fw_in, "nt", BF16, "fox_in_dx", b_slots=True)
    dcq_sl = _lane_pad(dcq.reshape(nh, s).T)
    dck_sl = _lane_pad(dck.reshape(nh, s).T)
    df, db_f = _cum_bwd(dcq_sl, dck_sl, f_logit, bf_row, "cum_bwd")
    dkv = _concat_cast(dk, dv, "fox_dkv")
    d_w_kvmt = _mm(dkv, hn_kv, "tn", BF16, "kv_dw")
    d_w_ft = _mm(df, hn_kv, "tn", BF16, "f_dw")
    dhn_f = _mm(df, w_ft, "nn", F32, "f_dx")
    dhn_kv = _mm(dkv, w_kvt, "nn", BF16, "kv_dx", add=dhn_f, b_rows=2 * fw)
    d_w_kvt = jnp.concatenate([d_w_kvmt, d_w_ft[:nh]], axis=0)
    tok = comm.send_grads(dict(fox_w_out=d_fw_out, fox_w_in=d_fw_in, kv_w=d_w_kvt), "exchange_fox")
    dh1, do0, dg_pre1, dg_kv, dg_post0 = _norm_bwd2(dh2, h1, dxn1, dhn_kv, g_pre1, g_kv, o0, g_post0 + tok[0, 0],
                                                      "resid_norms_bwd")

    d_w_out = _mm(y3, do0, "tn", BF16, "s5_out_dw")
    dy3 = _mm(do0, w_out, "nt", BF16, "s5_out_dx")
    duz, dgp, dyg_direct, db_glu = _s5_gate_bwd(dy3, y_ssm, gp, bglu_row, z0, "s5_gate_bwd")
    d_w_glu = _mm(yg, dgp, "tn", BF16, "s5_glu_dw")
    gelu_bwd = lambda dyg, y: jax.vjp(jax.nn.gelu, y)[1](dyg)[0]
    dy_ssm = _mm(dgp, w_glu, "nt", F32, "s5_glu_dx", add=dyg_direct, epilogue=(gelu_bwd, y_ssm))
    d_row = d_row + comm.send_grads(dict(s5_w_out=d_w_out, s5_w_glu=d_w_glu), "exchange_s5")[0, 0]
    ab_imn = -ab_im
    ge_re, ge_im = _s5_scan_bwd(dy_ssm, u, h_re, h_im, bd_re, bd_im, cd_re, cd_im, ab_re, ab_imn, zero_seg, zero_seg,
                                d_row, False, "s5_adj_ends")
    gi_re, gi_im = _s5_seg_fix(ge_re, ge_im, ab_re, ab_imn, seg_len, True, "s5_adj_fix")
    duz, dbd_re, dbd_im, dcd_re, dcd_im, dab_re, dab_im, dd = _s5_scan_bwd(
        dy_ssm, u, h_re, h_im, bd_re, bd_im, cd_re, cd_im, ab_re, ab_imn, gi_re, gi_im, d_row, True, "s5_adj", duz=duz)
    d_w_in = _mm(xn0, duz, "tn", BF16, "s5_in_dw", col_slots=True)
    tok = comm.send_grads(dict(s5_w_in=d_w_in), "exchange_s5_in")
    dxn0 = _mm(duz, w_in, "nt", BF16, "s5_in_dx", after=tok, b_slots=True)
    grad_x, dg_pre0 = _norm_bwd1(dh1, x, dxn0, g_pre0, "norm_pre0_bwd")

    dbb_re = _block_diag_in_extract(dbd_re, p, S5_GROUP).reshape(g * p, S5_GROUP)
    dbb_im = _block_diag_in_extract(dbd_im, p, S5_GROUP).reshape(g * p, S5_GROUP)
    dcr_col, dci_col, db_re, db_im = _s5_bbar_bwd(cr_col, ci_col, b_re2, b_im2, dbb_re, dbb_im)
    da_re, da_im, dldt = _s5_disc_bwd(a_re, a_im, ldt, dab_re.reshape(g, p), dab_im.reshape(g, p),
                                      dcr_col.reshape(g, p), dci_col.reshape(g, p))
    dc_re = _block_diag_out_extract(dcd_re, S5_GROUP, p)
    dc_im = -_block_diag_out_extract(dcd_im, S5_GROUP, p)

    small = dict(
        norm_pre=jnp.concatenate([dg_pre0, dg_pre1], axis=0), norm_post=jnp.concatenate([dg_post0, dg_post1], axis=0),
        s5_a_re=da_re, s5_a_im=da_im, s5_log_dt=dldt.reshape(g), s5_b_re=db_re.reshape(g, p, S5_GROUP),
        s5_b_im=db_im.reshape(g, p, S5_GROUP), s5_c_re=dc_re, s5_c_im=dc_im, s5_d=dd.reshape(-1),
        s5_b_glu=db_glu.reshape(-1), kv_norm=dg_kv.reshape(-1), kv_b_f=db_f[0, :nh])
    return loss, grad_x, small


_BIG = ("s5_w_in", "s5_w_glu", "s5_w_out", "kv_w", "fox_w_in", "fox_w_out")
_COL_SHARDED = ("s5_w_in", "fox_w_in")
_SMALL = ("norm_pre", "norm_post", "s5_a_re", "s5_a_im", "s5_log_dt", "s5_b_re", "s5_b_im", "s5_c_re", "s5_c_im",
          "s5_d", "s5_b_glu", "kv_norm", "kv_b_f")
_SMALL_SHARDED = ("s5_d", "s5_b_glu")
_PACK_QUANTUM = SUBLANES * LANES
_WEIGHTS = ('norm_pre', 'norm_post', 's5_w_in', 's5_a_re', 's5_a_im', 's5_log_dt', 's5_b_re', 's5_b_im', 's5_c_re', 's5_c_im',
            's5_d', 's5_w_glu', 's5_b_glu', 's5_w_out', 'kv_norm', 'kv_w', 'kv_b_f', 'fox_w_in', 'fox_w_out')


def _full_from_slots(name, slots):
    n, r, c = slots.shape
    if name in _COL_SHARDED:
        return slots.transpose(1, 0, 2).reshape(r, n * c)
    return slots.reshape(n * r, c)


def _slots_from_full(name, full):
    if name in _COL_SHARDED:
        r, nc = full.shape
        return full.reshape(r, N_DEV, nc // N_DEV).transpose(1, 0, 2)
    nr, c = full.shape
    return full.reshape(N_DEV, nr // N_DEV, c)


def _groups_last(shape):
    return len(shape) >= 3 and shape[-1] < LANES and shape[-3] % LANES == 0


def _pack(vals):
    parts = []
    for v in vals:
        flat = jnp.moveaxis(v, -3, -1).reshape(-1) if _groups_last(v.shape) else v.reshape(-1)
        parts.append(jnp.pad(flat, (0, (-flat.shape[0]) % _PACK_QUANTUM)))
    total = sum(p.shape[0] for p in parts)
    parts.append(jnp.zeros(((-total) % (N_DEV * _PACK_QUANTUM),), F32))
    return jnp.concatenate(parts).reshape(-1, LANES)


def _unpack(packed, shapes):
    flat = packed.reshape(-1)
    out, off = [], 0
    for sh in shapes:
        n = math.prod(sh)
        piece = flat[off:off + n]
        if _groups_last(sh):
            piece = jnp.moveaxis(piece.reshape(sh[:-3] + sh[-2:] + sh[-3:-2]), -1, -3)
        out.append(piece.reshape(sh))
        off += n + (-n) % _PACK_QUANTUM
    return out


class _Comm:
    _GROUPS = (("s5_w_in",) + _SMALL_SHARDED, ("s5_w_glu", "s5_w_out"), ("kv_w", "fox_w_in"), ("fox_w_out",))
    _SLOT_FORM = ("s5_w_in", "fox_w_in")

    def __init__(self, shards, vectors, early=()):
        self._shards = {**shards, **vectors}
        self._full, self._gathers = {}, {}
        self._early = list(early)
        self.token = jnp.zeros((), F32)
        for group in self._GROUPS[:-1]:
            self.token = self.token + self._start(group, ())[0, 0]
        self.late_token = None
        self._sent = []

    def _start(self, group, after):
        state, tok = _exchange_start([self._shards[n] for n in group], False, "gather_start_" + group[0], after,
                                     peers=_CHIP_PEERS)
        self._gathers[group] = state
        return tok

    def vector(self, name):
        return self._full[name]

    def weight(self, name, after):
        if name not in self._full:
            group = next(g for g in self._GROUPS if name in g)
            if group == self._GROUPS[0]:
                after = (list(after) if isinstance(after, (list, tuple)) else [after]) + self._early
            slots = _exchange_wait(self._gathers.pop(group), after, "gather_wait_" + group[0])
            slots = _forward_to_sibling(slots, "gather_forward_" + group[0])
            for n, sl in zip(group, slots):
                if n in _SMALL_SHARDED:
                    self._full[n] = sl.reshape(-1)
                else:
                    self._full[n] = sl if n in self._SLOT_FORM else _full_from_slots(n, sl)
            if group == self._GROUPS[-2]:
                self.late_token = self._start(self._GROUPS[-1], [slots[0]])[0, 0]
        return self._full[name]

    def send_grads(self, grads, name):
        names = list(grads)
        slots = [grads[n] if grads[n].ndim == 3 else _slots_from_full(n, grads[n]).astype(BF16) for n in names]
        state, tok = _exchange_start(slots, True, name + "_start")
        self._sent.append((names, state, name + "_wait"))
        return tok

    def received_grads(self, group, after):
        names, state, name = self._sent[group]
        return list(zip(names, _exchange_wait(state, after, name)))


def kernel(x, norm_pre, norm_post, s5_w_in, s5_a_re, s5_a_im, s5_log_dt, s5_b_re, s5_b_im, s5_c_re, s5_c_im, s5_d, s5_w_glu, s5_b_glu, s5_w_out, kv_norm, kv_w, kv_b_f, fox_w_in, fox_w_out, loss_target, m_norm_pre, m_norm_post, m_s5_w_in, m_s5_a_re, m_s5_a_im, m_s5_log_dt, m_s5_b_re, m_s5_b_im, m_s5_c_re, m_s5_c_im, m_s5_d, m_s5_w_glu, m_s5_b_glu, m_s5_w_out, m_kv_norm, m_kv_w, m_kv_b_f, m_fox_w_in, m_fox_w_out, v_norm_pre, v_norm_post, v_s5_w_in, v_s5_a_re, v_s5_a_im, v_s5_log_dt, v_s5_b_re, v_s5_b_im, v_s5_c_re, v_s5_c_im, v_s5_d, v_s5_w_glu, v_s5_b_glu, v_s5_w_out, v_kv_norm, v_kv_w, v_kv_b_f, v_fox_w_in, v_fox_w_out):
    env = dict(locals())
    wts = {n: env[n] for n in _WEIGHTS}
    mom = {n: env["m_" + n] for n in _WEIGHTS}
    var = {n: env["v_" + n] for n in _WEIGHTS}
    me = 4 * lax.axis_index("x") + 2 * lax.axis_index("y") + lax.axis_index("c")
    shard2d = {n: (wts[n].T if n == "kv_w" else wts[n].reshape(wts[n].shape[-2:])) for n in _BIG}
    full_shape = {n: ((wts[n].size * N_DEV,) if n in _SMALL_SHARDED else wts[n].shape) for n in _SMALL}

    def spread(n, v):
        if n not in _SMALL_SHARDED:
            return v
        flat = v.reshape(-1)
        return lax.dynamic_update_slice(jnp.zeros(full_shape[n], F32), flat, (me * flat.shape[0],))

    packed = [_pack([spread(n, src[n]) for n in _SMALL] + [jnp.zeros((1,), F32)]) for src in (wts, mom, var)]
    comm = _Comm({n: _cast_bf16(shard2d[n], "cast_" + n) for n in _BIG}, {n: wts[n].reshape(1, -1) for n in _SMALL_SHARDED}, packed)

    loss_local, grad_x, small = _local_step(
        x[0], loss_target[0], norm_pre, norm_post, kv_norm, kv_b_f, s5_a_re[0], s5_a_im[0], s5_log_dt[0],
        s5_b_re[0], s5_b_im[0], s5_c_re[0], s5_c_im[0], comm)

    small_pack = _pack([small[n] for n in _SMALL] + [loss_local.reshape(1)])
    slice_rows = small_pack.shape[0] // N_DEV
    small_state, small_tok = _exchange_start([small_pack.reshape(N_DEV, slice_rows, LANES)], True, "reduce_small_start")

    res = {}

    def finish(group, after):
        for n, recv in comm.received_grads(group, after):
            if n == "kv_w":
                res[n] = [o.T for o in _adamw(recv, wts[n].T, mom[n].T, var[n].T, "adamw_" + n)]
            else:
                res[n] = _adamw(recv, wts[n], mom[n], var[n], "adamw_" + n)

    finish(0, [small_tok, grad_x])
    my_sum = _sum_parts(_exchange_wait(small_state, res["kv_w"][0], "reduce_small_wait")[0], "sum_small")
    gather_state, gather_tok = _exchange_start([my_sum], False, "gather_small_start")
    finish(1, gather_tok)
    finish(2, gather_tok)
    g_all = _exchange_wait(gather_state, res["s5_w_in"][0], "gather_small_wait")[0].reshape(1, small_pack.shape[0], LANES)
    outs = _adamw(g_all, *packed, "adamw_small")
    unpacked = [_unpack(o, [full_shape[n] for n in _SMALL] + [(1,)]) for o in outs]
    loss = unpacked[0][-1][0]
    for i, n in enumerate(_SMALL):
        vals = [u[i] for u in unpacked]
        if n in _SMALL_SHARDED:
            k = wts[n].size
            vals = [lax.dynamic_slice(v, (me * k,), (k,)) for v in vals]
        res[n] = [v.reshape(wts[n].shape) for v in vals]

    return (loss, grad_x[None], *[res[n][0] for n in _WEIGHTS], *[res[n][1] for n in _WEIGHTS],
            *[res[n][2] for n in _WEIGHTS], *[res[n][3] for n in _WEIGHTS])
```

```python
import math

import jax
import jax.numpy as jnp
from jax import lax
from jax.experimental import pallas as pl
from jax.experimental.pallas import tpu as pltpu

F32 = jnp.float32
BF16 = jnp.bfloat16

N_DEV = 8
MESH_AXES = ("x", "y", "c")
S5_GROUP = 16
S5_STATE = 64
LANES = 128
SUBLANES = 8
GROUPS_PER_BLOCK = LANES // S5_GROUP
BLOCK_STATE = GROUPS_PER_BLOCK * S5_STATE
N_SEG = SUBLANES
HEAD_DIM = 128
RMS_EPS = 1e-6
NEG_INF = -1e30
LOG2E = math.log2(math.e)
ADAM_LR = 0.001
ADAM_B1 = 0.9
ADAM_B2 = 0.999
ADAM_EPS = 1e-08
ADAM_WD = 0.01
ADAM_STEP = 10
VMEM_LIMIT = 56 * 1024 * 1024


def _tile(n, pref, quantum=LANES):
    if n <= pref:
        return n
    t = (pref // quantum) * quantum
    while t >= quantum:
        if n % t == 0:
            return t
        t -= quantum
    return n


def _cparams(*sem):
    return pltpu.CompilerParams(dimension_semantics=sem if sem else None, vmem_limit_bytes=VMEM_LIMIT)


_DOT_DIMS = {"nn": ((1,), (0,)), "nt": ((1,), (1,)), "tn": ((0,), (0,))}


def _mm(a, b, mode, out_dtype, name, add=None, scale=None, b_cols=None, after=None, col_slots=False, b_slots=False,
        b_rows=None, epilogue=None):
    slot_w = b.shape[2] if b_slots else None
    b2d = (b.shape[1], b.shape[0] * b.shape[2]) if b_slots else b.shape
    b_shape = b2d if b_cols is None else (b2d[0], b_cols[1])
    if b_rows is not None:
        b_shape = (b_rows, b_shape[1])
    if mode == "nn":
        (M, K), (K2, N) = a.shape, b_shape
    elif mode == "nt":
        (M, K), (N, K2) = a.shape, b_shape
    else:
        (K, M), (K2, N) = a.shape, b_shape
    assert K == K2, (name, a.shape, b_shape)
    tm, tn, tk = _tile(M, 1024 if K <= 2048 else 512), (N // N_DEV if col_slots else _tile(N, 1024)), _tile(K, 4096)
    if b_slots and mode == "nn":
        tn = slot_w
    nk = K // tk
    dims = (_DOT_DIMS[mode], ((), ()))
    col0 = 0
    if b_cols is not None:
        assert mode != "tn" and b_cols[0] % (tn if mode == "nn" else tk) == 0
        col0 = b_cols[0] // (tn if mode == "nn" else tk)
    assert not b_slots or (mode == "nn" or (mode == "nt" and nk == 1 and b_cols is None))

    def body(*refs):
        a_ref, b_ref = refs[:2]
        c_ref = refs[2] if add is not None else None
        e_ref = refs[2 + (add is not None)] if epilogue is not None else None
        o_ref = refs[2 + (add is not None) + (epilogue is not None) + (after is not None)]
        if b_slots and mode == "nt":
            part = lax.dot_general(a_ref[:, :slot_w], b_ref[0], dims, preferred_element_type=F32)
            for sl in range(1, b_ref.shape[0]):
                part += lax.dot_general(a_ref[:, sl * slot_w:(sl + 1) * slot_w], b_ref[sl], dims, preferred_element_type=F32)
        else:
            part = lax.dot_general(a_ref[...], b_ref[...], dims, preferred_element_type=F32)

        def finish(r):
            if scale is not None:
                r = r * scale
            if add is not None:
                r = r + c_ref[...]
            if epilogue is not None:
                r = epilogue[0](r, e_ref[...])
            o_ref[...] = r.astype(out_dtype)

        if nk == 1:
            finish(part)
            return
        acc = refs[-1]
        k = pl.program_id(2)

        @pl.when(k == 0)
        def _():
            acc[...] = part

        @pl.when(jnp.logical_and(k > 0, k < nk - 1))
        def _():
            acc[...] += part

        @pl.when(k == nk - 1)
        def _():
            finish(acc[...] + part)

    if mode == "tn":
        a_spec = pl.BlockSpec((tk, tm), lambda i, j, k: (k, i))
    else:
        a_spec = pl.BlockSpec((tm, tk), lambda i, j, k: (i, k))
    if b_slots and mode == "nn":
        b_spec = pl.BlockSpec((None, tk, tn), lambda i, j, k: (j + col0, k, 0))
    elif b_slots:
        b_spec = pl.BlockSpec((b.shape[0], tn, slot_w), lambda i, j, k: (0, j, 0))
    elif mode == "nt":
        b_spec = pl.BlockSpec((tn, tk), lambda i, j, k: (j, k + col0))
    else:
        b_spec = pl.BlockSpec((tk, tn), lambda i, j, k: (k, j + col0))
    o_spec = pl.BlockSpec((tm, tn), lambda i, j, k: (i, j))
    in_specs = [a_spec, b_spec] + ([o_spec] if add is not None else [])
    args = (a, b) + ((add,) if add is not None else ())
    if epilogue is not None:
        in_specs.append(o_spec)
        args += (epilogue[1],)
    if after is not None:
        in_specs.append(pl.BlockSpec(after.shape, lambda i, j, k: (0, 0)))
        args += (after,)
    out_shape = jax.ShapeDtypeStruct((M, N), out_dtype)
    if col_slots:
        assert add is None
        o_spec = pl.BlockSpec((None, tm, tn), lambda i, j, k: (j, i, 0))
        out_shape = jax.ShapeDtypeStruct((N_DEV, M, tn), out_dtype)
    return pl.pallas_call(
        body, name=name, grid=(M // tm, N // tn, nk),
        in_specs=in_specs, out_specs=o_spec,
        out_shape=out_shape,
        scratch_shapes=[pltpu.VMEM((tm, tn), F32)] if nk > 1 else [],
        compiler_params=_cparams("parallel", "parallel", "arbitrary"),
    )(*args)


class _NatIn:
    def __init__(self, ref):
        self.ref = ref

    def __getitem__(self, idx):
        v = jnp.swapaxes(self.ref[...], 0, 1)
        return v.reshape(v.shape[0] * N_SEG, v.shape[2])


class _NatOut:
    def __init__(self, ref):
        self.ref = ref

    def __setitem__(self, idx, val):
        self.ref[...] = jnp.swapaxes(val.reshape(val.shape[0] // N_SEG, N_SEG, val.shape[1]), 0, 1)


def _rowcall(body, name, n_rows, ins, outs, tile_rows=256):
    tr = _tile(n_rows, tile_rows, SUBLANES * 2)
    n_in = len(ins)
    in_kinds = [k for _, k in ins]
    kinds = [k for _, _, k in outs]

    def kern(*refs):
        @pl.when(pl.program_id(0) == 0)
        def _():
            for r, kind in zip(refs[n_in:], kinds):
                if kind == "acc":
                    r[...] = jnp.zeros_like(r)

        wrapped = [_NatIn(r) if k == "nat" else r for r, k in zip(refs[:n_in], in_kinds)]
        wrapped += [_NatOut(r) if k == "nat" else r for r, k in zip(refs[n_in:], kinds)]
        body(*wrapped)

    in_specs, args = [], []
    for arr, kind in ins:
        if kind == "row":
            in_specs.append(pl.BlockSpec((tr, arr.shape[1]), lambda i: (i, 0)))
        elif kind == "nat":
            in_specs.append(pl.BlockSpec((N_SEG, tr // N_SEG, arr.shape[1]), lambda i: (0, i, 0)))
            arr = arr.reshape(N_SEG, n_rows // N_SEG, arr.shape[1])
        else:
            in_specs.append(pl.BlockSpec(arr.shape, lambda i, nd=arr.ndim: (0,) * nd))
        args.append(arr)
    out_specs, out_shape = [], []
    for width, dtype, kind in outs:
        if kind == "row":
            out_specs.append(pl.BlockSpec((tr, width), lambda i: (i, 0)))
            out_shape.append(jax.ShapeDtypeStruct((n_rows, width), dtype))
        elif kind == "right":
            out_specs.append(pl.BlockSpec((tr, width), lambda i: (i, 1)))
            out_shape.append(jax.ShapeDtypeStruct((n_rows, 2 * width), dtype))
        elif kind == "nat":
            out_specs.append(pl.BlockSpec((N_SEG, tr // N_SEG, width), lambda i: (0, i, 0)))
            out_shape.append(jax.ShapeDtypeStruct((N_SEG, n_rows // N_SEG, width), dtype))
        else:
            out_specs.append(pl.BlockSpec((1, width), lambda i: (0, 0)))
            out_shape.append(jax.ShapeDtypeStruct((1, width), F32))
    res = pl.pallas_call(
        kern, name=name, grid=(n_rows // tr,), in_specs=in_specs, out_specs=out_specs, out_shape=out_shape,
        compiler_params=_cparams("arbitrary"),
    )(*args)
    return [r.reshape(n_rows, r.shape[2]) if k == "nat" else r for r, k in zip(res, kinds)]


def _rstd(x):
    return lax.rsqrt(jnp.mean(x * x, axis=-1, keepdims=True) + RMS_EPS)


def _rms_bwd(x, g, dy):
    xh = x * _rstd(x)
    dxh = dy * g
    dx = _rstd(x) * (dxh - xh * jnp.mean(dxh * xh, axis=-1, keepdims=True))
    return dx, jnp.sum(dy * xh, axis=0, keepdims=True)


def _silu(z):
    return z * jax.nn.sigmoid(z)


def _norm_cast(x, g, name, x_kind="row"):
    def body(x_ref, g_ref, o_ref):
        x = x_ref[...]
        o_ref[...] = (x * _rstd(x) * g_ref[...]).astype(BF16)

    return _rowcall(body, name, x.shape[0], [(x, x_kind), (g, "full")], [(x.shape[1], BF16, "row")])[0]


def _resid_norm2(x, o, g_post, g_kv, g_pre, name):
    def body(x_ref, o_ref, go_ref, gk_ref, gp_ref, h_ref, nk_ref, np_ref):
        o = o_ref[...]
        h = x_ref[...] + o * _rstd(o) * go_ref[...]
        h_ref[...] = h
        hn = h * _rstd(h)
        nk_ref[...] = (hn * gk_ref[...]).astype(BF16)
        np_ref[...] = (hn * gp_ref[...]).astype(BF16)

    d = x.shape[1]
    return _rowcall(body, name, x.shape[0], [(x, "nat"), (o, "row"), (g_post, "full"), (g_kv, "full"), (g_pre, "full")],
                    [(d, F32, "nat"), (d, BF16, "nat"), (d, BF16, "nat")])


def _post_norm_loss(o, g, h1, target, name):
    d = o.shape[1]

    def body(o_ref, g_ref, h_ref, t_ref, dh_ref, do_ref, acc_ref, dg_ref):
        o = o_ref[...]
        e = h_ref[...] + o * _rstd(o) * g_ref[...] - t_ref[...]
        dh = e * (1.0 / d)
        dh_ref[...] = dh
        acc_ref[...] += jnp.sum(e * e, axis=0, keepdims=True)
        dx, dg = _rms_bwd(o, g_ref[...], dh)
        do_ref[...] = dx.astype(BF16)
        dg_ref[...] += dg

    return _rowcall(body, name, o.shape[0], [(o, "row"), (g, "full"), (h1, "row"), (target, "row")],
                    [(d, F32, "row"), (d, BF16, "row"), (d, F32, "acc"), (d, F32, "acc")])


def _gate_bwd(d_oz, o, z, name):
    def body(d_ref, o_ref, z_ref, do_ref, dz_ref):
        _, vjp = jax.vjp(lambda o, z: o * _silu(z), o_ref[...], z_ref[...].astype(F32))
        do, dz = vjp(d_ref[...].astype(F32))
        do_ref[...] = do.astype(BF16)
        dz_ref[...] = dz.astype(BF16)

    w = o.shape[1]
    return _rowcall(body, name, o.shape[0], [(d_oz, "row"), (o, "row"), (z, "row")], [(w, BF16, "row"), (w, BF16, "right")])


def _norm_bwd2(dh2, h1, dxn1, dhn_kv, g_pre, g_kv, o0, g_post0, name):
    def body(dh2_ref, h_ref, d1_ref, dk_ref, gp_ref, gk_ref, o_ref, go_ref, dh1_ref, do_ref, dgp_ref, dgk_ref, dgo_ref):
        h = h_ref[...]
        dx1, dg1 = _rms_bwd(h, gp_ref[...], d1_ref[...].astype(F32))
        dxk, dgk = _rms_bwd(h, gk_ref[...], dk_ref[...].astype(F32))
        dh1 = dh2_ref[...] + dx1 + dxk
        dh1_ref[...] = dh1
        dgp_ref[...] += dg1
        dgk_ref[...] += dgk
        dxo, dgo = _rms_bwd(o_ref[...], go_ref[...], dh1)
        do_ref[...] = dxo.astype(BF16)
        dgo_ref[...] += dgo

    d = h1.shape[1]
    return _rowcall(body, name, h1.shape[0],
                    [(dh2, "nat"), (h1, "nat"), (dxn1, "nat"), (dhn_kv, "nat"), (g_pre, "full"), (g_kv, "full"),
                     (o0, "row"), (g_post0, "full")],
                    [(d, F32, "nat"), (d, BF16, "row"), (d, F32, "acc"), (d, F32, "acc"), (d, F32, "acc")])


def _norm_bwd1(dres, x, dxn, g, name):
    def body(dr_ref, x_ref, dn_ref, g_ref, dx_ref, dg_ref):
        dx, dg = _rms_bwd(x_ref[...], g_ref[...], dn_ref[...].astype(F32))
        dx_ref[...] = dr_ref[...] + dx
        dg_ref[...] += dg

    d = x.shape[1]
    return _rowcall(body, name, x.shape[0], [(dres, "nat"), (x, "nat"), (dxn, "row"), (g, "full")],
                    [(d, F32, "nat"), (d, F32, "acc")])


def _s5_gate(y_ssm, gp, b_glu, z, name):
    def body(y_ref, gp_ref, b_ref, z_ref, o_ref):
        yg = jax.nn.gelu(y_ref[...])
        o_ref[...] = (yg * jax.nn.sigmoid(gp_ref[...] + b_ref[...]) * _silu(z_ref[...].astype(F32))).astype(BF16)

    return _rowcall(body, name, y_ssm.shape[0], [(y_ssm, "row"), (gp, "row"), (b_glu, "full"), (z, "row")],
                    [(y_ssm.shape[1], BF16, "row")])[0]


def _s5_gate_bwd(dy3, y_ssm, gp, b_glu, z, name):
    def body(d_ref, y_ref, gp_ref, b_ref, z_ref, dz_ref, dgp_ref, dyg_ref, db_ref):
        yg = jax.nn.gelu(y_ref[...])
        _, vjp = jax.vjp(lambda yg, gp, z: yg * jax.nn.sigmoid(gp) * _silu(z), yg, gp_ref[...] + b_ref[...],
                         z_ref[...].astype(F32))
        dyg, dgp, dz = vjp(d_ref[...].astype(F32))
        dz_ref[...] = dz.astype(BF16)
        dgp_ref[...] = dgp.astype(BF16)
        dyg_ref[...] = dyg
        db_ref[...] += jnp.sum(dgp, axis=0, keepdims=True)

    w = y_ssm.shape[1]
    return _rowcall(body, name, y_ssm.shape[0],
                    [(dy3, "row"), (y_ssm, "row"), (gp, "row"), (b_glu, "full"), (z, "row")],
                    [(w, BF16, "right"), (w, BF16, "row"), (w, F32, "row"), (w, F32, "acc")])


def _cast_bf16(x, name):
    r, c = x.shape
    by_cols = r % (2 * SUBLANES) != 0
    tr, tc = (r, _tile(c, 256)) if by_cols else (_tile(r, 512, 2 * SUBLANES), c)
    pos = (lambda i: (0, i)) if by_cols else (lambda i: (i, 0))

    def body(x_ref, o_ref):
        o_ref[...] = x_ref[...].astype(BF16)

    return pl.pallas_call(
        body, name=name, grid=(c // tc if by_cols else r // tr,),
        in_specs=[pl.BlockSpec((tr, tc), pos)], out_specs=pl.BlockSpec((tr, tc), pos),
        out_shape=jax.ShapeDtypeStruct((r, c), BF16), compiler_params=_cparams("parallel"),
    )(x)


def _concat_cast(a, b, name):
    def body(a_ref, b_ref, o_ref):
        w = a_ref.shape[1]
        o_ref[:, :w] = a_ref[...].astype(BF16)
        o_ref[:, w:] = b_ref[...].astype(BF16)

    return _rowcall(body, name, a.shape[0], [(a, "row"), (b, "row")], [(a.shape[1] + b.shape[1], BF16, "row")])[0]


def _disc(ar, ai, ldt):
    dt = jnp.exp(ldt)
    mag = jnp.exp(ar * dt)
    abr = mag * jnp.cos(ai * dt)
    abi = mag * jnp.sin(ai * dt)
    den = ar * ar + ai * ai
    nr = abr - 1.0
    return abr, abi, (nr * ar + abi * ai) / den, (abi * ar - nr * ai) / den


def _s5_disc_fwd(a_re, a_im, ldt):
    def body(ar, ai, ld, o1, o2, o3, o4):
        o1[...], o2[...], o3[...], o4[...] = _disc(ar[...], ai[...], ld[...])

    sh = jax.ShapeDtypeStruct(a_re.shape, F32)
    return pl.pallas_call(body, name="s5_disc_fwd", out_shape=(sh, sh, sh, sh))(a_re, a_im, ldt)


def _s5_disc_bwd(a_re, a_im, ldt, d_abr, d_abi, d_cr, d_ci):
    def body(ar, ai, ld, g1, g2, g3, g4, o1, o2, o3):
        _, vjp = jax.vjp(_disc, ar[...], ai[...], ld[...])
        o1[...], o2[...], o3[...] = vjp((g1[...], g2[...], g3[...], g4[...]))

    sh = jax.ShapeDtypeStruct(a_re.shape, F32)
    return pl.pallas_call(body, name="s5_disc_bwd", out_shape=(sh, sh, jax.ShapeDtypeStruct(ldt.shape, F32)))(
        a_re, a_im, ldt, d_abr, d_abi, d_cr, d_ci)


def _bbar(cr, ci, br, bi):
    return cr * br - ci * bi, cr * bi + ci * br


def _s5_bbar_fwd(cr_col, ci_col, b_re, b_im):
    def body(cr, ci, br, bi, o1, o2):
        o1[...], o2[...] = _bbar(cr[...], ci[...], br[...], bi[...])

    w = b_re.shape[1]
    return _rowcall(body, "s5_bbar_fwd", b_re.shape[0], [(cr_col, "row"), (ci_col, "row"), (b_re, "row"), (b_im, "row")],
                    [(w, F32, "row"), (w, F32, "row")], tile_rows=1024)


def _s5_bbar_bwd(cr_col, ci_col, b_re, b_im, d_re, d_im):
    def body(cr, ci, br, bi, g1, g2, o1, o2, o3, o4):
        _, vjp = jax.vjp(_bbar, cr[...], ci[...], br[...], bi[...])
        o1[...], o2[...], o3[...], o4[...] = vjp((g1[...], g2[...]))

    w = b_re.shape[1]
    return _rowcall(body, "s5_bbar_bwd", b_re.shape[0],
                    [(cr_col, "row"), (ci_col, "row"), (b_re, "row"), (b_im, "row"), (d_re, "row"), (d_im, "row")],
                    [(1, F32, "row"), (1, F32, "row"), (w, F32, "row"), (w, F32, "row")], tile_rows=1024)


def _block_diag_in(t):
    g, p, c = t.shape
    nb = g // GROUPS_PER_BLOCK
    t4 = t.reshape(nb, GROUPS_PER_BLOCK, p, c).transpose(0, 1, 3, 2)
    eye = jnp.eye(GROUPS_PER_BLOCK, dtype=t.dtype)
    return (t4[:, :, :, None, :] * eye[None, :, None, :, None]).reshape(nb, GROUPS_PER_BLOCK * c, GROUPS_PER_BLOCK * p)


def _block_diag_in_extract(d, p, c):
    nb = d.shape[0]
    d5 = d.reshape(nb, GROUPS_PER_BLOCK, c, GROUPS_PER_BLOCK, p)
    diag = jnp.stack([d5[:, g, :, g, :] for g in range(GROUPS_PER_BLOCK)], axis=1)
    return diag.transpose(0, 1, 3, 2).reshape(nb * GROUPS_PER_BLOCK, p, c)


def _block_diag_out(t):
    g, c, p = t.shape
    nb = g // GROUPS_PER_BLOCK
    t4 = t.reshape(nb, GROUPS_PER_BLOCK, c, p).transpose(0, 1, 3, 2)
    eye = jnp.eye(GROUPS_PER_BLOCK, dtype=t.dtype)
    return (t4[:, :, :, None, :] * eye[None, :, None, :, None]).reshape(nb, GROUPS_PER_BLOCK * p, GROUPS_PER_BLOCK * c)


def _block_diag_out_extract(d, c, p):
    nb = d.shape[0]
    d5 = d.reshape(nb, GROUPS_PER_BLOCK, p, GROUPS_PER_BLOCK, c)
    diag = jnp.stack([d5[:, g, :, g, :] for g in range(GROUPS_PER_BLOCK)], axis=1)
    return diag.transpose(0, 1, 3, 2).reshape(nb * GROUPS_PER_BLOCK, c, p)


def _scan_step(ar, ai, hr, hi, xr, xi):
    return ar * hr - ai * hi + xr, ar * hi + ai * hr + xi


def _s5_blocks_per_step(nb):
    return 2 if nb % 2 == 0 else 1


def _s5_scan_fwd(u, bd_re, bd_im, cd_re, cd_im, ab_re, ab_im, init_re, init_im, d_row, full, name):
    s, w = u.shape
    nb = w // LANES
    rows = _tile(s, 512, SUBLANES)
    nc = s // rows
    steps = rows // N_SEG
    ns = nb * BLOCK_STATE

    nblk = _s5_blocks_per_step(nb)

    def body(u_ref, bdr, bdi, cdr, cdi, ar_ref, ai_ref, ir_ref, ii_ref, d_ref, *outs):
        if full:
            y_ref, yg_ref, hr_ref, hi_ref, er_ref, ei_ref, cr, ci = outs
        else:
            er_ref, ei_ref, hr_ref, hi_ref, cr, ci = outs
        c = pl.program_id(1)
        cols = lambda b, width: slice(b * width, (b + 1) * width)

        @pl.when(c == 0)
        def _():
            cr[...] = ir_ref[...]
            ci[...] = ii_ref[...]

        for b in range(nblk):
            ub = u_ref[:, cols(b, LANES)].astype(BF16)
            hr_ref[:, cols(b, BLOCK_STATE)] = jnp.dot(ub, bdr[b], preferred_element_type=F32)
            hi_ref[:, cols(b, BLOCK_STATE)] = jnp.dot(ub, bdi[b], preferred_element_type=F32)
        ar, ai = ar_ref[...], ai_ref[...]

        hr, hi = cr[...], ci[...]
        for j in range(steps):
            rows_j = pl.ds(j * N_SEG, N_SEG)
            hr, hi = _scan_step(ar, ai, hr, hi, hr_ref[rows_j, :], hi_ref[rows_j, :])
            hr_ref[rows_j, :] = hr
            hi_ref[rows_j, :] = hi
        cr[...] = hr
        ci[...] = hi
        if full:
            for b in range(nblk):
                st_b, ln_b = cols(b, BLOCK_STATE), cols(b, LANES)
                y = (jnp.dot(hr_ref[:, st_b].astype(BF16), cdr[b], preferred_element_type=F32)
                     + jnp.dot(hi_ref[:, st_b].astype(BF16), cdi[b], preferred_element_type=F32)
                     + d_ref[:, ln_b] * u_ref[:, ln_b])
                y_ref[:, ln_b] = y
                yg_ref[:, ln_b] = jax.nn.gelu(y).astype(BF16)

        @pl.when(c == nc - 1)
        def _():
            er_ref[...] = hr
            ei_ref[...] = hi

    lanes, states = LANES * nblk, BLOCK_STATE * nblk
    blk3 = lambda a: pl.BlockSpec((nblk,) + a.shape[1:], lambda k, c: (k, 0, 0))
    seg = pl.BlockSpec((N_SEG, states), lambda k, c: (0, k))
    st = pl.BlockSpec((rows, states), lambda k, c: (c, k))
    in_specs = [pl.BlockSpec((rows, lanes), lambda k, c: (c, k)), blk3(bd_re), blk3(bd_im), blk3(cd_re), blk3(cd_im),
                seg, seg, seg, seg, pl.BlockSpec((1, lanes), lambda k, c: (0, k))]
    seg_shape = jax.ShapeDtypeStruct((N_SEG, ns), F32)
    st_shape = jax.ShapeDtypeStruct((s, ns), F32)
    carry = [pltpu.VMEM((N_SEG, states), F32)] * 2
    if full:
        ych = pl.BlockSpec((rows, lanes), lambda k, c: (c, k))
        out_specs = [ych, ych, st, st, seg, seg]
        out_shape = [jax.ShapeDtypeStruct((s, w), F32), jax.ShapeDtypeStruct((s, w), BF16), st_shape, st_shape, seg_shape, seg_shape]
        scratch = carry
    else:
        out_specs = [seg, seg]
        out_shape = [seg_shape, seg_shape]
        scratch = [pltpu.VMEM((rows, states), F32)] * 2 + carry
    return pl.pallas_call(
        body, name=name, grid=(nb // nblk, nc), in_specs=in_specs, out_specs=out_specs, out_shape=out_shape,
        scratch_shapes=scratch, compiler_params=_cparams("parallel", "arbitrary"),
    )(u, bd_re, bd_im, cd_re, cd_im, ab_re, ab_im, init_re, init_im, d_row)


def _s5_seg_fix(e_re, e_im, ab_re, ab_im, seg_len, reverse, name):
    assert seg_len & (seg_len - 1) == 0

    def body(er, ei, ar, ai, o_re, o_im):
        pr, pi = ar[0:1, :], ai[0:1, :]
        for _ in range(int(math.log2(seg_len))):
            pr, pi = pr * pr - pi * pi, 2.0 * pr * pi
        tr = jnp.zeros_like(pr)
        ti = jnp.zeros_like(pr)
        order = list(range(N_SEG - 1, -1, -1)) if reverse else list(range(N_SEG))
        for n, sgm in enumerate(order):
            o_re[sgm:sgm + 1, :] = tr
            o_im[sgm:sgm + 1, :] = ti
            if n < N_SEG - 1:
                tr, ti = _scan_step(pr, pi, tr, ti, er[sgm:sgm + 1, :], ei[sgm:sgm + 1, :])

    sh = jax.ShapeDtypeStruct(e_re.shape, F32)
    return pl.pallas_call(body, name=name, out_shape=(sh, sh))(e_re, e_im, ab_re, ab_im)


def _s5_scan_bwd(dy, u, h_re, h_im, bd_re, bd_im, cd_re, cd_im, ab_re, ab_imn, gin_re, gin_im, d_row, full, name, duz=None):
    s, w = u.shape
    nb = w // LANES
    rows = _tile(s, 512, SUBLANES)
    nc = s // rows
    steps = rows // N_SEG
    ns = nb * BLOCK_STATE

    nblk = _s5_blocks_per_step(nb)

    def body(dy_ref, u_ref, hr_ref, hi_ref, bdr, bdi, cdr, cdi, ar_ref, ai_ref, ir_ref, ii_ref, d_ref, *outs):
        if full:
            _, du_ref, dbr_ref, dbi_ref, dcr_ref, dci_ref, dar_ref, dai_ref, dd_ref, gr, gi, accr, acci = outs
        else:
            er_ref, ei_ref, gr, gi = outs
        c = pl.program_id(1)
        cols = lambda b, width: slice(b * width, (b + 1) * width)

        @pl.when(c == 0)
        def _():
            gr[pl.ds(rows, N_SEG), :] = ir_ref[...]
            gi[pl.ds(rows, N_SEG), :] = ii_ref[...]
            if full:
                for r in (dbr_ref, dbi_ref, dcr_ref, dci_ref, dd_ref, accr, acci):
                    r[...] = jnp.zeros_like(r)

        nt = (_DOT_DIMS["nt"], ((), ()))
        tn = (_DOT_DIMS["tn"], ((), ()))
        for b in range(nblk):
            dyb = dy_ref[:, cols(b, LANES)].astype(BF16)
            gr[pl.ds(0, rows), cols(b, BLOCK_STATE)] = lax.dot_general(dyb, cdr[b], nt, preferred_element_type=F32)
            gi[pl.ds(0, rows), cols(b, BLOCK_STATE)] = lax.dot_general(dyb, cdi[b], nt, preferred_element_type=F32)
        ar, ai = ar_ref[...], ai_ref[...]

        g0r, g0i = gr[pl.ds(rows, N_SEG), :], gi[pl.ds(rows, N_SEG), :]
        for j in range(steps - 1, -1, -1):
            rows_j = pl.ds(j * N_SEG, N_SEG)
            g0r, g0i = _scan_step(ar, ai, g0r, g0i, gr[rows_j, :], gi[rows_j, :])
            gr[rows_j, :] = g0r
            gi[rows_j, :] = g0i
        if full:
            for b in range(nblk):
                st_b, ln_b = cols(b, BLOCK_STATE), cols(b, LANES)
                hr, hi = hr_ref[:, st_b], hi_ref[:, st_b]
                gnr, gni = gr[pl.ds(N_SEG, rows), st_b], gi[pl.ds(N_SEG, rows), st_b]
                accr[:, st_b] += jnp.sum((gnr * hr + gni * hi).reshape(steps, N_SEG, BLOCK_STATE), axis=0)
                acci[:, st_b] += jnp.sum((gni * hr - gnr * hi).reshape(steps, N_SEG, BLOCK_STATE), axis=0)
                dyb = dy_ref[:, ln_b].astype(BF16)
                ub = u_ref[:, ln_b].astype(BF16)
                gbr, gbi = gr[pl.ds(0, rows), st_b].astype(BF16), gi[pl.ds(0, rows), st_b].astype(BF16)
                dcr_ref[b] += lax.dot_general(hr.astype(BF16), dyb, tn, preferred_element_type=F32)
                dci_ref[b] += lax.dot_general(hi.astype(BF16), dyb, tn, preferred_element_type=F32)
                dbr_ref[b] += lax.dot_general(ub, gbr, tn, preferred_element_type=F32)
                dbi_ref[b] += lax.dot_general(ub, gbi, tn, preferred_element_type=F32)
                du_ref[:, ln_b] = (lax.dot_general(gbr, bdr[b], nt, preferred_element_type=F32)
                                   + lax.dot_general(gbi, bdi[b], nt, preferred_element_type=F32)
                                   + d_ref[:, ln_b] * dy_ref[:, ln_b]).astype(BF16)
                dd_ref[:, ln_b] += jnp.sum(dy_ref[:, ln_b] * u_ref[:, ln_b], axis=0, keepdims=True)
        gr[pl.ds(rows, N_SEG), :] = g0r
        gi[pl.ds(rows, N_SEG), :] = g0i

        @pl.when(c == nc - 1)
        def _():
            if full:
                dar_ref[...] = jnp.sum(accr[...], axis=0, keepdims=True)
                dai_ref[...] = jnp.sum(acci[...], axis=0, keepdims=True)
            else:
                er_ref[...] = g0r
                ei_ref[...] = g0i

    lanes, states = LANES * nblk, BLOCK_STATE * nblk
    rev = lambda k, c: (nc - 1 - c, k)
    blk3 = lambda a: pl.BlockSpec((nblk,) + a.shape[1:], lambda k, c: (k, 0, 0))
    seg = pl.BlockSpec((N_SEG, states), lambda k, c: (0, k))
    st = pl.BlockSpec((rows, states), rev)
    ch = pl.BlockSpec((rows, lanes), rev)
    vec = pl.BlockSpec((1, lanes), lambda k, c: (0, k))
    if not full:
        st = pl.BlockSpec((rows, states), lambda k, c: (0, k))
    in_specs = [ch, ch if full else pl.BlockSpec((rows, lanes), lambda k, c: (0, k)), st, st,
                blk3(bd_re), blk3(bd_im), blk3(cd_re), blk3(cd_im), seg, seg, seg, seg, vec]
    args = [dy, u, h_re, h_im, bd_re, bd_im, cd_re, cd_im, ab_re, ab_imn, gin_re, gin_im, d_row]
    gbuf = [pltpu.VMEM((rows + N_SEG, states), F32)] * 2
    if full:
        row1 = pl.BlockSpec((1, states), lambda k, c: (0, k))
        out_specs = [ch, blk3(bd_re), blk3(bd_im), blk3(cd_re), blk3(cd_im), row1, row1, vec]
        out_shape = [jax.ShapeDtypeStruct(duz.shape, BF16),
                     jax.ShapeDtypeStruct(bd_re.shape, F32), jax.ShapeDtypeStruct(bd_im.shape, F32),
                     jax.ShapeDtypeStruct(cd_re.shape, F32), jax.ShapeDtypeStruct(cd_im.shape, F32),
                     jax.ShapeDtypeStruct((1, ns), F32), jax.ShapeDtypeStruct((1, ns), F32),
                     jax.ShapeDtypeStruct((1, w), F32)]
        scratch = gbuf + [pltpu.VMEM((N_SEG, states), F32)] * 2
        in_specs.append(pl.BlockSpec(memory_space=pl.ANY))
        args.append(duz)
        aliases = {len(args) - 1: 0}
    else:
        out_specs = [seg, seg]
        out_shape = [jax.ShapeDtypeStruct((N_SEG, ns), F32)] * 2
        scratch = gbuf
        aliases = {}
    return pl.pallas_call(
        body, name=name, grid=(nb // nblk, nc), in_specs=in_specs, out_specs=out_specs, out_shape=out_shape,
        input_output_aliases=aliases, scratch_shapes=scratch, compiler_params=_cparams("parallel", "arbitrary"),
    )(*args)


def _log_sigmoid(x):
    return jnp.minimum(x, 0.0) - jnp.log(1.0 + jnp.exp(-jnp.abs(x)))


def _tri(n, upper):
    r = lax.broadcasted_iota(jnp.int32, (n, n), 0)
    c = lax.broadcasted_iota(jnp.int32, (n, n), 1)
    return jnp.where((c >= r) if upper else (r >= c), 1.0, 0.0).astype(F32)


def _cum_fwd(f_logit, b_row, name):
    s, w = f_logit.shape
    t = _tile(s, 256, SUBLANES)

    def body(f_ref, b_ref, o_ref, carry):
        @pl.when(pl.program_id(0) == 0)
        def _():
            carry[...] = jnp.zeros_like(carry)

        lf = _log_sigmoid(f_ref[...] + b_ref[...])
        cum = jnp.dot(_tri(t, False), lf, precision=lax.Precision.HIGHEST, preferred_element_type=F32) + carry[...]
        o_ref[...] = cum * LOG2E
        carry[...] = cum[t - 1:t, :]

    return pl.pallas_call(
        body, name=name, grid=(s // t,),
        in_specs=[pl.BlockSpec((t, w), lambda i: (i, 0)), pl.BlockSpec((1, w), lambda i: (0, 0))],
        out_specs=pl.BlockSpec((t, w), lambda i: (i, 0)), out_shape=jax.ShapeDtypeStruct((s, w), F32),
        scratch_shapes=[pltpu.VMEM((1, w), F32)], compiler_params=_cparams("arbitrary"),
    )(f_logit, b_row)


def _cum_bwd(dcq, dck, f_logit, b_row, name):
    s, w = f_logit.shape
    t = _tile(s, 256, SUBLANES)
    nt = s // t

    def body(q_ref, k_ref, f_ref, b_ref, df_ref, db_ref, carry):
        @pl.when(pl.program_id(0) == 0)
        def _():
            carry[...] = jnp.zeros_like(carry)
            db_ref[...] = jnp.zeros_like(db_ref)

        dc = q_ref[...] - k_ref[...]
        rc = jnp.dot(_tri(t, True), dc, precision=lax.Precision.HIGHEST, preferred_element_type=F32) + carry[...]
        carry[...] = rc[0:1, :]
        df = rc * (1.0 - jax.nn.sigmoid(f_ref[...] + b_ref[...]))
        df_ref[...] = df.astype(BF16)
        db_ref[...] += jnp.sum(df, axis=0, keepdims=True)

    rev = pl.BlockSpec((t, w), lambda i: (nt - 1 - i, 0))
    one = pl.BlockSpec((1, w), lambda i: (0, 0))
    return pl.pallas_call(
        body, name=name, grid=(nt,), in_specs=[rev, rev, rev, one], out_specs=[rev, one],
        out_shape=[jax.ShapeDtypeStruct((s, w), BF16), jax.ShapeDtypeStruct((1, w), F32)],
        scratch_shapes=[pltpu.VMEM((1, w), F32)], compiler_params=_cparams("arbitrary"),
    )(dcq, dck, f_logit, b_row)


def _head_col(cum_tile, h):
    lane = lax.broadcasted_iota(jnp.int32, cum_tile.shape, 1)
    return jnp.sum(jnp.where(lane == h, cum_tile, 0.0), axis=1, keepdims=True)


def _attn_tiles(s):
    return _tile(s, 512, LANES)


def _exp2_rows(sc, sub):
    return jnp.concatenate([jnp.exp2(sc[:, b * LANES:(b + 1) * LANES] - sub) for b in range(sc.shape[1] // LANES)], axis=1)


def _row_of(rep):
    return jnp.transpose(rep)[0:1, :]


def _causal(sc, keys_on_rows):
    r = lax.broadcasted_iota(jnp.int32, sc.shape, 0)
    c = lax.broadcasted_iota(jnp.int32, sc.shape, 1)
    return jnp.where((r <= c) if keys_on_rows else (c <= r), sc, NEG_INF)


def _fox_fwd(q2, kv, cum2_t, z, name):
    s, w = q2.shape
    nh = w // HEAD_DIM
    tq = _attn_tiles(s)
    nq = s // tq
    nt = (_DOT_DIMS["nt"], ((), ()))

    def body(q_ref, k_ref, v_ref, ct_ref, z_ref, o_ref, oz_ref, lse_row_ref, m_s, acc_s, vaug, s_buf):
        i = pl.program_id(1)

        @pl.when(i == 0)
        def _():
            vaug[:, :HEAD_DIM] = v_ref[...]
            vaug[:, HEAD_DIM:] = jnp.ones((s, LANES), BF16)

        qb = q_ref[...]
        m_s[...] = jnp.full_like(m_s, NEG_INF)
        acc_s[...] = jnp.zeros_like(acc_s)

        def scores(j):
            off = pl.multiple_of(j * tq, tq)
            return lax.dot_general(qb, k_ref[pl.ds(off, tq), :], nt, preferred_element_type=F32) - ct_ref[:, pl.ds(off, tq)]

        def softmax_pv(j, sc):
            m_old = m_s[...]
            m_new = jnp.maximum(m_old, jnp.max(sc, axis=1, keepdims=True))
            p = _exp2_rows(sc, m_new)
            alpha = jnp.exp2(m_old - m_new)
            pv = jnp.dot(p.astype(BF16), vaug[pl.ds(pl.multiple_of(j * tq, tq), tq), :], preferred_element_type=F32)
            acc_s[...] = jnp.concatenate([alpha, alpha], axis=1) * acc_s[...] + pv
            m_s[...] = m_new

        s_buf[...] = scores(0)

        def loop(j, carry):
            nxt = scores(j + 1)
            softmax_pv(j, s_buf[...])
            s_buf[...] = nxt
            return carry

        lax.fori_loop(0, i, loop, 0)
        softmax_pv(i, _causal(s_buf[...], False))
        l = acc_s[:, HEAD_DIM:]
        o = acc_s[:, :HEAD_DIM] / l
        o_ref[...] = o
        oz_ref[...] = (o * _silu(z_ref[...].astype(F32))).astype(BF16)
        lse_row_ref[...] = _row_of(m_s[...] + jnp.log(l) * LOG2E)

    return pl.pallas_call(
        body, name=name, grid=(nh, nq),
        in_specs=[pl.BlockSpec((tq, HEAD_DIM), lambda h, i: (i, h)),
                  pl.BlockSpec((s, HEAD_DIM), lambda h, i: (0, h)),
                  pl.BlockSpec((s, HEAD_DIM), lambda h, i: (0, nh + h)),
                  pl.BlockSpec((None, 1, s), lambda h, i: (h, 0, 0)),
                  pl.BlockSpec((tq, HEAD_DIM), lambda h, i: (i, h))],
        out_specs=[pl.BlockSpec((tq, HEAD_DIM), lambda h, i: (i, h)),
                   pl.BlockSpec((tq, HEAD_DIM), lambda h, i: (i, h)),
                   pl.BlockSpec((None, 1, tq), lambda h, i: (h, 0, i))],
        out_shape=[jax.ShapeDtypeStruct((s, w), F32), jax.ShapeDtypeStruct((s, w), BF16),
                   jax.ShapeDtypeStruct((nh, 1, s), F32)],
        scratch_shapes=[pltpu.VMEM((tq, LANES), F32), pltpu.VMEM((tq, HEAD_DIM + LANES), F32),
                        pltpu.VMEM((s, HEAD_DIM + LANES), BF16), pltpu.VMEM((tq, tq), F32)],
        compiler_params=_cparams("arbitrary", "arbitrary"),
    )(q2, kv, kv, cum2_t, z)


def _fox_bwd(q2, kv, do, o, lse2_t, cum2, dqz, name):
    s, w = q2.shape
    nh = w // HEAD_DIM
    tk = _attn_tiles(s)
    nk = s // tk
    scale = HEAD_DIM ** -0.5
    nt = (_DOT_DIMS["nt"], ((), ()))
    tn = (_DOT_DIMS["tn"], ((), ()))

    def body(q_ref, k_ref, v_ref, do_ref, o_ref, lse_ref, c_ref, _, dk_ref, dv_ref, dq_ref, dcq_ref, dck_ref,
             dk_s, dv_s, dc_s, dq_s, dcq_s, dl_s, s_buf, dp_buf):
        h, j = pl.program_id(0), pl.program_id(1)

        @pl.when(j == 0)
        def _():
            dq_s[...] = jnp.zeros_like(dq_s)
            dcq_s[...] = jnp.zeros_like(dcq_s)
            for i in range(nk):
                rows = pl.ds(i * tk, tk)
                d = jnp.sum(do_ref[rows, :].astype(F32) * o_ref[rows, :], axis=1, keepdims=True)
                dl_s[:, i * tk:(i + 1) * tk] = _row_of(jnp.broadcast_to(d, (tk, LANES)))

        kb = k_ref[...]
        vb = v_ref[...]
        ck = jnp.broadcast_to(_head_col(c_ref[...], h), (tk, LANES))
        dk_s[...] = jnp.zeros_like(dk_s)
        dv_s[...] = jnp.zeros_like(dv_s)
        dc_s[...] = jnp.zeros_like(dc_s)

        def scores(i):
            off = pl.multiple_of(i * tk, tk)
            sc = lax.dot_general(kb, q_ref[pl.ds(off, tk), :], nt, preferred_element_type=F32) - lse_ref[:, pl.ds(off, tk)]
            dp = lax.dot_general(vb, do_ref[pl.ds(off, tk), :], nt, preferred_element_type=F32) - dl_s[:, pl.ds(off, tk)]
            return sc, dp

        def accumulate(i, sc, dp):
            off = pl.multiple_of(i * tk, tk)
            p = _exp2_rows(sc, ck)
            dv_s[...] += jnp.dot(p.astype(BF16), do_ref[pl.ds(off, tk), :], preferred_element_type=F32)
            ds = p * dp
            dsb = ds.astype(BF16)
            dk_s[...] += jnp.dot(dsb, q_ref[pl.ds(off, tk), :], preferred_element_type=F32)
            dq_s[pl.ds(off, tk), :] += lax.dot_general(dsb, kb, tn, preferred_element_type=F32)
            dcq_s[:, pl.ds(off, tk)] += jnp.sum(ds, axis=0, keepdims=True)
            part = ds[:, :LANES]
            for b in range(1, tk // LANES):
                part = part + ds[:, b * LANES:(b + 1) * LANES]
            dc_s[...] += part

        sc0, dp0 = scores(j)
        s_buf[...] = _causal(sc0, True)
        dp_buf[...] = dp0

        def loop(i, carry):
            nxt = scores(i + 1)
            accumulate(i, s_buf[...], dp_buf[...])
            s_buf[...], dp_buf[...] = nxt
            return carry

        lax.fori_loop(j, nk - 1, loop, 0)
        accumulate(nk - 1, s_buf[...], dp_buf[...])
        dk_ref[...] = (dk_s[...] * (1.0 / LOG2E)).astype(BF16)
        dv_ref[...] = dv_s[...].astype(BF16)
        dck_ref[...] = jnp.sum(jnp.transpose(dc_s[...]), axis=0, keepdims=True)

        @pl.when(j == nk - 1)
        def _():
            dq_ref[...] = (dq_s[...] * scale).astype(BF16)
            dcq_ref[...] = dcq_s[...]

    col = pl.BlockSpec((s, HEAD_DIM), lambda h, j: (0, h))
    row = pl.BlockSpec((None, 1, s), lambda h, j: (h, 0, 0))
    kspec = pl.BlockSpec((tk, HEAD_DIM), lambda h, j: (j, h))
    return pl.pallas_call(
        body, name=name, grid=(nh, nk),
        in_specs=[col, kspec, pl.BlockSpec((tk, HEAD_DIM), lambda h, j: (j, nh + h)), col, col, row,
                  pl.BlockSpec((tk, LANES), lambda h, j: (j, 0)), pl.BlockSpec(memory_space=pl.ANY)],
        out_specs=[kspec, kspec, col, row, pl.BlockSpec((None, 1, tk), lambda h, j: (h, 0, j))],
        out_shape=[jax.ShapeDtypeStruct((s, w), BF16), jax.ShapeDtypeStruct((s, w), BF16),
                   jax.ShapeDtypeStruct(dqz.shape, BF16), jax.ShapeDtypeStruct((nh, 1, s), F32),
                   jax.ShapeDtypeStruct((nh, 1, s), F32)],
        input_output_aliases={7: 2},
        scratch_shapes=[pltpu.VMEM((tk, HEAD_DIM), F32), pltpu.VMEM((tk, HEAD_DIM), F32), pltpu.VMEM((tk, LANES), F32),
                        pltpu.VMEM((s, HEAD_DIM), F32), pltpu.VMEM((1, s), F32), pltpu.VMEM((1, s), F32),
                        pltpu.VMEM((tk, tk), F32), pltpu.VMEM((tk, tk), F32)],
        compiler_params=_cparams("arbitrary", "arbitrary"),
    )(q2, kv, kv, do, o, lse2_t, cum2, dqz)


_ALL_PEERS = tuple(range(1, N_DEV))
_CHIP_PEERS = (1, 2, 4, 6)


def _exchange_copies(ins, outs, send_sems, recv_sems, local_sems, scatter, peers=_ALL_PEERS):
    x, y, c = (lax.axis_index(a) for a in MESH_AXES)
    me = 4 * x + 2 * y + c
    local, remote = [], []
    for a in range(len(ins)):
        local.append(pltpu.make_async_copy(ins[a].at[me] if scatter else ins[a], outs[a].at[me], local_sems.at[a]))
        for k in peers:
            px, py, pc = (1 - x if k & 4 else x), (1 - y if k & 2 else y), (1 - c if k & 1 else c)
            remote.append(pltpu.make_async_remote_copy(
                src_ref=ins[a].at[4 * px + 2 * py + pc] if scatter else ins[a], dst_ref=outs[a].at[me],
                send_sem=send_sems.at[a * (N_DEV - 1) + k - 1], recv_sem=recv_sems.at[a * (N_DEV - 1) + k - 1],
                device_id=(px, py, pc), device_id_type=pl.DeviceIdType.MESH))
    return local, remote


def _exchange_out_shapes(arrs, scatter):
    return [((N_DEV,) + a.shape[1:]) if scatter else ((N_DEV,) + a.shape) for a in arrs]


_HBM =pl.BlockSpec(memory_space=pltpu.HBM)
_SEM = pl.BlockSpec(memory_space=pltpu.SEMAPHORE)


def _exchange_start(arrs, scatter, name, after=(), peers=_ALL_PEERS):
    n = len(arrs)
    after = list(after)
    lands = [lax.empty(s, a.dtype) for s, a in zip(_exchange_out_shapes(arrs, scatter), arrs)]

    def body(*refs):
        ins, outs = refs[:n], refs[n:2 * n]
        send_sems, recv_sems, local_sems = refs[2 * n + len(after):2 * n + len(after) + 3]
        token = refs[-1]
        local, remote = _exchange_copies(ins, outs, send_sems, recv_sems, local_sems, scatter, peers)
        for cp in local + remote:
            cp.start()
        token[...] = jnp.zeros_like(token)

    hbm = lambda a: pltpu.HBM(a.shape, a.dtype)
    res = pl.pallas_call(
        body, name=name,
        out_shape=(pltpu.SemaphoreType.DMA((n * (N_DEV - 1),)), pltpu.SemaphoreType.DMA((n * (N_DEV - 1),)),
                   pltpu.SemaphoreType.DMA((n,)), *[hbm(a) for a in arrs], *[hbm(a) for a in lands],
                   jax.ShapeDtypeStruct((SUBLANES, LANES), F32)),
        in_specs=[_HBM] * (2 * n) + [pl.BlockSpec(memory_space=pl.ANY)] * len(after),
        out_specs=(_SEM, _SEM, _SEM, *[_HBM] * (2 * n), pl.BlockSpec(memory_space=pltpu.VMEM)),
        input_output_aliases={i: 3 + i for i in range(2 * n)},
        compiler_params=pltpu.CompilerParams(has_side_effects=pltpu.SideEffectType.DATAFLOW_SIDE_EFFECTING),
    )(*[pltpu.with_memory_space_constraint(a, pltpu.HBM) for a in list(arrs) + lands], *after)
    return (n, scatter, res[:3], res[3:3 + n], res[3 + n:3 + 2 * n], peers), res[-1]


def _exchange_wait(state, after, name):
    n, scatter, sems, srcs, lands, peers = state
    after = list(after) if isinstance(after, (list, tuple)) else [after]

    def body(*refs):
        ins, outs = refs[:n], refs[n:2 * n]
        send_sems, recv_sems, local_sems = refs[2 * n:2 * n + 3]
        local, remote = _exchange_copies(ins, outs, send_sems, recv_sems, local_sems, scatter, peers)
        for cp in remote:
            cp.wait_send()
            cp.wait_recv()
        for cp in local:
            cp.wait()

    hbm = lambda a: pltpu.HBM(a.shape, a.dtype)
    res = pl.pallas_call(
        body, name=name,
        out_shape=(*[hbm(a) for a in srcs], *[hbm(a) for a in lands]),
        in_specs=[_HBM] * (2 * n) + [_SEM] * 3 + [pl.BlockSpec(memory_space=pl.ANY)] * len(after),
        out_specs=tuple([_HBM] * (2 * n)),
        input_output_aliases={i: i for i in range(2 * n)},
        compiler_params=pltpu.CompilerParams(has_side_effects=pltpu.SideEffectType.DATAFLOW_SIDE_EFFECTING),
    )(*srcs, *lands, *sems, *after)
    return list(res[n:])


def _forward_to_sibling(slots, name):
    n = len(slots)
    hops = (2, 4, 6)

    def body(*refs):
        ins, outs, (send_sems, recv_sems) = refs[:n], refs[n:2 * n], refs[2 * n:]
        x, y, c = (lax.axis_index(a) for a in MESH_AXES)
        copies = []
        for a in range(n):
            for i, k in enumerate(hops):
                slot = 4 * (1 - x if k & 4 else x) + 2 * (1 - y if k & 2 else y) + c
                copies.append(pltpu.make_async_remote_copy(
                    src_ref=ins[a].at[slot], dst_ref=outs[a].at[slot],
                    send_sem=send_sems.at[a * len(hops) + i], recv_sem=recv_sems.at[a * len(hops) + i],
                    device_id=(x, y, 1 - c), device_id_type=pl.DeviceIdType.MESH))
        for cp in copies:
            cp.start()
        for cp in copies:
            cp.wait_send()
            cp.wait_recv()

    return pl.pallas_call(
        body, name=name, out_shape=[jax.ShapeDtypeStruct(s.shape, s.dtype) for s in slots],
        in_specs=[pl.BlockSpec(memory_space=pl.ANY)] * n, out_specs=[pl.BlockSpec(memory_space=pl.ANY)] * n,
        input_output_aliases={i: i for i in range(n)},
        scratch_shapes=[pltpu.SemaphoreType.DMA((n * len(hops),)), pltpu.SemaphoreType.DMA((n * len(hops),))],
    )(*slots)


def _adamw_math(w, g, m, v):
    m = ADAM_B1 * m + (1.0 - ADAM_B1) * g
    v = ADAM_B2 * v + (1.0 - ADAM_B2) * (g * g)
    m_hat = m / (1.0 - ADAM_B1 ** ADAM_STEP)
    v_hat = v / (1.0 - ADAM_B2 ** ADAM_STEP)
    return -ADAM_LR * (m_hat / (jnp.sqrt(v_hat) + ADAM_EPS) + ADAM_WD * w), m, v


def _slot_sum(p_ref):
    g = p_ref[0].astype(F32)
    for d in range(1, p_ref.shape[0]):
        g = g + p_ref[d].astype(F32)
    return g


def _adamw_tile(r, c):
    return _tile(r, max(SUBLANES, (256 * 1024) // c // SUBLANES * SUBLANES), SUBLANES)


def _adamw(parts, w, m, v, name):
    r, c = w.shape[-2:]
    by_cols = r % SUBLANES != 0
    tr, tc = (r, _tile(c, 256)) if by_cols else (_adamw_tile(r, c), c)

    def body(p_ref, w_ref, m_ref, v_ref, g_ref, d_ref, nm_ref, nv_ref):
        g = _slot_sum(p_ref)
        g_ref[...] = g
        d_ref[...], nm_ref[...], nv_ref[...] = _adamw_math(w_ref[...], g, m_ref[...], v_ref[...])

    pos = (lambda i: (0, i)) if by_cols else (lambda i: (i, 0))
    if w.ndim == 3:
        blk = pl.BlockSpec((None, tr, tc), lambda i: (0,) + pos(i))
    else:
        blk = pl.BlockSpec((tr, tc), pos)
    sh = jax.ShapeDtypeStruct(w.shape, F32)
    return pl.pallas_call(
        body, name=name, grid=(c // tc if by_cols else r // tr,),
        in_specs=[pl.BlockSpec((parts.shape[0], tr, tc), lambda i: (0,) + pos(i)), blk, blk, blk],
        out_specs=[blk] * 4, out_shape=[sh] * 4, compiler_params=_cparams("parallel"),
    )(parts, w, m, v)


def _sum_parts(parts, name):
    _, r, c = parts.shape
    tr = _adamw_tile(r, c)

    def body(p_ref, o_ref):
        o_ref[...] = _slot_sum(p_ref)

    return pl.pallas_call(
        body, name=name, grid=(r // tr,),
        in_specs=[pl.BlockSpec((parts.shape[0], tr, c), lambda i: (0, i, 0))],
        out_specs=pl.BlockSpec((tr, c), lambda i: (i, 0)), out_shape=jax.ShapeDtypeStruct((r, c), F32),
        compiler_params=_cparams("parallel"),
    )(parts)


def _lane_pad(a, width=LANES):
    return jnp.pad(a, ((0, 0), (0, width - a.shape[1])))


def _local_step(x, target, norm_pre, norm_post, kv_norm, kv_b_f, a_re, a_im, log_dt, b_re, b_im, c_re, c_im, comm):
    s, d = x.shape
    g, p = a_re.shape
    w = g * S5_GROUP
    fw = d
    nh = fw // HEAD_DIM
    seg_len = s // N_SEG
    row = lambda v: v.reshape(1, -1)
    g_pre0, g_pre1, g_post0, g_post1, g_kv = row(norm_pre[0]), row(norm_pre[1]), row(norm_post[0]), row(norm_post[1]), row(kv_norm)

    ldt = log_dt.reshape(g, 1)
    abr, abi, cr, ci = _s5_disc_fwd(a_re, a_im, ldt)
    cr_col, ci_col = cr.reshape(g * p, 1), ci.reshape(g * p, 1)
    b_re2, b_im2 = b_re.reshape(g * p, S5_GROUP), b_im.reshape(g * p, S5_GROUP)
    bb_re, bb_im = _s5_bbar_fwd(cr_col, ci_col, b_re2, b_im2)
    bd_re = _block_diag_in(bb_re.reshape(g, p, S5_GROUP)).astype(BF16)
    bd_im = _block_diag_in(bb_im.reshape(g, p, S5_GROUP)).astype(BF16)
    cd_re = _block_diag_out(c_re).astype(BF16)
    cd_im = _block_diag_out(-c_im).astype(BF16)
    ab_re = jnp.broadcast_to(abr.reshape(1, g * p), (N_SEG, g * p))
    ab_im = jnp.broadcast_to(abi.reshape(1, g * p), (N_SEG, g * p))
    zero_seg = jnp.zeros((N_SEG, g * p), F32)

    xn0 = _norm_cast(x, g_pre0 + comm.token, "norm_pre0", x_kind="nat")
    w_in = comm.weight("s5_w_in", [xn0, bd_re, bd_im, cd_re, cd_im, ab_re, ab_im])
    d_row, bglu_row = row(comm.vector("s5_d")), row(comm.vector("s5_b_glu"))
    u = _mm(xn0, w_in, "nn", BF16, "s5_in_u", b_cols=(0, w), b_slots=True)
    z0 = _mm(xn0, w_in, "nn", BF16, "s5_in_z", b_cols=(w, w), b_slots=True)
    e_re, e_im = _s5_scan_fwd(u, bd_re, bd_im, cd_re, cd_im, ab_re, ab_im, zero_seg, zero_seg, d_row, False, "s5_scan_ends")
    i_re, i_im = _s5_seg_fix(e_re, e_im, ab_re, ab_im, seg_len, False, "s5_seg_fix")
    y_ssm, yg, h_re, h_im, _, _ = _s5_scan_fwd(u, bd_re, bd_im, cd_re, cd_im, ab_re, ab_im, i_re, i_im, d_row, True, "s5_scan")
    w_glu, w_out = comm.weight("s5_w_glu", yg), comm.weight("s5_w_out", yg)
    gp = _mm(yg, w_glu, "nn", BF16, "s5_glu")
    y3 = _s5_gate(y_ssm, gp, bglu_row, z0, "s5_gate")
    w_kvt, fw_in = comm.weight("kv_w", y3), comm.weight("fox_w_in", y3)
    w_ft = jnp.pad(w_kvt[2 * fw:], ((0, LANES - nh), (0, 0)))
    o0 = _mm(y3, w_out, "nn", F32, "s5_out")

    h1, hn_kv, xn1 = _resid_norm2(x, o0, g_post0 + comm.late_token, g_kv, g_pre1, "resid_norms")
    kv = _mm(hn_kv, w_kvt, "nt", BF16, "kv_proj", b_rows=2 * fw)
    f_logit = _mm(hn_kv, w_ft, "nt", F32, "f_proj")
    bf_row = _lane_pad(row(kv_b_f))
    cum2 = _cum_fwd(f_logit, bf_row, "cum_fwd")
    cum2_t = cum2[:, :nh].T.reshape(nh, 1, s)
    q2 = _mm(xn1, fw_in, "nn", BF16, "fox_q", scale=HEAD_DIM ** -0.5 * LOG2E, b_cols=(0, fw), b_slots=True)
    z1 = _mm(xn1, fw_in, "nn", BF16, "fox_z", b_cols=(fw, fw), b_slots=True)
    o, oz, lse2_t = _fox_fwd(q2, kv, cum2_t, z1, "fox_fwd")
    fw_out = comm.weight("fox_w_out", oz)
    o1 = _mm(oz, fw_out, "nn", F32, "fox_out")
    dh2, do1, sq, dg_post1 = _post_norm_loss(o1, g_post1, h1, target, "norm_post1_loss")
    loss = 0.5 * jnp.sum(sq) / d

    d_fw_out = _mm(oz, do1, "tn", BF16, "fox_out_dw")
    d_oz = _mm(do1, fw_out, "nt", BF16, "fox_out_dx")
    do, dqz = _gate_bwd(d_oz, o, z1, "fox_gate_bwd")
    dk, dv, dqz, dcq, dck = _fox_bwd(q2, kv, do, o, lse2_t, cum2, dqz, "fox_bwd")
    d_fw_in = _mm(xn1, dqz, "tn", BF16, "fox_in_dw", col_slots=True)
    dxn1 = _mm(dqz, fw_in, "nt", BF16, "fox_in_dx", b_slots=True)
    dcq_sl = _lane_pad(dcq.reshape(nh, s).T)
    dck_sl = _lane_pad(dck.reshape(nh, s).T)
    df, db_f = _cum_bwd(dcq_sl, dck_sl, f_logit, bf_row, "cum_bwd")
    dkv = _concat_cast(dk, dv, "fox_dkv")
    d_w_kvmt = _mm(dkv, hn_kv, "tn", BF16, "kv_dw")
    d_w_ft = _mm(df, hn_kv, "tn", BF16, "f_dw")
    dhn_f = _mm(df, w_ft, "nn", F32, "f_dx")
    dhn_kv = _mm(dkv, w_kvt, "nn", BF16, "kv_dx", add=dhn_f, b_rows=2 * fw)
    d_w_kvt = jnp.concatenate([d_w_kvmt, d_w_ft[:nh]], axis=0)
    tok = comm.send_grads(dict(fox_w_out=d_fw_out, fox_w_in=d_fw_in, kv_w=d_w_kvt), "exchange_fox")
    dh1, do0, dg_pre1, dg_kv, dg_post0 = _norm_bwd2(dh2, h1, dxn1, dhn_kv, g_pre1, g_kv, o0, g_post0 + tok[0, 0],
                                                      "resid_norms_bwd")

    d_w_out = _mm(y3, do0, "tn", BF16, "s5_out_dw")
    dy3 = _mm(do0, w_out, "nt", BF16, "s5_out_dx")
    duz, dgp, dyg_direct, db_glu = _s5_gate_bwd(dy3, y_ssm, gp, bglu_row, z0, "s5_gate_bwd")
    d_w_glu = _mm(yg, dgp, "tn", BF16, "s5_glu_dw")
    gelu_bwd = lambda dyg, y: jax.vjp(jax.nn.gelu, y)[1](dyg)[0]
    dy_ssm = _mm(dgp, w_glu, "nt", F32, "s5_glu_dx", add=dyg_direct, epilogue=(gelu_bwd, y_ssm))
    d_row = d_row + comm.send_grads(dict(s5_w_out=d_w_out, s5_w_glu=d_w_glu), "exchange_s5")[0, 0]
    ab_imn = -ab_im
    ge_re, ge_im = _s5_scan_bwd(dy_ssm, u, h_re, h_im, bd_re, bd_im, cd_re, cd_im, ab_re, ab_imn, zero_seg, zero_seg,
                                d_row, False, "s5_adj_ends")
    gi_re, gi_im = _s5_seg_fix(ge_re, ge_im, ab_re, ab_imn, seg_len, True, "s5_adj_fix")
    duz, dbd_re, dbd_im, dcd_re, dcd_im, dab_re, dab_im, dd = _s5_scan_bwd(
        dy_ssm, u, h_re, h_im, bd_re, bd_im, cd_re, cd_im, ab_re, ab_imn, gi_re, gi_im, d_row, True, "s5_adj", duz=duz)
    d_w_in = _mm(xn0, duz, "tn", BF16, "s5_in_dw", col_slots=True)
    tok = comm.send_grads(dict(s5_w_in=d_w_in), "exchange_s5_in")
    dxn0 = _mm(duz, w_in, "nt", BF16, "s5_in_dx", after=tok, b_slots=True)
    grad_x, dg_pre0 = _norm_bwd1(dh1, x, dxn0, g_pre0, "norm_pre0_bwd")

    dbb_re = _block_diag_in_extract(dbd_re, p, S5_GROUP).reshape(g * p, S5_GROUP)
    dbb_im = _block_diag_in_extract(dbd_im, p, S5_GROUP).reshape(g * p, S5_GROUP)
    dcr_col, dci_col, db_re, db_im = _s5_bbar_bwd(cr_col, ci_col, b_re2, b_im2, dbb_re, dbb_im)
    da_re, da_im, dldt = _s5_disc_bwd(a_re, a_im, ldt, dab_re.reshape(g, p), dab_im.reshape(g, p),
                                      dcr_col.reshape(g, p), dci_col.reshape(g, p))
    dc_re = _block_diag_out_extract(dcd_re, S5_GROUP, p)
    dc_im = -_block_diag_out_extract(dcd_im, S5_GROUP, p)

    small = dict(
        norm_pre=jnp.concatenate([dg_pre0, dg_pre1], axis=0), norm_post=jnp.concatenate([dg_post0, dg_post1], axis=0),
        s5_a_re=da_re, s5_a_im=da_im, s5_log_dt=dldt.reshape(g), s5_b_re=db_re.reshape(g, p, S5_GROUP),
        s5_b_im=db_im.reshape(g, p, S5_GROUP), s5_c_re=dc_re, s5_c_im=dc_im, s5_d=dd.reshape(-1),
        s5_b_glu=db_glu.reshape(-1), kv_norm=dg_kv.reshape(-1), kv_b_f=db_f[0, :nh])
    return loss, grad_x, small


_BIG = ("s5_w_in", "s5_w_glu", "s5_w_out", "kv_w", "fox_w_in", "fox_w_out")
_COL_SHARDED = ("s5_w_in", "fox_w_in")
_SMALL = ("norm_pre", "norm_post", "s5_a_re", "s5_a_im", "s5_log_dt", "s5_b_re", "s5_b_im", "s5_c_re", "s5_c_im",
          "s5_d", "s5_b_glu", "kv_norm", "kv_b_f")
_SMALL_SHARDED = ("s5_d", "s5_b_glu")
_PACK_QUANTUM = SUBLANES * LANES
_WEIGHTS = ('norm_pre', 'norm_post', 's5_w_in', 's5_a_re', 's5_a_im', 's5_log_dt', 's5_b_re', 's5_b_im', 's5_c_re', 's5_c_im',
            's5_d', 's5_w_glu', 's5_b_glu', 's5_w_out', 'kv_norm', 'kv_w', 'kv_b_f', 'fox_w_in', 'fox_w_out')


def _full_from_slots(name, slots):
    n, r, c = slots.shape
    if name in _COL_SHARDED:
        return slots.transpose(1, 0, 2).reshape(r, n * c)
    return slots.reshape(n * r, c)


def _slots_from_full(name, full):
    if name in _COL_SHARDED:
        r, nc = full.shape
        return full.reshape(r, N_DEV, nc // N_DEV).transpose(1, 0, 2)
    nr, c = full.shape
    return full.reshape(N_DEV, nr // N_DEV, c)


def _groups_last(shape):
    return len(shape) >= 3 and shape[-1] < LANES and shape[-3] % LANES == 0


def _pack(vals):
    parts = []
    for v in vals:
        flat = jnp.moveaxis(v, -3, -1).reshape(-1) if _groups_last(v.shape) else v.reshape(-1)
        parts.append(jnp.pad(flat, (0, (-flat.shape[0]) % _PACK_QUANTUM)))
    total = sum(p.shape[0] for p in parts)
    parts.append(jnp.zeros(((-total) % (N_DEV * _PACK_QUANTUM),), F32))
    return jnp.concatenate(parts).reshape(-1, LANES)


def _unpack(packed, shapes):
    flat = packed.reshape(-1)
    out, off = [], 0
    for sh in shapes:
        n = math.prod(sh)
        piece = flat[off:off + n]
        if _groups_last(sh):
            piece = jnp.moveaxis(piece.reshape(sh[:-3] + sh[-2:] + sh[-3:-2]), -1, -3)
        out.append(piece.reshape(sh))
        off += n + (-n) % _PACK_QUANTUM
    return out


class _Comm:
    _GROUPS = (("s5_w_in",) + _SMALL_SHARDED, ("s5_w_glu", "s5_w_out"), ("kv_w", "fox_w_in"), ("fox_w_out",))
    _SLOT_FORM = ("s5_w_in", "fox_w_in")

    def __init__(self, shards, vectors, early=()):
        self._shards = {**shards, **vectors}
        self._full, self._gathers = {}, {}
        self._early = list(early)
        self.token = jnp.zeros((), F32)
        for group in self._GROUPS[:-1]:
            self.token = self.token + self._start(group, ())[0, 0]
        self.late_token = None
        self._sent = []

    def _start(self, group, after):
        state, tok = _exchange_start([self._shards[n] for n in group], False, "gather_start_" + group[0], after,
                                     peers=_CHIP_PEERS)
        self._gathers[group] = state
        return tok

    def vector(self, name):
        return self._full[name]

    def weight(self, name, after):
        if name not in self._full:
            group = next(g for g in self._GROUPS if name in g)
            if group == self._GROUPS[0]:
                after = (list(after) if isinstance(after, (list, tuple)) else [after]) + self._early
            slots = _exchange_wait(self._gathers.pop(group), after, "gather_wait_" + group[0])
            slots = _forward_to_sibling(slots, "gather_forward_" + group[0])
            for n, sl in zip(group, slots):
                if n in _SMALL_SHARDED:
                    self._full[n] = sl.reshape(-1)
                else:
                    self._full[n] = sl if n in self._SLOT_FORM else _full_from_slots(n, sl)
            if group == self._GROUPS[-2]:
                self.late_token = self._start(self._GROUPS[-1], [slots[0]])[0, 0]
        return self._full[name]

    def send_grads(self, grads, name):
        names = list(grads)
        slots = [grads[n] if grads[n].ndim == 3 else _slots_from_full(n, grads[n]).astype(BF16) for n in names]
        state, tok = _exchange_start(slots, True, name + "_start")
        self._sent.append((names, state, name + "_wait"))
        return tok

    def received_grads(self, group, after):
        names, state, name = self._sent[group]
        return list(zip(names, _exchange_wait(state, after, name)))


def kernel(x, norm_pre, norm_post, s5_w_in, s5_a_re, s5_a_im, s5_log_dt, s5_b_re, s5_b_im, s5_c_re, s5_c_im, s5_d, s5_w_glu, s5_b_glu, s5_w_out, kv_norm, kv_w, kv_b_f, fox_w_in, fox_w_out, loss_target, m_norm_pre, m_norm_post, m_s5_w_in, m_s5_a_re, m_s5_a_im, m_s5_log_dt, m_s5_b_re, m_s5_b_im, m_s5_c_re, m_s5_c_im, m_s5_d, m_s5_w_glu, m_s5_b_glu, m_s5_w_out, m_kv_norm, m_kv_w, m_kv_b_f, m_fox_w_in, m_fox_w_out, v_norm_pre, v_norm_post, v_s5_w_in, v_s5_a_re, v_s5_a_im, v_s5_log_dt, v_s5_b_re, v_s5_b_im, v_s5_c_re, v_s5_c_im, v_s5_d, v_s5_w_glu, v_s5_b_glu, v_s5_w_out, v_kv_norm, v_kv_w, v_kv_b_f, v_fox_w_in, v_fox_w_out):
    env = dict(locals())
    wts = {n: env[n] for n in _WEIGHTS}
    mom = {n: env["m_" + n] for n in _WEIGHTS}
    var = {n: env["v_" + n] for n in _WEIGHTS}
    me = 4 * lax.axis_index("x") + 2 * lax.axis_index("y") + lax.axis_index("c")
    shard2d = {n: (wts[n].T if n == "kv_w" else wts[n].reshape(wts[n].shape[-2:])) for n in _BIG}
    full_shape = {n: ((wts[n].size * N_DEV,) if n in _SMALL_SHARDED else wts[n].shape) for n in _SMALL}

    def spread(n, v):
        if n not in _SMALL_SHARDED:
            return v
        flat = v.reshape(-1)
        return lax.dynamic_update_slice(jnp.zeros(full_shape[n], F32), flat, (me * flat.shape[0],))

    packed = [_pack([spread(n, src[n]) for n in _SMALL] + [jnp.zeros((1,), F32)]) for src in (wts, mom, var)]
    comm = _Comm({n: _cast_bf16(shard2d[n], "cast_" + n) for n in _BIG}, {n: wts[n].reshape(1, -1) for n in _SMALL_SHARDED}, packed)

    loss_local, grad_x, small = _local_step(
        x[0], loss_target[0], norm_pre, norm_post, kv_norm, kv_b_f, s5_a_re[0], s5_a_im[0], s5_log_dt[0],
        s5_b_re[0], s5_b_im[0], s5_c_re[0], s5_c_im[0], comm)

    small_pack = _pack([small[n] for n in _SMALL] + [loss_local.reshape(1)])
    slice_rows = small_pack.shape[0] // N_DEV
    small_state, small_tok = _exchange_start([small_pack.reshape(N_DEV, slice_rows, LANES)], True, "reduce_small_start")

    res = {}

    def finish(group, after):
        for n, recv in comm.received_grads(group, after):
            if n == "kv_w":
                res[n] = [o.T for o in _adamw(recv, wts[n].T, mom[n].T, var[n].T, "adamw_" + n)]
            else:
                res[n] = _adamw(recv, wts[n], mom[n], var[n], "adamw_" + n)

    finish(0, [small_tok, grad_x])
    my_sum = _sum_parts(_exchange_wait(small_state, res["kv_w"][0], "reduce_small_wait")[0], "sum_small")
    gather_state, gather_tok = _exchange_start([my_sum], False, "gather_small_start")
    finish(1, gather_tok)
    finish(2, gather_tok)
    g_all = _exchange_wait(gather_state, res["s5_w_in"][0], "gather_small_wait")[0].reshape(1, small_pack.shape[0], LANES)
    outs = _adamw(g_all, *packed, "adamw_small")
    unpacked = [_unpack(o, [full_shape[n] for n in _SMALL] + [(1,)]) for o in outs]
    loss = unpacked[0][-1][0]
    for i, n in enumerate(_SMALL):
        vals = [u[i] for u in unpacked]
        if n in _SMALL_SHARDED:
            k = wts[n].size
            vals = [lax.dynamic_slice(v, (me * k,), (k,)) for v in vals]
        res[n] = [v.reshape(wts[n].shape) for v in vals]

    return (loss, grad_x[None], *[res[n][0] for n in _WEIGHTS], *[res[n][1] for n in _WEIGHTS],
            *[res[n][2] for n in _WEIGHTS], *[res[n][3] for n in _WEIGHTS])
```

```python
import math

import jax
import jax.numpy as jnp
from jax import lax
from jax.experimental import pallas as pl
from jax.experimental.pallas import tpu as pltpu

F32 = jnp.float32
BF16 = jnp.bfloat16

N_DEV = 8
MESH_AXES = ("x", "y", "c")
S5_GROUP = 16
S5_STATE = 64
LANES = 128
SUBLANES = 8
GROUPS_PER_BLOCK = LANES // S5_GROUP
BLOCK_STATE = GROUPS_PER_BLOCK * S5_STATE
N_SEG = SUBLANES
HEAD_DIM = 128
RMS_EPS = 1e-6
NEG_INF = -1e30
LOG2E = math.log2(math.e)
ADAM_LR = 0.001
ADAM_B1 = 0.9
ADAM_B2 = 0.999
ADAM_EPS = 1e-08
ADAM_WD = 0.01
ADAM_STEP = 10
VMEM_LIMIT = 56 * 1024 * 1024


def _tile(n, pref, quantum=LANES):
    if n <= pref:
        return n
    t = (pref // quantum) * quantum
    while t >= quantum:
        if n % t == 0:
            return t
        t -= quantum
    return n


def _cparams(*sem):
    return pltpu.CompilerParams(dimension_semantics=sem if sem else None, vmem_limit_bytes=VMEM_LIMIT)


_DOT_DIMS = {"nn": ((1,), (0,)), "nt": ((1,), (1,)), "tn": ((0,), (0,))}


def _mm(a, b, mode, out_dtype, name, add=None, scale=None, b_cols=None, after=None, col_slots=False, b_slots=False,
        b_rows=None, epilogue=None):
    slot_w = b.shape[2] if b_slots else None
    b2d = (b.shape[1], b.shape[0] * b.shape[2]) if b_slots else b.shape
    b_shape = b2d if b_cols is None else (b2d[0], b_cols[1])
    if b_rows is not None:
        b_shape = (b_rows, b_shape[1])
    if mode == "nn":
        (M, K), (K2, N) = a.shape, b_shape
    elif mode == "nt":
        (M, K), (N, K2) = a.shape, b_shape
    else:
        (K, M), (K2, N) = a.shape, b_shape
    assert K == K2, (name, a.shape, b_shape)
    tm, tn, tk = _tile(M, 1024 if K <= 2048 else 512), (N // N_DEV if col_slots else _tile(N, 1024)), _tile(K, 4096)
    if b_slots and mode == "nn":
        tn = slot_w
    nk = K // tk
    dims = (_DOT_DIMS[mode], ((), ()))
    col0 = 0
    if b_cols is not None:
        assert mode != "tn" and b_cols[0] % (tn if mode == "nn" else tk) == 0
        col0 = b_cols[0] // (tn if mode == "nn" else tk)
    assert not b_slots or (mode == "nn" or (mode == "nt" and nk == 1 and b_cols is None))

    def body(*refs):
        a_ref, b_ref = refs[:2]
        c_ref = refs[2] if add is not None else None
        e_ref = refs[2 + (add is not None)] if epilogue is not None else None
        o_ref = refs[2 + (add is not None) + (epilogue is not None) + (after is not None)]
        if b_slots and mode == "nt":
            part = lax.dot_general(a_ref[:, :slot_w], b_ref[0], dims, preferred_element_type=F32)
            for sl in range(1, b_ref.shape[0]):
                part += lax.dot_general(a_ref[:, sl * slot_w:(sl + 1) * slot_w], b_ref[sl], dims, preferred_element_type=F32)
        else:
            part = lax.dot_general(a_ref[...], b_ref[...], dims, preferred_element_type=F32)

        def finish(r):
            if scale is not None:
                r = r * scale
            if add is not None:
                r = r + c_ref[...]
            if epilogue is not None:
                r = epilogue[0](r, e_ref[...])
            o_ref[...] = r.astype(out_dtype)

        if nk == 1:
            finish(part)
            return
        acc = refs[-1]
        k = pl.program_id(2)

        @pl.when(k == 0)
        def _():
            acc[...] = part

        @pl.when(jnp.logical_and(k > 0, k < nk - 1))
        def _():
            acc[...] += part

        @pl.when(k == nk - 1)
        def _():
            finish(acc[...] + part)

    if mode == "tn":
        a_spec = pl.BlockSpec((tk, tm), lambda i, j, k: (k, i))
    else:
        a_spec = pl.BlockSpec((tm, tk), lambda i, j, k: (i, k))
    if b_slots and mode == "nn":
        b_spec = pl.BlockSpec((None, tk, tn), lambda i, j, k: (j + col0, k, 0))
    elif b_slots:
        b_spec = pl.BlockSpec((b.shape[0], tn, slot_w), lambda i, j, k: (0, j, 0))
    elif mode == "nt":
        b_spec = pl.BlockSpec((tn, tk), lambda i, j, k: (j, k + col0))
    else:
        b_spec = pl.BlockSpec((tk, tn), lambda i, j, k: (k, j + col0))
    o_spec = pl.BlockSpec((tm, tn), lambda i, j, k: (i, j))
    in_specs = [a_spec, b_spec] + ([o_spec] if add is not None else [])
    args = (a, b) + ((add,) if add is not None else ())
    if epilogue is not None:
        in_specs.append(o_spec)
        args += (epilogue[1],)
    if after is not None:
        in_specs.append(pl.BlockSpec(after.shape, lambda i, j, k: (0, 0)))
        args += (after,)
    out_shape = jax.ShapeDtypeStruct((M, N), out_dtype)
    if col_slots:
        assert add is None
        o_spec = pl.BlockSpec((None, tm, tn), lambda i, j, k: (j, i, 0))
        out_shape = jax.ShapeDtypeStruct((N_DEV, M, tn), out_dtype)
    return pl.pallas_call(
        body, name=name, grid=(M // tm, N // tn, nk),
        in_specs=in_specs, out_specs=o_spec,
        out_shape=out_shape,
        scratch_shapes=[pltpu.VMEM((tm, tn), F32)] if nk > 1 else [],
        compiler_params=_cparams("parallel", "parallel", "arbitrary"),
    )(*args)


class _NatIn:
    def __init__(self, ref):
        self.ref = ref

    def __getitem__(self, idx):
        v = jnp.swapaxes(self.ref[...], 0, 1)
        return v.reshape(v.shape[0] * N_SEG, v.shape[2])


class _NatOut:
    def __init__(self, ref):
        self.ref = ref

    def __setitem__(self, idx, val):
        self.ref[...] = jnp.swapaxes(val.reshape(val.shape[0] // N_SEG, N_SEG, val.shape[1]), 0, 1)


def _rowcall(body, name, n_rows, ins, outs, tile_rows=256):
    tr = _tile(n_rows, tile_rows, SUBLANES * 2)
    n_in = len(ins)
    in_kinds = [k for _, k in ins]
    kinds = [k for _, _, k in outs]

    def kern(*refs):
        @pl.when(pl.program_id(0) == 0)
        def _():
            for r, kind in zip(refs[n_in:], kinds):
                if kind == "acc":
                    r[...] = jnp.zeros_like(r)

        wrapped = [_NatIn(r) if k == "nat" else r for r, k in zip(refs[:n_in], in_kinds)]
        wrapped += [_NatOut(r) if k == "nat" else r for r, k in zip(refs[n_in:], kinds)]
        body(*wrapped)

    in_specs, args = [], []
    for arr, kind in ins:
        if kind == "row":
            in_specs.append(pl.BlockSpec((tr, arr.shape[1]), lambda i: (i, 0)))
        elif kind == "nat":
            in_specs.append(pl.BlockSpec((N_SEG, tr // N_SEG, arr.shape[1]), lambda i: (0, i, 0)))
            arr = arr.reshape(N_SEG, n_rows // N_SEG, arr.shape[1])
        else:
            in_specs.append(pl.BlockSpec(arr.shape, lambda i, nd=arr.ndim: (0,) * nd))
        args.append(arr)
    out_specs, out_shape = [], []
    for width, dtype, kind in outs:
        if kind == "row":
            out_specs.append(pl.BlockSpec((tr, width), lambda i: (i, 0)))
            out_shape.append(jax.ShapeDtypeStruct((n_rows, width), dtype))
        elif kind == "right":
            out_specs.append(pl.BlockSpec((tr, width), lambda i: (i, 1)))
            out_shape.append(jax.ShapeDtypeStruct((n_rows, 2 * width), dtype))
        elif kind == "nat":
            out_specs.append(pl.BlockSpec((N_SEG, tr // N_SEG, width), lambda i: (0, i, 0)))
            out_shape.append(jax.ShapeDtypeStruct((N_SEG, n_rows // N_SEG, width), dtype))
        else:
            out_specs.append(pl.BlockSpec((1, width), lambda i: (0, 0)))
            out_shape.append(jax.ShapeDtypeStruct((1, width), F32))
    res = pl.pallas_call(
        kern, name=name, grid=(n_rows // tr,), in_specs=in_specs, out_specs=out_specs, out_shape=out_shape,
        compiler_params=_cparams("arbitrary"),
    )(*args)
    return [r.reshape(n_rows, r.shape[2]) if k == "nat" else r for r, k in zip(res, kinds)]


def _rstd(x):
    return lax.rsqrt(jnp.mean(x * x, axis=-1, keepdims=True) + RMS_EPS)


def _rms_bwd(x, g, dy):
    xh = x * _rstd(x)
    dxh = dy * g
    dx = _rstd(x) * (dxh - xh * jnp.mean(dxh * xh, axis=-1, keepdims=True))
    return dx, jnp.sum(dy * xh, axis=0, keepdims=True)


def _silu(z):
    return z * jax.nn.sigmoid(z)


def _norm_cast(x, g, name, x_kind="row"):
    def body(x_ref, g_ref, o_ref):
        x = x_ref[...]
        o_ref[...] = (x * _rstd(x) * g_ref[...]).astype(BF16)

    return _rowcall(body, name, x.shape[0], [(x, x_kind), (g, "full")], [(x.shape[1], BF16, "row")])[0]


def _resid_norm2(x, o, g_post, g_kv, g_pre, name):
    def body(x_ref, o_ref, go_ref, gk_ref, gp_ref, h_ref, nk_ref, np_ref):
        o = o_ref[...]
        h = x_ref[...] + o * _rstd(o) * go_ref[...]
        h_ref[...] = h
        hn = h * _rstd(h)
        nk_ref[...] = (hn * gk_ref[...]).astype(BF16)
        np_ref[...] = (hn * gp_ref[...]).astype(BF16)

    d = x.shape[1]
    return _rowcall(body, name, x.shape[0], [(x, "nat"), (o, "row"), (g_post, "full"), (g_kv, "full"), (g_pre, "full")],
                    [(d, F32, "nat"), (d, BF16, "nat"), (d, BF16, "nat")])


def _post_norm_loss(o, g, h1, target, name):
    d = o.shape[1]

    def body(o_ref, g_ref, h_ref, t_ref, dh_ref, do_ref, acc_ref, dg_ref):
        o = o_ref[...]
        e = h_ref[...] + o * _rstd(o) * g_ref[...] - t_ref[...]
        dh = e * (1.0 / d)
        dh_ref[...] = dh
        acc_ref[...] += jnp.sum(e * e, axis=0, keepdims=True)
        dx, dg = _rms_bwd(o, g_ref[...], dh)
        do_ref[...] = dx.astype(BF16)
        dg_ref[...] += dg

    return _rowcall(body, name, o.shape[0], [(o, "row"), (g, "full"), (h1, "row"), (target, "row")],
                    [(d, F32, "row"), (d, BF16, "row"), (d, F32, "acc"), (d, F32, "acc")])


def _gate_bwd(d_oz, o, z, name):
    def body(d_ref, o_ref, z_ref, do_ref, dz_ref):
        _, vjp = jax.vjp(lambda o, z: o * _silu(z), o_ref[...], z_ref[...].astype(F32))
        do, dz = vjp(d_ref[...].astype(F32))
        do_ref[...] = do.astype(BF16)
        dz_ref[...] = dz.astype(BF16)

    w = o.shape[1]
    return _rowcall(body, name, o.shape[0], [(d_oz, "row"), (o, "row"), (z, "row")], [(w, BF16, "row"), (w, BF16, "right")])


def _norm_bwd2(dh2, h1, dxn1, dhn_kv, g_pre, g_kv, o0, g_post0, name):
    def body(dh2_ref, h_ref, d1_ref, dk_ref, gp_ref, gk_ref, o_ref, go_ref, dh1_ref, do_ref, dgp_ref, dgk_ref, dgo_ref):
        h = h_ref[...]
        dx1, dg1 = _rms_bwd(h, gp_ref[...], d1_ref[...].astype(F32))
        dxk, dgk = _rms_bwd(h, gk_ref[...], dk_ref[...].astype(F32))
        dh1 = dh2_ref[...] + dx1 + dxk
        dh1_ref[...] = dh1
        dgp_ref[...] += dg1
        dgk_ref[...] += dgk
        dxo, dgo = _rms_bwd(o_ref[...], go_ref[...], dh1)
        do_ref[...] = dxo.astype(BF16)
        dgo_ref[...] += dgo

    d = h1.shape[1]
    return _rowcall(body, name, h1.shape[0],
                    [(dh2, "nat"), (h1, "nat"), (dxn1, "nat"), (dhn_kv, "nat"), (g_pre, "full"), (g_kv, "full"),
                     (o0, "row"), (g_post0, "full")],
                    [(d, F32, "nat"), (d, BF16, "row"), (d, F32, "acc"), (d, F32, "acc"), (d, F32, "acc")])


def _norm_bwd1(dres, x, dxn, g, name):
    def body(dr_ref, x_ref, dn_ref, g_ref, dx_ref, dg_ref):
        dx, dg = _rms_bwd(x_ref[...], g_ref[...], dn_ref[...].astype(F32))
        dx_ref[...] = dr_ref[...] + dx
        dg_ref[...] += dg

    d = x.shape[1]
    return _rowcall(body, name, x.shape[0], [(dres, "nat"), (x, "nat"), (dxn, "row"), (g, "full")],
                    [(d, F32, "nat"), (d, F32, "acc")])


def _s5_gate(y_ssm, gp, b_glu, z, name):
    def body(y_ref, gp_ref, b_ref, z_ref, o_ref):
        yg = jax.nn.gelu(y_ref[...])
        o_ref[...] = (yg * jax.nn.sigmoid(gp_ref[...] + b_ref[...]) * _silu(z_ref[...].astype(F32))).astype(BF16)

    return _rowcall(body, name, y_ssm.shape[0], [(y_ssm, "row"), (gp, "row"), (b_glu, "full"), (z, "row")],
                    [(y_ssm.shape[1], BF16, "row")])[0]


def _s5_gate_bwd(dy3, y_ssm, gp, b_glu, z, name):
    def body(d_ref, y_ref, gp_ref, b_ref, z_ref, dz_ref, dgp_ref, dyg_ref, db_ref):
        yg = jax.nn.gelu(y_ref[...])
        _, vjp = jax.vjp(lambda yg, gp, z: yg * jax.nn.sigmoid(gp) * _silu(z), yg, gp_ref[...] + b_ref[...],
                         z_ref[...].astype(F32))
        dyg, dgp, dz = vjp(d_ref[...].astype(F32))
        dz_ref[...] = dz.astype(BF16)
        dgp_ref[...] = dgp.astype(BF16)
        dyg_ref[...] = dyg
        db_ref[...] += jnp.sum(dgp, axis=0, keepdims=True)

    w = y_ssm.shape[1]
    return _rowcall(body, name, y_ssm.shape[0],
                    [(dy3, "row"), (y_ssm, "row"), (gp, "row"), (b_glu, "full"), (z, "row")],
                    [(w, BF16, "right"), (w, BF16, "row"), (w, F32, "row"), (w, F32, "acc")])


def _cast_bf16(x, name):
    r, c = x.shape
    by_cols = r % (2 * SUBLANES) != 0
    tr, tc = (r, _tile(c, 256)) if by_cols else (_tile(r, 512, 2 * SUBLANES), c)
    pos = (lambda i: (0, i)) if by_cols else (lambda i: (i, 0))

    def body(x_ref, o_ref):
        o_ref[...] = x_ref[...].astype(BF16)

    return pl.pallas_call(
        body, name=name, grid=(c // tc if by_cols else r // tr,),
        in_specs=[pl.BlockSpec((tr, tc), pos)], out_specs=pl.BlockSpec((tr, tc), pos),
        out_shape=jax.ShapeDtypeStruct((r, c), BF16), compiler_params=_cparams("parallel"),
    )(x)


def _concat_cast(a, b, name):
    def body(a_ref, b_ref, o_ref):
        w = a_ref.shape[1]
        o_ref[:, :w] = a_ref[...].astype(BF16)
        o_ref[:, w:] = b_ref[...].astype(BF16)

    return _rowcall(body, name, a.shape[0], [(a, "row"), (b, "row")], [(a.shape[1] + b.shape[1], BF16, "row")])[0]


def _disc(ar, ai, ldt):
    dt = jnp.exp(ldt)
    mag = jnp.exp(ar * dt)
    abr = mag * jnp.cos(ai * dt)
    abi = mag * jnp.sin(ai * dt)
    den = ar * ar + ai * ai
    nr = abr - 1.0
    return abr, abi, (nr * ar + abi * ai) / den, (abi * ar - nr * ai) / den


def _s5_disc_fwd(a_re, a_im, ldt):
    def body(ar, ai, ld, o1, o2, o3, o4):
        o1[...], o2[...], o3[...], o4[...] = _disc(ar[...], ai[...], ld[...])

    sh = jax.ShapeDtypeStruct(a_re.shape, F32)
    return pl.pallas_call(body, name="s5_disc_fwd", out_shape=(sh, sh, sh, sh))(a_re, a_im, ldt)


def _s5_disc_bwd(a_re, a_im, ldt, d_abr, d_abi, d_cr, d_ci):
    def body(ar, ai, ld, g1, g2, g3, g4, o1, o2, o3):
        _, vjp = jax.vjp(_disc, ar[...], ai[...], ld[...])
        o1[...], o2[...], o3[...] = vjp((g1[...], g2[...], g3[...], g4[...]))

    sh = jax.ShapeDtypeStruct(a_re.shape, F32)
    return pl.pallas_call(body, name="s5_disc_bwd", out_shape=(sh, sh, jax.ShapeDtypeStruct(ldt.shape, F32)))(
        a_re, a_im, ldt, d_abr, d_abi, d_cr, d_ci)


def _bbar(cr, ci, br, bi):
    return cr * br - ci * bi, cr * bi + ci * br


def _s5_bbar_fwd(cr_col, ci_col, b_re, b_im):
    def body(cr, ci, br, bi, o1, o2):
        o1[...], o2[...] = _bbar(cr[...], ci[...], br[...], bi[...])

    w = b_re.shape[1]
    return _rowcall(body, "s5_bbar_fwd", b_re.shape[0], [(cr_col, "row"), (ci_col, "row"), (b_re, "row"), (b_im, "row")],
                    [(w, F32, "row"), (w, F32, "row")], tile_rows=1024)


def _s5_bbar_bwd(cr_col, ci_col, b_re, b_im, d_re, d_im):
    def body(cr, ci, br, bi, g1, g2, o1, o2, o3, o4):
        _, vjp = jax.vjp(_bbar, cr[...], ci[...], br[...], bi[...])
        o1[...], o2[...], o3[...], o4[...] = vjp((g1[...], g2[...]))

    w = b_re.shape[1]
    return _rowcall(body, "s5_bbar_bwd", b_re.shape[0],
                    [(cr_col, "row"), (ci_col, "row"), (b_re, "row"), (b_im, "row"), (d_re, "row"), (d_im, "row")],
                    [(1, F32, "row"), (1, F32, "row"), (w, F32, "row"), (w, F32, "row")], tile_rows=1024)


def _block_diag_in(t):
    g, p, c = t.shape
    nb = g // GROUPS_PER_BLOCK
    t4 = t.reshape(nb, GROUPS_PER_BLOCK, p, c).transpose(0, 1, 3, 2)
    eye = jnp.eye(GROUPS_PER_BLOCK, dtype=t.dtype)
    return (t4[:, :, :, None, :] * eye[None, :, None, :, None]).reshape(nb, GROUPS_PER_BLOCK * c, GROUPS_PER_BLOCK * p)


def _block_diag_in_extract(d, p, c):
    nb = d.shape[0]
    d5 = d.reshape(nb, GROUPS_PER_BLOCK, c, GROUPS_PER_BLOCK, p)
    diag = jnp.stack([d5[:, g, :, g, :] for g in range(GROUPS_PER_BLOCK)], axis=1)
    return diag.transpose(0, 1, 3, 2).reshape(nb * GROUPS_PER_BLOCK, p, c)


def _block_diag_out(t):
    g, c, p = t.shape
    nb = g // GROUPS_PER_BLOCK
    t4 = t.reshape(nb, GROUPS_PER_BLOCK, c, p).transpose(0, 1, 3, 2)
    eye = jnp.eye(GROUPS_PER_BLOCK, dtype=t.dtype)
    return (t4[:, :, :, None, :] * eye[None, :, None, :, None]).reshape(nb, GROUPS_PER_BLOCK * p, GROUPS_PER_BLOCK * c)


def _block_diag_out_extract(d, c, p):
    nb = d.shape[0]
    d5 = d.reshape(nb, GROUPS_PER_BLOCK, p, GROUPS_PER_BLOCK, c)
    diag = jnp.stack([d5[:, g, :, g, :] for g in range(GROUPS_PER_BLOCK)], axis=1)
    return diag.transpose(0, 1, 3, 2).reshape(nb * GROUPS_PER_BLOCK, c, p)


def _scan_step(ar, ai, hr, hi, xr, xi):
    return ar * hr - ai * hi + xr, ar * hi + ai * hr + xi


def _s5_blocks_per_step(nb, full):
    want = 2 if full else 4
    while nb % want:
        want //= 2
    return want


def _s5_scan_fwd(u, bd_re, bd_im, cd_re, cd_im, ab_re, ab_im, init_re, init_im, d_row, full, name):
    s, w = u.shape
    nb = w // LANES
    rows = _tile(s, 512, SUBLANES)
    nc = s // rows
    steps = rows // N_SEG
    ns = nb * BLOCK_STATE

    nblk = _s5_blocks_per_step(nb, full)

    def body(u_ref, bdr, bdi, cdr, cdi, ar_ref, ai_ref, ir_ref, ii_ref, d_ref, *outs):
        if full:
            y_ref, yg_ref, hr_ref, hi_ref, er_ref, ei_ref, cr, ci = outs
        else:
            er_ref, ei_ref, hr_ref, hi_ref, cr, ci = outs
        c = pl.program_id(1)
        cols = lambda b, width: slice(b * width, (b + 1) * width)

        @pl.when(c == 0)
        def _():
            cr[...] = ir_ref[...]
            ci[...] = ii_ref[...]

        for b in range(nblk):
            ub = u_ref[:, cols(b, LANES)].astype(BF16)
            hr_ref[:, cols(b, BLOCK_STATE)] = jnp.dot(ub, bdr[b], preferred_element_type=F32)
            hi_ref[:, cols(b, BLOCK_STATE)] = jnp.dot(ub, bdi[b], preferred_element_type=F32)
        ar, ai = ar_ref[...], ai_ref[...]

        hr, hi = cr[...], ci[...]
        for j in range(steps):
            rows_j = pl.ds(j * N_SEG, N_SEG)
            hr, hi = _scan_step(ar, ai, hr, hi, hr_ref[rows_j, :], hi_ref[rows_j, :])
            hr_ref[rows_j, :] = hr
            hi_ref[rows_j, :] = hi
        cr[...] = hr
        ci[...] = hi
        if full:
            for b in range(nblk):
                st_b, ln_b = cols(b, BLOCK_STATE), cols(b, LANES)
                y = (jnp.dot(hr_ref[:, st_b].astype(BF16), cdr[b], preferred_element_type=F32)
                     + jnp.dot(hi_ref[:, st_b].astype(BF16), cdi[b], preferred_element_type=F32)
                     + d_ref[:, ln_b] * u_ref[:, ln_b])
                y_ref[:, ln_b] = y
                yg_ref[:, ln_b] = jax.nn.gelu(y).astype(BF16)

        @pl.when(c == nc - 1)
        def _():
            er_ref[...] = hr
            ei_ref[...] = hi

    lanes, states = LANES * nblk, BLOCK_STATE * nblk
    blk3 = lambda a: pl.BlockSpec((nblk,) + a.shape[1:], lambda k, c: (k, 0, 0))
    seg = pl.BlockSpec((N_SEG, states), lambda k, c: (0, k))
    st = pl.BlockSpec((rows, states), lambda k, c: (c, k))
    in_specs = [pl.BlockSpec((rows, lanes), lambda k, c: (c, k)), blk3(bd_re), blk3(bd_im), blk3(cd_re), blk3(cd_im),
                seg, seg, seg, seg, pl.BlockSpec((1, lanes), lambda k, c: (0, k))]
    seg_shape = jax.ShapeDtypeStruct((N_SEG, ns), F32)
    st_shape = jax.ShapeDtypeStruct((s, ns), F32)
    carry = [pltpu.VMEM((N_SEG, states), F32)] * 2
    if full:
        ych = pl.BlockSpec((rows, lanes), lambda k, c: (c, k))
        out_specs = [ych, ych, st, st, seg, seg]
        out_shape = [jax.ShapeDtypeStruct((s, w), F32), jax.ShapeDtypeStruct((s, w), BF16), st_shape, st_shape, seg_shape, seg_shape]
        scratch = carry
    else:
        out_specs = [seg, seg]
        out_shape = [seg_shape, seg_shape]
        scratch = [pltpu.VMEM((rows, states), F32)] * 2 + carry
    return pl.pallas_call(
        body, name=name, grid=(nb // nblk, nc), in_specs=in_specs, out_specs=out_specs, out_shape=out_shape,
        scratch_shapes=scratch, compiler_params=_cparams("parallel", "arbitrary"),
    )(u, bd_re, bd_im, cd_re, cd_im, ab_re, ab_im, init_re, init_im, d_row)


def _s5_seg_fix(e_re, e_im, ab_re, ab_im, seg_len, reverse, name):
    assert seg_len & (seg_len - 1) == 0

    def body(er, ei, ar, ai, o_re, o_im):
        pr, pi = ar[0:1, :], ai[0:1, :]
        for _ in range(int(math.log2(seg_len))):
            pr, pi = pr * pr - pi * pi, 2.0 * pr * pi
        tr = jnp.zeros_like(pr)
        ti = jnp.zeros_like(pr)
        order = list(range(N_SEG - 1, -1, -1)) if reverse else list(range(N_SEG))
        for n, sgm in enumerate(order):
            o_re[sgm:sgm + 1, :] = tr
            o_im[sgm:sgm + 1, :] = ti
            if n < N_SEG - 1:
                tr, ti = _scan_step(pr, pi, tr, ti, er[sgm:sgm + 1, :], ei[sgm:sgm + 1, :])

    sh = jax.ShapeDtypeStruct(e_re.shape, F32)
    return pl.pallas_call(body, name=name, out_shape=(sh, sh))(e_re, e_im, ab_re, ab_im)


def _s5_scan_bwd(dy, u, h_re, h_im, bd_re, bd_im, cd_re, cd_im, ab_re, ab_imn, gin_re, gin_im, d_row, full, name, duz=None):
    s, w = u.shape
    nb = w // LANES
    rows = _tile(s, 512, SUBLANES)
    nc = s // rows
    steps = rows // N_SEG
    ns = nb * BLOCK_STATE

    nblk = _s5_blocks_per_step(nb, full)

    def body(dy_ref, u_ref, hr_ref, hi_ref, bdr, bdi, cdr, cdi, ar_ref, ai_ref, ir_ref, ii_ref, d_ref, *outs):
        if full:
            _, du_ref, dbr_ref, dbi_ref, dcr_ref, dci_ref, dar_ref, dai_ref, dd_ref, gr, gi, accr, acci = outs
        else:
            er_ref, ei_ref, gr, gi = outs
        c = pl.program_id(1)
        cols = lambda b, width: slice(b * width, (b + 1) * width)

        @pl.when(c == 0)
        def _():
            gr[pl.ds(rows, N_SEG), :] = ir_ref[...]
            gi[pl.ds(rows, N_SEG), :] = ii_ref[...]
            if full:
                for r in (dbr_ref, dbi_ref, dcr_ref, dci_ref, dd_ref, accr, acci):
                    r[...] = jnp.zeros_like(r)

        nt = (_DOT_DIMS["nt"], ((), ()))
        tn = (_DOT_DIMS["tn"], ((), ()))
        for b in range(nblk):
            dyb = dy_ref[:, cols(b, LANES)].astype(BF16)
            gr[pl.ds(0, rows), cols(b, BLOCK_STATE)] = lax.dot_general(dyb, cdr[b], nt, preferred_element_type=F32)
            gi[pl.ds(0, rows), cols(b, BLOCK_STATE)] = lax.dot_general(dyb, cdi[b], nt, preferred_element_type=F32)
        ar, ai = ar_ref[...], ai_ref[...]

        g0r, g0i = gr[pl.ds(rows, N_SEG), :], gi[pl.ds(rows, N_SEG), :]
        for j in range(steps - 1, -1, -1):
            rows_j = pl.ds(j * N_SEG, N_SEG)
            g0r, g0i = _scan_step(ar, ai, g0r, g0i, gr[rows_j, :], gi[rows_j, :])
            gr[rows_j, :] = g0r
            gi[rows_j, :] = g0i
        if full:
            for b in range(nblk):
                st_b, ln_b = cols(b, BLOCK_STATE), cols(b, LANES)
                hr, hi = hr_ref[:, st_b], hi_ref[:, st_b]
                gnr, gni = gr[pl.ds(N_SEG, rows), st_b], gi[pl.ds(N_SEG, rows), st_b]
                accr[:, st_b] += jnp.sum((gnr * hr + gni * hi).reshape(steps, N_SEG, BLOCK_STATE), axis=0)
                acci[:, st_b] += jnp.sum((gni * hr - gnr * hi).reshape(steps, N_SEG, BLOCK_STATE), axis=0)
                dyb = dy_ref[:, ln_b].astype(BF16)
                ub = u_ref[:, ln_b].astype(BF16)
                gbr, gbi = gr[pl.ds(0, rows), st_b].astype(BF16), gi[pl.ds(0, rows), st_b].astype(BF16)
                dcr_ref[b] += lax.dot_general(hr.astype(BF16), dyb, tn, preferred_element_type=F32)
                dci_ref[b] += lax.dot_general(hi.astype(BF16), dyb, tn, preferred_element_type=F32)
                dbr_ref[b] += lax.dot_general(ub, gbr, tn, preferred_element_type=F32)
                dbi_ref[b] += lax.dot_general(ub, gbi, tn, preferred_element_type=F32)
                du_ref[:, ln_b] = (lax.dot_general(gbr, bdr[b], nt, preferred_element_type=F32)
                                   + lax.dot_general(gbi, bdi[b], nt, preferred_element_type=F32)
                                   + d_ref[:, ln_b] * dy_ref[:, ln_b]).astype(BF16)
                dd_ref[:, ln_b] += jnp.sum(dy_ref[:, ln_b] * u_ref[:, ln_b], axis=0, keepdims=True)
        gr[pl.ds(rows, N_SEG), :] = g0r
        gi[pl.ds(rows, N_SEG), :] = g0i

        @pl.when(c == nc - 1)
        def _():
            if full:
                dar_ref[...] = jnp.sum(accr[...], axis=0, keepdims=True)
                dai_ref[...] = jnp.sum(acci[...], axis=0, keepdims=True)
            else:
                er_ref[...] = g0r
                ei_ref[...] = g0i

    lanes, states = LANES * nblk, BLOCK_STATE * nblk
    rev = lambda k, c: (nc - 1 - c, k)
    blk3 = lambda a: pl.BlockSpec((nblk,) + a.shape[1:], lambda k, c: (k, 0, 0))
    seg = pl.BlockSpec((N_SEG, states), lambda k, c: (0, k))
    st = pl.BlockSpec((rows, states), rev)
    ch = pl.BlockSpec((rows, lanes), rev)
    vec = pl.BlockSpec((1, lanes), lambda k, c: (0, k))
    if not full:
        st = pl.BlockSpec((rows, states), lambda k, c: (0, k))
    in_specs = [ch, ch if full else pl.BlockSpec((rows, lanes), lambda k, c: (0, k)), st, st,
                blk3(bd_re), blk3(bd_im), blk3(cd_re), blk3(cd_im), seg, seg, seg, seg, vec]
    args = [dy, u, h_re, h_im, bd_re, bd_im, cd_re, cd_im, ab_re, ab_imn, gin_re, gin_im, d_row]
    gbuf = [pltpu.VMEM((rows + N_SEG, states), F32)] * 2
    if full:
        row1 = pl.BlockSpec((1, states), lambda k, c: (0, k))
        out_specs = [ch, blk3(bd_re), blk3(bd_im), blk3(cd_re), blk3(cd_im), row1, row1, vec]
        out_shape = [jax.ShapeDtypeStruct(duz.shape, BF16),
                     jax.ShapeDtypeStruct(bd_re.shape, F32), jax.ShapeDtypeStruct(bd_im.shape, F32),
                     jax.ShapeDtypeStruct(cd_re.shape, F32), jax.ShapeDtypeStruct(cd_im.shape, F32),
                     jax.ShapeDtypeStruct((1, ns), F32), jax.ShapeDtypeStruct((1, ns), F32),
                     jax.ShapeDtypeStruct((1, w), F32)]
        scratch = gbuf + [pltpu.VMEM((N_SEG, states), F32)] * 2
        in_specs.append(pl.BlockSpec(memory_space=pl.ANY))
        args.append(duz)
        aliases = {len(args) - 1: 0}
    else:
        out_specs = [seg, seg]
        out_shape = [jax.ShapeDtypeStruct((N_SEG, ns), F32)] * 2
        scratch = gbuf
        aliases = {}
    return pl.pallas_call(
        body, name=name, grid=(nb // nblk, nc), in_specs=in_specs, out_specs=out_specs, out_shape=out_shape,
        input_output_aliases=aliases, scratch_shapes=scratch, compiler_params=_cparams("parallel", "arbitrary"),
    )(*args)


def _log_sigmoid(x):
    return jnp.minimum(x, 0.0) - jnp.log(1.0 + jnp.exp(-jnp.abs(x)))


def _tri(n, upper):
    r = lax.broadcasted_iota(jnp.int32, (n, n), 0)
    c = lax.broadcasted_iota(jnp.int32, (n, n), 1)
    return jnp.where((c >= r) if upper else (r >= c), 1.0, 0.0).astype(F32)


def _cum_fwd(f_logit, b_row, name):
    s, w = f_logit.shape
    t = _tile(s, 256, SUBLANES)

    def body(f_ref, b_ref, o_ref, carry):
        @pl.when(pl.program_id(0) == 0)
        def _():
            carry[...] = jnp.zeros_like(carry)

        lf = _log_sigmoid(f_ref[...] + b_ref[...])
        cum = jnp.dot(_tri(t, False), lf, precision=lax.Precision.HIGHEST, preferred_element_type=F32) + carry[...]
        o_ref[...] = cum * LOG2E
        carry[...] = cum[t - 1:t, :]

    return pl.pallas_call(
        body, name=name, grid=(s // t,),
        in_specs=[pl.BlockSpec((t, w), lambda i: (i, 0)), pl.BlockSpec((1, w), lambda i: (0, 0))],
        out_specs=pl.BlockSpec((t, w), lambda i: (i, 0)), out_shape=jax.ShapeDtypeStruct((s, w), F32),
        scratch_shapes=[pltpu.VMEM((1, w), F32)], compiler_params=_cparams("arbitrary"),
    )(f_logit, b_row)


def _cum_bwd(dcq, dck, f_logit, b_row, name):
    s, w = f_logit.shape
    t = _tile(s, 256, SUBLANES)
    nt = s // t

    def body(q_ref, k_ref, f_ref, b_ref, df_ref, db_ref, carry):
        @pl.when(pl.program_id(0) == 0)
        def _():
            carry[...] = jnp.zeros_like(carry)
            db_ref[...] = jnp.zeros_like(db_ref)

        dc = q_ref[...] - k_ref[...]
        rc = jnp.dot(_tri(t, True), dc, precision=lax.Precision.HIGHEST, preferred_element_type=F32) + carry[...]
        carry[...] = rc[0:1, :]
        df = rc * (1.0 - jax.nn.sigmoid(f_ref[...] + b_ref[...]))
        df_ref[...] = df.astype(BF16)
        db_ref[...] += jnp.sum(df, axis=0, keepdims=True)

    rev = pl.BlockSpec((t, w), lambda i: (nt - 1 - i, 0))
    one = pl.BlockSpec((1, w), lambda i: (0, 0))
    return pl.pallas_call(
        body, name=name, grid=(nt,), in_specs=[rev, rev, rev, one], out_specs=[rev, one],
        out_shape=[jax.ShapeDtypeStruct((s, w), BF16), jax.ShapeDtypeStruct((1, w), F32)],
        scratch_shapes=[pltpu.VMEM((1, w), F32)], compiler_params=_cparams("arbitrary"),
    )(dcq, dck, f_logit, b_row)


def _head_col(cum_tile, h):
    lane = lax.broadcasted_iota(jnp.int32, cum_tile.shape, 1)
    return jnp.sum(jnp.where(lane == h, cum_tile, 0.0), axis=1, keepdims=True)


def _attn_tiles(s):
    return _tile(s, 512, LANES)


def _exp2_rows(sc, sub):
    return jnp.concatenate([jnp.exp2(sc[:, b * LANES:(b + 1) * LANES] - sub) for b in range(sc.shape[1] // LANES)], axis=1)


def _row_of(rep):
    return jnp.transpose(rep)[0:1, :]


def _causal(sc, keys_on_rows):
    r = lax.broadcasted_iota(jnp.int32, sc.shape, 0)
    c = lax.broadcasted_iota(jnp.int32, sc.shape, 1)
    return jnp.where((r <= c) if keys_on_rows else (c <= r), sc, NEG_INF)


def _fox_fwd(q2, kv, cum2_t, z, name):
    s, w = q2.shape
    nh = w // HEAD_DIM
    tq = _attn_tiles(s)
    nq = s // tq
    nt = (_DOT_DIMS["nt"], ((), ()))

    def body(q_ref, k_ref, v_ref, ct_ref, z_ref, o_ref, oz_ref, lse_row_ref, m_s, acc_s, vaug, s_buf):
        i = pl.program_id(1)

        @pl.when(i == 0)
        def _():
            vaug[:, :HEAD_DIM] = v_ref[...]
            vaug[:, HEAD_DIM:] = jnp.ones((s, LANES), BF16)

        qb = q_ref[...]
        m_s[...] = jnp.full_like(m_s, NEG_INF)
        acc_s[...] = jnp.zeros_like(acc_s)

        def scores(j):
            off = pl.multiple_of(j * tq, tq)
            return lax.dot_general(qb, k_ref[pl.ds(off, tq), :], nt, preferred_element_type=F32) - ct_ref[:, pl.ds(off, tq)]

        def softmax_pv(j, sc):
            m_old = m_s[...]
            m_new = jnp.maximum(m_old, jnp.max(sc, axis=1, keepdims=True))
            p = _exp2_rows(sc, m_new)
            alpha = jnp.exp2(m_old - m_new)
            pv = jnp.dot(p.astype(BF16), vaug[pl.ds(pl.multiple_of(j * tq, tq), tq), :], preferred_element_type=F32)
            acc_s[...] = jnp.concatenate([alpha, alpha], axis=1) * acc_s[...] + pv
            m_s[...] = m_new

        s_buf[...] = scores(0)

        def loop(j, carry):
            nxt = scores(j + 1)
            softmax_pv(j, s_buf[...])
            s_buf[...] = nxt
            return carry

        lax.fori_loop(0, i, loop, 0)
        softmax_pv(i, _causal(s_buf[...], False))
        l = acc_s[:, HEAD_DIM:]
        o = acc_s[:, :HEAD_DIM] / l
        o_ref[...] = o
        oz_ref[...] = (o * _silu(z_ref[...].astype(F32))).astype(BF16)
        lse_row_ref[...] = _row_of(m_s[...] + jnp.log(l) * LOG2E)

    return pl.pallas_call(
        body, name=name, grid=(nh, nq),
        in_specs=[pl.BlockSpec((tq, HEAD_DIM), lambda h, i: (i, h)),
                  pl.BlockSpec((s, HEAD_DIM), lambda h, i: (0, h)),
                  pl.BlockSpec((s, HEAD_DIM), lambda h, i: (0, nh + h)),
                  pl.BlockSpec((None, 1, s), lambda h, i: (h, 0, 0)),
                  pl.BlockSpec((tq, HEAD_DIM), lambda h, i: (i, h))],
        out_specs=[pl.BlockSpec((tq, HEAD_DIM), lambda h, i: (i, h)),
                   pl.BlockSpec((tq, HEAD_DIM), lambda h, i: (i, h)),
                   pl.BlockSpec((None, 1, tq), lambda h, i: (h, 0, i))],
        out_shape=[jax.ShapeDtypeStruct((s, w), F32), jax.ShapeDtypeStruct((s, w), BF16),
                   jax.ShapeDtypeStruct((nh, 1, s), F32)],
        scratch_shapes=[pltpu.VMEM((tq, LANES), F32), pltpu.VMEM((tq, HEAD_DIM + LANES), F32),
                        pltpu.VMEM((s, HEAD_DIM + LANES), BF16), pltpu.VMEM((tq, tq), F32)],
        compiler_params=_cparams("arbitrary", "arbitrary"),
    )(q2, kv, kv, cum2_t, z)


def _fox_bwd(q2, kv, do, o, lse2_t, cum2, dqz, name):
    s, w = q2.shape
    nh = w // HEAD_DIM
    tk = _attn_tiles(s)
    nk = s // tk
    scale = HEAD_DIM ** -0.5
    nt = (_DOT_DIMS["nt"], ((), ()))
    tn = (_DOT_DIMS["tn"], ((), ()))

    def body(q_ref, k_ref, v_ref, do_ref, o_ref, lse_ref, c_ref, _, dk_ref, dv_ref, dq_ref, dcq_ref, dck_ref,
             dk_s, dv_s, dc_s, dq_s, dcq_s, dl_s, s_buf, dp_buf):
        h, j = pl.program_id(0), pl.program_id(1)

        @pl.when(j == 0)
        def _():
            dq_s[...] = jnp.zeros_like(dq_s)
            dcq_s[...] = jnp.zeros_like(dcq_s)
            for i in range(nk):
                rows = pl.ds(i * tk, tk)
                d = jnp.sum(do_ref[rows, :].astype(F32) * o_ref[rows, :], axis=1, keepdims=True)
                dl_s[:, i * tk:(i + 1) * tk] = _row_of(jnp.broadcast_to(d, (tk, LANES)))

        kb = k_ref[...]
        vb = v_ref[...]
        ck = jnp.broadcast_to(_head_col(c_ref[...], h), (tk, LANES))
        dk_s[...] = jnp.zeros_like(dk_s)
        dv_s[...] = jnp.zeros_like(dv_s)
        dc_s[...] = jnp.zeros_like(dc_s)

        def scores(i):
            off = pl.multiple_of(i * tk, tk)
            sc = lax.dot_general(kb, q_ref[pl.ds(off, tk), :], nt, preferred_element_type=F32) - lse_ref[:, pl.ds(off, tk)]
            dp = lax.dot_general(vb, do_ref[pl.ds(off, tk), :], nt, preferred_element_type=F32) - dl_s[:, pl.ds(off, tk)]
            return sc, dp

        def accumulate(i, sc, dp):
            off = pl.multiple_of(i * tk, tk)
            p = _exp2_rows(sc, ck)
            dv_s[...] += jnp.dot(p.astype(BF16), do_ref[pl.ds(off, tk), :], preferred_element_type=F32)
            ds = p * dp
            dsb = ds.astype(BF16)
            dk_s[...] += jnp.dot(dsb, q_ref[pl.ds(off, tk), :], preferred_element_type=F32)
            dq_s[pl.ds(off, tk), :] += lax.dot_general(dsb, kb, tn, preferred_element_type=F32)
            dcq_s[:, pl.ds(off, tk)] += jnp.sum(ds, axis=0, keepdims=True)
            part = ds[:, :LANES]
            for b in range(1, tk // LANES):
                part = part + ds[:, b * LANES:(b + 1) * LANES]
            dc_s[...] += part

        sc0, dp0 = scores(j)
        s_buf[...] = _causal(sc0, True)
        dp_buf[...] = dp0

        def loop(i, carry):
            nxt = scores(i + 1)
            accumulate(i, s_buf[...], dp_buf[...])
            s_buf[...], dp_buf[...] = nxt
            return carry

        lax.fori_loop(j, nk - 1, loop, 0)
        accumulate(nk - 1, s_buf[...], dp_buf[...])
        dk_ref[...] = (dk_s[...] * (1.0 / LOG2E)).astype(BF16)
        dv_ref[...] = dv_s[...].astype(BF16)
        dck_ref[...] = jnp.sum(jnp.transpose(dc_s[...]), axis=0, keepdims=True)

        @pl.when(j == nk - 1)
        def _():
            dq_ref[...] = (dq_s[...] * scale).astype(BF16)
            dcq_ref[...] = dcq_s[...]

    col = pl.BlockSpec((s, HEAD_DIM), lambda h, j: (0, h))
    row = pl.BlockSpec((None, 1, s), lambda h, j: (h, 0, 0))
    kspec = pl.BlockSpec((tk, HEAD_DIM), lambda h, j: (j, h))
    return pl.pallas_call(
        body, name=name, grid=(nh, nk),
        in_specs=[col, kspec, pl.BlockSpec((tk, HEAD_DIM), lambda h, j: (j, nh + h)), col, col, row,
                  pl.BlockSpec((tk, LANES), lambda h, j: (j, 0)), pl.BlockSpec(memory_space=pl.ANY)],
        out_specs=[kspec, kspec, col, row, pl.BlockSpec((None, 1, tk), lambda h, j: (h, 0, j))],
        out_shape=[jax.ShapeDtypeStruct((s, w), BF16), jax.ShapeDtypeStruct((s, w), BF16),
                   jax.ShapeDtypeStruct(dqz.shape, BF16), jax.ShapeDtypeStruct((nh, 1, s), F32),
                   jax.ShapeDtypeStruct((nh, 1, s), F32)],
        input_output_aliases={7: 2},
        scratch_shapes=[pltpu.VMEM((tk, HEAD_DIM), F32), pltpu.VMEM((tk, HEAD_DIM), F32), pltpu.VMEM((tk, LANES), F32),
                        pltpu.VMEM((s, HEAD_DIM), F32), pltpu.VMEM((1, s), F32), pltpu.VMEM((1, s), F32),
                        pltpu.VMEM((tk, tk), F32), pltpu.VMEM((tk, tk), F32)],
        compiler_params=_cparams("arbitrary", "arbitrary"),
    )(q2, kv, kv, do, o, lse2_t, cum2, dqz)


_ALL_PEERS = tuple(range(1, N_DEV))
_CHIP_PEERS = (1, 2, 4, 6)


def _exchange_copies(ins, outs, send_sems, recv_sems, local_sems, scatter, peers=_ALL_PEERS):
    x, y, c = (lax.axis_index(a) for a in MESH_AXES)
    me = 4 * x + 2 * y + c
    local, remote = [], []
    for a in range(len(ins)):
        local.append(pltpu.make_async_copy(ins[a].at[me] if scatter else ins[a], outs[a].at[me], local_sems.at[a]))
        for k in peers:
            px, py, pc = (1 - x if k & 4 else x), (1 - y if k & 2 else y), (1 - c if k & 1 else c)
            remote.append(pltpu.make_async_remote_copy(
                src_ref=ins[a].at[4 * px + 2 * py + pc] if scatter else ins[a], dst_ref=outs[a].at[me],
                send_sem=send_sems.at[a * (N_DEV - 1) + k - 1], recv_sem=recv_sems.at[a * (N_DEV - 1) + k - 1],
                device_id=(px, py, pc), device_id_type=pl.DeviceIdType.MESH))
    return local, remote


def _exchange_out_shapes(arrs, scatter):
    return [((N_DEV,) + a.shape[1:]) if scatter else ((N_DEV,) + a.shape) for a in arrs]


_HBM =pl.BlockSpec(memory_space=pltpu.HBM)
_SEM = pl.BlockSpec(memory_space=pltpu.SEMAPHORE)


def _exchange_start(arrs, scatter, name, after=(), peers=_ALL_PEERS):
    n = len(arrs)
    after = list(after)
    lands = [lax.empty(s, a.dtype) for s, a in zip(_exchange_out_shapes(arrs, scatter), arrs)]

    def body(*refs):
        ins, outs = refs[:n], refs[n:2 * n]
        send_sems, recv_sems, local_sems = refs[2 * n + len(after):2 * n + len(after) + 3]
        token = refs[-1]
        local, remote = _exchange_copies(ins, outs, send_sems, recv_sems, local_sems, scatter, peers)
        for cp in local + remote:
            cp.start()
        token[...] = jnp.zeros_like(token)

    hbm = lambda a: pltpu.HBM(a.shape, a.dtype)
    res = pl.pallas_call(
        body, name=name,
        out_shape=(pltpu.SemaphoreType.DMA((n * (N_DEV - 1),)), pltpu.SemaphoreType.DMA((n * (N_DEV - 1),)),
                   pltpu.SemaphoreType.DMA((n,)), *[hbm(a) for a in arrs], *[hbm(a) for a in lands],
                   jax.ShapeDtypeStruct((SUBLANES, LANES), F32)),
        in_specs=[_HBM] * (2 * n) + [pl.BlockSpec(memory_space=pl.ANY)] * len(after),
        out_specs=(_SEM, _SEM, _SEM, *[_HBM] * (2 * n), pl.BlockSpec(memory_space=pltpu.VMEM)),
        input_output_aliases={i: 3 + i for i in range(2 * n)},
        compiler_params=pltpu.CompilerParams(has_side_effects=pltpu.SideEffectType.DATAFLOW_SIDE_EFFECTING),
    )(*[pltpu.with_memory_space_constraint(a, pltpu.HBM) for a in list(arrs) + lands], *after)
    return (n, scatter, res[:3], res[3:3 + n], res[3 + n:3 + 2 * n], peers), res[-1]


def _exchange_wait(state, after, name):
    n, scatter, sems, srcs, lands, peers = state
    after = list(after) if isinstance(after, (list, tuple)) else [after]

    def body(*refs):
        ins, outs = refs[:n], refs[n:2 * n]
        send_sems, recv_sems, local_sems = refs[2 * n:2 * n + 3]
        local, remote = _exchange_copies(ins, outs, send_sems, recv_sems, local_sems, scatter, peers)
        for cp in remote:
            cp.wait_send()
            cp.wait_recv()
        for cp in local:
            cp.wait()

    hbm = lambda a: pltpu.HBM(a.shape, a.dtype)
    res = pl.pallas_call(
        body, name=name,
        out_shape=(*[hbm(a) for a in srcs], *[hbm(a) for a in lands]),
        in_specs=[_HBM] * (2 * n) + [_SEM] * 3 + [pl.BlockSpec(memory_space=pl.ANY)] * len(after),
        out_specs=tuple([_HBM] * (2 * n)),
        input_output_aliases={i: i for i in range(2 * n)},
        compiler_params=pltpu.CompilerParams(has_side_effects=pltpu.SideEffectType.DATAFLOW_SIDE_EFFECTING),
    )(*srcs, *lands, *sems, *after)
    return list(res[n:])


def _forward_to_sibling(slots, name):
    n = len(slots)
    hops = (2, 4, 6)

    def body(*refs):
        ins, outs, (send_sems, recv_sems) = refs[:n], refs[n:2 * n], refs[2 * n:]
        x, y, c = (lax.axis_index(a) for a in MESH_AXES)
        copies = []
        for a in range(n):
            for i, k in enumerate(hops):
                slot = 4 * (1 - x if k & 4 else x) + 2 * (1 - y if k & 2 else y) + c
                copies.append(pltpu.make_async_remote_copy(
                    src_ref=ins[a].at[slot], dst_ref=outs[a].at[slot],
                    send_sem=send_sems.at[a * len(hops) + i], recv_sem=recv_sems.at[a * len(hops) + i],
                    device_id=(x, y, 1 - c), device_id_type=pl.DeviceIdType.MESH))
        for cp in copies:
            cp.start()
        for cp in copies:
            cp.wait_send()
            cp.wait_recv()

    return pl.pallas_call(
        body, name=name, out_shape=[jax.ShapeDtypeStruct(s.shape, s.dtype) for s in slots],
        in_specs=[pl.BlockSpec(memory_space=pl.ANY)] * n, out_specs=[pl.BlockSpec(memory_space=pl.ANY)] * n,
        input_output_aliases={i: i for i in range(n)},
        scratch_shapes=[pltpu.SemaphoreType.DMA((n * len(hops),)), pltpu.SemaphoreType.DMA((n * len(hops),))],
    )(*slots)


def _adamw_math(w, g, m, v):
    m = ADAM_B1 * m + (1.0 - ADAM_B1) * g
    v = ADAM_B2 * v + (1.0 - ADAM_B2) * (g * g)
    m_hat = m / (1.0 - ADAM_B1 ** ADAM_STEP)
    v_hat = v / (1.0 - ADAM_B2 ** ADAM_STEP)
    return -ADAM_LR * (m_hat / (jnp.sqrt(v_hat) + ADAM_EPS) + ADAM_WD * w), m, v


def _slot_sum(p_ref):
    g = p_ref[0].astype(F32)
    for d in range(1, p_ref.shape[0]):
        g = g + p_ref[d].astype(F32)
    return g


def _adamw_tile(r, c):
    return _tile(r, max(SUBLANES, (256 * 1024) // c // SUBLANES * SUBLANES), SUBLANES)


def _adamw(parts, w, m, v, name):
    r, c = w.shape[-2:]
    by_cols = r % SUBLANES != 0
    tr, tc = (r, _tile(c, 256)) if by_cols else (_adamw_tile(r, c), c)

    def body(p_ref, w_ref, m_ref, v_ref, g_ref, d_ref, nm_ref, nv_ref):
        g = _slot_sum(p_ref)
        g_ref[...] = g
        d_ref[...], nm_ref[...], nv_ref[...] = _adamw_math(w_ref[...], g, m_ref[...], v_ref[...])

    pos = (lambda i: (0, i)) if by_cols else (lambda i: (i, 0))
    if w.ndim == 3:
        blk = pl.BlockSpec((None, tr, tc), lambda i: (0,) + pos(i))
    else:
        blk = pl.BlockSpec((tr, tc), pos)
    sh = jax.ShapeDtypeStruct(w.shape, F32)
    return pl.pallas_call(
        body, name=name, grid=(c // tc if by_cols else r // tr,),
        in_specs=[pl.BlockSpec((parts.shape[0], tr, tc), lambda i: (0,) + pos(i)), blk, blk, blk],
        out_specs=[blk] * 4, out_shape=[sh] * 4, compiler_params=_cparams("parallel"),
    )(parts, w, m, v)


def _sum_parts(parts, name):
    _, r, c = parts.shape
    tr = _adamw_tile(r, c)

    def body(p_ref, o_ref):
        o_ref[...] = _slot_sum(p_ref)

    return pl.pallas_call(
        body, name=name, grid=(r // tr,),
        in_specs=[pl.BlockSpec((parts.shape[0], tr, c), lambda i: (0, i, 0))],
        out_specs=pl.BlockSpec((tr, c), lambda i: (i, 0)), out_shape=jax.ShapeDtypeStruct((r, c), F32),
        compiler_params=_cparams("parallel"),
    )(parts)


def _lane_pad(a, width=LANES):
    return jnp.pad(a, ((0, 0), (0, width - a.shape[1])))


def _local_step(x, target, norm_pre, norm_post, kv_norm, kv_b_f, a_re, a_im, log_dt, b_re, b_im, c_re, c_im, comm):
    s, d = x.shape
    g, p = a_re.shape
    w = g * S5_GROUP
    fw = d
    nh = fw // HEAD_DIM
    seg_len = s // N_SEG
    row = lambda v: v.reshape(1, -1)
    g_pre0, g_pre1, g_post0, g_post1, g_kv = row(norm_pre[0]), row(norm_pre[1]), row(norm_post[0]), row(norm_post[1]), row(kv_norm)

    ldt = log_dt.reshape(g, 1)
    abr, abi, cr, ci = _s5_disc_fwd(a_re, a_im, ldt)
    cr_col, ci_col = cr.reshape(g * p, 1), ci.reshape(g * p, 1)
    b_re2, b_im2 = b_re.reshape(g * p, S5_GROUP), b_im.reshape(g * p, S5_GROUP)
    bb_re, bb_im = _s5_bbar_fwd(cr_col, ci_col, b_re2, b_im2)
    bd_re = _block_diag_in(bb_re.reshape(g, p, S5_GROUP)).astype(BF16)
    bd_im = _block_diag_in(bb_im.reshape(g, p, S5_GROUP)).astype(BF16)
    cd_re = _block_diag_out(c_re).astype(BF16)
    cd_im = _block_diag_out(-c_im).astype(BF16)
    ab_re = jnp.broadcast_to(abr.reshape(1, g * p), (N_SEG, g * p))
    ab_im = jnp.broadcast_to(abi.reshape(1, g * p), (N_SEG, g * p))
    zero_seg = jnp.zeros((N_SEG, g * p), F32)

    xn0 = _norm_cast(x, g_pre0 + comm.token, "norm_pre0", x_kind="nat")
    w_in = comm.weight("s5_w_in", [xn0, bd_re, bd_im, cd_re, cd_im, ab_re, ab_im])
    d_row, bglu_row = row(comm.vector("s5_d")), row(comm.vector("s5_b_glu"))
    u = _mm(xn0, w_in, "nn", BF16, "s5_in_u", b_cols=(0, w), b_slots=True)
    z0 = _mm(xn0, w_in, "nn", BF16, "s5_in_z", b_cols=(w, w), b_slots=True)
    e_re, e_im = _s5_scan_fwd(u, bd_re, bd_im, cd_re, cd_im, ab_re, ab_im, zero_seg, zero_seg, d_row, False, "s5_scan_ends")
    i_re, i_im = _s5_seg_fix(e_re, e_im, ab_re, ab_im, seg_len, False, "s5_seg_fix")
    y_ssm, yg, h_re, h_im, _, _ = _s5_scan_fwd(u, bd_re, bd_im, cd_re, cd_im, ab_re, ab_im, i_re, i_im, d_row, True, "s5_scan")
    w_glu, w_out = comm.weight("s5_w_glu", yg), comm.weight("s5_w_out", yg)
    gp = _mm(yg, w_glu, "nn", BF16, "s5_glu")
    y3 = _s5_gate(y_ssm, gp, bglu_row, z0, "s5_gate")
    w_kvt, fw_in = comm.weight("kv_w", y3), comm.weight("fox_w_in", y3)
    w_ft = jnp.pad(w_kvt[2 * fw:], ((0, LANES - nh), (0, 0)))
    o0 = _mm(y3, w_out, "nn", F32, "s5_out")

    h1, hn_kv, xn1 = _resid_norm2(x, o0, g_post0 + comm.late_token, g_kv, g_pre1, "resid_norms")
    kv = _mm(hn_kv, w_kvt, "nt", BF16, "kv_proj", b_rows=2 * fw)
    f_logit = _mm(hn_kv, w_ft, "nt", F32, "f_proj")
    bf_row = _lane_pad(row(kv_b_f))
    cum2 = _cum_fwd(f_logit, bf_row, "cum_fwd")
    cum2_t = cum2[:, :nh].T.reshape(nh, 1, s)
    q2 = _mm(xn1, fw_in, "nn", BF16, "fox_q", scale=HEAD_DIM ** -0.5 * LOG2E, b_cols=(0, fw), b_slots=True)
    z1 = _mm(xn1, fw_in, "nn", BF16, "fox_z", b_cols=(fw, fw), b_slots=True)
    o, oz, lse2_t = _fox_fwd(q2, kv, cum2_t, z1, "fox_fwd")
    fw_out = comm.weight("fox_w_out", oz)
    o1 = _mm(oz, fw_out, "nn", F32, "fox_out")
    dh2, do1, sq, dg_post1 = _post_norm_loss(o1, g_post1, h1, target, "norm_post1_loss")
    loss = 0.5 * jnp.sum(sq) / d

    d_fw_out = _mm(oz, do1, "tn", BF16, "fox_out_dw")
    d_oz = _mm(do1, fw_out, "nt", BF16, "fox_out_dx")
    do, dqz = _gate_bwd(d_oz, o, z1, "fox_gate_bwd")
    dk, dv, dqz, dcq, dck = _fox_bwd(q2, kv, do, o, lse2_t, cum2, dqz, "fox_bwd")
    d_fw_in = _mm(xn1, dqz, "tn", BF16, "fox_in_dw", col_slots=True)
    dxn1 = _mm(dqz, fw_in, "nt", BF16, "fox_in_dx", b_slots=True)
    dcq_sl = _lane_pad(dcq.reshape(nh, s).T)
    dck_sl = _lane_pad(dck.reshape(nh, s).T)
    df, db_f = _cum_bwd(dcq_sl, dck_sl, f_logit, bf_row, "cum_bwd")
    dkv = _concat_cast(dk, dv, "fox_dkv")
    d_w_kvmt = _mm(dkv, hn_kv, "tn", BF16, "kv_dw")
    d_w_ft = _mm(df, hn_kv, "tn", BF16, "f_dw")
    dhn_f = _mm(df, w_ft, "nn", F32, "f_dx")
    dhn_kv = _mm(dkv, w_kvt, "nn", BF16, "kv_dx", add=dhn_f, b_rows=2 * fw)
    d_w_kvt = jnp.concatenate([d_w_kvmt, d_w_ft[:nh]], axis=0)
    tok = comm.send_grads(dict(fox_w_out=d_fw_out, fox_w_in=d_fw_in, kv_w=d_w_kvt), "exchange_fox")
    dh1, do0, dg_pre1, dg_kv, dg_post0 = _norm_bwd2(dh2, h1, dxn1, dhn_kv, g_pre1, g_kv, o0, g_post0 + tok[0, 0],
                                                      "resid_norms_bwd")

    d_w_out = _mm(y3, do0, "tn", BF16, "s5_out_dw")
    dy3 = _mm(do0, w_out, "nt", BF16, "s5_out_dx")
    duz, dgp, dyg_direct, db_glu = _s5_gate_bwd(dy3, y_ssm, gp, bglu_row, z0, "s5_gate_bwd")
    d_w_glu = _mm(yg, dgp, "tn", BF16, "s5_glu_dw")
    gelu_bwd = lambda dyg, y: jax.vjp(jax.nn.gelu, y)[1](dyg)[0]
    dy_ssm = _mm(dgp, w_glu, "nt", F32, "s5_glu_dx", add=dyg_direct, epilogue=(gelu_bwd, y_ssm))
    d_row = d_row + comm.send_grads(dict(s5_w_out=d_w_out, s5_w_glu=d_w_glu), "exchange_s5")[0, 0]
    ab_imn = -ab_im
    ge_re, ge_im = _s5_scan_bwd(dy_ssm, u, h_re, h_im, bd_re, bd_im, cd_re, cd_im, ab_re, ab_imn, zero_seg, zero_seg,
                                d_row, False, "s5_adj_ends")
    gi_re, gi_im = _s5_seg_fix(ge_re, ge_im, ab_re, ab_imn, seg_len, True, "s5_adj_fix")
    duz, dbd_re, dbd_im, dcd_re, dcd_im, dab_re, dab_im, dd = _s5_scan_bwd(
        dy_ssm, u, h_re, h_im, bd_re, bd_im, cd_re, cd_im, ab_re, ab_imn, gi_re, gi_im, d_row, True, "s5_adj", duz=duz)
    d_w_in = _mm(xn0, duz, "tn", BF16, "s5_in_dw", col_slots=True)
    tok = comm.send_grads(dict(s5_w_in=d_w_in), "exchange_s5_in")
    dxn0 = _mm(duz, w_in, "nt", BF16, "s5_in_dx", after=tok, b_slots=True)
    grad_x, dg_pre0 = _norm_bwd1(dh1, x, dxn0, g_pre0, "norm_pre0_bwd")

    dbb_re = _block_diag_in_extract(dbd_re, p, S5_GROUP).reshape(g * p, S5_GROUP)
    dbb_im = _block_diag_in_extract(dbd_im, p, S5_GROUP).reshape(g * p, S5_GROUP)
    dcr_col, dci_col, db_re, db_im = _s5_bbar_bwd(cr_col, ci_col, b_re2, b_im2, dbb_re, dbb_im)
    da_re, da_im, dldt = _s5_disc_bwd(a_re, a_im, ldt, dab_re.reshape(g, p), dab_im.reshape(g, p),
                                      dcr_col.reshape(g, p), dci_col.reshape(g, p))
    dc_re = _block_diag_out_extract(dcd_re, S5_GROUP, p)
    dc_im = -_block_diag_out_extract(dcd_im, S5_GROUP, p)

    small = dict(
        norm_pre=jnp.concatenate([dg_pre0, dg_pre1], axis=0), norm_post=jnp.concatenate([dg_post0, dg_post1], axis=0),
        s5_a_re=da_re, s5_a_im=da_im, s5_log_dt=dldt.reshape(g), s5_b_re=db_re.reshape(g, p, S5_GROUP),
        s5_b_im=db_im.reshape(g, p, S5_GROUP), s5_c_re=dc_re, s5_c_im=dc_im, s5_d=dd.reshape(-1),
        s5_b_glu=db_glu.reshape(-1), kv_norm=dg_kv.reshape(-1), kv_b_f=db_f[0, :nh])
    return loss, grad_x, small


_BIG = ("s5_w_in", "s5_w_glu", "s5_w_out", "kv_w", "fox_w_in", "fox_w_out")
_COL_SHARDED = ("s5_w_in", "fox_w_in")
_SMALL = ("norm_pre", "norm_post", "s5_a_re", "s5_a_im", "s5_log_dt", "s5_b_re", "s5_b_im", "s5_c_re", "s5_c_im",
          "s5_d", "s5_b_glu", "kv_norm", "kv_b_f")
_SMALL_SHARDED = ("s5_d", "s5_b_glu")
_PACK_QUANTUM = SUBLANES * LANES
_WEIGHTS = ('norm_pre', 'norm_post', 's5_w_in', 's5_a_re', 's5_a_im', 's5_log_dt', 's5_b_re', 's5_b_im', 's5_c_re', 's5_c_im',
            's5_d', 's5_w_glu', 's5_b_glu', 's5_w_out', 'kv_norm', 'kv_w', 'kv_b_f', 'fox_w_in', 'fox_w_out')


def _full_from_slots(name, slots):
    n, r, c = slots.shape
    if name in _COL_SHARDED:
        return slots.transpose(1, 0, 2).reshape(r, n * c)
    return slots.reshape(n * r, c)


def _slots_from_full(name, full):
    if name in _COL_SHARDED:
        r, nc = full.shape
        return full.reshape(r, N_DEV, nc // N_DEV).transpose(1, 0, 2)
    nr, c = full.shape
    return full.reshape(N_DEV, nr // N_DEV, c)


def _groups_last(shape):
    return len(shape) >= 3 and shape[-1] < LANES and shape[-3] % LANES == 0


def _pack(vals):
    parts = []
    for v in vals:
        flat = jnp.moveaxis(v, -3, -1).reshape(-1) if _groups_last(v.shape) else v.reshape(-1)
        parts.append(jnp.pad(flat, (0, (-flat.shape[0]) % _PACK_QUANTUM)))
    total = sum(p.shape[0] for p in parts)
    parts.append(jnp.zeros(((-total) % (N_DEV * _PACK_QUANTUM),), F32))
    return jnp.concatenate(parts).reshape(-1, LANES)


def _unpack(packed, shapes):
    flat = packed.reshape(-1)
    out, off = [], 0
    for sh in shapes:
        n = math.prod(sh)
        piece = flat[off:off + n]
        if _groups_last(sh):
            piece = jnp.moveaxis(piece.reshape(sh[:-3] + sh[-2:] + sh[-3:-2]), -1, -3)
        out.append(piece.reshape(sh))
        off += n + (-n) % _PACK_QUANTUM
    return out


class _Comm:
    _GROUPS = (("s5_w_in",) + _SMALL_SHARDED, ("s5_w_glu", "s5_w_out"), ("kv_w", "fox_w_in"), ("fox_w_out",))
    _SLOT_FORM = ("s5_w_in", "fox_w_in")

    def __init__(self, shards, vectors, early=()):
        self._shards = {**shards, **vectors}
        self._full, self._gathers = {}, {}
        self._early = list(early)
        self.token = jnp.zeros((), F32)
        for group in self._GROUPS[:-1]:
            self.token = self.token + self._start(group, ())[0, 0]
        self.late_token = None
        self._sent = []

    def _start(self, group, after):
        state, tok = _exchange_start([self._shards[n] for n in group], False, "gather_start_" + group[0], after,
                                     peers=_CHIP_PEERS)
        self._gathers[group] = state
        return tok

    def vector(self, name):
        return self._full[name]

    def weight(self, name, after):
        if name not in self._full:
            group = next(g for g in self._GROUPS if name in g)
            if group == self._GROUPS[0]:
                after = (list(after) if isinstance(after, (list, tuple)) else [after]) + self._early
            slots = _exchange_wait(self._gathers.pop(group), after, "gather_wait_" + group[0])
            slots = _forward_to_sibling(slots, "gather_forward_" + group[0])
            for n, sl in zip(group, slots):
                if n in _SMALL_SHARDED:
                    self._full[n] = sl.reshape(-1)
                else:
                    self._full[n] = sl if n in self._SLOT_FORM else _full_from_slots(n, sl)
            if group == self._GROUPS[-2]:
                self.late_token = self._start(self._GROUPS[-1], [slots[0]])[0, 0]
        return self._full[name]

    def send_grads(self, grads, name):
        names = list(grads)
        slots = [grads[n] if grads[n].ndim == 3 else _slots_from_full(n, grads[n]).astype(BF16) for n in names]
        state, tok = _exchange_start(slots, True, name + "_start")
        self._sent.append((names, state, name + "_wait"))
        return tok

    def received_grads(self, group, after):
        names, state, name = self._sent[group]
        return list(zip(names, _exchange_wait(state, after, name)))


def kernel(x, norm_pre, norm_post, s5_w_in, s5_a_re, s5_a_im, s5_log_dt, s5_b_re, s5_b_im, s5_c_re, s5_c_im, s5_d, s5_w_glu, s5_b_glu, s5_w_out, kv_norm, kv_w, kv_b_f, fox_w_in, fox_w_out, loss_target, m_norm_pre, m_norm_post, m_s5_w_in, m_s5_a_re, m_s5_a_im, m_s5_log_dt, m_s5_b_re, m_s5_b_im, m_s5_c_re, m_s5_c_im, m_s5_d, m_s5_w_glu, m_s5_b_glu, m_s5_w_out, m_kv_norm, m_kv_w, m_kv_b_f, m_fox_w_in, m_fox_w_out, v_norm_pre, v_norm_post, v_s5_w_in, v_s5_a_re, v_s5_a_im, v_s5_log_dt, v_s5_b_re, v_s5_b_im, v_s5_c_re, v_s5_c_im, v_s5_d, v_s5_w_glu, v_s5_b_glu, v_s5_w_out, v_kv_norm, v_kv_w, v_kv_b_f, v_fox_w_in, v_fox_w_out):
    env = dict(locals())
    wts = {n: env[n] for n in _WEIGHTS}
    mom = {n: env["m_" + n] for n in _WEIGHTS}
    var = {n: env["v_" + n] for n in _WEIGHTS}
    me = 4 * lax.axis_index("x") + 2 * lax.axis_index("y") + lax.axis_index("c")
    shard2d = {n: (wts[n].T if n == "kv_w" else wts[n].reshape(wts[n].shape[-2:])) for n in _BIG}
    full_shape = {n: ((wts[n].size * N_DEV,) if n in _SMALL_SHARDED else wts[n].shape) for n in _SMALL}

    def spread(n, v):
        if n not in _SMALL_SHARDED:
            return v
        flat = v.reshape(-1)
        return lax.dynamic_update_slice(jnp.zeros(full_shape[n], F32), flat, (me * flat.shape[0],))

    packed = [_pack([spread(n, src[n]) for n in _SMALL] + [jnp.zeros((1,), F32)]) for src in (wts, mom, var)]
    comm = _Comm({n: _cast_bf16(shard2d[n], "cast_" + n) for n in _BIG}, {n: wts[n].reshape(1, -1) for n in _SMALL_SHARDED}, packed)

    loss_local, grad_x, small = _local_step(
        x[0], loss_target[0], norm_pre, norm_post, kv_norm, kv_b_f, s5_a_re[0], s5_a_im[0], s5_log_dt[0],
        s5_b_re[0], s5_b_im[0], s5_c_re[0], s5_c_im[0], comm)

    small_pack = _pack([small[n] for n in _SMALL] + [loss_local.reshape(1)])
    slice_rows = small_pack.shape[0] // N_DEV
    small_state, small_tok = _exchange_start([small_pack.reshape(N_DEV, slice_rows, LANES)], True, "reduce_small_start")

    res = {}

    def finish(group, after):
        for n, recv in comm.received_grads(group, after):
            if n == "kv_w":
                res[n] = [o.T for o in _adamw(recv, wts[n].T, mom[n].T, var[n].T, "adamw_" + n)]
            else:
                res[n] = _adamw(recv, wts[n], mom[n], var[n], "adamw_" + n)

    finish(0, [small_tok, grad_x])
    my_sum = _sum_parts(_exchange_wait(small_state, res["kv_w"][0], "reduce_small_wait")[0], "sum_small")
    gather_state, gather_tok = _exchange_start([my_sum], False, "gather_small_start")
    finish(1, gather_tok)
    finish(2, gather_tok)
    g_all = _exchange_wait(gather_state, res["s5_w_in"][0], "gather_small_wait")[0].reshape(1, small_pack.shape[0], LANES)
    outs = _adamw(g_all, *packed, "adamw_small")
    unpacked = [_unpack(o, [full_shape[n] for n in _SMALL] + [(1,)]) for o in outs]
    loss = unpacked[0][-1][0]
    for i, n in enumerate(_SMALL):
        vals = [u[i] for u in unpacked]
        if n in _SMALL_SHARDED:
            k = wts[n].size
            vals = [lax.dynamic_slice(v, (me * k,), (k,)) for v in vals]
        res[n] = [v.reshape(wts[n].shape) for v in vals]

    return (loss, grad_x[None], *[res[n][0] for n in _WEIGHTS], *[res[n][1] for n in _WEIGHTS],
            *[res[n][2] for n in _WEIGHTS], *[res[n][3] for n in _WEIGHTS])
```

```python
import math

import jax
import jax.numpy as jnp
from jax import lax
from jax.experimental import pallas as pl
from jax.experimental.pallas import tpu as pltpu

F32 = jnp.float32
BF16 = jnp.bfloat16

N_DEV = 8
MESH_AXES = ("x", "y", "c")
S5_GROUP = 16
S5_STATE = 64
LANES = 128
SUBLANES = 8
GROUPS_PER_BLOCK = LANES // S5_GROUP
BLOCK_STATE = GROUPS_PER_BLOCK * S5_STATE
N_SEG = SUBLANES
HEAD_DIM = 128
RMS_EPS = 1e-6
NEG_INF = -1e30
LOG2E = math.log2(math.e)
ADAM_LR = 0.001
ADAM_B1 = 0.9
ADAM_B2 = 0.999
ADAM_EPS = 1e-08
ADAM_WD = 0.01
ADAM_STEP = 10
VMEM_LIMIT = 56 * 1024 * 1024


def _tile(n, pref, quantum=LANES):
    if n <= pref:
        return n
    t = (pref // quantum) * quantum
    while t >= quantum:
        if n % t == 0:
            return t
        t -= quantum
    return n


def _cparams(*sem):
    return pltpu.CompilerParams(dimension_semantics=sem if sem else None, vmem_limit_bytes=VMEM_LIMIT)


_DOT_DIMS = {"nn": ((1,), (0,)), "nt": ((1,), (1,)), "tn": ((0,), (0,))}


def _mm(a, b, mode, out_dtype, name, add=None, scale=None, b_cols=None, after=None, col_slots=False, b_slots=False,
        b_rows=None, epilogue=None):
    slot_w = b.shape[2] if b_slots else None
    b2d = (b.shape[1], b.shape[0] * b.shape[2]) if b_slots else b.shape
    b_shape = b2d if b_cols is None else (b2d[0], b_cols[1])
    if b_rows is not None:
        b_shape = (b_rows, b_shape[1])
    if mode == "nn":
        (M, K), (K2, N) = a.shape, b_shape
    elif mode == "nt":
        (M, K), (N, K2) = a.shape, b_shape
    else:
        (K, M), (K2, N) = a.shape, b_shape
    assert K == K2, (name, a.shape, b_shape)
    tm, tn, tk = _tile(M, 1024 if K <= 2048 else 512), (N // N_DEV if col_slots else _tile(N, 1024)), _tile(K, 4096)
    if b_slots and mode == "nn":
        tn = slot_w
    nk = K // tk
    dims = (_DOT_DIMS[mode], ((), ()))
    col0 = 0
    if b_cols is not None:
        assert mode != "tn" and b_cols[0] % (tn if mode == "nn" else tk) == 0
        col0 = b_cols[0] // (tn if mode == "nn" else tk)
    assert not b_slots or (mode == "nn" or (mode == "nt" and nk == 1 and b_cols is None))

    def body(*refs):
        a_ref, b_ref = refs[:2]
        c_ref = refs[2] if add is not None else None
        e_ref = refs[2 + (add is not None)] if epilogue is not None else None
        o_ref = refs[2 + (add is not None) + (epilogue is not None) + (after is not None)]
        if b_slots and mode == "nt":
            part = lax.dot_general(a_ref[:, :slot_w], b_ref[0], dims, preferred_element_type=F32)
            for sl in range(1, b_ref.shape[0]):
                part += lax.dot_general(a_ref[:, sl * slot_w:(sl + 1) * slot_w], b_ref[sl], dims, preferred_element_type=F32)
        else:
            part = lax.dot_general(a_ref[...], b_ref[...], dims, preferred_element_type=F32)

        def finish(r):
            if scale is not None:
                r = r * scale
            if add is not None:
                r = r + c_ref[...]
            if epilogue is not None:
                r = epilogue[0](r, e_ref[...])
            o_ref[...] = r.astype(out_dtype)

        if nk == 1:
            finish(part)
            return
        acc = refs[-1]
        k = pl.program_id(2)

        @pl.when(k == 0)
        def _():
            acc[...] = part

        @pl.when(jnp.logical_and(k > 0, k < nk - 1))
        def _():
            acc[...] += part

        @pl.when(k == nk - 1)
        def _():
            finish(acc[...] + part)

    if mode == "tn":
        a_spec = pl.BlockSpec((tk, tm), lambda i, j, k: (k, i))
    else:
        a_spec = pl.BlockSpec((tm, tk), lambda i, j, k: (i, k))
    if b_slots and mode == "nn":
        b_spec = pl.BlockSpec((None, tk, tn), lambda i, j, k: (j + col0, k, 0))
    elif b_slots:
        b_spec = pl.BlockSpec((b.shape[0], tn, slot_w), lambda i, j, k: (0, j, 0))
    elif mode == "nt":
        b_spec = pl.BlockSpec((tn, tk), lambda i, j, k: (j, k + col0))
    else:
        b_spec = pl.BlockSpec((tk, tn), lambda i, j, k: (k, j + col0))
    o_spec = pl.BlockSpec((tm, tn), lambda i, j, k: (i, j))
    in_specs = [a_spec, b_spec] + ([o_spec] if add is not None else [])
    args = (a, b) + ((add,) if add is not None else ())
    if epilogue is not None:
        in_specs.append(o_spec)
        args += (epilogue[1],)
    if after is not None:
        in_specs.append(pl.BlockSpec(after.shape, lambda i, j, k: (0, 0)))
        args += (after,)
    out_shape = jax.ShapeDtypeStruct((M, N), out_dtype)
    if col_slots:
        assert add is None
        o_spec = pl.BlockSpec((None, tm, tn), lambda i, j, k: (j, i, 0))
        out_shape = jax.ShapeDtypeStruct((N_DEV, M, tn), out_dtype)
    return pl.pallas_call(
        body, name=name, grid=(M // tm, N // tn, nk),
        in_specs=in_specs, out_specs=o_spec,
        out_shape=out_shape,
        scratch_shapes=[pltpu.VMEM((tm, tn), F32)] if nk > 1 else [],
        compiler_params=_cparams("parallel", "parallel", "arbitrary"),
    )(*args)


class _NatIn:
    def __init__(self, ref):
        self.ref = ref

    def __getitem__(self, idx):
        v = jnp.swapaxes(self.ref[...], 0, 1)
        return v.reshape(v.shape[0] * N_SEG, v.shape[2])


class _NatOut:
    def __init__(self, ref):
        self.ref = ref

    def __setitem__(self, idx, val):
        self.ref[...] = jnp.swapaxes(val.reshape(val.shape[0] // N_SEG, N_SEG, val.shape[1]), 0, 1)


def _rowcall(body, name, n_rows, ins, outs, tile_rows=256):
    tr = _tile(n_rows, tile_rows, SUBLANES * 2)
    n_in = len(ins)
    in_kinds = [k for _, k in ins]
    kinds = [k for _, _, k in outs]

    def kern(*refs):
        @pl.when(pl.program_id(0) == 0)
        def _():
            for r, kind in zip(refs[n_in:], kinds):
                if kind == "acc":
                    r[...] = jnp.zeros_like(r)

        wrapped = [_NatIn(r) if k == "nat" else r for r, k in zip(refs[:n_in], in_kinds)]
        wrapped += [_NatOut(r) if k == "nat" else r for r, k in zip(refs[n_in:], kinds)]
        body(*wrapped)

    in_specs, args = [], []
    for arr, kind in ins:
        if kind == "row":
            in_specs.append(pl.BlockSpec((tr, arr.shape[1]), lambda i: (i, 0)))
        elif kind == "nat":
            in_specs.append(pl.BlockSpec((N_SEG, tr // N_SEG, arr.shape[1]), lambda i: (0, i, 0)))
            arr = arr.reshape(N_SEG, n_rows // N_SEG, arr.shape[1])
        else:
            in_specs.append(pl.BlockSpec(arr.shape, lambda i, nd=arr.ndim: (0,) * nd))
        args.append(arr)
    out_specs, out_shape = [], []
    for width, dtype, kind in outs:
        if kind == "row":
            out_specs.append(pl.BlockSpec((tr, width), lambda i: (i, 0)))
            out_shape.append(jax.ShapeDtypeStruct((n_rows, width), dtype))
        elif kind == "right":
            out_specs.append(pl.BlockSpec((tr, width), lambda i: (i, 1)))
            out_shape.append(jax.ShapeDtypeStruct((n_rows, 2 * width), dtype))
        elif kind == "nat":
            out_specs.append(pl.BlockSpec((N_SEG, tr // N_SEG, width), lambda i: (0, i, 0)))
            out_shape.append(jax.ShapeDtypeStruct((N_SEG, n_rows // N_SEG, width), dtype))
        else:
            out_specs.append(pl.BlockSpec((1, width), lambda i: (0, 0)))
            out_shape.append(jax.ShapeDtypeStruct((1, width), F32))
    res = pl.pallas_call(
        kern, name=name, grid=(n_rows // tr,), in_specs=in_specs, out_specs=out_specs, out_shape=out_shape,
        compiler_params=_cparams("arbitrary"),
    )(*args)
    return [r.reshape(n_rows, r.shape[2]) if k == "nat" else r for r, k in zip(res, kinds)]


def _rstd(x):
    return lax.rsqrt(jnp.mean(x * x, axis=-1, keepdims=True) + RMS_EPS)


def _rms_bwd(x, g, dy):
    xh = x * _rstd(x)
    dxh = dy * g
    dx = _rstd(x) * (dxh - xh * jnp.mean(dxh * xh, axis=-1, keepdims=True))
    return dx, jnp.sum(dy * xh, axis=0, keepdims=True)


def _silu(z):
    return z * jax.nn.sigmoid(z)


def _norm_cast(x, g, name, x_kind="row"):
    def body(x_ref, g_ref, o_ref):
        x = x_ref[...]
        o_ref[...] = (x * _rstd(x) * g_ref[...]).astype(BF16)

    return _rowcall(body, name, x.shape[0], [(x, x_kind), (g, "full")], [(x.shape[1], BF16, "row")])[0]


def _resid_norm2(x, o, g_post, g_kv, g_pre, name):
    def body(x_ref, o_ref, go_ref, gk_ref, gp_ref, h_ref, nk_ref, np_ref):
        o = o_ref[...]
        h = x_ref[...] + o * _rstd(o) * go_ref[...]
        h_ref[...] = h
        hn = h * _rstd(h)
        nk_ref[...] = (hn * gk_ref[...]).astype(BF16)
        np_ref[...] = (hn * gp_ref[...]).astype(BF16)

    d = x.shape[1]
    return _rowcall(body, name, x.shape[0], [(x, "nat"), (o, "row"), (g_post, "full"), (g_kv, "full"), (g_pre, "full")],
                    [(d, F32, "nat"), (d, BF16, "nat"), (d, BF16, "nat")])


def _post_norm_loss(o, g, h1, target, name):
    d = o.shape[1]

    def body(o_ref, g_ref, h_ref, t_ref, dh_ref, do_ref, acc_ref, dg_ref):
        o = o_ref[...]
        e = h_ref[...] + o * _rstd(o) * g_ref[...] - t_ref[...]
        dh = e * (1.0 / d)
        dh_ref[...] = dh
        acc_ref[...] += jnp.sum(e * e, axis=0, keepdims=True)
        dx, dg = _rms_bwd(o, g_ref[...], dh)
        do_ref[...] = dx.astype(BF16)
        dg_ref[...] += dg

    return _rowcall(body, name, o.shape[0], [(o, "row"), (g, "full"), (h1, "row"), (target, "row")],
                    [(d, F32, "row"), (d, BF16, "row"), (d, F32, "acc"), (d, F32, "acc")])


def _gate_bwd(d_oz, o, z, name):
    def body(d_ref, o_ref, z_ref, do_ref, dz_ref):
        _, vjp = jax.vjp(lambda o, z: o * _silu(z), o_ref[...], z_ref[...].astype(F32))
        do, dz = vjp(d_ref[...].astype(F32))
        do_ref[...] = do.astype(BF16)
        dz_ref[...] = dz.astype(BF16)

    w = o.shape[1]
    return _rowcall(body, name, o.shape[0], [(d_oz, "row"), (o, "row"), (z, "row")], [(w, BF16, "row"), (w, BF16, "right")])


def _norm_bwd2(dh2, h1, dxn1, dhn_kv, g_pre, g_kv, o0, g_post0, name):
    def body(dh2_ref, h_ref, d1_ref, dk_ref, gp_ref, gk_ref, o_ref, go_ref, dh1_ref, do_ref, dgp_ref, dgk_ref, dgo_ref):
        h = h_ref[...]
        dx1, dg1 = _rms_bwd(h, gp_ref[...], d1_ref[...].astype(F32))
        dxk, dgk = _rms_bwd(h, gk_ref[...], dk_ref[...].astype(F32))
        dh1 = dh2_ref[...] + dx1 + dxk
        dh1_ref[...] = dh1
        dgp_ref[...] += dg1
        dgk_ref[...] += dgk
        dxo, dgo = _rms_bwd(o_ref[...], go_ref[...], dh1)
        do_ref[...] = dxo.astype(BF16)
        dgo_ref[...] += dgo

    d = h1.shape[1]
    return _rowcall(body, name, h1.shape[0],
                    [(dh2, "nat"), (h1, "nat"), (dxn1, "nat"), (dhn_kv, "nat"), (g_pre, "full"), (g_kv, "full"),
                     (o0, "row"), (g_post0, "full")],
                    [(d, F32, "nat"), (d, BF16, "row"), (d, F32, "acc"), (d, F32, "acc"), (d, F32, "acc")])


def _norm_bwd1(dres, x, dxn, g, name):
    def body(dr_ref, x_ref, dn_ref, g_ref, dx_ref, dg_ref):
        dx, dg = _rms_bwd(x_ref[...], g_ref[...], dn_ref[...].astype(F32))
        dx_ref[...] = dr_ref[...] + dx
        dg_ref[...] += dg

    d = x.shape[1]
    return _rowcall(body, name, x.shape[0], [(dres, "nat"), (x, "nat"), (dxn, "row"), (g, "full")],
                    [(d, F32, "nat"), (d, F32, "acc")])


def _s5_gate(y_ssm, gp, b_glu, z, name):
    def body(y_ref, gp_ref, b_ref, z_ref, o_ref):
        yg = jax.nn.gelu(y_ref[...])
        o_ref[...] = (yg * jax.nn.sigmoid(gp_ref[...] + b_ref[...]) * _silu(z_ref[...].astype(F32))).astype(BF16)

    return _rowcall(body, name, y_ssm.shape[0], [(y_ssm, "row"), (gp, "row"), (b_glu, "full"), (z, "row")],
                    [(y_ssm.shape[1], BF16, "row")])[0]


def _s5_gate_bwd(dy3, y_ssm, gp, b_glu, z, name):
    def body(d_ref, y_ref, gp_ref, b_ref, z_ref, dz_ref, dgp_ref, dyg_ref, db_ref):
        yg = jax.nn.gelu(y_ref[...])
        _, vjp = jax.vjp(lambda yg, gp, z: yg * jax.nn.sigmoid(gp) * _silu(z), yg, gp_ref[...] + b_ref[...],
                         z_ref[...].astype(F32))
        dyg, dgp, dz = vjp(d_ref[...].astype(F32))
        dz_ref[...] = dz.astype(BF16)
        dgp_ref[...] = dgp.astype(BF16)
        dyg_ref[...] = dyg
        db_ref[...] += jnp.sum(dgp, axis=0, keepdims=True)

    w = y_ssm.shape[1]
    return _rowcall(body, name, y_ssm.shape[0],
                    [(dy3, "row"), (y_ssm, "row"), (gp, "row"), (b_glu, "full"), (z, "row")],
                    [(w, BF16, "right"), (w, BF16, "row"), (w, F32, "row"), (w, F32, "acc")])


def _cast_bf16(x, name):
    r, c = x.shape
    by_cols = r % (2 * SUBLANES) != 0
    tr, tc = (r, _tile(c, 256)) if by_cols else (_tile(r, 512, 2 * SUBLANES), c)
    pos = (lambda i: (0, i)) if by_cols else (lambda i: (i, 0))

    def body(x_ref, o_ref):
        o_ref[...] = x_ref[...].astype(BF16)

    return pl.pallas_call(
        body, name=name, grid=(c // tc if by_cols else r // tr,),
        in_specs=[pl.BlockSpec((tr, tc), pos)], out_specs=pl.BlockSpec((tr, tc), pos),
        out_shape=jax.ShapeDtypeStruct((r, c), BF16), compiler_params=_cparams("parallel"),
    )(x)


def _concat_cast(a, b, name):
    def body(a_ref, b_ref, o_ref):
        w = a_ref.shape[1]
        o_ref[:, :w] = a_ref[...].astype(BF16)
        o_ref[:, w:] = b_ref[...].astype(BF16)

    return _rowcall(body, name, a.shape[0], [(a, "row"), (b, "row")], [(a.shape[1] + b.shape[1], BF16, "row")])[0]


def _disc(ar, ai, ldt):
    dt = jnp.exp(ldt)
    mag = jnp.exp(ar * dt)
    abr = mag * jnp.cos(ai * dt)
    abi = mag * jnp.sin(ai * dt)
    den = ar * ar + ai * ai
    nr = abr - 1.0
    return abr, abi, (nr * ar + abi * ai) / den, (abi * ar - nr * ai) / den


def _s5_disc_fwd(a_re, a_im, ldt):
    def body(ar, ai, ld, o1, o2, o3, o4):
        o1[...], o2[...], o3[...], o4[...] = _disc(ar[...], ai[...], ld[...])

    sh = jax.ShapeDtypeStruct(a_re.shape, F32)
    return pl.pallas_call(body, name="s5_disc_fwd", out_shape=(sh, sh, sh, sh))(a_re, a_im, ldt)


def _s5_disc_bwd(a_re, a_im, ldt, d_abr, d_abi, d_cr, d_ci):
    def body(ar, ai, ld, g1, g2, g3, g4, o1, o2, o3):
        _, vjp = jax.vjp(_disc, ar[...], ai[...], ld[...])
        o1[...], o2[...], o3[...] = vjp((g1[...], g2[...], g3[...], g4[...]))

    sh = jax.ShapeDtypeStruct(a_re.shape, F32)
    return pl.pallas_call(body, name="s5_disc_bwd", out_shape=(sh, sh, jax.ShapeDtypeStruct(ldt.shape, F32)))(
        a_re, a_im, ldt, d_abr, d_abi, d_cr, d_ci)


def _bbar(cr, ci, br, bi):
    return cr * br - ci * bi, cr * bi + ci * br


def _s5_bbar_fwd(cr_col, ci_col, b_re, b_im):
    def body(cr, ci, br, bi, o1, o2):
        o1[...], o2[...] = _bbar(cr[...], ci[...], br[...], bi[...])

    w = b_re.shape[1]
    return _rowcall(body, "s5_bbar_fwd", b_re.shape[0], [(cr_col, "row"), (ci_col, "row"), (b_re, "row"), (b_im, "row")],
                    [(w, F32, "row"), (w, F32, "row")], tile_rows=1024)


def _s5_bbar_bwd(cr_col, ci_col, b_re, b_im, d_re, d_im):
    def body(cr, ci, br, bi, g1, g2, o1, o2, o3, o4):
        _, vjp = jax.vjp(_bbar, cr[...], ci[...], br[...], bi[...])
        o1[...], o2[...], o3[...], o4[...] = vjp((g1[...], g2[...]))

    w = b_re.shape[1]
    return _rowcall(body, "s5_bbar_bwd", b_re.shape[0],
                    [(cr_col, "row"), (ci_col, "row"), (b_re, "row"), (b_im, "row"), (d_re, "row"), (d_im, "row")],
                    [(1, F32, "row"), (1, F32, "row"), (w, F32, "row"), (w, F32, "row")], tile_rows=1024)


def _block_diag(t):
    l, g, a, b = t.shape
    nb = g // GROUPS_PER_BLOCK
    t5 = t.reshape(l, nb, GROUPS_PER_BLOCK, a, b).transpose(0, 1, 2, 4, 3)
    eye = jnp.eye(GROUPS_PER_BLOCK, dtype=t.dtype)
    return (t5[:, :, :, :, None, :] * eye[None, None, :, None, :, None]).reshape(
        l, nb, GROUPS_PER_BLOCK * b, GROUPS_PER_BLOCK * a)


def _block_diag_extract(d, a, b):
    nb = d.shape[0]
    d5 = d.reshape(nb, GROUPS_PER_BLOCK, b, GROUPS_PER_BLOCK, a)
    eye = jnp.eye(GROUPS_PER_BLOCK, dtype=d.dtype)
    diag = jnp.sum(d5 * eye[None, :, None, :, None], axis=3)
    return diag.transpose(0, 1, 3, 2).reshape(nb * GROUPS_PER_BLOCK, a, b)


def _scan_step(ar, ai, hr, hi, xr, xi):
    return ar * hr - ai * hi + xr, ar * hi + ai * hr + xi


def _s5_blocks_per_step(nb, full):
    want = 2 if full else 4
    while nb % want:
        want //= 2
    return want


def _s5_scan_fwd(u, bd_re, bd_im, cd_re, cd_im, ab_re, ab_im, init_re, init_im, d_row, full, name):
    s, w = u.shape
    nb = w // LANES
    rows = _tile(s, 512, SUBLANES)
    nc = s // rows
    steps = rows // N_SEG
    ns = nb * BLOCK_STATE

    nblk = _s5_blocks_per_step(nb, full)

    def body(u_ref, bdr, bdi, cdr, cdi, ar_ref, ai_ref, ir_ref, ii_ref, d_ref, *outs):
        if full:
            y_ref, yg_ref, hr_ref, hi_ref, er_ref, ei_ref, cr, ci = outs
        else:
            er_ref, ei_ref, hr_ref, hi_ref, cr, ci = outs
        c = pl.program_id(1)
        cols = lambda b, width: slice(b * width, (b + 1) * width)

        @pl.when(c == 0)
        def _():
            cr[...] = ir_ref[...]
            ci[...] = ii_ref[...]

        for b in range(nblk):
            ub = u_ref[:, cols(b, LANES)].astype(BF16)
            hr_ref[:, cols(b, BLOCK_STATE)] = jnp.dot(ub, bdr[b], preferred_element_type=F32)
            hi_ref[:, cols(b, BLOCK_STATE)] = jnp.dot(ub, bdi[b], preferred_element_type=F32)
        ar, ai = ar_ref[...], ai_ref[...]

        hr, hi = cr[...], ci[...]
        for j in range(steps):
            rows_j = pl.ds(j * N_SEG, N_SEG)
            hr, hi = _scan_step(ar, ai, hr, hi, hr_ref[rows_j, :], hi_ref[rows_j, :])
            hr_ref[rows_j, :] = hr
            hi_ref[rows_j, :] = hi
        cr[...] = hr
        ci[...] = hi
        if full:
            for b in range(nblk):
                st_b, ln_b = cols(b, BLOCK_STATE), cols(b, LANES)
                y = (jnp.dot(hr_ref[:, st_b].astype(BF16), cdr[b], preferred_element_type=F32)
                     + jnp.dot(hi_ref[:, st_b].astype(BF16), cdi[b], preferred_element_type=F32)
                     + d_ref[:, ln_b] * u_ref[:, ln_b])
                y_ref[:, ln_b] = y
                yg_ref[:, ln_b] = jax.nn.gelu(y).astype(BF16)

        @pl.when(c == nc - 1)
        def _():
            er_ref[...] = hr
            ei_ref[...] = hi

    lanes, states = LANES * nblk, BLOCK_STATE * nblk
    blk3 = lambda a: pl.BlockSpec((nblk,) + a.shape[1:], lambda k, c: (k, 0, 0))
    seg = pl.BlockSpec((N_SEG, states), lambda k, c: (0, k))
    st = pl.BlockSpec((rows, states), lambda k, c: (c, k))
    in_specs = [pl.BlockSpec((rows, lanes), lambda k, c: (c, k)), blk3(bd_re), blk3(bd_im), blk3(cd_re), blk3(cd_im),
                seg, seg, seg, seg, pl.BlockSpec((1, lanes), lambda k, c: (0, k))]
    seg_shape = jax.ShapeDtypeStruct((N_SEG, ns), F32)
    st_shape = jax.ShapeDtypeStruct((s, ns), F32)
    carry = [pltpu.VMEM((N_SEG, states), F32)] * 2
    if full:
        ych = pl.BlockSpec((rows, lanes), lambda k, c: (c, k))
        out_specs = [ych, ych, st, st, seg, seg]
        out_shape = [jax.ShapeDtypeStruct((s, w), F32), jax.ShapeDtypeStruct((s, w), BF16), st_shape, st_shape, seg_shape, seg_shape]
        scratch = carry
    else:
        out_specs = [seg, seg]
        out_shape = [seg_shape, seg_shape]
        scratch = [pltpu.VMEM((rows, states), F32)] * 2 + carry
    return pl.pallas_call(
        body, name=name, grid=(nb // nblk, nc), in_specs=in_specs, out_specs=out_specs, out_shape=out_shape,
        scratch_shapes=scratch, compiler_params=_cparams("parallel", "arbitrary"),
    )(u, bd_re, bd_im, cd_re, cd_im, ab_re, ab_im, init_re, init_im, d_row)


def _s5_seg_fix(e_re, e_im, ab_re, ab_im, seg_len, reverse, name):
    assert seg_len & (seg_len - 1) == 0

    def body(er, ei, ar, ai, o_re, o_im):
        pr, pi = ar[0:1, :], ai[0:1, :]
        for _ in range(int(math.log2(seg_len))):
            pr, pi = pr * pr - pi * pi, 2.0 * pr * pi
        tr = jnp.zeros_like(pr)
        ti = jnp.zeros_like(pr)
        order = list(range(N_SEG - 1, -1, -1)) if reverse else list(range(N_SEG))
        for n, sgm in enumerate(order):
            o_re[sgm:sgm + 1, :] = tr
            o_im[sgm:sgm + 1, :] = ti
            if n < N_SEG - 1:
                tr, ti = _scan_step(pr, pi, tr, ti, er[sgm:sgm + 1, :], ei[sgm:sgm + 1, :])

    sh = jax.ShapeDtypeStruct(e_re.shape, F32)
    return pl.pallas_call(body, name=name, out_shape=(sh, sh))(e_re, e_im, ab_re, ab_im)


def _s5_scan_bwd(dy, u, h_re, h_im, bd_re, bd_im, cd_re, cd_im, ab_re, ab_imn, gin_re, gin_im, d_row, full, name, duz=None):
    s, w = u.shape
    nb = w // LANES
    rows = _tile(s, 512, SUBLANES)
    nc = s // rows
    steps = rows // N_SEG
    ns = nb * BLOCK_STATE

    nblk = _s5_blocks_per_step(nb, full)

    def body(dy_ref, u_ref, hr_ref, hi_ref, bdr, bdi, cdr, cdi, ar_ref, ai_ref, ir_ref, ii_ref, d_ref, *outs):
        if full:
            _, du_ref, dbr_ref, dbi_ref, dcr_ref, dci_ref, dar_ref, dai_ref, dd_ref, gr, gi, accr, acci = outs
        else:
            er_ref, ei_ref, gr, gi = outs
        c = pl.program_id(1)
        cols = lambda b, width: slice(b * width, (b + 1) * width)

        @pl.when(c == 0)
        def _():
            gr[pl.ds(rows, N_SEG), :] = ir_ref[...]
            gi[pl.ds(rows, N_SEG), :] = ii_ref[...]
            if full:
                for r in (dbr_ref, dbi_ref, dcr_ref, dci_ref, dd_ref, accr, acci):
                    r[...] = jnp.zeros_like(r)

        nt = (_DOT_DIMS["nt"], ((), ()))
        tn = (_DOT_DIMS["tn"], ((), ()))
        for b in range(nblk):
            dyb = dy_ref[:, cols(b, LANES)].astype(BF16)
            gr[pl.ds(0, rows), cols(b, BLOCK_STATE)] = lax.dot_general(dyb, cdr[b], nt, preferred_element_type=F32)
            gi[pl.ds(0, rows), cols(b, BLOCK_STATE)] = lax.dot_general(dyb, cdi[b], nt, preferred_element_type=F32)
        ar, ai = ar_ref[...], ai_ref[...]

        g0r, g0i = gr[pl.ds(rows, N_SEG), :], gi[pl.ds(rows, N_SEG), :]
        for j in range(steps - 1, -1, -1):
            rows_j = pl.ds(j * N_SEG, N_SEG)
            g0r, g0i = _scan_step(ar, ai, g0r, g0i, gr[rows_j, :], gi[rows_j, :])
            gr[rows_j, :] = g0r
            gi[rows_j, :] = g0i
        if full:
            for b in range(nblk):
                st_b, ln_b = cols(b, BLOCK_STATE), cols(b, LANES)
                hr, hi = hr_ref[:, st_b], hi_ref[:, st_b]
                gnr, gni = gr[pl.ds(N_SEG, rows), st_b], gi[pl.ds(N_SEG, rows), st_b]
                accr[:, st_b] += jnp.sum((gnr * hr + gni * hi).reshape(steps, N_SEG, BLOCK_STATE), axis=0)
                acci[:, st_b] += jnp.sum((gni * hr - gnr * hi).reshape(steps, N_SEG, BLOCK_STATE), axis=0)
                dyb = dy_ref[:, ln_b].astype(BF16)
                ub = u_ref[:, ln_b].astype(BF16)
                gbr, gbi = gr[pl.ds(0, rows), st_b].astype(BF16), gi[pl.ds(0, rows), st_b].astype(BF16)
                dcr_ref[b] += lax.dot_general(hr.astype(BF16), dyb, tn, preferred_element_type=F32)
                dci_ref[b] += lax.dot_general(hi.astype(BF16), dyb, tn, preferred_element_type=F32)
                dbr_ref[b] += lax.dot_general(ub, gbr, tn, preferred_element_type=F32)
                dbi_ref[b] += lax.dot_general(ub, gbi, tn, preferred_element_type=F32)
                du_ref[:, ln_b] = (lax.dot_general(gbr, bdr[b], nt, preferred_element_type=F32)
                                   + lax.dot_general(gbi, bdi[b], nt, preferred_element_type=F32)
                                   + d_ref[:, ln_b] * dy_ref[:, ln_b]).astype(BF16)
                dd_ref[:, ln_b] += jnp.sum(dy_ref[:, ln_b] * u_ref[:, ln_b], axis=0, keepdims=True)
        gr[pl.ds(rows, N_SEG), :] = g0r
        gi[pl.ds(rows, N_SEG), :] = g0i

        @pl.when(c == nc - 1)
        def _():
            if full:
                dar_ref[...] = jnp.sum(accr[...], axis=0, keepdims=True)
                dai_ref[...] = jnp.sum(acci[...], axis=0, keepdims=True)
            else:
                er_ref[...] = g0r
                ei_ref[...] = g0i

    lanes, states = LANES * nblk, BLOCK_STATE * nblk
    rev = lambda k, c: (nc - 1 - c, k)
    blk3 = lambda a: pl.BlockSpec((nblk,) + a.shape[1:], lambda k, c: (k, 0, 0))
    seg = pl.BlockSpec((N_SEG, states), lambda k, c: (0, k))
    st = pl.BlockSpec((rows, states), rev)
    ch = pl.BlockSpec((rows, lanes), rev)
    vec = pl.BlockSpec((1, lanes), lambda k, c: (0, k))
    if not full:
        st = pl.BlockSpec((rows, states), lambda k, c: (0, k))
    in_specs = [ch, ch if full else pl.BlockSpec((rows, lanes), lambda k, c: (0, k)), st, st,
                blk3(bd_re), blk3(bd_im), blk3(cd_re), blk3(cd_im), seg, seg, seg, seg, vec]
    args = [dy, u, h_re, h_im, bd_re, bd_im, cd_re, cd_im, ab_re, ab_imn, gin_re, gin_im, d_row]
    gbuf = [pltpu.VMEM((rows + N_SEG, states), F32)] * 2
    if full:
        row1 = pl.BlockSpec((1, states), lambda k, c: (0, k))
        out_specs = [ch, blk3(bd_re), blk3(bd_im), blk3(cd_re), blk3(cd_im), row1, row1, vec]
        out_shape = [jax.ShapeDtypeStruct(duz.shape, BF16),
                     jax.ShapeDtypeStruct(bd_re.shape, F32), jax.ShapeDtypeStruct(bd_im.shape, F32),
                     jax.ShapeDtypeStruct(cd_re.shape, F32), jax.ShapeDtypeStruct(cd_im.shape, F32),
                     jax.ShapeDtypeStruct((1, ns), F32), jax.ShapeDtypeStruct((1, ns), F32),
                     jax.ShapeDtypeStruct((1, w), F32)]
        scratch = gbuf + [pltpu.VMEM((N_SEG, states), F32)] * 2
        in_specs.append(pl.BlockSpec(memory_space=pl.ANY))
        args.append(duz)
        aliases = {len(args) - 1: 0}
    else:
        out_specs = [seg, seg]
        out_shape = [jax.ShapeDtypeStruct((N_SEG, ns), F32)] * 2
        scratch = gbuf
        aliases = {}
    return pl.pallas_call(
        body, name=name, grid=(nb // nblk, nc), in_specs=in_specs, out_specs=out_specs, out_shape=out_shape,
        input_output_aliases=aliases, scratch_shapes=scratch, compiler_params=_cparams("parallel", "arbitrary"),
    )(*args)


def _log_sigmoid(x):
    return jnp.minimum(x, 0.0) - jnp.log(1.0 + jnp.exp(-jnp.abs(x)))


def _tri(n, upper):
    r = lax.broadcasted_iota(jnp.int32, (n, n), 0)
    c = lax.broadcasted_iota(jnp.int32, (n, n), 1)
    return jnp.where((c >= r) if upper else (r >= c), 1.0, 0.0).astype(F32)


def _cum_fwd(f_logit, b_row, name):
    s, w = f_logit.shape
    t = _tile(s, 256, SUBLANES)

    def body(f_ref, b_ref, o_ref, carry):
        @pl.when(pl.program_id(0) == 0)
        def _():
            carry[...] = jnp.zeros_like(carry)

        lf = _log_sigmoid(f_ref[...] + b_ref[...])
        cum = jnp.dot(_tri(t, False), lf, precision=lax.Precision.HIGHEST, preferred_element_type=F32) + carry[...]
        o_ref[...] = cum * LOG2E
        carry[...] = cum[t - 1:t, :]

    return pl.pallas_call(
        body, name=name, grid=(s // t,),
        in_specs=[pl.BlockSpec((t, w), lambda i: (i, 0)), pl.BlockSpec((1, w), lambda i: (0, 0))],
        out_specs=pl.BlockSpec((t, w), lambda i: (i, 0)), out_shape=jax.ShapeDtypeStruct((s, w), F32),
        scratch_shapes=[pltpu.VMEM((1, w), F32)], compiler_params=_cparams("arbitrary"),
    )(f_logit, b_row)


def _cum_bwd(dcq, dck, f_logit, b_row, name):
    s, w = f_logit.shape
    t = _tile(s, 256, SUBLANES)
    nt = s // t

    def body(q_ref, k_ref, f_ref, b_ref, df_ref, db_ref, carry):
        @pl.when(pl.program_id(0) == 0)
        def _():
            carry[...] = jnp.zeros_like(carry)
            db_ref[...] = jnp.zeros_like(db_ref)

        dc = q_ref[...] - k_ref[...]
        rc = jnp.dot(_tri(t, True), dc, precision=lax.Precision.HIGHEST, preferred_element_type=F32) + carry[...]
        carry[...] = rc[0:1, :]
        df = rc * (1.0 - jax.nn.sigmoid(f_ref[...] + b_ref[...]))
        df_ref[...] = df.astype(BF16)
        db_ref[...] += jnp.sum(df, axis=0, keepdims=True)

    rev = pl.BlockSpec((t, w), lambda i: (nt - 1 - i, 0))
    one = pl.BlockSpec((1, w), lambda i: (0, 0))
    return pl.pallas_call(
        body, name=name, grid=(nt,), in_specs=[rev, rev, rev, one], out_specs=[rev, one],
        out_shape=[jax.ShapeDtypeStruct((s, w), BF16), jax.ShapeDtypeStruct((1, w), F32)],
        scratch_shapes=[pltpu.VMEM((1, w), F32)], compiler_params=_cparams("arbitrary"),
    )(dcq, dck, f_logit, b_row)


def _head_col(cum_tile, h):
    lane = lax.broadcasted_iota(jnp.int32, cum_tile.shape, 1)
    return jnp.sum(jnp.where(lane == h, cum_tile, 0.0), axis=1, keepdims=True)


def _attn_tiles(s):
    return _tile(s, 512, LANES)


def _exp2_rows(sc, sub):
    return jnp.concatenate([jnp.exp2(sc[:, b * LANES:(b + 1) * LANES] - sub) for b in range(sc.shape[1] // LANES)], axis=1)


def _row_of(rep):
    return jnp.transpose(rep)[0:1, :]


def _causal(sc, keys_on_rows):
    r = lax.broadcasted_iota(jnp.int32, sc.shape, 0)
    c = lax.broadcasted_iota(jnp.int32, sc.shape, 1)
    return jnp.where((r <= c) if keys_on_rows else (c <= r), sc, NEG_INF)


def _fox_fwd(q2, kv, cum2_t, z, name):
    s, w = q2.shape
    nh = w // HEAD_DIM
    tq = _attn_tiles(s)
    nq = s // tq
    nt = (_DOT_DIMS["nt"], ((), ()))

    def body(q_ref, k_ref, v_ref, ct_ref, z_ref, o_ref, oz_ref, lse_row_ref, m_s, acc_s, vaug, s_buf):
        i = pl.program_id(1)

        @pl.when(i == 0)
        def _():
            vaug[:, :HEAD_DIM] = v_ref[...]
            vaug[:, HEAD_DIM:] = jnp.ones((s, LANES), BF16)

        qb = q_ref[...]
        m_s[...] = jnp.full_like(m_s, NEG_INF)
        acc_s[...] = jnp.zeros_like(acc_s)

        def scores(j):
            off = pl.multiple_of(j * tq, tq)
            return lax.dot_general(qb, k_ref[pl.ds(off, tq), :], nt, preferred_element_type=F32) - ct_ref[:, pl.ds(off, tq)]

        def softmax_pv(j, sc):
            m_old = m_s[...]
            m_new = jnp.maximum(m_old, jnp.max(sc, axis=1, keepdims=True))
            p = _exp2_rows(sc, m_new)
            alpha = jnp.exp2(m_old - m_new)
            pv = jnp.dot(p.astype(BF16), vaug[pl.ds(pl.multiple_of(j * tq, tq), tq), :], preferred_element_type=F32)
            acc_s[...] = jnp.concatenate([alpha, alpha], axis=1) * acc_s[...] + pv
            m_s[...] = m_new

        s_buf[...] = scores(0)

        def loop(j, carry):
            nxt = scores(j + 1)
            softmax_pv(j, s_buf[...])
            s_buf[...] = nxt
            return carry

        lax.fori_loop(0, i, loop, 0)
        softmax_pv(i, _causal(s_buf[...], False))
        l = acc_s[:, HEAD_DIM:]
        o = acc_s[:, :HEAD_DIM] / l
        o_ref[...] = o
        oz_ref[...] = (o * _silu(z_ref[...].astype(F32))).astype(BF16)
        lse_row_ref[...] = _row_of(m_s[...] + jnp.log(l) * LOG2E)

    return pl.pallas_call(
        body, name=name, grid=(nh, nq),
        in_specs=[pl.BlockSpec((tq, HEAD_DIM), lambda h, i: (i, h)),
                  pl.BlockSpec((s, HEAD_DIM), lambda h, i: (0, h)),
                  pl.BlockSpec((s, HEAD_DIM), lambda h, i: (0, nh + h)),
                  pl.BlockSpec((None, 1, s), lambda h, i: (h, 0, 0)),
                  pl.BlockSpec((tq, HEAD_DIM), lambda h, i: (i, h))],
        out_specs=[pl.BlockSpec((tq, HEAD_DIM), lambda h, i: (i, h)),
                   pl.BlockSpec((tq, HEAD_DIM), lambda h, i: (i, h)),
                   pl.BlockSpec((None, 1, tq), lambda h, i: (h, 0, i))],
        out_shape=[jax.ShapeDtypeStruct((s, w), F32), jax.ShapeDtypeStruct((s, w), BF16),
                   jax.ShapeDtypeStruct((nh, 1, s), F32)],
        scratch_shapes=[pltpu.VMEM((tq, LANES), F32), pltpu.VMEM((tq, HEAD_DIM + LANES), F32),
                        pltpu.VMEM((s, HEAD_DIM + LANES), BF16), pltpu.VMEM((tq, tq), F32)],
        compiler_params=_cparams("arbitrary", "arbitrary"),
    )(q2, kv, kv, cum2_t, z)


def _fox_bwd(q2, kv, do, o, lse2_t, cum2, dqz, name):
    s, w = q2.shape
    nh = w // HEAD_DIM
    tk = _attn_tiles(s)
    nk = s // tk
    scale = HEAD_DIM ** -0.5
    nt = (_DOT_DIMS["nt"], ((), ()))
    tn = (_DOT_DIMS["tn"], ((), ()))

    def body(q_ref, k_ref, v_ref, do_ref, o_ref, lse_ref, c_ref, _, dk_ref, dv_ref, dq_ref, dcq_ref, dck_ref,
             dk_s, dv_s, dc_s, dq_s, dcq_s, dl_s, s_buf, dp_buf):
        h, j = pl.program_id(0), pl.program_id(1)

        @pl.when(j == 0)
        def _():
            dq_s[...] = jnp.zeros_like(dq_s)
            dcq_s[...] = jnp.zeros_like(dcq_s)
            for i in range(nk):
                rows = pl.ds(i * tk, tk)
                d = jnp.sum(do_ref[rows, :].astype(F32) * o_ref[rows, :], axis=1, keepdims=True)
                dl_s[:, i * tk:(i + 1) * tk] = _row_of(jnp.broadcast_to(d, (tk, LANES)))

        kb = k_ref[...]
        vb = v_ref[...]
        ck = jnp.broadcast_to(_head_col(c_ref[...], h), (tk, LANES))
        dk_s[...] = jnp.zeros_like(dk_s)
        dv_s[...] = jnp.zeros_like(dv_s)
        dc_s[...] = jnp.zeros_like(dc_s)

        def scores(i):
            off = pl.multiple_of(i * tk, tk)
            sc = lax.dot_general(kb, q_ref[pl.ds(off, tk), :], nt, preferred_element_type=F32) - lse_ref[:, pl.ds(off, tk)]
            dp = lax.dot_general(vb, do_ref[pl.ds(off, tk), :], nt, preferred_element_type=F32) - dl_s[:, pl.ds(off, tk)]
            return sc, dp

        def accumulate(i, sc, dp):
            off = pl.multiple_of(i * tk, tk)
            p = _exp2_rows(sc, ck)
            dv_s[...] += jnp.dot(p.astype(BF16), do_ref[pl.ds(off, tk), :], preferred_element_type=F32)
            ds = p * dp
            dsb = ds.astype(BF16)
            dk_s[...] += jnp.dot(dsb, q_ref[pl.ds(off, tk), :], preferred_element_type=F32)
            dq_s[pl.ds(off, tk), :] += lax.dot_general(dsb, kb, tn, preferred_element_type=F32)
            dcq_s[:, pl.ds(off, tk)] += jnp.sum(ds, axis=0, keepdims=True)
            part = ds[:, :LANES]
            for b in range(1, tk // LANES):
                part = part + ds[:, b * LANES:(b + 1) * LANES]
            dc_s[...] += part

        sc0, dp0 = scores(j)
        s_buf[...] = _causal(sc0, True)
        dp_buf[...] = dp0

        def loop(i, carry):
            nxt = scores(i + 1)
            accumulate(i, s_buf[...], dp_buf[...])
            s_buf[...], dp_buf[...] = nxt
            return carry

        lax.fori_loop(j, nk - 1, loop, 0)
        accumulate(nk - 1, s_buf[...], dp_buf[...])
        dk_ref[...] = (dk_s[...] * (1.0 / LOG2E)).astype(BF16)
        dv_ref[...] = dv_s[...].astype(BF16)
        dck_ref[...] = jnp.sum(jnp.transpose(dc_s[...]), axis=0, keepdims=True)

        @pl.when(j == nk - 1)
        def _():
            dq_ref[...] = (dq_s[...] * scale).astype(BF16)
            dcq_ref[...] = dcq_s[...]

    col = pl.BlockSpec((s, HEAD_DIM), lambda h, j: (0, h))
    row = pl.BlockSpec((None, 1, s), lambda h, j: (h, 0, 0))
    kspec = pl.BlockSpec((tk, HEAD_DIM), lambda h, j: (j, h))
    return pl.pallas_call(
        body, name=name, grid=(nh, nk),
        in_specs=[col, kspec, pl.BlockSpec((tk, HEAD_DIM), lambda h, j: (j, nh + h)), col, col, row,
                  pl.BlockSpec((tk, LANES), lambda h, j: (j, 0)), pl.BlockSpec(memory_space=pl.ANY)],
        out_specs=[kspec, kspec, col, row, pl.BlockSpec((None, 1, tk), lambda h, j: (h, 0, j))],
        out_shape=[jax.ShapeDtypeStruct((s, w), BF16), jax.ShapeDtypeStruct((s, w), BF16),
                   jax.ShapeDtypeStruct(dqz.shape, BF16), jax.ShapeDtypeStruct((nh, 1, s), F32),
                   jax.ShapeDtypeStruct((nh, 1, s), F32)],
        input_output_aliases={7: 2},
        scratch_shapes=[pltpu.VMEM((tk, HEAD_DIM), F32), pltpu.VMEM((tk, HEAD_DIM), F32), pltpu.VMEM((tk, LANES), F32),
                        pltpu.VMEM((s, HEAD_DIM), F32), pltpu.VMEM((1, s), F32), pltpu.VMEM((1, s), F32),
                        pltpu.VMEM((tk, tk), F32), pltpu.VMEM((tk, tk), F32)],
        compiler_params=_cparams("arbitrary", "arbitrary"),
    )(q2, kv, kv, do, o, lse2_t, cum2, dqz)


_ALL_PEERS = tuple(range(1, N_DEV))
_CHIP_PEERS = (1, 2, 4, 6)


def _exchange_copies(ins, outs, send_sems, recv_sems, local_sems, scatter, peers=_ALL_PEERS):
    x, y, c = (lax.axis_index(a) for a in MESH_AXES)
    me = 4 * x + 2 * y + c
    local, remote = [], []
    for a in range(len(ins)):
        local.append(pltpu.make_async_copy(ins[a].at[me] if scatter else ins[a], outs[a].at[me], local_sems.at[a]))
        for k in peers:
            px, py, pc = (1 - x if k & 4 else x), (1 - y if k & 2 else y), (1 - c if k & 1 else c)
            remote.append(pltpu.make_async_remote_copy(
                src_ref=ins[a].at[4 * px + 2 * py + pc] if scatter else ins[a], dst_ref=outs[a].at[me],
                send_sem=send_sems.at[a * (N_DEV - 1) + k - 1], recv_sem=recv_sems.at[a * (N_DEV - 1) + k - 1],
                device_id=(px, py, pc), device_id_type=pl.DeviceIdType.MESH))
    return local, remote


def _exchange_out_shapes(arrs, scatter):
    return [((N_DEV,) + a.shape[1:]) if scatter else ((N_DEV,) + a.shape) for a in arrs]


_HBM =pl.BlockSpec(memory_space=pltpu.HBM)
_SEM = pl.BlockSpec(memory_space=pltpu.SEMAPHORE)


def _exchange_start(arrs, scatter, name, after=(), peers=_ALL_PEERS):
    n = len(arrs)
    after = list(after)
    lands = [lax.empty(s, a.dtype) for s, a in zip(_exchange_out_shapes(arrs, scatter), arrs)]

    def body(*refs):
        ins, outs = refs[:n], refs[n:2 * n]
        send_sems, recv_sems, local_sems = refs[2 * n + len(after):2 * n + len(after) + 3]
        token = refs[-1]
        local, remote = _exchange_copies(ins, outs, send_sems, recv_sems, local_sems, scatter, peers)
        for cp in local + remote:
            cp.start()
        token[...] = jnp.zeros_like(token)

    hbm = lambda a: pltpu.HBM(a.shape, a.dtype)
    res = pl.pallas_call(
        body, name=name,
        out_shape=(pltpu.SemaphoreType.DMA((n * (N_DEV - 1),)), pltpu.SemaphoreType.DMA((n * (N_DEV - 1),)),
                   pltpu.SemaphoreType.DMA((n,)), *[hbm(a) for a in arrs], *[hbm(a) for a in lands],
                   jax.ShapeDtypeStruct((SUBLANES, LANES), F32)),
        in_specs=[_HBM] * (2 * n) + [pl.BlockSpec(memory_space=pl.ANY)] * len(after),
        out_specs=(_SEM, _SEM, _SEM, *[_HBM] * (2 * n), pl.BlockSpec(memory_space=pltpu.VMEM)),
        input_output_aliases={i: 3 + i for i in range(2 * n)},
        compiler_params=pltpu.CompilerParams(has_side_effects=pltpu.SideEffectType.DATAFLOW_SIDE_EFFECTING),
    )(*[pltpu.with_memory_space_constraint(a, pltpu.HBM) for a in list(arrs) + lands], *after)
    return (n, scatter, res[:3], res[3:3 + n], res[3 + n:3 + 2 * n], peers), res[-1]


def _exchange_wait(state, after, name):
    n, scatter, sems, srcs, lands, peers = state
    after = list(after) if isinstance(after, (list, tuple)) else [after]

    def body(*refs):
        ins, outs = refs[:n], refs[n:2 * n]
        send_sems, recv_sems, local_sems = refs[2 * n:2 * n + 3]
        local, remote = _exchange_copies(ins, outs, send_sems, recv_sems, local_sems, scatter, peers)
        for cp in remote:
            cp.wait_send()
            cp.wait_recv()
        for cp in local:
            cp.wait()

    hbm = lambda a: pltpu.HBM(a.shape, a.dtype)
    res = pl.pallas_call(
        body, name=name,
        out_shape=(*[hbm(a) for a in srcs], *[hbm(a) for a in lands]),
        in_specs=[_HBM] * (2 * n) + [_SEM] * 3 + [pl.BlockSpec(memory_space=pl.ANY)] * len(after),
        out_specs=tuple([_HBM] * (2 * n)),
        input_output_aliases={i: i for i in range(2 * n)},
        compiler_params=pltpu.CompilerParams(has_side_effects=pltpu.SideEffectType.DATAFLOW_SIDE_EFFECTING),
    )(*srcs, *lands, *sems, *after)
    return list(res[n:])


def _forward_to_sibling(slots, name):
    n = len(slots)
    hops = (2, 4, 6)

    def body(*refs):
        ins, outs, (send_sems, recv_sems) = refs[:n], refs[n:2 * n], refs[2 * n:]
        x, y, c = (lax.axis_index(a) for a in MESH_AXES)
        copies = []
        for a in range(n):
            for i, k in enumerate(hops):
                slot = 4 * (1 - x if k & 4 else x) + 2 * (1 - y if k & 2 else y) + c
                copies.append(pltpu.make_async_remote_copy(
                    src_ref=ins[a].at[slot], dst_ref=outs[a].at[slot],
                    send_sem=send_sems.at[a * len(hops) + i], recv_sem=recv_sems.at[a * len(hops) + i],
                    device_id=(x, y, 1 - c), device_id_type=pl.DeviceIdType.MESH))
        for cp in copies:
            cp.start()
        for cp in copies:
            cp.wait_send()
            cp.wait_recv()

    return pl.pallas_call(
        body, name=name, out_shape=[jax.ShapeDtypeStruct(s.shape, s.dtype) for s in slots],
        in_specs=[pl.BlockSpec(memory_space=pl.ANY)] * n, out_specs=[pl.BlockSpec(memory_space=pl.ANY)] * n,
        input_output_aliases={i: i for i in range(n)},
        scratch_shapes=[pltpu.SemaphoreType.DMA((n * len(hops),)), pltpu.SemaphoreType.DMA((n * len(hops),))],
    )(*slots)


def _adamw_math(w, g, m, v):
    m = ADAM_B1 * m + (1.0 - ADAM_B1) * g
    v = ADAM_B2 * v + (1.0 - ADAM_B2) * (g * g)
    m_hat = m / (1.0 - ADAM_B1 ** ADAM_STEP)
    v_hat = v / (1.0 - ADAM_B2 ** ADAM_STEP)
    return -ADAM_LR * (m_hat / (jnp.sqrt(v_hat) + ADAM_EPS) + ADAM_WD * w), m, v


def _slot_sum(p_ref):
    g = p_ref[0].astype(F32)
    for d in range(1, p_ref.shape[0]):
        g = g + p_ref[d].astype(F32)
    return g


def _adamw_tile(r, c):
    return _tile(r, max(SUBLANES, (256 * 1024) // c // SUBLANES * SUBLANES), SUBLANES)


def _adamw(parts, w, m, v, name):
    r, c = w.shape[-2:]
    by_cols = r % SUBLANES != 0
    tr, tc = (r, _tile(c, 256)) if by_cols else (_adamw_tile(r, c), c)

    def body(p_ref, w_ref, m_ref, v_ref, g_ref, d_ref, nm_ref, nv_ref):
        g = _slot_sum(p_ref)
        g_ref[...] = g
        d_ref[...], nm_ref[...], nv_ref[...] = _adamw_math(w_ref[...], g, m_ref[...], v_ref[...])

    pos = (lambda i: (0, i)) if by_cols else (lambda i: (i, 0))
    if w.ndim == 3:
        blk = pl.BlockSpec((None, tr, tc), lambda i: (0,) + pos(i))
    else:
        blk = pl.BlockSpec((tr, tc), pos)
    sh = jax.ShapeDtypeStruct(w.shape, F32)
    return pl.pallas_call(
        body, name=name, grid=(c // tc if by_cols else r // tr,),
        in_specs=[pl.BlockSpec((parts.shape[0], tr, tc), lambda i: (0,) + pos(i)), blk, blk, blk],
        out_specs=[blk] * 4, out_shape=[sh] * 4, compiler_params=_cparams("parallel"),
    )(parts, w, m, v)


def _sum_parts(parts, name):
    _, r, c = parts.shape
    tr = _adamw_tile(r, c)

    def body(p_ref, o_ref):
        o_ref[...] = _slot_sum(p_ref)

    return pl.pallas_call(
        body, name=name, grid=(r // tr,),
        in_specs=[pl.BlockSpec((parts.shape[0], tr, c), lambda i: (0, i, 0))],
        out_specs=pl.BlockSpec((tr, c), lambda i: (i, 0)), out_shape=jax.ShapeDtypeStruct((r, c), F32),
        compiler_params=_cparams("parallel"),
    )(parts)


def _lane_pad(a, width=LANES):
    return jnp.pad(a, ((0, 0), (0, width - a.shape[1])))


def _local_step(x, target, norm_pre, norm_post, kv_norm, kv_b_f, a_re, a_im, log_dt, b_re, b_im, c_re, c_im, comm):
    s, d = x.shape
    g, p = a_re.shape
    w = g * S5_GROUP
    fw = d
    nh = fw // HEAD_DIM
    seg_len = s // N_SEG
    row = lambda v: v.reshape(1, -1)
    g_pre0, g_pre1, g_post0, g_post1, g_kv = row(norm_pre[0]), row(norm_pre[1]), row(norm_post[0]), row(norm_post[1]), row(kv_norm)

    ldt = log_dt.reshape(g, 1)
    abr, abi, cr, ci = _s5_disc_fwd(a_re, a_im, ldt)
    cr_col, ci_col = cr.reshape(g * p, 1), ci.reshape(g * p, 1)
    b_re2, b_im2 = b_re.reshape(g * p, S5_GROUP), b_im.reshape(g * p, S5_GROUP)
    bb_re, bb_im = _s5_bbar_fwd(cr_col, ci_col, b_re2, b_im2)
    bd_re, bd_im = _block_diag(jnp.stack([bb_re, bb_im]).reshape(2, g, p, S5_GROUP)).astype(BF16)
    cd_re, cd_im = _block_diag(jnp.stack([c_re, -c_im])).astype(BF16)
    ab_re = jnp.broadcast_to(abr.reshape(1, g * p), (N_SEG, g * p))
    ab_im = jnp.broadcast_to(abi.reshape(1, g * p), (N_SEG, g * p))
    zero_seg = jnp.zeros((N_SEG, g * p), F32)

    xn0 = _norm_cast(x, g_pre0 + comm.token, "norm_pre0", x_kind="nat")
    w_in = comm.weight("s5_w_in", [xn0, bd_re, bd_im, cd_re, cd_im, ab_re, ab_im])
    d_row, bglu_row = row(comm.vector("s5_d")), row(comm.vector("s5_b_glu"))
    u = _mm(xn0, w_in, "nn", BF16, "s5_in_u", b_cols=(0, w), b_slots=True)
    z0 = _mm(xn0, w_in, "nn", BF16, "s5_in_z", b_cols=(w, w), b_slots=True)
    e_re, e_im = _s5_scan_fwd(u, bd_re, bd_im, cd_re, cd_im, ab_re, ab_im, zero_seg, zero_seg, d_row, False, "s5_scan_ends")
    i_re, i_im = _s5_seg_fix(e_re, e_im, ab_re, ab_im, seg_len, False, "s5_seg_fix")
    y_ssm, yg, h_re, h_im, _, _ = _s5_scan_fwd(u, bd_re, bd_im, cd_re, cd_im, ab_re, ab_im, i_re, i_im, d_row, True, "s5_scan")
    w_glu, w_out = comm.weight("s5_w_glu", yg), comm.weight("s5_w_out", yg)
    gp = _mm(yg, w_glu, "nn", BF16, "s5_glu")
    y3 = _s5_gate(y_ssm, gp, bglu_row, z0, "s5_gate")
    w_kvt, fw_in = comm.weight("kv_w", y3), comm.weight("fox_w_in", y3)
    w_ft = jnp.pad(w_kvt[2 * fw:], ((0, LANES - nh), (0, 0)))
    o0 = _mm(y3, w_out, "nn", F32, "s5_out")

    h1, hn_kv, xn1 = _resid_norm2(x, o0, g_post0 + comm.late_token, g_kv, g_pre1, "resid_norms")
    kv = _mm(hn_kv, w_kvt, "nt", BF16, "kv_proj", b_rows=2 * fw)
    f_logit = _mm(hn_kv, w_ft, "nt", F32, "f_proj")
    bf_row = _lane_pad(row(kv_b_f))
    cum2 = _cum_fwd(f_logit, bf_row, "cum_fwd")
    cum2_t = cum2[:, :nh].T.reshape(nh, 1, s)
    q2 = _mm(xn1, fw_in, "nn", BF16, "fox_q", scale=HEAD_DIM ** -0.5 * LOG2E, b_cols=(0, fw), b_slots=True)
    z1 = _mm(xn1, fw_in, "nn", BF16, "fox_z", b_cols=(fw, fw), b_slots=True)
    o, oz, lse2_t = _fox_fwd(q2, kv, cum2_t, z1, "fox_fwd")
    fw_out = comm.weight("fox_w_out", oz)
    o1 = _mm(oz, fw_out, "nn", F32, "fox_out")
    dh2, do1, sq, dg_post1 = _post_norm_loss(o1, g_post1, h1, target, "norm_post1_loss")
    loss = 0.5 * jnp.sum(sq) / d

    d_fw_out = _mm(oz, do1, "tn", BF16, "fox_out_dw")
    d_oz = _mm(do1, fw_out, "nt", BF16, "fox_out_dx")
    do, dqz = _gate_bwd(d_oz, o, z1, "fox_gate_bwd")
    dk, dv, dqz, dcq, dck = _fox_bwd(q2, kv, do, o, lse2_t, cum2, dqz, "fox_bwd")
    d_fw_in = _mm(xn1, dqz, "tn", BF16, "fox_in_dw", col_slots=True)
    dxn1 = _mm(dqz, fw_in, "nt", BF16, "fox_in_dx", b_slots=True)
    dcq_sl = _lane_pad(dcq.reshape(nh, s).T)
    dck_sl = _lane_pad(dck.reshape(nh, s).T)
    df, db_f = _cum_bwd(dcq_sl, dck_sl, f_logit, bf_row, "cum_bwd")
    dkv = _concat_cast(dk, dv, "fox_dkv")
    d_w_kvmt = _mm(dkv, hn_kv, "tn", BF16, "kv_dw")
    d_w_ft = _mm(df, hn_kv, "tn", BF16, "f_dw")
    dhn_f = _mm(df, w_ft, "nn", F32, "f_dx")
    dhn_kv = _mm(dkv, w_kvt, "nn", BF16, "kv_dx", add=dhn_f, b_rows=2 * fw)
    d_w_kvt = jnp.concatenate([d_w_kvmt, d_w_ft[:nh]], axis=0)
    tok = comm.send_grads(dict(fox_w_out=d_fw_out, fox_w_in=d_fw_in, kv_w=d_w_kvt), "exchange_fox")
    dh1, do0, dg_pre1, dg_kv, dg_post0 = _norm_bwd2(dh2, h1, dxn1, dhn_kv, g_pre1, g_kv, o0, g_post0 + tok[0, 0],
                                                      "resid_norms_bwd")

    d_w_out = _mm(y3, do0, "tn", BF16, "s5_out_dw")
    dy3 = _mm(do0, w_out, "nt", BF16, "s5_out_dx")
    duz, dgp, dyg_direct, db_glu = _s5_gate_bwd(dy3, y_ssm, gp, bglu_row, z0, "s5_gate_bwd")
    d_w_glu = _mm(yg, dgp, "tn", BF16, "s5_glu_dw")
    gelu_bwd = lambda dyg, y: jax.vjp(jax.nn.gelu, y)[1](dyg)[0]
    dy_ssm = _mm(dgp, w_glu, "nt", F32, "s5_glu_dx", add=dyg_direct, epilogue=(gelu_bwd, y_ssm))
    d_row = d_row + comm.send_grads(dict(s5_w_out=d_w_out, s5_w_glu=d_w_glu), "exchange_s5")[0, 0]
    ab_imn = -ab_im
    ge_re, ge_im = _s5_scan_bwd(dy_ssm, u, h_re, h_im, bd_re, bd_im, cd_re, cd_im, ab_re, ab_imn, zero_seg, zero_seg,
                                d_row, False, "s5_adj_ends")
    gi_re, gi_im = _s5_seg_fix(ge_re, ge_im, ab_re, ab_imn, seg_len, True, "s5_adj_fix")
    duz, dbd_re, dbd_im, dcd_re, dcd_im, dab_re, dab_im, dd = _s5_scan_bwd(
        dy_ssm, u, h_re, h_im, bd_re, bd_im, cd_re, cd_im, ab_re, ab_imn, gi_re, gi_im, d_row, True, "s5_adj", duz=duz)
    d_w_in = _mm(xn0, duz, "tn", BF16, "s5_in_dw", col_slots=True)
    tok = comm.send_grads(dict(s5_w_in=d_w_in), "exchange_s5_in")
    dxn0 = _mm(duz, w_in, "nt", BF16, "s5_in_dx", after=tok, b_slots=True)
    grad_x, dg_pre0 = _norm_bwd1(dh1, x, dxn0, g_pre0, "norm_pre0_bwd")

    dbb_re = _block_diag_extract(dbd_re, p, S5_GROUP).reshape(g * p, S5_GROUP)
    dbb_im = _block_diag_extract(dbd_im, p, S5_GROUP).reshape(g * p, S5_GROUP)
    dcr_col, dci_col, db_re, db_im = _s5_bbar_bwd(cr_col, ci_col, b_re2, b_im2, dbb_re, dbb_im)
    da_re, da_im, dldt = _s5_disc_bwd(a_re, a_im, ldt, dab_re.reshape(g, p), dab_im.reshape(g, p),
                                      dcr_col.reshape(g, p), dci_col.reshape(g, p))
    dc_re = _block_diag_extract(dcd_re, S5_GROUP, p)
    dc_im = -_block_diag_extract(dcd_im, S5_GROUP, p)

    small = dict(
        norm_pre=jnp.concatenate([dg_pre0, dg_pre1], axis=0), norm_post=jnp.concatenate([dg_post0, dg_post1], axis=0),
        s5_a_re=da_re, s5_a_im=da_im, s5_log_dt=dldt.reshape(g), s5_b_re=db_re.reshape(g, p, S5_GROUP),
        s5_b_im=db_im.reshape(g, p, S5_GROUP), s5_c_re=dc_re, s5_c_im=dc_im, s5_d=dd.reshape(-1),
        s5_b_glu=db_glu.reshape(-1), kv_norm=dg_kv.reshape(-1), kv_b_f=db_f[0, :nh])
    return loss, grad_x, small


_BIG = ("s5_w_in", "s5_w_glu", "s5_w_out", "kv_w", "fox_w_in", "fox_w_out")
_COL_SHARDED = ("s5_w_in", "fox_w_in")
_SMALL = ("norm_pre", "norm_post", "s5_a_re", "s5_a_im", "s5_log_dt", "s5_b_re", "s5_b_im", "s5_c_re", "s5_c_im",
          "s5_d", "s5_b_glu", "kv_norm", "kv_b_f")
_SMALL_SHARDED = ("s5_d", "s5_b_glu")
_PACK_QUANTUM = SUBLANES * LANES
_WEIGHTS = ('norm_pre', 'norm_post', 's5_w_in', 's5_a_re', 's5_a_im', 's5_log_dt', 's5_b_re', 's5_b_im', 's5_c_re', 's5_c_im',
            's5_d', 's5_w_glu', 's5_b_glu', 's5_w_out', 'kv_norm', 'kv_w', 'kv_b_f', 'fox_w_in', 'fox_w_out')


def _full_from_slots(name, slots):
    n, r, c = slots.shape
    if name in _COL_SHARDED:
        return slots.transpose(1, 0, 2).reshape(r, n * c)
    return slots.reshape(n * r, c)


def _slots_from_full(name, full):
    if name in _COL_SHARDED:
        r, nc = full.shape
        return full.reshape(r, N_DEV, nc // N_DEV).transpose(1, 0, 2)
    nr, c = full.shape
    return full.reshape(N_DEV, nr // N_DEV, c)


def _groups_last(shape):
    return len(shape) >= 3 and shape[-1] < LANES and shape[-3] % LANES == 0


def _pack(vals):
    parts = []
    for v in vals:
        flat = jnp.moveaxis(v, -3, -1).reshape(-1) if _groups_last(v.shape) else v.reshape(-1)
        parts.append(jnp.pad(flat, (0, (-flat.shape[0]) % _PACK_QUANTUM)))
    total = sum(p.shape[0] for p in parts)
    parts.append(jnp.zeros(((-total) % (N_DEV * _PACK_QUANTUM),), F32))
    return jnp.concatenate(parts).reshape(-1, LANES)


def _unpack(packed, shapes):
    flat = packed.reshape(-1)
    out, off = [], 0
    for sh in shapes:
        n = math.prod(sh)
        piece = flat[off:off + n]
        if _groups_last(sh):
            piece = jnp.moveaxis(piece.reshape(sh[:-3] + sh[-2:] + sh[-3:-2]), -1, -3)
        out.append(piece.reshape(sh))
        off += n + (-n) % _PACK_QUANTUM
    return out


class _Comm:
    _GROUPS = (("s5_w_in",) + _SMALL_SHARDED, ("s5_w_glu", "s5_w_out"), ("kv_w", "fox_w_in"), ("fox_w_out",))
    _SLOT_FORM = ("s5_w_in", "fox_w_in")

    def __init__(self, shards, vectors, early=()):
        self._shards = {**shards, **vectors}
        self._full, self._gathers = {}, {}
        self._early = list(early)
        self.token = jnp.zeros((), F32)
        for group in self._GROUPS[:-1]:
            self.token = self.token + self._start(group, ())[0, 0]
        self.late_token = None
        self._sent = []

    def _start(self, group, after):
        state, tok = _exchange_start([self._shards[n] for n in group], False, "gather_start_" + group[0], after,
                                     peers=_CHIP_PEERS)
        self._gathers[group] = state
        return tok

    def vector(self, name):
        return self._full[name]

    def weight(self, name, after):
        if name not in self._full:
            group = next(g for g in self._GROUPS if name in g)
            if group == self._GROUPS[0]:
                after = (list(after) if isinstance(after, (list, tuple)) else [after]) + self._early
            slots = _exchange_wait(self._gathers.pop(group), after, "gather_wait_" + group[0])
            slots = _forward_to_sibling(slots, "gather_forward_" + group[0])
            for n, sl in zip(group, slots):
                if n in _SMALL_SHARDED:
                    self._full[n] = sl.reshape(-1)
                else:
                    self._full[n] = sl if n in self._SLOT_FORM else _full_from_slots(n, sl)
            if group == self._GROUPS[-2]:
                self.late_token = self._start(self._GROUPS[-1], [slots[0]])[0, 0]
        return self._full[name]

    def send_grads(self, grads, name):
        names = list(grads)
        slots = [grads[n] if grads[n].ndim == 3 else _slots_from_full(n, grads[n]).astype(BF16) for n in names]
        state, tok = _exchange_start(slots, True, name + "_start")
        self._sent.append((names, state, name + "_wait"))
        return tok

    def received_grads(self, group, after):
        names, state, name = self._sent[group]
        return list(zip(names, _exchange_wait(state, after, name)))


def kernel(x, norm_pre, norm_post, s5_w_in, s5_a_re, s5_a_im, s5_log_dt, s5_b_re, s5_b_im, s5_c_re, s5_c_im, s5_d, s5_w_glu, s5_b_glu, s5_w_out, kv_norm, kv_w, kv_b_f, fox_w_in, fox_w_out, loss_target, m_norm_pre, m_norm_post, m_s5_w_in, m_s5_a_re, m_s5_a_im, m_s5_log_dt, m_s5_b_re, m_s5_b_im, m_s5_c_re, m_s5_c_im, m_s5_d, m_s5_w_glu, m_s5_b_glu, m_s5_w_out, m_kv_norm, m_kv_w, m_kv_b_f, m_fox_w_in, m_fox_w_out, v_norm_pre, v_norm_post, v_s5_w_in, v_s5_a_re, v_s5_a_im, v_s5_log_dt, v_s5_b_re, v_s5_b_im, v_s5_c_re, v_s5_c_im, v_s5_d, v_s5_w_glu, v_s5_b_glu, v_s5_w_out, v_kv_norm, v_kv_w, v_kv_b_f, v_fox_w_in, v_fox_w_out):
    env = dict(locals())
    wts = {n: env[n] for n in _WEIGHTS}
    mom = {n: env["m_" + n] for n in _WEIGHTS}
    var = {n: env["v_" + n] for n in _WEIGHTS}
    me = 4 * lax.axis_index("x") + 2 * lax.axis_index("y") + lax.axis_index("c")
    shard2d = {n: (wts[n].T if n == "kv_w" else wts[n].reshape(wts[n].shape[-2:])) for n in _BIG}
    full_shape = {n: ((wts[n].size * N_DEV,) if n in _SMALL_SHARDED else wts[n].shape) for n in _SMALL}

    def spread(n, v):
        if n not in _SMALL_SHARDED:
            return v
        flat = v.reshape(-1)
        return lax.dynamic_update_slice(jnp.zeros(full_shape[n], F32), flat, (me * flat.shape[0],))

    packed = [_pack([spread(n, src[n]) for n in _SMALL] + [jnp.zeros((1,), F32)]) for src in (wts, mom, var)]
    comm = _Comm({n: _cast_bf16(shard2d[n], "cast_" + n) for n in _BIG}, {n: wts[n].reshape(1, -1) for n in _SMALL_SHARDED}, packed)

    loss_local, grad_x, small = _local_step(
        x[0], loss_target[0], norm_pre, norm_post, kv_norm, kv_b_f, s5_a_re[0], s5_a_im[0], s5_log_dt[0],
        s5_b_re[0], s5_b_im[0], s5_c_re[0], s5_c_im[0], comm)

    small_pack = _pack([small[n] for n in _SMALL] + [loss_local.reshape(1)])
    slice_rows = small_pack.shape[0] // N_DEV
    small_state, small_tok = _exchange_start([small_pack.reshape(N_DEV, slice_rows, LANES)], True, "reduce_small_start")

    res = {}

    def finish(group, after):
        for n, recv in comm.received_grads(group, after):
            if n == "kv_w":
                res[n] = [o.T for o in _adamw(recv, wts[n].T, mom[n].T, var[n].T, "adamw_" + n)]
            else:
                res[n] = _adamw(recv, wts[n], mom[n], var[n], "adamw_" + n)

    finish(0, [small_tok, grad_x])
    my_sum = _sum_parts(_exchange_wait(small_state, res["kv_w"][0], "reduce_small_wait")[0], "sum_small")
    gather_state, gather_tok = _exchange_start([my_sum], False, "gather_small_start")
    finish(1, gather_tok)
    finish(2, gather_tok)
    g_all = _exchange_wait(gather_state, res["s5_w_in"][0], "gather_small_wait")[0].reshape(1, small_pack.shape[0], LANES)
    outs = _adamw(g_all, *packed, "adamw_small")
    unpacked = [_unpack(o, [full_shape[n] for n in _SMALL] + [(1,)]) for o in outs]
    loss = unpacked[0][-1][0]
    for i, n in enumerate(_SMALL):
        vals = [u[i] for u in unpacked]
        if n in _SMALL_SHARDED:
            k = wts[n].size
            vals = [lax.dynamic_slice(v, (me * k,), (k,)) for v in vals]
        res[n] = [v.reshape(wts[n].shape) for v in vals]

    return (loss, grad_x[None], *[res[n][0] for n in _WEIGHTS], *[res[n][1] for n in _WEIGHTS],
            *[res[n][2] for n in _WEIGHTS], *[res[n][3] for n in _WEIGHTS])
```

```python
import math

import jax
import jax.numpy as jnp
from jax import lax
from jax.experimental import pallas as pl
from jax.experimental.pallas import tpu as pltpu

F32 = jnp.float32
BF16 = jnp.bfloat16

N_DEV = 8
MESH_AXES = ("x", "y", "c")
S5_GROUP = 16
S5_STATE = 64
LANES = 128
SUBLANES = 8
GROUPS_PER_BLOCK = LANES // S5_GROUP
BLOCK_STATE = GROUPS_PER_BLOCK * S5_STATE
N_SEG = SUBLANES
HEAD_DIM = 128
RMS_EPS = 1e-6
NEG_INF = -1e30
LOG2E = math.log2(math.e)
ADAM_LR = 0.001
ADAM_B1 = 0.9
ADAM_B2 = 0.999
ADAM_EPS = 1e-08
ADAM_WD = 0.01
ADAM_STEP = 10
VMEM_LIMIT = 56 * 1024 * 1024


def _tile(n, pref, quantum=LANES):
    if n <= pref:
        return n
    t = (pref // quantum) * quantum
    while t >= quantum:
        if n % t == 0:
            return t
        t -= quantum
    return n


def _cparams(*sem):
    return pltpu.CompilerParams(dimension_semantics=sem if sem else None, vmem_limit_bytes=VMEM_LIMIT)


_DOT_DIMS = {"nn": ((1,), (0,)), "nt": ((1,), (1,)), "tn": ((0,), (0,))}


def _mm(a, b, mode, out_dtype, name, add=None, scale=None, b_cols=None, after=None, col_slots=False, b_slots=False,
        b_rows=None, epilogue=None):
    slot_w = b.shape[2] if b_slots else None
    b2d = (b.shape[1], b.shape[0] * b.shape[2]) if b_slots else b.shape
    b_shape = b2d if b_cols is None else (b2d[0], b_cols[1])
    if b_rows is not None:
        b_shape = (b_rows, b_shape[1])
    if mode == "nn":
        (M, K), (K2, N) = a.shape, b_shape
    elif mode == "nt":
        (M, K), (N, K2) = a.shape, b_shape
    else:
        (K, M), (K2, N) = a.shape, b_shape
    assert K == K2, (name, a.shape, b_shape)
    wide = K <= 2048 and add is None and epilogue is None
    tm = _tile(M, 2048 if wide else 1024 if K <= 2048 else 512)
    tn, tk = (N // N_DEV if col_slots else _tile(N, 1024)), _tile(K, 4096)
    if b_slots and mode == "nn":
        tn = slot_w
    nk = K // tk
    dims = (_DOT_DIMS[mode], ((), ()))
    col0 = 0
    if b_cols is not None:
        assert mode != "tn" and b_cols[0] % (tn if mode == "nn" else tk) == 0
        col0 = b_cols[0] // (tn if mode == "nn" else tk)
    assert not b_slots or (mode == "nn" or (mode == "nt" and nk == 1 and b_cols is None))

    def body(*refs):
        a_ref, b_ref = refs[:2]
        c_ref = refs[2] if add is not None else None
        e_ref = refs[2 + (add is not None)] if epilogue is not None else None
        o_ref = refs[2 + (add is not None) + (epilogue is not None) + (after is not None)]
        if b_slots and mode == "nt":
            part = lax.dot_general(a_ref[:, :slot_w], b_ref[0], dims, preferred_element_type=F32)
            for sl in range(1, b_ref.shape[0]):
                part += lax.dot_general(a_ref[:, sl * slot_w:(sl + 1) * slot_w], b_ref[sl], dims, preferred_element_type=F32)
        else:
            part = lax.dot_general(a_ref[...], b_ref[...], dims, preferred_element_type=F32)

        def finish(r):
            if scale is not None:
                r = r * scale
            if add is not None:
                r = r + c_ref[...]
            if epilogue is not None:
                r = epilogue[0](r, e_ref[...])
            o_ref[...] = r.astype(out_dtype)

        if nk == 1:
            finish(part)
            return
        acc = refs[-1]
        k = pl.program_id(2)

        @pl.when(k == 0)
        def _():
            acc[...] = part

        @pl.when(jnp.logical_and(k > 0, k < nk - 1))
        def _():
            acc[...] += part

        @pl.when(k == nk - 1)
        def _():
            finish(acc[...] + part)

    if mode == "tn":
        a_spec = pl.BlockSpec((tk, tm), lambda i, j, k: (k, i))
    else:
        a_spec = pl.BlockSpec((tm, tk), lambda i, j, k: (i, k))
    if b_slots and mode == "nn":
        b_spec = pl.BlockSpec((None, tk, tn), lambda i, j, k: (j + col0, k, 0))
    elif b_slots:
        b_spec = pl.BlockSpec((b.shape[0], tn, slot_w), lambda i, j, k: (0, j, 0))
    elif mode == "nt":
        b_spec = pl.BlockSpec((tn, tk), lambda i, j, k: (j, k + col0))
    else:
        b_spec = pl.BlockSpec((tk, tn), lambda i, j, k: (k, j + col0))
    o_spec = pl.BlockSpec((tm, tn), lambda i, j, k: (i, j))
    in_specs = [a_spec, b_spec] + ([o_spec] if add is not None else [])
    args = (a, b) + ((add,) if add is not None else ())
    if epilogue is not None:
        in_specs.append(o_spec)
        args += (epilogue[1],)
    if after is not None:
        in_specs.append(pl.BlockSpec(after.shape, lambda i, j, k: (0, 0)))
        args += (after,)
    out_shape = jax.ShapeDtypeStruct((M, N), out_dtype)
    if col_slots:
        assert add is None
        o_spec = pl.BlockSpec((None, tm, tn), lambda i, j, k: (j, i, 0))
        out_shape = jax.ShapeDtypeStruct((N_DEV, M, tn), out_dtype)
    return pl.pallas_call(
        body, name=name, grid=(M // tm, N // tn, nk),
        in_specs=in_specs, out_specs=o_spec,
        out_shape=out_shape,
        scratch_shapes=[pltpu.VMEM((tm, tn), F32)] if nk > 1 else [],
        compiler_params=_cparams("parallel", "parallel", "arbitrary"),
    )(*args)


class _NatIn:
    def __init__(self, ref):
        self.ref = ref

    def __getitem__(self, idx):
        v = jnp.swapaxes(self.ref[...], 0, 1)
        return v.reshape(v.shape[0] * N_SEG, v.shape[2])


class _NatOut:
    def __init__(self, ref):
        self.ref = ref

    def __setitem__(self, idx, val):
        self.ref[...] = jnp.swapaxes(val.reshape(val.shape[0] // N_SEG, N_SEG, val.shape[1]), 0, 1)


def _rowcall(body, name, n_rows, ins, outs, tile_rows=256):
    tr = _tile(n_rows, tile_rows, SUBLANES * 2)
    n_in = len(ins)
    in_kinds = [k for _, k in ins]
    kinds = [k for _, _, k in outs]

    def kern(*refs):
        @pl.when(pl.program_id(0) == 0)
        def _():
            for r, kind in zip(refs[n_in:], kinds):
                if kind == "acc":
                    r[...] = jnp.zeros_like(r)

        wrapped = [_NatIn(r) if k == "nat" else r for r, k in zip(refs[:n_in], in_kinds)]
        wrapped += [_NatOut(r) if k == "nat" else r for r, k in zip(refs[n_in:], kinds)]
        body(*wrapped)

    in_specs, args = [], []
    for arr, kind in ins:
        if kind == "row":
            in_specs.append(pl.BlockSpec((tr, arr.shape[1]), lambda i: (i, 0)))
        elif kind == "nat":
            in_specs.append(pl.BlockSpec((N_SEG, tr // N_SEG, arr.shape[1]), lambda i: (0, i, 0)))
            arr = arr.reshape(N_SEG, n_rows // N_SEG, arr.shape[1])
        else:
            in_specs.append(pl.BlockSpec(arr.shape, lambda i, nd=arr.ndim: (0,) * nd))
        args.append(arr)
    out_specs, out_shape = [], []
    for width, dtype, kind in outs:
        if kind == "row":
            out_specs.append(pl.BlockSpec((tr, width), lambda i: (i, 0)))
            out_shape.append(jax.ShapeDtypeStruct((n_rows, width), dtype))
        elif kind == "right":
            out_specs.append(pl.BlockSpec((tr, width), lambda i: (i, 1)))
            out_shape.append(jax.ShapeDtypeStruct((n_rows, 2 * width), dtype))
        elif kind == "nat":
            out_specs.append(pl.BlockSpec((N_SEG, tr // N_SEG, width), lambda i: (0, i, 0)))
            out_shape.append(jax.ShapeDtypeStruct((N_SEG, n_rows // N_SEG, width), dtype))
        else:
            out_specs.append(pl.BlockSpec((1, width), lambda i: (0, 0)))
            out_shape.append(jax.ShapeDtypeStruct((1, width), F32))
    res = pl.pallas_call(
        kern, name=name, grid=(n_rows // tr,), in_specs=in_specs, out_specs=out_specs, out_shape=out_shape,
        compiler_params=_cparams("arbitrary"),
    )(*args)
    return [r.reshape(n_rows, r.shape[2]) if k == "nat" else r for r, k in zip(res, kinds)]


def _rstd(x):
    return lax.rsqrt(jnp.mean(x * x, axis=-1, keepdims=True) + RMS_EPS)


def _rms_bwd(x, g, dy):
    xh = x * _rstd(x)
    dxh = dy * g
    dx = _rstd(x) * (dxh - xh * jnp.mean(dxh * xh, axis=-1, keepdims=True))
    return dx, jnp.sum(dy * xh, axis=0, keepdims=True)


def _silu(z):
    return z * jax.nn.sigmoid(z)


def _norm_cast(x, g, name, x_kind="row"):
    def body(x_ref, g_ref, o_ref):
        x = x_ref[...]
        o_ref[...] = (x * _rstd(x) * g_ref[...]).astype(BF16)

    return _rowcall(body, name, x.shape[0], [(x, x_kind), (g, "full")], [(x.shape[1], BF16, "row")])[0]


def _resid_norm2(x, o, g_post, g_kv, g_pre, name):
    def body(x_ref, o_ref, go_ref, gk_ref, gp_ref, h_ref, nk_ref, np_ref):
        o = o_ref[...]
        h = x_ref[...] + o * _rstd(o) * go_ref[...]
        h_ref[...] = h
        hn = h * _rstd(h)
        nk_ref[...] = (hn * gk_ref[...]).astype(BF16)
        np_ref[...] = (hn * gp_ref[...]).astype(BF16)

    d = x.shape[1]
    return _rowcall(body, name, x.shape[0], [(x, "nat"), (o, "row"), (g_post, "full"), (g_kv, "full"), (g_pre, "full")],
                    [(d, F32, "nat"), (d, BF16, "nat"), (d, BF16, "nat")])


def _post_norm_loss(o, g, h1, target, name):
    d = o.shape[1]

    def body(o_ref, g_ref, h_ref, t_ref, dh_ref, do_ref, acc_ref, dg_ref):
        o = o_ref[...]
        e = h_ref[...] + o * _rstd(o) * g_ref[...] - t_ref[...]
        dh = e * (1.0 / d)
        dh_ref[...] = dh
        acc_ref[...] += jnp.sum(e * e, axis=0, keepdims=True)
        dx, dg = _rms_bwd(o, g_ref[...], dh)
        do_ref[...] = dx.astype(BF16)
        dg_ref[...] += dg

    return _rowcall(body, name, o.shape[0], [(o, "row"), (g, "full"), (h1, "row"), (target, "row")],
                    [(d, F32, "row"), (d, BF16, "row"), (d, F32, "acc"), (d, F32, "acc")])


def _gate_bwd(d_oz, o, z, name):
    def body(d_ref, o_ref, z_ref, do_ref, dz_ref):
        _, vjp = jax.vjp(lambda o, z: o * _silu(z), o_ref[...], z_ref[...].astype(F32))
        do, dz = vjp(d_ref[...].astype(F32))
        do_ref[...] = do.astype(BF16)
        dz_ref[...] = dz.astype(BF16)

    w = o.shape[1]
    return _rowcall(body, name, o.shape[0], [(d_oz, "row"), (o, "row"), (z, "row")], [(w, BF16, "row"), (w, BF16, "right")])


def _norm_bwd2(dh2, h1, dxn1, dhn_kv, g_pre, g_kv, o0, g_post0, name):
    def body(dh2_ref, h_ref, d1_ref, dk_ref, gp_ref, gk_ref, o_ref, go_ref, dh1_ref, do_ref, dgp_ref, dgk_ref, dgo_ref):
        h = h_ref[...]
        dx1, dg1 = _rms_bwd(h, gp_ref[...], d1_ref[...].astype(F32))
        dxk, dgk = _rms_bwd(h, gk_ref[...], dk_ref[...].astype(F32))
        dh1 = dh2_ref[...] + dx1 + dxk
        dh1_ref[...] = dh1
        dgp_ref[...] += dg1
        dgk_ref[...] += dgk
        dxo, dgo = _rms_bwd(o_ref[...], go_ref[...], dh1)
        do_ref[...] = dxo.astype(BF16)
        dgo_ref[...] += dgo

    d = h1.shape[1]
    return _rowcall(body, name, h1.shape[0],
                    [(dh2, "nat"), (h1, "nat"), (dxn1, "nat"), (dhn_kv, "nat"), (g_pre, "full"), (g_kv, "full"),
                     (o0, "row"), (g_post0, "full")],
                    [(d, F32, "nat"), (d, BF16, "row"), (d, F32, "acc"), (d, F32, "acc"), (d, F32, "acc")])


def _norm_bwd1(dres, x, dxn, g, name):
    def body(dr_ref, x_ref, dn_ref, g_ref, dx_ref, dg_ref):
        dx, dg = _rms_bwd(x_ref[...], g_ref[...], dn_ref[...].astype(F32))
        dx_ref[...] = dr_ref[...] + dx
        dg_ref[...] += dg

    d = x.shape[1]
    return _rowcall(body, name, x.shape[0], [(dres, "nat"), (x, "nat"), (dxn, "row"), (g, "full")],
                    [(d, F32, "nat"), (d, F32, "acc")])


def _s5_gate(y_ssm, gp, b_glu, z, name):
    def body(y_ref, gp_ref, b_ref, z_ref, o_ref):
        yg = jax.nn.gelu(y_ref[...])
        o_ref[...] = (yg * jax.nn.sigmoid(gp_ref[...] + b_ref[...]) * _silu(z_ref[...].astype(F32))).astype(BF16)

    return _rowcall(body, name, y_ssm.shape[0], [(y_ssm, "row"), (gp, "row"), (b_glu, "full"), (z, "row")],
                    [(y_ssm.shape[1], BF16, "row")])[0]


def _s5_gate_bwd(dy3, y_ssm, gp, b_glu, z, name):
    def body(d_ref, y_ref, gp_ref, b_ref, z_ref, dz_ref, dgp_ref, dyg_ref, db_ref):
        yg = jax.nn.gelu(y_ref[...])
        _, vjp = jax.vjp(lambda yg, gp, z: yg * jax.nn.sigmoid(gp) * _silu(z), yg, gp_ref[...] + b_ref[...],
                         z_ref[...].astype(F32))
        dyg, dgp, dz = vjp(d_ref[...].astype(F32))
        dz_ref[...] = dz.astype(BF16)
        dgp_ref[...] = dgp.astype(BF16)
        dyg_ref[...] = dyg
        db_ref[...] += jnp.sum(dgp, axis=0, keepdims=True)

    w = y_ssm.shape[1]
    return _rowcall(body, name, y_ssm.shape[0],
                    [(dy3, "row"), (y_ssm, "row"), (gp, "row"), (b_glu, "full"), (z, "row")],
                    [(w, BF16, "right"), (w, BF16, "row"), (w, F32, "row"), (w, F32, "acc")])


def _cast_bf16(x, name):
    r, c = x.shape
    by_cols = r % (2 * SUBLANES) != 0
    tr, tc = (r, _tile(c, 256)) if by_cols else (_tile(r, 512, 2 * SUBLANES), c)
    pos = (lambda i: (0, i)) if by_cols else (lambda i: (i, 0))

    def body(x_ref, o_ref):
        o_ref[...] = x_ref[...].astype(BF16)

    return pl.pallas_call(
        body, name=name, grid=(c // tc if by_cols else r // tr,),
        in_specs=[pl.BlockSpec((tr, tc), pos)], out_specs=pl.BlockSpec((tr, tc), pos),
        out_shape=jax.ShapeDtypeStruct((r, c), BF16), compiler_params=_cparams("parallel"),
    )(x)


def _concat_cast(a, b, name):
    def body(a_ref, b_ref, o_ref):
        w = a_ref.shape[1]
        o_ref[:, :w] = a_ref[...].astype(BF16)
        o_ref[:, w:] = b_ref[...].astype(BF16)

    return _rowcall(body, name, a.shape[0], [(a, "row"), (b, "row")], [(a.shape[1] + b.shape[1], BF16, "row")])[0]


def _disc(ar, ai, ldt):
    dt = jnp.exp(ldt)
    mag = jnp.exp(ar * dt)
    abr = mag * jnp.cos(ai * dt)
    abi = mag * jnp.sin(ai * dt)
    den = ar * ar + ai * ai
    nr = abr - 1.0
    return abr, abi, (nr * ar + abi * ai) / den, (abi * ar - nr * ai) / den


def _s5_disc_fwd(a_re, a_im, ldt):
    def body(ar, ai, ld, o1, o2, o3, o4):
        o1[...], o2[...], o3[...], o4[...] = _disc(ar[...], ai[...], ld[...])

    sh = jax.ShapeDtypeStruct(a_re.shape, F32)
    return pl.pallas_call(body, name="s5_disc_fwd", out_shape=(sh, sh, sh, sh))(a_re, a_im, ldt)


def _s5_disc_bwd(a_re, a_im, ldt, d_abr, d_abi, d_cr, d_ci):
    def body(ar, ai, ld, g1, g2, g3, g4, o1, o2, o3):
        _, vjp = jax.vjp(_disc, ar[...], ai[...], ld[...])
        o1[...], o2[...], o3[...] = vjp((g1[...], g2[...], g3[...], g4[...]))

    sh = jax.ShapeDtypeStruct(a_re.shape, F32)
    return pl.pallas_call(body, name="s5_disc_bwd", out_shape=(sh, sh, jax.ShapeDtypeStruct(ldt.shape, F32)))(
        a_re, a_im, ldt, d_abr, d_abi, d_cr, d_ci)


def _bbar(cr, ci, br, bi):
    return cr * br - ci * bi, cr * bi + ci * br


def _s5_bbar_fwd(cr_col, ci_col, b_re, b_im):
    def body(cr, ci, br, bi, o1, o2):
        o1[...], o2[...] = _bbar(cr[...], ci[...], br[...], bi[...])

    w = b_re.shape[1]
    return _rowcall(body, "s5_bbar_fwd", b_re.shape[0], [(cr_col, "row"), (ci_col, "row"), (b_re, "row"), (b_im, "row")],
                    [(w, F32, "row"), (w, F32, "row")], tile_rows=1024)


def _s5_bbar_bwd(cr_col, ci_col, b_re, b_im, d_re, d_im):
    def body(cr, ci, br, bi, g1, g2, o1, o2, o3, o4):
        _, vjp = jax.vjp(_bbar, cr[...], ci[...], br[...], bi[...])
        o1[...], o2[...], o3[...], o4[...] = vjp((g1[...], g2[...]))

    w = b_re.shape[1]
    return _rowcall(body, "s5_bbar_bwd", b_re.shape[0],
                    [(cr_col, "row"), (ci_col, "row"), (b_re, "row"), (b_im, "row"), (d_re, "row"), (d_im, "row")],
                    [(1, F32, "row"), (1, F32, "row"), (w, F32, "row"), (w, F32, "row")], tile_rows=1024)


def _block_diag(t):
    g, a, b = t.shape
    nb = g // GROUPS_PER_BLOCK
    t4 = t.reshape(nb, GROUPS_PER_BLOCK, a, b).transpose(0, 1, 3, 2)
    eye = jnp.eye(GROUPS_PER_BLOCK, dtype=t.dtype)
    return (t4[:, :, :, None, :] * eye[None, :, None, :, None]).reshape(nb, GROUPS_PER_BLOCK * b, GROUPS_PER_BLOCK * a)


def _block_diag_extract(d, a, b):
    nb = d.shape[0]
    d5 = d.reshape(nb, GROUPS_PER_BLOCK, b, GROUPS_PER_BLOCK, a)
    diag = jnp.stack([d5[:, g, :, g, :] for g in range(GROUPS_PER_BLOCK)], axis=1)
    return diag.transpose(0, 1, 3, 2).reshape(nb * GROUPS_PER_BLOCK, a, b)


def _scan_step(ar, ai, hr, hi, xr, xi):
    return ar * hr - ai * hi + xr, ar * hi + ai * hr + xi


def _s5_blocks_per_step(nb, full):
    want = 2 if full else 4
    while nb % want:
        want //= 2
    return want


def _s5_scan_fwd(u, bd_re, bd_im, cd_re, cd_im, ab_re, ab_im, init_re, init_im, d_row, full, name):
    s, w = u.shape
    nb = w // LANES
    rows = _tile(s, 512, SUBLANES)
    nc = s // rows
    steps = rows // N_SEG
    ns = nb * BLOCK_STATE

    nblk = _s5_blocks_per_step(nb, full)

    def body(u_ref, bdr, bdi, cdr, cdi, ar_ref, ai_ref, ir_ref, ii_ref, d_ref, *outs):
        if full:
            y_ref, yg_ref, hr_ref, hi_ref, er_ref, ei_ref, cr, ci = outs
        else:
            er_ref, ei_ref, hr_ref, hi_ref, cr, ci = outs
        c = pl.program_id(1)
        cols = lambda b, width: slice(b * width, (b + 1) * width)

        @pl.when(c == 0)
        def _():
            cr[...] = ir_ref[...]
            ci[...] = ii_ref[...]

        for b in range(nblk):
            ub = u_ref[:, cols(b, LANES)].astype(BF16)
            hr_ref[:, cols(b, BLOCK_STATE)] = jnp.dot(ub, bdr[b], preferred_element_type=F32)
            hi_ref[:, cols(b, BLOCK_STATE)] = jnp.dot(ub, bdi[b], preferred_element_type=F32)
        ar, ai = ar_ref[...], ai_ref[...]

        hr, hi = cr[...], ci[...]
        for j in range(steps):
            rows_j = pl.ds(j * N_SEG, N_SEG)
            hr, hi = _scan_step(ar, ai, hr, hi, hr_ref[rows_j, :], hi_ref[rows_j, :])
            hr_ref[rows_j, :] = hr
            hi_ref[rows_j, :] = hi
        cr[...] = hr
        ci[...] = hi
        if full:
            for b in range(nblk):
                st_b, ln_b = cols(b, BLOCK_STATE), cols(b, LANES)
                y = (jnp.dot(hr_ref[:, st_b].astype(BF16), cdr[b], preferred_element_type=F32)
                     + jnp.dot(hi_ref[:, st_b].astype(BF16), cdi[b], preferred_element_type=F32)
                     + d_ref[:, ln_b] * u_ref[:, ln_b])
                y_ref[:, ln_b] = y
                yg_ref[:, ln_b] = jax.nn.gelu(y).astype(BF16)

        @pl.when(c == nc - 1)
        def _():
            er_ref[...] = hr
            ei_ref[...] = hi

    lanes, states = LANES * nblk, BLOCK_STATE * nblk
    blk3 = lambda a: pl.BlockSpec((nblk,) + a.shape[1:], lambda k, c: (k, 0, 0))
    seg = pl.BlockSpec((N_SEG, states), lambda k, c: (0, k))
    st = pl.BlockSpec((rows, states), lambda k, c: (c, k))
    in_specs = [pl.BlockSpec((rows, lanes), lambda k, c: (c, k)), blk3(bd_re), blk3(bd_im), blk3(cd_re), blk3(cd_im),
                seg, seg, seg, seg, pl.BlockSpec((1, lanes), lambda k, c: (0, k))]
    seg_shape = jax.ShapeDtypeStruct((N_SEG, ns), F32)
    st_shape = jax.ShapeDtypeStruct((s, ns), F32)
    carry = [pltpu.VMEM((N_SEG, states), F32)] * 2
    if full:
        ych = pl.BlockSpec((rows, lanes), lambda k, c: (c, k))
        out_specs = [ych, ych, st, st, seg, seg]
        out_shape = [jax.ShapeDtypeStruct((s, w), F32), jax.ShapeDtypeStruct((s, w), BF16), st_shape, st_shape, seg_shape, seg_shape]
        scratch = carry
    else:
        out_specs = [seg, seg]
        out_shape = [seg_shape, seg_shape]
        scratch = [pltpu.VMEM((rows, states), F32)] * 2 + carry
    return pl.pallas_call(
        body, name=name, grid=(nb // nblk, nc), in_specs=in_specs, out_specs=out_specs, out_shape=out_shape,
        scratch_shapes=scratch, compiler_params=_cparams("parallel", "arbitrary"),
    )(u, bd_re, bd_im, cd_re, cd_im, ab_re, ab_im, init_re, init_im, d_row)


def _s5_seg_fix(e_re, e_im, ab_re, ab_im, seg_len, reverse, name):
    assert seg_len & (seg_len - 1) == 0

    def body(er, ei, ar, ai, o_re, o_im):
        pr, pi = ar[0:1, :], ai[0:1, :]
        for _ in range(int(math.log2(seg_len))):
            pr, pi = pr * pr - pi * pi, 2.0 * pr * pi
        tr = jnp.zeros_like(pr)
        ti = jnp.zeros_like(pr)
        order = list(range(N_SEG - 1, -1, -1)) if reverse else list(range(N_SEG))
        for n, sgm in enumerate(order):
            o_re[sgm:sgm + 1, :] = tr
            o_im[sgm:sgm + 1, :] = ti
            if n < N_SEG - 1:
                tr, ti = _scan_step(pr, pi, tr, ti, er[sgm:sgm + 1, :], ei[sgm:sgm + 1, :])

    sh = jax.ShapeDtypeStruct(e_re.shape, F32)
    return pl.pallas_call(body, name=name, out_shape=(sh, sh))(e_re, e_im, ab_re, ab_im)


def _s5_scan_bwd(dy, u, h_re, h_im, bd_re, bd_im, cd_re, cd_im, ab_re, ab_imn, gin_re, gin_im, d_row, full, name, duz=None):
    s, w = u.shape
    nb = w // LANES
    rows = _tile(s, 512, SUBLANES)
    nc = s // rows
    steps = rows // N_SEG
    ns = nb * BLOCK_STATE

    nblk = _s5_blocks_per_step(nb, full)

    def body(dy_ref, u_ref, hr_ref, hi_ref, bdr, bdi, cdr, cdi, ar_ref, ai_ref, ir_ref, ii_ref, d_ref, *outs):
        if full:
            _, du_ref, dbr_ref, dbi_ref, dcr_ref, dci_ref, dar_ref, dai_ref, dd_ref, gr, gi, accr, acci = outs
        else:
            er_ref, ei_ref, gr, gi = outs
        c = pl.program_id(1)
        cols = lambda b, width: slice(b * width, (b + 1) * width)

        @pl.when(c == 0)
        def _():
            gr[pl.ds(rows, N_SEG), :] = ir_ref[...]
            gi[pl.ds(rows, N_SEG), :] = ii_ref[...]
            if full:
                for r in (dbr_ref, dbi_ref, dcr_ref, dci_ref, dd_ref, accr, acci):
                    r[...] = jnp.zeros_like(r)

        nt = (_DOT_DIMS["nt"], ((), ()))
        tn = (_DOT_DIMS["tn"], ((), ()))
        for b in range(nblk):
            dyb = dy_ref[:, cols(b, LANES)].astype(BF16)
            gr[pl.ds(0, rows), cols(b, BLOCK_STATE)] = lax.dot_general(dyb, cdr[b], nt, preferred_element_type=F32)
            gi[pl.ds(0, rows), cols(b, BLOCK_STATE)] = lax.dot_general(dyb, cdi[b], nt, preferred_element_type=F32)
        ar, ai = ar_ref[...], ai_ref[...]

        g0r, g0i = gr[pl.ds(rows, N_SEG), :], gi[pl.ds(rows, N_SEG), :]
        for j in range(steps - 1, -1, -1):
            rows_j = pl.ds(j * N_SEG, N_SEG)
            g0r, g0i = _scan_step(ar, ai, g0r, g0i, gr[rows_j, :], gi[rows_j, :])
            gr[rows_j, :] = g0r
            gi[rows_j, :] = g0i
        if full:
            for b in range(nblk):
                st_b, ln_b = cols(b, BLOCK_STATE), cols(b, LANES)
                hr, hi = hr_ref[:, st_b], hi_ref[:, st_b]
                gnr, gni = gr[pl.ds(N_SEG, rows), st_b], gi[pl.ds(N_SEG, rows), st_b]
                accr[:, st_b] += jnp.sum((gnr * hr + gni * hi).reshape(steps, N_SEG, BLOCK_STATE), axis=0)
                acci[:, st_b] += jnp.sum((gni * hr - gnr * hi).reshape(steps, N_SEG, BLOCK_STATE), axis=0)
                dyb = dy_ref[:, ln_b].astype(BF16)
                ub = u_ref[:, ln_b].astype(BF16)
                gbr, gbi = gr[pl.ds(0, rows), st_b].astype(BF16), gi[pl.ds(0, rows), st_b].astype(BF16)
                dcr_ref[b] += lax.dot_general(hr.astype(BF16), dyb, tn, preferred_element_type=F32)
                dci_ref[b] += lax.dot_general(hi.astype(BF16), dyb, tn, preferred_element_type=F32)
                dbr_ref[b] += lax.dot_general(ub, gbr, tn, preferred_element_type=F32)
                dbi_ref[b] += lax.dot_general(ub, gbi, tn, preferred_element_type=F32)
                du_ref[:, ln_b] = (lax.dot_general(gbr, bdr[b], nt, preferred_element_type=F32)
                                   + lax.dot_general(gbi, bdi[b], nt, preferred_element_type=F32)
                                   + d_ref[:, ln_b] * dy_ref[:, ln_b]).astype(BF16)
                dd_ref[:, ln_b] += jnp.sum(dy_ref[:, ln_b] * u_ref[:, ln_b], axis=0, keepdims=True)
        gr[pl.ds(rows, N_SEG), :] = g0r
        gi[pl.ds(rows, N_SEG), :] = g0i

        @pl.when(c == nc - 1)
        def _():
            if full:
                dar_ref[...] = jnp.sum(accr[...], axis=0, keepdims=True)
                dai_ref[...] = jnp.sum(acci[...], axis=0, keepdims=True)
            else:
                er_ref[...] = g0r
                ei_ref[...] = g0i

    lanes, states = LANES * nblk, BLOCK_STATE * nblk
    rev = lambda k, c: (nc - 1 - c, k)
    blk3 = lambda a: pl.BlockSpec((nblk,) + a.shape[1:], lambda k, c: (k, 0, 0))
    seg = pl.BlockSpec((N_SEG, states), lambda k, c: (0, k))
    st = pl.BlockSpec((rows, states), rev)
    ch = pl.BlockSpec((rows, lanes), rev)
    vec = pl.BlockSpec((1, lanes), lambda k, c: (0, k))
    if not full:
        st = pl.BlockSpec((rows, states), lambda k, c: (0, k))
    in_specs = [ch, ch if full else pl.BlockSpec((rows, lanes), lambda k, c: (0, k)), st, st,
                blk3(bd_re), blk3(bd_im), blk3(cd_re), blk3(cd_im), seg, seg, seg, seg, vec]
    args = [dy, u, h_re, h_im, bd_re, bd_im, cd_re, cd_im, ab_re, ab_imn, gin_re, gin_im, d_row]
    gbuf = [pltpu.VMEM((rows + N_SEG, states), F32)] * 2
    if full:
        row1 = pl.BlockSpec((1, states), lambda k, c: (0, k))
        out_specs = [ch, blk3(bd_re), blk3(bd_im), blk3(cd_re), blk3(cd_im), row1, row1, vec]
        out_shape = [jax.ShapeDtypeStruct(duz.shape, BF16),
                     jax.ShapeDtypeStruct(bd_re.shape, F32), jax.ShapeDtypeStruct(bd_im.shape, F32),
                     jax.ShapeDtypeStruct(cd_re.shape, F32), jax.ShapeDtypeStruct(cd_im.shape, F32),
                     jax.ShapeDtypeStruct((1, ns), F32), jax.ShapeDtypeStruct((1, ns), F32),
                     jax.ShapeDtypeStruct((1, w), F32)]
        scratch = gbuf + [pltpu.VMEM((N_SEG, states), F32)] * 2
        in_specs.append(pl.BlockSpec(memory_space=pl.ANY))
        args.append(duz)
        aliases = {len(args) - 1: 0}
    else:
        out_specs = [seg, seg]
        out_shape = [jax.ShapeDtypeStruct((N_SEG, ns), F32)] * 2
        scratch = gbuf
        aliases = {}
    return pl.pallas_call(
        body, name=name, grid=(nb // nblk, nc), in_specs=in_specs, out_specs=out_specs, out_shape=out_shape,
        input_output_aliases=aliases, scratch_shapes=scratch, compiler_params=_cparams("parallel", "arbitrary"),
    )(*args)


def _log_sigmoid(x):
    return jnp.minimum(x, 0.0) - jnp.log(1.0 + jnp.exp(-jnp.abs(x)))


def _tri(n, upper):
    r = lax.broadcasted_iota(jnp.int32, (n, n), 0)
    c = lax.broadcasted_iota(jnp.int32, (n, n), 1)
    return jnp.where((c >= r) if upper else (r >= c), 1.0, 0.0).astype(F32)


def _cum_fwd(f_logit, b_row, name):
    s, w = f_logit.shape
    t = _tile(s, 256, SUBLANES)

    def body(f_ref, b_ref, o_ref, carry):
        @pl.when(pl.program_id(0) == 0)
        def _():
            carry[...] = jnp.zeros_like(carry)

        lf = _log_sigmoid(f_ref[...] + b_ref[...])
        cum = jnp.dot(_tri(t, False), lf, precision=lax.Precision.HIGHEST, preferred_element_type=F32) + carry[...]
        o_ref[...] = cum * LOG2E
        carry[...] = cum[t - 1:t, :]

    return pl.pallas_call(
        body, name=name, grid=(s // t,),
        in_specs=[pl.BlockSpec((t, w), lambda i: (i, 0)), pl.BlockSpec((1, w), lambda i: (0, 0))],
        out_specs=pl.BlockSpec((t, w), lambda i: (i, 0)), out_shape=jax.ShapeDtypeStruct((s, w), F32),
        scratch_shapes=[pltpu.VMEM((1, w), F32)], compiler_params=_cparams("arbitrary"),
    )(f_logit, b_row)


def _cum_bwd(dcq, dck, f_logit, b_row, name):
    s, w = f_logit.shape
    t = _tile(s, 256, SUBLANES)
    nt = s // t

    def body(q_ref, k_ref, f_ref, b_ref, df_ref, db_ref, carry):
        @pl.when(pl.program_id(0) == 0)
        def _():
            carry[...] = jnp.zeros_like(carry)
            db_ref[...] = jnp.zeros_like(db_ref)

        dc = q_ref[...] - k_ref[...]
        rc = jnp.dot(_tri(t, True), dc, precision=lax.Precision.HIGHEST, preferred_element_type=F32) + carry[...]
        carry[...] = rc[0:1, :]
        df = rc * (1.0 - jax.nn.sigmoid(f_ref[...] + b_ref[...]))
        df_ref[...] = df.astype(BF16)
        db_ref[...] += jnp.sum(df, axis=0, keepdims=True)

    rev = pl.BlockSpec((t, w), lambda i: (nt - 1 - i, 0))
    one = pl.BlockSpec((1, w), lambda i: (0, 0))
    return pl.pallas_call(
        body, name=name, grid=(nt,), in_specs=[rev, rev, rev, one], out_specs=[rev, one],
        out_shape=[jax.ShapeDtypeStruct((s, w), BF16), jax.ShapeDtypeStruct((1, w), F32)],
        scratch_shapes=[pltpu.VMEM((1, w), F32)], compiler_params=_cparams("arbitrary"),
    )(dcq, dck, f_logit, b_row)


def _head_col(cum_tile, h):
    lane = lax.broadcasted_iota(jnp.int32, cum_tile.shape, 1)
    return jnp.sum(jnp.where(lane == h, cum_tile, 0.0), axis=1, keepdims=True)


def _attn_tiles(s):
    return _tile(s, 512, LANES)


def _exp2_rows(sc, sub):
    return jnp.concatenate([jnp.exp2(sc[:, b * LANES:(b + 1) * LANES] - sub) for b in range(sc.shape[1] // LANES)], axis=1)


def _row_of(rep):
    return jnp.transpose(rep)[0:1, :]


def _causal(sc, keys_on_rows):
    r = lax.broadcasted_iota(jnp.int32, sc.shape, 0)
    c = lax.broadcasted_iota(jnp.int32, sc.shape, 1)
    return jnp.where((r <= c) if keys_on_rows else (c <= r), sc, NEG_INF)


def _fox_fwd(q2, kv, cum2_t, z, name):
    s, w = q2.shape
    nh = w // HEAD_DIM
    tq = _attn_tiles(s)
    nq = s // tq
    nt = (_DOT_DIMS["nt"], ((), ()))

    def body(q_ref, k_ref, v_ref, ct_ref, z_ref, o_ref, oz_ref, lse_row_ref, m_s, acc_s, vaug, s_buf):
        i = pl.program_id(1)

        @pl.when(i == 0)
        def _():
            vaug[:, :HEAD_DIM] = v_ref[...]
            vaug[:, HEAD_DIM:] = jnp.ones((s, LANES), BF16)

        qb = q_ref[...]
        m_s[...] = jnp.full_like(m_s, NEG_INF)
        acc_s[...] = jnp.zeros_like(acc_s)

        def scores(j):
            off = pl.multiple_of(j * tq, tq)
            return lax.dot_general(qb, k_ref[pl.ds(off, tq), :], nt, preferred_element_type=F32) - ct_ref[:, pl.ds(off, tq)]

        def softmax_pv(j, sc):
            m_old = m_s[...]
            m_new = jnp.maximum(m_old, jnp.max(sc, axis=1, keepdims=True))
            p = _exp2_rows(sc, m_new)
            alpha = jnp.exp2(m_old - m_new)
            pv = jnp.dot(p.astype(BF16), vaug[pl.ds(pl.multiple_of(j * tq, tq), tq), :], preferred_element_type=F32)
            acc_s[...] = jnp.concatenate([alpha, alpha], axis=1) * acc_s[...] + pv
            m_s[...] = m_new

        s_buf[...] = scores(0)

        def loop(j, carry):
            nxt = scores(j + 1)
            softmax_pv(j, s_buf[...])
            s_buf[...] = nxt
            return carry

        lax.fori_loop(0, i, loop, 0)
        softmax_pv(i, _causal(s_buf[...], False))
        l = acc_s[:, HEAD_DIM:]
        o = acc_s[:, :HEAD_DIM] / l
        o_ref[...] = o
        oz_ref[...] = (o * _silu(z_ref[...].astype(F32))).astype(BF16)
        lse_row_ref[...] = _row_of(m_s[...] + jnp.log(l) * LOG2E)

    return pl.pallas_call(
        body, name=name, grid=(nh, nq),
        in_specs=[pl.BlockSpec((tq, HEAD_DIM), lambda h, i: (i, h)),
                  pl.BlockSpec((s, HEAD_DIM), lambda h, i: (0, h)),
                  pl.BlockSpec((s, HEAD_DIM), lambda h, i: (0, nh + h)),
                  pl.BlockSpec((None, 1, s), lambda h, i: (h, 0, 0)),
                  pl.BlockSpec((tq, HEAD_DIM), lambda h, i: (i, h))],
        out_specs=[pl.BlockSpec((tq, HEAD_DIM), lambda h, i: (i, h)),
                   pl.BlockSpec((tq, HEAD_DIM), lambda h, i: (i, h)),
                   pl.BlockSpec((None, 1, tq), lambda h, i: (h, 0, i))],
        out_shape=[jax.ShapeDtypeStruct((s, w), F32), jax.ShapeDtypeStruct((s, w), BF16),
                   jax.ShapeDtypeStruct((nh, 1, s), F32)],
        scratch_shapes=[pltpu.VMEM((tq, LANES), F32), pltpu.VMEM((tq, HEAD_DIM + LANES), F32),
                        pltpu.VMEM((s, HEAD_DIM + LANES), BF16), pltpu.VMEM((tq, tq), F32)],
        compiler_params=_cparams("arbitrary", "arbitrary"),
    )(q2, kv, kv, cum2_t, z)


def _fox_bwd(q2, kv, do, o, lse2_t, cum2, dqz, name):
    s, w = q2.shape
    nh = w // HEAD_DIM
    tk = _attn_tiles(s)
    nk = s // tk
    scale = HEAD_DIM ** -0.5
    nt = (_DOT_DIMS["nt"], ((), ()))
    tn = (_DOT_DIMS["tn"], ((), ()))

    def body(q_ref, k_ref, v_ref, do_ref, o_ref, lse_ref, c_ref, _, dk_ref, dv_ref, dq_ref, dcq_ref, dck_ref,
             dk_s, dv_s, dc_s, dq_s, dcq_s, dl_s, s_buf, dp_buf):
        h, j = pl.program_id(0), pl.program_id(1)

        @pl.when(j == 0)
        def _():
            dq_s[...] = jnp.zeros_like(dq_s)
            dcq_s[...] = jnp.zeros_like(dcq_s)
            for i in range(nk):
                rows = pl.ds(i * tk, tk)
                d = jnp.sum(do_ref[rows, :].astype(F32) * o_ref[rows, :], axis=1, keepdims=True)
                dl_s[:, i * tk:(i + 1) * tk] = _row_of(jnp.broadcast_to(d, (tk, LANES)))

        kb = k_ref[...]
        vb = v_ref[...]
        ck = jnp.broadcast_to(_head_col(c_ref[...], h), (tk, LANES))
        dk_s[...] = jnp.zeros_like(dk_s)
        dv_s[...] = jnp.zeros_like(dv_s)
        dc_s[...] = jnp.zeros_like(dc_s)

        def scores(i):
            off = pl.multiple_of(i * tk, tk)
            sc = lax.dot_general(kb, q_ref[pl.ds(off, tk), :], nt, preferred_element_type=F32) - lse_ref[:, pl.ds(off, tk)]
            dp = lax.dot_general(vb, do_ref[pl.ds(off, tk), :], nt, preferred_element_type=F32) - dl_s[:, pl.ds(off, tk)]
            return sc, dp

        def accumulate(i, sc, dp):
            off = pl.multiple_of(i * tk, tk)
            p = _exp2_rows(sc, ck)
            dv_s[...] += jnp.dot(p.astype(BF16), do_ref[pl.ds(off, tk), :], preferred_element_type=F32)
            ds = p * dp
            dsb = ds.astype(BF16)
            dk_s[...] += jnp.dot(dsb, q_ref[pl.ds(off, tk), :], preferred_element_type=F32)
            dq_s[pl.ds(off, tk), :] += lax.dot_general(dsb, kb, tn, preferred_element_type=F32)
            dcq_s[:, pl.ds(off, tk)] += jnp.sum(ds, axis=0, keepdims=True)
            part = ds[:, :LANES]
            for b in range(1, tk // LANES):
                part = part + ds[:, b * LANES:(b + 1) * LANES]
            dc_s[...] += part

        sc0, dp0 = scores(j)
        s_buf[...] = _causal(sc0, True)
        dp_buf[...] = dp0

        def loop(i, carry):
            nxt = scores(i + 1)
            accumulate(i, s_buf[...], dp_buf[...])
            s_buf[...], dp_buf[...] = nxt
            return carry

        lax.fori_loop(j, nk - 1, loop, 0)
        accumulate(nk - 1, s_buf[...], dp_buf[...])
        dk_ref[...] = (dk_s[...] * (1.0 / LOG2E)).astype(BF16)
        dv_ref[...] = dv_s[...].astype(BF16)
        dck_ref[...] = jnp.sum(jnp.transpose(dc_s[...]), axis=0, keepdims=True)

        @pl.when(j == nk - 1)
        def _():
            dq_ref[...] = (dq_s[...] * scale).astype(BF16)
            dcq_ref[...] = dcq_s[...]

    col = pl.BlockSpec((s, HEAD_DIM), lambda h, j: (0, h))
    row = pl.BlockSpec((None, 1, s), lambda h, j: (h, 0, 0))
    kspec = pl.BlockSpec((tk, HEAD_DIM), lambda h, j: (j, h))
    return pl.pallas_call(
        body, name=name, grid=(nh, nk),
        in_specs=[col, kspec, pl.BlockSpec((tk, HEAD_DIM), lambda h, j: (j, nh + h)), col, col, row,
                  pl.BlockSpec((tk, LANES), lambda h, j: (j, 0)), pl.BlockSpec(memory_space=pl.ANY)],
        out_specs=[kspec, kspec, col, row, pl.BlockSpec((None, 1, tk), lambda h, j: (h, 0, j))],
        out_shape=[jax.ShapeDtypeStruct((s, w), BF16), jax.ShapeDtypeStruct((s, w), BF16),
                   jax.ShapeDtypeStruct(dqz.shape, BF16), jax.ShapeDtypeStruct((nh, 1, s), F32),
                   jax.ShapeDtypeStruct((nh, 1, s), F32)],
        input_output_aliases={7: 2},
        scratch_shapes=[pltpu.VMEM((tk, HEAD_DIM), F32), pltpu.VMEM((tk, HEAD_DIM), F32), pltpu.VMEM((tk, LANES), F32),
                        pltpu.VMEM((s, HEAD_DIM), F32), pltpu.VMEM((1, s), F32), pltpu.VMEM((1, s), F32),
                        pltpu.VMEM((tk, tk), F32), pltpu.VMEM((tk, tk), F32)],
        compiler_params=_cparams("arbitrary", "arbitrary"),
    )(q2, kv, kv, do, o, lse2_t, cum2, dqz)


_ALL_PEERS = tuple(range(1, N_DEV))
_CHIP_PEERS = (1, 2, 4, 6)


def _exchange_copies(ins, outs, send_sems, recv_sems, local_sems, scatter, peers=_ALL_PEERS):
    x, y, c = (lax.axis_index(a) for a in MESH_AXES)
    me = 4 * x + 2 * y + c
    local, remote = [], []
    for a in range(len(ins)):
        local.append(pltpu.make_async_copy(ins[a].at[me] if scatter else ins[a], outs[a].at[me], local_sems.at[a]))
        for k in peers:
            px, py, pc = (1 - x if k & 4 else x), (1 - y if k & 2 else y), (1 - c if k & 1 else c)
            remote.append(pltpu.make_async_remote_copy(
                src_ref=ins[a].at[4 * px + 2 * py + pc] if scatter else ins[a], dst_ref=outs[a].at[me],
                send_sem=send_sems.at[a * (N_DEV - 1) + k - 1], recv_sem=recv_sems.at[a * (N_DEV - 1) + k - 1],
                device_id=(px, py, pc), device_id_type=pl.DeviceIdType.MESH))
    return local, remote


def _exchange_out_shapes(arrs, scatter):
    return [((N_DEV,) + a.shape[1:]) if scatter else ((N_DEV,) + a.shape) for a in arrs]


_HBM =pl.BlockSpec(memory_space=pltpu.HBM)
_SEM = pl.BlockSpec(memory_space=pltpu.SEMAPHORE)


def _exchange_start(arrs, scatter, name, after=(), peers=_ALL_PEERS):
    n = len(arrs)
    after = list(after)
    lands = [lax.empty(s, a.dtype) for s, a in zip(_exchange_out_shapes(arrs, scatter), arrs)]

    def body(*refs):
        ins, outs = refs[:n], refs[n:2 * n]
        send_sems, recv_sems, local_sems = refs[2 * n + len(after):2 * n + len(after) + 3]
        token = refs[-1]
        local, remote = _exchange_copies(ins, outs, send_sems, recv_sems, local_sems, scatter, peers)
        for cp in local + remote:
            cp.start()
        token[...] = jnp.zeros_like(token)

    hbm = lambda a: pltpu.HBM(a.shape, a.dtype)
    res = pl.pallas_call(
        body, name=name,
        out_shape=(pltpu.SemaphoreType.DMA((n * (N_DEV - 1),)), pltpu.SemaphoreType.DMA((n * (N_DEV - 1),)),
                   pltpu.SemaphoreType.DMA((n,)), *[hbm(a) for a in arrs], *[hbm(a) for a in lands],
                   jax.ShapeDtypeStruct((SUBLANES, LANES), F32)),
        in_specs=[_HBM] * (2 * n) + [pl.BlockSpec(memory_space=pl.ANY)] * len(after),
        out_specs=(_SEM, _SEM, _SEM, *[_HBM] * (2 * n), pl.BlockSpec(memory_space=pltpu.VMEM)),
        input_output_aliases={i: 3 + i for i in range(2 * n)},
        compiler_params=pltpu.CompilerParams(has_side_effects=pltpu.SideEffectType.DATAFLOW_SIDE_EFFECTING),
    )(*[pltpu.with_memory_space_constraint(a, pltpu.HBM) for a in list(arrs) + lands], *after)
    return (n, scatter, res[:3], res[3:3 + n], res[3 + n:3 + 2 * n], peers), res[-1]


def _exchange_wait(state, after, name):
    n, scatter, sems, srcs, lands, peers = state
    after = list(after) if isinstance(after, (list, tuple)) else [after]

    def body(*refs):
        ins, outs = refs[:n], refs[n:2 * n]
        send_sems, recv_sems, local_sems = refs[2 * n:2 * n + 3]
        local, remote = _exchange_copies(ins, outs, send_sems, recv_sems, local_sems, scatter, peers)
        for cp in remote:
            cp.wait_send()
            cp.wait_recv()
        for cp in local:
            cp.wait()

    hbm = lambda a: pltpu.HBM(a.shape, a.dtype)
    res = pl.pallas_call(
        body, name=name,
        out_shape=(*[hbm(a) for a in srcs], *[hbm(a) for a in lands]),
        in_specs=[_HBM] * (2 * n) + [_SEM] * 3 + [pl.BlockSpec(memory_space=pl.ANY)] * len(after),
        out_specs=tuple([_HBM] * (2 * n)),
        input_output_aliases={i: i for i in range(2 * n)},
        compiler_params=pltpu.CompilerParams(has_side_effects=pltpu.SideEffectType.DATAFLOW_SIDE_EFFECTING),
    )(*srcs, *lands, *sems, *after)
    return list(res[n:])


def _forward_to_sibling(slots, name):
    n = len(slots)
    hops = (2, 4, 6)

    def body(*refs):
        ins, outs, (send_sems, recv_sems) = refs[:n], refs[n:2 * n], refs[2 * n:]
        x, y, c = (lax.axis_index(a) for a in MESH_AXES)
        copies = []
        for a in range(n):
            for i, k in enumerate(hops):
                slot = 4 * (1 - x if k & 4 else x) + 2 * (1 - y if k & 2 else y) + c
                copies.append(pltpu.make_async_remote_copy(
                    src_ref=ins[a].at[slot], dst_ref=outs[a].at[slot],
                    send_sem=send_sems.at[a * len(hops) + i], recv_sem=recv_sems.at[a * len(hops) + i],
                    device_id=(x, y, 1 - c), device_id_type=pl.DeviceIdType.MESH))
        for cp in copies:
            cp.start()
        for cp in copies:
            cp.wait_send()
            cp.wait_recv()

    return pl.pallas_call(
        body, name=name, out_shape=[jax.ShapeDtypeStruct(s.shape, s.dtype) for s in slots],
        in_specs=[pl.BlockSpec(memory_space=pl.ANY)] * n, out_specs=[pl.BlockSpec(memory_space=pl.ANY)] * n,
        input_output_aliases={i: i for i in range(n)},
        scratch_shapes=[pltpu.SemaphoreType.DMA((n * len(hops),)), pltpu.SemaphoreType.DMA((n * len(hops),))],
    )(*slots)


def _adamw_math(w, g, m, v):
    m = ADAM_B1 * m + (1.0 - ADAM_B1) * g
    v = ADAM_B2 * v + (1.0 - ADAM_B2) * (g * g)
    m_hat = m / (1.0 - ADAM_B1 ** ADAM_STEP)
    v_hat = v / (1.0 - ADAM_B2 ** ADAM_STEP)
    return -ADAM_LR * (m_hat / (jnp.sqrt(v_hat) + ADAM_EPS) + ADAM_WD * w), m, v


def _slot_sum(p_ref):
    g = p_ref[0].astype(F32)
    for d in range(1, p_ref.shape[0]):
        g = g + p_ref[d].astype(F32)
    return g


def _adamw_tile(r, c):
    return _tile(r, max(SUBLANES, (256 * 1024) // c // SUBLANES * SUBLANES), SUBLANES)


def _adamw(parts, w, m, v, name):
    r, c = w.shape[-2:]
    by_cols = r % SUBLANES != 0
    tr, tc = (r, _tile(c, 256)) if by_cols else (_adamw_tile(r, c), c)

    def body(p_ref, w_ref, m_ref, v_ref, g_ref, d_ref, nm_ref, nv_ref):
        g = _slot_sum(p_ref)
        g_ref[...] = g
        d_ref[...], nm_ref[...], nv_ref[...] = _adamw_math(w_ref[...], g, m_ref[...], v_ref[...])

    pos = (lambda i: (0, i)) if by_cols else (lambda i: (i, 0))
    if w.ndim == 3:
        blk = pl.BlockSpec((None, tr, tc), lambda i: (0,) + pos(i))
    else:
        blk = pl.BlockSpec((tr, tc), pos)
    sh = jax.ShapeDtypeStruct(w.shape, F32)
    return pl.pallas_call(
        body, name=name, grid=(c // tc if by_cols else r // tr,),
        in_specs=[pl.BlockSpec((parts.shape[0], tr, tc), lambda i: (0,) + pos(i)), blk, blk, blk],
        out_specs=[blk] * 4, out_shape=[sh] * 4, compiler_params=_cparams("parallel"),
    )(parts, w, m, v)


def _sum_parts(parts, name):
    _, r, c = parts.shape
    tr = _adamw_tile(r, c)

    def body(p_ref, o_ref):
        o_ref[...] = _slot_sum(p_ref)

    return pl.pallas_call(
        body, name=name, grid=(r // tr,),
        in_specs=[pl.BlockSpec((parts.shape[0], tr, c), lambda i: (0, i, 0))],
        out_specs=pl.BlockSpec((tr, c), lambda i: (i, 0)), out_shape=jax.ShapeDtypeStruct((r, c), F32),
        compiler_params=_cparams("parallel"),
    )(parts)


def _lane_pad(a, width=LANES):
    return jnp.pad(a, ((0, 0), (0, width - a.shape[1])))


def _local_step(x, target, norm_pre, norm_post, kv_norm, kv_b_f, a_re, a_im, log_dt, b_re, b_im, c_re, c_im, comm):
    s, d = x.shape
    g, p = a_re.shape
    w = g * S5_GROUP
    fw = d
    nh = fw // HEAD_DIM
    seg_len = s // N_SEG
    row = lambda v: v.reshape(1, -1)
    g_pre0, g_pre1, g_post0, g_post1, g_kv = row(norm_pre[0]), row(norm_pre[1]), row(norm_post[0]), row(norm_post[1]), row(kv_norm)

    ldt = log_dt.reshape(g, 1)
    abr, abi, cr, ci = _s5_disc_fwd(a_re, a_im, ldt)
    cr_col, ci_col = cr.reshape(g * p, 1), ci.reshape(g * p, 1)
    b_re2, b_im2 = b_re.reshape(g * p, S5_GROUP), b_im.reshape(g * p, S5_GROUP)
    bb_re, bb_im = _s5_bbar_fwd(cr_col, ci_col, b_re2, b_im2)
    bd_re = _block_diag(bb_re.reshape(g, p, S5_GROUP)).astype(BF16)
    bd_im = _block_diag(bb_im.reshape(g, p, S5_GROUP)).astype(BF16)
    cd_re = _block_diag(c_re).astype(BF16)
    cd_im = _block_diag(-c_im).astype(BF16)
    ab_re = jnp.broadcast_to(abr.reshape(1, g * p), (N_SEG, g * p))
    ab_im = jnp.broadcast_to(abi.reshape(1, g * p), (N_SEG, g * p))
    zero_seg = jnp.zeros((N_SEG, g * p), F32)

    xn0 = _norm_cast(x, g_pre0 + comm.token, "norm_pre0", x_kind="nat")
    w_in = comm.weight("s5_w_in", [xn0, bd_re, bd_im, cd_re, cd_im, ab_re, ab_im])
    d_row, bglu_row = row(comm.vector("s5_d")), row(comm.vector("s5_b_glu"))
    u = _mm(xn0, w_in, "nn", BF16, "s5_in_u", b_cols=(0, w), b_slots=True)
    z0 = _mm(xn0, w_in, "nn", BF16, "s5_in_z", b_cols=(w, w), b_slots=True)
    e_re, e_im = _s5_scan_fwd(u, bd_re, bd_im, cd_re, cd_im, ab_re, ab_im, zero_seg, zero_seg, d_row, False, "s5_scan_ends")
    i_re, i_im = _s5_seg_fix(e_re, e_im, ab_re, ab_im, seg_len, False, "s5_seg_fix")
    y_ssm, yg, h_re, h_im, _, _ = _s5_scan_fwd(u, bd_re, bd_im, cd_re, cd_im, ab_re, ab_im, i_re, i_im, d_row, True, "s5_scan")
    w_glu, w_out = comm.weight("s5_w_glu", yg), comm.weight("s5_w_out", yg)
    gp = _mm(yg, w_glu, "nn", BF16, "s5_glu")
    y3 = _s5_gate(y_ssm, gp, bglu_row, z0, "s5_gate")
    w_kvt, fw_in = comm.weight("kv_w", y3), comm.weight("fox_w_in", y3)
    w_ft = jnp.pad(w_kvt[2 * fw:], ((0, LANES - nh), (0, 0)))
    o0 = _mm(y3, w_out, "nn", F32, "s5_out")

    h1, hn_kv, xn1 = _resid_norm2(x, o0, g_post0 + comm.late_token, g_kv, g_pre1, "resid_norms")
    kv = _mm(hn_kv, w_kvt, "nt", BF16, "kv_proj", b_rows=2 * fw)
    f_logit = _mm(hn_kv, w_ft, "nt", F32, "f_proj")
    bf_row = _lane_pad(row(kv_b_f))
    cum2 = _cum_fwd(f_logit, bf_row, "cum_fwd")
    cum2_t = cum2[:, :nh].T.reshape(nh, 1, s)
    q2 = _mm(xn1, fw_in, "nn", BF16, "fox_q", scale=HEAD_DIM ** -0.5 * LOG2E, b_cols=(0, fw), b_slots=True)
    z1 = _mm(xn1, fw_in, "nn", BF16, "fox_z", b_cols=(fw, fw), b_slots=True)
    o, oz, lse2_t = _fox_fwd(q2, kv, cum2_t, z1, "fox_fwd")
    fw_out = comm.weight("fox_w_out", oz)
    o1 = _mm(oz, fw_out, "nn", F32, "fox_out")
    dh2, do1, sq, dg_post1 = _post_norm_loss(o1, g_post1, h1, target, "norm_post1_loss")
    loss = 0.5 * jnp.sum(sq) / d

    d_fw_out = _mm(oz, do1, "tn", BF16, "fox_out_dw")
    d_oz = _mm(do1, fw_out, "nt", BF16, "fox_out_dx")
    do, dqz = _gate_bwd(d_oz, o, z1, "fox_gate_bwd")
    dk, dv, dqz, dcq, dck = _fox_bwd(q2, kv, do, o, lse2_t, cum2, dqz, "fox_bwd")
    d_fw_in = _mm(xn1, dqz, "tn", BF16, "fox_in_dw", col_slots=True)
    dxn1 = _mm(dqz, fw_in, "nt", BF16, "fox_in_dx", b_slots=True)
    dcq_sl = _lane_pad(dcq.reshape(nh, s).T)
    dck_sl = _lane_pad(dck.reshape(nh, s).T)
    df, db_f = _cum_bwd(dcq_sl, dck_sl, f_logit, bf_row, "cum_bwd")
    dkv = _concat_cast(dk, dv, "fox_dkv")
    d_w_kvmt = _mm(dkv, hn_kv, "tn", BF16, "kv_dw")
    d_w_ft = _mm(df, hn_kv, "tn", BF16, "f_dw")
    dhn_f = _mm(df, w_ft, "nn", F32, "f_dx")
    dhn_kv = _mm(dkv, w_kvt, "nn", BF16, "kv_dx", add=dhn_f, b_rows=2 * fw)
    d_w_kvt = jnp.concatenate([d_w_kvmt, d_w_ft[:nh]], axis=0)
    tok = comm.send_grads(dict(fox_w_out=d_fw_out, fox_w_in=d_fw_in, kv_w=d_w_kvt), "exchange_fox")
    dh1, do0, dg_pre1, dg_kv, dg_post0 = _norm_bwd2(dh2, h1, dxn1, dhn_kv, g_pre1, g_kv, o0, g_post0 + tok[0, 0],
                                                      "resid_norms_bwd")

    d_w_out = _mm(y3, do0, "tn", BF16, "s5_out_dw")
    dy3 = _mm(do0, w_out, "nt", BF16, "s5_out_dx")
    duz, dgp, dyg_direct, db_glu = _s5_gate_bwd(dy3, y_ssm, gp, bglu_row, z0, "s5_gate_bwd")
    d_w_glu = _mm(yg, dgp, "tn", BF16, "s5_glu_dw")
    gelu_bwd = lambda dyg, y: jax.vjp(jax.nn.gelu, y)[1](dyg)[0]
    dy_ssm = _mm(dgp, w_glu, "nt", F32, "s5_glu_dx", add=dyg_direct, epilogue=(gelu_bwd, y_ssm))
    d_row = d_row + comm.send_grads(dict(s5_w_out=d_w_out, s5_w_glu=d_w_glu), "exchange_s5")[0, 0]
    ab_imn = -ab_im
    ge_re, ge_im = _s5_scan_bwd(dy_ssm, u, h_re, h_im, bd_re, bd_im, cd_re, cd_im, ab_re, ab_imn, zero_seg, zero_seg,
                                d_row, False, "s5_adj_ends")
    gi_re, gi_im = _s5_seg_fix(ge_re, ge_im, ab_re, ab_imn, seg_len, True, "s5_adj_fix")
    duz, dbd_re, dbd_im, dcd_re, dcd_im, dab_re, dab_im, dd = _s5_scan_bwd(
        dy_ssm, u, h_re, h_im, bd_re, bd_im, cd_re, cd_im, ab_re, ab_imn, gi_re, gi_im, d_row, True, "s5_adj", duz=duz)
    d_w_in = _mm(xn0, duz, "tn", BF16, "s5_in_dw", col_slots=True)
    tok = comm.send_grads(dict(s5_w_in=d_w_in), "exchange_s5_in")
    dxn0 = _mm(duz, w_in, "nt", BF16, "s5_in_dx", after=tok, b_slots=True)
    grad_x, dg_pre0 = _norm_bwd1(dh1, x, dxn0, g_pre0, "norm_pre0_bwd")

    dbb_re = _block_diag_extract(dbd_re, p, S5_GROUP).reshape(g * p, S5_GROUP)
    dbb_im = _block_diag_extract(dbd_im, p, S5_GROUP).reshape(g * p, S5_GROUP)
    dcr_col, dci_col, db_re, db_im = _s5_bbar_bwd(cr_col, ci_col, b_re2, b_im2, dbb_re, dbb_im)
    da_re, da_im, dldt = _s5_disc_bwd(a_re, a_im, ldt, dab_re.reshape(g, p), dab_im.reshape(g, p),
                                      dcr_col.reshape(g, p), dci_col.reshape(g, p))
    dc_re = _block_diag_extract(dcd_re, S5_GROUP, p)
    dc_im = -_block_diag_extract(dcd_im, S5_GROUP, p)

    small = dict(
        norm_pre=jnp.concatenate([dg_pre0, dg_pre1], axis=0), norm_post=jnp.concatenate([dg_post0, dg_post1], axis=0),
        s5_a_re=da_re, s5_a_im=da_im, s5_log_dt=dldt.reshape(g), s5_b_re=db_re.reshape(g, p, S5_GROUP),
        s5_b_im=db_im.reshape(g, p, S5_GROUP), s5_c_re=dc_re, s5_c_im=dc_im, s5_d=dd.reshape(-1),
        s5_b_glu=db_glu.reshape(-1), kv_norm=dg_kv.reshape(-1), kv_b_f=db_f[0, :nh])
    return loss, grad_x, small


_BIG = ("s5_w_in", "s5_w_glu", "s5_w_out", "kv_w", "fox_w_in", "fox_w_out")
_COL_SHARDED = ("s5_w_in", "fox_w_in")
_SMALL = ("norm_pre", "norm_post", "s5_a_re", "s5_a_im", "s5_log_dt", "s5_b_re", "s5_b_im", "s5_c_re", "s5_c_im",
          "s5_d", "s5_b_glu", "kv_norm", "kv_b_f")
_SMALL_SHARDED = ("s5_d", "s5_b_glu")
_PACK_QUANTUM = SUBLANES * LANES
_WEIGHTS = ('norm_pre', 'norm_post', 's5_w_in', 's5_a_re', 's5_a_im', 's5_log_dt', 's5_b_re', 's5_b_im', 's5_c_re', 's5_c_im',
            's5_d', 's5_w_glu', 's5_b_glu', 's5_w_out', 'kv_norm', 'kv_w', 'kv_b_f', 'fox_w_in', 'fox_w_out')


def _full_from_slots(name, slots):
    n, r, c = slots.shape
    if name in _COL_SHARDED:
        return slots.transpose(1, 0, 2).reshape(r, n * c)
    return slots.reshape(n * r, c)


def _slots_from_full(name, full):
    if name in _COL_SHARDED:
        r, nc = full.shape
        return full.reshape(r, N_DEV, nc // N_DEV).transpose(1, 0, 2)
    nr, c = full.shape
    return full.reshape(N_DEV, nr // N_DEV, c)


def _groups_last(shape):
    return len(shape) >= 3 and shape[-1] < LANES and shape[-3] % LANES == 0


def _pack(vals):
    parts = []
    for v in vals:
        flat = jnp.moveaxis(v, -3, -1).reshape(-1) if _groups_last(v.shape) else v.reshape(-1)
        parts.append(jnp.pad(flat, (0, (-flat.shape[0]) % _PACK_QUANTUM)))
    total = sum(p.shape[0] for p in parts)
    parts.append(jnp.zeros(((-total) % (N_DEV * _PACK_QUANTUM),), F32))
    return jnp.concatenate(parts).reshape(-1, LANES)


def _unpack(packed, shapes):
    flat = packed.reshape(-1)
    out, off = [], 0
    for sh in shapes:
        n = math.prod(sh)
        piece = flat[off:off + n]
        if _groups_last(sh):
            piece = jnp.moveaxis(piece.reshape(sh[:-3] + sh[-2:] + sh[-3:-2]), -1, -3)
        out.append(piece.reshape(sh))
        off += n + (-n) % _PACK_QUANTUM
    return out


class _Comm:
    _GROUPS = (("s5_w_in",) + _SMALL_SHARDED, ("s5_w_glu", "s5_w_out"), ("kv_w", "fox_w_in"), ("fox_w_out",))
    _SLOT_FORM = ("s5_w_in", "fox_w_in")

    def __init__(self, shards, vectors, early=()):
        self._shards = {**shards, **vectors}
        self._full, self._gathers = {}, {}
        self._early = list(early)
        self.token = jnp.zeros((), F32)
        for group in self._GROUPS[:-1]:
            self.token = self.token + self._start(group, ())[0, 0]
        self.late_token = None
        self._sent = []

    def _start(self, group, after):
        state, tok = _exchange_start([self._shards[n] for n in group], False, "gather_start_" + group[0], after,
                                     peers=_CHIP_PEERS)
        self._gathers[group] = state
        return tok

    def vector(self, name):
        return self._full[name]

    def weight(self, name, after):
        if name not in self._full:
            group = next(g for g in self._GROUPS if name in g)
            if group == self._GROUPS[0]:
                after = (list(after) if isinstance(after, (list, tuple)) else [after]) + self._early
            slots = _exchange_wait(self._gathers.pop(group), after, "gather_wait_" + group[0])
            slots = _forward_to_sibling(slots, "gather_forward_" + group[0])
            for n, sl in zip(group, slots):
                if n in _SMALL_SHARDED:
                    self._full[n] = sl.reshape(-1)
                else:
                    self._full[n] = sl if n in self._SLOT_FORM else _full_from_slots(n, sl)
            if group == self._GROUPS[-2]:
                self.late_token = self._start(self._GROUPS[-1], [slots[0]])[0, 0]
        return self._full[name]

    def send_grads(self, grads, name):
        names = list(grads)
        slots = [grads[n] if grads[n].ndim == 3 else _slots_from_full(n, grads[n]).astype(BF16) for n in names]
        state, tok = _exchange_start(slots, True, name + "_start")
        self._sent.append((names, state, name + "_wait"))
        return tok

    def received_grads(self, group, after):
        names, state, name = self._sent[group]
        return list(zip(names, _exchange_wait(state, after, name)))


def kernel(x, norm_pre, norm_post, s5_w_in, s5_a_re, s5_a_im, s5_log_dt, s5_b_re, s5_b_im, s5_c_re, s5_c_im, s5_d, s5_w_glu, s5_b_glu, s5_w_out, kv_norm, kv_w, kv_b_f, fox_w_in, fox_w_out, loss_target, m_norm_pre, m_norm_post, m_s5_w_in, m_s5_a_re, m_s5_a_im, m_s5_log_dt, m_s5_b_re, m_s5_b_im, m_s5_c_re, m_s5_c_im, m_s5_d, m_s5_w_glu, m_s5_b_glu, m_s5_w_out, m_kv_norm, m_kv_w, m_kv_b_f, m_fox_w_in, m_fox_w_out, v_norm_pre, v_norm_post, v_s5_w_in, v_s5_a_re, v_s5_a_im, v_s5_log_dt, v_s5_b_re, v_s5_b_im, v_s5_c_re, v_s5_c_im, v_s5_d, v_s5_w_glu, v_s5_b_glu, v_s5_w_out, v_kv_norm, v_kv_w, v_kv_b_f, v_fox_w_in, v_fox_w_out):
    env = dict(locals())
    wts = {n: env[n] for n in _WEIGHTS}
    mom = {n: env["m_" + n] for n in _WEIGHTS}
    var = {n: env["v_" + n] for n in _WEIGHTS}
    me = 4 * lax.axis_index("x") + 2 * lax.axis_index("y") + lax.axis_index("c")
    shard2d = {n: (wts[n].T if n == "kv_w" else wts[n].reshape(wts[n].shape[-2:])) for n in _BIG}
    full_shape = {n: ((wts[n].size * N_DEV,) if n in _SMALL_SHARDED else wts[n].shape) for n in _SMALL}

    def spread(n, v):
        if n not in _SMALL_SHARDED:
            return v
        flat = v.reshape(-1)
        return lax.dynamic_update_slice(jnp.zeros(full_shape[n], F32), flat, (me * flat.shape[0],))

    packed = [_pack([spread(n, src[n]) for n in _SMALL] + [jnp.zeros((1,), F32)]) for src in (wts, mom, var)]
    comm = _Comm({n: _cast_bf16(shard2d[n], "cast_" + n) for n in _BIG}, {n: wts[n].reshape(1, -1) for n in _SMALL_SHARDED}, packed)

    loss_local, grad_x, small = _local_step(
        x[0], loss_target[0], norm_pre, norm_post, kv_norm, kv_b_f, s5_a_re[0], s5_a_im[0], s5_log_dt[0],
        s5_b_re[0], s5_b_im[0], s5_c_re[0], s5_c_im[0], comm)

    small_pack = _pack([small[n] for n in _SMALL] + [loss_local.reshape(1)])
    slice_rows = small_pack.shape[0] // N_DEV
    small_state, small_tok = _exchange_start([small_pack.reshape(N_DEV, slice_rows, LANES)], True, "reduce_small_start")

    res = {}

    def finish(group, after):
        for n, recv in comm.received_grads(group, after):
            if n == "kv_w":
                res[n] = [o.T for o in _adamw(recv, wts[n].T, mom[n].T, var[n].T, "adamw_" + n)]
            else:
                res[n] = _adamw(recv, wts[n], mom[n], var[n], "adamw_" + n)

    finish(0, [small_tok, grad_x])
    my_sum = _sum_parts(_exchange_wait(small_state, res["kv_w"][0], "reduce_small_wait")[0], "sum_small")
    gather_state, gather_tok = _exchange_start([my_sum], False, "gather_small_start")
    finish(1, gather_tok)
    finish(2, gather_tok)
    g_all = _exchange_wait(gather_state, res["s5_w_in"][0], "gather_small_wait")[0].reshape(1, small_pack.shape[0], LANES)
    outs = _adamw(g_all, *packed, "adamw_small")
    unpacked = [_unpack(o, [full_shape[n] for n in _SMALL] + [(1,)]) for o in outs]
    loss = unpacked[0][-1][0]
    for i, n in enumerate(_SMALL):
        vals = [u[i] for u in unpacked]
        if n in _SMALL_SHARDED:
            k = wts[n].size
            vals = [lax.dynamic_slice(v, (me * k,), (k,)) for v in vals]
        res[n] = [v.reshape(wts[n].shape) for v in vals]

    return (loss, grad_x[None], *[res[n][0] for n in _WEIGHTS], *[res[n][1] for n in _WEIGHTS],
            *[res[n][2] for n in _WEIGHTS], *[res[n][3] for n in _WEIGHTS])
```

```python
import math

import jax
import jax.numpy as jnp
from jax import lax
from jax.experimental import pallas as pl
from jax.experimental.pallas import tpu as pltpu

F32 = jnp.float32
BF16 = jnp.bfloat16

N_DEV = 8
MESH_AXES = ("x", "y", "c")
S5_GROUP = 16
S5_STATE = 64
LANES = 128
SUBLANES = 8
GROUPS_PER_BLOCK = LANES // S5_GROUP
BLOCK_STATE = GROUPS_PER_BLOCK * S5_STATE
N_SEG = SUBLANES
HEAD_DIM = 128
RMS_EPS = 1e-6
NEG_INF = -1e30
LOG2E = math.log2(math.e)
ADAM_LR = 0.001
ADAM_B1 = 0.9
ADAM_B2 = 0.999
ADAM_EPS = 1e-08
ADAM_WD = 0.01
ADAM_STEP = 10
VMEM_LIMIT = 56 * 1024 * 1024


def _tile(n, pref, quantum=LANES):
    if n <= pref:
        return n
    t = (pref // quantum) * quantum
    while t >= quantum:
        if n % t == 0:
            return t
        t -= quantum
    return n


def _cparams(*sem):
    return pltpu.CompilerParams(dimension_semantics=sem if sem else None, vmem_limit_bytes=VMEM_LIMIT)


_DOT_DIMS = {"nn": ((1,), (0,)), "nt": ((1,), (1,)), "tn": ((0,), (0,))}


def _mm(a, b, mode, out_dtype, name, add=None, scale=None, b_cols=None, after=None, col_slots=False, b_slots=False,
        b_rows=None, epilogue=None):
    slot_w = b.shape[2] if b_slots else None
    b2d = (b.shape[1], b.shape[0] * b.shape[2]) if b_slots else b.shape
    b_shape = b2d if b_cols is None else (b2d[0], b_cols[1])
    if b_rows is not None:
        b_shape = (b_rows, b_shape[1])
    if mode == "nn":
        (M, K), (K2, N) = a.shape, b_shape
    elif mode == "nt":
        (M, K), (N, K2) = a.shape, b_shape
    else:
        (K, M), (K2, N) = a.shape, b_shape
    assert K == K2, (name, a.shape, b_shape)
    tm, tn, tk = _tile(M, 1024 if K <= 2048 else 512), (N // N_DEV if col_slots else _tile(N, 1024)), _tile(K, 4096)
    if b_slots and mode == "nn":
        tn = slot_w
    nk = K // tk
    dims = (_DOT_DIMS[mode], ((), ()))
    col0 = 0
    if b_cols is not None:
        assert mode != "tn" and b_cols[0] % (tn if mode == "nn" else tk) == 0
        col0 = b_cols[0] // (tn if mode == "nn" else tk)
    assert not b_slots or (mode == "nn" or (mode == "nt" and nk == 1 and b_cols is None))

    def body(*refs):
        a_ref, b_ref = refs[:2]
        c_ref = refs[2] if add is not None else None
        e_ref = refs[2 + (add is not None)] if epilogue is not None else None
        o_ref = refs[2 + (add is not None) + (epilogue is not None) + (after is not None)]
        if b_slots and mode == "nt":
            part = lax.dot_general(a_ref[:, :slot_w], b_ref[0], dims, preferred_element_type=F32)
            for sl in range(1, b_ref.shape[0]):
                part += lax.dot_general(a_ref[:, sl * slot_w:(sl + 1) * slot_w], b_ref[sl], dims, preferred_element_type=F32)
        else:
            part = lax.dot_general(a_ref[...], b_ref[...], dims, preferred_element_type=F32)

        def finish(r):
            if scale is not None:
                r = r * scale
            if add is not None:
                r = r + c_ref[...]
            if epilogue is not None:
                r = epilogue[0](r, e_ref[...])
            o_ref[...] = r.astype(out_dtype)

        if nk == 1:
            finish(part)
            return
        acc = refs[-1]
        k = pl.program_id(2)

        @pl.when(k == 0)
        def _():
            acc[...] = part

        @pl.when(jnp.logical_and(k > 0, k < nk - 1))
        def _():
            acc[...] += part

        @pl.when(k == nk - 1)
        def _():
            finish(acc[...] + part)

    if mode == "tn":
        a_spec = pl.BlockSpec((tk, tm), lambda i, j, k: (k, i))
    else:
        a_spec = pl.BlockSpec((tm, tk), lambda i, j, k: (i, k))
    if b_slots and mode == "nn":
        b_spec = pl.BlockSpec((None, tk, tn), lambda i, j, k: (j + col0, k, 0))
    elif b_slots:
        b_spec = pl.BlockSpec((b.shape[0], tn, slot_w), lambda i, j, k: (0, j, 0))
    elif mode == "nt":
        b_spec = pl.BlockSpec((tn, tk), lambda i, j, k: (j, k + col0))
    else:
        b_spec = pl.BlockSpec((tk, tn), lambda i, j, k: (k, j + col0))
    o_spec = pl.BlockSpec((tm, tn), lambda i, j, k: (i, j))
    in_specs = [a_spec, b_spec] + ([o_spec] if add is not None else [])
    args = (a, b) + ((add,) if add is not None else ())
    if epilogue is not None:
        in_specs.append(o_spec)
        args += (epilogue[1],)
    if after is not None:
        in_specs.append(pl.BlockSpec(after.shape, lambda i, j, k: (0, 0)))
        args += (after,)
    out_shape = jax.ShapeDtypeStruct((M, N), out_dtype)
    if col_slots:
        assert add is None
        o_spec = pl.BlockSpec((None, tm, tn), lambda i, j, k: (j, i, 0))
        out_shape = jax.ShapeDtypeStruct((N_DEV, M, tn), out_dtype)
    return pl.pallas_call(
        body, name=name, grid=(M // tm, N // tn, nk),
        in_specs=in_specs, out_specs=o_spec,
        out_shape=out_shape,
        scratch_shapes=[pltpu.VMEM((tm, tn), F32)] if nk > 1 else [],
        compiler_params=_cparams("parallel", "parallel", "arbitrary"),
    )(*args)


class _NatIn:
    def __init__(self, ref):
        self.ref = ref

    def __getitem__(self, idx):
        v = jnp.swapaxes(self.ref[...], 0, 1)
        return v.reshape(v.shape[0] * N_SEG, v.shape[2])


class _NatOut:
    def __init__(self, ref):
        self.ref = ref

    def __setitem__(self, idx, val):
        self.ref[...] = jnp.swapaxes(val.reshape(val.shape[0] // N_SEG, N_SEG, val.shape[1]), 0, 1)


def _rowcall(body, name, n_rows, ins, outs, tile_rows=256):
    tr = _tile(n_rows, tile_rows, SUBLANES * 2)
    n_in = len(ins)
    in_kinds = [k for _, k in ins]
    kinds = [k for _, _, k in outs]

    def kern(*refs):
        @pl.when(pl.program_id(0) == 0)
        def _():
            for r, kind in zip(refs[n_in:], kinds):
                if kind == "acc":
                    r[...] = jnp.zeros_like(r)

        wrapped = [_NatIn(r) if k == "nat" else r for r, k in zip(refs[:n_in], in_kinds)]
        wrapped += [_NatOut(r) if k == "nat" else r for r, k in zip(refs[n_in:], kinds)]
        body(*wrapped)

    in_specs, args = [], []
    for arr, kind in ins:
        if kind == "row":
            in_specs.append(pl.BlockSpec((tr, arr.shape[1]), lambda i: (i, 0)))
        elif kind == "nat":
            in_specs.append(pl.BlockSpec((N_SEG, tr // N_SEG, arr.shape[1]), lambda i: (0, i, 0)))
            arr = arr.reshape(N_SEG, n_rows // N_SEG, arr.shape[1])
        else:
            in_specs.append(pl.BlockSpec(arr.shape, lambda i, nd=arr.ndim: (0,) * nd))
        args.append(arr)
    out_specs, out_shape = [], []
    for width, dtype, kind in outs:
        if kind == "row":
            out_specs.append(pl.BlockSpec((tr, width), lambda i: (i, 0)))
            out_shape.append(jax.ShapeDtypeStruct((n_rows, width), dtype))
        elif kind == "right":
            out_specs.append(pl.BlockSpec((tr, width), lambda i: (i, 1)))
            out_shape.append(jax.ShapeDtypeStruct((n_rows, 2 * width), dtype))
        elif kind == "nat":
            out_specs.append(pl.BlockSpec((N_SEG, tr // N_SEG, width), lambda i: (0, i, 0)))
            out_shape.append(jax.ShapeDtypeStruct((N_SEG, n_rows // N_SEG, width), dtype))
        else:
            out_specs.append(pl.BlockSpec((1, width), lambda i: (0, 0)))
            out_shape.append(jax.ShapeDtypeStruct((1, width), F32))
    res = pl.pallas_call(
        kern, name=name, grid=(n_rows // tr,), in_specs=in_specs, out_specs=out_specs, out_shape=out_shape,
        compiler_params=_cparams("arbitrary"),
    )(*args)
    return [r.reshape(n_rows, r.shape[2]) if k == "nat" else r for r, k in zip(res, kinds)]


def _rstd(x):
    return lax.rsqrt(jnp.mean(x * x, axis=-1, keepdims=True) + RMS_EPS)


def _rms_bwd(x, g, dy):
    xh = x * _rstd(x)
    dxh = dy * g
    dx = _rstd(x) * (dxh - xh * jnp.mean(dxh * xh, axis=-1, keepdims=True))
    return dx, jnp.sum(dy * xh, axis=0, keepdims=True)


def _silu(z):
    return z * jax.nn.sigmoid(z)


def _norm_cast(x, g, name, x_kind="row"):
    def body(x_ref, g_ref, o_ref):
        x = x_ref[...]
        o_ref[...] = (x * _rstd(x) * g_ref[...]).astype(BF16)

    return _rowcall(body, name, x.shape[0], [(x, x_kind), (g, "full")], [(x.shape[1], BF16, "row")])[0]


def _resid_norm2(x, o, g_post, g_kv, g_pre, name):
    def body(x_ref, o_ref, go_ref, gk_ref, gp_ref, h_ref, nk_ref, np_ref):
        o = o_ref[...]
        h = x_ref[...] + o * _rstd(o) * go_ref[...]
        h_ref[...] = h
        hn = h * _rstd(h)
        nk_ref[...] = (hn * gk_ref[...]).astype(BF16)
        np_ref[...] = (hn * gp_ref[...]).astype(BF16)

    d = x.shape[1]
    return _rowcall(body, name, x.shape[0], [(x, "nat"), (o, "row"), (g_post, "full"), (g_kv, "full"), (g_pre, "full")],
                    [(d, F32, "nat"), (d, BF16, "nat"), (d, BF16, "nat")])


def _post_norm_loss(o, g, h1, target, name):
    d = o.shape[1]

    def body(o_ref, g_ref, h_ref, t_ref, dh_ref, do_ref, acc_ref, dg_ref):
        o = o_ref[...]
        e = h_ref[...] + o * _rstd(o) * g_ref[...] - t_ref[...]
        dh = e * (1.0 / d)
        dh_ref[...] = dh
        acc_ref[...] += jnp.sum(e * e, axis=0, keepdims=True)
        dx, dg = _rms_bwd(o, g_ref[...], dh)
        do_ref[...] = dx.astype(BF16)
        dg_ref[...] += dg

    return _rowcall(body, name, o.shape[0], [(o, "row"), (g, "full"), (h1, "row"), (target, "row")],
                    [(d, F32, "row"), (d, BF16, "row"), (d, F32, "acc"), (d, F32, "acc")])


def _gate_bwd(d_oz, o, z, name):
    def body(d_ref, o_ref, z_ref, do_ref, dz_ref):
        _, vjp = jax.vjp(lambda o, z: o * _silu(z), o_ref[...], z_ref[...].astype(F32))
        do, dz = vjp(d_ref[...].astype(F32))
        do_ref[...] = do.astype(BF16)
        dz_ref[...] = dz.astype(BF16)

    w = o.shape[1]
    return _rowcall(body, name, o.shape[0], [(d_oz, "row"), (o, "row"), (z, "row")], [(w, BF16, "row"), (w, BF16, "right")])


def _norm_bwd2(dh2, h1, dxn1, dhn_kv, g_pre, g_kv, o0, g_post0, name):
    def body(dh2_ref, h_ref, d1_ref, dk_ref, gp_ref, gk_ref, o_ref, go_ref, dh1_ref, do_ref, dgp_ref, dgk_ref, dgo_ref):
        h = h_ref[...]
        dx1, dg1 = _rms_bwd(h, gp_ref[...], d1_ref[...].astype(F32))
        dxk, dgk = _rms_bwd(h, gk_ref[...], dk_ref[...].astype(F32))
        dh1 = dh2_ref[...] + dx1 + dxk
        dh1_ref[...] = dh1
        dgp_ref[...] += dg1
        dgk_ref[...] += dgk
        dxo, dgo = _rms_bwd(o_ref[...], go_ref[...], dh1)
        do_ref[...] = dxo.astype(BF16)
        dgo_ref[...] += dgo

    d = h1.shape[1]
    return _rowcall(body, name, h1.shape[0],
                    [(dh2, "nat"), (h1, "nat"), (dxn1, "nat"), (dhn_kv, "nat"), (g_pre, "full"), (g_kv, "full"),
                     (o0, "row"), (g_post0, "full")],
                    [(d, F32, "nat"), (d, BF16, "row"), (d, F32, "acc"), (d, F32, "acc"), (d, F32, "acc")])


def _norm_bwd1(dres, x, dxn, g, name):
    def body(dr_ref, x_ref, dn_ref, g_ref, dx_ref, dg_ref):
        dx, dg = _rms_bwd(x_ref[...], g_ref[...], dn_ref[...].astype(F32))
        dx_ref[...] = dr_ref[...] + dx
        dg_ref[...] += dg

    d = x.shape[1]
    return _rowcall(body, name, x.shape[0], [(dres, "nat"), (x, "nat"), (dxn, "row"), (g, "full")],
                    [(d, F32, "nat"), (d, F32, "acc")])


def _s5_gate(y_ssm, gp, b_glu, z, name):
    def body(y_ref, gp_ref, b_ref, z_ref, o_ref):
        yg = jax.nn.gelu(y_ref[...])
        o_ref[...] = (yg * jax.nn.sigmoid(gp_ref[...] + b_ref[...]) * _silu(z_ref[...].astype(F32))).astype(BF16)

    return _rowcall(body, name, y_ssm.shape[0], [(y_ssm, "row"), (gp, "row"), (b_glu, "full"), (z, "row")],
                    [(y_ssm.shape[1], BF16, "row")])[0]


def _s5_gate_bwd(dy3, y_ssm, gp, b_glu, z, name):
    def body(d_ref, y_ref, gp_ref, b_ref, z_ref, dz_ref, dgp_ref, dyg_ref, db_ref):
        yg = jax.nn.gelu(y_ref[...])
        _, vjp = jax.vjp(lambda yg, gp, z: yg * jax.nn.sigmoid(gp) * _silu(z), yg, gp_ref[...] + b_ref[...],
                         z_ref[...].astype(F32))
        dyg, dgp, dz = vjp(d_ref[...].astype(F32))
        dz_ref[...] = dz.astype(BF16)
        dgp_ref[...] = dgp.astype(BF16)
        dyg_ref[...] = dyg
        db_ref[...] += jnp.sum(dgp, axis=0, keepdims=True)

    w = y_ssm.shape[1]
    return _rowcall(body, name, y_ssm.shape[0],
                    [(dy3, "row"), (y_ssm, "row"), (gp, "row"), (b_glu, "full"), (z, "row")],
                    [(w, BF16, "right"), (w, BF16, "row"), (w, F32, "row"), (w, F32, "acc")])


def _cast_bf16(x, name):
    r, c = x.shape
    by_cols = r % (2 * SUBLANES) != 0
    tr, tc = (r, _tile(c, 256)) if by_cols else (_tile(r, 512, 2 * SUBLANES), c)
    pos = (lambda i: (0, i)) if by_cols else (lambda i: (i, 0))

    def body(x_ref, o_ref):
        o_ref[...] = x_ref[...].astype(BF16)

    return pl.pallas_call(
        body, name=name, grid=(c // tc if by_cols else r // tr,),
        in_specs=[pl.BlockSpec((tr, tc), pos)], out_specs=pl.BlockSpec((tr, tc), pos),
        out_shape=jax.ShapeDtypeStruct((r, c), BF16), compiler_params=_cparams("parallel"),
    )(x)


def _concat_cast(a, b, name):
    def body(a_ref, b_ref, o_ref):
        w = a_ref.shape[1]
        o_ref[:, :w] = a_ref[...].astype(BF16)
        o_ref[:, w:] = b_ref[...].astype(BF16)

    return _rowcall(body, name, a.shape[0], [(a, "row"), (b, "row")], [(a.shape[1] + b.shape[1], BF16, "row")])[0]


def _disc(ar, ai, ldt):
    dt = jnp.exp(ldt)
    mag = jnp.exp(ar * dt)
    abr = mag * jnp.cos(ai * dt)
    abi = mag * jnp.sin(ai * dt)
    den = ar * ar + ai * ai
    nr = abr - 1.0
    return abr, abi, (nr * ar + abi * ai) / den, (abi * ar - nr * ai) / den


def _s5_disc_fwd(a_re, a_im, ldt):
    def body(ar, ai, ld, o1, o2, o3, o4):
        o1[...], o2[...], o3[...], o4[...] = _disc(ar[...], ai[...], ld[...])

    sh = jax.ShapeDtypeStruct(a_re.shape, F32)
    return pl.pallas_call(body, name="s5_disc_fwd", out_shape=(sh, sh, sh, sh))(a_re, a_im, ldt)


def _s5_disc_bwd(a_re, a_im, ldt, d_abr, d_abi, d_cr, d_ci):
    def body(ar, ai, ld, g1, g2, g3, g4, o1, o2, o3):
        _, vjp = jax.vjp(_disc, ar[...], ai[...], ld[...])
        o1[...], o2[...], o3[...] = vjp((g1[...], g2[...], g3[...], g4[...]))

    sh = jax.ShapeDtypeStruct(a_re.shape, F32)
    return pl.pallas_call(body, name="s5_disc_bwd", out_shape=(sh, sh, jax.ShapeDtypeStruct(ldt.shape, F32)))(
        a_re, a_im, ldt, d_abr, d_abi, d_cr, d_ci)


def _bbar(cr, ci, br, bi):
    return cr * br - ci * bi, cr * bi + ci * br


def _s5_bbar_fwd(cr_col, ci_col, b_re, b_im):
    def body(cr, ci, br, bi, o1, o2):
        o1[...], o2[...] = _bbar(cr[...], ci[...], br[...], bi[...])

    w = b_re.shape[1]
    return _rowcall(body, "s5_bbar_fwd", b_re.shape[0], [(cr_col, "row"), (ci_col, "row"), (b_re, "row"), (b_im, "row")],
                    [(w, F32, "row"), (w, F32, "row")], tile_rows=1024)


def _s5_bbar_bwd(cr_col, ci_col, b_re, b_im, d_re, d_im):
    def body(cr, ci, br, bi, g1, g2, o1, o2, o3, o4):
        _, vjp = jax.vjp(_bbar, cr[...], ci[...], br[...], bi[...])
        o1[...], o2[...], o3[...], o4[...] = vjp((g1[...], g2[...]))

    w = b_re.shape[1]
    return _rowcall(body, "s5_bbar_bwd", b_re.shape[0],
                    [(cr_col, "row"), (ci_col, "row"), (b_re, "row"), (b_im, "row"), (d_re, "row"), (d_im, "row")],
                    [(1, F32, "row"), (1, F32, "row"), (w, F32, "row"), (w, F32, "row")], tile_rows=1024)


def _block_diag(t):
    g, a, b = t.shape
    nb = g // GROUPS_PER_BLOCK
    t4 = t.reshape(nb, GROUPS_PER_BLOCK, a, b).transpose(0, 1, 3, 2)
    eye = jnp.eye(GROUPS_PER_BLOCK, dtype=t.dtype)
    return (t4[:, :, :, None, :] * eye[None, :, None, :, None]).reshape(nb, GROUPS_PER_BLOCK * b, GROUPS_PER_BLOCK * a)


def _block_diag_extract(d, a, b):
    nb = d.shape[0]
    d5 = d.reshape(nb, GROUPS_PER_BLOCK, b, GROUPS_PER_BLOCK, a)
    diag = jnp.stack([d5[:, g, :, g, :] for g in range(GROUPS_PER_BLOCK)], axis=1)
    return diag.transpose(0, 1, 3, 2).reshape(nb * GROUPS_PER_BLOCK, a, b)


def _scan_step(ar, ai, hr, hi, xr, xi):
    return ar * hr - ai * hi + xr, ar * hi + ai * hr + xi


def _s5_blocks_per_step(nb, full):
    want = 2 if full else 4
    while nb % want:
        want //= 2
    return want


def _s5_scan_fwd(u, bd_re, bd_im, cd_re, cd_im, ab_re, ab_im, init_re, init_im, d_row, full, name):
    s, w = u.shape
    nb = w // LANES
    rows = _tile(s, 512, SUBLANES)
    nc = s // rows
    steps = rows // N_SEG
    ns = nb * BLOCK_STATE

    nblk = _s5_blocks_per_step(nb, full)

    def body(u_ref, bdr, bdi, cdr, cdi, ar_ref, ai_ref, ir_ref, ii_ref, d_ref, *outs):
        if full:
            y_ref, yg_ref, hr_out, hi_out, er_ref, ei_ref, hr_ref, hi_ref, cr, ci = outs
        else:
            er_ref, ei_ref, hr_ref, hi_ref, cr, ci = outs
        c = pl.program_id(1)
        cols = lambda b, width: slice(b * width, (b + 1) * width)

        @pl.when(c == 0)
        def _():
            cr[...] = ir_ref[...]
            ci[...] = ii_ref[...]

        for b in range(nblk):
            ub = u_ref[:, cols(b, LANES)].astype(BF16)
            hr_ref[:, cols(b, BLOCK_STATE)] = jnp.dot(ub, bdr[b], preferred_element_type=F32)
            hi_ref[:, cols(b, BLOCK_STATE)] = jnp.dot(ub, bdi[b], preferred_element_type=F32)
        ar, ai = ar_ref[...], ai_ref[...]

        hr, hi = cr[...], ci[...]
        for j in range(steps):
            rows_j = pl.ds(j * N_SEG, N_SEG)
            hr, hi = _scan_step(ar, ai, hr, hi, hr_ref[rows_j, :], hi_ref[rows_j, :])
            hr_ref[rows_j, :] = hr
            hi_ref[rows_j, :] = hi
        cr[...] = hr
        ci[...] = hi
        if full:
            hr_out[...] = hr_ref[...].astype(BF16)
            hi_out[...] = hi_ref[...].astype(BF16)
            for b in range(nblk):
                st_b, ln_b = cols(b, BLOCK_STATE), cols(b, LANES)
                y = (jnp.dot(hr_out[:, st_b], cdr[b], preferred_element_type=F32)
                     + jnp.dot(hi_out[:, st_b], cdi[b], preferred_element_type=F32)
                     + d_ref[:, ln_b] * u_ref[:, ln_b])
                y_ref[:, ln_b] = y
                yg_ref[:, ln_b] = jax.nn.gelu(y).astype(BF16)

        @pl.when(c == nc - 1)
        def _():
            er_ref[...] = hr
            ei_ref[...] = hi

    lanes, states = LANES * nblk, BLOCK_STATE * nblk
    blk3 = lambda a: pl.BlockSpec((nblk,) + a.shape[1:], lambda k, c: (k, 0, 0))
    seg = pl.BlockSpec((N_SEG, states), lambda k, c: (0, k))
    st = pl.BlockSpec((rows, states), lambda k, c: (c, k))
    in_specs = [pl.BlockSpec((rows, lanes), lambda k, c: (c, k)), blk3(bd_re), blk3(bd_im), blk3(cd_re), blk3(cd_im),
                seg, seg, seg, seg, pl.BlockSpec((1, lanes), lambda k, c: (0, k))]
    seg_shape = jax.ShapeDtypeStruct((N_SEG, ns), F32)
    st_shape = jax.ShapeDtypeStruct((s, ns), BF16)
    scratch = [pltpu.VMEM((rows, states), F32)] * 2 + [pltpu.VMEM((N_SEG, states), F32)] * 2
    if full:
        ych = pl.BlockSpec((rows, lanes), lambda k, c: (c, k))
        out_specs = [ych, ych, st, st, seg, seg]
        out_shape = [jax.ShapeDtypeStruct((s, w), F32), jax.ShapeDtypeStruct((s, w), BF16), st_shape, st_shape, seg_shape, seg_shape]
    else:
        out_specs = [seg, seg]
        out_shape = [seg_shape, seg_shape]
    return pl.pallas_call(
        body, name=name, grid=(nb // nblk, nc), in_specs=in_specs, out_specs=out_specs, out_shape=out_shape,
        scratch_shapes=scratch, compiler_params=_cparams("parallel", "arbitrary"),
    )(u, bd_re, bd_im, cd_re, cd_im, ab_re, ab_im, init_re, init_im, d_row)


def _s5_seg_fix(e_re, e_im, ab_re, ab_im, seg_len, reverse, name):
    assert seg_len & (seg_len - 1) == 0

    def body(er, ei, ar, ai, o_re, o_im):
        pr, pi = ar[0:1, :], ai[0:1, :]
        for _ in range(int(math.log2(seg_len))):
            pr, pi = pr * pr - pi * pi, 2.0 * pr * pi
        tr = jnp.zeros_like(pr)
        ti = jnp.zeros_like(pr)
        order = list(range(N_SEG - 1, -1, -1)) if reverse else list(range(N_SEG))
        for n, sgm in enumerate(order):
            o_re[sgm:sgm + 1, :] = tr
            o_im[sgm:sgm + 1, :] = ti
            if n < N_SEG - 1:
                tr, ti = _scan_step(pr, pi, tr, ti, er[sgm:sgm + 1, :], ei[sgm:sgm + 1, :])

    sh = jax.ShapeDtypeStruct(e_re.shape, F32)
    return pl.pallas_call(body, name=name, out_shape=(sh, sh))(e_re, e_im, ab_re, ab_im)


def _s5_scan_bwd(dy, u, h_re, h_im, bd_re, bd_im, cd_re, cd_im, ab_re, ab_imn, gin_re, gin_im, d_row, full, name, duz=None):
    s, w = u.shape
    nb = w // LANES
    rows = _tile(s, 512, SUBLANES)
    nc = s // rows
    steps = rows // N_SEG
    ns = nb * BLOCK_STATE

    nblk = _s5_blocks_per_step(nb, full)

    def body(dy_ref, u_ref, hr_ref, hi_ref, bdr, bdi, cdr, cdi, ar_ref, ai_ref, ir_ref, ii_ref, d_ref, *outs):
        if full:
            _, du_ref, dbr_ref, dbi_ref, dcr_ref, dci_ref, dar_ref, dai_ref, dd_ref, gr, gi, accr, acci = outs
        else:
            er_ref, ei_ref, gr, gi = outs
        c = pl.program_id(1)
        cols = lambda b, width: slice(b * width, (b + 1) * width)

        @pl.when(c == 0)
        def _():
            gr[pl.ds(rows, N_SEG), :] = ir_ref[...]
            gi[pl.ds(rows, N_SEG), :] = ii_ref[...]
            if full:
                for r in (dbr_ref, dbi_ref, dcr_ref, dci_ref, dd_ref, accr, acci):
                    r[...] = jnp.zeros_like(r)

        nt = (_DOT_DIMS["nt"], ((), ()))
        tn = (_DOT_DIMS["tn"], ((), ()))
        for b in range(nblk):
            dyb = dy_ref[:, cols(b, LANES)].astype(BF16)
            gr[pl.ds(0, rows), cols(b, BLOCK_STATE)] = lax.dot_general(dyb, cdr[b], nt, preferred_element_type=F32)
            gi[pl.ds(0, rows), cols(b, BLOCK_STATE)] = lax.dot_general(dyb, cdi[b], nt, preferred_element_type=F32)
        ar, ai = ar_ref[...], ai_ref[...]

        g0r, g0i = gr[pl.ds(rows, N_SEG), :], gi[pl.ds(rows, N_SEG), :]
        for j in range(steps - 1, -1, -1):
            rows_j = pl.ds(j * N_SEG, N_SEG)
            g0r, g0i = _scan_step(ar, ai, g0r, g0i, gr[rows_j, :], gi[rows_j, :])
            gr[rows_j, :] = g0r
            gi[rows_j, :] = g0i
        if full:
            for b in range(nblk):
                st_b, ln_b = cols(b, BLOCK_STATE), cols(b, LANES)
                hr, hi = hr_ref[:, st_b], hi_ref[:, st_b]
                gnr, gni = gr[pl.ds(N_SEG, rows), st_b], gi[pl.ds(N_SEG, rows), st_b]
                accr[:, st_b] += jnp.sum((gnr * hr + gni * hi).reshape(steps, N_SEG, BLOCK_STATE), axis=0)
                acci[:, st_b] += jnp.sum((gni * hr - gnr * hi).reshape(steps, N_SEG, BLOCK_STATE), axis=0)
                dyb = dy_ref[:, ln_b].astype(BF16)
                ub = u_ref[:, ln_b].astype(BF16)
                gbr, gbi = gr[pl.ds(0, rows), st_b].astype(BF16), gi[pl.ds(0, rows), st_b].astype(BF16)
                dcr_ref[b] += lax.dot_general(hr.astype(BF16), dyb, tn, preferred_element_type=F32)
                dci_ref[b] += lax.dot_general(hi.astype(BF16), dyb, tn, preferred_element_type=F32)
                dbr_ref[b] += lax.dot_general(ub, gbr, tn, preferred_element_type=F32)
                dbi_ref[b] += lax.dot_general(ub, gbi, tn, preferred_element_type=F32)
                du_ref[:, ln_b] = (lax.dot_general(gbr, bdr[b], nt, preferred_element_type=F32)
                                   + lax.dot_general(gbi, bdi[b], nt, preferred_element_type=F32)
                                   + d_ref[:, ln_b] * dy_ref[:, ln_b]).astype(BF16)
                dd_ref[:, ln_b] += jnp.sum(dy_ref[:, ln_b] * u_ref[:, ln_b], axis=0, keepdims=True)
        gr[pl.ds(rows, N_SEG), :] = g0r
        gi[pl.ds(rows, N_SEG), :] = g0i

        @pl.when(c == nc - 1)
        def _():
            if full:
                dar_ref[...] = jnp.sum(accr[...], axis=0, keepdims=True)
                dai_ref[...] = jnp.sum(acci[...], axis=0, keepdims=True)
            else:
                er_ref[...] = g0r
                ei_ref[...] = g0i

    lanes, states = LANES * nblk, BLOCK_STATE * nblk
    rev = lambda k, c: (nc - 1 - c, k)
    blk3 = lambda a: pl.BlockSpec((nblk,) + a.shape[1:], lambda k, c: (k, 0, 0))
    seg = pl.BlockSpec((N_SEG, states), lambda k, c: (0, k))
    st = pl.BlockSpec((rows, states), rev)
    ch = pl.BlockSpec((rows, lanes), rev)
    vec = pl.BlockSpec((1, lanes), lambda k, c: (0, k))
    if not full:
        st = pl.BlockSpec((rows, states), lambda k, c: (0, k))
    in_specs = [ch, ch if full else pl.BlockSpec((rows, lanes), lambda k, c: (0, k)), st, st,
                blk3(bd_re), blk3(bd_im), blk3(cd_re), blk3(cd_im), seg, seg, seg, seg, vec]
    args = [dy, u, h_re, h_im, bd_re, bd_im, cd_re, cd_im, ab_re, ab_imn, gin_re, gin_im, d_row]
    gbuf = [pltpu.VMEM((rows + N_SEG, states), F32)] * 2
    if full:
        row1 = pl.BlockSpec((1, states), lambda k, c: (0, k))
        out_specs = [ch, blk3(bd_re), blk3(bd_im), blk3(cd_re), blk3(cd_im), row1, row1, vec]
        out_shape = [jax.ShapeDtypeStruct(duz.shape, BF16),
                     jax.ShapeDtypeStruct(bd_re.shape, F32), jax.ShapeDtypeStruct(bd_im.shape, F32),
                     jax.ShapeDtypeStruct(cd_re.shape, F32), jax.ShapeDtypeStruct(cd_im.shape, F32),
                     jax.ShapeDtypeStruct((1, ns), F32), jax.ShapeDtypeStruct((1, ns), F32),
                     jax.ShapeDtypeStruct((1, w), F32)]
        scratch = gbuf + [pltpu.VMEM((N_SEG, states), F32)] * 2
        in_specs.append(pl.BlockSpec(memory_space=pl.ANY))
        args.append(duz)
        aliases = {len(args) - 1: 0}
    else:
        out_specs = [seg, seg]
        out_shape = [jax.ShapeDtypeStruct((N_SEG, ns), F32)] * 2
        scratch = gbuf
        aliases = {}
    return pl.pallas_call(
        body, name=name, grid=(nb // nblk, nc), in_specs=in_specs, out_specs=out_specs, out_shape=out_shape,
        input_output_aliases=aliases, scratch_shapes=scratch, compiler_params=_cparams("parallel", "arbitrary"),
    )(*args)


def _log_sigmoid(x):
    return jnp.minimum(x, 0.0) - jnp.log(1.0 + jnp.exp(-jnp.abs(x)))


def _tri(n, upper):
    r = lax.broadcasted_iota(jnp.int32, (n, n), 0)
    c = lax.broadcasted_iota(jnp.int32, (n, n), 1)
    return jnp.where((c >= r) if upper else (r >= c), 1.0, 0.0).astype(F32)


def _cum_fwd(f_logit, b_row, name):
    s, w = f_logit.shape
    t = _tile(s, 256, SUBLANES)

    def body(f_ref, b_ref, o_ref, carry):
        @pl.when(pl.program_id(0) == 0)
        def _():
            carry[...] = jnp.zeros_like(carry)

        lf = _log_sigmoid(f_ref[...] + b_ref[...])
        cum = jnp.dot(_tri(t, False), lf, precision=lax.Precision.HIGHEST, preferred_element_type=F32) + carry[...]
        o_ref[...] = cum * LOG2E
        carry[...] = cum[t - 1:t, :]

    return pl.pallas_call(
        body, name=name, grid=(s // t,),
        in_specs=[pl.BlockSpec((t, w), lambda i: (i, 0)), pl.BlockSpec((1, w), lambda i: (0, 0))],
        out_specs=pl.BlockSpec((t, w), lambda i: (i, 0)), out_shape=jax.ShapeDtypeStruct((s, w), F32),
        scratch_shapes=[pltpu.VMEM((1, w), F32)], compiler_params=_cparams("arbitrary"),
    )(f_logit, b_row)


def _cum_bwd(dcq, dck, f_logit, b_row, name):
    s, w = f_logit.shape
    t = _tile(s, 256, SUBLANES)
    nt = s // t

    def body(q_ref, k_ref, f_ref, b_ref, df_ref, db_ref, carry):
        @pl.when(pl.program_id(0) == 0)
        def _():
            carry[...] = jnp.zeros_like(carry)
            db_ref[...] = jnp.zeros_like(db_ref)

        dc = q_ref[...] - k_ref[...]
        rc = jnp.dot(_tri(t, True), dc, precision=lax.Precision.HIGHEST, preferred_element_type=F32) + carry[...]
        carry[...] = rc[0:1, :]
        df = rc * (1.0 - jax.nn.sigmoid(f_ref[...] + b_ref[...]))
        df_ref[...] = df.astype(BF16)
        db_ref[...] += jnp.sum(df, axis=0, keepdims=True)

    rev = pl.BlockSpec((t, w), lambda i: (nt - 1 - i, 0))
    one = pl.BlockSpec((1, w), lambda i: (0, 0))
    return pl.pallas_call(
        body, name=name, grid=(nt,), in_specs=[rev, rev, rev, one], out_specs=[rev, one],
        out_shape=[jax.ShapeDtypeStruct((s, w), BF16), jax.ShapeDtypeStruct((1, w), F32)],
        scratch_shapes=[pltpu.VMEM((1, w), F32)], compiler_params=_cparams("arbitrary"),
    )(dcq, dck, f_logit, b_row)


def _head_col(cum_tile, h):
    lane = lax.broadcasted_iota(jnp.int32, cum_tile.shape, 1)
    return jnp.sum(jnp.where(lane == h, cum_tile, 0.0), axis=1, keepdims=True)


def _attn_tiles(s):
    return _tile(s, 512, LANES)


def _exp2_rows(sc, sub):
    return jnp.concatenate([jnp.exp2(sc[:, b * LANES:(b + 1) * LANES] - sub) for b in range(sc.shape[1] // LANES)], axis=1)


def _row_of(rep):
    return jnp.transpose(rep)[0:1, :]


def _causal(sc, keys_on_rows):
    r = lax.broadcasted_iota(jnp.int32, sc.shape, 0)
    c = lax.broadcasted_iota(jnp.int32, sc.shape, 1)
    return jnp.where((r <= c) if keys_on_rows else (c <= r), sc, NEG_INF)


def _fox_fwd(q2, kv, cum2_t, z, name):
    s, w = q2.shape
    nh = w // HEAD_DIM
    tq = _attn_tiles(s)
    nq = s // tq
    nt = (_DOT_DIMS["nt"], ((), ()))

    def body(q_ref, k_ref, v_ref, ct_ref, z_ref, o_ref, oz_ref, lse_row_ref, m_s, acc_s, vaug, s_buf):
        i = pl.program_id(1)

        @pl.when(i == 0)
        def _():
            vaug[:, :HEAD_DIM] = v_ref[...]
            vaug[:, HEAD_DIM:] = jnp.ones((s, LANES), BF16)

        qb = q_ref[...]
        m_s[...] = jnp.full_like(m_s, NEG_INF)
        acc_s[...] = jnp.zeros_like(acc_s)

        def scores(j):
            off = pl.multiple_of(j * tq, tq)
            return lax.dot_general(qb, k_ref[pl.ds(off, tq), :], nt, preferred_element_type=F32) - ct_ref[:, pl.ds(off, tq)]

        def softmax_pv(j, sc):
            m_old = m_s[...]
            m_new = jnp.maximum(m_old, jnp.max(sc, axis=1, keepdims=True))
            p = _exp2_rows(sc, m_new)
            alpha = jnp.exp2(m_old - m_new)
            pv = jnp.dot(p.astype(BF16), vaug[pl.ds(pl.multiple_of(j * tq, tq), tq), :], preferred_element_type=F32)
            acc_s[...] = jnp.concatenate([alpha, alpha], axis=1) * acc_s[...] + pv
            m_s[...] = m_new

        s_buf[...] = scores(0)

        def loop(j, carry):
            nxt = scores(j + 1)
            softmax_pv(j, s_buf[...])
            s_buf[...] = nxt
            return carry

        lax.fori_loop(0, i, loop, 0)
        softmax_pv(i, _causal(s_buf[...], False))
        l = acc_s[:, HEAD_DIM:]
        o = acc_s[:, :HEAD_DIM] / l
        o_ref[...] = o
        oz_ref[...] = (o * _silu(z_ref[...].astype(F32))).astype(BF16)
        lse_row_ref[...] = _row_of(m_s[...] + jnp.log(l) * LOG2E)

    return pl.pallas_call(
        body, name=name, grid=(nh, nq),
        in_specs=[pl.BlockSpec((tq, HEAD_DIM), lambda h, i: (i, h)),
                  pl.BlockSpec((s, HEAD_DIM), lambda h, i: (0, h)),
                  pl.BlockSpec((s, HEAD_DIM), lambda h, i: (0, nh + h)),
                  pl.BlockSpec((None, 1, s), lambda h, i: (h, 0, 0)),
                  pl.BlockSpec((tq, HEAD_DIM), lambda h, i: (i, h))],
        out_specs=[pl.BlockSpec((tq, HEAD_DIM), lambda h, i: (i, h)),
                   pl.BlockSpec((tq, HEAD_DIM), lambda h, i: (i, h)),
                   pl.BlockSpec((None, 1, tq), lambda h, i: (h, 0, i))],
        out_shape=[jax.ShapeDtypeStruct((s, w), F32), jax.ShapeDtypeStruct((s, w), BF16),
                   jax.ShapeDtypeStruct((nh, 1, s), F32)],
        scratch_shapes=[pltpu.VMEM((tq, LANES), F32), pltpu.VMEM((tq, HEAD_DIM + LANES), F32),
                        pltpu.VMEM((s, HEAD_DIM + LANES), BF16), pltpu.VMEM((tq, tq), F32)],
        compiler_params=_cparams("arbitrary", "arbitrary"),
    )(q2, kv, kv, cum2_t, z)


def _fox_bwd(q2, kv, do, o, lse2_t, cum2, dqz, name):
    s, w = q2.shape
    nh = w // HEAD_DIM
    tk = _attn_tiles(s)
    nk = s // tk
    scale = HEAD_DIM ** -0.5
    nt = (_DOT_DIMS["nt"], ((), ()))
    tn = (_DOT_DIMS["tn"], ((), ()))

    def body(q_ref, k_ref, v_ref, do_ref, o_ref, lse_ref, c_ref, _, dk_ref, dv_ref, dq_ref, dcq_ref, dck_ref,
             dk_s, dv_s, dc_s, dq_s, dcq_s, dl_s, s_buf, dp_buf):
        h, j = pl.program_id(0), pl.program_id(1)

        @pl.when(j == 0)
        def _():
            dq_s[...] = jnp.zeros_like(dq_s)
            dcq_s[...] = jnp.zeros_like(dcq_s)
            for i in range(nk):
                rows = pl.ds(i * tk, tk)
                d = jnp.sum(do_ref[rows, :].astype(F32) * o_ref[rows, :], axis=1, keepdims=True)
                dl_s[:, i * tk:(i + 1) * tk] = _row_of(jnp.broadcast_to(d, (tk, LANES)))

        kb = k_ref[...]
        vb = v_ref[...]
        ck = jnp.broadcast_to(_head_col(c_ref[...], h), (tk, LANES))
        dk_s[...] = jnp.zeros_like(dk_s)
        dv_s[...] = jnp.zeros_like(dv_s)
        dc_s[...] = jnp.zeros_like(dc_s)

        def scores(i):
            off = pl.multiple_of(i * tk, tk)
            sc = lax.dot_general(kb, q_ref[pl.ds(off, tk), :], nt, preferred_element_type=F32) - lse_ref[:, pl.ds(off, tk)]
            dp = lax.dot_general(vb, do_ref[pl.ds(off, tk), :], nt, preferred_element_type=F32) - dl_s[:, pl.ds(off, tk)]
            return sc, dp

        def accumulate(i, sc, dp):
            off = pl.multiple_of(i * tk, tk)
            p = _exp2_rows(sc, ck)
            dv_s[...] += jnp.dot(p.astype(BF16), do_ref[pl.ds(off, tk), :], preferred_element_type=F32)
            ds = p * dp
            dsb = ds.astype(BF16)
            dk_s[...] += jnp.dot(dsb, q_ref[pl.ds(off, tk), :], preferred_element_type=F32)
            dq_s[pl.ds(off, tk), :] += lax.dot_general(dsb, kb, tn, preferred_element_type=F32)
            dcq_s[:, pl.ds(off, tk)] += jnp.sum(ds, axis=0, keepdims=True)
            part = ds[:, :LANES]
            for b in range(1, tk // LANES):
                part = part + ds[:, b * LANES:(b + 1) * LANES]
            dc_s[...] += part

        sc0, dp0 = scores(j)
        s_buf[...] = _causal(sc0, True)
        dp_buf[...] = dp0

        def loop(i, carry):
            nxt = scores(i + 1)
            accumulate(i, s_buf[...], dp_buf[...])
            s_buf[...], dp_buf[...] = nxt
            return carry

        lax.fori_loop(j, nk - 1, loop, 0)
        accumulate(nk - 1, s_buf[...], dp_buf[...])
        dk_ref[...] = (dk_s[...] * (1.0 / LOG2E)).astype(BF16)
        dv_ref[...] = dv_s[...].astype(BF16)
        dck_ref[...] = jnp.sum(jnp.transpose(dc_s[...]), axis=0, keepdims=True)

        @pl.when(j == nk - 1)
        def _():
            dq_ref[...] = (dq_s[...] * scale).astype(BF16)
            dcq_ref[...] = dcq_s[...]

    col = pl.BlockSpec((s, HEAD_DIM), lambda h, j: (0, h))
    row = pl.BlockSpec((None, 1, s), lambda h, j: (h, 0, 0))
    kspec = pl.BlockSpec((tk, HEAD_DIM), lambda h, j: (j, h))
    return pl.pallas_call(
        body, name=name, grid=(nh, nk),
        in_specs=[col, kspec, pl.BlockSpec((tk, HEAD_DIM), lambda h, j: (j, nh + h)), col, col, row,
                  pl.BlockSpec((tk, LANES), lambda h, j: (j, 0)), pl.BlockSpec(memory_space=pl.ANY)],
        out_specs=[kspec, kspec, col, row, pl.BlockSpec((None, 1, tk), lambda h, j: (h, 0, j))],
        out_shape=[jax.ShapeDtypeStruct((s, w), BF16), jax.ShapeDtypeStruct((s, w), BF16),
                   jax.ShapeDtypeStruct(dqz.shape, BF16), jax.ShapeDtypeStruct((nh, 1, s), F32),
                   jax.ShapeDtypeStruct((nh, 1, s), F32)],
        input_output_aliases={7: 2},
        scratch_shapes=[pltpu.VMEM((tk, HEAD_DIM), F32), pltpu.VMEM((tk, HEAD_DIM), F32), pltpu.VMEM((tk, LANES), F32),
                        pltpu.VMEM((s, HEAD_DIM), F32), pltpu.VMEM((1, s), F32), pltpu.VMEM((1, s), F32),
                        pltpu.VMEM((tk, tk), F32), pltpu.VMEM((tk, tk), F32)],
        compiler_params=_cparams("arbitrary", "arbitrary"),
    )(q2, kv, kv, do, o, lse2_t, cum2, dqz)


_ALL_PEERS = tuple(range(1, N_DEV))
_CHIP_PEERS = (1, 2, 4, 6)


def _exchange_copies(ins, outs, send_sems, recv_sems, local_sems, scatter, peers=_ALL_PEERS):
    x, y, c = (lax.axis_index(a) for a in MESH_AXES)
    me = 4 * x + 2 * y + c
    local, remote = [], []
    for a in range(len(ins)):
        local.append(pltpu.make_async_copy(ins[a].at[me] if scatter else ins[a], outs[a].at[me], local_sems.at[a]))
        for k in peers:
            px, py, pc = (1 - x if k & 4 else x), (1 - y if k & 2 else y), (1 - c if k & 1 else c)
            remote.append(pltpu.make_async_remote_copy(
                src_ref=ins[a].at[4 * px + 2 * py + pc] if scatter else ins[a], dst_ref=outs[a].at[me],
                send_sem=send_sems.at[a * (N_DEV - 1) + k - 1], recv_sem=recv_sems.at[a * (N_DEV - 1) + k - 1],
                device_id=(px, py, pc), device_id_type=pl.DeviceIdType.MESH))
    return local, remote


def _exchange_out_shapes(arrs, scatter):
    return [((N_DEV,) + a.shape[1:]) if scatter else ((N_DEV,) + a.shape) for a in arrs]


_HBM =pl.BlockSpec(memory_space=pltpu.HBM)
_SEM = pl.BlockSpec(memory_space=pltpu.SEMAPHORE)


def _exchange_start(arrs, scatter, name, after=(), peers=_ALL_PEERS):
    n = len(arrs)
    after = list(after)
    lands = [lax.empty(s, a.dtype) for s, a in zip(_exchange_out_shapes(arrs, scatter), arrs)]

    def body(*refs):
        ins, outs = refs[:n], refs[n:2 * n]
        send_sems, recv_sems, local_sems = refs[2 * n + len(after):2 * n + len(after) + 3]
        token = refs[-1]
        local, remote = _exchange_copies(ins, outs, send_sems, recv_sems, local_sems, scatter, peers)
        for cp in local + remote:
            cp.start()
        token[...] = jnp.zeros_like(token)

    hbm = lambda a: pltpu.HBM(a.shape, a.dtype)
    res = pl.pallas_call(
        body, name=name,
        out_shape=(pltpu.SemaphoreType.DMA((n * (N_DEV - 1),)), pltpu.SemaphoreType.DMA((n * (N_DEV - 1),)),
                   pltpu.SemaphoreType.DMA((n,)), *[hbm(a) for a in arrs], *[hbm(a) for a in lands],
                   jax.ShapeDtypeStruct((SUBLANES, LANES), F32)),
        in_specs=[_HBM] * (2 * n) + [pl.BlockSpec(memory_space=pl.ANY)] * len(after),
        out_specs=(_SEM, _SEM, _SEM, *[_HBM] * (2 * n), pl.BlockSpec(memory_space=pltpu.VMEM)),
        input_output_aliases={i: 3 + i for i in range(2 * n)},
        compiler_params=pltpu.CompilerParams(has_side_effects=pltpu.SideEffectType.DATAFLOW_SIDE_EFFECTING),
    )(*[pltpu.with_memory_space_constraint(a, pltpu.HBM) for a in list(arrs) + lands], *after)
    return (n, scatter, res[:3], res[3:3 + n], res[3 + n:3 + 2 * n], peers), res[-1]


def _exchange_wait(state, after, name):
    n, scatter, sems, srcs, lands, peers = state
    after = list(after) if isinstance(after, (list, tuple)) else [after]

    def body(*refs):
        ins, outs = refs[:n], refs[n:2 * n]
        send_sems, recv_sems, local_sems = refs[2 * n:2 * n + 3]
        local, remote = _exchange_copies(ins, outs, send_sems, recv_sems, local_sems, scatter, peers)
        for cp in remote:
            cp.wait_send()
            cp.wait_recv()
        for cp in local:
            cp.wait()

    hbm = lambda a: pltpu.HBM(a.shape, a.dtype)
    res = pl.pallas_call(
        body, name=name,
        out_shape=(*[hbm(a) for a in srcs], *[hbm(a) for a in lands]),
        in_specs=[_HBM] * (2 * n) + [_SEM] * 3 + [pl.BlockSpec(memory_space=pl.ANY)] * len(after),
        out_specs=tuple([_HBM] * (2 * n)),
        input_output_aliases={i: i for i in range(2 * n)},
        compiler_params=pltpu.CompilerParams(has_side_effects=pltpu.SideEffectType.DATAFLOW_SIDE_EFFECTING),
    )(*srcs, *lands, *sems, *after)
    return list(res[n:])


def _forward_to_sibling(slots, name):
    n = len(slots)
    hops = (2, 4, 6)

    def body(*refs):
        ins, outs, (send_sems, recv_sems) = refs[:n], refs[n:2 * n], refs[2 * n:]
        x, y, c = (lax.axis_index(a) for a in MESH_AXES)
        copies = []
        for a in range(n):
            for i, k in enumerate(hops):
                slot = 4 * (1 - x if k & 4 else x) + 2 * (1 - y if k & 2 else y) + c
                copies.append(pltpu.make_async_remote_copy(
                    src_ref=ins[a].at[slot], dst_ref=outs[a].at[slot],
                    send_sem=send_sems.at[a * len(hops) + i], recv_sem=recv_sems.at[a * len(hops) + i],
                    device_id=(x, y, 1 - c), device_id_type=pl.DeviceIdType.MESH))
        for cp in copies:
            cp.start()
        for cp in copies:
            cp.wait_send()
            cp.wait_recv()

    return pl.pallas_call(
        body, name=name, out_shape=[jax.ShapeDtypeStruct(s.shape, s.dtype) for s in slots],
        in_specs=[pl.BlockSpec(memory_space=pl.ANY)] * n, out_specs=[pl.BlockSpec(memory_space=pl.ANY)] * n,
        input_output_aliases={i: i for i in range(n)},
        scratch_shapes=[pltpu.SemaphoreType.DMA((n * len(hops),)), pltpu.SemaphoreType.DMA((n * len(hops),))],
    )(*slots)


def _adamw_math(w, g, m, v):
    m = ADAM_B1 * m + (1.0 - ADAM_B1) * g
    v = ADAM_B2 * v + (1.0 - ADAM_B2) * (g * g)
    m_hat = m / (1.0 - ADAM_B1 ** ADAM_STEP)
    v_hat = v / (1.0 - ADAM_B2 ** ADAM_STEP)
    return -ADAM_LR * (m_hat / (jnp.sqrt(v_hat) + ADAM_EPS) + ADAM_WD * w), m, v


def _slot_sum(p_ref):
    g = p_ref[0].astype(F32)
    for d in range(1, p_ref.shape[0]):
        g = g + p_ref[d].astype(F32)
    return g


def _adamw_tile(r, c):
    return _tile(r, max(SUBLANES, (256 * 1024) // c // SUBLANES * SUBLANES), SUBLANES)


def _adamw(parts, w, m, v, name):
    r, c = w.shape[-2:]
    by_cols = r % SUBLANES != 0
    tr, tc = (r, _tile(c, 256)) if by_cols else (_adamw_tile(r, c), c)

    def body(p_ref, w_ref, m_ref, v_ref, g_ref, d_ref, nm_ref, nv_ref):
        g = _slot_sum(p_ref)
        g_ref[...] = g
        d_ref[...], nm_ref[...], nv_ref[...] = _adamw_math(w_ref[...], g, m_ref[...], v_ref[...])

    pos = (lambda i: (0, i)) if by_cols else (lambda i: (i, 0))
    if w.ndim == 3:
        blk = pl.BlockSpec((None, tr, tc), lambda i: (0,) + pos(i))
    else:
        blk = pl.BlockSpec((tr, tc), pos)
    sh = jax.ShapeDtypeStruct(w.shape, F32)
    return pl.pallas_call(
        body, name=name, grid=(c // tc if by_cols else r // tr,),
        in_specs=[pl.BlockSpec((parts.shape[0], tr, tc), lambda i: (0,) + pos(i)), blk, blk, blk],
        out_specs=[blk] * 4, out_shape=[sh] * 4, compiler_params=_cparams("parallel"),
    )(parts, w, m, v)


def _sum_parts(parts, name):
    _, r, c = parts.shape
    tr = _adamw_tile(r, c)

    def body(p_ref, o_ref):
        o_ref[...] = _slot_sum(p_ref)

    return pl.pallas_call(
        body, name=name, grid=(r // tr,),
        in_specs=[pl.BlockSpec((parts.shape[0], tr, c), lambda i: (0, i, 0))],
        out_specs=pl.BlockSpec((tr, c), lambda i: (i, 0)), out_shape=jax.ShapeDtypeStruct((r, c), F32),
        compiler_params=_cparams("parallel"),
    )(parts)


def _lane_pad(a, width=LANES):
    return jnp.pad(a, ((0, 0), (0, width - a.shape[1])))


def _local_step(x, target, norm_pre, norm_post, kv_norm, kv_b_f, a_re, a_im, log_dt, b_re, b_im, c_re, c_im, comm):
    s, d = x.shape
    g, p = a_re.shape
    w = g * S5_GROUP
    fw = d
    nh = fw // HEAD_DIM
    seg_len = s // N_SEG
    row = lambda v: v.reshape(1, -1)
    g_pre0, g_pre1, g_post0, g_post1, g_kv = row(norm_pre[0]), row(norm_pre[1]), row(norm_post[0]), row(norm_post[1]), row(kv_norm)

    ldt = log_dt.reshape(g, 1)
    abr, abi, cr, ci = _s5_disc_fwd(a_re, a_im, ldt)
    cr_col, ci_col = cr.reshape(g * p, 1), ci.reshape(g * p, 1)
    b_re2, b_im2 = b_re.reshape(g * p, S5_GROUP), b_im.reshape(g * p, S5_GROUP)
    bb_re, bb_im = _s5_bbar_fwd(cr_col, ci_col, b_re2, b_im2)
    bd_re = _block_diag(bb_re.reshape(g, p, S5_GROUP)).astype(BF16)
    bd_im = _block_diag(bb_im.reshape(g, p, S5_GROUP)).astype(BF16)
    cd_re = _block_diag(c_re).astype(BF16)
    cd_im = _block_diag(-c_im).astype(BF16)
    ab_re = jnp.broadcast_to(abr.reshape(1, g * p), (N_SEG, g * p))
    ab_im = jnp.broadcast_to(abi.reshape(1, g * p), (N_SEG, g * p))
    zero_seg = jnp.zeros((N_SEG, g * p), F32)

    xn0 = _norm_cast(x, g_pre0 + comm.token, "norm_pre0", x_kind="nat")
    w_in = comm.weight("s5_w_in", [xn0, bd_re, bd_im, cd_re, cd_im, ab_re, ab_im])
    d_row, bglu_row = row(comm.vector("s5_d")), row(comm.vector("s5_b_glu"))
    u = _mm(xn0, w_in, "nn", BF16, "s5_in_u", b_cols=(0, w), b_slots=True)
    z0 = _mm(xn0, w_in, "nn", BF16, "s5_in_z", b_cols=(w, w), b_slots=True)
    e_re, e_im = _s5_scan_fwd(u, bd_re, bd_im, cd_re, cd_im, ab_re, ab_im, zero_seg, zero_seg, d_row, False, "s5_scan_ends")
    i_re, i_im = _s5_seg_fix(e_re, e_im, ab_re, ab_im, seg_len, False, "s5_seg_fix")
    y_ssm, yg, h_re, h_im, _, _ = _s5_scan_fwd(u, bd_re, bd_im, cd_re, cd_im, ab_re, ab_im, i_re, i_im, d_row, True, "s5_scan")
    w_glu, w_out = comm.weight("s5_w_glu", yg), comm.weight("s5_w_out", yg)
    gp = _mm(yg, w_glu, "nn", BF16, "s5_glu")
    y3 = _s5_gate(y_ssm, gp, bglu_row, z0, "s5_gate")
    w_kvt, fw_in = comm.weight("kv_w", y3), comm.weight("fox_w_in", y3)
    w_ft = jnp.pad(w_kvt[2 * fw:], ((0, LANES - nh), (0, 0)))
    o0 = _mm(y3, w_out, "nn", F32, "s5_out")

    h1, hn_kv, xn1 = _resid_norm2(x, o0, g_post0 + comm.late_token, g_kv, g_pre1, "resid_norms")
    kv = _mm(hn_kv, w_kvt, "nt", BF16, "kv_proj", b_rows=2 * fw)
    f_logit = _mm(hn_kv, w_ft, "nt", F32, "f_proj")
    bf_row = _lane_pad(row(kv_b_f))
    cum2 = _cum_fwd(f_logit, bf_row, "cum_fwd")
    cum2_t = cum2[:, :nh].T.reshape(nh, 1, s)
    q2 = _mm(xn1, fw_in, "nn", BF16, "fox_q", scale=HEAD_DIM ** -0.5 * LOG2E, b_cols=(0, fw), b_slots=True)
    z1 = _mm(xn1, fw_in, "nn", BF16, "fox_z", b_cols=(fw, fw), b_slots=True)
    o, oz, lse2_t = _fox_fwd(q2, kv, cum2_t, z1, "fox_fwd")
    fw_out = comm.weight("fox_w_out", oz)
    o1 = _mm(oz, fw_out, "nn", F32, "fox_out")
    dh2, do1, sq, dg_post1 = _post_norm_loss(o1, g_post1, h1, target, "norm_post1_loss")
    loss = 0.5 * jnp.sum(sq) / d

    d_fw_out = _mm(oz, do1, "tn", BF16, "fox_out_dw")
    d_oz = _mm(do1, fw_out, "nt", BF16, "fox_out_dx")
    do, dqz = _gate_bwd(d_oz, o, z1, "fox_gate_bwd")
    dk, dv, dqz, dcq, dck = _fox_bwd(q2, kv, do, o, lse2_t, cum2, dqz, "fox_bwd")
    d_fw_in = _mm(xn1, dqz, "tn", BF16, "fox_in_dw", col_slots=True)
    dxn1 = _mm(dqz, fw_in, "nt", BF16, "fox_in_dx", b_slots=True)
    dcq_sl = _lane_pad(dcq.reshape(nh, s).T)
    dck_sl = _lane_pad(dck.reshape(nh, s).T)
    df, db_f = _cum_bwd(dcq_sl, dck_sl, f_logit, bf_row, "cum_bwd")
    dkv = _concat_cast(dk, dv, "fox_dkv")
    d_w_kvmt = _mm(dkv, hn_kv, "tn", BF16, "kv_dw")
    d_w_ft = _mm(df, hn_kv, "tn", BF16, "f_dw")
    dhn_f = _mm(df, w_ft, "nn", F32, "f_dx")
    dhn_kv = _mm(dkv, w_kvt, "nn", BF16, "kv_dx", add=dhn_f, b_rows=2 * fw)
    d_w_kvt = jnp.concatenate([d_w_kvmt, d_w_ft[:nh]], axis=0)
    tok = comm.send_grads(dict(fox_w_out=d_fw_out, fox_w_in=d_fw_in, kv_w=d_w_kvt), "exchange_fox")
    dh1, do0, dg_pre1, dg_kv, dg_post0 = _norm_bwd2(dh2, h1, dxn1, dhn_kv, g_pre1, g_kv, o0, g_post0 + tok[0, 0],
                                                      "resid_norms_bwd")

    d_w_out = _mm(y3, do0, "tn", BF16, "s5_out_dw")
    dy3 = _mm(do0, w_out, "nt", BF16, "s5_out_dx")
    duz, dgp, dyg_direct, db_glu = _s5_gate_bwd(dy3, y_ssm, gp, bglu_row, z0, "s5_gate_bwd")
    d_w_glu = _mm(yg, dgp, "tn", BF16, "s5_glu_dw")
    gelu_bwd = lambda dyg, y: jax.vjp(jax.nn.gelu, y)[1](dyg)[0]
    dy_ssm = _mm(dgp, w_glu, "nt", F32, "s5_glu_dx", add=dyg_direct, epilogue=(gelu_bwd, y_ssm))
    d_row = d_row + comm.send_grads(dict(s5_w_out=d_w_out, s5_w_glu=d_w_glu), "exchange_s5")[0, 0]
    ab_imn = -ab_im
    ge_re, ge_im = _s5_scan_bwd(dy_ssm, u, h_re, h_im, bd_re, bd_im, cd_re, cd_im, ab_re, ab_imn, zero_seg, zero_seg,
                                d_row, False, "s5_adj_ends")
    gi_re, gi_im = _s5_seg_fix(ge_re, ge_im, ab_re, ab_imn, seg_len, True, "s5_adj_fix")
    duz, dbd_re, dbd_im, dcd_re, dcd_im, dab_re, dab_im, dd = _s5_scan_bwd(
        dy_ssm, u, h_re, h_im, bd_re, bd_im, cd_re, cd_im, ab_re, ab_imn, gi_re, gi_im, d_row, True, "s5_adj", duz=duz)
    d_w_in = _mm(xn0, duz, "tn", BF16, "s5_in_dw", col_slots=True)
    tok = comm.send_grads(dict(s5_w_in=d_w_in), "exchange_s5_in")
    dxn0 = _mm(duz, w_in, "nt", BF16, "s5_in_dx", after=tok, b_slots=True)
    grad_x, dg_pre0 = _norm_bwd1(dh1, x, dxn0, g_pre0, "norm_pre0_bwd")

    dbb_re = _block_diag_extract(dbd_re, p, S5_GROUP).reshape(g * p, S5_GROUP)
    dbb_im = _block_diag_extract(dbd_im, p, S5_GROUP).reshape(g * p, S5_GROUP)
    dcr_col, dci_col, db_re, db_im = _s5_bbar_bwd(cr_col, ci_col, b_re2, b_im2, dbb_re, dbb_im)
    da_re, da_im, dldt = _s5_disc_bwd(a_re, a_im, ldt, dab_re.reshape(g, p), dab_im.reshape(g, p),
                                      dcr_col.reshape(g, p), dci_col.reshape(g, p))
    dc_re = _block_diag_extract(dcd_re, S5_GROUP, p)
    dc_im = -_block_diag_extract(dcd_im, S5_GROUP, p)

    small = dict(
        norm_pre=jnp.concatenate([dg_pre0, dg_pre1], axis=0), norm_post=jnp.concatenate([dg_post0, dg_post1], axis=0),
        s5_a_re=da_re, s5_a_im=da_im, s5_log_dt=dldt.reshape(g), s5_b_re=db_re.reshape(g, p, S5_GROUP),
        s5_b_im=db_im.reshape(g, p, S5_GROUP), s5_c_re=dc_re, s5_c_im=dc_im, s5_d=dd.reshape(-1),
        s5_b_glu=db_glu.reshape(-1), kv_norm=dg_kv.reshape(-1), kv_b_f=db_f[0, :nh])
    return loss, grad_x, small


_BIG = ("s5_w_in", "s5_w_glu", "s5_w_out", "kv_w", "fox_w_in", "fox_w_out")
_COL_SHARDED = ("s5_w_in", "fox_w_in")
_SMALL = ("norm_pre", "norm_post", "s5_a_re", "s5_a_im", "s5_log_dt", "s5_b_re", "s5_b_im", "s5_c_re", "s5_c_im",
          "s5_d", "s5_b_glu", "kv_norm", "kv_b_f")
_SMALL_SHARDED = ("s5_d", "s5_b_glu")
_PACK_QUANTUM = SUBLANES * LANES
_WEIGHTS = ('norm_pre', 'norm_post', 's5_w_in', 's5_a_re', 's5_a_im', 's5_log_dt', 's5_b_re', 's5_b_im', 's5_c_re', 's5_c_im',
            's5_d', 's5_w_glu', 's5_b_glu', 's5_w_out', 'kv_norm', 'kv_w', 'kv_b_f', 'fox_w_in', 'fox_w_out')


def _full_from_slots(name, slots):
    n, r, c = slots.shape
    if name in _COL_SHARDED:
        return slots.transpose(1, 0, 2).reshape(r, n * c)
    return slots.reshape(n * r, c)


def _slots_from_full(name, full):
    if name in _COL_SHARDED:
        r, nc = full.shape
        return full.reshape(r, N_DEV, nc // N_DEV).transpose(1, 0, 2)
    nr, c = full.shape
    return full.reshape(N_DEV, nr // N_DEV, c)


def _groups_last(shape):
    return len(shape) >= 3 and shape[-1] < LANES and shape[-3] % LANES == 0


def _pack(vals):
    parts = []
    for v in vals:
        flat = jnp.moveaxis(v, -3, -1).reshape(-1) if _groups_last(v.shape) else v.reshape(-1)
        parts.append(jnp.pad(flat, (0, (-flat.shape[0]) % _PACK_QUANTUM)))
    total = sum(p.shape[0] for p in parts)
    parts.append(jnp.zeros(((-total) % (N_DEV * _PACK_QUANTUM),), F32))
    return jnp.concatenate(parts).reshape(-1, LANES)


def _unpack(packed, shapes):
    flat = packed.reshape(-1)
    out, off = [], 0
    for sh in shapes:
        n = math.prod(sh)
        piece = flat[off:off + n]
        if _groups_last(sh):
            piece = jnp.moveaxis(piece.reshape(sh[:-3] + sh[-2:] + sh[-3:-2]), -1, -3)
        out.append(piece.reshape(sh))
        off += n + (-n) % _PACK_QUANTUM
    return out


class _Comm:
    _GROUPS = (("s5_w_in",) + _SMALL_SHARDED, ("s5_w_glu", "s5_w_out"), ("kv_w", "fox_w_in"), ("fox_w_out",))
    _SLOT_FORM = ("s5_w_in", "fox_w_in")

    def __init__(self, shards, vectors, early=()):
        self._shards = {**shards, **vectors}
        self._full, self._gathers = {}, {}
        self._early = list(early)
        self.token = jnp.zeros((), F32)
        for group in self._GROUPS[:-1]:
            self.token = self.token + self._start(group, ())[0, 0]
        self.late_token = None
        self._sent = []

    def _start(self, group, after):
        state, tok = _exchange_start([self._shards[n] for n in group], False, "gather_start_" + group[0], after,
                                     peers=_CHIP_PEERS)
        self._gathers[group] = state
        return tok

    def vector(self, name):
        return self._full[name]

    def weight(self, name, after):
        if name not in self._full:
            group = next(g for g in self._GROUPS if name in g)
            if group == self._GROUPS[0]:
                after = (list(after) if isinstance(after, (list, tuple)) else [after]) + self._early
            slots = _exchange_wait(self._gathers.pop(group), after, "gather_wait_" + group[0])
            slots = _forward_to_sibling(slots, "gather_forward_" + group[0])
            for n, sl in zip(group, slots):
                if n in _SMALL_SHARDED:
                    self._full[n] = sl.reshape(-1)
                else:
                    self._full[n] = sl if n in self._SLOT_FORM else _full_from_slots(n, sl)
            if group == self._GROUPS[-2]:
                self.late_token = self._start(self._GROUPS[-1], [slots[0]])[0, 0]
        return self._full[name]

    def send_grads(self, grads, name):
        names = list(grads)
        slots = [grads[n] if grads[n].ndim == 3 else _slots_from_full(n, grads[n]).astype(BF16) for n in names]
        state, tok = _exchange_start(slots, True, name + "_start")
        self._sent.append((names, state, name + "_wait"))
        return tok

    def received_grads(self, group, after):
        names, state, name = self._sent[group]
        return list(zip(names, _exchange_wait(state, after, name)))


def kernel(x, norm_pre, norm_post, s5_w_in, s5_a_re, s5_a_im, s5_log_dt, s5_b_re, s5_b_im, s5_c_re, s5_c_im, s5_d, s5_w_glu, s5_b_glu, s5_w_out, kv_norm, kv_w, kv_b_f, fox_w_in, fox_w_out, loss_target, m_norm_pre, m_norm_post, m_s5_w_in, m_s5_a_re, m_s5_a_im, m_s5_log_dt, m_s5_b_re, m_s5_b_im, m_s5_c_re, m_s5_c_im, m_s5_d, m_s5_w_glu, m_s5_b_glu, m_s5_w_out, m_kv_norm, m_kv_w, m_kv_b_f, m_fox_w_in, m_fox_w_out, v_norm_pre, v_norm_post, v_s5_w_in, v_s5_a_re, v_s5_a_im, v_s5_log_dt, v_s5_b_re, v_s5_b_im, v_s5_c_re, v_s5_c_im, v_s5_d, v_s5_w_glu, v_s5_b_glu, v_s5_w_out, v_kv_norm, v_kv_w, v_kv_b_f, v_fox_w_in, v_fox_w_out):
    env = dict(locals())
    wts = {n: env[n] for n in _WEIGHTS}
    mom = {n: env["m_" + n] for n in _WEIGHTS}
    var = {n: env["v_" + n] for n in _WEIGHTS}
    me = 4 * lax.axis_index("x") + 2 * lax.axis_index("y") + lax.axis_index("c")
    shard2d = {n: (wts[n].T if n == "kv_w" else wts[n].reshape(wts[n].shape[-2:])) for n in _BIG}
    full_shape = {n: ((wts[n].size * N_DEV,) if n in _SMALL_SHARDED else wts[n].shape) for n in _SMALL}

    def spread(n, v):
        if n not in _SMALL_SHARDED:
            return v
        flat = v.reshape(-1)
        return lax.dynamic_update_slice(jnp.zeros(full_shape[n], F32), flat, (me * flat.shape[0],))

    packed = [_pack([spread(n, src[n]) for n in _SMALL] + [jnp.zeros((1,), F32)]) for src in (wts, mom, var)]
    comm = _Comm({n: _cast_bf16(shard2d[n], "cast_" + n) for n in _BIG}, {n: wts[n].reshape(1, -1) for n in _SMALL_SHARDED}, packed)

    loss_local, grad_x, small = _local_step(
        x[0], loss_target[0], norm_pre, norm_post, kv_norm, kv_b_f, s5_a_re[0], s5_a_im[0], s5_log_dt[0],
        s5_b_re[0], s5_b_im[0], s5_c_re[0], s5_c_im[0], comm)

    small_pack = _pack([small[n] for n in _SMALL] + [loss_local.reshape(1)])
    slice_rows = small_pack.shape[0] // N_DEV
    small_state, small_tok = _exchange_start([small_pack.reshape(N_DEV, slice_rows, LANES)], True, "reduce_small_start")

    res = {}

    def finish(group, after):
        for n, recv in comm.received_grads(group, after):
            if n == "kv_w":
                res[n] = [o.T for o in _adamw(recv, wts[n].T, mom[n].T, var[n].T, "adamw_" + n)]
            else:
                res[n] = _adamw(recv, wts[n], mom[n], var[n], "adamw_" + n)

    finish(0, [small_tok, grad_x])
    my_sum = _sum_parts(_exchange_wait(small_state, res["kv_w"][0], "reduce_small_wait")[0], "sum_small")
    gather_state, gather_tok = _exchange_start([my_sum], False, "gather_small_start")
    finish(1, gather_tok)
    finish(2, gather_tok)
    g_all = _exchange_wait(gather_state, res["s5_w_in"][0], "gather_small_wait")[0].reshape(1, small_pack.shape[0], LANES)
    outs = _adamw(g_all, *packed, "adamw_small")
    unpacked = [_unpack(o, [full_shape[n] for n in _SMALL] + [(1,)]) for o in outs]
    loss = unpacked[0][-1][0]
    for i, n in enumerate(_SMALL):
        vals = [u[i] for u in unpacked]
        if n in _SMALL_SHARDED:
            k = wts[n].size
            vals = [lax.dynamic_slice(v, (me * k,), (k,)) for v in vals]
        res[n] = [v.reshape(wts[n].shape) for v in vals]

    return (loss, grad_x[None], *[res[n][0] for n in _WEIGHTS], *[res[n][1] for n in _WEIGHTS],
            *[res[n][2] for n in _WEIGHTS], *[res[n][3] for n in _WEIGHTS])
```

```python
import math

import jax
import jax.numpy as jnp
from jax import lax
from jax.experimental import pallas as pl
from jax.experimental.pallas import tpu as pltpu

F32 = jnp.float32
BF16 = jnp.bfloat16

N_DEV = 8
MESH_AXES = ("x", "y", "c")
S5_GROUP = 16
S5_STATE = 64
LANES = 128
SUBLANES = 8
GROUPS_PER_BLOCK = LANES // S5_GROUP
BLOCK_STATE = GROUPS_PER_BLOCK * S5_STATE
N_SEG = SUBLANES
HEAD_DIM = 128
RMS_EPS = 1e-6
NEG_INF = -1e30
LOG2E = math.log2(math.e)
ADAM_LR = 0.001
ADAM_B1 = 0.9
ADAM_B2 = 0.999
ADAM_EPS = 1e-08
ADAM_WD = 0.01
ADAM_STEP = 10
VMEM_LIMIT = 56 * 1024 * 1024


def _tile(n, pref, quantum=LANES):
    if n <= pref:
        return n
    t = (pref // quantum) * quantum
    while t >= quantum:
        if n % t == 0:
            return t
        t -= quantum
    return n


def _cparams(*sem):
    return pltpu.CompilerParams(dimension_semantics=sem if sem else None, vmem_limit_bytes=VMEM_LIMIT)


_DOT_DIMS = {"nn": ((1,), (0,)), "nt": ((1,), (1,)), "tn": ((0,), (0,))}


def _mm(a, b, mode, out_dtype, name, add=None, scale=None, b_cols=None, after=None, col_slots=False, b_slots=False,
        b_rows=None, epilogue=None):
    slot_w = b.shape[2] if b_slots else None
    b2d = (b.shape[1], b.shape[0] * b.shape[2]) if b_slots else b.shape
    b_shape = b2d if b_cols is None else (b2d[0], b_cols[1])
    if b_rows is not None:
        b_shape = (b_rows, b_shape[1])
    if mode == "nn":
        (M, K), (K2, N) = a.shape, b_shape
    elif mode == "nt":
        (M, K), (N, K2) = a.shape, b_shape
    else:
        (K, M), (K2, N) = a.shape, b_shape
    assert K == K2, (name, a.shape, b_shape)
    tm, tn, tk = _tile(M, 1024 if K <= 2048 else 512), (N // N_DEV if col_slots else _tile(N, 1024)), _tile(K, 4096)
    if b_slots and mode == "nn":
        tn = slot_w
    nk = K // tk
    dims = (_DOT_DIMS[mode], ((), ()))
    col0 = 0
    if b_cols is not None:
        assert mode != "tn" and b_cols[0] % (tn if mode == "nn" else tk) == 0
        col0 = b_cols[0] // (tn if mode == "nn" else tk)
    assert not b_slots or (mode == "nn" or (mode == "nt" and nk == 1 and b_cols is None))

    def body(*refs):
        a_ref, b_ref = refs[:2]
        c_ref = refs[2] if add is not None else None
        e_ref = refs[2 + (add is not None)] if epilogue is not None else None
        o_ref = refs[2 + (add is not None) + (epilogue is not None) + (after is not None)]
        if b_slots and mode == "nt":
            part = lax.dot_general(a_ref[:, :slot_w], b_ref[0], dims, preferred_element_type=F32)
            for sl in range(1, b_ref.shape[0]):
                part += lax.dot_general(a_ref[:, sl * slot_w:(sl + 1) * slot_w], b_ref[sl], dims, preferred_element_type=F32)
        else:
            part = lax.dot_general(a_ref[...], b_ref[...], dims, preferred_element_type=F32)

        def finish(r):
            if scale is not None:
                r = r * scale
            if add is not None:
                r = r + c_ref[...]
            if epilogue is not None:
                r = epilogue[0](r, e_ref[...])
            o_ref[...] = r.astype(out_dtype)

        if nk == 1:
            finish(part)
            return
        acc = refs[-1]
        k = pl.program_id(2)

        @pl.when(k == 0)
        def _():
            acc[...] = part

        @pl.when(jnp.logical_and(k > 0, k < nk - 1))
        def _():
            acc[...] += part

        @pl.when(k == nk - 1)
        def _():
            finish(acc[...] + part)

    if mode == "tn":
        a_spec = pl.BlockSpec((tk, tm), lambda i, j, k: (k, i))
    else:
        a_spec = pl.BlockSpec((tm, tk), lambda i, j, k: (i, k))
    if b_slots and mode == "nn":
        b_spec = pl.BlockSpec((None, tk, tn), lambda i, j, k: (j + col0, k, 0))
    elif b_slots:
        b_spec = pl.BlockSpec((b.shape[0], tn, slot_w), lambda i, j, k: (0, j, 0))
    elif mode == "nt":
        b_spec = pl.BlockSpec((tn, tk), lambda i, j, k: (j, k + col0))
    else:
        b_spec = pl.BlockSpec((tk, tn), lambda i, j, k: (k, j + col0))
    o_spec = pl.BlockSpec((tm, tn), lambda i, j, k: (i, j))
    in_specs = [a_spec, b_spec] + ([o_spec] if add is not None else [])
    args = (a, b) + ((add,) if add is not None else ())
    if epilogue is not None:
        in_specs.append(o_spec)
        args += (epilogue[1],)
    if after is not None:
        in_specs.append(pl.BlockSpec(after.shape, lambda i, j, k: (0, 0)))
        args += (after,)
    out_shape = jax.ShapeDtypeStruct((M, N), out_dtype)
    if col_slots:
        assert add is None
        o_spec = pl.BlockSpec((None, tm, tn), lambda i, j, k: (j, i, 0))
        out_shape = jax.ShapeDtypeStruct((N_DEV, M, tn), out_dtype)
    return pl.pallas_call(
        body, name=name, grid=(M // tm, N // tn, nk),
        in_specs=in_specs, out_specs=o_spec,
        out_shape=out_shape,
        scratch_shapes=[pltpu.VMEM((tm, tn), F32)] if nk > 1 else [],
        compiler_params=_cparams("parallel", "parallel", "arbitrary"),
    )(*args)


class _NatIn:
    def __init__(self, ref):
        self.ref = ref

    def __getitem__(self, idx):
        v = jnp.swapaxes(self.ref[...], 0, 1)
        return v.reshape(v.shape[0] * N_SEG, v.shape[2])


class _NatOut:
    def __init__(self, ref):
        self.ref = ref

    def __setitem__(self, idx, val):
        self.ref[...] = jnp.swapaxes(val.reshape(val.shape[0] // N_SEG, N_SEG, val.shape[1]), 0, 1)


def _rowcall(body, name, n_rows, ins, outs, tile_rows=256):
    tr = _tile(n_rows, tile_rows, SUBLANES * 2)
    n_in = len(ins)
    in_kinds = [k for _, k in ins]
    kinds = [k for _, _, k in outs]

    def kern(*refs):
        @pl.when(pl.program_id(0) == 0)
        def _():
            for r, kind in zip(refs[n_in:], kinds):
                if kind == "acc":
                    r[...] = jnp.zeros_like(r)

        wrapped = [_NatIn(r) if k == "nat" else r for r, k in zip(refs[:n_in], in_kinds)]
        wrapped += [_NatOut(r) if k == "nat" else r for r, k in zip(refs[n_in:], kinds)]
        body(*wrapped)

    in_specs, args = [], []
    for arr, kind in ins:
        if kind == "row":
            in_specs.append(pl.BlockSpec((tr, arr.shape[1]), lambda i: (i, 0)))
        elif kind == "nat":
            in_specs.append(pl.BlockSpec((N_SEG, tr // N_SEG, arr.shape[1]), lambda i: (0, i, 0)))
            arr = arr.reshape(N_SEG, n_rows // N_SEG, arr.shape[1])
        else:
            in_specs.append(pl.BlockSpec(arr.shape, lambda i, nd=arr.ndim: (0,) * nd))
        args.append(arr)
    out_specs, out_shape = [], []
    for width, dtype, kind in outs:
        if kind == "row":
            out_specs.append(pl.BlockSpec((tr, width), lambda i: (i, 0)))
            out_shape.append(jax.ShapeDtypeStruct((n_rows, width), dtype))
        elif kind == "right":
            out_specs.append(pl.BlockSpec((tr, width), lambda i: (i, 1)))
            out_shape.append(jax.ShapeDtypeStruct((n_rows, 2 * width), dtype))
        elif kind == "nat":
            out_specs.append(pl.BlockSpec((N_SEG, tr // N_SEG, width), lambda i: (0, i, 0)))
            out_shape.append(jax.ShapeDtypeStruct((N_SEG, n_rows // N_SEG, width), dtype))
        else:
            out_specs.append(pl.BlockSpec((1, width), lambda i: (0, 0)))
            out_shape.append(jax.ShapeDtypeStruct((1, width), F32))
    res = pl.pallas_call(
        kern, name=name, grid=(n_rows // tr,), in_specs=in_specs, out_specs=out_specs, out_shape=out_shape,
        compiler_params=_cparams("arbitrary"),
    )(*args)
    return [r.reshape(n_rows, r.shape[2]) if k == "nat" else r for r, k in zip(res, kinds)]


def _rstd(x):
    return lax.rsqrt(jnp.mean(x * x, axis=-1, keepdims=True) + RMS_EPS)


def _rms_bwd(x, g, dy):
    xh = x * _rstd(x)
    dxh = dy * g
    dx = _rstd(x) * (dxh - xh * jnp.mean(dxh * xh, axis=-1, keepdims=True))
    return dx, jnp.sum(dy * xh, axis=0, keepdims=True)


def _silu(z):
    return z * jax.nn.sigmoid(z)


def _norm_cast(x, g, name, x_kind="row"):
    def body(x_ref, g_ref, o_ref):
        x = x_ref[...]
        o_ref[...] = (x * _rstd(x) * g_ref[...]).astype(BF16)

    return _rowcall(body, name, x.shape[0], [(x, x_kind), (g, "full")], [(x.shape[1], BF16, "row")])[0]


def _resid_norm2(x, o, g_post, g_kv, g_pre, name):
    def body(x_ref, o_ref, go_ref, gk_ref, gp_ref, h_ref, nk_ref, np_ref):
        o = o_ref[...]
        h = x_ref[...] + o * _rstd(o) * go_ref[...]
        h_ref[...] = h
        hn = h * _rstd(h)
        nk_ref[...] = (hn * gk_ref[...]).astype(BF16)
        np_ref[...] = (hn * gp_ref[...]).astype(BF16)

    d = x.shape[1]
    return _rowcall(body, name, x.shape[0], [(x, "nat"), (o, "row"), (g_post, "full"), (g_kv, "full"), (g_pre, "full")],
                    [(d, F32, "nat"), (d, BF16, "nat"), (d, BF16, "nat")])


def _post_norm_loss(o, g, h1, target, name):
    d = o.shape[1]

    def body(o_ref, g_ref, h_ref, t_ref, dh_ref, do_ref, acc_ref, dg_ref):
        o = o_ref[...]
        e = h_ref[...] + o * _rstd(o) * g_ref[...] - t_ref[...]
        dh = e * (1.0 / d)
        dh_ref[...] = dh
        acc_ref[...] += jnp.sum(e * e, axis=0, keepdims=True)
        dx, dg = _rms_bwd(o, g_ref[...], dh)
        do_ref[...] = dx.astype(BF16)
        dg_ref[...] += dg

    return _rowcall(body, name, o.shape[0], [(o, "row"), (g, "full"), (h1, "row"), (target, "row")],
                    [(d, F32, "row"), (d, BF16, "row"), (d, F32, "acc"), (d, F32, "acc")])


def _gate_bwd(d_oz, o, z, name):
    def body(d_ref, o_ref, z_ref, do_ref, dz_ref):
        _, vjp = jax.vjp(lambda o, z: o * _silu(z), o_ref[...].astype(F32), z_ref[...].astype(F32))
        do, dz = vjp(d_ref[...].astype(F32))
        do_ref[...] = do.astype(BF16)
        dz_ref[...] = dz.astype(BF16)

    w = o.shape[1]
    return _rowcall(body, name, o.shape[0], [(d_oz, "row"), (o, "row"), (z, "row")], [(w, BF16, "row"), (w, BF16, "right")])


def _norm_bwd2(dh2, h1, dxn1, dhn_kv, g_pre, g_kv, o0, g_post0, name):
    def body(dh2_ref, h_ref, d1_ref, dk_ref, gp_ref, gk_ref, o_ref, go_ref, dh1_ref, do_ref, dgp_ref, dgk_ref, dgo_ref):
        h = h_ref[...]
        dx1, dg1 = _rms_bwd(h, gp_ref[...], d1_ref[...].astype(F32))
        dxk, dgk = _rms_bwd(h, gk_ref[...], dk_ref[...].astype(F32))
        dh1 = dh2_ref[...] + dx1 + dxk
        dh1_ref[...] = dh1
        dgp_ref[...] += dg1
        dgk_ref[...] += dgk
        dxo, dgo = _rms_bwd(o_ref[...], go_ref[...], dh1)
        do_ref[...] = dxo.astype(BF16)
        dgo_ref[...] += dgo

    d = h1.shape[1]
    return _rowcall(body, name, h1.shape[0],
                    [(dh2, "nat"), (h1, "nat"), (dxn1, "nat"), (dhn_kv, "nat"), (g_pre, "full"), (g_kv, "full"),
                     (o0, "row"), (g_post0, "full")],
                    [(d, F32, "nat"), (d, BF16, "row"), (d, F32, "acc"), (d, F32, "acc"), (d, F32, "acc")])


def _norm_bwd1(dres, x, dxn, g, name):
    def body(dr_ref, x_ref, dn_ref, g_ref, dx_ref, dg_ref):
        dx, dg = _rms_bwd(x_ref[...], g_ref[...], dn_ref[...].astype(F32))
        dx_ref[...] = dr_ref[...] + dx
        dg_ref[...] += dg

    d = x.shape[1]
    return _rowcall(body, name, x.shape[0], [(dres, "nat"), (x, "nat"), (dxn, "row"), (g, "full")],
                    [(d, F32, "nat"), (d, F32, "acc")])


def _s5_gate(y_ssm, gp, b_glu, z, name):
    def body(y_ref, gp_ref, b_ref, z_ref, o_ref):
        yg = jax.nn.gelu(y_ref[...].astype(F32))
        o_ref[...] = (yg * jax.nn.sigmoid(gp_ref[...] + b_ref[...]) * _silu(z_ref[...].astype(F32))).astype(BF16)

    return _rowcall(body, name, y_ssm.shape[0], [(y_ssm, "row"), (gp, "row"), (b_glu, "full"), (z, "row")],
                    [(y_ssm.shape[1], BF16, "row")])[0]


def _s5_gate_bwd(dy3, y_ssm, gp, b_glu, z, name):
    def body(d_ref, y_ref, gp_ref, b_ref, z_ref, dz_ref, dgp_ref, dyg_ref, db_ref):
        yg = jax.nn.gelu(y_ref[...].astype(F32))
        _, vjp = jax.vjp(lambda yg, gp, z: yg * jax.nn.sigmoid(gp) * _silu(z), yg, gp_ref[...] + b_ref[...],
                         z_ref[...].astype(F32))
        dyg, dgp, dz = vjp(d_ref[...].astype(F32))
        dz_ref[...] = dz.astype(BF16)
        dgp_ref[...] = dgp.astype(BF16)
        dyg_ref[...] = dyg
        db_ref[...] += jnp.sum(dgp, axis=0, keepdims=True)

    w = y_ssm.shape[1]
    return _rowcall(body, name, y_ssm.shape[0],
                    [(dy3, "row"), (y_ssm, "row"), (gp, "row"), (b_glu, "full"), (z, "row")],
                    [(w, BF16, "right"), (w, BF16, "row"), (w, F32, "row"), (w, F32, "acc")])


def _cast_bf16(x, name):
    r, c = x.shape
    by_cols = r % (2 * SUBLANES) != 0
    tr, tc = (r, _tile(c, 256)) if by_cols else (_tile(r, 512, 2 * SUBLANES), c)
    pos = (lambda i: (0, i)) if by_cols else (lambda i: (i, 0))

    def body(x_ref, o_ref):
        o_ref[...] = x_ref[...].astype(BF16)

    return pl.pallas_call(
        body, name=name, grid=(c // tc if by_cols else r // tr,),
        in_specs=[pl.BlockSpec((tr, tc), pos)], out_specs=pl.BlockSpec((tr, tc), pos),
        out_shape=jax.ShapeDtypeStruct((r, c), BF16), compiler_params=_cparams("parallel"),
    )(x)


def _concat_cast(a, b, name):
    def body(a_ref, b_ref, o_ref):
        w = a_ref.shape[1]
        o_ref[:, :w] = a_ref[...].astype(BF16)
        o_ref[:, w:] = b_ref[...].astype(BF16)

    return _rowcall(body, name, a.shape[0], [(a, "row"), (b, "row")], [(a.shape[1] + b.shape[1], BF16, "row")])[0]


def _disc(ar, ai, ldt):
    dt = jnp.exp(ldt)
    mag = jnp.exp(ar * dt)
    abr = mag * jnp.cos(ai * dt)
    abi = mag * jnp.sin(ai * dt)
    den = ar * ar + ai * ai
    nr = abr - 1.0
    return abr, abi, (nr * ar + abi * ai) / den, (abi * ar - nr * ai) / den


def _s5_disc_fwd(a_re, a_im, ldt):
    def body(ar, ai, ld, o1, o2, o3, o4):
        o1[...], o2[...], o3[...], o4[...] = _disc(ar[...], ai[...], ld[...])

    sh = jax.ShapeDtypeStruct(a_re.shape, F32)
    return pl.pallas_call(body, name="s5_disc_fwd", out_shape=(sh, sh, sh, sh))(a_re, a_im, ldt)


def _s5_disc_bwd(a_re, a_im, ldt, d_abr, d_abi, d_cr, d_ci):
    def body(ar, ai, ld, g1, g2, g3, g4, o1, o2, o3):
        _, vjp = jax.vjp(_disc, ar[...], ai[...], ld[...])
        o1[...], o2[...], o3[...] = vjp((g1[...], g2[...], g3[...], g4[...]))

    sh = jax.ShapeDtypeStruct(a_re.shape, F32)
    return pl.pallas_call(body, name="s5_disc_bwd", out_shape=(sh, sh, jax.ShapeDtypeStruct(ldt.shape, F32)))(
        a_re, a_im, ldt, d_abr, d_abi, d_cr, d_ci)


def _bbar(cr, ci, br, bi):
    return cr * br - ci * bi, cr * bi + ci * br


def _s5_bbar_fwd(cr_col, ci_col, b_re, b_im):
    def body(cr, ci, br, bi, o1, o2):
        o1[...], o2[...] = _bbar(cr[...], ci[...], br[...], bi[...])

    w = b_re.shape[1]
    return _rowcall(body, "s5_bbar_fwd", b_re.shape[0], [(cr_col, "row"), (ci_col, "row"), (b_re, "row"), (b_im, "row")],
                    [(w, F32, "row"), (w, F32, "row")], tile_rows=1024)


def _s5_bbar_bwd(cr_col, ci_col, b_re, b_im, d_re, d_im):
    def body(cr, ci, br, bi, g1, g2, o1, o2, o3, o4):
        _, vjp = jax.vjp(_bbar, cr[...], ci[...], br[...], bi[...])
        o1[...], o2[...], o3[...], o4[...] = vjp((g1[...], g2[...]))

    w = b_re.shape[1]
    return _rowcall(body, "s5_bbar_bwd", b_re.shape[0],
                    [(cr_col, "row"), (ci_col, "row"), (b_re, "row"), (b_im, "row"), (d_re, "row"), (d_im, "row")],
                    [(1, F32, "row"), (1, F32, "row"), (w, F32, "row"), (w, F32, "row")], tile_rows=1024)


def _block_diag(t):
    g, a, b = t.shape
    nb = g // GROUPS_PER_BLOCK
    t4 = t.reshape(nb, GROUPS_PER_BLOCK, a, b).transpose(0, 1, 3, 2)
    eye = jnp.eye(GROUPS_PER_BLOCK, dtype=t.dtype)
    return (t4[:, :, :, None, :] * eye[None, :, None, :, None]).reshape(nb, GROUPS_PER_BLOCK * b, GROUPS_PER_BLOCK * a)


def _block_diag_extract(d, a, b):
    nb = d.shape[0]
    d5 = d.reshape(nb, GROUPS_PER_BLOCK, b, GROUPS_PER_BLOCK, a)
    diag = jnp.stack([d5[:, g, :, g, :] for g in range(GROUPS_PER_BLOCK)], axis=1)
    return diag.transpose(0, 1, 3, 2).reshape(nb * GROUPS_PER_BLOCK, a, b)


def _scan_step(ar, ai, hr, hi, xr, xi):
    return ar * hr - ai * hi + xr, ar * hi + ai * hr + xi


def _s5_blocks_per_step(nb, full):
    want = 2 if full else 4
    while nb % want:
        want //= 2
    return want


def _s5_scan_fwd(u, bd_re, bd_im, cd_re, cd_im, ab_re, ab_im, init_re, init_im, d_row, full, name):
    s, w = u.shape
    nb = w // LANES
    rows = _tile(s, 512, SUBLANES)
    nc = s // rows
    steps = rows // N_SEG
    ns = nb * BLOCK_STATE

    nblk = _s5_blocks_per_step(nb, full)

    def body(u_ref, bdr, bdi, cdr, cdi, ar_ref, ai_ref, ir_ref, ii_ref, d_ref, *outs):
        if full:
            y_ref, yg_ref, hr_out, hi_out, er_ref, ei_ref, hr_ref, hi_ref, cr, ci = outs
        else:
            er_ref, ei_ref, hr_ref, hi_ref, cr, ci = outs
        c = pl.program_id(1)
        cols = lambda b, width: slice(b * width, (b + 1) * width)

        @pl.when(c == 0)
        def _():
            cr[...] = ir_ref[...]
            ci[...] = ii_ref[...]

        for b in range(nblk):
            ub = u_ref[:, cols(b, LANES)].astype(BF16)
            hr_ref[:, cols(b, BLOCK_STATE)] = jnp.dot(ub, bdr[b], preferred_element_type=F32)
            hi_ref[:, cols(b, BLOCK_STATE)] = jnp.dot(ub, bdi[b], preferred_element_type=F32)
        ar, ai = ar_ref[...], ai_ref[...]

        hr, hi = cr[...], ci[...]
        for j in range(steps):
            rows_j = pl.ds(j * N_SEG, N_SEG)
            hr, hi = _scan_step(ar, ai, hr, hi, hr_ref[rows_j, :], hi_ref[rows_j, :])
            hr_ref[rows_j, :] = hr
            hi_ref[rows_j, :] = hi
        cr[...] = hr
        ci[...] = hi
        if full:
            hr_out[...] = hr_ref[...].astype(BF16)
            hi_out[...] = hi_ref[...].astype(BF16)
            for b in range(nblk):
                st_b, ln_b = cols(b, BLOCK_STATE), cols(b, LANES)
                y = (jnp.dot(hr_out[:, st_b], cdr[b], preferred_element_type=F32)
                     + jnp.dot(hi_out[:, st_b], cdi[b], preferred_element_type=F32)
                     + d_ref[:, ln_b] * u_ref[:, ln_b])
                y_ref[:, ln_b] = y.astype(BF16)
                yg_ref[:, ln_b] = jax.nn.gelu(y).astype(BF16)

        @pl.when(c == nc - 1)
        def _():
            er_ref[...] = hr
            ei_ref[...] = hi

    lanes, states = LANES * nblk, BLOCK_STATE * nblk
    blk3 = lambda a: pl.BlockSpec((nblk,) + a.shape[1:], lambda k, c: (k, 0, 0))
    seg = pl.BlockSpec((N_SEG, states), lambda k, c: (0, k))
    st = pl.BlockSpec((rows, states), lambda k, c: (c, k))
    in_specs = [pl.BlockSpec((rows, lanes), lambda k, c: (c, k)), blk3(bd_re), blk3(bd_im), blk3(cd_re), blk3(cd_im),
                seg, seg, seg, seg, pl.BlockSpec((1, lanes), lambda k, c: (0, k))]
    seg_shape = jax.ShapeDtypeStruct((N_SEG, ns), F32)
    st_shape = jax.ShapeDtypeStruct((s, ns), BF16)
    scratch = [pltpu.VMEM((rows, states), F32)] * 2 + [pltpu.VMEM((N_SEG, states), F32)] * 2
    if full:
        ych = pl.BlockSpec((rows, lanes), lambda k, c: (c, k))
        out_specs = [ych, ych, st, st, seg, seg]
        out_shape = [jax.ShapeDtypeStruct((s, w), BF16), jax.ShapeDtypeStruct((s, w), BF16), st_shape, st_shape, seg_shape, seg_shape]
    else:
        out_specs = [seg, seg]
        out_shape = [seg_shape, seg_shape]
    return pl.pallas_call(
        body, name=name, grid=(nb // nblk, nc), in_specs=in_specs, out_specs=out_specs, out_shape=out_shape,
        scratch_shapes=scratch, compiler_params=_cparams("parallel", "arbitrary"),
    )(u, bd_re, bd_im, cd_re, cd_im, ab_re, ab_im, init_re, init_im, d_row)


def _s5_seg_fix(e_re, e_im, ab_re, ab_im, seg_len, reverse, name):
    assert seg_len & (seg_len - 1) == 0

    def body(er, ei, ar, ai, o_re, o_im):
        pr, pi = ar[0:1, :], ai[0:1, :]
        for _ in range(int(math.log2(seg_len))):
            pr, pi = pr * pr - pi * pi, 2.0 * pr * pi
        tr = jnp.zeros_like(pr)
        ti = jnp.zeros_like(pr)
        order = list(range(N_SEG - 1, -1, -1)) if reverse else list(range(N_SEG))
        for n, sgm in enumerate(order):
            o_re[sgm:sgm + 1, :] = tr
            o_im[sgm:sgm + 1, :] = ti
            if n < N_SEG - 1:
                tr, ti = _scan_step(pr, pi, tr, ti, er[sgm:sgm + 1, :], ei[sgm:sgm + 1, :])

    sh = jax.ShapeDtypeStruct(e_re.shape, F32)
    return pl.pallas_call(body, name=name, out_shape=(sh, sh))(e_re, e_im, ab_re, ab_im)


def _s5_scan_bwd(dy, u, h_re, h_im, bd_re, bd_im, cd_re, cd_im, ab_re, ab_imn, gin_re, gin_im, d_row, full, name, duz=None):
    s, w = u.shape
    nb = w // LANES
    rows = _tile(s, 512, SUBLANES)
    nc = s // rows
    steps = rows // N_SEG
    ns = nb * BLOCK_STATE

    nblk = _s5_blocks_per_step(nb, full)

    def body(dy_ref, u_ref, hr_ref, hi_ref, bdr, bdi, cdr, cdi, ar_ref, ai_ref, ir_ref, ii_ref, d_ref, *outs):
        if full:
            _, du_ref, dbr_ref, dbi_ref, dcr_ref, dci_ref, dar_ref, dai_ref, dd_ref, gr, gi, accr, acci = outs
        else:
            er_ref, ei_ref, gr, gi = outs
        c = pl.program_id(1)
        cols = lambda b, width: slice(b * width, (b + 1) * width)

        @pl.when(c == 0)
        def _():
            gr[pl.ds(rows, N_SEG), :] = ir_ref[...]
            gi[pl.ds(rows, N_SEG), :] = ii_ref[...]
            if full:
                for r in (dbr_ref, dbi_ref, dcr_ref, dci_ref, dd_ref, accr, acci):
                    r[...] = jnp.zeros_like(r)

        nt = (_DOT_DIMS["nt"], ((), ()))
        tn = (_DOT_DIMS["tn"], ((), ()))
        for b in range(nblk):
            dyb = dy_ref[:, cols(b, LANES)].astype(BF16)
            gr[pl.ds(0, rows), cols(b, BLOCK_STATE)] = lax.dot_general(dyb, cdr[b], nt, preferred_element_type=F32)
            gi[pl.ds(0, rows), cols(b, BLOCK_STATE)] = lax.dot_general(dyb, cdi[b], nt, preferred_element_type=F32)
        ar, ai = ar_ref[...], ai_ref[...]

        g0r, g0i = gr[pl.ds(rows, N_SEG), :], gi[pl.ds(rows, N_SEG), :]
        for j in range(steps - 1, -1, -1):
            rows_j = pl.ds(j * N_SEG, N_SEG)
            g0r, g0i = _scan_step(ar, ai, g0r, g0i, gr[rows_j, :], gi[rows_j, :])
            gr[rows_j, :] = g0r
            gi[rows_j, :] = g0i
        if full:
            for b in range(nblk):
                st_b, ln_b = cols(b, BLOCK_STATE), cols(b, LANES)
                hr, hi = hr_ref[:, st_b], hi_ref[:, st_b]
                gnr, gni = gr[pl.ds(N_SEG, rows), st_b], gi[pl.ds(N_SEG, rows), st_b]
                accr[:, st_b] += jnp.sum((gnr * hr + gni * hi).reshape(steps, N_SEG, BLOCK_STATE), axis=0)
                acci[:, st_b] += jnp.sum((gni * hr - gnr * hi).reshape(steps, N_SEG, BLOCK_STATE), axis=0)
                dyb = dy_ref[:, ln_b].astype(BF16)
                ub = u_ref[:, ln_b].astype(BF16)
                gbr, gbi = gr[pl.ds(0, rows), st_b].astype(BF16), gi[pl.ds(0, rows), st_b].astype(BF16)
                dcr_ref[b] += lax.dot_general(hr.astype(BF16), dyb, tn, preferred_element_type=F32)
                dci_ref[b] += lax.dot_general(hi.astype(BF16), dyb, tn, preferred_element_type=F32)
                dbr_ref[b] += lax.dot_general(ub, gbr, tn, preferred_element_type=F32)
                dbi_ref[b] += lax.dot_general(ub, gbi, tn, preferred_element_type=F32)
                du_ref[:, ln_b] = (lax.dot_general(gbr, bdr[b], nt, preferred_element_type=F32)
                                   + lax.dot_general(gbi, bdi[b], nt, preferred_element_type=F32)
                                   + d_ref[:, ln_b] * dy_ref[:, ln_b]).astype(BF16)
                dd_ref[:, ln_b] += jnp.sum(dy_ref[:, ln_b] * u_ref[:, ln_b], axis=0, keepdims=True)
        gr[pl.ds(rows, N_SEG), :] = g0r
        gi[pl.ds(rows, N_SEG), :] = g0i

        @pl.when(c == nc - 1)
        def _():
            if full:
                dar_ref[...] = jnp.sum(accr[...], axis=0, keepdims=True)
                dai_ref[...] = jnp.sum(acci[...], axis=0, keepdims=True)
            else:
                er_ref[...] = g0r
                ei_ref[...] = g0i

    lanes, states = LANES * nblk, BLOCK_STATE * nblk
    rev = lambda k, c: (nc - 1 - c, k)
    blk3 = lambda a: pl.BlockSpec((nblk,) + a.shape[1:], lambda k, c: (k, 0, 0))
    seg = pl.BlockSpec((N_SEG, states), lambda k, c: (0, k))
    st = pl.BlockSpec((rows, states), rev)
    ch = pl.BlockSpec((rows, lanes), rev)
    vec = pl.BlockSpec((1, lanes), lambda k, c: (0, k))
    if not full:
        st = pl.BlockSpec((rows, states), lambda k, c: (0, k))
    in_specs = [ch, ch if full else pl.BlockSpec((rows, lanes), lambda k, c: (0, k)), st, st,
                blk3(bd_re), blk3(bd_im), blk3(cd_re), blk3(cd_im), seg, seg, seg, seg, vec]
    args = [dy, u, h_re, h_im, bd_re, bd_im, cd_re, cd_im, ab_re, ab_imn, gin_re, gin_im, d_row]
    gbuf = [pltpu.VMEM((rows + N_SEG, states), F32)] * 2
    if full:
        row1 = pl.BlockSpec((1, states), lambda k, c: (0, k))
        out_specs = [ch, blk3(bd_re), blk3(bd_im), blk3(cd_re), blk3(cd_im), row1, row1, vec]
        out_shape = [jax.ShapeDtypeStruct(duz.shape, BF16),
                     jax.ShapeDtypeStruct(bd_re.shape, F32), jax.ShapeDtypeStruct(bd_im.shape, F32),
                     jax.ShapeDtypeStruct(cd_re.shape, F32), jax.ShapeDtypeStruct(cd_im.shape, F32),
                     jax.ShapeDtypeStruct((1, ns), F32), jax.ShapeDtypeStruct((1, ns), F32),
                     jax.ShapeDtypeStruct((1, w), F32)]
        scratch = gbuf + [pltpu.VMEM((N_SEG, states), F32)] * 2
        in_specs.append(pl.BlockSpec(memory_space=pl.ANY))
        args.append(duz)
        aliases = {len(args) - 1: 0}
    else:
        out_specs = [seg, seg]
        out_shape = [jax.ShapeDtypeStruct((N_SEG, ns), F32)] * 2
        scratch = gbuf
        aliases = {}
    return pl.pallas_call(
        body, name=name, grid=(nb // nblk, nc), in_specs=in_specs, out_specs=out_specs, out_shape=out_shape,
        input_output_aliases=aliases, scratch_shapes=scratch, compiler_params=_cparams("parallel", "arbitrary"),
    )(*args)


def _log_sigmoid(x):
    return jnp.minimum(x, 0.0) - jnp.log(1.0 + jnp.exp(-jnp.abs(x)))


def _tri(n, upper):
    r = lax.broadcasted_iota(jnp.int32, (n, n), 0)
    c = lax.broadcasted_iota(jnp.int32, (n, n), 1)
    return jnp.where((c >= r) if upper else (r >= c), 1.0, 0.0).astype(F32)


def _cum_fwd(f_logit, b_row, name):
    s, w = f_logit.shape
    t = _tile(s, 256, SUBLANES)

    def body(f_ref, b_ref, o_ref, carry):
        @pl.when(pl.program_id(0) == 0)
        def _():
            carry[...] = jnp.zeros_like(carry)

        lf = _log_sigmoid(f_ref[...] + b_ref[...])
        cum = jnp.dot(_tri(t, False), lf, precision=lax.Precision.HIGHEST, preferred_element_type=F32) + carry[...]
        o_ref[...] = cum * LOG2E
        carry[...] = cum[t - 1:t, :]

    return pl.pallas_call(
        body, name=name, grid=(s // t,),
        in_specs=[pl.BlockSpec((t, w), lambda i: (i, 0)), pl.BlockSpec((1, w), lambda i: (0, 0))],
        out_specs=pl.BlockSpec((t, w), lambda i: (i, 0)), out_shape=jax.ShapeDtypeStruct((s, w), F32),
        scratch_shapes=[pltpu.VMEM((1, w), F32)], compiler_params=_cparams("arbitrary"),
    )(f_logit, b_row)


def _cum_bwd(dcq, dck, f_logit, b_row, name):
    s, w = f_logit.shape
    t = _tile(s, 256, SUBLANES)
    nt = s // t

    def body(q_ref, k_ref, f_ref, b_ref, df_ref, db_ref, carry):
        @pl.when(pl.program_id(0) == 0)
        def _():
            carry[...] = jnp.zeros_like(carry)
            db_ref[...] = jnp.zeros_like(db_ref)

        dc = q_ref[...] - k_ref[...]
        rc = jnp.dot(_tri(t, True), dc, precision=lax.Precision.HIGHEST, preferred_element_type=F32) + carry[...]
        carry[...] = rc[0:1, :]
        df = rc * (1.0 - jax.nn.sigmoid(f_ref[...] + b_ref[...]))
        df_ref[...] = df.astype(BF16)
        db_ref[...] += jnp.sum(df, axis=0, keepdims=True)

    rev = pl.BlockSpec((t, w), lambda i: (nt - 1 - i, 0))
    one = pl.BlockSpec((1, w), lambda i: (0, 0))
    return pl.pallas_call(
        body, name=name, grid=(nt,), in_specs=[rev, rev, rev, one], out_specs=[rev, one],
        out_shape=[jax.ShapeDtypeStruct((s, w), BF16), jax.ShapeDtypeStruct((1, w), F32)],
        scratch_shapes=[pltpu.VMEM((1, w), F32)], compiler_params=_cparams("arbitrary"),
    )(dcq, dck, f_logit, b_row)


def _head_col(cum_tile, h):
    lane = lax.broadcasted_iota(jnp.int32, cum_tile.shape, 1)
    return jnp.sum(jnp.where(lane == h, cum_tile, 0.0), axis=1, keepdims=True)


def _attn_tiles(s):
    return _tile(s, 512, LANES)


def _exp2_rows(sc, sub):
    return jnp.concatenate([jnp.exp2(sc[:, b * LANES:(b + 1) * LANES] - sub) for b in range(sc.shape[1] // LANES)], axis=1)


def _row_of(rep):
    return jnp.transpose(rep)[0:1, :]


def _causal(sc, keys_on_rows):
    r = lax.broadcasted_iota(jnp.int32, sc.shape, 0)
    c = lax.broadcasted_iota(jnp.int32, sc.shape, 1)
    return jnp.where((r <= c) if keys_on_rows else (c <= r), sc, NEG_INF)


def _fox_fwd(q2, kv, cum2_t, z, name):
    s, w = q2.shape
    nh = w // HEAD_DIM
    tq = _attn_tiles(s)
    nq = s // tq
    nt = (_DOT_DIMS["nt"], ((), ()))

    def body(q_ref, k_ref, v_ref, ct_ref, z_ref, o_ref, oz_ref, lse_row_ref, m_s, acc_s, vaug, s_buf):
        i = pl.program_id(1)

        @pl.when(i == 0)
        def _():
            vaug[:, :HEAD_DIM] = v_ref[...]
            vaug[:, HEAD_DIM:] = jnp.ones((s, LANES), BF16)

        qb = q_ref[...]
        m_s[...] = jnp.full_like(m_s, NEG_INF)
        acc_s[...] = jnp.zeros_like(acc_s)

        def scores(j):
            off = pl.multiple_of(j * tq, tq)
            return lax.dot_general(qb, k_ref[pl.ds(off, tq), :], nt, preferred_element_type=F32) - ct_ref[:, pl.ds(off, tq)]

        def softmax_pv(j, sc):
            m_old = m_s[...]
            m_new = jnp.maximum(m_old, jnp.max(sc, axis=1, keepdims=True))
            p = _exp2_rows(sc, m_new)
            alpha = jnp.exp2(m_old - m_new)
            pv = jnp.dot(p.astype(BF16), vaug[pl.ds(pl.multiple_of(j * tq, tq), tq), :], preferred_element_type=F32)
            acc_s[...] = jnp.concatenate([alpha, alpha], axis=1) * acc_s[...] + pv
            m_s[...] = m_new

        s_buf[...] = scores(0)

        def loop(j, carry):
            nxt = scores(j + 1)
            softmax_pv(j, s_buf[...])
            s_buf[...] = nxt
            return carry

        lax.fori_loop(0, i, loop, 0)
        softmax_pv(i, _causal(s_buf[...], False))
        l = acc_s[:, HEAD_DIM:]
        o = acc_s[:, :HEAD_DIM] / l
        o_ref[...] = o.astype(BF16)
        oz_ref[...] = (o * _silu(z_ref[...].astype(F32))).astype(BF16)
        lse_row_ref[...] = _row_of(m_s[...] + jnp.log(l) * LOG2E)

    return pl.pallas_call(
        body, name=name, grid=(nh, nq),
        in_specs=[pl.BlockSpec((tq, HEAD_DIM), lambda h, i: (i, h)),
                  pl.BlockSpec((s, HEAD_DIM), lambda h, i: (0, h)),
                  pl.BlockSpec((s, HEAD_DIM), lambda h, i: (0, nh + h)),
                  pl.BlockSpec((None, 1, s), lambda h, i: (h, 0, 0)),
                  pl.BlockSpec((tq, HEAD_DIM), lambda h, i: (i, h))],
        out_specs=[pl.BlockSpec((tq, HEAD_DIM), lambda h, i: (i, h)),
                   pl.BlockSpec((tq, HEAD_DIM), lambda h, i: (i, h)),
                   pl.BlockSpec((None, 1, tq), lambda h, i: (h, 0, i))],
        out_shape=[jax.ShapeDtypeStruct((s, w), BF16), jax.ShapeDtypeStruct((s, w), BF16),
                   jax.ShapeDtypeStruct((nh, 1, s), F32)],
        scratch_shapes=[pltpu.VMEM((tq, LANES), F32), pltpu.VMEM((tq, HEAD_DIM + LANES), F32),
                        pltpu.VMEM((s, HEAD_DIM + LANES), BF16), pltpu.VMEM((tq, tq), F32)],
        compiler_params=_cparams("arbitrary", "arbitrary"),
    )(q2, kv, kv, cum2_t, z)


def _fox_bwd(q2, kv, do, o, lse2_t, cum2, dqz, name):
    s, w = q2.shape
    nh = w // HEAD_DIM
    tk = _attn_tiles(s)
    nk = s // tk
    scale = HEAD_DIM ** -0.5
    nt = (_DOT_DIMS["nt"], ((), ()))
    tn = (_DOT_DIMS["tn"], ((), ()))

    def body(q_ref, k_ref, v_ref, do_ref, o_ref, lse_ref, c_ref, _, dk_ref, dv_ref, dq_ref, dcq_ref, dck_ref,
             dk_s, dv_s, dc_s, dq_s, dcq_s, dl_s, s_buf, dp_buf):
        h, j = pl.program_id(0), pl.program_id(1)

        @pl.when(j == 0)
        def _():
            dq_s[...] = jnp.zeros_like(dq_s)
            dcq_s[...] = jnp.zeros_like(dcq_s)
            for i in range(nk):
                rows = pl.ds(i * tk, tk)
                d = jnp.sum(do_ref[rows, :].astype(F32) * o_ref[rows, :].astype(F32), axis=1, keepdims=True)
                dl_s[:, i * tk:(i + 1) * tk] = _row_of(jnp.broadcast_to(d, (tk, LANES)))

        kb = k_ref[...]
        vb = v_ref[...]
        ck = jnp.broadcast_to(_head_col(c_ref[...], h), (tk, LANES))
        dk_s[...] = jnp.zeros_like(dk_s)
        dv_s[...] = jnp.zeros_like(dv_s)
        dc_s[...] = jnp.zeros_like(dc_s)

        def scores(i):
            off = pl.multiple_of(i * tk, tk)
            sc = lax.dot_general(kb, q_ref[pl.ds(off, tk), :], nt, preferred_element_type=F32) - lse_ref[:, pl.ds(off, tk)]
            dp = lax.dot_general(vb, do_ref[pl.ds(off, tk), :], nt, preferred_element_type=F32) - dl_s[:, pl.ds(off, tk)]
            return sc, dp

        def accumulate(i, sc, dp):
            off = pl.multiple_of(i * tk, tk)
            p = _exp2_rows(sc, ck)
            dv_s[...] += jnp.dot(p.astype(BF16), do_ref[pl.ds(off, tk), :], preferred_element_type=F32)
            ds = p * dp
            dsb = ds.astype(BF16)
            dk_s[...] += jnp.dot(dsb, q_ref[pl.ds(off, tk), :], preferred_element_type=F32)
            dq_s[pl.ds(off, tk), :] += lax.dot_general(dsb, kb, tn, preferred_element_type=F32)
            dcq_s[:, pl.ds(off, tk)] += jnp.sum(ds, axis=0, keepdims=True)
            part = ds[:, :LANES]
            for b in range(1, tk // LANES):
                part = part + ds[:, b * LANES:(b + 1) * LANES]
            dc_s[...] += part

        sc0, dp0 = scores(j)
        s_buf[...] = _causal(sc0, True)
        dp_buf[...] = dp0

        def loop(i, carry):
            nxt = scores(i + 1)
            accumulate(i, s_buf[...], dp_buf[...])
            s_buf[...], dp_buf[...] = nxt
            return carry

        lax.fori_loop(j, nk - 1, loop, 0)
        accumulate(nk - 1, s_buf[...], dp_buf[...])
        dk_ref[...] = (dk_s[...] * (1.0 / LOG2E)).astype(BF16)
        dv_ref[...] = dv_s[...].astype(BF16)
        dck_ref[...] = jnp.sum(jnp.transpose(dc_s[...]), axis=0, keepdims=True)

        @pl.when(j == nk - 1)
        def _():
            dq_ref[...] = (dq_s[...] * scale).astype(BF16)
            dcq_ref[...] = dcq_s[...]

    col = pl.BlockSpec((s, HEAD_DIM), lambda h, j: (0, h))
    row = pl.BlockSpec((None, 1, s), lambda h, j: (h, 0, 0))
    kspec = pl.BlockSpec((tk, HEAD_DIM), lambda h, j: (j, h))
    return pl.pallas_call(
        body, name=name, grid=(nh, nk),
        in_specs=[col, kspec, pl.BlockSpec((tk, HEAD_DIM), lambda h, j: (j, nh + h)), col, col, row,
                  pl.BlockSpec((tk, LANES), lambda h, j: (j, 0)), pl.BlockSpec(memory_space=pl.ANY)],
        out_specs=[kspec, kspec, col, row, pl.BlockSpec((None, 1, tk), lambda h, j: (h, 0, j))],
        out_shape=[jax.ShapeDtypeStruct((s, w), BF16), jax.ShapeDtypeStruct((s, w), BF16),
                   jax.ShapeDtypeStruct(dqz.shape, BF16), jax.ShapeDtypeStruct((nh, 1, s), F32),
                   jax.ShapeDtypeStruct((nh, 1, s), F32)],
        input_output_aliases={7: 2},
        scratch_shapes=[pltpu.VMEM((tk, HEAD_DIM), F32), pltpu.VMEM((tk, HEAD_DIM), F32), pltpu.VMEM((tk, LANES), F32),
                        pltpu.VMEM((s, HEAD_DIM), F32), pltpu.VMEM((1, s), F32), pltpu.VMEM((1, s), F32),
                        pltpu.VMEM((tk, tk), F32), pltpu.VMEM((tk, tk), F32)],
        compiler_params=_cparams("arbitrary", "arbitrary"),
    )(q2, kv, kv, do, o, lse2_t, cum2, dqz)


_ALL_PEERS = tuple(range(1, N_DEV))
_CHIP_PEERS = (1, 2, 4, 6)


def _exchange_copies(ins, outs, send_sems, recv_sems, local_sems, scatter, peers=_ALL_PEERS):
    x, y, c = (lax.axis_index(a) for a in MESH_AXES)
    me = 4 * x + 2 * y + c
    local, remote = [], []
    for a in range(len(ins)):
        local.append(pltpu.make_async_copy(ins[a].at[me] if scatter else ins[a], outs[a].at[me], local_sems.at[a]))
        for k in peers:
            px, py, pc = (1 - x if k & 4 else x), (1 - y if k & 2 else y), (1 - c if k & 1 else c)
            remote.append(pltpu.make_async_remote_copy(
                src_ref=ins[a].at[4 * px + 2 * py + pc] if scatter else ins[a], dst_ref=outs[a].at[me],
                send_sem=send_sems.at[a * (N_DEV - 1) + k - 1], recv_sem=recv_sems.at[a * (N_DEV - 1) + k - 1],
                device_id=(px, py, pc), device_id_type=pl.DeviceIdType.MESH))
    return local, remote


def _exchange_out_shapes(arrs, scatter):
    return [((N_DEV,) + a.shape[1:]) if scatter else ((N_DEV,) + a.shape) for a in arrs]


_HBM =pl.BlockSpec(memory_space=pltpu.HBM)
_SEM = pl.BlockSpec(memory_space=pltpu.SEMAPHORE)


def _exchange_start(arrs, scatter, name, after=(), peers=_ALL_PEERS):
    n = len(arrs)
    after = list(after)
    lands = [lax.empty(s, a.dtype) for s, a in zip(_exchange_out_shapes(arrs, scatter), arrs)]

    def body(*refs):
        ins, outs = refs[:n], refs[n:2 * n]
        send_sems, recv_sems, local_sems = refs[2 * n + len(after):2 * n + len(after) + 3]
        token = refs[-1]
        local, remote = _exchange_copies(ins, outs, send_sems, recv_sems, local_sems, scatter, peers)
        for cp in local + remote:
            cp.start()
        token[...] = jnp.zeros_like(token)

    hbm = lambda a: pltpu.HBM(a.shape, a.dtype)
    res = pl.pallas_call(
        body, name=name,
        out_shape=(pltpu.SemaphoreType.DMA((n * (N_DEV - 1),)), pltpu.SemaphoreType.DMA((n * (N_DEV - 1),)),
                   pltpu.SemaphoreType.DMA((n,)), *[hbm(a) for a in arrs], *[hbm(a) for a in lands],
                   jax.ShapeDtypeStruct((SUBLANES, LANES), F32)),
        in_specs=[_HBM] * (2 * n) + [pl.BlockSpec(memory_space=pl.ANY)] * len(after),
        out_specs=(_SEM, _SEM, _SEM, *[_HBM] * (2 * n), pl.BlockSpec(memory_space=pltpu.VMEM)),
        input_output_aliases={i: 3 + i for i in range(2 * n)},
        compiler_params=pltpu.CompilerParams(has_side_effects=pltpu.SideEffectType.DATAFLOW_SIDE_EFFECTING),
    )(*[pltpu.with_memory_space_constraint(a, pltpu.HBM) for a in list(arrs) + lands], *after)
    return (n, scatter, res[:3], res[3:3 + n], res[3 + n:3 + 2 * n], peers), res[-1]


def _exchange_wait(state, after, name):
    n, scatter, sems, srcs, lands, peers = state
    after = list(after) if isinstance(after, (list, tuple)) else [after]

    def body(*refs):
        ins, outs = refs[:n], refs[n:2 * n]
        send_sems, recv_sems, local_sems = refs[2 * n:2 * n + 3]
        local, remote = _exchange_copies(ins, outs, send_sems, recv_sems, local_sems, scatter, peers)
        for cp in remote:
            cp.wait_send()
            cp.wait_recv()
        for cp in local:
            cp.wait()

    hbm = lambda a: pltpu.HBM(a.shape, a.dtype)
    res = pl.pallas_call(
        body, name=name,
        out_shape=(*[hbm(a) for a in srcs], *[hbm(a) for a in lands]),
        in_specs=[_HBM] * (2 * n) + [_SEM] * 3 + [pl.BlockSpec(memory_space=pl.ANY)] * len(after),
        out_specs=tuple([_HBM] * (2 * n)),
        input_output_aliases={i: i for i in range(2 * n)},
        compiler_params=pltpu.CompilerParams(has_side_effects=pltpu.SideEffectType.DATAFLOW_SIDE_EFFECTING),
    )(*srcs, *lands, *sems, *after)
    return list(res[n:])


def _forward_to_sibling(slots, name):
    n = len(slots)
    hops = (2, 4, 6)

    def body(*refs):
        ins, outs, (send_sems, recv_sems) = refs[:n], refs[n:2 * n], refs[2 * n:]
        x, y, c = (lax.axis_index(a) for a in MESH_AXES)
        copies = []
        for a in range(n):
            for i, k in enumerate(hops):
                slot = 4 * (1 - x if k & 4 else x) + 2 * (1 - y if k & 2 else y) + c
                copies.append(pltpu.make_async_remote_copy(
                    src_ref=ins[a].at[slot], dst_ref=outs[a].at[slot],
                    send_sem=send_sems.at[a * len(hops) + i], recv_sem=recv_sems.at[a * len(hops) + i],
                    device_id=(x, y, 1 - c), device_id_type=pl.DeviceIdType.MESH))
        for cp in copies:
            cp.start()
        for cp in copies:
            cp.wait_send()
            cp.wait_recv()

    return pl.pallas_call(
        body, name=name, out_shape=[jax.ShapeDtypeStruct(s.shape, s.dtype) for s in slots],
        in_specs=[pl.BlockSpec(memory_space=pl.ANY)] * n, out_specs=[pl.BlockSpec(memory_space=pl.ANY)] * n,
        input_output_aliases={i: i for i in range(n)},
        scratch_shapes=[pltpu.SemaphoreType.DMA((n * len(hops),)), pltpu.SemaphoreType.DMA((n * len(hops),))],
    )(*slots)


def _adamw_math(w, g, m, v):
    m = ADAM_B1 * m + (1.0 - ADAM_B1) * g
    v = ADAM_B2 * v + (1.0 - ADAM_B2) * (g * g)
    m_hat = m / (1.0 - ADAM_B1 ** ADAM_STEP)
    v_hat = v / (1.0 - ADAM_B2 ** ADAM_STEP)
    return -ADAM_LR * (m_hat / (jnp.sqrt(v_hat) + ADAM_EPS) + ADAM_WD * w), m, v


def _slot_sum(p_ref):
    g = p_ref[0].astype(F32)
    for d in range(1, p_ref.shape[0]):
        g = g + p_ref[d].astype(F32)
    return g


def _adamw_tile(r, c):
    return _tile(r, max(SUBLANES, (256 * 1024) // c // SUBLANES * SUBLANES), SUBLANES)


def _adamw(parts, w, m, v, name):
    r, c = w.shape[-2:]
    by_cols = r % SUBLANES != 0
    tr, tc = (r, _tile(c, 256)) if by_cols else (_adamw_tile(r, c), c)

    def body(p_ref, w_ref, m_ref, v_ref, g_ref, d_ref, nm_ref, nv_ref):
        g = _slot_sum(p_ref)
        g_ref[...] = g
        d_ref[...], nm_ref[...], nv_ref[...] = _adamw_math(w_ref[...], g, m_ref[...], v_ref[...])

    pos = (lambda i: (0, i)) if by_cols else (lambda i: (i, 0))
    if w.ndim == 3:
        blk = pl.BlockSpec((None, tr, tc), lambda i: (0,) + pos(i))
    else:
        blk = pl.BlockSpec((tr, tc), pos)
    sh = jax.ShapeDtypeStruct(w.shape, F32)
    return pl.pallas_call(
        body, name=name, grid=(c // tc if by_cols else r // tr,),
        in_specs=[pl.BlockSpec((parts.shape[0], tr, tc), lambda i: (0,) + pos(i)), blk, blk, blk],
        out_specs=[blk] * 4, out_shape=[sh] * 4, compiler_params=_cparams("parallel"),
    )(parts, w, m, v)


def _sum_parts(parts, name):
    _, r, c = parts.shape
    tr = _adamw_tile(r, c)

    def body(p_ref, o_ref):
        o_ref[...] = _slot_sum(p_ref)

    return pl.pallas_call(
        body, name=name, grid=(r // tr,),
        in_specs=[pl.BlockSpec((parts.shape[0], tr, c), lambda i: (0, i, 0))],
        out_specs=pl.BlockSpec((tr, c), lambda i: (i, 0)), out_shape=jax.ShapeDtypeStruct((r, c), F32),
        compiler_params=_cparams("parallel"),
    )(parts)


def _lane_pad(a, width=LANES):
    return jnp.pad(a, ((0, 0), (0, width - a.shape[1])))


def _local_step(x, target, norm_pre, norm_post, kv_norm, kv_b_f, a_re, a_im, log_dt, b_re, b_im, c_re, c_im, comm):
    s, d = x.shape
    g, p = a_re.shape
    w = g * S5_GROUP
    fw = d
    nh = fw // HEAD_DIM
    seg_len = s // N_SEG
    row = lambda v: v.reshape(1, -1)
    g_pre0, g_pre1, g_post0, g_post1, g_kv = row(norm_pre[0]), row(norm_pre[1]), row(norm_post[0]), row(norm_post[1]), row(kv_norm)

    ldt = log_dt.reshape(g, 1)
    abr, abi, cr, ci = _s5_disc_fwd(a_re, a_im, ldt)
    cr_col, ci_col = cr.reshape(g * p, 1), ci.reshape(g * p, 1)
    b_re2, b_im2 = b_re.reshape(g * p, S5_GROUP), b_im.reshape(g * p, S5_GROUP)
    bb_re, bb_im = _s5_bbar_fwd(cr_col, ci_col, b_re2, b_im2)
    bd_re = _block_diag(bb_re.reshape(g, p, S5_GROUP)).astype(BF16)
    bd_im = _block_diag(bb_im.reshape(g, p, S5_GROUP)).astype(BF16)
    cd_re = _block_diag(c_re).astype(BF16)
    cd_im = _block_diag(-c_im).astype(BF16)
    ab_re = jnp.broadcast_to(abr.reshape(1, g * p), (N_SEG, g * p))
    ab_im = jnp.broadcast_to(abi.reshape(1, g * p), (N_SEG, g * p))
    zero_seg = jnp.zeros((N_SEG, g * p), F32)

    xn0 = _norm_cast(x, g_pre0 + comm.token, "norm_pre0", x_kind="nat")
    w_in = comm.weight("s5_w_in", [xn0, bd_re, bd_im, cd_re, cd_im, ab_re, ab_im])
    d_row, bglu_row = row(comm.vector("s5_d")), row(comm.vector("s5_b_glu"))
    u = _mm(xn0, w_in, "nn", BF16, "s5_in_u", b_cols=(0, w), b_slots=True)
    z0 = _mm(xn0, w_in, "nn", BF16, "s5_in_z", b_cols=(w, w), b_slots=True)
    e_re, e_im = _s5_scan_fwd(u, bd_re, bd_im, cd_re, cd_im, ab_re, ab_im, zero_seg, zero_seg, d_row, False, "s5_scan_ends")
    i_re, i_im = _s5_seg_fix(e_re, e_im, ab_re, ab_im, seg_len, False, "s5_seg_fix")
    y_ssm, yg, h_re, h_im, _, _ = _s5_scan_fwd(u, bd_re, bd_im, cd_re, cd_im, ab_re, ab_im, i_re, i_im, d_row, True, "s5_scan")
    w_glu, w_out = comm.weight("s5_w_glu", yg), comm.weight("s5_w_out", yg)
    gp = _mm(yg, w_glu, "nn", BF16, "s5_glu")
    y3 = _s5_gate(y_ssm, gp, bglu_row, z0, "s5_gate")
    w_kvt, fw_in = comm.weight("kv_w", y3), comm.weight("fox_w_in", y3)
    w_ft = jnp.pad(w_kvt[2 * fw:], ((0, LANES - nh), (0, 0)))
    o0 = _mm(y3, w_out, "nn", F32, "s5_out")

    h1, hn_kv, xn1 = _resid_norm2(x, o0, g_post0 + comm.late_token, g_kv, g_pre1, "resid_norms")
    kv = _mm(hn_kv, w_kvt, "nt", BF16, "kv_proj", b_rows=2 * fw)
    f_logit = _mm(hn_kv, w_ft, "nt", F32, "f_proj")
    bf_row = _lane_pad(row(kv_b_f))
    cum2 = _cum_fwd(f_logit, bf_row, "cum_fwd")
    cum2_t = cum2[:, :nh].T.reshape(nh, 1, s)
    q2 = _mm(xn1, fw_in, "nn", BF16, "fox_q", scale=HEAD_DIM ** -0.5 * LOG2E, b_cols=(0, fw), b_slots=True)
    z1 = _mm(xn1, fw_in, "nn", BF16, "fox_z", b_cols=(fw, fw), b_slots=True)
    o, oz, lse2_t = _fox_fwd(q2, kv, cum2_t, z1, "fox_fwd")
    fw_out = comm.weight("fox_w_out", oz)
    o1 = _mm(oz, fw_out, "nn", F32, "fox_out")
    dh2, do1, sq, dg_post1 = _post_norm_loss(o1, g_post1, h1, target, "norm_post1_loss")
    loss = 0.5 * jnp.sum(sq) / d

    d_fw_out = _mm(oz, do1, "tn", BF16, "fox_out_dw")
    d_oz = _mm(do1, fw_out, "nt", BF16, "fox_out_dx")
    do, dqz = _gate_bwd(d_oz, o, z1, "fox_gate_bwd")
    dk, dv, dqz, dcq, dck = _fox_bwd(q2, kv, do, o, lse2_t, cum2, dqz, "fox_bwd")
    d_fw_in = _mm(xn1, dqz, "tn", BF16, "fox_in_dw", col_slots=True)
    dxn1 = _mm(dqz, fw_in, "nt", BF16, "fox_in_dx", b_slots=True)
    dcq_sl = _lane_pad(dcq.reshape(nh, s).T)
    dck_sl = _lane_pad(dck.reshape(nh, s).T)
    df, db_f = _cum_bwd(dcq_sl, dck_sl, f_logit, bf_row, "cum_bwd")
    dkv = _concat_cast(dk, dv, "fox_dkv")
    d_w_kvmt = _mm(dkv, hn_kv, "tn", BF16, "kv_dw")
    d_w_ft = _mm(df, hn_kv, "tn", BF16, "f_dw")
    dhn_f = _mm(df, w_ft, "nn", F32, "f_dx")
    dhn_kv = _mm(dkv, w_kvt, "nn", BF16, "kv_dx", add=dhn_f, b_rows=2 * fw)
    d_w_kvt = jnp.concatenate([d_w_kvmt, d_w_ft[:nh]], axis=0)
    tok = comm.send_grads(dict(fox_w_out=d_fw_out, fox_w_in=d_fw_in, kv_w=d_w_kvt), "exchange_fox")
    dh1, do0, dg_pre1, dg_kv, dg_post0 = _norm_bwd2(dh2, h1, dxn1, dhn_kv, g_pre1, g_kv, o0, g_post0 + tok[0, 0],
                                                      "resid_norms_bwd")

    d_w_out = _mm(y3, do0, "tn", BF16, "s5_out_dw")
    dy3 = _mm(do0, w_out, "nt", BF16, "s5_out_dx")
    duz, dgp, dyg_direct, db_glu = _s5_gate_bwd(dy3, y_ssm, gp, bglu_row, z0, "s5_gate_bwd")
    d_w_glu = _mm(yg, dgp, "tn", BF16, "s5_glu_dw")
    gelu_bwd = lambda dyg, y: jax.vjp(jax.nn.gelu, y.astype(F32))[1](dyg)[0]
    dy_ssm = _mm(dgp, w_glu, "nt", F32, "s5_glu_dx", add=dyg_direct, epilogue=(gelu_bwd, y_ssm))
    d_row = d_row + comm.send_grads(dict(s5_w_out=d_w_out, s5_w_glu=d_w_glu), "exchange_s5")[0, 0]
    ab_imn = -ab_im
    ge_re, ge_im = _s5_scan_bwd(dy_ssm, u, h_re, h_im, bd_re, bd_im, cd_re, cd_im, ab_re, ab_imn, zero_seg, zero_seg,
                                d_row, False, "s5_adj_ends")
    gi_re, gi_im = _s5_seg_fix(ge_re, ge_im, ab_re, ab_imn, seg_len, True, "s5_adj_fix")
    duz, dbd_re, dbd_im, dcd_re, dcd_im, dab_re, dab_im, dd = _s5_scan_bwd(
        dy_ssm, u, h_re, h_im, bd_re, bd_im, cd_re, cd_im, ab_re, ab_imn, gi_re, gi_im, d_row, True, "s5_adj", duz=duz)
    d_w_in = _mm(xn0, duz, "tn", BF16, "s5_in_dw", col_slots=True)
    tok = comm.send_grads(dict(s5_w_in=d_w_in), "exchange_s5_in")
    dxn0 = _mm(duz, w_in, "nt", BF16, "s5_in_dx", after=tok, b_slots=True)
    grad_x, dg_pre0 = _norm_bwd1(dh1, x, dxn0, g_pre0, "norm_pre0_bwd")

    dbb_re = _block_diag_extract(dbd_re, p, S5_GROUP).reshape(g * p, S5_GROUP)
    dbb_im = _block_diag_extract(dbd_im, p, S5_GROUP).reshape(g * p, S5_GROUP)
    dcr_col, dci_col, db_re, db_im = _s5_bbar_bwd(cr_col, ci_col, b_re2, b_im2, dbb_re, dbb_im)
    da_re, da_im, dldt = _s5_disc_bwd(a_re, a_im, ldt, dab_re.reshape(g, p), dab_im.reshape(g, p),
                                      dcr_col.reshape(g, p), dci_col.reshape(g, p))
    dc_re = _block_diag_extract(dcd_re, S5_GROUP, p)
    dc_im = -_block_diag_extract(dcd_im, S5_GROUP, p)

    small = dict(
        norm_pre=jnp.concatenate([dg_pre0, dg_pre1], axis=0), norm_post=jnp.concatenate([dg_post0, dg_post1], axis=0),
        s5_a_re=da_re, s5_a_im=da_im, s5_log_dt=dldt.reshape(g), s5_b_re=db_re.reshape(g, p, S5_GROUP),
        s5_b_im=db_im.reshape(g, p, S5_GROUP), s5_c_re=dc_re, s5_c_im=dc_im, s5_d=dd.reshape(-1),
        s5_b_glu=db_glu.reshape(-1), kv_norm=dg_kv.reshape(-1), kv_b_f=db_f[0, :nh])
    return loss, grad_x, small


_BIG = ("s5_w_in", "s5_w_glu", "s5_w_out", "kv_w", "fox_w_in", "fox_w_out")
_COL_SHARDED = ("s5_w_in", "fox_w_in")
_SMALL = ("norm_pre", "norm_post", "s5_a_re", "s5_a_im", "s5_log_dt", "s5_b_re", "s5_b_im", "s5_c_re", "s5_c_im",
          "s5_d", "s5_b_glu", "kv_norm", "kv_b_f")
_SMALL_SHARDED = ("s5_d", "s5_b_glu")
_PACK_QUANTUM = SUBLANES * LANES
_WEIGHTS = ('norm_pre', 'norm_post', 's5_w_in', 's5_a_re', 's5_a_im', 's5_log_dt', 's5_b_re', 's5_b_im', 's5_c_re', 's5_c_im',
            's5_d', 's5_w_glu', 's5_b_glu', 's5_w_out', 'kv_norm', 'kv_w', 'kv_b_f', 'fox_w_in', 'fox_w_out')


def _full_from_slots(name, slots):
    n, r, c = slots.shape
    if name in _COL_SHARDED:
        return slots.transpose(1, 0, 2).reshape(r, n * c)
    return slots.reshape(n * r, c)


def _slots_from_full(name, full):
    if name in _COL_SHARDED:
        r, nc = full.shape
        return full.reshape(r, N_DEV, nc // N_DEV).transpose(1, 0, 2)
    nr, c = full.shape
    return full.reshape(N_DEV, nr // N_DEV, c)


def _groups_last(shape):
    return len(shape) >= 3 and shape[-1] < LANES and shape[-3] % LANES == 0


def _pack(vals):
    parts = []
    for v in vals:
        flat = jnp.moveaxis(v, -3, -1).reshape(-1) if _groups_last(v.shape) else v.reshape(-1)
        parts.append(jnp.pad(flat, (0, (-flat.shape[0]) % _PACK_QUANTUM)))
    total = sum(p.shape[0] for p in parts)
    parts.append(jnp.zeros(((-total) % (N_DEV * _PACK_QUANTUM),), F32))
    return jnp.concatenate(parts).reshape(-1, LANES)


def _unpack(packed, shapes):
    flat = packed.reshape(-1)
    out, off = [], 0
    for sh in shapes:
        n = math.prod(sh)
        piece = flat[off:off + n]
        if _groups_last(sh):
            piece = jnp.moveaxis(piece.reshape(sh[:-3] + sh[-2:] + sh[-3:-2]), -1, -3)
        out.append(piece.reshape(sh))
        off += n + (-n) % _PACK_QUANTUM
    return out


class _Comm:
    _GROUPS = (("s5_w_in",) + _SMALL_SHARDED, ("s5_w_glu", "s5_w_out"), ("kv_w", "fox_w_in"), ("fox_w_out",))
    _SLOT_FORM = ("s5_w_in", "fox_w_in")

    def __init__(self, shards, vectors, early=()):
        self._shards = {**shards, **vectors}
        self._full, self._gathers = {}, {}
        self._early = list(early)
        self.token = jnp.zeros((), F32)
        for group in self._GROUPS[:-1]:
            self.token = self.token + self._start(group, ())[0, 0]
        self.late_token = None
        self._sent = []

    def _start(self, group, after):
        state, tok = _exchange_start([self._shards[n] for n in group], False, "gather_start_" + group[0], after,
                                     peers=_CHIP_PEERS)
        self._gathers[group] = state
        return tok

    def vector(self, name):
        return self._full[name]

    def weight(self, name, after):
        if name not in self._full:
            group = next(g for g in self._GROUPS if name in g)
            if group == self._GROUPS[0]:
                after = (list(after) if isinstance(after, (list, tuple)) else [after]) + self._early
            slots = _exchange_wait(self._gathers.pop(group), after, "gather_wait_" + group[0])
            slots = _forward_to_sibling(slots, "gather_forward_" + group[0])
            for n, sl in zip(group, slots):
                if n in _SMALL_SHARDED:
                    self._full[n] = sl.reshape(-1)
                else:
                    self._full[n] = sl if n in self._SLOT_FORM else _full_from_slots(n, sl)
            if group == self._GROUPS[-2]:
                self.late_token = self._start(self._GROUPS[-1], [slots[0]])[0, 0]
        return self._full[name]

    def send_grads(self, grads, name):
        names = list(grads)
        slots = [grads[n] if grads[n].ndim == 3 else _slots_from_full(n, grads[n]).astype(BF16) for n in names]
        state, tok = _exchange_start(slots, True, name + "_start")
        self._sent.append((names, state, name + "_wait"))
        return tok

    def received_grads(self, group, after):
        names, state, name = self._sent[group]
        return list(zip(names, _exchange_wait(state, after, name)))


def kernel(x, norm_pre, norm_post, s5_w_in, s5_a_re, s5_a_im, s5_log_dt, s5_b_re, s5_b_im, s5_c_re, s5_c_im, s5_d, s5_w_glu, s5_b_glu, s5_w_out, kv_norm, kv_w, kv_b_f, fox_w_in, fox_w_out, loss_target, m_norm_pre, m_norm_post, m_s5_w_in, m_s5_a_re, m_s5_a_im, m_s5_log_dt, m_s5_b_re, m_s5_b_im, m_s5_c_re, m_s5_c_im, m_s5_d, m_s5_w_glu, m_s5_b_glu, m_s5_w_out, m_kv_norm, m_kv_w, m_kv_b_f, m_fox_w_in, m_fox_w_out, v_norm_pre, v_norm_post, v_s5_w_in, v_s5_a_re, v_s5_a_im, v_s5_log_dt, v_s5_b_re, v_s5_b_im, v_s5_c_re, v_s5_c_im, v_s5_d, v_s5_w_glu, v_s5_b_glu, v_s5_w_out, v_kv_norm, v_kv_w, v_kv_b_f, v_fox_w_in, v_fox_w_out):
    env = dict(locals())
    wts = {n: env[n] for n in _WEIGHTS}
    mom = {n: env["m_" + n] for n in _WEIGHTS}
    var = {n: env["v_" + n] for n in _WEIGHTS}
    me = 4 * lax.axis_index("x") + 2 * lax.axis_index("y") + lax.axis_index("c")
    shard2d = {n: (wts[n].T if n == "kv_w" else wts[n].reshape(wts[n].shape[-2:])) for n in _BIG}
    full_shape = {n: ((wts[n].size * N_DEV,) if n in _SMALL_SHARDED else wts[n].shape) for n in _SMALL}

    def spread(n, v):
        if n not in _SMALL_SHARDED:
            return v
        flat = v.reshape(-1)
        return lax.dynamic_update_slice(jnp.zeros(full_shape[n], F32), flat, (me * flat.shape[0],))

    packed = [_pack([spread(n, src[n]) for n in _SMALL] + [jnp.zeros((1,), F32)]) for src in (wts, mom, var)]
    comm = _Comm({n: _cast_bf16(shard2d[n], "cast_" + n) for n in _BIG}, {n: wts[n].reshape(1, -1) for n in _SMALL_SHARDED}, packed)

    loss_local, grad_x, small = _local_step(
        x[0], loss_target[0], norm_pre, norm_post, kv_norm, kv_b_f, s5_a_re[0], s5_a_im[0], s5_log_dt[0],
        s5_b_re[0], s5_b_im[0], s5_c_re[0], s5_c_im[0], comm)

    small_pack = _pack([small[n] for n in _SMALL] + [loss_local.reshape(1)])
    slice_rows = small_pack.shape[0] // N_DEV
    small_state, small_tok = _exchange_start([small_pack.reshape(N_DEV, slice_rows, LANES)], True, "reduce_small_start")

    res = {}

    def finish(group, after):
        for n, recv in comm.received_grads(group, after):
            if n == "kv_w":
                res[n] = [o.T for o in _adamw(recv, wts[n].T, mom[n].T, var[n].T, "adamw_" + n)]
            else:
                res[n] = _adamw(recv, wts[n], mom[n], var[n], "adamw_" + n)

    finish(0, [small_tok, grad_x])
    my_sum = _sum_parts(_exchange_wait(small_state, res["kv_w"][0], "reduce_small_wait")[0], "sum_small")
    gather_state, gather_tok = _exchange_start([my_sum], False, "gather_small_start")
    finish(1, gather_tok)
    finish(2, gather_tok)
    g_all = _exchange_wait(gather_state, res["s5_w_in"][0], "gather_small_wait")[0].reshape(1, small_pack.shape[0], LANES)
    outs = _adamw(g_all, *packed, "adamw_small")
    unpacked = [_unpack(o, [full_shape[n] for n in _SMALL] + [(1,)]) for o in outs]
    loss = unpacked[0][-1][0]
    for i, n in enumerate(_SMALL):
        vals = [u[i] for u in unpacked]
        if n in _SMALL_SHARDED:
            k = wts[n].size
            vals = [lax.dynamic_slice(v, (me * k,), (k,)) for v in vals]
        res[n] = [v.reshape(wts[n].shape) for v in vals]

    return (loss, grad_x[None], *[res[n][0] for n in _WEIGHTS], *[res[n][1] for n in _WEIGHTS],
            *[res[n][2] for n in _WEIGHTS], *[res[n][3] for n in _WEIGHTS])
```

```python
import math

import jax
import jax.numpy as jnp
from jax import lax
from jax.experimental import pallas as pl
from jax.experimental.pallas import tpu as pltpu

F32 = jnp.float32
BF16 = jnp.bfloat16

N_DEV = 8
MESH_AXES = ("x", "y", "c")
S5_GROUP = 16
S5_STATE = 64
LANES = 128
SUBLANES = 8
GROUPS_PER_BLOCK = LANES // S5_GROUP
BLOCK_STATE = GROUPS_PER_BLOCK * S5_STATE
N_SEG = SUBLANES
HEAD_DIM = 128
RMS_EPS = 1e-6
NEG_INF = -1e30
LOG2E = math.log2(math.e)
ADAM_LR = 0.001
ADAM_B1 = 0.9
ADAM_B2 = 0.999
ADAM_EPS = 1e-08
ADAM_WD = 0.01
ADAM_STEP = 10
VMEM_LIMIT = 56 * 1024 * 1024


def _tile(n, pref, quantum=LANES):
    if n <= pref:
        return n
    t = (pref // quantum) * quantum
    while t >= quantum:
        if n % t == 0:
            return t
        t -= quantum
    return n


def _cparams(*sem):
    return pltpu.CompilerParams(dimension_semantics=sem if sem else None, vmem_limit_bytes=VMEM_LIMIT)


_DOT_DIMS = {"nn": ((1,), (0,)), "nt": ((1,), (1,)), "tn": ((0,), (0,))}


def _mm(a, b, mode, out_dtype, name, add=None, scale=None, b_cols=None, after=None, col_slots=False, b_slots=False,
        b_rows=None, epilogue=None):
    slot_w = b.shape[2] if b_slots else None
    b2d = (b.shape[1], b.shape[0] * b.shape[2]) if b_slots else b.shape
    b_shape = b2d if b_cols is None else (b2d[0], b_cols[1])
    if b_rows is not None:
        b_shape = (b_rows, b_shape[1])
    if mode == "nn":
        (M, K), (K2, N) = a.shape, b_shape
    elif mode == "nt":
        (M, K), (N, K2) = a.shape, b_shape
    else:
        (K, M), (K2, N) = a.shape, b_shape
    assert K == K2, (name, a.shape, b_shape)
    tm, tn, tk = _tile(M, 1024 if K <= 2048 else 512), (N // N_DEV if col_slots else _tile(N, 1024)), _tile(K, 4096)
    if b_slots and mode == "nn":
        tn = slot_w
    nk = K // tk
    dims = (_DOT_DIMS[mode], ((), ()))
    col0 = 0
    if b_cols is not None:
        assert mode != "tn" and b_cols[0] % (tn if mode == "nn" else tk) == 0
        col0 = b_cols[0] // (tn if mode == "nn" else tk)
    assert not b_slots or (mode == "nn" or (mode == "nt" and nk == 1 and b_cols is None))

    def body(*refs):
        a_ref, b_ref = refs[:2]
        c_ref = refs[2] if add is not None else None
        e_ref = refs[2 + (add is not None)] if epilogue is not None else None
        o_ref = refs[2 + (add is not None) + (epilogue is not None) + (after is not None)]
        if b_slots and mode == "nt":
            part = lax.dot_general(a_ref[:, :slot_w], b_ref[0], dims, preferred_element_type=F32)
            for sl in range(1, b_ref.shape[0]):
                part += lax.dot_general(a_ref[:, sl * slot_w:(sl + 1) * slot_w], b_ref[sl], dims, preferred_element_type=F32)
        else:
            part = lax.dot_general(a_ref[...], b_ref[...], dims, preferred_element_type=F32)

        def finish(r):
            if scale is not None:
                r = r * scale
            if add is not None:
                r = r + c_ref[...]
            if epilogue is not None:
                r = epilogue[0](r, e_ref[...])
            o_ref[...] = r.astype(out_dtype)

        if nk == 1:
            finish(part)
            return
        acc = refs[-1]
        k = pl.program_id(2)

        @pl.when(k == 0)
        def _():
            acc[...] = part

        @pl.when(jnp.logical_and(k > 0, k < nk - 1))
        def _():
            acc[...] += part

        @pl.when(k == nk - 1)
        def _():
            finish(acc[...] + part)

    if mode == "tn":
        a_spec = pl.BlockSpec((tk, tm), lambda i, j, k: (k, i))
    else:
        a_spec = pl.BlockSpec((tm, tk), lambda i, j, k: (i, k))
    if b_slots and mode == "nn":
        b_spec = pl.BlockSpec((None, tk, tn), lambda i, j, k: (j + col0, k, 0))
    elif b_slots:
        b_spec = pl.BlockSpec((b.shape[0], tn, slot_w), lambda i, j, k: (0, j, 0))
    elif mode == "nt":
        b_spec = pl.BlockSpec((tn, tk), lambda i, j, k: (j, k + col0))
    else:
        b_spec = pl.BlockSpec((tk, tn), lambda i, j, k: (k, j + col0))
    o_spec = pl.BlockSpec((tm, tn), lambda i, j, k: (i, j))
    in_specs = [a_spec, b_spec] + ([o_spec] if add is not None else [])
    args = (a, b) + ((add,) if add is not None else ())
    if epilogue is not None:
        in_specs.append(o_spec)
        args += (epilogue[1],)
    if after is not None:
        in_specs.append(pl.BlockSpec(after.shape, lambda i, j, k: (0, 0)))
        args += (after,)
    out_shape = jax.ShapeDtypeStruct((M, N), out_dtype)
    if col_slots:
        assert add is None
        o_spec = pl.BlockSpec((None, tm, tn), lambda i, j, k: (j, i, 0))
        out_shape = jax.ShapeDtypeStruct((N_DEV, M, tn), out_dtype)
    return pl.pallas_call(
        body, name=name, grid=(M // tm, N // tn, nk),
        in_specs=in_specs, out_specs=o_spec,
        out_shape=out_shape,
        scratch_shapes=[pltpu.VMEM((tm, tn), F32)] if nk > 1 else [],
        compiler_params=_cparams("parallel", "parallel", "arbitrary"),
    )(*args)


class _NatIn:
    def __init__(self, ref):
        self.ref = ref

    def __getitem__(self, idx):
        v = jnp.swapaxes(self.ref[...], 0, 1)
        return v.reshape(v.shape[0] * N_SEG, v.shape[2])


class _NatOut:
    def __init__(self, ref):
        self.ref = ref

    def __setitem__(self, idx, val):
        self.ref[...] = jnp.swapaxes(val.reshape(val.shape[0] // N_SEG, N_SEG, val.shape[1]), 0, 1)


def _rowcall(body, name, n_rows, ins, outs, tile_rows=256):
    tr = _tile(n_rows, tile_rows, SUBLANES * 2)
    n_in = len(ins)
    in_kinds = [k for _, k in ins]
    kinds = [k for _, _, k in outs]

    def kern(*refs):
        @pl.when(pl.program_id(0) == 0)
        def _():
            for r, kind in zip(refs[n_in:], kinds):
                if kind == "acc":
                    r[...] = jnp.zeros_like(r)

        wrapped = [_NatIn(r) if k == "nat" else r for r, k in zip(refs[:n_in], in_kinds)]
        wrapped += [_NatOut(r) if k == "nat" else r for r, k in zip(refs[n_in:], kinds)]
        body(*wrapped)

    in_specs, args = [], []
    for arr, kind in ins:
        if kind == "row":
            in_specs.append(pl.BlockSpec((tr, arr.shape[1]), lambda i: (i, 0)))
        elif kind == "nat":
            in_specs.append(pl.BlockSpec((N_SEG, tr // N_SEG, arr.shape[1]), lambda i: (0, i, 0)))
            arr = arr.reshape(N_SEG, n_rows // N_SEG, arr.shape[1])
        else:
            in_specs.append(pl.BlockSpec(arr.shape, lambda i, nd=arr.ndim: (0,) * nd))
        args.append(arr)
    out_specs, out_shape = [], []
    for width, dtype, kind in outs:
        if kind == "row":
            out_specs.append(pl.BlockSpec((tr, width), lambda i: (i, 0)))
            out_shape.append(jax.ShapeDtypeStruct((n_rows, width), dtype))
        elif kind == "right":
            out_specs.append(pl.BlockSpec((tr, width), lambda i: (i, 1)))
            out_shape.append(jax.ShapeDtypeStruct((n_rows, 2 * width), dtype))
        elif kind == "nat":
            out_specs.append(pl.BlockSpec((N_SEG, tr // N_SEG, width), lambda i: (0, i, 0)))
            out_shape.append(jax.ShapeDtypeStruct((N_SEG, n_rows // N_SEG, width), dtype))
        else:
            out_specs.append(pl.BlockSpec((1, width), lambda i: (0, 0)))
            out_shape.append(jax.ShapeDtypeStruct((1, width), F32))
    res = pl.pallas_call(
        kern, name=name, grid=(n_rows // tr,), in_specs=in_specs, out_specs=out_specs, out_shape=out_shape,
        compiler_params=_cparams("arbitrary"),
    )(*args)
    return [r.reshape(n_rows, r.shape[2]) if k == "nat" else r for r, k in zip(res, kinds)]


def _rstd(x):
    return lax.rsqrt(jnp.mean(x * x, axis=-1, keepdims=True) + RMS_EPS)


def _rms_bwd(x, g, dy):
    xh = x * _rstd(x)
    dxh = dy * g
    dx = _rstd(x) * (dxh - xh * jnp.mean(dxh * xh, axis=-1, keepdims=True))
    return dx, jnp.sum(dy * xh, axis=0, keepdims=True)


def _silu(z):
    return z * jax.nn.sigmoid(z)


def _norm_cast(x, g, name, x_kind="row"):
    def body(x_ref, g_ref, o_ref):
        x = x_ref[...]
        o_ref[...] = (x * _rstd(x) * g_ref[...]).astype(BF16)

    return _rowcall(body, name, x.shape[0], [(x, x_kind), (g, "full")], [(x.shape[1], BF16, "row")])[0]


def _resid_norm2(x, o, g_post, g_kv, g_pre, name):
    def body(x_ref, o_ref, go_ref, gk_ref, gp_ref, h_ref, nk_ref, np_ref):
        o = o_ref[...]
        h = x_ref[...] + o * _rstd(o) * go_ref[...]
        h_ref[...] = h
        hn = h * _rstd(h)
        nk_ref[...] = (hn * gk_ref[...]).astype(BF16)
        np_ref[...] = (hn * gp_ref[...]).astype(BF16)

    d = x.shape[1]
    return _rowcall(body, name, x.shape[0], [(x, "nat"), (o, "row"), (g_post, "full"), (g_kv, "full"), (g_pre, "full")],
                    [(d, F32, "nat"), (d, BF16, "nat"), (d, BF16, "nat")])


def _post_norm_loss(o, g, h1, target, name):
    d = o.shape[1]

    def body(o_ref, g_ref, h_ref, t_ref, dh_ref, do_ref, acc_ref, dg_ref):
        o = o_ref[...]
        e = h_ref[...] + o * _rstd(o) * g_ref[...] - t_ref[...]
        dh = e * (1.0 / d)
        dh_ref[...] = dh
        acc_ref[...] += jnp.sum(e * e, axis=0, keepdims=True)
        dx, dg = _rms_bwd(o, g_ref[...], dh)
        do_ref[...] = dx.astype(BF16)
        dg_ref[...] += dg

    return _rowcall(body, name, o.shape[0], [(o, "row"), (g, "full"), (h1, "row"), (target, "row")],
                    [(d, F32, "row"), (d, BF16, "row"), (d, F32, "acc"), (d, F32, "acc")])


def _gate_bwd(d_oz, o, z, name):
    def body(d_ref, o_ref, z_ref, do_ref, dz_ref):
        _, vjp = jax.vjp(lambda o, z: o * _silu(z), o_ref[...].astype(F32), z_ref[...].astype(F32))
        do, dz = vjp(d_ref[...].astype(F32))
        do_ref[...] = do.astype(BF16)
        dz_ref[...] = dz.astype(BF16)

    w = o.shape[1]
    return _rowcall(body, name, o.shape[0], [(d_oz, "row"), (o, "row"), (z, "row")], [(w, BF16, "row"), (w, BF16, "right")])


def _norm_bwd2(dh2, h1, dxn1, dhn_kv, g_pre, g_kv, o0, g_post0, name):
    def body(dh2_ref, h_ref, d1_ref, dk_ref, gp_ref, gk_ref, o_ref, go_ref, dh1_ref, do_ref, dgp_ref, dgk_ref, dgo_ref):
        h = h_ref[...]
        dx1, dg1 = _rms_bwd(h, gp_ref[...], d1_ref[...].astype(F32))
        dxk, dgk = _rms_bwd(h, gk_ref[...], dk_ref[...].astype(F32))
        dh1 = dh2_ref[...] + dx1 + dxk
        dh1_ref[...] = dh1
        dgp_ref[...] += dg1
        dgk_ref[...] += dgk
        dxo, dgo = _rms_bwd(o_ref[...], go_ref[...], dh1)
        do_ref[...] = dxo.astype(BF16)
        dgo_ref[...] += dgo

    d = h1.shape[1]
    return _rowcall(body, name, h1.shape[0],
                    [(dh2, "nat"), (h1, "nat"), (dxn1, "nat"), (dhn_kv, "nat"), (g_pre, "full"), (g_kv, "full"),
                     (o0, "row"), (g_post0, "full")],
                    [(d, F32, "nat"), (d, BF16, "row"), (d, F32, "acc"), (d, F32, "acc"), (d, F32, "acc")])


def _norm_bwd1(dres, x, dxn, g, name):
    def body(dr_ref, x_ref, dn_ref, g_ref, dx_ref, dg_ref):
        dx, dg = _rms_bwd(x_ref[...], g_ref[...], dn_ref[...].astype(F32))
        dx_ref[...] = dr_ref[...] + dx
        dg_ref[...] += dg

    d = x.shape[1]
    return _rowcall(body, name, x.shape[0], [(dres, "nat"), (x, "nat"), (dxn, "row"), (g, "full")],
                    [(d, F32, "nat"), (d, F32, "acc")])


def _s5_gate(y_ssm, gp, b_glu, z, name):
    def body(y_ref, gp_ref, b_ref, z_ref, o_ref):
        yg = jax.nn.gelu(y_ref[...].astype(F32))
        o_ref[...] = (yg * jax.nn.sigmoid(gp_ref[...] + b_ref[...]) * _silu(z_ref[...].astype(F32))).astype(BF16)

    return _rowcall(body, name, y_ssm.shape[0], [(y_ssm, "row"), (gp, "row"), (b_glu, "full"), (z, "row")],
                    [(y_ssm.shape[1], BF16, "row")])[0]


def _s5_gate_bwd(dy3, y_ssm, gp, b_glu, z, name):
    def body(d_ref, y_ref, gp_ref, b_ref, z_ref, dz_ref, dgp_ref, dyg_ref, db_ref):
        yg = jax.nn.gelu(y_ref[...].astype(F32))
        _, vjp = jax.vjp(lambda yg, gp, z: yg * jax.nn.sigmoid(gp) * _silu(z), yg, gp_ref[...] + b_ref[...],
                         z_ref[...].astype(F32))
        dyg, dgp, dz = vjp(d_ref[...].astype(F32))
        dz_ref[...] = dz.astype(BF16)
        dgp_ref[...] = dgp.astype(BF16)
        dyg_ref[...] = dyg
        db_ref[...] += jnp.sum(dgp, axis=0, keepdims=True)

    w = y_ssm.shape[1]
    return _rowcall(body, name, y_ssm.shape[0],
                    [(dy3, "row"), (y_ssm, "row"), (gp, "row"), (b_glu, "full"), (z, "row")],
                    [(w, BF16, "right"), (w, BF16, "row"), (w, F32, "row"), (w, F32, "acc")])


def _cast_bf16(x, name):
    r, c = x.shape
    by_cols = r % (2 * SUBLANES) != 0
    tr, tc = (r, _tile(c, 256)) if by_cols else (_tile(r, 512, 2 * SUBLANES), c)
    pos = (lambda i: (0, i)) if by_cols else (lambda i: (i, 0))

    def body(x_ref, o_ref):
        o_ref[...] = x_ref[...].astype(BF16)

    return pl.pallas_call(
        body, name=name, grid=(c // tc if by_cols else r // tr,),
        in_specs=[pl.BlockSpec((tr, tc), pos)], out_specs=pl.BlockSpec((tr, tc), pos),
        out_shape=jax.ShapeDtypeStruct((r, c), BF16), compiler_params=_cparams("parallel"),
    )(x)


def _concat_cast(a, b, name):
    def body(a_ref, b_ref, o_ref):
        w = a_ref.shape[1]
        o_ref[:, :w] = a_ref[...].astype(BF16)
        o_ref[:, w:] = b_ref[...].astype(BF16)

    return _rowcall(body, name, a.shape[0], [(a, "row"), (b, "row")], [(a.shape[1] + b.shape[1], BF16, "row")])[0]


def _disc(ar, ai, ldt):
    dt = jnp.exp(ldt)
    mag = jnp.exp(ar * dt)
    abr = mag * jnp.cos(ai * dt)
    abi = mag * jnp.sin(ai * dt)
    den = ar * ar + ai * ai
    nr = abr - 1.0
    return abr, abi, (nr * ar + abi * ai) / den, (abi * ar - nr * ai) / den


def _s5_disc_fwd(a_re, a_im, ldt):
    def body(ar, ai, ld, o1, o2, o3, o4):
        o1[...], o2[...], o3[...], o4[...] = _disc(ar[...], ai[...], ld[...])

    sh = jax.ShapeDtypeStruct(a_re.shape, F32)
    return pl.pallas_call(body, name="s5_disc_fwd", out_shape=(sh, sh, sh, sh))(a_re, a_im, ldt)


def _s5_disc_bwd(a_re, a_im, ldt, d_abr, d_abi, d_cr, d_ci):
    def body(ar, ai, ld, g1, g2, g3, g4, o1, o2, o3):
        _, vjp = jax.vjp(_disc, ar[...], ai[...], ld[...])
        o1[...], o2[...], o3[...] = vjp((g1[...], g2[...], g3[...], g4[...]))

    sh = jax.ShapeDtypeStruct(a_re.shape, F32)
    return pl.pallas_call(body, name="s5_disc_bwd", out_shape=(sh, sh, jax.ShapeDtypeStruct(ldt.shape, F32)))(
        a_re, a_im, ldt, d_abr, d_abi, d_cr, d_ci)


def _bbar(cr, ci, br, bi):
    return cr * br - ci * bi, cr * bi + ci * br


def _s5_bbar_fwd(cr_col, ci_col, b_re, b_im):
    def body(cr, ci, br, bi, o1, o2):
        o1[...], o2[...] = _bbar(cr[...], ci[...], br[...], bi[...])

    w = b_re.shape[1]
    return _rowcall(body, "s5_bbar_fwd", b_re.shape[0], [(cr_col, "row"), (ci_col, "row"), (b_re, "row"), (b_im, "row")],
                    [(w, F32, "row"), (w, F32, "row")], tile_rows=1024)


def _s5_bbar_bwd(cr_col, ci_col, b_re, b_im, d_re, d_im):
    def body(cr, ci, br, bi, g1, g2, o1, o2, o3, o4):
        _, vjp = jax.vjp(_bbar, cr[...], ci[...], br[...], bi[...])
        o1[...], o2[...], o3[...], o4[...] = vjp((g1[...], g2[...]))

    w = b_re.shape[1]
    return _rowcall(body, "s5_bbar_bwd", b_re.shape[0],
                    [(cr_col, "row"), (ci_col, "row"), (b_re, "row"), (b_im, "row"), (d_re, "row"), (d_im, "row")],
                    [(1, F32, "row"), (1, F32, "row"), (w, F32, "row"), (w, F32, "row")], tile_rows=1024)


def _block_diag(t):
    g, a, b = t.shape
    nb = g // GROUPS_PER_BLOCK
    t4 = t.reshape(nb, GROUPS_PER_BLOCK, a, b).transpose(0, 1, 3, 2)
    eye = jnp.eye(GROUPS_PER_BLOCK, dtype=t.dtype)
    return (t4[:, :, :, None, :] * eye[None, :, None, :, None]).reshape(nb, GROUPS_PER_BLOCK * b, GROUPS_PER_BLOCK * a)


def _block_diag_extract(d, a, b):
    nb = d.shape[0]
    d5 = d.reshape(nb, GROUPS_PER_BLOCK, b, GROUPS_PER_BLOCK, a)
    diag = jnp.stack([d5[:, g, :, g, :] for g in range(GROUPS_PER_BLOCK)], axis=1)
    return diag.transpose(0, 1, 3, 2).reshape(nb * GROUPS_PER_BLOCK, a, b)


def _scan_step(ar, ai, hr, hi, xr, xi):
    return ar * hr - ai * hi + xr, ar * hi + ai * hr + xi


def _s5_blocks_per_step(nb, full):
    want = 2 if full else 4
    while nb % want:
        want //= 2
    return want


def _s5_scan_fwd(u, bd_re, bd_im, cd_re, cd_im, ab_re, ab_im, init_re, init_im, d_row, full, name):
    s, w = u.shape
    nb = w // LANES
    rows = _tile(s, 512, SUBLANES)
    nc = s // rows
    steps = rows // N_SEG
    ns = nb * BLOCK_STATE

    nblk = _s5_blocks_per_step(nb, full)

    def body(u_ref, bdr, bdi, cdr, cdi, ar_ref, ai_ref, ir_ref, ii_ref, d_ref, *outs):
        if full:
            y_ref, yg_ref, hr_out, hi_out, er_ref, ei_ref, hr_ref, hi_ref, cr, ci = outs
        else:
            er_ref, ei_ref, hr_ref, hi_ref, cr, ci = outs
        c = pl.program_id(1)
        cols = lambda b, width: slice(b * width, (b + 1) * width)

        @pl.when(c == 0)
        def _():
            cr[...] = ir_ref[...]
            ci[...] = ii_ref[...]

        for b in range(nblk):
            ub = u_ref[:, cols(b, LANES)].astype(BF16)
            hr_ref[:, cols(b, BLOCK_STATE)] = jnp.dot(ub, bdr[b], preferred_element_type=F32)
            hi_ref[:, cols(b, BLOCK_STATE)] = jnp.dot(ub, bdi[b], preferred_element_type=F32)
        ar, ai = ar_ref[...], ai_ref[...]

        hr, hi = cr[...], ci[...]
        for j in range(steps):
            rows_j = pl.ds(j * N_SEG, N_SEG)
            hr, hi = _scan_step(ar, ai, hr, hi, hr_ref[rows_j, :], hi_ref[rows_j, :])
            hr_ref[rows_j, :] = hr
            hi_ref[rows_j, :] = hi
        cr[...] = hr
        ci[...] = hi
        if full:
            hr_out[...] = hr_ref[...].astype(BF16)
            hi_out[...] = hi_ref[...].astype(BF16)
            for b in range(nblk):
                st_b, ln_b = cols(b, BLOCK_STATE), cols(b, LANES)
                y = (jnp.dot(hr_out[:, st_b], cdr[b], preferred_element_type=F32)
                     + jnp.dot(hi_out[:, st_b], cdi[b], preferred_element_type=F32)
                     + d_ref[:, ln_b] * u_ref[:, ln_b])
                y_ref[:, ln_b] = y.astype(BF16)
                yg_ref[:, ln_b] = jax.nn.gelu(y).astype(BF16)

        @pl.when(c == nc - 1)
        def _():
            er_ref[...] = hr
            ei_ref[...] = hi

    lanes, states = LANES * nblk, BLOCK_STATE * nblk
    blk3 = lambda a: pl.BlockSpec((nblk,) + a.shape[1:], lambda k, c: (k, 0, 0))
    seg = pl.BlockSpec((N_SEG, states), lambda k, c: (0, k))
    st = pl.BlockSpec((rows, states), lambda k, c: (c, k))
    in_specs = [pl.BlockSpec((rows, lanes), lambda k, c: (c, k)), blk3(bd_re), blk3(bd_im), blk3(cd_re), blk3(cd_im),
                seg, seg, seg, seg, pl.BlockSpec((1, lanes), lambda k, c: (0, k))]
    seg_shape = jax.ShapeDtypeStruct((N_SEG, ns), F32)
    st_shape = jax.ShapeDtypeStruct((s, ns), BF16)
    scratch = [pltpu.VMEM((rows, states), F32)] * 2 + [pltpu.VMEM((N_SEG, states), F32)] * 2
    if full:
        ych = pl.BlockSpec((rows, lanes), lambda k, c: (c, k))
        out_specs = [ych, ych, st, st, seg, seg]
        out_shape = [jax.ShapeDtypeStruct((s, w), BF16), jax.ShapeDtypeStruct((s, w), BF16), st_shape, st_shape, seg_shape, seg_shape]
    else:
        out_specs = [seg, seg]
        out_shape = [seg_shape, seg_shape]
    return pl.pallas_call(
        body, name=name, grid=(nb // nblk, nc), in_specs=in_specs, out_specs=out_specs, out_shape=out_shape,
        scratch_shapes=scratch, compiler_params=_cparams("parallel", "arbitrary"),
    )(u, bd_re, bd_im, cd_re, cd_im, ab_re, ab_im, init_re, init_im, d_row)


def _s5_seg_fix(e_re, e_im, ab_re, ab_im, seg_len, reverse, name):
    assert seg_len & (seg_len - 1) == 0

    def body(er, ei, ar, ai, o_re, o_im):
        pr, pi = ar[0:1, :], ai[0:1, :]
        for _ in range(int(math.log2(seg_len))):
            pr, pi = pr * pr - pi * pi, 2.0 * pr * pi
        tr = jnp.zeros_like(pr)
        ti = jnp.zeros_like(pr)
        order = list(range(N_SEG - 1, -1, -1)) if reverse else list(range(N_SEG))
        for n, sgm in enumerate(order):
            o_re[sgm:sgm + 1, :] = tr
            o_im[sgm:sgm + 1, :] = ti
            if n < N_SEG - 1:
                tr, ti = _scan_step(pr, pi, tr, ti, er[sgm:sgm + 1, :], ei[sgm:sgm + 1, :])

    sh = jax.ShapeDtypeStruct(e_re.shape, F32)
    return pl.pallas_call(body, name=name, out_shape=(sh, sh))(e_re, e_im, ab_re, ab_im)


def _s5_scan_bwd(dy, u, h_re, h_im, bd_re, bd_im, cd_re, cd_im, ab_re, ab_imn, gin_re, gin_im, d_row, full, name, duz=None):
    s, w = u.shape
    nb = w // LANES
    rows = _tile(s, 512, SUBLANES)
    nc = s // rows
    steps = rows // N_SEG
    ns = nb * BLOCK_STATE

    nblk = _s5_blocks_per_step(nb, full)

    def body(dy_ref, u_ref, hr_ref, hi_ref, bdr, bdi, cdr, cdi, ar_ref, ai_ref, ir_ref, ii_ref, d_ref, *outs):
        if full:
            _, du_ref, dbr_ref, dbi_ref, dcr_ref, dci_ref, dar_ref, dai_ref, dd_ref, gr, gi, accr, acci = outs
        else:
            er_ref, ei_ref, gr, gi = outs
        c = pl.program_id(1)
        cols = lambda b, width: slice(b * width, (b + 1) * width)

        @pl.when(c == 0)
        def _():
            gr[pl.ds(rows, N_SEG), :] = ir_ref[...]
            gi[pl.ds(rows, N_SEG), :] = ii_ref[...]
            if full:
                for r in (dbr_ref, dbi_ref, dcr_ref, dci_ref, dd_ref, accr, acci):
                    r[...] = jnp.zeros_like(r)

        nt = (_DOT_DIMS["nt"], ((), ()))
        tn = (_DOT_DIMS["tn"], ((), ()))
        for b in range(nblk):
            dyb = dy_ref[:, cols(b, LANES)]
            gr[pl.ds(0, rows), cols(b, BLOCK_STATE)] = lax.dot_general(dyb, cdr[b], nt, preferred_element_type=F32)
            gi[pl.ds(0, rows), cols(b, BLOCK_STATE)] = lax.dot_general(dyb, cdi[b], nt, preferred_element_type=F32)
        ar, ai = ar_ref[...], ai_ref[...]

        g0r, g0i = gr[pl.ds(rows, N_SEG), :], gi[pl.ds(rows, N_SEG), :]
        for j in range(steps - 1, -1, -1):
            rows_j = pl.ds(j * N_SEG, N_SEG)
            g0r, g0i = _scan_step(ar, ai, g0r, g0i, gr[rows_j, :], gi[rows_j, :])
            gr[rows_j, :] = g0r
            gi[rows_j, :] = g0i
        if full:
            for b in range(nblk):
                st_b, ln_b = cols(b, BLOCK_STATE), cols(b, LANES)
                hr, hi = hr_ref[:, st_b], hi_ref[:, st_b]
                gnr, gni = gr[pl.ds(N_SEG, rows), st_b], gi[pl.ds(N_SEG, rows), st_b]
                accr[:, st_b] += jnp.sum((gnr * hr + gni * hi).reshape(steps, N_SEG, BLOCK_STATE), axis=0)
                acci[:, st_b] += jnp.sum((gni * hr - gnr * hi).reshape(steps, N_SEG, BLOCK_STATE), axis=0)
                dyb = dy_ref[:, ln_b]
                ub = u_ref[:, ln_b].astype(BF16)
                gbr, gbi = gr[pl.ds(0, rows), st_b].astype(BF16), gi[pl.ds(0, rows), st_b].astype(BF16)
                dcr_ref[b] += lax.dot_general(hr.astype(BF16), dyb, tn, preferred_element_type=F32)
                dci_ref[b] += lax.dot_general(hi.astype(BF16), dyb, tn, preferred_element_type=F32)
                dbr_ref[b] += lax.dot_general(ub, gbr, tn, preferred_element_type=F32)
                dbi_ref[b] += lax.dot_general(ub, gbi, tn, preferred_element_type=F32)
                du_ref[:, ln_b] = (lax.dot_general(gbr, bdr[b], nt, preferred_element_type=F32)
                                   + lax.dot_general(gbi, bdi[b], nt, preferred_element_type=F32)
                                   + d_ref[:, ln_b] * dy_ref[:, ln_b].astype(F32)).astype(BF16)
                dd_ref[:, ln_b] += jnp.sum(dy_ref[:, ln_b].astype(F32) * u_ref[:, ln_b], axis=0, keepdims=True)
        gr[pl.ds(rows, N_SEG), :] = g0r
        gi[pl.ds(rows, N_SEG), :] = g0i

        @pl.when(c == nc - 1)
        def _():
            if full:
                dar_ref[...] = jnp.sum(accr[...], axis=0, keepdims=True)
                dai_ref[...] = jnp.sum(acci[...], axis=0, keepdims=True)
            else:
                er_ref[...] = g0r
                ei_ref[...] = g0i

    lanes, states = LANES * nblk, BLOCK_STATE * nblk
    rev = lambda k, c: (nc - 1 - c, k)
    blk3 = lambda a: pl.BlockSpec((nblk,) + a.shape[1:], lambda k, c: (k, 0, 0))
    seg = pl.BlockSpec((N_SEG, states), lambda k, c: (0, k))
    st = pl.BlockSpec((rows, states), rev)
    ch = pl.BlockSpec((rows, lanes), rev)
    vec = pl.BlockSpec((1, lanes), lambda k, c: (0, k))
    if not full:
        st = pl.BlockSpec((rows, states), lambda k, c: (0, k))
    in_specs = [ch, ch if full else pl.BlockSpec((rows, lanes), lambda k, c: (0, k)), st, st,
                blk3(bd_re), blk3(bd_im), blk3(cd_re), blk3(cd_im), seg, seg, seg, seg, vec]
    args = [dy, u, h_re, h_im, bd_re, bd_im, cd_re, cd_im, ab_re, ab_imn, gin_re, gin_im, d_row]
    gbuf = [pltpu.VMEM((rows + N_SEG, states), F32)] * 2
    if full:
        row1 = pl.BlockSpec((1, states), lambda k, c: (0, k))
        out_specs = [ch, blk3(bd_re), blk3(bd_im), blk3(cd_re), blk3(cd_im), row1, row1, vec]
        out_shape = [jax.ShapeDtypeStruct(duz.shape, BF16),
                     jax.ShapeDtypeStruct(bd_re.shape, F32), jax.ShapeDtypeStruct(bd_im.shape, F32),
                     jax.ShapeDtypeStruct(cd_re.shape, F32), jax.ShapeDtypeStruct(cd_im.shape, F32),
                     jax.ShapeDtypeStruct((1, ns), F32), jax.ShapeDtypeStruct((1, ns), F32),
                     jax.ShapeDtypeStruct((1, w), F32)]
        scratch = gbuf + [pltpu.VMEM((N_SEG, states), F32)] * 2
        in_specs.append(pl.BlockSpec(memory_space=pl.ANY))
        args.append(duz)
        aliases = {len(args) - 1: 0}
    else:
        out_specs = [seg, seg]
        out_shape = [jax.ShapeDtypeStruct((N_SEG, ns), F32)] * 2
        scratch = gbuf
        aliases = {}
    return pl.pallas_call(
        body, name=name, grid=(nb // nblk, nc), in_specs=in_specs, out_specs=out_specs, out_shape=out_shape,
        input_output_aliases=aliases, scratch_shapes=scratch, compiler_params=_cparams("parallel", "arbitrary"),
    )(*args)


def _log_sigmoid(x):
    return jnp.minimum(x, 0.0) - jnp.log(1.0 + jnp.exp(-jnp.abs(x)))


def _tri(n, upper):
    r = lax.broadcasted_iota(jnp.int32, (n, n), 0)
    c = lax.broadcasted_iota(jnp.int32, (n, n), 1)
    return jnp.where((c >= r) if upper else (r >= c), 1.0, 0.0).astype(F32)


def _cum_fwd(f_logit, b_row, name):
    s, w = f_logit.shape
    t = _tile(s, 256, SUBLANES)

    def body(f_ref, b_ref, o_ref, carry):
        @pl.when(pl.program_id(0) == 0)
        def _():
            carry[...] = jnp.zeros_like(carry)

        lf = _log_sigmoid(f_ref[...] + b_ref[...])
        cum = jnp.dot(_tri(t, False), lf, precision=lax.Precision.HIGHEST, preferred_element_type=F32) + carry[...]
        o_ref[...] = cum * LOG2E
        carry[...] = cum[t - 1:t, :]

    return pl.pallas_call(
        body, name=name, grid=(s // t,),
        in_specs=[pl.BlockSpec((t, w), lambda i: (i, 0)), pl.BlockSpec((1, w), lambda i: (0, 0))],
        out_specs=pl.BlockSpec((t, w), lambda i: (i, 0)), out_shape=jax.ShapeDtypeStruct((s, w), F32),
        scratch_shapes=[pltpu.VMEM((1, w), F32)], compiler_params=_cparams("arbitrary"),
    )(f_logit, b_row)


def _cum_bwd(dcq, dck, f_logit, b_row, name):
    s, w = f_logit.shape
    t = _tile(s, 256, SUBLANES)
    nt = s // t

    def body(q_ref, k_ref, f_ref, b_ref, df_ref, db_ref, carry):
        @pl.when(pl.program_id(0) == 0)
        def _():
            carry[...] = jnp.zeros_like(carry)
            db_ref[...] = jnp.zeros_like(db_ref)

        dc = q_ref[...] - k_ref[...]
        rc = jnp.dot(_tri(t, True), dc, precision=lax.Precision.HIGHEST, preferred_element_type=F32) + carry[...]
        carry[...] = rc[0:1, :]
        df = rc * (1.0 - jax.nn.sigmoid(f_ref[...] + b_ref[...]))
        df_ref[...] = df.astype(BF16)
        db_ref[...] += jnp.sum(df, axis=0, keepdims=True)

    rev = pl.BlockSpec((t, w), lambda i: (nt - 1 - i, 0))
    one = pl.BlockSpec((1, w), lambda i: (0, 0))
    return pl.pallas_call(
        body, name=name, grid=(nt,), in_specs=[rev, rev, rev, one], out_specs=[rev, one],
        out_shape=[jax.ShapeDtypeStruct((s, w), BF16), jax.ShapeDtypeStruct((1, w), F32)],
        scratch_shapes=[pltpu.VMEM((1, w), F32)], compiler_params=_cparams("arbitrary"),
    )(dcq, dck, f_logit, b_row)


def _head_col(cum_tile, h):
    lane = lax.broadcasted_iota(jnp.int32, cum_tile.shape, 1)
    return jnp.sum(jnp.where(lane == h, cum_tile, 0.0), axis=1, keepdims=True)


def _attn_tiles(s):
    return _tile(s, 512, LANES)


def _exp2_rows(sc, sub):
    return jnp.concatenate([jnp.exp2(sc[:, b * LANES:(b + 1) * LANES] - sub) for b in range(sc.shape[1] // LANES)], axis=1)


def _row_of(rep):
    return jnp.transpose(rep)[0:1, :]


def _causal(sc, keys_on_rows):
    r = lax.broadcasted_iota(jnp.int32, sc.shape, 0)
    c = lax.broadcasted_iota(jnp.int32, sc.shape, 1)
    return jnp.where((r <= c) if keys_on_rows else (c <= r), sc, NEG_INF)


def _fox_fwd(q2, kv, cum2_t, z, name):
    s, w = q2.shape
    nh = w // HEAD_DIM
    tq = _attn_tiles(s)
    nq = s // tq
    nt = (_DOT_DIMS["nt"], ((), ()))

    def body(q_ref, k_ref, v_ref, ct_ref, z_ref, o_ref, oz_ref, lse_row_ref, m_s, acc_s, vaug, s_buf):
        i = pl.program_id(1)

        @pl.when(i == 0)
        def _():
            vaug[:, :HEAD_DIM] = v_ref[...]
            vaug[:, HEAD_DIM:] = jnp.ones((s, LANES), BF16)

        qb = q_ref[...]
        m_s[...] = jnp.full_like(m_s, NEG_INF)
        acc_s[...] = jnp.zeros_like(acc_s)

        def scores(j):
            off = pl.multiple_of(j * tq, tq)
            return lax.dot_general(qb, k_ref[pl.ds(off, tq), :], nt, preferred_element_type=F32) - ct_ref[:, pl.ds(off, tq)]

        def softmax_pv(j, sc):
            m_old = m_s[...]
            m_new = jnp.maximum(m_old, jnp.max(sc, axis=1, keepdims=True))
            p = _exp2_rows(sc, m_new)
            alpha = jnp.exp2(m_old - m_new)
            pv = jnp.dot(p.astype(BF16), vaug[pl.ds(pl.multiple_of(j * tq, tq), tq), :], preferred_element_type=F32)
            acc_s[...] = jnp.concatenate([alpha, alpha], axis=1) * acc_s[...] + pv
            m_s[...] = m_new

        s_buf[...] = scores(0)

        def loop(j, carry):
            nxt = scores(j + 1)
            softmax_pv(j, s_buf[...])
            s_buf[...] = nxt
            return carry

        lax.fori_loop(0, i, loop, 0)
        softmax_pv(i, _causal(s_buf[...], False))
        l = acc_s[:, HEAD_DIM:]
        o = acc_s[:, :HEAD_DIM] / l
        o_ref[...] = o.astype(BF16)
        oz_ref[...] = (o * _silu(z_ref[...].astype(F32))).astype(BF16)
        lse_row_ref[...] = _row_of(m_s[...] + jnp.log(l) * LOG2E)

    return pl.pallas_call(
        body, name=name, grid=(nh, nq),
        in_specs=[pl.BlockSpec((tq, HEAD_DIM), lambda h, i: (i, h)),
                  pl.BlockSpec((s, HEAD_DIM), lambda h, i: (0, h)),
                  pl.BlockSpec((s, HEAD_DIM), lambda h, i: (0, nh + h)),
                  pl.BlockSpec((None, 1, s), lambda h, i: (h, 0, 0)),
                  pl.BlockSpec((tq, HEAD_DIM), lambda h, i: (i, h))],
        out_specs=[pl.BlockSpec((tq, HEAD_DIM), lambda h, i: (i, h)),
                   pl.BlockSpec((tq, HEAD_DIM), lambda h, i: (i, h)),
                   pl.BlockSpec((None, 1, tq), lambda h, i: (h, 0, i))],
        out_shape=[jax.ShapeDtypeStruct((s, w), BF16), jax.ShapeDtypeStruct((s, w), BF16),
                   jax.ShapeDtypeStruct((nh, 1, s), F32)],
        scratch_shapes=[pltpu.VMEM((tq, LANES), F32), pltpu.VMEM((tq, HEAD_DIM + LANES), F32),
                        pltpu.VMEM((s, HEAD_DIM + LANES), BF16), pltpu.VMEM((tq, tq), F32)],
        compiler_params=_cparams("arbitrary", "arbitrary"),
    )(q2, kv, kv, cum2_t, z)


def _fox_bwd(q2, kv, do, o, lse2_t, cum2, dqz, name):
    s, w = q2.shape
    nh = w // HEAD_DIM
    tk = _attn_tiles(s)
    nk = s // tk
    scale = HEAD_DIM ** -0.5
    nt = (_DOT_DIMS["nt"], ((), ()))
    tn = (_DOT_DIMS["tn"], ((), ()))

    def body(q_ref, k_ref, v_ref, do_ref, o_ref, lse_ref, c_ref, _, dk_ref, dv_ref, dq_ref, dcq_ref, dck_ref,
             dk_s, dv_s, dc_s, dq_s, dcq_s, dl_s, s_buf, dp_buf):
        h, j = pl.program_id(0), pl.program_id(1)

        @pl.when(j == 0)
        def _():
            dq_s[...] = jnp.zeros_like(dq_s)
            dcq_s[...] = jnp.zeros_like(dcq_s)
            for i in range(nk):
                rows = pl.ds(i * tk, tk)
                d = jnp.sum(do_ref[rows, :].astype(F32) * o_ref[rows, :].astype(F32), axis=1, keepdims=True)
                dl_s[:, i * tk:(i + 1) * tk] = _row_of(jnp.broadcast_to(d, (tk, LANES)))

        kb = k_ref[...]
        vb = v_ref[...]
        ck = jnp.broadcast_to(_head_col(c_ref[...], h), (tk, LANES))
        dk_s[...] = jnp.zeros_like(dk_s)
        dv_s[...] = jnp.zeros_like(dv_s)
        dc_s[...] = jnp.zeros_like(dc_s)

        def scores(i):
            off = pl.multiple_of(i * tk, tk)
            sc = lax.dot_general(kb, q_ref[pl.ds(off, tk), :], nt, preferred_element_type=F32) - lse_ref[:, pl.ds(off, tk)]
            dp = lax.dot_general(vb, do_ref[pl.ds(off, tk), :], nt, preferred_element_type=F32) - dl_s[:, pl.ds(off, tk)]
            return sc, dp

        def accumulate(i, sc, dp):
            off = pl.multiple_of(i * tk, tk)
            p = _exp2_rows(sc, ck)
            dv_s[...] += jnp.dot(p.astype(BF16), do_ref[pl.ds(off, tk), :], preferred_element_type=F32)
            ds = p * dp
            dsb = ds.astype(BF16)
            dk_s[...] += jnp.dot(dsb, q_ref[pl.ds(off, tk), :], preferred_element_type=F32)
            dq_s[pl.ds(off, tk), :] += lax.dot_general(dsb, kb, tn, preferred_element_type=F32)
            dcq_s[:, pl.ds(off, tk)] += jnp.sum(ds, axis=0, keepdims=True)
            part = ds[:, :LANES]
            for b in range(1, tk // LANES):
                part = part + ds[:, b * LANES:(b + 1) * LANES]
            dc_s[...] += part

        sc0, dp0 = scores(j)
        s_buf[...] = _causal(sc0, True)
        dp_buf[...] = dp0

        def loop(i, carry):
            nxt = scores(i + 1)
            accumulate(i, s_buf[...], dp_buf[...])
            s_buf[...], dp_buf[...] = nxt
            return carry

        lax.fori_loop(j, nk - 1, loop, 0)
        accumulate(nk - 1, s_buf[...], dp_buf[...])
        dk_ref[...] = (dk_s[...] * (1.0 / LOG2E)).astype(BF16)
        dv_ref[...] = dv_s[...].astype(BF16)
        dck_ref[...] = jnp.sum(jnp.transpose(dc_s[...]), axis=0, keepdims=True)

        @pl.when(j == nk - 1)
        def _():
            dq_ref[...] = (dq_s[...] * scale).astype(BF16)
            dcq_ref[...] = dcq_s[...]

    col = pl.BlockSpec((s, HEAD_DIM), lambda h, j: (0, h))
    row = pl.BlockSpec((None, 1, s), lambda h, j: (h, 0, 0))
    kspec = pl.BlockSpec((tk, HEAD_DIM), lambda h, j: (j, h))
    return pl.pallas_call(
        body, name=name, grid=(nh, nk),
        in_specs=[col, kspec, pl.BlockSpec((tk, HEAD_DIM), lambda h, j: (j, nh + h)), col, col, row,
                  pl.BlockSpec((tk, LANES), lambda h, j: (j, 0)), pl.BlockSpec(memory_space=pl.ANY)],
        out_specs=[kspec, kspec, col, row, pl.BlockSpec((None, 1, tk), lambda h, j: (h, 0, j))],
        out_shape=[jax.ShapeDtypeStruct((s, w), BF16), jax.ShapeDtypeStruct((s, w), BF16),
                   jax.ShapeDtypeStruct(dqz.shape, BF16), jax.ShapeDtypeStruct((nh, 1, s), F32),
                   jax.ShapeDtypeStruct((nh, 1, s), F32)],
        input_output_aliases={7: 2},
        scratch_shapes=[pltpu.VMEM((tk, HEAD_DIM), F32), pltpu.VMEM((tk, HEAD_DIM), F32), pltpu.VMEM((tk, LANES), F32),
                        pltpu.VMEM((s, HEAD_DIM), F32), pltpu.VMEM((1, s), F32), pltpu.VMEM((1, s), F32),
                        pltpu.VMEM((tk, tk), F32), pltpu.VMEM((tk, tk), F32)],
        compiler_params=_cparams("arbitrary", "arbitrary"),
    )(q2, kv, kv, do, o, lse2_t, cum2, dqz)


_ALL_PEERS = tuple(range(1, N_DEV))
_CHIP_PEERS = (1, 2, 4, 6)


def _exchange_copies(ins, outs, send_sems, recv_sems, local_sems, scatter, peers=_ALL_PEERS):
    x, y, c = (lax.axis_index(a) for a in MESH_AXES)
    me = 4 * x + 2 * y + c
    local, remote = [], []
    for a in range(len(ins)):
        local.append(pltpu.make_async_copy(ins[a].at[me] if scatter else ins[a], outs[a].at[me], local_sems.at[a]))
        for k in peers:
            px, py, pc = (1 - x if k & 4 else x), (1 - y if k & 2 else y), (1 - c if k & 1 else c)
            remote.append(pltpu.make_async_remote_copy(
                src_ref=ins[a].at[4 * px + 2 * py + pc] if scatter else ins[a], dst_ref=outs[a].at[me],
                send_sem=send_sems.at[a * (N_DEV - 1) + k - 1], recv_sem=recv_sems.at[a * (N_DEV - 1) + k - 1],
                device_id=(px, py, pc), device_id_type=pl.DeviceIdType.MESH))
    return local, remote


def _exchange_out_shapes(arrs, scatter):
    return [((N_DEV,) + a.shape[1:]) if scatter else ((N_DEV,) + a.shape) for a in arrs]


_HBM =pl.BlockSpec(memory_space=pltpu.HBM)
_SEM = pl.BlockSpec(memory_space=pltpu.SEMAPHORE)


def _exchange_start(arrs, scatter, name, after=(), peers=_ALL_PEERS):
    n = len(arrs)
    after = list(after)
    lands = [lax.empty(s, a.dtype) for s, a in zip(_exchange_out_shapes(arrs, scatter), arrs)]

    def body(*refs):
        ins, outs = refs[:n], refs[n:2 * n]
        send_sems, recv_sems, local_sems = refs[2 * n + len(after):2 * n + len(after) + 3]
        token = refs[-1]
        local, remote = _exchange_copies(ins, outs, send_sems, recv_sems, local_sems, scatter, peers)
        for cp in local + remote:
            cp.start()
        token[...] = jnp.zeros_like(token)

    hbm = lambda a: pltpu.HBM(a.shape, a.dtype)
    res = pl.pallas_call(
        body, name=name,
        out_shape=(pltpu.SemaphoreType.DMA((n * (N_DEV - 1),)), pltpu.SemaphoreType.DMA((n * (N_DEV - 1),)),
                   pltpu.SemaphoreType.DMA((n,)), *[hbm(a) for a in arrs], *[hbm(a) for a in lands],
                   jax.ShapeDtypeStruct((SUBLANES, LANES), F32)),
        in_specs=[_HBM] * (2 * n) + [pl.BlockSpec(memory_space=pl.ANY)] * len(after),
        out_specs=(_SEM, _SEM, _SEM, *[_HBM] * (2 * n), pl.BlockSpec(memory_space=pltpu.VMEM)),
        input_output_aliases={i: 3 + i for i in range(2 * n)},
        compiler_params=pltpu.CompilerParams(has_side_effects=pltpu.SideEffectType.DATAFLOW_SIDE_EFFECTING),
    )(*[pltpu.with_memory_space_constraint(a, pltpu.HBM) for a in list(arrs) + lands], *after)
    return (n, scatter, res[:3], res[3:3 + n], res[3 + n:3 + 2 * n], peers), res[-1]


def _exchange_wait(state, after, name):
    n, scatter, sems, srcs, lands, peers = state
    after = list(after) if isinstance(after, (list, tuple)) else [after]

    def body(*refs):
        ins, outs = refs[:n], refs[n:2 * n]
        send_sems, recv_sems, local_sems = refs[2 * n:2 * n + 3]
        local, remote = _exchange_copies(ins, outs, send_sems, recv_sems, local_sems, scatter, peers)
        for cp in remote:
            cp.wait_send()
            cp.wait_recv()
        for cp in local:
            cp.wait()

    hbm = lambda a: pltpu.HBM(a.shape, a.dtype)
    res = pl.pallas_call(
        body, name=name,
        out_shape=(*[hbm(a) for a in srcs], *[hbm(a) for a in lands]),
        in_specs=[_HBM] * (2 * n) + [_SEM] * 3 + [pl.BlockSpec(memory_space=pl.ANY)] * len(after),
        out_specs=tuple([_HBM] * (2 * n)),
        input_output_aliases={i: i for i in range(2 * n)},
        compiler_params=pltpu.CompilerParams(has_side_effects=pltpu.SideEffectType.DATAFLOW_SIDE_EFFECTING),
    )(*srcs, *lands, *sems, *after)
    return list(res[n:])


def _forward_to_sibling(slots, name):
    n = len(slots)
    hops = (2, 4, 6)

    def body(*refs):
        ins, outs, (send_sems, recv_sems) = refs[:n], refs[n:2 * n], refs[2 * n:]
        x, y, c = (lax.axis_index(a) for a in MESH_AXES)
        copies = []
        for a in range(n):
            for i, k in enumerate(hops):
                slot = 4 * (1 - x if k & 4 else x) + 2 * (1 - y if k & 2 else y) + c
                copies.append(pltpu.make_async_remote_copy(
                    src_ref=ins[a].at[slot], dst_ref=outs[a].at[slot],
                    send_sem=send_sems.at[a * len(hops) + i], recv_sem=recv_sems.at[a * len(hops) + i],
                    device_id=(x, y, 1 - c), device_id_type=pl.DeviceIdType.MESH))
        for cp in copies:
            cp.start()
        for cp in copies:
            cp.wait_send()
            cp.wait_recv()

    return pl.pallas_call(
        body, name=name, out_shape=[jax.ShapeDtypeStruct(s.shape, s.dtype) for s in slots],
        in_specs=[pl.BlockSpec(memory_space=pl.ANY)] * n, out_specs=[pl.BlockSpec(memory_space=pl.ANY)] * n,
        input_output_aliases={i: i for i in range(n)},
        scratch_shapes=[pltpu.SemaphoreType.DMA((n * len(hops),)), pltpu.SemaphoreType.DMA((n * len(hops),))],
    )(*slots)


def _adamw_math(w, g, m, v):
    m = ADAM_B1 * m + (1.0 - ADAM_B1) * g
    v = ADAM_B2 * v + (1.0 - ADAM_B2) * (g * g)
    m_hat = m / (1.0 - ADAM_B1 ** ADAM_STEP)
    v_hat = v / (1.0 - ADAM_B2 ** ADAM_STEP)
    return -ADAM_LR * (m_hat / (jnp.sqrt(v_hat) + ADAM_EPS) + ADAM_WD * w), m, v


def _slot_sum(p_ref):
    g = p_ref[0].astype(F32)
    for d in range(1, p_ref.shape[0]):
        g = g + p_ref[d].astype(F32)
    return g


def _adamw_tile(r, c):
    return _tile(r, max(SUBLANES, (256 * 1024) // c // SUBLANES * SUBLANES), SUBLANES)


def _adamw(parts, w, m, v, name):
    r, c = w.shape[-2:]
    by_cols = r % SUBLANES != 0
    tr, tc = (r, _tile(c, 256)) if by_cols else (_adamw_tile(r, c), c)

    def body(p_ref, w_ref, m_ref, v_ref, g_ref, d_ref, nm_ref, nv_ref):
        g = _slot_sum(p_ref)
        g_ref[...] = g
        d_ref[...], nm_ref[...], nv_ref[...] = _adamw_math(w_ref[...], g, m_ref[...], v_ref[...])

    pos = (lambda i: (0, i)) if by_cols else (lambda i: (i, 0))
    if w.ndim == 3:
        blk = pl.BlockSpec((None, tr, tc), lambda i: (0,) + pos(i))
    else:
        blk = pl.BlockSpec((tr, tc), pos)
    sh = jax.ShapeDtypeStruct(w.shape, F32)
    return pl.pallas_call(
        body, name=name, grid=(c // tc if by_cols else r // tr,),
        in_specs=[pl.BlockSpec((parts.shape[0], tr, tc), lambda i: (0,) + pos(i)), blk, blk, blk],
        out_specs=[blk] * 4, out_shape=[sh] * 4, compiler_params=_cparams("parallel"),
    )(parts, w, m, v)


def _sum_parts(parts, name):
    _, r, c = parts.shape
    tr = _adamw_tile(r, c)

    def body(p_ref, o_ref):
        o_ref[...] = _slot_sum(p_ref)

    return pl.pallas_call(
        body, name=name, grid=(r // tr,),
        in_specs=[pl.BlockSpec((parts.shape[0], tr, c), lambda i: (0, i, 0))],
        out_specs=pl.BlockSpec((tr, c), lambda i: (i, 0)), out_shape=jax.ShapeDtypeStruct((r, c), F32),
        compiler_params=_cparams("parallel"),
    )(parts)


def _lane_pad(a, width=LANES):
    return jnp.pad(a, ((0, 0), (0, width - a.shape[1])))


def _local_step(x, target, norm_pre, norm_post, kv_norm, kv_b_f, a_re, a_im, log_dt, b_re, b_im, c_re, c_im, comm):
    s, d = x.shape
    g, p = a_re.shape
    w = g * S5_GROUP
    fw = d
    nh = fw // HEAD_DIM
    seg_len = s // N_SEG
    row = lambda v: v.reshape(1, -1)
    g_pre0, g_pre1, g_post0, g_post1, g_kv = row(norm_pre[0]), row(norm_pre[1]), row(norm_post[0]), row(norm_post[1]), row(kv_norm)

    ldt = log_dt.reshape(g, 1)
    abr, abi, cr, ci = _s5_disc_fwd(a_re, a_im, ldt)
    cr_col, ci_col = cr.reshape(g * p, 1), ci.reshape(g * p, 1)
    b_re2, b_im2 = b_re.reshape(g * p, S5_GROUP), b_im.reshape(g * p, S5_GROUP)
    bb_re, bb_im = _s5_bbar_fwd(cr_col, ci_col, b_re2, b_im2)
    bd_re = _block_diag(bb_re.reshape(g, p, S5_GROUP)).astype(BF16)
    bd_im = _block_diag(bb_im.reshape(g, p, S5_GROUP)).astype(BF16)
    cd_re = _block_diag(c_re).astype(BF16)
    cd_im = _block_diag(-c_im).astype(BF16)
    ab_re = jnp.broadcast_to(abr.reshape(1, g * p), (N_SEG, g * p))
    ab_im = jnp.broadcast_to(abi.reshape(1, g * p), (N_SEG, g * p))
    zero_seg = jnp.zeros((N_SEG, g * p), F32)

    xn0 = _norm_cast(x, g_pre0 + comm.token, "norm_pre0", x_kind="nat")
    w_in = comm.weight("s5_w_in", [xn0, bd_re, bd_im, cd_re, cd_im, ab_re, ab_im])
    d_row, bglu_row = row(comm.vector("s5_d")), row(comm.vector("s5_b_glu"))
    u = _mm(xn0, w_in, "nn", BF16, "s5_in_u", b_cols=(0, w), b_slots=True)
    z0 = _mm(xn0, w_in, "nn", BF16, "s5_in_z", b_cols=(w, w), b_slots=True)
    e_re, e_im = _s5_scan_fwd(u, bd_re, bd_im, cd_re, cd_im, ab_re, ab_im, zero_seg, zero_seg, d_row, False, "s5_scan_ends")
    i_re, i_im = _s5_seg_fix(e_re, e_im, ab_re, ab_im, seg_len, False, "s5_seg_fix")
    y_ssm, yg, h_re, h_im, _, _ = _s5_scan_fwd(u, bd_re, bd_im, cd_re, cd_im, ab_re, ab_im, i_re, i_im, d_row, True, "s5_scan")
    w_glu, w_out = comm.weight("s5_w_glu", yg), comm.weight("s5_w_out", yg)
    gp = _mm(yg, w_glu, "nn", BF16, "s5_glu")
    y3 = _s5_gate(y_ssm, gp, bglu_row, z0, "s5_gate")
    w_kvt, fw_in = comm.weight("kv_w", y3), comm.weight("fox_w_in", y3)
    w_ft = jnp.pad(w_kvt[2 * fw:], ((0, LANES - nh), (0, 0)))
    o0 = _mm(y3, w_out, "nn", F32, "s5_out")

    h1, hn_kv, xn1 = _resid_norm2(x, o0, g_post0 + comm.late_token, g_kv, g_pre1, "resid_norms")
    kv = _mm(hn_kv, w_kvt, "nt", BF16, "kv_proj", b_rows=2 * fw)
    f_logit = _mm(hn_kv, w_ft, "nt", F32, "f_proj")
    bf_row = _lane_pad(row(kv_b_f))
    cum2 = _cum_fwd(f_logit, bf_row, "cum_fwd")
    cum2_t = cum2[:, :nh].T.reshape(nh, 1, s)
    q2 = _mm(xn1, fw_in, "nn", BF16, "fox_q", scale=HEAD_DIM ** -0.5 * LOG2E, b_cols=(0, fw), b_slots=True)
    z1 = _mm(xn1, fw_in, "nn", BF16, "fox_z", b_cols=(fw, fw), b_slots=True)
    o, oz, lse2_t = _fox_fwd(q2, kv, cum2_t, z1, "fox_fwd")
    fw_out = comm.weight("fox_w_out", oz)
    o1 = _mm(oz, fw_out, "nn", F32, "fox_out")
    dh2, do1, sq, dg_post1 = _post_norm_loss(o1, g_post1, h1, target, "norm_post1_loss")
    loss = 0.5 * jnp.sum(sq) / d

    d_fw_out = _mm(oz, do1, "tn", BF16, "fox_out_dw")
    d_oz = _mm(do1, fw_out, "nt", BF16, "fox_out_dx")
    do, dqz = _gate_bwd(d_oz, o, z1, "fox_gate_bwd")
    dk, dv, dqz, dcq, dck = _fox_bwd(q2, kv, do, o, lse2_t, cum2, dqz, "fox_bwd")
    d_fw_in = _mm(xn1, dqz, "tn", BF16, "fox_in_dw", col_slots=True)
    dxn1 = _mm(dqz, fw_in, "nt", BF16, "fox_in_dx", b_slots=True)
    dcq_sl = _lane_pad(dcq.reshape(nh, s).T)
    dck_sl = _lane_pad(dck.reshape(nh, s).T)
    df, db_f = _cum_bwd(dcq_sl, dck_sl, f_logit, bf_row, "cum_bwd")
    dkv = _concat_cast(dk, dv, "fox_dkv")
    d_w_kvmt = _mm(dkv, hn_kv, "tn", BF16, "kv_dw")
    d_w_ft = _mm(df, hn_kv, "tn", BF16, "f_dw")
    dhn_f = _mm(df, w_ft, "nn", F32, "f_dx")
    dhn_kv = _mm(dkv, w_kvt, "nn", BF16, "kv_dx", add=dhn_f, b_rows=2 * fw)
    d_w_kvt = jnp.concatenate([d_w_kvmt, d_w_ft[:nh]], axis=0)
    tok = comm.send_grads(dict(fox_w_out=d_fw_out, fox_w_in=d_fw_in, kv_w=d_w_kvt), "exchange_fox")
    dh1, do0, dg_pre1, dg_kv, dg_post0 = _norm_bwd2(dh2, h1, dxn1, dhn_kv, g_pre1, g_kv, o0, g_post0 + tok[0, 0],
                                                      "resid_norms_bwd")

    d_w_out = _mm(y3, do0, "tn", BF16, "s5_out_dw")
    dy3 = _mm(do0, w_out, "nt", BF16, "s5_out_dx")
    duz, dgp, dyg_direct, db_glu = _s5_gate_bwd(dy3, y_ssm, gp, bglu_row, z0, "s5_gate_bwd")
    d_w_glu = _mm(yg, dgp, "tn", BF16, "s5_glu_dw")
    gelu_bwd = lambda dyg, y: jax.vjp(jax.nn.gelu, y.astype(F32))[1](dyg)[0]
    dy_ssm = _mm(dgp, w_glu, "nt", BF16, "s5_glu_dx", add=dyg_direct, epilogue=(gelu_bwd, y_ssm))
    d_row = d_row + comm.send_grads(dict(s5_w_out=d_w_out, s5_w_glu=d_w_glu), "exchange_s5")[0, 0]
    ab_imn = -ab_im
    ge_re, ge_im = _s5_scan_bwd(dy_ssm, u, h_re, h_im, bd_re, bd_im, cd_re, cd_im, ab_re, ab_imn, zero_seg, zero_seg,
                                d_row, False, "s5_adj_ends")
    gi_re, gi_im = _s5_seg_fix(ge_re, ge_im, ab_re, ab_imn, seg_len, True, "s5_adj_fix")
    duz, dbd_re, dbd_im, dcd_re, dcd_im, dab_re, dab_im, dd = _s5_scan_bwd(
        dy_ssm, u, h_re, h_im, bd_re, bd_im, cd_re, cd_im, ab_re, ab_imn, gi_re, gi_im, d_row, True, "s5_adj", duz=duz)
    d_w_in = _mm(xn0, duz, "tn", BF16, "s5_in_dw", col_slots=True)
    tok = comm.send_grads(dict(s5_w_in=d_w_in), "exchange_s5_in")
    dxn0 = _mm(duz, w_in, "nt", BF16, "s5_in_dx", after=tok, b_slots=True)
    grad_x, dg_pre0 = _norm_bwd1(dh1, x, dxn0, g_pre0, "norm_pre0_bwd")

    dbb_re = _block_diag_extract(dbd_re, p, S5_GROUP).reshape(g * p, S5_GROUP)
    dbb_im = _block_diag_extract(dbd_im, p, S5_GROUP).reshape(g * p, S5_GROUP)
    dcr_col, dci_col, db_re, db_im = _s5_bbar_bwd(cr_col, ci_col, b_re2, b_im2, dbb_re, dbb_im)
    da_re, da_im, dldt = _s5_disc_bwd(a_re, a_im, ldt, dab_re.reshape(g, p), dab_im.reshape(g, p),
                                      dcr_col.reshape(g, p), dci_col.reshape(g, p))
    dc_re = _block_diag_extract(dcd_re, S5_GROUP, p)
    dc_im = -_block_diag_extract(dcd_im, S5_GROUP, p)

    small = dict(
        norm_pre=jnp.concatenate([dg_pre0, dg_pre1], axis=0), norm_post=jnp.concatenate([dg_post0, dg_post1], axis=0),
        s5_a_re=da_re, s5_a_im=da_im, s5_log_dt=dldt.reshape(g), s5_b_re=db_re.reshape(g, p, S5_GROUP),
        s5_b_im=db_im.reshape(g, p, S5_GROUP), s5_c_re=dc_re, s5_c_im=dc_im, s5_d=dd.reshape(-1),
        s5_b_glu=db_glu.reshape(-1), kv_norm=dg_kv.reshape(-1), kv_b_f=db_f[0, :nh])
    return loss, grad_x, small


_BIG = ("s5_w_in", "s5_w_glu", "s5_w_out", "kv_w", "fox_w_in", "fox_w_out")
_COL_SHARDED = ("s5_w_in", "fox_w_in")
_SMALL = ("norm_pre", "norm_post", "s5_a_re", "s5_a_im", "s5_log_dt", "s5_b_re", "s5_b_im", "s5_c_re", "s5_c_im",
          "s5_d", "s5_b_glu", "kv_norm", "kv_b_f")
_SMALL_SHARDED = ("s5_d", "s5_b_glu")
_PACK_QUANTUM = SUBLANES * LANES
_WEIGHTS = ('norm_pre', 'norm_post', 's5_w_in', 's5_a_re', 's5_a_im', 's5_log_dt', 's5_b_re', 's5_b_im', 's5_c_re', 's5_c_im',
            's5_d', 's5_w_glu', 's5_b_glu', 's5_w_out', 'kv_norm', 'kv_w', 'kv_b_f', 'fox_w_in', 'fox_w_out')


def _full_from_slots(name, slots):
    n, r, c = slots.shape
    if name in _COL_SHARDED:
        return slots.transpose(1, 0, 2).reshape(r, n * c)
    return slots.reshape(n * r, c)


def _slots_from_full(name, full):
    if name in _COL_SHARDED:
        r, nc = full.shape
        return full.reshape(r, N_DEV, nc // N_DEV).transpose(1, 0, 2)
    nr, c = full.shape
    return full.reshape(N_DEV, nr // N_DEV, c)


def _groups_last(shape):
    return len(shape) >= 3 and shape[-1] < LANES and shape[-3] % LANES == 0


def _pack(vals):
    parts = []
    for v in vals:
        flat = jnp.moveaxis(v, -3, -1).reshape(-1) if _groups_last(v.shape) else v.reshape(-1)
        parts.append(jnp.pad(flat, (0, (-flat.shape[0]) % _PACK_QUANTUM)))
    total = sum(p.shape[0] for p in parts)
    parts.append(jnp.zeros(((-total) % (N_DEV * _PACK_QUANTUM),), F32))
    return jnp.concatenate(parts).reshape(-1, LANES)


def _unpack(packed, shapes):
    flat = packed.reshape(-1)
    out, off = [], 0
    for sh in shapes:
        n = math.prod(sh)
        piece = flat[off:off + n]
        if _groups_last(sh):
            piece = jnp.moveaxis(piece.reshape(sh[:-3] + sh[-2:] + sh[-3:-2]), -1, -3)
        out.append(piece.reshape(sh))
        off += n + (-n) % _PACK_QUANTUM
    return out


class _Comm:
    _GROUPS = (("s5_w_in",) + _SMALL_SHARDED, ("s5_w_glu", "s5_w_out"), ("kv_w", "fox_w_in"), ("fox_w_out",))
    _SLOT_FORM = ("s5_w_in", "fox_w_in")

    def __init__(self, shards, vectors, early=()):
        self._shards = {**shards, **vectors}
        self._full, self._gathers = {}, {}
        self._early = list(early)
        self.token = jnp.zeros((), F32)
        for group in self._GROUPS[:-1]:
            self.token = self.token + self._start(group, ())[0, 0]
        self.late_token = None
        self._sent = []

    def _start(self, group, after):
        state, tok = _exchange_start([self._shards[n] for n in group], False, "gather_start_" + group[0], after,
                                     peers=_CHIP_PEERS)
        self._gathers[group] = state
        return tok

    def vector(self, name):
        return self._full[name]

    def weight(self, name, after):
        if name not in self._full:
            group = next(g for g in self._GROUPS if name in g)
            if group == self._GROUPS[0]:
                after = (list(after) if isinstance(after, (list, tuple)) else [after]) + self._early
            slots = _exchange_wait(self._gathers.pop(group), after, "gather_wait_" + group[0])
            slots = _forward_to_sibling(slots, "gather_forward_" + group[0])
            for n, sl in zip(group, slots):
                if n in _SMALL_SHARDED:
                    self._full[n] = sl.reshape(-1)
                else:
                    self._full[n] = sl if n in self._SLOT_FORM else _full_from_slots(n, sl)
            if group == self._GROUPS[-2]:
                self.late_token = self._start(self._GROUPS[-1], [slots[0]])[0, 0]
        return self._full[name]

    def send_grads(self, grads, name):
        names = list(grads)
        slots = [grads[n] if grads[n].ndim == 3 else _slots_from_full(n, grads[n]).astype(BF16) for n in names]
        state, tok = _exchange_start(slots, True, name + "_start")
        self._sent.append((names, state, name + "_wait"))
        return tok

    def received_grads(self, group, after):
        names, state, name = self._sent[group]
        return list(zip(names, _exchange_wait(state, after, name)))


def kernel(x, norm_pre, norm_post, s5_w_in, s5_a_re, s5_a_im, s5_log_dt, s5_b_re, s5_b_im, s5_c_re, s5_c_im, s5_d, s5_w_glu, s5_b_glu, s5_w_out, kv_norm, kv_w, kv_b_f, fox_w_in, fox_w_out, loss_target, m_norm_pre, m_norm_post, m_s5_w_in, m_s5_a_re, m_s5_a_im, m_s5_log_dt, m_s5_b_re, m_s5_b_im, m_s5_c_re, m_s5_c_im, m_s5_d, m_s5_w_glu, m_s5_b_glu, m_s5_w_out, m_kv_norm, m_kv_w, m_kv_b_f, m_fox_w_in, m_fox_w_out, v_norm_pre, v_norm_post, v_s5_w_in, v_s5_a_re, v_s5_a_im, v_s5_log_dt, v_s5_b_re, v_s5_b_im, v_s5_c_re, v_s5_c_im, v_s5_d, v_s5_w_glu, v_s5_b_glu, v_s5_w_out, v_kv_norm, v_kv_w, v_kv_b_f, v_fox_w_in, v_fox_w_out):
    env = dict(locals())
    wts = {n: env[n] for n in _WEIGHTS}
    mom = {n: env["m_" + n] for n in _WEIGHTS}
    var = {n: env["v_" + n] for n in _WEIGHTS}
    me = 4 * lax.axis_index("x") + 2 * lax.axis_index("y") + lax.axis_index("c")
    shard2d = {n: (wts[n].T if n == "kv_w" else wts[n].reshape(wts[n].shape[-2:])) for n in _BIG}
    full_shape = {n: ((wts[n].size * N_DEV,) if n in _SMALL_SHARDED else wts[n].shape) for n in _SMALL}

    def spread(n, v):
        if n not in _SMALL_SHARDED:
            return v
        flat = v.reshape(-1)
        return lax.dynamic_update_slice(jnp.zeros(full_shape[n], F32), flat, (me * flat.shape[0],))

    packed = [_pack([spread(n, src[n]) for n in _SMALL] + [jnp.zeros((1,), F32)]) for src in (wts, mom, var)]
    comm = _Comm({n: _cast_bf16(shard2d[n], "cast_" + n) for n in _BIG}, {n: wts[n].reshape(1, -1) for n in _SMALL_SHARDED}, packed)

    loss_local, grad_x, small = _local_step(
        x[0], loss_target[0], norm_pre, norm_post, kv_norm, kv_b_f, s5_a_re[0], s5_a_im[0], s5_log_dt[0],
        s5_b_re[0], s5_b_im[0], s5_c_re[0], s5_c_im[0], comm)

    small_pack = _pack([small[n] for n in _SMALL] + [loss_local.reshape(1)])
    slice_rows = small_pack.shape[0] // N_DEV
    small_state, small_tok = _exchange_start([small_pack.reshape(N_DEV, slice_rows, LANES)], True, "reduce_small_start")

    res = {}

    def finish(group, after):
        for n, recv in comm.received_grads(group, after):
            if n == "kv_w":
                res[n] = [o.T for o in _adamw(recv, wts[n].T, mom[n].T, var[n].T, "adamw_" + n)]
            else:
                res[n] = _adamw(recv, wts[n], mom[n], var[n], "adamw_" + n)

    finish(0, [small_tok, grad_x])
    my_sum = _sum_parts(_exchange_wait(small_state, res["kv_w"][0], "reduce_small_wait")[0], "sum_small")
    gather_state, gather_tok = _exchange_start([my_sum], False, "gather_small_start")
    finish(1, gather_tok)
    finish(2, gather_tok)
    g_all = _exchange_wait(gather_state, res["s5_w_in"][0], "gather_small_wait")[0].reshape(1, small_pack.shape[0], LANES)
    outs = _adamw(g_all, *packed, "adamw_small")
    unpacked = [_unpack(o, [full_shape[n] for n in _SMALL] + [(1,)]) for o in outs]
    loss = unpacked[0][-1][0]
    for i, n in enumerate(_SMALL):
        vals = [u[i] for u in unpacked]
        if n in _SMALL_SHARDED:
            k = wts[n].size
            vals = [lax.dynamic_slice(v, (me * k,), (k,)) for v in vals]
        res[n] = [v.reshape(wts[n].shape) for v in vals]

    return (loss, grad_x[None], *[res[n][0] for n in _WEIGHTS], *[res[n][1] for n in _WEIGHTS],
            *[res[n][2] for n in _WEIGHTS], *[res[n][3] for n in _WEIGHTS])
```

```python
import math

import jax
import jax.numpy as jnp
from jax import lax
from jax.experimental import pallas as pl
from jax.experimental.pallas import tpu as pltpu

F32 = jnp.float32
BF16 = jnp.bfloat16

N_DEV = 8
MESH_AXES = ("x", "y", "c")
S5_GROUP = 16
S5_STATE = 64
LANES = 128
SUBLANES = 8
GROUPS_PER_BLOCK = LANES // S5_GROUP
BLOCK_STATE = GROUPS_PER_BLOCK * S5_STATE
N_SEG = SUBLANES
HEAD_DIM = 128
RMS_EPS = 1e-6
NEG_INF = -1e30
LOG2E = math.log2(math.e)
ADAM_LR = 0.001
ADAM_B1 = 0.9
ADAM_B2 = 0.999
ADAM_EPS = 1e-08
ADAM_WD = 0.01
ADAM_STEP = 10
VMEM_LIMIT = 56 * 1024 * 1024


def _tile(n, pref, quantum=LANES):
    if n <= pref:
        return n
    t = (pref // quantum) * quantum
    while t >= quantum:
        if n % t == 0:
            return t
        t -= quantum
    return n


def _cparams(*sem):
    return pltpu.CompilerParams(dimension_semantics=sem if sem else None, vmem_limit_bytes=VMEM_LIMIT)


_DOT_DIMS = {"nn": ((1,), (0,)), "nt": ((1,), (1,)), "tn": ((0,), (0,))}


def _mm(a, b, mode, out_dtype, name, add=None, scale=None, b_cols=None, after=None, col_slots=False, b_slots=False,
        b_rows=None, epilogue=None):
    slot_w = b.shape[2] if b_slots else None
    b2d = (b.shape[1], b.shape[0] * b.shape[2]) if b_slots else b.shape
    b_shape = b2d if b_cols is None else (b2d[0], b_cols[1])
    if b_rows is not None:
        b_shape = (b_rows, b_shape[1])
    if mode == "nn":
        (M, K), (K2, N) = a.shape, b_shape
    elif mode == "nt":
        (M, K), (N, K2) = a.shape, b_shape
    else:
        (K, M), (K2, N) = a.shape, b_shape
    assert K == K2, (name, a.shape, b_shape)
    tm, tn, tk = _tile(M, 1024 if K <= 2048 else 512), (N // N_DEV if col_slots else _tile(N, 1024)), _tile(K, 4096)
    if b_slots and mode == "nn":
        tn = slot_w
    nk = K // tk
    dims = (_DOT_DIMS[mode], ((), ()))
    col0 = 0
    if b_cols is not None:
        assert mode != "tn" and b_cols[0] % (tn if mode == "nn" else tk) == 0
        col0 = b_cols[0] // (tn if mode == "nn" else tk)
    assert not b_slots or (mode == "nn" or (mode == "nt" and nk == 1 and b_cols is None))

    def body(*refs):
        a_ref, b_ref = refs[:2]
        c_ref = refs[2] if add is not None else None
        e_ref = refs[2 + (add is not None)] if epilogue is not None else None
        o_ref = refs[2 + (add is not None) + (epilogue is not None) + (after is not None)]
        if b_slots and mode == "nt":
            part = lax.dot_general(a_ref[:, :slot_w], b_ref[0], dims, preferred_element_type=F32)
            for sl in range(1, b_ref.shape[0]):
                part += lax.dot_general(a_ref[:, sl * slot_w:(sl + 1) * slot_w], b_ref[sl], dims, preferred_element_type=F32)
        else:
            part = lax.dot_general(a_ref[...], b_ref[...], dims, preferred_element_type=F32)

        def finish(r):
            if scale is not None:
                r = r * scale
            if add is not None:
                r = r + c_ref[...]
            if epilogue is not None:
                r = epilogue[0](r, e_ref[...])
            o_ref[...] = r.astype(out_dtype)

        if nk == 1:
            finish(part)
            return
        acc = refs[-1]
        k = pl.program_id(2)

        @pl.when(k == 0)
        def _():
            acc[...] = part

        @pl.when(jnp.logical_and(k > 0, k < nk - 1))
        def _():
            acc[...] += part

        @pl.when(k == nk - 1)
        def _():
            finish(acc[...] + part)

    if mode == "tn":
        a_spec = pl.BlockSpec((tk, tm), lambda i, j, k: (k, i))
    else:
        a_spec = pl.BlockSpec((tm, tk), lambda i, j, k: (i, k))
    if b_slots and mode == "nn":
        b_spec = pl.BlockSpec((None, tk, tn), lambda i, j, k: (j + col0, k, 0))
    elif b_slots:
        b_spec = pl.BlockSpec((b.shape[0], tn, slot_w), lambda i, j, k: (0, j, 0))
    elif mode == "nt":
        b_spec = pl.BlockSpec((tn, tk), lambda i, j, k: (j, k + col0))
    else:
        b_spec = pl.BlockSpec((tk, tn), lambda i, j, k: (k, j + col0))
    o_spec = pl.BlockSpec((tm, tn), lambda i, j, k: (i, j))
    in_specs = [a_spec, b_spec] + ([o_spec] if add is not None else [])
    args = (a, b) + ((add,) if add is not None else ())
    if epilogue is not None:
        in_specs.append(o_spec)
        args += (epilogue[1],)
    if after is not None:
        in_specs.append(pl.BlockSpec(after.shape, lambda i, j, k: (0, 0)))
        args += (after,)
    out_shape = jax.ShapeDtypeStruct((M, N), out_dtype)
    if col_slots:
        assert add is None
        o_spec = pl.BlockSpec((None, tm, tn), lambda i, j, k: (j, i, 0))
        out_shape = jax.ShapeDtypeStruct((N_DEV, M, tn), out_dtype)
    return pl.pallas_call(
        body, name=name, grid=(M // tm, N // tn, nk),
        in_specs=in_specs, out_specs=o_spec,
        out_shape=out_shape,
        scratch_shapes=[pltpu.VMEM((tm, tn), F32)] if nk > 1 else [],
        compiler_params=_cparams("parallel", "parallel", "arbitrary"),
    )(*args)


class _NatIn:
    def __init__(self, ref):
        self.ref = ref

    def __getitem__(self, idx):
        v = jnp.swapaxes(self.ref[...], 0, 1)
        return v.reshape(v.shape[0] * N_SEG, v.shape[2])


class _NatOut:
    def __init__(self, ref):
        self.ref = ref

    def __setitem__(self, idx, val):
        self.ref[...] = jnp.swapaxes(val.reshape(val.shape[0] // N_SEG, N_SEG, val.shape[1]), 0, 1)


def _rowcall(body, name, n_rows, ins, outs, tile_rows=256):
    tr = _tile(n_rows, tile_rows, SUBLANES * 2)
    n_in = len(ins)
    in_kinds = [k for _, k in ins]
    kinds = [k for _, _, k in outs]

    def kern(*refs):
        @pl.when(pl.program_id(0) == 0)
        def _():
            for r, kind in zip(refs[n_in:], kinds):
                if kind == "acc":
                    r[...] = jnp.zeros_like(r)

        wrapped = [_NatIn(r) if k == "nat" else r for r, k in zip(refs[:n_in], in_kinds)]
        wrapped += [_NatOut(r) if k == "nat" else r for r, k in zip(refs[n_in:], kinds)]
        body(*wrapped)

    in_specs, args = [], []
    for arr, kind in ins:
        if kind == "row":
            in_specs.append(pl.BlockSpec((tr, arr.shape[1]), lambda i: (i, 0)))
        elif kind == "nat":
            in_specs.append(pl.BlockSpec((N_SEG, tr // N_SEG, arr.shape[1]), lambda i: (0, i, 0)))
            arr = arr.reshape(N_SEG, n_rows // N_SEG, arr.shape[1])
        else:
            in_specs.append(pl.BlockSpec(arr.shape, lambda i, nd=arr.ndim: (0,) * nd))
        args.append(arr)
    out_specs, out_shape = [], []
    for width, dtype, kind in outs:
        if kind == "row":
            out_specs.append(pl.BlockSpec((tr, width), lambda i: (i, 0)))
            out_shape.append(jax.ShapeDtypeStruct((n_rows, width), dtype))
        elif kind == "right":
            out_specs.append(pl.BlockSpec((tr, width), lambda i: (i, 1)))
            out_shape.append(jax.ShapeDtypeStruct((n_rows, 2 * width), dtype))
        elif kind == "nat":
            out_specs.append(pl.BlockSpec((N_SEG, tr // N_SEG, width), lambda i: (0, i, 0)))
            out_shape.append(jax.ShapeDtypeStruct((N_SEG, n_rows // N_SEG, width), dtype))
        else:
            out_specs.append(pl.BlockSpec((1, width), lambda i: (0, 0)))
            out_shape.append(jax.ShapeDtypeStruct((1, width), F32))
    res = pl.pallas_call(
        kern, name=name, grid=(n_rows // tr,), in_specs=in_specs, out_specs=out_specs, out_shape=out_shape,
        compiler_params=_cparams("arbitrary"),
    )(*args)
    return [r.reshape(n_rows, r.shape[2]) if k == "nat" else r for r, k in zip(res, kinds)]


def _rstd(x):
    return lax.rsqrt(jnp.mean(x * x, axis=-1, keepdims=True) + RMS_EPS)


def _rms_bwd(x, g, dy):
    xh = x * _rstd(x)
    dxh = dy * g
    dx = _rstd(x) * (dxh - xh * jnp.mean(dxh * xh, axis=-1, keepdims=True))
    return dx, jnp.sum(dy * xh, axis=0, keepdims=True)


def _silu(z):
    return z * jax.nn.sigmoid(z)


def _norm_cast(x, g, name, x_kind="row"):
    def body(x_ref, g_ref, o_ref):
        x = x_ref[...]
        o_ref[...] = (x * _rstd(x) * g_ref[...]).astype(BF16)

    return _rowcall(body, name, x.shape[0], [(x, x_kind), (g, "full")], [(x.shape[1], BF16, "row")])[0]


def _resid_norm2(x, o, g_post, g_kv, g_pre, name):
    def body(x_ref, o_ref, go_ref, gk_ref, gp_ref, h_ref, nk_ref, np_ref):
        o = o_ref[...]
        h = x_ref[...] + o * _rstd(o) * go_ref[...]
        h_ref[...] = h
        hn = h * _rstd(h)
        nk_ref[...] = (hn * gk_ref[...]).astype(BF16)
        np_ref[...] = (hn * gp_ref[...]).astype(BF16)

    d = x.shape[1]
    return _rowcall(body, name, x.shape[0], [(x, "nat"), (o, "row"), (g_post, "full"), (g_kv, "full"), (g_pre, "full")],
                    [(d, F32, "nat"), (d, BF16, "nat"), (d, BF16, "nat")])


def _post_norm_loss(o, g, h1, target, name):
    d = o.shape[1]

    def body(o_ref, g_ref, h_ref, t_ref, dh_ref, do_ref, acc_ref, dg_ref):
        o = o_ref[...]
        e = h_ref[...] + o * _rstd(o) * g_ref[...] - t_ref[...]
        dh = e * (1.0 / d)
        dh_ref[...] = dh
        acc_ref[...] += jnp.sum(e * e, axis=0, keepdims=True)
        dx, dg = _rms_bwd(o, g_ref[...], dh)
        do_ref[...] = dx.astype(BF16)
        dg_ref[...] += dg

    return _rowcall(body, name, o.shape[0], [(o, "row"), (g, "full"), (h1, "row"), (target, "row")],
                    [(d, F32, "row"), (d, BF16, "row"), (d, F32, "acc"), (d, F32, "acc")])


def _gate_bwd(d_oz, o, z, name):
    def body(d_ref, o_ref, z_ref, do_ref, dz_ref):
        _, vjp = jax.vjp(lambda o, z: o * _silu(z), o_ref[...].astype(F32), z_ref[...].astype(F32))
        do, dz = vjp(d_ref[...].astype(F32))
        do_ref[...] = do.astype(BF16)
        dz_ref[...] = dz.astype(BF16)

    w = o.shape[1]
    return _rowcall(body, name, o.shape[0], [(d_oz, "row"), (o, "row"), (z, "row")], [(w, BF16, "row"), (w, BF16, "right")])


def _norm_bwd2(dh2, h1, dxn1, dhn_kv, g_pre, g_kv, o0, g_post0, name):
    def body(dh2_ref, h_ref, d1_ref, dk_ref, gp_ref, gk_ref, o_ref, go_ref, dh1_ref, do_ref, dgp_ref, dgk_ref, dgo_ref):
        h = h_ref[...]
        dx1, dg1 = _rms_bwd(h, gp_ref[...], d1_ref[...].astype(F32))
        dxk, dgk = _rms_bwd(h, gk_ref[...], dk_ref[...].astype(F32))
        dh1 = dh2_ref[...] + dx1 + dxk
        dh1_ref[...] = dh1
        dgp_ref[...] += dg1
        dgk_ref[...] += dgk
        dxo, dgo = _rms_bwd(o_ref[...], go_ref[...], dh1)
        do_ref[...] = dxo.astype(BF16)
        dgo_ref[...] += dgo

    d = h1.shape[1]
    return _rowcall(body, name, h1.shape[0],
                    [(dh2, "nat"), (h1, "nat"), (dxn1, "nat"), (dhn_kv, "nat"), (g_pre, "full"), (g_kv, "full"),
                     (o0, "row"), (g_post0, "full")],
                    [(d, F32, "nat"), (d, BF16, "row"), (d, F32, "acc"), (d, F32, "acc"), (d, F32, "acc")])


def _norm_bwd1(dres, x, dxn, g, name):
    def body(dr_ref, x_ref, dn_ref, g_ref, dx_ref, dg_ref):
        dx, dg = _rms_bwd(x_ref[...], g_ref[...], dn_ref[...].astype(F32))
        dx_ref[...] = dr_ref[...] + dx
        dg_ref[...] += dg

    d = x.shape[1]
    return _rowcall(body, name, x.shape[0], [(dres, "nat"), (x, "nat"), (dxn, "row"), (g, "full")],
                    [(d, F32, "nat"), (d, F32, "acc")])


def _s5_gate(y_ssm, gp, b_glu, z, name):
    def body(y_ref, gp_ref, b_ref, z_ref, o_ref):
        yg = jax.nn.gelu(y_ref[...].astype(F32))
        o_ref[...] = (yg * jax.nn.sigmoid(gp_ref[...] + b_ref[...]) * _silu(z_ref[...].astype(F32))).astype(BF16)

    return _rowcall(body, name, y_ssm.shape[0], [(y_ssm, "row"), (gp, "row"), (b_glu, "full"), (z, "row")],
                    [(y_ssm.shape[1], BF16, "row")])[0]


def _s5_gate_bwd(dy3, y_ssm, gp, b_glu, z, name):
    def body(d_ref, y_ref, gp_ref, b_ref, z_ref, dz_ref, dgp_ref, dyg_ref, db_ref):
        yg = jax.nn.gelu(y_ref[...].astype(F32))
        _, vjp = jax.vjp(lambda yg, gp, z: yg * jax.nn.sigmoid(gp) * _silu(z), yg, gp_ref[...] + b_ref[...],
                         z_ref[...].astype(F32))
        dyg, dgp, dz = vjp(d_ref[...].astype(F32))
        dz_ref[...] = dz.astype(BF16)
        dgp_ref[...] = dgp.astype(BF16)
        dyg_ref[...] = dyg.astype(BF16)
        db_ref[...] += jnp.sum(dgp, axis=0, keepdims=True)

    w = y_ssm.shape[1]
    return _rowcall(body, name, y_ssm.shape[0],
                    [(dy3, "row"), (y_ssm, "row"), (gp, "row"), (b_glu, "full"), (z, "row")],
                    [(w, BF16, "right"), (w, BF16, "row"), (w, BF16, "row"), (w, F32, "acc")])


def _cast_bf16(x, name):
    r, c = x.shape
    by_cols = r % (2 * SUBLANES) != 0
    tr, tc = (r, _tile(c, 256)) if by_cols else (_tile(r, 512, 2 * SUBLANES), c)
    pos = (lambda i: (0, i)) if by_cols else (lambda i: (i, 0))

    def body(x_ref, o_ref):
        o_ref[...] = x_ref[...].astype(BF16)

    return pl.pallas_call(
        body, name=name, grid=(c // tc if by_cols else r // tr,),
        in_specs=[pl.BlockSpec((tr, tc), pos)], out_specs=pl.BlockSpec((tr, tc), pos),
        out_shape=jax.ShapeDtypeStruct((r, c), BF16), compiler_params=_cparams("parallel"),
    )(x)


def _concat_cast(a, b, name):
    def body(a_ref, b_ref, o_ref):
        w = a_ref.shape[1]
        o_ref[:, :w] = a_ref[...].astype(BF16)
        o_ref[:, w:] = b_ref[...].astype(BF16)

    return _rowcall(body, name, a.shape[0], [(a, "row"), (b, "row")], [(a.shape[1] + b.shape[1], BF16, "row")])[0]


def _disc(ar, ai, ldt):
    dt = jnp.exp(ldt)
    mag = jnp.exp(ar * dt)
    abr = mag * jnp.cos(ai * dt)
    abi = mag * jnp.sin(ai * dt)
    den = ar * ar + ai * ai
    nr = abr - 1.0
    return abr, abi, (nr * ar + abi * ai) / den, (abi * ar - nr * ai) / den


def _s5_disc_fwd(a_re, a_im, ldt):
    def body(ar, ai, ld, o1, o2, o3, o4):
        o1[...], o2[...], o3[...], o4[...] = _disc(ar[...], ai[...], ld[...])

    sh = jax.ShapeDtypeStruct(a_re.shape, F32)
    return pl.pallas_call(body, name="s5_disc_fwd", out_shape=(sh, sh, sh, sh))(a_re, a_im, ldt)


def _s5_disc_bwd(a_re, a_im, ldt, d_abr, d_abi, d_cr, d_ci):
    def body(ar, ai, ld, g1, g2, g3, g4, o1, o2, o3):
        _, vjp = jax.vjp(_disc, ar[...], ai[...], ld[...])
        o1[...], o2[...], o3[...] = vjp((g1[...], g2[...], g3[...], g4[...]))

    sh = jax.ShapeDtypeStruct(a_re.shape, F32)
    return pl.pallas_call(body, name="s5_disc_bwd", out_shape=(sh, sh, jax.ShapeDtypeStruct(ldt.shape, F32)))(
        a_re, a_im, ldt, d_abr, d_abi, d_cr, d_ci)


def _bbar(cr, ci, br, bi):
    return cr * br - ci * bi, cr * bi + ci * br


def _s5_bbar_fwd(cr_col, ci_col, b_re, b_im):
    def body(cr, ci, br, bi, o1, o2):
        o1[...], o2[...] = _bbar(cr[...], ci[...], br[...], bi[...])

    w = b_re.shape[1]
    return _rowcall(body, "s5_bbar_fwd", b_re.shape[0], [(cr_col, "row"), (ci_col, "row"), (b_re, "row"), (b_im, "row")],
                    [(w, F32, "row"), (w, F32, "row")], tile_rows=1024)


def _s5_bbar_bwd(cr_col, ci_col, b_re, b_im, d_re, d_im):
    def body(cr, ci, br, bi, g1, g2, o1, o2, o3, o4):
        _, vjp = jax.vjp(_bbar, cr[...], ci[...], br[...], bi[...])
        o1[...], o2[...], o3[...], o4[...] = vjp((g1[...], g2[...]))

    w = b_re.shape[1]
    return _rowcall(body, "s5_bbar_bwd", b_re.shape[0],
                    [(cr_col, "row"), (ci_col, "row"), (b_re, "row"), (b_im, "row"), (d_re, "row"), (d_im, "row")],
                    [(1, F32, "row"), (1, F32, "row"), (w, F32, "row"), (w, F32, "row")], tile_rows=1024)


def _block_diag(t):
    g, a, b = t.shape
    nb = g // GROUPS_PER_BLOCK
    t4 = t.reshape(nb, GROUPS_PER_BLOCK, a, b).transpose(0, 1, 3, 2)
    eye = jnp.eye(GROUPS_PER_BLOCK, dtype=t.dtype)
    return (t4[:, :, :, None, :] * eye[None, :, None, :, None]).reshape(nb, GROUPS_PER_BLOCK * b, GROUPS_PER_BLOCK * a)


def _block_diag_extract(d, a, b):
    nb = d.shape[0]
    d5 = d.reshape(nb, GROUPS_PER_BLOCK, b, GROUPS_PER_BLOCK, a)
    diag = jnp.stack([d5[:, g, :, g, :] for g in range(GROUPS_PER_BLOCK)], axis=1)
    return diag.transpose(0, 1, 3, 2).reshape(nb * GROUPS_PER_BLOCK, a, b)


def _scan_step(ar, ai, hr, hi, xr, xi):
    return ar * hr - ai * hi + xr, ar * hi + ai * hr + xi


def _s5_blocks_per_step(nb, full):
    want = 2 if full else 4
    while nb % want:
        want //= 2
    return want


def _s5_scan_fwd(u, bd_re, bd_im, cd_re, cd_im, ab_re, ab_im, init_re, init_im, d_row, full, name):
    s, w = u.shape
    nb = w // LANES
    rows = _tile(s, 512, SUBLANES)
    nc = s // rows
    steps = rows // N_SEG
    ns = nb * BLOCK_STATE

    nblk = _s5_blocks_per_step(nb, full)

    def body(u_ref, bdr, bdi, cdr, cdi, ar_ref, ai_ref, ir_ref, ii_ref, d_ref, *outs):
        if full:
            y_ref, yg_ref, hr_out, hi_out, er_ref, ei_ref, hr_ref, hi_ref, cr, ci = outs
        else:
            er_ref, ei_ref, hr_ref, hi_ref, cr, ci = outs
        c = pl.program_id(1)
        cols = lambda b, width: slice(b * width, (b + 1) * width)

        @pl.when(c == 0)
        def _():
            cr[...] = ir_ref[...]
            ci[...] = ii_ref[...]

        for b in range(nblk):
            ub = u_ref[:, cols(b, LANES)].astype(BF16)
            hr_ref[:, cols(b, BLOCK_STATE)] = jnp.dot(ub, bdr[b], preferred_element_type=F32)
            hi_ref[:, cols(b, BLOCK_STATE)] = jnp.dot(ub, bdi[b], preferred_element_type=F32)
        ar, ai = ar_ref[...], ai_ref[...]

        hr, hi = cr[...], ci[...]
        for j in range(steps):
            rows_j = pl.ds(j * N_SEG, N_SEG)
            hr, hi = _scan_step(ar, ai, hr, hi, hr_ref[rows_j, :], hi_ref[rows_j, :])
            hr_ref[rows_j, :] = hr
            hi_ref[rows_j, :] = hi
        cr[...] = hr
        ci[...] = hi
        if full:
            hr_out[...] = hr_ref[...].astype(BF16)
            hi_out[...] = hi_ref[...].astype(BF16)
            for b in range(nblk):
                st_b, ln_b = cols(b, BLOCK_STATE), cols(b, LANES)
                y = (jnp.dot(hr_out[:, st_b], cdr[b], preferred_element_type=F32)
                     + jnp.dot(hi_out[:, st_b], cdi[b], preferred_element_type=F32)
                     + d_ref[:, ln_b] * u_ref[:, ln_b])
                y_ref[:, ln_b] = y.astype(BF16)
                yg_ref[:, ln_b] = jax.nn.gelu(y).astype(BF16)

        @pl.when(c == nc - 1)
        def _():
            er_ref[...] = hr
            ei_ref[...] = hi

    lanes, states = LANES * nblk, BLOCK_STATE * nblk
    blk3 = lambda a: pl.BlockSpec((nblk,) + a.shape[1:], lambda k, c: (k, 0, 0))
    seg = pl.BlockSpec((N_SEG, states), lambda k, c: (0, k))
    st = pl.BlockSpec((rows, states), lambda k, c: (c, k))
    in_specs = [pl.BlockSpec((rows, lanes), lambda k, c: (c, k)), blk3(bd_re), blk3(bd_im), blk3(cd_re), blk3(cd_im),
                seg, seg, seg, seg, pl.BlockSpec((1, lanes), lambda k, c: (0, k))]
    seg_shape = jax.ShapeDtypeStruct((N_SEG, ns), F32)
    st_shape = jax.ShapeDtypeStruct((s, ns), BF16)
    scratch = [pltpu.VMEM((rows, states), F32)] * 2 + [pltpu.VMEM((N_SEG, states), F32)] * 2
    if full:
        ych = pl.BlockSpec((rows, lanes), lambda k, c: (c, k))
        out_specs = [ych, ych, st, st, seg, seg]
        out_shape = [jax.ShapeDtypeStruct((s, w), BF16), jax.ShapeDtypeStruct((s, w), BF16), st_shape, st_shape, seg_shape, seg_shape]
    else:
        out_specs = [seg, seg]
        out_shape = [seg_shape, seg_shape]
    return pl.pallas_call(
        body, name=name, grid=(nb // nblk, nc), in_specs=in_specs, out_specs=out_specs, out_shape=out_shape,
        scratch_shapes=scratch, compiler_params=_cparams("parallel", "arbitrary"),
    )(u, bd_re, bd_im, cd_re, cd_im, ab_re, ab_im, init_re, init_im, d_row)


def _s5_seg_fix(e_re, e_im, ab_re, ab_im, seg_len, reverse, name):
    assert seg_len & (seg_len - 1) == 0

    def body(er, ei, ar, ai, o_re, o_im):
        pr, pi = ar[0:1, :], ai[0:1, :]
        for _ in range(int(math.log2(seg_len))):
            pr, pi = pr * pr - pi * pi, 2.0 * pr * pi
        tr = jnp.zeros_like(pr)
        ti = jnp.zeros_like(pr)
        order = list(range(N_SEG - 1, -1, -1)) if reverse else list(range(N_SEG))
        for n, sgm in enumerate(order):
            o_re[sgm:sgm + 1, :] = tr
            o_im[sgm:sgm + 1, :] = ti
            if n < N_SEG - 1:
                tr, ti = _scan_step(pr, pi, tr, ti, er[sgm:sgm + 1, :], ei[sgm:sgm + 1, :])

    sh = jax.ShapeDtypeStruct(e_re.shape, F32)
    return pl.pallas_call(body, name=name, out_shape=(sh, sh))(e_re, e_im, ab_re, ab_im)


def _s5_scan_bwd(dy, u, h_re, h_im, bd_re, bd_im, cd_re, cd_im, ab_re, ab_imn, gin_re, gin_im, d_row, full, name, duz=None):
    s, w = u.shape
    nb = w // LANES
    rows = _tile(s, 512, SUBLANES)
    nc = s // rows
    steps = rows // N_SEG
    ns = nb * BLOCK_STATE

    nblk = _s5_blocks_per_step(nb, full)

    def body(dy_ref, u_ref, hr_ref, hi_ref, bdr, bdi, cdr, cdi, ar_ref, ai_ref, ir_ref, ii_ref, d_ref, *outs):
        if full:
            _, du_ref, dbr_ref, dbi_ref, dcr_ref, dci_ref, dar_ref, dai_ref, dd_ref, gr, gi, accr, acci = outs
        else:
            er_ref, ei_ref, gr, gi = outs
        c = pl.program_id(1)
        cols = lambda b, width: slice(b * width, (b + 1) * width)

        @pl.when(c == 0)
        def _():
            gr[pl.ds(rows, N_SEG), :] = ir_ref[...]
            gi[pl.ds(rows, N_SEG), :] = ii_ref[...]
            if full:
                for r in (dbr_ref, dbi_ref, dcr_ref, dci_ref, dd_ref, accr, acci):
                    r[...] = jnp.zeros_like(r)

        nt = (_DOT_DIMS["nt"], ((), ()))
        tn = (_DOT_DIMS["tn"], ((), ()))
        for b in range(nblk):
            dyb = dy_ref[:, cols(b, LANES)]
            gr[pl.ds(0, rows), cols(b, BLOCK_STATE)] = lax.dot_general(dyb, cdr[b], nt, preferred_element_type=F32)
            gi[pl.ds(0, rows), cols(b, BLOCK_STATE)] = lax.dot_general(dyb, cdi[b], nt, preferred_element_type=F32)
        ar, ai = ar_ref[...], ai_ref[...]

        g0r, g0i = gr[pl.ds(rows, N_SEG), :], gi[pl.ds(rows, N_SEG), :]
        for j in range(steps - 1, -1, -1):
            rows_j = pl.ds(j * N_SEG, N_SEG)
            g0r, g0i = _scan_step(ar, ai, g0r, g0i, gr[rows_j, :], gi[rows_j, :])
            gr[rows_j, :] = g0r
            gi[rows_j, :] = g0i
        if full:
            for b in range(nblk):
                st_b, ln_b = cols(b, BLOCK_STATE), cols(b, LANES)
                hr, hi = hr_ref[:, st_b], hi_ref[:, st_b]
                gnr, gni = gr[pl.ds(N_SEG, rows), st_b], gi[pl.ds(N_SEG, rows), st_b]
                accr[:, st_b] += jnp.sum((gnr * hr + gni * hi).reshape(steps, N_SEG, BLOCK_STATE), axis=0)
                acci[:, st_b] += jnp.sum((gni * hr - gnr * hi).reshape(steps, N_SEG, BLOCK_STATE), axis=0)
                dyb = dy_ref[:, ln_b]
                ub = u_ref[:, ln_b].astype(BF16)
                gbr, gbi = gr[pl.ds(0, rows), st_b].astype(BF16), gi[pl.ds(0, rows), st_b].astype(BF16)
                dcr_ref[b] += lax.dot_general(hr.astype(BF16), dyb, tn, preferred_element_type=F32)
                dci_ref[b] += lax.dot_general(hi.astype(BF16), dyb, tn, preferred_element_type=F32)
                dbr_ref[b] += lax.dot_general(ub, gbr, tn, preferred_element_type=F32)
                dbi_ref[b] += lax.dot_general(ub, gbi, tn, preferred_element_type=F32)
                du_ref[:, ln_b] = (lax.dot_general(gbr, bdr[b], nt, preferred_element_type=F32)
                                   + lax.dot_general(gbi, bdi[b], nt, preferred_element_type=F32)
                                   + d_ref[:, ln_b] * dy_ref[:, ln_b].astype(F32)).astype(BF16)
                dd_ref[:, ln_b] += jnp.sum(dy_ref[:, ln_b].astype(F32) * u_ref[:, ln_b], axis=0, keepdims=True)
        gr[pl.ds(rows, N_SEG), :] = g0r
        gi[pl.ds(rows, N_SEG), :] = g0i

        @pl.when(c == nc - 1)
        def _():
            if full:
                dar_ref[...] = jnp.sum(accr[...], axis=0, keepdims=True)
                dai_ref[...] = jnp.sum(acci[...], axis=0, keepdims=True)
            else:
                er_ref[...] = g0r
                ei_ref[...] = g0i

    lanes, states = LANES * nblk, BLOCK_STATE * nblk
    rev = lambda k, c: (nc - 1 - c, k)
    blk3 = lambda a: pl.BlockSpec((nblk,) + a.shape[1:], lambda k, c: (k, 0, 0))
    seg = pl.BlockSpec((N_SEG, states), lambda k, c: (0, k))
    st = pl.BlockSpec((rows, states), rev)
    ch = pl.BlockSpec((rows, lanes), rev)
    vec = pl.BlockSpec((1, lanes), lambda k, c: (0, k))
    if not full:
        st = pl.BlockSpec((rows, states), lambda k, c: (0, k))
    in_specs = [ch, ch if full else pl.BlockSpec((rows, lanes), lambda k, c: (0, k)), st, st,
                blk3(bd_re), blk3(bd_im), blk3(cd_re), blk3(cd_im), seg, seg, seg, seg, vec]
    args = [dy, u, h_re, h_im, bd_re, bd_im, cd_re, cd_im, ab_re, ab_imn, gin_re, gin_im, d_row]
    gbuf = [pltpu.VMEM((rows + N_SEG, states), F32)] * 2
    if full:
        row1 = pl.BlockSpec((1, states), lambda k, c: (0, k))
        out_specs = [ch, blk3(bd_re), blk3(bd_im), blk3(cd_re), blk3(cd_im), row1, row1, vec]
        out_shape = [jax.ShapeDtypeStruct(duz.shape, BF16),
                     jax.ShapeDtypeStruct(bd_re.shape, F32), jax.ShapeDtypeStruct(bd_im.shape, F32),
                     jax.ShapeDtypeStruct(cd_re.shape, F32), jax.ShapeDtypeStruct(cd_im.shape, F32),
                     jax.ShapeDtypeStruct((1, ns), F32), jax.ShapeDtypeStruct((1, ns), F32),
                     jax.ShapeDtypeStruct((1, w), F32)]
        scratch = gbuf + [pltpu.VMEM((N_SEG, states), F32)] * 2
        in_specs.append(pl.BlockSpec(memory_space=pl.ANY))
        args.append(duz)
        aliases = {len(args) - 1: 0}
    else:
        out_specs = [seg, seg]
        out_shape = [jax.ShapeDtypeStruct((N_SEG, ns), F32)] * 2
        scratch = gbuf
        aliases = {}
    return pl.pallas_call(
        body, name=name, grid=(nb // nblk, nc), in_specs=in_specs, out_specs=out_specs, out_shape=out_shape,
        input_output_aliases=aliases, scratch_shapes=scratch, compiler_params=_cparams("parallel", "arbitrary"),
    )(*args)


def _log_sigmoid(x):
    return jnp.minimum(x, 0.0) - jnp.log(1.0 + jnp.exp(-jnp.abs(x)))


def _tri(n, upper):
    r = lax.broadcasted_iota(jnp.int32, (n, n), 0)
    c = lax.broadcasted_iota(jnp.int32, (n, n), 1)
    return jnp.where((c >= r) if upper else (r >= c), 1.0, 0.0).astype(F32)


def _cum_fwd(f_logit, b_row, name):
    s, w = f_logit.shape
    t = _tile(s, 256, SUBLANES)

    def body(f_ref, b_ref, o_ref, carry):
        @pl.when(pl.program_id(0) == 0)
        def _():
            carry[...] = jnp.zeros_like(carry)

        lf = _log_sigmoid(f_ref[...] + b_ref[...])
        cum = jnp.dot(_tri(t, False), lf, precision=lax.Precision.HIGHEST, preferred_element_type=F32) + carry[...]
        o_ref[...] = cum * LOG2E
        carry[...] = cum[t - 1:t, :]

    return pl.pallas_call(
        body, name=name, grid=(s // t,),
        in_specs=[pl.BlockSpec((t, w), lambda i: (i, 0)), pl.BlockSpec((1, w), lambda i: (0, 0))],
        out_specs=pl.BlockSpec((t, w), lambda i: (i, 0)), out_shape=jax.ShapeDtypeStruct((s, w), F32),
        scratch_shapes=[pltpu.VMEM((1, w), F32)], compiler_params=_cparams("arbitrary"),
    )(f_logit, b_row)


def _cum_bwd(dcq, dck, f_logit, b_row, name):
    s, w = f_logit.shape
    t = _tile(s, 256, SUBLANES)
    nt = s // t

    def body(q_ref, k_ref, f_ref, b_ref, df_ref, db_ref, carry):
        @pl.when(pl.program_id(0) == 0)
        def _():
            carry[...] = jnp.zeros_like(carry)
            db_ref[...] = jnp.zeros_like(db_ref)

        dc = q_ref[...] - k_ref[...]
        rc = jnp.dot(_tri(t, True), dc, precision=lax.Precision.HIGHEST, preferred_element_type=F32) + carry[...]
        carry[...] = rc[0:1, :]
        df = rc * (1.0 - jax.nn.sigmoid(f_ref[...] + b_ref[...]))
        df_ref[...] = df.astype(BF16)
        db_ref[...] += jnp.sum(df, axis=0, keepdims=True)

    rev = pl.BlockSpec((t, w), lambda i: (nt - 1 - i, 0))
    one = pl.BlockSpec((1, w), lambda i: (0, 0))
    return pl.pallas_call(
        body, name=name, grid=(nt,), in_specs=[rev, rev, rev, one], out_specs=[rev, one],
        out_shape=[jax.ShapeDtypeStruct((s, w), BF16), jax.ShapeDtypeStruct((1, w), F32)],
        scratch_shapes=[pltpu.VMEM((1, w), F32)], compiler_params=_cparams("arbitrary"),
    )(dcq, dck, f_logit, b_row)


def _head_col(cum_tile, h):
    lane = lax.broadcasted_iota(jnp.int32, cum_tile.shape, 1)
    return jnp.sum(jnp.where(lane == h, cum_tile, 0.0), axis=1, keepdims=True)


def _attn_tiles(s):
    return _tile(s, 512, LANES)


def _exp2_rows(sc, sub):
    return jnp.concatenate([jnp.exp2(sc[:, b * LANES:(b + 1) * LANES] - sub) for b in range(sc.shape[1] // LANES)], axis=1)


def _row_of(rep):
    return jnp.transpose(rep)[0:1, :]


def _causal(sc, keys_on_rows):
    r = lax.broadcasted_iota(jnp.int32, sc.shape, 0)
    c = lax.broadcasted_iota(jnp.int32, sc.shape, 1)
    return jnp.where((r <= c) if keys_on_rows else (c <= r), sc, NEG_INF)


def _fox_fwd(q2, kv, cum2_t, z, name):
    s, w = q2.shape
    nh = w // HEAD_DIM
    tq = _attn_tiles(s)
    nq = s // tq
    nt = (_DOT_DIMS["nt"], ((), ()))

    def body(q_ref, k_ref, v_ref, ct_ref, z_ref, o_ref, oz_ref, lse_row_ref, m_s, acc_s, vaug, s_buf):
        i = pl.program_id(1)

        @pl.when(i == 0)
        def _():
            vaug[:, :HEAD_DIM] = v_ref[...]
            vaug[:, HEAD_DIM:] = jnp.ones((s, LANES), BF16)

        qb = q_ref[...]
        m_s[...] = jnp.full_like(m_s, NEG_INF)
        acc_s[...] = jnp.zeros_like(acc_s)

        def scores(j):
            off = pl.multiple_of(j * tq, tq)
            return lax.dot_general(qb, k_ref[pl.ds(off, tq), :], nt, preferred_element_type=F32) - ct_ref[:, pl.ds(off, tq)]

        def softmax_pv(j, sc):
            m_old = m_s[...]
            m_new = jnp.maximum(m_old, jnp.max(sc, axis=1, keepdims=True))
            p = _exp2_rows(sc, m_new)
            alpha = jnp.exp2(m_old - m_new)
            pv = jnp.dot(p.astype(BF16), vaug[pl.ds(pl.multiple_of(j * tq, tq), tq), :], preferred_element_type=F32)
            acc_s[...] = jnp.concatenate([alpha, alpha], axis=1) * acc_s[...] + pv
            m_s[...] = m_new

        s_buf[...] = scores(0)

        def loop(j, carry):
            nxt = scores(j + 1)
            softmax_pv(j, s_buf[...])
            s_buf[...] = nxt
            return carry

        lax.fori_loop(0, i, loop, 0)
        softmax_pv(i, _causal(s_buf[...], False))
        l = acc_s[:, HEAD_DIM:]
        o = acc_s[:, :HEAD_DIM] / l
        o_ref[...] = o.astype(BF16)
        oz_ref[...] = (o * _silu(z_ref[...].astype(F32))).astype(BF16)
        lse_row_ref[...] = _row_of(m_s[...] + jnp.log(l) * LOG2E)

    return pl.pallas_call(
        body, name=name, grid=(nh, nq),
        in_specs=[pl.BlockSpec((tq, HEAD_DIM), lambda h, i: (i, h)),
                  pl.BlockSpec((s, HEAD_DIM), lambda h, i: (0, h)),
                  pl.BlockSpec((s, HEAD_DIM), lambda h, i: (0, nh + h)),
                  pl.BlockSpec((None, 1, s), lambda h, i: (h, 0, 0)),
                  pl.BlockSpec((tq, HEAD_DIM), lambda h, i: (i, h))],
        out_specs=[pl.BlockSpec((tq, HEAD_DIM), lambda h, i: (i, h)),
                   pl.BlockSpec((tq, HEAD_DIM), lambda h, i: (i, h)),
                   pl.BlockSpec((None, 1, tq), lambda h, i: (h, 0, i))],
        out_shape=[jax.ShapeDtypeStruct((s, w), BF16), jax.ShapeDtypeStruct((s, w), BF16),
                   jax.ShapeDtypeStruct((nh, 1, s), F32)],
        scratch_shapes=[pltpu.VMEM((tq, LANES), F32), pltpu.VMEM((tq, HEAD_DIM + LANES), F32),
                        pltpu.VMEM((s, HEAD_DIM + LANES), BF16), pltpu.VMEM((tq, tq), F32)],
        compiler_params=_cparams("arbitrary", "arbitrary"),
    )(q2, kv, kv, cum2_t, z)


def _fox_bwd(q2, kv, do, o, lse2_t, cum2, dqz, name):
    s, w = q2.shape
    nh = w // HEAD_DIM
    tk = _attn_tiles(s)
    nk = s // tk
    scale = HEAD_DIM ** -0.5
    nt = (_DOT_DIMS["nt"], ((), ()))
    tn = (_DOT_DIMS["tn"], ((), ()))

    def body(q_ref, k_ref, v_ref, do_ref, o_ref, lse_ref, c_ref, _, dk_ref, dv_ref, dq_ref, dcq_ref, dck_ref,
             dk_s, dv_s, dc_s, dq_s, dcq_s, dl_s, s_buf, dp_buf):
        h, j = pl.program_id(0), pl.program_id(1)

        @pl.when(j == 0)
        def _():
            dq_s[...] = jnp.zeros_like(dq_s)
            dcq_s[...] = jnp.zeros_like(dcq_s)
            for i in range(nk):
                rows = pl.ds(i * tk, tk)
                d = jnp.sum(do_ref[rows, :].astype(F32) * o_ref[rows, :].astype(F32), axis=1, keepdims=True)
                dl_s[:, i * tk:(i + 1) * tk] = _row_of(jnp.broadcast_to(d, (tk, LANES)))

        kb = k_ref[...]
        vb = v_ref[...]
        ck = jnp.broadcast_to(_head_col(c_ref[...], h), (tk, LANES))
        dk_s[...] = jnp.zeros_like(dk_s)
        dv_s[...] = jnp.zeros_like(dv_s)
        dc_s[...] = jnp.zeros_like(dc_s)

        def scores(i):
            off = pl.multiple_of(i * tk, tk)
            sc = lax.dot_general(kb, q_ref[pl.ds(off, tk), :], nt, preferred_element_type=F32) - lse_ref[:, pl.ds(off, tk)]
            dp = lax.dot_general(vb, do_ref[pl.ds(off, tk), :], nt, preferred_element_type=F32) - dl_s[:, pl.ds(off, tk)]
            return sc, dp

        def accumulate(i, sc, dp):
            off = pl.multiple_of(i * tk, tk)
            p = _exp2_rows(sc, ck)
            dv_s[...] += jnp.dot(p.astype(BF16), do_ref[pl.ds(off, tk), :], preferred_element_type=F32)
            ds = p * dp
            dsb = ds.astype(BF16)
            dk_s[...] += jnp.dot(dsb, q_ref[pl.ds(off, tk), :], preferred_element_type=F32)
            dq_s[pl.ds(off, tk), :] += lax.dot_general(dsb, kb, tn, preferred_element_type=F32)
            dcq_s[:, pl.ds(off, tk)] += jnp.sum(ds, axis=0, keepdims=True)
            part = ds[:, :LANES]
            for b in range(1, tk // LANES):
                part = part + ds[:, b * LANES:(b + 1) * LANES]
            dc_s[...] += part

        sc0, dp0 = scores(j)
        s_buf[...] = _causal(sc0, True)
        dp_buf[...] = dp0

        def loop(i, carry):
            nxt = scores(i + 1)
            accumulate(i, s_buf[...], dp_buf[...])
            s_buf[...], dp_buf[...] = nxt
            return carry

        lax.fori_loop(j, nk - 1, loop, 0)
        accumulate(nk - 1, s_buf[...], dp_buf[...])
        dk_ref[...] = (dk_s[...] * (1.0 / LOG2E)).astype(BF16)
        dv_ref[...] = dv_s[...].astype(BF16)
        dck_ref[...] = jnp.sum(jnp.transpose(dc_s[...]), axis=0, keepdims=True)

        @pl.when(j == nk - 1)
        def _():
            dq_ref[...] = (dq_s[...] * scale).astype(BF16)
            dcq_ref[...] = dcq_s[...]

    col = pl.BlockSpec((s, HEAD_DIM), lambda h, j: (0, h))
    row = pl.BlockSpec((None, 1, s), lambda h, j: (h, 0, 0))
    kspec = pl.BlockSpec((tk, HEAD_DIM), lambda h, j: (j, h))
    return pl.pallas_call(
        body, name=name, grid=(nh, nk),
        in_specs=[col, kspec, pl.BlockSpec((tk, HEAD_DIM), lambda h, j: (j, nh + h)), col, col, row,
                  pl.BlockSpec((tk, LANES), lambda h, j: (j, 0)), pl.BlockSpec(memory_space=pl.ANY)],
        out_specs=[kspec, kspec, col, row, pl.BlockSpec((None, 1, tk), lambda h, j: (h, 0, j))],
        out_shape=[jax.ShapeDtypeStruct((s, w), BF16), jax.ShapeDtypeStruct((s, w), BF16),
                   jax.ShapeDtypeStruct(dqz.shape, BF16), jax.ShapeDtypeStruct((nh, 1, s), F32),
                   jax.ShapeDtypeStruct((nh, 1, s), F32)],
        input_output_aliases={7: 2},
        scratch_shapes=[pltpu.VMEM((tk, HEAD_DIM), F32), pltpu.VMEM((tk, HEAD_DIM), F32), pltpu.VMEM((tk, LANES), F32),
                        pltpu.VMEM((s, HEAD_DIM), F32), pltpu.VMEM((1, s), F32), pltpu.VMEM((1, s), F32),
                        pltpu.VMEM((tk, tk), F32), pltpu.VMEM((tk, tk), F32)],
        compiler_params=_cparams("arbitrary", "arbitrary"),
    )(q2, kv, kv, do, o, lse2_t, cum2, dqz)


_ALL_PEERS = tuple(range(1, N_DEV))
_CHIP_PEERS = (1, 2, 4, 6)


def _exchange_copies(ins, outs, send_sems, recv_sems, local_sems, scatter, peers=_ALL_PEERS):
    x, y, c = (lax.axis_index(a) for a in MESH_AXES)
    me = 4 * x + 2 * y + c
    local, remote = [], []
    for a in range(len(ins)):
        local.append(pltpu.make_async_copy(ins[a].at[me] if scatter else ins[a], outs[a].at[me], local_sems.at[a]))
        for k in peers:
            px, py, pc = (1 - x if k & 4 else x), (1 - y if k & 2 else y), (1 - c if k & 1 else c)
            remote.append(pltpu.make_async_remote_copy(
                src_ref=ins[a].at[4 * px + 2 * py + pc] if scatter else ins[a], dst_ref=outs[a].at[me],
                send_sem=send_sems.at[a * (N_DEV - 1) + k - 1], recv_sem=recv_sems.at[a * (N_DEV - 1) + k - 1],
                device_id=(px, py, pc), device_id_type=pl.DeviceIdType.MESH))
    return local, remote


def _exchange_out_shapes(arrs, scatter):
    return [((N_DEV,) + a.shape[1:]) if scatter else ((N_DEV,) + a.shape) for a in arrs]


_HBM =pl.BlockSpec(memory_space=pltpu.HBM)
_SEM = pl.BlockSpec(memory_space=pltpu.SEMAPHORE)


def _exchange_start(arrs, scatter, name, after=(), peers=_ALL_PEERS):
    n = len(arrs)
    after = list(after)
    lands = [lax.empty(s, a.dtype) for s, a in zip(_exchange_out_shapes(arrs, scatter), arrs)]

    def body(*refs):
        ins, outs = refs[:n], refs[n:2 * n]
        send_sems, recv_sems, local_sems = refs[2 * n + len(after):2 * n + len(after) + 3]
        token = refs[-1]
        local, remote = _exchange_copies(ins, outs, send_sems, recv_sems, local_sems, scatter, peers)
        for cp in local + remote:
            cp.start()
        token[...] = jnp.zeros_like(token)

    hbm = lambda a: pltpu.HBM(a.shape, a.dtype)
    res = pl.pallas_call(
        body, name=name,
        out_shape=(pltpu.SemaphoreType.DMA((n * (N_DEV - 1),)), pltpu.SemaphoreType.DMA((n * (N_DEV - 1),)),
                   pltpu.SemaphoreType.DMA((n,)), *[hbm(a) for a in arrs], *[hbm(a) for a in lands],
                   jax.ShapeDtypeStruct((SUBLANES, LANES), F32)),
        in_specs=[_HBM] * (2 * n) + [pl.BlockSpec(memory_space=pl.ANY)] * len(after),
        out_specs=(_SEM, _SEM, _SEM, *[_HBM] * (2 * n), pl.BlockSpec(memory_space=pltpu.VMEM)),
        input_output_aliases={i: 3 + i for i in range(2 * n)},
        compiler_params=pltpu.CompilerParams(has_side_effects=pltpu.SideEffectType.DATAFLOW_SIDE_EFFECTING),
    )(*[pltpu.with_memory_space_constraint(a, pltpu.HBM) for a in list(arrs) + lands], *after)
    return (n, scatter, res[:3], res[3:3 + n], res[3 + n:3 + 2 * n], peers), res[-1]


def _exchange_wait(state, after, name):
    n, scatter, sems, srcs, lands, peers = state
    after = list(after) if isinstance(after, (list, tuple)) else [after]

    def body(*refs):
        ins, outs = refs[:n], refs[n:2 * n]
        send_sems, recv_sems, local_sems = refs[2 * n:2 * n + 3]
        local, remote = _exchange_copies(ins, outs, send_sems, recv_sems, local_sems, scatter, peers)
        for cp in remote:
            cp.wait_send()
            cp.wait_recv()
        for cp in local:
            cp.wait()

    hbm = lambda a: pltpu.HBM(a.shape, a.dtype)
    res = pl.pallas_call(
        body, name=name,
        out_shape=(*[hbm(a) for a in srcs], *[hbm(a) for a in lands]),
        in_specs=[_HBM] * (2 * n) + [_SEM] * 3 + [pl.BlockSpec(memory_space=pl.ANY)] * len(after),
        out_specs=tuple([_HBM] * (2 * n)),
        input_output_aliases={i: i for i in range(2 * n)},
        compiler_params=pltpu.CompilerParams(has_side_effects=pltpu.SideEffectType.DATAFLOW_SIDE_EFFECTING),
    )(*srcs, *lands, *sems, *after)
    return list(res[n:])


def _forward_to_sibling(slots, name):
    n = len(slots)
    hops = (2, 4, 6)

    def body(*refs):
        ins, outs, (send_sems, recv_sems) = refs[:n], refs[n:2 * n], refs[2 * n:]
        x, y, c = (lax.axis_index(a) for a in MESH_AXES)
        copies = []
        for a in range(n):
            for i, k in enumerate(hops):
                slot = 4 * (1 - x if k & 4 else x) + 2 * (1 - y if k & 2 else y) + c
                copies.append(pltpu.make_async_remote_copy(
                    src_ref=ins[a].at[slot], dst_ref=outs[a].at[slot],
                    send_sem=send_sems.at[a * len(hops) + i], recv_sem=recv_sems.at[a * len(hops) + i],
                    device_id=(x, y, 1 - c), device_id_type=pl.DeviceIdType.MESH))
        for cp in copies:
            cp.start()
        for cp in copies:
            cp.wait_send()
            cp.wait_recv()

    return pl.pallas_call(
        body, name=name, out_shape=[jax.ShapeDtypeStruct(s.shape, s.dtype) for s in slots],
        in_specs=[pl.BlockSpec(memory_space=pl.ANY)] * n, out_specs=[pl.BlockSpec(memory_space=pl.ANY)] * n,
        input_output_aliases={i: i for i in range(n)},
        scratch_shapes=[pltpu.SemaphoreType.DMA((n * len(hops),)), pltpu.SemaphoreType.DMA((n * len(hops),))],
    )(*slots)


def _adamw_math(w, g, m, v):
    m = ADAM_B1 * m + (1.0 - ADAM_B1) * g
    v = ADAM_B2 * v + (1.0 - ADAM_B2) * (g * g)
    m_hat = m / (1.0 - ADAM_B1 ** ADAM_STEP)
    v_hat = v / (1.0 - ADAM_B2 ** ADAM_STEP)
    return -ADAM_LR * (m_hat / (jnp.sqrt(v_hat) + ADAM_EPS) + ADAM_WD * w), m, v


def _slot_sum(p_ref):
    g = p_ref[0].astype(F32)
    for d in range(1, p_ref.shape[0]):
        g = g + p_ref[d].astype(F32)
    return g


def _adamw_tile(r, c):
    return _tile(r, max(SUBLANES, (256 * 1024) // c // SUBLANES * SUBLANES), SUBLANES)


def _adamw(parts, w, m, v, name):
    r, c = w.shape[-2:]
    by_cols = r % SUBLANES != 0
    tr, tc = (r, _tile(c, 256)) if by_cols else (_adamw_tile(r, c), c)

    def body(p_ref, w_ref, m_ref, v_ref, g_ref, d_ref, nm_ref, nv_ref):
        g = _slot_sum(p_ref)
        g_ref[...] = g
        d_ref[...], nm_ref[...], nv_ref[...] = _adamw_math(w_ref[...], g, m_ref[...], v_ref[...])

    pos = (lambda i: (0, i)) if by_cols else (lambda i: (i, 0))
    if w.ndim == 3:
        blk = pl.BlockSpec((None, tr, tc), lambda i: (0,) + pos(i))
    else:
        blk = pl.BlockSpec((tr, tc), pos)
    sh = jax.ShapeDtypeStruct(w.shape, F32)
    return pl.pallas_call(
        body, name=name, grid=(c // tc if by_cols else r // tr,),
        in_specs=[pl.BlockSpec((parts.shape[0], tr, tc), lambda i: (0,) + pos(i)), blk, blk, blk],
        out_specs=[blk] * 4, out_shape=[sh] * 4, compiler_params=_cparams("parallel"),
    )(parts, w, m, v)


def _sum_parts(parts, name):
    _, r, c = parts.shape
    tr = _adamw_tile(r, c)

    def body(p_ref, o_ref):
        o_ref[...] = _slot_sum(p_ref)

    return pl.pallas_call(
        body, name=name, grid=(r // tr,),
        in_specs=[pl.BlockSpec((parts.shape[0], tr, c), lambda i: (0, i, 0))],
        out_specs=pl.BlockSpec((tr, c), lambda i: (i, 0)), out_shape=jax.ShapeDtypeStruct((r, c), F32),
        compiler_params=_cparams("parallel"),
    )(parts)


def _lane_pad(a, width=LANES):
    return jnp.pad(a, ((0, 0), (0, width - a.shape[1])))


def _local_step(x, target, norm_pre, norm_post, kv_norm, kv_b_f, a_re, a_im, log_dt, b_re, b_im, c_re, c_im, comm):
    s, d = x.shape
    g, p = a_re.shape
    w = g * S5_GROUP
    fw = d
    nh = fw // HEAD_DIM
    seg_len = s // N_SEG
    row = lambda v: v.reshape(1, -1)
    g_pre0, g_pre1, g_post0, g_post1, g_kv = row(norm_pre[0]), row(norm_pre[1]), row(norm_post[0]), row(norm_post[1]), row(kv_norm)

    ldt = log_dt.reshape(g, 1)
    abr, abi, cr, ci = _s5_disc_fwd(a_re, a_im, ldt)
    cr_col, ci_col = cr.reshape(g * p, 1), ci.reshape(g * p, 1)
    b_re2, b_im2 = b_re.reshape(g * p, S5_GROUP), b_im.reshape(g * p, S5_GROUP)
    bb_re, bb_im = _s5_bbar_fwd(cr_col, ci_col, b_re2, b_im2)
    bd_re = _block_diag(bb_re.reshape(g, p, S5_GROUP)).astype(BF16)
    bd_im = _block_diag(bb_im.reshape(g, p, S5_GROUP)).astype(BF16)
    cd_re = _block_diag(c_re).astype(BF16)
    cd_im = _block_diag(-c_im).astype(BF16)
    ab_re = jnp.broadcast_to(abr.reshape(1, g * p), (N_SEG, g * p))
    ab_im = jnp.broadcast_to(abi.reshape(1, g * p), (N_SEG, g * p))
    zero_seg = jnp.zeros((N_SEG, g * p), F32)

    xn0 = _norm_cast(x, g_pre0 + comm.token, "norm_pre0", x_kind="nat")
    w_in = comm.weight("s5_w_in", [xn0, bd_re, bd_im, cd_re, cd_im, ab_re, ab_im])
    d_row, bglu_row = row(comm.vector("s5_d")), row(comm.vector("s5_b_glu"))
    u = _mm(xn0, w_in, "nn", BF16, "s5_in_u", b_cols=(0, w), b_slots=True)
    z0 = _mm(xn0, w_in, "nn", BF16, "s5_in_z", b_cols=(w, w), b_slots=True)
    e_re, e_im = _s5_scan_fwd(u, bd_re, bd_im, cd_re, cd_im, ab_re, ab_im, zero_seg, zero_seg, d_row, False, "s5_scan_ends")
    i_re, i_im = _s5_seg_fix(e_re, e_im, ab_re, ab_im, seg_len, False, "s5_seg_fix")
    y_ssm, yg, h_re, h_im, _, _ = _s5_scan_fwd(u, bd_re, bd_im, cd_re, cd_im, ab_re, ab_im, i_re, i_im, d_row, True, "s5_scan")
    w_glu, w_out = comm.weight("s5_w_glu", yg), comm.weight("s5_w_out", yg)
    gp = _mm(yg, w_glu, "nn", BF16, "s5_glu")
    y3 = _s5_gate(y_ssm, gp, bglu_row, z0, "s5_gate")
    w_kvt, fw_in = comm.weight("kv_w", y3), comm.weight("fox_w_in", y3)
    w_ft = jnp.pad(w_kvt[2 * fw:], ((0, LANES - nh), (0, 0)))
    o0 = _mm(y3, w_out, "nn", F32, "s5_out")

    h1, hn_kv, xn1 = _resid_norm2(x, o0, g_post0 + comm.late_token, g_kv, g_pre1, "resid_norms")
    kv = _mm(hn_kv, w_kvt, "nt", BF16, "kv_proj", b_rows=2 * fw)
    f_logit = _mm(hn_kv, w_ft, "nt", F32, "f_proj")
    bf_row = _lane_pad(row(kv_b_f))
    cum2 = _cum_fwd(f_logit, bf_row, "cum_fwd")
    cum2_t = cum2[:, :nh].T.reshape(nh, 1, s)
    q2 = _mm(xn1, fw_in, "nn", BF16, "fox_q", scale=HEAD_DIM ** -0.5 * LOG2E, b_cols=(0, fw), b_slots=True)
    z1 = _mm(xn1, fw_in, "nn", BF16, "fox_z", b_cols=(fw, fw), b_slots=True)
    o, oz, lse2_t = _fox_fwd(q2, kv, cum2_t, z1, "fox_fwd")
    fw_out = comm.weight("fox_w_out", oz)
    o1 = _mm(oz, fw_out, "nn", F32, "fox_out")
    dh2, do1, sq, dg_post1 = _post_norm_loss(o1, g_post1, h1, target, "norm_post1_loss")
    loss = 0.5 * jnp.sum(sq) / d

    d_fw_out = _mm(oz, do1, "tn", BF16, "fox_out_dw")
    d_oz = _mm(do1, fw_out, "nt", BF16, "fox_out_dx")
    do, dqz = _gate_bwd(d_oz, o, z1, "fox_gate_bwd")
    dk, dv, dqz, dcq, dck = _fox_bwd(q2, kv, do, o, lse2_t, cum2, dqz, "fox_bwd")
    d_fw_in = _mm(xn1, dqz, "tn", BF16, "fox_in_dw", col_slots=True)
    dxn1 = _mm(dqz, fw_in, "nt", BF16, "fox_in_dx", b_slots=True)
    dcq_sl = _lane_pad(dcq.reshape(nh, s).T)
    dck_sl = _lane_pad(dck.reshape(nh, s).T)
    df, db_f = _cum_bwd(dcq_sl, dck_sl, f_logit, bf_row, "cum_bwd")
    dkv = _concat_cast(dk, dv, "fox_dkv")
    d_w_kvmt = _mm(dkv, hn_kv, "tn", BF16, "kv_dw")
    d_w_ft = _mm(df, hn_kv, "tn", BF16, "f_dw")
    dhn_f = _mm(df, w_ft, "nn", BF16, "f_dx")
    dhn_kv = _mm(dkv, w_kvt, "nn", BF16, "kv_dx", add=dhn_f, b_rows=2 * fw)
    d_w_kvt = jnp.concatenate([d_w_kvmt, d_w_ft[:nh]], axis=0)
    tok = comm.send_grads(dict(fox_w_out=d_fw_out, fox_w_in=d_fw_in, kv_w=d_w_kvt), "exchange_fox")
    dh1, do0, dg_pre1, dg_kv, dg_post0 = _norm_bwd2(dh2, h1, dxn1, dhn_kv, g_pre1, g_kv, o0, g_post0 + tok[0, 0],
                                                      "resid_norms_bwd")

    d_w_out = _mm(y3, do0, "tn", BF16, "s5_out_dw")
    dy3 = _mm(do0, w_out, "nt", BF16, "s5_out_dx")
    duz, dgp, dyg_direct, db_glu = _s5_gate_bwd(dy3, y_ssm, gp, bglu_row, z0, "s5_gate_bwd")
    d_w_glu = _mm(yg, dgp, "tn", BF16, "s5_glu_dw")
    gelu_bwd = lambda dyg, y: jax.vjp(jax.nn.gelu, y.astype(F32))[1](dyg)[0]
    dy_ssm = _mm(dgp, w_glu, "nt", BF16, "s5_glu_dx", add=dyg_direct, epilogue=(gelu_bwd, y_ssm))
    d_row = d_row + comm.send_grads(dict(s5_w_out=d_w_out, s5_w_glu=d_w_glu), "exchange_s5")[0, 0]
    ab_imn = -ab_im
    ge_re, ge_im = _s5_scan_bwd(dy_ssm, u, h_re, h_im, bd_re, bd_im, cd_re, cd_im, ab_re, ab_imn, zero_seg, zero_seg,
                                d_row, False, "s5_adj_ends")
    gi_re, gi_im = _s5_seg_fix(ge_re, ge_im, ab_re, ab_imn, seg_len, True, "s5_adj_fix")
    duz, dbd_re, dbd_im, dcd_re, dcd_im, dab_re, dab_im, dd = _s5_scan_bwd(
        dy_ssm, u, h_re, h_im, bd_re, bd_im, cd_re, cd_im, ab_re, ab_imn, gi_re, gi_im, d_row, True, "s5_adj", duz=duz)
    d_w_in = _mm(xn0, duz, "tn", BF16, "s5_in_dw", col_slots=True)
    tok = comm.send_grads(dict(s5_w_in=d_w_in), "exchange_s5_in")
    dxn0 = _mm(duz, w_in, "nt", BF16, "s5_in_dx", after=tok, b_slots=True)
    grad_x, dg_pre0 = _norm_bwd1(dh1, x, dxn0, g_pre0, "norm_pre0_bwd")

    dbb_re = _block_diag_extract(dbd_re, p, S5_GROUP).reshape(g * p, S5_GROUP)
    dbb_im = _block_diag_extract(dbd_im, p, S5_GROUP).reshape(g * p, S5_GROUP)
    dcr_col, dci_col, db_re, db_im = _s5_bbar_bwd(cr_col, ci_col, b_re2, b_im2, dbb_re, dbb_im)
    da_re, da_im, dldt = _s5_disc_bwd(a_re, a_im, ldt, dab_re.reshape(g, p), dab_im.reshape(g, p),
                                      dcr_col.reshape(g, p), dci_col.reshape(g, p))
    dc_re = _block_diag_extract(dcd_re, S5_GROUP, p)
    dc_im = -_block_diag_extract(dcd_im, S5_GROUP, p)

    small = dict(
        norm_pre=jnp.concatenate([dg_pre0, dg_pre1], axis=0), norm_post=jnp.concatenate([dg_post0, dg_post1], axis=0),
        s5_a_re=da_re, s5_a_im=da_im, s5_log_dt=dldt.reshape(g), s5_b_re=db_re.reshape(g, p, S5_GROUP),
        s5_b_im=db_im.reshape(g, p, S5_GROUP), s5_c_re=dc_re, s5_c_im=dc_im, s5_d=dd.reshape(-1),
        s5_b_glu=db_glu.reshape(-1), kv_norm=dg_kv.reshape(-1), kv_b_f=db_f[0, :nh])
    return loss, grad_x, small


_BIG = ("s5_w_in", "s5_w_glu", "s5_w_out", "kv_w", "fox_w_in", "fox_w_out")
_COL_SHARDED = ("s5_w_in", "fox_w_in")
_SMALL = ("norm_pre", "norm_post", "s5_a_re", "s5_a_im", "s5_log_dt", "s5_b_re", "s5_b_im", "s5_c_re", "s5_c_im",
          "s5_d", "s5_b_glu", "kv_norm", "kv_b_f")
_SMALL_SHARDED = ("s5_d", "s5_b_glu")
_PACK_QUANTUM = SUBLANES * LANES
_WEIGHTS = ('norm_pre', 'norm_post', 's5_w_in', 's5_a_re', 's5_a_im', 's5_log_dt', 's5_b_re', 's5_b_im', 's5_c_re', 's5_c_im',
            's5_d', 's5_w_glu', 's5_b_glu', 's5_w_out', 'kv_norm', 'kv_w', 'kv_b_f', 'fox_w_in', 'fox_w_out')


def _full_from_slots(name, slots):
    n, r, c = slots.shape
    if name in _COL_SHARDED:
        return slots.transpose(1, 0, 2).reshape(r, n * c)
    return slots.reshape(n * r, c)


def _slots_from_full(name, full):
    if name in _COL_SHARDED:
        r, nc = full.shape
        return full.reshape(r, N_DEV, nc // N_DEV).transpose(1, 0, 2)
    nr, c = full.shape
    return full.reshape(N_DEV, nr // N_DEV, c)


def _groups_last(shape):
    return len(shape) >= 3 and shape[-1] < LANES and shape[-3] % LANES == 0


def _pack(vals):
    parts = []
    for v in vals:
        flat = jnp.moveaxis(v, -3, -1).reshape(-1) if _groups_last(v.shape) else v.reshape(-1)
        parts.append(jnp.pad(flat, (0, (-flat.shape[0]) % _PACK_QUANTUM)))
    total = sum(p.shape[0] for p in parts)
    parts.append(jnp.zeros(((-total) % (N_DEV * _PACK_QUANTUM),), F32))
    return jnp.concatenate(parts).reshape(-1, LANES)


def _unpack(packed, shapes):
    flat = packed.reshape(-1)
    out, off = [], 0
    for sh in shapes:
        n = math.prod(sh)
        piece = flat[off:off + n]
        if _groups_last(sh):
            piece = jnp.moveaxis(piece.reshape(sh[:-3] + sh[-2:] + sh[-3:-2]), -1, -3)
        out.append(piece.reshape(sh))
        off += n + (-n) % _PACK_QUANTUM
    return out


class _Comm:
    _GROUPS = (("s5_w_in",) + _SMALL_SHARDED, ("s5_w_glu", "s5_w_out"), ("kv_w", "fox_w_in"), ("fox_w_out",))
    _SLOT_FORM = ("s5_w_in", "fox_w_in")

    def __init__(self, shards, vectors, early=()):
        self._shards = {**shards, **vectors}
        self._full, self._gathers = {}, {}
        self._early = list(early)
        self.token = jnp.zeros((), F32)
        for group in self._GROUPS[:-1]:
            self.token = self.token + self._start(group, ())[0, 0]
        self.late_token = None
        self._sent = []

    def _start(self, group, after):
        state, tok = _exchange_start([self._shards[n] for n in group], False, "gather_start_" + group[0], after,
                                     peers=_CHIP_PEERS)
        self._gathers[group] = state
        return tok

    def vector(self, name):
        return self._full[name]

    def weight(self, name, after):
        if name not in self._full:
            group = next(g for g in self._GROUPS if name in g)
            if group == self._GROUPS[0]:
                after = (list(after) if isinstance(after, (list, tuple)) else [after]) + self._early
            slots = _exchange_wait(self._gathers.pop(group), after, "gather_wait_" + group[0])
            slots = _forward_to_sibling(slots, "gather_forward_" + group[0])
            for n, sl in zip(group, slots):
                if n in _SMALL_SHARDED:
                    self._full[n] = sl.reshape(-1)
                else:
                    self._full[n] = sl if n in self._SLOT_FORM else _full_from_slots(n, sl)
            if group == self._GROUPS[-2]:
                self.late_token = self._start(self._GROUPS[-1], [slots[0]])[0, 0]
        return self._full[name]

    def send_grads(self, grads, name):
        names = list(grads)
        slots = [grads[n] if grads[n].ndim == 3 else _slots_from_full(n, grads[n]).astype(BF16) for n in names]
        state, tok = _exchange_start(slots, True, name + "_start")
        self._sent.append((names, state, name + "_wait"))
        return tok

    def received_grads(self, group, after):
        names, state, name = self._sent[group]
        return list(zip(names, _exchange_wait(state, after, name)))


def kernel(x, norm_pre, norm_post, s5_w_in, s5_a_re, s5_a_im, s5_log_dt, s5_b_re, s5_b_im, s5_c_re, s5_c_im, s5_d, s5_w_glu, s5_b_glu, s5_w_out, kv_norm, kv_w, kv_b_f, fox_w_in, fox_w_out, loss_target, m_norm_pre, m_norm_post, m_s5_w_in, m_s5_a_re, m_s5_a_im, m_s5_log_dt, m_s5_b_re, m_s5_b_im, m_s5_c_re, m_s5_c_im, m_s5_d, m_s5_w_glu, m_s5_b_glu, m_s5_w_out, m_kv_norm, m_kv_w, m_kv_b_f, m_fox_w_in, m_fox_w_out, v_norm_pre, v_norm_post, v_s5_w_in, v_s5_a_re, v_s5_a_im, v_s5_log_dt, v_s5_b_re, v_s5_b_im, v_s5_c_re, v_s5_c_im, v_s5_d, v_s5_w_glu, v_s5_b_glu, v_s5_w_out, v_kv_norm, v_kv_w, v_kv_b_f, v_fox_w_in, v_fox_w_out):
    env = dict(locals())
    wts = {n: env[n] for n in _WEIGHTS}
    mom = {n: env["m_" + n] for n in _WEIGHTS}
    var = {n: env["v_" + n] for n in _WEIGHTS}
    me = 4 * lax.axis_index("x") + 2 * lax.axis_index("y") + lax.axis_index("c")
    shard2d = {n: (wts[n].T if n == "kv_w" else wts[n].reshape(wts[n].shape[-2:])) for n in _BIG}
    full_shape = {n: ((wts[n].size * N_DEV,) if n in _SMALL_SHARDED else wts[n].shape) for n in _SMALL}

    def spread(n, v):
        if n not in _SMALL_SHARDED:
            return v
        flat = v.reshape(-1)
        return lax.dynamic_update_slice(jnp.zeros(full_shape[n], F32), flat, (me * flat.shape[0],))

    packed = [_pack([spread(n, src[n]) for n in _SMALL] + [jnp.zeros((1,), F32)]) for src in (wts, mom, var)]
    comm = _Comm({n: _cast_bf16(shard2d[n], "cast_" + n) for n in _BIG}, {n: wts[n].reshape(1, -1) for n in _SMALL_SHARDED}, packed)

    loss_local, grad_x, small = _local_step(
        x[0], loss_target[0], norm_pre, norm_post, kv_norm, kv_b_f, s5_a_re[0], s5_a_im[0], s5_log_dt[0],
        s5_b_re[0], s5_b_im[0], s5_c_re[0], s5_c_im[0], comm)

    small_pack = _pack([small[n] for n in _SMALL] + [loss_local.reshape(1)])
    slice_rows = small_pack.shape[0] // N_DEV
    small_state, small_tok = _exchange_start([small_pack.reshape(N_DEV, slice_rows, LANES)], True, "reduce_small_start")

    res = {}

    def finish(group, after):
        for n, recv in comm.received_grads(group, after):
            if n == "kv_w":
                res[n] = [o.T for o in _adamw(recv, wts[n].T, mom[n].T, var[n].T, "adamw_" + n)]
            else:
                res[n] = _adamw(recv, wts[n], mom[n], var[n], "adamw_" + n)

    finish(0, [small_tok, grad_x])
    my_sum = _sum_parts(_exchange_wait(small_state, res["kv_w"][0], "reduce_small_wait")[0], "sum_small")
    gather_state, gather_tok = _exchange_start([my_sum], False, "gather_small_start")
    finish(1, gather_tok)
    finish(2, gather_tok)
    g_all = _exchange_wait(gather_state, res["s5_w_in"][0], "gather_small_wait")[0].reshape(1, small_pack.shape[0], LANES)
    outs = _adamw(g_all, *packed, "adamw_small")
    unpacked = [_unpack(o, [full_shape[n] for n in _SMALL] + [(1,)]) for o in outs]
    loss = unpacked[0][-1][0]
    for i, n in enumerate(_SMALL):
        vals = [u[i] for u in unpacked]
        if n in _SMALL_SHARDED:
            k = wts[n].size
            vals = [lax.dynamic_slice(v, (me * k,), (k,)) for v in vals]
        res[n] = [v.reshape(wts[n].shape) for v in vals]

    return (loss, grad_x[None], *[res[n][0] for n in _WEIGHTS], *[res[n][1] for n in _WEIGHTS],
            *[res[n][2] for n in _WEIGHTS], *[res[n][3] for n in _WEIGHTS])
```

```python
import math

import jax
import jax.numpy as jnp
from jax import lax
from jax.experimental import pallas as pl
from jax.experimental.pallas import tpu as pltpu

F32 = jnp.float32
BF16 = jnp.bfloat16

N_DEV = 8
MESH_AXES = ("x", "y", "c")
S5_GROUP = 16
S5_STATE = 64
LANES = 128
SUBLANES = 8
GROUPS_PER_BLOCK = LANES // S5_GROUP
BLOCK_STATE = GROUPS_PER_BLOCK * S5_STATE
N_SEG = SUBLANES
HEAD_DIM = 128
RMS_EPS = 1e-6
NEG_INF = -1e30
LOG2E = math.log2(math.e)
ADAM_LR = 0.001
ADAM_B1 = 0.9
ADAM_B2 = 0.999
ADAM_EPS = 1e-08
ADAM_WD = 0.01
ADAM_STEP = 10
VMEM_LIMIT = 56 * 1024 * 1024


def _tile(n, pref, quantum=LANES):
    if n <= pref:
        return n
    t = (pref // quantum) * quantum
    while t >= quantum:
        if n % t == 0:
            return t
        t -= quantum
    return n


def _cparams(*sem):
    return pltpu.CompilerParams(dimension_semantics=sem if sem else None, vmem_limit_bytes=VMEM_LIMIT)


_DOT_DIMS = {"nn": ((1,), (0,)), "nt": ((1,), (1,)), "tn": ((0,), (0,))}


def _mm(a, b, mode, out_dtype, name, add=None, scale=None, b_cols=None, after=None, col_slots=False, b_slots=False,
        b_rows=None, epilogue=None):
    slot_w = b.shape[2] if b_slots else None
    b2d = (b.shape[1], b.shape[0] * b.shape[2]) if b_slots else b.shape
    b_shape = b2d if b_cols is None else (b2d[0], b_cols[1])
    if b_rows is not None:
        b_shape = (b_rows, b_shape[1])
    if mode == "nn":
        (M, K), (K2, N) = a.shape, b_shape
    elif mode == "nt":
        (M, K), (N, K2) = a.shape, b_shape
    else:
        (K, M), (K2, N) = a.shape, b_shape
    assert K == K2, (name, a.shape, b_shape)
    tm, tn, tk = _tile(M, 1024 if K <= 2048 else 512), (N // N_DEV if col_slots else _tile(N, 1024)), _tile(K, 4096)
    if b_slots and mode == "nn":
        tn = slot_w
    nk = K // tk
    dims = (_DOT_DIMS[mode], ((), ()))
    col0 = 0
    if b_cols is not None:
        assert mode != "tn" and b_cols[0] % (tn if mode == "nn" else tk) == 0
        col0 = b_cols[0] // (tn if mode == "nn" else tk)
    assert not b_slots or (mode == "nn" or (mode == "nt" and nk == 1 and b_cols is None))

    def body(*refs):
        a_ref, b_ref = refs[:2]
        c_ref = refs[2] if add is not None else None
        e_ref = refs[2 + (add is not None)] if epilogue is not None else None
        o_ref = refs[2 + (add is not None) + (epilogue is not None) + (after is not None)]
        if b_slots and mode == "nt":
            part = lax.dot_general(a_ref[:, :slot_w], b_ref[0], dims, preferred_element_type=F32)
            for sl in range(1, b_ref.shape[0]):
                part += lax.dot_general(a_ref[:, sl * slot_w:(sl + 1) * slot_w], b_ref[sl], dims, preferred_element_type=F32)
        else:
            part = lax.dot_general(a_ref[...], b_ref[...], dims, preferred_element_type=F32)

        def finish(r):
            if scale is not None:
                r = r * scale
            if add is not None:
                r = r + c_ref[...]
            if epilogue is not None:
                r = epilogue[0](r, e_ref[...])
            o_ref[...] = r.astype(out_dtype)

        if nk == 1:
            finish(part)
            return
        acc = refs[-1]
        k = pl.program_id(2)

        @pl.when(k == 0)
        def _():
            acc[...] = part

        @pl.when(jnp.logical_and(k > 0, k < nk - 1))
        def _():
            acc[...] += part

        @pl.when(k == nk - 1)
        def _():
            finish(acc[...] + part)

    if mode == "tn":
        a_spec = pl.BlockSpec((tk, tm), lambda i, j, k: (k, i))
    else:
        a_spec = pl.BlockSpec((tm, tk), lambda i, j, k: (i, k))
    if b_slots and mode == "nn":
        b_spec = pl.BlockSpec((None, tk, tn), lambda i, j, k: (j + col0, k, 0))
    elif b_slots:
        b_spec = pl.BlockSpec((b.shape[0], tn, slot_w), lambda i, j, k: (0, j, 0))
    elif mode == "nt":
        b_spec = pl.BlockSpec((tn, tk), lambda i, j, k: (j, k + col0))
    else:
        b_spec = pl.BlockSpec((tk, tn), lambda i, j, k: (k, j + col0))
    o_spec = pl.BlockSpec((tm, tn), lambda i, j, k: (i, j))
    in_specs = [a_spec, b_spec] + ([o_spec] if add is not None else [])
    args = (a, b) + ((add,) if add is not None else ())
    if epilogue is not None:
        in_specs.append(o_spec)
        args += (epilogue[1],)
    if after is not None:
        in_specs.append(pl.BlockSpec(after.shape, lambda i, j, k: (0, 0)))
        args += (after,)
    out_shape = jax.ShapeDtypeStruct((M, N), out_dtype)
    if col_slots:
        assert add is None
        o_spec = pl.BlockSpec((None, tm, tn), lambda i, j, k: (j, i, 0))
        out_shape = jax.ShapeDtypeStruct((N_DEV, M, tn), out_dtype)
    return pl.pallas_call(
        body, name=name, grid=(M // tm, N // tn, nk),
        in_specs=in_specs, out_specs=o_spec,
        out_shape=out_shape,
        scratch_shapes=[pltpu.VMEM((tm, tn), F32)] if nk > 1 else [],
        compiler_params=_cparams("parallel", "parallel", "arbitrary"),
    )(*args)


class _NatIn:
    def __init__(self, ref):
        self.ref = ref

    def __getitem__(self, idx):
        v = jnp.swapaxes(self.ref[...], 0, 1)
        return v.reshape(v.shape[0] * N_SEG, v.shape[2])


class _NatOut:
    def __init__(self, ref):
        self.ref = ref

    def __setitem__(self, idx, val):
        self.ref[...] = jnp.swapaxes(val.reshape(val.shape[0] // N_SEG, N_SEG, val.shape[1]), 0, 1)


def _rowcall(body, name, n_rows, ins, outs, tile_rows=256):
    tr = _tile(n_rows, tile_rows, SUBLANES * 2)
    n_in = len(ins)
    in_kinds = [k for _, k in ins]
    kinds = [k for _, _, k in outs]

    def kern(*refs):
        @pl.when(pl.program_id(0) == 0)
        def _():
            for r, kind in zip(refs[n_in:], kinds):
                if kind == "acc":
                    r[...] = jnp.zeros_like(r)

        wrapped = [_NatIn(r) if k == "nat" else r for r, k in zip(refs[:n_in], in_kinds)]
        wrapped += [_NatOut(r) if k == "nat" else r for r, k in zip(refs[n_in:], kinds)]
        body(*wrapped)

    in_specs, args = [], []
    for arr, kind in ins:
        if kind == "row":
            in_specs.append(pl.BlockSpec((tr, arr.shape[1]), lambda i: (i, 0)))
        elif kind == "nat":
            in_specs.append(pl.BlockSpec((N_SEG, tr // N_SEG, arr.shape[1]), lambda i: (0, i, 0)))
            arr = arr.reshape(N_SEG, n_rows // N_SEG, arr.shape[1])
        else:
            in_specs.append(pl.BlockSpec(arr.shape, lambda i, nd=arr.ndim: (0,) * nd))
        args.append(arr)
    out_specs, out_shape = [], []
    for width, dtype, kind in outs:
        if kind == "row":
            out_specs.append(pl.BlockSpec((tr, width), lambda i: (i, 0)))
            out_shape.append(jax.ShapeDtypeStruct((n_rows, width), dtype))
        elif kind == "right":
            out_specs.append(pl.BlockSpec((tr, width), lambda i: (i, 1)))
            out_shape.append(jax.ShapeDtypeStruct((n_rows, 2 * width), dtype))
        elif kind == "nat":
            out_specs.append(pl.BlockSpec((N_SEG, tr // N_SEG, width), lambda i: (0, i, 0)))
            out_shape.append(jax.ShapeDtypeStruct((N_SEG, n_rows // N_SEG, width), dtype))
        else:
            out_specs.append(pl.BlockSpec((1, width), lambda i: (0, 0)))
            out_shape.append(jax.ShapeDtypeStruct((1, width), F32))
    res = pl.pallas_call(
        kern, name=name, grid=(n_rows // tr,), in_specs=in_specs, out_specs=out_specs, out_shape=out_shape,
        compiler_params=_cparams("arbitrary"),
    )(*args)
    return [r.reshape(n_rows, r.shape[2]) if k == "nat" else r for r, k in zip(res, kinds)]


def _rstd(x):
    return lax.rsqrt(jnp.mean(x * x, axis=-1, keepdims=True) + RMS_EPS)


def _rms_bwd(x, g, dy):
    xh = x * _rstd(x)
    dxh = dy * g
    dx = _rstd(x) * (dxh - xh * jnp.mean(dxh * xh, axis=-1, keepdims=True))
    return dx, jnp.sum(dy * xh, axis=0, keepdims=True)


def _silu(z):
    return z * jax.nn.sigmoid(z)


def _norm_cast(x, g, name, x_kind="row"):
    def body(x_ref, g_ref, o_ref):
        x = x_ref[...]
        o_ref[...] = (x * _rstd(x) * g_ref[...]).astype(BF16)

    return _rowcall(body, name, x.shape[0], [(x, x_kind), (g, "full")], [(x.shape[1], BF16, "row")])[0]


def _resid_norm2(x, o, g_post, g_kv, g_pre, name):
    def body(x_ref, o_ref, go_ref, gk_ref, gp_ref, h_ref, nk_ref, np_ref):
        o = o_ref[...].astype(F32)
        h = x_ref[...] + o * _rstd(o) * go_ref[...]
        h_ref[...] = h
        hn = h * _rstd(h)
        nk_ref[...] = (hn * gk_ref[...]).astype(BF16)
        np_ref[...] = (hn * gp_ref[...]).astype(BF16)

    d = x.shape[1]
    return _rowcall(body, name, x.shape[0], [(x, "nat"), (o, "row"), (g_post, "full"), (g_kv, "full"), (g_pre, "full")],
                    [(d, F32, "nat"), (d, BF16, "nat"), (d, BF16, "nat")])


def _post_norm_loss(o, g, h1, target, name):
    d = o.shape[1]

    def body(o_ref, g_ref, h_ref, t_ref, dh_ref, do_ref, acc_ref, dg_ref):
        o = o_ref[...].astype(F32)
        e = h_ref[...] + o * _rstd(o) * g_ref[...] - t_ref[...]
        dh = e * (1.0 / d)
        dh_ref[...] = dh
        acc_ref[...] += jnp.sum(e * e, axis=0, keepdims=True)
        dx, dg = _rms_bwd(o, g_ref[...], dh)
        do_ref[...] = dx.astype(BF16)
        dg_ref[...] += dg

    return _rowcall(body, name, o.shape[0], [(o, "row"), (g, "full"), (h1, "row"), (target, "row")],
                    [(d, F32, "row"), (d, BF16, "row"), (d, F32, "acc"), (d, F32, "acc")])


def _gate_bwd(d_oz, o, z, name):
    def body(d_ref, o_ref, z_ref, do_ref, dz_ref):
        _, vjp = jax.vjp(lambda o, z: o * _silu(z), o_ref[...].astype(F32), z_ref[...].astype(F32))
        do, dz = vjp(d_ref[...].astype(F32))
        do_ref[...] = do.astype(BF16)
        dz_ref[...] = dz.astype(BF16)

    w = o.shape[1]
    return _rowcall(body, name, o.shape[0], [(d_oz, "row"), (o, "row"), (z, "row")], [(w, BF16, "row"), (w, BF16, "right")])


def _norm_bwd2(dh2, h1, dxn1, dhn_kv, g_pre, g_kv, o0, g_post0, name):
    def body(dh2_ref, h_ref, d1_ref, dk_ref, gp_ref, gk_ref, o_ref, go_ref, dh1_ref, do_ref, dgp_ref, dgk_ref, dgo_ref):
        h = h_ref[...]
        dx1, dg1 = _rms_bwd(h, gp_ref[...], d1_ref[...].astype(F32))
        dxk, dgk = _rms_bwd(h, gk_ref[...], dk_ref[...].astype(F32))
        dh1 = dh2_ref[...] + dx1 + dxk
        dh1_ref[...] = dh1
        dgp_ref[...] += dg1
        dgk_ref[...] += dgk
        dxo, dgo = _rms_bwd(o_ref[...].astype(F32), go_ref[...], dh1)
        do_ref[...] = dxo.astype(BF16)
        dgo_ref[...] += dgo

    d = h1.shape[1]
    return _rowcall(body, name, h1.shape[0],
                    [(dh2, "nat"), (h1, "nat"), (dxn1, "nat"), (dhn_kv, "nat"), (g_pre, "full"), (g_kv, "full"),
                     (o0, "row"), (g_post0, "full")],
                    [(d, F32, "nat"), (d, BF16, "row"), (d, F32, "acc"), (d, F32, "acc"), (d, F32, "acc")])


def _norm_bwd1(dres, x, dxn, g, name):
    def body(dr_ref, x_ref, dn_ref, g_ref, dx_ref, dg_ref):
        dx, dg = _rms_bwd(x_ref[...], g_ref[...], dn_ref[...].astype(F32))
        dx_ref[...] = dr_ref[...] + dx
        dg_ref[...] += dg

    d = x.shape[1]
    return _rowcall(body, name, x.shape[0], [(dres, "nat"), (x, "nat"), (dxn, "row"), (g, "full")],
                    [(d, F32, "nat"), (d, F32, "acc")])


def _s5_gate(y_ssm, gp, b_glu, z, name):
    def body(y_ref, gp_ref, b_ref, z_ref, o_ref):
        yg = jax.nn.gelu(y_ref[...].astype(F32))
        o_ref[...] = (yg * jax.nn.sigmoid(gp_ref[...] + b_ref[...]) * _silu(z_ref[...].astype(F32))).astype(BF16)

    return _rowcall(body, name, y_ssm.shape[0], [(y_ssm, "row"), (gp, "row"), (b_glu, "full"), (z, "row")],
                    [(y_ssm.shape[1], BF16, "row")])[0]


def _s5_gate_bwd(dy3, y_ssm, gp, b_glu, z, name):
    def body(d_ref, y_ref, gp_ref, b_ref, z_ref, dz_ref, dgp_ref, dyg_ref, db_ref):
        yg = jax.nn.gelu(y_ref[...].astype(F32))
        _, vjp = jax.vjp(lambda yg, gp, z: yg * jax.nn.sigmoid(gp) * _silu(z), yg, gp_ref[...] + b_ref[...],
                         z_ref[...].astype(F32))
        dyg, dgp, dz = vjp(d_ref[...].astype(F32))
        dz_ref[...] = dz.astype(BF16)
        dgp_ref[...] = dgp.astype(BF16)
        dyg_ref[...] = dyg.astype(BF16)
        db_ref[...] += jnp.sum(dgp, axis=0, keepdims=True)

    w = y_ssm.shape[1]
    return _rowcall(body, name, y_ssm.shape[0],
                    [(dy3, "row"), (y_ssm, "row"), (gp, "row"), (b_glu, "full"), (z, "row")],
                    [(w, BF16, "right"), (w, BF16, "row"), (w, BF16, "row"), (w, F32, "acc")])


def _cast_bf16(x, name):
    r, c = x.shape
    by_cols = r % (2 * SUBLANES) != 0
    tr, tc = (r, _tile(c, 256)) if by_cols else (_tile(r, 512, 2 * SUBLANES), c)
    pos = (lambda i: (0, i)) if by_cols else (lambda i: (i, 0))

    def body(x_ref, o_ref):
        o_ref[...] = x_ref[...].astype(BF16)

    return pl.pallas_call(
        body, name=name, grid=(c // tc if by_cols else r // tr,),
        in_specs=[pl.BlockSpec((tr, tc), pos)], out_specs=pl.BlockSpec((tr, tc), pos),
        out_shape=jax.ShapeDtypeStruct((r, c), BF16), compiler_params=_cparams("parallel"),
    )(x)


def _concat_cast(a, b, name):
    def body(a_ref, b_ref, o_ref):
        w = a_ref.shape[1]
        o_ref[:, :w] = a_ref[...].astype(BF16)
        o_ref[:, w:] = b_ref[...].astype(BF16)

    return _rowcall(body, name, a.shape[0], [(a, "row"), (b, "row")], [(a.shape[1] + b.shape[1], BF16, "row")])[0]


def _disc(ar, ai, ldt):
    dt = jnp.exp(ldt)
    mag = jnp.exp(ar * dt)
    abr = mag * jnp.cos(ai * dt)
    abi = mag * jnp.sin(ai * dt)
    den = ar * ar + ai * ai
    nr = abr - 1.0
    return abr, abi, (nr * ar + abi * ai) / den, (abi * ar - nr * ai) / den


def _s5_disc_fwd(a_re, a_im, ldt):
    def body(ar, ai, ld, o1, o2, o3, o4):
        o1[...], o2[...], o3[...], o4[...] = _disc(ar[...], ai[...], ld[...])

    sh = jax.ShapeDtypeStruct(a_re.shape, F32)
    return pl.pallas_call(body, name="s5_disc_fwd", out_shape=(sh, sh, sh, sh))(a_re, a_im, ldt)


def _s5_disc_bwd(a_re, a_im, ldt, d_abr, d_abi, d_cr, d_ci):
    def body(ar, ai, ld, g1, g2, g3, g4, o1, o2, o3):
        _, vjp = jax.vjp(_disc, ar[...], ai[...], ld[...])
        o1[...], o2[...], o3[...] = vjp((g1[...], g2[...], g3[...], g4[...]))

    sh = jax.ShapeDtypeStruct(a_re.shape, F32)
    return pl.pallas_call(body, name="s5_disc_bwd", out_shape=(sh, sh, jax.ShapeDtypeStruct(ldt.shape, F32)))(
        a_re, a_im, ldt, d_abr, d_abi, d_cr, d_ci)


def _bbar(cr, ci, br, bi):
    return cr * br - ci * bi, cr * bi + ci * br


def _s5_bbar_fwd(cr_col, ci_col, b_re, b_im):
    def body(cr, ci, br, bi, o1, o2):
        o1[...], o2[...] = _bbar(cr[...], ci[...], br[...], bi[...])

    w = b_re.shape[1]
    return _rowcall(body, "s5_bbar_fwd", b_re.shape[0], [(cr_col, "row"), (ci_col, "row"), (b_re, "row"), (b_im, "row")],
                    [(w, F32, "row"), (w, F32, "row")], tile_rows=1024)


def _s5_bbar_bwd(cr_col, ci_col, b_re, b_im, d_re, d_im):
    def body(cr, ci, br, bi, g1, g2, o1, o2, o3, o4):
        _, vjp = jax.vjp(_bbar, cr[...], ci[...], br[...], bi[...])
        o1[...], o2[...], o3[...], o4[...] = vjp((g1[...], g2[...]))

    w = b_re.shape[1]
    return _rowcall(body, "s5_bbar_bwd", b_re.shape[0],
                    [(cr_col, "row"), (ci_col, "row"), (b_re, "row"), (b_im, "row"), (d_re, "row"), (d_im, "row")],
                    [(1, F32, "row"), (1, F32, "row"), (w, F32, "row"), (w, F32, "row")], tile_rows=1024)


def _block_diag(t):
    g, a, b = t.shape
    nb = g // GROUPS_PER_BLOCK
    t4 = t.reshape(nb, GROUPS_PER_BLOCK, a, b).transpose(0, 1, 3, 2)
    eye = jnp.eye(GROUPS_PER_BLOCK, dtype=t.dtype)
    return (t4[:, :, :, None, :] * eye[None, :, None, :, None]).reshape(nb, GROUPS_PER_BLOCK * b, GROUPS_PER_BLOCK * a)


def _block_diag_extract(d, a, b):
    nb = d.shape[0]
    d5 = d.reshape(nb, GROUPS_PER_BLOCK, b, GROUPS_PER_BLOCK, a)
    diag = jnp.stack([d5[:, g, :, g, :] for g in range(GROUPS_PER_BLOCK)], axis=1)
    return diag.transpose(0, 1, 3, 2).reshape(nb * GROUPS_PER_BLOCK, a, b)


def _scan_step(ar, ai, hr, hi, xr, xi):
    return ar * hr - ai * hi + xr, ar * hi + ai * hr + xi


def _s5_blocks_per_step(nb, full):
    want = 2 if full else 4
    while nb % want:
        want //= 2
    return want


def _s5_scan_fwd(u, bd_re, bd_im, cd_re, cd_im, ab_re, ab_im, init_re, init_im, d_row, full, name):
    s, w = u.shape
    nb = w // LANES
    rows = _tile(s, 512, SUBLANES)
    nc = s // rows
    steps = rows // N_SEG
    ns = nb * BLOCK_STATE

    nblk = _s5_blocks_per_step(nb, full)

    def body(u_ref, bdr, bdi, cdr, cdi, ar_ref, ai_ref, ir_ref, ii_ref, d_ref, *outs):
        if full:
            y_ref, yg_ref, hr_out, hi_out, er_ref, ei_ref, hr_ref, hi_ref, cr, ci = outs
        else:
            er_ref, ei_ref, hr_ref, hi_ref, cr, ci = outs
        c = pl.program_id(1)
        cols = lambda b, width: slice(b * width, (b + 1) * width)

        @pl.when(c == 0)
        def _():
            cr[...] = ir_ref[...]
            ci[...] = ii_ref[...]

        for b in range(nblk):
            ub = u_ref[:, cols(b, LANES)].astype(BF16)
            hr_ref[:, cols(b, BLOCK_STATE)] = jnp.dot(ub, bdr[b], preferred_element_type=F32)
            hi_ref[:, cols(b, BLOCK_STATE)] = jnp.dot(ub, bdi[b], preferred_element_type=F32)
        ar, ai = ar_ref[...], ai_ref[...]

        hr, hi = cr[...], ci[...]
        for j in range(steps):
            rows_j = pl.ds(j * N_SEG, N_SEG)
            hr, hi = _scan_step(ar, ai, hr, hi, hr_ref[rows_j, :], hi_ref[rows_j, :])
            hr_ref[rows_j, :] = hr
            hi_ref[rows_j, :] = hi
        cr[...] = hr
        ci[...] = hi
        if full:
            hr_out[...] = hr_ref[...].astype(BF16)
            hi_out[...] = hi_ref[...].astype(BF16)
            for b in range(nblk):
                st_b, ln_b = cols(b, BLOCK_STATE), cols(b, LANES)
                y = (jnp.dot(hr_out[:, st_b], cdr[b], preferred_element_type=F32)
                     + jnp.dot(hi_out[:, st_b], cdi[b], preferred_element_type=F32)
                     + d_ref[:, ln_b] * u_ref[:, ln_b])
                y_ref[:, ln_b] = y.astype(BF16)
                yg_ref[:, ln_b] = jax.nn.gelu(y).astype(BF16)

        @pl.when(c == nc - 1)
        def _():
            er_ref[...] = hr
            ei_ref[...] = hi

    lanes, states = LANES * nblk, BLOCK_STATE * nblk
    blk3 = lambda a: pl.BlockSpec((nblk,) + a.shape[1:], lambda k, c: (k, 0, 0))
    seg = pl.BlockSpec((N_SEG, states), lambda k, c: (0, k))
    st = pl.BlockSpec((rows, states), lambda k, c: (c, k))
    in_specs = [pl.BlockSpec((rows, lanes), lambda k, c: (c, k)), blk3(bd_re), blk3(bd_im), blk3(cd_re), blk3(cd_im),
                seg, seg, seg, seg, pl.BlockSpec((1, lanes), lambda k, c: (0, k))]
    seg_shape = jax.ShapeDtypeStruct((N_SEG, ns), F32)
    st_shape = jax.ShapeDtypeStruct((s, ns), BF16)
    scratch = [pltpu.VMEM((rows, states), F32)] * 2 + [pltpu.VMEM((N_SEG, states), F32)] * 2
    if full:
        ych = pl.BlockSpec((rows, lanes), lambda k, c: (c, k))
        out_specs = [ych, ych, st, st, seg, seg]
        out_shape = [jax.ShapeDtypeStruct((s, w), BF16), jax.ShapeDtypeStruct((s, w), BF16), st_shape, st_shape, seg_shape, seg_shape]
    else:
        out_specs = [seg, seg]
        out_shape = [seg_shape, seg_shape]
    return pl.pallas_call(
        body, name=name, grid=(nb // nblk, nc), in_specs=in_specs, out_specs=out_specs, out_shape=out_shape,
        scratch_shapes=scratch, compiler_params=_cparams("parallel", "arbitrary"),
    )(u, bd_re, bd_im, cd_re, cd_im, ab_re, ab_im, init_re, init_im, d_row)


def _s5_seg_fix(e_re, e_im, ab_re, ab_im, seg_len, reverse, name):
    assert seg_len & (seg_len - 1) == 0

    def body(er, ei, ar, ai, o_re, o_im):
        pr, pi = ar[0:1, :], ai[0:1, :]
        for _ in range(int(math.log2(seg_len))):
            pr, pi = pr * pr - pi * pi, 2.0 * pr * pi
        tr = jnp.zeros_like(pr)
        ti = jnp.zeros_like(pr)
        order = list(range(N_SEG - 1, -1, -1)) if reverse else list(range(N_SEG))
        for n, sgm in enumerate(order):
            o_re[sgm:sgm + 1, :] = tr
            o_im[sgm:sgm + 1, :] = ti
            if n < N_SEG - 1:
                tr, ti = _scan_step(pr, pi, tr, ti, er[sgm:sgm + 1, :], ei[sgm:sgm + 1, :])

    sh = jax.ShapeDtypeStruct(e_re.shape, F32)
    return pl.pallas_call(body, name=name, out_shape=(sh, sh))(e_re, e_im, ab_re, ab_im)


def _s5_scan_bwd(dy, u, h_re, h_im, bd_re, bd_im, cd_re, cd_im, ab_re, ab_imn, gin_re, gin_im, d_row, full, name, duz=None):
    s, w = u.shape
    nb = w // LANES
    rows = _tile(s, 512, SUBLANES)
    nc = s // rows
    steps = rows // N_SEG
    ns = nb * BLOCK_STATE

    nblk = _s5_blocks_per_step(nb, full)

    def body(dy_ref, u_ref, hr_ref, hi_ref, bdr, bdi, cdr, cdi, ar_ref, ai_ref, ir_ref, ii_ref, d_ref, *outs):
        if full:
            _, du_ref, dbr_ref, dbi_ref, dcr_ref, dci_ref, dar_ref, dai_ref, dd_ref, gr, gi, accr, acci = outs
        else:
            er_ref, ei_ref, gr, gi = outs
        c = pl.program_id(1)
        cols = lambda b, width: slice(b * width, (b + 1) * width)

        @pl.when(c == 0)
        def _():
            gr[pl.ds(rows, N_SEG), :] = ir_ref[...]
            gi[pl.ds(rows, N_SEG), :] = ii_ref[...]
            if full:
                for r in (dbr_ref, dbi_ref, dcr_ref, dci_ref, dd_ref, accr, acci):
                    r[...] = jnp.zeros_like(r)

        nt = (_DOT_DIMS["nt"], ((), ()))
        tn = (_DOT_DIMS["tn"], ((), ()))
        for b in range(nblk):
            dyb = dy_ref[:, cols(b, LANES)]
            gr[pl.ds(0, rows), cols(b, BLOCK_STATE)] = lax.dot_general(dyb, cdr[b], nt, preferred_element_type=F32)
            gi[pl.ds(0, rows), cols(b, BLOCK_STATE)] = lax.dot_general(dyb, cdi[b], nt, preferred_element_type=F32)
        ar, ai = ar_ref[...], ai_ref[...]

        g0r, g0i = gr[pl.ds(rows, N_SEG), :], gi[pl.ds(rows, N_SEG), :]
        for j in range(steps - 1, -1, -1):
            rows_j = pl.ds(j * N_SEG, N_SEG)
            g0r, g0i = _scan_step(ar, ai, g0r, g0i, gr[rows_j, :], gi[rows_j, :])
            gr[rows_j, :] = g0r
            gi[rows_j, :] = g0i
        if full:
            for b in range(nblk):
                st_b, ln_b = cols(b, BLOCK_STATE), cols(b, LANES)
                hr, hi = hr_ref[:, st_b], hi_ref[:, st_b]
                gnr, gni = gr[pl.ds(N_SEG, rows), st_b], gi[pl.ds(N_SEG, rows), st_b]
                accr[:, st_b] += jnp.sum((gnr * hr + gni * hi).reshape(steps, N_SEG, BLOCK_STATE), axis=0)
                acci[:, st_b] += jnp.sum((gni * hr - gnr * hi).reshape(steps, N_SEG, BLOCK_STATE), axis=0)
                dyb = dy_ref[:, ln_b]
                ub = u_ref[:, ln_b].astype(BF16)
                gbr, gbi = gr[pl.ds(0, rows), st_b].astype(BF16), gi[pl.ds(0, rows), st_b].astype(BF16)
                dcr_ref[b] += lax.dot_general(hr.astype(BF16), dyb, tn, preferred_element_type=F32)
                dci_ref[b] += lax.dot_general(hi.astype(BF16), dyb, tn, preferred_element_type=F32)
                dbr_ref[b] += lax.dot_general(ub, gbr, tn, preferred_element_type=F32)
                dbi_ref[b] += lax.dot_general(ub, gbi, tn, preferred_element_type=F32)
                du_ref[:, ln_b] = (lax.dot_general(gbr, bdr[b], nt, preferred_element_type=F32)
                                   + lax.dot_general(gbi, bdi[b], nt, preferred_element_type=F32)
                                   + d_ref[:, ln_b] * dy_ref[:, ln_b].astype(F32)).astype(BF16)
                dd_ref[:, ln_b] += jnp.sum(dy_ref[:, ln_b].astype(F32) * u_ref[:, ln_b], axis=0, keepdims=True)
        gr[pl.ds(rows, N_SEG), :] = g0r
        gi[pl.ds(rows, N_SEG), :] = g0i

        @pl.when(c == nc - 1)
        def _():
            if full:
                dar_ref[...] = jnp.sum(accr[...], axis=0, keepdims=True)
                dai_ref[...] = jnp.sum(acci[...], axis=0, keepdims=True)
            else:
                er_ref[...] = g0r
                ei_ref[...] = g0i

    lanes, states = LANES * nblk, BLOCK_STATE * nblk
    rev = lambda k, c: (nc - 1 - c, k)
    blk3 = lambda a: pl.BlockSpec((nblk,) + a.shape[1:], lambda k, c: (k, 0, 0))
    seg = pl.BlockSpec((N_SEG, states), lambda k, c: (0, k))
    st = pl.BlockSpec((rows, states), rev)
    ch = pl.BlockSpec((rows, lanes), rev)
    vec = pl.BlockSpec((1, lanes), lambda k, c: (0, k))
    if not full:
        st = pl.BlockSpec((rows, states), lambda k, c: (0, k))
    in_specs = [ch, ch if full else pl.BlockSpec((rows, lanes), lambda k, c: (0, k)), st, st,
                blk3(bd_re), blk3(bd_im), blk3(cd_re), blk3(cd_im), seg, seg, seg, seg, vec]
    args = [dy, u, h_re, h_im, bd_re, bd_im, cd_re, cd_im, ab_re, ab_imn, gin_re, gin_im, d_row]
    gbuf = [pltpu.VMEM((rows + N_SEG, states), F32)] * 2
    if full:
        row1 = pl.BlockSpec((1, states), lambda k, c: (0, k))
        out_specs = [ch, blk3(bd_re), blk3(bd_im), blk3(cd_re), blk3(cd_im), row1, row1, vec]
        out_shape = [jax.ShapeDtypeStruct(duz.shape, BF16),
                     jax.ShapeDtypeStruct(bd_re.shape, F32), jax.ShapeDtypeStruct(bd_im.shape, F32),
                     jax.ShapeDtypeStruct(cd_re.shape, F32), jax.ShapeDtypeStruct(cd_im.shape, F32),
                     jax.ShapeDtypeStruct((1, ns), F32), jax.ShapeDtypeStruct((1, ns), F32),
                     jax.ShapeDtypeStruct((1, w), F32)]
        scratch = gbuf + [pltpu.VMEM((N_SEG, states), F32)] * 2
        in_specs.append(pl.BlockSpec(memory_space=pl.ANY))
        args.append(duz)
        aliases = {len(args) - 1: 0}
    else:
        out_specs = [seg, seg]
        out_shape = [jax.ShapeDtypeStruct((N_SEG, ns), F32)] * 2
        scratch = gbuf
        aliases = {}
    return pl.pallas_call(
        body, name=name, grid=(nb // nblk, nc), in_specs=in_specs, out_specs=out_specs, out_shape=out_shape,
        input_output_aliases=aliases, scratch_shapes=scratch, compiler_params=_cparams("parallel", "arbitrary"),
    )(*args)


def _log_sigmoid(x):
    return jnp.minimum(x, 0.0) - jnp.log(1.0 + jnp.exp(-jnp.abs(x)))


def _tri(n, upper):
    r = lax.broadcasted_iota(jnp.int32, (n, n), 0)
    c = lax.broadcasted_iota(jnp.int32, (n, n), 1)
    return jnp.where((c >= r) if upper else (r >= c), 1.0, 0.0).astype(F32)


def _cum_fwd(f_logit, b_row, name):
    s, w = f_logit.shape
    t = _tile(s, 256, SUBLANES)

    def body(f_ref, b_ref, o_ref, carry):
        @pl.when(pl.program_id(0) == 0)
        def _():
            carry[...] = jnp.zeros_like(carry)

        lf = _log_sigmoid(f_ref[...] + b_ref[...])
        cum = jnp.dot(_tri(t, False), lf, precision=lax.Precision.HIGHEST, preferred_element_type=F32) + carry[...]
        o_ref[...] = cum * LOG2E
        carry[...] = cum[t - 1:t, :]

    return pl.pallas_call(
        body, name=name, grid=(s // t,),
        in_specs=[pl.BlockSpec((t, w), lambda i: (i, 0)), pl.BlockSpec((1, w), lambda i: (0, 0))],
        out_specs=pl.BlockSpec((t, w), lambda i: (i, 0)), out_shape=jax.ShapeDtypeStruct((s, w), F32),
        scratch_shapes=[pltpu.VMEM((1, w), F32)], compiler_params=_cparams("arbitrary"),
    )(f_logit, b_row)


def _cum_bwd(dcq, dck, f_logit, b_row, name):
    s, w = f_logit.shape
    t = _tile(s, 256, SUBLANES)
    nt = s // t

    def body(q_ref, k_ref, f_ref, b_ref, df_ref, db_ref, carry):
        @pl.when(pl.program_id(0) == 0)
        def _():
            carry[...] = jnp.zeros_like(carry)
            db_ref[...] = jnp.zeros_like(db_ref)

        dc = q_ref[...] - k_ref[...]
        rc = jnp.dot(_tri(t, True), dc, precision=lax.Precision.HIGHEST, preferred_element_type=F32) + carry[...]
        carry[...] = rc[0:1, :]
        df = rc * (1.0 - jax.nn.sigmoid(f_ref[...] + b_ref[...]))
        df_ref[...] = df.astype(BF16)
        db_ref[...] += jnp.sum(df, axis=0, keepdims=True)

    rev = pl.BlockSpec((t, w), lambda i: (nt - 1 - i, 0))
    one = pl.BlockSpec((1, w), lambda i: (0, 0))
    return pl.pallas_call(
        body, name=name, grid=(nt,), in_specs=[rev, rev, rev, one], out_specs=[rev, one],
        out_shape=[jax.ShapeDtypeStruct((s, w), BF16), jax.ShapeDtypeStruct((1, w), F32)],
        scratch_shapes=[pltpu.VMEM((1, w), F32)], compiler_params=_cparams("arbitrary"),
    )(dcq, dck, f_logit, b_row)


def _head_col(cum_tile, h):
    lane = lax.broadcasted_iota(jnp.int32, cum_tile.shape, 1)
    return jnp.sum(jnp.where(lane == h, cum_tile, 0.0), axis=1, keepdims=True)


def _attn_tiles(s):
    return _tile(s, 512, LANES)


def _exp2_rows(sc, sub):
    return jnp.concatenate([jnp.exp2(sc[:, b * LANES:(b + 1) * LANES] - sub) for b in range(sc.shape[1] // LANES)], axis=1)


def _row_of(rep):
    return jnp.transpose(rep)[0:1, :]


def _causal(sc, keys_on_rows):
    r = lax.broadcasted_iota(jnp.int32, sc.shape, 0)
    c = lax.broadcasted_iota(jnp.int32, sc.shape, 1)
    return jnp.where((r <= c) if keys_on_rows else (c <= r), sc, NEG_INF)


def _fox_fwd(q2, kv, cum2_t, z, name):
    s, w = q2.shape
    nh = w // HEAD_DIM
    tq = _attn_tiles(s)
    nq = s // tq
    nt = (_DOT_DIMS["nt"], ((), ()))

    def body(q_ref, k_ref, v_ref, ct_ref, z_ref, o_ref, oz_ref, lse_row_ref, m_s, acc_s, vaug, s_buf):
        i = pl.program_id(1)

        @pl.when(i == 0)
        def _():
            vaug[:, :HEAD_DIM] = v_ref[...]
            vaug[:, HEAD_DIM:] = jnp.ones((s, LANES), BF16)

        qb = q_ref[...]
        m_s[...] = jnp.full_like(m_s, NEG_INF)
        acc_s[...] = jnp.zeros_like(acc_s)

        def scores(j):
            off = pl.multiple_of(j * tq, tq)
            return lax.dot_general(qb, k_ref[pl.ds(off, tq), :], nt, preferred_element_type=F32) - ct_ref[:, pl.ds(off, tq)]

        def softmax_pv(j, sc):
            m_old = m_s[...]
            m_new = jnp.maximum(m_old, jnp.max(sc, axis=1, keepdims=True))
            p = _exp2_rows(sc, m_new)
            alpha = jnp.exp2(m_old - m_new)
            pv = jnp.dot(p.astype(BF16), vaug[pl.ds(pl.multiple_of(j * tq, tq), tq), :], preferred_element_type=F32)
            acc_s[...] = jnp.concatenate([alpha, alpha], axis=1) * acc_s[...] + pv
            m_s[...] = m_new

        s_buf[...] = scores(0)

        def loop(j, carry):
            nxt = scores(j + 1)
            softmax_pv(j, s_buf[...])
            s_buf[...] = nxt
            return carry

        lax.fori_loop(0, i, loop, 0)
        softmax_pv(i, _causal(s_buf[...], False))
        l = acc_s[:, HEAD_DIM:]
        o = acc_s[:, :HEAD_DIM] / l
        o_ref[...] = o.astype(BF16)
        oz_ref[...] = (o * _silu(z_ref[...].astype(F32))).astype(BF16)
        lse_row_ref[...] = _row_of(m_s[...] + jnp.log(l) * LOG2E)

    return pl.pallas_call(
        body, name=name, grid=(nh, nq),
        in_specs=[pl.BlockSpec((tq, HEAD_DIM), lambda h, i: (i, h)),
                  pl.BlockSpec((s, HEAD_DIM), lambda h, i: (0, h)),
                  pl.BlockSpec((s, HEAD_DIM), lambda h, i: (0, nh + h)),
                  pl.BlockSpec((None, 1, s), lambda h, i: (h, 0, 0)),
                  pl.BlockSpec((tq, HEAD_DIM), lambda h, i: (i, h))],
        out_specs=[pl.BlockSpec((tq, HEAD_DIM), lambda h, i: (i, h)),
                   pl.BlockSpec((tq, HEAD_DIM), lambda h, i: (i, h)),
                   pl.BlockSpec((None, 1, tq), lambda h, i: (h, 0, i))],
        out_shape=[jax.ShapeDtypeStruct((s, w), BF16), jax.ShapeDtypeStruct((s, w), BF16),
                   jax.ShapeDtypeStruct((nh, 1, s), F32)],
        scratch_shapes=[pltpu.VMEM((tq, LANES), F32), pltpu.VMEM((tq, HEAD_DIM + LANES), F32),
                        pltpu.VMEM((s, HEAD_DIM + LANES), BF16), pltpu.VMEM((tq, tq), F32)],
        compiler_params=_cparams("arbitrary", "arbitrary"),
    )(q2, kv, kv, cum2_t, z)


def _fox_bwd(q2, kv, do, o, lse2_t, cum2, dqz, name):
    s, w = q2.shape
    nh = w // HEAD_DIM
    tk = _attn_tiles(s)
    nk = s // tk
    scale = HEAD_DIM ** -0.5
    nt = (_DOT_DIMS["nt"], ((), ()))
    tn = (_DOT_DIMS["tn"], ((), ()))

    def body(q_ref, k_ref, v_ref, do_ref, o_ref, lse_ref, c_ref, _, dk_ref, dv_ref, dq_ref, dcq_ref, dck_ref,
             dk_s, dv_s, dc_s, dq_s, dcq_s, dl_s, s_buf, dp_buf):
        h, j = pl.program_id(0), pl.program_id(1)

        @pl.when(j == 0)
        def _():
            dq_s[...] = jnp.zeros_like(dq_s)
            dcq_s[...] = jnp.zeros_like(dcq_s)
            for i in range(nk):
                rows = pl.ds(i * tk, tk)
                d = jnp.sum(do_ref[rows, :].astype(F32) * o_ref[rows, :].astype(F32), axis=1, keepdims=True)
                dl_s[:, i * tk:(i + 1) * tk] = _row_of(jnp.broadcast_to(d, (tk, LANES)))

        kb = k_ref[...]
        vb = v_ref[...]
        ck = jnp.broadcast_to(_head_col(c_ref[...], h), (tk, LANES))
        dk_s[...] = jnp.zeros_like(dk_s)
        dv_s[...] = jnp.zeros_like(dv_s)
        dc_s[...] = jnp.zeros_like(dc_s)

        def scores(i):
            off = pl.multiple_of(i * tk, tk)
            sc = lax.dot_general(kb, q_ref[pl.ds(off, tk), :], nt, preferred_element_type=F32) - lse_ref[:, pl.ds(off, tk)]
            dp = lax.dot_general(vb, do_ref[pl.ds(off, tk), :], nt, preferred_element_type=F32) - dl_s[:, pl.ds(off, tk)]
            return sc, dp

        def accumulate(i, sc, dp):
            off = pl.multiple_of(i * tk, tk)
            p = _exp2_rows(sc, ck)
            dv_s[...] += jnp.dot(p.astype(BF16), do_ref[pl.ds(off, tk), :], preferred_element_type=F32)
            ds = p * dp
            dsb = ds.astype(BF16)
            dk_s[...] += jnp.dot(dsb, q_ref[pl.ds(off, tk), :], preferred_element_type=F32)
            dq_s[pl.ds(off, tk), :] += lax.dot_general(dsb, kb, tn, preferred_element_type=F32)
            dcq_s[:, pl.ds(off, tk)] += jnp.sum(ds, axis=0, keepdims=True)
            part = ds[:, :LANES]
            for b in range(1, tk // LANES):
                part = part + ds[:, b * LANES:(b + 1) * LANES]
            dc_s[...] += part

        sc0, dp0 = scores(j)
        s_buf[...] = _causal(sc0, True)
        dp_buf[...] = dp0

        def loop(i, carry):
            nxt = scores(i + 1)
            accumulate(i, s_buf[...], dp_buf[...])
            s_buf[...], dp_buf[...] = nxt
            return carry

        lax.fori_loop(j, nk - 1, loop, 0)
        accumulate(nk - 1, s_buf[...], dp_buf[...])
        dk_ref[...] = (dk_s[...] * (1.0 / LOG2E)).astype(BF16)
        dv_ref[...] = dv_s[...].astype(BF16)
        dck_ref[...] = jnp.sum(jnp.transpose(dc_s[...]), axis=0, keepdims=True)

        @pl.when(j == nk - 1)
        def _():
            dq_ref[...] = (dq_s[...] * scale).astype(BF16)
            dcq_ref[...] = dcq_s[...]

    col = pl.BlockSpec((s, HEAD_DIM), lambda h, j: (0, h))
    row = pl.BlockSpec((None, 1, s), lambda h, j: (h, 0, 0))
    kspec = pl.BlockSpec((tk, HEAD_DIM), lambda h, j: (j, h))
    return pl.pallas_call(
        body, name=name, grid=(nh, nk),
        in_specs=[col, kspec, pl.BlockSpec((tk, HEAD_DIM), lambda h, j: (j, nh + h)), col, col, row,
                  pl.BlockSpec((tk, LANES), lambda h, j: (j, 0)), pl.BlockSpec(memory_space=pl.ANY)],
        out_specs=[kspec, kspec, col, row, pl.BlockSpec((None, 1, tk), lambda h, j: (h, 0, j))],
        out_shape=[jax.ShapeDtypeStruct((s, w), BF16), jax.ShapeDtypeStruct((s, w), BF16),
                   jax.ShapeDtypeStruct(dqz.shape, BF16), jax.ShapeDtypeStruct((nh, 1, s), F32),
                   jax.ShapeDtypeStruct((nh, 1, s), F32)],
        input_output_aliases={7: 2},
        scratch_shapes=[pltpu.VMEM((tk, HEAD_DIM), F32), pltpu.VMEM((tk, HEAD_DIM), F32), pltpu.VMEM((tk, LANES), F32),
                        pltpu.VMEM((s, HEAD_DIM), F32), pltpu.VMEM((1, s), F32), pltpu.VMEM((1, s), F32),
                        pltpu.VMEM((tk, tk), F32), pltpu.VMEM((tk, tk), F32)],
        compiler_params=_cparams("arbitrary", "arbitrary"),
    )(q2, kv, kv, do, o, lse2_t, cum2, dqz)


_ALL_PEERS = tuple(range(1, N_DEV))
_CHIP_PEERS = (1, 2, 4, 6)


def _exchange_copies(ins, outs, send_sems, recv_sems, local_sems, scatter, peers=_ALL_PEERS):
    x, y, c = (lax.axis_index(a) for a in MESH_AXES)
    me = 4 * x + 2 * y + c
    local, remote = [], []
    for a in range(len(ins)):
        local.append(pltpu.make_async_copy(ins[a].at[me] if scatter else ins[a], outs[a].at[me], local_sems.at[a]))
        for k in peers:
            px, py, pc = (1 - x if k & 4 else x), (1 - y if k & 2 else y), (1 - c if k & 1 else c)
            remote.append(pltpu.make_async_remote_copy(
                src_ref=ins[a].at[4 * px + 2 * py + pc] if scatter else ins[a], dst_ref=outs[a].at[me],
                send_sem=send_sems.at[a * (N_DEV - 1) + k - 1], recv_sem=recv_sems.at[a * (N_DEV - 1) + k - 1],
                device_id=(px, py, pc), device_id_type=pl.DeviceIdType.MESH))
    return local, remote


def _exchange_out_shapes(arrs, scatter):
    return [((N_DEV,) + a.shape[1:]) if scatter else ((N_DEV,) + a.shape) for a in arrs]


_HBM =pl.BlockSpec(memory_space=pltpu.HBM)
_SEM = pl.BlockSpec(memory_space=pltpu.SEMAPHORE)


def _exchange_start(arrs, scatter, name, after=(), peers=_ALL_PEERS):
    n = len(arrs)
    after = list(after)
    lands = [lax.empty(s, a.dtype) for s, a in zip(_exchange_out_shapes(arrs, scatter), arrs)]

    def body(*refs):
        ins, outs = refs[:n], refs[n:2 * n]
        send_sems, recv_sems, local_sems = refs[2 * n + len(after):2 * n + len(after) + 3]
        token = refs[-1]
        local, remote = _exchange_copies(ins, outs, send_sems, recv_sems, local_sems, scatter, peers)
        for cp in local + remote:
            cp.start()
        token[...] = jnp.zeros_like(token)

    hbm = lambda a: pltpu.HBM(a.shape, a.dtype)
    res = pl.pallas_call(
        body, name=name,
        out_shape=(pltpu.SemaphoreType.DMA((n * (N_DEV - 1),)), pltpu.SemaphoreType.DMA((n * (N_DEV - 1),)),
                   pltpu.SemaphoreType.DMA((n,)), *[hbm(a) for a in arrs], *[hbm(a) for a in lands],
                   jax.ShapeDtypeStruct((SUBLANES, LANES), F32)),
        in_specs=[_HBM] * (2 * n) + [pl.BlockSpec(memory_space=pl.ANY)] * len(after),
        out_specs=(_SEM, _SEM, _SEM, *[_HBM] * (2 * n), pl.BlockSpec(memory_space=pltpu.VMEM)),
        input_output_aliases={i: 3 + i for i in range(2 * n)},
        compiler_params=pltpu.CompilerParams(has_side_effects=pltpu.SideEffectType.DATAFLOW_SIDE_EFFECTING),
    )(*[pltpu.with_memory_space_constraint(a, pltpu.HBM) for a in list(arrs) + lands], *after)
    return (n, scatter, res[:3], res[3:3 + n], res[3 + n:3 + 2 * n], peers), res[-1]


def _exchange_wait(state, after, name):
    n, scatter, sems, srcs, lands, peers = state
    after = list(after) if isinstance(after, (list, tuple)) else [after]

    def body(*refs):
        ins, outs = refs[:n], refs[n:2 * n]
        send_sems, recv_sems, local_sems = refs[2 * n:2 * n + 3]
        local, remote = _exchange_copies(ins, outs, send_sems, recv_sems, local_sems, scatter, peers)
        for cp in remote:
            cp.wait_send()
            cp.wait_recv()
        for cp in local:
            cp.wait()

    hbm = lambda a: pltpu.HBM(a.shape, a.dtype)
    res = pl.pallas_call(
        body, name=name,
        out_shape=(*[hbm(a) for a in srcs], *[hbm(a) for a in lands]),
        in_specs=[_HBM] * (2 * n) + [_SEM] * 3 + [pl.BlockSpec(memory_space=pl.ANY)] * len(after),
        out_specs=tuple([_HBM] * (2 * n)),
        input_output_aliases={i: i for i in range(2 * n)},
        compiler_params=pltpu.CompilerParams(has_side_effects=pltpu.SideEffectType.DATAFLOW_SIDE_EFFECTING),
    )(*srcs, *lands, *sems, *after)
    return list(res[n:])


def _forward_to_sibling(slots, name):
    n = len(slots)
    hops = (2, 4, 6)

    def body(*refs):
        ins, outs, (send_sems, recv_sems) = refs[:n], refs[n:2 * n], refs[2 * n:]
        x, y, c = (lax.axis_index(a) for a in MESH_AXES)
        copies = []
        for a in range(n):
            for i, k in enumerate(hops):
                slot = 4 * (1 - x if k & 4 else x) + 2 * (1 - y if k & 2 else y) + c
                copies.append(pltpu.make_async_remote_copy(
                    src_ref=ins[a].at[slot], dst_ref=outs[a].at[slot],
                    send_sem=send_sems.at[a * len(hops) + i], recv_sem=recv_sems.at[a * len(hops) + i],
                    device_id=(x, y, 1 - c), device_id_type=pl.DeviceIdType.MESH))
        for cp in copies:
            cp.start()
        for cp in copies:
            cp.wait_send()
            cp.wait_recv()

    return pl.pallas_call(
        body, name=name, out_shape=[jax.ShapeDtypeStruct(s.shape, s.dtype) for s in slots],
        in_specs=[pl.BlockSpec(memory_space=pl.ANY)] * n, out_specs=[pl.BlockSpec(memory_space=pl.ANY)] * n,
        input_output_aliases={i: i for i in range(n)},
        scratch_shapes=[pltpu.SemaphoreType.DMA((n * len(hops),)), pltpu.SemaphoreType.DMA((n * len(hops),))],
    )(*slots)


def _adamw_math(w, g, m, v):
    m = ADAM_B1 * m + (1.0 - ADAM_B1) * g
    v = ADAM_B2 * v + (1.0 - ADAM_B2) * (g * g)
    m_hat = m / (1.0 - ADAM_B1 ** ADAM_STEP)
    v_hat = v / (1.0 - ADAM_B2 ** ADAM_STEP)
    return -ADAM_LR * (m_hat / (jnp.sqrt(v_hat) + ADAM_EPS) + ADAM_WD * w), m, v


def _slot_sum(p_ref):
    g = p_ref[0].astype(F32)
    for d in range(1, p_ref.shape[0]):
        g = g + p_ref[d].astype(F32)
    return g


def _adamw_tile(r, c):
    return _tile(r, max(SUBLANES, (256 * 1024) // c // SUBLANES * SUBLANES), SUBLANES)


def _adamw(parts, w, m, v, name):
    r, c = w.shape[-2:]
    by_cols = r % SUBLANES != 0
    tr, tc = (r, _tile(c, 256)) if by_cols else (_adamw_tile(r, c), c)

    def body(p_ref, w_ref, m_ref, v_ref, g_ref, d_ref, nm_ref, nv_ref):
        g = _slot_sum(p_ref)
        g_ref[...] = g
        d_ref[...], nm_ref[...], nv_ref[...] = _adamw_math(w_ref[...], g, m_ref[...], v_ref[...])

    pos = (lambda i: (0, i)) if by_cols else (lambda i: (i, 0))
    if w.ndim == 3:
        blk = pl.BlockSpec((None, tr, tc), lambda i: (0,) + pos(i))
    else:
        blk = pl.BlockSpec((tr, tc), pos)
    sh = jax.ShapeDtypeStruct(w.shape, F32)
    return pl.pallas_call(
        body, name=name, grid=(c // tc if by_cols else r // tr,),
        in_specs=[pl.BlockSpec((parts.shape[0], tr, tc), lambda i: (0,) + pos(i)), blk, blk, blk],
        out_specs=[blk] * 4, out_shape=[sh] * 4, compiler_params=_cparams("parallel"),
    )(parts, w, m, v)


def _sum_parts(parts, name):
    _, r, c = parts.shape
    tr = _adamw_tile(r, c)

    def body(p_ref, o_ref):
        o_ref[...] = _slot_sum(p_ref)

    return pl.pallas_call(
        body, name=name, grid=(r // tr,),
        in_specs=[pl.BlockSpec((parts.shape[0], tr, c), lambda i: (0, i, 0))],
        out_specs=pl.BlockSpec((tr, c), lambda i: (i, 0)), out_shape=jax.ShapeDtypeStruct((r, c), F32),
        compiler_params=_cparams("parallel"),
    )(parts)


def _lane_pad(a, width=LANES):
    return jnp.pad(a, ((0, 0), (0, width - a.shape[1])))


def _local_step(x, target, norm_pre, norm_post, kv_norm, kv_b_f, a_re, a_im, log_dt, b_re, b_im, c_re, c_im, comm):
    s, d = x.shape
    g, p = a_re.shape
    w = g * S5_GROUP
    fw = d
    nh = fw // HEAD_DIM
    seg_len = s // N_SEG
    row = lambda v: v.reshape(1, -1)
    g_pre0, g_pre1, g_post0, g_post1, g_kv = row(norm_pre[0]), row(norm_pre[1]), row(norm_post[0]), row(norm_post[1]), row(kv_norm)

    ldt = log_dt.reshape(g, 1)
    abr, abi, cr, ci = _s5_disc_fwd(a_re, a_im, ldt)
    cr_col, ci_col = cr.reshape(g * p, 1), ci.reshape(g * p, 1)
    b_re2, b_im2 = b_re.reshape(g * p, S5_GROUP), b_im.reshape(g * p, S5_GROUP)
    bb_re, bb_im = _s5_bbar_fwd(cr_col, ci_col, b_re2, b_im2)
    bd_re = _block_diag(bb_re.reshape(g, p, S5_GROUP)).astype(BF16)
    bd_im = _block_diag(bb_im.reshape(g, p, S5_GROUP)).astype(BF16)
    cd_re = _block_diag(c_re).astype(BF16)
    cd_im = _block_diag(-c_im).astype(BF16)
    ab_re = jnp.broadcast_to(abr.reshape(1, g * p), (N_SEG, g * p))
    ab_im = jnp.broadcast_to(abi.reshape(1, g * p), (N_SEG, g * p))
    zero_seg = jnp.zeros((N_SEG, g * p), F32)

    xn0 = _norm_cast(x, g_pre0 + comm.token, "norm_pre0", x_kind="nat")
    w_in = comm.weight("s5_w_in", [xn0, bd_re, bd_im, cd_re, cd_im, ab_re, ab_im])
    d_row, bglu_row = row(comm.vector("s5_d")), row(comm.vector("s5_b_glu"))
    u = _mm(xn0, w_in, "nn", BF16, "s5_in_u", b_cols=(0, w), b_slots=True)
    z0 = _mm(xn0, w_in, "nn", BF16, "s5_in_z", b_cols=(w, w), b_slots=True)
    e_re, e_im = _s5_scan_fwd(u, bd_re, bd_im, cd_re, cd_im, ab_re, ab_im, zero_seg, zero_seg, d_row, False, "s5_scan_ends")
    i_re, i_im = _s5_seg_fix(e_re, e_im, ab_re, ab_im, seg_len, False, "s5_seg_fix")
    y_ssm, yg, h_re, h_im, _, _ = _s5_scan_fwd(u, bd_re, bd_im, cd_re, cd_im, ab_re, ab_im, i_re, i_im, d_row, True, "s5_scan")
    w_glu, w_out = comm.weight("s5_w_glu", yg), comm.weight("s5_w_out", yg)
    gp = _mm(yg, w_glu, "nn", BF16, "s5_glu")
    y3 = _s5_gate(y_ssm, gp, bglu_row, z0, "s5_gate")
    w_kvt, fw_in = comm.weight("kv_w", y3), comm.weight("fox_w_in", y3)
    w_ft = jnp.pad(w_kvt[2 * fw:], ((0, LANES - nh), (0, 0)))
    o0 = _mm(y3, w_out, "nn", BF16, "s5_out")

    h1, hn_kv, xn1 = _resid_norm2(x, o0, g_post0 + comm.late_token, g_kv, g_pre1, "resid_norms")
    kv = _mm(hn_kv, w_kvt, "nt", BF16, "kv_proj", b_rows=2 * fw)
    f_logit = _mm(hn_kv, w_ft, "nt", F32, "f_proj")
    bf_row = _lane_pad(row(kv_b_f))
    cum2 = _cum_fwd(f_logit, bf_row, "cum_fwd")
    cum2_t = cum2[:, :nh].T.reshape(nh, 1, s)
    q2 = _mm(xn1, fw_in, "nn", BF16, "fox_q", scale=HEAD_DIM ** -0.5 * LOG2E, b_cols=(0, fw), b_slots=True)
    z1 = _mm(xn1, fw_in, "nn", BF16, "fox_z", b_cols=(fw, fw), b_slots=True)
    o, oz, lse2_t = _fox_fwd(q2, kv, cum2_t, z1, "fox_fwd")
    fw_out = comm.weight("fox_w_out", oz)
    o1 = _mm(oz, fw_out, "nn", BF16, "fox_out")
    dh2, do1, sq, dg_post1 = _post_norm_loss(o1, g_post1, h1, target, "norm_post1_loss")
    loss = 0.5 * jnp.sum(sq) / d

    d_fw_out = _mm(oz, do1, "tn", BF16, "fox_out_dw")
    d_oz = _mm(do1, fw_out, "nt", BF16, "fox_out_dx")
    do, dqz = _gate_bwd(d_oz, o, z1, "fox_gate_bwd")
    dk, dv, dqz, dcq, dck = _fox_bwd(q2, kv, do, o, lse2_t, cum2, dqz, "fox_bwd")
    d_fw_in = _mm(xn1, dqz, "tn", BF16, "fox_in_dw", col_slots=True)
    dxn1 = _mm(dqz, fw_in, "nt", BF16, "fox_in_dx", b_slots=True)
    dcq_sl = _lane_pad(dcq.reshape(nh, s).T)
    dck_sl = _lane_pad(dck.reshape(nh, s).T)
    df, db_f = _cum_bwd(dcq_sl, dck_sl, f_logit, bf_row, "cum_bwd")
    dkv = _concat_cast(dk, dv, "fox_dkv")
    d_w_kvmt = _mm(dkv, hn_kv, "tn", BF16, "kv_dw")
    d_w_ft = _mm(df, hn_kv, "tn", BF16, "f_dw")
    dhn_f = _mm(df, w_ft, "nn", BF16, "f_dx")
    dhn_kv = _mm(dkv, w_kvt, "nn", BF16, "kv_dx", add=dhn_f, b_rows=2 * fw)
    d_w_kvt = jnp.concatenate([d_w_kvmt, d_w_ft[:nh]], axis=0)
    tok = comm.send_grads(dict(fox_w_out=d_fw_out, fox_w_in=d_fw_in, kv_w=d_w_kvt), "exchange_fox")
    dh1, do0, dg_pre1, dg_kv, dg_post0 = _norm_bwd2(dh2, h1, dxn1, dhn_kv, g_pre1, g_kv, o0, g_post0 + tok[0, 0],
                                                      "resid_norms_bwd")

    d_w_out = _mm(y3, do0, "tn", BF16, "s5_out_dw")
    dy3 = _mm(do0, w_out, "nt", BF16, "s5_out_dx")
    duz, dgp, dyg_direct, db_glu = _s5_gate_bwd(dy3, y_ssm, gp, bglu_row, z0, "s5_gate_bwd")
    d_w_glu = _mm(yg, dgp, "tn", BF16, "s5_glu_dw")
    gelu_bwd = lambda dyg, y: jax.vjp(jax.nn.gelu, y.astype(F32))[1](dyg)[0]
    dy_ssm = _mm(dgp, w_glu, "nt", BF16, "s5_glu_dx", add=dyg_direct, epilogue=(gelu_bwd, y_ssm))
    d_row = d_row + comm.send_grads(dict(s5_w_out=d_w_out, s5_w_glu=d_w_glu), "exchange_s5")[0, 0]
    ab_imn = -ab_im
    ge_re, ge_im = _s5_scan_bwd(dy_ssm, u, h_re, h_im, bd_re, bd_im, cd_re, cd_im, ab_re, ab_imn, zero_seg, zero_seg,
                                d_row, False, "s5_adj_ends")
    gi_re, gi_im = _s5_seg_fix(ge_re, ge_im, ab_re, ab_imn, seg_len, True, "s5_adj_fix")
    duz, dbd_re, dbd_im, dcd_re, dcd_im, dab_re, dab_im, dd = _s5_scan_bwd(
        dy_ssm, u, h_re, h_im, bd_re, bd_im, cd_re, cd_im, ab_re, ab_imn, gi_re, gi_im, d_row, True, "s5_adj", duz=duz)
    d_w_in = _mm(xn0, duz, "tn", BF16, "s5_in_dw", col_slots=True)
    tok = comm.send_grads(dict(s5_w_in=d_w_in), "exchange_s5_in")
    dxn0 = _mm(duz, w_in, "nt", BF16, "s5_in_dx", after=tok, b_slots=True)
    grad_x, dg_pre0 = _norm_bwd1(dh1, x, dxn0, g_pre0, "norm_pre0_bwd")

    dbb_re = _block_diag_extract(dbd_re, p, S5_GROUP).reshape(g * p, S5_GROUP)
    dbb_im = _block_diag_extract(dbd_im, p, S5_GROUP).reshape(g * p, S5_GROUP)
    dcr_col, dci_col, db_re, db_im = _s5_bbar_bwd(cr_col, ci_col, b_re2, b_im2, dbb_re, dbb_im)
    da_re, da_im, dldt = _s5_disc_bwd(a_re, a_im, ldt, dab_re.reshape(g, p), dab_im.reshape(g, p),
                                      dcr_col.reshape(g, p), dci_col.reshape(g, p))
    dc_re = _block_diag_extract(dcd_re, S5_GROUP, p)
    dc_im = -_block_diag_extract(dcd_im, S5_GROUP, p)

    small = dict(
        norm_pre=jnp.concatenate([dg_pre0, dg_pre1], axis=0), norm_post=jnp.concatenate([dg_post0, dg_post1], axis=0),
        s5_a_re=da_re, s5_a_im=da_im, s5_log_dt=dldt.reshape(g), s5_b_re=db_re.reshape(g, p, S5_GROUP),
        s5_b_im=db_im.reshape(g, p, S5_GROUP), s5_c_re=dc_re, s5_c_im=dc_im, s5_d=dd.reshape(-1),
        s5_b_glu=db_glu.reshape(-1), kv_norm=dg_kv.reshape(-1), kv_b_f=db_f[0, :nh])
    return loss, grad_x, small


_BIG = ("s5_w_in", "s5_w_glu", "s5_w_out", "kv_w", "fox_w_in", "fox_w_out")
_COL_SHARDED = ("s5_w_in", "fox_w_in")
_SMALL = ("norm_pre", "norm_post", "s5_a_re", "s5_a_im", "s5_log_dt", "s5_b_re", "s5_b_im", "s5_c_re", "s5_c_im",
          "s5_d", "s5_b_glu", "kv_norm", "kv_b_f")
_SMALL_SHARDED = ("s5_d", "s5_b_glu")
_PACK_QUANTUM = SUBLANES * LANES
_WEIGHTS = ('norm_pre', 'norm_post', 's5_w_in', 's5_a_re', 's5_a_im', 's5_log_dt', 's5_b_re', 's5_b_im', 's5_c_re', 's5_c_im',
            's5_d', 's5_w_glu', 's5_b_glu', 's5_w_out', 'kv_norm', 'kv_w', 'kv_b_f', 'fox_w_in', 'fox_w_out')


def _full_from_slots(name, slots):
    n, r, c = slots.shape
    if name in _COL_SHARDED:
        return slots.transpose(1, 0, 2).reshape(r, n * c)
    return slots.reshape(n * r, c)


def _slots_from_full(name, full):
    if name in _COL_SHARDED:
        r, nc = full.shape
        return full.reshape(r, N_DEV, nc // N_DEV).transpose(1, 0, 2)
    nr, c = full.shape
    return full.reshape(N_DEV, nr // N_DEV, c)


def _groups_last(shape):
    return len(shape) >= 3 and shape[-1] < LANES and shape[-3] % LANES == 0


def _pack(vals):
    parts = []
    for v in vals:
        flat = jnp.moveaxis(v, -3, -1).reshape(-1) if _groups_last(v.shape) else v.reshape(-1)
        parts.append(jnp.pad(flat, (0, (-flat.shape[0]) % _PACK_QUANTUM)))
    total = sum(p.shape[0] for p in parts)
    parts.append(jnp.zeros(((-total) % (N_DEV * _PACK_QUANTUM),), F32))
    return jnp.concatenate(parts).reshape(-1, LANES)


def _unpack(packed, shapes):
    flat = packed.reshape(-1)
    out, off = [], 0
    for sh in shapes:
        n = math.prod(sh)
        piece = flat[off:off + n]
        if _groups_last(sh):
            piece = jnp.moveaxis(piece.reshape(sh[:-3] + sh[-2:] + sh[-3:-2]), -1, -3)
        out.append(piece.reshape(sh))
        off += n + (-n) % _PACK_QUANTUM
    return out


class _Comm:
    _GROUPS = (("s5_w_in",) + _SMALL_SHARDED, ("s5_w_glu", "s5_w_out"), ("kv_w", "fox_w_in"), ("fox_w_out",))
    _SLOT_FORM = ("s5_w_in", "fox_w_in")

    def __init__(self, shards, vectors, early=()):
        self._shards = {**shards, **vectors}
        self._full, self._gathers = {}, {}
        self._early = list(early)
        self.token = jnp.zeros((), F32)
        for group in self._GROUPS[:-1]:
            self.token = self.token + self._start(group, ())[0, 0]
        self.late_token = None
        self._sent = []

    def _start(self, group, after):
        state, tok = _exchange_start([self._shards[n] for n in group], False, "gather_start_" + group[0], after,
                                     peers=_CHIP_PEERS)
        self._gathers[group] = state
        return tok

    def vector(self, name):
        return self._full[name]

    def weight(self, name, after):
        if name not in self._full:
            group = next(g for g in self._GROUPS if name in g)
            if group == self._GROUPS[0]:
                after = (list(after) if isinstance(after, (list, tuple)) else [after]) + self._early
            slots = _exchange_wait(self._gathers.pop(group), after, "gather_wait_" + group[0])
            slots = _forward_to_sibling(slots, "gather_forward_" + group[0])
            for n, sl in zip(group, slots):
                if n in _SMALL_SHARDED:
                    self._full[n] = sl.reshape(-1)
                else:
                    self._full[n] = sl if n in self._SLOT_FORM else _full_from_slots(n, sl)
            if group == self._GROUPS[-2]:
                self.late_token = self._start(self._GROUPS[-1], [slots[0]])[0, 0]
        return self._full[name]

    def send_grads(self, grads, name):
        names = list(grads)
        slots = [grads[n] if grads[n].ndim == 3 else _slots_from_full(n, grads[n]).astype(BF16) for n in names]
        state, tok = _exchange_start(slots, True, name + "_start")
        self._sent.append((names, state, name + "_wait"))
        return tok

    def received_grads(self, group, after):
        names, state, name = self._sent[group]
        return list(zip(names, _exchange_wait(state, after, name)))


def kernel(x, norm_pre, norm_post, s5_w_in, s5_a_re, s5_a_im, s5_log_dt, s5_b_re, s5_b_im, s5_c_re, s5_c_im, s5_d, s5_w_glu, s5_b_glu, s5_w_out, kv_norm, kv_w, kv_b_f, fox_w_in, fox_w_out, loss_target, m_norm_pre, m_norm_post, m_s5_w_in, m_s5_a_re, m_s5_a_im, m_s5_log_dt, m_s5_b_re, m_s5_b_im, m_s5_c_re, m_s5_c_im, m_s5_d, m_s5_w_glu, m_s5_b_glu, m_s5_w_out, m_kv_norm, m_kv_w, m_kv_b_f, m_fox_w_in, m_fox_w_out, v_norm_pre, v_norm_post, v_s5_w_in, v_s5_a_re, v_s5_a_im, v_s5_log_dt, v_s5_b_re, v_s5_b_im, v_s5_c_re, v_s5_c_im, v_s5_d, v_s5_w_glu, v_s5_b_glu, v_s5_w_out, v_kv_norm, v_kv_w, v_kv_b_f, v_fox_w_in, v_fox_w_out):
    env = dict(locals())
    wts = {n: env[n] for n in _WEIGHTS}
    mom = {n: env["m_" + n] for n in _WEIGHTS}
    var = {n: env["v_" + n] for n in _WEIGHTS}
    me = 4 * lax.axis_index("x") + 2 * lax.axis_index("y") + lax.axis_index("c")
    shard2d = {n: (wts[n].T if n == "kv_w" else wts[n].reshape(wts[n].shape[-2:])) for n in _BIG}
    full_shape = {n: ((wts[n].size * N_DEV,) if n in _SMALL_SHARDED else wts[n].shape) for n in _SMALL}

    def spread(n, v):
        if n not in _SMALL_SHARDED:
            return v
        flat = v.reshape(-1)
        return lax.dynamic_update_slice(jnp.zeros(full_shape[n], F32), flat, (me * flat.shape[0],))

    packed = [_pack([spread(n, src[n]) for n in _SMALL] + [jnp.zeros((1,), F32)]) for src in (wts, mom, var)]
    comm = _Comm({n: _cast_bf16(shard2d[n], "cast_" + n) for n in _BIG}, {n: wts[n].reshape(1, -1) for n in _SMALL_SHARDED}, packed)

    loss_local, grad_x, small = _local_step(
        x[0], loss_target[0], norm_pre, norm_post, kv_norm, kv_b_f, s5_a_re[0], s5_a_im[0], s5_log_dt[0],
        s5_b_re[0], s5_b_im[0], s5_c_re[0], s5_c_im[0], comm)

    small_pack = _pack([small[n] for n in _SMALL] + [loss_local.reshape(1)])
    slice_rows = small_pack.shape[0] // N_DEV
    small_state, small_tok = _exchange_start([small_pack.reshape(N_DEV, slice_rows, LANES)], True, "reduce_small_start")

    res = {}

    def finish(group, after):
        for n, recv in comm.received_grads(group, after):
            if n == "kv_w":
                res[n] = [o.T for o in _adamw(recv, wts[n].T, mom[n].T, var[n].T, "adamw_" + n)]
            else:
                res[n] = _adamw(recv, wts[n], mom[n], var[n], "adamw_" + n)

    finish(0, [small_tok, grad_x])
    my_sum = _sum_parts(_exchange_wait(small_state, res["kv_w"][0], "reduce_small_wait")[0], "sum_small")
    gather_state, gather_tok = _exchange_start([my_sum], False, "gather_small_start")
    finish(1, gather_tok)
    finish(2, gather_tok)
    g_all = _exchange_wait(gather_state, res["s5_w_in"][0], "gather_small_wait")[0].reshape(1, small_pack.shape[0], LANES)
    outs = _adamw(g_all, *packed, "adamw_small")
    unpacked = [_unpack(o, [full_shape[n] for n in _SMALL] + [(1,)]) for o in outs]
    loss = unpacked[0][-1][0]
    for i, n in enumerate(_SMALL):
        vals = [u[i] for u in unpacked]
        if n in _SMALL_SHARDED:
            k = wts[n].size
            vals = [lax.dynamic_slice(v, (me * k,), (k,)) for v in vals]
        res[n] = [v.reshape(wts[n].shape) for v in vals]

    return (loss, grad_x[None], *[res[n][0] for n in _WEIGHTS], *[res[n][1] for n in _WEIGHTS],
            *[res[n][2] for n in _WEIGHTS], *[res[n][3] for n in _WEIGHTS])
```

```python
import math

import jax
import jax.numpy as jnp
from jax import lax
from jax.experimental import pallas as pl
from jax.experimental.pallas import tpu as pltpu

F32 = jnp.float32
BF16 = jnp.bfloat16

N_DEV = 8
MESH_AXES = ("x", "y", "c")
S5_GROUP = 16
S5_STATE = 64
LANES = 128
SUBLANES = 8
GROUPS_PER_BLOCK = LANES // S5_GROUP
BLOCK_STATE = GROUPS_PER_BLOCK * S5_STATE
N_SEG = SUBLANES
HEAD_DIM = 128
RMS_EPS = 1e-6
NEG_INF = -1e30
LOG2E = math.log2(math.e)
ADAM_LR = 0.001
ADAM_B1 = 0.9
ADAM_B2 = 0.999
ADAM_EPS = 1e-08
ADAM_WD = 0.01
ADAM_STEP = 10
VMEM_LIMIT = 56 * 1024 * 1024


def _tile(n, pref, quantum=LANES):
    if n <= pref:
        return n
    t = (pref // quantum) * quantum
    while t >= quantum:
        if n % t == 0:
            return t
        t -= quantum
    return n


def _cparams(*sem):
    return pltpu.CompilerParams(dimension_semantics=sem if sem else None, vmem_limit_bytes=VMEM_LIMIT)


_DOT_DIMS = {"nn": ((1,), (0,)), "nt": ((1,), (1,)), "tn": ((0,), (0,))}


def _mm(a, b, mode, out_dtype, name, add=None, scale=None, b_cols=None, after=None, col_slots=False, b_slots=False,
        b_rows=None, epilogue=None):
    slot_w = b.shape[2] if b_slots else None
    b2d = (b.shape[1], b.shape[0] * b.shape[2]) if b_slots else b.shape
    b_shape = b2d if b_cols is None else (b2d[0], b_cols[1])
    if b_rows is not None:
        b_shape = (b_rows, b_shape[1])
    if mode == "nn":
        (M, K), (K2, N) = a.shape, b_shape
    elif mode == "nt":
        (M, K), (N, K2) = a.shape, b_shape
    else:
        (K, M), (K2, N) = a.shape, b_shape
    assert K == K2, (name, a.shape, b_shape)
    tm, tn, tk = _tile(M, 1024 if K <= 2048 else 512), (N // N_DEV if col_slots else _tile(N, 1024)), _tile(K, 4096)
    if b_slots and mode == "nn":
        tn = slot_w
    nk = K // tk
    dims = (_DOT_DIMS[mode], ((), ()))
    col0 = 0
    if b_cols is not None:
        assert mode != "tn" and b_cols[0] % (tn if mode == "nn" else tk) == 0
        col0 = b_cols[0] // (tn if mode == "nn" else tk)
    assert not b_slots or (mode == "nn" or (mode == "nt" and nk == 1 and b_cols is None))

    def body(*refs):
        a_ref, b_ref = refs[:2]
        c_ref = refs[2] if add is not None else None
        e_ref = refs[2 + (add is not None)] if epilogue is not None else None
        o_ref = refs[2 + (add is not None) + (epilogue is not None) + (after is not None)]
        if b_slots and mode == "nt":
            part = lax.dot_general(a_ref[:, :slot_w], b_ref[0], dims, preferred_element_type=F32)
            for sl in range(1, b_ref.shape[0]):
                part += lax.dot_general(a_ref[:, sl * slot_w:(sl + 1) * slot_w], b_ref[sl], dims, preferred_element_type=F32)
        else:
            part = lax.dot_general(a_ref[...], b_ref[...], dims, preferred_element_type=F32)

        def finish(r):
            if scale is not None:
                r = r * scale
            if add is not None:
                r = r + c_ref[...]
            if epilogue is not None:
                r = epilogue[0](r, e_ref[...])
            o_ref[...] = r.astype(out_dtype)

        if nk == 1:
            finish(part)
            return
        acc = refs[-1]
        k = pl.program_id(2)

        @pl.when(k == 0)
        def _():
            acc[...] = part

        @pl.when(jnp.logical_and(k > 0, k < nk - 1))
        def _():
            acc[...] += part

        @pl.when(k == nk - 1)
        def _():
            finish(acc[...] + part)

    if mode == "tn":
        a_spec = pl.BlockSpec((tk, tm), lambda i, j, k: (k, i))
    else:
        a_spec = pl.BlockSpec((tm, tk), lambda i, j, k: (i, k))
    if b_slots and mode == "nn":
        b_spec = pl.BlockSpec((None, tk, tn), lambda i, j, k: (j + col0, k, 0))
    elif b_slots:
        b_spec = pl.BlockSpec((b.shape[0], tn, slot_w), lambda i, j, k: (0, j, 0))
    elif mode == "nt":
        b_spec = pl.BlockSpec((tn, tk), lambda i, j, k: (j, k + col0))
    else:
        b_spec = pl.BlockSpec((tk, tn), lambda i, j, k: (k, j + col0))
    o_spec = pl.BlockSpec((tm, tn), lambda i, j, k: (i, j))
    in_specs = [a_spec, b_spec] + ([o_spec] if add is not None else [])
    args = (a, b) + ((add,) if add is not None else ())
    if epilogue is not None:
        in_specs.append(o_spec)
        args += (epilogue[1],)
    if after is not None:
        in_specs.append(pl.BlockSpec(after.shape, lambda i, j, k: (0, 0)))
        args += (after,)
    out_shape = jax.ShapeDtypeStruct((M, N), out_dtype)
    if col_slots:
        assert add is None
        o_spec = pl.BlockSpec((None, tm, tn), lambda i, j, k: (j, i, 0))
        out_shape = jax.ShapeDtypeStruct((N_DEV, M, tn), out_dtype)
    return pl.pallas_call(
        body, name=name, grid=(M // tm, N // tn, nk),
        in_specs=in_specs, out_specs=o_spec,
        out_shape=out_shape,
        scratch_shapes=[pltpu.VMEM((tm, tn), F32)] if nk > 1 else [],
        compiler_params=_cparams("parallel", "parallel", "arbitrary"),
    )(*args)


class _NatIn:
    def __init__(self, ref):
        self.ref = ref

    def __getitem__(self, idx):
        v = jnp.swapaxes(self.ref[...], 0, 1)
        return v.reshape(v.shape[0] * N_SEG, v.shape[2])


class _NatOut:
    def __init__(self, ref):
        self.ref = ref

    def __setitem__(self, idx, val):
        self.ref[...] = jnp.swapaxes(val.reshape(val.shape[0] // N_SEG, N_SEG, val.shape[1]), 0, 1)


def _rowcall(body, name, n_rows, ins, outs, tile_rows=256):
    tr = _tile(n_rows, tile_rows, SUBLANES * 2)
    n_in = len(ins)
    in_kinds = [k for _, k in ins]
    kinds = [k for _, _, k in outs]

    def kern(*refs):
        @pl.when(pl.program_id(0) == 0)
        def _():
            for r, kind in zip(refs[n_in:], kinds):
                if kind == "acc":
                    r[...] = jnp.zeros_like(r)

        wrapped = [_NatIn(r) if k == "nat" else r for r, k in zip(refs[:n_in], in_kinds)]
        wrapped += [_NatOut(r) if k == "nat" else r for r, k in zip(refs[n_in:], kinds)]
        body(*wrapped)

    in_specs, args = [], []
    for arr, kind in ins:
        if kind == "row":
            in_specs.append(pl.BlockSpec((tr, arr.shape[1]), lambda i: (i, 0)))
        elif kind == "nat":
            in_specs.append(pl.BlockSpec((N_SEG, tr // N_SEG, arr.shape[1]), lambda i: (0, i, 0)))
            arr = arr.reshape(N_SEG, n_rows // N_SEG, arr.shape[1])
        else:
            in_specs.append(pl.BlockSpec(arr.shape, lambda i, nd=arr.ndim: (0,) * nd))
        args.append(arr)
    out_specs, out_shape = [], []
    for width, dtype, kind in outs:
        if kind == "row":
            out_specs.append(pl.BlockSpec((tr, width), lambda i: (i, 0)))
            out_shape.append(jax.ShapeDtypeStruct((n_rows, width), dtype))
        elif kind == "right":
            out_specs.append(pl.BlockSpec((tr, width), lambda i: (i, 1)))
            out_shape.append(jax.ShapeDtypeStruct((n_rows, 2 * width), dtype))
        elif kind == "nat":
            out_specs.append(pl.BlockSpec((N_SEG, tr // N_SEG, width), lambda i: (0, i, 0)))
            out_shape.append(jax.ShapeDtypeStruct((N_SEG, n_rows // N_SEG, width), dtype))
        else:
            out_specs.append(pl.BlockSpec((1, width), lambda i: (0, 0)))
            out_shape.append(jax.ShapeDtypeStruct((1, width), F32))
    res = pl.pallas_call(
        kern, name=name, grid=(n_rows // tr,), in_specs=in_specs, out_specs=out_specs, out_shape=out_shape,
        compiler_params=_cparams("arbitrary"),
    )(*args)
    return [r.reshape(n_rows, r.shape[2]) if k == "nat" else r for r, k in zip(res, kinds)]


def _rstd(x):
    return lax.rsqrt(jnp.mean(x * x, axis=-1, keepdims=True) + RMS_EPS)


def _rms_bwd(x, g, dy):
    xh = x * _rstd(x)
    dxh = dy * g
    dx = _rstd(x) * (dxh - xh * jnp.mean(dxh * xh, axis=-1, keepdims=True))
    return dx, jnp.sum(dy * xh, axis=0, keepdims=True)


def _silu(z):
    return z * jax.nn.sigmoid(z)


def _norm_cast(x, g, name, x_kind="row"):
    def body(x_ref, g_ref, o_ref):
        x = x_ref[...]
        o_ref[...] = (x * _rstd(x) * g_ref[...]).astype(BF16)

    return _rowcall(body, name, x.shape[0], [(x, x_kind), (g, "full")], [(x.shape[1], BF16, "row")])[0]


def _resid_norm2(x, o, g_post, g_kv, g_pre, name):
    def body(x_ref, o_ref, go_ref, gk_ref, gp_ref, h_ref, nk_ref, np_ref):
        o = o_ref[...]
        h = x_ref[...] + o * _rstd(o) * go_ref[...]
        h_ref[...] = h
        hn = h * _rstd(h)
        nk_ref[...] = (hn * gk_ref[...]).astype(BF16)
        np_ref[...] = (hn * gp_ref[...]).astype(BF16)

    d = x.shape[1]
    return _rowcall(body, name, x.shape[0], [(x, "nat"), (o, "row"), (g_post, "full"), (g_kv, "full"), (g_pre, "full")],
                    [(d, F32, "nat"), (d, BF16, "nat"), (d, BF16, "nat")])


def _post_norm_loss(o, g, h1, target, name):
    d = o.shape[1]

    def body(o_ref, g_ref, h_ref, t_ref, dh_ref, do_ref, acc_ref, dg_ref):
        o = o_ref[...]
        e = h_ref[...] + o * _rstd(o) * g_ref[...] - t_ref[...]
        dh = e * (1.0 / d)
        dh_ref[...] = dh
        acc_ref[...] += jnp.sum(e * e, axis=0, keepdims=True)
        dx, dg = _rms_bwd(o, g_ref[...], dh)
        do_ref[...] = dx.astype(BF16)
        dg_ref[...] += dg

    return _rowcall(body, name, o.shape[0], [(o, "row"), (g, "full"), (h1, "row"), (target, "row")],
                    [(d, F32, "row"), (d, BF16, "row"), (d, F32, "acc"), (d, F32, "acc")])


def _gate_bwd(d_oz, o, z, name):
    def body(d_ref, o_ref, z_ref, do_ref, dz_ref):
        _, vjp = jax.vjp(lambda o, z: o * _silu(z), o_ref[...].astype(F32), z_ref[...].astype(F32))
        do, dz = vjp(d_ref[...].astype(F32))
        do_ref[...] = do.astype(BF16)
        dz_ref[...] = dz.astype(BF16)

    w = o.shape[1]
    return _rowcall(body, name, o.shape[0], [(d_oz, "row"), (o, "row"), (z, "row")], [(w, BF16, "row"), (w, BF16, "right")])


def _norm_bwd2(dh2, h1, dxn1, dhn_kv, g_pre, g_kv, o0, g_post0, name):
    def body(dh2_ref, h_ref, d1_ref, dk_ref, gp_ref, gk_ref, o_ref, go_ref, dh1_ref, do_ref, dgp_ref, dgk_ref, dgo_ref):
        h = h_ref[...]
        dx1, dg1 = _rms_bwd(h, gp_ref[...], d1_ref[...].astype(F32))
        dxk, dgk = _rms_bwd(h, gk_ref[...], dk_ref[...].astype(F32))
        dh1 = dh2_ref[...] + dx1 + dxk
        dh1_ref[...] = dh1
        dgp_ref[...] += dg1
        dgk_ref[...] += dgk
        dxo, dgo = _rms_bwd(o_ref[...], go_ref[...], dh1)
        do_ref[...] = dxo.astype(BF16)
        dgo_ref[...] += dgo

    d = h1.shape[1]
    return _rowcall(body, name, h1.shape[0],
                    [(dh2, "nat"), (h1, "nat"), (dxn1, "nat"), (dhn_kv, "nat"), (g_pre, "full"), (g_kv, "full"),
                     (o0, "row"), (g_post0, "full")],
                    [(d, F32, "nat"), (d, BF16, "row"), (d, F32, "acc"), (d, F32, "acc"), (d, F32, "acc")])


def _norm_bwd1(dres, x, dxn, g, name):
    def body(dr_ref, x_ref, dn_ref, g_ref, dx_ref, dg_ref):
        dx, dg = _rms_bwd(x_ref[...], g_ref[...], dn_ref[...].astype(F32))
        dx_ref[...] = dr_ref[...] + dx
        dg_ref[...] += dg

    d = x.shape[1]
    return _rowcall(body, name, x.shape[0], [(dres, "nat"), (x, "nat"), (dxn, "row"), (g, "full")],
                    [(d, F32, "nat"), (d, F32, "acc")])


def _s5_gate(y_ssm, gp, b_glu, z, name):
    def body(y_ref, gp_ref, b_ref, z_ref, o_ref):
        yg = jax.nn.gelu(y_ref[...].astype(F32))
        o_ref[...] = (yg * jax.nn.sigmoid(gp_ref[...] + b_ref[...]) * _silu(z_ref[...].astype(F32))).astype(BF16)

    return _rowcall(body, name, y_ssm.shape[0], [(y_ssm, "row"), (gp, "row"), (b_glu, "full"), (z, "row")],
                    [(y_ssm.shape[1], BF16, "row")])[0]


def _s5_gate_bwd(dy3, y_ssm, gp, b_glu, z, name):
    def body(d_ref, y_ref, gp_ref, b_ref, z_ref, dz_ref, dgp_ref, dyg_ref, db_ref):
        yg = jax.nn.gelu(y_ref[...].astype(F32))
        _, vjp = jax.vjp(lambda yg, gp, z: yg * jax.nn.sigmoid(gp) * _silu(z), yg, gp_ref[...] + b_ref[...],
                         z_ref[...].astype(F32))
        dyg, dgp, dz = vjp(d_ref[...].astype(F32))
        dz_ref[...] = dz.astype(BF16)
        dgp_ref[...] = dgp.astype(BF16)
        dyg_ref[...] = dyg.astype(BF16)
        db_ref[...] += jnp.sum(dgp, axis=0, keepdims=True)

    w = y_ssm.shape[1]
    return _rowcall(body, name, y_ssm.shape[0],
                    [(dy3, "row"), (y_ssm, "row"), (gp, "row"), (b_glu, "full"), (z, "row")],
                    [(w, BF16, "right"), (w, BF16, "row"), (w, BF16, "row"), (w, F32, "acc")])


def _cast_bf16(x, name):
    r, c = x.shape
    by_cols = r % (2 * SUBLANES) != 0
    tr, tc = (r, _tile(c, 256)) if by_cols else (_tile(r, 512, 2 * SUBLANES), c)
    pos = (lambda i: (0, i)) if by_cols else (lambda i: (i, 0))

    def body(x_ref, o_ref):
        o_ref[...] = x_ref[...].astype(BF16)

    return pl.pallas_call(
        body, name=name, grid=(c // tc if by_cols else r // tr,),
        in_specs=[pl.BlockSpec((tr, tc), pos)], out_specs=pl.BlockSpec((tr, tc), pos),
        out_shape=jax.ShapeDtypeStruct((r, c), BF16), compiler_params=_cparams("parallel"),
    )(x)


def _concat_cast(a, b, name):
    assert a.dtype == b.dtype == BF16 and a.shape[0] == b.shape[0]
    w = a.shape[1]

    def body(a_ref, b_ref, o_ref, sems):
        copies = [pltpu.make_async_copy(a_ref, o_ref.at[:, pl.ds(0, w)], sems.at[0]),
                  pltpu.make_async_copy(b_ref, o_ref.at[:, pl.ds(w, b_ref.shape[1])], sems.at[1])]
        for cp in copies:
            cp.start()
        for cp in copies:
            cp.wait()

    return pl.pallas_call(
        body, name=name, out_shape=jax.ShapeDtypeStruct((a.shape[0], w + b.shape[1]), BF16),
        in_specs=[pl.BlockSpec(memory_space=pl.ANY)] * 2, out_specs=pl.BlockSpec(memory_space=pl.ANY),
        scratch_shapes=[pltpu.SemaphoreType.DMA((2,))],
    )(a, b)


def _disc(ar, ai, ldt):
    dt = jnp.exp(ldt)
    mag = jnp.exp(ar * dt)
    abr = mag * jnp.cos(ai * dt)
    abi = mag * jnp.sin(ai * dt)
    den = ar * ar + ai * ai
    nr = abr - 1.0
    return abr, abi, (nr * ar + abi * ai) / den, (abi * ar - nr * ai) / den


def _s5_disc_fwd(a_re, a_im, ldt):
    def body(ar, ai, ld, o1, o2, o3, o4):
        o1[...], o2[...], o3[...], o4[...] = _disc(ar[...], ai[...], ld[...])

    sh = jax.ShapeDtypeStruct(a_re.shape, F32)
    return pl.pallas_call(body, name="s5_disc_fwd", out_shape=(sh, sh, sh, sh))(a_re, a_im, ldt)


def _s5_disc_bwd(a_re, a_im, ldt, d_abr, d_abi, d_cr, d_ci):
    def body(ar, ai, ld, g1, g2, g3, g4, o1, o2, o3):
        _, vjp = jax.vjp(_disc, ar[...], ai[...], ld[...])
        o1[...], o2[...], o3[...] = vjp((g1[...], g2[...], g3[...], g4[...]))

    sh = jax.ShapeDtypeStruct(a_re.shape, F32)
    return pl.pallas_call(body, name="s5_disc_bwd", out_shape=(sh, sh, jax.ShapeDtypeStruct(ldt.shape, F32)))(
        a_re, a_im, ldt, d_abr, d_abi, d_cr, d_ci)


def _bbar(cr, ci, br, bi):
    return cr * br - ci * bi, cr * bi + ci * br


def _s5_bbar_fwd(cr_col, ci_col, b_re, b_im):
    def body(cr, ci, br, bi, o1, o2):
        o1[...], o2[...] = _bbar(cr[...], ci[...], br[...], bi[...])

    w = b_re.shape[1]
    return _rowcall(body, "s5_bbar_fwd", b_re.shape[0], [(cr_col, "row"), (ci_col, "row"), (b_re, "row"), (b_im, "row")],
                    [(w, F32, "row"), (w, F32, "row")], tile_rows=1024)


def _s5_bbar_bwd(cr_col, ci_col, b_re, b_im, d_re, d_im):
    def body(cr, ci, br, bi, g1, g2, o1, o2, o3, o4):
        _, vjp = jax.vjp(_bbar, cr[...], ci[...], br[...], bi[...])
        o1[...], o2[...], o3[...], o4[...] = vjp((g1[...], g2[...]))

    w = b_re.shape[1]
    return _rowcall(body, "s5_bbar_bwd", b_re.shape[0],
                    [(cr_col, "row"), (ci_col, "row"), (b_re, "row"), (b_im, "row"), (d_re, "row"), (d_im, "row")],
                    [(1, F32, "row"), (1, F32, "row"), (w, F32, "row"), (w, F32, "row")], tile_rows=1024)


def _block_diag(t):
    g, a, b = t.shape
    nb = g // GROUPS_PER_BLOCK
    t4 = t.reshape(nb, GROUPS_PER_BLOCK, a, b).transpose(0, 1, 3, 2)
    eye = jnp.eye(GROUPS_PER_BLOCK, dtype=t.dtype)
    return (t4[:, :, :, None, :] * eye[None, :, None, :, None]).reshape(nb, GROUPS_PER_BLOCK * b, GROUPS_PER_BLOCK * a)


def _block_diag_extract(d, a, b):
    nb = d.shape[0]
    d5 = d.reshape(nb, GROUPS_PER_BLOCK, b, GROUPS_PER_BLOCK, a)
    diag = jnp.stack([d5[:, g, :, g, :] for g in range(GROUPS_PER_BLOCK)], axis=1)
    return diag.transpose(0, 1, 3, 2).reshape(nb * GROUPS_PER_BLOCK, a, b)


def _scan_step(ar, ai, hr, hi, xr, xi):
    return ar * hr - ai * hi + xr, ar * hi + ai * hr + xi


def _s5_blocks_per_step(nb, full):
    want = 2 if full else 4
    while nb % want:
        want //= 2
    return want


def _s5_scan_fwd(u, bd_re, bd_im, cd_re, cd_im, ab_re, ab_im, init_re, init_im, d_row, full, name):
    s, w = u.shape
    nb = w // LANES
    rows = _tile(s, 512, SUBLANES)
    nc = s // rows
    steps = rows // N_SEG
    ns = nb * BLOCK_STATE

    nblk = _s5_blocks_per_step(nb, full)

    def body(u_ref, bdr, bdi, cdr, cdi, ar_ref, ai_ref, ir_ref, ii_ref, d_ref, *outs):
        if full:
            y_ref, yg_ref, hr_out, hi_out, er_ref, ei_ref, hr_ref, hi_ref, cr, ci = outs
        else:
            er_ref, ei_ref, hr_ref, hi_ref, cr, ci = outs
        c = pl.program_id(1)
        cols = lambda b, width: slice(b * width, (b + 1) * width)

        @pl.when(c == 0)
        def _():
            cr[...] = ir_ref[...]
            ci[...] = ii_ref[...]

        for b in range(nblk):
            ub = u_ref[:, cols(b, LANES)].astype(BF16)
            hr_ref[:, cols(b, BLOCK_STATE)] = jnp.dot(ub, bdr[b], preferred_element_type=F32)
            hi_ref[:, cols(b, BLOCK_STATE)] = jnp.dot(ub, bdi[b], preferred_element_type=F32)
        ar, ai = ar_ref[...], ai_ref[...]

        hr, hi = cr[...], ci[...]
        for j in range(steps):
            rows_j = pl.ds(j * N_SEG, N_SEG)
            hr, hi = _scan_step(ar, ai, hr, hi, hr_ref[rows_j, :], hi_ref[rows_j, :])
            hr_ref[rows_j, :] = hr
            hi_ref[rows_j, :] = hi
        cr[...] = hr
        ci[...] = hi
        if full:
            hr_out[...] = hr_ref[...].astype(BF16)
            hi_out[...] = hi_ref[...].astype(BF16)
            for b in range(nblk):
                st_b, ln_b = cols(b, BLOCK_STATE), cols(b, LANES)
                y = (jnp.dot(hr_out[:, st_b], cdr[b], preferred_element_type=F32)
                     + jnp.dot(hi_out[:, st_b], cdi[b], preferred_element_type=F32)
                     + d_ref[:, ln_b] * u_ref[:, ln_b])
                y_ref[:, ln_b] = y.astype(BF16)
                yg_ref[:, ln_b] = jax.nn.gelu(y).astype(BF16)

        @pl.when(c == nc - 1)
        def _():
            er_ref[...] = hr
            ei_ref[...] = hi

    lanes, states = LANES * nblk, BLOCK_STATE * nblk
    blk3 = lambda a: pl.BlockSpec((nblk,) + a.shape[1:], lambda k, c: (k, 0, 0))
    seg = pl.BlockSpec((N_SEG, states), lambda k, c: (0, k))
    st = pl.BlockSpec((rows, states), lambda k, c: (c, k))
    in_specs = [pl.BlockSpec((rows, lanes), lambda k, c: (c, k)), blk3(bd_re), blk3(bd_im), blk3(cd_re), blk3(cd_im),
                seg, seg, seg, seg, pl.BlockSpec((1, lanes), lambda k, c: (0, k))]
    seg_shape = jax.ShapeDtypeStruct((N_SEG, ns), F32)
    st_shape = jax.ShapeDtypeStruct((s, ns), BF16)
    scratch = [pltpu.VMEM((rows, states), F32)] * 2 + [pltpu.VMEM((N_SEG, states), F32)] * 2
    if full:
        ych = pl.BlockSpec((rows, lanes), lambda k, c: (c, k))
        out_specs = [ych, ych, st, st, seg, seg]
        out_shape = [jax.ShapeDtypeStruct((s, w), BF16), jax.ShapeDtypeStruct((s, w), BF16), st_shape, st_shape, seg_shape, seg_shape]
    else:
        out_specs = [seg, seg]
        out_shape = [seg_shape, seg_shape]
    return pl.pallas_call(
        body, name=name, grid=(nb // nblk, nc), in_specs=in_specs, out_specs=out_specs, out_shape=out_shape,
        scratch_shapes=scratch, compiler_params=_cparams("parallel", "arbitrary"),
    )(u, bd_re, bd_im, cd_re, cd_im, ab_re, ab_im, init_re, init_im, d_row)


def _s5_seg_fix(e_re, e_im, ab_re, ab_im, seg_len, reverse, name):
    assert seg_len & (seg_len - 1) == 0

    def body(er, ei, ar, ai, o_re, o_im):
        pr, pi = ar[0:1, :], ai[0:1, :]
        for _ in range(int(math.log2(seg_len))):
            pr, pi = pr * pr - pi * pi, 2.0 * pr * pi
        tr = jnp.zeros_like(pr)
        ti = jnp.zeros_like(pr)
        order = list(range(N_SEG - 1, -1, -1)) if reverse else list(range(N_SEG))
        for n, sgm in enumerate(order):
            o_re[sgm:sgm + 1, :] = tr
            o_im[sgm:sgm + 1, :] = ti
            if n < N_SEG - 1:
                tr, ti = _scan_step(pr, pi, tr, ti, er[sgm:sgm + 1, :], ei[sgm:sgm + 1, :])

    sh = jax.ShapeDtypeStruct(e_re.shape, F32)
    return pl.pallas_call(body, name=name, out_shape=(sh, sh))(e_re, e_im, ab_re, ab_im)


def _s5_scan_bwd(dy, u, h_re, h_im, bd_re, bd_im, cd_re, cd_im, ab_re, ab_imn, gin_re, gin_im, d_row, full, name, duz=None):
    s, w = u.shape
    nb = w // LANES
    rows = _tile(s, 512, SUBLANES)
    nc = s // rows
    steps = rows // N_SEG
    ns = nb * BLOCK_STATE

    nblk = _s5_blocks_per_step(nb, full)

    def body(dy_ref, u_ref, hr_ref, hi_ref, bdr, bdi, cdr, cdi, ar_ref, ai_ref, ir_ref, ii_ref, d_ref, *outs):
        if full:
            _, du_ref, dbr_ref, dbi_ref, dcr_ref, dci_ref, dar_ref, dai_ref, dd_ref, gr, gi, accr, acci = outs
        else:
            er_ref, ei_ref, gr, gi = outs
        c = pl.program_id(1)
        cols = lambda b, width: slice(b * width, (b + 1) * width)

        @pl.when(c == 0)
        def _():
            gr[pl.ds(rows, N_SEG), :] = ir_ref[...]
            gi[pl.ds(rows, N_SEG), :] = ii_ref[...]
            if full:
                for r in (dbr_ref, dbi_ref, dcr_ref, dci_ref, dd_ref, accr, acci):
                    r[...] = jnp.zeros_like(r)

        nt = (_DOT_DIMS["nt"], ((), ()))
        tn = (_DOT_DIMS["tn"], ((), ()))
        for b in range(nblk):
            dyb = dy_ref[:, cols(b, LANES)]
            gr[pl.ds(0, rows), cols(b, BLOCK_STATE)] = lax.dot_general(dyb, cdr[b], nt, preferred_element_type=F32)
            gi[pl.ds(0, rows), cols(b, BLOCK_STATE)] = lax.dot_general(dyb, cdi[b], nt, preferred_element_type=F32)
        ar, ai = ar_ref[...], ai_ref[...]

        g0r, g0i = gr[pl.ds(rows, N_SEG), :], gi[pl.ds(rows, N_SEG), :]
        for j in range(steps - 1, -1, -1):
            rows_j = pl.ds(j * N_SEG, N_SEG)
            g0r, g0i = _scan_step(ar, ai, g0r, g0i, gr[rows_j, :], gi[rows_j, :])
            gr[rows_j, :] = g0r
            gi[rows_j, :] = g0i
        if full:
            for b in range(nblk):
                st_b, ln_b = cols(b, BLOCK_STATE), cols(b, LANES)
                hr, hi = hr_ref[:, st_b], hi_ref[:, st_b]
                gnr, gni = gr[pl.ds(N_SEG, rows), st_b], gi[pl.ds(N_SEG, rows), st_b]
                accr[:, st_b] += jnp.sum((gnr * hr + gni * hi).reshape(steps, N_SEG, BLOCK_STATE), axis=0)
                acci[:, st_b] += jnp.sum((gni * hr - gnr * hi).reshape(steps, N_SEG, BLOCK_STATE), axis=0)
                dyb = dy_ref[:, ln_b]
                ub = u_ref[:, ln_b].astype(BF16)
                gbr, gbi = gr[pl.ds(0, rows), st_b].astype(BF16), gi[pl.ds(0, rows), st_b].astype(BF16)
                dcr_ref[b] += lax.dot_general(hr.astype(BF16), dyb, tn, preferred_element_type=F32)
                dci_ref[b] += lax.dot_general(hi.astype(BF16), dyb, tn, preferred_element_type=F32)
                dbr_ref[b] += lax.dot_general(ub, gbr, tn, preferred_element_type=F32)
                dbi_ref[b] += lax.dot_general(ub, gbi, tn, preferred_element_type=F32)
                du_ref[:, ln_b] = (lax.dot_general(gbr, bdr[b], nt, preferred_element_type=F32)
                                   + lax.dot_general(gbi, bdi[b], nt, preferred_element_type=F32)
                                   + d_ref[:, ln_b] * dy_ref[:, ln_b].astype(F32)).astype(BF16)
                dd_ref[:, ln_b] += jnp.sum(dy_ref[:, ln_b].astype(F32) * u_ref[:, ln_b], axis=0, keepdims=True)
        gr[pl.ds(rows, N_SEG), :] = g0r
        gi[pl.ds(rows, N_SEG), :] = g0i

        @pl.when(c == nc - 1)
        def _():
            if full:
                dar_ref[...] = jnp.sum(accr[...], axis=0, keepdims=True)
                dai_ref[...] = jnp.sum(acci[...], axis=0, keepdims=True)
            else:
                er_ref[...] = g0r
                ei_ref[...] = g0i

    lanes, states = LANES * nblk, BLOCK_STATE * nblk
    rev = lambda k, c: (nc - 1 - c, k)
    blk3 = lambda a: pl.BlockSpec((nblk,) + a.shape[1:], lambda k, c: (k, 0, 0))
    seg = pl.BlockSpec((N_SEG, states), lambda k, c: (0, k))
    st = pl.BlockSpec((rows, states), rev)
    ch = pl.BlockSpec((rows, lanes), rev)
    vec = pl.BlockSpec((1, lanes), lambda k, c: (0, k))
    if not full:
        st = pl.BlockSpec((rows, states), lambda k, c: (0, k))
    in_specs = [ch, ch if full else pl.BlockSpec((rows, lanes), lambda k, c: (0, k)), st, st,
                blk3(bd_re), blk3(bd_im), blk3(cd_re), blk3(cd_im), seg, seg, seg, seg, vec]
    args = [dy, u, h_re, h_im, bd_re, bd_im, cd_re, cd_im, ab_re, ab_imn, gin_re, gin_im, d_row]
    gbuf = [pltpu.VMEM((rows + N_SEG, states), F32)] * 2
    if full:
        row1 = pl.BlockSpec((1, states), lambda k, c: (0, k))
        out_specs = [ch, blk3(bd_re), blk3(bd_im), blk3(cd_re), blk3(cd_im), row1, row1, vec]
        out_shape = [jax.ShapeDtypeStruct(duz.shape, BF16),
                     jax.ShapeDtypeStruct(bd_re.shape, F32), jax.ShapeDtypeStruct(bd_im.shape, F32),
                     jax.ShapeDtypeStruct(cd_re.shape, F32), jax.ShapeDtypeStruct(cd_im.shape, F32),
                     jax.ShapeDtypeStruct((1, ns), F32), jax.ShapeDtypeStruct((1, ns), F32),
                     jax.ShapeDtypeStruct((1, w), F32)]
        scratch = gbuf + [pltpu.VMEM((N_SEG, states), F32)] * 2
        in_specs.append(pl.BlockSpec(memory_space=pl.ANY))
        args.append(duz)
        aliases = {len(args) - 1: 0}
    else:
        out_specs = [seg, seg]
        out_shape = [jax.ShapeDtypeStruct((N_SEG, ns), F32)] * 2
        scratch = gbuf
        aliases = {}
    return pl.pallas_call(
        body, name=name, grid=(nb // nblk, nc), in_specs=in_specs, out_specs=out_specs, out_shape=out_shape,
        input_output_aliases=aliases, scratch_shapes=scratch, compiler_params=_cparams("parallel", "arbitrary"),
    )(*args)


def _log_sigmoid(x):
    return jnp.minimum(x, 0.0) - jnp.log(1.0 + jnp.exp(-jnp.abs(x)))


def _tri(n, upper):
    r = lax.broadcasted_iota(jnp.int32, (n, n), 0)
    c = lax.broadcasted_iota(jnp.int32, (n, n), 1)
    return jnp.where((c >= r) if upper else (r >= c), 1.0, 0.0).astype(F32)


def _cum_fwd(f_logit, b_row, name):
    s, w = f_logit.shape
    t = _tile(s, 256, SUBLANES)

    def body(f_ref, b_ref, o_ref, carry):
        @pl.when(pl.program_id(0) == 0)
        def _():
            carry[...] = jnp.zeros_like(carry)

        lf = _log_sigmoid(f_ref[...] + b_ref[...])
        cum = jnp.dot(_tri(t, False), lf, precision=lax.Precision.HIGHEST, preferred_element_type=F32) + carry[...]
        o_ref[...] = cum * LOG2E
        carry[...] = cum[t - 1:t, :]

    return pl.pallas_call(
        body, name=name, grid=(s // t,),
        in_specs=[pl.BlockSpec((t, w), lambda i: (i, 0)), pl.BlockSpec((1, w), lambda i: (0, 0))],
        out_specs=pl.BlockSpec((t, w), lambda i: (i, 0)), out_shape=jax.ShapeDtypeStruct((s, w), F32),
        scratch_shapes=[pltpu.VMEM((1, w), F32)], compiler_params=_cparams("arbitrary"),
    )(f_logit, b_row)


def _cum_bwd(dcq, dck, f_logit, b_row, name):
    s, w = f_logit.shape
    t = _tile(s, 256, SUBLANES)
    nt = s // t

    def body(q_ref, k_ref, f_ref, b_ref, df_ref, db_ref, carry):
        @pl.when(pl.program_id(0) == 0)
        def _():
            carry[...] = jnp.zeros_like(carry)
            db_ref[...] = jnp.zeros_like(db_ref)

        dc = q_ref[...] - k_ref[...]
        rc = jnp.dot(_tri(t, True), dc, precision=lax.Precision.HIGHEST, preferred_element_type=F32) + carry[...]
        carry[...] = rc[0:1, :]
        df = rc * (1.0 - jax.nn.sigmoid(f_ref[...] + b_ref[...]))
        df_ref[...] = df.astype(BF16)
        db_ref[...] += jnp.sum(df, axis=0, keepdims=True)

    rev = pl.BlockSpec((t, w), lambda i: (nt - 1 - i, 0))
    one = pl.BlockSpec((1, w), lambda i: (0, 0))
    return pl.pallas_call(
        body, name=name, grid=(nt,), in_specs=[rev, rev, rev, one], out_specs=[rev, one],
        out_shape=[jax.ShapeDtypeStruct((s, w), BF16), jax.ShapeDtypeStruct((1, w), F32)],
        scratch_shapes=[pltpu.VMEM((1, w), F32)], compiler_params=_cparams("arbitrary"),
    )(dcq, dck, f_logit, b_row)


def _head_col(cum_tile, h):
    lane = lax.broadcasted_iota(jnp.int32, cum_tile.shape, 1)
    return jnp.sum(jnp.where(lane == h, cum_tile, 0.0), axis=1, keepdims=True)


def _attn_tiles(s):
    return _tile(s, 512, LANES)


def _exp2_rows(sc, sub):
    return jnp.concatenate([jnp.exp2(sc[:, b * LANES:(b + 1) * LANES] - sub) for b in range(sc.shape[1] // LANES)], axis=1)


def _row_of(rep):
    return jnp.transpose(rep)[0:1, :]


def _causal(sc, keys_on_rows):
    r = lax.broadcasted_iota(jnp.int32, sc.shape, 0)
    c = lax.broadcasted_iota(jnp.int32, sc.shape, 1)
    return jnp.where((r <= c) if keys_on_rows else (c <= r), sc, NEG_INF)


def _fox_fwd(q2, kv, cum2_t, z, name):
    s, w = q2.shape
    nh = w // HEAD_DIM
    tq = _attn_tiles(s)
    nq = s // tq
    nt = (_DOT_DIMS["nt"], ((), ()))

    def body(q_ref, k_ref, v_ref, ct_ref, z_ref, o_ref, oz_ref, lse_row_ref, m_s, acc_s, vaug, s_buf):
        i = pl.program_id(1)

        @pl.when(i == 0)
        def _():
            vaug[:, :HEAD_DIM] = v_ref[...]
            vaug[:, HEAD_DIM:] = jnp.ones((s, LANES), BF16)

        qb = q_ref[...]
        m_s[...] = jnp.full_like(m_s, NEG_INF)
        acc_s[...] = jnp.zeros_like(acc_s)

        def scores(j):
            off = pl.multiple_of(j * tq, tq)
            return lax.dot_general(qb, k_ref[pl.ds(off, tq), :], nt, preferred_element_type=F32) - ct_ref[:, pl.ds(off, tq)]

        def softmax_pv(j, sc):
            m_old = m_s[...]
            m_new = jnp.maximum(m_old, jnp.max(sc, axis=1, keepdims=True))
            p = _exp2_rows(sc, m_new)
            alpha = jnp.exp2(m_old - m_new)
            pv = jnp.dot(p.astype(BF16), vaug[pl.ds(pl.multiple_of(j * tq, tq), tq), :], preferred_element_type=F32)
            acc_s[...] = jnp.concatenate([alpha, alpha], axis=1) * acc_s[...] + pv
            m_s[...] = m_new

        s_buf[...] = scores(0)

        def loop(j, carry):
            nxt = scores(j + 1)
            softmax_pv(j, s_buf[...])
            s_buf[...] = nxt
            return carry

        lax.fori_loop(0, i, loop, 0)
        softmax_pv(i, _causal(s_buf[...], False))
        l = acc_s[:, HEAD_DIM:]
        o = acc_s[:, :HEAD_DIM] / l
        o_ref[...] = o.astype(BF16)
        oz_ref[...] = (o * _silu(z_ref[...].astype(F32))).astype(BF16)
        lse_row_ref[...] = _row_of(m_s[...] + jnp.log(l) * LOG2E)

    return pl.pallas_call(
        body, name=name, grid=(nh, nq),
        in_specs=[pl.BlockSpec((tq, HEAD_DIM), lambda h, i: (i, h)),
                  pl.BlockSpec((s, HEAD_DIM), lambda h, i: (0, h)),
                  pl.BlockSpec((s, HEAD_DIM), lambda h, i: (0, nh + h)),
                  pl.BlockSpec((None, 1, s), lambda h, i: (h, 0, 0)),
                  pl.BlockSpec((tq, HEAD_DIM), lambda h, i: (i, h))],
        out_specs=[pl.BlockSpec((tq, HEAD_DIM), lambda h, i: (i, h)),
                   pl.BlockSpec((tq, HEAD_DIM), lambda h, i: (i, h)),
                   pl.BlockSpec((None, 1, tq), lambda h, i: (h, 0, i))],
        out_shape=[jax.ShapeDtypeStruct((s, w), BF16), jax.ShapeDtypeStruct((s, w), BF16),
                   jax.ShapeDtypeStruct((nh, 1, s), F32)],
        scratch_shapes=[pltpu.VMEM((tq, LANES), F32), pltpu.VMEM((tq, HEAD_DIM + LANES), F32),
                        pltpu.VMEM((s, HEAD_DIM + LANES), BF16), pltpu.VMEM((tq, tq), F32)],
        compiler_params=_cparams("arbitrary", "arbitrary"),
    )(q2, kv, kv, cum2_t, z)


def _fox_bwd(q2, kv, do, o, lse2_t, cum2, dqz, name):
    s, w = q2.shape
    nh = w // HEAD_DIM
    tk = _attn_tiles(s)
    nk = s // tk
    scale = HEAD_DIM ** -0.5
    nt = (_DOT_DIMS["nt"], ((), ()))
    tn = (_DOT_DIMS["tn"], ((), ()))

    def body(q_ref, k_ref, v_ref, do_ref, o_ref, lse_ref, c_ref, _, dk_ref, dv_ref, dq_ref, dcq_ref, dck_ref,
             dk_s, dv_s, dc_s, dq_s, dcq_s, dl_s, s_buf, dp_buf):
        h, j = pl.program_id(0), pl.program_id(1)

        @pl.when(j == 0)
        def _():
            dq_s[...] = jnp.zeros_like(dq_s)
            dcq_s[...] = jnp.zeros_like(dcq_s)
            for i in range(nk):
                rows = pl.ds(i * tk, tk)
                d = jnp.sum(do_ref[rows, :].astype(F32) * o_ref[rows, :].astype(F32), axis=1, keepdims=True)
                dl_s[:, i * tk:(i + 1) * tk] = _row_of(jnp.broadcast_to(d, (tk, LANES)))

        kb = k_ref[...]
        vb = v_ref[...]
        ck = jnp.broadcast_to(_head_col(c_ref[...], h), (tk, LANES))
        dk_s[...] = jnp.zeros_like(dk_s)
        dv_s[...] = jnp.zeros_like(dv_s)
        dc_s[...] = jnp.zeros_like(dc_s)

        def scores(i):
            off = pl.multiple_of(i * tk, tk)
            sc = lax.dot_general(kb, q_ref[pl.ds(off, tk), :], nt, preferred_element_type=F32) - lse_ref[:, pl.ds(off, tk)]
            dp = lax.dot_general(vb, do_ref[pl.ds(off, tk), :], nt, preferred_element_type=F32) - dl_s[:, pl.ds(off, tk)]
            return sc, dp

        def accumulate(i, sc, dp):
            off = pl.multiple_of(i * tk, tk)
            p = _exp2_rows(sc, ck)
            dv_s[...] += jnp.dot(p.astype(BF16), do_ref[pl.ds(off, tk), :], preferred_element_type=F32)
            ds = p * dp
            dsb = ds.astype(BF16)
            dk_s[...] += jnp.dot(dsb, q_ref[pl.ds(off, tk), :], preferred_element_type=F32)
            dq_s[pl.ds(off, tk), :] += lax.dot_general(dsb, kb, tn, preferred_element_type=F32)
            dcq_s[:, pl.ds(off, tk)] += jnp.sum(ds, axis=0, keepdims=True)
            part = ds[:, :LANES]
            for b in range(1, tk // LANES):
                part = part + ds[:, b * LANES:(b + 1) * LANES]
            dc_s[...] += part

        sc0, dp0 = scores(j)
        s_buf[...] = _causal(sc0, True)
        dp_buf[...] = dp0

        def loop(i, carry):
            nxt = scores(i + 1)
            accumulate(i, s_buf[...], dp_buf[...])
            s_buf[...], dp_buf[...] = nxt
            return carry

        lax.fori_loop(j, nk - 1, loop, 0)
        accumulate(nk - 1, s_buf[...], dp_buf[...])
        dk_ref[...] = (dk_s[...] * (1.0 / LOG2E)).astype(BF16)
        dv_ref[...] = dv_s[...].astype(BF16)
        dck_ref[...] = jnp.sum(jnp.transpose(dc_s[...]), axis=0, keepdims=True)

        @pl.when(j == nk - 1)
        def _():
            dq_ref[...] = (dq_s[...] * scale).astype(BF16)
            dcq_ref[...] = dcq_s[...]

    col = pl.BlockSpec((s, HEAD_DIM), lambda h, j: (0, h))
    row = pl.BlockSpec((None, 1, s), lambda h, j: (h, 0, 0))
    kspec = pl.BlockSpec((tk, HEAD_DIM), lambda h, j: (j, h))
    return pl.pallas_call(
        body, name=name, grid=(nh, nk),
        in_specs=[col, kspec, pl.BlockSpec((tk, HEAD_DIM), lambda h, j: (j, nh + h)), col, col, row,
                  pl.BlockSpec((tk, LANES), lambda h, j: (j, 0)), pl.BlockSpec(memory_space=pl.ANY)],
        out_specs=[kspec, kspec, col, row, pl.BlockSpec((None, 1, tk), lambda h, j: (h, 0, j))],
        out_shape=[jax.ShapeDtypeStruct((s, w), BF16), jax.ShapeDtypeStruct((s, w), BF16),
                   jax.ShapeDtypeStruct(dqz.shape, BF16), jax.ShapeDtypeStruct((nh, 1, s), F32),
                   jax.ShapeDtypeStruct((nh, 1, s), F32)],
        input_output_aliases={7: 2},
        scratch_shapes=[pltpu.VMEM((tk, HEAD_DIM), F32), pltpu.VMEM((tk, HEAD_DIM), F32), pltpu.VMEM((tk, LANES), F32),
                        pltpu.VMEM((s, HEAD_DIM), F32), pltpu.VMEM((1, s), F32), pltpu.VMEM((1, s), F32),
                        pltpu.VMEM((tk, tk), F32), pltpu.VMEM((tk, tk), F32)],
        compiler_params=_cparams("arbitrary", "arbitrary"),
    )(q2, kv, kv, do, o, lse2_t, cum2, dqz)


_ALL_PEERS = tuple(range(1, N_DEV))
_CHIP_PEERS = (1, 2, 4, 6)


def _exchange_copies(ins, outs, send_sems, recv_sems, local_sems, scatter, peers=_ALL_PEERS):
    x, y, c = (lax.axis_index(a) for a in MESH_AXES)
    me = 4 * x + 2 * y + c
    local, remote = [], []
    for a in range(len(ins)):
        local.append(pltpu.make_async_copy(ins[a].at[me] if scatter else ins[a], outs[a].at[me], local_sems.at[a]))
        for k in peers:
            px, py, pc = (1 - x if k & 4 else x), (1 - y if k & 2 else y), (1 - c if k & 1 else c)
            remote.append(pltpu.make_async_remote_copy(
                src_ref=ins[a].at[4 * px + 2 * py + pc] if scatter else ins[a], dst_ref=outs[a].at[me],
                send_sem=send_sems.at[a * (N_DEV - 1) + k - 1], recv_sem=recv_sems.at[a * (N_DEV - 1) + k - 1],
                device_id=(px, py, pc), device_id_type=pl.DeviceIdType.MESH))
    return local, remote


def _exchange_out_shapes(arrs, scatter):
    return [((N_DEV,) + a.shape[1:]) if scatter else ((N_DEV,) + a.shape) for a in arrs]


_HBM =pl.BlockSpec(memory_space=pltpu.HBM)
_SEM = pl.BlockSpec(memory_space=pltpu.SEMAPHORE)


def _exchange_start(arrs, scatter, name, after=(), peers=_ALL_PEERS):
    n = len(arrs)
    after = list(after)
    lands = [lax.empty(s, a.dtype) for s, a in zip(_exchange_out_shapes(arrs, scatter), arrs)]

    def body(*refs):
        ins, outs = refs[:n], refs[n:2 * n]
        send_sems, recv_sems, local_sems = refs[2 * n + len(after):2 * n + len(after) + 3]
        token = refs[-1]
        local, remote = _exchange_copies(ins, outs, send_sems, recv_sems, local_sems, scatter, peers)
        for cp in local + remote:
            cp.start()
        token[...] = jnp.zeros_like(token)

    hbm = lambda a: pltpu.HBM(a.shape, a.dtype)
    res = pl.pallas_call(
        body, name=name,
        out_shape=(pltpu.SemaphoreType.DMA((n * (N_DEV - 1),)), pltpu.SemaphoreType.DMA((n * (N_DEV - 1),)),
                   pltpu.SemaphoreType.DMA((n,)), *[hbm(a) for a in arrs], *[hbm(a) for a in lands],
                   jax.ShapeDtypeStruct((SUBLANES, LANES), F32)),
        in_specs=[_HBM] * (2 * n) + [pl.BlockSpec(memory_space=pl.ANY)] * len(after),
        out_specs=(_SEM, _SEM, _SEM, *[_HBM] * (2 * n), pl.BlockSpec(memory_space=pltpu.VMEM)),
        input_output_aliases={i: 3 + i for i in range(2 * n)},
        compiler_params=pltpu.CompilerParams(has_side_effects=pltpu.SideEffectType.DATAFLOW_SIDE_EFFECTING),
    )(*[pltpu.with_memory_space_constraint(a, pltpu.HBM) for a in list(arrs) + lands], *after)
    return (n, scatter, res[:3], res[3:3 + n], res[3 + n:3 + 2 * n], peers), res[-1]


def _exchange_wait(state, after, name):
    n, scatter, sems, srcs, lands, peers = state
    after = list(after) if isinstance(after, (list, tuple)) else [after]

    def body(*refs):
        ins, outs = refs[:n], refs[n:2 * n]
        send_sems, recv_sems, local_sems = refs[2 * n:2 * n + 3]
        local, remote = _exchange_copies(ins, outs, send_sems, recv_sems, local_sems, scatter, peers)
        for cp in remote:
            cp.wait_send()
            cp.wait_recv()
        for cp in local:
            cp.wait()

    hbm = lambda a: pltpu.HBM(a.shape, a.dtype)
    res = pl.pallas_call(
        body, name=name,
        out_shape=(*[hbm(a) for a in srcs], *[hbm(a) for a in lands]),
        in_specs=[_HBM] * (2 * n) + [_SEM] * 3 + [pl.BlockSpec(memory_space=pl.ANY)] * len(after),
        out_specs=tuple([_HBM] * (2 * n)),
        input_output_aliases={i: i for i in range(2 * n)},
        compiler_params=pltpu.CompilerParams(has_side_effects=pltpu.SideEffectType.DATAFLOW_SIDE_EFFECTING),
    )(*srcs, *lands, *sems, *after)
    return list(res[n:])


def _forward_to_sibling(slots, name):
    n = len(slots)
    hops = (2, 4, 6)

    def body(*refs):
        ins, outs, (send_sems, recv_sems) = refs[:n], refs[n:2 * n], refs[2 * n:]
        x, y, c = (lax.axis_index(a) for a in MESH_AXES)
        copies = []
        for a in range(n):
            for i, k in enumerate(hops):
                slot = 4 * (1 - x if k & 4 else x) + 2 * (1 - y if k & 2 else y) + c
                copies.append(pltpu.make_async_remote_copy(
                    src_ref=ins[a].at[slot], dst_ref=outs[a].at[slot],
                    send_sem=send_sems.at[a * len(hops) + i], recv_sem=recv_sems.at[a * len(hops) + i],
                    device_id=(x, y, 1 - c), device_id_type=pl.DeviceIdType.MESH))
        for cp in copies:
            cp.start()
        for cp in copies:
            cp.wait_send()
            cp.wait_recv()

    return pl.pallas_call(
        body, name=name, out_shape=[jax.ShapeDtypeStruct(s.shape, s.dtype) for s in slots],
        in_specs=[pl.BlockSpec(memory_space=pl.ANY)] * n, out_specs=[pl.BlockSpec(memory_space=pl.ANY)] * n,
        input_output_aliases={i: i for i in range(n)},
        scratch_shapes=[pltpu.SemaphoreType.DMA((n * len(hops),)), pltpu.SemaphoreType.DMA((n * len(hops),))],
    )(*slots)


def _adamw_math(w, g, m, v):
    m = ADAM_B1 * m + (1.0 - ADAM_B1) * g
    v = ADAM_B2 * v + (1.0 - ADAM_B2) * (g * g)
    m_hat = m / (1.0 - ADAM_B1 ** ADAM_STEP)
    v_hat = v / (1.0 - ADAM_B2 ** ADAM_STEP)
    return -ADAM_LR * (m_hat / (jnp.sqrt(v_hat) + ADAM_EPS) + ADAM_WD * w), m, v


def _slot_sum(p_ref):
    g = p_ref[0].astype(F32)
    for d in range(1, p_ref.shape[0]):
        g = g + p_ref[d].astype(F32)
    return g


def _adamw_tile(r, c):
    return _tile(r, max(SUBLANES, (256 * 1024) // c // SUBLANES * SUBLANES), SUBLANES)


def _adamw(parts, w, m, v, name):
    r, c = w.shape[-2:]
    by_cols = r % SUBLANES != 0
    tr, tc = (r, _tile(c, 256)) if by_cols else (_adamw_tile(r, c), c)

    def body(p_ref, w_ref, m_ref, v_ref, g_ref, d_ref, nm_ref, nv_ref):
        g = _slot_sum(p_ref)
        g_ref[...] = g
        d_ref[...], nm_ref[...], nv_ref[...] = _adamw_math(w_ref[...], g, m_ref[...], v_ref[...])

    pos = (lambda i: (0, i)) if by_cols else (lambda i: (i, 0))
    if w.ndim == 3:
        blk = pl.BlockSpec((None, tr, tc), lambda i: (0,) + pos(i))
    else:
        blk = pl.BlockSpec((tr, tc), pos)
    sh = jax.ShapeDtypeStruct(w.shape, F32)
    return pl.pallas_call(
        body, name=name, grid=(c // tc if by_cols else r // tr,),
        in_specs=[pl.BlockSpec((parts.shape[0], tr, tc), lambda i: (0,) + pos(i)), blk, blk, blk],
        out_specs=[blk] * 4, out_shape=[sh] * 4, compiler_params=_cparams("parallel"),
    )(parts, w, m, v)


def _sum_parts(parts, name):
    _, r, c = parts.shape
    tr = _adamw_tile(r, c)

    def body(p_ref, o_ref):
        o_ref[...] = _slot_sum(p_ref)

    return pl.pallas_call(
        body, name=name, grid=(r // tr,),
        in_specs=[pl.BlockSpec((parts.shape[0], tr, c), lambda i: (0, i, 0))],
        out_specs=pl.BlockSpec((tr, c), lambda i: (i, 0)), out_shape=jax.ShapeDtypeStruct((r, c), F32),
        compiler_params=_cparams("parallel"),
    )(parts)


def _lane_pad(a, width=LANES):
    return jnp.pad(a, ((0, 0), (0, width - a.shape[1])))


def _local_step(x, target, norm_pre, norm_post, kv_norm, kv_b_f, a_re, a_im, log_dt, b_re, b_im, c_re, c_im, comm):
    s, d = x.shape
    g, p = a_re.shape
    w = g * S5_GROUP
    fw = d
    nh = fw // HEAD_DIM
    seg_len = s // N_SEG
    row = lambda v: v.reshape(1, -1)
    g_pre0, g_pre1, g_post0, g_post1, g_kv = row(norm_pre[0]), row(norm_pre[1]), row(norm_post[0]), row(norm_post[1]), row(kv_norm)

    ldt = log_dt.reshape(g, 1)
    abr, abi, cr, ci = _s5_disc_fwd(a_re, a_im, ldt)
    cr_col, ci_col = cr.reshape(g * p, 1), ci.reshape(g * p, 1)
    b_re2, b_im2 = b_re.reshape(g * p, S5_GROUP), b_im.reshape(g * p, S5_GROUP)
    bb_re, bb_im = _s5_bbar_fwd(cr_col, ci_col, b_re2, b_im2)
    bd_re = _block_diag(bb_re.reshape(g, p, S5_GROUP)).astype(BF16)
    bd_im = _block_diag(bb_im.reshape(g, p, S5_GROUP)).astype(BF16)
    cd_re = _block_diag(c_re).astype(BF16)
    cd_im = _block_diag(-c_im).astype(BF16)
    ab_re = jnp.broadcast_to(abr.reshape(1, g * p), (N_SEG, g * p))
    ab_im = jnp.broadcast_to(abi.reshape(1, g * p), (N_SEG, g * p))
    zero_seg = jnp.zeros((N_SEG, g * p), F32)

    xn0 = _norm_cast(x, g_pre0 + comm.token, "norm_pre0", x_kind="nat")
    w_in = comm.weight("s5_w_in", [xn0, bd_re, bd_im, cd_re, cd_im, ab_re, ab_im])
    d_row, bglu_row = row(comm.vector("s5_d")), row(comm.vector("s5_b_glu"))
    u = _mm(xn0, w_in, "nn", BF16, "s5_in_u", b_cols=(0, w), b_slots=True)
    z0 = _mm(xn0, w_in, "nn", BF16, "s5_in_z", b_cols=(w, w), b_slots=True)
    e_re, e_im = _s5_scan_fwd(u, bd_re, bd_im, cd_re, cd_im, ab_re, ab_im, zero_seg, zero_seg, d_row, False, "s5_scan_ends")
    i_re, i_im = _s5_seg_fix(e_re, e_im, ab_re, ab_im, seg_len, False, "s5_seg_fix")
    y_ssm, yg, h_re, h_im, _, _ = _s5_scan_fwd(u, bd_re, bd_im, cd_re, cd_im, ab_re, ab_im, i_re, i_im, d_row, True, "s5_scan")
    w_glu, w_out = comm.weight("s5_w_glu", yg), comm.weight("s5_w_out", yg)
    gp = _mm(yg, w_glu, "nn", BF16, "s5_glu")
    y3 = _s5_gate(y_ssm, gp, bglu_row, z0, "s5_gate")
    w_kvt, fw_in = comm.weight("kv_w", y3), comm.weight("fox_w_in", y3)
    w_ft = jnp.pad(w_kvt[2 * fw:], ((0, LANES - nh), (0, 0)))
    o0 = _mm(y3, w_out, "nn", F32, "s5_out")

    h1, hn_kv, xn1 = _resid_norm2(x, o0, g_post0 + comm.late_token, g_kv, g_pre1, "resid_norms")
    kv = _mm(hn_kv, w_kvt, "nt", BF16, "kv_proj", b_rows=2 * fw)
    f_logit = _mm(hn_kv, w_ft, "nt", F32, "f_proj")
    bf_row = _lane_pad(row(kv_b_f))
    cum2 = _cum_fwd(f_logit, bf_row, "cum_fwd")
    cum2_t = cum2[:, :nh].T.reshape(nh, 1, s)
    q2 = _mm(xn1, fw_in, "nn", BF16, "fox_q", scale=HEAD_DIM ** -0.5 * LOG2E, b_cols=(0, fw), b_slots=True)
    z1 = _mm(xn1, fw_in, "nn", BF16, "fox_z", b_cols=(fw, fw), b_slots=True)
    o, oz, lse2_t = _fox_fwd(q2, kv, cum2_t, z1, "fox_fwd")
    fw_out = comm.weight("fox_w_out", oz)
    o1 = _mm(oz, fw_out, "nn", F32, "fox_out")
    dh2, do1, sq, dg_post1 = _post_norm_loss(o1, g_post1, h1, target, "norm_post1_loss")
    loss = 0.5 * jnp.sum(sq) / d

    d_fw_out = _mm(oz, do1, "tn", BF16, "fox_out_dw")
    d_oz = _mm(do1, fw_out, "nt", BF16, "fox_out_dx")
    do, dqz = _gate_bwd(d_oz, o, z1, "fox_gate_bwd")
    dk, dv, dqz, dcq, dck = _fox_bwd(q2, kv, do, o, lse2_t, cum2, dqz, "fox_bwd")
    d_fw_in = _mm(xn1, dqz, "tn", BF16, "fox_in_dw", col_slots=True)
    dxn1 = _mm(dqz, fw_in, "nt", BF16, "fox_in_dx", b_slots=True)
    dcq_sl = _lane_pad(dcq.reshape(nh, s).T)
    dck_sl = _lane_pad(dck.reshape(nh, s).T)
    df, db_f = _cum_bwd(dcq_sl, dck_sl, f_logit, bf_row, "cum_bwd")
    dkv = _concat_cast(dk, dv, "fox_dkv")
    d_w_kvmt = _mm(dkv, hn_kv, "tn", BF16, "kv_dw")
    d_w_ft = _mm(df, hn_kv, "tn", BF16, "f_dw")
    dhn_f = _mm(df, w_ft, "nn", BF16, "f_dx")
    dhn_kv = _mm(dkv, w_kvt, "nn", BF16, "kv_dx", add=dhn_f, b_rows=2 * fw)
    d_w_kvt = jnp.concatenate([d_w_kvmt, d_w_ft[:nh]], axis=0)
    tok = comm.send_grads(dict(fox_w_out=d_fw_out, fox_w_in=d_fw_in, kv_w=d_w_kvt), "exchange_fox")
    dh1, do0, dg_pre1, dg_kv, dg_post0 = _norm_bwd2(dh2, h1, dxn1, dhn_kv, g_pre1, g_kv, o0, g_post0 + tok[0, 0],
                                                      "resid_norms_bwd")

    d_w_out = _mm(y3, do0, "tn", BF16, "s5_out_dw")
    dy3 = _mm(do0, w_out, "nt", BF16, "s5_out_dx")
    duz, dgp, dyg_direct, db_glu = _s5_gate_bwd(dy3, y_ssm, gp, bglu_row, z0, "s5_gate_bwd")
    d_w_glu = _mm(yg, dgp, "tn", BF16, "s5_glu_dw")
    gelu_bwd = lambda dyg, y: jax.vjp(jax.nn.gelu, y.astype(F32))[1](dyg)[0]
    dy_ssm = _mm(dgp, w_glu, "nt", BF16, "s5_glu_dx", add=dyg_direct, epilogue=(gelu_bwd, y_ssm))
    d_row = d_row + comm.send_grads(dict(s5_w_out=d_w_out, s5_w_glu=d_w_glu), "exchange_s5")[0, 0]
    ab_imn = -ab_im
    ge_re, ge_im = _s5_scan_bwd(dy_ssm, u, h_re, h_im, bd_re, bd_im, cd_re, cd_im, ab_re, ab_imn, zero_seg, zero_seg,
                                d_row, False, "s5_adj_ends")
    gi_re, gi_im = _s5_seg_fix(ge_re, ge_im, ab_re, ab_imn, seg_len, True, "s5_adj_fix")
    duz, dbd_re, dbd_im, dcd_re, dcd_im, dab_re, dab_im, dd = _s5_scan_bwd(
        dy_ssm, u, h_re, h_im, bd_re, bd_im, cd_re, cd_im, ab_re, ab_imn, gi_re, gi_im, d_row, True, "s5_adj", duz=duz)
    d_w_in = _mm(xn0, duz, "tn", BF16, "s5_in_dw", col_slots=True)
    tok = comm.send_grads(dict(s5_w_in=d_w_in), "exchange_s5_in")
    dxn0 = _mm(duz, w_in, "nt", BF16, "s5_in_dx", after=tok, b_slots=True)
    grad_x, dg_pre0 = _norm_bwd1(dh1, x, dxn0, g_pre0, "norm_pre0_bwd")

    dbb_re = _block_diag_extract(dbd_re, p, S5_GROUP).reshape(g * p, S5_GROUP)
    dbb_im = _block_diag_extract(dbd_im, p, S5_GROUP).reshape(g * p, S5_GROUP)
    dcr_col, dci_col, db_re, db_im = _s5_bbar_bwd(cr_col, ci_col, b_re2, b_im2, dbb_re, dbb_im)
    da_re, da_im, dldt = _s5_disc_bwd(a_re, a_im, ldt, dab_re.reshape(g, p), dab_im.reshape(g, p),
                                      dcr_col.reshape(g, p), dci_col.reshape(g, p))
    dc_re = _block_diag_extract(dcd_re, S5_GROUP, p)
    dc_im = -_block_diag_extract(dcd_im, S5_GROUP, p)

    small = dict(
        norm_pre=jnp.concatenate([dg_pre0, dg_pre1], axis=0), norm_post=jnp.concatenate([dg_post0, dg_post1], axis=0),
        s5_a_re=da_re, s5_a_im=da_im, s5_log_dt=dldt.reshape(g), s5_b_re=db_re.reshape(g, p, S5_GROUP),
        s5_b_im=db_im.reshape(g, p, S5_GROUP), s5_c_re=dc_re, s5_c_im=dc_im, s5_d=dd.reshape(-1),
        s5_b_glu=db_glu.reshape(-1), kv_norm=dg_kv.reshape(-1), kv_b_f=db_f[0, :nh])
    return loss, grad_x, small


_BIG = ("s5_w_in", "s5_w_glu", "s5_w_out", "kv_w", "fox_w_in", "fox_w_out")
_COL_SHARDED = ("s5_w_in", "fox_w_in")
_SMALL = ("norm_pre", "norm_post", "s5_a_re", "s5_a_im", "s5_log_dt", "s5_b_re", "s5_b_im", "s5_c_re", "s5_c_im",
          "s5_d", "s5_b_glu", "kv_norm", "kv_b_f")
_SMALL_SHARDED = ("s5_d", "s5_b_glu")
_PACK_QUANTUM = SUBLANES * LANES
_WEIGHTS = ('norm_pre', 'norm_post', 's5_w_in', 's5_a_re', 's5_a_im', 's5_log_dt', 's5_b_re', 's5_b_im', 's5_c_re', 's5_c_im',
            's5_d', 's5_w_glu', 's5_b_glu', 's5_w_out', 'kv_norm', 'kv_w', 'kv_b_f', 'fox_w_in', 'fox_w_out')


def _full_from_slots(name, slots):
    n, r, c = slots.shape
    if name in _COL_SHARDED:
        return slots.transpose(1, 0, 2).reshape(r, n * c)
    return slots.reshape(n * r, c)


def _slots_from_full(name, full):
    if name in _COL_SHARDED:
        r, nc = full.shape
        return full.reshape(r, N_DEV, nc // N_DEV).transpose(1, 0, 2)
    nr, c = full.shape
    return full.reshape(N_DEV, nr // N_DEV, c)


def _groups_last(shape):
    return len(shape) >= 3 and shape[-1] < LANES and shape[-3] % LANES == 0


def _pack(vals):
    parts = []
    for v in vals:
        flat = jnp.moveaxis(v, -3, -1).reshape(-1) if _groups_last(v.shape) else v.reshape(-1)
        parts.append(jnp.pad(flat, (0, (-flat.shape[0]) % _PACK_QUANTUM)))
    total = sum(p.shape[0] for p in parts)
    parts.append(jnp.zeros(((-total) % (N_DEV * _PACK_QUANTUM),), F32))
    return jnp.concatenate(parts).reshape(-1, LANES)


def _unpack(packed, shapes):
    flat = packed.reshape(-1)
    out, off = [], 0
    for sh in shapes:
        n = math.prod(sh)
        piece = flat[off:off + n]
        if _groups_last(sh):
            piece = jnp.moveaxis(piece.reshape(sh[:-3] + sh[-2:] + sh[-3:-2]), -1, -3)
        out.append(piece.reshape(sh))
        off += n + (-n) % _PACK_QUANTUM
    return out


class _Comm:
    _GROUPS = (("s5_w_in",) + _SMALL_SHARDED, ("s5_w_glu", "s5_w_out"), ("kv_w", "fox_w_in"), ("fox_w_out",))
    _SLOT_FORM = ("s5_w_in", "fox_w_in")

    def __init__(self, shards, vectors, early=()):
        self._shards = {**shards, **vectors}
        self._full, self._gathers = {}, {}
        self._early = list(early)
        self.token = jnp.zeros((), F32)
        for group in self._GROUPS[:-1]:
            self.token = self.token + self._start(group, ())[0, 0]
        self.late_token = None
        self._sent = []

    def _start(self, group, after):
        state, tok = _exchange_start([self._shards[n] for n in group], False, "gather_start_" + group[0], after,
                                     peers=_CHIP_PEERS)
        self._gathers[group] = state
        return tok

    def vector(self, name):
        return self._full[name]

    def weight(self, name, after):
        if name not in self._full:
            group = next(g for g in self._GROUPS if name in g)
            if group == self._GROUPS[0]:
                after = (list(after) if isinstance(after, (list, tuple)) else [after]) + self._early
            slots = _exchange_wait(self._gathers.pop(group), after, "gather_wait_" + group[0])
            slots = _forward_to_sibling(slots, "gather_forward_" + group[0])
            for n, sl in zip(group, slots):
                if n in _SMALL_SHARDED:
                    self._full[n] = sl.reshape(-1)
                else:
                    self._full[n] = sl if n in self._SLOT_FORM else _full_from_slots(n, sl)
            if group == self._GROUPS[-2]:
                self.late_token = self._start(self._GROUPS[-1], [slots[0]])[0, 0]
        return self._full[name]

    def send_grads(self, grads, name):
        names = list(grads)
        slots = [grads[n] if grads[n].ndim == 3 else _slots_from_full(n, grads[n]).astype(BF16) for n in names]
        state, tok = _exchange_start(slots, True, name + "_start")
        self._sent.append((names, state, name + "_wait"))
        return tok

    def received_grads(self, group, after):
        names, state, name = self._sent[group]
        return list(zip(names, _exchange_wait(state, after, name)))


def kernel(x, norm_pre, norm_post, s5_w_in, s5_a_re, s5_a_im, s5_log_dt, s5_b_re, s5_b_im, s5_c_re, s5_c_im, s5_d, s5_w_glu, s5_b_glu, s5_w_out, kv_norm, kv_w, kv_b_f, fox_w_in, fox_w_out, loss_target, m_norm_pre, m_norm_post, m_s5_w_in, m_s5_a_re, m_s5_a_im, m_s5_log_dt, m_s5_b_re, m_s5_b_im, m_s5_c_re, m_s5_c_im, m_s5_d, m_s5_w_glu, m_s5_b_glu, m_s5_w_out, m_kv_norm, m_kv_w, m_kv_b_f, m_fox_w_in, m_fox_w_out, v_norm_pre, v_norm_post, v_s5_w_in, v_s5_a_re, v_s5_a_im, v_s5_log_dt, v_s5_b_re, v_s5_b_im, v_s5_c_re, v_s5_c_im, v_s5_d, v_s5_w_glu, v_s5_b_glu, v_s5_w_out, v_kv_norm, v_kv_w, v_kv_b_f, v_fox_w_in, v_fox_w_out):
    env = dict(locals())
    wts = {n: env[n] for n in _WEIGHTS}
    mom = {n: env["m_" + n] for n in _WEIGHTS}
    var = {n: env["v_" + n] for n in _WEIGHTS}
    me = 4 * lax.axis_index("x") + 2 * lax.axis_index("y") + lax.axis_index("c")
    shard2d = {n: (wts[n].T if n == "kv_w" else wts[n].reshape(wts[n].shape[-2:])) for n in _BIG}
    full_shape = {n: ((wts[n].size * N_DEV,) if n in _SMALL_SHARDED else wts[n].shape) for n in _SMALL}

    def spread(n, v):
        if n not in _SMALL_SHARDED:
            return v
        flat = v.reshape(-1)
        return lax.dynamic_update_slice(jnp.zeros(full_shape[n], F32), flat, (me * flat.shape[0],))

    packed = [_pack([spread(n, src[n]) for n in _SMALL] + [jnp.zeros((1,), F32)]) for src in (wts, mom, var)]
    comm = _Comm({n: _cast_bf16(shard2d[n], "cast_" + n) for n in _BIG}, {n: wts[n].reshape(1, -1) for n in _SMALL_SHARDED}, packed)

    loss_local, grad_x, small = _local_step(
        x[0], loss_target[0], norm_pre, norm_post, kv_norm, kv_b_f, s5_a_re[0], s5_a_im[0], s5_log_dt[0],
        s5_b_re[0], s5_b_im[0], s5_c_re[0], s5_c_im[0], comm)

    small_pack = _pack([small[n] for n in _SMALL] + [loss_local.reshape(1)])
    slice_rows = small_pack.shape[0] // N_DEV
    small_state, small_tok = _exchange_start([small_pack.reshape(N_DEV, slice_rows, LANES)], True, "reduce_small_start")

    res = {}

    def finish(group, after):
        for n, recv in comm.received_grads(group, after):
            if n == "kv_w":
                res[n] = [o.T for o in _adamw(recv, wts[n].T, mom[n].T, var[n].T, "adamw_" + n)]
            else:
                res[n] = _adamw(recv, wts[n], mom[n], var[n], "adamw_" + n)

    finish(0, [small_tok, grad_x])
    my_sum = _sum_parts(_exchange_wait(small_state, res["kv_w"][0], "reduce_small_wait")[0], "sum_small")
    gather_state, gather_tok = _exchange_start([my_sum], False, "gather_small_start")
    finish(1, gather_tok)
    finish(2, gather_tok)
    g_all = _exchange_wait(gather_state, res["s5_w_in"][0], "gather_small_wait")[0].reshape(1, small_pack.shape[0], LANES)
    outs = _adamw(g_all, *packed, "adamw_small")
    unpacked = [_unpack(o, [full_shape[n] for n in _SMALL] + [(1,)]) for o in outs]
    loss = unpacked[0][-1][0]
    for i, n in enumerate(_SMALL):
        vals = [u[i] for u in unpacked]
        if n in _SMALL_SHARDED:
            k = wts[n].size
            vals = [lax.dynamic_slice(v, (me * k,), (k,)) for v in vals]
        res[n] = [v.reshape(wts[n].shape) for v in vals]

    return (loss, grad_x[None], *[res[n][0] for n in _WEIGHTS], *[res[n][1] for n in _WEIGHTS],
            *[res[n][2] for n in _WEIGHTS], *[res[n][3] for n in _WEIGHTS])
```
